```python
import jax
import jax.numpy as jnp
from jax import lax
import numpy as np

D_MODEL = 1024
BATCH = 8
SEQ = 8192
DEPTH = 1

GRID_W = 64
CTX_LEN = 256
EPS = 1e-6

A_WIDTH = 512
A_GROUPS = 4
A_GROUP_DIM = A_WIDTH // A_GROUPS
A_CHUNK = 128
A_ROW_GROUPS = 2

B_HEADS = 4
B_DK = 64
B_DV = 128
B_KEY_WIDTH = B_HEADS * B_DK
B_VAL_WIDTH = B_HEADS * B_DV
B_GATE_RANK = 16
B_GATE_TAU = 16.0
B_CHUNK = 64

Q0 = 0
K0 = Q0 + B_KEY_WIDTH
V0 = K0 + B_KEY_WIDTH
LR0 = V0 + B_VAL_WIDTH
ZB0 = LR0 + 2 * B_GATE_RANK
UA0 = ZB0 + B_VAL_WIDTH
VA0 = UA0 + A_WIDTH
ZA0 = VA0 + A_WIDTH
G0 = ZA0 + A_WIDTH
IN_WIDTH = G0 + 2 * D_MODEL

kernel_name = "hybrid_gmlp_gla_prefix_block"


def rmsnorm(x, g):
    xf = x.astype(jnp.float32)
    y = xf * lax.rsqrt(jnp.mean(xf * xf, axis=-1, keepdims=True) + EPS)
    return (y * g.astype(jnp.float32)).astype(x.dtype)


def layernorm(x, g, b):
    xf = x.astype(jnp.float32)
    xc = xf - jnp.mean(xf, axis=-1, keepdims=True)
    y = xc * lax.rsqrt(jnp.mean(xc * xc, axis=-1, keepdims=True) + EPS)
    return (y * g.astype(jnp.float32) + b.astype(jnp.float32)).astype(x.dtype)


def spatial_mix(vg, ws, bs):
    b, l, g, dg = vg.shape
    vr = vg.reshape(b, l // A_CHUNK, A_CHUNK, g, dg)
    s = jnp.einsum('gij,bnjgc->bnigc', ws, vr) + bs.T[None, None, :, :, None]
    return s.reshape(b, l, g, dg)


def to_colmajor(t, rows):
    b, l, g, dg = t.shape
    return t.reshape(b, rows, GRID_W, g, dg).transpose(0, 2, 1, 3, 4).reshape(b, l, g, dg)


def from_colmajor(t, rows):
    b, l, g, dg = t.shape
    return t.reshape(b, GRID_W, rows, g, dg).transpose(0, 2, 1, 3, 4).reshape(b, l, g, dg)


def chunk_mlp_branch(p, rows, ln_g, ln_b, ws, bs, w_proj):
    b, l, _ = p.shape
    u = p[..., UA0:VA0]
    z = p[..., ZA0:G0]
    vn = layernorm(p[..., VA0:ZA0], ln_g, ln_b).reshape(b, l, A_GROUPS, A_GROUP_DIM)
    if rows is None:
        sv = spatial_mix(vn, ws, bs)
    else:
        r = A_ROW_GROUPS
        sv_row = spatial_mix(vn[:, :, :r], ws[:r], bs[:r])
        sv_col = from_colmajor(spatial_mix(to_colmajor(vn[:, :, r:], rows), ws[r:], bs[r:]), rows)
        sv = jnp.concatenate([sv_row, sv_col], axis=2)
    return (u * sv.reshape(b, l, A_WIDTH) * jax.nn.silu(z)) @ w_proj


def to_chunks(t):
    b, l, h, d = t.shape
    return t.reshape(b, l // B_CHUNK, B_CHUNK, h, d).transpose(0, 3, 1, 2, 4)


def from_chunks(t):
    b, h, n, c, d = t.shape
    return t.transpose(0, 2, 3, 1, 4).reshape(b, n * c, h, d)


def gla_states(kc, vc, ac, s0):
    cum = jnp.cumsum(ac, axis=3)
    cum_last = cum[:, :, :, -1]
    k_dec = kc * jnp.exp(cum_last[:, :, :, None] - cum)
    kv = jnp.einsum('bhncd,bhnce->bhnde', k_dec, vc)

    def step(s, inp):
        decay, upd = inp
        return decay[..., None] * s + upd, s

    s_final, s_before = lax.scan(step, s0, (jnp.moveaxis(jnp.exp(cum_last), 2, 0), jnp.moveaxis(kv, 2, 0)))
    return s_final, jnp.moveaxis(s_before, 0, 2), cum


def gla_chunk_outputs(qc, kc, vc, cum, s_before):
    qd = qc * jnp.exp(cum)
    kd = kc * jnp.exp(-cum)
    scores = jnp.einsum('bhnid,bhnjd->bhnij', qd, kd)
    mask = jnp.tril(jnp.ones((B_CHUNK, B_CHUNK), dtype=bool))
    scores = jnp.where(mask, scores, 0.0)
    return jnp.einsum('bhnij,bhnje->bhnie', scores, vc) + jnp.einsum('bhnid,bhnde->bhnie', qd, s_before)


def gla_direction(q, k, v, log_a, s0, reverse):
    if reverse:
        k, v, log_a = jnp.flip(k, 1), jnp.flip(v, 1), jnp.flip(log_a, 1)
        if q is not None:
            q = jnp.flip(q, 1)
    kc, vc, ac = to_chunks(k), to_chunks(v), to_chunks(log_a)
    s_final, s_before, cum = gla_states(kc, vc, ac, s0)
    if q is None:
        return None, s_final
    o = from_chunks(gla_chunk_outputs(to_chunks(q), kc, vc, cum, s_before))
    if reverse:
        o = jnp.flip(o, 1)
    return o, s_final


def gla_q(p):
    b, l, _ = p.shape
    return p[..., Q0:K0].reshape(b, l, B_HEADS, B_DK).astype(jnp.float32) * (B_DK ** -0.5)


def gla_kva(p, base, w2, gb):
    b, l, _ = p.shape
    k = p[..., K0 - base:V0 - base].reshape(b, l, B_HEADS, B_DK).astype(jnp.float32)
    v = p[..., V0 - base:LR0 - base].reshape(b, l, B_HEADS, B_DV).astype(jnp.float32)
    lr = p[..., LR0 - base:ZB0 - base].reshape(b, l, 2, B_GATE_RANK)
    logits = jnp.einsum('blrk,rkd->blrd', lr, w2) + gb
    log_a = (jax.nn.log_sigmoid(logits.astype(jnp.float32)) / B_GATE_TAU).reshape(b, l, 2, B_HEADS, B_DK)
    return k, v, log_a[:, :, 0], log_a[:, :, 1]


def gla_branch_out(o, z, g, w_proj):
    b, l, _, _ = o.shape
    on = o * lax.rsqrt(jnp.mean(o * o, axis=-1, keepdims=True) + EPS) * g.reshape(B_HEADS, B_DV).astype(jnp.float32)
    on = on.reshape(b, l, B_VAL_WIDTH).astype(z.dtype)
    return (on * jax.nn.silu(z)) @ w_proj


def merge_branches(p, ya, yb, w_out):
    g = jax.nn.sigmoid(p[..., G0:])
    return (g[..., :D_MODEL] * ya + g[..., D_MODEL:] * yb) @ w_out


def _fwd_setup_inputs(seed: int = 0) -> dict:
    key = jax.random.key(seed)
    ks = jax.random.split(key, 20)

    def nrm(k, shape, s):
        return jax.random.normal(k, shape, jnp.float32) * s

    return {
        "x": nrm(ks[0], (BATCH, SEQ, D_MODEL), 1.0),
        "c": nrm(ks[1], (BATCH, D_MODEL), 1.0),
        "ctx": nrm(ks[2], (BATCH, CTX_LEN, D_MODEL), 1.0),
        "c_ctx": nrm(ks[3], (D_MODEL,), 1.0),
        "w_mod": nrm(ks[4], (DEPTH, D_MODEL, 3 * D_MODEL), D_MODEL ** -0.5),
        "b_mod": nrm(ks[5], (DEPTH, 3 * D_MODEL), 0.01),
        "norm_g": 1.0 + nrm(ks[6], (DEPTH, D_MODEL), 0.01),
        "w_in": nrm(ks[7], (DEPTH, D_MODEL, IN_WIDTH), D_MODEL ** -0.5),
        "a_ln_g": 1.0 + nrm(ks[8], (DEPTH, A_WIDTH), 0.01),
        "a_ln_b": nrm(ks[9], (DEPTH, A_WIDTH), 0.01),
        "a_ws": nrm(ks[10], (DEPTH, A_GROUPS, A_CHUNK, A_CHUNK), A_CHUNK ** -0.5),
        "a_bs": 1.0 + nrm(ks[11], (DEPTH, A_GROUPS, A_CHUNK), 0.01),
        "b_gate_w2": nrm(ks[12], (DEPTH, 2, B_GATE_RANK, B_KEY_WIDTH), B_GATE_RANK ** -0.5),
        "b_gate_b": nrm(ks[13], (DEPTH, 2, B_KEY_WIDTH), 0.1),
        "b_norm_g": 1.0 + nrm(ks[14], (DEPTH, B_VAL_WIDTH), 0.01),
        "w_proj_a": nrm(ks[15], (DEPTH, A_WIDTH, D_MODEL), A_WIDTH ** -0.5),
        "w_proj_b": nrm(ks[16], (DEPTH, B_VAL_WIDTH, D_MODEL), B_VAL_WIDTH ** -0.5),
        "w_out": nrm(ks[17], (DEPTH, D_MODEL, D_MODEL), D_MODEL ** -0.5),
        "final_norm_g": 1.0 + nrm(ks[18], (D_MODEL,), 0.01),
    }


def _fwd_reference(x, c, ctx, c_ctx, w_mod, b_mod, norm_g, w_in, a_ln_g, a_ln_b, a_ws, a_bs,
              b_gate_w2, b_gate_b, b_norm_g, w_proj_a, w_proj_b, w_out, final_norm_g):
    rows = x.shape[1] // GRID_W
    xc = ctx
    for layer in range(DEPTH):
        last = layer == DEPTH - 1
        wm, bm, wi = w_mod[layer], b_mod[layer], w_in[layer]

        mod = jax.nn.silu(c) @ wm + bm
        shift, scale, gate = jnp.split(mod, 3, axis=-1)
        h = rmsnorm(x, norm_g[layer]) * (1 + scale[:, None]) + shift[:, None]
        p = h @ wi

        n_mod = (2 if last else 3) * D_MODEL
        mod_c = jax.nn.silu(c_ctx) @ wm[:, :n_mod] + bm[:n_mod]
        hc = rmsnorm(xc, norm_g[layer]) * (1 + mod_c[D_MODEL:2 * D_MODEL]) + mod_c[:D_MODEL]
        base_c = K0 if last else 0
        pc = hc @ (wi[:, K0:ZB0] if last else wi)

        kc, vc, ac_f, ac_b = gla_kva(pc, base_c, b_gate_w2[layer], b_gate_b[layer])
        qc = None if last else gla_q(pc)
        s0 = jnp.zeros((xc.shape[0], B_HEADS, B_DK, B_DV), jnp.float32)
        oc_f, sc_f = gla_direction(qc, kc, vc, ac_f, s0, False)
        oc_b, sc_b = gla_direction(qc, kc, vc, ac_b, s0, True)

        k, v, a_f, a_b = gla_kva(p, 0, b_gate_w2[layer], b_gate_b[layer])
        q = gla_q(p)
        o_f, _ = gla_direction(q, k, v, a_f, sc_f, False)
        o_b, _ = gla_direction(q, k, v, a_b, sc_b, True)
        yb = gla_branch_out(o_f + o_b, p[..., ZB0:UA0], b_norm_g[layer], w_proj_b[layer])

        ya = chunk_mlp_branch(p, rows, a_ln_g[layer], a_ln_b[layer], a_ws[layer], a_bs[layer], w_proj_a[layer])

        x = x + gate[:, None] * merge_branches(p, ya, yb, w_out[layer])

        if not last:
            ybc = gla_branch_out(oc_f + oc_b, pc[..., ZB0:UA0], b_norm_g[layer], w_proj_b[layer])
            yac = chunk_mlp_branch(pc, None, a_ln_g[layer], a_ln_b[layer], a_ws[layer], a_bs[layer], w_proj_a[layer])
            xc = xc + mod_c[2 * D_MODEL:] * merge_branches(pc, yac, ybc, w_out[layer])
    return rmsnorm(x, final_norm_g)


import jax as _jax
import jax.numpy as _jnp

TWIN_FORMAT = 'train_step'
FWD_PARAMS = ['x', 'c', 'ctx', 'c_ctx', 'w_mod', 'b_mod', 'norm_g', 'w_in', 'a_ln_g', 'a_ln_b', 'a_ws', 'a_bs', 'b_gate_w2', 'b_gate_b', 'b_norm_g', 'w_proj_a', 'w_proj_b', 'w_out', 'final_norm_g']
TWIN_WEIGHTS = ['c_ctx', 'w_mod', 'b_mod', 'norm_g', 'w_in', 'a_ln_g', 'a_ln_b', 'a_ws', 'a_bs', 'b_gate_w2', 'b_gate_b', 'b_norm_g', 'w_proj_a', 'w_proj_b', 'w_out', 'final_norm_g']
TWIN_DIFF_INPUT = 'x'
TWIN_INPUTS = ['x', 'c', 'ctx', 'c_ctx', 'w_mod', 'b_mod', 'norm_g', 'w_in', 'a_ln_g', 'a_ln_b', 'a_ws', 'a_bs', 'b_gate_w2', 'b_gate_b', 'b_norm_g', 'w_proj_a', 'w_proj_b', 'w_out', 'final_norm_g', 'loss_target', 'm_c_ctx', 'm_w_mod', 'm_b_mod', 'm_norm_g', 'm_w_in', 'm_a_ln_g', 'm_a_ln_b', 'm_a_ws', 'm_a_bs', 'm_b_gate_w2', 'm_b_gate_b', 'm_b_norm_g', 'm_w_proj_a', 'm_w_proj_b', 'm_w_out', 'm_final_norm_g', 'v_c_ctx', 'v_w_mod', 'v_b_mod', 'v_norm_g', 'v_w_in', 'v_a_ln_g', 'v_a_ln_b', 'v_a_ws', 'v_a_bs', 'v_b_gate_w2', 'v_b_gate_b', 'v_b_norm_g', 'v_w_proj_a', 'v_w_proj_b', 'v_w_out', 'v_final_norm_g']
TWIN_OUTPUTS = ['loss', 'grad_x', 'grad_c_ctx', 'grad_w_mod', 'grad_b_mod', 'grad_norm_g', 'grad_w_in', 'grad_a_ln_g', 'grad_a_ln_b', 'grad_a_ws', 'grad_a_bs', 'grad_b_gate_w2', 'grad_b_gate_b', 'grad_b_norm_g', 'grad_w_proj_a', 'grad_w_proj_b', 'grad_w_out', 'grad_final_norm_g', 'delta_c_ctx', 'delta_w_mod', 'delta_b_mod', 'delta_norm_g', 'delta_w_in', 'delta_a_ln_g', 'delta_a_ln_b', 'delta_a_ws', 'delta_a_bs', 'delta_b_gate_w2', 'delta_b_gate_b', 'delta_b_norm_g', 'delta_w_proj_a', 'delta_w_proj_b', 'delta_w_out', 'delta_final_norm_g', 'new_m_c_ctx', 'new_m_w_mod', 'new_m_b_mod', 'new_m_norm_g', 'new_m_w_in', 'new_m_a_ln_g', 'new_m_a_ln_b', 'new_m_a_ws', 'new_m_a_bs', 'new_m_b_gate_w2', 'new_m_b_gate_b', 'new_m_b_norm_g', 'new_m_w_proj_a', 'new_m_w_proj_b', 'new_m_w_out', 'new_m_final_norm_g', 'new_v_c_ctx', 'new_v_w_mod', 'new_v_b_mod', 'new_v_norm_g', 'new_v_w_in', 'new_v_a_ln_g', 'new_v_a_ln_b', 'new_v_a_ws', 'new_v_a_bs', 'new_v_b_gate_w2', 'new_v_b_gate_b', 'new_v_b_norm_g', 'new_v_w_proj_a', 'new_v_w_proj_b', 'new_v_w_out', 'new_v_final_norm_g']
TWIN_LEAF_KINDS = {'loss': 'loss', 'grad_x': 'grad_x', 'grad_c_ctx': 'grad_w', 'grad_w_mod': 'grad_w', 'grad_b_mod': 'grad_w', 'grad_norm_g': 'grad_w', 'grad_w_in': 'grad_w', 'grad_a_ln_g': 'grad_w', 'grad_a_ln_b': 'grad_w', 'grad_a_ws': 'grad_w', 'grad_a_bs': 'grad_w', 'grad_b_gate_w2': 'grad_w', 'grad_b_gate_b': 'grad_w', 'grad_b_norm_g': 'grad_w', 'grad_w_proj_a': 'grad_w', 'grad_w_proj_b': 'grad_w', 'grad_w_out': 'grad_w', 'grad_final_norm_g': 'grad_w', 'delta_c_ctx': 'delta_w', 'delta_w_mod': 'delta_w', 'delta_b_mod': 'delta_w', 'delta_norm_g': 'delta_w', 'delta_w_in': 'delta_w', 'delta_a_ln_g': 'delta_w', 'delta_a_ln_b': 'delta_w', 'delta_a_ws': 'delta_w', 'delta_a_bs': 'delta_w', 'delta_b_gate_w2': 'delta_w', 'delta_b_gate_b': 'delta_w', 'delta_b_norm_g': 'delta_w', 'delta_w_proj_a': 'delta_w', 'delta_w_proj_b': 'delta_w', 'delta_w_out': 'delta_w', 'delta_final_norm_g': 'delta_w', 'new_m_c_ctx': 'new_m', 'new_m_w_mod': 'new_m', 'new_m_b_mod': 'new_m', 'new_m_norm_g': 'new_m', 'new_m_w_in': 'new_m', 'new_m_a_ln_g': 'new_m', 'new_m_a_ln_b': 'new_m', 'new_m_a_ws': 'new_m', 'new_m_a_bs': 'new_m', 'new_m_b_gate_w2': 'new_m', 'new_m_b_gate_b': 'new_m', 'new_m_b_norm_g': 'new_m', 'new_m_w_proj_a': 'new_m', 'new_m_w_proj_b': 'new_m', 'new_m_w_out': 'new_m', 'new_m_final_norm_g': 'new_m', 'new_v_c_ctx': 'new_v', 'new_v_w_mod': 'new_v', 'new_v_b_mod': 'new_v', 'new_v_norm_g': 'new_v', 'new_v_w_in': 'new_v', 'new_v_a_ln_g': 'new_v', 'new_v_a_ln_b': 'new_v', 'new_v_a_ws': 'new_v', 'new_v_a_bs': 'new_v', 'new_v_b_gate_w2': 'new_v', 'new_v_b_gate_b': 'new_v', 'new_v_b_norm_g': 'new_v', 'new_v_w_proj_a': 'new_v', 'new_v_w_proj_b': 'new_v', 'new_v_w_out': 'new_v', 'new_v_final_norm_g': 'new_v'}


def _forward(args):
    return _fwd_reference(*[args[k] for k in FWD_PARAMS])


def _output_shape():
    def fwd():
        inp = _fwd_setup_inputs(0)
        return _fwd_reference(*[inp[k] for k in FWD_PARAMS])
    out = _jax.eval_shape(fwd)
    return out.shape, out.dtype

N_MICROBATCH = 1
ADAM_LR = 0.001
ADAM_B1 = 0.9
ADAM_B2 = 0.999
ADAM_EPS = 1e-08
ADAM_WD = 0.01
ADAM_STEP = 10
PER_EXAMPLE_BATCH_AXIS = {'x': 0, 'c': 0, 'ctx': 0, 'loss_target': 0}
SHARED_INPUTS = []
_WEIGHT_DTYPES = {'c_ctx': _jnp.float32, 'w_mod': _jnp.float32, 'b_mod': _jnp.float32, 'norm_g': _jnp.float32, 'w_in': _jnp.float32, 'a_ln_g': _jnp.float32, 'a_ln_b': _jnp.float32, 'a_ws': _jnp.float32, 'a_bs': _jnp.float32, 'b_gate_w2': _jnp.float32, 'b_gate_b': _jnp.float32, 'b_norm_g': _jnp.float32, 'w_proj_a': _jnp.float32, 'w_proj_b': _jnp.float32, 'w_out': _jnp.float32, 'final_norm_g': _jnp.float32}
MOMENT_SCALE = {'c_ctx': 1.824999e-02, 'w_mod': 1.255127e-01, 'b_mod': 2.142666e-01, 'norm_g': 2.129774e-01, 'w_in': 1.050376e-01, 'a_ln_g': 1.191627e-01, 'a_ln_b': 1.195377e-01, 'a_ws': 1.125684e-01, 'a_bs': 1.147648e-01, 'b_gate_w2': 3.254731e-02, 'b_gate_b': 5.181006e-02, 'b_norm_g': 8.865971e-02, 'w_proj_a': 1.110963e-01, 'w_proj_b': 6.080496e-02, 'w_out': 1.268385e-01, 'final_norm_g': 6.458310e+01}


def _to_microbatches(a, axis):
    t = _jnp.moveaxis(a, axis, 0)
    t = t.reshape((N_MICROBATCH, t.shape[0] // N_MICROBATCH) + t.shape[1:])
    return _jnp.moveaxis(t, 1, axis + 1)


def setup_inputs(seed: int = 0) -> dict:
    inp = _fwd_setup_inputs(seed)
    key = _jax.random.fold_in(_jax.random.key(seed), 7919)
    shape, _ = _output_shape()
    out = dict(inp)
    out["loss_target"] = _jax.random.normal(_jax.random.fold_in(key, 0), shape, _jnp.float32)
    for i, name in enumerate(TWIN_WEIGHTS):
        w = inp[name].astype(_jnp.float32)
        if MOMENT_SCALE is None:
            s = _jnp.sqrt(_jnp.mean(_jnp.square(w)) + 1e-30)
        else:
            s = MOMENT_SCALE[name]
        km, kv = _jax.random.split(_jax.random.fold_in(key, i + 1))
        out[name] = w
        out["m_" + name] = s * _jax.random.normal(km, w.shape, _jnp.float32)
        out["v_" + name] = (s * s) * _jax.random.uniform(kv, w.shape, _jnp.float32, 0.5, 1.5)
    if N_MICROBATCH > 1:
        for name, axis in PER_EXAMPLE_BATCH_AXIS.items():
            out[name] = _to_microbatches(out[name], axis)
    return {'x': out['x'], 'c': out['c'], 'ctx': out['ctx'], 'c_ctx': out['c_ctx'], 'w_mod': out['w_mod'], 'b_mod': out['b_mod'], 'norm_g': out['norm_g'], 'w_in': out['w_in'], 'a_ln_g': out['a_ln_g'], 'a_ln_b': out['a_ln_b'], 'a_ws': out['a_ws'], 'a_bs': out['a_bs'], 'b_gate_w2': out['b_gate_w2'], 'b_gate_b': out['b_gate_b'], 'b_norm_g': out['b_norm_g'], 'w_proj_a': out['w_proj_a'], 'w_proj_b': out['w_proj_b'], 'w_out': out['w_out'], 'final_norm_g': out['final_norm_g'], 'loss_target': out['loss_target'], 'm_c_ctx': out['m_c_ctx'], 'm_w_mod': out['m_w_mod'], 'm_b_mod': out['m_b_mod'], 'm_norm_g': out['m_norm_g'], 'm_w_in': out['m_w_in'], 'm_a_ln_g': out['m_a_ln_g'], 'm_a_ln_b': out['m_a_ln_b'], 'm_a_ws': out['m_a_ws'], 'm_a_bs': out['m_a_bs'], 'm_b_gate_w2': out['m_b_gate_w2'], 'm_b_gate_b': out['m_b_gate_b'], 'm_b_norm_g': out['m_b_norm_g'], 'm_w_proj_a': out['m_w_proj_a'], 'm_w_proj_b': out['m_w_proj_b'], 'm_w_out': out['m_w_out'], 'm_final_norm_g': out['m_final_norm_g'], 'v_c_ctx': out['v_c_ctx'], 'v_w_mod': out['v_w_mod'], 'v_b_mod': out['v_b_mod'], 'v_norm_g': out['v_norm_g'], 'v_w_in': out['v_w_in'], 'v_a_ln_g': out['v_a_ln_g'], 'v_a_ln_b': out['v_a_ln_b'], 'v_a_ws': out['v_a_ws'], 'v_a_bs': out['v_a_bs'], 'v_b_gate_w2': out['v_b_gate_w2'], 'v_b_gate_b': out['v_b_gate_b'], 'v_b_norm_g': out['v_b_norm_g'], 'v_w_proj_a': out['v_w_proj_a'], 'v_w_proj_b': out['v_w_proj_b'], 'v_w_out': out['v_w_out'], 'v_final_norm_g': out['v_final_norm_g']}


def _loss(weights, diff, rest, loss_target):
    with _jax.named_scope("forward"):
        args = {**rest, TWIN_DIFF_INPUT: diff, **{k: w.astype(_WEIGHT_DTYPES[k]) for k, w in weights.items()}}
        y = _forward(args)
    with _jax.named_scope("loss_head"):
        err = _jnp.square(y.astype(_jnp.float32) - loss_target)
        return 0.5 * _jnp.sum(_jnp.mean(err, axis=-1)) if err.ndim else 0.5 * err


def _adamw(w, g, m, v):
    m = ADAM_B1 * m + (1.0 - ADAM_B1) * g
    v = ADAM_B2 * v + (1.0 - ADAM_B2) * _jnp.square(g)
    m_hat = m / (1.0 - ADAM_B1 ** ADAM_STEP)
    v_hat = v / (1.0 - ADAM_B2 ** ADAM_STEP)
    delta = -ADAM_LR * (m_hat / (_jnp.sqrt(v_hat) + ADAM_EPS) + ADAM_WD * w)
    return delta, m, v


def reference(x, c, ctx, c_ctx, w_mod, b_mod, norm_g, w_in, a_ln_g, a_ln_b, a_ws, a_bs, b_gate_w2, b_gate_b, b_norm_g, w_proj_a, w_proj_b, w_out, final_norm_g, loss_target, m_c_ctx, m_w_mod, m_b_mod, m_norm_g, m_w_in, m_a_ln_g, m_a_ln_b, m_a_ws, m_a_bs, m_b_gate_w2, m_b_gate_b, m_b_norm_g, m_w_proj_a, m_w_proj_b, m_w_out, m_final_norm_g, v_c_ctx, v_w_mod, v_b_mod, v_norm_g, v_w_in, v_a_ln_g, v_a_ln_b, v_a_ws, v_a_bs, v_b_gate_w2, v_b_gate_b, v_b_norm_g, v_w_proj_a, v_w_proj_b, v_w_out, v_final_norm_g):
    given = dict(x=x, c=c, ctx=ctx, c_ctx=c_ctx, w_mod=w_mod, b_mod=b_mod, norm_g=norm_g, w_in=w_in, a_ln_g=a_ln_g, a_ln_b=a_ln_b, a_ws=a_ws, a_bs=a_bs, b_gate_w2=b_gate_w2, b_gate_b=b_gate_b, b_norm_g=b_norm_g, w_proj_a=w_proj_a, w_proj_b=w_proj_b, w_out=w_out, final_norm_g=final_norm_g, loss_target=loss_target, m_c_ctx=m_c_ctx, m_w_mod=m_w_mod, m_b_mod=m_b_mod, m_norm_g=m_norm_g, m_w_in=m_w_in, m_a_ln_g=m_a_ln_g, m_a_ln_b=m_a_ln_b, m_a_ws=m_a_ws, m_a_bs=m_a_bs, m_b_gate_w2=m_b_gate_w2, m_b_gate_b=m_b_gate_b, m_b_norm_g=m_b_norm_g, m_w_proj_a=m_w_proj_a, m_w_proj_b=m_w_proj_b, m_w_out=m_w_out, m_final_norm_g=m_final_norm_g, v_c_ctx=v_c_ctx, v_w_mod=v_w_mod, v_b_mod=v_b_mod, v_norm_g=v_norm_g, v_w_in=v_w_in, v_a_ln_g=v_a_ln_g, v_a_ln_b=v_a_ln_b, v_a_ws=v_a_ws, v_a_bs=v_a_bs, v_b_gate_w2=v_b_gate_w2, v_b_gate_b=v_b_gate_b, v_b_norm_g=v_b_norm_g, v_w_proj_a=v_w_proj_a, v_w_proj_b=v_w_proj_b, v_w_out=v_w_out, v_final_norm_g=v_final_norm_g)
    weights = {n: given[n] for n in TWIN_WEIGHTS}
    shared = {n: given[n] for n in SHARED_INPUTS}
    per_example = {n: given[n] for n in ['x', 'c', 'ctx']}
    grad_fn = _jax.value_and_grad(_loss, argnums=(0, 1))

    def one_microbatch(ex, loss_target):
        ex = dict(ex)
        diff = ex.pop(TWIN_DIFF_INPUT)
        return grad_fn(weights, diff, {**shared, **ex}, loss_target)

    if N_MICROBATCH == 1:
        loss, (grad_w, grad_x) = one_microbatch(per_example, given["loss_target"])
    else:
        def body(carry, xs):
            loss_sum, grad_sum = carry
            l_k, (gw_k, gx_k) = one_microbatch(xs[0], xs[1])
            with _jax.named_scope("update"):
                return (loss_sum + l_k, _jax.tree.map(_jnp.add, grad_sum, gw_k)), gx_k

        init = (_jnp.zeros((), _jnp.float32), _jax.tree.map(_jnp.zeros_like, weights))
        (loss, grad_w), grad_x = _jax.lax.scan(body, init, (per_example, given["loss_target"]))
    with _jax.named_scope("update"):
        delta_w, new_m, new_v = {}, {}, {}
        for n in TWIN_WEIGHTS:
            delta_w[n], new_m[n], new_v[n] = _adamw(weights[n], grad_w[n], given["m_" + n], given["v_" + n])
    return (loss, grad_x, *[grad_w[n] for n in TWIN_WEIGHTS], *[delta_w[n] for n in TWIN_WEIGHTS],
            *[new_m[n] for n in TWIN_WEIGHTS], *[new_v[n] for n in TWIN_WEIGHTS])
```

```python
import functools

import jax
import jax.numpy as jnp
from jax import lax
from jax.experimental import pallas as pl
from jax.experimental.pallas import tpu as pltpu

f32, bf16 = jnp.float32, jnp.bfloat16

D = 1024
CTX = 256
EPS = 1e-6
AW = 512
ACH = 128
GW = 64
KW = 256
VW = 512
NH = 4
HK = 64
HV = 128
RANK = 16
TAU = 16.0
CH = 64
QSCALE = HK ** -0.5
INW = 5152
NDEV = 8

PQ, PK, PV, PZB, PUA, PVA, PZA, PG1, PG2, PLR, PW = 0, 256, 512, 1024, 1536, 2048, 2560, 3072, 4096, 5120, 5248
LRW = 128

ADAM_LR, ADAM_B1, ADAM_B2, ADAM_EPS, ADAM_WD, ADAM_STEP = 0.001, 0.9, 0.999, 1e-08, 0.01, 10

VMEM_LIMIT = 56 * 1024 * 1024
FLAT_ROWS = 7680
MESH = pl.DeviceIdType.MESH


def _cp(sem=None):
    return pltpu.CompilerParams(dimension_semantics=sem, vmem_limit_bytes=VMEM_LIMIT)


def _dot(a, b):
    return jnp.dot(a.astype(bf16), b.astype(bf16), preferred_element_type=f32)


def _nt(a, b):
    return lax.dot_general(a.astype(bf16), b.astype(bf16), (((1,), (1,)), ((), ())), preferred_element_type=f32)


def _tn(a, b):
    return lax.dot_general(a.astype(bf16), b.astype(bf16), (((0,), (0,)), ((), ())), preferred_element_type=f32)


def _dot_hi(a, b):
    return jnp.dot(a, b, preferred_element_type=f32, precision=lax.Precision.HIGHEST)


def _sigmoid(x):
    return 1.0 / (1.0 + jnp.exp(-x))


def _log_sigmoid(x):
    return jnp.minimum(x, 0.0) - jnp.log(1.0 + jnp.exp(-jnp.abs(x)))


def _silu_and_grad(z):
    s = _sigmoid(z)
    return z * s, s * (1.0 + z * (1.0 - s))


def _me():
    return 4 * lax.axis_index("x") + 2 * lax.axis_index("y") + lax.axis_index("c")


def _peer(k):
    x, y, c = lax.axis_index("x"), lax.axis_index("y"), lax.axis_index("c")
    px = 1 - x if k & 4 else x
    py = 1 - y if k & 2 else y
    pc = 1 - c if k & 1 else c
    return (px, py, pc), 4 * px + 2 * py + pc


def _all_gather(v, name):
    r, c = v.shape

    def body(v_ref, out_ref, send_sems, recv_sems, local_sem):
        me = _me()
        mine = pltpu.make_async_copy(v_ref, out_ref.at[me], local_sem)
        mine.start()
        sends, recvs = [], []
        for k in range(1, NDEV):
            dev, idx = _peer(k)
            sends.append(pltpu.make_async_remote_copy(
                src_ref=v_ref, dst_ref=out_ref.at[me], send_sem=send_sems.at[k - 1], recv_sem=recv_sems.at[k - 1],
                device_id=dev, device_id_type=MESH))
            recvs.append(pltpu.make_async_remote_copy(
                src_ref=v_ref, dst_ref=out_ref.at[idx], send_sem=send_sems.at[k - 1], recv_sem=recv_sems.at[k - 1],
                device_id=dev, device_id_type=MESH))
        for cp in sends:
            cp.start()
        for cp in recvs:
            cp.wait_recv()
        for cp in sends:
            cp.wait_send()
        mine.wait()

    return pl.pallas_call(
        body, name=name, out_shape=jax.ShapeDtypeStruct((NDEV, r, c), v.dtype),
        in_specs=[pl.BlockSpec(memory_space=pl.ANY)], out_specs=pl.BlockSpec(memory_space=pl.ANY),
        scratch_shapes=[pltpu.SemaphoreType.DMA((NDEV - 1,)), pltpu.SemaphoreType.DMA((NDEV - 1,)), pltpu.SemaphoreType.DMA(())],
    )(v)


def _all_to_all(v, name):
    _, r, c = v.shape

    def body(v_ref, out_ref, send_sems, recv_sems, local_sem):
        me = _me()
        mine = pltpu.make_async_copy(v_ref.at[me], out_ref.at[me], local_sem)
        mine.start()
        sends, recvs = [], []
        for k in range(1, NDEV):
            dev, idx = _peer(k)
            sends.append(pltpu.make_async_remote_copy(
                src_ref=v_ref.at[idx], dst_ref=out_ref.at[me], send_sem=send_sems.at[k - 1], recv_sem=recv_sems.at[k - 1],
                device_id=dev, device_id_type=MESH))
            recvs.append(pltpu.make_async_remote_copy(
                src_ref=v_ref.at[idx], dst_ref=out_ref.at[idx], send_sem=send_sems.at[k - 1], recv_sem=recv_sems.at[k - 1],
                device_id=dev, device_id_type=MESH))
        for cp in sends:
            cp.start()
        for cp in recvs:
            cp.wait_recv()
        for cp in sends:
            cp.wait_send()
        mine.wait()

    return pl.pallas_call(
        body, name=name, out_shape=jax.ShapeDtypeStruct((NDEV, r, c), v.dtype),
        in_specs=[pl.BlockSpec(memory_space=pl.ANY)], out_specs=pl.BlockSpec(memory_space=pl.ANY),
        scratch_shapes=[pltpu.SemaphoreType.DMA((NDEV - 1,)), pltpu.SemaphoreType.DMA((NDEV - 1,)), pltpu.SemaphoreType.DMA(())],
    )(v)


def _mod_fwd(cs, wm, bm):
    def body(cs_ref, wm_ref, bm_ref, o_ref):
        s, _ = _silu_and_grad(cs_ref[...])
        o_ref[...] = _dot_hi(s, wm_ref[...]) + bm_ref[...]

    return pl.pallas_call(body, name="mod_fwd", out_shape=jax.ShapeDtypeStruct((16, wm.shape[1]), f32),
                          compiler_params=_cp())(cs, wm, bm)


def _mod_bwd(cs, dm, wm):
    def body(cs_ref, dm_ref, wm_ref, gw_ref, gc_ref):
        s, ds = _silu_and_grad(cs_ref[...])
        gw_ref[...] = lax.dot_general(s, dm_ref[...], (((0,), (0,)), ((), ())), preferred_element_type=f32,
                                      precision=lax.Precision.HIGHEST)
        part = lax.dot_general(dm_ref[8:9, :], wm_ref[...], (((1,), (1,)), ((), ())), preferred_element_type=f32,
                               precision=lax.Precision.HIGHEST)
        gc_ref[...] = part * ds[8:9, :]

    return pl.pallas_call(body, name="mod_bwd",
                          out_shape=(jax.ShapeDtypeStruct(wm.shape, f32), jax.ShapeDtypeStruct((1, D), f32)),
                          compiler_params=_cp())(cs, dm, wm)


def _adam(parts, w, m, v, rows, name):
    p, r, c = parts.shape
    c1 = 1.0 / (1.0 - ADAM_B1 ** ADAM_STEP)
    c2 = 1.0 / (1.0 - ADAM_B2 ** ADAM_STEP)

    def body(g_ref, w_ref, m_ref, v_ref, go_ref, d_ref, mo_ref, vo_ref):
        g = g_ref[0]
        for i in range(1, p):
            g = g + g_ref[i]
        mn = ADAM_B1 * m_ref[...] + (1.0 - ADAM_B1) * g
        vn = ADAM_B2 * v_ref[...] + (1.0 - ADAM_B2) * (g * g)
        go_ref[...] = g
        mo_ref[...] = mn
        vo_ref[...] = vn
        d_ref[...] = -ADAM_LR * ((mn * c1) / (jnp.sqrt(vn * c2) + ADAM_EPS) + ADAM_WD * w_ref[...])

    blk = pl.BlockSpec((rows, c), lambda i: (i, 0))
    return pl.pallas_call(
        body, name=name, grid=(r // rows,), out_shape=tuple(jax.ShapeDtypeStruct((r, c), f32) for _ in range(4)),
        in_specs=[pl.BlockSpec((p, rows, c), lambda i: (0, i, 0)), blk, blk, blk], out_specs=(blk, blk, blk, blk),
        compiler_params=_cp(("arbitrary",)),
    )(parts, w, m, v)


def _in_proj(x, g, scale, shift, w_pad, tl):
    l = x.shape[0]

    def body(x_ref, g_ref, sc_ref, sh_ref, w_ref, p_ref, h_ref):
        xv = x_ref[...]
        r = lax.rsqrt(jnp.mean(xv * xv, axis=-1, keepdims=True) + EPS)
        h = (xv * r) * (g_ref[...] * (1.0 + sc_ref[...])) + sh_ref[...]
        hb = h.astype(bf16)
        h_ref[...] = hb
        p_ref[...] = jnp.dot(hb, w_ref[...], preferred_element_type=f32).astype(bf16)

    vec = pl.BlockSpec((1, D), lambda i: (0, 0))
    return pl.pallas_call(
        body, name="in_proj", grid=(l // tl,),
        out_shape=(jax.ShapeDtypeStruct((l, PW), bf16), jax.ShapeDtypeStruct((l, D), bf16)),
        in_specs=[pl.BlockSpec((tl, D), lambda i: (i, 0)), vec, vec, vec, pl.BlockSpec((D, PW), lambda i: (0, 0))],
        out_specs=(pl.BlockSpec((tl, PW), lambda i: (i, 0)), pl.BlockSpec((tl, D), lambda i: (i, 0))),
        compiler_params=_cp(("arbitrary",)),
    )(x, g, scale, shift, w_pad)


def _tri(rev):
    i = lax.broadcasted_iota(jnp.int32, (CH, CH), 0)
    j = lax.broadcasted_iota(jnp.int32, (CH, CH), 1)
    return jnp.where((j >= i) if rev else (j <= i), 1.0, 0.0).astype(f32)


def _head_masks():
    lane = lax.broadcasted_iota(jnp.int32, (1, KW), 1) // HK
    return [jnp.where(lane == h, 1.0, 0.0).astype(f32) for h in range(NH)]


def _block_diag():
    r = lax.broadcasted_iota(jnp.int32, (VW, KW), 0) // HV
    c = lax.broadcasted_iota(jnp.int32, (VW, KW), 1) // HK
    return jnp.where(r == c, 1.0, 0.0).astype(f32)


def _decay(lr, w2, gb, tri, rev):
    logits = _dot(lr, w2) + gb
    a = _log_sigmoid(logits) * (1.0 / TAU)
    c = _dot_hi(tri, a)
    cl = c[0:1, :] if rev else c[CH - 1:CH, :]
    return logits, c, cl


def _stack_heads(t, hm):
    return jnp.concatenate([t * hm[h] for h in range(NH)], axis=0)


def _chunk_fwd(q, k, v, c, cl, st, tri, hm, bd):
    tri4 = jnp.concatenate([tri] * NH, axis=0)
    qd = q * jnp.exp(c) * QSCALE
    kd = k * jnp.exp(-c)
    kdec = k * jnp.exp(cl - c)
    pst = _nt(_stack_heads(qd, hm), kd) * tri4
    intra = jnp.concatenate([_dot(pst[h * CH:(h + 1) * CH], v[:, h * HV:(h + 1) * HV]) for h in range(NH)], axis=1)
    o = _nt(qd, st) + intra
    st_new = st * jnp.exp(cl) + bd * _tn(v, kdec)
    return o, st_new


def _state_fwd(k, v, c, cl, st, bd):
    return st * jnp.exp(cl) + bd * _tn(v, k * jnp.exp(cl - c))


def _chunk_bwd(q, k, v, c, cl, st0, dst, do, tri, trit, hm, bd):
    ecl = jnp.exp(cl)
    edec = jnp.exp(cl - c)
    kdec = k * edec
    dv = _nt(kdec, dst)
    dkdec = _dot(v, dst)
    dcl = jnp.sum(dst * st0, axis=0, keepdims=True) * ecl + jnp.sum(dkdec * kdec, axis=0, keepdims=True)
    dc = -dkdec * kdec
    dk = dkdec * edec
    dst0 = dst * ecl
    dq = None
    if q is not None:
        tri4 = jnp.concatenate([tri] * NH, axis=0)
        ec, enc = jnp.exp(c), jnp.exp(-c)
        qd = q * ec * QSCALE
        kd = k * enc
        qs = _stack_heads(qd, hm)
        pst = _nt(qs, kd) * tri4
        dpst = jnp.concatenate([_nt(do[:, h * HV:(h + 1) * HV], v[:, h * HV:(h + 1) * HV]) for h in range(NH)], axis=0) * tri4
        dv = dv + jnp.concatenate([_tn(pst[h * CH:(h + 1) * CH], do[:, h * HV:(h + 1) * HV]) for h in range(NH)], axis=1)
        dqd = _dot(do, st0)
        for h in range(NH):
            dqd = dqd + hm[h] * _dot(dpst[h * CH:(h + 1) * CH], kd)
        dkd = _tn(dpst, qs)
        dst0 = dst0 + bd * _tn(do, qd)
        dc = dc + dqd * qd - dkd * kd
        dq = dqd * ec * QSCALE
        dk = dk + dkd * enc
    da = _dot_hi(trit, dc) + dcl
    return dq, dk, dv, da, dst0


def _gla_fwd(p, w2p, gb, s0, rev, tg, name):
    l = p.shape[0]
    nb, nc = l // tg, tg // CH

    def body(q_ref, k_ref, v_ref, lr_ref, w2_ref, gb_ref, s0_ref, o_ref, st_ref, st):
        @pl.when(pl.program_id(0) == 0)
        def _():
            st[...] = s0_ref[...]

        tri, hm, bd = _tri(rev), _head_masks(), _block_diag()

        def step(j, carry):
            jj = (nc - 1 - j) if rev else j
            rows = pl.ds(pl.multiple_of(jj * CH, CH), CH)
            k = k_ref[rows, :].astype(f32)
            _, c, cl = _decay(lr_ref[rows, :], w2_ref[...], gb_ref[...], tri, rev)
            s = st[...]
            st_ref[jj] = s
            o, s_new = _chunk_fwd(q_ref[rows, :].astype(f32), k, v_ref[rows, :].astype(f32), c, cl, s, tri, hm, bd)
            o_ref[rows, :] = o
            st[...] = s_new
            return carry

        lax.fori_loop(0, nc, step, 0)

    blk = (lambda i: nb - 1 - i) if rev else (lambda i: i)
    return pl.pallas_call(
        body, name=name, grid=(nb,),
        out_shape=(jax.ShapeDtypeStruct((l, VW), f32), jax.ShapeDtypeStruct((l // CH, VW, KW), f32)),
        in_specs=[pl.BlockSpec((tg, KW), lambda i: (blk(i), PQ // KW)), pl.BlockSpec((tg, KW), lambda i: (blk(i), PK // KW)),
                  pl.BlockSpec((tg, VW), lambda i: (blk(i), PV // VW)), pl.BlockSpec((tg, LRW), lambda i: (blk(i), PLR // LRW)),
                  pl.BlockSpec((LRW, KW), lambda i: (0, 0)), pl.BlockSpec((1, KW), lambda i: (0, 0)),
                  pl.BlockSpec((VW, KW), lambda i: (0, 0))],
        out_specs=(pl.BlockSpec((tg, VW), lambda i: (blk(i), 0)), pl.BlockSpec((nc, VW, KW), lambda i: (blk(i), 0, 0))),
        scratch_shapes=[pltpu.VMEM((VW, KW), f32)],
        compiler_params=_cp(("arbitrary",)),
    )(p, p, p, p, w2p, gb, s0)


def _gla_bwd(p, do, states, w2p, gb, prev, rev, tg, name):
    l = p.shape[0]
    nb, nc = l // tg, tg // CH
    out_dt = f32 if prev is None else bf16

    def body(*refs):
        q_ref, k_ref, v_ref, lr_ref, do_ref, st_ref, w2_ref, gb_ref = refs[:8]
        refs = refs[8:]
        if prev is not None:
            pq_ref, pl_ref = refs[:2]
            refs = refs[2:]
        dqkv_ref, dlr_ref, dw2_ref, dgb_ref, ds0_ref, dst = refs

        @pl.when(pl.program_id(0) == 0)
        def _():
            dst[...] = jnp.zeros_like(dst)
            dw2_ref[...] = jnp.zeros_like(dw2_ref)
            dgb_ref[...] = jnp.zeros_like(dgb_ref)

        tri, trit, hm, bd = _tri(rev), _tri(not rev), _head_masks(), _block_diag()

        def step(j, carry):
            jj = j if rev else (nc - 1 - j)
            rows = pl.ds(pl.multiple_of(jj * CH, CH), CH)
            k = k_ref[rows, :].astype(f32)
            lr = lr_ref[rows, :]
            logits, c, cl = _decay(lr, w2_ref[...], gb_ref[...], tri, rev)
            dq, dk, dv, da, dst0 = _chunk_bwd(q_ref[rows, :].astype(f32), k, v_ref[rows, :].astype(f32), c, cl,
                                              st_ref[jj], dst[...], do_ref[rows, :].astype(f32), tri, trit, hm, bd)
            dst[...] = dst0
            dlog = da * _sigmoid(-logits) * (1.0 / TAU)
            dlr = _nt(dlog, w2_ref[...])
            dw2_ref[...] += _tn(lr, dlog)
            dgb_ref[...] += jnp.sum(dlog, axis=0, keepdims=True)
            dqkv = jnp.concatenate([dq, dk, dv], axis=1)
            if prev is not None:
                dqkv = dqkv + pq_ref[rows, :]
                dlr = dlr + pl_ref[rows, :]
            dqkv_ref[rows, :] = dqkv.astype(out_dt)
            dlr_ref[rows, :] = dlr.astype(out_dt)
            return carry

        lax.fori_loop(0, nc, step, 0)
        ds0_ref[...] = dst[...]

    blk = (lambda i: i) if rev else (lambda i: nb - 1 - i)
    in_specs = [pl.BlockSpec((tg, KW), lambda i: (blk(i), PQ // KW)), pl.BlockSpec((tg, KW), lambda i: (blk(i), PK // KW)),
                pl.BlockSpec((tg, VW), lambda i: (blk(i), PV // VW)), pl.BlockSpec((tg, LRW), lambda i: (blk(i), PLR // LRW)),
                pl.BlockSpec((tg, VW), lambda i: (blk(i), 0)), pl.BlockSpec((nc, VW, KW), lambda i: (blk(i), 0, 0)),
                pl.BlockSpec((LRW, KW), lambda i: (0, 0)), pl.BlockSpec((1, KW), lambda i: (0, 0))]
    args = [p, p, p, p, do, states, w2p, gb]
    if prev is not None:
        in_specs += [pl.BlockSpec((tg, 2 * KW + VW), lambda i: (blk(i), 0)), pl.BlockSpec((tg, LRW), lambda i: (blk(i), 0))]
        args += list(prev)
    return pl.pallas_call(
        body, name=name, grid=(nb,),
        out_shape=(jax.ShapeDtypeStruct((l, 2 * KW + VW), out_dt), jax.ShapeDtypeStruct((l, LRW), out_dt),
                   jax.ShapeDtypeStruct((LRW, KW), f32), jax.ShapeDtypeStruct((1, KW), f32), jax.ShapeDtypeStruct((VW, KW), f32)),
        in_specs=in_specs,
        out_specs=(pl.BlockSpec((tg, 2 * KW + VW), lambda i: (blk(i), 0)), pl.BlockSpec((tg, LRW), lambda i: (blk(i), 0)),
                   pl.BlockSpec((LRW, KW), lambda i: (0, 0)), pl.BlockSpec((1, KW), lambda i: (0, 0)),
                   pl.BlockSpec((VW, KW), lambda i: (0, 0))),
        scratch_shapes=[pltpu.VMEM((VW, KW), f32)],
        compiler_params=_cp(("arbitrary",)),
    )(*args)


def _ctx_hidden(ctx_ref, g_ref, sc_ref, sh_ref):
    xv = ctx_ref[...]
    r = lax.rsqrt(jnp.mean(xv * xv, axis=-1, keepdims=True) + EPS)
    xn = xv * r
    return xn, xn * (g_ref[...] * (1.0 + sc_ref[...])) + sh_ref[...]


_CTX_W_SPECS = [pl.BlockSpec((D, KW), lambda i: (0, PK // KW)), pl.BlockSpec((D, VW), lambda i: (0, PV // VW)),
                pl.BlockSpec((D, LRW), lambda i: (0, PLR // LRW))]


def _ctx_fwd(ctx, g, scale, shift, w_pad, w2f, w2b, gbf, gbb):
    ncc = CTX // CH

    def body(ctx_ref, g_ref, sc_ref, sh_ref, wk_ref, wv_ref, wl_ref, w2f_ref, w2b_ref, gbf_ref, gbb_ref, sf_ref, sb_ref):
        _, hc = _ctx_hidden(ctx_ref, g_ref, sc_ref, sh_ref)
        k, v, lr = _dot(hc, wk_ref[...]), _dot(hc, wv_ref[...]), _dot(hc, wl_ref[...])
        bd = _block_diag()
        for rev, w2_ref, gb_ref, out in ((False, w2f_ref, gbf_ref, sf_ref), (True, w2b_ref, gbb_ref, sb_ref)):
            tri = _tri(rev)
            st = jnp.zeros((VW, KW), f32)
            for j in (range(ncc - 1, -1, -1) if rev else range(ncc)):
                rows = slice(j * CH, (j + 1) * CH)
                _, c, cl = _decay(lr[rows], w2_ref[...], gb_ref[...], tri, rev)
                st = _state_fwd(k[rows], v[rows], c, cl, st, bd)
            out[...] = st

    vec = pl.BlockSpec((1, D), lambda i: (0, 0))
    w2s = pl.BlockSpec((LRW, KW), lambda i: (0, 0))
    gbs = pl.BlockSpec((1, KW), lambda i: (0, 0))
    sts = pl.BlockSpec((VW, KW), lambda i: (0, 0))
    return pl.pallas_call(
        body, name="ctx_fwd", grid=(1,), out_shape=(jax.ShapeDtypeStruct((VW, KW), f32),) * 2,
        in_specs=[pl.BlockSpec((CTX, D), lambda i: (0, 0)), vec, vec, vec] + _CTX_W_SPECS + [w2s, w2s, gbs, gbs],
        out_specs=(sts, sts), compiler_params=_cp(("arbitrary",)),
    )(ctx, g, scale, shift, w_pad, w_pad, w_pad, w2f, w2b, gbf, gbb)


def _ctx_bwd(ctx, g, scale, shift, w_pad, w2f, w2b, gbf, gbb, dsf, dsb):
    ncc = CTX // CH

    def body(ctx_ref, g_ref, sc_ref, sh_ref, wk_ref, wv_ref, wl_ref, w2f_ref, w2b_ref, gbf_ref, gbb_ref, dsf_ref, dsb_ref,
             dwk_ref, dwv_ref, dwl_ref, dmod_ref, dg_ref, dw2_ref, dgb_ref):
        xn, hc = _ctx_hidden(ctx_ref, g_ref, sc_ref, sh_ref)
        k, v, lr = _dot(hc, wk_ref[...]), _dot(hc, wv_ref[...]), _dot(hc, wl_ref[...])
        bd = _block_diag()
        dk_rows, dv_rows, dl_rows = [None] * ncc, [None] * ncc, [None] * ncc
        for d, (rev, w2_ref, gb_ref, ds_ref) in enumerate(((False, w2f_ref, gbf_ref, dsf_ref), (True, w2b_ref, gbb_ref, dsb_ref))):
            tri = _tri(rev)
            order = list(range(ncc - 1, -1, -1) if rev else range(ncc))
            st, saved = jnp.zeros((VW, KW), f32), {}
            for j in order:
                rows = slice(j * CH, (j + 1) * CH)
                logits, c, cl = _decay(lr[rows], w2_ref[...], gb_ref[...], tri, rev)
                saved[j] = (logits, c, cl, st)
                st = _state_fwd(k[rows], v[rows], c, cl, st, bd)
            dst = ds_ref[...]
            dw2 = jnp.zeros((LRW, KW), f32)
            dgb = jnp.zeros((1, KW), f32)
            for j in reversed(order):
                rows = slice(j * CH, (j + 1) * CH)
                logits, c, cl, st0 = saved[j]
                _, dk, dv, da, dst = _chunk_bwd(None, k[rows], v[rows], c, cl, st0, dst, None, tri, _tri(not rev), None, bd)
                dlog = da * _sigmoid(-logits) * (1.0 / TAU)
                dl = _nt(dlog, w2_ref[...])
                dw2 = dw2 + _tn(lr[rows], dlog)
                dgb = dgb + jnp.sum(dlog, axis=0, keepdims=True)
                dk_rows[j] = dk if dk_rows[j] is None else dk_rows[j] + dk
                dv_rows[j] = dv if dv_rows[j] is None else dv_rows[j] + dv
                dl_rows[j] = dl if dl_rows[j] is None else dl_rows[j] + dl
            dw2_ref[d] = dw2
            dgb_ref[d] = dgb
        dk, dv, dl = (jnp.concatenate(t, axis=0) for t in (dk_rows, dv_rows, dl_rows))
        dwk_ref[...] = _tn(hc, dk)
        dwv_ref[...] = _tn(hc, dv)
        dwl_ref[...] = _tn(hc, dl)
        dh = _nt(dk, wk_ref[...]) + _nt(dv, wv_ref[...]) + _nt(dl, wl_ref[...])
        gx = dh * xn
        dmod_ref[:, 0:D] = jnp.sum(dh, axis=0, keepdims=True)
        dmod_ref[:, D:2 * D] = jnp.sum(gx, axis=0, keepdims=True) * g_ref[...]
        dg_ref[...] = jnp.sum(gx, axis=0, keepdims=True) * (1.0 + sc_ref[...])

    vec = pl.BlockSpec((1, D), lambda i: (0, 0))
    w2s = pl.BlockSpec((LRW, KW), lambda i: (0, 0))
    gbs = pl.BlockSpec((1, KW), lambda i: (0, 0))
    sts = pl.BlockSpec((VW, KW), lambda i: (0, 0))
    full = lambda *s: pl.BlockSpec(s, lambda i: (0,) * len(s))
    return pl.pallas_call(
        body, name="ctx_bwd", grid=(1,),
        out_shape=(jax.ShapeDtypeStruct((D, KW), f32), jax.ShapeDtypeStruct((D, VW), f32), jax.ShapeDtypeStruct((D, LRW), f32),
                   jax.ShapeDtypeStruct((1, 2 * D), f32), jax.ShapeDtypeStruct((1, D), f32),
                   jax.ShapeDtypeStruct((2, LRW, KW), f32), jax.ShapeDtypeStruct((2, 1, KW), f32)),
        in_specs=[pl.BlockSpec((CTX, D), lambda i: (0, 0)), vec, vec, vec] + _CTX_W_SPECS + [w2s, w2s, gbs, gbs, sts, sts],
        out_specs=(full(D, KW), full(D, VW), full(D, LRW), full(1, 2 * D), full(1, D), full(2, LRW, KW), full(2, 1, KW)),
        compiler_params=_cp(("arbitrary",)),
    )(ctx, g, scale, shift, w_pad, w_pad, w_pad, w2f, w2b, gbf, gbb, dsf, dsb)


def _layernorm(va, g, b):
    mu = jnp.mean(va, axis=-1, keepdims=True)
    xc = va - mu
    rstd = lax.rsqrt(jnp.mean(xc * xc, axis=-1, keepdims=True) + EPS)
    vhat = xc * rstd
    return vhat, rstd, vhat * g + b


def _mix_fwd(p, ln_g, ln_b, ws, bs_t):
    l = p.shape[0]
    nch = l // ACH
    half = AW // 2

    def body(p_ref, g_ref, b_ref, ws_ref, bs_ref, sv_ref, va_buf, col_buf, sem):
        cp = pltpu.make_async_copy(p_ref.at[:, pl.ds(PVA, AW)], va_buf, sem)
        cp.start()
        cp.wait()

        def rows_step(n, carry):
            rows = pl.ds(pl.multiple_of(n * ACH, ACH), ACH)
            _, _, vn = _layernorm(va_buf[rows, :].astype(f32), g_ref[...], b_ref[...])
            for gi in range(2):
                sl = slice(gi * ACH, (gi + 1) * ACH)
                sv_ref[rows, sl] = (_dot(ws_ref[gi], vn[:, sl]) + bs_ref[:, gi:gi + 1]).astype(bf16)
            col_buf[0, rows, :] = vn[:, half:half + ACH]
            col_buf[1, rows, :] = vn[:, half + ACH:]
            return carry

        lax.fori_loop(0, nch, rows_step, 0)

        def cols_step(cidx, carry):
            rows = pl.ds(cidx, ACH, stride=GW)
            for gi in range(2, 4):
                col_buf[gi - 2, rows, :] = _dot(ws_ref[gi], col_buf[gi - 2, rows, :]) + bs_ref[:, gi:gi + 1]
            return carry

        lax.fori_loop(0, GW, cols_step, 0)

        def out_step(n, carry):
            rows = pl.ds(pl.multiple_of(n * ACH, ACH), ACH)
            sv_ref[rows, half:half + ACH] = col_buf[0, rows, :].astype(bf16)
            sv_ref[rows, half + ACH:] = col_buf[1, rows, :].astype(bf16)
            return carry

        lax.fori_loop(0, nch, out_step, 0)

    vm = pl.BlockSpec(memory_space=pltpu.VMEM)
    return pl.pallas_call(
        body, name="mix_fwd", out_shape=jax.ShapeDtypeStruct((l, AW), bf16),
        in_specs=[pl.BlockSpec(memory_space=pl.ANY), vm, vm, vm, vm], out_specs=vm,
        scratch_shapes=[pltpu.VMEM((l, AW), bf16), pltpu.VMEM((2, l, ACH), f32), pltpu.SemaphoreType.DMA(())],
        compiler_params=_cp(),
    )(p, ln_g, ln_b, ws, bs_t)


def _mix_bwd(p, dsv, ln_g, ln_b, ws_t):
    l = p.shape[0]
    nch = l // ACH
    half = AW // 2

    def body(p_ref, dsv_hbm, g_ref, b_ref, wst_ref, dva_ref, dws_ref, dbs_ref, dg_ref, db_ref, va_buf, dsv_buf, vn_col, ds_col, sems):
        cp1 = pltpu.make_async_copy(p_ref.at[:, pl.ds(PVA, AW)], va_buf, sems.at[0])
        cp2 = pltpu.make_async_copy(dsv_hbm, dsv_buf, sems.at[1])
        cp1.start()
        cp2.start()
        dws_ref[...] = jnp.zeros_like(dws_ref)
        dbs_ref[...] = jnp.zeros_like(dbs_ref)
        dg_ref[...] = jnp.zeros_like(dg_ref)
        db_ref[...] = jnp.zeros_like(db_ref)
        cp1.wait()
        cp2.wait()

        def rows_step(n, carry):
            rows = pl.ds(pl.multiple_of(n * ACH, ACH), ACH)
            _, _, vn = _layernorm(va_buf[rows, :].astype(f32), g_ref[...], b_ref[...])
            ds = dsv_buf[rows, :].astype(f32)
            for gi in range(2):
                sl = slice(gi * ACH, (gi + 1) * ACH)
                dws_ref[gi] += _nt(ds[:, sl], vn[:, sl])
                dbs_ref[gi] += ds[:, sl]
            for gi in range(2):
                sl = slice(half + gi * ACH, half + (gi + 1) * ACH)
                vn_col[gi, rows, :] = vn[:, sl]
                ds_col[gi, rows, :] = ds[:, sl]
            return carry

        lax.fori_loop(0, nch, rows_step, 0)

        def cols_step(cidx, carry):
            rows = pl.ds(cidx, ACH, stride=GW)
            for gi in range(2, 4):
                ds = ds_col[gi - 2, rows, :]
                dws_ref[gi] += _nt(ds, vn_col[gi - 2, rows, :])
                dbs_ref[gi] += ds
                ds_col[gi - 2, rows, :] = _dot(wst_ref[gi], ds)
            return carry

        lax.fori_loop(0, GW, cols_step, 0)

        def out_step(n, carry):
            rows = pl.ds(pl.multiple_of(n * ACH, ACH), ACH)
            vhat, rstd, _ = _layernorm(va_buf[rows, :].astype(f32), g_ref[...], b_ref[...])
            ds = dsv_buf[rows, :].astype(f32)
            dvn = jnp.concatenate([_dot(wst_ref[0], ds[:, 0:ACH]), _dot(wst_ref[1], ds[:, ACH:half]), ds_col[0, rows, :], ds_col[1, rows, :]], axis=1)
            dg_ref[...] += jnp.sum(dvn * vhat, axis=0, keepdims=True)
            db_ref[...] += jnp.sum(dvn, axis=0, keepdims=True)
            dvh = dvn * g_ref[...]
            dva = rstd * (dvh - jnp.mean(dvh, axis=-1, keepdims=True) - vhat * jnp.mean(dvh * vhat, axis=-1, keepdims=True))
            dva_ref[rows, :] = dva.astype(bf16)
            return carry

        lax.fori_loop(0, nch, out_step, 0)

    vm = pl.BlockSpec(memory_space=pltpu.VMEM)
    hbm = pl.BlockSpec(memory_space=pl.ANY)
    return pl.pallas_call(
        body, name="mix_bwd",
        out_shape=(jax.ShapeDtypeStruct((l, AW), bf16), jax.ShapeDtypeStruct((4, ACH, ACH), f32), jax.ShapeDtypeStruct((4, ACH, ACH), f32),
                   jax.ShapeDtypeStruct((1, AW), f32), jax.ShapeDtypeStruct((1, AW), f32)),
        in_specs=[hbm, hbm, vm, vm, vm], out_specs=(vm, vm, vm, vm, vm),
        scratch_shapes=[pltpu.VMEM((l, AW), bf16), pltpu.VMEM((l, AW), bf16), pltpu.VMEM((2, l, ACH), f32), pltpu.VMEM((2, l, ACH), f32),
                        pltpu.SemaphoreType.DMA((2,))],
        compiler_params=_cp(),
    )(p, dsv, ln_g, ln_b, ws_t)


def _mid(x, tgt, p, o_f, o_b, sv, gate, gf, gb_norm, w_pa, w_pb, w_out, tl):
    l = x.shape[0]

    def body(x_ref, t_ref, zb_ref, ua_ref, za_ref, g1_ref, g2_ref, of_ref, ob_ref, sv_ref, gate_ref, gf_ref, gbn_ref,
             wpa_ref, wpb_ref, wout_ref,
             dx1_ref, dzbua_ref, dzag_ref, dsv_ref, do_ref, dwout_ref, dwpa_ref, dwpb_ref, dgf_ref, dgate_ref, dgbn_ref, loss_ref):
        @pl.when(pl.program_id(0) == 0)
        def _():
            for r in (dwout_ref, dwpa_ref, dwpb_ref, dgf_ref, dgate_ref, dgbn_ref, loss_ref):
                r[...] = jnp.zeros_like(r)

        o = of_ref[...] + ob_ref[...]
        rr = jnp.concatenate(
            [jnp.broadcast_to(lax.rsqrt(jnp.mean(o[:, h * HV:(h + 1) * HV] ** 2, axis=-1, keepdims=True) + EPS), (tl, HV))
             for h in range(NH)], axis=1)
        ohat = o * rr
        on = ohat * gbn_ref[...]
        szb, dszb = _silu_and_grad(zb_ref[...].astype(f32))
        tb = on * szb
        u = ua_ref[...].astype(f32)
        svv = sv_ref[...].astype(f32)
        sza, dsza = _silu_and_grad(za_ref[...].astype(f32))
        ta = u * svv * sza
        ya = _dot(ta, wpa_ref[...])
        yb = _dot(tb, wpb_ref[...])
        g1 = _sigmoid(g1_ref[...].astype(f32))
        g2 = _sigmoid(g2_ref[...].astype(f32))
        m = g1 * ya + g2 * yb
        y2 = _dot(m, wout_ref[...])
        x1 = x_ref[...] + gate_ref[...] * y2
        r1 = lax.rsqrt(jnp.mean(x1 * x1, axis=-1, keepdims=True) + EPS)
        x1n = x1 * r1
        err = x1n * gf_ref[...] - t_ref[...]
        loss_ref[...] += jnp.sum(jnp.sum(err * err, axis=-1, keepdims=True), axis=0, keepdims=True) * (0.5 / D)
        dout = err * (1.0 / D)
        dgf_ref[...] += jnp.sum(dout * x1n, axis=0, keepdims=True)
        dx1n = dout * gf_ref[...]
        dx1 = r1 * (dx1n - x1n * jnp.mean(dx1n * x1n, axis=-1, keepdims=True))
        dx1_ref[...] = dx1
        dgate_ref[...] += jnp.sum(dx1 * y2, axis=0, keepdims=True)
        dy2 = dx1 * gate_ref[...]
        dwout_ref[...] += _tn(m, dy2)
        dm = _nt(dy2, wout_ref[...])
        dya = dm * g1
        dyb = dm * g2
        dzag_ref[:, AW:AW + D] = (dm * ya * g1 * (1.0 - g1)).astype(bf16)
        dzag_ref[:, AW + D:] = (dm * yb * g2 * (1.0 - g2)).astype(bf16)
        dwpa_ref[...] += _tn(ta, dya)
        dta = _nt(dya, wpa_ref[...])
        dzbua_ref[:, AW:] = (dta * svv * sza).astype(bf16)
        dsv_ref[...] = (dta * u * sza).astype(bf16)
        dzag_ref[:, 0:AW] = (dta * u * svv * dsza).astype(bf16)
        dwpb_ref[...] += _tn(tb, dyb)
        dtb = _nt(dyb, wpb_ref[...])
        don = dtb * szb
        dzbua_ref[:, 0:AW] = (dtb * on * dszb).astype(bf16)
        dgbn_ref[...] += jnp.sum(don * ohat, axis=0, keepdims=True)
        doh = don * gbn_ref[...]
        prod = doh * ohat
        mh = jnp.concatenate(
            [jnp.broadcast_to(jnp.mean(prod[:, h * HV:(h + 1) * HV], axis=-1, keepdims=True), (tl, HV)) for h in range(NH)], axis=1)
        do_ref[...] = (rr * (doh - ohat * mh)).astype(bf16)

    row = lambda w, j: pl.BlockSpec((tl, w), lambda i, j=j: (i, j))
    full = lambda *s: pl.BlockSpec(s, lambda i: (0,) * len(s))
    return pl.pallas_call(
        body, name="mid", grid=(l // tl,),
        out_shape=(jax.ShapeDtypeStruct((l, D), f32), jax.ShapeDtypeStruct((l, 2 * AW), bf16), jax.ShapeDtypeStruct((l, AW + 2 * D), bf16),
                   jax.ShapeDtypeStruct((l, AW), bf16), jax.ShapeDtypeStruct((l, VW), bf16),
                   jax.ShapeDtypeStruct((D, D), f32), jax.ShapeDtypeStruct((AW, D), f32), jax.ShapeDtypeStruct((VW, D), f32),
                   jax.ShapeDtypeStruct((1, D), f32), jax.ShapeDtypeStruct((1, D), f32), jax.ShapeDtypeStruct((1, VW), f32),
                   jax.ShapeDtypeStruct((1, 1), f32)),
        in_specs=[row(D, 0), row(D, 0), row(AW, PZB // AW), row(AW, PUA // AW), row(AW, PZA // AW), row(D, PG1 // D), row(D, PG2 // D),
                  row(VW, 0), row(VW, 0), row(AW, 0), full(1, D), full(1, D), full(1, VW), full(AW, D), full(VW, D), full(D, D)],
        out_specs=(row(D, 0), row(2 * AW, 0), row(AW + 2 * D, 0), row(AW, 0), row(VW, 0),
                   full(D, D), full(AW, D), full(VW, D), full(1, D), full(1, D), full(1, VW), full(1, 1)),
        compiler_params=_cp(("arbitrary",)),
    )(x, tgt, p, p, p, p, p, o_f, o_b, sv, gate, gf, gb_norm, w_pa, w_pb, w_out)


def _in_bwd(x, dx1, dqkv, dzbua, dva, dzag, dlr, wt_pad, g, scale, tl):
    l = x.shape[0]

    def body(x_ref, dx1_ref, a_ref, b_ref, c_ref, e_ref, lr_ref, wa_ref, wb_ref, wc_ref, we_ref, wl_ref, g_ref, sc_ref,
             gx_ref, dsh_ref, dsc_ref, dg_ref):
        @pl.when(pl.program_id(0) == 0)
        def _():
            for r in (dsh_ref, dsc_ref, dg_ref):
                r[...] = jnp.zeros_like(r)

        dh = (jnp.dot(a_ref[...], wa_ref[...], preferred_element_type=f32) + jnp.dot(b_ref[...], wb_ref[...], preferred_element_type=f32)
              + jnp.dot(c_ref[...], wc_ref[...], preferred_element_type=f32) + jnp.dot(e_ref[...], we_ref[...], preferred_element_type=f32)
              + jnp.dot(lr_ref[...], wl_ref[...], preferred_element_type=f32))
        xv = x_ref[...]
        r = lax.rsqrt(jnp.mean(xv * xv, axis=-1, keepdims=True) + EPS)
        xn = xv * r
        gxn = jnp.sum(dh * xn, axis=0, keepdims=True)
        dsh_ref[...] += jnp.sum(dh, axis=0, keepdims=True)
        dsc_ref[...] += gxn * g_ref[...]
        dg_ref[...] += gxn * (1.0 + sc_ref[...])
        dxn = dh * (g_ref[...] * (1.0 + sc_ref[...]))
        gx_ref[...] = dx1_ref[...] + r * (dxn - xn * jnp.mean(dxn * xn, axis=-1, keepdims=True))

    row = lambda w: pl.BlockSpec((tl, w), lambda i: (i, 0))
    wrow = lambda w, j: pl.BlockSpec((w, D), lambda i, j=j: (j, 0))
    vec = pl.BlockSpec((1, D), lambda i: (0, 0))
    return pl.pallas_call(
        body, name="in_bwd", grid=(l // tl,),
        out_shape=(jax.ShapeDtypeStruct((l, D), f32),) + (jax.ShapeDtypeStruct((1, D), f32),) * 3,
        in_specs=[row(D), row(D), row(2 * KW + VW), row(2 * AW), row(AW), row(AW + 2 * D), row(LRW),
                  wrow(2 * KW + VW, 0), wrow(2 * AW, PZB // (2 * AW)), wrow(AW, PVA // AW), wrow(AW + 2 * D, PZA // (AW + 2 * D)),
                  wrow(LRW, PLR // LRW), vec, vec],
        out_specs=(row(D), vec, vec, vec),
        compiler_params=_cp(("arbitrary",)),
    )(x, dx1, dqkv, dzbua, dva, dzag, dlr, wt_pad, wt_pad, wt_pad, wt_pad, wt_pad, g, scale)


def _tn_matmul(a, b, tl, name):
    l, m = a.shape
    n = b.shape[1]

    def body(a_ref, b_ref, o_ref):
        @pl.when(pl.program_id(0) == 0)
        def _():
            o_ref[...] = jnp.zeros_like(o_ref)

        o_ref[...] += _tn(a_ref[...], b_ref[...])

    return pl.pallas_call(
        body, name=name, grid=(l // tl,), out_shape=jax.ShapeDtypeStruct((m, n), f32),
        in_specs=[pl.BlockSpec((tl, m), lambda i: (i, 0)), pl.BlockSpec((tl, n), lambda i: (i, 0))],
        out_specs=pl.BlockSpec((m, n), lambda i: (0, 0)), compiler_params=_cp(("arbitrary",)),
    )(a, b)


def _pad_gate(w2, gb):
    z = jnp.zeros((RANK, KW), f32)
    tail = jnp.zeros((LRW - 2 * RANK, KW), f32)
    w2f = jnp.concatenate([w2[0], z, tail], axis=0)
    w2b = jnp.concatenate([z, w2[1], tail], axis=0)
    return w2f, w2b, gb[0:1], gb[1:2]


def _local_step(x, ctx, tgt, mod, modc, norm_g, w_pad, wt_pad, ln_g, ln_b, ws, bs, w2, gb, gb_norm, w_pa, w_pb, w_out, gf):
    shift, scale, gate = mod[:, 0:D], mod[:, D:2 * D], mod[:, 2 * D:]
    shift_c, scale_c = modc[:, 0:D], modc[:, D:]
    w2f, w2b, gbf, gbb = _pad_gate(w2, gb)

    p, h = _in_proj(x, norm_g, scale, shift, w_pad, 256)
    sc_f, sc_b = _ctx_fwd(ctx, norm_g, scale_c, shift_c, w_pad, w2f, w2b, gbf, gbb)
    o_f, st_f = _gla_fwd(p, w2f, gbf, sc_f, False, 512, "gla_fwd_f")
    o_b, st_b = _gla_fwd(p, w2b, gbb, sc_b, True, 512, "gla_fwd_b")
    sv = _mix_fwd(p, ln_g, ln_b, ws.astype(bf16), bs.T)
    (dx1, dzbua, dzag, dsv, do, dw_out, dw_pa, dw_pb, dgf, dgate, dgbn, loss) = _mid(
        x, tgt, p, o_f, o_b, sv, gate, gf, gb_norm, w_pa, w_pb, w_out, 256)
    dva, dws, dbs_acc, dln_g, dln_b = _mix_bwd(p, dsv, ln_g, ln_b, jnp.swapaxes(ws, 1, 2).astype(bf16))
    dqkv_f, dlr_f, dw2f, dgbf, dsc_f = _gla_bwd(p, do, st_f, w2f, gbf, None, False, 512, "gla_bwd_f")
    dqkv, dlr, dw2b, dgbb, dsc_b = _gla_bwd(p, do, st_b, w2b, gbb, (dqkv_f, dlr_f), True, 512, "gla_bwd_b")
    dwk_c, dwv_c, dwl_c, dmodc, dg_c, dw2c, dgbc = _ctx_bwd(ctx, norm_g, scale_c, shift_c, w_pad, w2f, w2b, gbf, gbb, dsc_f, dsc_b)
    gx, dshift, dscale, dg = _in_bwd(x, dx1, dqkv, dzbua, dva, dzag, dlr, wt_pad, norm_g, scale, 256)
    dw_qkv = _tn_matmul(h, dqkv, 512, "dw_qkv")
    dw_zbua = _tn_matmul(h, dzbua, 512, "dw_zbua")
    dw_va = _tn_matmul(h, dva, 512, "dw_va")
    dw_zag = _tn_matmul(h, dzag, 512, "dw_zag")
    dw_lr = _tn_matmul(h, dlr, 512, "dw_lr")

    dw_qkv = dw_qkv + jnp.concatenate([jnp.zeros((D, KW), f32), dwk_c, dwv_c], axis=1)
    dw_lr = dw_lr + dwl_c
    dw_in = jnp.concatenate([dw_qkv, dw_lr[:, 0:2 * RANK], dw_zbua, dw_va, dw_zag], axis=1)
    dw2 = jnp.stack([dw2f[0:RANK] + dw2c[0, 0:RANK], dw2b[RANK:2 * RANK] + dw2c[1, RANK:2 * RANK]])
    dgb = jnp.concatenate([dgbf + dgbc[0], dgbb + dgbc[1]], axis=0)
    return dict(loss=loss, gx=gx, dmod=jnp.concatenate([dshift, dscale, dgate], axis=1), dmodc=dmodc, dnorm_g=dg + dg_c,
                dw_in=dw_in, dln_g=dln_g, dln_b=dln_b, dws=dws, dbs=jnp.sum(dbs_acc, axis=-1), dw2=dw2, dgb=dgb, dgbn=dgbn,
                dw_pa=dw_pa, dw_pb=dw_pb, dw_out=dw_out, dgf=dgf)


def _rows128(*vs):
    out = []
    for t in vs:
        t = t.reshape(-1)
        pad = (-t.shape[0]) % 128
        out.append(jnp.pad(t, (0, pad)) if pad else t)
    return jnp.concatenate(out).reshape(-1, 128)


def _flat_shards(w_in, w_pa, w_pb, w_out, w2, gb):
    flat = _rows128(w_in, w_pa, w_pb, w_out, jnp.concatenate([w2.reshape(-1), gb.reshape(-1)]))
    return jnp.pad(flat, ((0, FLAT_ROWS - flat.shape[0]), (0, 0)))


_FLAT_SPLITS = ((D * 644) // 128, (AW * 128) // 128, (VW * 128) // 128, (128 * D) // 128)


def _unflat_shards(flat):
    r0, r1, r2, r3 = _FLAT_SPLITS
    o1, o2, o3 = r0 + r1, r0 + r1 + r2, r0 + r1 + r2 + r3
    tail = flat[o3:o3 + 9].reshape(-1)
    return (flat[0:r0].reshape(1, D, 644), flat[r0:o1].reshape(1, AW, 128), flat[o1:o2].reshape(1, VW, 128),
            flat[o2:o3].reshape(1, 128, D), tail[0:2 * RANK * 32].reshape(1, 2, RANK, 32), tail[2 * RANK * 32:2 * RANK * 32 + 64].reshape(1, 2, 32))


def kernel(x, c, ctx, c_ctx, w_mod, b_mod, norm_g, w_in, a_ln_g, a_ln_b, a_ws, a_bs, b_gate_w2, b_gate_b, b_norm_g, w_proj_a, w_proj_b, w_out, final_norm_g, loss_target, m_c_ctx, m_w_mod, m_b_mod, m_norm_g, m_w_in, m_a_ln_g, m_a_ln_b, m_a_ws, m_a_bs, m_b_gate_w2, m_b_gate_b, m_b_norm_g, m_w_proj_a, m_w_proj_b, m_w_out, m_final_norm_g, v_c_ctx, v_w_mod, v_b_mod, v_norm_g, v_w_in, v_a_ln_g, v_a_ln_b, v_a_ws, v_a_bs, v_b_gate_w2, v_b_gate_b, v_b_norm_g, v_w_proj_a, v_w_proj_b, v_w_out, v_final_norm_g):
    me = _me()
    ncol = w_mod.shape[2]

    early = _all_gather(_rows128(c), "gather_c")
    wflat = _all_gather(_flat_shards(w_in[0], w_proj_a[0], w_proj_b[0], w_out[0], b_gate_w2[0], b_gate_b[0]).astype(bf16), "gather_w")
    gates = _all_gather(_rows128(b_gate_w2, b_gate_b), "gather_gate")
    cs = jnp.concatenate([early.reshape(NDEV, D), c_ctx.reshape(1, D), jnp.zeros((7, D), f32)], axis=0)

    r0, r1, r2, r3 = _FLAT_SPLITS
    w_full = jnp.transpose(wflat[:, 0:r0].reshape(NDEV, D, 644), (1, 0, 2)).reshape(D, INW)
    w_pad = jnp.concatenate([w_full[:, 0:1024], w_full[:, 1056:3104], w_full[:, 3104:5152], w_full[:, 1024:1056],
                             jnp.zeros((D, LRW - 2 * RANK), bf16)], axis=1)
    wt_pad = w_pad.T
    w_pa = jnp.transpose(wflat[:, r0:r0 + r1].reshape(NDEV, AW, 128), (1, 0, 2)).reshape(AW, D)
    w_pb = jnp.transpose(wflat[:, r0 + r1:r0 + r1 + r2].reshape(NDEV, VW, 128), (1, 0, 2)).reshape(VW, D)
    w_o = wflat[:, r0 + r1 + r2:r0 + r1 + r2 + r3].reshape(D, D)
    gflat = gates.reshape(NDEV, 9 * 128)
    w2 = jnp.transpose(gflat[:, 0:2 * RANK * 32].reshape(NDEV, 2, RANK, 32), (1, 2, 0, 3)).reshape(2, RANK, KW)
    gb = jnp.transpose(gflat[:, 2 * RANK * 32:2 * RANK * 32 + 64].reshape(NDEV, 2, 32), (1, 0, 2)).reshape(2, KW)

    bm_mine = lax.dynamic_slice(b_mod, (0, me * ncol), (1, ncol))
    mods = _all_gather(_mod_fwd(cs, w_mod[0], bm_mine), "gather_mod")
    mods = jnp.transpose(mods, (1, 0, 2)).reshape(16, 3 * D)
    mod = lax.dynamic_slice(mods, (me, 0), (1, 3 * D))
    modc = mods[8:9, 0:2 * D]

    r = _local_step(x[0], ctx[0], loss_target[0], mod, modc, norm_g, w_pad, wt_pad, a_ln_g, a_ln_b, a_ws[0], a_bs[0], w2, gb,
                    b_norm_g, w_pa, w_pb, w_o, final_norm_g.reshape(1, D))

    dbm = r["dmod"] + jnp.concatenate([r["dmodc"], jnp.zeros((1, D), f32)], axis=1)
    small = _rows128(r["dnorm_g"], r["dln_g"], r["dln_b"], r["dws"], r["dbs"], r["dgbn"], r["dgf"], dbm,
                     r["dmod"], r["dmodc"], jnp.broadcast_to(r["loss"], (1, 128)))
    n_rep = (D + AW + AW + 4 * ACH * ACH + AW + VW + D + 3 * D) // 128
    smalls = _all_gather(small, "gather_small")
    w_rep = _rows128(norm_g, a_ln_g, a_ln_b, a_ws, a_bs, b_norm_g, final_norm_g, b_mod)
    m_rep = _rows128(m_norm_g, m_a_ln_g, m_a_ln_b, m_a_ws, m_a_bs, m_b_norm_g, m_final_norm_g, m_b_mod)
    v_rep = _rows128(v_norm_g, v_a_ln_g, v_a_ln_b, v_a_ws, v_a_bs, v_b_norm_g, v_final_norm_g, v_b_mod)
    rep = _adam(smalls[:, 0:n_rep], w_rep, m_rep, v_rep, n_rep, "adam_rep")
    tail = smalls[:, n_rep:].reshape(NDEV, -1)
    dmods = tail[:, 0:3 * D]
    dmodc_all = tail[:, 3 * D:5 * D]
    dmodc_tot = dmodc_all[0:1]
    for i in range(1, NDEV):
        dmodc_tot = dmodc_tot + dmodc_all[i:i + 1]
    losses = tail[:, 5 * D]
    loss = losses[0]
    for i in range(1, NDEV):
        loss = loss + losses[i]

    def split_rep(t):
        t = t.reshape(-1)
        sizes = (D, AW, AW, 4 * ACH * ACH, AW, VW, D, 3 * D)
        shapes = ((1, D), (1, AW), (1, AW), (1, 4, ACH, ACH), (1, 4, ACH), (1, VW), (D,), (1, 3 * D))
        out, o = [], 0
        for s, sh in zip(sizes, shapes):
            out.append(t[o:o + s].reshape(sh))
            o += s
        return out

    rep = [split_rep(t) for t in rep]

    dm_rows = jnp.concatenate([dmods, jnp.concatenate([dmodc_tot, jnp.zeros((1, D), f32)], axis=1), jnp.zeros((7, 3 * D), f32)], axis=0)
    dm_mine = lax.dynamic_slice(dm_rows, (0, me * ncol), (16, ncol))
    g_wmod, gc_part = _mod_bwd(cs, dm_mine, w_mod[0])
    wm = _adam(g_wmod[None], w_mod[0], m_w_mod[0], v_w_mod[0], 256, "adam_wmod")
    gcs = _all_gather(gc_part.reshape(8, 128), "gather_cctx")
    cc = _adam(gcs, c_ctx.reshape(8, 128), m_c_ctx.reshape(8, 128), v_c_ctx.reshape(8, 128), 8, "adam_cctx")

    dw_in = jnp.transpose(r["dw_in"].reshape(D, NDEV, 644), (1, 0, 2)).reshape(NDEV, r0, 128)
    dw_pa = jnp.transpose(r["dw_pa"].reshape(AW, NDEV, 128), (1, 0, 2))
    dw_pb = jnp.transpose(r["dw_pb"].reshape(VW, NDEV, 128), (1, 0, 2))
    dw_o = r["dw_out"].reshape(NDEV, r3, 128)
    dgt = jnp.concatenate([jnp.transpose(r["dw2"].reshape(2, RANK, NDEV, 32), (2, 0, 1, 3)).reshape(NDEV, 2 * RANK * 32),
                           jnp.transpose(r["dgb"].reshape(2, NDEV, 32), (1, 0, 2)).reshape(NDEV, 64),
                           jnp.zeros((NDEV, 64), f32)], axis=1).reshape(NDEV, 9, 128)
    used = r0 + r1 + r2 + r3 + 9
    blocks = jnp.concatenate([dw_in, dw_pa, dw_pb, dw_o, dgt, jnp.zeros((NDEV, FLAT_ROWS - used, 128), f32)], axis=1)
    parts = _all_to_all(blocks, "exchange_grads")
    sh = _adam(parts, _flat_shards(w_in[0], w_proj_a[0], w_proj_b[0], w_out[0], b_gate_w2[0], b_gate_b[0]),
               _flat_shards(m_w_in[0], m_w_proj_a[0], m_w_proj_b[0], m_w_out[0], m_b_gate_w2[0], m_b_gate_b[0]),
               _flat_shards(v_w_in[0], v_w_proj_a[0], v_w_proj_b[0], v_w_out[0], v_b_gate_w2[0], v_b_gate_b[0]), 512, "adam_sharded")
    sh = [_unflat_shards(t) for t in sh]

    outs = [loss, r["gx"][None]]
    for k in range(4):
        n_g, lg, lb, aws, abs_, bng, fng, bmod = rep[k]
        s_in, s_pa, s_pb, s_out, s_w2, s_gb = sh[k]
        outs += [cc[k].reshape(D), wm[k][None], bmod, n_g, s_in, lg, lb, aws, abs_, s_w2, s_gb, bng, s_pa, s_pb, s_out, fng]
    return tuple(outs)
```

```python
import functools

import jax
import jax.numpy as jnp
from jax import lax
from jax.experimental import pallas as pl
from jax.experimental.pallas import tpu as pltpu

f32, bf16 = jnp.float32, jnp.bfloat16

D = 1024
CTX = 256
EPS = 1e-6
AW = 512
ACH = 128
GW = 64
KW = 256
VW = 512
NH = 4
HK = 64
HV = 128
RANK = 16
TAU = 16.0
CH = 64
QSCALE = HK ** -0.5
INW = 5152
NDEV = 8

PQ, PK, PV, PZB, PUA, PVA, PZA, PG1, PG2, PLR, PW = 0, 256, 512, 1024, 1536, 2048, 2560, 3072, 4096, 5120, 5248
LRW = 128

ADAM_LR, ADAM_B1, ADAM_B2, ADAM_EPS, ADAM_WD, ADAM_STEP = 0.001, 0.9, 0.999, 1e-08, 0.01, 10

VMEM_LIMIT = 56 * 1024 * 1024
MESH = pl.DeviceIdType.MESH


def _cp(sem=None):
    return pltpu.CompilerParams(dimension_semantics=sem, vmem_limit_bytes=VMEM_LIMIT)


def _dot(a, b):
    return jnp.dot(a.astype(bf16), b.astype(bf16), preferred_element_type=f32)


def _nt(a, b):
    return lax.dot_general(a.astype(bf16), b.astype(bf16), (((1,), (1,)), ((), ())), preferred_element_type=f32)


def _tn(a, b):
    return lax.dot_general(a.astype(bf16), b.astype(bf16), (((0,), (0,)), ((), ())), preferred_element_type=f32)


def _dot_hi(a, b):
    return jnp.dot(a, b, preferred_element_type=f32, precision=lax.Precision.HIGHEST)


def _sigmoid(x):
    return 1.0 / (1.0 + jnp.exp(-x))


def _log_sigmoid(x):
    return jnp.minimum(x, 0.0) - jnp.log(1.0 + jnp.exp(-jnp.abs(x)))


def _silu_and_grad(z):
    s = _sigmoid(z)
    return z * s, s * (1.0 + z * (1.0 - s))


def _me():
    return 4 * lax.axis_index("x") + 2 * lax.axis_index("y") + lax.axis_index("c")


def _peer(k):
    x, y, c = lax.axis_index("x"), lax.axis_index("y"), lax.axis_index("c")
    px = 1 - x if k & 4 else x
    py = 1 - y if k & 2 else y
    pc = 1 - c if k & 1 else c
    return (px, py, pc), 4 * px + 2 * py + pc


def _all_gather(v, name):
    r, c = v.shape

    def body(v_ref, out_ref, send_sems, recv_sems, local_sem):
        me = _me()
        mine = pltpu.make_async_copy(v_ref, out_ref.at[me], local_sem)
        mine.start()
        sends, recvs = [], []
        for k in range(1, NDEV):
            dev, idx = _peer(k)
            sends.append(pltpu.make_async_remote_copy(
                src_ref=v_ref, dst_ref=out_ref.at[me], send_sem=send_sems.at[k - 1], recv_sem=recv_sems.at[k - 1],
                device_id=dev, device_id_type=MESH))
            recvs.append(pltpu.make_async_remote_copy(
                src_ref=v_ref, dst_ref=out_ref.at[idx], send_sem=send_sems.at[k - 1], recv_sem=recv_sems.at[k - 1],
                device_id=dev, device_id_type=MESH))
        for cp in sends:
            cp.start()
        for cp in recvs:
            cp.wait_recv()
        for cp in sends:
            cp.wait_send()
        mine.wait()

    return pl.pallas_call(
        body, name=name, out_shape=jax.ShapeDtypeStruct((NDEV, r, c), v.dtype),
        in_specs=[pl.BlockSpec(memory_space=pl.ANY)], out_specs=pl.BlockSpec(memory_space=pl.ANY),
        scratch_shapes=[pltpu.SemaphoreType.DMA((NDEV - 1,)), pltpu.SemaphoreType.DMA((NDEV - 1,)), pltpu.SemaphoreType.DMA(())],
    )(v)


def _fanout(srcs, dsts, send_sems, recv_sems, local_sems):
    me = _me()
    n = len(srcs)
    local = [pltpu.make_async_copy(srcs[a](me), dsts[a](me), local_sems.at[a]) for a in range(n)]
    sends, recvs = [], []
    for k in range(1, NDEV):
        dev, idx = _peer(k)
        for a in range(n):
            s = (k - 1) * n + a
            sends.append(pltpu.make_async_remote_copy(
                src_ref=srcs[a](idx), dst_ref=dsts[a](me), send_sem=send_sems.at[s], recv_sem=recv_sems.at[s],
                device_id=dev, device_id_type=MESH))
            recvs.append(pltpu.make_async_remote_copy(
                src_ref=srcs[a](idx), dst_ref=dsts[a](idx), send_sem=send_sems.at[s], recv_sem=recv_sems.at[s],
                device_id=dev, device_id_type=MESH))

    def start():
        for cp in local + sends:
            cp.start()

    def finish():
        for cp in recvs:
            cp.wait_recv()
        for cp in sends:
            cp.wait_send()
        for cp in local:
            cp.wait()

    return start, finish


def _fanout_sems(n):
    return [pltpu.SemaphoreType.DMA(((NDEV - 1) * n,)), pltpu.SemaphoreType.DMA(((NDEV - 1) * n,)), pltpu.SemaphoreType.DMA((n,))]


def _lanes(j):
    return pl.ds(pl.multiple_of(j * 128, 128), 128)


def _gather_weights(c_rows, wi, pa, pb, wo, gate):
    def body(c_ref, wi_ref, pa_ref, pb_ref, wo_ref, g_ref, oc, owi, opa, opb, owo, og, send_sems, recv_sems, local_sems):
        srcs = [lambda j, r=r: r for r in (c_ref, wi_ref, pa_ref, pb_ref, wo_ref, g_ref)]
        dsts = [lambda j: oc.at[j], lambda j: owi.at[j], lambda j: opa.at[:, _lanes(j)], lambda j: opb.at[:, _lanes(j)],
                lambda j: owo.at[j], lambda j: og.at[j]]
        start, finish = _fanout(srcs, dsts, send_sems, recv_sems, local_sems)
        start()
        finish()

    hbm = pl.BlockSpec(memory_space=pl.ANY)
    return pl.pallas_call(
        body, name="gather_weights",
        out_shape=(jax.ShapeDtypeStruct((NDEV,) + c_rows.shape, f32), jax.ShapeDtypeStruct((NDEV,) + wi.shape, bf16),
                   jax.ShapeDtypeStruct((AW, D), bf16), jax.ShapeDtypeStruct((VW, D), bf16),
                   jax.ShapeDtypeStruct((NDEV,) + wo.shape, bf16), jax.ShapeDtypeStruct((NDEV,) + gate.shape, f32)),
        in_specs=[hbm] * 6, out_specs=(hbm,) * 6, scratch_shapes=_fanout_sems(6),
    )(c_rows, wi, pa, pb, wo, gate)


def _exchange_grads(blocks, dw_pa, dw_pb, dw_out, dgt):
    def body(b_ref, pa_ref, pb_ref, wo_ref, g_ref, ob, opa, opb, owo, og, send_sems, recv_sems, local_sems):
        srcs = [lambda j: b_ref.at[j], lambda j: pa_ref.at[:, _lanes(j)], lambda j: pb_ref.at[:, _lanes(j)],
                lambda j: wo_ref.at[j], lambda j: g_ref.at[j]]
        dsts = [lambda j, r=r: r.at[j] for r in (ob, opa, opb, owo, og)]
        start, finish = _fanout(srcs, dsts, send_sems, recv_sems, local_sems)
        start()
        finish()

    hbm = pl.BlockSpec(memory_space=pl.ANY)
    return pl.pallas_call(
        body, name="exchange_grads",
        out_shape=(jax.ShapeDtypeStruct(blocks.shape, f32), jax.ShapeDtypeStruct((NDEV, AW, 128), f32),
                   jax.ShapeDtypeStruct((NDEV, VW, 128), f32), jax.ShapeDtypeStruct(dw_out.shape, f32),
                   jax.ShapeDtypeStruct(dgt.shape, f32)),
        in_specs=[hbm] * 5, out_specs=(hbm,) * 5, scratch_shapes=_fanout_sems(5),
    )(blocks, dw_pa, dw_pb, dw_out, dgt)


SHARD = INW // NDEV
ROWS_RP = 128


def _overlap(lo, hi, a, b):
    s, e = max(lo, a), min(hi, b)
    return (s, e) if s < e else None


def _repack_w(wg):
    segs = ((0, 1024, PQ), (1024, 1024 + 2 * RANK, PLR), (1024 + 2 * RANK, INW, PZB))

    def body(g_ref, o_ref):
        for j in range(NDEV):
            lo, hi = j * SHARD, (j + 1) * SHARD
            for a, b, pad0 in segs:
                ov = _overlap(lo, hi, a, b)
                if ov:
                    s, e = ov
                    o_ref[:, pad0 + s - a:pad0 + e - a] = g_ref[j, :, s - lo:e - lo]
        o_ref[:, PLR + 2 * RANK:PW] = jnp.zeros((ROWS_RP, PW - PLR - 2 * RANK), bf16)

    return pl.pallas_call(
        body, name="repack_w", grid=(D // ROWS_RP,), out_shape=jax.ShapeDtypeStruct((D, PW), bf16),
        in_specs=[pl.BlockSpec((NDEV, ROWS_RP, SHARD), lambda i: (0, i, 0))],
        out_specs=pl.BlockSpec((ROWS_RP, PW), lambda i: (i, 0)), compiler_params=_cp(("arbitrary",)),
    )(wg)


def _pack_dw(dw_qkv, dw_lr, dw_zbua, dw_va, dw_zag, dwk_c, dwv_c, dwl_c):
    def body(qkv_ref, lr_ref, zbua_ref, va_ref, zag_ref, kc_ref, vc_ref, lc_ref, o_ref):
        qkv = qkv_ref[...] + jnp.concatenate([jnp.zeros((ROWS_RP, KW), f32), kc_ref[...], vc_ref[...]], axis=1)
        lr = lr_ref[...] + lc_ref[...]
        srcs = ((0, 1024, qkv), (1024, 1024 + 2 * RANK, lr), (1056, 2080, zbua_ref), (2080, 2592, va_ref), (2592, INW, zag_ref))
        for j in range(NDEV):
            lo, hi = j * SHARD, (j + 1) * SHARD
            for a, b, src in srcs:
                ov = _overlap(lo, hi, a, b)
                if ov:
                    s, e = ov
                    o_ref[j, :, s - lo:e - lo] = src[:, s - a:e - a]

    row = lambda w: pl.BlockSpec((ROWS_RP, w), lambda i: (i, 0))
    return pl.pallas_call(
        body, name="pack_dw", grid=(D // ROWS_RP,), out_shape=jax.ShapeDtypeStruct((NDEV, D, SHARD), f32),
        in_specs=[row(2 * KW + VW), row(LRW), row(2 * AW), row(AW), row(AW + 2 * D), row(KW), row(VW), row(LRW)],
        out_specs=pl.BlockSpec((NDEV, ROWS_RP, SHARD), lambda i: (0, i, 0)), compiler_params=_cp(("arbitrary",)),
    )(dw_qkv, dw_lr, dw_zbua, dw_va, dw_zag, dwk_c, dwv_c, dwl_c)


def _mod_fwd(cs, wm, bm):
    def body(cs_ref, wm_ref, bm_ref, o_ref):
        s, _ = _silu_and_grad(cs_ref[...])
        o_ref[...] = _dot_hi(s, wm_ref[...]) + bm_ref[...]

    return pl.pallas_call(body, name="mod_fwd", out_shape=jax.ShapeDtypeStruct((16, wm.shape[1]), f32),
                          compiler_params=_cp())(cs, wm, bm)


def _mod_bwd(cs, dm, wm):
    def body(cs_ref, dm_ref, wm_ref, gw_ref, gc_ref):
        s, ds = _silu_and_grad(cs_ref[...])
        gw_ref[...] = lax.dot_general(s, dm_ref[...], (((0,), (0,)), ((), ())), preferred_element_type=f32,
                                      precision=lax.Precision.HIGHEST)
        part = lax.dot_general(dm_ref[8:9, :], wm_ref[...], (((1,), (1,)), ((), ())), preferred_element_type=f32,
                               precision=lax.Precision.HIGHEST)
        gc_ref[...] = part * ds[8:9, :]

    return pl.pallas_call(body, name="mod_bwd",
                          out_shape=(jax.ShapeDtypeStruct(wm.shape, f32), jax.ShapeDtypeStruct((1, D), f32)),
                          compiler_params=_cp())(cs, dm, wm)


def _adam(parts, w, m, v, rows, name):
    p, r, c = parts.shape
    c1 = 1.0 / (1.0 - ADAM_B1 ** ADAM_STEP)
    c2 = 1.0 / (1.0 - ADAM_B2 ** ADAM_STEP)

    def body(g_ref, w_ref, m_ref, v_ref, go_ref, d_ref, mo_ref, vo_ref):
        g = g_ref[0]
        for i in range(1, p):
            g = g + g_ref[i]
        mn = ADAM_B1 * m_ref[...] + (1.0 - ADAM_B1) * g
        vn = ADAM_B2 * v_ref[...] + (1.0 - ADAM_B2) * (g * g)
        go_ref[...] = g
        mo_ref[...] = mn
        vo_ref[...] = vn
        d_ref[...] = -ADAM_LR * ((mn * c1) / (jnp.sqrt(vn * c2) + ADAM_EPS) + ADAM_WD * w_ref[...])

    blk = pl.BlockSpec((rows, c), lambda i: (i, 0))
    return pl.pallas_call(
        body, name=name, grid=(r // rows,), out_shape=tuple(jax.ShapeDtypeStruct((r, c), f32) for _ in range(4)),
        in_specs=[pl.BlockSpec((p, rows, c), lambda i: (0, i, 0)), blk, blk, blk], out_specs=(blk, blk, blk, blk),
        compiler_params=_cp(("arbitrary",)),
    )(parts, w, m, v)


def _in_proj(x, g, scale, shift, w_pad, tl):
    l = x.shape[0]

    def body(x_ref, g_ref, sc_ref, sh_ref, w_ref, p_ref, h_ref):
        xv = x_ref[...]
        r = lax.rsqrt(jnp.mean(xv * xv, axis=-1, keepdims=True) + EPS)
        h = (xv * r) * (g_ref[...] * (1.0 + sc_ref[...])) + sh_ref[...]
        hb = h.astype(bf16)
        h_ref[...] = hb
        p_ref[...] = jnp.dot(hb, w_ref[...], preferred_element_type=f32).astype(bf16)

    vec = pl.BlockSpec((1, D), lambda i: (0, 0))
    return pl.pallas_call(
        body, name="in_proj", grid=(l // tl,),
        out_shape=(jax.ShapeDtypeStruct((l, PW), bf16), jax.ShapeDtypeStruct((l, D), bf16)),
        in_specs=[pl.BlockSpec((tl, D), lambda i: (i, 0)), vec, vec, vec, pl.BlockSpec((D, PW), lambda i: (0, 0))],
        out_specs=(pl.BlockSpec((tl, PW), lambda i: (i, 0)), pl.BlockSpec((tl, D), lambda i: (i, 0))),
        compiler_params=_cp(("arbitrary",)),
    )(x, g, scale, shift, w_pad)


def _tri(rev):
    i = lax.broadcasted_iota(jnp.int32, (CH, CH), 0)
    j = lax.broadcasted_iota(jnp.int32, (CH, CH), 1)
    return jnp.where((j >= i) if rev else (j <= i), 1.0, 0.0).astype(f32)


def _head_masks():
    lane = lax.broadcasted_iota(jnp.int32, (1, KW), 1) // HK
    return [jnp.where(lane == h, 1.0, 0.0).astype(f32) for h in range(NH)]


def _block_diag():
    r = lax.broadcasted_iota(jnp.int32, (VW, KW), 0) // HV
    c = lax.broadcasted_iota(jnp.int32, (VW, KW), 1) // HK
    return jnp.where(r == c, 1.0, 0.0).astype(f32)


def _decay(lr, w2, gb, tri, rev):
    logits = _dot(lr, w2) + gb
    a = _log_sigmoid(logits) * (1.0 / TAU)
    c = _dot_hi(tri, a)
    cl = c[0:1, :] if rev else c[CH - 1:CH, :]
    return logits, c, cl


def _stack_heads(t, hm):
    return jnp.concatenate([t * hm[h] for h in range(NH)], axis=0)


def _chunk_fwd(q, k, v, c, cl, st, tri, hm, bd):
    tri4 = jnp.concatenate([tri] * NH, axis=0)
    qd = q * jnp.exp(c) * QSCALE
    kd = k * jnp.exp(-c)
    kdec = k * jnp.exp(cl - c)
    pst = _nt(_stack_heads(qd, hm), kd) * tri4
    intra = jnp.concatenate([_dot(pst[h * CH:(h + 1) * CH], v[:, h * HV:(h + 1) * HV]) for h in range(NH)], axis=1)
    o = _nt(qd, st) + intra
    st_new = st * jnp.exp(cl) + bd * _tn(v, kdec)
    return o, st_new


def _state_fwd(k, v, c, cl, st, bd):
    return st * jnp.exp(cl) + bd * _tn(v, k * jnp.exp(cl - c))


def _chunk_bwd(q, k, v, c, cl, st0, dst, do, tri, trit, hm, bd):
    ecl = jnp.exp(cl)
    edec = jnp.exp(cl - c)
    kdec = k * edec
    dv = _nt(kdec, dst)
    dkdec = _dot(v, dst)
    dcl = jnp.sum(dst * st0, axis=0, keepdims=True) * ecl + jnp.sum(dkdec * kdec, axis=0, keepdims=True)
    dc = -dkdec * kdec
    dk = dkdec * edec
    dst0 = dst * ecl
    dq = None
    if q is not None:
        tri4 = jnp.concatenate([tri] * NH, axis=0)
        ec, enc = jnp.exp(c), jnp.exp(-c)
        qd = q * ec * QSCALE
        kd = k * enc
        qs = _stack_heads(qd, hm)
        pst = _nt(qs, kd) * tri4
        dpst = jnp.concatenate([_nt(do[:, h * HV:(h + 1) * HV], v[:, h * HV:(h + 1) * HV]) for h in range(NH)], axis=0) * tri4
        dv = dv + jnp.concatenate([_tn(pst[h * CH:(h + 1) * CH], do[:, h * HV:(h + 1) * HV]) for h in range(NH)], axis=1)
        dqd = _dot(do, st0)
        for h in range(NH):
            dqd = dqd + hm[h] * _dot(dpst[h * CH:(h + 1) * CH], kd)
        dkd = _tn(dpst, qs)
        dst0 = dst0 + bd * _tn(do, qd)
        dc = dc + dqd * qd - dkd * kd
        dq = dqd * ec * QSCALE
        dk = dk + dkd * enc
    da = _dot_hi(trit, dc) + dcl
    return dq, dk, dv, da, dst0


def _gla_fwd(p, w2p, gb, s0, rev, tg, name):
    l = p.shape[0]
    nb, nc = l // tg, tg // CH

    def body(q_ref, k_ref, v_ref, lr_ref, w2_ref, gb_ref, s0_ref, o_ref, st_ref, st):
        @pl.when(pl.program_id(0) == 0)
        def _():
            st[...] = s0_ref[...]

        tri, hm, bd = _tri(rev), _head_masks(), _block_diag()

        def step(j, carry):
            jj = (nc - 1 - j) if rev else j
            rows = pl.ds(pl.multiple_of(jj * CH, CH), CH)
            k = k_ref[rows, :].astype(f32)
            _, c, cl = _decay(lr_ref[rows, :], w2_ref[...], gb_ref[...], tri, rev)
            s = st[...]
            st_ref[jj] = s
            o, s_new = _chunk_fwd(q_ref[rows, :].astype(f32), k, v_ref[rows, :].astype(f32), c, cl, s, tri, hm, bd)
            o_ref[rows, :] = o
            st[...] = s_new
            return carry

        lax.fori_loop(0, nc, step, 0)

    blk = (lambda i: nb - 1 - i) if rev else (lambda i: i)
    return pl.pallas_call(
        body, name=name, grid=(nb,),
        out_shape=(jax.ShapeDtypeStruct((l, VW), f32), jax.ShapeDtypeStruct((l // CH, VW, KW), f32)),
        in_specs=[pl.BlockSpec((tg, KW), lambda i: (blk(i), PQ // KW)), pl.BlockSpec((tg, KW), lambda i: (blk(i), PK // KW)),
                  pl.BlockSpec((tg, VW), lambda i: (blk(i), PV // VW)), pl.BlockSpec((tg, LRW), lambda i: (blk(i), PLR // LRW)),
                  pl.BlockSpec((LRW, KW), lambda i: (0, 0)), pl.BlockSpec((1, KW), lambda i: (0, 0)),
                  pl.BlockSpec((VW, KW), lambda i: (0, 0))],
        out_specs=(pl.BlockSpec((tg, VW), lambda i: (blk(i), 0)), pl.BlockSpec((nc, VW, KW), lambda i: (blk(i), 0, 0))),
        scratch_shapes=[pltpu.VMEM((VW, KW), f32)],
        compiler_params=_cp(("arbitrary",)),
    )(p, p, p, p, w2p, gb, s0)


def _gla_bwd(p, do, states, w2p, gb, prev, rev, tg, name):
    l = p.shape[0]
    nb, nc = l // tg, tg // CH
    out_dt = f32 if prev is None else bf16

    def body(*refs):
        q_ref, k_ref, v_ref, lr_ref, do_ref, st_ref, w2_ref, gb_ref = refs[:8]
        refs = refs[8:]
        if prev is not None:
            pq_ref, pl_ref = refs[:2]
            refs = refs[2:]
        dqkv_ref, dlr_ref, dw2_ref, dgb_ref, ds0_ref, dst = refs

        @pl.when(pl.program_id(0) == 0)
        def _():
            dst[...] = jnp.zeros_like(dst)
            dw2_ref[...] = jnp.zeros_like(dw2_ref)
            dgb_ref[...] = jnp.zeros_like(dgb_ref)

        tri, trit, hm, bd = _tri(rev), _tri(not rev), _head_masks(), _block_diag()

        def step(j, carry):
            jj = j if rev else (nc - 1 - j)
            rows = pl.ds(pl.multiple_of(jj * CH, CH), CH)
            k = k_ref[rows, :].astype(f32)
            lr = lr_ref[rows, :]
            logits, c, cl = _decay(lr, w2_ref[...], gb_ref[...], tri, rev)
            dq, dk, dv, da, dst0 = _chunk_bwd(q_ref[rows, :].astype(f32), k, v_ref[rows, :].astype(f32), c, cl,
                                              st_ref[jj], dst[...], do_ref[rows, :].astype(f32), tri, trit, hm, bd)
            dst[...] = dst0
            dlog = da * _sigmoid(-logits) * (1.0 / TAU)
            dlr = _nt(dlog, w2_ref[...])
            dw2_ref[...] += _tn(lr, dlog)
            dgb_ref[...] += jnp.sum(dlog, axis=0, keepdims=True)
            dqkv = jnp.concatenate([dq, dk, dv], axis=1)
            if prev is not None:
                dqkv = dqkv + pq_ref[rows, :]
                dlr = dlr + pl_ref[rows, :]
            dqkv_ref[rows, :] = dqkv.astype(out_dt)
            dlr_ref[rows, :] = dlr.astype(out_dt)
            return carry

        lax.fori_loop(0, nc, step, 0)
        ds0_ref[...] = dst[...]

    blk = (lambda i: i) if rev else (lambda i: nb - 1 - i)
    in_specs = [pl.BlockSpec((tg, KW), lambda i: (blk(i), PQ // KW)), pl.BlockSpec((tg, KW), lambda i: (blk(i), PK // KW)),
                pl.BlockSpec((tg, VW), lambda i: (blk(i), PV // VW)), pl.BlockSpec((tg, LRW), lambda i: (blk(i), PLR // LRW)),
                pl.BlockSpec((tg, VW), lambda i: (blk(i), 0)), pl.BlockSpec((nc, VW, KW), lambda i: (blk(i), 0, 0)),
                pl.BlockSpec((LRW, KW), lambda i: (0, 0)), pl.BlockSpec((1, KW), lambda i: (0, 0))]
    args = [p, p, p, p, do, states, w2p, gb]
    if prev is not None:
        in_specs += [pl.BlockSpec((tg, 2 * KW + VW), lambda i: (blk(i), 0)), pl.BlockSpec((tg, LRW), lambda i: (blk(i), 0))]
        args += list(prev)
    return pl.pallas_call(
        body, name=name, grid=(nb,),
        out_shape=(jax.ShapeDtypeStruct((l, 2 * KW + VW), out_dt), jax.ShapeDtypeStruct((l, LRW), out_dt),
                   jax.ShapeDtypeStruct((LRW, KW), f32), jax.ShapeDtypeStruct((1, KW), f32), jax.ShapeDtypeStruct((VW, KW), f32)),
        in_specs=in_specs,
        out_specs=(pl.BlockSpec((tg, 2 * KW + VW), lambda i: (blk(i), 0)), pl.BlockSpec((tg, LRW), lambda i: (blk(i), 0)),
                   pl.BlockSpec((LRW, KW), lambda i: (0, 0)), pl.BlockSpec((1, KW), lambda i: (0, 0)),
                   pl.BlockSpec((VW, KW), lambda i: (0, 0))),
        scratch_shapes=[pltpu.VMEM((VW, KW), f32)],
        compiler_params=_cp(("arbitrary",)),
    )(*args)


def _ctx_hidden(ctx_ref, g_ref, sc_ref, sh_ref):
    xv = ctx_ref[...]
    r = lax.rsqrt(jnp.mean(xv * xv, axis=-1, keepdims=True) + EPS)
    xn = xv * r
    return xn, xn * (g_ref[...] * (1.0 + sc_ref[...])) + sh_ref[...]


_CTX_W_SPECS = [pl.BlockSpec((D, KW), lambda i: (0, PK // KW)), pl.BlockSpec((D, VW), lambda i: (0, PV // VW)),
                pl.BlockSpec((D, LRW), lambda i: (0, PLR // LRW))]


def _ctx_fwd(ctx, g, scale, shift, w_pad, w2f, w2b, gbf, gbb):
    ncc = CTX // CH

    def body(ctx_ref, g_ref, sc_ref, sh_ref, wk_ref, wv_ref, wl_ref, w2f_ref, w2b_ref, gbf_ref, gbb_ref, sf_ref, sb_ref):
        _, hc = _ctx_hidden(ctx_ref, g_ref, sc_ref, sh_ref)
        k, v, lr = _dot(hc, wk_ref[...]), _dot(hc, wv_ref[...]), _dot(hc, wl_ref[...])
        bd = _block_diag()
        for rev, w2_ref, gb_ref, out in ((False, w2f_ref, gbf_ref, sf_ref), (True, w2b_ref, gbb_ref, sb_ref)):
            tri = _tri(rev)
            st = jnp.zeros((VW, KW), f32)
            for j in (range(ncc - 1, -1, -1) if rev else range(ncc)):
                rows = slice(j * CH, (j + 1) * CH)
                _, c, cl = _decay(lr[rows], w2_ref[...], gb_ref[...], tri, rev)
                st = _state_fwd(k[rows], v[rows], c, cl, st, bd)
            out[...] = st

    vec = pl.BlockSpec((1, D), lambda i: (0, 0))
    w2s = pl.BlockSpec((LRW, KW), lambda i: (0, 0))
    gbs = pl.BlockSpec((1, KW), lambda i: (0, 0))
    sts = pl.BlockSpec((VW, KW), lambda i: (0, 0))
    return pl.pallas_call(
        body, name="ctx_fwd", grid=(1,), out_shape=(jax.ShapeDtypeStruct((VW, KW), f32),) * 2,
        in_specs=[pl.BlockSpec((CTX, D), lambda i: (0, 0)), vec, vec, vec] + _CTX_W_SPECS + [w2s, w2s, gbs, gbs],
        out_specs=(sts, sts), compiler_params=_cp(("arbitrary",)),
    )(ctx, g, scale, shift, w_pad, w_pad, w_pad, w2f, w2b, gbf, gbb)


def _ctx_bwd(ctx, g, scale, shift, w_pad, w2f, w2b, gbf, gbb, dsf, dsb):
    ncc = CTX // CH

    def body(ctx_ref, g_ref, sc_ref, sh_ref, wk_ref, wv_ref, wl_ref, w2f_ref, w2b_ref, gbf_ref, gbb_ref, dsf_ref, dsb_ref,
             dwk_ref, dwv_ref, dwl_ref, dmod_ref, dg_ref, dw2_ref, dgb_ref):
        xn, hc = _ctx_hidden(ctx_ref, g_ref, sc_ref, sh_ref)
        k, v, lr = _dot(hc, wk_ref[...]), _dot(hc, wv_ref[...]), _dot(hc, wl_ref[...])
        bd = _block_diag()
        dk_rows, dv_rows, dl_rows = [None] * ncc, [None] * ncc, [None] * ncc
        for d, (rev, w2_ref, gb_ref, ds_ref) in enumerate(((False, w2f_ref, gbf_ref, dsf_ref), (True, w2b_ref, gbb_ref, dsb_ref))):
            tri = _tri(rev)
            order = list(range(ncc - 1, -1, -1) if rev else range(ncc))
            st, saved = jnp.zeros((VW, KW), f32), {}
            for j in order:
                rows = slice(j * CH, (j + 1) * CH)
                logits, c, cl = _decay(lr[rows], w2_ref[...], gb_ref[...], tri, rev)
                saved[j] = (logits, c, cl, st)
                st = _state_fwd(k[rows], v[rows], c, cl, st, bd)
            dst = ds_ref[...]
            dw2 = jnp.zeros((LRW, KW), f32)
            dgb = jnp.zeros((1, KW), f32)
            for j in reversed(order):
                rows = slice(j * CH, (j + 1) * CH)
                logits, c, cl, st0 = saved[j]
                _, dk, dv, da, dst = _chunk_bwd(None, k[rows], v[rows], c, cl, st0, dst, None, tri, _tri(not rev), None, bd)
                dlog = da * _sigmoid(-logits) * (1.0 / TAU)
                dl = _nt(dlog, w2_ref[...])
                dw2 = dw2 + _tn(lr[rows], dlog)
                dgb = dgb + jnp.sum(dlog, axis=0, keepdims=True)
                dk_rows[j] = dk if dk_rows[j] is None else dk_rows[j] + dk
                dv_rows[j] = dv if dv_rows[j] is None else dv_rows[j] + dv
                dl_rows[j] = dl if dl_rows[j] is None else dl_rows[j] + dl
            dw2_ref[d] = dw2
            dgb_ref[d] = dgb
        dk, dv, dl = (jnp.concatenate(t, axis=0) for t in (dk_rows, dv_rows, dl_rows))
        dwk_ref[...] = _tn(hc, dk)
        dwv_ref[...] = _tn(hc, dv)
        dwl_ref[...] = _tn(hc, dl)
        dh = _nt(dk, wk_ref[...]) + _nt(dv, wv_ref[...]) + _nt(dl, wl_ref[...])
        gx = dh * xn
        dmod_ref[:, 0:D] = jnp.sum(dh, axis=0, keepdims=True)
        dmod_ref[:, D:2 * D] = jnp.sum(gx, axis=0, keepdims=True) * g_ref[...]
        dg_ref[...] = jnp.sum(gx, axis=0, keepdims=True) * (1.0 + sc_ref[...])

    vec = pl.BlockSpec((1, D), lambda i: (0, 0))
    w2s = pl.BlockSpec((LRW, KW), lambda i: (0, 0))
    gbs = pl.BlockSpec((1, KW), lambda i: (0, 0))
    sts = pl.BlockSpec((VW, KW), lambda i: (0, 0))
    full = lambda *s: pl.BlockSpec(s, lambda i: (0,) * len(s))
    return pl.pallas_call(
        body, name="ctx_bwd", grid=(1,),
        out_shape=(jax.ShapeDtypeStruct((D, KW), f32), jax.ShapeDtypeStruct((D, VW), f32), jax.ShapeDtypeStruct((D, LRW), f32),
                   jax.ShapeDtypeStruct((1, 2 * D), f32), jax.ShapeDtypeStruct((1, D), f32),
                   jax.ShapeDtypeStruct((2, LRW, KW), f32), jax.ShapeDtypeStruct((2, 1, KW), f32)),
        in_specs=[pl.BlockSpec((CTX, D), lambda i: (0, 0)), vec, vec, vec] + _CTX_W_SPECS + [w2s, w2s, gbs, gbs, sts, sts],
        out_specs=(full(D, KW), full(D, VW), full(D, LRW), full(1, 2 * D), full(1, D), full(2, LRW, KW), full(2, 1, KW)),
        compiler_params=_cp(("arbitrary",)),
    )(ctx, g, scale, shift, w_pad, w_pad, w_pad, w2f, w2b, gbf, gbb, dsf, dsb)


def _layernorm(va, g, b):
    mu = jnp.mean(va, axis=-1, keepdims=True)
    xc = va - mu
    rstd = lax.rsqrt(jnp.mean(xc * xc, axis=-1, keepdims=True) + EPS)
    vhat = xc * rstd
    return vhat, rstd, vhat * g + b


def _mix_fwd(p, ln_g, ln_b, ws, bs_t):
    l = p.shape[0]
    nch = l // ACH
    half = AW // 2

    def body(p_ref, g_ref, b_ref, ws_ref, bs_ref, sv_ref, va_buf, col_buf, sem):
        cp = pltpu.make_async_copy(p_ref.at[:, pl.ds(PVA, AW)], va_buf, sem)
        cp.start()
        cp.wait()

        def rows_step(n, carry):
            rows = pl.ds(pl.multiple_of(n * ACH, ACH), ACH)
            _, _, vn = _layernorm(va_buf[rows, :].astype(f32), g_ref[...], b_ref[...])
            for gi in range(2):
                sl = slice(gi * ACH, (gi + 1) * ACH)
                sv_ref[rows, sl] = (_dot(ws_ref[gi], vn[:, sl]) + bs_ref[:, gi:gi + 1]).astype(bf16)
            col_buf[0, rows, :] = vn[:, half:half + ACH]
            col_buf[1, rows, :] = vn[:, half + ACH:]
            return carry

        lax.fori_loop(0, nch, rows_step, 0)

        def cols_step(cidx, carry):
            rows = pl.ds(cidx, ACH, stride=GW)
            for gi in range(2, 4):
                col_buf[gi - 2, rows, :] = _dot(ws_ref[gi], col_buf[gi - 2, rows, :]) + bs_ref[:, gi:gi + 1]
            return carry

        lax.fori_loop(0, GW, cols_step, 0)

        def out_step(n, carry):
            rows = pl.ds(pl.multiple_of(n * ACH, ACH), ACH)
            sv_ref[rows, half:half + ACH] = col_buf[0, rows, :].astype(bf16)
            sv_ref[rows, half + ACH:] = col_buf[1, rows, :].astype(bf16)
            return carry

        lax.fori_loop(0, nch, out_step, 0)

    vm = pl.BlockSpec(memory_space=pltpu.VMEM)
    return pl.pallas_call(
        body, name="mix_fwd", out_shape=jax.ShapeDtypeStruct((l, AW), bf16),
        in_specs=[pl.BlockSpec(memory_space=pl.ANY), vm, vm, vm, vm], out_specs=vm,
        scratch_shapes=[pltpu.VMEM((l, AW), bf16), pltpu.VMEM((2, l, ACH), f32), pltpu.SemaphoreType.DMA(())],
        compiler_params=_cp(),
    )(p, ln_g, ln_b, ws, bs_t)


def _mix_bwd(p, dsv, ln_g, ln_b, ws_t):
    l = p.shape[0]
    nch = l // ACH
    half = AW // 2

    def body(p_ref, dsv_hbm, g_ref, b_ref, wst_ref, dva_ref, dws_ref, dbs_ref, dg_ref, db_ref, va_buf, dsv_buf, vn_col, ds_col, sems):
        cp1 = pltpu.make_async_copy(p_ref.at[:, pl.ds(PVA, AW)], va_buf, sems.at[0])
        cp2 = pltpu.make_async_copy(dsv_hbm, dsv_buf, sems.at[1])
        cp1.start()
        cp2.start()
        dws_ref[...] = jnp.zeros_like(dws_ref)
        dbs_ref[...] = jnp.zeros_like(dbs_ref)
        dg_ref[...] = jnp.zeros_like(dg_ref)
        db_ref[...] = jnp.zeros_like(db_ref)
        cp1.wait()
        cp2.wait()

        def rows_step(n, carry):
            rows = pl.ds(pl.multiple_of(n * ACH, ACH), ACH)
            _, _, vn = _layernorm(va_buf[rows, :].astype(f32), g_ref[...], b_ref[...])
            ds = dsv_buf[rows, :].astype(f32)
            for gi in range(2):
                sl = slice(gi * ACH, (gi + 1) * ACH)
                dws_ref[gi] += _nt(ds[:, sl], vn[:, sl])
                dbs_ref[gi] += ds[:, sl]
            for gi in range(2):
                sl = slice(half + gi * ACH, half + (gi + 1) * ACH)
                vn_col[gi, rows, :] = vn[:, sl]
                ds_col[gi, rows, :] = ds[:, sl]
            return carry

        lax.fori_loop(0, nch, rows_step, 0)

        def cols_step(cidx, carry):
            rows = pl.ds(cidx, ACH, stride=GW)
            for gi in range(2, 4):
                ds = ds_col[gi - 2, rows, :]
                dws_ref[gi] += _nt(ds, vn_col[gi - 2, rows, :])
                dbs_ref[gi] += ds
                ds_col[gi - 2, rows, :] = _dot(wst_ref[gi], ds)
            return carry

        lax.fori_loop(0, GW, cols_step, 0)

        def out_step(n, carry):
            rows = pl.ds(pl.multiple_of(n * ACH, ACH), ACH)
            vhat, rstd, _ = _layernorm(va_buf[rows, :].astype(f32), g_ref[...], b_ref[...])
            ds = dsv_buf[rows, :].astype(f32)
            dvn = jnp.concatenate([_dot(wst_ref[0], ds[:, 0:ACH]), _dot(wst_ref[1], ds[:, ACH:half]), ds_col[0, rows, :], ds_col[1, rows, :]], axis=1)
            dg_ref[...] += jnp.sum(dvn * vhat, axis=0, keepdims=True)
            db_ref[...] += jnp.sum(dvn, axis=0, keepdims=True)
            dvh = dvn * g_ref[...]
            dva = rstd * (dvh - jnp.mean(dvh, axis=-1, keepdims=True) - vhat * jnp.mean(dvh * vhat, axis=-1, keepdims=True))
            dva_ref[rows, :] = dva.astype(bf16)
            return carry

        lax.fori_loop(0, nch, out_step, 0)

    vm = pl.BlockSpec(memory_space=pltpu.VMEM)
    hbm = pl.BlockSpec(memory_space=pl.ANY)
    return pl.pallas_call(
        body, name="mix_bwd",
        out_shape=(jax.ShapeDtypeStruct((l, AW), bf16), jax.ShapeDtypeStruct((4, ACH, ACH), f32), jax.ShapeDtypeStruct((4, ACH, ACH), f32),
                   jax.ShapeDtypeStruct((1, AW), f32), jax.ShapeDtypeStruct((1, AW), f32)),
        in_specs=[hbm, hbm, vm, vm, vm], out_specs=(vm, vm, vm, vm, vm),
        scratch_shapes=[pltpu.VMEM((l, AW), bf16), pltpu.VMEM((l, AW), bf16), pltpu.VMEM((2, l, ACH), f32), pltpu.VMEM((2, l, ACH), f32),
                        pltpu.SemaphoreType.DMA((2,))],
        compiler_params=_cp(),
    )(p, dsv, ln_g, ln_b, ws_t)


def _mid(x, tgt, p, o_f, o_b, sv, gate, gf, gb_norm, w_pa, w_pb, w_out, tl):
    l = x.shape[0]

    def body(x_ref, t_ref, zb_ref, ua_ref, za_ref, g1_ref, g2_ref, of_ref, ob_ref, sv_ref, gate_ref, gf_ref, gbn_ref,
             wpa_ref, wpb_ref, wout_ref,
             dx1_ref, dzbua_ref, dzag_ref, dsv_ref, do_ref, dwout_ref, dwpa_ref, dwpb_ref, dgf_ref, dgate_ref, dgbn_ref, loss_ref):
        @pl.when(pl.program_id(0) == 0)
        def _():
            for r in (dwout_ref, dwpa_ref, dwpb_ref, dgf_ref, dgate_ref, dgbn_ref, loss_ref):
                r[...] = jnp.zeros_like(r)

        o = of_ref[...] + ob_ref[...]
        rr = jnp.concatenate(
            [jnp.broadcast_to(lax.rsqrt(jnp.mean(o[:, h * HV:(h + 1) * HV] ** 2, axis=-1, keepdims=True) + EPS), (tl, HV))
             for h in range(NH)], axis=1)
        ohat = o * rr
        on = ohat * gbn_ref[...]
        szb, dszb = _silu_and_grad(zb_ref[...].astype(f32))
        tb = on * szb
        u = ua_ref[...].astype(f32)
        svv = sv_ref[...].astype(f32)
        sza, dsza = _silu_and_grad(za_ref[...].astype(f32))
        ta = u * svv * sza
        ya = _dot(ta, wpa_ref[...])
        yb = _dot(tb, wpb_ref[...])
        g1 = _sigmoid(g1_ref[...].astype(f32))
        g2 = _sigmoid(g2_ref[...].astype(f32))
        m = g1 * ya + g2 * yb
        y2 = _dot(m, wout_ref[...])
        x1 = x_ref[...] + gate_ref[...] * y2
        r1 = lax.rsqrt(jnp.mean(x1 * x1, axis=-1, keepdims=True) + EPS)
        x1n = x1 * r1
        err = x1n * gf_ref[...] - t_ref[...]
        loss_ref[...] += jnp.sum(jnp.sum(err * err, axis=-1, keepdims=True), axis=0, keepdims=True) * (0.5 / D)
        dout = err * (1.0 / D)
        dgf_ref[...] += jnp.sum(dout * x1n, axis=0, keepdims=True)
        dx1n = dout * gf_ref[...]
        dx1 = r1 * (dx1n - x1n * jnp.mean(dx1n * x1n, axis=-1, keepdims=True))
        dx1_ref[...] = dx1
        dgate_ref[...] += jnp.sum(dx1 * y2, axis=0, keepdims=True)
        dy2 = dx1 * gate_ref[...]
        dwout_ref[...] += _tn(m, dy2)
        dm = _nt(dy2, wout_ref[...])
        dya = dm * g1
        dyb = dm * g2
        dzag_ref[:, AW:AW + D] = (dm * ya * g1 * (1.0 - g1)).astype(bf16)
        dzag_ref[:, AW + D:] = (dm * yb * g2 * (1.0 - g2)).astype(bf16)
        dwpa_ref[...] += _tn(ta, dya)
        dta = _nt(dya, wpa_ref[...])
        dzbua_ref[:, AW:] = (dta * svv * sza).astype(bf16)
        dsv_ref[...] = (dta * u * sza).astype(bf16)
        dzag_ref[:, 0:AW] = (dta * u * svv * dsza).astype(bf16)
        dwpb_ref[...] += _tn(tb, dyb)
        dtb = _nt(dyb, wpb_ref[...])
        don = dtb * szb
        dzbua_ref[:, 0:AW] = (dtb * on * dszb).astype(bf16)
        dgbn_ref[...] += jnp.sum(don * ohat, axis=0, keepdims=True)
        doh = don * gbn_ref[...]
        prod = doh * ohat
        mh = jnp.concatenate(
            [jnp.broadcast_to(jnp.mean(prod[:, h * HV:(h + 1) * HV], axis=-1, keepdims=True), (tl, HV)) for h in range(NH)], axis=1)
        do_ref[...] = (rr * (doh - ohat * mh)).astype(bf16)

    row = lambda w, j: pl.BlockSpec((tl, w), lambda i, j=j: (i, j))
    full = lambda *s: pl.BlockSpec(s, lambda i: (0,) * len(s))
    return pl.pallas_call(
        body, name="mid", grid=(l // tl,),
        out_shape=(jax.ShapeDtypeStruct((l, D), f32), jax.ShapeDtypeStruct((l, 2 * AW), bf16), jax.ShapeDtypeStruct((l, AW + 2 * D), bf16),
                   jax.ShapeDtypeStruct((l, AW), bf16), jax.ShapeDtypeStruct((l, VW), bf16),
                   jax.ShapeDtypeStruct((D, D), f32), jax.ShapeDtypeStruct((AW, D), f32), jax.ShapeDtypeStruct((VW, D), f32),
                   jax.ShapeDtypeStruct((1, D), f32), jax.ShapeDtypeStruct((1, D), f32), jax.ShapeDtypeStruct((1, VW), f32),
                   jax.ShapeDtypeStruct((1, 1), f32)),
        in_specs=[row(D, 0), row(D, 0), row(AW, PZB // AW), row(AW, PUA // AW), row(AW, PZA // AW), row(D, PG1 // D), row(D, PG2 // D),
                  row(VW, 0), row(VW, 0), row(AW, 0), full(1, D), full(1, D), full(1, VW), full(AW, D), full(VW, D), full(D, D)],
        out_specs=(row(D, 0), row(2 * AW, 0), row(AW + 2 * D, 0), row(AW, 0), row(VW, 0),
                   full(D, D), full(AW, D), full(VW, D), full(1, D), full(1, D), full(1, VW), full(1, 1)),
        compiler_params=_cp(("arbitrary",)),
    )(x, tgt, p, p, p, p, p, o_f, o_b, sv, gate, gf, gb_norm, w_pa, w_pb, w_out)


def _in_bwd(x, dx1, dqkv, dzbua, dva, dzag, dlr, w_pad, g, scale, tl):
    l = x.shape[0]

    def body(x_ref, dx1_ref, a_ref, b_ref, c_ref, e_ref, lr_ref, wa_ref, wb_ref, wc_ref, we_ref, wl_ref, g_ref, sc_ref,
             gx_ref, dsh_ref, dsc_ref, dg_ref):
        @pl.when(pl.program_id(0) == 0)
        def _():
            for r in (dsh_ref, dsc_ref, dg_ref):
                r[...] = jnp.zeros_like(r)

        dh = (_nt(a_ref[...], wa_ref[...]) + _nt(b_ref[...], wb_ref[...]) + _nt(c_ref[...], wc_ref[...]) + _nt(e_ref[...], we_ref[...])
              + _nt(lr_ref[...], wl_ref[...]))
        xv = x_ref[...]
        r = lax.rsqrt(jnp.mean(xv * xv, axis=-1, keepdims=True) + EPS)
        xn = xv * r
        gxn = jnp.sum(dh * xn, axis=0, keepdims=True)
        dsh_ref[...] += jnp.sum(dh, axis=0, keepdims=True)
        dsc_ref[...] += gxn * g_ref[...]
        dg_ref[...] += gxn * (1.0 + sc_ref[...])
        dxn = dh * (g_ref[...] * (1.0 + sc_ref[...]))
        gx_ref[...] = dx1_ref[...] + r * (dxn - xn * jnp.mean(dxn * xn, axis=-1, keepdims=True))

    row = lambda w: pl.BlockSpec((tl, w), lambda i: (i, 0))
    wcol = lambda w, j: pl.BlockSpec((D, w), lambda i, j=j: (0, j))
    vec = pl.BlockSpec((1, D), lambda i: (0, 0))
    return pl.pallas_call(
        body, name="in_bwd", grid=(l // tl,),
        out_shape=(jax.ShapeDtypeStruct((l, D), f32),) + (jax.ShapeDtypeStruct((1, D), f32),) * 3,
        in_specs=[row(D), row(D), row(2 * KW + VW), row(2 * AW), row(AW), row(AW + 2 * D), row(LRW),
                  wcol(2 * KW + VW, 0), wcol(2 * AW, PZB // (2 * AW)), wcol(AW, PVA // AW), wcol(AW + 2 * D, PZA // (AW + 2 * D)),
                  wcol(LRW, PLR // LRW), vec, vec],
        out_specs=(row(D), vec, vec, vec),
        compiler_params=_cp(("arbitrary",)),
    )(x, dx1, dqkv, dzbua, dva, dzag, dlr, w_pad, w_pad, w_pad, w_pad, w_pad, g, scale)


def _tn_matmul(a, b, tl, name):
    l, m = a.shape
    n = b.shape[1]

    def body(a_ref, b_ref, o_ref):
        @pl.when(pl.program_id(0) == 0)
        def _():
            o_ref[...] = jnp.zeros_like(o_ref)

        o_ref[...] += _tn(a_ref[...], b_ref[...])

    return pl.pallas_call(
        body, name=name, grid=(l // tl,), out_shape=jax.ShapeDtypeStruct((m, n), f32),
        in_specs=[pl.BlockSpec((tl, m), lambda i: (i, 0)), pl.BlockSpec((tl, n), lambda i: (i, 0))],
        out_specs=pl.BlockSpec((m, n), lambda i: (0, 0)), compiler_params=_cp(("arbitrary",)),
    )(a, b)


def _pad_gate(w2, gb):
    z = jnp.zeros((RANK, KW), f32)
    tail = jnp.zeros((LRW - 2 * RANK, KW), f32)
    w2f = jnp.concatenate([w2[0], z, tail], axis=0)
    w2b = jnp.concatenate([z, w2[1], tail], axis=0)
    return w2f, w2b, gb[0:1], gb[1:2]


def _local_step(x, ctx, tgt, mod, modc, norm_g, w_pad, ln_g, ln_b, ws, bs, w2, gb, gb_norm, w_pa, w_pb, w_out, gf):
    shift, scale, gate = mod[:, 0:D], mod[:, D:2 * D], mod[:, 2 * D:]
    shift_c, scale_c = modc[:, 0:D], modc[:, D:]
    w2f, w2b, gbf, gbb = _pad_gate(w2, gb)

    p, h = _in_proj(x, norm_g, scale, shift, w_pad, 256)
    sc_f, sc_b = _ctx_fwd(ctx, norm_g, scale_c, shift_c, w_pad, w2f, w2b, gbf, gbb)
    o_f, st_f = _gla_fwd(p, w2f, gbf, sc_f, False, 512, "gla_fwd_f")
    o_b, st_b = _gla_fwd(p, w2b, gbb, sc_b, True, 512, "gla_fwd_b")
    sv = _mix_fwd(p, ln_g, ln_b, ws.astype(bf16), bs.T)
    (dx1, dzbua, dzag, dsv, do, dw_out, dw_pa, dw_pb, dgf, dgate, dgbn, loss) = _mid(
        x, tgt, p, o_f, o_b, sv, gate, gf, gb_norm, w_pa, w_pb, w_out, 256)
    dva, dws, dbs_acc, dln_g, dln_b = _mix_bwd(p, dsv, ln_g, ln_b, jnp.swapaxes(ws, 1, 2).astype(bf16))
    dqkv_f, dlr_f, dw2f, dgbf, dsc_f = _gla_bwd(p, do, st_f, w2f, gbf, None, False, 512, "gla_bwd_f")
    dqkv, dlr, dw2b, dgbb, dsc_b = _gla_bwd(p, do, st_b, w2b, gbb, (dqkv_f, dlr_f), True, 512, "gla_bwd_b")
    dwk_c, dwv_c, dwl_c, dmodc, dg_c, dw2c, dgbc = _ctx_bwd(ctx, norm_g, scale_c, shift_c, w_pad, w2f, w2b, gbf, gbb, dsc_f, dsc_b)
    gx, dshift, dscale, dg = _in_bwd(x, dx1, dqkv, dzbua, dva, dzag, dlr, w_pad, norm_g, scale, 256)
    dw_qkv = _tn_matmul(h, dqkv, 512, "dw_qkv")
    dw_zbua = _tn_matmul(h, dzbua, 512, "dw_zbua")
    dw_va = _tn_matmul(h, dva, 512, "dw_va")
    dw_zag = _tn_matmul(h, dzag, 512, "dw_zag")
    dw_lr = _tn_matmul(h, dlr, 512, "dw_lr")

    dw_blocks = _pack_dw(dw_qkv, dw_lr, dw_zbua, dw_va, dw_zag, dwk_c, dwv_c, dwl_c)
    dw2 = jnp.stack([dw2f[0:RANK] + dw2c[0, 0:RANK], dw2b[RANK:2 * RANK] + dw2c[1, RANK:2 * RANK]])
    dgb = jnp.concatenate([dgbf + dgbc[0], dgbb + dgbc[1]], axis=0)
    return dict(loss=loss, gx=gx, dmod=jnp.concatenate([dshift, dscale, dgate], axis=1), dmodc=dmodc, dnorm_g=dg + dg_c,
                dw_blocks=dw_blocks, dln_g=dln_g, dln_b=dln_b, dws=dws, dbs=jnp.sum(dbs_acc, axis=-1), dw2=dw2, dgb=dgb, dgbn=dgbn,
                dw_pa=dw_pa, dw_pb=dw_pb, dw_out=dw_out, dgf=dgf)


def _rows128(*vs):
    out = []
    for t in vs:
        t = t.reshape(-1)
        pad = (-t.shape[0]) % 128
        out.append(jnp.pad(t, (0, pad)) if pad else t)
    return jnp.concatenate(out).reshape(-1, 128)


def kernel(x, c, ctx, c_ctx, w_mod, b_mod, norm_g, w_in, a_ln_g, a_ln_b, a_ws, a_bs, b_gate_w2, b_gate_b, b_norm_g, w_proj_a, w_proj_b, w_out, final_norm_g, loss_target, m_c_ctx, m_w_mod, m_b_mod, m_norm_g, m_w_in, m_a_ln_g, m_a_ln_b, m_a_ws, m_a_bs, m_b_gate_w2, m_b_gate_b, m_b_norm_g, m_w_proj_a, m_w_proj_b, m_w_out, m_final_norm_g, v_c_ctx, v_w_mod, v_b_mod, v_norm_g, v_w_in, v_a_ln_g, v_a_ln_b, v_a_ws, v_a_bs, v_b_gate_w2, v_b_gate_b, v_b_norm_g, v_w_proj_a, v_w_proj_b, v_w_out, v_final_norm_g):
    me = _me()
    ncol = w_mod.shape[2]

    gate_mine = _rows128(jnp.concatenate([b_gate_w2.reshape(-1), b_gate_b.reshape(-1)]))
    early, wg, w_pa, w_pb, wo, gates = _gather_weights(
        _rows128(c), w_in[0].astype(bf16), w_proj_a[0].astype(bf16), w_proj_b[0].astype(bf16), w_out[0].astype(bf16), gate_mine)
    cs = jnp.concatenate([early.reshape(NDEV, D), c_ctx.reshape(1, D), jnp.zeros((7, D), f32)], axis=0)
    w_pad = _repack_w(wg)
    w_o = wo.reshape(D, D)
    gflat = gates.reshape(NDEV, 9 * 128)
    w2 = jnp.transpose(gflat[:, 0:2 * RANK * 32].reshape(NDEV, 2, RANK, 32), (1, 2, 0, 3)).reshape(2, RANK, KW)
    gb = jnp.transpose(gflat[:, 2 * RANK * 32:2 * RANK * 32 + 64].reshape(NDEV, 2, 32), (1, 0, 2)).reshape(2, KW)

    bm_mine = lax.dynamic_slice(b_mod, (0, me * ncol), (1, ncol))
    mods = _all_gather(_mod_fwd(cs, w_mod[0], bm_mine), "gather_mod")
    mods = jnp.transpose(mods, (1, 0, 2)).reshape(16, 3 * D)
    mod = lax.dynamic_slice(mods, (me, 0), (1, 3 * D))
    modc = mods[8:9, 0:2 * D]

    r = _local_step(x[0], ctx[0], loss_target[0], mod, modc, norm_g, w_pad, a_ln_g, a_ln_b, a_ws[0], a_bs[0], w2, gb,
                    b_norm_g, w_pa, w_pb, w_o, final_norm_g.reshape(1, D))

    dbm = r["dmod"] + jnp.concatenate([r["dmodc"], jnp.zeros((1, D), f32)], axis=1)
    small = _rows128(r["dnorm_g"], r["dln_g"], r["dln_b"], r["dws"], r["dbs"], r["dgbn"], r["dgf"], dbm,
                     r["dmod"], r["dmodc"], jnp.broadcast_to(r["loss"], (1, 128)))
    n_rep = (D + AW + AW + 4 * ACH * ACH + AW + VW + D + 3 * D) // 128
    smalls = _all_gather(small, "gather_small")
    w_rep = _rows128(norm_g, a_ln_g, a_ln_b, a_ws, a_bs, b_norm_g, final_norm_g, b_mod)
    m_rep = _rows128(m_norm_g, m_a_ln_g, m_a_ln_b, m_a_ws, m_a_bs, m_b_norm_g, m_final_norm_g, m_b_mod)
    v_rep = _rows128(v_norm_g, v_a_ln_g, v_a_ln_b, v_a_ws, v_a_bs, v_b_norm_g, v_final_norm_g, v_b_mod)
    rep = _adam(smalls[:, 0:n_rep], w_rep, m_rep, v_rep, n_rep, "adam_rep")
    tail = smalls[:, n_rep:].reshape(NDEV, -1)
    dmods = tail[:, 0:3 * D]
    dmodc_all = tail[:, 3 * D:5 * D]
    dmodc_tot = dmodc_all[0:1]
    for i in range(1, NDEV):
        dmodc_tot = dmodc_tot + dmodc_all[i:i + 1]
    losses = tail[:, 5 * D]
    loss = losses[0]
    for i in range(1, NDEV):
        loss = loss + losses[i]

    def split_rep(t):
        t = t.reshape(-1)
        sizes = (D, AW, AW, 4 * ACH * ACH, AW, VW, D, 3 * D)
        shapes = ((1, D), (1, AW), (1, AW), (1, 4, ACH, ACH), (1, 4, ACH), (1, VW), (D,), (1, 3 * D))
        out, o = [], 0
        for s, sh in zip(sizes, shapes):
            out.append(t[o:o + s].reshape(sh))
            o += s
        return out

    rep = [split_rep(t) for t in rep]

    dm_rows = jnp.concatenate([dmods, jnp.concatenate([dmodc_tot, jnp.zeros((1, D), f32)], axis=1), jnp.zeros((7, 3 * D), f32)], axis=0)
    dm_mine = lax.dynamic_slice(dm_rows, (0, me * ncol), (16, ncol))
    g_wmod, gc_part = _mod_bwd(cs, dm_mine, w_mod[0])
    wm = _adam(g_wmod[None], w_mod[0], m_w_mod[0], v_w_mod[0], 256, "adam_wmod")
    gcs = _all_gather(gc_part.reshape(8, 128), "gather_cctx")
    cc = _adam(gcs, c_ctx.reshape(8, 128), m_c_ctx.reshape(8, 128), v_c_ctx.reshape(8, 128), 8, "adam_cctx")

    dgt = jnp.concatenate([jnp.transpose(r["dw2"].reshape(2, RANK, NDEV, 32), (2, 0, 1, 3)).reshape(NDEV, 2 * RANK * 32),
                           jnp.transpose(r["dgb"].reshape(2, NDEV, 32), (1, 0, 2)).reshape(NDEV, 64),
                           jnp.zeros((NDEV, 64), f32)], axis=1).reshape(NDEV, 9, 128)
    p_in, p_pa, p_pb, p_out, p_gt = _exchange_grads(r["dw_blocks"], r["dw_pa"], r["dw_pb"], r["dw_out"].reshape(NDEV, 128, D), dgt)
    a_in = _adam(p_in, w_in[0], m_w_in[0], v_w_in[0], 128, "adam_w_in")
    a_pa = _adam(p_pa, w_proj_a[0], m_w_proj_a[0], v_w_proj_a[0], AW, "adam_w_pa")
    a_pb = _adam(p_pb, w_proj_b[0], m_w_proj_b[0], v_w_proj_b[0], VW, "adam_w_pb")
    a_out = _adam(p_out, w_out[0], m_w_out[0], v_w_out[0], 128, "adam_w_out")
    gate_m = _rows128(jnp.concatenate([m_b_gate_w2.reshape(-1), m_b_gate_b.reshape(-1)]))
    gate_v = _rows128(jnp.concatenate([v_b_gate_w2.reshape(-1), v_b_gate_b.reshape(-1)]))
    a_gt = [t.reshape(-1) for t in _adam(p_gt, gate_mine, gate_m, gate_v, 9, "adam_gate")]
    nw2 = 2 * RANK * 32
    sh = [(a_in[k][None], a_pa[k][None], a_pb[k][None], a_out[k][None], a_gt[k][0:nw2].reshape(1, 2, RANK, 32),
           a_gt[k][nw2:nw2 + 64].reshape(1, 2, 32)) for k in range(4)]

    outs = [loss, r["gx"][None]]
    for k in range(4):
        n_g, lg, lb, aws, abs_, bng, fng, bmod = rep[k]
        s_in, s_pa, s_pb, s_out, s_w2, s_gb = sh[k]
        outs += [cc[k].reshape(D), wm[k][None], bmod, n_g, s_in, lg, lb, aws, abs_, s_w2, s_gb, bng, s_pa, s_pb, s_out, fng]
    return tuple(outs)
```

```python
import functools

import jax
import jax.numpy as jnp
from jax import lax
from jax.experimental import pallas as pl
from jax.experimental.pallas import tpu as pltpu

f32, bf16 = jnp.float32, jnp.bfloat16

D = 1024
CTX = 256
EPS = 1e-6
AW = 512
ACH = 128
GW = 64
KW = 256
VW = 512
NH = 4
HK = 64
HV = 128
RANK = 16
TAU = 16.0
CH = 64
QSCALE = HK ** -0.5
INW = 5152
NDEV = 8

PQ, PK, PV, PZB, PUA, PVA, PZA, PG1, PG2, PLR, PW = 0, 256, 512, 1024, 1536, 2048, 2560, 3072, 4096, 5120, 5248
LRW = 128

ADAM_LR, ADAM_B1, ADAM_B2, ADAM_EPS, ADAM_WD, ADAM_STEP = 0.001, 0.9, 0.999, 1e-08, 0.01, 10

VMEM_LIMIT = 56 * 1024 * 1024
MESH = pl.DeviceIdType.MESH


def _cp(sem=None):
    return pltpu.CompilerParams(dimension_semantics=sem, vmem_limit_bytes=VMEM_LIMIT)


def _dot(a, b):
    return jnp.dot(a.astype(bf16), b.astype(bf16), preferred_element_type=f32)


def _nt(a, b):
    return lax.dot_general(a.astype(bf16), b.astype(bf16), (((1,), (1,)), ((), ())), preferred_element_type=f32)


def _tn(a, b):
    return lax.dot_general(a.astype(bf16), b.astype(bf16), (((0,), (0,)), ((), ())), preferred_element_type=f32)


def _dot_hi(a, b):
    return jnp.dot(a, b, preferred_element_type=f32, precision=lax.Precision.HIGHEST)


def _sigmoid(x):
    return 1.0 / (1.0 + jnp.exp(-x))


def _log_sigmoid(x):
    return jnp.minimum(x, 0.0) - jnp.log(1.0 + jnp.exp(-jnp.abs(x)))


def _silu_and_grad(z):
    s = _sigmoid(z)
    return z * s, s * (1.0 + z * (1.0 - s))


def _me():
    return 4 * lax.axis_index("x") + 2 * lax.axis_index("y") + lax.axis_index("c")


def _peer(k):
    x, y, c = lax.axis_index("x"), lax.axis_index("y"), lax.axis_index("c")
    px = 1 - x if k & 4 else x
    py = 1 - y if k & 2 else y
    pc = 1 - c if k & 1 else c
    return (px, py, pc), 4 * px + 2 * py + pc


def _all_gather(v, name):
    r, c = v.shape

    def body(v_ref, out_ref, send_sems, recv_sems, local_sem):
        me = _me()
        mine = pltpu.make_async_copy(v_ref, out_ref.at[me], local_sem)
        mine.start()
        sends, recvs = [], []
        for k in range(1, NDEV):
            dev, idx = _peer(k)
            sends.append(pltpu.make_async_remote_copy(
                src_ref=v_ref, dst_ref=out_ref.at[me], send_sem=send_sems.at[k - 1], recv_sem=recv_sems.at[k - 1],
                device_id=dev, device_id_type=MESH))
            recvs.append(pltpu.make_async_remote_copy(
                src_ref=v_ref, dst_ref=out_ref.at[idx], send_sem=send_sems.at[k - 1], recv_sem=recv_sems.at[k - 1],
                device_id=dev, device_id_type=MESH))
        for cp in sends:
            cp.start()
        for cp in recvs:
            cp.wait_recv()
        for cp in sends:
            cp.wait_send()
        mine.wait()

    return pl.pallas_call(
        body, name=name, out_shape=jax.ShapeDtypeStruct((NDEV, r, c), v.dtype),
        in_specs=[pl.BlockSpec(memory_space=pl.ANY)], out_specs=pl.BlockSpec(memory_space=pl.ANY),
        scratch_shapes=[pltpu.SemaphoreType.DMA((NDEV - 1,)), pltpu.SemaphoreType.DMA((NDEV - 1,)), pltpu.SemaphoreType.DMA(())],
    )(v)


def _fanout(srcs, dsts, send_sems, recv_sems, local_sems, owners=None):
    me = _me()
    n = len(srcs)
    owns = lambda a, j: True if owners is None or owners[a] is None else owners[a](j)

    def guarded(cond, fn):
        if cond is True:
            fn()
        else:
            pl.when(cond)(fn)

    def copies(with_recvs):
        local = [pltpu.make_async_copy(srcs[a](me), dsts[a](me), local_sems.at[a]) for a in range(n)]
        sends, recvs = [], []
        for k in range(1, NDEV):
            dev, idx = _peer(k)
            for a in range(n):
                s = (k - 1) * n + a
                sends.append((owns(a, idx), pltpu.make_async_remote_copy(
                    src_ref=srcs[a](idx), dst_ref=dsts[a](me), send_sem=send_sems.at[s], recv_sem=recv_sems.at[s],
                    device_id=dev, device_id_type=MESH)))
                if with_recvs:
                    recvs.append((owns(a, me), pltpu.make_async_remote_copy(
                        src_ref=srcs[a](idx), dst_ref=dsts[a](idx), send_sem=send_sems.at[s], recv_sem=recv_sems.at[s],
                        device_id=dev, device_id_type=MESH)))
        return local, sends, recvs

    def start():
        local, sends, _ = copies(False)
        for a, cp in enumerate(local):
            guarded(owns(a, me), cp.start)
        for cond, cp in sends:
            guarded(cond, cp.start)

    def finish():
        local, sends, recvs = copies(True)
        for cond, cp in recvs:
            guarded(cond, cp.wait_recv)
        for cond, cp in sends:
            guarded(cond, cp.wait_send)
        for a, cp in enumerate(local):
            guarded(owns(a, me), cp.wait)

    return start, finish


class _Comm:
    def __init__(self, ins, outs, plan):
        self.ins, self.outs, self.plan = list(ins), list(outs), plan
        self.n = len(self.outs)

    def specs(self):
        hbm = pl.BlockSpec(memory_space=pl.ANY)
        return [hbm] * len(self.ins), [hbm] * self.n, _fanout_sems(self.n) if self.n else []

    def run(self, in_refs, out_refs, sems, nsteps, compute):
        if not self.n:
            compute()
            return

        def hooks():
            srcs, dsts, owners = self.plan(in_refs, out_refs)
            return _fanout(srcs, dsts, sems[0], sems[1], sems[2], owners)

        pl.when(pl.program_id(0) == 0)(lambda: hooks()[0]())
        compute()
        pl.when(pl.program_id(0) == nsteps - 1)(lambda: hooks()[1]())


def _fanout_sems(n):
    return [pltpu.SemaphoreType.DMA(((NDEV - 1) * n,)), pltpu.SemaphoreType.DMA(((NDEV - 1) * n,)), pltpu.SemaphoreType.DMA((n,))]


def _lanes(j):
    return pl.ds(pl.multiple_of(j * 128, 128), 128)


def _gather_first(c_rows, wi, gate):
    def body(c_ref, wi_ref, g_ref, oc, owi, og, send_sems, recv_sems, local_sems):
        srcs = [lambda j, r=r: r for r in (c_ref, wi_ref, g_ref)]
        dsts = [lambda j, r=r: r.at[j] for r in (oc, owi, og)]
        start, finish = _fanout(srcs, dsts, send_sems, recv_sems, local_sems)
        start()
        finish()

    hbm = pl.BlockSpec(memory_space=pl.ANY)
    return pl.pallas_call(
        body, name="gather_first",
        out_shape=(jax.ShapeDtypeStruct((NDEV,) + c_rows.shape, f32), jax.ShapeDtypeStruct((NDEV,) + wi.shape, bf16),
                   jax.ShapeDtypeStruct((NDEV,) + gate.shape, f32)),
        in_specs=[hbm] * 3, out_specs=(hbm,) * 3, scratch_shapes=_fanout_sems(3),
    )(c_rows, wi, gate)


SHARD = INW // NDEV
ROWS_RP = 128


def _overlap(lo, hi, a, b):
    s, e = max(lo, a), min(hi, b)
    return (s, e) if s < e else None


def _repack_w(wg):
    segs = ((0, 1024, PQ), (1024, 1024 + 2 * RANK, PLR), (1024 + 2 * RANK, INW, PZB))

    def body(g_ref, o_ref):
        for j in range(NDEV):
            lo, hi = j * SHARD, (j + 1) * SHARD
            for a, b, pad0 in segs:
                ov = _overlap(lo, hi, a, b)
                if ov:
                    s, e = ov
                    o_ref[:, pad0 + s - a:pad0 + e - a] = g_ref[j, :, s - lo:e - lo]
        o_ref[:, PLR + 2 * RANK:PW] = jnp.zeros((ROWS_RP, PW - PLR - 2 * RANK), bf16)

    return pl.pallas_call(
        body, name="repack_w", grid=(D // ROWS_RP,), out_shape=jax.ShapeDtypeStruct((D, PW), bf16),
        in_specs=[pl.BlockSpec((NDEV, ROWS_RP, SHARD), lambda i: (0, i, 0))],
        out_specs=pl.BlockSpec((ROWS_RP, PW), lambda i: (i, 0)), compiler_params=_cp(("arbitrary",)),
    )(wg)


LATE_END = 1024 + 2 * RANK
LATE_DESTS = 2


def _pack_blocks(o_ref, srcs, dests, dtype):
    for n, j in enumerate(dests):
        lo, hi = j * SHARD, (j + 1) * SHARD
        done = lo
        for a, b, src in srcs:
            ov = _overlap(lo, hi, a, b)
            if ov:
                s, e = ov
                if s > done:
                    o_ref[n, :, done - lo:s - lo] = jnp.zeros((ROWS_RP, s - done), dtype)
                o_ref[n, :, s - lo:e - lo] = src[:, s - a:e - a].astype(dtype)
                done = e
        if done < hi:
            o_ref[n, :, done - lo:hi - lo] = jnp.zeros((ROWS_RP, hi - done), dtype)


def _pack_dw_early(dw_zbua, dw_va, dw_zag, row0, nrows, name):
    def body(zbua_ref, va_ref, zag_ref, o_ref):
        _pack_blocks(o_ref, ((LATE_END, 2080, zbua_ref), (2080, 2592, va_ref), (2592, INW, zag_ref)), range(NDEV), f32)

    row = lambda w: pl.BlockSpec((ROWS_RP, w), lambda i: (i + row0 // ROWS_RP, 0))
    return pl.pallas_call(
        body, name=name, grid=(nrows // ROWS_RP,), out_shape=jax.ShapeDtypeStruct((NDEV, nrows, SHARD), f32),
        in_specs=[row(2 * AW), row(AW), row(AW + 2 * D)],
        out_specs=pl.BlockSpec((NDEV, ROWS_RP, SHARD), lambda i: (0, i, 0)), compiler_params=_cp(("arbitrary",)),
    )(dw_zbua, dw_va, dw_zag)


def _pack_dw_late(dw_qkv, dw_lr, dwk_c, dwv_c, dwl_c):
    def body(qkv_ref, lr_ref, kc_ref, vc_ref, lc_ref, o_ref):
        qkv = qkv_ref[...] + jnp.concatenate([jnp.zeros((ROWS_RP, KW), f32), kc_ref[...], vc_ref[...]], axis=1)
        lr = lr_ref[...] + lc_ref[...]
        _pack_blocks(o_ref, ((0, 1024, qkv), (1024, LATE_END, lr)), range(LATE_DESTS), bf16)

    row = lambda w: pl.BlockSpec((ROWS_RP, w), lambda i: (i, 0))
    return pl.pallas_call(
        body, name="pack_dw_late", grid=(D // ROWS_RP,), out_shape=jax.ShapeDtypeStruct((LATE_DESTS, D, SHARD), bf16),
        in_specs=[row(2 * KW + VW), row(LRW), row(KW), row(VW), row(LRW)],
        out_specs=pl.BlockSpec((LATE_DESTS, ROWS_RP, SHARD), lambda i: (0, i, 0)), compiler_params=_cp(("arbitrary",)),
    )(dw_qkv, dw_lr, dwk_c, dwv_c, dwl_c)


def _mod_fwd(cs, wm, bm):
    def body(cs_ref, wm_ref, bm_ref, o_ref):
        s, _ = _silu_and_grad(cs_ref[...])
        o_ref[...] = _dot_hi(s, wm_ref[...]) + bm_ref[...]

    return pl.pallas_call(body, name="mod_fwd", out_shape=jax.ShapeDtypeStruct((16, wm.shape[1]), f32),
                          compiler_params=_cp())(cs, wm, bm)


def _mod_bwd(cs, dm, wm):
    def body(cs_ref, dm_ref, wm_ref, gw_ref, gc_ref):
        s, ds = _silu_and_grad(cs_ref[...])
        gw_ref[...] = lax.dot_general(s, dm_ref[...], (((0,), (0,)), ((), ())), preferred_element_type=f32,
                                      precision=lax.Precision.HIGHEST)
        part = lax.dot_general(dm_ref[8:9, :], wm_ref[...], (((1,), (1,)), ((), ())), preferred_element_type=f32,
                               precision=lax.Precision.HIGHEST)
        gc_ref[...] = part * ds[8:9, :]

    return pl.pallas_call(body, name="mod_bwd",
                          out_shape=(jax.ShapeDtypeStruct(wm.shape, f32), jax.ShapeDtypeStruct((1, D), f32)),
                          compiler_params=_cp())(cs, dm, wm)


def _adam_update(g, w_ref, m_ref, v_ref, go_ref, d_ref, mo_ref, vo_ref):
    c1 = 1.0 / (1.0 - ADAM_B1 ** ADAM_STEP)
    c2 = 1.0 / (1.0 - ADAM_B2 ** ADAM_STEP)
    mn = ADAM_B1 * m_ref[...] + (1.0 - ADAM_B1) * g
    vn = ADAM_B2 * v_ref[...] + (1.0 - ADAM_B2) * (g * g)
    go_ref[...] = g
    mo_ref[...] = mn
    vo_ref[...] = vn
    d_ref[...] = -ADAM_LR * ((mn * c1) / (jnp.sqrt(vn * c2) + ADAM_EPS) + ADAM_WD * w_ref[...])


def _adam_w_in(parts_a, parts_b, parts_late, w, m, v):
    na = EARLY_A_ROWS // ROWS_RP

    def body(a_ref, b_ref, l_ref, w_ref, m_ref, v_ref, go_ref, d_ref, mo_ref, vo_ref):
        me = _me()
        first = pl.program_id(0) < na
        early = jnp.where(first, a_ref[0], b_ref[0])
        late = l_ref[0].astype(f32)
        for i in range(1, NDEV):
            early = early + jnp.where(first, a_ref[i], b_ref[i])
            late = late + l_ref[i].astype(f32)
        g = jnp.where(me >= LATE_DESTS - 1, early, 0.0) + jnp.where(me < LATE_DESTS, late, 0.0)
        _adam_update(g, w_ref, m_ref, v_ref, go_ref, d_ref, mo_ref, vo_ref)

    blk = pl.BlockSpec((ROWS_RP, SHARD), lambda i: (i, 0))
    return pl.pallas_call(
        body, name="adam_w_in", grid=(D // ROWS_RP,), out_shape=tuple(jax.ShapeDtypeStruct((D, SHARD), f32) for _ in range(4)),
        in_specs=[pl.BlockSpec((NDEV, ROWS_RP, SHARD), lambda i: (0, jnp.minimum(i, na - 1), 0)),
                  pl.BlockSpec((NDEV, ROWS_RP, SHARD), lambda i: (0, jnp.maximum(i - na, 0), 0)),
                  pl.BlockSpec((NDEV, ROWS_RP, SHARD), lambda i: (0, i, 0)), blk, blk, blk],
        out_specs=(blk, blk, blk, blk), compiler_params=_cp(("arbitrary",)),
    )(parts_a, parts_b, parts_late, w, m, v)


def _adam(parts, w, m, v, rows, name):
    p, r, c = parts.shape

    def body(g_ref, w_ref, m_ref, v_ref, go_ref, d_ref, mo_ref, vo_ref):
        g = g_ref[0]
        for i in range(1, p):
            g = g + g_ref[i]
        _adam_update(g, w_ref, m_ref, v_ref, go_ref, d_ref, mo_ref, vo_ref)

    blk = pl.BlockSpec((rows, c), lambda i: (i, 0))
    return pl.pallas_call(
        body, name=name, grid=(r // rows,), out_shape=tuple(jax.ShapeDtypeStruct((r, c), f32) for _ in range(4)),
        in_specs=[pl.BlockSpec((p, rows, c), lambda i: (0, i, 0)), blk, blk, blk], out_specs=(blk, blk, blk, blk),
        compiler_params=_cp(("arbitrary",)),
    )(parts, w, m, v)


def _in_proj(x, g, scale, shift, w_pad, tl, comm):
    l = x.shape[0]
    c_in, c_out, c_sems = comm.specs()

    def body(*refs):
        x_ref, g_ref, sc_ref, sh_ref, w_ref = refs[:5]
        cin = refs[5:5 + len(c_in)]
        p_ref, h_ref = refs[5 + len(c_in):7 + len(c_in)]
        cout = refs[7 + len(c_in):7 + len(c_in) + comm.n]
        sems = refs[7 + len(c_in) + comm.n:]

        def compute():
            xv = x_ref[...]
            r = lax.rsqrt(jnp.mean(xv * xv, axis=-1, keepdims=True) + EPS)
            h = (xv * r) * (g_ref[...] * (1.0 + sc_ref[...])) + sh_ref[...]
            hb = h.astype(bf16)
            h_ref[...] = hb
            p_ref[...] = jnp.dot(hb, w_ref[...], preferred_element_type=f32).astype(bf16)

        comm.run(cin, cout, sems, l // tl, compute)

    vec = pl.BlockSpec((1, D), lambda i: (0, 0))
    return pl.pallas_call(
        body, name="in_proj", grid=(l // tl,),
        out_shape=(jax.ShapeDtypeStruct((l, PW), bf16), jax.ShapeDtypeStruct((l, D), bf16)) + tuple(comm.outs),
        in_specs=[pl.BlockSpec((tl, D), lambda i: (i, 0)), vec, vec, vec, pl.BlockSpec((D, PW), lambda i: (0, 0))] + c_in,
        out_specs=(pl.BlockSpec((tl, PW), lambda i: (i, 0)), pl.BlockSpec((tl, D), lambda i: (i, 0))) + tuple(c_out),
        scratch_shapes=c_sems, compiler_params=_cp(("arbitrary",)),
    )(x, g, scale, shift, w_pad, *comm.ins)


def _tri(rev):
    i = lax.broadcasted_iota(jnp.int32, (CH, CH), 0)
    j = lax.broadcasted_iota(jnp.int32, (CH, CH), 1)
    return jnp.where((j >= i) if rev else (j <= i), 1.0, 0.0).astype(f32)


def _head_masks():
    lane = lax.broadcasted_iota(jnp.int32, (1, KW), 1) // HK
    return [jnp.where(lane == h, 1.0, 0.0).astype(f32) for h in range(NH)]


def _block_diag():
    r = lax.broadcasted_iota(jnp.int32, (VW, KW), 0) // HV
    c = lax.broadcasted_iota(jnp.int32, (VW, KW), 1) // HK
    return jnp.where(r == c, 1.0, 0.0).astype(f32)


def _decay(lr, w2, gb, tri, rev):
    logits = _dot(lr, w2) + gb
    a = _log_sigmoid(logits) * (1.0 / TAU)
    c = _dot_hi(tri, a)
    cl = c[0:1, :] if rev else c[CH - 1:CH, :]
    return logits, c, cl


def _stack_heads(t, hm):
    return jnp.concatenate([t * hm[h] for h in range(NH)], axis=0)


def _chunk_fwd(q, k, v, c, cl, st, tri, hm, bd):
    tri4 = jnp.concatenate([tri] * NH, axis=0)
    qd = q * jnp.exp(c) * QSCALE
    kd = k * jnp.exp(-c)
    kdec = k * jnp.exp(cl - c)
    pst = _nt(_stack_heads(qd, hm), kd) * tri4
    intra = jnp.concatenate([_dot(pst[h * CH:(h + 1) * CH], v[:, h * HV:(h + 1) * HV]) for h in range(NH)], axis=1)
    o = _nt(qd, st) + intra
    st_new = st * jnp.exp(cl) + bd * _tn(v, kdec)
    return o, st_new


def _state_fwd(k, v, c, cl, st, bd):
    return st * jnp.exp(cl) + bd * _tn(v, k * jnp.exp(cl - c))


def _chunk_bwd(q, k, v, c, cl, st0, dst, do, tri, trit, hm, bd):
    ecl = jnp.exp(cl)
    edec = jnp.exp(cl - c)
    kdec = k * edec
    dv = _nt(kdec, dst)
    dkdec = _dot(v, dst)
    dcl = jnp.sum(dst * st0, axis=0, keepdims=True) * ecl + jnp.sum(dkdec * kdec, axis=0, keepdims=True)
    dc = -dkdec * kdec
    dk = dkdec * edec
    dst0 = dst * ecl
    dq = None
    if q is not None:
        tri4 = jnp.concatenate([tri] * NH, axis=0)
        ec, enc = jnp.exp(c), jnp.exp(-c)
        qd = q * ec * QSCALE
        kd = k * enc
        qs = _stack_heads(qd, hm)
        pst = _nt(qs, kd) * tri4
        dpst = jnp.concatenate([_nt(do[:, h * HV:(h + 1) * HV], v[:, h * HV:(h + 1) * HV]) for h in range(NH)], axis=0) * tri4
        dv = dv + jnp.concatenate([_tn(pst[h * CH:(h + 1) * CH], do[:, h * HV:(h + 1) * HV]) for h in range(NH)], axis=1)
        dqd = _dot(do, st0)
        for h in range(NH):
            dqd = dqd + hm[h] * _dot(dpst[h * CH:(h + 1) * CH], kd)
        dkd = _tn(dpst, qs)
        dst0 = dst0 + bd * _tn(do, qd)
        dc = dc + dqd * qd - dkd * kd
        dq = dqd * ec * QSCALE
        dk = dk + dkd * enc
    da = _dot_hi(trit, dc) + dcl
    return dq, dk, dv, da, dst0


def _gla_fwd(p, w2p, gb, s0, rev, tg, name):
    l = p.shape[0]
    nb, nc = l // tg, tg // CH

    def body(q_ref, k_ref, v_ref, lr_ref, w2_ref, gb_ref, s0_ref, o_ref, st_ref, st):
        @pl.when(pl.program_id(0) == 0)
        def _():
            st[...] = s0_ref[...]

        tri, hm, bd = _tri(rev), _head_masks(), _block_diag()

        def step(j, carry):
            jj = (nc - 1 - j) if rev else j
            rows = pl.ds(pl.multiple_of(jj * CH, CH), CH)
            k = k_ref[rows, :].astype(f32)
            _, c, cl = _decay(lr_ref[rows, :], w2_ref[...], gb_ref[...], tri, rev)
            s = st[...]
            st_ref[jj] = s
            o, s_new = _chunk_fwd(q_ref[rows, :].astype(f32), k, v_ref[rows, :].astype(f32), c, cl, s, tri, hm, bd)
            o_ref[rows, :] = o
            st[...] = s_new
            return carry

        lax.fori_loop(0, nc, step, 0)

    blk = (lambda i: nb - 1 - i) if rev else (lambda i: i)
    return pl.pallas_call(
        body, name=name, grid=(nb,),
        out_shape=(jax.ShapeDtypeStruct((l, VW), f32), jax.ShapeDtypeStruct((l // CH, VW, KW), f32)),
        in_specs=[pl.BlockSpec((tg, KW), lambda i: (blk(i), PQ // KW)), pl.BlockSpec((tg, KW), lambda i: (blk(i), PK // KW)),
                  pl.BlockSpec((tg, VW), lambda i: (blk(i), PV // VW)), pl.BlockSpec((tg, LRW), lambda i: (blk(i), PLR // LRW)),
                  pl.BlockSpec((LRW, KW), lambda i: (0, 0)), pl.BlockSpec((1, KW), lambda i: (0, 0)),
                  pl.BlockSpec((VW, KW), lambda i: (0, 0))],
        out_specs=(pl.BlockSpec((tg, VW), lambda i: (blk(i), 0)), pl.BlockSpec((nc, VW, KW), lambda i: (blk(i), 0, 0))),
        scratch_shapes=[pltpu.VMEM((VW, KW), f32)],
        compiler_params=_cp(("arbitrary",)),
    )(p, p, p, p, w2p, gb, s0)


def _gla_bwd(p, do, states, w2p, gb, prev, rev, tg, name, comm):
    l = p.shape[0]
    nb, nc = l // tg, tg // CH
    out_dt = f32 if prev is None else bf16
    c_in, c_out, c_sems = comm.specs()

    def body(*refs):
        q_ref, k_ref, v_ref, lr_ref, do_ref, st_ref, w2_ref, gb_ref = refs[:8]
        refs = refs[8:]
        if prev is not None:
            pq_ref, pl_ref = refs[:2]
            refs = refs[2:]
        cin, refs = refs[:len(c_in)], refs[len(c_in):]
        dqkv_ref, dlr_ref, dw2_ref, dgb_ref, ds0_ref = refs[:5]
        cout, dst, sems = refs[5:5 + comm.n], refs[5 + comm.n], refs[6 + comm.n:]
        comm.run(cin, cout, sems, nb, lambda: compute(q_ref, k_ref, v_ref, lr_ref, do_ref, st_ref, w2_ref, gb_ref,
                                                      pq_ref if prev is not None else None, pl_ref if prev is not None else None,
                                                      dqkv_ref, dlr_ref, dw2_ref, dgb_ref, ds0_ref, dst))

    def compute(q_ref, k_ref, v_ref, lr_ref, do_ref, st_ref, w2_ref, gb_ref, pq_ref, pl_ref,
                dqkv_ref, dlr_ref, dw2_ref, dgb_ref, ds0_ref, dst):
        @pl.when(pl.program_id(0) == 0)
        def _():
            dst[...] = jnp.zeros_like(dst)
            dw2_ref[...] = jnp.zeros_like(dw2_ref)
            dgb_ref[...] = jnp.zeros_like(dgb_ref)

        tri, trit, hm, bd = _tri(rev), _tri(not rev), _head_masks(), _block_diag()

        def step(j, carry):
            jj = j if rev else (nc - 1 - j)
            rows = pl.ds(pl.multiple_of(jj * CH, CH), CH)
            k = k_ref[rows, :].astype(f32)
            lr = lr_ref[rows, :]
            logits, c, cl = _decay(lr, w2_ref[...], gb_ref[...], tri, rev)
            dq, dk, dv, da, dst0 = _chunk_bwd(q_ref[rows, :].astype(f32), k, v_ref[rows, :].astype(f32), c, cl,
                                              st_ref[jj], dst[...], do_ref[rows, :].astype(f32), tri, trit, hm, bd)
            dst[...] = dst0
            dlog = da * _sigmoid(-logits) * (1.0 / TAU)
            dlr = _nt(dlog, w2_ref[...])
            dw2_ref[...] += _tn(lr, dlog)
            dgb_ref[...] += jnp.sum(dlog, axis=0, keepdims=True)
            dqkv = jnp.concatenate([dq, dk, dv], axis=1)
            if prev is not None:
                dqkv = dqkv + pq_ref[rows, :]
                dlr = dlr + pl_ref[rows, :]
            dqkv_ref[rows, :] = dqkv.astype(out_dt)
            dlr_ref[rows, :] = dlr.astype(out_dt)
            return carry

        lax.fori_loop(0, nc, step, 0)
        ds0_ref[...] = dst[...]

    blk = (lambda i: i) if rev else (lambda i: nb - 1 - i)
    in_specs = [pl.BlockSpec((tg, KW), lambda i: (blk(i), PQ // KW)), pl.BlockSpec((tg, KW), lambda i: (blk(i), PK // KW)),
                pl.BlockSpec((tg, VW), lambda i: (blk(i), PV // VW)), pl.BlockSpec((tg, LRW), lambda i: (blk(i), PLR // LRW)),
                pl.BlockSpec((tg, VW), lambda i: (blk(i), 0)), pl.BlockSpec((nc, VW, KW), lambda i: (blk(i), 0, 0)),
                pl.BlockSpec((LRW, KW), lambda i: (0, 0)), pl.BlockSpec((1, KW), lambda i: (0, 0))]
    args = [p, p, p, p, do, states, w2p, gb]
    if prev is not None:
        in_specs += [pl.BlockSpec((tg, 2 * KW + VW), lambda i: (blk(i), 0)), pl.BlockSpec((tg, LRW), lambda i: (blk(i), 0))]
        args += list(prev)
    return pl.pallas_call(
        body, name=name, grid=(nb,),
        out_shape=(jax.ShapeDtypeStruct((l, 2 * KW + VW), out_dt), jax.ShapeDtypeStruct((l, LRW), out_dt),
                   jax.ShapeDtypeStruct((LRW, KW), f32), jax.ShapeDtypeStruct((1, KW), f32), jax.ShapeDtypeStruct((VW, KW), f32))
        + tuple(comm.outs),
        in_specs=in_specs + c_in,
        out_specs=(pl.BlockSpec((tg, 2 * KW + VW), lambda i: (blk(i), 0)), pl.BlockSpec((tg, LRW), lambda i: (blk(i), 0)),
                   pl.BlockSpec((LRW, KW), lambda i: (0, 0)), pl.BlockSpec((1, KW), lambda i: (0, 0)),
                   pl.BlockSpec((VW, KW), lambda i: (0, 0))) + tuple(c_out),
        scratch_shapes=[pltpu.VMEM((VW, KW), f32)] + c_sems,
        compiler_params=_cp(("arbitrary",)),
    )(*args, *comm.ins)


def _ctx_hidden(ctx_ref, g_ref, sc_ref, sh_ref):
    xv = ctx_ref[...]
    r = lax.rsqrt(jnp.mean(xv * xv, axis=-1, keepdims=True) + EPS)
    xn = xv * r
    return xn, xn * (g_ref[...] * (1.0 + sc_ref[...])) + sh_ref[...]


_CTX_W_SPECS = [pl.BlockSpec((D, KW), lambda i: (0, PK // KW)), pl.BlockSpec((D, VW), lambda i: (0, PV // VW)),
                pl.BlockSpec((D, LRW), lambda i: (0, PLR // LRW))]


def _ctx_fwd(ctx, g, scale, shift, w_pad, w2f, w2b, gbf, gbb):
    ncc = CTX // CH

    def body(ctx_ref, g_ref, sc_ref, sh_ref, wk_ref, wv_ref, wl_ref, w2f_ref, w2b_ref, gbf_ref, gbb_ref, sf_ref, sb_ref):
        _, hc = _ctx_hidden(ctx_ref, g_ref, sc_ref, sh_ref)
        k, v, lr = _dot(hc, wk_ref[...]), _dot(hc, wv_ref[...]), _dot(hc, wl_ref[...])
        bd = _block_diag()
        for rev, w2_ref, gb_ref, out in ((False, w2f_ref, gbf_ref, sf_ref), (True, w2b_ref, gbb_ref, sb_ref)):
            tri = _tri(rev)
            st = jnp.zeros((VW, KW), f32)
            for j in (range(ncc - 1, -1, -1) if rev else range(ncc)):
                rows = slice(j * CH, (j + 1) * CH)
                _, c, cl = _decay(lr[rows], w2_ref[...], gb_ref[...], tri, rev)
                st = _state_fwd(k[rows], v[rows], c, cl, st, bd)
            out[...] = st

    vec = pl.BlockSpec((1, D), lambda i: (0, 0))
    w2s = pl.BlockSpec((LRW, KW), lambda i: (0, 0))
    gbs = pl.BlockSpec((1, KW), lambda i: (0, 0))
    sts = pl.BlockSpec((VW, KW), lambda i: (0, 0))
    return pl.pallas_call(
        body, name="ctx_fwd", grid=(1,), out_shape=(jax.ShapeDtypeStruct((VW, KW), f32),) * 2,
        in_specs=[pl.BlockSpec((CTX, D), lambda i: (0, 0)), vec, vec, vec] + _CTX_W_SPECS + [w2s, w2s, gbs, gbs],
        out_specs=(sts, sts), compiler_params=_cp(("arbitrary",)),
    )(ctx, g, scale, shift, w_pad, w_pad, w_pad, w2f, w2b, gbf, gbb)


def _ctx_bwd(ctx, g, scale, shift, w_pad, w2f, w2b, gbf, gbb, dsf, dsb):
    ncc = CTX // CH

    def body(ctx_ref, g_ref, sc_ref, sh_ref, wk_ref, wv_ref, wl_ref, w2f_ref, w2b_ref, gbf_ref, gbb_ref, dsf_ref, dsb_ref,
             dwk_ref, dwv_ref, dwl_ref, dmod_ref, dg_ref, dw2_ref, dgb_ref):
        xn, hc = _ctx_hidden(ctx_ref, g_ref, sc_ref, sh_ref)
        k, v, lr = _dot(hc, wk_ref[...]), _dot(hc, wv_ref[...]), _dot(hc, wl_ref[...])
        bd = _block_diag()
        dk_rows, dv_rows, dl_rows = [None] * ncc, [None] * ncc, [None] * ncc
        for d, (rev, w2_ref, gb_ref, ds_ref) in enumerate(((False, w2f_ref, gbf_ref, dsf_ref), (True, w2b_ref, gbb_ref, dsb_ref))):
            tri = _tri(rev)
            order = list(range(ncc - 1, -1, -1) if rev else range(ncc))
            st, saved = jnp.zeros((VW, KW), f32), {}
            for j in order:
                rows = slice(j * CH, (j + 1) * CH)
                logits, c, cl = _decay(lr[rows], w2_ref[...], gb_ref[...], tri, rev)
                saved[j] = (logits, c, cl, st)
                st = _state_fwd(k[rows], v[rows], c, cl, st, bd)
            dst = ds_ref[...]
            dw2 = jnp.zeros((LRW, KW), f32)
            dgb = jnp.zeros((1, KW), f32)
            for j in reversed(order):
                rows = slice(j * CH, (j + 1) * CH)
                logits, c, cl, st0 = saved[j]
                _, dk, dv, da, dst = _chunk_bwd(None, k[rows], v[rows], c, cl, st0, dst, None, tri, _tri(not rev), None, bd)
                dlog = da * _sigmoid(-logits) * (1.0 / TAU)
                dl = _nt(dlog, w2_ref[...])
                dw2 = dw2 + _tn(lr[rows], dlog)
                dgb = dgb + jnp.sum(dlog, axis=0, keepdims=True)
                dk_rows[j] = dk if dk_rows[j] is None else dk_rows[j] + dk
                dv_rows[j] = dv if dv_rows[j] is None else dv_rows[j] + dv
                dl_rows[j] = dl if dl_rows[j] is None else dl_rows[j] + dl
            dw2_ref[d] = dw2
            dgb_ref[d] = dgb
        dk, dv, dl = (jnp.concatenate(t, axis=0) for t in (dk_rows, dv_rows, dl_rows))
        dwk_ref[...] = _tn(hc, dk)
        dwv_ref[...] = _tn(hc, dv)
        dwl_ref[...] = _tn(hc, dl)
        dh = _nt(dk, wk_ref[...]) + _nt(dv, wv_ref[...]) + _nt(dl, wl_ref[...])
        gx = dh * xn
        dmod_ref[:, 0:D] = jnp.sum(dh, axis=0, keepdims=True)
        dmod_ref[:, D:2 * D] = jnp.sum(gx, axis=0, keepdims=True) * g_ref[...]
        dg_ref[...] = jnp.sum(gx, axis=0, keepdims=True) * (1.0 + sc_ref[...])

    vec = pl.BlockSpec((1, D), lambda i: (0, 0))
    w2s = pl.BlockSpec((LRW, KW), lambda i: (0, 0))
    gbs = pl.BlockSpec((1, KW), lambda i: (0, 0))
    sts = pl.BlockSpec((VW, KW), lambda i: (0, 0))
    full = lambda *s: pl.BlockSpec(s, lambda i: (0,) * len(s))
    return pl.pallas_call(
        body, name="ctx_bwd", grid=(1,),
        out_shape=(jax.ShapeDtypeStruct((D, KW), f32), jax.ShapeDtypeStruct((D, VW), f32), jax.ShapeDtypeStruct((D, LRW), f32),
                   jax.ShapeDtypeStruct((1, 2 * D), f32), jax.ShapeDtypeStruct((1, D), f32),
                   jax.ShapeDtypeStruct((2, LRW, KW), f32), jax.ShapeDtypeStruct((2, 1, KW), f32)),
        in_specs=[pl.BlockSpec((CTX, D), lambda i: (0, 0)), vec, vec, vec] + _CTX_W_SPECS + [w2s, w2s, gbs, gbs, sts, sts],
        out_specs=(full(D, KW), full(D, VW), full(D, LRW), full(1, 2 * D), full(1, D), full(2, LRW, KW), full(2, 1, KW)),
        compiler_params=_cp(("arbitrary",)),
    )(ctx, g, scale, shift, w_pad, w_pad, w_pad, w2f, w2b, gbf, gbb, dsf, dsb)


def _layernorm(va, g, b):
    mu = jnp.mean(va, axis=-1, keepdims=True)
    xc = va - mu
    rstd = lax.rsqrt(jnp.mean(xc * xc, axis=-1, keepdims=True) + EPS)
    vhat = xc * rstd
    return vhat, rstd, vhat * g + b


def _mix_fwd(p, ln_g, ln_b, ws, bs_t):
    l = p.shape[0]
    nch = l // ACH
    half = AW // 2

    def body(p_ref, g_ref, b_ref, ws_ref, bs_ref, sv_ref, va_buf, col_buf, sem):
        cp = pltpu.make_async_copy(p_ref.at[:, pl.ds(PVA, AW)], va_buf, sem)
        cp.start()
        cp.wait()

        def rows_step(n, carry):
            rows = pl.ds(pl.multiple_of(n * ACH, ACH), ACH)
            _, _, vn = _layernorm(va_buf[rows, :].astype(f32), g_ref[...], b_ref[...])
            for gi in range(2):
                sl = slice(gi * ACH, (gi + 1) * ACH)
                sv_ref[rows, sl] = (_dot(ws_ref[gi], vn[:, sl]) + bs_ref[:, gi:gi + 1]).astype(bf16)
            col_buf[0, rows, :] = vn[:, half:half + ACH]
            col_buf[1, rows, :] = vn[:, half + ACH:]
            return carry

        lax.fori_loop(0, nch, rows_step, 0)

        def cols_step(cidx, carry):
            rows = pl.ds(cidx, ACH, stride=GW)
            for gi in range(2, 4):
                col_buf[gi - 2, rows, :] = _dot(ws_ref[gi], col_buf[gi - 2, rows, :]) + bs_ref[:, gi:gi + 1]
            return carry

        lax.fori_loop(0, GW, cols_step, 0)

        def out_step(n, carry):
            rows = pl.ds(pl.multiple_of(n * ACH, ACH), ACH)
            sv_ref[rows, half:half + ACH] = col_buf[0, rows, :].astype(bf16)
            sv_ref[rows, half + ACH:] = col_buf[1, rows, :].astype(bf16)
            return carry

        lax.fori_loop(0, nch, out_step, 0)

    vm = pl.BlockSpec(memory_space=pltpu.VMEM)
    return pl.pallas_call(
        body, name="mix_fwd", out_shape=jax.ShapeDtypeStruct((l, AW), bf16),
        in_specs=[pl.BlockSpec(memory_space=pl.ANY), vm, vm, vm, vm], out_specs=vm,
        scratch_shapes=[pltpu.VMEM((l, AW), bf16), pltpu.VMEM((2, l, ACH), f32), pltpu.SemaphoreType.DMA(())],
        compiler_params=_cp(),
    )(p, ln_g, ln_b, ws, bs_t)


def _mix_bwd(p, dsv, ln_g, ln_b, ws_t):
    l = p.shape[0]
    nch = l // ACH
    half = AW // 2

    def body(p_ref, dsv_hbm, g_ref, b_ref, wst_ref, dva_ref, dws_ref, dbs_ref, dg_ref, db_ref, va_buf, dsv_buf, vn_col, ds_col, sems):
        cp1 = pltpu.make_async_copy(p_ref.at[:, pl.ds(PVA, AW)], va_buf, sems.at[0])
        cp2 = pltpu.make_async_copy(dsv_hbm, dsv_buf, sems.at[1])
        cp1.start()
        cp2.start()
        dws_ref[...] = jnp.zeros_like(dws_ref)
        dbs_ref[...] = jnp.zeros_like(dbs_ref)
        dg_ref[...] = jnp.zeros_like(dg_ref)
        db_ref[...] = jnp.zeros_like(db_ref)
        cp1.wait()
        cp2.wait()

        def rows_step(n, carry):
            rows = pl.ds(pl.multiple_of(n * ACH, ACH), ACH)
            _, _, vn = _layernorm(va_buf[rows, :].astype(f32), g_ref[...], b_ref[...])
            ds = dsv_buf[rows, :].astype(f32)
            for gi in range(2):
                sl = slice(gi * ACH, (gi + 1) * ACH)
                dws_ref[gi] += _nt(ds[:, sl], vn[:, sl])
                dbs_ref[gi] += ds[:, sl]
            for gi in range(2):
                sl = slice(half + gi * ACH, half + (gi + 1) * ACH)
                vn_col[gi, rows, :] = vn[:, sl]
                ds_col[gi, rows, :] = ds[:, sl]
            return carry

        lax.fori_loop(0, nch, rows_step, 0)

        def cols_step(cidx, carry):
            rows = pl.ds(cidx, ACH, stride=GW)
            for gi in range(2, 4):
                ds = ds_col[gi - 2, rows, :]
                dws_ref[gi] += _nt(ds, vn_col[gi - 2, rows, :])
                dbs_ref[gi] += ds
                ds_col[gi - 2, rows, :] = _dot(wst_ref[gi], ds)
            return carry

        lax.fori_loop(0, GW, cols_step, 0)

        def out_step(n, carry):
            rows = pl.ds(pl.multiple_of(n * ACH, ACH), ACH)
            vhat, rstd, _ = _layernorm(va_buf[rows, :].astype(f32), g_ref[...], b_ref[...])
            ds = dsv_buf[rows, :].astype(f32)
            dvn = jnp.concatenate([_dot(wst_ref[0], ds[:, 0:ACH]), _dot(wst_ref[1], ds[:, ACH:half]), ds_col[0, rows, :], ds_col[1, rows, :]], axis=1)
            dg_ref[...] += jnp.sum(dvn * vhat, axis=0, keepdims=True)
            db_ref[...] += jnp.sum(dvn, axis=0, keepdims=True)
            dvh = dvn * g_ref[...]
            dva = rstd * (dvh - jnp.mean(dvh, axis=-1, keepdims=True) - vhat * jnp.mean(dvh * vhat, axis=-1, keepdims=True))
            dva_ref[rows, :] = dva.astype(bf16)
            return carry

        lax.fori_loop(0, nch, out_step, 0)

    vm = pl.BlockSpec(memory_space=pltpu.VMEM)
    hbm = pl.BlockSpec(memory_space=pl.ANY)
    return pl.pallas_call(
        body, name="mix_bwd",
        out_shape=(jax.ShapeDtypeStruct((l, AW), bf16), jax.ShapeDtypeStruct((4, ACH, ACH), f32), jax.ShapeDtypeStruct((4, ACH, ACH), f32),
                   jax.ShapeDtypeStruct((1, AW), f32), jax.ShapeDtypeStruct((1, AW), f32)),
        in_specs=[hbm, hbm, vm, vm, vm], out_specs=(vm, vm, vm, vm, vm),
        scratch_shapes=[pltpu.VMEM((l, AW), bf16), pltpu.VMEM((l, AW), bf16), pltpu.VMEM((2, l, ACH), f32), pltpu.VMEM((2, l, ACH), f32),
                        pltpu.SemaphoreType.DMA((2,))],
        compiler_params=_cp(),
    )(p, dsv, ln_g, ln_b, ws_t)


def _mid(x, tgt, p, o_f, o_b, sv, gate, gf, gb_norm, w_pa, w_pb, w_out, tl):
    l = x.shape[0]

    def body(x_ref, t_ref, zb_ref, ua_ref, za_ref, g1_ref, g2_ref, of_ref, ob_ref, sv_ref, gate_ref, gf_ref, gbn_ref,
             wpa_ref, wpb_ref, wout_ref,
             dx1_ref, dzbua_ref, dzag_ref, dsv_ref, do_ref, dwout_ref, dwpa_ref, dwpb_ref, dgf_ref, dgate_ref, dgbn_ref, loss_ref):
        @pl.when(pl.program_id(0) == 0)
        def _():
            for r in (dwout_ref, dwpa_ref, dwpb_ref, dgf_ref, dgate_ref, dgbn_ref, loss_ref):
                r[...] = jnp.zeros_like(r)

        o = of_ref[...] + ob_ref[...]
        rr = jnp.concatenate(
            [jnp.broadcast_to(lax.rsqrt(jnp.mean(o[:, h * HV:(h + 1) * HV] ** 2, axis=-1, keepdims=True) + EPS), (tl, HV))
             for h in range(NH)], axis=1)
        ohat = o * rr
        on = ohat * gbn_ref[...]
        szb, dszb = _silu_and_grad(zb_ref[...].astype(f32))
        tb = on * szb
        u = ua_ref[...].astype(f32)
        svv = sv_ref[...].astype(f32)
        sza, dsza = _silu_and_grad(za_ref[...].astype(f32))
        ta = u * svv * sza
        ya = _dot(ta, wpa_ref[...])
        yb = _dot(tb, wpb_ref[...])
        g1 = _sigmoid(g1_ref[...].astype(f32))
        g2 = _sigmoid(g2_ref[...].astype(f32))
        m = g1 * ya + g2 * yb
        y2 = _dot(m, wout_ref[...])
        x1 = x_ref[...] + gate_ref[...] * y2
        r1 = lax.rsqrt(jnp.mean(x1 * x1, axis=-1, keepdims=True) + EPS)
        x1n = x1 * r1
        err = x1n * gf_ref[...] - t_ref[...]
        loss_ref[...] += jnp.sum(jnp.sum(err * err, axis=-1, keepdims=True), axis=0, keepdims=True) * (0.5 / D)
        dout = err * (1.0 / D)
        dgf_ref[...] += jnp.sum(dout * x1n, axis=0, keepdims=True)
        dx1n = dout * gf_ref[...]
        dx1 = r1 * (dx1n - x1n * jnp.mean(dx1n * x1n, axis=-1, keepdims=True))
        dx1_ref[...] = dx1
        dgate_ref[...] += jnp.sum(dx1 * y2, axis=0, keepdims=True)
        dy2 = dx1 * gate_ref[...]
        dwout_ref[...] += _tn(m, dy2)
        dm = _nt(dy2, wout_ref[...])
        dya = dm * g1
        dyb = dm * g2
        dzag_ref[:, AW:AW + D] = (dm * ya * g1 * (1.0 - g1)).astype(bf16)
        dzag_ref[:, AW + D:] = (dm * yb * g2 * (1.0 - g2)).astype(bf16)
        dwpa_ref[...] += _tn(ta, dya)
        dta = _nt(dya, wpa_ref[...])
        dzbua_ref[:, AW:] = (dta * svv * sza).astype(bf16)
        dsv_ref[...] = (dta * u * sza).astype(bf16)
        dzag_ref[:, 0:AW] = (dta * u * svv * dsza).astype(bf16)
        dwpb_ref[...] += _tn(tb, dyb)
        dtb = _nt(dyb, wpb_ref[...])
        don = dtb * szb
        dzbua_ref[:, 0:AW] = (dtb * on * dszb).astype(bf16)
        dgbn_ref[...] += jnp.sum(don * ohat, axis=0, keepdims=True)
        doh = don * gbn_ref[...]
        prod = doh * ohat
        mh = jnp.concatenate(
            [jnp.broadcast_to(jnp.mean(prod[:, h * HV:(h + 1) * HV], axis=-1, keepdims=True), (tl, HV)) for h in range(NH)], axis=1)
        do_ref[...] = (rr * (doh - ohat * mh)).astype(bf16)

    row = lambda w, j: pl.BlockSpec((tl, w), lambda i, j=j: (i, j))
    full = lambda *s: pl.BlockSpec(s, lambda i: (0,) * len(s))
    return pl.pallas_call(
        body, name="mid", grid=(l // tl,),
        out_shape=(jax.ShapeDtypeStruct((l, D), f32), jax.ShapeDtypeStruct((l, 2 * AW), bf16), jax.ShapeDtypeStruct((l, AW + 2 * D), bf16),
                   jax.ShapeDtypeStruct((l, AW), bf16), jax.ShapeDtypeStruct((l, VW), bf16),
                   jax.ShapeDtypeStruct((D, D), f32), jax.ShapeDtypeStruct((AW, D), f32), jax.ShapeDtypeStruct((VW, D), f32),
                   jax.ShapeDtypeStruct((1, D), f32), jax.ShapeDtypeStruct((1, D), f32), jax.ShapeDtypeStruct((1, VW), f32),
                   jax.ShapeDtypeStruct((1, 1), f32)),
        in_specs=[row(D, 0), row(D, 0), row(AW, PZB // AW), row(AW, PUA // AW), row(AW, PZA // AW), row(D, PG1 // D), row(D, PG2 // D),
                  row(VW, 0), row(VW, 0), row(AW, 0), full(1, D), full(1, D), full(1, VW), full(AW, D), full(VW, D), full(D, D)],
        out_specs=(row(D, 0), row(2 * AW, 0), row(AW + 2 * D, 0), row(AW, 0), row(VW, 0),
                   full(D, D), full(AW, D), full(VW, D), full(1, D), full(1, D), full(1, VW), full(1, 1)),
        compiler_params=_cp(("arbitrary",)),
    )(x, tgt, p, p, p, p, p, o_f, o_b, sv, gate, gf, gb_norm, w_pa, w_pb, w_out)


def _in_bwd(x, dx1, dqkv, dzbua, dva, dzag, dlr, w_pad, g, scale, tl, comm):
    l = x.shape[0]
    c_in, c_out, c_sems = comm.specs()

    def body(*refs):
        cin = refs[14:14 + len(c_in)]
        outs = refs[14 + len(c_in):]
        comm.run(cin, outs[4:4 + comm.n], outs[4 + comm.n:], l // tl, lambda: compute(*refs[:14], *outs[:4]))

    def compute(x_ref, dx1_ref, a_ref, b_ref, c_ref, e_ref, lr_ref, wa_ref, wb_ref, wc_ref, we_ref, wl_ref, g_ref, sc_ref,
                gx_ref, dsh_ref, dsc_ref, dg_ref):
        @pl.when(pl.program_id(0) == 0)
        def _():
            for r in (dsh_ref, dsc_ref, dg_ref):
                r[...] = jnp.zeros_like(r)

        dh = (_nt(a_ref[...], wa_ref[...]) + _nt(b_ref[...], wb_ref[...]) + _nt(c_ref[...], wc_ref[...]) + _nt(e_ref[...], we_ref[...])
              + _nt(lr_ref[...], wl_ref[...]))
        xv = x_ref[...]
        r = lax.rsqrt(jnp.mean(xv * xv, axis=-1, keepdims=True) + EPS)
        xn = xv * r
        gxn = jnp.sum(dh * xn, axis=0, keepdims=True)
        dsh_ref[...] += jnp.sum(dh, axis=0, keepdims=True)
        dsc_ref[...] += gxn * g_ref[...]
        dg_ref[...] += gxn * (1.0 + sc_ref[...])
        dxn = dh * (g_ref[...] * (1.0 + sc_ref[...]))
        gx_ref[...] = dx1_ref[...] + r * (dxn - xn * jnp.mean(dxn * xn, axis=-1, keepdims=True))

    row = lambda w: pl.BlockSpec((tl, w), lambda i: (i, 0))
    wcol = lambda w, j: pl.BlockSpec((D, w), lambda i, j=j: (0, j))
    vec = pl.BlockSpec((1, D), lambda i: (0, 0))
    return pl.pallas_call(
        body, name="in_bwd", grid=(l // tl,),
        out_shape=(jax.ShapeDtypeStruct((l, D), f32),) + (jax.ShapeDtypeStruct((1, D), f32),) * 3 + tuple(comm.outs),
        in_specs=[row(D), row(D), row(2 * KW + VW), row(2 * AW), row(AW), row(AW + 2 * D), row(LRW),
                  wcol(2 * KW + VW, 0), wcol(2 * AW, PZB // (2 * AW)), wcol(AW, PVA // AW), wcol(AW + 2 * D, PZA // (AW + 2 * D)),
                  wcol(LRW, PLR // LRW), vec, vec] + c_in,
        out_specs=(row(D), vec, vec, vec) + tuple(c_out), scratch_shapes=c_sems,
        compiler_params=_cp(("arbitrary",)),
    )(x, dx1, dqkv, dzbua, dva, dzag, dlr, w_pad, w_pad, w_pad, w_pad, w_pad, g, scale, *comm.ins)


def _tn_matmul(a, b, tl, name):
    l, m = a.shape
    n = b.shape[1]

    def body(a_ref, b_ref, o_ref):
        @pl.when(pl.program_id(0) == 0)
        def _():
            o_ref[...] = jnp.zeros_like(o_ref)

        o_ref[...] += _tn(a_ref[...], b_ref[...])

    return pl.pallas_call(
        body, name=name, grid=(l // tl,), out_shape=jax.ShapeDtypeStruct((m, n), f32),
        in_specs=[pl.BlockSpec((tl, m), lambda i: (i, 0)), pl.BlockSpec((tl, n), lambda i: (i, 0))],
        out_specs=pl.BlockSpec((m, n), lambda i: (0, 0)), compiler_params=_cp(("arbitrary",)),
    )(a, b)


def _pad_gate(w2, gb):
    z = jnp.zeros((RANK, KW), f32)
    tail = jnp.zeros((LRW - 2 * RANK, KW), f32)
    w2f = jnp.concatenate([w2[0], z, tail], axis=0)
    w2b = jnp.concatenate([z, w2[1], tail], axis=0)
    return w2f, w2b, gb[0:1], gb[1:2]


EARLY_A_ROWS = 256


class _NoExchange:
    def __init__(self, w_pa, w_pb, w_out):
        self.weights = (w_pa, w_pb, w_out)

    def gather_proj(self):
        return _Comm([], [], None)

    def proj_weights(self, got):
        return self.weights

    def early_a(self, dw_out, dw_pa, dw_pb, small, blocks):
        return _Comm([], [], None)

    def early_b(self, blocks):
        return _Comm([], [], None)

    def late(self, blocks, dgt):
        return _Comm([], [], None)


class _Exchanges:
    def __init__(self, pa, pb, wo):
        self.shards = (pa, pb, wo)

    def gather_proj(self):
        def plan(i, o):
            srcs = [lambda j, r=r: r for r in i]
            dsts = [lambda j: o[0].at[:, _lanes(j)], lambda j: o[1].at[:, _lanes(j)], lambda j: o[2].at[j]]
            return srcs, dsts, None
        sds = jax.ShapeDtypeStruct
        return _Comm(self.shards, [sds((AW, D), bf16), sds((VW, D), bf16), sds((NDEV, 128, D), bf16)], plan)

    def proj_weights(self, got):
        return got[0], got[1], got[2].reshape(D, D)

    def early_a(self, dw_out, dw_pa, dw_pb, small, blocks):
        def plan(i, o):
            srcs = [lambda j: i[0].at[j], lambda j: i[1].at[:, _lanes(j)], lambda j: i[2].at[:, _lanes(j)], lambda j: i[3],
                    lambda j: i[4].at[j]]
            dsts = [lambda j, r=r: r.at[j] for r in o]
            return srcs, dsts, [None, None, None, None, lambda j: j >= LATE_DESTS - 1]
        sds = jax.ShapeDtypeStruct
        return _Comm([dw_out.reshape(NDEV, 128, D), dw_pa, dw_pb, small, blocks],
                     [sds((NDEV, 128, D), f32), sds((NDEV, AW, 128), f32), sds((NDEV, VW, 128), f32),
                      sds((NDEV,) + small.shape, f32), sds(blocks.shape, f32)], plan)

    def early_b(self, blocks):
        def plan(i, o):
            return [lambda j: i[0].at[j]], [lambda j: o[0].at[j]], [lambda j: j >= LATE_DESTS - 1]
        return _Comm([blocks], [jax.ShapeDtypeStruct(blocks.shape, f32)], plan)

    def late(self, blocks, dgt):
        def plan(i, o):
            srcs = [lambda j: i[0].at[jnp.minimum(j, LATE_DESTS - 1)], lambda j: i[1].at[j]]
            dsts = [lambda j, r=r: r.at[j] for r in o]
            return srcs, dsts, [lambda j: j < LATE_DESTS, None]
        sds = jax.ShapeDtypeStruct
        return _Comm([blocks, dgt], [sds((NDEV, D, SHARD), bf16), sds(dgt.shape, f32)], plan)


def _local_step(x, ctx, tgt, mod, modc, norm_g, w_pad, ln_g, ln_b, ws, bs, w2, gb, gb_norm, gf, xch):
    shift, scale, gate = mod[:, 0:D], mod[:, D:2 * D], mod[:, 2 * D:]
    shift_c, scale_c = modc[:, 0:D], modc[:, D:]
    w2f, w2b, gbf, gbb = _pad_gate(w2, gb)

    p, h, *got_proj = _in_proj(x, norm_g, scale, shift, w_pad, 256, xch.gather_proj())
    w_pa, w_pb, w_out = xch.proj_weights(got_proj)
    sc_f, sc_b = _ctx_fwd(ctx, norm_g, scale_c, shift_c, w_pad, w2f, w2b, gbf, gbb)
    o_f, st_f = _gla_fwd(p, w2f, gbf, sc_f, False, 512, "gla_fwd_f")
    o_b, st_b = _gla_fwd(p, w2b, gbb, sc_b, True, 512, "gla_fwd_b")
    sv = _mix_fwd(p, ln_g, ln_b, ws.astype(bf16), bs.T)
    (dx1, dzbua, dzag, dsv, do, dw_out, dw_pa, dw_pb, dgf, dgate, dgbn, loss) = _mid(
        x, tgt, p, o_f, o_b, sv, gate, gf, gb_norm, w_pa, w_pb, w_out, 256)
    dva, dws, dbs_acc, dln_g, dln_b = _mix_bwd(p, dsv, ln_g, ln_b, jnp.swapaxes(ws, 1, 2).astype(bf16))
    dw_zbua = _tn_matmul(h, dzbua, 512, "dw_zbua")
    dw_va = _tn_matmul(h, dva, 512, "dw_va")
    dw_zag = _tn_matmul(h, dzag, 512, "dw_zag")
    blocks_a = _pack_dw_early(dw_zbua, dw_va, dw_zag, 0, EARLY_A_ROWS, "pack_dw_early_a")
    blocks_b = _pack_dw_early(dw_zbua, dw_va, dw_zag, EARLY_A_ROWS, D - EARLY_A_ROWS, "pack_dw_early_b")
    small = _rows128(dln_g, dln_b, dws, jnp.sum(dbs_acc, axis=-1), dgbn, dgf, jnp.broadcast_to(loss, (1, 128)))

    dqkv_f, dlr_f, dw2f, dgbf, dsc_f, *got_a = _gla_bwd(p, do, st_f, w2f, gbf, None, False, 512, "gla_bwd_f",
                                                        xch.early_a(dw_out, dw_pa, dw_pb, small, blocks_a))
    dqkv, dlr, dw2b, dgbb, dsc_b, *got_b = _gla_bwd(p, do, st_b, w2b, gbb, (dqkv_f, dlr_f), True, 512, "gla_bwd_b",
                                                    xch.early_b(blocks_b))
    dwk_c, dwv_c, dwl_c, dmodc, dg_c, dw2c, dgbc = _ctx_bwd(ctx, norm_g, scale_c, shift_c, w_pad, w2f, w2b, gbf, gbb, dsc_f, dsc_b)
    dw_qkv = _tn_matmul(h, dqkv, 512, "dw_qkv")
    dw_lr = _tn_matmul(h, dlr, 512, "dw_lr")
    blocks_late = _pack_dw_late(dw_qkv, dw_lr, dwk_c, dwv_c, dwl_c)
    dw2 = jnp.stack([dw2f[0:RANK] + dw2c[0, 0:RANK], dw2b[RANK:2 * RANK] + dw2c[1, RANK:2 * RANK]])
    dgb = jnp.concatenate([dgbf + dgbc[0], dgbb + dgbc[1]], axis=0)
    dgt = jnp.concatenate([jnp.transpose(dw2.reshape(2, RANK, NDEV, 32), (2, 0, 1, 3)).reshape(NDEV, 2 * RANK * 32),
                           jnp.transpose(dgb.reshape(2, NDEV, 32), (1, 0, 2)).reshape(NDEV, 64),
                           jnp.zeros((NDEV, 64), f32)], axis=1).reshape(NDEV, 9, 128)
    gx, dshift, dscale, dg, *got_late = _in_bwd(x, dx1, dqkv, dzbua, dva, dzag, dlr, w_pad, norm_g, scale, 256,
                                                xch.late(blocks_late, dgt))
    return dict(loss=loss, gx=gx, dmod=jnp.concatenate([dshift, dscale, dgate], axis=1), dmodc=dmodc, dnorm_g=dg + dg_c,
                small=small, blocks_a=blocks_a, blocks_b=blocks_b, blocks_late=blocks_late, dw2=dw2, dgb=dgb,
                dw_pa=dw_pa, dw_pb=dw_pb, dw_out=dw_out, got_a=got_a, got_b=got_b, got_late=got_late)


def _rows128(*vs):
    out = []
    for t in vs:
        t = t.reshape(-1)
        pad = (-t.shape[0]) % 128
        out.append(jnp.pad(t, (0, pad)) if pad else t)
    return jnp.concatenate(out).reshape(-1, 128)


def kernel(x, c, ctx, c_ctx, w_mod, b_mod, norm_g, w_in, a_ln_g, a_ln_b, a_ws, a_bs, b_gate_w2, b_gate_b, b_norm_g, w_proj_a, w_proj_b, w_out, final_norm_g, loss_target, m_c_ctx, m_w_mod, m_b_mod, m_norm_g, m_w_in, m_a_ln_g, m_a_ln_b, m_a_ws, m_a_bs, m_b_gate_w2, m_b_gate_b, m_b_norm_g, m_w_proj_a, m_w_proj_b, m_w_out, m_final_norm_g, v_c_ctx, v_w_mod, v_b_mod, v_norm_g, v_w_in, v_a_ln_g, v_a_ln_b, v_a_ws, v_a_bs, v_b_gate_w2, v_b_gate_b, v_b_norm_g, v_w_proj_a, v_w_proj_b, v_w_out, v_final_norm_g):
    me = _me()
    ncol = w_mod.shape[2]

    gate_mine = _rows128(jnp.concatenate([b_gate_w2.reshape(-1), b_gate_b.reshape(-1)]))
    early, wg, gates = _gather_first(_rows128(c), w_in[0].astype(bf16), gate_mine)
    cs = jnp.concatenate([early.reshape(NDEV, D), c_ctx.reshape(1, D), jnp.zeros((7, D), f32)], axis=0)
    w_pad = _repack_w(wg)
    gflat = gates.reshape(NDEV, 9 * 128)
    w2 = jnp.transpose(gflat[:, 0:2 * RANK * 32].reshape(NDEV, 2, RANK, 32), (1, 2, 0, 3)).reshape(2, RANK, KW)
    gb = jnp.transpose(gflat[:, 2 * RANK * 32:2 * RANK * 32 + 64].reshape(NDEV, 2, 32), (1, 0, 2)).reshape(2, KW)

    bm_mine = lax.dynamic_slice(b_mod, (0, me * ncol), (1, ncol))
    mods = _all_gather(_mod_fwd(cs, w_mod[0], bm_mine), "gather_mod")
    mods = jnp.transpose(mods, (1, 0, 2)).reshape(16, 3 * D)
    mod = lax.dynamic_slice(mods, (me, 0), (1, 3 * D))
    modc = mods[8:9, 0:2 * D]

    xch = _Exchanges(w_proj_a[0].astype(bf16), w_proj_b[0].astype(bf16), w_out[0].astype(bf16))
    r = _local_step(x[0], ctx[0], loss_target[0], mod, modc, norm_g, w_pad, a_ln_g, a_ln_b, a_ws[0], a_bs[0], w2, gb,
                    b_norm_g, final_norm_g.reshape(1, D), xch)
    p_out, p_pa, p_pb, smalls_e, p_in_a = r["got_a"]
    (p_in_b,) = r["got_b"]
    p_in_late, p_gt = r["got_late"]

    def split(t, sizes, shapes):
        t = t.reshape(-1)
        out, o = [], 0
        for s, sh in zip(sizes, shapes):
            out.append(t[o:o + s].reshape(sh))
            o += s
        return out

    n_e = (AW + AW + 4 * ACH * ACH + AW + VW + D) // 128
    rep_e = _adam(smalls_e[:, 0:n_e], _rows128(a_ln_g, a_ln_b, a_ws, a_bs, b_norm_g, final_norm_g),
                  _rows128(m_a_ln_g, m_a_ln_b, m_a_ws, m_a_bs, m_b_norm_g, m_final_norm_g),
                  _rows128(v_a_ln_g, v_a_ln_b, v_a_ws, v_a_bs, v_b_norm_g, v_final_norm_g), n_e, "adam_rep_early")
    rep_e = [split(t, (AW, AW, 4 * ACH * ACH, AW, VW, D), ((1, AW), (1, AW), (1, 4, ACH, ACH), (1, 4, ACH), (1, VW), (D,)))
             for t in rep_e]
    losses = smalls_e[:, n_e, 0]
    loss = losses[0]
    for i in range(1, NDEV):
        loss = loss + losses[i]

    dbm = r["dmod"] + jnp.concatenate([r["dmodc"], jnp.zeros((1, D), f32)], axis=1)
    smalls_l = _all_gather(_rows128(r["dnorm_g"], dbm, r["dmod"], r["dmodc"]), "gather_small")
    n_l = (D + 3 * D) // 128
    rep_l = _adam(smalls_l[:, 0:n_l], _rows128(norm_g, b_mod), _rows128(m_norm_g, m_b_mod), _rows128(v_norm_g, v_b_mod), n_l,
                  "adam_rep_late")
    rep_l = [split(t, (D, 3 * D), ((1, D), (1, 3 * D))) for t in rep_l]
    tail = smalls_l[:, n_l:].reshape(NDEV, -1)
    dmods = tail[:, 0:3 * D]
    dmodc_all = tail[:, 3 * D:5 * D]
    dmodc_tot = dmodc_all[0:1]
    for i in range(1, NDEV):
        dmodc_tot = dmodc_tot + dmodc_all[i:i + 1]

    dm_rows = jnp.concatenate([dmods, jnp.concatenate([dmodc_tot, jnp.zeros((1, D), f32)], axis=1), jnp.zeros((7, 3 * D), f32)], axis=0)
    dm_mine = lax.dynamic_slice(dm_rows, (0, me * ncol), (16, ncol))
    g_wmod, gc_part = _mod_bwd(cs, dm_mine, w_mod[0])
    wm = _adam(g_wmod[None], w_mod[0], m_w_mod[0], v_w_mod[0], 256, "adam_wmod")
    gcs = _all_gather(gc_part.reshape(8, 128), "gather_cctx")
    cc = _adam(gcs, c_ctx.reshape(8, 128), m_c_ctx.reshape(8, 128), v_c_ctx.reshape(8, 128), 8, "adam_cctx")

    a_in = _adam_w_in(p_in_a, p_in_b, p_in_late, w_in[0], m_w_in[0], v_w_in[0])
    a_pa = _adam(p_pa, w_proj_a[0], m_w_proj_a[0], v_w_proj_a[0], AW, "adam_w_pa")
    a_pb = _adam(p_pb, w_proj_b[0], m_w_proj_b[0], v_w_proj_b[0], VW, "adam_w_pb")
    a_out = _adam(p_out, w_out[0], m_w_out[0], v_w_out[0], 128, "adam_w_out")
    gate_m = _rows128(jnp.concatenate([m_b_gate_w2.reshape(-1), m_b_gate_b.reshape(-1)]))
    gate_v = _rows128(jnp.concatenate([v_b_gate_w2.reshape(-1), v_b_gate_b.reshape(-1)]))
    a_gt = [t.reshape(-1) for t in _adam(p_gt, gate_mine, gate_m, gate_v, 9, "adam_gate")]
    nw2 = 2 * RANK * 32
    sh = [(a_in[k][None], a_pa[k][None], a_pb[k][None], a_out[k][None], a_gt[k][0:nw2].reshape(1, 2, RANK, 32),
           a_gt[k][nw2:nw2 + 64].reshape(1, 2, 32)) for k in range(4)]

    outs = [loss, r["gx"][None]]
    for k in range(4):
        lg, lb, aws, abs_, bng, fng = rep_e[k]
        n_g, bmod = rep_l[k]
        s_in, s_pa, s_pb, s_out, s_w2, s_gb = sh[k]
        outs += [cc[k].reshape(D), wm[k][None], bmod, n_g, s_in, lg, lb, aws, abs_, s_w2, s_gb, bng, s_pa, s_pb, s_out, fng]
    return tuple(outs)
```

```python
import functools

import jax
import jax.numpy as jnp
from jax import lax
from jax.experimental import pallas as pl
from jax.experimental.pallas import tpu as pltpu

f32, bf16 = jnp.float32, jnp.bfloat16

D = 1024
CTX = 256
EPS = 1e-6
AW = 512
ACH = 128
GW = 64
KW = 256
VW = 512
NH = 4
HK = 64
HV = 128
RANK = 16
TAU = 16.0
CH = 64
QSCALE = HK ** -0.5
INW = 5152
NDEV = 8

PQ, PK, PV, PZB, PUA, PVA, PZA, PG1, PG2, PLR, PW = 0, 256, 512, 1024, 1536, 2048, 2560, 3072, 4096, 5120, 5248
LRW = 128

ADAM_LR, ADAM_B1, ADAM_B2, ADAM_EPS, ADAM_WD, ADAM_STEP = 0.001, 0.9, 0.999, 1e-08, 0.01, 10

VMEM_LIMIT = 56 * 1024 * 1024
MESH = pl.DeviceIdType.MESH


def _cp(sem=None):
    return pltpu.CompilerParams(dimension_semantics=sem, vmem_limit_bytes=VMEM_LIMIT)


def _dot(a, b):
    return jnp.dot(a.astype(bf16), b.astype(bf16), preferred_element_type=f32)


def _nt(a, b):
    return lax.dot_general(a.astype(bf16), b.astype(bf16), (((1,), (1,)), ((), ())), preferred_element_type=f32)


def _tn(a, b):
    return lax.dot_general(a.astype(bf16), b.astype(bf16), (((0,), (0,)), ((), ())), preferred_element_type=f32)


def _dot_hi(a, b):
    return jnp.dot(a, b, preferred_element_type=f32, precision=lax.Precision.HIGHEST)


def _sigmoid(x):
    return 1.0 / (1.0 + jnp.exp(-x))


def _log_sigmoid(x):
    return jnp.minimum(x, 0.0) - jnp.log(1.0 + jnp.exp(-jnp.abs(x)))


def _silu_and_grad(z):
    s = _sigmoid(z)
    return z * s, s * (1.0 + z * (1.0 - s))


def _me():
    return 4 * lax.axis_index("x") + 2 * lax.axis_index("y") + lax.axis_index("c")


def _peer(k):
    x, y, c = lax.axis_index("x"), lax.axis_index("y"), lax.axis_index("c")
    px = 1 - x if k & 4 else x
    py = 1 - y if k & 2 else y
    pc = 1 - c if k & 1 else c
    return (px, py, pc), 4 * px + 2 * py + pc


def _all_gather(v, name):
    r, c = v.shape

    def body(v_ref, out_ref, send_sems, recv_sems, local_sem):
        me = _me()
        mine = pltpu.make_async_copy(v_ref, out_ref.at[me], local_sem)
        mine.start()
        sends, recvs = [], []
        for k in range(1, NDEV):
            dev, idx = _peer(k)
            sends.append(pltpu.make_async_remote_copy(
                src_ref=v_ref, dst_ref=out_ref.at[me], send_sem=send_sems.at[k - 1], recv_sem=recv_sems.at[k - 1],
                device_id=dev, device_id_type=MESH))
            recvs.append(pltpu.make_async_remote_copy(
                src_ref=v_ref, dst_ref=out_ref.at[idx], send_sem=send_sems.at[k - 1], recv_sem=recv_sems.at[k - 1],
                device_id=dev, device_id_type=MESH))
        for cp in sends:
            cp.start()
        for cp in recvs:
            cp.wait_recv()
        for cp in sends:
            cp.wait_send()
        mine.wait()

    return pl.pallas_call(
        body, name=name, out_shape=jax.ShapeDtypeStruct((NDEV, r, c), v.dtype),
        in_specs=[pl.BlockSpec(memory_space=pl.ANY)], out_specs=pl.BlockSpec(memory_space=pl.ANY),
        scratch_shapes=[pltpu.SemaphoreType.DMA((NDEV - 1,)), pltpu.SemaphoreType.DMA((NDEV - 1,)), pltpu.SemaphoreType.DMA(())],
    )(v)


def _fanout(srcs, dsts, send_sems, recv_sems, local_sems, owners=None):
    me = _me()
    n = len(srcs)
    owns = lambda a, j: True if owners is None or owners[a] is None else owners[a](j)

    def guarded(cond, fn):
        if cond is True:
            fn()
        else:
            pl.when(cond)(fn)

    def copies(with_recvs):
        local = [pltpu.make_async_copy(srcs[a](me), dsts[a](me), local_sems.at[a]) for a in range(n)]
        sends, recvs = [], []
        for k in range(1, NDEV):
            dev, idx = _peer(k)
            for a in range(n):
                s = (k - 1) * n + a
                sends.append((owns(a, idx), pltpu.make_async_remote_copy(
                    src_ref=srcs[a](idx), dst_ref=dsts[a](me), send_sem=send_sems.at[s], recv_sem=recv_sems.at[s],
                    device_id=dev, device_id_type=MESH)))
                if with_recvs:
                    recvs.append((owns(a, me), pltpu.make_async_remote_copy(
                        src_ref=srcs[a](idx), dst_ref=dsts[a](idx), send_sem=send_sems.at[s], recv_sem=recv_sems.at[s],
                        device_id=dev, device_id_type=MESH)))
        return local, sends, recvs

    def start():
        local, sends, _ = copies(False)
        for a, cp in enumerate(local):
            guarded(owns(a, me), cp.start)
        for cond, cp in sends:
            guarded(cond, cp.start)

    def finish():
        local, sends, recvs = copies(True)
        for cond, cp in recvs:
            guarded(cond, cp.wait_recv)
        for cond, cp in sends:
            guarded(cond, cp.wait_send)
        for a, cp in enumerate(local):
            guarded(owns(a, me), cp.wait)

    return start, finish


class _Comm:
    def __init__(self, ins, outs, plan):
        self.ins, self.outs, self.plan = list(ins), list(outs), plan
        self.n = len(self.outs)

    def specs(self):
        hbm = pl.BlockSpec(memory_space=pl.ANY)
        return [hbm] * len(self.ins), [hbm] * self.n, _fanout_sems(self.n) if self.n else []

    def run(self, in_refs, out_refs, sems, nsteps, compute):
        if not self.n:
            compute()
            return

        def hooks():
            srcs, dsts, owners = self.plan(in_refs, out_refs)
            return _fanout(srcs, dsts, sems[0], sems[1], sems[2], owners)

        pl.when(pl.program_id(0) == 0)(lambda: hooks()[0]())
        compute()
        pl.when(pl.program_id(0) == nsteps - 1)(lambda: hooks()[1]())


def _fanout_sems(n):
    return [pltpu.SemaphoreType.DMA(((NDEV - 1) * n,)), pltpu.SemaphoreType.DMA(((NDEV - 1) * n,)), pltpu.SemaphoreType.DMA((n,))]


def _lanes(j):
    return pl.ds(pl.multiple_of(j * 128, 128), 128)


def _gather_first(c_rows, wi, gate):
    def body(c_ref, wi_ref, g_ref, oc, owi, og, send_sems, recv_sems, local_sems):
        srcs = [lambda j, r=r: r for r in (c_ref, wi_ref, g_ref)]
        dsts = [lambda j, r=r: r.at[j] for r in (oc, owi, og)]
        start, finish = _fanout(srcs, dsts, send_sems, recv_sems, local_sems)
        start()
        finish()

    hbm = pl.BlockSpec(memory_space=pl.ANY)
    return pl.pallas_call(
        body, name="gather_first",
        out_shape=(jax.ShapeDtypeStruct((NDEV,) + c_rows.shape, f32), jax.ShapeDtypeStruct((NDEV,) + wi.shape, bf16),
                   jax.ShapeDtypeStruct((NDEV,) + gate.shape, f32)),
        in_specs=[hbm] * 3, out_specs=(hbm,) * 3, scratch_shapes=_fanout_sems(3),
    )(c_rows, wi, gate)


SHARD = INW // NDEV
ROWS_RP = 128


def _overlap(lo, hi, a, b):
    s, e = max(lo, a), min(hi, b)
    return (s, e) if s < e else None


def _repack_w(wg):
    segs = ((0, 1024, PQ), (1024, 1024 + 2 * RANK, PLR), (1024 + 2 * RANK, INW, PZB))

    def body(g_ref, o_ref):
        for j in range(NDEV):
            lo, hi = j * SHARD, (j + 1) * SHARD
            for a, b, pad0 in segs:
                ov = _overlap(lo, hi, a, b)
                if ov:
                    s, e = ov
                    o_ref[:, pad0 + s - a:pad0 + e - a] = g_ref[j, :, s - lo:e - lo]
        o_ref[:, PLR + 2 * RANK:PW] = jnp.zeros((ROWS_RP, PW - PLR - 2 * RANK), bf16)

    return pl.pallas_call(
        body, name="repack_w", grid=(D // ROWS_RP,), out_shape=jax.ShapeDtypeStruct((D, PW), bf16),
        in_specs=[pl.BlockSpec((NDEV, ROWS_RP, SHARD), lambda i: (0, i, 0))],
        out_specs=pl.BlockSpec((ROWS_RP, PW), lambda i: (i, 0)), compiler_params=_cp(("arbitrary",)),
    )(wg)


LATE_END = 1024 + 2 * RANK
LATE_DESTS = 2


def _pack_blocks(o_ref, srcs, dests, dtype):
    for n, j in enumerate(dests):
        lo, hi = j * SHARD, (j + 1) * SHARD
        done = lo
        for a, b, src in srcs:
            ov = _overlap(lo, hi, a, b)
            if ov:
                s, e = ov
                if s > done:
                    o_ref[n, :, done - lo:s - lo] = jnp.zeros((ROWS_RP, s - done), dtype)
                o_ref[n, :, s - lo:e - lo] = src[:, s - a:e - a].astype(dtype)
                done = e
        if done < hi:
            o_ref[n, :, done - lo:hi - lo] = jnp.zeros((ROWS_RP, hi - done), dtype)


def _pack_dw_early(dw_zbua, dw_va, dw_zag, row0, nrows, name):
    def body(zbua_ref, va_ref, zag_ref, o_ref):
        _pack_blocks(o_ref, ((LATE_END, 2080, zbua_ref), (2080, 2592, va_ref), (2592, INW, zag_ref)), range(NDEV), f32)

    row = lambda w: pl.BlockSpec((ROWS_RP, w), lambda i: (i + row0 // ROWS_RP, 0))
    return pl.pallas_call(
        body, name=name, grid=(nrows // ROWS_RP,), out_shape=jax.ShapeDtypeStruct((NDEV, nrows, SHARD), f32),
        in_specs=[row(2 * AW), row(AW), row(AW + 2 * D)],
        out_specs=pl.BlockSpec((NDEV, ROWS_RP, SHARD), lambda i: (0, i, 0)), compiler_params=_cp(("arbitrary",)),
    )(dw_zbua, dw_va, dw_zag)


def _pack_dw_late(dw_qkv, dw_lr, dwk_c, dwv_c, dwl_c):
    def body(qkv_ref, lr_ref, kc_ref, vc_ref, lc_ref, o_ref):
        qkv = qkv_ref[...] + jnp.concatenate([jnp.zeros((ROWS_RP, KW), f32), kc_ref[...], vc_ref[...]], axis=1)
        lr = lr_ref[...] + lc_ref[...]
        _pack_blocks(o_ref, ((0, 1024, qkv), (1024, LATE_END, lr)), range(LATE_DESTS), bf16)

    row = lambda w: pl.BlockSpec((ROWS_RP, w), lambda i: (i, 0))
    return pl.pallas_call(
        body, name="pack_dw_late", grid=(D // ROWS_RP,), out_shape=jax.ShapeDtypeStruct((LATE_DESTS, D, SHARD), bf16),
        in_specs=[row(2 * KW + VW), row(LRW), row(KW), row(VW), row(LRW)],
        out_specs=pl.BlockSpec((LATE_DESTS, ROWS_RP, SHARD), lambda i: (0, i, 0)), compiler_params=_cp(("arbitrary",)),
    )(dw_qkv, dw_lr, dwk_c, dwv_c, dwl_c)


def _mod_fwd(cs, wm, bm):
    def body(cs_ref, wm_ref, bm_ref, o_ref):
        s, _ = _silu_and_grad(cs_ref[...])
        o_ref[...] = _dot_hi(s, wm_ref[...]) + bm_ref[...]

    return pl.pallas_call(body, name="mod_fwd", out_shape=jax.ShapeDtypeStruct((16, wm.shape[1]), f32),
                          compiler_params=_cp())(cs, wm, bm)


def _mod_bwd(cs, dm, wm):
    def body(cs_ref, dm_ref, wm_ref, gw_ref, gc_ref):
        s, ds = _silu_and_grad(cs_ref[...])
        gw_ref[...] = lax.dot_general(s, dm_ref[...], (((0,), (0,)), ((), ())), preferred_element_type=f32,
                                      precision=lax.Precision.HIGHEST)
        part = lax.dot_general(dm_ref[8:9, :], wm_ref[...], (((1,), (1,)), ((), ())), preferred_element_type=f32,
                               precision=lax.Precision.HIGHEST)
        gc_ref[...] = part * ds[8:9, :]

    return pl.pallas_call(body, name="mod_bwd",
                          out_shape=(jax.ShapeDtypeStruct(wm.shape, f32), jax.ShapeDtypeStruct((1, D), f32)),
                          compiler_params=_cp())(cs, dm, wm)


def _adam_update(g, w_ref, m_ref, v_ref, go_ref, d_ref, mo_ref, vo_ref):
    c1 = 1.0 / (1.0 - ADAM_B1 ** ADAM_STEP)
    c2 = 1.0 / (1.0 - ADAM_B2 ** ADAM_STEP)
    mn = ADAM_B1 * m_ref[...] + (1.0 - ADAM_B1) * g
    vn = ADAM_B2 * v_ref[...] + (1.0 - ADAM_B2) * (g * g)
    go_ref[...] = g
    mo_ref[...] = mn
    vo_ref[...] = vn
    d_ref[...] = -ADAM_LR * ((mn * c1) / (jnp.sqrt(vn * c2) + ADAM_EPS) + ADAM_WD * w_ref[...])


def _adam_w_in(parts_a, parts_b, parts_late, w, m, v):
    na = EARLY_A_ROWS // ROWS_RP

    def body(a_ref, b_ref, l_ref, w_ref, m_ref, v_ref, go_ref, d_ref, mo_ref, vo_ref):
        me = _me()
        first = pl.program_id(0) < na
        early = jnp.where(first, a_ref[0], b_ref[0])
        late = l_ref[0].astype(f32)
        for i in range(1, NDEV):
            early = early + jnp.where(first, a_ref[i], b_ref[i])
            late = late + l_ref[i].astype(f32)
        g = jnp.where(me >= LATE_DESTS - 1, early, 0.0) + jnp.where(me < LATE_DESTS, late, 0.0)
        _adam_update(g, w_ref, m_ref, v_ref, go_ref, d_ref, mo_ref, vo_ref)

    blk = pl.BlockSpec((ROWS_RP, SHARD), lambda i: (i, 0))
    return pl.pallas_call(
        body, name="adam_w_in", grid=(D // ROWS_RP,), out_shape=tuple(jax.ShapeDtypeStruct((D, SHARD), f32) for _ in range(4)),
        in_specs=[pl.BlockSpec((NDEV, ROWS_RP, SHARD), lambda i: (0, jnp.minimum(i, na - 1), 0)),
                  pl.BlockSpec((NDEV, ROWS_RP, SHARD), lambda i: (0, jnp.maximum(i - na, 0), 0)),
                  pl.BlockSpec((NDEV, ROWS_RP, SHARD), lambda i: (0, i, 0)), blk, blk, blk],
        out_specs=(blk, blk, blk, blk), compiler_params=_cp(("arbitrary",)),
    )(parts_a, parts_b, parts_late, w, m, v)


def _adam(parts, w, m, v, rows, name):
    p, r, c = parts.shape

    def body(g_ref, w_ref, m_ref, v_ref, go_ref, d_ref, mo_ref, vo_ref):
        g = g_ref[0]
        for i in range(1, p):
            g = g + g_ref[i]
        _adam_update(g, w_ref, m_ref, v_ref, go_ref, d_ref, mo_ref, vo_ref)

    blk = pl.BlockSpec((rows, c), lambda i: (i, 0))
    return pl.pallas_call(
        body, name=name, grid=(r // rows,), out_shape=tuple(jax.ShapeDtypeStruct((r, c), f32) for _ in range(4)),
        in_specs=[pl.BlockSpec((p, rows, c), lambda i: (0, i, 0)), blk, blk, blk], out_specs=(blk, blk, blk, blk),
        compiler_params=_cp(("arbitrary",)),
    )(parts, w, m, v)


def _in_proj(x, g, scale, shift, w_pad, tl, comm):
    l = x.shape[0]
    c_in, c_out, c_sems = comm.specs()

    def body(*refs):
        x_ref, g_ref, sc_ref, sh_ref, w_ref = refs[:5]
        cin = refs[5:5 + len(c_in)]
        p_ref, h_ref = refs[5 + len(c_in):7 + len(c_in)]
        cout = refs[7 + len(c_in):7 + len(c_in) + comm.n]
        sems = refs[7 + len(c_in) + comm.n:]

        def compute():
            xv = x_ref[...]
            r = lax.rsqrt(jnp.mean(xv * xv, axis=-1, keepdims=True) + EPS)
            h = (xv * r) * (g_ref[...] * (1.0 + sc_ref[...])) + sh_ref[...]
            hb = h.astype(bf16)
            h_ref[...] = hb
            p_ref[...] = jnp.dot(hb, w_ref[...], preferred_element_type=f32).astype(bf16)

        comm.run(cin, cout, sems, l // tl, compute)

    vec = pl.BlockSpec((1, D), lambda i: (0, 0))
    return pl.pallas_call(
        body, name="in_proj", grid=(l // tl,),
        out_shape=(jax.ShapeDtypeStruct((l, PW), bf16), jax.ShapeDtypeStruct((l, D), bf16)) + tuple(comm.outs),
        in_specs=[pl.BlockSpec((tl, D), lambda i: (i, 0)), vec, vec, vec, pl.BlockSpec((D, PW), lambda i: (0, 0))] + c_in,
        out_specs=(pl.BlockSpec((tl, PW), lambda i: (i, 0)), pl.BlockSpec((tl, D), lambda i: (i, 0))) + tuple(c_out),
        scratch_shapes=c_sems, compiler_params=_cp(("arbitrary",)),
    )(x, g, scale, shift, w_pad, *comm.ins)


def _tri(rev):
    i = lax.broadcasted_iota(jnp.int32, (CH, CH), 0)
    j = lax.broadcasted_iota(jnp.int32, (CH, CH), 1)
    return jnp.where((j >= i) if rev else (j <= i), 1.0, 0.0).astype(f32)


def _head_masks():
    lane = lax.broadcasted_iota(jnp.int32, (1, KW), 1) // HK
    return [jnp.where(lane == h, 1.0, 0.0).astype(f32) for h in range(NH)]


def _block_diag():
    r = lax.broadcasted_iota(jnp.int32, (VW, KW), 0) // HV
    c = lax.broadcasted_iota(jnp.int32, (VW, KW), 1) // HK
    return jnp.where(r == c, 1.0, 0.0).astype(f32)


def _decay(lr, w2, gb, tri, rev):
    logits = _dot(lr, w2) + gb
    a = _log_sigmoid(logits) * (1.0 / TAU)
    c = _dot_hi(tri, a)
    cl = c[0:1, :] if rev else c[CH - 1:CH, :]
    return logits, c, cl


def _stack_heads(t, hm):
    return jnp.concatenate([t * hm[h] for h in range(NH)], axis=0)


def _chunk_fwd(q, k, v, c, cl, st, tri, hm, bd):
    tri4 = jnp.concatenate([tri] * NH, axis=0)
    qd = q * jnp.exp(c) * QSCALE
    kd = k * jnp.exp(-c)
    kdec = k * jnp.exp(cl - c)
    pst = _nt(_stack_heads(qd, hm), kd) * tri4
    intra = jnp.concatenate([_dot(pst[h * CH:(h + 1) * CH], v[:, h * HV:(h + 1) * HV]) for h in range(NH)], axis=1)
    o = _nt(qd, st) + intra
    st_new = st * jnp.exp(cl) + bd * _tn(v, kdec)
    return o, st_new


def _state_fwd(k, v, c, cl, st, bd):
    return st * jnp.exp(cl) + bd * _tn(v, k * jnp.exp(cl - c))


def _chunk_bwd(q, k, v, c, cl, st0, dst, do, tri, trit, hm, bd):
    ecl = jnp.exp(cl)
    edec = jnp.exp(cl - c)
    kdec = k * edec
    dv = _nt(kdec, dst)
    dkdec = _dot(v, dst)
    dcl = jnp.sum(dst * st0, axis=0, keepdims=True) * ecl + jnp.sum(dkdec * kdec, axis=0, keepdims=True)
    dc = -dkdec * kdec
    dk = dkdec * edec
    dst0 = dst * ecl
    dq = None
    if q is not None:
        tri4 = jnp.concatenate([tri] * NH, axis=0)
        ec, enc = jnp.exp(c), jnp.exp(-c)
        qd = q * ec * QSCALE
        kd = k * enc
        qs = _stack_heads(qd, hm)
        pst = _nt(qs, kd) * tri4
        dpst = jnp.concatenate([_nt(do[:, h * HV:(h + 1) * HV], v[:, h * HV:(h + 1) * HV]) for h in range(NH)], axis=0) * tri4
        dv = dv + jnp.concatenate([_tn(pst[h * CH:(h + 1) * CH], do[:, h * HV:(h + 1) * HV]) for h in range(NH)], axis=1)
        dqd = _dot(do, st0)
        for h in range(NH):
            dqd = dqd + hm[h] * _dot(dpst[h * CH:(h + 1) * CH], kd)
        dkd = _tn(dpst, qs)
        dst0 = dst0 + bd * _tn(do, qd)
        dc = dc + dqd * qd - dkd * kd
        dq = dqd * ec * QSCALE
        dk = dk + dkd * enc
    da = _dot_hi(trit, dc) + dcl
    return dq, dk, dv, da, dst0


def _bdot(a, b):
    return lax.dot_general(a.astype(bf16), b.astype(bf16), (((2,), (1,)), ((0,), (0,))), preferred_element_type=f32)


def _bnt(a, b):
    return lax.dot_general(a.astype(bf16), b.astype(bf16), (((2,), (2,)), ((0,), (0,))), preferred_element_type=f32)


def _btn(a, b):
    return lax.dot_general(a.astype(bf16), b.astype(bf16), (((1,), (1,)), ((0,), (0,))), preferred_element_type=f32)


def _scan_chunks(x, rev):
    t = x.shape[0]
    pos = lax.broadcasted_iota(jnp.int32, x.shape, 0) % CH
    s = 1
    while s < CH:
        if rev:
            x = x + jnp.where(pos < CH - s, pltpu.roll(x, t - s, 0), 0.0)
        else:
            x = x + jnp.where(pos >= s, pltpu.roll(x, s, 0), 0.0)
        s *= 2
    return x


class _Tile:
    pass


def _tile_prep(q_ref, k_ref, lr_ref, w2_ref, gb_ref, rev, nc):
    t = _Tile()
    tg = nc * CH
    t.logits = _dot(lr_ref[...], w2_ref[...]) + gb_ref[...]
    c = _scan_chunks(_log_sigmoid(t.logits) * (1.0 / TAU), rev).reshape(nc, CH, KW)
    cl = c[:, 0:1, :] if rev else c[:, CH - 1:CH, :]
    k = k_ref[...].astype(f32).reshape(nc, CH, KW)
    t.ec, t.enc, t.edec, t.ecl = jnp.exp(c), jnp.exp(-c), jnp.exp(cl - c), jnp.exp(cl)
    t.qd = q_ref[...].astype(f32).reshape(nc, CH, KW) * t.ec * QSCALE
    t.kd = k * t.enc
    t.kdec = k * t.edec
    hm = _head_masks()
    t.tri4 = jnp.concatenate([_tri(rev)] * NH, axis=0)[None]
    t.qs = jnp.concatenate([t.qd * hm[h] for h in range(NH)], axis=1)
    t.pst = _bnt(t.qs, t.kd) * t.tri4
    return t


def _gla_fwd(p, w2p, gb, s0, rev, tg, name):
    l = p.shape[0]
    nb, nc = l // tg, tg // CH

    def body(q_ref, k_ref, v_ref, lr_ref, w2_ref, gb_ref, s0_ref, o_ref, st_ref, st):
        @pl.when(pl.program_id(0) == 0)
        def _():
            st[...] = s0_ref[...]

        t = _tile_prep(q_ref, k_ref, lr_ref, w2_ref, gb_ref, rev, nc)
        v = v_ref[...].reshape(nc, CH, VW)
        intra = jnp.concatenate([_bdot(t.pst[:, h * CH:(h + 1) * CH], v[:, :, h * HV:(h + 1) * HV]) for h in range(NH)], axis=2)
        kv = _btn(v, t.kdec) * _block_diag()[None]
        s = st[...]
        for n in (range(nc - 1, -1, -1) if rev else range(nc)):
            st_ref[n] = s
            s = s * t.ecl[n] + kv[n]
        st[...] = s
        o_ref[...] = (_bnt(t.qd, st_ref[...]) + intra).reshape(tg, VW)

    blk = (lambda i: nb - 1 - i) if rev else (lambda i: i)
    return pl.pallas_call(
        body, name=name, grid=(nb,),
        out_shape=(jax.ShapeDtypeStruct((l, VW), f32), jax.ShapeDtypeStruct((l // CH, VW, KW), f32)),
        in_specs=[pl.BlockSpec((tg, KW), lambda i: (blk(i), PQ // KW)), pl.BlockSpec((tg, KW), lambda i: (blk(i), PK // KW)),
                  pl.BlockSpec((tg, VW), lambda i: (blk(i), PV // VW)), pl.BlockSpec((tg, LRW), lambda i: (blk(i), PLR // LRW)),
                  pl.BlockSpec((LRW, KW), lambda i: (0, 0)), pl.BlockSpec((1, KW), lambda i: (0, 0)),
                  pl.BlockSpec((VW, KW), lambda i: (0, 0))],
        out_specs=(pl.BlockSpec((tg, VW), lambda i: (blk(i), 0)), pl.BlockSpec((nc, VW, KW), lambda i: (blk(i), 0, 0))),
        scratch_shapes=[pltpu.VMEM((VW, KW), f32)],
        compiler_params=_cp(("arbitrary",)),
    )(p, p, p, p, w2p, gb, s0)


def _gla_bwd(p, do, states, w2p, gb, prev, rev, tg, name, comm):
    l = p.shape[0]
    nb, nc = l // tg, tg // CH
    out_dt = f32 if prev is None else bf16
    c_in, c_out, c_sems = comm.specs()

    def body(*refs):
        q_ref, k_ref, v_ref, lr_ref, do_ref, st_ref, w2_ref, gb_ref = refs[:8]
        refs = refs[8:]
        if prev is not None:
            pq_ref, pl_ref = refs[:2]
            refs = refs[2:]
        cin, refs = refs[:len(c_in)], refs[len(c_in):]
        dqkv_ref, dlr_ref, dw2_ref, dgb_ref, ds0_ref = refs[:5]
        cout, dst, ds_buf, sems = refs[5:5 + comm.n], refs[5 + comm.n], refs[6 + comm.n], refs[7 + comm.n:]
        comm.run(cin, cout, sems, nb, lambda: compute(q_ref, k_ref, v_ref, lr_ref, do_ref, st_ref, w2_ref, gb_ref,
                                                      pq_ref if prev is not None else None, pl_ref if prev is not None else None,
                                                      dqkv_ref, dlr_ref, dw2_ref, dgb_ref, ds0_ref, dst, ds_buf))

    def compute(q_ref, k_ref, v_ref, lr_ref, do_ref, st_ref, w2_ref, gb_ref, pq_ref, pl_ref,
                dqkv_ref, dlr_ref, dw2_ref, dgb_ref, ds0_ref, dst, ds_buf):
        @pl.when(pl.program_id(0) == 0)
        def _():
            dst[...] = jnp.zeros_like(dst)
            dw2_ref[...] = jnp.zeros_like(dw2_ref)
            dgb_ref[...] = jnp.zeros_like(dgb_ref)

        t = _tile_prep(q_ref, k_ref, lr_ref, w2_ref, gb_ref, rev, nc)
        hm = _head_masks()
        v = v_ref[...].reshape(nc, CH, VW)
        do = do_ref[...].reshape(nc, CH, VW)
        heads = lambda a, h: a[:, :, h * HV:(h + 1) * HV]
        dpst = jnp.concatenate([_bnt(heads(do, h), heads(v, h)) for h in range(NH)], axis=1) * t.tri4
        dv = jnp.concatenate([_btn(t.pst[:, h * CH:(h + 1) * CH], heads(do, h)) for h in range(NH)], axis=2)
        dqd = _bdot(do, st_ref[...])
        for h in range(NH):
            dqd = dqd + hm[h] * _bdot(dpst[:, h * CH:(h + 1) * CH], t.kd)
        dkd = _btn(dpst, t.qs)
        u = _btn(do, t.qd) * _block_diag()[None]
        d = dst[...]
        for n in (range(nc) if rev else range(nc - 1, -1, -1)):
            ds_buf[n] = d
            d = d * t.ecl[n] + u[n]
        dst[...] = d
        ds0_ref[...] = d
        ds = ds_buf[...]
        dv = dv + _bnt(t.kdec, ds)
        dkdec = _bdot(v, ds)
        dcl = jnp.sum(ds * st_ref[...], axis=1, keepdims=True) * t.ecl + jnp.sum(dkdec * t.kdec, axis=1, keepdims=True)
        dc = dqd * t.qd - dkd * t.kd - dkdec * t.kdec
        da = _scan_chunks(dc.reshape(tg, KW), not rev).reshape(nc, CH, KW) + dcl
        dq = dqd * t.ec * QSCALE
        dk = dkd * t.enc + dkdec * t.edec
        dlog = da.reshape(tg, KW) * _sigmoid(-t.logits) * (1.0 / TAU)
        dlr = _nt(dlog, w2_ref[...])
        dw2_ref[...] += _tn(lr_ref[...], dlog)
        dgb_ref[...] += jnp.sum(dlog, axis=0, keepdims=True)
        dqkv = jnp.concatenate([dq, dk, dv], axis=2).reshape(tg, 2 * KW + VW)
        if prev is not None:
            dqkv = dqkv + pq_ref[...]
            dlr = dlr + pl_ref[...]
        dqkv_ref[...] = dqkv.astype(out_dt)
        dlr_ref[...] = dlr.astype(out_dt)

    blk = (lambda i: i) if rev else (lambda i: nb - 1 - i)
    in_specs = [pl.BlockSpec((tg, KW), lambda i: (blk(i), PQ // KW)), pl.BlockSpec((tg, KW), lambda i: (blk(i), PK // KW)),
                pl.BlockSpec((tg, VW), lambda i: (blk(i), PV // VW)), pl.BlockSpec((tg, LRW), lambda i: (blk(i), PLR // LRW)),
                pl.BlockSpec((tg, VW), lambda i: (blk(i), 0)), pl.BlockSpec((nc, VW, KW), lambda i: (blk(i), 0, 0)),
                pl.BlockSpec((LRW, KW), lambda i: (0, 0)), pl.BlockSpec((1, KW), lambda i: (0, 0))]
    args = [p, p, p, p, do, states, w2p, gb]
    if prev is not None:
        in_specs += [pl.BlockSpec((tg, 2 * KW + VW), lambda i: (blk(i), 0)), pl.BlockSpec((tg, LRW), lambda i: (blk(i), 0))]
        args += list(prev)
    return pl.pallas_call(
        body, name=name, grid=(nb,),
        out_shape=(jax.ShapeDtypeStruct((l, 2 * KW + VW), out_dt), jax.ShapeDtypeStruct((l, LRW), out_dt),
                   jax.ShapeDtypeStruct((LRW, KW), f32), jax.ShapeDtypeStruct((1, KW), f32), jax.ShapeDtypeStruct((VW, KW), f32))
        + tuple(comm.outs),
        in_specs=in_specs + c_in,
        out_specs=(pl.BlockSpec((tg, 2 * KW + VW), lambda i: (blk(i), 0)), pl.BlockSpec((tg, LRW), lambda i: (blk(i), 0)),
                   pl.BlockSpec((LRW, KW), lambda i: (0, 0)), pl.BlockSpec((1, KW), lambda i: (0, 0)),
                   pl.BlockSpec((VW, KW), lambda i: (0, 0))) + tuple(c_out),
        scratch_shapes=[pltpu.VMEM((VW, KW), f32), pltpu.VMEM((nc, VW, KW), f32)] + c_sems,
        compiler_params=_cp(("arbitrary",)),
    )(*args, *comm.ins)


def _ctx_hidden(ctx_ref, g_ref, sc_ref, sh_ref):
    xv = ctx_ref[...]
    r = lax.rsqrt(jnp.mean(xv * xv, axis=-1, keepdims=True) + EPS)
    xn = xv * r
    return xn, xn * (g_ref[...] * (1.0 + sc_ref[...])) + sh_ref[...]


_CTX_W_SPECS = [pl.BlockSpec((D, KW), lambda i: (0, PK // KW)), pl.BlockSpec((D, VW), lambda i: (0, PV // VW)),
                pl.BlockSpec((D, LRW), lambda i: (0, PLR // LRW))]


def _ctx_fwd(ctx, g, scale, shift, w_pad, w2f, w2b, gbf, gbb):
    ncc = CTX // CH

    def body(ctx_ref, g_ref, sc_ref, sh_ref, wk_ref, wv_ref, wl_ref, w2f_ref, w2b_ref, gbf_ref, gbb_ref, sf_ref, sb_ref):
        _, hc = _ctx_hidden(ctx_ref, g_ref, sc_ref, sh_ref)
        k, v, lr = _dot(hc, wk_ref[...]), _dot(hc, wv_ref[...]), _dot(hc, wl_ref[...])
        bd = _block_diag()
        for rev, w2_ref, gb_ref, out in ((False, w2f_ref, gbf_ref, sf_ref), (True, w2b_ref, gbb_ref, sb_ref)):
            tri = _tri(rev)
            st = jnp.zeros((VW, KW), f32)
            for j in (range(ncc - 1, -1, -1) if rev else range(ncc)):
                rows = slice(j * CH, (j + 1) * CH)
                _, c, cl = _decay(lr[rows], w2_ref[...], gb_ref[...], tri, rev)
                st = _state_fwd(k[rows], v[rows], c, cl, st, bd)
            out[...] = st

    vec = pl.BlockSpec((1, D), lambda i: (0, 0))
    w2s = pl.BlockSpec((LRW, KW), lambda i: (0, 0))
    gbs = pl.BlockSpec((1, KW), lambda i: (0, 0))
    sts = pl.BlockSpec((VW, KW), lambda i: (0, 0))
    return pl.pallas_call(
        body, name="ctx_fwd", grid=(1,), out_shape=(jax.ShapeDtypeStruct((VW, KW), f32),) * 2,
        in_specs=[pl.BlockSpec((CTX, D), lambda i: (0, 0)), vec, vec, vec] + _CTX_W_SPECS + [w2s, w2s, gbs, gbs],
        out_specs=(sts, sts), compiler_params=_cp(("arbitrary",)),
    )(ctx, g, scale, shift, w_pad, w_pad, w_pad, w2f, w2b, gbf, gbb)


def _ctx_bwd(ctx, g, scale, shift, w_pad, w2f, w2b, gbf, gbb, dsf, dsb):
    ncc = CTX // CH

    def body(ctx_ref, g_ref, sc_ref, sh_ref, wk_ref, wv_ref, wl_ref, w2f_ref, w2b_ref, gbf_ref, gbb_ref, dsf_ref, dsb_ref,
             dwk_ref, dwv_ref, dwl_ref, dmod_ref, dg_ref, dw2_ref, dgb_ref):
        xn, hc = _ctx_hidden(ctx_ref, g_ref, sc_ref, sh_ref)
        k, v, lr = _dot(hc, wk_ref[...]), _dot(hc, wv_ref[...]), _dot(hc, wl_ref[...])
        bd = _block_diag()
        dk_rows, dv_rows, dl_rows = [None] * ncc, [None] * ncc, [None] * ncc
        for d, (rev, w2_ref, gb_ref, ds_ref) in enumerate(((False, w2f_ref, gbf_ref, dsf_ref), (True, w2b_ref, gbb_ref, dsb_ref))):
            tri = _tri(rev)
            order = list(range(ncc - 1, -1, -1) if rev else range(ncc))
            st, saved = jnp.zeros((VW, KW), f32), {}
            for j in order:
                rows = slice(j * CH, (j + 1) * CH)
                logits, c, cl = _decay(lr[rows], w2_ref[...], gb_ref[...], tri, rev)
                saved[j] = (logits, c, cl, st)
                st = _state_fwd(k[rows], v[rows], c, cl, st, bd)
            dst = ds_ref[...]
            dw2 = jnp.zeros((LRW, KW), f32)
            dgb = jnp.zeros((1, KW), f32)
            for j in reversed(order):
                rows = slice(j * CH, (j + 1) * CH)
                logits, c, cl, st0 = saved[j]
                _, dk, dv, da, dst = _chunk_bwd(None, k[rows], v[rows], c, cl, st0, dst, None, tri, _tri(not rev), None, bd)
                dlog = da * _sigmoid(-logits) * (1.0 / TAU)
                dl = _nt(dlog, w2_ref[...])
                dw2 = dw2 + _tn(lr[rows], dlog)
                dgb = dgb + jnp.sum(dlog, axis=0, keepdims=True)
                dk_rows[j] = dk if dk_rows[j] is None else dk_rows[j] + dk
                dv_rows[j] = dv if dv_rows[j] is None else dv_rows[j] + dv
                dl_rows[j] = dl if dl_rows[j] is None else dl_rows[j] + dl
            dw2_ref[d] = dw2
            dgb_ref[d] = dgb
        dk, dv, dl = (jnp.concatenate(t, axis=0) for t in (dk_rows, dv_rows, dl_rows))
        dwk_ref[...] = _tn(hc, dk)
        dwv_ref[...] = _tn(hc, dv)
        dwl_ref[...] = _tn(hc, dl)
        dh = _nt(dk, wk_ref[...]) + _nt(dv, wv_ref[...]) + _nt(dl, wl_ref[...])
        gx = dh * xn
        dmod_ref[:, 0:D] = jnp.sum(dh, axis=0, keepdims=True)
        dmod_ref[:, D:2 * D] = jnp.sum(gx, axis=0, keepdims=True) * g_ref[...]
        dg_ref[...] = jnp.sum(gx, axis=0, keepdims=True) * (1.0 + sc_ref[...])

    vec = pl.BlockSpec((1, D), lambda i: (0, 0))
    w2s = pl.BlockSpec((LRW, KW), lambda i: (0, 0))
    gbs = pl.BlockSpec((1, KW), lambda i: (0, 0))
    sts = pl.BlockSpec((VW, KW), lambda i: (0, 0))
    full = lambda *s: pl.BlockSpec(s, lambda i: (0,) * len(s))
    return pl.pallas_call(
        body, name="ctx_bwd", grid=(1,),
        out_shape=(jax.ShapeDtypeStruct((D, KW), f32), jax.ShapeDtypeStruct((D, VW), f32), jax.ShapeDtypeStruct((D, LRW), f32),
                   jax.ShapeDtypeStruct((1, 2 * D), f32), jax.ShapeDtypeStruct((1, D), f32),
                   jax.ShapeDtypeStruct((2, LRW, KW), f32), jax.ShapeDtypeStruct((2, 1, KW), f32)),
        in_specs=[pl.BlockSpec((CTX, D), lambda i: (0, 0)), vec, vec, vec] + _CTX_W_SPECS + [w2s, w2s, gbs, gbs, sts, sts],
        out_specs=(full(D, KW), full(D, VW), full(D, LRW), full(1, 2 * D), full(1, D), full(2, LRW, KW), full(2, 1, KW)),
        compiler_params=_cp(("arbitrary",)),
    )(ctx, g, scale, shift, w_pad, w_pad, w_pad, w2f, w2b, gbf, gbb, dsf, dsb)


def _layernorm(va, g, b):
    mu = jnp.mean(va, axis=-1, keepdims=True)
    xc = va - mu
    rstd = lax.rsqrt(jnp.mean(xc * xc, axis=-1, keepdims=True) + EPS)
    vhat = xc * rstd
    return vhat, rstd, vhat * g + b


def _mix_fwd(p, ln_g, ln_b, ws, bs_t):
    l = p.shape[0]
    nch = l // ACH
    half = AW // 2

    def body(p_ref, g_ref, b_ref, ws_ref, bs_ref, sv_ref, va_buf, col_buf, sem):
        cp = pltpu.make_async_copy(p_ref.at[:, pl.ds(PVA, AW)], va_buf, sem)
        cp.start()
        cp.wait()

        def rows_step(n, carry):
            rows = pl.ds(pl.multiple_of(n * ACH, ACH), ACH)
            _, _, vn = _layernorm(va_buf[rows, :].astype(f32), g_ref[...], b_ref[...])
            for gi in range(2):
                sl = slice(gi * ACH, (gi + 1) * ACH)
                sv_ref[rows, sl] = (_dot(ws_ref[gi], vn[:, sl]) + bs_ref[:, gi:gi + 1]).astype(bf16)
            col_buf[0, rows, :] = vn[:, half:half + ACH]
            col_buf[1, rows, :] = vn[:, half + ACH:]
            return carry

        lax.fori_loop(0, nch, rows_step, 0)

        def cols_step(cidx, carry):
            rows = pl.ds(cidx, ACH, stride=GW)
            for gi in range(2, 4):
                col_buf[gi - 2, rows, :] = _dot(ws_ref[gi], col_buf[gi - 2, rows, :]) + bs_ref[:, gi:gi + 1]
            return carry

        lax.fori_loop(0, GW, cols_step, 0)

        def out_step(n, carry):
            rows = pl.ds(pl.multiple_of(n * ACH, ACH), ACH)
            sv_ref[rows, half:half + ACH] = col_buf[0, rows, :].astype(bf16)
            sv_ref[rows, half + ACH:] = col_buf[1, rows, :].astype(bf16)
            return carry

        lax.fori_loop(0, nch, out_step, 0)

    vm = pl.BlockSpec(memory_space=pltpu.VMEM)
    return pl.pallas_call(
        body, name="mix_fwd", out_shape=jax.ShapeDtypeStruct((l, AW), bf16),
        in_specs=[pl.BlockSpec(memory_space=pl.ANY), vm, vm, vm, vm], out_specs=vm,
        scratch_shapes=[pltpu.VMEM((l, AW), bf16), pltpu.VMEM((2, l, ACH), f32), pltpu.SemaphoreType.DMA(())],
        compiler_params=_cp(),
    )(p, ln_g, ln_b, ws, bs_t)


def _mix_bwd(p, dsv, ln_g, ln_b, ws_t):
    l = p.shape[0]
    nch = l // ACH
    half = AW // 2

    def body(p_ref, dsv_hbm, g_ref, b_ref, wst_ref, dva_ref, dws_ref, dbs_ref, dg_ref, db_ref, va_buf, dsv_buf, vn_col, ds_col, sems):
        cp1 = pltpu.make_async_copy(p_ref.at[:, pl.ds(PVA, AW)], va_buf, sems.at[0])
        cp2 = pltpu.make_async_copy(dsv_hbm, dsv_buf, sems.at[1])
        cp1.start()
        cp2.start()
        dws_ref[...] = jnp.zeros_like(dws_ref)
        dbs_ref[...] = jnp.zeros_like(dbs_ref)
        dg_ref[...] = jnp.zeros_like(dg_ref)
        db_ref[...] = jnp.zeros_like(db_ref)
        cp1.wait()
        cp2.wait()

        def rows_step(n, carry):
            rows = pl.ds(pl.multiple_of(n * ACH, ACH), ACH)
            _, _, vn = _layernorm(va_buf[rows, :].astype(f32), g_ref[...], b_ref[...])
            ds = dsv_buf[rows, :].astype(f32)
            for gi in range(2):
                sl = slice(gi * ACH, (gi + 1) * ACH)
                dws_ref[gi] += _nt(ds[:, sl], vn[:, sl])
                dbs_ref[gi] += ds[:, sl]
            for gi in range(2):
                sl = slice(half + gi * ACH, half + (gi + 1) * ACH)
                vn_col[gi, rows, :] = vn[:, sl]
                ds_col[gi, rows, :] = ds[:, sl]
            return carry

        lax.fori_loop(0, nch, rows_step, 0)

        def cols_step(cidx, carry):
            rows = pl.ds(cidx, ACH, stride=GW)
            for gi in range(2, 4):
                ds = ds_col[gi - 2, rows, :]
                dws_ref[gi] += _nt(ds, vn_col[gi - 2, rows, :])
                dbs_ref[gi] += ds
                ds_col[gi - 2, rows, :] = _dot(wst_ref[gi], ds)
            return carry

        lax.fori_loop(0, GW, cols_step, 0)

        def out_step(n, carry):
            rows = pl.ds(pl.multiple_of(n * ACH, ACH), ACH)
            vhat, rstd, _ = _layernorm(va_buf[rows, :].astype(f32), g_ref[...], b_ref[...])
            ds = dsv_buf[rows, :].astype(f32)
            dvn = jnp.concatenate([_dot(wst_ref[0], ds[:, 0:ACH]), _dot(wst_ref[1], ds[:, ACH:half]), ds_col[0, rows, :], ds_col[1, rows, :]], axis=1)
            dg_ref[...] += jnp.sum(dvn * vhat, axis=0, keepdims=True)
            db_ref[...] += jnp.sum(dvn, axis=0, keepdims=True)
            dvh = dvn * g_ref[...]
            dva = rstd * (dvh - jnp.mean(dvh, axis=-1, keepdims=True) - vhat * jnp.mean(dvh * vhat, axis=-1, keepdims=True))
            dva_ref[rows, :] = dva.astype(bf16)
            return carry

        lax.fori_loop(0, nch, out_step, 0)

    vm = pl.BlockSpec(memory_space=pltpu.VMEM)
    hbm = pl.BlockSpec(memory_space=pl.ANY)
    return pl.pallas_call(
        body, name="mix_bwd",
        out_shape=(jax.ShapeDtypeStruct((l, AW), bf16), jax.ShapeDtypeStruct((4, ACH, ACH), f32), jax.ShapeDtypeStruct((4, ACH, ACH), f32),
                   jax.ShapeDtypeStruct((1, AW), f32), jax.ShapeDtypeStruct((1, AW), f32)),
        in_specs=[hbm, hbm, vm, vm, vm], out_specs=(vm, vm, vm, vm, vm),
        scratch_shapes=[pltpu.VMEM((l, AW), bf16), pltpu.VMEM((l, AW), bf16), pltpu.VMEM((2, l, ACH), f32), pltpu.VMEM((2, l, ACH), f32),
                        pltpu.SemaphoreType.DMA((2,))],
        compiler_params=_cp(),
    )(p, dsv, ln_g, ln_b, ws_t)


def _mid(x, tgt, p, o_f, o_b, sv, gate, gf, gb_norm, w_pa, w_pb, w_out, tl):
    l = x.shape[0]

    def body(x_ref, t_ref, zb_ref, ua_ref, za_ref, g1_ref, g2_ref, of_ref, ob_ref, sv_ref, gate_ref, gf_ref, gbn_ref,
             wpa_ref, wpb_ref, wout_ref,
             dx1_ref, dzbua_ref, dzag_ref, dsv_ref, do_ref, dwout_ref, dwpa_ref, dwpb_ref, dgf_ref, dgate_ref, dgbn_ref, loss_ref):
        @pl.when(pl.program_id(0) == 0)
        def _():
            for r in (dwout_ref, dwpa_ref, dwpb_ref, dgf_ref, dgate_ref, dgbn_ref, loss_ref):
                r[...] = jnp.zeros_like(r)

        o = of_ref[...] + ob_ref[...]
        rr = jnp.concatenate(
            [jnp.broadcast_to(lax.rsqrt(jnp.mean(o[:, h * HV:(h + 1) * HV] ** 2, axis=-1, keepdims=True) + EPS), (tl, HV))
             for h in range(NH)], axis=1)
        ohat = o * rr
        on = ohat * gbn_ref[...]
        szb, dszb = _silu_and_grad(zb_ref[...].astype(f32))
        tb = on * szb
        u = ua_ref[...].astype(f32)
        svv = sv_ref[...].astype(f32)
        sza, dsza = _silu_and_grad(za_ref[...].astype(f32))
        ta = u * svv * sza
        ya = _dot(ta, wpa_ref[...])
        yb = _dot(tb, wpb_ref[...])
        g1 = _sigmoid(g1_ref[...].astype(f32))
        g2 = _sigmoid(g2_ref[...].astype(f32))
        m = g1 * ya + g2 * yb
        y2 = _dot(m, wout_ref[...])
        x1 = x_ref[...] + gate_ref[...] * y2
        r1 = lax.rsqrt(jnp.mean(x1 * x1, axis=-1, keepdims=True) + EPS)
        x1n = x1 * r1
        err = x1n * gf_ref[...] - t_ref[...]
        loss_ref[...] += jnp.sum(jnp.sum(err * err, axis=-1, keepdims=True), axis=0, keepdims=True) * (0.5 / D)
        dout = err * (1.0 / D)
        dgf_ref[...] += jnp.sum(dout * x1n, axis=0, keepdims=True)
        dx1n = dout * gf_ref[...]
        dx1 = r1 * (dx1n - x1n * jnp.mean(dx1n * x1n, axis=-1, keepdims=True))
        dx1_ref[...] = dx1
        dgate_ref[...] += jnp.sum(dx1 * y2, axis=0, keepdims=True)
        dy2 = dx1 * gate_ref[...]
        dwout_ref[...] += _tn(m, dy2)
        dm = _nt(dy2, wout_ref[...])
        dya = dm * g1
        dyb = dm * g2
        dzag_ref[:, AW:AW + D] = (dm * ya * g1 * (1.0 - g1)).astype(bf16)
        dzag_ref[:, AW + D:] = (dm * yb * g2 * (1.0 - g2)).astype(bf16)
        dwpa_ref[...] += _tn(ta, dya)
        dta = _nt(dya, wpa_ref[...])
        dzbua_ref[:, AW:] = (dta * svv * sza).astype(bf16)
        dsv_ref[...] = (dta * u * sza).astype(bf16)
        dzag_ref[:, 0:AW] = (dta * u * svv * dsza).astype(bf16)
        dwpb_ref[...] += _tn(tb, dyb)
        dtb = _nt(dyb, wpb_ref[...])
        don = dtb * szb
        dzbua_ref[:, 0:AW] = (dtb * on * dszb).astype(bf16)
        dgbn_ref[...] += jnp.sum(don * ohat, axis=0, keepdims=True)
        doh = don * gbn_ref[...]
        prod = doh * ohat
        mh = jnp.concatenate(
            [jnp.broadcast_to(jnp.mean(prod[:, h * HV:(h + 1) * HV], axis=-1, keepdims=True), (tl, HV)) for h in range(NH)], axis=1)
        do_ref[...] = (rr * (doh - ohat * mh)).astype(bf16)

    row = lambda w, j: pl.BlockSpec((tl, w), lambda i, j=j: (i, j))
    full = lambda *s: pl.BlockSpec(s, lambda i: (0,) * len(s))
    return pl.pallas_call(
        body, name="mid", grid=(l // tl,),
        out_shape=(jax.ShapeDtypeStruct((l, D), f32), jax.ShapeDtypeStruct((l, 2 * AW), bf16), jax.ShapeDtypeStruct((l, AW + 2 * D), bf16),
                   jax.ShapeDtypeStruct((l, AW), bf16), jax.ShapeDtypeStruct((l, VW), bf16),
                   jax.ShapeDtypeStruct((D, D), f32), jax.ShapeDtypeStruct((AW, D), f32), jax.ShapeDtypeStruct((VW, D), f32),
                   jax.ShapeDtypeStruct((1, D), f32), jax.ShapeDtypeStruct((1, D), f32), jax.ShapeDtypeStruct((1, VW), f32),
                   jax.ShapeDtypeStruct((1, 1), f32)),
        in_specs=[row(D, 0), row(D, 0), row(AW, PZB // AW), row(AW, PUA // AW), row(AW, PZA // AW), row(D, PG1 // D), row(D, PG2 // D),
                  row(VW, 0), row(VW, 0), row(AW, 0), full(1, D), full(1, D), full(1, VW), full(AW, D), full(VW, D), full(D, D)],
        out_specs=(row(D, 0), row(2 * AW, 0), row(AW + 2 * D, 0), row(AW, 0), row(VW, 0),
                   full(D, D), full(AW, D), full(VW, D), full(1, D), full(1, D), full(1, VW), full(1, 1)),
        compiler_params=_cp(("arbitrary",)),
    )(x, tgt, p, p, p, p, p, o_f, o_b, sv, gate, gf, gb_norm, w_pa, w_pb, w_out)


def _in_bwd(x, dx1, dqkv, dzbua, dva, dzag, dlr, w_pad, g, scale, tl, comm):
    l = x.shape[0]
    c_in, c_out, c_sems = comm.specs()

    def body(*refs):
        cin = refs[14:14 + len(c_in)]
        outs = refs[14 + len(c_in):]
        comm.run(cin, outs[4:4 + comm.n], outs[4 + comm.n:], l // tl, lambda: compute(*refs[:14], *outs[:4]))

    def compute(x_ref, dx1_ref, a_ref, b_ref, c_ref, e_ref, lr_ref, wa_ref, wb_ref, wc_ref, we_ref, wl_ref, g_ref, sc_ref,
                gx_ref, dsh_ref, dsc_ref, dg_ref):
        @pl.when(pl.program_id(0) == 0)
        def _():
            for r in (dsh_ref, dsc_ref, dg_ref):
                r[...] = jnp.zeros_like(r)

        dh = (_nt(a_ref[...], wa_ref[...]) + _nt(b_ref[...], wb_ref[...]) + _nt(c_ref[...], wc_ref[...]) + _nt(e_ref[...], we_ref[...])
              + _nt(lr_ref[...], wl_ref[...]))
        xv = x_ref[...]
        r = lax.rsqrt(jnp.mean(xv * xv, axis=-1, keepdims=True) + EPS)
        xn = xv * r
        gxn = jnp.sum(dh * xn, axis=0, keepdims=True)
        dsh_ref[...] += jnp.sum(dh, axis=0, keepdims=True)
        dsc_ref[...] += gxn * g_ref[...]
        dg_ref[...] += gxn * (1.0 + sc_ref[...])
        dxn = dh * (g_ref[...] * (1.0 + sc_ref[...]))
        gx_ref[...] = dx1_ref[...] + r * (dxn - xn * jnp.mean(dxn * xn, axis=-1, keepdims=True))

    row = lambda w: pl.BlockSpec((tl, w), lambda i: (i, 0))
    wcol = lambda w, j: pl.BlockSpec((D, w), lambda i, j=j: (0, j))
    vec = pl.BlockSpec((1, D), lambda i: (0, 0))
    return pl.pallas_call(
        body, name="in_bwd", grid=(l // tl,),
        out_shape=(jax.ShapeDtypeStruct((l, D), f32),) + (jax.ShapeDtypeStruct((1, D), f32),) * 3 + tuple(comm.outs),
        in_specs=[row(D), row(D), row(2 * KW + VW), row(2 * AW), row(AW), row(AW + 2 * D), row(LRW),
                  wcol(2 * KW + VW, 0), wcol(2 * AW, PZB // (2 * AW)), wcol(AW, PVA // AW), wcol(AW + 2 * D, PZA // (AW + 2 * D)),
                  wcol(LRW, PLR // LRW), vec, vec] + c_in,
        out_specs=(row(D), vec, vec, vec) + tuple(c_out), scratch_shapes=c_sems,
        compiler_params=_cp(("arbitrary",)),
    )(x, dx1, dqkv, dzbua, dva, dzag, dlr, w_pad, w_pad, w_pad, w_pad, w_pad, g, scale, *comm.ins)


def _tn_matmul(a, b, tl, name):
    l, m = a.shape
    n = b.shape[1]

    def body(a_ref, b_ref, o_ref):
        @pl.when(pl.program_id(0) == 0)
        def _():
            o_ref[...] = jnp.zeros_like(o_ref)

        o_ref[...] += _tn(a_ref[...], b_ref[...])

    return pl.pallas_call(
        body, name=name, grid=(l // tl,), out_shape=jax.ShapeDtypeStruct((m, n), f32),
        in_specs=[pl.BlockSpec((tl, m), lambda i: (i, 0)), pl.BlockSpec((tl, n), lambda i: (i, 0))],
        out_specs=pl.BlockSpec((m, n), lambda i: (0, 0)), compiler_params=_cp(("arbitrary",)),
    )(a, b)


def _pad_gate(w2, gb):
    z = jnp.zeros((RANK, KW), f32)
    tail = jnp.zeros((LRW - 2 * RANK, KW), f32)
    w2f = jnp.concatenate([w2[0], z, tail], axis=0)
    w2b = jnp.concatenate([z, w2[1], tail], axis=0)
    return w2f, w2b, gb[0:1], gb[1:2]


EARLY_A_ROWS = 256


class _NoExchange:
    def __init__(self, w_pa, w_pb, w_out):
        self.weights = (w_pa, w_pb, w_out)

    def gather_proj(self):
        return _Comm([], [], None)

    def proj_weights(self, got):
        return self.weights

    def early_a(self, dw_out, dw_pa, dw_pb, small, blocks):
        return _Comm([], [], None)

    def early_b(self, blocks):
        return _Comm([], [], None)

    def late(self, blocks, dgt):
        return _Comm([], [], None)


class _Exchanges:
    def __init__(self, pa, pb, wo):
        self.shards = (pa, pb, wo)

    def gather_proj(self):
        def plan(i, o):
            srcs = [lambda j, r=r: r for r in i]
            dsts = [lambda j: o[0].at[:, _lanes(j)], lambda j: o[1].at[:, _lanes(j)], lambda j: o[2].at[j]]
            return srcs, dsts, None
        sds = jax.ShapeDtypeStruct
        return _Comm(self.shards, [sds((AW, D), bf16), sds((VW, D), bf16), sds((NDEV, 128, D), bf16)], plan)

    def proj_weights(self, got):
        return got[0], got[1], got[2].reshape(D, D)

    def early_a(self, dw_out, dw_pa, dw_pb, small, blocks):
        def plan(i, o):
            srcs = [lambda j: i[0].at[j], lambda j: i[1].at[:, _lanes(j)], lambda j: i[2].at[:, _lanes(j)], lambda j: i[3],
                    lambda j: i[4].at[j]]
            dsts = [lambda j, r=r: r.at[j] for r in o]
            return srcs, dsts, [None, None, None, None, lambda j: j >= LATE_DESTS - 1]
        sds = jax.ShapeDtypeStruct
        return _Comm([dw_out.reshape(NDEV, 128, D), dw_pa, dw_pb, small, blocks],
                     [sds((NDEV, 128, D), f32), sds((NDEV, AW, 128), f32), sds((NDEV, VW, 128), f32),
                      sds((NDEV,) + small.shape, f32), sds(blocks.shape, f32)], plan)

    def early_b(self, blocks):
        def plan(i, o):
            return [lambda j: i[0].at[j]], [lambda j: o[0].at[j]], [lambda j: j >= LATE_DESTS - 1]
        return _Comm([blocks], [jax.ShapeDtypeStruct(blocks.shape, f32)], plan)

    def late(self, blocks, dgt):
        def plan(i, o):
            srcs = [lambda j: i[0].at[jnp.minimum(j, LATE_DESTS - 1)], lambda j: i[1].at[j]]
            dsts = [lambda j, r=r: r.at[j] for r in o]
            return srcs, dsts, [lambda j: j < LATE_DESTS, None]
        sds = jax.ShapeDtypeStruct
        return _Comm([blocks, dgt], [sds((NDEV, D, SHARD), bf16), sds(dgt.shape, f32)], plan)


def _local_step(x, ctx, tgt, mod, modc, norm_g, w_pad, ln_g, ln_b, ws, bs, w2, gb, gb_norm, gf, xch):
    shift, scale, gate = mod[:, 0:D], mod[:, D:2 * D], mod[:, 2 * D:]
    shift_c, scale_c = modc[:, 0:D], modc[:, D:]
    w2f, w2b, gbf, gbb = _pad_gate(w2, gb)

    p, h, *got_proj = _in_proj(x, norm_g, scale, shift, w_pad, 256, xch.gather_proj())
    w_pa, w_pb, w_out = xch.proj_weights(got_proj)
    sc_f, sc_b = _ctx_fwd(ctx, norm_g, scale_c, shift_c, w_pad, w2f, w2b, gbf, gbb)
    o_f, st_f = _gla_fwd(p, w2f, gbf, sc_f, False, 512, "gla_fwd_f")
    o_b, st_b = _gla_fwd(p, w2b, gbb, sc_b, True, 512, "gla_fwd_b")
    sv = _mix_fwd(p, ln_g, ln_b, ws.astype(bf16), bs.T)
    (dx1, dzbua, dzag, dsv, do, dw_out, dw_pa, dw_pb, dgf, dgate, dgbn, loss) = _mid(
        x, tgt, p, o_f, o_b, sv, gate, gf, gb_norm, w_pa, w_pb, w_out, 256)
    dva, dws, dbs_acc, dln_g, dln_b = _mix_bwd(p, dsv, ln_g, ln_b, jnp.swapaxes(ws, 1, 2).astype(bf16))
    dw_zbua = _tn_matmul(h, dzbua, 512, "dw_zbua")
    dw_va = _tn_matmul(h, dva, 512, "dw_va")
    dw_zag = _tn_matmul(h, dzag, 512, "dw_zag")
    blocks_a = _pack_dw_early(dw_zbua, dw_va, dw_zag, 0, EARLY_A_ROWS, "pack_dw_early_a")
    blocks_b = _pack_dw_early(dw_zbua, dw_va, dw_zag, EARLY_A_ROWS, D - EARLY_A_ROWS, "pack_dw_early_b")
    small = _rows128(dln_g, dln_b, dws, jnp.sum(dbs_acc, axis=-1), dgbn, dgf, jnp.broadcast_to(loss, (1, 128)))

    dqkv_f, dlr_f, dw2f, dgbf, dsc_f, *got_a = _gla_bwd(p, do, st_f, w2f, gbf, None, False, 512, "gla_bwd_f",
                                                        xch.early_a(dw_out, dw_pa, dw_pb, small, blocks_a))
    dqkv, dlr, dw2b, dgbb, dsc_b, *got_b = _gla_bwd(p, do, st_b, w2b, gbb, (dqkv_f, dlr_f), True, 512, "gla_bwd_b",
                                                    xch.early_b(blocks_b))
    dwk_c, dwv_c, dwl_c, dmodc, dg_c, dw2c, dgbc = _ctx_bwd(ctx, norm_g, scale_c, shift_c, w_pad, w2f, w2b, gbf, gbb, dsc_f, dsc_b)
    dw_qkv = _tn_matmul(h, dqkv, 512, "dw_qkv")
    dw_lr = _tn_matmul(h, dlr, 512, "dw_lr")
    blocks_late = _pack_dw_late(dw_qkv, dw_lr, dwk_c, dwv_c, dwl_c)
    dw2 = jnp.stack([dw2f[0:RANK] + dw2c[0, 0:RANK], dw2b[RANK:2 * RANK] + dw2c[1, RANK:2 * RANK]])
    dgb = jnp.concatenate([dgbf + dgbc[0], dgbb + dgbc[1]], axis=0)
    dgt = jnp.concatenate([jnp.transpose(dw2.reshape(2, RANK, NDEV, 32), (2, 0, 1, 3)).reshape(NDEV, 2 * RANK * 32),
                           jnp.transpose(dgb.reshape(2, NDEV, 32), (1, 0, 2)).reshape(NDEV, 64),
                           jnp.zeros((NDEV, 64), f32)], axis=1).reshape(NDEV, 9, 128)
    gx, dshift, dscale, dg, *got_late = _in_bwd(x, dx1, dqkv, dzbua, dva, dzag, dlr, w_pad, norm_g, scale, 256,
                                                xch.late(blocks_late, dgt))
    return dict(loss=loss, gx=gx, dmod=jnp.concatenate([dshift, dscale, dgate], axis=1), dmodc=dmodc, dnorm_g=dg + dg_c,
                small=small, blocks_a=blocks_a, blocks_b=blocks_b, blocks_late=blocks_late, dw2=dw2, dgb=dgb,
                dw_pa=dw_pa, dw_pb=dw_pb, dw_out=dw_out, got_a=got_a, got_b=got_b, got_late=got_late)


def _rows128(*vs):
    out = []
    for t in vs:
        t = t.reshape(-1)
        pad = (-t.shape[0]) % 128
        out.append(jnp.pad(t, (0, pad)) if pad else t)
    return jnp.concatenate(out).reshape(-1, 128)


def kernel(x, c, ctx, c_ctx, w_mod, b_mod, norm_g, w_in, a_ln_g, a_ln_b, a_ws, a_bs, b_gate_w2, b_gate_b, b_norm_g, w_proj_a, w_proj_b, w_out, final_norm_g, loss_target, m_c_ctx, m_w_mod, m_b_mod, m_norm_g, m_w_in, m_a_ln_g, m_a_ln_b, m_a_ws, m_a_bs, m_b_gate_w2, m_b_gate_b, m_b_norm_g, m_w_proj_a, m_w_proj_b, m_w_out, m_final_norm_g, v_c_ctx, v_w_mod, v_b_mod, v_norm_g, v_w_in, v_a_ln_g, v_a_ln_b, v_a_ws, v_a_bs, v_b_gate_w2, v_b_gate_b, v_b_norm_g, v_w_proj_a, v_w_proj_b, v_w_out, v_final_norm_g):
    me = _me()
    ncol = w_mod.shape[2]

    gate_mine = _rows128(jnp.concatenate([b_gate_w2.reshape(-1), b_gate_b.reshape(-1)]))
    early, wg, gates = _gather_first(_rows128(c), w_in[0].astype(bf16), gate_mine)
    cs = jnp.concatenate([early.reshape(NDEV, D), c_ctx.reshape(1, D), jnp.zeros((7, D), f32)], axis=0)
    w_pad = _repack_w(wg)
    gflat = gates.reshape(NDEV, 9 * 128)
    w2 = jnp.transpose(gflat[:, 0:2 * RANK * 32].reshape(NDEV, 2, RANK, 32), (1, 2, 0, 3)).reshape(2, RANK, KW)
    gb = jnp.transpose(gflat[:, 2 * RANK * 32:2 * RANK * 32 + 64].reshape(NDEV, 2, 32), (1, 0, 2)).reshape(2, KW)

    bm_mine = lax.dynamic_slice(b_mod, (0, me * ncol), (1, ncol))
    mods = _all_gather(_mod_fwd(cs, w_mod[0], bm_mine), "gather_mod")
    mods = jnp.transpose(mods, (1, 0, 2)).reshape(16, 3 * D)
    mod = lax.dynamic_slice(mods, (me, 0), (1, 3 * D))
    modc = mods[8:9, 0:2 * D]

    xch = _Exchanges(w_proj_a[0].astype(bf16), w_proj_b[0].astype(bf16), w_out[0].astype(bf16))
    r = _local_step(x[0], ctx[0], loss_target[0], mod, modc, norm_g, w_pad, a_ln_g, a_ln_b, a_ws[0], a_bs[0], w2, gb,
                    b_norm_g, final_norm_g.reshape(1, D), xch)
    p_out, p_pa, p_pb, smalls_e, p_in_a = r["got_a"]
    (p_in_b,) = r["got_b"]
    p_in_late, p_gt = r["got_late"]

    def split(t, sizes, shapes):
        t = t.reshape(-1)
        out, o = [], 0
        for s, sh in zip(sizes, shapes):
            out.append(t[o:o + s].reshape(sh))
            o += s
        return out

    n_e = (AW + AW + 4 * ACH * ACH + AW + VW + D) // 128
    rep_e = _adam(smalls_e[:, 0:n_e], _rows128(a_ln_g, a_ln_b, a_ws, a_bs, b_norm_g, final_norm_g),
                  _rows128(m_a_ln_g, m_a_ln_b, m_a_ws, m_a_bs, m_b_norm_g, m_final_norm_g),
                  _rows128(v_a_ln_g, v_a_ln_b, v_a_ws, v_a_bs, v_b_norm_g, v_final_norm_g), n_e, "adam_rep_early")
    rep_e = [split(t, (AW, AW, 4 * ACH * ACH, AW, VW, D), ((1, AW), (1, AW), (1, 4, ACH, ACH), (1, 4, ACH), (1, VW), (D,)))
             for t in rep_e]
    losses = smalls_e[:, n_e, 0]
    loss = losses[0]
    for i in range(1, NDEV):
        loss = loss + losses[i]

    dbm = r["dmod"] + jnp.concatenate([r["dmodc"], jnp.zeros((1, D), f32)], axis=1)
    smalls_l = _all_gather(_rows128(r["dnorm_g"], dbm, r["dmod"], r["dmodc"]), "gather_small")
    n_l = (D + 3 * D) // 128
    rep_l = _adam(smalls_l[:, 0:n_l], _rows128(norm_g, b_mod), _rows128(m_norm_g, m_b_mod), _rows128(v_norm_g, v_b_mod), n_l,
                  "adam_rep_late")
    rep_l = [split(t, (D, 3 * D), ((1, D), (1, 3 * D))) for t in rep_l]
    tail = smalls_l[:, n_l:].reshape(NDEV, -1)
    dmods = tail[:, 0:3 * D]
    dmodc_all = tail[:, 3 * D:5 * D]
    dmodc_tot = dmodc_all[0:1]
    for i in range(1, NDEV):
        dmodc_tot = dmodc_tot + dmodc_all[i:i + 1]

    dm_rows = jnp.concatenate([dmods, jnp.concatenate([dmodc_tot, jnp.zeros((1, D), f32)], axis=1), jnp.zeros((7, 3 * D), f32)], axis=0)
    dm_mine = lax.dynamic_slice(dm_rows, (0, me * ncol), (16, ncol))
    g_wmod, gc_part = _mod_bwd(cs, dm_mine, w_mod[0])
    wm = _adam(g_wmod[None], w_mod[0], m_w_mod[0], v_w_mod[0], 256, "adam_wmod")
    gcs = _all_gather(gc_part.reshape(8, 128), "gather_cctx")
    cc = _adam(gcs, c_ctx.reshape(8, 128), m_c_ctx.reshape(8, 128), v_c_ctx.reshape(8, 128), 8, "adam_cctx")

    a_in = _adam_w_in(p_in_a, p_in_b, p_in_late, w_in[0], m_w_in[0], v_w_in[0])
    a_pa = _adam(p_pa, w_proj_a[0], m_w_proj_a[0], v_w_proj_a[0], AW, "adam_w_pa")
    a_pb = _adam(p_pb, w_proj_b[0], m_w_proj_b[0], v_w_proj_b[0], VW, "adam_w_pb")
    a_out = _adam(p_out, w_out[0], m_w_out[0], v_w_out[0], 128, "adam_w_out")
    gate_m = _rows128(jnp.concatenate([m_b_gate_w2.reshape(-1), m_b_gate_b.reshape(-1)]))
    gate_v = _rows128(jnp.concatenate([v_b_gate_w2.reshape(-1), v_b_gate_b.reshape(-1)]))
    a_gt = [t.reshape(-1) for t in _adam(p_gt, gate_mine, gate_m, gate_v, 9, "adam_gate")]
    nw2 = 2 * RANK * 32
    sh = [(a_in[k][None], a_pa[k][None], a_pb[k][None], a_out[k][None], a_gt[k][0:nw2].reshape(1, 2, RANK, 32),
           a_gt[k][nw2:nw2 + 64].reshape(1, 2, 32)) for k in range(4)]

    outs = [loss, r["gx"][None]]
    for k in range(4):
        lg, lb, aws, abs_, bng, fng = rep_e[k]
        n_g, bmod = rep_l[k]
        s_in, s_pa, s_pb, s_out, s_w2, s_gb = sh[k]
        outs += [cc[k].reshape(D), wm[k][None], bmod, n_g, s_in, lg, lb, aws, abs_, s_w2, s_gb, bng, s_pa, s_pb, s_out, fng]
    return tuple(outs)
```

```python
import functools

import jax
import jax.numpy as jnp
from jax import lax
from jax.experimental import pallas as pl
from jax.experimental.pallas import tpu as pltpu

f32, bf16 = jnp.float32, jnp.bfloat16

D = 1024
CTX = 256
EPS = 1e-6
AW = 512
ACH = 128
GW = 64
KW = 256
VW = 512
NH = 4
HK = 64
HV = 128
RANK = 16
TAU = 16.0
CH = 64
QSCALE = HK ** -0.5
INW = 5152
NDEV = 8

PQ, PK, PV, PZB, PUA, PVA, PZA, PG1, PG2, PLR, PW = 0, 256, 512, 1024, 1536, 2048, 2560, 3072, 4096, 5120, 5248
LRW = 128

ADAM_LR, ADAM_B1, ADAM_B2, ADAM_EPS, ADAM_WD, ADAM_STEP = 0.001, 0.9, 0.999, 1e-08, 0.01, 10

VMEM_LIMIT = 56 * 1024 * 1024
MESH = pl.DeviceIdType.MESH


def _cp(sem=None):
    return pltpu.CompilerParams(dimension_semantics=sem, vmem_limit_bytes=VMEM_LIMIT)


def _dot(a, b):
    return jnp.dot(a.astype(bf16), b.astype(bf16), preferred_element_type=f32)


def _nt(a, b):
    return lax.dot_general(a.astype(bf16), b.astype(bf16), (((1,), (1,)), ((), ())), preferred_element_type=f32)


def _tn(a, b):
    return lax.dot_general(a.astype(bf16), b.astype(bf16), (((0,), (0,)), ((), ())), preferred_element_type=f32)


def _dot_hi(a, b):
    return jnp.dot(a, b, preferred_element_type=f32, precision=lax.Precision.HIGHEST)


def _sigmoid(x):
    return 1.0 / (1.0 + jnp.exp(-x))


def _log_sigmoid(x):
    return jnp.minimum(x, 0.0) - jnp.log(1.0 + jnp.exp(-jnp.abs(x)))


def _silu_and_grad(z):
    s = _sigmoid(z)
    return z * s, s * (1.0 + z * (1.0 - s))


def _me():
    return 4 * lax.axis_index("x") + 2 * lax.axis_index("y") + lax.axis_index("c")


def _peer(k):
    x, y, c = lax.axis_index("x"), lax.axis_index("y"), lax.axis_index("c")
    px = 1 - x if k & 4 else x
    py = 1 - y if k & 2 else y
    pc = 1 - c if k & 1 else c
    return (px, py, pc), 4 * px + 2 * py + pc


def _all_gather(v, name):
    r, c = v.shape

    def body(v_ref, out_ref, send_sems, recv_sems, local_sem):
        me = _me()
        mine = pltpu.make_async_copy(v_ref, out_ref.at[me], local_sem)
        mine.start()
        sends, recvs = [], []
        for k in range(1, NDEV):
            dev, idx = _peer(k)
            sends.append(pltpu.make_async_remote_copy(
                src_ref=v_ref, dst_ref=out_ref.at[me], send_sem=send_sems.at[k - 1], recv_sem=recv_sems.at[k - 1],
                device_id=dev, device_id_type=MESH))
            recvs.append(pltpu.make_async_remote_copy(
                src_ref=v_ref, dst_ref=out_ref.at[idx], send_sem=send_sems.at[k - 1], recv_sem=recv_sems.at[k - 1],
                device_id=dev, device_id_type=MESH))
        for cp in sends:
            cp.start()
        for cp in recvs:
            cp.wait_recv()
        for cp in sends:
            cp.wait_send()
        mine.wait()

    return pl.pallas_call(
        body, name=name, out_shape=jax.ShapeDtypeStruct((NDEV, r, c), v.dtype),
        in_specs=[pl.BlockSpec(memory_space=pl.ANY)], out_specs=pl.BlockSpec(memory_space=pl.ANY),
        scratch_shapes=[pltpu.SemaphoreType.DMA((NDEV - 1,)), pltpu.SemaphoreType.DMA((NDEV - 1,)), pltpu.SemaphoreType.DMA(())],
    )(v)


def _fanout(srcs, dsts, send_sems, recv_sems, local_sems, owners=None):
    me = _me()
    n = len(srcs)
    owns = lambda a, j: True if owners is None or owners[a] is None else owners[a](j)

    def guarded(cond, fn):
        if cond is True:
            fn()
        else:
            pl.when(cond)(fn)

    def copies(with_recvs):
        local = [pltpu.make_async_copy(srcs[a](me), dsts[a](me), local_sems.at[a]) for a in range(n)]
        sends, recvs = [], []
        for k in range(1, NDEV):
            dev, idx = _peer(k)
            for a in range(n):
                s = (k - 1) * n + a
                sends.append((owns(a, idx), pltpu.make_async_remote_copy(
                    src_ref=srcs[a](idx), dst_ref=dsts[a](me), send_sem=send_sems.at[s], recv_sem=recv_sems.at[s],
                    device_id=dev, device_id_type=MESH)))
                if with_recvs:
                    recvs.append((owns(a, me), pltpu.make_async_remote_copy(
                        src_ref=srcs[a](idx), dst_ref=dsts[a](idx), send_sem=send_sems.at[s], recv_sem=recv_sems.at[s],
                        device_id=dev, device_id_type=MESH)))
        return local, sends, recvs

    def start():
        local, sends, _ = copies(False)
        for a, cp in enumerate(local):
            guarded(owns(a, me), cp.start)
        for cond, cp in sends:
            guarded(cond, cp.start)

    def finish():
        local, sends, recvs = copies(True)
        for cond, cp in recvs:
            guarded(cond, cp.wait_recv)
        for cond, cp in sends:
            guarded(cond, cp.wait_send)
        for a, cp in enumerate(local):
            guarded(owns(a, me), cp.wait)

    return start, finish


class _Comm:
    def __init__(self, ins, outs, plan):
        self.ins, self.outs, self.plan = list(ins), list(outs), plan
        self.n = len(self.outs)

    def specs(self):
        hbm = pl.BlockSpec(memory_space=pl.ANY)
        return [hbm] * len(self.ins), [hbm] * self.n, _fanout_sems(self.n) if self.n else []

    def run(self, in_refs, out_refs, sems, nsteps, compute):
        if not self.n:
            compute()
            return

        def hooks():
            srcs, dsts, owners = self.plan(in_refs, out_refs)
            return _fanout(srcs, dsts, sems[0], sems[1], sems[2], owners)

        pl.when(pl.program_id(0) == 0)(lambda: hooks()[0]())
        compute()
        pl.when(pl.program_id(0) == nsteps - 1)(lambda: hooks()[1]())


def _fanout_sems(n):
    return [pltpu.SemaphoreType.DMA(((NDEV - 1) * n,)), pltpu.SemaphoreType.DMA(((NDEV - 1) * n,)), pltpu.SemaphoreType.DMA((n,))]


def _lanes(j):
    return pl.ds(pl.multiple_of(j * 128, 128), 128)


def _gather_first(c_rows, wi, gate):
    def body(c_ref, wi_ref, g_ref, oc, owi, og, send_sems, recv_sems, local_sems):
        srcs = [lambda j, r=r: r for r in (c_ref, wi_ref, g_ref)]
        dsts = [lambda j, r=r: r.at[j] for r in (oc, owi, og)]
        start, finish = _fanout(srcs, dsts, send_sems, recv_sems, local_sems)
        start()
        finish()

    hbm = pl.BlockSpec(memory_space=pl.ANY)
    return pl.pallas_call(
        body, name="gather_first",
        out_shape=(jax.ShapeDtypeStruct((NDEV,) + c_rows.shape, f32), jax.ShapeDtypeStruct((NDEV,) + wi.shape, bf16),
                   jax.ShapeDtypeStruct((NDEV,) + gate.shape, f32)),
        in_specs=[hbm] * 3, out_specs=(hbm,) * 3, scratch_shapes=_fanout_sems(3),
    )(c_rows, wi, gate)


SHARD = INW // NDEV
ROWS_RP = 128


def _overlap(lo, hi, a, b):
    s, e = max(lo, a), min(hi, b)
    return (s, e) if s < e else None


def _repack_w(wg):
    segs = ((0, 1024, PQ), (1024, 1024 + 2 * RANK, PLR), (1024 + 2 * RANK, INW, PZB))

    def body(g_ref, o_ref):
        for j in range(NDEV):
            lo, hi = j * SHARD, (j + 1) * SHARD
            for a, b, pad0 in segs:
                ov = _overlap(lo, hi, a, b)
                if ov:
                    s, e = ov
                    o_ref[:, pad0 + s - a:pad0 + e - a] = g_ref[j, :, s - lo:e - lo]
        o_ref[:, PLR + 2 * RANK:PW] = jnp.zeros((ROWS_RP, PW - PLR - 2 * RANK), bf16)

    return pl.pallas_call(
        body, name="repack_w", grid=(D // ROWS_RP,), out_shape=jax.ShapeDtypeStruct((D, PW), bf16),
        in_specs=[pl.BlockSpec((NDEV, ROWS_RP, SHARD), lambda i: (0, i, 0))],
        out_specs=pl.BlockSpec((ROWS_RP, PW), lambda i: (i, 0)), compiler_params=_cp(("arbitrary",)),
    )(wg)


LATE_END = 1024 + 2 * RANK
LATE_DESTS = 2


def _pack_blocks(o_ref, srcs, dests, dtype):
    for n, j in enumerate(dests):
        lo, hi = j * SHARD, (j + 1) * SHARD
        done = lo
        for a, b, src in srcs:
            ov = _overlap(lo, hi, a, b)
            if ov:
                s, e = ov
                if s > done:
                    o_ref[n, :, done - lo:s - lo] = jnp.zeros((ROWS_RP, s - done), dtype)
                o_ref[n, :, s - lo:e - lo] = src[:, s - a:e - a].astype(dtype)
                done = e
        if done < hi:
            o_ref[n, :, done - lo:hi - lo] = jnp.zeros((ROWS_RP, hi - done), dtype)


def _pack_dw_early(dw_zbua, dw_va, dw_zag, row0, nrows, name):
    def body(zbua_ref, va_ref, zag_ref, o_ref):
        _pack_blocks(o_ref, ((LATE_END, 2080, zbua_ref), (2080, 2592, va_ref), (2592, INW, zag_ref)), range(NDEV), bf16)

    row = lambda w: pl.BlockSpec((ROWS_RP, w), lambda i: (i + row0 // ROWS_RP, 0))
    return pl.pallas_call(
        body, name=name, grid=(nrows // ROWS_RP,), out_shape=jax.ShapeDtypeStruct((NDEV, nrows, SHARD), bf16),
        in_specs=[row(2 * AW), row(AW), row(AW + 2 * D)],
        out_specs=pl.BlockSpec((NDEV, ROWS_RP, SHARD), lambda i: (0, i, 0)), compiler_params=_cp(("arbitrary",)),
    )(dw_zbua, dw_va, dw_zag)


def _pack_dw_late(dw_qkv, dw_lr, dwk_c, dwv_c, dwl_c):
    def body(qkv_ref, lr_ref, kc_ref, vc_ref, lc_ref, o_ref):
        qkv = qkv_ref[...] + jnp.concatenate([jnp.zeros((ROWS_RP, KW), f32), kc_ref[...], vc_ref[...]], axis=1)
        lr = lr_ref[...] + lc_ref[...]
        _pack_blocks(o_ref, ((0, 1024, qkv), (1024, LATE_END, lr)), range(LATE_DESTS), bf16)

    row = lambda w: pl.BlockSpec((ROWS_RP, w), lambda i: (i, 0))
    return pl.pallas_call(
        body, name="pack_dw_late", grid=(D // ROWS_RP,), out_shape=jax.ShapeDtypeStruct((LATE_DESTS, D, SHARD), bf16),
        in_specs=[row(2 * KW + VW), row(LRW), row(KW), row(VW), row(LRW)],
        out_specs=pl.BlockSpec((LATE_DESTS, ROWS_RP, SHARD), lambda i: (0, i, 0)), compiler_params=_cp(("arbitrary",)),
    )(dw_qkv, dw_lr, dwk_c, dwv_c, dwl_c)


def _mod_fwd(cs, wm, bm):
    def body(cs_ref, wm_ref, bm_ref, o_ref):
        s, _ = _silu_and_grad(cs_ref[...])
        o_ref[...] = _dot_hi(s, wm_ref[...]) + bm_ref[...]

    return pl.pallas_call(body, name="mod_fwd", out_shape=jax.ShapeDtypeStruct((16, wm.shape[1]), f32),
                          compiler_params=_cp())(cs, wm, bm)


def _mod_bwd(cs, dm, wm):
    def body(cs_ref, dm_ref, wm_ref, gw_ref, gc_ref):
        s, ds = _silu_and_grad(cs_ref[...])
        gw_ref[...] = lax.dot_general(s, dm_ref[...], (((0,), (0,)), ((), ())), preferred_element_type=f32,
                                      precision=lax.Precision.HIGHEST)
        part = lax.dot_general(dm_ref[8:9, :], wm_ref[...], (((1,), (1,)), ((), ())), preferred_element_type=f32,
                               precision=lax.Precision.HIGHEST)
        gc_ref[...] = part * ds[8:9, :]

    return pl.pallas_call(body, name="mod_bwd",
                          out_shape=(jax.ShapeDtypeStruct(wm.shape, f32), jax.ShapeDtypeStruct((1, D), f32)),
                          compiler_params=_cp())(cs, dm, wm)


def _adam_update(g, w_ref, m_ref, v_ref, go_ref, d_ref, mo_ref, vo_ref):
    c1 = 1.0 / (1.0 - ADAM_B1 ** ADAM_STEP)
    c2 = 1.0 / (1.0 - ADAM_B2 ** ADAM_STEP)
    mn = ADAM_B1 * m_ref[...] + (1.0 - ADAM_B1) * g
    vn = ADAM_B2 * v_ref[...] + (1.0 - ADAM_B2) * (g * g)
    go_ref[...] = g
    mo_ref[...] = mn
    vo_ref[...] = vn
    d_ref[...] = -ADAM_LR * ((mn * c1) / (jnp.sqrt(vn * c2) + ADAM_EPS) + ADAM_WD * w_ref[...])


def _adam_w_in(parts_a, parts_b, parts_late, w, m, v):
    na = EARLY_A_ROWS // ROWS_RP

    def body(a_ref, b_ref, l_ref, w_ref, m_ref, v_ref, go_ref, d_ref, mo_ref, vo_ref):
        me = _me()
        first = pl.program_id(0) < na
        early = jnp.where(first, a_ref[0], b_ref[0]).astype(f32)
        late = l_ref[0].astype(f32)
        for i in range(1, NDEV):
            early = early + jnp.where(first, a_ref[i], b_ref[i]).astype(f32)
            late = late + l_ref[i].astype(f32)
        g = jnp.where(me >= LATE_DESTS - 1, early, 0.0) + jnp.where(me < LATE_DESTS, late, 0.0)
        _adam_update(g, w_ref, m_ref, v_ref, go_ref, d_ref, mo_ref, vo_ref)

    blk = pl.BlockSpec((None, ROWS_RP, SHARD), lambda i: (0, i, 0))
    return pl.pallas_call(
        body, name="adam_w_in", grid=(D // ROWS_RP,), out_shape=tuple(jax.ShapeDtypeStruct((1, D, SHARD), f32) for _ in range(4)),
        in_specs=[pl.BlockSpec((NDEV, ROWS_RP, SHARD), lambda i: (0, jnp.minimum(i, na - 1), 0)),
                  pl.BlockSpec((NDEV, ROWS_RP, SHARD), lambda i: (0, jnp.maximum(i - na, 0), 0)),
                  pl.BlockSpec((NDEV, ROWS_RP, SHARD), lambda i: (0, i, 0)), blk, blk, blk],
        out_specs=(blk, blk, blk, blk), compiler_params=_cp(("arbitrary",)),
    )(parts_a, parts_b, parts_late, w, m, v)


def _adam(parts, w, m, v, rows, name):
    p, r, c = parts.shape
    lead = w.ndim - 2

    def body(g_ref, w_ref, m_ref, v_ref, go_ref, d_ref, mo_ref, vo_ref):
        g = g_ref[0].astype(f32)
        for i in range(1, p):
            g = g + g_ref[i].astype(f32)
        _adam_update(g, w_ref, m_ref, v_ref, go_ref, d_ref, mo_ref, vo_ref)

    blk = pl.BlockSpec((None,) * lead + (rows, c), lambda i: (0,) * lead + (i, 0))
    return pl.pallas_call(
        body, name=name, grid=(r // rows,), out_shape=tuple(jax.ShapeDtypeStruct(w.shape, f32) for _ in range(4)),
        in_specs=[pl.BlockSpec((p, rows, c), lambda i: (0, i, 0)), blk, blk, blk], out_specs=(blk, blk, blk, blk),
        compiler_params=_cp(("arbitrary",)),
    )(parts, w, m, v)


def _in_proj(x, g, scale, shift, w_pad, tl, comm):
    l = x.shape[0]
    c_in, c_out, c_sems = comm.specs()

    def body(*refs):
        x_ref, g_ref, sc_ref, sh_ref, w_ref = refs[:5]
        cin = refs[5:5 + len(c_in)]
        p_ref, h_ref = refs[5 + len(c_in):7 + len(c_in)]
        cout = refs[7 + len(c_in):7 + len(c_in) + comm.n]
        sems = refs[7 + len(c_in) + comm.n:]

        def compute():
            xv = x_ref[...]
            r = lax.rsqrt(jnp.mean(xv * xv, axis=-1, keepdims=True) + EPS)
            h = (xv * r) * (g_ref[...] * (1.0 + sc_ref[...])) + sh_ref[...]
            hb = h.astype(bf16)
            h_ref[...] = hb
            p_ref[...] = jnp.dot(hb, w_ref[...], preferred_element_type=f32).astype(bf16)

        comm.run(cin, cout, sems, l // tl, compute)

    vec = pl.BlockSpec((1, D), lambda i: (0, 0))
    return pl.pallas_call(
        body, name="in_proj", grid=(l // tl,),
        out_shape=(jax.ShapeDtypeStruct((l, PW), bf16), jax.ShapeDtypeStruct((l, D), bf16)) + tuple(comm.outs),
        in_specs=[pl.BlockSpec((tl, D), lambda i: (i, 0)), vec, vec, vec, pl.BlockSpec((D, PW), lambda i: (0, 0))] + c_in,
        out_specs=(pl.BlockSpec((tl, PW), lambda i: (i, 0)), pl.BlockSpec((tl, D), lambda i: (i, 0))) + tuple(c_out),
        scratch_shapes=c_sems, compiler_params=_cp(("arbitrary",)),
    )(x, g, scale, shift, w_pad, *comm.ins)


def _tri(rev):
    i = lax.broadcasted_iota(jnp.int32, (CH, CH), 0)
    j = lax.broadcasted_iota(jnp.int32, (CH, CH), 1)
    return jnp.where((j >= i) if rev else (j <= i), 1.0, 0.0).astype(f32)


def _head_masks():
    lane = lax.broadcasted_iota(jnp.int32, (1, KW), 1) // HK
    return [jnp.where(lane == h, 1.0, 0.0).astype(f32) for h in range(NH)]


def _block_diag():
    r = lax.broadcasted_iota(jnp.int32, (VW, KW), 0) // HV
    c = lax.broadcasted_iota(jnp.int32, (VW, KW), 1) // HK
    return jnp.where(r == c, 1.0, 0.0).astype(f32)


def _decay(lr, w2, gb, tri, rev):
    logits = _dot(lr, w2) + gb
    a = _log_sigmoid(logits) * (1.0 / TAU)
    c = _dot_hi(tri, a)
    cl = c[0:1, :] if rev else c[CH - 1:CH, :]
    return logits, c, cl


def _stack_heads(t, hm):
    return jnp.concatenate([t * hm[h] for h in range(NH)], axis=0)


def _chunk_fwd(q, k, v, c, cl, st, tri, hm, bd):
    tri4 = jnp.concatenate([tri] * NH, axis=0)
    qd = q * jnp.exp(c) * QSCALE
    kd = k * jnp.exp(-c)
    kdec = k * jnp.exp(cl - c)
    pst = _nt(_stack_heads(qd, hm), kd) * tri4
    intra = jnp.concatenate([_dot(pst[h * CH:(h + 1) * CH], v[:, h * HV:(h + 1) * HV]) for h in range(NH)], axis=1)
    o = _nt(qd, st) + intra
    st_new = st * jnp.exp(cl) + bd * _tn(v, kdec)
    return o, st_new


def _state_fwd(k, v, c, cl, st, bd):
    return st * jnp.exp(cl) + bd * _tn(v, k * jnp.exp(cl - c))


def _chunk_bwd(q, k, v, c, cl, st0, dst, do, tri, trit, hm, bd):
    ecl = jnp.exp(cl)
    edec = jnp.exp(cl - c)
    kdec = k * edec
    dv = _nt(kdec, dst)
    dkdec = _dot(v, dst)
    dcl = jnp.sum(dst * st0, axis=0, keepdims=True) * ecl + jnp.sum(dkdec * kdec, axis=0, keepdims=True)
    dc = -dkdec * kdec
    dk = dkdec * edec
    dst0 = dst * ecl
    dq = None
    if q is not None:
        tri4 = jnp.concatenate([tri] * NH, axis=0)
        ec, enc = jnp.exp(c), jnp.exp(-c)
        qd = q * ec * QSCALE
        kd = k * enc
        qs = _stack_heads(qd, hm)
        pst = _nt(qs, kd) * tri4
        dpst = jnp.concatenate([_nt(do[:, h * HV:(h + 1) * HV], v[:, h * HV:(h + 1) * HV]) for h in range(NH)], axis=0) * tri4
        dv = dv + jnp.concatenate([_tn(pst[h * CH:(h + 1) * CH], do[:, h * HV:(h + 1) * HV]) for h in range(NH)], axis=1)
        dqd = _dot(do, st0)
        for h in range(NH):
            dqd = dqd + hm[h] * _dot(dpst[h * CH:(h + 1) * CH], kd)
        dkd = _tn(dpst, qs)
        dst0 = dst0 + bd * _tn(do, qd)
        dc = dc + dqd * qd - dkd * kd
        dq = dqd * ec * QSCALE
        dk = dk + dkd * enc
    da = _dot_hi(trit, dc) + dcl
    return dq, dk, dv, da, dst0


def _bdot(a, b):
    return lax.dot_general(a.astype(bf16), b.astype(bf16), (((2,), (1,)), ((0,), (0,))), preferred_element_type=f32)


def _bnt(a, b):
    return lax.dot_general(a.astype(bf16), b.astype(bf16), (((2,), (2,)), ((0,), (0,))), preferred_element_type=f32)


def _btn(a, b):
    return lax.dot_general(a.astype(bf16), b.astype(bf16), (((1,), (1,)), ((0,), (0,))), preferred_element_type=f32)


def _scan_chunks(x, rev):
    t = x.shape[0]
    pos = lax.broadcasted_iota(jnp.int32, x.shape, 0) % CH
    s = 1
    while s < CH:
        if rev:
            x = x + jnp.where(pos < CH - s, pltpu.roll(x, t - s, 0), 0.0)
        else:
            x = x + jnp.where(pos >= s, pltpu.roll(x, s, 0), 0.0)
        s *= 2
    return x


class _Tile:
    pass


def _tile_prep(q_ref, k_ref, lr_ref, w2_ref, gb_ref, rev, nc):
    t = _Tile()
    tg = nc * CH
    t.logits = _dot(lr_ref[...], w2_ref[...]) + gb_ref[...]
    c = _scan_chunks(_log_sigmoid(t.logits) * (1.0 / TAU), rev).reshape(nc, CH, KW)
    cl = c[:, 0:1, :] if rev else c[:, CH - 1:CH, :]
    k = k_ref[...].astype(f32).reshape(nc, CH, KW)
    t.ec, t.enc, t.edec, t.ecl = jnp.exp(c), jnp.exp(-c), jnp.exp(cl - c), jnp.exp(cl)
    t.qd = q_ref[...].astype(f32).reshape(nc, CH, KW) * t.ec * QSCALE
    t.kd = k * t.enc
    t.kdec = k * t.edec
    hm = _head_masks()
    t.tri4 = jnp.concatenate([_tri(rev)] * NH, axis=0)[None]
    t.qs = jnp.concatenate([t.qd * hm[h] for h in range(NH)], axis=1)
    t.pst = _bnt(t.qs, t.kd) * t.tri4
    return t


def _gla_fwd(p, w2p, gb, s0, rev, tg, name):
    l = p.shape[0]
    nb, nc = l // tg, tg // CH

    def body(q_ref, k_ref, v_ref, lr_ref, w2_ref, gb_ref, s0_ref, o_ref, st_ref, st):
        @pl.when(pl.program_id(0) == 0)
        def _():
            st[...] = s0_ref[...]

        t = _tile_prep(q_ref, k_ref, lr_ref, w2_ref, gb_ref, rev, nc)
        v = v_ref[...].reshape(nc, CH, VW)
        intra = jnp.concatenate([_bdot(t.pst[:, h * CH:(h + 1) * CH], v[:, :, h * HV:(h + 1) * HV]) for h in range(NH)], axis=2)
        kv = _btn(v, t.kdec) * _block_diag()[None]
        s = st[...]
        for n in (range(nc - 1, -1, -1) if rev else range(nc)):
            st_ref[n] = s
            s = s * t.ecl[n] + kv[n]
        st[...] = s
        o_ref[...] = (_bnt(t.qd, st_ref[...]) + intra).reshape(tg, VW)

    blk = (lambda i: nb - 1 - i) if rev else (lambda i: i)
    return pl.pallas_call(
        body, name=name, grid=(nb,),
        out_shape=(jax.ShapeDtypeStruct((l, VW), f32), jax.ShapeDtypeStruct((l // CH, VW, KW), f32)),
        in_specs=[pl.BlockSpec((tg, KW), lambda i: (blk(i), PQ // KW)), pl.BlockSpec((tg, KW), lambda i: (blk(i), PK // KW)),
                  pl.BlockSpec((tg, VW), lambda i: (blk(i), PV // VW)), pl.BlockSpec((tg, LRW), lambda i: (blk(i), PLR // LRW)),
                  pl.BlockSpec((LRW, KW), lambda i: (0, 0)), pl.BlockSpec((1, KW), lambda i: (0, 0)),
                  pl.BlockSpec((VW, KW), lambda i: (0, 0))],
        out_specs=(pl.BlockSpec((tg, VW), lambda i: (blk(i), 0)), pl.BlockSpec((nc, VW, KW), lambda i: (blk(i), 0, 0))),
        scratch_shapes=[pltpu.VMEM((VW, KW), f32)],
        compiler_params=_cp(("arbitrary",)),
    )(p, p, p, p, w2p, gb, s0)


def _gla_bwd(p, do, states, w2p, gb, prev, rev, tg, name, comm):
    l = p.shape[0]
    nb, nc = l // tg, tg // CH
    out_dt = f32 if prev is None else bf16
    c_in, c_out, c_sems = comm.specs()

    def body(*refs):
        q_ref, k_ref, v_ref, lr_ref, do_ref, st_ref, w2_ref, gb_ref = refs[:8]
        refs = refs[8:]
        if prev is not None:
            pq_ref, pl_ref = refs[:2]
            refs = refs[2:]
        cin, refs = refs[:len(c_in)], refs[len(c_in):]
        dqkv_ref, dlr_ref, dw2_ref, dgb_ref, ds0_ref = refs[:5]
        cout, dst, ds_buf, sems = refs[5:5 + comm.n], refs[5 + comm.n], refs[6 + comm.n], refs[7 + comm.n:]
        comm.run(cin, cout, sems, nb, lambda: compute(q_ref, k_ref, v_ref, lr_ref, do_ref, st_ref, w2_ref, gb_ref,
                                                      pq_ref if prev is not None else None, pl_ref if prev is not None else None,
                                                      dqkv_ref, dlr_ref, dw2_ref, dgb_ref, ds0_ref, dst, ds_buf))

    def compute(q_ref, k_ref, v_ref, lr_ref, do_ref, st_ref, w2_ref, gb_ref, pq_ref, pl_ref,
                dqkv_ref, dlr_ref, dw2_ref, dgb_ref, ds0_ref, dst, ds_buf):
        @pl.when(pl.program_id(0) == 0)
        def _():
            dst[...] = jnp.zeros_like(dst)
            dw2_ref[...] = jnp.zeros_like(dw2_ref)
            dgb_ref[...] = jnp.zeros_like(dgb_ref)

        t = _tile_prep(q_ref, k_ref, lr_ref, w2_ref, gb_ref, rev, nc)
        hm = _head_masks()
        v = v_ref[...].reshape(nc, CH, VW)
        do = do_ref[...].reshape(nc, CH, VW)
        heads = lambda a, h: a[:, :, h * HV:(h + 1) * HV]
        dpst = jnp.concatenate([_bnt(heads(do, h), heads(v, h)) for h in range(NH)], axis=1) * t.tri4
        dv = jnp.concatenate([_btn(t.pst[:, h * CH:(h + 1) * CH], heads(do, h)) for h in range(NH)], axis=2)
        dqd = _bdot(do, st_ref[...])
        for h in range(NH):
            dqd = dqd + hm[h] * _bdot(dpst[:, h * CH:(h + 1) * CH], t.kd)
        dkd = _btn(dpst, t.qs)
        u = _btn(do, t.qd) * _block_diag()[None]
        d = dst[...]
        for n in (range(nc) if rev else range(nc - 1, -1, -1)):
            ds_buf[n] = d
            d = d * t.ecl[n] + u[n]
        dst[...] = d
        ds0_ref[...] = d
        ds = ds_buf[...]
        dv = dv + _bnt(t.kdec, ds)
        dkdec = _bdot(v, ds)
        dcl = jnp.sum(ds * st_ref[...], axis=1, keepdims=True) * t.ecl + jnp.sum(dkdec * t.kdec, axis=1, keepdims=True)
        dc = dqd * t.qd - dkd * t.kd - dkdec * t.kdec
        da = _scan_chunks(dc.reshape(tg, KW), not rev).reshape(nc, CH, KW) + dcl
        dq = dqd * t.ec * QSCALE
        dk = dkd * t.enc + dkdec * t.edec
        dlog = da.reshape(tg, KW) * _sigmoid(-t.logits) * (1.0 / TAU)
        dlr = _nt(dlog, w2_ref[...])
        dw2_ref[...] += _tn(lr_ref[...], dlog)
        dgb_ref[...] += jnp.sum(dlog, axis=0, keepdims=True)
        dqkv = jnp.concatenate([dq, dk, dv], axis=2).reshape(tg, 2 * KW + VW)
        if prev is not None:
            dqkv = dqkv + pq_ref[...]
            dlr = dlr + pl_ref[...]
        dqkv_ref[...] = dqkv.astype(out_dt)
        dlr_ref[...] = dlr.astype(out_dt)

    blk = (lambda i: i) if rev else (lambda i: nb - 1 - i)
    in_specs = [pl.BlockSpec((tg, KW), lambda i: (blk(i), PQ // KW)), pl.BlockSpec((tg, KW), lambda i: (blk(i), PK // KW)),
                pl.BlockSpec((tg, VW), lambda i: (blk(i), PV // VW)), pl.BlockSpec((tg, LRW), lambda i: (blk(i), PLR // LRW)),
                pl.BlockSpec((tg, VW), lambda i: (blk(i), 0)), pl.BlockSpec((nc, VW, KW), lambda i: (blk(i), 0, 0)),
                pl.BlockSpec((LRW, KW), lambda i: (0, 0)), pl.BlockSpec((1, KW), lambda i: (0, 0))]
    args = [p, p, p, p, do, states, w2p, gb]
    if prev is not None:
        in_specs += [pl.BlockSpec((tg, 2 * KW + VW), lambda i: (blk(i), 0)), pl.BlockSpec((tg, LRW), lambda i: (blk(i), 0))]
        args += list(prev)
    return pl.pallas_call(
        body, name=name, grid=(nb,),
        out_shape=(jax.ShapeDtypeStruct((l, 2 * KW + VW), out_dt), jax.ShapeDtypeStruct((l, LRW), out_dt),
                   jax.ShapeDtypeStruct((LRW, KW), f32), jax.ShapeDtypeStruct((1, KW), f32), jax.ShapeDtypeStruct((VW, KW), f32))
        + tuple(comm.outs),
        in_specs=in_specs + c_in,
        out_specs=(pl.BlockSpec((tg, 2 * KW + VW), lambda i: (blk(i), 0)), pl.BlockSpec((tg, LRW), lambda i: (blk(i), 0)),
                   pl.BlockSpec((LRW, KW), lambda i: (0, 0)), pl.BlockSpec((1, KW), lambda i: (0, 0)),
                   pl.BlockSpec((VW, KW), lambda i: (0, 0))) + tuple(c_out),
        scratch_shapes=[pltpu.VMEM((VW, KW), f32), pltpu.VMEM((nc, VW, KW), f32)] + c_sems,
        compiler_params=_cp(("arbitrary",)),
    )(*args, *comm.ins)


def _ctx_hidden(ctx_ref, g_ref, sc_ref, sh_ref):
    xv = ctx_ref[...]
    r = lax.rsqrt(jnp.mean(xv * xv, axis=-1, keepdims=True) + EPS)
    xn = xv * r
    return xn, xn * (g_ref[...] * (1.0 + sc_ref[...])) + sh_ref[...]


_CTX_W_SPECS = [pl.BlockSpec((D, KW), lambda i: (0, PK // KW)), pl.BlockSpec((D, VW), lambda i: (0, PV // VW)),
                pl.BlockSpec((D, LRW), lambda i: (0, PLR // LRW))]


def _ctx_fwd(ctx, g, scale, shift, w_pad, w2f, w2b, gbf, gbb):
    ncc = CTX // CH

    def body(ctx_ref, g_ref, sc_ref, sh_ref, wk_ref, wv_ref, wl_ref, w2f_ref, w2b_ref, gbf_ref, gbb_ref, sf_ref, sb_ref):
        _, hc = _ctx_hidden(ctx_ref, g_ref, sc_ref, sh_ref)
        k, v, lr = _dot(hc, wk_ref[...]), _dot(hc, wv_ref[...]), _dot(hc, wl_ref[...])
        bd = _block_diag()
        for rev, w2_ref, gb_ref, out in ((False, w2f_ref, gbf_ref, sf_ref), (True, w2b_ref, gbb_ref, sb_ref)):
            tri = _tri(rev)
            st = jnp.zeros((VW, KW), f32)
            for j in (range(ncc - 1, -1, -1) if rev else range(ncc)):
                rows = slice(j * CH, (j + 1) * CH)
                _, c, cl = _decay(lr[rows], w2_ref[...], gb_ref[...], tri, rev)
                st = _state_fwd(k[rows], v[rows], c, cl, st, bd)
            out[...] = st

    vec = pl.BlockSpec((1, D), lambda i: (0, 0))
    w2s = pl.BlockSpec((LRW, KW), lambda i: (0, 0))
    gbs = pl.BlockSpec((1, KW), lambda i: (0, 0))
    sts = pl.BlockSpec((VW, KW), lambda i: (0, 0))
    return pl.pallas_call(
        body, name="ctx_fwd", grid=(1,), out_shape=(jax.ShapeDtypeStruct((VW, KW), f32),) * 2,
        in_specs=[pl.BlockSpec((CTX, D), lambda i: (0, 0)), vec, vec, vec] + _CTX_W_SPECS + [w2s, w2s, gbs, gbs],
        out_specs=(sts, sts), compiler_params=_cp(("arbitrary",)),
    )(ctx, g, scale, shift, w_pad, w_pad, w_pad, w2f, w2b, gbf, gbb)


def _ctx_bwd(ctx, g, scale, shift, w_pad, w2f, w2b, gbf, gbb, dsf, dsb):
    ncc = CTX // CH

    def body(ctx_ref, g_ref, sc_ref, sh_ref, wk_ref, wv_ref, wl_ref, w2f_ref, w2b_ref, gbf_ref, gbb_ref, dsf_ref, dsb_ref,
             dwk_ref, dwv_ref, dwl_ref, dmod_ref, dg_ref, dw2_ref, dgb_ref):
        xn, hc = _ctx_hidden(ctx_ref, g_ref, sc_ref, sh_ref)
        k, v, lr = _dot(hc, wk_ref[...]), _dot(hc, wv_ref[...]), _dot(hc, wl_ref[...])
        bd = _block_diag()
        dk_rows, dv_rows, dl_rows = [None] * ncc, [None] * ncc, [None] * ncc
        for d, (rev, w2_ref, gb_ref, ds_ref) in enumerate(((False, w2f_ref, gbf_ref, dsf_ref), (True, w2b_ref, gbb_ref, dsb_ref))):
            tri = _tri(rev)
            order = list(range(ncc - 1, -1, -1) if rev else range(ncc))
            st, saved = jnp.zeros((VW, KW), f32), {}
            for j in order:
                rows = slice(j * CH, (j + 1) * CH)
                logits, c, cl = _decay(lr[rows], w2_ref[...], gb_ref[...], tri, rev)
                saved[j] = (logits, c, cl, st)
                st = _state_fwd(k[rows], v[rows], c, cl, st, bd)
            dst = ds_ref[...]
            dw2 = jnp.zeros((LRW, KW), f32)
            dgb = jnp.zeros((1, KW), f32)
            for j in reversed(order):
                rows = slice(j * CH, (j + 1) * CH)
                logits, c, cl, st0 = saved[j]
                _, dk, dv, da, dst = _chunk_bwd(None, k[rows], v[rows], c, cl, st0, dst, None, tri, _tri(not rev), None, bd)
                dlog = da * _sigmoid(-logits) * (1.0 / TAU)
                dl = _nt(dlog, w2_ref[...])
                dw2 = dw2 + _tn(lr[rows], dlog)
                dgb = dgb + jnp.sum(dlog, axis=0, keepdims=True)
                dk_rows[j] = dk if dk_rows[j] is None else dk_rows[j] + dk
                dv_rows[j] = dv if dv_rows[j] is None else dv_rows[j] + dv
                dl_rows[j] = dl if dl_rows[j] is None else dl_rows[j] + dl
            dw2_ref[d] = dw2
            dgb_ref[d] = dgb
        dk, dv, dl = (jnp.concatenate(t, axis=0) for t in (dk_rows, dv_rows, dl_rows))
        dwk_ref[...] = _tn(hc, dk)
        dwv_ref[...] = _tn(hc, dv)
        dwl_ref[...] = _tn(hc, dl)
        dh = _nt(dk, wk_ref[...]) + _nt(dv, wv_ref[...]) + _nt(dl, wl_ref[...])
        gx = dh * xn
        dmod_ref[:, 0:D] = jnp.sum(dh, axis=0, keepdims=True)
        dmod_ref[:, D:2 * D] = jnp.sum(gx, axis=0, keepdims=True) * g_ref[...]
        dg_ref[...] = jnp.sum(gx, axis=0, keepdims=True) * (1.0 + sc_ref[...])

    vec = pl.BlockSpec((1, D), lambda i: (0, 0))
    w2s = pl.BlockSpec((LRW, KW), lambda i: (0, 0))
    gbs = pl.BlockSpec((1, KW), lambda i: (0, 0))
    sts = pl.BlockSpec((VW, KW), lambda i: (0, 0))
    full = lambda *s: pl.BlockSpec(s, lambda i: (0,) * len(s))
    return pl.pallas_call(
        body, name="ctx_bwd", grid=(1,),
        out_shape=(jax.ShapeDtypeStruct((D, KW), f32), jax.ShapeDtypeStruct((D, VW), f32), jax.ShapeDtypeStruct((D, LRW), f32),
                   jax.ShapeDtypeStruct((1, 2 * D), f32), jax.ShapeDtypeStruct((1, D), f32),
                   jax.ShapeDtypeStruct((2, LRW, KW), f32), jax.ShapeDtypeStruct((2, 1, KW), f32)),
        in_specs=[pl.BlockSpec((CTX, D), lambda i: (0, 0)), vec, vec, vec] + _CTX_W_SPECS + [w2s, w2s, gbs, gbs, sts, sts],
        out_specs=(full(D, KW), full(D, VW), full(D, LRW), full(1, 2 * D), full(1, D), full(2, LRW, KW), full(2, 1, KW)),
        compiler_params=_cp(("arbitrary",)),
    )(ctx, g, scale, shift, w_pad, w_pad, w_pad, w2f, w2b, gbf, gbb, dsf, dsb)


def _layernorm(va, g, b):
    mu = jnp.mean(va, axis=-1, keepdims=True)
    xc = va - mu
    rstd = lax.rsqrt(jnp.mean(xc * xc, axis=-1, keepdims=True) + EPS)
    vhat = xc * rstd
    return vhat, rstd, vhat * g + b


def _mix_fwd(p, ln_g, ln_b, ws, bs_t):
    l = p.shape[0]
    nch = l // ACH
    half = AW // 2

    def body(p_ref, g_ref, b_ref, ws_ref, bs_ref, sv_ref, va_buf, col_buf, sem):
        cp = pltpu.make_async_copy(p_ref.at[:, pl.ds(PVA, AW)], va_buf, sem)
        cp.start()
        cp.wait()

        def rows_step(n, carry):
            rows = pl.ds(pl.multiple_of(n * ACH, ACH), ACH)
            _, _, vn = _layernorm(va_buf[rows, :].astype(f32), g_ref[...], b_ref[...])
            for gi in range(2):
                sl = slice(gi * ACH, (gi + 1) * ACH)
                sv_ref[rows, sl] = (_dot(ws_ref[gi], vn[:, sl]) + bs_ref[:, gi:gi + 1]).astype(bf16)
            col_buf[0, rows, :] = vn[:, half:half + ACH]
            col_buf[1, rows, :] = vn[:, half + ACH:]
            return carry

        lax.fori_loop(0, nch, rows_step, 0)

        def cols_step(cidx, carry):
            rows = pl.ds(cidx, ACH, stride=GW)
            for gi in range(2, 4):
                col_buf[gi - 2, rows, :] = _dot(ws_ref[gi], col_buf[gi - 2, rows, :]) + bs_ref[:, gi:gi + 1]
            return carry

        lax.fori_loop(0, GW, cols_step, 0)

        def out_step(n, carry):
            rows = pl.ds(pl.multiple_of(n * ACH, ACH), ACH)
            sv_ref[rows, half:half + ACH] = col_buf[0, rows, :].astype(bf16)
            sv_ref[rows, half + ACH:] = col_buf[1, rows, :].astype(bf16)
            return carry

        lax.fori_loop(0, nch, out_step, 0)

    vm = pl.BlockSpec(memory_space=pltpu.VMEM)
    return pl.pallas_call(
        body, name="mix_fwd", out_shape=jax.ShapeDtypeStruct((l, AW), bf16),
        in_specs=[pl.BlockSpec(memory_space=pl.ANY), vm, vm, vm, vm], out_specs=vm,
        scratch_shapes=[pltpu.VMEM((l, AW), bf16), pltpu.VMEM((2, l, ACH), f32), pltpu.SemaphoreType.DMA(())],
        compiler_params=_cp(),
    )(p, ln_g, ln_b, ws, bs_t)


def _mix_bwd(p, dsv, ln_g, ln_b, ws_t):
    l = p.shape[0]
    nch = l // ACH
    half = AW // 2

    def body(p_ref, dsv_hbm, g_ref, b_ref, wst_ref, dva_ref, dws_ref, dbs_ref, dg_ref, db_ref, va_buf, dsv_buf, vn_col, ds_col, sems):
        cp1 = pltpu.make_async_copy(p_ref.at[:, pl.ds(PVA, AW)], va_buf, sems.at[0])
        cp2 = pltpu.make_async_copy(dsv_hbm, dsv_buf, sems.at[1])
        cp1.start()
        cp2.start()
        dws_ref[...] = jnp.zeros_like(dws_ref)
        dbs_ref[...] = jnp.zeros_like(dbs_ref)
        dg_ref[...] = jnp.zeros_like(dg_ref)
        db_ref[...] = jnp.zeros_like(db_ref)
        cp1.wait()
        cp2.wait()

        def rows_step(n, carry):
            rows = pl.ds(pl.multiple_of(n * ACH, ACH), ACH)
            _, _, vn = _layernorm(va_buf[rows, :].astype(f32), g_ref[...], b_ref[...])
            ds = dsv_buf[rows, :].astype(f32)
            for gi in range(2):
                sl = slice(gi * ACH, (gi + 1) * ACH)
                dws_ref[gi] += _nt(ds[:, sl], vn[:, sl])
                dbs_ref[gi] += ds[:, sl]
            for gi in range(2):
                sl = slice(half + gi * ACH, half + (gi + 1) * ACH)
                vn_col[gi, rows, :] = vn[:, sl]
                ds_col[gi, rows, :] = ds[:, sl]
            return carry

        lax.fori_loop(0, nch, rows_step, 0)

        def cols_step(cidx, carry):
            rows = pl.ds(cidx, ACH, stride=GW)
            for gi in range(2, 4):
                ds = ds_col[gi - 2, rows, :]
                dws_ref[gi] += _nt(ds, vn_col[gi - 2, rows, :])
                dbs_ref[gi] += ds
                ds_col[gi - 2, rows, :] = _dot(wst_ref[gi], ds)
            return carry

        lax.fori_loop(0, GW, cols_step, 0)

        def out_step(n, carry):
            rows = pl.ds(pl.multiple_of(n * ACH, ACH), ACH)
            vhat, rstd, _ = _layernorm(va_buf[rows, :].astype(f32), g_ref[...], b_ref[...])
            ds = dsv_buf[rows, :].astype(f32)
            dvn = jnp.concatenate([_dot(wst_ref[0], ds[:, 0:ACH]), _dot(wst_ref[1], ds[:, ACH:half]), ds_col[0, rows, :], ds_col[1, rows, :]], axis=1)
            dg_ref[...] += jnp.sum(dvn * vhat, axis=0, keepdims=True)
            db_ref[...] += jnp.sum(dvn, axis=0, keepdims=True)
            dvh = dvn * g_ref[...]
            dva = rstd * (dvh - jnp.mean(dvh, axis=-1, keepdims=True) - vhat * jnp.mean(dvh * vhat, axis=-1, keepdims=True))
            dva_ref[rows, :] = dva.astype(bf16)
            return carry

        lax.fori_loop(0, nch, out_step, 0)

    vm = pl.BlockSpec(memory_space=pltpu.VMEM)
    hbm = pl.BlockSpec(memory_space=pl.ANY)
    return pl.pallas_call(
        body, name="mix_bwd",
        out_shape=(jax.ShapeDtypeStruct((l, AW), bf16), jax.ShapeDtypeStruct((4, ACH, ACH), f32), jax.ShapeDtypeStruct((4, ACH, ACH), f32),
                   jax.ShapeDtypeStruct((1, AW), f32), jax.ShapeDtypeStruct((1, AW), f32)),
        in_specs=[hbm, hbm, vm, vm, vm], out_specs=(vm, vm, vm, vm, vm),
        scratch_shapes=[pltpu.VMEM((l, AW), bf16), pltpu.VMEM((l, AW), bf16), pltpu.VMEM((2, l, ACH), f32), pltpu.VMEM((2, l, ACH), f32),
                        pltpu.SemaphoreType.DMA((2,))],
        compiler_params=_cp(),
    )(p, dsv, ln_g, ln_b, ws_t)


def _mid(x, tgt, p, o_f, o_b, sv, gate, gf, gb_norm, w_pa, w_pb, w_out, tl):
    l = x.shape[0]

    def body(x_ref, t_ref, zb_ref, ua_ref, za_ref, g1_ref, g2_ref, of_ref, ob_ref, sv_ref, gate_ref, gf_ref, gbn_ref,
             wpa_ref, wpb_ref, wout_ref,
             dx1_ref, dzbua_ref, dzag_ref, dsv_ref, do_ref, dwout_bf, dwpa_bf, dwpb_bf, dgf_ref, dgate_ref, dgbn_ref, loss_ref,
             dwout_ref, dwpa_ref, dwpb_ref):
        @pl.when(pl.program_id(0) == 0)
        def _():
            for r in (dwout_ref, dwpa_ref, dwpb_ref, dgf_ref, dgate_ref, dgbn_ref, loss_ref):
                r[...] = jnp.zeros_like(r)

        o = of_ref[...] + ob_ref[...]
        rr = jnp.concatenate(
            [jnp.broadcast_to(lax.rsqrt(jnp.mean(o[:, h * HV:(h + 1) * HV] ** 2, axis=-1, keepdims=True) + EPS), (tl, HV))
             for h in range(NH)], axis=1)
        ohat = o * rr
        on = ohat * gbn_ref[...]
        szb, dszb = _silu_and_grad(zb_ref[...].astype(f32))
        tb = on * szb
        u = ua_ref[...].astype(f32)
        svv = sv_ref[...].astype(f32)
        sza, dsza = _silu_and_grad(za_ref[...].astype(f32))
        ta = u * svv * sza
        ya = _dot(ta, wpa_ref[...])
        yb = _dot(tb, wpb_ref[...])
        g1 = _sigmoid(g1_ref[...].astype(f32))
        g2 = _sigmoid(g2_ref[...].astype(f32))
        m = g1 * ya + g2 * yb
        y2 = _dot(m, wout_ref[...])
        x1 = x_ref[...] + gate_ref[...] * y2
        r1 = lax.rsqrt(jnp.mean(x1 * x1, axis=-1, keepdims=True) + EPS)
        x1n = x1 * r1
        err = x1n * gf_ref[...] - t_ref[...]
        loss_ref[...] += jnp.sum(jnp.sum(err * err, axis=-1, keepdims=True), axis=0, keepdims=True) * (0.5 / D)
        dout = err * (1.0 / D)
        dgf_ref[...] += jnp.sum(dout * x1n, axis=0, keepdims=True)
        dx1n = dout * gf_ref[...]
        dx1 = r1 * (dx1n - x1n * jnp.mean(dx1n * x1n, axis=-1, keepdims=True))
        dx1_ref[...] = dx1
        dgate_ref[...] += jnp.sum(dx1 * y2, axis=0, keepdims=True)
        dy2 = dx1 * gate_ref[...]
        dwout_ref[...] += _tn(m, dy2)
        dm = _nt(dy2, wout_ref[...])
        dya = dm * g1
        dyb = dm * g2
        dzag_ref[:, AW:AW + D] = (dm * ya * g1 * (1.0 - g1)).astype(bf16)
        dzag_ref[:, AW + D:] = (dm * yb * g2 * (1.0 - g2)).astype(bf16)
        dwpa_ref[...] += _tn(ta, dya)
        dta = _nt(dya, wpa_ref[...])
        dzbua_ref[:, AW:] = (dta * svv * sza).astype(bf16)
        dsv_ref[...] = (dta * u * sza).astype(bf16)
        dzag_ref[:, 0:AW] = (dta * u * svv * dsza).astype(bf16)
        dwpb_ref[...] += _tn(tb, dyb)
        dtb = _nt(dyb, wpb_ref[...])
        don = dtb * szb
        dzbua_ref[:, 0:AW] = (dtb * on * dszb).astype(bf16)
        dgbn_ref[...] += jnp.sum(don * ohat, axis=0, keepdims=True)
        doh = don * gbn_ref[...]
        prod = doh * ohat
        mh = jnp.concatenate(
            [jnp.broadcast_to(jnp.mean(prod[:, h * HV:(h + 1) * HV], axis=-1, keepdims=True), (tl, HV)) for h in range(NH)], axis=1)
        do_ref[...] = (rr * (doh - ohat * mh)).astype(bf16)

        @pl.when(pl.program_id(0) == l // tl - 1)
        def _():
            for acc, out in ((dwout_ref, dwout_bf), (dwpa_ref, dwpa_bf), (dwpb_ref, dwpb_bf)):
                out[...] = acc[...].astype(bf16)

    row = lambda w, j: pl.BlockSpec((tl, w), lambda i, j=j: (i, j))
    full = lambda *s: pl.BlockSpec(s, lambda i: (0,) * len(s))
    return pl.pallas_call(
        body, name="mid", grid=(l // tl,),
        out_shape=(jax.ShapeDtypeStruct((l, D), f32), jax.ShapeDtypeStruct((l, 2 * AW), bf16), jax.ShapeDtypeStruct((l, AW + 2 * D), bf16),
                   jax.ShapeDtypeStruct((l, AW), bf16), jax.ShapeDtypeStruct((l, VW), bf16),
                   jax.ShapeDtypeStruct((D, D), bf16), jax.ShapeDtypeStruct((AW, D), bf16), jax.ShapeDtypeStruct((VW, D), bf16),
                   jax.ShapeDtypeStruct((1, D), f32), jax.ShapeDtypeStruct((1, D), f32), jax.ShapeDtypeStruct((1, VW), f32),
                   jax.ShapeDtypeStruct((1, 1), f32)),
        scratch_shapes=[pltpu.VMEM((D, D), f32), pltpu.VMEM((AW, D), f32), pltpu.VMEM((VW, D), f32)],
        in_specs=[row(D, 0), row(D, 0), row(AW, PZB // AW), row(AW, PUA // AW), row(AW, PZA // AW), row(D, PG1 // D), row(D, PG2 // D),
                  row(VW, 0), row(VW, 0), row(AW, 0), full(1, D), full(1, D), full(1, VW), full(AW, D), full(VW, D), full(D, D)],
        out_specs=(row(D, 0), row(2 * AW, 0), row(AW + 2 * D, 0), row(AW, 0), row(VW, 0),
                   full(D, D), full(AW, D), full(VW, D), full(1, D), full(1, D), full(1, VW), full(1, 1)),
        compiler_params=_cp(("arbitrary",)),
    )(x, tgt, p, p, p, p, p, o_f, o_b, sv, gate, gf, gb_norm, w_pa, w_pb, w_out)


def _in_bwd(x, dx1, dqkv, dzbua, dva, dzag, dlr, w_pad, g, scale, tl, comm):
    l = x.shape[0]
    c_in, c_out, c_sems = comm.specs()

    def body(*refs):
        cin = refs[14:14 + len(c_in)]
        outs = refs[14 + len(c_in):]
        comm.run(cin, outs[4:4 + comm.n], outs[4 + comm.n:], l // tl, lambda: compute(*refs[:14], *outs[:4]))

    def compute(x_ref, dx1_ref, a_ref, b_ref, c_ref, e_ref, lr_ref, wa_ref, wb_ref, wc_ref, we_ref, wl_ref, g_ref, sc_ref,
                gx_ref, dsh_ref, dsc_ref, dg_ref):
        @pl.when(pl.program_id(0) == 0)
        def _():
            for r in (dsh_ref, dsc_ref, dg_ref):
                r[...] = jnp.zeros_like(r)

        dh = (_nt(a_ref[...], wa_ref[...]) + _nt(b_ref[...], wb_ref[...]) + _nt(c_ref[...], wc_ref[...]) + _nt(e_ref[...], we_ref[...])
              + _nt(lr_ref[...], wl_ref[...]))
        xv = x_ref[...]
        r = lax.rsqrt(jnp.mean(xv * xv, axis=-1, keepdims=True) + EPS)
        xn = xv * r
        gxn = jnp.sum(dh * xn, axis=0, keepdims=True)
        dsh_ref[...] += jnp.sum(dh, axis=0, keepdims=True)
        dsc_ref[...] += gxn * g_ref[...]
        dg_ref[...] += gxn * (1.0 + sc_ref[...])
        dxn = dh * (g_ref[...] * (1.0 + sc_ref[...]))
        gx_ref[...] = dx1_ref[...] + r * (dxn - xn * jnp.mean(dxn * xn, axis=-1, keepdims=True))

    row = lambda w: pl.BlockSpec((tl, w), lambda i: (i, 0))
    wcol = lambda w, j: pl.BlockSpec((D, w), lambda i, j=j: (0, j))
    vec = pl.BlockSpec((1, D), lambda i: (0, 0))
    return pl.pallas_call(
        body, name="in_bwd", grid=(l // tl,),
        out_shape=(jax.ShapeDtypeStruct((l, D), f32),) + (jax.ShapeDtypeStruct((1, D), f32),) * 3 + tuple(comm.outs),
        in_specs=[row(D), row(D), row(2 * KW + VW), row(2 * AW), row(AW), row(AW + 2 * D), row(LRW),
                  wcol(2 * KW + VW, 0), wcol(2 * AW, PZB // (2 * AW)), wcol(AW, PVA // AW), wcol(AW + 2 * D, PZA // (AW + 2 * D)),
                  wcol(LRW, PLR // LRW), vec, vec] + c_in,
        out_specs=(row(D), vec, vec, vec) + tuple(c_out), scratch_shapes=c_sems,
        compiler_params=_cp(("arbitrary",)),
    )(x, dx1, dqkv, dzbua, dva, dzag, dlr, w_pad, w_pad, w_pad, w_pad, w_pad, g, scale, *comm.ins)


def _tn_matmul(a, b, tl, name):
    l, m = a.shape
    n = b.shape[1]

    def body(a_ref, b_ref, o_ref):
        @pl.when(pl.program_id(0) == 0)
        def _():
            o_ref[...] = jnp.zeros_like(o_ref)

        o_ref[...] += _tn(a_ref[...], b_ref[...])

    return pl.pallas_call(
        body, name=name, grid=(l // tl,), out_shape=jax.ShapeDtypeStruct((m, n), f32),
        in_specs=[pl.BlockSpec((tl, m), lambda i: (i, 0)), pl.BlockSpec((tl, n), lambda i: (i, 0))],
        out_specs=pl.BlockSpec((m, n), lambda i: (0, 0)), compiler_params=_cp(("arbitrary",)),
    )(a, b)


def _pad_gate(w2, gb):
    z = jnp.zeros((RANK, KW), f32)
    tail = jnp.zeros((LRW - 2 * RANK, KW), f32)
    w2f = jnp.concatenate([w2[0], z, tail], axis=0)
    w2b = jnp.concatenate([z, w2[1], tail], axis=0)
    return w2f, w2b, gb[0:1], gb[1:2]


EARLY_A_ROWS = 256


class _NoExchange:
    def __init__(self, w_pa, w_pb, w_out):
        self.weights = (w_pa, w_pb, w_out)

    def gather_proj(self):
        return _Comm([], [], None)

    def proj_weights(self, got):
        return self.weights

    def early_a(self, dw_out, dw_pa, dw_pb, small, blocks):
        return _Comm([], [], None)

    def early_b(self, blocks):
        return _Comm([], [], None)

    def late(self, blocks, dgt):
        return _Comm([], [], None)


class _Exchanges:
    def __init__(self, pa, pb, wo):
        self.shards = (pa, pb, wo)

    def gather_proj(self):
        def plan(i, o):
            srcs = [lambda j, r=r: r for r in i]
            dsts = [lambda j: o[0].at[:, _lanes(j)], lambda j: o[1].at[:, _lanes(j)], lambda j: o[2].at[j]]
            return srcs, dsts, None
        sds = jax.ShapeDtypeStruct
        return _Comm(self.shards, [sds((AW, D), bf16), sds((VW, D), bf16), sds((NDEV, 128, D), bf16)], plan)

    def proj_weights(self, got):
        return got[0], got[1], got[2].reshape(D, D)

    def early_a(self, dw_out, dw_pa, dw_pb, small, blocks):
        def plan(i, o):
            srcs = [lambda j: i[0].at[j], lambda j: i[1].at[:, _lanes(j)], lambda j: i[2].at[:, _lanes(j)], lambda j: i[3],
                    lambda j: i[4].at[j]]
            dsts = [lambda j, r=r: r.at[j] for r in o]
            return srcs, dsts, [None, None, None, None, lambda j: j >= LATE_DESTS - 1]
        sds = jax.ShapeDtypeStruct
        return _Comm([dw_out.reshape(NDEV, 128, D), dw_pa, dw_pb, small, blocks],
                     [sds((NDEV, 128, D), bf16), sds((NDEV, AW, 128), bf16), sds((NDEV, VW, 128), bf16),
                      sds((NDEV,) + small.shape, f32), sds(blocks.shape, bf16)], plan)

    def early_b(self, blocks):
        def plan(i, o):
            return [lambda j: i[0].at[j]], [lambda j: o[0].at[j]], [lambda j: j >= LATE_DESTS - 1]
        return _Comm([blocks], [jax.ShapeDtypeStruct(blocks.shape, bf16)], plan)

    def late(self, blocks, dgt):
        def plan(i, o):
            srcs = [lambda j: i[0].at[jnp.minimum(j, LATE_DESTS - 1)], lambda j: i[1].at[j]]
            dsts = [lambda j, r=r: r.at[j] for r in o]
            return srcs, dsts, [lambda j: j < LATE_DESTS, None]
        sds = jax.ShapeDtypeStruct
        return _Comm([blocks, dgt], [sds((NDEV, D, SHARD), bf16), sds(dgt.shape, f32)], plan)


def _local_step(x, ctx, tgt, mod, modc, norm_g, w_pad, ln_g, ln_b, ws, bs, w2, gb, gb_norm, gf, xch):
    shift, scale, gate = mod[:, 0:D], mod[:, D:2 * D], mod[:, 2 * D:]
    shift_c, scale_c = modc[:, 0:D], modc[:, D:]
    w2f, w2b, gbf, gbb = _pad_gate(w2, gb)

    p, h, *got_proj = _in_proj(x, norm_g, scale, shift, w_pad, 256, xch.gather_proj())
    w_pa, w_pb, w_out = xch.proj_weights(got_proj)
    sc_f, sc_b = _ctx_fwd(ctx, norm_g, scale_c, shift_c, w_pad, w2f, w2b, gbf, gbb)
    o_f, st_f = _gla_fwd(p, w2f, gbf, sc_f, False, 512, "gla_fwd_f")
    o_b, st_b = _gla_fwd(p, w2b, gbb, sc_b, True, 512, "gla_fwd_b")
    sv = _mix_fwd(p, ln_g, ln_b, ws.astype(bf16), bs.T)
    (dx1, dzbua, dzag, dsv, do, dw_out, dw_pa, dw_pb, dgf, dgate, dgbn, loss) = _mid(
        x, tgt, p, o_f, o_b, sv, gate, gf, gb_norm, w_pa, w_pb, w_out, 256)
    dva, dws, dbs_acc, dln_g, dln_b = _mix_bwd(p, dsv, ln_g, ln_b, jnp.swapaxes(ws, 1, 2).astype(bf16))
    dw_zbua = _tn_matmul(h, dzbua, 512, "dw_zbua")
    dw_va = _tn_matmul(h, dva, 512, "dw_va")
    dw_zag = _tn_matmul(h, dzag, 512, "dw_zag")
    blocks_a = _pack_dw_early(dw_zbua, dw_va, dw_zag, 0, EARLY_A_ROWS, "pack_dw_early_a")
    blocks_b = _pack_dw_early(dw_zbua, dw_va, dw_zag, EARLY_A_ROWS, D - EARLY_A_ROWS, "pack_dw_early_b")
    small = _rows128(dln_g, dln_b, dws, jnp.sum(dbs_acc, axis=-1), dgbn, dgf, jnp.broadcast_to(loss, (1, 128)))

    dqkv_f, dlr_f, dw2f, dgbf, dsc_f, *got_a = _gla_bwd(p, do, st_f, w2f, gbf, None, False, 512, "gla_bwd_f",
                                                        xch.early_a(dw_out, dw_pa, dw_pb, small, blocks_a))
    dqkv, dlr, dw2b, dgbb, dsc_b, *got_b = _gla_bwd(p, do, st_b, w2b, gbb, (dqkv_f, dlr_f), True, 512, "gla_bwd_b",
                                                    xch.early_b(blocks_b))
    dwk_c, dwv_c, dwl_c, dmodc, dg_c, dw2c, dgbc = _ctx_bwd(ctx, norm_g, scale_c, shift_c, w_pad, w2f, w2b, gbf, gbb, dsc_f, dsc_b)
    dw_qkv = _tn_matmul(h, dqkv, 512, "dw_qkv")
    dw_lr = _tn_matmul(h, dlr, 512, "dw_lr")
    blocks_late = _pack_dw_late(dw_qkv, dw_lr, dwk_c, dwv_c, dwl_c)
    dw2 = jnp.stack([dw2f[0:RANK] + dw2c[0, 0:RANK], dw2b[RANK:2 * RANK] + dw2c[1, RANK:2 * RANK]])
    dgb = jnp.concatenate([dgbf + dgbc[0], dgbb + dgbc[1]], axis=0)
    dgt = jnp.concatenate([jnp.transpose(dw2.reshape(2, RANK, NDEV, 32), (2, 0, 1, 3)).reshape(NDEV, 2 * RANK * 32),
                           jnp.transpose(dgb.reshape(2, NDEV, 32), (1, 0, 2)).reshape(NDEV, 64),
                           jnp.zeros((NDEV, 64), f32)], axis=1).reshape(NDEV, 9, 128)
    gx, dshift, dscale, dg, *got_late = _in_bwd(x, dx1, dqkv, dzbua, dva, dzag, dlr, w_pad, norm_g, scale, 256,
                                                xch.late(blocks_late, dgt))
    return dict(loss=loss, gx=gx, dmod=jnp.concatenate([dshift, dscale, dgate], axis=1), dmodc=dmodc, dnorm_g=dg + dg_c,
                small=small, blocks_a=blocks_a, blocks_b=blocks_b, blocks_late=blocks_late, dw2=dw2, dgb=dgb,
                dw_pa=dw_pa, dw_pb=dw_pb, dw_out=dw_out, got_a=got_a, got_b=got_b, got_late=got_late)


def _rows128(*vs):
    out = []
    for t in vs:
        t = t.reshape(-1)
        pad = (-t.shape[0]) % 128
        out.append(jnp.pad(t, (0, pad)) if pad else t)
    return jnp.concatenate(out).reshape(-1, 128)


def kernel(x, c, ctx, c_ctx, w_mod, b_mod, norm_g, w_in, a_ln_g, a_ln_b, a_ws, a_bs, b_gate_w2, b_gate_b, b_norm_g, w_proj_a, w_proj_b, w_out, final_norm_g, loss_target, m_c_ctx, m_w_mod, m_b_mod, m_norm_g, m_w_in, m_a_ln_g, m_a_ln_b, m_a_ws, m_a_bs, m_b_gate_w2, m_b_gate_b, m_b_norm_g, m_w_proj_a, m_w_proj_b, m_w_out, m_final_norm_g, v_c_ctx, v_w_mod, v_b_mod, v_norm_g, v_w_in, v_a_ln_g, v_a_ln_b, v_a_ws, v_a_bs, v_b_gate_w2, v_b_gate_b, v_b_norm_g, v_w_proj_a, v_w_proj_b, v_w_out, v_final_norm_g):
    me = _me()
    ncol = w_mod.shape[2]

    gate_mine = _rows128(jnp.concatenate([b_gate_w2.reshape(-1), b_gate_b.reshape(-1)]))
    early, wg, gates = _gather_first(_rows128(c), w_in[0].astype(bf16), gate_mine)
    cs = jnp.concatenate([early.reshape(NDEV, D), c_ctx.reshape(1, D), jnp.zeros((7, D), f32)], axis=0)
    w_pad = _repack_w(wg)
    gflat = gates.reshape(NDEV, 9 * 128)
    w2 = jnp.transpose(gflat[:, 0:2 * RANK * 32].reshape(NDEV, 2, RANK, 32), (1, 2, 0, 3)).reshape(2, RANK, KW)
    gb = jnp.transpose(gflat[:, 2 * RANK * 32:2 * RANK * 32 + 64].reshape(NDEV, 2, 32), (1, 0, 2)).reshape(2, KW)

    bm_mine = lax.dynamic_slice(b_mod, (0, me * ncol), (1, ncol))
    mods = _all_gather(_mod_fwd(cs, w_mod[0], bm_mine), "gather_mod")
    mods = jnp.transpose(mods, (1, 0, 2)).reshape(16, 3 * D)
    mod = lax.dynamic_slice(mods, (me, 0), (1, 3 * D))
    modc = mods[8:9, 0:2 * D]

    xch = _Exchanges(w_proj_a[0].astype(bf16), w_proj_b[0].astype(bf16), w_out[0].astype(bf16))
    r = _local_step(x[0], ctx[0], loss_target[0], mod, modc, norm_g, w_pad, a_ln_g, a_ln_b, a_ws[0], a_bs[0], w2, gb,
                    b_norm_g, final_norm_g.reshape(1, D), xch)
    p_out, p_pa, p_pb, smalls_e, p_in_a = r["got_a"]
    (p_in_b,) = r["got_b"]
    p_in_late, p_gt = r["got_late"]

    def split(t, sizes, shapes):
        t = t.reshape(-1)
        out, o = [], 0
        for s, sh in zip(sizes, shapes):
            out.append(t[o:o + s].reshape(sh))
            o += s
        return out

    n_e = (AW + AW + 4 * ACH * ACH + AW + VW + D) // 128
    rep_e = _adam(smalls_e[:, 0:n_e], _rows128(a_ln_g, a_ln_b, a_ws, a_bs, b_norm_g, final_norm_g),
                  _rows128(m_a_ln_g, m_a_ln_b, m_a_ws, m_a_bs, m_b_norm_g, m_final_norm_g),
                  _rows128(v_a_ln_g, v_a_ln_b, v_a_ws, v_a_bs, v_b_norm_g, v_final_norm_g), n_e, "adam_rep_early")
    rep_e = [split(t, (AW, AW, 4 * ACH * ACH, AW, VW, D), ((1, AW), (1, AW), (1, 4, ACH, ACH), (1, 4, ACH), (1, VW), (D,)))
             for t in rep_e]
    losses = smalls_e[:, n_e, 0]
    loss = losses[0]
    for i in range(1, NDEV):
        loss = loss + losses[i]

    dbm = r["dmod"] + jnp.concatenate([r["dmodc"], jnp.zeros((1, D), f32)], axis=1)
    smalls_l = _all_gather(_rows128(r["dnorm_g"], dbm, r["dmod"], r["dmodc"]), "gather_small")
    n_l = (D + 3 * D) // 128
    rep_l = _adam(smalls_l[:, 0:n_l], _rows128(norm_g, b_mod), _rows128(m_norm_g, m_b_mod), _rows128(v_norm_g, v_b_mod), n_l,
                  "adam_rep_late")
    rep_l = [split(t, (D, 3 * D), ((1, D), (1, 3 * D))) for t in rep_l]
    tail = smalls_l[:, n_l:].reshape(NDEV, -1)
    dmods = tail[:, 0:3 * D]
    dmodc_all = tail[:, 3 * D:5 * D]
    dmodc_tot = dmodc_all[0:1]
    for i in range(1, NDEV):
        dmodc_tot = dmodc_tot + dmodc_all[i:i + 1]

    dm_rows = jnp.concatenate([dmods, jnp.concatenate([dmodc_tot, jnp.zeros((1, D), f32)], axis=1), jnp.zeros((7, 3 * D), f32)], axis=0)
    dm_mine = lax.dynamic_slice(dm_rows, (0, me * ncol), (16, ncol))
    g_wmod, gc_part = _mod_bwd(cs, dm_mine, w_mod[0])
    wm = _adam(g_wmod[None], w_mod, m_w_mod, v_w_mod, 256, "adam_wmod")
    gcs = _all_gather(gc_part.reshape(8, 128), "gather_cctx")
    cc = _adam(gcs, c_ctx.reshape(8, 128), m_c_ctx.reshape(8, 128), v_c_ctx.reshape(8, 128), 8, "adam_cctx")

    a_in = _adam_w_in(p_in_a, p_in_b, p_in_late, w_in, m_w_in, v_w_in)
    a_pa = _adam(p_pa, w_proj_a, m_w_proj_a, v_w_proj_a, AW, "adam_w_pa")
    a_pb = _adam(p_pb, w_proj_b, m_w_proj_b, v_w_proj_b, VW, "adam_w_pb")
    a_out = _adam(p_out, w_out, m_w_out, v_w_out, 128, "adam_w_out")
    gate_m = _rows128(jnp.concatenate([m_b_gate_w2.reshape(-1), m_b_gate_b.reshape(-1)]))
    gate_v = _rows128(jnp.concatenate([v_b_gate_w2.reshape(-1), v_b_gate_b.reshape(-1)]))
    a_gt = [t.reshape(-1) for t in _adam(p_gt, gate_mine, gate_m, gate_v, 9, "adam_gate")]
    nw2 = 2 * RANK * 32
    sh = [(a_in[k], a_pa[k], a_pb[k], a_out[k], a_gt[k][0:nw2].reshape(1, 2, RANK, 32),
           a_gt[k][nw2:nw2 + 64].reshape(1, 2, 32)) for k in range(4)]

    outs = [loss, r["gx"][None]]
    for k in range(4):
        lg, lb, aws, abs_, bng, fng = rep_e[k]
        n_g, bmod = rep_l[k]
        s_in, s_pa, s_pb, s_out, s_w2, s_gb = sh[k]
        outs += [cc[k].reshape(D), wm[k], bmod, n_g, s_in, lg, lb, aws, abs_, s_w2, s_gb, bng, s_pa, s_pb, s_out, fng]
    return tuple(outs)
```

```python
import functools

import jax
import jax.numpy as jnp
from jax import lax
from jax.experimental import pallas as pl
from jax.experimental.pallas import tpu as pltpu

f32, bf16 = jnp.float32, jnp.bfloat16

D = 1024
CTX = 256
EPS = 1e-6
AW = 512
ACH = 128
GW = 64
KW = 256
VW = 512
NH = 4
HK = 64
HV = 128
RANK = 16
TAU = 16.0
CH = 64
QSCALE = HK ** -0.5
INW = 5152
NDEV = 8

PQ, PK, PV, PZB, PUA, PVA, PZA, PG1, PG2, PLR, PW = 0, 256, 512, 1024, 1536, 2048, 2560, 3072, 4096, 5120, 5248
LRW = 128

ADAM_LR, ADAM_B1, ADAM_B2, ADAM_EPS, ADAM_WD, ADAM_STEP = 0.001, 0.9, 0.999, 1e-08, 0.01, 10

VMEM_LIMIT = 56 * 1024 * 1024
MESH = pl.DeviceIdType.MESH


def _cp(sem=None):
    return pltpu.CompilerParams(dimension_semantics=sem, vmem_limit_bytes=VMEM_LIMIT)


def _dot(a, b):
    return jnp.dot(a.astype(bf16), b.astype(bf16), preferred_element_type=f32)


def _nt(a, b):
    return lax.dot_general(a.astype(bf16), b.astype(bf16), (((1,), (1,)), ((), ())), preferred_element_type=f32)


def _tn(a, b):
    return lax.dot_general(a.astype(bf16), b.astype(bf16), (((0,), (0,)), ((), ())), preferred_element_type=f32)


def _dot_hi(a, b):
    return jnp.dot(a, b, preferred_element_type=f32, precision=lax.Precision.HIGHEST)


def _sigmoid(x):
    return 1.0 / (1.0 + jnp.exp(-x))


def _log_sigmoid(x):
    return jnp.minimum(x, 0.0) - jnp.log(1.0 + jnp.exp(-jnp.abs(x)))


def _silu_and_grad(z):
    s = _sigmoid(z)
    return z * s, s * (1.0 + z * (1.0 - s))


def _me():
    return 4 * lax.axis_index("x") + 2 * lax.axis_index("y") + lax.axis_index("c")


def _peer(k):
    x, y, c = lax.axis_index("x"), lax.axis_index("y"), lax.axis_index("c")
    px = 1 - x if k & 4 else x
    py = 1 - y if k & 2 else y
    pc = 1 - c if k & 1 else c
    return (px, py, pc), 4 * px + 2 * py + pc


def _all_gather(v, name):
    r, c = v.shape

    def body(v_ref, out_ref, send_sems, recv_sems, local_sem):
        me = _me()
        mine = pltpu.make_async_copy(v_ref, out_ref.at[me], local_sem)
        mine.start()
        sends, recvs = [], []
        for k in range(1, NDEV):
            dev, idx = _peer(k)
            sends.append(pltpu.make_async_remote_copy(
                src_ref=v_ref, dst_ref=out_ref.at[me], send_sem=send_sems.at[k - 1], recv_sem=recv_sems.at[k - 1],
                device_id=dev, device_id_type=MESH))
            recvs.append(pltpu.make_async_remote_copy(
                src_ref=v_ref, dst_ref=out_ref.at[idx], send_sem=send_sems.at[k - 1], recv_sem=recv_sems.at[k - 1],
                device_id=dev, device_id_type=MESH))
        for cp in sends:
            cp.start()
        for cp in recvs:
            cp.wait_recv()
        for cp in sends:
            cp.wait_send()
        mine.wait()

    return pl.pallas_call(
        body, name=name, out_shape=jax.ShapeDtypeStruct((NDEV, r, c), v.dtype),
        in_specs=[pl.BlockSpec(memory_space=pl.ANY)], out_specs=pl.BlockSpec(memory_space=pl.ANY),
        scratch_shapes=[pltpu.SemaphoreType.DMA((NDEV - 1,)), pltpu.SemaphoreType.DMA((NDEV - 1,)), pltpu.SemaphoreType.DMA(())],
    )(v)


def _fanout(srcs, dsts, send_sems, recv_sems, local_sems, owners=None):
    me = _me()
    n = len(srcs)
    owns = lambda a, j: True if owners is None or owners[a] is None else owners[a](j)

    def guarded(cond, fn):
        if cond is True:
            fn()
        else:
            pl.when(cond)(fn)

    def copies(with_recvs):
        local = [pltpu.make_async_copy(srcs[a](me), dsts[a](me), local_sems.at[a]) for a in range(n)]
        sends, recvs = [], []
        for k in range(1, NDEV):
            dev, idx = _peer(k)
            for a in range(n):
                s = (k - 1) * n + a
                sends.append((owns(a, idx), pltpu.make_async_remote_copy(
                    src_ref=srcs[a](idx), dst_ref=dsts[a](me), send_sem=send_sems.at[s], recv_sem=recv_sems.at[s],
                    device_id=dev, device_id_type=MESH)))
                if with_recvs:
                    recvs.append((owns(a, me), pltpu.make_async_remote_copy(
                        src_ref=srcs[a](idx), dst_ref=dsts[a](idx), send_sem=send_sems.at[s], recv_sem=recv_sems.at[s],
                        device_id=dev, device_id_type=MESH)))
        return local, sends, recvs

    def start():
        local, sends, _ = copies(False)
        for a, cp in enumerate(local):
            guarded(owns(a, me), cp.start)
        for cond, cp in sends:
            guarded(cond, cp.start)

    def finish():
        local, sends, recvs = copies(True)
        for cond, cp in recvs:
            guarded(cond, cp.wait_recv)
        for cond, cp in sends:
            guarded(cond, cp.wait_send)
        for a, cp in enumerate(local):
            guarded(owns(a, me), cp.wait)

    return start, finish


class _Comm:
    def __init__(self, ins, outs, plan):
        self.ins, self.outs, self.plan = list(ins), list(outs), plan
        self.n = len(self.outs)

    def specs(self):
        hbm = pl.BlockSpec(memory_space=pl.ANY)
        return [hbm] * len(self.ins), [hbm] * self.n, _fanout_sems(self.n) if self.n else []

    def run(self, in_refs, out_refs, sems, nsteps, compute):
        if not self.n:
            compute()
            return

        def hooks():
            srcs, dsts, owners = self.plan(in_refs, out_refs)
            return _fanout(srcs, dsts, sems[0], sems[1], sems[2], owners)

        pl.when(pl.program_id(0) == 0)(lambda: hooks()[0]())
        compute()
        pl.when(pl.program_id(0) == nsteps - 1)(lambda: hooks()[1]())


def _fanout_sems(n):
    return [pltpu.SemaphoreType.DMA(((NDEV - 1) * n,)), pltpu.SemaphoreType.DMA(((NDEV - 1) * n,)), pltpu.SemaphoreType.DMA((n,))]


def _lanes(j):
    return pl.ds(pl.multiple_of(j * 128, 128), 128)


def _gather_first(c_rows, wi, gate):
    def body(c_ref, wi_ref, g_ref, oc, owi, og, send_sems, recv_sems, local_sems):
        x, y, c = lax.axis_index("x"), lax.axis_index("y"), lax.axis_index("c")
        sibling = (x, y, 1 - c)
        chips = [(1 - x, y), (x, 1 - y), (1 - x, 1 - y)]
        index = lambda px, py, pc: 4 * px + 2 * py + pc
        arrays = ((c_ref, oc), (wi_ref, owi), (g_ref, og))
        n = len(arrays)

        def copy(a, k, block, to, own=False):
            src, out = arrays[a]
            return pltpu.make_async_remote_copy(
                src_ref=src if own else out.at[index(*block)], dst_ref=out.at[index(*block)],
                send_sem=send_sems.at[k * n + a], recv_sem=recv_sems.at[k * n + a], device_id=to, device_id_type=MESH)

        mine = [pltpu.make_async_copy(src, out.at[index(x, y, c)], local_sems.at[a]) for a, (src, out) in enumerate(arrays)]
        first = [copy(a, 0, (x, y, c), sibling, own=True) for a in range(n)]
        first += [copy(a, 1 + j, (x, y, c), (*chip, c), own=True) for j, chip in enumerate(chips) for a in range(n)]
        for cp in mine + first:
            cp.start()
        passed = []
        for j, chip in enumerate(chips):
            for a in range(n):
                copy(a, 1 + j, (*chip, c), (x, y, c)).wait_recv()
            for a in range(n):
                cp = copy(a, 4 + j, (*chip, c), sibling)
                cp.start()
                passed.append(cp)
        for a in range(n):
            copy(a, 0, sibling, (x, y, c)).wait_recv()
        for j, chip in enumerate(chips):
            for a in range(n):
                copy(a, 4 + j, (*chip, 1 - c), (x, y, c)).wait_recv()
        for cp in first + passed:
            cp.wait_send()
        for cp in mine:
            cp.wait()

    hbm = pl.BlockSpec(memory_space=pl.ANY)
    return pl.pallas_call(
        body, name="gather_first",
        out_shape=(jax.ShapeDtypeStruct((NDEV,) + c_rows.shape, f32), jax.ShapeDtypeStruct((NDEV,) + wi.shape, bf16),
                   jax.ShapeDtypeStruct((NDEV,) + gate.shape, f32)),
        in_specs=[hbm] * 3, out_specs=(hbm,) * 3, scratch_shapes=_fanout_sems(3),
    )(c_rows, wi, gate)


SHARD = INW // NDEV
ROWS_RP = 128


def _overlap(lo, hi, a, b):
    s, e = max(lo, a), min(hi, b)
    return (s, e) if s < e else None


def _repack_w(wg):
    segs = ((0, 1024, PQ), (1024, 1024 + 2 * RANK, PLR), (1024 + 2 * RANK, INW, PZB))

    def body(g_ref, o_ref):
        for j in range(NDEV):
            lo, hi = j * SHARD, (j + 1) * SHARD
            for a, b, pad0 in segs:
                ov = _overlap(lo, hi, a, b)
                if ov:
                    s, e = ov
                    o_ref[:, pad0 + s - a:pad0 + e - a] = g_ref[j, :, s - lo:e - lo]
        o_ref[:, PLR + 2 * RANK:PW] = jnp.zeros((ROWS_RP, PW - PLR - 2 * RANK), bf16)

    return pl.pallas_call(
        body, name="repack_w", grid=(D // ROWS_RP,), out_shape=jax.ShapeDtypeStruct((D, PW), bf16),
        in_specs=[pl.BlockSpec((NDEV, ROWS_RP, SHARD), lambda i: (0, i, 0))],
        out_specs=pl.BlockSpec((ROWS_RP, PW), lambda i: (i, 0)), compiler_params=_cp(("arbitrary",)),
    )(wg)


LATE_END = 1024 + 2 * RANK
LATE_DESTS = 2


def _pack_blocks(o_ref, srcs, dests, dtype):
    for n, j in enumerate(dests):
        lo, hi = j * SHARD, (j + 1) * SHARD
        done = lo
        for a, b, src in srcs:
            ov = _overlap(lo, hi, a, b)
            if ov:
                s, e = ov
                if s > done:
                    o_ref[n, :, done - lo:s - lo] = jnp.zeros((ROWS_RP, s - done), dtype)
                o_ref[n, :, s - lo:e - lo] = src[:, s - a:e - a].astype(dtype)
                done = e
        if done < hi:
            o_ref[n, :, done - lo:hi - lo] = jnp.zeros((ROWS_RP, hi - done), dtype)


def _pack_dw_early(dw_zbua, dw_va, dw_zag, row0, nrows, name):
    def body(zbua_ref, va_ref, zag_ref, o_ref):
        _pack_blocks(o_ref, ((LATE_END, 2080, zbua_ref), (2080, 2592, va_ref), (2592, INW, zag_ref)), range(NDEV), bf16)

    row = lambda w: pl.BlockSpec((ROWS_RP, w), lambda i: (i + row0 // ROWS_RP, 0))
    return pl.pallas_call(
        body, name=name, grid=(nrows // ROWS_RP,), out_shape=jax.ShapeDtypeStruct((NDEV, nrows, SHARD), bf16),
        in_specs=[row(2 * AW), row(AW), row(AW + 2 * D)],
        out_specs=pl.BlockSpec((NDEV, ROWS_RP, SHARD), lambda i: (0, i, 0)), compiler_params=_cp(("arbitrary",)),
    )(dw_zbua, dw_va, dw_zag)


def _pack_dw_late(dw_qkv, dw_lr, dwk_c, dwv_c, dwl_c):
    def body(qkv_ref, lr_ref, kc_ref, vc_ref, lc_ref, o_ref):
        qkv = qkv_ref[...] + jnp.concatenate([jnp.zeros((ROWS_RP, KW), f32), kc_ref[...], vc_ref[...]], axis=1)
        lr = lr_ref[...] + lc_ref[...]
        _pack_blocks(o_ref, ((0, 1024, qkv), (1024, LATE_END, lr)), range(LATE_DESTS), bf16)

    row = lambda w: pl.BlockSpec((ROWS_RP, w), lambda i: (i, 0))
    return pl.pallas_call(
        body, name="pack_dw_late", grid=(D // ROWS_RP,), out_shape=jax.ShapeDtypeStruct((LATE_DESTS, D, SHARD), bf16),
        in_specs=[row(2 * KW + VW), row(LRW), row(KW), row(VW), row(LRW)],
        out_specs=pl.BlockSpec((LATE_DESTS, ROWS_RP, SHARD), lambda i: (0, i, 0)), compiler_params=_cp(("arbitrary",)),
    )(dw_qkv, dw_lr, dwk_c, dwv_c, dwl_c)


def _mod_fwd(cs, wm, bm):
    def body(cs_ref, wm_ref, bm_ref, o_ref):
        s, _ = _silu_and_grad(cs_ref[...])
        o_ref[...] = _dot_hi(s, wm_ref[...]) + bm_ref[...]

    return pl.pallas_call(body, name="mod_fwd", out_shape=jax.ShapeDtypeStruct((16, wm.shape[1]), f32),
                          compiler_params=_cp())(cs, wm, bm)


def _mod_bwd(cs, dm, wm):
    def body(cs_ref, dm_ref, wm_ref, gw_ref, gc_ref):
        s, ds = _silu_and_grad(cs_ref[...])
        gw_ref[...] = lax.dot_general(s, dm_ref[...], (((0,), (0,)), ((), ())), preferred_element_type=f32,
                                      precision=lax.Precision.HIGHEST)
        part = lax.dot_general(dm_ref[8:9, :], wm_ref[...], (((1,), (1,)), ((), ())), preferred_element_type=f32,
                               precision=lax.Precision.HIGHEST)
        gc_ref[...] = part * ds[8:9, :]

    return pl.pallas_call(body, name="mod_bwd",
                          out_shape=(jax.ShapeDtypeStruct(wm.shape, f32), jax.ShapeDtypeStruct((1, D), f32)),
                          compiler_params=_cp())(cs, dm, wm)


def _adam_update(g, w_ref, m_ref, v_ref, go_ref, d_ref, mo_ref, vo_ref):
    c1 = 1.0 / (1.0 - ADAM_B1 ** ADAM_STEP)
    c2 = 1.0 / (1.0 - ADAM_B2 ** ADAM_STEP)
    mn = ADAM_B1 * m_ref[...] + (1.0 - ADAM_B1) * g
    vn = ADAM_B2 * v_ref[...] + (1.0 - ADAM_B2) * (g * g)
    go_ref[...] = g
    mo_ref[...] = mn
    vo_ref[...] = vn
    d_ref[...] = -ADAM_LR * ((mn * c1) / (jnp.sqrt(vn * c2) + ADAM_EPS) + ADAM_WD * w_ref[...])


def _adam_w_in(parts_a, parts_b, parts_late, w, m, v):
    na = EARLY_A_ROWS // ROWS_RP

    def body(a_ref, b_ref, l_ref, w_ref, m_ref, v_ref, go_ref, d_ref, mo_ref, vo_ref):
        me = _me()
        first = pl.program_id(0) < na
        early = jnp.where(first, a_ref[0], b_ref[0]).astype(f32)
        late = l_ref[0].astype(f32)
        for i in range(1, NDEV):
            early = early + jnp.where(first, a_ref[i], b_ref[i]).astype(f32)
            late = late + l_ref[i].astype(f32)
        g = jnp.where(me >= LATE_DESTS - 1, early, 0.0) + jnp.where(me < LATE_DESTS, late, 0.0)
        _adam_update(g, w_ref, m_ref, v_ref, go_ref, d_ref, mo_ref, vo_ref)

    blk = pl.BlockSpec((None, ROWS_RP, SHARD), lambda i: (0, i, 0))
    return pl.pallas_call(
        body, name="adam_w_in", grid=(D // ROWS_RP,), out_shape=tuple(jax.ShapeDtypeStruct((1, D, SHARD), f32) for _ in range(4)),
        in_specs=[pl.BlockSpec((NDEV, ROWS_RP, SHARD), lambda i: (0, jnp.minimum(i, na - 1), 0)),
                  pl.BlockSpec((NDEV, ROWS_RP, SHARD), lambda i: (0, jnp.maximum(i - na, 0), 0)),
                  pl.BlockSpec((NDEV, ROWS_RP, SHARD), lambda i: (0, i, 0)), blk, blk, blk],
        out_specs=(blk, blk, blk, blk), compiler_params=_cp(("arbitrary",)),
    )(parts_a, parts_b, parts_late, w, m, v)


def _adam(parts, w, m, v, rows, name):
    p, r, c = parts.shape
    lead = w.ndim - 2

    def body(g_ref, w_ref, m_ref, v_ref, go_ref, d_ref, mo_ref, vo_ref):
        g = g_ref[0].astype(f32)
        for i in range(1, p):
            g = g + g_ref[i].astype(f32)
        _adam_update(g, w_ref, m_ref, v_ref, go_ref, d_ref, mo_ref, vo_ref)

    blk = pl.BlockSpec((None,) * lead + (rows, c), lambda i: (0,) * lead + (i, 0))
    return pl.pallas_call(
        body, name=name, grid=(r // rows,), out_shape=tuple(jax.ShapeDtypeStruct(w.shape, f32) for _ in range(4)),
        in_specs=[pl.BlockSpec((p, rows, c), lambda i: (0, i, 0)), blk, blk, blk], out_specs=(blk, blk, blk, blk),
        compiler_params=_cp(("arbitrary",)),
    )(parts, w, m, v)


def _in_proj(x, g, scale, shift, w_pad, tl, comm):
    l = x.shape[0]
    c_in, c_out, c_sems = comm.specs()

    def body(*refs):
        x_ref, g_ref, sc_ref, sh_ref, w_ref = refs[:5]
        cin = refs[5:5 + len(c_in)]
        p_ref, h_ref = refs[5 + len(c_in):7 + len(c_in)]
        cout = refs[7 + len(c_in):7 + len(c_in) + comm.n]
        sems = refs[7 + len(c_in) + comm.n:]

        def compute():
            xv = x_ref[...]
            r = lax.rsqrt(jnp.mean(xv * xv, axis=-1, keepdims=True) + EPS)
            h = (xv * r) * (g_ref[...] * (1.0 + sc_ref[...])) + sh_ref[...]
            hb = h.astype(bf16)
            h_ref[...] = hb
            p_ref[...] = jnp.dot(hb, w_ref[...], preferred_element_type=f32).astype(bf16)

        comm.run(cin, cout, sems, l // tl, compute)

    vec = pl.BlockSpec((1, D), lambda i: (0, 0))
    return pl.pallas_call(
        body, name="in_proj", grid=(l // tl,),
        out_shape=(jax.ShapeDtypeStruct((l, PW), bf16), jax.ShapeDtypeStruct((l, D), bf16)) + tuple(comm.outs),
        in_specs=[pl.BlockSpec((tl, D), lambda i: (i, 0)), vec, vec, vec, pl.BlockSpec((D, PW), lambda i: (0, 0))] + c_in,
        out_specs=(pl.BlockSpec((tl, PW), lambda i: (i, 0)), pl.BlockSpec((tl, D), lambda i: (i, 0))) + tuple(c_out),
        scratch_shapes=c_sems, compiler_params=_cp(("arbitrary",)),
    )(x, g, scale, shift, w_pad, *comm.ins)


def _tri(rev):
    i = lax.broadcasted_iota(jnp.int32, (CH, CH), 0)
    j = lax.broadcasted_iota(jnp.int32, (CH, CH), 1)
    return jnp.where((j >= i) if rev else (j <= i), 1.0, 0.0).astype(f32)


def _head_masks():
    lane = lax.broadcasted_iota(jnp.int32, (1, KW), 1) // HK
    return [jnp.where(lane == h, 1.0, 0.0).astype(f32) for h in range(NH)]


def _block_diag():
    r = lax.broadcasted_iota(jnp.int32, (VW, KW), 0) // HV
    c = lax.broadcasted_iota(jnp.int32, (VW, KW), 1) // HK
    return jnp.where(r == c, 1.0, 0.0).astype(f32)


def _decay(lr, w2, gb, tri, rev):
    logits = _dot(lr, w2) + gb
    a = _log_sigmoid(logits) * (1.0 / TAU)
    c = _dot_hi(tri, a)
    cl = c[0:1, :] if rev else c[CH - 1:CH, :]
    return logits, c, cl


def _stack_heads(t, hm):
    return jnp.concatenate([t * hm[h] for h in range(NH)], axis=0)


def _chunk_fwd(q, k, v, c, cl, st, tri, hm, bd):
    tri4 = jnp.concatenate([tri] * NH, axis=0)
    qd = q * jnp.exp(c) * QSCALE
    kd = k * jnp.exp(-c)
    kdec = k * jnp.exp(cl - c)
    pst = _nt(_stack_heads(qd, hm), kd) * tri4
    intra = jnp.concatenate([_dot(pst[h * CH:(h + 1) * CH], v[:, h * HV:(h + 1) * HV]) for h in range(NH)], axis=1)
    o = _nt(qd, st) + intra
    st_new = st * jnp.exp(cl) + bd * _tn(v, kdec)
    return o, st_new


def _state_fwd(k, v, c, cl, st, bd):
    return st * jnp.exp(cl) + bd * _tn(v, k * jnp.exp(cl - c))


def _chunk_bwd(q, k, v, c, cl, st0, dst, do, tri, trit, hm, bd):
    ecl = jnp.exp(cl)
    edec = jnp.exp(cl - c)
    kdec = k * edec
    dv = _nt(kdec, dst)
    dkdec = _dot(v, dst)
    dcl = jnp.sum(dst * st0, axis=0, keepdims=True) * ecl + jnp.sum(dkdec * kdec, axis=0, keepdims=True)
    dc = -dkdec * kdec
    dk = dkdec * edec
    dst0 = dst * ecl
    dq = None
    if q is not None:
        tri4 = jnp.concatenate([tri] * NH, axis=0)
        ec, enc = jnp.exp(c), jnp.exp(-c)
        qd = q * ec * QSCALE
        kd = k * enc
        qs = _stack_heads(qd, hm)
        pst = _nt(qs, kd) * tri4
        dpst = jnp.concatenate([_nt(do[:, h * HV:(h + 1) * HV], v[:, h * HV:(h + 1) * HV]) for h in range(NH)], axis=0) * tri4
        dv = dv + jnp.concatenate([_tn(pst[h * CH:(h + 1) * CH], do[:, h * HV:(h + 1) * HV]) for h in range(NH)], axis=1)
        dqd = _dot(do, st0)
        for h in range(NH):
            dqd = dqd + hm[h] * _dot(dpst[h * CH:(h + 1) * CH], kd)
        dkd = _tn(dpst, qs)
        dst0 = dst0 + bd * _tn(do, qd)
        dc = dc + dqd * qd - dkd * kd
        dq = dqd * ec * QSCALE
        dk = dk + dkd * enc
    da = _dot_hi(trit, dc) + dcl
    return dq, dk, dv, da, dst0


def _bdot(a, b):
    return lax.dot_general(a.astype(bf16), b.astype(bf16), (((2,), (1,)), ((0,), (0,))), preferred_element_type=f32)


def _bnt(a, b):
    return lax.dot_general(a.astype(bf16), b.astype(bf16), (((2,), (2,)), ((0,), (0,))), preferred_element_type=f32)


def _btn(a, b):
    return lax.dot_general(a.astype(bf16), b.astype(bf16), (((1,), (1,)), ((0,), (0,))), preferred_element_type=f32)


def _scan_chunks(x, rev):
    t = x.shape[0]
    pos = lax.broadcasted_iota(jnp.int32, x.shape, 0) % CH
    s = 1
    while s < CH:
        if rev:
            x = x + jnp.where(pos < CH - s, pltpu.roll(x, t - s, 0), 0.0)
        else:
            x = x + jnp.where(pos >= s, pltpu.roll(x, s, 0), 0.0)
        s *= 2
    return x


class _Tile:
    pass


def _tile_prep(q_ref, k_ref, lr_ref, w2_ref, gb_ref, rev, nc):
    t = _Tile()
    tg = nc * CH
    t.logits = _dot(lr_ref[...], w2_ref[...]) + gb_ref[...]
    c = _scan_chunks(_log_sigmoid(t.logits) * (1.0 / TAU), rev).reshape(nc, CH, KW)
    cl = c[:, 0:1, :] if rev else c[:, CH - 1:CH, :]
    k = k_ref[...].astype(f32).reshape(nc, CH, KW)
    t.ec, t.enc, t.edec, t.ecl = jnp.exp(c), jnp.exp(-c), jnp.exp(cl - c), jnp.exp(cl)
    t.qd = q_ref[...].astype(f32).reshape(nc, CH, KW) * t.ec * QSCALE
    t.kd = k * t.enc
    t.kdec = k * t.edec
    hm = _head_masks()
    t.tri4 = jnp.concatenate([_tri(rev)] * NH, axis=0)[None]
    t.qs = jnp.concatenate([t.qd * hm[h] for h in range(NH)], axis=1)
    t.pst = _bnt(t.qs, t.kd) * t.tri4
    return t


def _gla_fwd(p, w2p, gb, s0, rev, tg, name):
    l = p.shape[0]
    nb, nc = l // tg, tg // CH

    def body(q_ref, k_ref, v_ref, lr_ref, w2_ref, gb_ref, s0_ref, o_ref, st_ref, st):
        @pl.when(pl.program_id(0) == 0)
        def _():
            st[...] = s0_ref[...]

        t = _tile_prep(q_ref, k_ref, lr_ref, w2_ref, gb_ref, rev, nc)
        v = v_ref[...].reshape(nc, CH, VW)
        intra = jnp.concatenate([_bdot(t.pst[:, h * CH:(h + 1) * CH], v[:, :, h * HV:(h + 1) * HV]) for h in range(NH)], axis=2)
        kv = _btn(v, t.kdec) * _block_diag()[None]
        s = st[...]
        for n in (range(nc - 1, -1, -1) if rev else range(nc)):
            st_ref[n] = s
            s = s * t.ecl[n] + kv[n]
        st[...] = s
        o_ref[...] = (_bnt(t.qd, st_ref[...]) + intra).reshape(tg, VW)

    blk = (lambda i: nb - 1 - i) if rev else (lambda i: i)
    return pl.pallas_call(
        body, name=name, grid=(nb,),
        out_shape=(jax.ShapeDtypeStruct((l, VW), f32), jax.ShapeDtypeStruct((l // CH, VW, KW), f32)),
        in_specs=[pl.BlockSpec((tg, KW), lambda i: (blk(i), PQ // KW)), pl.BlockSpec((tg, KW), lambda i: (blk(i), PK // KW)),
                  pl.BlockSpec((tg, VW), lambda i: (blk(i), PV // VW)), pl.BlockSpec((tg, LRW), lambda i: (blk(i), PLR // LRW)),
                  pl.BlockSpec((LRW, KW), lambda i: (0, 0)), pl.BlockSpec((1, KW), lambda i: (0, 0)),
                  pl.BlockSpec((VW, KW), lambda i: (0, 0))],
        out_specs=(pl.BlockSpec((tg, VW), lambda i: (blk(i), 0)), pl.BlockSpec((nc, VW, KW), lambda i: (blk(i), 0, 0))),
        scratch_shapes=[pltpu.VMEM((VW, KW), f32)],
        compiler_params=_cp(("arbitrary",)),
    )(p, p, p, p, w2p, gb, s0)


def _gla_bwd(p, do, states, w2p, gb, prev, rev, tg, name, comm):
    l = p.shape[0]
    nb, nc = l // tg, tg // CH
    out_dt = f32 if prev is None else bf16
    c_in, c_out, c_sems = comm.specs()

    def body(*refs):
        q_ref, k_ref, v_ref, lr_ref, do_ref, st_ref, w2_ref, gb_ref = refs[:8]
        refs = refs[8:]
        if prev is not None:
            pq_ref, pl_ref = refs[:2]
            refs = refs[2:]
        cin, refs = refs[:len(c_in)], refs[len(c_in):]
        dqkv_ref, dlr_ref, dw2_ref, dgb_ref, ds0_ref = refs[:5]
        cout, dst, ds_buf, sems = refs[5:5 + comm.n], refs[5 + comm.n], refs[6 + comm.n], refs[7 + comm.n:]
        comm.run(cin, cout, sems, nb, lambda: compute(q_ref, k_ref, v_ref, lr_ref, do_ref, st_ref, w2_ref, gb_ref,
                                                      pq_ref if prev is not None else None, pl_ref if prev is not None else None,
                                                      dqkv_ref, dlr_ref, dw2_ref, dgb_ref, ds0_ref, dst, ds_buf))

    def compute(q_ref, k_ref, v_ref, lr_ref, do_ref, st_ref, w2_ref, gb_ref, pq_ref, pl_ref,
                dqkv_ref, dlr_ref, dw2_ref, dgb_ref, ds0_ref, dst, ds_buf):
        @pl.when(pl.program_id(0) == 0)
        def _():
            dst[...] = jnp.zeros_like(dst)
            dw2_ref[...] = jnp.zeros_like(dw2_ref)
            dgb_ref[...] = jnp.zeros_like(dgb_ref)

        t = _tile_prep(q_ref, k_ref, lr_ref, w2_ref, gb_ref, rev, nc)
        hm = _head_masks()
        v = v_ref[...].reshape(nc, CH, VW)
        do = do_ref[...].reshape(nc, CH, VW)
        heads = lambda a, h: a[:, :, h * HV:(h + 1) * HV]
        dpst = jnp.concatenate([_bnt(heads(do, h), heads(v, h)) for h in range(NH)], axis=1) * t.tri4
        dv = jnp.concatenate([_btn(t.pst[:, h * CH:(h + 1) * CH], heads(do, h)) for h in range(NH)], axis=2)
        dqd = _bdot(do, st_ref[...])
        for h in range(NH):
            dqd = dqd + hm[h] * _bdot(dpst[:, h * CH:(h + 1) * CH], t.kd)
        dkd = _btn(dpst, t.qs)
        u = _btn(do, t.qd) * _block_diag()[None]
        d = dst[...]
        for n in (range(nc) if rev else range(nc - 1, -1, -1)):
            ds_buf[n] = d
            d = d * t.ecl[n] + u[n]
        dst[...] = d
        ds0_ref[...] = d
        ds = ds_buf[...]
        dv = dv + _bnt(t.kdec, ds)
        dkdec = _bdot(v, ds)
        dcl = jnp.sum(ds * st_ref[...], axis=1, keepdims=True) * t.ecl + jnp.sum(dkdec * t.kdec, axis=1, keepdims=True)
        dc = dqd * t.qd - dkd * t.kd - dkdec * t.kdec
        da = _scan_chunks(dc.reshape(tg, KW), not rev).reshape(nc, CH, KW) + dcl
        dq = dqd * t.ec * QSCALE
        dk = dkd * t.enc + dkdec * t.edec
        dlog = da.reshape(tg, KW) * _sigmoid(-t.logits) * (1.0 / TAU)
        dlr = _nt(dlog, w2_ref[...])
        dw2_ref[...] += _tn(lr_ref[...], dlog)
        dgb_ref[...] += jnp.sum(dlog, axis=0, keepdims=True)
        dqkv = jnp.concatenate([dq, dk, dv], axis=2).reshape(tg, 2 * KW + VW)
        if prev is not None:
            dqkv = dqkv + pq_ref[...]
            dlr = dlr + pl_ref[...]
        dqkv_ref[...] = dqkv.astype(out_dt)
        dlr_ref[...] = dlr.astype(out_dt)

    blk = (lambda i: i) if rev else (lambda i: nb - 1 - i)
    in_specs = [pl.BlockSpec((tg, KW), lambda i: (blk(i), PQ // KW)), pl.BlockSpec((tg, KW), lambda i: (blk(i), PK // KW)),
                pl.BlockSpec((tg, VW), lambda i: (blk(i), PV // VW)), pl.BlockSpec((tg, LRW), lambda i: (blk(i), PLR // LRW)),
                pl.BlockSpec((tg, VW), lambda i: (blk(i), 0)), pl.BlockSpec((nc, VW, KW), lambda i: (blk(i), 0, 0)),
                pl.BlockSpec((LRW, KW), lambda i: (0, 0)), pl.BlockSpec((1, KW), lambda i: (0, 0))]
    args = [p, p, p, p, do, states, w2p, gb]
    if prev is not None:
        in_specs += [pl.BlockSpec((tg, 2 * KW + VW), lambda i: (blk(i), 0)), pl.BlockSpec((tg, LRW), lambda i: (blk(i), 0))]
        args += list(prev)
    return pl.pallas_call(
        body, name=name, grid=(nb,),
        out_shape=(jax.ShapeDtypeStruct((l, 2 * KW + VW), out_dt), jax.ShapeDtypeStruct((l, LRW), out_dt),
                   jax.ShapeDtypeStruct((LRW, KW), f32), jax.ShapeDtypeStruct((1, KW), f32), jax.ShapeDtypeStruct((VW, KW), f32))
        + tuple(comm.outs),
        in_specs=in_specs + c_in,
        out_specs=(pl.BlockSpec((tg, 2 * KW + VW), lambda i: (blk(i), 0)), pl.BlockSpec((tg, LRW), lambda i: (blk(i), 0)),
                   pl.BlockSpec((LRW, KW), lambda i: (0, 0)), pl.BlockSpec((1, KW), lambda i: (0, 0)),
                   pl.BlockSpec((VW, KW), lambda i: (0, 0))) + tuple(c_out),
        scratch_shapes=[pltpu.VMEM((VW, KW), f32), pltpu.VMEM((nc, VW, KW), f32)] + c_sems,
        compiler_params=_cp(("arbitrary",)),
    )(*args, *comm.ins)


def _ctx_hidden(ctx_ref, g_ref, sc_ref, sh_ref):
    xv = ctx_ref[...]
    r = lax.rsqrt(jnp.mean(xv * xv, axis=-1, keepdims=True) + EPS)
    xn = xv * r
    return xn, xn * (g_ref[...] * (1.0 + sc_ref[...])) + sh_ref[...]


_CTX_W_SPECS = [pl.BlockSpec((D, KW), lambda i: (0, PK // KW)), pl.BlockSpec((D, VW), lambda i: (0, PV // VW)),
                pl.BlockSpec((D, LRW), lambda i: (0, PLR // LRW))]


def _ctx_fwd(ctx, g, scale, shift, w_pad, w2f, w2b, gbf, gbb):
    ncc = CTX // CH

    def body(ctx_ref, g_ref, sc_ref, sh_ref, wk_ref, wv_ref, wl_ref, w2f_ref, w2b_ref, gbf_ref, gbb_ref, sf_ref, sb_ref):
        _, hc = _ctx_hidden(ctx_ref, g_ref, sc_ref, sh_ref)
        k, v, lr = _dot(hc, wk_ref[...]), _dot(hc, wv_ref[...]), _dot(hc, wl_ref[...])
        bd = _block_diag()
        for rev, w2_ref, gb_ref, out in ((False, w2f_ref, gbf_ref, sf_ref), (True, w2b_ref, gbb_ref, sb_ref)):
            tri = _tri(rev)
            st = jnp.zeros((VW, KW), f32)
            for j in (range(ncc - 1, -1, -1) if rev else range(ncc)):
                rows = slice(j * CH, (j + 1) * CH)
                _, c, cl = _decay(lr[rows], w2_ref[...], gb_ref[...], tri, rev)
                st = _state_fwd(k[rows], v[rows], c, cl, st, bd)
            out[...] = st

    vec = pl.BlockSpec((1, D), lambda i: (0, 0))
    w2s = pl.BlockSpec((LRW, KW), lambda i: (0, 0))
    gbs = pl.BlockSpec((1, KW), lambda i: (0, 0))
    sts = pl.BlockSpec((VW, KW), lambda i: (0, 0))
    return pl.pallas_call(
        body, name="ctx_fwd", grid=(1,), out_shape=(jax.ShapeDtypeStruct((VW, KW), f32),) * 2,
        in_specs=[pl.BlockSpec((CTX, D), lambda i: (0, 0)), vec, vec, vec] + _CTX_W_SPECS + [w2s, w2s, gbs, gbs],
        out_specs=(sts, sts), compiler_params=_cp(("arbitrary",)),
    )(ctx, g, scale, shift, w_pad, w_pad, w_pad, w2f, w2b, gbf, gbb)


def _ctx_bwd(ctx, g, scale, shift, w_pad, w2f, w2b, gbf, gbb, dsf, dsb):
    ncc = CTX // CH

    def body(ctx_ref, g_ref, sc_ref, sh_ref, wk_ref, wv_ref, wl_ref, w2f_ref, w2b_ref, gbf_ref, gbb_ref, dsf_ref, dsb_ref,
             dwk_ref, dwv_ref, dwl_ref, dmod_ref, dg_ref, dw2_ref, dgb_ref):
        xn, hc = _ctx_hidden(ctx_ref, g_ref, sc_ref, sh_ref)
        k, v, lr = _dot(hc, wk_ref[...]), _dot(hc, wv_ref[...]), _dot(hc, wl_ref[...])
        bd = _block_diag()
        dk_rows, dv_rows, dl_rows = [None] * ncc, [None] * ncc, [None] * ncc
        for d, (rev, w2_ref, gb_ref, ds_ref) in enumerate(((False, w2f_ref, gbf_ref, dsf_ref), (True, w2b_ref, gbb_ref, dsb_ref))):
            tri = _tri(rev)
            order = list(range(ncc - 1, -1, -1) if rev else range(ncc))
            st, saved = jnp.zeros((VW, KW), f32), {}
            for j in order:
                rows = slice(j * CH, (j + 1) * CH)
                logits, c, cl = _decay(lr[rows], w2_ref[...], gb_ref[...], tri, rev)
                saved[j] = (logits, c, cl, st)
                st = _state_fwd(k[rows], v[rows], c, cl, st, bd)
            dst = ds_ref[...]
            dw2 = jnp.zeros((LRW, KW), f32)
            dgb = jnp.zeros((1, KW), f32)
            for j in reversed(order):
                rows = slice(j * CH, (j + 1) * CH)
                logits, c, cl, st0 = saved[j]
                _, dk, dv, da, dst = _chunk_bwd(None, k[rows], v[rows], c, cl, st0, dst, None, tri, _tri(not rev), None, bd)
                dlog = da * _sigmoid(-logits) * (1.0 / TAU)
                dl = _nt(dlog, w2_ref[...])
                dw2 = dw2 + _tn(lr[rows], dlog)
                dgb = dgb + jnp.sum(dlog, axis=0, keepdims=True)
                dk_rows[j] = dk if dk_rows[j] is None else dk_rows[j] + dk
                dv_rows[j] = dv if dv_rows[j] is None else dv_rows[j] + dv
                dl_rows[j] = dl if dl_rows[j] is None else dl_rows[j] + dl
            dw2_ref[d] = dw2
            dgb_ref[d] = dgb
        dk, dv, dl = (jnp.concatenate(t, axis=0) for t in (dk_rows, dv_rows, dl_rows))
        dwk_ref[...] = _tn(hc, dk)
        dwv_ref[...] = _tn(hc, dv)
        dwl_ref[...] = _tn(hc, dl)
        dh = _nt(dk, wk_ref[...]) + _nt(dv, wv_ref[...]) + _nt(dl, wl_ref[...])
        gx = dh * xn
        dmod_ref[:, 0:D] = jnp.sum(dh, axis=0, keepdims=True)
        dmod_ref[:, D:2 * D] = jnp.sum(gx, axis=0, keepdims=True) * g_ref[...]
        dg_ref[...] = jnp.sum(gx, axis=0, keepdims=True) * (1.0 + sc_ref[...])

    vec = pl.BlockSpec((1, D), lambda i: (0, 0))
    w2s = pl.BlockSpec((LRW, KW), lambda i: (0, 0))
    gbs = pl.BlockSpec((1, KW), lambda i: (0, 0))
    sts = pl.BlockSpec((VW, KW), lambda i: (0, 0))
    full = lambda *s: pl.BlockSpec(s, lambda i: (0,) * len(s))
    return pl.pallas_call(
        body, name="ctx_bwd", grid=(1,),
        out_shape=(jax.ShapeDtypeStruct((D, KW), f32), jax.ShapeDtypeStruct((D, VW), f32), jax.ShapeDtypeStruct((D, LRW), f32),
                   jax.ShapeDtypeStruct((1, 2 * D), f32), jax.ShapeDtypeStruct((1, D), f32),
                   jax.ShapeDtypeStruct((2, LRW, KW), f32), jax.ShapeDtypeStruct((2, 1, KW), f32)),
        in_specs=[pl.BlockSpec((CTX, D), lambda i: (0, 0)), vec, vec, vec] + _CTX_W_SPECS + [w2s, w2s, gbs, gbs, sts, sts],
        out_specs=(full(D, KW), full(D, VW), full(D, LRW), full(1, 2 * D), full(1, D), full(2, LRW, KW), full(2, 1, KW)),
        compiler_params=_cp(("arbitrary",)),
    )(ctx, g, scale, shift, w_pad, w_pad, w_pad, w2f, w2b, gbf, gbb, dsf, dsb)


def _layernorm(va, g, b):
    mu = jnp.mean(va, axis=-1, keepdims=True)
    xc = va - mu
    rstd = lax.rsqrt(jnp.mean(xc * xc, axis=-1, keepdims=True) + EPS)
    vhat = xc * rstd
    return vhat, rstd, vhat * g + b


def _mix_fwd(p, ln_g, ln_b, ws, bs_t):
    l = p.shape[0]
    nch = l // ACH
    half = AW // 2

    def body(p_ref, g_ref, b_ref, ws_ref, bs_ref, sv_ref, va_buf, col_buf, sem):
        cp = pltpu.make_async_copy(p_ref.at[:, pl.ds(PVA, AW)], va_buf, sem)
        cp.start()
        cp.wait()

        def rows_step(n, carry):
            rows = pl.ds(pl.multiple_of(n * ACH, ACH), ACH)
            _, _, vn = _layernorm(va_buf[rows, :].astype(f32), g_ref[...], b_ref[...])
            for gi in range(2):
                sl = slice(gi * ACH, (gi + 1) * ACH)
                sv_ref[rows, sl] = (_dot(ws_ref[gi], vn[:, sl]) + bs_ref[:, gi:gi + 1]).astype(bf16)
            col_buf[0, rows, :] = vn[:, half:half + ACH]
            col_buf[1, rows, :] = vn[:, half + ACH:]
            return carry

        lax.fori_loop(0, nch, rows_step, 0, unroll=4)

        def cols_step(cidx, carry):
            rows = pl.ds(cidx, ACH, stride=GW)
            for gi in range(2, 4):
                col_buf[gi - 2, rows, :] = _dot(ws_ref[gi], col_buf[gi - 2, rows, :]) + bs_ref[:, gi:gi + 1]
            return carry

        lax.fori_loop(0, GW, cols_step, 0, unroll=4)

        def out_step(n, carry):
            rows = pl.ds(pl.multiple_of(n * ACH, ACH), ACH)
            sv_ref[rows, half:half + ACH] = col_buf[0, rows, :].astype(bf16)
            sv_ref[rows, half + ACH:] = col_buf[1, rows, :].astype(bf16)
            return carry

        lax.fori_loop(0, nch, out_step, 0, unroll=4)

    vm = pl.BlockSpec(memory_space=pltpu.VMEM)
    return pl.pallas_call(
        body, name="mix_fwd", out_shape=jax.ShapeDtypeStruct((l, AW), bf16),
        in_specs=[pl.BlockSpec(memory_space=pl.ANY), vm, vm, vm, vm], out_specs=vm,
        scratch_shapes=[pltpu.VMEM((l, AW), bf16), pltpu.VMEM((2, l, ACH), f32), pltpu.SemaphoreType.DMA(())],
        compiler_params=_cp(),
    )(p, ln_g, ln_b, ws, bs_t)


def _mix_bwd(p, dsv, ln_g, ln_b, ws_t):
    l = p.shape[0]
    nch = l // ACH
    half = AW // 2

    def body(p_ref, dsv_hbm, g_ref, b_ref, wst_ref, dva_ref, dws_ref, dbs_ref, dg_ref, db_ref, va_buf, dsv_buf, vn_col, ds_col, sems):
        cp1 = pltpu.make_async_copy(p_ref.at[:, pl.ds(PVA, AW)], va_buf, sems.at[0])
        cp2 = pltpu.make_async_copy(dsv_hbm, dsv_buf, sems.at[1])
        cp1.start()
        cp2.start()
        dws_ref[...] = jnp.zeros_like(dws_ref)
        dbs_ref[...] = jnp.zeros_like(dbs_ref)
        dg_ref[...] = jnp.zeros_like(dg_ref)
        db_ref[...] = jnp.zeros_like(db_ref)
        cp1.wait()
        cp2.wait()

        def rows_step(n, carry):
            rows = pl.ds(pl.multiple_of(n * ACH, ACH), ACH)
            _, _, vn = _layernorm(va_buf[rows, :].astype(f32), g_ref[...], b_ref[...])
            ds = dsv_buf[rows, :].astype(f32)
            for gi in range(2):
                sl = slice(gi * ACH, (gi + 1) * ACH)
                dws_ref[gi] += _nt(ds[:, sl], vn[:, sl])
                dbs_ref[gi] += ds[:, sl]
            for gi in range(2):
                sl = slice(half + gi * ACH, half + (gi + 1) * ACH)
                vn_col[gi, rows, :] = vn[:, sl]
                ds_col[gi, rows, :] = ds[:, sl]
            return carry

        lax.fori_loop(0, nch, rows_step, 0, unroll=4)

        def cols_step(cidx, carry):
            rows = pl.ds(cidx, ACH, stride=GW)
            for gi in range(2, 4):
                ds = ds_col[gi - 2, rows, :]
                dws_ref[gi] += _nt(ds, vn_col[gi - 2, rows, :])
                dbs_ref[gi] += ds
                ds_col[gi - 2, rows, :] = _dot(wst_ref[gi], ds)
            return carry

        lax.fori_loop(0, GW, cols_step, 0, unroll=4)

        def out_step(n, carry):
            rows = pl.ds(pl.multiple_of(n * ACH, ACH), ACH)
            vhat, rstd, _ = _layernorm(va_buf[rows, :].astype(f32), g_ref[...], b_ref[...])
            ds = dsv_buf[rows, :].astype(f32)
            dvn = jnp.concatenate([_dot(wst_ref[0], ds[:, 0:ACH]), _dot(wst_ref[1], ds[:, ACH:half]), ds_col[0, rows, :], ds_col[1, rows, :]], axis=1)
            dg_ref[...] += jnp.sum(dvn * vhat, axis=0, keepdims=True)
            db_ref[...] += jnp.sum(dvn, axis=0, keepdims=True)
            dvh = dvn * g_ref[...]
            dva = rstd * (dvh - jnp.mean(dvh, axis=-1, keepdims=True) - vhat * jnp.mean(dvh * vhat, axis=-1, keepdims=True))
            dva_ref[rows, :] = dva.astype(bf16)
            return carry

        lax.fori_loop(0, nch, out_step, 0, unroll=4)

    vm = pl.BlockSpec(memory_space=pltpu.VMEM)
    hbm = pl.BlockSpec(memory_space=pl.ANY)
    return pl.pallas_call(
        body, name="mix_bwd",
        out_shape=(jax.ShapeDtypeStruct((l, AW), bf16), jax.ShapeDtypeStruct((4, ACH, ACH), f32), jax.ShapeDtypeStruct((4, ACH, ACH), f32),
                   jax.ShapeDtypeStruct((1, AW), f32), jax.ShapeDtypeStruct((1, AW), f32)),
        in_specs=[hbm, hbm, vm, vm, vm], out_specs=(vm, vm, vm, vm, vm),
        scratch_shapes=[pltpu.VMEM((l, AW), bf16), pltpu.VMEM((l, AW), bf16), pltpu.VMEM((2, l, ACH), f32), pltpu.VMEM((2, l, ACH), f32),
                        pltpu.SemaphoreType.DMA((2,))],
        compiler_params=_cp(),
    )(p, dsv, ln_g, ln_b, ws_t)


def _mid(x, tgt, p, o_f, o_b, sv, gate, gf, gb_norm, w_pa, w_pb, w_out, tl):
    l = x.shape[0]

    def body(x_ref, t_ref, zb_ref, ua_ref, za_ref, g1_ref, g2_ref, of_ref, ob_ref, sv_ref, gate_ref, gf_ref, gbn_ref,
             wpa_ref, wpb_ref, wout_ref,
             dx1_ref, dzbua_ref, dzag_ref, dsv_ref, do_ref, dwout_bf, dwpa_bf, dwpb_bf, dgf_ref, dgate_ref, dgbn_ref, loss_ref,
             dwout_ref, dwpa_ref, dwpb_ref):
        @pl.when(pl.program_id(0) == 0)
        def _():
            for r in (dwout_ref, dwpa_ref, dwpb_ref, dgf_ref, dgate_ref, dgbn_ref, loss_ref):
                r[...] = jnp.zeros_like(r)

        o = of_ref[...] + ob_ref[...]
        rr = jnp.concatenate(
            [jnp.broadcast_to(lax.rsqrt(jnp.mean(o[:, h * HV:(h + 1) * HV] ** 2, axis=-1, keepdims=True) + EPS), (tl, HV))
             for h in range(NH)], axis=1)
        ohat = o * rr
        on = ohat * gbn_ref[...]
        szb, dszb = _silu_and_grad(zb_ref[...].astype(f32))
        tb = on * szb
        u = ua_ref[...].astype(f32)
        svv = sv_ref[...].astype(f32)
        sza, dsza = _silu_and_grad(za_ref[...].astype(f32))
        ta = u * svv * sza
        ya = _dot(ta, wpa_ref[...])
        yb = _dot(tb, wpb_ref[...])
        g1 = _sigmoid(g1_ref[...].astype(f32))
        g2 = _sigmoid(g2_ref[...].astype(f32))
        m = g1 * ya + g2 * yb
        y2 = _dot(m, wout_ref[...])
        x1 = x_ref[...] + gate_ref[...] * y2
        r1 = lax.rsqrt(jnp.mean(x1 * x1, axis=-1, keepdims=True) + EPS)
        x1n = x1 * r1
        err = x1n * gf_ref[...] - t_ref[...]
        loss_ref[...] += jnp.sum(jnp.sum(err * err, axis=-1, keepdims=True), axis=0, keepdims=True) * (0.5 / D)
        dout = err * (1.0 / D)
        dgf_ref[...] += jnp.sum(dout * x1n, axis=0, keepdims=True)
        dx1n = dout * gf_ref[...]
        dx1 = r1 * (dx1n - x1n * jnp.mean(dx1n * x1n, axis=-1, keepdims=True))
        dx1_ref[...] = dx1
        dgate_ref[...] += jnp.sum(dx1 * y2, axis=0, keepdims=True)
        dy2 = dx1 * gate_ref[...]
        dwout_ref[...] += _tn(m, dy2)
        dm = _nt(dy2, wout_ref[...])
        dya = dm * g1
        dyb = dm * g2
        dzag_ref[:, AW:AW + D] = (dm * ya * g1 * (1.0 - g1)).astype(bf16)
        dzag_ref[:, AW + D:] = (dm * yb * g2 * (1.0 - g2)).astype(bf16)
        dwpa_ref[...] += _tn(ta, dya)
        dta = _nt(dya, wpa_ref[...])
        dzbua_ref[:, AW:] = (dta * svv * sza).astype(bf16)
        dsv_ref[...] = (dta * u * sza).astype(bf16)
        dzag_ref[:, 0:AW] = (dta * u * svv * dsza).astype(bf16)
        dwpb_ref[...] += _tn(tb, dyb)
        dtb = _nt(dyb, wpb_ref[...])
        don = dtb * szb
        dzbua_ref[:, 0:AW] = (dtb * on * dszb).astype(bf16)
        dgbn_ref[...] += jnp.sum(don * ohat, axis=0, keepdims=True)
        doh = don * gbn_ref[...]
        prod = doh * ohat
        mh = jnp.concatenate(
            [jnp.broadcast_to(jnp.mean(prod[:, h * HV:(h + 1) * HV], axis=-1, keepdims=True), (tl, HV)) for h in range(NH)], axis=1)
        do_ref[...] = (rr * (doh - ohat * mh)).astype(bf16)

        @pl.when(pl.program_id(0) == l // tl - 1)
        def _():
            for acc, out in ((dwout_ref, dwout_bf), (dwpa_ref, dwpa_bf), (dwpb_ref, dwpb_bf)):
                out[...] = acc[...].astype(bf16)

    row = lambda w, j: pl.BlockSpec((tl, w), lambda i, j=j: (i, j))
    full = lambda *s: pl.BlockSpec(s, lambda i: (0,) * len(s))
    return pl.pallas_call(
        body, name="mid", grid=(l // tl,),
        out_shape=(jax.ShapeDtypeStruct((l, D), f32), jax.ShapeDtypeStruct((l, 2 * AW), bf16), jax.ShapeDtypeStruct((l, AW + 2 * D), bf16),
                   jax.ShapeDtypeStruct((l, AW), bf16), jax.ShapeDtypeStruct((l, VW), bf16),
                   jax.ShapeDtypeStruct((D, D), bf16), jax.ShapeDtypeStruct((AW, D), bf16), jax.ShapeDtypeStruct((VW, D), bf16),
                   jax.ShapeDtypeStruct((1, D), f32), jax.ShapeDtypeStruct((1, D), f32), jax.ShapeDtypeStruct((1, VW), f32),
                   jax.ShapeDtypeStruct((1, 1), f32)),
        scratch_shapes=[pltpu.VMEM((D, D), f32), pltpu.VMEM((AW, D), f32), pltpu.VMEM((VW, D), f32)],
        in_specs=[row(D, 0), row(D, 0), row(AW, PZB // AW), row(AW, PUA // AW), row(AW, PZA // AW), row(D, PG1 // D), row(D, PG2 // D),
                  row(VW, 0), row(VW, 0), row(AW, 0), full(1, D), full(1, D), full(1, VW), full(AW, D), full(VW, D), full(D, D)],
        out_specs=(row(D, 0), row(2 * AW, 0), row(AW + 2 * D, 0), row(AW, 0), row(VW, 0),
                   full(D, D), full(AW, D), full(VW, D), full(1, D), full(1, D), full(1, VW), full(1, 1)),
        compiler_params=_cp(("arbitrary",)),
    )(x, tgt, p, p, p, p, p, o_f, o_b, sv, gate, gf, gb_norm, w_pa, w_pb, w_out)


def _in_bwd(x, dx1, dqkv, dzbua, dva, dzag, dlr, w_pad, g, scale, tl, comm):
    l = x.shape[0]
    c_in, c_out, c_sems = comm.specs()

    def body(*refs):
        cin = refs[14:14 + len(c_in)]
        outs = refs[14 + len(c_in):]
        comm.run(cin, outs[4:4 + comm.n], outs[4 + comm.n:], l // tl, lambda: compute(*refs[:14], *outs[:4]))

    def compute(x_ref, dx1_ref, a_ref, b_ref, c_ref, e_ref, lr_ref, wa_ref, wb_ref, wc_ref, we_ref, wl_ref, g_ref, sc_ref,
                gx_ref, dsh_ref, dsc_ref, dg_ref):
        @pl.when(pl.program_id(0) == 0)
        def _():
            for r in (dsh_ref, dsc_ref, dg_ref):
                r[...] = jnp.zeros_like(r)

        dh = (_nt(a_ref[...], wa_ref[...]) + _nt(b_ref[...], wb_ref[...]) + _nt(c_ref[...], wc_ref[...]) + _nt(e_ref[...], we_ref[...])
              + _nt(lr_ref[...], wl_ref[...]))
        xv = x_ref[...]
        r = lax.rsqrt(jnp.mean(xv * xv, axis=-1, keepdims=True) + EPS)
        xn = xv * r
        gxn = jnp.sum(dh * xn, axis=0, keepdims=True)
        dsh_ref[...] += jnp.sum(dh, axis=0, keepdims=True)
        dsc_ref[...] += gxn * g_ref[...]
        dg_ref[...] += gxn * (1.0 + sc_ref[...])
        dxn = dh * (g_ref[...] * (1.0 + sc_ref[...]))
        gx_ref[...] = dx1_ref[...] + r * (dxn - xn * jnp.mean(dxn * xn, axis=-1, keepdims=True))

    row = lambda w: pl.BlockSpec((tl, w), lambda i: (i, 0))
    wcol = lambda w, j: pl.BlockSpec((D, w), lambda i, j=j: (0, j))
    vec = pl.BlockSpec((1, D), lambda i: (0, 0))
    return pl.pallas_call(
        body, name="in_bwd", grid=(l // tl,),
        out_shape=(jax.ShapeDtypeStruct((l, D), f32),) + (jax.ShapeDtypeStruct((1, D), f32),) * 3 + tuple(comm.outs),
        in_specs=[row(D), row(D), row(2 * KW + VW), row(2 * AW), row(AW), row(AW + 2 * D), row(LRW),
                  wcol(2 * KW + VW, 0), wcol(2 * AW, PZB // (2 * AW)), wcol(AW, PVA // AW), wcol(AW + 2 * D, PZA // (AW + 2 * D)),
                  wcol(LRW, PLR // LRW), vec, vec] + c_in,
        out_specs=(row(D), vec, vec, vec) + tuple(c_out), scratch_shapes=c_sems,
        compiler_params=_cp(("arbitrary",)),
    )(x, dx1, dqkv, dzbua, dva, dzag, dlr, w_pad, w_pad, w_pad, w_pad, w_pad, g, scale, *comm.ins)


def _tn_matmul(a, bs, tl, name):
    l, m = a.shape
    k = len(bs)

    def body(a_ref, *refs):
        @pl.when(pl.program_id(0) == 0)
        def _():
            for o_ref in refs[k:]:
                o_ref[...] = jnp.zeros_like(o_ref)

        av = a_ref[...]
        for b_ref, o_ref in zip(refs[:k], refs[k:]):
            o_ref[...] += _tn(av, b_ref[...])

    return pl.pallas_call(
        body, name=name, grid=(l // tl,), out_shape=tuple(jax.ShapeDtypeStruct((m, b.shape[1]), f32) for b in bs),
        in_specs=[pl.BlockSpec((tl, m), lambda i: (i, 0))] + [pl.BlockSpec((tl, b.shape[1]), lambda i: (i, 0)) for b in bs],
        out_specs=tuple(pl.BlockSpec((m, b.shape[1]), lambda i: (0, 0)) for b in bs), compiler_params=_cp(("arbitrary",)),
    )(a, *bs)


def _pad_gate(w2, gb):
    z = jnp.zeros((RANK, KW), f32)
    tail = jnp.zeros((LRW - 2 * RANK, KW), f32)
    w2f = jnp.concatenate([w2[0], z, tail], axis=0)
    w2b = jnp.concatenate([z, w2[1], tail], axis=0)
    return w2f, w2b, gb[0:1], gb[1:2]


EARLY_A_ROWS = 256


class _NoExchange:
    def __init__(self, w_pa, w_pb, w_out):
        self.weights = (w_pa, w_pb, w_out)

    def gather_proj(self):
        return _Comm([], [], None)

    def proj_weights(self, got):
        return self.weights

    def early_a(self, dw_out, dw_pa, dw_pb, small, blocks):
        return _Comm([], [], None)

    def early_b(self, blocks):
        return _Comm([], [], None)

    def late(self, blocks, dgt):
        return _Comm([], [], None)


class _Exchanges:
    def __init__(self, pa, pb, wo):
        self.shards = (pa, pb, wo)

    def gather_proj(self):
        def plan(i, o):
            srcs = [lambda j, r=r: r for r in i]
            dsts = [lambda j: o[0].at[:, _lanes(j)], lambda j: o[1].at[:, _lanes(j)], lambda j: o[2].at[j]]
            return srcs, dsts, None
        sds = jax.ShapeDtypeStruct
        return _Comm(self.shards, [sds((AW, D), bf16), sds((VW, D), bf16), sds((NDEV, 128, D), bf16)], plan)

    def proj_weights(self, got):
        return got[0], got[1], got[2].reshape(D, D)

    def early_a(self, dw_out, dw_pa, dw_pb, small, blocks):
        def plan(i, o):
            srcs = [lambda j: i[0].at[j], lambda j: i[1].at[:, _lanes(j)], lambda j: i[2].at[:, _lanes(j)], lambda j: i[3],
                    lambda j: i[4].at[j]]
            dsts = [lambda j, r=r: r.at[j] for r in o]
            return srcs, dsts, [None, None, None, None, lambda j: j >= LATE_DESTS - 1]
        sds = jax.ShapeDtypeStruct
        return _Comm([dw_out.reshape(NDEV, 128, D), dw_pa, dw_pb, small, blocks],
                     [sds((NDEV, 128, D), bf16), sds((NDEV, AW, 128), bf16), sds((NDEV, VW, 128), bf16),
                      sds((NDEV,) + small.shape, f32), sds(blocks.shape, bf16)], plan)

    def early_b(self, blocks):
        def plan(i, o):
            return [lambda j: i[0].at[j]], [lambda j: o[0].at[j]], [lambda j: j >= LATE_DESTS - 1]
        return _Comm([blocks], [jax.ShapeDtypeStruct(blocks.shape, bf16)], plan)

    def late(self, blocks, dgt):
        def plan(i, o):
            srcs = [lambda j: i[0].at[jnp.minimum(j, LATE_DESTS - 1)], lambda j: i[1].at[j]]
            dsts = [lambda j, r=r: r.at[j] for r in o]
            return srcs, dsts, [lambda j: j < LATE_DESTS, None]
        sds = jax.ShapeDtypeStruct
        return _Comm([blocks, dgt], [sds((NDEV, D, SHARD), bf16), sds(dgt.shape, f32)], plan)


def _local_step(x, ctx, tgt, mod, modc, norm_g, w_pad, ln_g, ln_b, ws, bs, w2, gb, gb_norm, gf, xch):
    shift, scale, gate = mod[:, 0:D], mod[:, D:2 * D], mod[:, 2 * D:]
    shift_c, scale_c = modc[:, 0:D], modc[:, D:]
    w2f, w2b, gbf, gbb = _pad_gate(w2, gb)

    p, h, *got_proj = _in_proj(x, norm_g, scale, shift, w_pad, 256, xch.gather_proj())
    w_pa, w_pb, w_out = xch.proj_weights(got_proj)
    sc_f, sc_b = _ctx_fwd(ctx, norm_g, scale_c, shift_c, w_pad, w2f, w2b, gbf, gbb)
    o_f, st_f = _gla_fwd(p, w2f, gbf, sc_f, False, 512, "gla_fwd_f")
    o_b, st_b = _gla_fwd(p, w2b, gbb, sc_b, True, 512, "gla_fwd_b")
    sv = _mix_fwd(p, ln_g, ln_b, ws.astype(bf16), bs.T)
    (dx1, dzbua, dzag, dsv, do, dw_out, dw_pa, dw_pb, dgf, dgate, dgbn, loss) = _mid(
        x, tgt, p, o_f, o_b, sv, gate, gf, gb_norm, w_pa, w_pb, w_out, 256)
    dva, dws, dbs_acc, dln_g, dln_b = _mix_bwd(p, dsv, ln_g, ln_b, jnp.swapaxes(ws, 1, 2).astype(bf16))
    dw_zbua, dw_zag = _tn_matmul(h, [dzbua, dzag], 512, "dw_zbua_zag")
    (dw_va,) = _tn_matmul(h, [dva], 512, "dw_va")
    blocks_a = _pack_dw_early(dw_zbua, dw_va, dw_zag, 0, EARLY_A_ROWS, "pack_dw_early_a")
    blocks_b = _pack_dw_early(dw_zbua, dw_va, dw_zag, EARLY_A_ROWS, D - EARLY_A_ROWS, "pack_dw_early_b")
    small = _rows128(dln_g, dln_b, dws, jnp.sum(dbs_acc, axis=-1), dgbn, dgf, jnp.broadcast_to(loss, (1, 128)))

    dqkv_f, dlr_f, dw2f, dgbf, dsc_f, *got_a = _gla_bwd(p, do, st_f, w2f, gbf, None, False, 512, "gla_bwd_f",
                                                        xch.early_a(dw_out, dw_pa, dw_pb, small, blocks_a))
    dqkv, dlr, dw2b, dgbb, dsc_b, *got_b = _gla_bwd(p, do, st_b, w2b, gbb, (dqkv_f, dlr_f), True, 512, "gla_bwd_b",
                                                    xch.early_b(blocks_b))
    dwk_c, dwv_c, dwl_c, dmodc, dg_c, dw2c, dgbc = _ctx_bwd(ctx, norm_g, scale_c, shift_c, w_pad, w2f, w2b, gbf, gbb, dsc_f, dsc_b)
    dw_qkv, dw_lr = _tn_matmul(h, [dqkv, dlr], 512, "dw_qkv_lr")
    blocks_late = _pack_dw_late(dw_qkv, dw_lr, dwk_c, dwv_c, dwl_c)
    dw2 = jnp.stack([dw2f[0:RANK] + dw2c[0, 0:RANK], dw2b[RANK:2 * RANK] + dw2c[1, RANK:2 * RANK]])
    dgb = jnp.concatenate([dgbf + dgbc[0], dgbb + dgbc[1]], axis=0)
    dgt = jnp.concatenate([jnp.transpose(dw2.reshape(2, RANK, NDEV, 32), (2, 0, 1, 3)).reshape(NDEV, 2 * RANK * 32),
                           jnp.transpose(dgb.reshape(2, NDEV, 32), (1, 0, 2)).reshape(NDEV, 64),
                           jnp.zeros((NDEV, 64), f32)], axis=1).reshape(NDEV, 9, 128)
    gx, dshift, dscale, dg, *got_late = _in_bwd(x, dx1, dqkv, dzbua, dva, dzag, dlr, w_pad, norm_g, scale, 256,
                                                xch.late(blocks_late, dgt))
    return dict(loss=loss, gx=gx, dmod=jnp.concatenate([dshift, dscale, dgate], axis=1), dmodc=dmodc, dnorm_g=dg + dg_c,
                small=small, blocks_a=blocks_a, blocks_b=blocks_b, blocks_late=blocks_late, dw2=dw2, dgb=dgb,
                dw_pa=dw_pa, dw_pb=dw_pb, dw_out=dw_out, got_a=got_a, got_b=got_b, got_late=got_late)


def _rows128(*vs):
    out = []
    for t in vs:
        t = t.reshape(-1)
        pad = (-t.shape[0]) % 128
        out.append(jnp.pad(t, (0, pad)) if pad else t)
    return jnp.concatenate(out).reshape(-1, 128)


def kernel(x, c, ctx, c_ctx, w_mod, b_mod, norm_g, w_in, a_ln_g, a_ln_b, a_ws, a_bs, b_gate_w2, b_gate_b, b_norm_g, w_proj_a, w_proj_b, w_out, final_norm_g, loss_target, m_c_ctx, m_w_mod, m_b_mod, m_norm_g, m_w_in, m_a_ln_g, m_a_ln_b, m_a_ws, m_a_bs, m_b_gate_w2, m_b_gate_b, m_b_norm_g, m_w_proj_a, m_w_proj_b, m_w_out, m_final_norm_g, v_c_ctx, v_w_mod, v_b_mod, v_norm_g, v_w_in, v_a_ln_g, v_a_ln_b, v_a_ws, v_a_bs, v_b_gate_w2, v_b_gate_b, v_b_norm_g, v_w_proj_a, v_w_proj_b, v_w_out, v_final_norm_g):
    me = _me()
    ncol = w_mod.shape[2]

    gate_mine = _rows128(jnp.concatenate([b_gate_w2.reshape(-1), b_gate_b.reshape(-1)]))
    early, wg, gates = _gather_first(_rows128(c), w_in[0].astype(bf16), gate_mine)
    cs = jnp.concatenate([early.reshape(NDEV, D), c_ctx.reshape(1, D), jnp.zeros((7, D), f32)], axis=0)
    w_pad = _repack_w(wg)
    gflat = gates.reshape(NDEV, 9 * 128)
    w2 = jnp.transpose(gflat[:, 0:2 * RANK * 32].reshape(NDEV, 2, RANK, 32), (1, 2, 0, 3)).reshape(2, RANK, KW)
    gb = jnp.transpose(gflat[:, 2 * RANK * 32:2 * RANK * 32 + 64].reshape(NDEV, 2, 32), (1, 0, 2)).reshape(2, KW)

    bm_mine = lax.dynamic_slice(b_mod, (0, me * ncol), (1, ncol))
    mods = _all_gather(_mod_fwd(cs, w_mod[0], bm_mine), "gather_mod")
    mods = jnp.transpose(mods, (1, 0, 2)).reshape(16, 3 * D)
    mod = lax.dynamic_slice(mods, (me, 0), (1, 3 * D))
    modc = mods[8:9, 0:2 * D]

    xch = _Exchanges(w_proj_a[0].astype(bf16), w_proj_b[0].astype(bf16), w_out[0].astype(bf16))
    r = _local_step(x[0], ctx[0], loss_target[0], mod, modc, norm_g, w_pad, a_ln_g, a_ln_b, a_ws[0], a_bs[0], w2, gb,
                    b_norm_g, final_norm_g.reshape(1, D), xch)
    p_out, p_pa, p_pb, smalls_e, p_in_a = r["got_a"]
    (p_in_b,) = r["got_b"]
    p_in_late, p_gt = r["got_late"]

    def split(t, sizes, shapes):
        t = t.reshape(-1)
        out, o = [], 0
        for s, sh in zip(sizes, shapes):
            out.append(t[o:o + s].reshape(sh))
            o += s
        return out

    n_e = (AW + AW + 4 * ACH * ACH + AW + VW + D) // 128
    rep_e = _adam(smalls_e[:, 0:n_e], _rows128(a_ln_g, a_ln_b, a_ws, a_bs, b_norm_g, final_norm_g),
                  _rows128(m_a_ln_g, m_a_ln_b, m_a_ws, m_a_bs, m_b_norm_g, m_final_norm_g),
                  _rows128(v_a_ln_g, v_a_ln_b, v_a_ws, v_a_bs, v_b_norm_g, v_final_norm_g), n_e, "adam_rep_early")
    rep_e = [split(t, (AW, AW, 4 * ACH * ACH, AW, VW, D), ((1, AW), (1, AW), (1, 4, ACH, ACH), (1, 4, ACH), (1, VW), (D,)))
             for t in rep_e]
    losses = smalls_e[:, n_e, 0]
    loss = losses[0]
    for i in range(1, NDEV):
        loss = loss + losses[i]

    dbm = r["dmod"] + jnp.concatenate([r["dmodc"], jnp.zeros((1, D), f32)], axis=1)
    smalls_l = _all_gather(_rows128(r["dnorm_g"], dbm, r["dmod"], r["dmodc"]), "gather_small")
    n_l = (D + 3 * D) // 128
    rep_l = _adam(smalls_l[:, 0:n_l], _rows128(norm_g, b_mod), _rows128(m_norm_g, m_b_mod), _rows128(v_norm_g, v_b_mod), n_l,
                  "adam_rep_late")
    rep_l = [split(t, (D, 3 * D), ((1, D), (1, 3 * D))) for t in rep_l]
    tail = smalls_l[:, n_l:].reshape(NDEV, -1)
    dmods = tail[:, 0:3 * D]
    dmodc_all = tail[:, 3 * D:5 * D]
    dmodc_tot = dmodc_all[0:1]
    for i in range(1, NDEV):
        dmodc_tot = dmodc_tot + dmodc_all[i:i + 1]

    dm_rows = jnp.concatenate([dmods, jnp.concatenate([dmodc_tot, jnp.zeros((1, D), f32)], axis=1), jnp.zeros((7, 3 * D), f32)], axis=0)
    dm_mine = lax.dynamic_slice(dm_rows, (0, me * ncol), (16, ncol))
    g_wmod, gc_part = _mod_bwd(cs, dm_mine, w_mod[0])
    wm = _adam(g_wmod[None], w_mod, m_w_mod, v_w_mod, 256, "adam_wmod")
    gcs = _all_gather(gc_part.reshape(8, 128), "gather_cctx")
    cc = _adam(gcs, c_ctx.reshape(8, 128), m_c_ctx.reshape(8, 128), v_c_ctx.reshape(8, 128), 8, "adam_cctx")

    a_in = _adam_w_in(p_in_a, p_in_b, p_in_late, w_in, m_w_in, v_w_in)
    a_pa = _adam(p_pa, w_proj_a, m_w_proj_a, v_w_proj_a, AW, "adam_w_pa")
    a_pb = _adam(p_pb, w_proj_b, m_w_proj_b, v_w_proj_b, VW, "adam_w_pb")
    a_out = _adam(p_out, w_out, m_w_out, v_w_out, 128, "adam_w_out")
    gate_m = _rows128(jnp.concatenate([m_b_gate_w2.reshape(-1), m_b_gate_b.reshape(-1)]))
    gate_v = _rows128(jnp.concatenate([v_b_gate_w2.reshape(-1), v_b_gate_b.reshape(-1)]))
    a_gt = [t.reshape(-1) for t in _adam(p_gt, gate_mine, gate_m, gate_v, 9, "adam_gate")]
    nw2 = 2 * RANK * 32
    sh = [(a_in[k], a_pa[k], a_pb[k], a_out[k], a_gt[k][0:nw2].reshape(1, 2, RANK, 32),
           a_gt[k][nw2:nw2 + 64].reshape(1, 2, 32)) for k in range(4)]

    outs = [loss, r["gx"][None]]
    for k in range(4):
        lg, lb, aws, abs_, bng, fng = rep_e[k]
        n_g, bmod = rep_l[k]
        s_in, s_pa, s_pb, s_out, s_w2, s_gb = sh[k]
        outs += [cc[k].reshape(D), wm[k], bmod, n_g, s_in, lg, lb, aws, abs_, s_w2, s_gb, bng, s_pa, s_pb, s_out, fng]
    return tuple(outs)
```

```python
import functools

import jax
import jax.numpy as jnp
from jax import lax
from jax.experimental import pallas as pl
from jax.experimental.pallas import tpu as pltpu

f32, bf16 = jnp.float32, jnp.bfloat16

D = 1024
CTX = 256
EPS = 1e-6
AW = 512
ACH = 128
GW = 64
KW = 256
VW = 512
NH = 4
HK = 64
HV = 128
RANK = 16
TAU = 16.0
CH = 64
QSCALE = HK ** -0.5
INW = 5152
NDEV = 8

PQ, PK, PV, PZB, PUA, PVA, PZA, PG1, PG2, PLR, PW = 0, 256, 512, 1024, 1536, 2048, 2560, 3072, 4096, 5120, 5248
LRW = 128

ADAM_LR, ADAM_B1, ADAM_B2, ADAM_EPS, ADAM_WD, ADAM_STEP = 0.001, 0.9, 0.999, 1e-08, 0.01, 10

VMEM_LIMIT = 56 * 1024 * 1024
MESH = pl.DeviceIdType.MESH


def _cp(sem=None):
    return pltpu.CompilerParams(dimension_semantics=sem, vmem_limit_bytes=VMEM_LIMIT)


def _dot(a, b):
    return jnp.dot(a.astype(bf16), b.astype(bf16), preferred_element_type=f32)


def _nt(a, b):
    return lax.dot_general(a.astype(bf16), b.astype(bf16), (((1,), (1,)), ((), ())), preferred_element_type=f32)


def _tn(a, b):
    return lax.dot_general(a.astype(bf16), b.astype(bf16), (((0,), (0,)), ((), ())), preferred_element_type=f32)


def _dot_hi(a, b):
    return jnp.dot(a, b, preferred_element_type=f32, precision=lax.Precision.HIGHEST)


def _sigmoid(x):
    return 1.0 / (1.0 + jnp.exp(-x))


def _log_sigmoid(x):
    return jnp.minimum(x, 0.0) - jnp.log(1.0 + jnp.exp(-jnp.abs(x)))


def _silu_and_grad(z):
    s = _sigmoid(z)
    return z * s, s * (1.0 + z * (1.0 - s))


def _me():
    return 4 * lax.axis_index("x") + 2 * lax.axis_index("y") + lax.axis_index("c")


def _peer(k):
    x, y, c = lax.axis_index("x"), lax.axis_index("y"), lax.axis_index("c")
    px = 1 - x if k & 4 else x
    py = 1 - y if k & 2 else y
    pc = 1 - c if k & 1 else c
    return (px, py, pc), 4 * px + 2 * py + pc


def _all_gather(v, name):
    r, c = v.shape

    def body(v_ref, out_ref, send_sems, recv_sems, local_sem):
        me = _me()
        mine = pltpu.make_async_copy(v_ref, out_ref.at[me], local_sem)
        mine.start()
        sends, recvs = [], []
        for k in range(1, NDEV):
            dev, idx = _peer(k)
            sends.append(pltpu.make_async_remote_copy(
                src_ref=v_ref, dst_ref=out_ref.at[me], send_sem=send_sems.at[k - 1], recv_sem=recv_sems.at[k - 1],
                device_id=dev, device_id_type=MESH))
            recvs.append(pltpu.make_async_remote_copy(
                src_ref=v_ref, dst_ref=out_ref.at[idx], send_sem=send_sems.at[k - 1], recv_sem=recv_sems.at[k - 1],
                device_id=dev, device_id_type=MESH))
        for cp in sends:
            cp.start()
        for cp in recvs:
            cp.wait_recv()
        for cp in sends:
            cp.wait_send()
        mine.wait()

    return pl.pallas_call(
        body, name=name, out_shape=jax.ShapeDtypeStruct((NDEV, r, c), v.dtype),
        in_specs=[pl.BlockSpec(memory_space=pl.ANY)], out_specs=pl.BlockSpec(memory_space=pl.ANY),
        scratch_shapes=[pltpu.SemaphoreType.DMA((NDEV - 1,)), pltpu.SemaphoreType.DMA((NDEV - 1,)), pltpu.SemaphoreType.DMA(())],
    )(v)


def _fanout(srcs, dsts, send_sems, recv_sems, local_sems, owners=None):
    me = _me()
    n = len(srcs)
    owns = lambda a, j: True if owners is None or owners[a] is None else owners[a](j)

    def guarded(cond, fn):
        if cond is True:
            fn()
        else:
            pl.when(cond)(fn)

    def copies(with_recvs):
        local = [pltpu.make_async_copy(srcs[a](me), dsts[a](me), local_sems.at[a]) for a in range(n)]
        sends, recvs = [], []
        for k in range(1, NDEV):
            dev, idx = _peer(k)
            for a in range(n):
                s = (k - 1) * n + a
                sends.append((owns(a, idx), pltpu.make_async_remote_copy(
                    src_ref=srcs[a](idx), dst_ref=dsts[a](me), send_sem=send_sems.at[s], recv_sem=recv_sems.at[s],
                    device_id=dev, device_id_type=MESH)))
                if with_recvs:
                    recvs.append((owns(a, me), pltpu.make_async_remote_copy(
                        src_ref=srcs[a](idx), dst_ref=dsts[a](idx), send_sem=send_sems.at[s], recv_sem=recv_sems.at[s],
                        device_id=dev, device_id_type=MESH)))
        return local, sends, recvs

    def start():
        local, sends, _ = copies(False)
        for a, cp in enumerate(local):
            guarded(owns(a, me), cp.start)
        for cond, cp in sends:
            guarded(cond, cp.start)

    def finish():
        local, sends, recvs = copies(True)
        for cond, cp in recvs:
            guarded(cond, cp.wait_recv)
        for cond, cp in sends:
            guarded(cond, cp.wait_send)
        for a, cp in enumerate(local):
            guarded(owns(a, me), cp.wait)

    return start, finish


class _Comm:
    def __init__(self, ins, outs, plan):
        self.ins, self.outs, self.plan = list(ins), list(outs), plan
        self.n = len(self.outs)

    def specs(self):
        hbm = pl.BlockSpec(memory_space=pl.ANY)
        return [hbm] * len(self.ins), [hbm] * self.n, _fanout_sems(self.n) if self.n else []

    def run(self, in_refs, out_refs, sems, nsteps, compute):
        if not self.n:
            compute()
            return

        def hooks():
            srcs, dsts, owners = self.plan(in_refs, out_refs)
            return _fanout(srcs, dsts, sems[0], sems[1], sems[2], owners)

        pl.when(pl.program_id(0) == 0)(lambda: hooks()[0]())
        compute()
        pl.when(pl.program_id(0) == nsteps - 1)(lambda: hooks()[1]())


def _fanout_sems(n):
    return [pltpu.SemaphoreType.DMA(((NDEV - 1) * n,)), pltpu.SemaphoreType.DMA(((NDEV - 1) * n,)), pltpu.SemaphoreType.DMA((n,))]


def _lanes(j):
    return pl.ds(pl.multiple_of(j * 128, 128), 128)


def _gather_first(c_rows, wi, gate):
    def body(c_ref, wi_ref, g_ref, oc, owi, og, send_sems, recv_sems, local_sems):
        x, y, c = lax.axis_index("x"), lax.axis_index("y"), lax.axis_index("c")
        sibling = (x, y, 1 - c)
        chips = [(1 - x, y), (x, 1 - y), (1 - x, 1 - y)]
        index = lambda px, py, pc: 4 * px + 2 * py + pc
        arrays = ((c_ref, oc), (wi_ref, owi), (g_ref, og))
        n = len(arrays)

        def copy(a, k, block, to, own=False):
            src, out = arrays[a]
            return pltpu.make_async_remote_copy(
                src_ref=src if own else out.at[index(*block)], dst_ref=out.at[index(*block)],
                send_sem=send_sems.at[k * n + a], recv_sem=recv_sems.at[k * n + a], device_id=to, device_id_type=MESH)

        mine = [pltpu.make_async_copy(src, out.at[index(x, y, c)], local_sems.at[a]) for a, (src, out) in enumerate(arrays)]
        first = [copy(a, 0, (x, y, c), sibling, own=True) for a in range(n)]
        first += [copy(a, 1 + j, (x, y, c), (*chip, c), own=True) for j, chip in enumerate(chips) for a in range(n)]
        for cp in mine + first:
            cp.start()
        passed = []
        for j, chip in enumerate(chips):
            for a in range(n):
                copy(a, 1 + j, (*chip, c), (x, y, c)).wait_recv()
            for a in range(n):
                cp = copy(a, 4 + j, (*chip, c), sibling)
                cp.start()
                passed.append(cp)
        for a in range(n):
            copy(a, 0, sibling, (x, y, c)).wait_recv()
        for j, chip in enumerate(chips):
            for a in range(n):
                copy(a, 4 + j, (*chip, 1 - c), (x, y, c)).wait_recv()
        for cp in first + passed:
            cp.wait_send()
        for cp in mine:
            cp.wait()

    hbm = pl.BlockSpec(memory_space=pl.ANY)
    return pl.pallas_call(
        body, name="gather_first",
        out_shape=(jax.ShapeDtypeStruct((NDEV,) + c_rows.shape, f32), jax.ShapeDtypeStruct((NDEV,) + wi.shape, bf16),
                   jax.ShapeDtypeStruct((NDEV,) + gate.shape, f32)),
        in_specs=[hbm] * 3, out_specs=(hbm,) * 3, scratch_shapes=_fanout_sems(3),
    )(c_rows, wi, gate)


SHARD = INW // NDEV
ROWS_RP = 128


def _overlap(lo, hi, a, b):
    s, e = max(lo, a), min(hi, b)
    return (s, e) if s < e else None


def _repack_w(wg):
    segs = ((0, 1024, PQ), (1024, 1024 + 2 * RANK, PLR), (1024 + 2 * RANK, INW, PZB))

    def body(g_ref, o_ref):
        for j in range(NDEV):
            lo, hi = j * SHARD, (j + 1) * SHARD
            for a, b, pad0 in segs:
                ov = _overlap(lo, hi, a, b)
                if ov:
                    s, e = ov
                    o_ref[:, pad0 + s - a:pad0 + e - a] = g_ref[j, :, s - lo:e - lo]
        o_ref[:, PLR + 2 * RANK:PW] = jnp.zeros((ROWS_RP, PW - PLR - 2 * RANK), bf16)

    return pl.pallas_call(
        body, name="repack_w", grid=(D // ROWS_RP,), out_shape=jax.ShapeDtypeStruct((D, PW), bf16),
        in_specs=[pl.BlockSpec((NDEV, ROWS_RP, SHARD), lambda i: (0, i, 0))],
        out_specs=pl.BlockSpec((ROWS_RP, PW), lambda i: (i, 0)), compiler_params=_cp(("arbitrary",)),
    )(wg)


LATE_END = 1024 + 2 * RANK
LATE_DESTS = 2


def _pack_blocks(o_ref, srcs, dests, dtype):
    for n, j in enumerate(dests):
        lo, hi = j * SHARD, (j + 1) * SHARD
        done = lo
        for a, b, src in srcs:
            ov = _overlap(lo, hi, a, b)
            if ov:
                s, e = ov
                if s > done:
                    o_ref[n, :, done - lo:s - lo] = jnp.zeros((ROWS_RP, s - done), dtype)
                o_ref[n, :, s - lo:e - lo] = src[:, s - a:e - a].astype(dtype)
                done = e
        if done < hi:
            o_ref[n, :, done - lo:hi - lo] = jnp.zeros((ROWS_RP, hi - done), dtype)


def _pack_dw_early(dw_zbua, dw_va, dw_zag, row0, nrows, name):
    def body(zbua_ref, va_ref, zag_ref, o_ref):
        _pack_blocks(o_ref, ((LATE_END, 2080, zbua_ref), (2080, 2592, va_ref), (2592, INW, zag_ref)), range(NDEV), bf16)

    row = lambda w: pl.BlockSpec((ROWS_RP, w), lambda i: (i + row0 // ROWS_RP, 0))
    return pl.pallas_call(
        body, name=name, grid=(nrows // ROWS_RP,), out_shape=jax.ShapeDtypeStruct((NDEV, nrows, SHARD), bf16),
        in_specs=[row(2 * AW), row(AW), row(AW + 2 * D)],
        out_specs=pl.BlockSpec((NDEV, ROWS_RP, SHARD), lambda i: (0, i, 0)), compiler_params=_cp(("arbitrary",)),
    )(dw_zbua, dw_va, dw_zag)


def _pack_dw_late(dw_qkv, dw_lr, dwk_c, dwv_c, dwl_c):
    def body(qkv_ref, lr_ref, kc_ref, vc_ref, lc_ref, o_ref):
        qkv = qkv_ref[...] + jnp.concatenate([jnp.zeros((ROWS_RP, KW), f32), kc_ref[...], vc_ref[...]], axis=1)
        lr = lr_ref[...] + lc_ref[...]
        _pack_blocks(o_ref, ((0, 1024, qkv), (1024, LATE_END, lr)), range(LATE_DESTS), bf16)

    row = lambda w: pl.BlockSpec((ROWS_RP, w), lambda i: (i, 0))
    return pl.pallas_call(
        body, name="pack_dw_late", grid=(D // ROWS_RP,), out_shape=jax.ShapeDtypeStruct((LATE_DESTS, D, SHARD), bf16),
        in_specs=[row(2 * KW + VW), row(LRW), row(KW), row(VW), row(LRW)],
        out_specs=pl.BlockSpec((LATE_DESTS, ROWS_RP, SHARD), lambda i: (0, i, 0)), compiler_params=_cp(("arbitrary",)),
    )(dw_qkv, dw_lr, dwk_c, dwv_c, dwl_c)


def _mod_fwd(cs, wm, bm):
    def body(cs_ref, wm_ref, bm_ref, o_ref):
        s, _ = _silu_and_grad(cs_ref[...])
        o_ref[...] = _dot_hi(s, wm_ref[...]) + bm_ref[...]

    return pl.pallas_call(body, name="mod_fwd", out_shape=jax.ShapeDtypeStruct((16, wm.shape[1]), f32),
                          compiler_params=_cp())(cs, wm, bm)


def _mod_bwd(cs, dm, wm):
    def body(cs_ref, dm_ref, wm_ref, gw_ref, gc_ref):
        s, ds = _silu_and_grad(cs_ref[...])
        gw_ref[...] = lax.dot_general(s, dm_ref[...], (((0,), (0,)), ((), ())), preferred_element_type=f32,
                                      precision=lax.Precision.HIGHEST)
        part = lax.dot_general(dm_ref[8:9, :], wm_ref[...], (((1,), (1,)), ((), ())), preferred_element_type=f32,
                               precision=lax.Precision.HIGHEST)
        gc_ref[...] = part * ds[8:9, :]

    return pl.pallas_call(body, name="mod_bwd",
                          out_shape=(jax.ShapeDtypeStruct(wm.shape, f32), jax.ShapeDtypeStruct((1, D), f32)),
                          compiler_params=_cp())(cs, dm, wm)


def _adam_update(g, w_ref, m_ref, v_ref, go_ref, d_ref, mo_ref, vo_ref):
    c1 = 1.0 / (1.0 - ADAM_B1 ** ADAM_STEP)
    c2 = 1.0 / (1.0 - ADAM_B2 ** ADAM_STEP)
    mn = ADAM_B1 * m_ref[...] + (1.0 - ADAM_B1) * g
    vn = ADAM_B2 * v_ref[...] + (1.0 - ADAM_B2) * (g * g)
    go_ref[...] = g
    mo_ref[...] = mn
    vo_ref[...] = vn
    d_ref[...] = -ADAM_LR * ((mn * c1) / (jnp.sqrt(vn * c2) + ADAM_EPS) + ADAM_WD * w_ref[...])


def _adam_w_in(parts_a, parts_b, parts_late, w, m, v):
    na = EARLY_A_ROWS // ROWS_RP

    def body(a_ref, b_ref, l_ref, w_ref, m_ref, v_ref, go_ref, d_ref, mo_ref, vo_ref):
        me = _me()
        first = pl.program_id(0) < na
        early = jnp.where(first, a_ref[0], b_ref[0]).astype(f32)
        late = l_ref[0].astype(f32)
        for i in range(1, NDEV):
            early = early + jnp.where(first, a_ref[i], b_ref[i]).astype(f32)
            late = late + l_ref[i].astype(f32)
        g = jnp.where(me >= LATE_DESTS - 1, early, 0.0) + jnp.where(me < LATE_DESTS, late, 0.0)
        _adam_update(g, w_ref, m_ref, v_ref, go_ref, d_ref, mo_ref, vo_ref)

    blk = pl.BlockSpec((None, ROWS_RP, SHARD), lambda i: (0, i, 0))
    return pl.pallas_call(
        body, name="adam_w_in", grid=(D // ROWS_RP,), out_shape=tuple(jax.ShapeDtypeStruct((1, D, SHARD), f32) for _ in range(4)),
        in_specs=[pl.BlockSpec((NDEV, ROWS_RP, SHARD), lambda i: (0, jnp.minimum(i, na - 1), 0)),
                  pl.BlockSpec((NDEV, ROWS_RP, SHARD), lambda i: (0, jnp.maximum(i - na, 0), 0)),
                  pl.BlockSpec((NDEV, ROWS_RP, SHARD), lambda i: (0, i, 0)), blk, blk, blk],
        out_specs=(blk, blk, blk, blk), compiler_params=_cp(("arbitrary",)),
    )(parts_a, parts_b, parts_late, w, m, v)


def _adam(parts, w, m, v, rows, name):
    p, r, c = parts.shape
    lead = w.ndim - 2

    def body(g_ref, w_ref, m_ref, v_ref, go_ref, d_ref, mo_ref, vo_ref):
        g = g_ref[0].astype(f32)
        for i in range(1, p):
            g = g + g_ref[i].astype(f32)
        _adam_update(g, w_ref, m_ref, v_ref, go_ref, d_ref, mo_ref, vo_ref)

    blk = pl.BlockSpec((None,) * lead + (rows, c), lambda i: (0,) * lead + (i, 0))
    return pl.pallas_call(
        body, name=name, grid=(r // rows,), out_shape=tuple(jax.ShapeDtypeStruct(w.shape, f32) for _ in range(4)),
        in_specs=[pl.BlockSpec((p, rows, c), lambda i: (0, i, 0)), blk, blk, blk], out_specs=(blk, blk, blk, blk),
        compiler_params=_cp(("arbitrary",)),
    )(parts, w, m, v)


def _in_proj(x, g, scale, shift, w_pad, tl, comm):
    l = x.shape[0]
    c_in, c_out, c_sems = comm.specs()

    def body(*refs):
        x_ref, g_ref, sc_ref, sh_ref, w_ref = refs[:5]
        cin = refs[5:5 + len(c_in)]
        p_ref, h_ref = refs[5 + len(c_in):7 + len(c_in)]
        cout = refs[7 + len(c_in):7 + len(c_in) + comm.n]
        sems = refs[7 + len(c_in) + comm.n:]

        def compute():
            xv = x_ref[...]
            r = lax.rsqrt(jnp.mean(xv * xv, axis=-1, keepdims=True) + EPS)
            h = (xv * r) * (g_ref[...] * (1.0 + sc_ref[...])) + sh_ref[...]
            hb = h.astype(bf16)
            h_ref[...] = hb
            p_ref[...] = jnp.dot(hb, w_ref[...], preferred_element_type=f32).astype(bf16)

        comm.run(cin, cout, sems, l // tl, compute)

    vec = pl.BlockSpec((1, D), lambda i: (0, 0))
    return pl.pallas_call(
        body, name="in_proj", grid=(l // tl,),
        out_shape=(jax.ShapeDtypeStruct((l, PW), bf16), jax.ShapeDtypeStruct((l, D), bf16)) + tuple(comm.outs),
        in_specs=[pl.BlockSpec((tl, D), lambda i: (i, 0)), vec, vec, vec, pl.BlockSpec((D, PW), lambda i: (0, 0))] + c_in,
        out_specs=(pl.BlockSpec((tl, PW), lambda i: (i, 0)), pl.BlockSpec((tl, D), lambda i: (i, 0))) + tuple(c_out),
        scratch_shapes=c_sems, compiler_params=_cp(("arbitrary",)),
    )(x, g, scale, shift, w_pad, *comm.ins)


def _tri(rev):
    i = lax.broadcasted_iota(jnp.int32, (CH, CH), 0)
    j = lax.broadcasted_iota(jnp.int32, (CH, CH), 1)
    return jnp.where((j >= i) if rev else (j <= i), 1.0, 0.0).astype(f32)


def _head_masks():
    lane = lax.broadcasted_iota(jnp.int32, (1, KW), 1) // HK
    return [jnp.where(lane == h, 1.0, 0.0).astype(f32) for h in range(NH)]


def _block_diag():
    r = lax.broadcasted_iota(jnp.int32, (VW, KW), 0) // HV
    c = lax.broadcasted_iota(jnp.int32, (VW, KW), 1) // HK
    return jnp.where(r == c, 1.0, 0.0).astype(f32)


def _decay(lr, w2, gb, tri, rev):
    logits = _dot(lr, w2) + gb
    a = _log_sigmoid(logits) * (1.0 / TAU)
    c = _dot_hi(tri, a)
    cl = c[0:1, :] if rev else c[CH - 1:CH, :]
    return logits, c, cl


def _stack_heads(t, hm):
    return jnp.concatenate([t * hm[h] for h in range(NH)], axis=0)


def _chunk_fwd(q, k, v, c, cl, st, tri, hm, bd):
    tri4 = jnp.concatenate([tri] * NH, axis=0)
    qd = q * jnp.exp(c) * QSCALE
    kd = k * jnp.exp(-c)
    kdec = k * jnp.exp(cl - c)
    pst = _nt(_stack_heads(qd, hm), kd) * tri4
    intra = jnp.concatenate([_dot(pst[h * CH:(h + 1) * CH], v[:, h * HV:(h + 1) * HV]) for h in range(NH)], axis=1)
    o = _nt(qd, st) + intra
    st_new = st * jnp.exp(cl) + bd * _tn(v, kdec)
    return o, st_new


def _state_fwd(k, v, c, cl, st, bd):
    return st * jnp.exp(cl) + bd * _tn(v, k * jnp.exp(cl - c))


def _chunk_bwd(q, k, v, c, cl, st0, dst, do, tri, trit, hm, bd):
    ecl = jnp.exp(cl)
    edec = jnp.exp(cl - c)
    kdec = k * edec
    dv = _nt(kdec, dst)
    dkdec = _dot(v, dst)
    dcl = jnp.sum(dst * st0, axis=0, keepdims=True) * ecl + jnp.sum(dkdec * kdec, axis=0, keepdims=True)
    dc = -dkdec * kdec
    dk = dkdec * edec
    dst0 = dst * ecl
    dq = None
    if q is not None:
        tri4 = jnp.concatenate([tri] * NH, axis=0)
        ec, enc = jnp.exp(c), jnp.exp(-c)
        qd = q * ec * QSCALE
        kd = k * enc
        qs = _stack_heads(qd, hm)
        pst = _nt(qs, kd) * tri4
        dpst = jnp.concatenate([_nt(do[:, h * HV:(h + 1) * HV], v[:, h * HV:(h + 1) * HV]) for h in range(NH)], axis=0) * tri4
        dv = dv + jnp.concatenate([_tn(pst[h * CH:(h + 1) * CH], do[:, h * HV:(h + 1) * HV]) for h in range(NH)], axis=1)
        dqd = _dot(do, st0)
        for h in range(NH):
            dqd = dqd + hm[h] * _dot(dpst[h * CH:(h + 1) * CH], kd)
        dkd = _tn(dpst, qs)
        dst0 = dst0 + bd * _tn(do, qd)
        dc = dc + dqd * qd - dkd * kd
        dq = dqd * ec * QSCALE
        dk = dk + dkd * enc
    da = _dot_hi(trit, dc) + dcl
    return dq, dk, dv, da, dst0


def _bdot(a, b):
    return lax.dot_general(a.astype(bf16), b.astype(bf16), (((2,), (1,)), ((0,), (0,))), preferred_element_type=f32)


def _bnt(a, b):
    return lax.dot_general(a.astype(bf16), b.astype(bf16), (((2,), (2,)), ((0,), (0,))), preferred_element_type=f32)


def _btn(a, b):
    return lax.dot_general(a.astype(bf16), b.astype(bf16), (((1,), (1,)), ((0,), (0,))), preferred_element_type=f32)


def _scan_chunks(x, rev):
    nc = x.shape[0]
    hi = x.astype(bf16)
    r1 = x - hi.astype(f32)
    mid = r1.astype(bf16)
    lo = (r1 - mid.astype(f32)).astype(bf16)
    terms = jnp.concatenate([hi, mid, lo], axis=1)
    tri3 = jnp.broadcast_to(jnp.concatenate([_tri(rev)] * 3, axis=1).astype(bf16)[None], (nc, CH, 3 * CH))
    return lax.dot_general(tri3, terms, (((2,), (1,)), ((0,), (0,))), preferred_element_type=f32)


class _Tile:
    pass


def _tile_prep(q_ref, k_ref, lr_ref, w2_ref, gb_ref, rev, nc):
    t = _Tile()
    tg = nc * CH
    t.logits = _dot(lr_ref[...], w2_ref[...]) + gb_ref[...]
    c = _scan_chunks((_log_sigmoid(t.logits) * (1.0 / TAU)).reshape(nc, CH, KW), rev)
    cl = c[:, 0:1, :] if rev else c[:, CH - 1:CH, :]
    k = k_ref[...].astype(f32).reshape(nc, CH, KW)
    t.ec, t.enc, t.edec, t.ecl = jnp.exp(c), jnp.exp(-c), jnp.exp(cl - c), jnp.exp(cl)
    t.qd = q_ref[...].astype(f32).reshape(nc, CH, KW) * t.ec * QSCALE
    t.kd = k * t.enc
    t.kdec = k * t.edec
    hm = _head_masks()
    t.tri4 = jnp.concatenate([_tri(rev)] * NH, axis=0)[None]
    t.qs = jnp.concatenate([t.qd * hm[h] for h in range(NH)], axis=1)
    t.pst = _bnt(t.qs, t.kd) * t.tri4
    return t


def _gla_fwd(p, w2p, gb, s0, rev, tg, name):
    l = p.shape[0]
    nb, nc = l // tg, tg // CH

    def body(q_ref, k_ref, v_ref, lr_ref, w2_ref, gb_ref, s0_ref, o_ref, st_ref, st):
        @pl.when(pl.program_id(0) == 0)
        def _():
            st[...] = s0_ref[...]

        t = _tile_prep(q_ref, k_ref, lr_ref, w2_ref, gb_ref, rev, nc)
        v = v_ref[...].reshape(nc, CH, VW)
        intra = jnp.concatenate([_bdot(t.pst[:, h * CH:(h + 1) * CH], v[:, :, h * HV:(h + 1) * HV]) for h in range(NH)], axis=2)
        kv = _btn(v, t.kdec) * _block_diag()[None]
        s = st[...]
        for n in (range(nc - 1, -1, -1) if rev else range(nc)):
            st_ref[n] = s
            s = s * t.ecl[n] + kv[n]
        st[...] = s
        o_ref[...] = (_bnt(t.qd, st_ref[...]) + intra).reshape(tg, VW)

    blk = (lambda i: nb - 1 - i) if rev else (lambda i: i)
    return pl.pallas_call(
        body, name=name, grid=(nb,),
        out_shape=(jax.ShapeDtypeStruct((l, VW), f32), jax.ShapeDtypeStruct((l // CH, VW, KW), f32)),
        in_specs=[pl.BlockSpec((tg, KW), lambda i: (blk(i), PQ // KW)), pl.BlockSpec((tg, KW), lambda i: (blk(i), PK // KW)),
                  pl.BlockSpec((tg, VW), lambda i: (blk(i), PV // VW)), pl.BlockSpec((tg, LRW), lambda i: (blk(i), PLR // LRW)),
                  pl.BlockSpec((LRW, KW), lambda i: (0, 0)), pl.BlockSpec((1, KW), lambda i: (0, 0)),
                  pl.BlockSpec((VW, KW), lambda i: (0, 0))],
        out_specs=(pl.BlockSpec((tg, VW), lambda i: (blk(i), 0)), pl.BlockSpec((nc, VW, KW), lambda i: (blk(i), 0, 0))),
        scratch_shapes=[pltpu.VMEM((VW, KW), f32)],
        compiler_params=_cp(("arbitrary",)),
    )(p, p, p, p, w2p, gb, s0)


def _gla_bwd(p, do, states, w2p, gb, prev, rev, tg, name, comm):
    l = p.shape[0]
    nb, nc = l // tg, tg // CH
    out_dt = f32 if prev is None else bf16
    c_in, c_out, c_sems = comm.specs()

    def body(*refs):
        q_ref, k_ref, v_ref, lr_ref, do_ref, st_ref, w2_ref, gb_ref = refs[:8]
        refs = refs[8:]
        if prev is not None:
            pq_ref, pl_ref = refs[:2]
            refs = refs[2:]
        cin, refs = refs[:len(c_in)], refs[len(c_in):]
        dqkv_ref, dlr_ref, dw2_ref, dgb_ref, ds0_ref = refs[:5]
        cout, dst, ds_buf, sems = refs[5:5 + comm.n], refs[5 + comm.n], refs[6 + comm.n], refs[7 + comm.n:]
        comm.run(cin, cout, sems, nb, lambda: compute(q_ref, k_ref, v_ref, lr_ref, do_ref, st_ref, w2_ref, gb_ref,
                                                      pq_ref if prev is not None else None, pl_ref if prev is not None else None,
                                                      dqkv_ref, dlr_ref, dw2_ref, dgb_ref, ds0_ref, dst, ds_buf))

    def compute(q_ref, k_ref, v_ref, lr_ref, do_ref, st_ref, w2_ref, gb_ref, pq_ref, pl_ref,
                dqkv_ref, dlr_ref, dw2_ref, dgb_ref, ds0_ref, dst, ds_buf):
        @pl.when(pl.program_id(0) == 0)
        def _():
            dst[...] = jnp.zeros_like(dst)
            dw2_ref[...] = jnp.zeros_like(dw2_ref)
            dgb_ref[...] = jnp.zeros_like(dgb_ref)

        t = _tile_prep(q_ref, k_ref, lr_ref, w2_ref, gb_ref, rev, nc)
        hm = _head_masks()
        v = v_ref[...].reshape(nc, CH, VW)
        do = do_ref[...].reshape(nc, CH, VW)
        heads = lambda a, h: a[:, :, h * HV:(h + 1) * HV]
        dpst = jnp.concatenate([_bnt(heads(do, h), heads(v, h)) for h in range(NH)], axis=1) * t.tri4
        dv = jnp.concatenate([_btn(t.pst[:, h * CH:(h + 1) * CH], heads(do, h)) for h in range(NH)], axis=2)
        dqd = _bdot(do, st_ref[...])
        for h in range(NH):
            dqd = dqd + hm[h] * _bdot(dpst[:, h * CH:(h + 1) * CH], t.kd)
        dkd = _btn(dpst, t.qs)
        u = _btn(do, t.qd) * _block_diag()[None]
        d = dst[...]
        for n in (range(nc) if rev else range(nc - 1, -1, -1)):
            ds_buf[n] = d
            d = d * t.ecl[n] + u[n]
        dst[...] = d
        ds0_ref[...] = d
        ds = ds_buf[...]
        dv = dv + _bnt(t.kdec, ds)
        dkdec = _bdot(v, ds)
        dcl = jnp.sum(ds * st_ref[...], axis=1, keepdims=True) * t.ecl + jnp.sum(dkdec * t.kdec, axis=1, keepdims=True)
        dc = dqd * t.qd - dkd * t.kd - dkdec * t.kdec
        da = _scan_chunks(dc, not rev) + dcl
        dq = dqd * t.ec * QSCALE
        dk = dkd * t.enc + dkdec * t.edec
        dlog = da.reshape(tg, KW) * _sigmoid(-t.logits) * (1.0 / TAU)
        dlr = _nt(dlog, w2_ref[...])
        dw2_ref[...] += _tn(lr_ref[...], dlog)
        dgb_ref[...] += jnp.sum(dlog, axis=0, keepdims=True)
        dqkv = jnp.concatenate([dq, dk, dv], axis=2).reshape(tg, 2 * KW + VW)
        if prev is not None:
            dqkv = dqkv + pq_ref[...]
            dlr = dlr + pl_ref[...]
        dqkv_ref[...] = dqkv.astype(out_dt)
        dlr_ref[...] = dlr.astype(out_dt)

    blk = (lambda i: i) if rev else (lambda i: nb - 1 - i)
    in_specs = [pl.BlockSpec((tg, KW), lambda i: (blk(i), PQ // KW)), pl.BlockSpec((tg, KW), lambda i: (blk(i), PK // KW)),
                pl.BlockSpec((tg, VW), lambda i: (blk(i), PV // VW)), pl.BlockSpec((tg, LRW), lambda i: (blk(i), PLR // LRW)),
                pl.BlockSpec((tg, VW), lambda i: (blk(i), 0)), pl.BlockSpec((nc, VW, KW), lambda i: (blk(i), 0, 0)),
                pl.BlockSpec((LRW, KW), lambda i: (0, 0)), pl.BlockSpec((1, KW), lambda i: (0, 0))]
    args = [p, p, p, p, do, states, w2p, gb]
    if prev is not None:
        in_specs += [pl.BlockSpec((tg, 2 * KW + VW), lambda i: (blk(i), 0)), pl.BlockSpec((tg, LRW), lambda i: (blk(i), 0))]
        args += list(prev)
    return pl.pallas_call(
        body, name=name, grid=(nb,),
        out_shape=(jax.ShapeDtypeStruct((l, 2 * KW + VW), out_dt), jax.ShapeDtypeStruct((l, LRW), out_dt),
                   jax.ShapeDtypeStruct((LRW, KW), f32), jax.ShapeDtypeStruct((1, KW), f32), jax.ShapeDtypeStruct((VW, KW), f32))
        + tuple(comm.outs),
        in_specs=in_specs + c_in,
        out_specs=(pl.BlockSpec((tg, 2 * KW + VW), lambda i: (blk(i), 0)), pl.BlockSpec((tg, LRW), lambda i: (blk(i), 0)),
                   pl.BlockSpec((LRW, KW), lambda i: (0, 0)), pl.BlockSpec((1, KW), lambda i: (0, 0)),
                   pl.BlockSpec((VW, KW), lambda i: (0, 0))) + tuple(c_out),
        scratch_shapes=[pltpu.VMEM((VW, KW), f32), pltpu.VMEM((nc, VW, KW), f32)] + c_sems,
        compiler_params=_cp(("arbitrary",)),
    )(*args, *comm.ins)


def _ctx_hidden(ctx_ref, g_ref, sc_ref, sh_ref):
    xv = ctx_ref[...]
    r = lax.rsqrt(jnp.mean(xv * xv, axis=-1, keepdims=True) + EPS)
    xn = xv * r
    return xn, xn * (g_ref[...] * (1.0 + sc_ref[...])) + sh_ref[...]


_CTX_W_SPECS = [pl.BlockSpec((D, KW), lambda i: (0, PK // KW)), pl.BlockSpec((D, VW), lambda i: (0, PV // VW)),
                pl.BlockSpec((D, LRW), lambda i: (0, PLR // LRW))]


def _ctx_fwd(ctx, g, scale, shift, w_pad, w2f, w2b, gbf, gbb):
    ncc = CTX // CH

    def body(ctx_ref, g_ref, sc_ref, sh_ref, wk_ref, wv_ref, wl_ref, w2f_ref, w2b_ref, gbf_ref, gbb_ref, sf_ref, sb_ref):
        _, hc = _ctx_hidden(ctx_ref, g_ref, sc_ref, sh_ref)
        k, v, lr = _dot(hc, wk_ref[...]), _dot(hc, wv_ref[...]), _dot(hc, wl_ref[...])
        bd = _block_diag()
        for rev, w2_ref, gb_ref, out in ((False, w2f_ref, gbf_ref, sf_ref), (True, w2b_ref, gbb_ref, sb_ref)):
            tri = _tri(rev)
            st = jnp.zeros((VW, KW), f32)
            for j in (range(ncc - 1, -1, -1) if rev else range(ncc)):
                rows = slice(j * CH, (j + 1) * CH)
                _, c, cl = _decay(lr[rows], w2_ref[...], gb_ref[...], tri, rev)
                st = _state_fwd(k[rows], v[rows], c, cl, st, bd)
            out[...] = st

    vec = pl.BlockSpec((1, D), lambda i: (0, 0))
    w2s = pl.BlockSpec((LRW, KW), lambda i: (0, 0))
    gbs = pl.BlockSpec((1, KW), lambda i: (0, 0))
    sts = pl.BlockSpec((VW, KW), lambda i: (0, 0))
    return pl.pallas_call(
        body, name="ctx_fwd", grid=(1,), out_shape=(jax.ShapeDtypeStruct((VW, KW), f32),) * 2,
        in_specs=[pl.BlockSpec((CTX, D), lambda i: (0, 0)), vec, vec, vec] + _CTX_W_SPECS + [w2s, w2s, gbs, gbs],
        out_specs=(sts, sts), compiler_params=_cp(("arbitrary",)),
    )(ctx, g, scale, shift, w_pad, w_pad, w_pad, w2f, w2b, gbf, gbb)


def _ctx_bwd(ctx, g, scale, shift, w_pad, w2f, w2b, gbf, gbb, dsf, dsb):
    ncc = CTX // CH

    def body(ctx_ref, g_ref, sc_ref, sh_ref, wk_ref, wv_ref, wl_ref, w2f_ref, w2b_ref, gbf_ref, gbb_ref, dsf_ref, dsb_ref,
             dwk_ref, dwv_ref, dwl_ref, dmod_ref, dg_ref, dw2_ref, dgb_ref):
        xn, hc = _ctx_hidden(ctx_ref, g_ref, sc_ref, sh_ref)
        k, v, lr = _dot(hc, wk_ref[...]), _dot(hc, wv_ref[...]), _dot(hc, wl_ref[...])
        bd = _block_diag()
        dk_rows, dv_rows, dl_rows = [None] * ncc, [None] * ncc, [None] * ncc
        for d, (rev, w2_ref, gb_ref, ds_ref) in enumerate(((False, w2f_ref, gbf_ref, dsf_ref), (True, w2b_ref, gbb_ref, dsb_ref))):
            tri = _tri(rev)
            order = list(range(ncc - 1, -1, -1) if rev else range(ncc))
            st, saved = jnp.zeros((VW, KW), f32), {}
            for j in order:
                rows = slice(j * CH, (j + 1) * CH)
                logits, c, cl = _decay(lr[rows], w2_ref[...], gb_ref[...], tri, rev)
                saved[j] = (logits, c, cl, st)
                st = _state_fwd(k[rows], v[rows], c, cl, st, bd)
            dst = ds_ref[...]
            dw2 = jnp.zeros((LRW, KW), f32)
            dgb = jnp.zeros((1, KW), f32)
            for j in reversed(order):
                rows = slice(j * CH, (j + 1) * CH)
                logits, c, cl, st0 = saved[j]
                _, dk, dv, da, dst = _chunk_bwd(None, k[rows], v[rows], c, cl, st0, dst, None, tri, _tri(not rev), None, bd)
                dlog = da * _sigmoid(-logits) * (1.0 / TAU)
                dl = _nt(dlog, w2_ref[...])
                dw2 = dw2 + _tn(lr[rows], dlog)
                dgb = dgb + jnp.sum(dlog, axis=0, keepdims=True)
                dk_rows[j] = dk if dk_rows[j] is None else dk_rows[j] + dk
                dv_rows[j] = dv if dv_rows[j] is None else dv_rows[j] + dv
                dl_rows[j] = dl if dl_rows[j] is None else dl_rows[j] + dl
            dw2_ref[d] = dw2
            dgb_ref[d] = dgb
        dk, dv, dl = (jnp.concatenate(t, axis=0) for t in (dk_rows, dv_rows, dl_rows))
        dwk_ref[...] = _tn(hc, dk)
        dwv_ref[...] = _tn(hc, dv)
        dwl_ref[...] = _tn(hc, dl)
        dh = _nt(dk, wk_ref[...]) + _nt(dv, wv_ref[...]) + _nt(dl, wl_ref[...])
        gx = dh * xn
        dmod_ref[:, 0:D] = jnp.sum(dh, axis=0, keepdims=True)
        dmod_ref[:, D:2 * D] = jnp.sum(gx, axis=0, keepdims=True) * g_ref[...]
        dg_ref[...] = jnp.sum(gx, axis=0, keepdims=True) * (1.0 + sc_ref[...])

    vec = pl.BlockSpec((1, D), lambda i: (0, 0))
    w2s = pl.BlockSpec((LRW, KW), lambda i: (0, 0))
    gbs = pl.BlockSpec((1, KW), lambda i: (0, 0))
    sts = pl.BlockSpec((VW, KW), lambda i: (0, 0))
    full = lambda *s: pl.BlockSpec(s, lambda i: (0,) * len(s))
    return pl.pallas_call(
        body, name="ctx_bwd", grid=(1,),
        out_shape=(jax.ShapeDtypeStruct((D, KW), f32), jax.ShapeDtypeStruct((D, VW), f32), jax.ShapeDtypeStruct((D, LRW), f32),
                   jax.ShapeDtypeStruct((1, 2 * D), f32), jax.ShapeDtypeStruct((1, D), f32),
                   jax.ShapeDtypeStruct((2, LRW, KW), f32), jax.ShapeDtypeStruct((2, 1, KW), f32)),
        in_specs=[pl.BlockSpec((CTX, D), lambda i: (0, 0)), vec, vec, vec] + _CTX_W_SPECS + [w2s, w2s, gbs, gbs, sts, sts],
        out_specs=(full(D, KW), full(D, VW), full(D, LRW), full(1, 2 * D), full(1, D), full(2, LRW, KW), full(2, 1, KW)),
        compiler_params=_cp(("arbitrary",)),
    )(ctx, g, scale, shift, w_pad, w_pad, w_pad, w2f, w2b, gbf, gbb, dsf, dsb)


def _layernorm(va, g, b):
    mu = jnp.mean(va, axis=-1, keepdims=True)
    xc = va - mu
    rstd = lax.rsqrt(jnp.mean(xc * xc, axis=-1, keepdims=True) + EPS)
    vhat = xc * rstd
    return vhat, rstd, vhat * g + b


def _mix_fwd(p, ln_g, ln_b, ws, bs_t):
    l = p.shape[0]
    nch = l // ACH
    half = AW // 2

    def body(p_ref, g_ref, b_ref, ws_ref, bs_ref, sv_ref, va_buf, col_buf, sem):
        cp = pltpu.make_async_copy(p_ref.at[:, pl.ds(PVA, AW)], va_buf, sem)
        cp.start()
        cp.wait()

        def rows_step(n, carry):
            rows = pl.ds(pl.multiple_of(n * ACH, ACH), ACH)
            _, _, vn = _layernorm(va_buf[rows, :].astype(f32), g_ref[...], b_ref[...])
            for gi in range(2):
                sl = slice(gi * ACH, (gi + 1) * ACH)
                sv_ref[rows, sl] = (_dot(ws_ref[gi], vn[:, sl]) + bs_ref[:, gi:gi + 1]).astype(bf16)
            col_buf[0, rows, :] = vn[:, half:half + ACH]
            col_buf[1, rows, :] = vn[:, half + ACH:]
            return carry

        lax.fori_loop(0, nch, rows_step, 0, unroll=4)

        def cols_step(cidx, carry):
            rows = pl.ds(cidx, ACH, stride=GW)
            for gi in range(2, 4):
                col_buf[gi - 2, rows, :] = _dot(ws_ref[gi], col_buf[gi - 2, rows, :]) + bs_ref[:, gi:gi + 1]
            return carry

        lax.fori_loop(0, GW, cols_step, 0, unroll=4)

        def out_step(n, carry):
            rows = pl.ds(pl.multiple_of(n * ACH, ACH), ACH)
            sv_ref[rows, half:half + ACH] = col_buf[0, rows, :].astype(bf16)
            sv_ref[rows, half + ACH:] = col_buf[1, rows, :].astype(bf16)
            return carry

        lax.fori_loop(0, nch, out_step, 0, unroll=4)

    vm = pl.BlockSpec(memory_space=pltpu.VMEM)
    return pl.pallas_call(
        body, name="mix_fwd", out_shape=jax.ShapeDtypeStruct((l, AW), bf16),
        in_specs=[pl.BlockSpec(memory_space=pl.ANY), vm, vm, vm, vm], out_specs=vm,
        scratch_shapes=[pltpu.VMEM((l, AW), bf16), pltpu.VMEM((2, l, ACH), f32), pltpu.SemaphoreType.DMA(())],
        compiler_params=_cp(),
    )(p, ln_g, ln_b, ws, bs_t)


def _mix_bwd(p, dsv, ln_g, ln_b, ws_t):
    l = p.shape[0]
    nch = l // ACH
    half = AW // 2

    def body(p_ref, dsv_hbm, g_ref, b_ref, wst_ref, dva_ref, dws_ref, dbs_ref, dg_ref, db_ref, va_buf, dsv_buf, vn_col, ds_col, sems):
        cp1 = pltpu.make_async_copy(p_ref.at[:, pl.ds(PVA, AW)], va_buf, sems.at[0])
        cp2 = pltpu.make_async_copy(dsv_hbm, dsv_buf, sems.at[1])
        cp1.start()
        cp2.start()
        dws_ref[...] = jnp.zeros_like(dws_ref)
        dbs_ref[...] = jnp.zeros_like(dbs_ref)
        dg_ref[...] = jnp.zeros_like(dg_ref)
        db_ref[...] = jnp.zeros_like(db_ref)
        cp1.wait()
        cp2.wait()

        def rows_step(n, carry):
            rows = pl.ds(pl.multiple_of(n * ACH, ACH), ACH)
            _, _, vn = _layernorm(va_buf[rows, :].astype(f32), g_ref[...], b_ref[...])
            ds = dsv_buf[rows, :].astype(f32)
            for gi in range(2):
                sl = slice(gi * ACH, (gi + 1) * ACH)
                dws_ref[gi] += _nt(ds[:, sl], vn[:, sl])
                dbs_ref[gi] += ds[:, sl]
            for gi in range(2):
                sl = slice(half + gi * ACH, half + (gi + 1) * ACH)
                vn_col[gi, rows, :] = vn[:, sl]
                ds_col[gi, rows, :] = ds[:, sl]
            return carry

        lax.fori_loop(0, nch, rows_step, 0, unroll=4)

        def cols_step(cidx, carry):
            rows = pl.ds(cidx, ACH, stride=GW)
            for gi in range(2, 4):
                ds = ds_col[gi - 2, rows, :]
                dws_ref[gi] += _nt(ds, vn_col[gi - 2, rows, :])
                dbs_ref[gi] += ds
                ds_col[gi - 2, rows, :] = _dot(wst_ref[gi], ds)
            return carry

        lax.fori_loop(0, GW, cols_step, 0, unroll=4)

        def out_step(n, carry):
            rows = pl.ds(pl.multiple_of(n * ACH, ACH), ACH)
            vhat, rstd, _ = _layernorm(va_buf[rows, :].astype(f32), g_ref[...], b_ref[...])
            ds = dsv_buf[rows, :].astype(f32)
            dvn = jnp.concatenate([_dot(wst_ref[0], ds[:, 0:ACH]), _dot(wst_ref[1], ds[:, ACH:half]), ds_col[0, rows, :], ds_col[1, rows, :]], axis=1)
            dg_ref[...] += jnp.sum(dvn * vhat, axis=0, keepdims=True)
            db_ref[...] += jnp.sum(dvn, axis=0, keepdims=True)
            dvh = dvn * g_ref[...]
            dva = rstd * (dvh - jnp.mean(dvh, axis=-1, keepdims=True) - vhat * jnp.mean(dvh * vhat, axis=-1, keepdims=True))
            dva_ref[rows, :] = dva.astype(bf16)
            return carry

        lax.fori_loop(0, nch, out_step, 0, unroll=4)

    vm = pl.BlockSpec(memory_space=pltpu.VMEM)
    hbm = pl.BlockSpec(memory_space=pl.ANY)
    return pl.pallas_call(
        body, name="mix_bwd",
        out_shape=(jax.ShapeDtypeStruct((l, AW), bf16), jax.ShapeDtypeStruct((4, ACH, ACH), f32), jax.ShapeDtypeStruct((4, ACH, ACH), f32),
                   jax.ShapeDtypeStruct((1, AW), f32), jax.ShapeDtypeStruct((1, AW), f32)),
        in_specs=[hbm, hbm, vm, vm, vm], out_specs=(vm, vm, vm, vm, vm),
        scratch_shapes=[pltpu.VMEM((l, AW), bf16), pltpu.VMEM((l, AW), bf16), pltpu.VMEM((2, l, ACH), f32), pltpu.VMEM((2, l, ACH), f32),
                        pltpu.SemaphoreType.DMA((2,))],
        compiler_params=_cp(),
    )(p, dsv, ln_g, ln_b, ws_t)


def _mid(x, tgt, p, o_f, o_b, sv, gate, gf, gb_norm, w_pa, w_pb, w_out, tl):
    l = x.shape[0]

    def body(x_ref, t_ref, zb_ref, ua_ref, za_ref, g1_ref, g2_ref, of_ref, ob_ref, sv_ref, gate_ref, gf_ref, gbn_ref,
             wpa_ref, wpb_ref, wout_ref,
             dx1_ref, dzbua_ref, dzag_ref, dsv_ref, do_ref, dwout_bf, dwpa_bf, dwpb_bf, dgf_ref, dgate_ref, dgbn_ref, loss_ref,
             dwout_ref, dwpa_ref, dwpb_ref):
        @pl.when(pl.program_id(0) == 0)
        def _():
            for r in (dwout_ref, dwpa_ref, dwpb_ref, dgf_ref, dgate_ref, dgbn_ref, loss_ref):
                r[...] = jnp.zeros_like(r)

        o = of_ref[...] + ob_ref[...]
        rr = jnp.concatenate(
            [jnp.broadcast_to(lax.rsqrt(jnp.mean(o[:, h * HV:(h + 1) * HV] ** 2, axis=-1, keepdims=True) + EPS), (tl, HV))
             for h in range(NH)], axis=1)
        ohat = o * rr
        on = ohat * gbn_ref[...]
        szb, dszb = _silu_and_grad(zb_ref[...].astype(f32))
        tb = on * szb
        u = ua_ref[...].astype(f32)
        svv = sv_ref[...].astype(f32)
        sza, dsza = _silu_and_grad(za_ref[...].astype(f32))
        ta = u * svv * sza
        ya = _dot(ta, wpa_ref[...])
        yb = _dot(tb, wpb_ref[...])
        g1 = _sigmoid(g1_ref[...].astype(f32))
        g2 = _sigmoid(g2_ref[...].astype(f32))
        m = g1 * ya + g2 * yb
        y2 = _dot(m, wout_ref[...])
        x1 = x_ref[...] + gate_ref[...] * y2
        r1 = lax.rsqrt(jnp.mean(x1 * x1, axis=-1, keepdims=True) + EPS)
        x1n = x1 * r1
        err = x1n * gf_ref[...] - t_ref[...]
        loss_ref[...] += jnp.sum(jnp.sum(err * err, axis=-1, keepdims=True), axis=0, keepdims=True) * (0.5 / D)
        dout = err * (1.0 / D)
        dgf_ref[...] += jnp.sum(dout * x1n, axis=0, keepdims=True)
        dx1n = dout * gf_ref[...]
        dx1 = r1 * (dx1n - x1n * jnp.mean(dx1n * x1n, axis=-1, keepdims=True))
        dx1_ref[...] = dx1
        dgate_ref[...] += jnp.sum(dx1 * y2, axis=0, keepdims=True)
        dy2 = dx1 * gate_ref[...]
        dwout_ref[...] += _tn(m, dy2)
        dm = _nt(dy2, wout_ref[...])
        dya = dm * g1
        dyb = dm * g2
        dzag_ref[:, AW:AW + D] = (dm * ya * g1 * (1.0 - g1)).astype(bf16)
        dzag_ref[:, AW + D:] = (dm * yb * g2 * (1.0 - g2)).astype(bf16)
        dwpa_ref[...] += _tn(ta, dya)
        dta = _nt(dya, wpa_ref[...])
        dzbua_ref[:, AW:] = (dta * svv * sza).astype(bf16)
        dsv_ref[...] = (dta * u * sza).astype(bf16)
        dzag_ref[:, 0:AW] = (dta * u * svv * dsza).astype(bf16)
        dwpb_ref[...] += _tn(tb, dyb)
        dtb = _nt(dyb, wpb_ref[...])
        don = dtb * szb
        dzbua_ref[:, 0:AW] = (dtb * on * dszb).astype(bf16)
        dgbn_ref[...] += jnp.sum(don * ohat, axis=0, keepdims=True)
        doh = don * gbn_ref[...]
        prod = doh * ohat
        mh = jnp.concatenate(
            [jnp.broadcast_to(jnp.mean(prod[:, h * HV:(h + 1) * HV], axis=-1, keepdims=True), (tl, HV)) for h in range(NH)], axis=1)
        do_ref[...] = (rr * (doh - ohat * mh)).astype(bf16)

        @pl.when(pl.program_id(0) == l // tl - 1)
        def _():
            for acc, out in ((dwout_ref, dwout_bf), (dwpa_ref, dwpa_bf), (dwpb_ref, dwpb_bf)):
                out[...] = acc[...].astype(bf16)

    row = lambda w, j: pl.BlockSpec((tl, w), lambda i, j=j: (i, j))
    full = lambda *s: pl.BlockSpec(s, lambda i: (0,) * len(s))
    return pl.pallas_call(
        body, name="mid", grid=(l // tl,),
        out_shape=(jax.ShapeDtypeStruct((l, D), f32), jax.ShapeDtypeStruct((l, 2 * AW), bf16), jax.ShapeDtypeStruct((l, AW + 2 * D), bf16),
                   jax.ShapeDtypeStruct((l, AW), bf16), jax.ShapeDtypeStruct((l, VW), bf16),
                   jax.ShapeDtypeStruct((D, D), bf16), jax.ShapeDtypeStruct((AW, D), bf16), jax.ShapeDtypeStruct((VW, D), bf16),
                   jax.ShapeDtypeStruct((1, D), f32), jax.ShapeDtypeStruct((1, D), f32), jax.ShapeDtypeStruct((1, VW), f32),
                   jax.ShapeDtypeStruct((1, 1), f32)),
        scratch_shapes=[pltpu.VMEM((D, D), f32), pltpu.VMEM((AW, D), f32), pltpu.VMEM((VW, D), f32)],
        in_specs=[row(D, 0), row(D, 0), row(AW, PZB // AW), row(AW, PUA // AW), row(AW, PZA // AW), row(D, PG1 // D), row(D, PG2 // D),
                  row(VW, 0), row(VW, 0), row(AW, 0), full(1, D), full(1, D), full(1, VW), full(AW, D), full(VW, D), full(D, D)],
        out_specs=(row(D, 0), row(2 * AW, 0), row(AW + 2 * D, 0), row(AW, 0), row(VW, 0),
                   full(D, D), full(AW, D), full(VW, D), full(1, D), full(1, D), full(1, VW), full(1, 1)),
        compiler_params=_cp(("arbitrary",)),
    )(x, tgt, p, p, p, p, p, o_f, o_b, sv, gate, gf, gb_norm, w_pa, w_pb, w_out)


def _in_bwd(x, dx1, dqkv, dzbua, dva, dzag, dlr, w_pad, g, scale, tl, comm):
    l = x.shape[0]
    c_in, c_out, c_sems = comm.specs()

    def body(*refs):
        cin = refs[14:14 + len(c_in)]
        outs = refs[14 + len(c_in):]
        comm.run(cin, outs[4:4 + comm.n], outs[4 + comm.n:], l // tl, lambda: compute(*refs[:14], *outs[:4]))

    def compute(x_ref, dx1_ref, a_ref, b_ref, c_ref, e_ref, lr_ref, wa_ref, wb_ref, wc_ref, we_ref, wl_ref, g_ref, sc_ref,
                gx_ref, dsh_ref, dsc_ref, dg_ref):
        @pl.when(pl.program_id(0) == 0)
        def _():
            for r in (dsh_ref, dsc_ref, dg_ref):
                r[...] = jnp.zeros_like(r)

        dh = (_nt(a_ref[...], wa_ref[...]) + _nt(b_ref[...], wb_ref[...]) + _nt(c_ref[...], wc_ref[...]) + _nt(e_ref[...], we_ref[...])
              + _nt(lr_ref[...], wl_ref[...]))
        xv = x_ref[...]
        r = lax.rsqrt(jnp.mean(xv * xv, axis=-1, keepdims=True) + EPS)
        xn = xv * r
        gxn = jnp.sum(dh * xn, axis=0, keepdims=True)
        dsh_ref[...] += jnp.sum(dh, axis=0, keepdims=True)
        dsc_ref[...] += gxn * g_ref[...]
        dg_ref[...] += gxn * (1.0 + sc_ref[...])
        dxn = dh * (g_ref[...] * (1.0 + sc_ref[...]))
        gx_ref[...] = dx1_ref[...] + r * (dxn - xn * jnp.mean(dxn * xn, axis=-1, keepdims=True))

    row = lambda w: pl.BlockSpec((tl, w), lambda i: (i, 0))
    wcol = lambda w, j: pl.BlockSpec((D, w), lambda i, j=j: (0, j))
    vec = pl.BlockSpec((1, D), lambda i: (0, 0))
    return pl.pallas_call(
        body, name="in_bwd", grid=(l // tl,),
        out_shape=(jax.ShapeDtypeStruct((l, D), f32),) + (jax.ShapeDtypeStruct((1, D), f32),) * 3 + tuple(comm.outs),
        in_specs=[row(D), row(D), row(2 * KW + VW), row(2 * AW), row(AW), row(AW + 2 * D), row(LRW),
                  wcol(2 * KW + VW, 0), wcol(2 * AW, PZB // (2 * AW)), wcol(AW, PVA // AW), wcol(AW + 2 * D, PZA // (AW + 2 * D)),
                  wcol(LRW, PLR // LRW), vec, vec] + c_in,
        out_specs=(row(D), vec, vec, vec) + tuple(c_out), scratch_shapes=c_sems,
        compiler_params=_cp(("arbitrary",)),
    )(x, dx1, dqkv, dzbua, dva, dzag, dlr, w_pad, w_pad, w_pad, w_pad, w_pad, g, scale, *comm.ins)


def _tn_matmul(a, bs, tl, name, comm=None):
    l, m = a.shape
    k = len(bs)
    comm = comm or _Comm([], [], None)
    c_in, c_out, c_sems = comm.specs()

    def body(a_ref, *refs):
        b_refs, cin = refs[:k], refs[k:k + len(c_in)]
        o_refs = refs[k + len(c_in):2 * k + len(c_in)]
        cout, sems = refs[2 * k + len(c_in):2 * k + len(c_in) + comm.n], refs[2 * k + len(c_in) + comm.n:]

        def compute():
            @pl.when(pl.program_id(0) == 0)
            def _():
                for o_ref in o_refs:
                    o_ref[...] = jnp.zeros_like(o_ref)

            av = a_ref[...]
            for b_ref, o_ref in zip(b_refs, o_refs):
                o_ref[...] += _tn(av, b_ref[...])

        comm.run(cin, cout, sems, l // tl, compute)

    return pl.pallas_call(
        body, name=name, grid=(l // tl,), out_shape=tuple(jax.ShapeDtypeStruct((m, b.shape[1]), f32) for b in bs) + tuple(comm.outs),
        in_specs=[pl.BlockSpec((tl, m), lambda i: (i, 0))] + [pl.BlockSpec((tl, b.shape[1]), lambda i: (i, 0)) for b in bs] + c_in,
        out_specs=tuple(pl.BlockSpec((m, b.shape[1]), lambda i: (0, 0)) for b in bs) + tuple(c_out),
        scratch_shapes=c_sems, compiler_params=_cp(("arbitrary",)),
    )(a, *bs, *comm.ins)


def _pad_gate(w2, gb):
    z = jnp.zeros((RANK, KW), f32)
    tail = jnp.zeros((LRW - 2 * RANK, KW), f32)
    w2f = jnp.concatenate([w2[0], z, tail], axis=0)
    w2b = jnp.concatenate([z, w2[1], tail], axis=0)
    return w2f, w2b, gb[0:1], gb[1:2]


EARLY_A_ROWS = 384


class _NoExchange:
    def __init__(self, w_pa, w_pb, w_out):
        self.weights = (w_pa, w_pb, w_out)

    def gather_proj(self):
        return _Comm([], [], None)

    def proj_weights(self, got):
        return self.weights

    def first(self, dw_out, dw_pa, dw_pb):
        return _Comm([], [], None)

    def early_a(self, small, blocks):
        return _Comm([], [], None)

    def early_b(self, blocks):
        return _Comm([], [], None)

    def late(self, blocks, dgt):
        return _Comm([], [], None)


class _Exchanges:
    def __init__(self, pa, pb, wo):
        self.shards = (pa, pb, wo)

    def gather_proj(self):
        def plan(i, o):
            srcs = [lambda j, r=r: r for r in i]
            dsts = [lambda j: o[0].at[:, _lanes(j)], lambda j: o[1].at[:, _lanes(j)], lambda j: o[2].at[j]]
            return srcs, dsts, None
        sds = jax.ShapeDtypeStruct
        return _Comm(self.shards, [sds((AW, D), bf16), sds((VW, D), bf16), sds((NDEV, 128, D), bf16)], plan)

    def proj_weights(self, got):
        return got[0], got[1], got[2].reshape(D, D)

    def first(self, dw_out, dw_pa, dw_pb):
        def plan(i, o):
            srcs = [lambda j: i[0].at[j], lambda j: i[1].at[:, _lanes(j)], lambda j: i[2].at[:, _lanes(j)]]
            dsts = [lambda j, r=r: r.at[j] for r in o]
            return srcs, dsts, None
        sds = jax.ShapeDtypeStruct
        return _Comm([dw_out.reshape(NDEV, 128, D), dw_pa, dw_pb],
                     [sds((NDEV, 128, D), bf16), sds((NDEV, AW, 128), bf16), sds((NDEV, VW, 128), bf16)], plan)

    def early_a(self, small, blocks):
        def plan(i, o):
            srcs = [lambda j: i[0], lambda j: i[1].at[j]]
            dsts = [lambda j, r=r: r.at[j] for r in o]
            return srcs, dsts, [None, lambda j: j >= LATE_DESTS - 1]
        sds = jax.ShapeDtypeStruct
        return _Comm([small, blocks], [sds((NDEV,) + small.shape, f32), sds(blocks.shape, bf16)], plan)

    def early_b(self, blocks):
        def plan(i, o):
            return [lambda j: i[0].at[j]], [lambda j: o[0].at[j]], [lambda j: j >= LATE_DESTS - 1]
        return _Comm([blocks], [jax.ShapeDtypeStruct(blocks.shape, bf16)], plan)

    def late(self, blocks, dgt):
        def plan(i, o):
            srcs = [lambda j: i[0].at[jnp.minimum(j, LATE_DESTS - 1)], lambda j: i[1].at[j]]
            dsts = [lambda j, r=r: r.at[j] for r in o]
            return srcs, dsts, [lambda j: j < LATE_DESTS, None]
        sds = jax.ShapeDtypeStruct
        return _Comm([blocks, dgt], [sds((NDEV, D, SHARD), bf16), sds(dgt.shape, f32)], plan)


def _local_step(x, ctx, tgt, mod, modc, norm_g, w_pad, ln_g, ln_b, ws, bs, w2, gb, gb_norm, gf, xch):
    shift, scale, gate = mod[:, 0:D], mod[:, D:2 * D], mod[:, 2 * D:]
    shift_c, scale_c = modc[:, 0:D], modc[:, D:]
    w2f, w2b, gbf, gbb = _pad_gate(w2, gb)

    p, h, *got_proj = _in_proj(x, norm_g, scale, shift, w_pad, 256, xch.gather_proj())
    w_pa, w_pb, w_out = xch.proj_weights(got_proj)
    sc_f, sc_b = _ctx_fwd(ctx, norm_g, scale_c, shift_c, w_pad, w2f, w2b, gbf, gbb)
    o_f, st_f = _gla_fwd(p, w2f, gbf, sc_f, False, 512, "gla_fwd_f")
    o_b, st_b = _gla_fwd(p, w2b, gbb, sc_b, True, 512, "gla_fwd_b")
    sv = _mix_fwd(p, ln_g, ln_b, ws.astype(bf16), bs.T)
    (dx1, dzbua, dzag, dsv, do, dw_out, dw_pa, dw_pb, dgf, dgate, dgbn, loss) = _mid(
        x, tgt, p, o_f, o_b, sv, gate, gf, gb_norm, w_pa, w_pb, w_out, 256)
    dva, dws, dbs_acc, dln_g, dln_b = _mix_bwd(p, dsv, ln_g, ln_b, jnp.swapaxes(ws, 1, 2).astype(bf16))
    dw_zbua, dw_zag, *got_first = _tn_matmul(h, [dzbua, dzag], 512, "dw_zbua_zag", xch.first(dw_out, dw_pa, dw_pb))
    (dw_va,) = _tn_matmul(h, [dva], 512, "dw_va")
    blocks_a = _pack_dw_early(dw_zbua, dw_va, dw_zag, 0, EARLY_A_ROWS, "pack_dw_early_a")
    blocks_b = _pack_dw_early(dw_zbua, dw_va, dw_zag, EARLY_A_ROWS, D - EARLY_A_ROWS, "pack_dw_early_b")
    small = _rows128(dln_g, dln_b, dws, jnp.sum(dbs_acc, axis=-1), dgbn, dgf, jnp.broadcast_to(loss, (1, 128)))

    dqkv_f, dlr_f, dw2f, dgbf, dsc_f, *got_a = _gla_bwd(p, do, st_f, w2f, gbf, None, False, 512, "gla_bwd_f",
                                                        xch.early_a(small, blocks_a))
    dqkv, dlr, dw2b, dgbb, dsc_b, *got_b = _gla_bwd(p, do, st_b, w2b, gbb, (dqkv_f, dlr_f), True, 512, "gla_bwd_b",
                                                    xch.early_b(blocks_b))
    dwk_c, dwv_c, dwl_c, dmodc, dg_c, dw2c, dgbc = _ctx_bwd(ctx, norm_g, scale_c, shift_c, w_pad, w2f, w2b, gbf, gbb, dsc_f, dsc_b)
    dw_qkv, dw_lr = _tn_matmul(h, [dqkv, dlr], 512, "dw_qkv_lr")
    blocks_late = _pack_dw_late(dw_qkv, dw_lr, dwk_c, dwv_c, dwl_c)
    dw2 = jnp.stack([dw2f[0:RANK] + dw2c[0, 0:RANK], dw2b[RANK:2 * RANK] + dw2c[1, RANK:2 * RANK]])
    dgb = jnp.concatenate([dgbf + dgbc[0], dgbb + dgbc[1]], axis=0)
    dgt = jnp.concatenate([jnp.transpose(dw2.reshape(2, RANK, NDEV, 32), (2, 0, 1, 3)).reshape(NDEV, 2 * RANK * 32),
                           jnp.transpose(dgb.reshape(2, NDEV, 32), (1, 0, 2)).reshape(NDEV, 64),
                           jnp.zeros((NDEV, 64), f32)], axis=1).reshape(NDEV, 9, 128)
    gx, dshift, dscale, dg, *got_late = _in_bwd(x, dx1, dqkv, dzbua, dva, dzag, dlr, w_pad, norm_g, scale, 256,
                                                xch.late(blocks_late, dgt))
    return dict(loss=loss, gx=gx, dmod=jnp.concatenate([dshift, dscale, dgate], axis=1), dmodc=dmodc, dnorm_g=dg + dg_c,
                small=small, blocks_a=blocks_a, blocks_b=blocks_b, blocks_late=blocks_late, dw2=dw2, dgb=dgb,
                dw_pa=dw_pa, dw_pb=dw_pb, dw_out=dw_out, got_first=got_first, got_a=got_a, got_b=got_b, got_late=got_late)


def _rows128(*vs):
    out = []
    for t in vs:
        t = t.reshape(-1)
        pad = (-t.shape[0]) % 128
        out.append(jnp.pad(t, (0, pad)) if pad else t)
    return jnp.concatenate(out).reshape(-1, 128)


def kernel(x, c, ctx, c_ctx, w_mod, b_mod, norm_g, w_in, a_ln_g, a_ln_b, a_ws, a_bs, b_gate_w2, b_gate_b, b_norm_g, w_proj_a, w_proj_b, w_out, final_norm_g, loss_target, m_c_ctx, m_w_mod, m_b_mod, m_norm_g, m_w_in, m_a_ln_g, m_a_ln_b, m_a_ws, m_a_bs, m_b_gate_w2, m_b_gate_b, m_b_norm_g, m_w_proj_a, m_w_proj_b, m_w_out, m_final_norm_g, v_c_ctx, v_w_mod, v_b_mod, v_norm_g, v_w_in, v_a_ln_g, v_a_ln_b, v_a_ws, v_a_bs, v_b_gate_w2, v_b_gate_b, v_b_norm_g, v_w_proj_a, v_w_proj_b, v_w_out, v_final_norm_g):
    me = _me()
    ncol = w_mod.shape[2]

    gate_mine = _rows128(jnp.concatenate([b_gate_w2.reshape(-1), b_gate_b.reshape(-1)]))
    early, wg, gates = _gather_first(_rows128(c), w_in[0].astype(bf16), gate_mine)
    cs = jnp.concatenate([early.reshape(NDEV, D), c_ctx.reshape(1, D), jnp.zeros((7, D), f32)], axis=0)
    w_pad = _repack_w(wg)
    gflat = gates.reshape(NDEV, 9 * 128)
    w2 = jnp.transpose(gflat[:, 0:2 * RANK * 32].reshape(NDEV, 2, RANK, 32), (1, 2, 0, 3)).reshape(2, RANK, KW)
    gb = jnp.transpose(gflat[:, 2 * RANK * 32:2 * RANK * 32 + 64].reshape(NDEV, 2, 32), (1, 0, 2)).reshape(2, KW)

    bm_mine = lax.dynamic_slice(b_mod, (0, me * ncol), (1, ncol))
    mods = _all_gather(_mod_fwd(cs, w_mod[0], bm_mine), "gather_mod")
    mods = jnp.transpose(mods, (1, 0, 2)).reshape(16, 3 * D)
    mod = lax.dynamic_slice(mods, (me, 0), (1, 3 * D))
    modc = mods[8:9, 0:2 * D]

    xch = _Exchanges(w_proj_a[0].astype(bf16), w_proj_b[0].astype(bf16), w_out[0].astype(bf16))
    r = _local_step(x[0], ctx[0], loss_target[0], mod, modc, norm_g, w_pad, a_ln_g, a_ln_b, a_ws[0], a_bs[0], w2, gb,
                    b_norm_g, final_norm_g.reshape(1, D), xch)
    p_out, p_pa, p_pb = r["got_first"]
    smalls_e, p_in_a = r["got_a"]
    (p_in_b,) = r["got_b"]
    p_in_late, p_gt = r["got_late"]

    def split(t, sizes, shapes):
        t = t.reshape(-1)
        out, o = [], 0
        for s, sh in zip(sizes, shapes):
            out.append(t[o:o + s].reshape(sh))
            o += s
        return out

    n_e = (AW + AW + 4 * ACH * ACH + AW + VW + D) // 128
    rep_e = _adam(smalls_e[:, 0:n_e], _rows128(a_ln_g, a_ln_b, a_ws, a_bs, b_norm_g, final_norm_g),
                  _rows128(m_a_ln_g, m_a_ln_b, m_a_ws, m_a_bs, m_b_norm_g, m_final_norm_g),
                  _rows128(v_a_ln_g, v_a_ln_b, v_a_ws, v_a_bs, v_b_norm_g, v_final_norm_g), n_e, "adam_rep_early")
    rep_e = [split(t, (AW, AW, 4 * ACH * ACH, AW, VW, D), ((1, AW), (1, AW), (1, 4, ACH, ACH), (1, 4, ACH), (1, VW), (D,)))
             for t in rep_e]
    losses = smalls_e[:, n_e, 0]
    loss = losses[0]
    for i in range(1, NDEV):
        loss = loss + losses[i]

    dbm = r["dmod"] + jnp.concatenate([r["dmodc"], jnp.zeros((1, D), f32)], axis=1)
    smalls_l = _all_gather(_rows128(r["dnorm_g"], dbm, r["dmod"], r["dmodc"]), "gather_small")
    n_l = (D + 3 * D) // 128
    rep_l = _adam(smalls_l[:, 0:n_l], _rows128(norm_g, b_mod), _rows128(m_norm_g, m_b_mod), _rows128(v_norm_g, v_b_mod), n_l,
                  "adam_rep_late")
    rep_l = [split(t, (D, 3 * D), ((1, D), (1, 3 * D))) for t in rep_l]
    tail = smalls_l[:, n_l:].reshape(NDEV, -1)
    dmods = tail[:, 0:3 * D]
    dmodc_all = tail[:, 3 * D:5 * D]
    dmodc_tot = dmodc_all[0:1]
    for i in range(1, NDEV):
        dmodc_tot = dmodc_tot + dmodc_all[i:i + 1]

    dm_rows = jnp.concatenate([dmods, jnp.concatenate([dmodc_tot, jnp.zeros((1, D), f32)], axis=1), jnp.zeros((7, 3 * D), f32)], axis=0)
    dm_mine = lax.dynamic_slice(dm_rows, (0, me * ncol), (16, ncol))
    g_wmod, gc_part = _mod_bwd(cs, dm_mine, w_mod[0])
    wm = _adam(g_wmod[None], w_mod, m_w_mod, v_w_mod, 256, "adam_wmod")
    gcs = _all_gather(gc_part.reshape(8, 128), "gather_cctx")
    cc = _adam(gcs, c_ctx.reshape(8, 128), m_c_ctx.reshape(8, 128), v_c_ctx.reshape(8, 128), 8, "adam_cctx")

    a_in = _adam_w_in(p_in_a, p_in_b, p_in_late, w_in, m_w_in, v_w_in)
    a_pa = _adam(p_pa, w_proj_a, m_w_proj_a, v_w_proj_a, AW, "adam_w_pa")
    a_pb = _adam(p_pb, w_proj_b, m_w_proj_b, v_w_proj_b, VW, "adam_w_pb")
    a_out = _adam(p_out, w_out, m_w_out, v_w_out, 128, "adam_w_out")
    gate_m = _rows128(jnp.concatenate([m_b_gate_w2.reshape(-1), m_b_gate_b.reshape(-1)]))
    gate_v = _rows128(jnp.concatenate([v_b_gate_w2.reshape(-1), v_b_gate_b.reshape(-1)]))
    a_gt = [t.reshape(-1) for t in _adam(p_gt, gate_mine, gate_m, gate_v, 9, "adam_gate")]
    nw2 = 2 * RANK * 32
    sh = [(a_in[k], a_pa[k], a_pb[k], a_out[k], a_gt[k][0:nw2].reshape(1, 2, RANK, 32),
           a_gt[k][nw2:nw2 + 64].reshape(1, 2, 32)) for k in range(4)]

    outs = [loss, r["gx"][None]]
    for k in range(4):
        lg, lb, aws, abs_, bng, fng = rep_e[k]
        n_g, bmod = rep_l[k]
        s_in, s_pa, s_pb, s_out, s_w2, s_gb = sh[k]
        outs += [cc[k].reshape(D), wm[k], bmod, n_g, s_in, lg, lb, aws, abs_, s_w2, s_gb, bng, s_pa, s_pb, s_out, fng]
    return tuple(outs)
```

```python
import functools

import jax
import jax.numpy as jnp
from jax import lax
from jax.experimental import pallas as pl
from jax.experimental.pallas import tpu as pltpu

f32, bf16 = jnp.float32, jnp.bfloat16

D = 1024
CTX = 256
EPS = 1e-6
AW = 512
ACH = 128
GW = 64
KW = 256
VW = 512
NH = 4
HK = 64
HV = 128
RANK = 16
TAU = 16.0
CH = 64
QSCALE = HK ** -0.5
INW = 5152
NDEV = 8

PQ, PK, PV, PZB, PUA, PVA, PZA, PG1, PG2, PLR, PW = 0, 256, 512, 1024, 1536, 2048, 2560, 3072, 4096, 5120, 5248
LRW = 128

ADAM_LR, ADAM_B1, ADAM_B2, ADAM_EPS, ADAM_WD, ADAM_STEP = 0.001, 0.9, 0.999, 1e-08, 0.01, 10

VMEM_LIMIT = 56 * 1024 * 1024
MESH = pl.DeviceIdType.MESH


def _cp(sem=None):
    return pltpu.CompilerParams(dimension_semantics=sem, vmem_limit_bytes=VMEM_LIMIT)


def _dot(a, b):
    return jnp.dot(a.astype(bf16), b.astype(bf16), preferred_element_type=f32)


def _nt(a, b):
    return lax.dot_general(a.astype(bf16), b.astype(bf16), (((1,), (1,)), ((), ())), preferred_element_type=f32)


def _tn(a, b):
    return lax.dot_general(a.astype(bf16), b.astype(bf16), (((0,), (0,)), ((), ())), preferred_element_type=f32)


def _dot_hi(a, b):
    return jnp.dot(a, b, preferred_element_type=f32, precision=lax.Precision.HIGHEST)


def _sigmoid(x):
    return 1.0 / (1.0 + jnp.exp(-x))


def _log_sigmoid(x):
    return jnp.minimum(x, 0.0) - jnp.log(1.0 + jnp.exp(-jnp.abs(x)))


def _silu_and_grad(z):
    s = _sigmoid(z)
    return z * s, s * (1.0 + z * (1.0 - s))


def _me():
    return 4 * lax.axis_index("x") + 2 * lax.axis_index("y") + lax.axis_index("c")


def _peer(k):
    x, y, c = lax.axis_index("x"), lax.axis_index("y"), lax.axis_index("c")
    px = 1 - x if k & 4 else x
    py = 1 - y if k & 2 else y
    pc = 1 - c if k & 1 else c
    return (px, py, pc), 4 * px + 2 * py + pc


def _all_gather(v, name):
    r, c = v.shape

    def body(v_ref, out_ref, send_sems, recv_sems, local_sem):
        me = _me()
        mine = pltpu.make_async_copy(v_ref, out_ref.at[me], local_sem)
        mine.start()
        sends, recvs = [], []
        for k in range(1, NDEV):
            dev, idx = _peer(k)
            sends.append(pltpu.make_async_remote_copy(
                src_ref=v_ref, dst_ref=out_ref.at[me], send_sem=send_sems.at[k - 1], recv_sem=recv_sems.at[k - 1],
                device_id=dev, device_id_type=MESH))
            recvs.append(pltpu.make_async_remote_copy(
                src_ref=v_ref, dst_ref=out_ref.at[idx], send_sem=send_sems.at[k - 1], recv_sem=recv_sems.at[k - 1],
                device_id=dev, device_id_type=MESH))
        for cp in sends:
            cp.start()
        for cp in recvs:
            cp.wait_recv()
        for cp in sends:
            cp.wait_send()
        mine.wait()

    return pl.pallas_call(
        body, name=name, out_shape=jax.ShapeDtypeStruct((NDEV, r, c), v.dtype),
        in_specs=[pl.BlockSpec(memory_space=pl.ANY)], out_specs=pl.BlockSpec(memory_space=pl.ANY),
        scratch_shapes=[pltpu.SemaphoreType.DMA((NDEV - 1,)), pltpu.SemaphoreType.DMA((NDEV - 1,)), pltpu.SemaphoreType.DMA(())],
    )(v)


def _fanout(srcs, dsts, send_sems, recv_sems, local_sems, owners=None):
    me = _me()
    n = len(srcs)
    owns = lambda a, j: True if owners is None or owners[a] is None else owners[a](j)

    def guarded(cond, fn):
        if cond is True:
            fn()
        else:
            pl.when(cond)(fn)

    def copies(with_recvs):
        local = [pltpu.make_async_copy(srcs[a](me), dsts[a](me), local_sems.at[a]) for a in range(n)]
        sends, recvs = [], []
        for k in range(1, NDEV):
            dev, idx = _peer(k)
            for a in range(n):
                s = (k - 1) * n + a
                sends.append((owns(a, idx), pltpu.make_async_remote_copy(
                    src_ref=srcs[a](idx), dst_ref=dsts[a](me), send_sem=send_sems.at[s], recv_sem=recv_sems.at[s],
                    device_id=dev, device_id_type=MESH)))
                if with_recvs:
                    recvs.append((owns(a, me), pltpu.make_async_remote_copy(
                        src_ref=srcs[a](idx), dst_ref=dsts[a](idx), send_sem=send_sems.at[s], recv_sem=recv_sems.at[s],
                        device_id=dev, device_id_type=MESH)))
        return local, sends, recvs

    def start():
        local, sends, _ = copies(False)
        for a, cp in enumerate(local):
            guarded(owns(a, me), cp.start)
        for cond, cp in sends:
            guarded(cond, cp.start)

    def finish():
        local, sends, recvs = copies(True)
        for cond, cp in recvs:
            guarded(cond, cp.wait_recv)
        for cond, cp in sends:
            guarded(cond, cp.wait_send)
        for a, cp in enumerate(local):
            guarded(owns(a, me), cp.wait)

    return start, finish


class _Comm:
    def __init__(self, ins, outs, plan):
        self.ins, self.outs, self.plan = list(ins), list(outs), plan
        self.n = len(self.outs)

    def specs(self):
        hbm = pl.BlockSpec(memory_space=pl.ANY)
        return [hbm] * len(self.ins), [hbm] * self.n, _fanout_sems(self.n) if self.n else []

    def run(self, in_refs, out_refs, sems, nsteps, compute):
        if not self.n:
            compute()
            return

        def hooks():
            srcs, dsts, owners = self.plan(in_refs, out_refs)
            return _fanout(srcs, dsts, sems[0], sems[1], sems[2], owners)

        pl.when(pl.program_id(0) == 0)(lambda: hooks()[0]())
        compute()
        pl.when(pl.program_id(0) == nsteps - 1)(lambda: hooks()[1]())


def _fanout_sems(n):
    return [pltpu.SemaphoreType.DMA(((NDEV - 1) * n,)), pltpu.SemaphoreType.DMA(((NDEV - 1) * n,)), pltpu.SemaphoreType.DMA((n,))]


def _lanes(j):
    return pl.ds(pl.multiple_of(j * 128, 128), 128)


def _gather_first(c_rows, wi, gate):
    def body(c_ref, wi_ref, g_ref, oc, owi, og, send_sems, recv_sems, local_sems):
        x, y, c = lax.axis_index("x"), lax.axis_index("y"), lax.axis_index("c")
        sibling = (x, y, 1 - c)
        chips = [(1 - x, y), (x, 1 - y), (1 - x, 1 - y)]
        index = lambda px, py, pc: 4 * px + 2 * py + pc
        arrays = ((c_ref, oc), (wi_ref, owi), (g_ref, og))
        n = len(arrays)

        def copy(a, k, block, to, own=False):
            src, out = arrays[a]
            return pltpu.make_async_remote_copy(
                src_ref=src if own else out.at[index(*block)], dst_ref=out.at[index(*block)],
                send_sem=send_sems.at[k * n + a], recv_sem=recv_sems.at[k * n + a], device_id=to, device_id_type=MESH)

        mine = [pltpu.make_async_copy(src, out.at[index(x, y, c)], local_sems.at[a]) for a, (src, out) in enumerate(arrays)]
        first = [copy(a, 0, (x, y, c), sibling, own=True) for a in range(n)]
        first += [copy(a, 1 + j, (x, y, c), (*chip, c), own=True) for j, chip in enumerate(chips) for a in range(n)]
        for cp in mine + first:
            cp.start()
        passed = []
        for j, chip in enumerate(chips):
            for a in range(n):
                copy(a, 1 + j, (*chip, c), (x, y, c)).wait_recv()
            for a in range(n):
                cp = copy(a, 4 + j, (*chip, c), sibling)
                cp.start()
                passed.append(cp)
        for a in range(n):
            copy(a, 0, sibling, (x, y, c)).wait_recv()
        for j, chip in enumerate(chips):
            for a in range(n):
                copy(a, 4 + j, (*chip, 1 - c), (x, y, c)).wait_recv()
        for cp in first + passed:
            cp.wait_send()
        for cp in mine:
            cp.wait()

    hbm = pl.BlockSpec(memory_space=pl.ANY)
    return pl.pallas_call(
        body, name="gather_first",
        out_shape=(jax.ShapeDtypeStruct((NDEV,) + c_rows.shape, f32), jax.ShapeDtypeStruct((NDEV,) + wi.shape, bf16),
                   jax.ShapeDtypeStruct((NDEV,) + gate.shape, f32)),
        in_specs=[hbm] * 3, out_specs=(hbm,) * 3, scratch_shapes=_fanout_sems(3),
    )(c_rows, wi, gate)


SHARD = INW // NDEV
ROWS_RP = 128


def _overlap(lo, hi, a, b):
    s, e = max(lo, a), min(hi, b)
    return (s, e) if s < e else None


def _repack_w(wg):
    segs = ((0, 1024, PQ), (1024, 1024 + 2 * RANK, PLR), (1024 + 2 * RANK, INW, PZB))

    def body(g_ref, o_ref):
        for j in range(NDEV):
            lo, hi = j * SHARD, (j + 1) * SHARD
            for a, b, pad0 in segs:
                ov = _overlap(lo, hi, a, b)
                if ov:
                    s, e = ov
                    o_ref[:, pad0 + s - a:pad0 + e - a] = g_ref[j, :, s - lo:e - lo]
        o_ref[:, PLR + 2 * RANK:PW] = jnp.zeros((ROWS_RP, PW - PLR - 2 * RANK), bf16)

    return pl.pallas_call(
        body, name="repack_w", grid=(D // ROWS_RP,), out_shape=jax.ShapeDtypeStruct((D, PW), bf16),
        in_specs=[pl.BlockSpec((NDEV, ROWS_RP, SHARD), lambda i: (0, i, 0))],
        out_specs=pl.BlockSpec((ROWS_RP, PW), lambda i: (i, 0)), compiler_params=_cp(("arbitrary",)),
    )(wg)


LATE_END = 1024 + 2 * RANK
LATE_DESTS = 2


def _pack_blocks(o_ref, srcs, dests, dtype):
    for n, j in enumerate(dests):
        lo, hi = j * SHARD, (j + 1) * SHARD
        done = lo
        for a, b, src in srcs:
            ov = _overlap(lo, hi, a, b)
            if ov:
                s, e = ov
                if s > done:
                    o_ref[n, :, done - lo:s - lo] = jnp.zeros((ROWS_RP, s - done), dtype)
                o_ref[n, :, s - lo:e - lo] = src[:, s - a:e - a].astype(dtype)
                done = e
        if done < hi:
            o_ref[n, :, done - lo:hi - lo] = jnp.zeros((ROWS_RP, hi - done), dtype)


def _pack_dw_early(dw_zbua, dw_va, dw_zag, row0, nrows, name):
    def body(zbua_ref, va_ref, zag_ref, o_ref):
        _pack_blocks(o_ref, ((LATE_END, 2080, zbua_ref), (2080, 2592, va_ref), (2592, INW, zag_ref)), range(NDEV), bf16)

    row = lambda w: pl.BlockSpec((ROWS_RP, w), lambda i: (i + row0 // ROWS_RP, 0))
    return pl.pallas_call(
        body, name=name, grid=(nrows // ROWS_RP,), out_shape=jax.ShapeDtypeStruct((NDEV, nrows, SHARD), bf16),
        in_specs=[row(2 * AW), row(AW), row(AW + 2 * D)],
        out_specs=pl.BlockSpec((NDEV, ROWS_RP, SHARD), lambda i: (0, i, 0)), compiler_params=_cp(("arbitrary",)),
    )(dw_zbua, dw_va, dw_zag)


def _pack_dw_late(dw_qkv, dw_lr, dwk_c, dwv_c, dwl_c):
    def body(qkv_ref, lr_ref, kc_ref, vc_ref, lc_ref, o_ref):
        qkv = qkv_ref[...] + jnp.concatenate([jnp.zeros((ROWS_RP, KW), f32), kc_ref[...], vc_ref[...]], axis=1)
        lr = lr_ref[...] + lc_ref[...]
        _pack_blocks(o_ref, ((0, 1024, qkv), (1024, LATE_END, lr)), range(LATE_DESTS), bf16)

    row = lambda w: pl.BlockSpec((ROWS_RP, w), lambda i: (i, 0))
    return pl.pallas_call(
        body, name="pack_dw_late", grid=(D // ROWS_RP,), out_shape=jax.ShapeDtypeStruct((LATE_DESTS, D, SHARD), bf16),
        in_specs=[row(2 * KW + VW), row(LRW), row(KW), row(VW), row(LRW)],
        out_specs=pl.BlockSpec((LATE_DESTS, ROWS_RP, SHARD), lambda i: (0, i, 0)), compiler_params=_cp(("arbitrary",)),
    )(dw_qkv, dw_lr, dwk_c, dwv_c, dwl_c)


def _mod_fwd(cs, wm, bm):
    def body(cs_ref, wm_ref, bm_ref, o_ref):
        s, _ = _silu_and_grad(cs_ref[...])
        o_ref[...] = _dot_hi(s, wm_ref[...]) + bm_ref[...]

    return pl.pallas_call(body, name="mod_fwd", out_shape=jax.ShapeDtypeStruct((16, wm.shape[1]), f32),
                          compiler_params=_cp())(cs, wm, bm)


def _mod_bwd(cs, dm, wm):
    def body(cs_ref, dm_ref, wm_ref, gw_ref, gc_ref):
        s, ds = _silu_and_grad(cs_ref[...])
        gw_ref[...] = lax.dot_general(s, dm_ref[...], (((0,), (0,)), ((), ())), preferred_element_type=f32,
                                      precision=lax.Precision.HIGHEST)
        part = lax.dot_general(dm_ref[8:9, :], wm_ref[...], (((1,), (1,)), ((), ())), preferred_element_type=f32,
                               precision=lax.Precision.HIGHEST)
        gc_ref[...] = part * ds[8:9, :]

    return pl.pallas_call(body, name="mod_bwd",
                          out_shape=(jax.ShapeDtypeStruct(wm.shape, f32), jax.ShapeDtypeStruct((1, D), f32)),
                          compiler_params=_cp())(cs, dm, wm)


def _adam_update(g, w_ref, m_ref, v_ref, go_ref, d_ref, mo_ref, vo_ref):
    c1 = 1.0 / (1.0 - ADAM_B1 ** ADAM_STEP)
    c2 = 1.0 / (1.0 - ADAM_B2 ** ADAM_STEP)
    mn = ADAM_B1 * m_ref[...] + (1.0 - ADAM_B1) * g
    vn = ADAM_B2 * v_ref[...] + (1.0 - ADAM_B2) * (g * g)
    go_ref[...] = g
    mo_ref[...] = mn
    vo_ref[...] = vn
    d_ref[...] = -ADAM_LR * ((mn * c1) / (jnp.sqrt(vn * c2) + ADAM_EPS) + ADAM_WD * w_ref[...])


def _adam_w_in(parts_a, parts_b, parts_late, w, m, v):
    na = EARLY_A_ROWS // ROWS_RP

    def body(a_ref, b_ref, l_ref, w_ref, m_ref, v_ref, go_ref, d_ref, mo_ref, vo_ref):
        me = _me()
        first = pl.program_id(0) < na
        early = jnp.where(first, a_ref[0], b_ref[0]).astype(f32)
        late = l_ref[0].astype(f32)
        for i in range(1, NDEV):
            early = early + jnp.where(first, a_ref[i], b_ref[i]).astype(f32)
            late = late + l_ref[i].astype(f32)
        g = jnp.where(me >= LATE_DESTS - 1, early, 0.0) + jnp.where(me < LATE_DESTS, late, 0.0)
        _adam_update(g, w_ref, m_ref, v_ref, go_ref, d_ref, mo_ref, vo_ref)

    blk = pl.BlockSpec((None, ROWS_RP, SHARD), lambda i: (0, i, 0))
    return pl.pallas_call(
        body, name="adam_w_in", grid=(D // ROWS_RP,), out_shape=tuple(jax.ShapeDtypeStruct((1, D, SHARD), f32) for _ in range(4)),
        in_specs=[pl.BlockSpec((NDEV, ROWS_RP, SHARD), lambda i: (0, jnp.minimum(i, na - 1), 0)),
                  pl.BlockSpec((NDEV, ROWS_RP, SHARD), lambda i: (0, jnp.maximum(i - na, 0), 0)),
                  pl.BlockSpec((NDEV, ROWS_RP, SHARD), lambda i: (0, i, 0)), blk, blk, blk],
        out_specs=(blk, blk, blk, blk), compiler_params=_cp(("arbitrary",)),
    )(parts_a, parts_b, parts_late, w, m, v)


def _adam_params(parts, ws, ms, vs, name):
    p = parts.shape[0]
    k = len(ws)
    nrows = [w.size // 128 for w in ws]
    starts = [sum(nrows[:i]) for i in range(k)]

    def shaped(g, shape):
        if len(shape) == 2:
            return jnp.concatenate([g[r:r + 1] for r in range(g.shape[0])], axis=1)
        return g.reshape(shape)

    def body(*refs):
        g_ref = refs[0]
        w_refs, m_refs, v_refs = refs[1:1 + k], refs[1 + k:1 + 2 * k], refs[1 + 2 * k:1 + 3 * k]
        outs = refs[1 + 3 * k:]
        for i in range(k):
            rows = slice(starts[i], starts[i] + nrows[i])
            g = g_ref[0, rows, :]
            for j in range(1, p):
                g = g + g_ref[j, rows, :]
            _adam_update(shaped(g, ws[i].shape), w_refs[i], m_refs[i], v_refs[i], outs[i], outs[k + i], outs[2 * k + i], outs[3 * k + i])

    vm = pl.BlockSpec(memory_space=pltpu.VMEM)
    res = pl.pallas_call(
        body, name=name, out_shape=tuple(jax.ShapeDtypeStruct(w.shape, f32) for _ in range(4) for w in ws),
        in_specs=[vm] * (1 + 3 * k), out_specs=(vm,) * (4 * k), compiler_params=_cp(),
    )(parts, *ws, *ms, *vs)
    return [res[i * k:(i + 1) * k] for i in range(4)]


def _adam(parts, w, m, v, rows, name):
    p, r, c = parts.shape
    lead = w.ndim - 2

    def body(g_ref, w_ref, m_ref, v_ref, go_ref, d_ref, mo_ref, vo_ref):
        g = g_ref[0].astype(f32)
        for i in range(1, p):
            g = g + g_ref[i].astype(f32)
        _adam_update(g, w_ref, m_ref, v_ref, go_ref, d_ref, mo_ref, vo_ref)

    blk = pl.BlockSpec((None,) * lead + (rows, c), lambda i: (0,) * lead + (i, 0))
    return pl.pallas_call(
        body, name=name, grid=(r // rows,), out_shape=tuple(jax.ShapeDtypeStruct(w.shape, f32) for _ in range(4)),
        in_specs=[pl.BlockSpec((p, rows, c), lambda i: (0, i, 0)), blk, blk, blk], out_specs=(blk, blk, blk, blk),
        compiler_params=_cp(("arbitrary",)),
    )(parts, w, m, v)


def _in_proj(x, g, scale, shift, w_pad, tl, comm):
    l = x.shape[0]
    c_in, c_out, c_sems = comm.specs()

    def body(*refs):
        x_ref, g_ref, sc_ref, sh_ref, w_ref = refs[:5]
        cin = refs[5:5 + len(c_in)]
        p_ref, h_ref = refs[5 + len(c_in):7 + len(c_in)]
        cout = refs[7 + len(c_in):7 + len(c_in) + comm.n]
        sems = refs[7 + len(c_in) + comm.n:]

        def compute():
            xv = x_ref[...]
            r = lax.rsqrt(jnp.mean(xv * xv, axis=-1, keepdims=True) + EPS)
            h = (xv * r) * (g_ref[...] * (1.0 + sc_ref[...])) + sh_ref[...]
            hb = h.astype(bf16)
            h_ref[...] = hb
            p_ref[...] = jnp.dot(hb, w_ref[...], preferred_element_type=f32).astype(bf16)

        comm.run(cin, cout, sems, l // tl, compute)

    vec = pl.BlockSpec((1, D), lambda i: (0, 0))
    return pl.pallas_call(
        body, name="in_proj", grid=(l // tl,),
        out_shape=(jax.ShapeDtypeStruct((l, PW), bf16), jax.ShapeDtypeStruct((l, D), bf16)) + tuple(comm.outs),
        in_specs=[pl.BlockSpec((tl, D), lambda i: (i, 0)), vec, vec, vec, pl.BlockSpec((D, PW), lambda i: (0, 0))] + c_in,
        out_specs=(pl.BlockSpec((tl, PW), lambda i: (i, 0)), pl.BlockSpec((tl, D), lambda i: (i, 0))) + tuple(c_out),
        scratch_shapes=c_sems, compiler_params=_cp(("arbitrary",)),
    )(x, g, scale, shift, w_pad, *comm.ins)


def _tri(rev):
    i = lax.broadcasted_iota(jnp.int32, (CH, CH), 0)
    j = lax.broadcasted_iota(jnp.int32, (CH, CH), 1)
    return jnp.where((j >= i) if rev else (j <= i), 1.0, 0.0).astype(f32)


def _head_masks():
    lane = lax.broadcasted_iota(jnp.int32, (1, KW), 1) // HK
    return [jnp.where(lane == h, 1.0, 0.0).astype(f32) for h in range(NH)]


def _block_diag():
    r = lax.broadcasted_iota(jnp.int32, (VW, KW), 0) // HV
    c = lax.broadcasted_iota(jnp.int32, (VW, KW), 1) // HK
    return jnp.where(r == c, 1.0, 0.0).astype(f32)


def _decay(lr, w2, gb, tri, rev):
    logits = _dot(lr, w2) + gb
    a = _log_sigmoid(logits) * (1.0 / TAU)
    c = _dot_hi(tri, a)
    cl = c[0:1, :] if rev else c[CH - 1:CH, :]
    return logits, c, cl


def _stack_heads(t, hm):
    return jnp.concatenate([t * hm[h] for h in range(NH)], axis=0)


def _chunk_fwd(q, k, v, c, cl, st, tri, hm, bd):
    tri4 = jnp.concatenate([tri] * NH, axis=0)
    qd = q * jnp.exp(c) * QSCALE
    kd = k * jnp.exp(-c)
    kdec = k * jnp.exp(cl - c)
    pst = _nt(_stack_heads(qd, hm), kd) * tri4
    intra = jnp.concatenate([_dot(pst[h * CH:(h + 1) * CH], v[:, h * HV:(h + 1) * HV]) for h in range(NH)], axis=1)
    o = _nt(qd, st) + intra
    st_new = st * jnp.exp(cl) + bd * _tn(v, kdec)
    return o, st_new


def _state_fwd(k, v, c, cl, st, bd):
    return st * jnp.exp(cl) + bd * _tn(v, k * jnp.exp(cl - c))


def _chunk_bwd(q, k, v, c, cl, st0, dst, do, tri, trit, hm, bd):
    ecl = jnp.exp(cl)
    edec = jnp.exp(cl - c)
    kdec = k * edec
    dv = _nt(kdec, dst)
    dkdec = _dot(v, dst)
    dcl = jnp.sum(dst * st0, axis=0, keepdims=True) * ecl + jnp.sum(dkdec * kdec, axis=0, keepdims=True)
    dc = -dkdec * kdec
    dk = dkdec * edec
    dst0 = dst * ecl
    dq = None
    if q is not None:
        tri4 = jnp.concatenate([tri] * NH, axis=0)
        ec, enc = jnp.exp(c), jnp.exp(-c)
        qd = q * ec * QSCALE
        kd = k * enc
        qs = _stack_heads(qd, hm)
        pst = _nt(qs, kd) * tri4
        dpst = jnp.concatenate([_nt(do[:, h * HV:(h + 1) * HV], v[:, h * HV:(h + 1) * HV]) for h in range(NH)], axis=0) * tri4
        dv = dv + jnp.concatenate([_tn(pst[h * CH:(h + 1) * CH], do[:, h * HV:(h + 1) * HV]) for h in range(NH)], axis=1)
        dqd = _dot(do, st0)
        for h in range(NH):
            dqd = dqd + hm[h] * _dot(dpst[h * CH:(h + 1) * CH], kd)
        dkd = _tn(dpst, qs)
        dst0 = dst0 + bd * _tn(do, qd)
        dc = dc + dqd * qd - dkd * kd
        dq = dqd * ec * QSCALE
        dk = dk + dkd * enc
    da = _dot_hi(trit, dc) + dcl
    return dq, dk, dv, da, dst0


def _bdot(a, b):
    return lax.dot_general(a.astype(bf16), b.astype(bf16), (((2,), (1,)), ((0,), (0,))), preferred_element_type=f32)


def _bnt(a, b):
    return lax.dot_general(a.astype(bf16), b.astype(bf16), (((2,), (2,)), ((0,), (0,))), preferred_element_type=f32)


def _btn(a, b):
    return lax.dot_general(a.astype(bf16), b.astype(bf16), (((1,), (1,)), ((0,), (0,))), preferred_element_type=f32)


def _scan_chunks(x, rev):
    nc = x.shape[0]
    hi = x.astype(bf16)
    r1 = x - hi.astype(f32)
    mid = r1.astype(bf16)
    lo = (r1 - mid.astype(f32)).astype(bf16)
    terms = jnp.concatenate([hi, mid, lo], axis=1)
    tri3 = jnp.broadcast_to(jnp.concatenate([_tri(rev)] * 3, axis=1).astype(bf16)[None], (nc, CH, 3 * CH))
    return lax.dot_general(tri3, terms, (((2,), (1,)), ((0,), (0,))), preferred_element_type=f32)


class _Tile:
    pass


def _tile_prep(q_ref, k_ref, lr_ref, w2_ref, gb_ref, rev, nc):
    t = _Tile()
    tg = nc * CH
    t.logits = _dot(lr_ref[...], w2_ref[...]) + gb_ref[...]
    c = _scan_chunks((_log_sigmoid(t.logits) * (1.0 / TAU)).reshape(nc, CH, KW), rev)
    cl = c[:, 0:1, :] if rev else c[:, CH - 1:CH, :]
    k = k_ref[...].astype(f32).reshape(nc, CH, KW)
    t.ec, t.enc, t.edec, t.ecl = jnp.exp(c), jnp.exp(-c), jnp.exp(cl - c), jnp.exp(cl)
    t.qd = q_ref[...].astype(f32).reshape(nc, CH, KW) * t.ec * QSCALE
    t.kd = k * t.enc
    t.kdec = k * t.edec
    hm = _head_masks()
    t.tri4 = jnp.concatenate([_tri(rev)] * NH, axis=0)[None]
    t.qs = jnp.concatenate([t.qd * hm[h] for h in range(NH)], axis=1)
    t.pst = _bnt(t.qs, t.kd) * t.tri4
    return t


def _gla_fwd(p, w2p, gb, s0, rev, tg, name):
    l = p.shape[0]
    nb, nc = l // tg, tg // CH

    def body(q_ref, k_ref, v_ref, lr_ref, w2_ref, gb_ref, s0_ref, o_ref, st_ref, st):
        @pl.when(pl.program_id(0) == 0)
        def _():
            st[...] = s0_ref[...]

        t = _tile_prep(q_ref, k_ref, lr_ref, w2_ref, gb_ref, rev, nc)
        v = v_ref[...].reshape(nc, CH, VW)
        intra = jnp.concatenate([_bdot(t.pst[:, h * CH:(h + 1) * CH], v[:, :, h * HV:(h + 1) * HV]) for h in range(NH)], axis=2)
        kv = _btn(v, t.kdec) * _block_diag()[None]
        s = st[...]
        for n in (range(nc - 1, -1, -1) if rev else range(nc)):
            st_ref[n] = s
            s = s * t.ecl[n] + kv[n]
        st[...] = s
        o_ref[...] = (_bnt(t.qd, st_ref[...]) + intra).reshape(tg, VW)

    blk = (lambda i: nb - 1 - i) if rev else (lambda i: i)
    return pl.pallas_call(
        body, name=name, grid=(nb,),
        out_shape=(jax.ShapeDtypeStruct((l, VW), f32), jax.ShapeDtypeStruct((l // CH, VW, KW), f32)),
        in_specs=[pl.BlockSpec((tg, KW), lambda i: (blk(i), PQ // KW)), pl.BlockSpec((tg, KW), lambda i: (blk(i), PK // KW)),
                  pl.BlockSpec((tg, VW), lambda i: (blk(i), PV // VW)), pl.BlockSpec((tg, LRW), lambda i: (blk(i), PLR // LRW)),
                  pl.BlockSpec((LRW, KW), lambda i: (0, 0)), pl.BlockSpec((1, KW), lambda i: (0, 0)),
                  pl.BlockSpec((VW, KW), lambda i: (0, 0))],
        out_specs=(pl.BlockSpec((tg, VW), lambda i: (blk(i), 0)), pl.BlockSpec((nc, VW, KW), lambda i: (blk(i), 0, 0))),
        scratch_shapes=[pltpu.VMEM((VW, KW), f32)],
        compiler_params=_cp(("arbitrary",)),
    )(p, p, p, p, w2p, gb, s0)


def _gla_bwd(p, do, states, w2p, gb, prev, rev, tg, name, comm):
    l = p.shape[0]
    nb, nc = l // tg, tg // CH
    out_dt = f32 if prev is None else bf16
    c_in, c_out, c_sems = comm.specs()

    def body(*refs):
        q_ref, k_ref, v_ref, lr_ref, do_ref, st_ref, w2_ref, gb_ref = refs[:8]
        refs = refs[8:]
        if prev is not None:
            pq_ref, pl_ref = refs[:2]
            refs = refs[2:]
        cin, refs = refs[:len(c_in)], refs[len(c_in):]
        dqkv_ref, dlr_ref, dw2_ref, dgb_ref, ds0_ref = refs[:5]
        cout, dst, ds_buf, sems = refs[5:5 + comm.n], refs[5 + comm.n], refs[6 + comm.n], refs[7 + comm.n:]
        comm.run(cin, cout, sems, nb, lambda: compute(q_ref, k_ref, v_ref, lr_ref, do_ref, st_ref, w2_ref, gb_ref,
                                                      pq_ref if prev is not None else None, pl_ref if prev is not None else None,
                                                      dqkv_ref, dlr_ref, dw2_ref, dgb_ref, ds0_ref, dst, ds_buf))

    def compute(q_ref, k_ref, v_ref, lr_ref, do_ref, st_ref, w2_ref, gb_ref, pq_ref, pl_ref,
                dqkv_ref, dlr_ref, dw2_ref, dgb_ref, ds0_ref, dst, ds_buf):
        @pl.when(pl.program_id(0) == 0)
        def _():
            dst[...] = jnp.zeros_like(dst)
            dw2_ref[...] = jnp.zeros_like(dw2_ref)
            dgb_ref[...] = jnp.zeros_like(dgb_ref)

        t = _tile_prep(q_ref, k_ref, lr_ref, w2_ref, gb_ref, rev, nc)
        hm = _head_masks()
        v = v_ref[...].reshape(nc, CH, VW)
        do = do_ref[...].reshape(nc, CH, VW)
        heads = lambda a, h: a[:, :, h * HV:(h + 1) * HV]
        dpst = jnp.concatenate([_bnt(heads(do, h), heads(v, h)) for h in range(NH)], axis=1) * t.tri4
        dv = jnp.concatenate([_btn(t.pst[:, h * CH:(h + 1) * CH], heads(do, h)) for h in range(NH)], axis=2)
        dqd = _bdot(do, st_ref[...])
        for h in range(NH):
            dqd = dqd + hm[h] * _bdot(dpst[:, h * CH:(h + 1) * CH], t.kd)
        dkd = _btn(dpst, t.qs)
        u = _btn(do, t.qd) * _block_diag()[None]
        d = dst[...]
        for n in (range(nc) if rev else range(nc - 1, -1, -1)):
            ds_buf[n] = d
            d = d * t.ecl[n] + u[n]
        dst[...] = d
        ds0_ref[...] = d
        ds = ds_buf[...]
        dv = dv + _bnt(t.kdec, ds)
        dkdec = _bdot(v, ds)
        dcl = jnp.sum(ds * st_ref[...], axis=1, keepdims=True) * t.ecl + jnp.sum(dkdec * t.kdec, axis=1, keepdims=True)
        dc = dqd * t.qd - dkd * t.kd - dkdec * t.kdec
        da = _scan_chunks(dc, not rev) + dcl
        dq = dqd * t.ec * QSCALE
        dk = dkd * t.enc + dkdec * t.edec
        dlog = da.reshape(tg, KW) * _sigmoid(-t.logits) * (1.0 / TAU)
        dlr = _nt(dlog, w2_ref[...])
        dw2_ref[...] += _tn(lr_ref[...], dlog)
        dgb_ref[...] += jnp.sum(dlog, axis=0, keepdims=True)
        dqkv = jnp.concatenate([dq, dk, dv], axis=2).reshape(tg, 2 * KW + VW)
        if prev is not None:
            dqkv = dqkv + pq_ref[...]
            dlr = dlr + pl_ref[...]
        dqkv_ref[...] = dqkv.astype(out_dt)
        dlr_ref[...] = dlr.astype(out_dt)

    blk = (lambda i: i) if rev else (lambda i: nb - 1 - i)
    in_specs = [pl.BlockSpec((tg, KW), lambda i: (blk(i), PQ // KW)), pl.BlockSpec((tg, KW), lambda i: (blk(i), PK // KW)),
                pl.BlockSpec((tg, VW), lambda i: (blk(i), PV // VW)), pl.BlockSpec((tg, LRW), lambda i: (blk(i), PLR // LRW)),
                pl.BlockSpec((tg, VW), lambda i: (blk(i), 0)), pl.BlockSpec((nc, VW, KW), lambda i: (blk(i), 0, 0)),
                pl.BlockSpec((LRW, KW), lambda i: (0, 0)), pl.BlockSpec((1, KW), lambda i: (0, 0))]
    args = [p, p, p, p, do, states, w2p, gb]
    if prev is not None:
        in_specs += [pl.BlockSpec((tg, 2 * KW + VW), lambda i: (blk(i), 0)), pl.BlockSpec((tg, LRW), lambda i: (blk(i), 0))]
        args += list(prev)
    return pl.pallas_call(
        body, name=name, grid=(nb,),
        out_shape=(jax.ShapeDtypeStruct((l, 2 * KW + VW), out_dt), jax.ShapeDtypeStruct((l, LRW), out_dt),
                   jax.ShapeDtypeStruct((LRW, KW), f32), jax.ShapeDtypeStruct((1, KW), f32), jax.ShapeDtypeStruct((VW, KW), f32))
        + tuple(comm.outs),
        in_specs=in_specs + c_in,
        out_specs=(pl.BlockSpec((tg, 2 * KW + VW), lambda i: (blk(i), 0)), pl.BlockSpec((tg, LRW), lambda i: (blk(i), 0)),
                   pl.BlockSpec((LRW, KW), lambda i: (0, 0)), pl.BlockSpec((1, KW), lambda i: (0, 0)),
                   pl.BlockSpec((VW, KW), lambda i: (0, 0))) + tuple(c_out),
        scratch_shapes=[pltpu.VMEM((VW, KW), f32), pltpu.VMEM((nc, VW, KW), f32)] + c_sems,
        compiler_params=_cp(("arbitrary",)),
    )(*args, *comm.ins)


def _ctx_hidden(ctx_ref, g_ref, sc_ref, sh_ref):
    xv = ctx_ref[...]
    r = lax.rsqrt(jnp.mean(xv * xv, axis=-1, keepdims=True) + EPS)
    xn = xv * r
    return xn, xn * (g_ref[...] * (1.0 + sc_ref[...])) + sh_ref[...]


_CTX_W_SPECS = [pl.BlockSpec((D, KW), lambda i: (0, PK // KW)), pl.BlockSpec((D, VW), lambda i: (0, PV // VW)),
                pl.BlockSpec((D, LRW), lambda i: (0, PLR // LRW))]


def _ctx_fwd(ctx, g, scale, shift, w_pad, w2f, w2b, gbf, gbb):
    ncc = CTX // CH

    def body(ctx_ref, g_ref, sc_ref, sh_ref, wk_ref, wv_ref, wl_ref, w2f_ref, w2b_ref, gbf_ref, gbb_ref, sf_ref, sb_ref):
        _, hc = _ctx_hidden(ctx_ref, g_ref, sc_ref, sh_ref)
        k, v, lr = _dot(hc, wk_ref[...]), _dot(hc, wv_ref[...]), _dot(hc, wl_ref[...])
        bd = _block_diag()
        for rev, w2_ref, gb_ref, out in ((False, w2f_ref, gbf_ref, sf_ref), (True, w2b_ref, gbb_ref, sb_ref)):
            tri = _tri(rev)
            st = jnp.zeros((VW, KW), f32)
            for j in (range(ncc - 1, -1, -1) if rev else range(ncc)):
                rows = slice(j * CH, (j + 1) * CH)
                _, c, cl = _decay(lr[rows], w2_ref[...], gb_ref[...], tri, rev)
                st = _state_fwd(k[rows], v[rows], c, cl, st, bd)
            out[...] = st

    vec = pl.BlockSpec((1, D), lambda i: (0, 0))
    w2s = pl.BlockSpec((LRW, KW), lambda i: (0, 0))
    gbs = pl.BlockSpec((1, KW), lambda i: (0, 0))
    sts = pl.BlockSpec((VW, KW), lambda i: (0, 0))
    return pl.pallas_call(
        body, name="ctx_fwd", grid=(1,), out_shape=(jax.ShapeDtypeStruct((VW, KW), f32),) * 2,
        in_specs=[pl.BlockSpec((CTX, D), lambda i: (0, 0)), vec, vec, vec] + _CTX_W_SPECS + [w2s, w2s, gbs, gbs],
        out_specs=(sts, sts), compiler_params=_cp(("arbitrary",)),
    )(ctx, g, scale, shift, w_pad, w_pad, w_pad, w2f, w2b, gbf, gbb)


def _ctx_bwd(ctx, g, scale, shift, w_pad, w2f, w2b, gbf, gbb, dsf, dsb):
    ncc = CTX // CH

    def body(ctx_ref, g_ref, sc_ref, sh_ref, wk_ref, wv_ref, wl_ref, w2f_ref, w2b_ref, gbf_ref, gbb_ref, dsf_ref, dsb_ref,
             dwk_ref, dwv_ref, dwl_ref, dmod_ref, dg_ref, dw2_ref, dgb_ref):
        xn, hc = _ctx_hidden(ctx_ref, g_ref, sc_ref, sh_ref)
        k, v, lr = _dot(hc, wk_ref[...]), _dot(hc, wv_ref[...]), _dot(hc, wl_ref[...])
        bd = _block_diag()
        dk_rows, dv_rows, dl_rows = [None] * ncc, [None] * ncc, [None] * ncc
        for d, (rev, w2_ref, gb_ref, ds_ref) in enumerate(((False, w2f_ref, gbf_ref, dsf_ref), (True, w2b_ref, gbb_ref, dsb_ref))):
            tri = _tri(rev)
            order = list(range(ncc - 1, -1, -1) if rev else range(ncc))
            st, saved = jnp.zeros((VW, KW), f32), {}
            for j in order:
                rows = slice(j * CH, (j + 1) * CH)
                logits, c, cl = _decay(lr[rows], w2_ref[...], gb_ref[...], tri, rev)
                saved[j] = (logits, c, cl, st)
                st = _state_fwd(k[rows], v[rows], c, cl, st, bd)
            dst = ds_ref[...]
            dw2 = jnp.zeros((LRW, KW), f32)
            dgb = jnp.zeros((1, KW), f32)
            for j in reversed(order):
                rows = slice(j * CH, (j + 1) * CH)
                logits, c, cl, st0 = saved[j]
                _, dk, dv, da, dst = _chunk_bwd(None, k[rows], v[rows], c, cl, st0, dst, None, tri, _tri(not rev), None, bd)
                dlog = da * _sigmoid(-logits) * (1.0 / TAU)
                dl = _nt(dlog, w2_ref[...])
                dw2 = dw2 + _tn(lr[rows], dlog)
                dgb = dgb + jnp.sum(dlog, axis=0, keepdims=True)
                dk_rows[j] = dk if dk_rows[j] is None else dk_rows[j] + dk
                dv_rows[j] = dv if dv_rows[j] is None else dv_rows[j] + dv
                dl_rows[j] = dl if dl_rows[j] is None else dl_rows[j] + dl
            dw2_ref[d] = dw2
            dgb_ref[d] = dgb
        dk, dv, dl = (jnp.concatenate(t, axis=0) for t in (dk_rows, dv_rows, dl_rows))
        dwk_ref[...] = _tn(hc, dk)
        dwv_ref[...] = _tn(hc, dv)
        dwl_ref[...] = _tn(hc, dl)
        dh = _nt(dk, wk_ref[...]) + _nt(dv, wv_ref[...]) + _nt(dl, wl_ref[...])
        gx = dh * xn
        dmod_ref[:, 0:D] = jnp.sum(dh, axis=0, keepdims=True)
        dmod_ref[:, D:2 * D] = jnp.sum(gx, axis=0, keepdims=True) * g_ref[...]
        dg_ref[...] = jnp.sum(gx, axis=0, keepdims=True) * (1.0 + sc_ref[...])

    vec = pl.BlockSpec((1, D), lambda i: (0, 0))
    w2s = pl.BlockSpec((LRW, KW), lambda i: (0, 0))
    gbs = pl.BlockSpec((1, KW), lambda i: (0, 0))
    sts = pl.BlockSpec((VW, KW), lambda i: (0, 0))
    full = lambda *s: pl.BlockSpec(s, lambda i: (0,) * len(s))
    return pl.pallas_call(
        body, name="ctx_bwd", grid=(1,),
        out_shape=(jax.ShapeDtypeStruct((D, KW), f32), jax.ShapeDtypeStruct((D, VW), f32), jax.ShapeDtypeStruct((D, LRW), f32),
                   jax.ShapeDtypeStruct((1, 2 * D), f32), jax.ShapeDtypeStruct((1, D), f32),
                   jax.ShapeDtypeStruct((2, LRW, KW), f32), jax.ShapeDtypeStruct((2, 1, KW), f32)),
        in_specs=[pl.BlockSpec((CTX, D), lambda i: (0, 0)), vec, vec, vec] + _CTX_W_SPECS + [w2s, w2s, gbs, gbs, sts, sts],
        out_specs=(full(D, KW), full(D, VW), full(D, LRW), full(1, 2 * D), full(1, D), full(2, LRW, KW), full(2, 1, KW)),
        compiler_params=_cp(("arbitrary",)),
    )(ctx, g, scale, shift, w_pad, w_pad, w_pad, w2f, w2b, gbf, gbb, dsf, dsb)


def _layernorm(va, g, b):
    mu = jnp.mean(va, axis=-1, keepdims=True)
    xc = va - mu
    rstd = lax.rsqrt(jnp.mean(xc * xc, axis=-1, keepdims=True) + EPS)
    vhat = xc * rstd
    return vhat, rstd, vhat * g + b


def _mix_fwd(p, ln_g, ln_b, ws, bs_t):
    l = p.shape[0]
    nch = l // ACH
    half = AW // 2

    def body(p_ref, g_ref, b_ref, ws_ref, bs_ref, sv_ref, va_buf, col_buf, sem):
        cp = pltpu.make_async_copy(p_ref.at[:, pl.ds(PVA, AW)], va_buf, sem)
        cp.start()
        cp.wait()

        def rows_step(n, carry):
            rows = pl.ds(pl.multiple_of(n * ACH, ACH), ACH)
            _, _, vn = _layernorm(va_buf[rows, :].astype(f32), g_ref[...], b_ref[...])
            for gi in range(2):
                sl = slice(gi * ACH, (gi + 1) * ACH)
                sv_ref[rows, sl] = (_dot(ws_ref[gi], vn[:, sl]) + bs_ref[:, gi:gi + 1]).astype(bf16)
            col_buf[0, rows, :] = vn[:, half:half + ACH]
            col_buf[1, rows, :] = vn[:, half + ACH:]
            return carry

        lax.fori_loop(0, nch, rows_step, 0, unroll=4)

        def cols_step(cidx, carry):
            rows = pl.ds(cidx, ACH, stride=GW)
            for gi in range(2, 4):
                col_buf[gi - 2, rows, :] = _dot(ws_ref[gi], col_buf[gi - 2, rows, :]) + bs_ref[:, gi:gi + 1]
            return carry

        lax.fori_loop(0, GW, cols_step, 0, unroll=4)

        def out_step(n, carry):
            rows = pl.ds(pl.multiple_of(n * ACH, ACH), ACH)
            sv_ref[rows, half:half + ACH] = col_buf[0, rows, :].astype(bf16)
            sv_ref[rows, half + ACH:] = col_buf[1, rows, :].astype(bf16)
            return carry

        lax.fori_loop(0, nch, out_step, 0, unroll=4)

    vm = pl.BlockSpec(memory_space=pltpu.VMEM)
    return pl.pallas_call(
        body, name="mix_fwd", out_shape=jax.ShapeDtypeStruct((l, AW), bf16),
        in_specs=[pl.BlockSpec(memory_space=pl.ANY), vm, vm, vm, vm], out_specs=vm,
        scratch_shapes=[pltpu.VMEM((l, AW), bf16), pltpu.VMEM((2, l, ACH), f32), pltpu.SemaphoreType.DMA(())],
        compiler_params=_cp(),
    )(p, ln_g, ln_b, ws, bs_t)


def _mix_bwd(p, dsv, ln_g, ln_b, ws_t):
    l = p.shape[0]
    nch = l // ACH
    half = AW // 2

    def body(p_ref, dsv_hbm, g_ref, b_ref, wst_ref, dva_ref, dws_ref, dbs_ref, dg_ref, db_ref, va_buf, dsv_buf, vn_col, ds_col, sems):
        cp1 = pltpu.make_async_copy(p_ref.at[:, pl.ds(PVA, AW)], va_buf, sems.at[0])
        cp2 = pltpu.make_async_copy(dsv_hbm, dsv_buf, sems.at[1])
        cp1.start()
        cp2.start()
        dws_ref[...] = jnp.zeros_like(dws_ref)
        dbs_ref[...] = jnp.zeros_like(dbs_ref)
        dg_ref[...] = jnp.zeros_like(dg_ref)
        db_ref[...] = jnp.zeros_like(db_ref)
        cp1.wait()
        cp2.wait()

        def rows_step(n, carry):
            rows = pl.ds(pl.multiple_of(n * ACH, ACH), ACH)
            _, _, vn = _layernorm(va_buf[rows, :].astype(f32), g_ref[...], b_ref[...])
            ds = dsv_buf[rows, :].astype(f32)
            for gi in range(2):
                sl = slice(gi * ACH, (gi + 1) * ACH)
                dws_ref[gi] += _nt(ds[:, sl], vn[:, sl])
                dbs_ref[gi] += ds[:, sl]
            for gi in range(2):
                sl = slice(half + gi * ACH, half + (gi + 1) * ACH)
                vn_col[gi, rows, :] = vn[:, sl]
                ds_col[gi, rows, :] = ds[:, sl]
            return carry

        lax.fori_loop(0, nch, rows_step, 0, unroll=4)

        def cols_step(cidx, carry):
            rows = pl.ds(cidx, ACH, stride=GW)
            for gi in range(2, 4):
                ds = ds_col[gi - 2, rows, :]
                dws_ref[gi] += _nt(ds, vn_col[gi - 2, rows, :])
                dbs_ref[gi] += ds
                ds_col[gi - 2, rows, :] = _dot(wst_ref[gi], ds)
            return carry

        lax.fori_loop(0, GW, cols_step, 0, unroll=4)

        def out_step(n, carry):
            rows = pl.ds(pl.multiple_of(n * ACH, ACH), ACH)
            vhat, rstd, _ = _layernorm(va_buf[rows, :].astype(f32), g_ref[...], b_ref[...])
            ds = dsv_buf[rows, :].astype(f32)
            dvn = jnp.concatenate([_dot(wst_ref[0], ds[:, 0:ACH]), _dot(wst_ref[1], ds[:, ACH:half]), ds_col[0, rows, :], ds_col[1, rows, :]], axis=1)
            dg_ref[...] += jnp.sum(dvn * vhat, axis=0, keepdims=True)
            db_ref[...] += jnp.sum(dvn, axis=0, keepdims=True)
            dvh = dvn * g_ref[...]
            dva = rstd * (dvh - jnp.mean(dvh, axis=-1, keepdims=True) - vhat * jnp.mean(dvh * vhat, axis=-1, keepdims=True))
            dva_ref[rows, :] = dva.astype(bf16)
            return carry

        lax.fori_loop(0, nch, out_step, 0, unroll=4)

    vm = pl.BlockSpec(memory_space=pltpu.VMEM)
    hbm = pl.BlockSpec(memory_space=pl.ANY)
    return pl.pallas_call(
        body, name="mix_bwd",
        out_shape=(jax.ShapeDtypeStruct((l, AW), bf16), jax.ShapeDtypeStruct((4, ACH, ACH), f32), jax.ShapeDtypeStruct((4, ACH, ACH), f32),
                   jax.ShapeDtypeStruct((1, AW), f32), jax.ShapeDtypeStruct((1, AW), f32)),
        in_specs=[hbm, hbm, vm, vm, vm], out_specs=(vm, vm, vm, vm, vm),
        scratch_shapes=[pltpu.VMEM((l, AW), bf16), pltpu.VMEM((l, AW), bf16), pltpu.VMEM((2, l, ACH), f32), pltpu.VMEM((2, l, ACH), f32),
                        pltpu.SemaphoreType.DMA((2,))],
        compiler_params=_cp(),
    )(p, dsv, ln_g, ln_b, ws_t)


def _mid(x, tgt, p, o_f, o_b, sv, gate, gf, gb_norm, w_pa, w_pb, w_out, tl):
    l = x.shape[0]

    def body(x_ref, t_ref, zb_ref, ua_ref, za_ref, g1_ref, g2_ref, of_ref, ob_ref, sv_ref, gate_ref, gf_ref, gbn_ref,
             wpa_ref, wpb_ref, wout_ref,
             dx1_ref, dzbua_ref, dzag_ref, dsv_ref, do_ref, dwout_bf, dwpa_bf, dwpb_bf, dgf_ref, dgate_ref, dgbn_ref, loss_ref,
             dwout_ref, dwpa_ref, dwpb_ref):
        @pl.when(pl.program_id(0) == 0)
        def _():
            for r in (dwout_ref, dwpa_ref, dwpb_ref, dgf_ref, dgate_ref, dgbn_ref, loss_ref):
                r[...] = jnp.zeros_like(r)

        o = of_ref[...] + ob_ref[...]
        rr = jnp.concatenate(
            [jnp.broadcast_to(lax.rsqrt(jnp.mean(o[:, h * HV:(h + 1) * HV] ** 2, axis=-1, keepdims=True) + EPS), (tl, HV))
             for h in range(NH)], axis=1)
        ohat = o * rr
        on = ohat * gbn_ref[...]
        szb, dszb = _silu_and_grad(zb_ref[...].astype(f32))
        tb = on * szb
        u = ua_ref[...].astype(f32)
        svv = sv_ref[...].astype(f32)
        sza, dsza = _silu_and_grad(za_ref[...].astype(f32))
        ta = u * svv * sza
        ya = _dot(ta, wpa_ref[...])
        yb = _dot(tb, wpb_ref[...])
        g1 = _sigmoid(g1_ref[...].astype(f32))
        g2 = _sigmoid(g2_ref[...].astype(f32))
        m = g1 * ya + g2 * yb
        y2 = _dot(m, wout_ref[...])
        x1 = x_ref[...] + gate_ref[...] * y2
        r1 = lax.rsqrt(jnp.mean(x1 * x1, axis=-1, keepdims=True) + EPS)
        x1n = x1 * r1
        err = x1n * gf_ref[...] - t_ref[...]
        loss_ref[...] += jnp.sum(jnp.sum(err * err, axis=-1, keepdims=True), axis=0, keepdims=True) * (0.5 / D)
        dout = err * (1.0 / D)
        dgf_ref[...] += jnp.sum(dout * x1n, axis=0, keepdims=True)
        dx1n = dout * gf_ref[...]
        dx1 = r1 * (dx1n - x1n * jnp.mean(dx1n * x1n, axis=-1, keepdims=True))
        dx1_ref[...] = dx1
        dgate_ref[...] += jnp.sum(dx1 * y2, axis=0, keepdims=True)
        dy2 = dx1 * gate_ref[...]
        dwout_ref[...] += _tn(m, dy2)
        dm = _nt(dy2, wout_ref[...])
        dya = dm * g1
        dyb = dm * g2
        dzag_ref[:, AW:AW + D] = (dm * ya * g1 * (1.0 - g1)).astype(bf16)
        dzag_ref[:, AW + D:] = (dm * yb * g2 * (1.0 - g2)).astype(bf16)
        dwpa_ref[...] += _tn(ta, dya)
        dta = _nt(dya, wpa_ref[...])
        dzbua_ref[:, AW:] = (dta * svv * sza).astype(bf16)
        dsv_ref[...] = (dta * u * sza).astype(bf16)
        dzag_ref[:, 0:AW] = (dta * u * svv * dsza).astype(bf16)
        dwpb_ref[...] += _tn(tb, dyb)
        dtb = _nt(dyb, wpb_ref[...])
        don = dtb * szb
        dzbua_ref[:, 0:AW] = (dtb * on * dszb).astype(bf16)
        dgbn_ref[...] += jnp.sum(don * ohat, axis=0, keepdims=True)
        doh = don * gbn_ref[...]
        prod = doh * ohat
        mh = jnp.concatenate(
            [jnp.broadcast_to(jnp.mean(prod[:, h * HV:(h + 1) * HV], axis=-1, keepdims=True), (tl, HV)) for h in range(NH)], axis=1)
        do_ref[...] = (rr * (doh - ohat * mh)).astype(bf16)

        @pl.when(pl.program_id(0) == l // tl - 1)
        def _():
            for acc, out in ((dwout_ref, dwout_bf), (dwpa_ref, dwpa_bf), (dwpb_ref, dwpb_bf)):
                out[...] = acc[...].astype(bf16)

    row = lambda w, j: pl.BlockSpec((tl, w), lambda i, j=j: (i, j))
    full = lambda *s: pl.BlockSpec(s, lambda i: (0,) * len(s))
    return pl.pallas_call(
        body, name="mid", grid=(l // tl,),
        out_shape=(jax.ShapeDtypeStruct((l, D), f32), jax.ShapeDtypeStruct((l, 2 * AW), bf16), jax.ShapeDtypeStruct((l, AW + 2 * D), bf16),
                   jax.ShapeDtypeStruct((l, AW), bf16), jax.ShapeDtypeStruct((l, VW), bf16),
                   jax.ShapeDtypeStruct((D, D), bf16), jax.ShapeDtypeStruct((AW, D), bf16), jax.ShapeDtypeStruct((VW, D), bf16),
                   jax.ShapeDtypeStruct((1, D), f32), jax.ShapeDtypeStruct((1, D), f32), jax.ShapeDtypeStruct((1, VW), f32),
                   jax.ShapeDtypeStruct((1, 1), f32)),
        scratch_shapes=[pltpu.VMEM((D, D), f32), pltpu.VMEM((AW, D), f32), pltpu.VMEM((VW, D), f32)],
        in_specs=[row(D, 0), row(D, 0), row(AW, PZB // AW), row(AW, PUA // AW), row(AW, PZA // AW), row(D, PG1 // D), row(D, PG2 // D),
                  row(VW, 0), row(VW, 0), row(AW, 0), full(1, D), full(1, D), full(1, VW), full(AW, D), full(VW, D), full(D, D)],
        out_specs=(row(D, 0), row(2 * AW, 0), row(AW + 2 * D, 0), row(AW, 0), row(VW, 0),
                   full(D, D), full(AW, D), full(VW, D), full(1, D), full(1, D), full(1, VW), full(1, 1)),
        compiler_params=_cp(("arbitrary",)),
    )(x, tgt, p, p, p, p, p, o_f, o_b, sv, gate, gf, gb_norm, w_pa, w_pb, w_out)


def _in_bwd(x, dx1, dqkv, dzbua, dva, dzag, dlr, w_pad, g, scale, tl, comm):
    l = x.shape[0]
    c_in, c_out, c_sems = comm.specs()

    def body(*refs):
        cin = refs[14:14 + len(c_in)]
        outs = refs[14 + len(c_in):]
        comm.run(cin, outs[4:4 + comm.n], outs[4 + comm.n:], l // tl, lambda: compute(*refs[:14], *outs[:4]))

    def compute(x_ref, dx1_ref, a_ref, b_ref, c_ref, e_ref, lr_ref, wa_ref, wb_ref, wc_ref, we_ref, wl_ref, g_ref, sc_ref,
                gx_ref, dsh_ref, dsc_ref, dg_ref):
        @pl.when(pl.program_id(0) == 0)
        def _():
            for r in (dsh_ref, dsc_ref, dg_ref):
                r[...] = jnp.zeros_like(r)

        dh = (_nt(a_ref[...], wa_ref[...]) + _nt(b_ref[...], wb_ref[...]) + _nt(c_ref[...], wc_ref[...]) + _nt(e_ref[...], we_ref[...])
              + _nt(lr_ref[...], wl_ref[...]))
        xv = x_ref[...]
        r = lax.rsqrt(jnp.mean(xv * xv, axis=-1, keepdims=True) + EPS)
        xn = xv * r
        gxn = jnp.sum(dh * xn, axis=0, keepdims=True)
        dsh_ref[...] += jnp.sum(dh, axis=0, keepdims=True)
        dsc_ref[...] += gxn * g_ref[...]
        dg_ref[...] += gxn * (1.0 + sc_ref[...])
        dxn = dh * (g_ref[...] * (1.0 + sc_ref[...]))
        gx_ref[...] = dx1_ref[...] + r * (dxn - xn * jnp.mean(dxn * xn, axis=-1, keepdims=True))

    row = lambda w: pl.BlockSpec((tl, w), lambda i: (i, 0))
    wcol = lambda w, j: pl.BlockSpec((D, w), lambda i, j=j: (0, j))
    vec = pl.BlockSpec((1, D), lambda i: (0, 0))
    return pl.pallas_call(
        body, name="in_bwd", grid=(l // tl,),
        out_shape=(jax.ShapeDtypeStruct((l, D), f32),) + (jax.ShapeDtypeStruct((1, D), f32),) * 3 + tuple(comm.outs),
        in_specs=[row(D), row(D), row(2 * KW + VW), row(2 * AW), row(AW), row(AW + 2 * D), row(LRW),
                  wcol(2 * KW + VW, 0), wcol(2 * AW, PZB // (2 * AW)), wcol(AW, PVA // AW), wcol(AW + 2 * D, PZA // (AW + 2 * D)),
                  wcol(LRW, PLR // LRW), vec, vec] + c_in,
        out_specs=(row(D), vec, vec, vec) + tuple(c_out), scratch_shapes=c_sems,
        compiler_params=_cp(("arbitrary",)),
    )(x, dx1, dqkv, dzbua, dva, dzag, dlr, w_pad, w_pad, w_pad, w_pad, w_pad, g, scale, *comm.ins)


def _tn_matmul(a, bs, tl, name, comm=None):
    l, m = a.shape
    k = len(bs)
    comm = comm or _Comm([], [], None)
    c_in, c_out, c_sems = comm.specs()

    def body(a_ref, *refs):
        b_refs, cin = refs[:k], refs[k:k + len(c_in)]
        o_refs = refs[k + len(c_in):2 * k + len(c_in)]
        cout, sems = refs[2 * k + len(c_in):2 * k + len(c_in) + comm.n], refs[2 * k + len(c_in) + comm.n:]

        def compute():
            @pl.when(pl.program_id(0) == 0)
            def _():
                for o_ref in o_refs:
                    o_ref[...] = jnp.zeros_like(o_ref)

            av = a_ref[...]
            for b_ref, o_ref in zip(b_refs, o_refs):
                o_ref[...] += _tn(av, b_ref[...])

        comm.run(cin, cout, sems, l // tl, compute)

    return pl.pallas_call(
        body, name=name, grid=(l // tl,), out_shape=tuple(jax.ShapeDtypeStruct((m, b.shape[1]), f32) for b in bs) + tuple(comm.outs),
        in_specs=[pl.BlockSpec((tl, m), lambda i: (i, 0))] + [pl.BlockSpec((tl, b.shape[1]), lambda i: (i, 0)) for b in bs] + c_in,
        out_specs=tuple(pl.BlockSpec((m, b.shape[1]), lambda i: (0, 0)) for b in bs) + tuple(c_out),
        scratch_shapes=c_sems, compiler_params=_cp(("arbitrary",)),
    )(a, *bs, *comm.ins)


def _pad_gate(w2, gb):
    z = jnp.zeros((RANK, KW), f32)
    tail = jnp.zeros((LRW - 2 * RANK, KW), f32)
    w2f = jnp.concatenate([w2[0], z, tail], axis=0)
    w2b = jnp.concatenate([z, w2[1], tail], axis=0)
    return w2f, w2b, gb[0:1], gb[1:2]


EARLY_A_ROWS = 384


class _NoExchange:
    def __init__(self, w_pa, w_pb, w_out):
        self.weights = (w_pa, w_pb, w_out)

    def gather_proj(self):
        return _Comm([], [], None)

    def proj_weights(self, got):
        return self.weights

    def first(self, dw_out, dw_pa, dw_pb):
        return _Comm([], [], None)

    def early_a(self, small, blocks):
        return _Comm([], [], None)

    def early_b(self, blocks):
        return _Comm([], [], None)

    def late(self, blocks, dgt):
        return _Comm([], [], None)


class _Exchanges:
    def __init__(self, pa, pb, wo):
        self.shards = (pa, pb, wo)

    def gather_proj(self):
        def plan(i, o):
            srcs = [lambda j, r=r: r for r in i]
            dsts = [lambda j: o[0].at[:, _lanes(j)], lambda j: o[1].at[:, _lanes(j)], lambda j: o[2].at[j]]
            return srcs, dsts, None
        sds = jax.ShapeDtypeStruct
        return _Comm(self.shards, [sds((AW, D), bf16), sds((VW, D), bf16), sds((NDEV, 128, D), bf16)], plan)

    def proj_weights(self, got):
        return got[0], got[1], got[2].reshape(D, D)

    def first(self, dw_out, dw_pa, dw_pb):
        def plan(i, o):
            srcs = [lambda j: i[0].at[j], lambda j: i[1].at[:, _lanes(j)], lambda j: i[2].at[:, _lanes(j)]]
            dsts = [lambda j, r=r: r.at[j] for r in o]
            return srcs, dsts, None
        sds = jax.ShapeDtypeStruct
        return _Comm([dw_out.reshape(NDEV, 128, D), dw_pa, dw_pb],
                     [sds((NDEV, 128, D), bf16), sds((NDEV, AW, 128), bf16), sds((NDEV, VW, 128), bf16)], plan)

    def early_a(self, small, blocks):
        def plan(i, o):
            srcs = [lambda j: i[0], lambda j: i[1].at[j]]
            dsts = [lambda j, r=r: r.at[j] for r in o]
            return srcs, dsts, [None, lambda j: j >= LATE_DESTS - 1]
        sds = jax.ShapeDtypeStruct
        return _Comm([small, blocks], [sds((NDEV,) + small.shape, f32), sds(blocks.shape, bf16)], plan)

    def early_b(self, blocks):
        def plan(i, o):
            return [lambda j: i[0].at[j]], [lambda j: o[0].at[j]], [lambda j: j >= LATE_DESTS - 1]
        return _Comm([blocks], [jax.ShapeDtypeStruct(blocks.shape, bf16)], plan)

    def late(self, blocks, dgt):
        def plan(i, o):
            srcs = [lambda j: i[0].at[jnp.minimum(j, LATE_DESTS - 1)], lambda j: i[1].at[j]]
            dsts = [lambda j, r=r: r.at[j] for r in o]
            return srcs, dsts, [lambda j: j < LATE_DESTS, None]
        sds = jax.ShapeDtypeStruct
        return _Comm([blocks, dgt], [sds((NDEV, D, SHARD), bf16), sds(dgt.shape, f32)], plan)


def _local_step(x, ctx, tgt, mod, modc, norm_g, w_pad, ln_g, ln_b, ws, bs, w2, gb, gb_norm, gf, xch):
    shift, scale, gate = mod[:, 0:D], mod[:, D:2 * D], mod[:, 2 * D:]
    shift_c, scale_c = modc[:, 0:D], modc[:, D:]
    w2f, w2b, gbf, gbb = _pad_gate(w2, gb)

    p, h, *got_proj = _in_proj(x, norm_g, scale, shift, w_pad, 512, xch.gather_proj())
    w_pa, w_pb, w_out = xch.proj_weights(got_proj)
    sc_f, sc_b = _ctx_fwd(ctx, norm_g, scale_c, shift_c, w_pad, w2f, w2b, gbf, gbb)
    o_f, st_f = _gla_fwd(p, w2f, gbf, sc_f, False, 512, "gla_fwd_f")
    o_b, st_b = _gla_fwd(p, w2b, gbb, sc_b, True, 512, "gla_fwd_b")
    sv = _mix_fwd(p, ln_g, ln_b, ws.astype(bf16), bs.T)
    (dx1, dzbua, dzag, dsv, do, dw_out, dw_pa, dw_pb, dgf, dgate, dgbn, loss) = _mid(
        x, tgt, p, o_f, o_b, sv, gate, gf, gb_norm, w_pa, w_pb, w_out, 256)
    dva, dws, dbs_acc, dln_g, dln_b = _mix_bwd(p, dsv, ln_g, ln_b, jnp.swapaxes(ws, 1, 2).astype(bf16))
    dw_zbua, dw_zag, *got_first = _tn_matmul(h, [dzbua, dzag], 512, "dw_zbua_zag", xch.first(dw_out, dw_pa, dw_pb))
    (dw_va,) = _tn_matmul(h, [dva], 512, "dw_va")
    blocks_a = _pack_dw_early(dw_zbua, dw_va, dw_zag, 0, EARLY_A_ROWS, "pack_dw_early_a")
    blocks_b = _pack_dw_early(dw_zbua, dw_va, dw_zag, EARLY_A_ROWS, D - EARLY_A_ROWS, "pack_dw_early_b")
    small = _rows128(dln_g, dln_b, dws, jnp.sum(dbs_acc, axis=-1), dgbn, dgf, jnp.broadcast_to(loss, (1, 128)))

    dqkv_f, dlr_f, dw2f, dgbf, dsc_f, *got_a = _gla_bwd(p, do, st_f, w2f, gbf, None, False, 512, "gla_bwd_f",
                                                        xch.early_a(small, blocks_a))
    dqkv, dlr, dw2b, dgbb, dsc_b, *got_b = _gla_bwd(p, do, st_b, w2b, gbb, (dqkv_f, dlr_f), True, 512, "gla_bwd_b",
                                                    xch.early_b(blocks_b))
    dwk_c, dwv_c, dwl_c, dmodc, dg_c, dw2c, dgbc = _ctx_bwd(ctx, norm_g, scale_c, shift_c, w_pad, w2f, w2b, gbf, gbb, dsc_f, dsc_b)
    dw_qkv, dw_lr = _tn_matmul(h, [dqkv, dlr], 512, "dw_qkv_lr")
    blocks_late = _pack_dw_late(dw_qkv, dw_lr, dwk_c, dwv_c, dwl_c)
    dw2 = jnp.stack([dw2f[0:RANK] + dw2c[0, 0:RANK], dw2b[RANK:2 * RANK] + dw2c[1, RANK:2 * RANK]])
    dgb = jnp.concatenate([dgbf + dgbc[0], dgbb + dgbc[1]], axis=0)
    dgt = jnp.concatenate([jnp.transpose(dw2.reshape(2, RANK, NDEV, 32), (2, 0, 1, 3)).reshape(NDEV, 2 * RANK * 32),
                           jnp.transpose(dgb.reshape(2, NDEV, 32), (1, 0, 2)).reshape(NDEV, 64),
                           jnp.zeros((NDEV, 64), f32)], axis=1).reshape(NDEV, 9, 128)
    gx, dshift, dscale, dg, *got_late = _in_bwd(x, dx1, dqkv, dzbua, dva, dzag, dlr, w_pad, norm_g, scale, 512,
                                                xch.late(blocks_late, dgt))
    return dict(loss=loss, gx=gx, dmod=jnp.concatenate([dshift, dscale, dgate], axis=1), dmodc=dmodc, dnorm_g=dg + dg_c,
                small=small, blocks_a=blocks_a, blocks_b=blocks_b, blocks_late=blocks_late, dw2=dw2, dgb=dgb,
                dw_pa=dw_pa, dw_pb=dw_pb, dw_out=dw_out, got_first=got_first, got_a=got_a, got_b=got_b, got_late=got_late)


def _rows128(*vs):
    out = []
    for t in vs:
        t = t.reshape(-1)
        pad = (-t.shape[0]) % 128
        out.append(jnp.pad(t, (0, pad)) if pad else t)
    return jnp.concatenate(out).reshape(-1, 128)


def kernel(x, c, ctx, c_ctx, w_mod, b_mod, norm_g, w_in, a_ln_g, a_ln_b, a_ws, a_bs, b_gate_w2, b_gate_b, b_norm_g, w_proj_a, w_proj_b, w_out, final_norm_g, loss_target, m_c_ctx, m_w_mod, m_b_mod, m_norm_g, m_w_in, m_a_ln_g, m_a_ln_b, m_a_ws, m_a_bs, m_b_gate_w2, m_b_gate_b, m_b_norm_g, m_w_proj_a, m_w_proj_b, m_w_out, m_final_norm_g, v_c_ctx, v_w_mod, v_b_mod, v_norm_g, v_w_in, v_a_ln_g, v_a_ln_b, v_a_ws, v_a_bs, v_b_gate_w2, v_b_gate_b, v_b_norm_g, v_w_proj_a, v_w_proj_b, v_w_out, v_final_norm_g):
    me = _me()
    ncol = w_mod.shape[2]

    gate_mine = _rows128(jnp.concatenate([b_gate_w2.reshape(-1), b_gate_b.reshape(-1)]))
    early, wg, gates = _gather_first(_rows128(c), w_in[0].astype(bf16), gate_mine)
    cs = jnp.concatenate([early.reshape(NDEV, D), c_ctx.reshape(1, D), jnp.zeros((7, D), f32)], axis=0)
    w_pad = _repack_w(wg)
    gflat = gates.reshape(NDEV, 9 * 128)
    w2 = jnp.transpose(gflat[:, 0:2 * RANK * 32].reshape(NDEV, 2, RANK, 32), (1, 2, 0, 3)).reshape(2, RANK, KW)
    gb = jnp.transpose(gflat[:, 2 * RANK * 32:2 * RANK * 32 + 64].reshape(NDEV, 2, 32), (1, 0, 2)).reshape(2, KW)

    bm_mine = lax.dynamic_slice(b_mod, (0, me * ncol), (1, ncol))
    mods = _all_gather(_mod_fwd(cs, w_mod[0], bm_mine), "gather_mod")
    mods = jnp.transpose(mods, (1, 0, 2)).reshape(16, 3 * D)
    mod = lax.dynamic_slice(mods, (me, 0), (1, 3 * D))
    modc = mods[8:9, 0:2 * D]

    xch = _Exchanges(w_proj_a[0].astype(bf16), w_proj_b[0].astype(bf16), w_out[0].astype(bf16))
    r = _local_step(x[0], ctx[0], loss_target[0], mod, modc, norm_g, w_pad, a_ln_g, a_ln_b, a_ws[0], a_bs[0], w2, gb,
                    b_norm_g, final_norm_g.reshape(1, D), xch)
    p_out, p_pa, p_pb = r["got_first"]
    smalls_e, p_in_a = r["got_a"]
    (p_in_b,) = r["got_b"]
    p_in_late, p_gt = r["got_late"]

    n_e = (AW + AW + 4 * ACH * ACH + AW + VW + D) // 128
    row = lambda t: t.reshape(1, D)
    rep_e = _adam_params(smalls_e, [a_ln_g, a_ln_b, a_ws, a_bs, b_norm_g, row(final_norm_g)],
                         [m_a_ln_g, m_a_ln_b, m_a_ws, m_a_bs, m_b_norm_g, row(m_final_norm_g)],
                         [v_a_ln_g, v_a_ln_b, v_a_ws, v_a_bs, v_b_norm_g, row(v_final_norm_g)], "adam_rep_early")
    rep_e = [t[0:5] + (t[5].reshape(D),) for t in rep_e]
    losses = smalls_e[:, n_e, 0]
    loss = losses[0]
    for i in range(1, NDEV):
        loss = loss + losses[i]

    dbm = r["dmod"] + jnp.concatenate([r["dmodc"], jnp.zeros((1, D), f32)], axis=1)
    smalls_l = _all_gather(_rows128(r["dnorm_g"], dbm, r["dmod"], r["dmodc"]), "gather_small")
    n_l = (D + 3 * D) // 128
    rep_l = _adam_params(smalls_l, [norm_g, b_mod], [m_norm_g, m_b_mod], [v_norm_g, v_b_mod], "adam_rep_late")
    tail = smalls_l[:, n_l:].reshape(NDEV, -1)
    dmods = tail[:, 0:3 * D]
    dmodc_all = tail[:, 3 * D:5 * D]
    dmodc_tot = dmodc_all[0:1]
    for i in range(1, NDEV):
        dmodc_tot = dmodc_tot + dmodc_all[i:i + 1]

    dm_rows = jnp.concatenate([dmods, jnp.concatenate([dmodc_tot, jnp.zeros((1, D), f32)], axis=1), jnp.zeros((7, 3 * D), f32)], axis=0)
    dm_mine = lax.dynamic_slice(dm_rows, (0, me * ncol), (16, ncol))
    g_wmod, gc_part = _mod_bwd(cs, dm_mine, w_mod[0])
    wm = _adam(g_wmod[None], w_mod, m_w_mod, v_w_mod, 256, "adam_wmod")
    gcs = _all_gather(gc_part.reshape(8, 128), "gather_cctx")
    cc = _adam(gcs, c_ctx.reshape(8, 128), m_c_ctx.reshape(8, 128), v_c_ctx.reshape(8, 128), 8, "adam_cctx")

    a_in = _adam_w_in(p_in_a, p_in_b, p_in_late, w_in, m_w_in, v_w_in)
    a_pa = _adam(p_pa, w_proj_a, m_w_proj_a, v_w_proj_a, AW, "adam_w_pa")
    a_pb = _adam(p_pb, w_proj_b, m_w_proj_b, v_w_proj_b, VW, "adam_w_pb")
    a_out = _adam(p_out, w_out, m_w_out, v_w_out, 128, "adam_w_out")
    gate_m = _rows128(jnp.concatenate([m_b_gate_w2.reshape(-1), m_b_gate_b.reshape(-1)]))
    gate_v = _rows128(jnp.concatenate([v_b_gate_w2.reshape(-1), v_b_gate_b.reshape(-1)]))
    a_gt = [t.reshape(-1) for t in _adam(p_gt, gate_mine, gate_m, gate_v, 9, "adam_gate")]
    nw2 = 2 * RANK * 32
    sh = [(a_in[k], a_pa[k], a_pb[k], a_out[k], a_gt[k][0:nw2].reshape(1, 2, RANK, 32),
           a_gt[k][nw2:nw2 + 64].reshape(1, 2, 32)) for k in range(4)]

    outs = [loss, r["gx"][None]]
    for k in range(4):
        lg, lb, aws, abs_, bng, fng = rep_e[k]
        n_g, bmod = rep_l[k]
        s_in, s_pa, s_pb, s_out, s_w2, s_gb = sh[k]
        outs += [cc[k].reshape(D), wm[k], bmod, n_g, s_in, lg, lb, aws, abs_, s_w2, s_gb, bng, s_pa, s_pb, s_out, fng]
    return tuple(outs)
```

```python
import functools

import jax
import jax.numpy as jnp
from jax import lax
from jax.experimental import pallas as pl
from jax.experimental.pallas import tpu as pltpu

f32, bf16 = jnp.float32, jnp.bfloat16

D = 1024
CTX = 256
EPS = 1e-6
AW = 512
ACH = 128
GW = 64
KW = 256
VW = 512
NH = 4
HK = 64
HV = 128
RANK = 16
TAU = 16.0
CH = 64
QSCALE = HK ** -0.5
INW = 5152
NDEV = 8

PQ, PK, PV, PZB, PUA, PVA, PZA, PG1, PG2, PLR, PW = 0, 256, 512, 1024, 1536, 2048, 2560, 3072, 4096, 5120, 5248
LRW = 128

ADAM_LR, ADAM_B1, ADAM_B2, ADAM_EPS, ADAM_WD, ADAM_STEP = 0.001, 0.9, 0.999, 1e-08, 0.01, 10

VMEM_LIMIT = 56 * 1024 * 1024
MESH = pl.DeviceIdType.MESH


def _cp(sem=None):
    return pltpu.CompilerParams(dimension_semantics=sem, vmem_limit_bytes=VMEM_LIMIT)


def _dot(a, b):
    return jnp.dot(a.astype(bf16), b.astype(bf16), preferred_element_type=f32)


def _nt(a, b):
    return lax.dot_general(a.astype(bf16), b.astype(bf16), (((1,), (1,)), ((), ())), preferred_element_type=f32)


def _tn(a, b):
    return lax.dot_general(a.astype(bf16), b.astype(bf16), (((0,), (0,)), ((), ())), preferred_element_type=f32)


def _dot_hi(a, b):
    return jnp.dot(a, b, preferred_element_type=f32, precision=lax.Precision.HIGHEST)


def _sigmoid(x):
    return 1.0 / (1.0 + jnp.exp(-x))


def _log_sigmoid(x):
    return jnp.minimum(x, 0.0) - jnp.log(1.0 + jnp.exp(-jnp.abs(x)))


def _silu_and_grad(z):
    s = _sigmoid(z)
    return z * s, s * (1.0 + z * (1.0 - s))


def _me():
    return 4 * lax.axis_index("x") + 2 * lax.axis_index("y") + lax.axis_index("c")


def _peer(k):
    x, y, c = lax.axis_index("x"), lax.axis_index("y"), lax.axis_index("c")
    px = 1 - x if k & 4 else x
    py = 1 - y if k & 2 else y
    pc = 1 - c if k & 1 else c
    return (px, py, pc), 4 * px + 2 * py + pc


def _all_gather(v, name):
    r, c = v.shape

    def body(v_ref, out_ref, send_sems, recv_sems, local_sem):
        me = _me()
        mine = pltpu.make_async_copy(v_ref, out_ref.at[me], local_sem)
        mine.start()
        sends, recvs = [], []
        for k in range(1, NDEV):
            dev, idx = _peer(k)
            sends.append(pltpu.make_async_remote_copy(
                src_ref=v_ref, dst_ref=out_ref.at[me], send_sem=send_sems.at[k - 1], recv_sem=recv_sems.at[k - 1],
                device_id=dev, device_id_type=MESH))
            recvs.append(pltpu.make_async_remote_copy(
                src_ref=v_ref, dst_ref=out_ref.at[idx], send_sem=send_sems.at[k - 1], recv_sem=recv_sems.at[k - 1],
                device_id=dev, device_id_type=MESH))
        for cp in sends:
            cp.start()
        for cp in recvs:
            cp.wait_recv()
        for cp in sends:
            cp.wait_send()
        mine.wait()

    return pl.pallas_call(
        body, name=name, out_shape=jax.ShapeDtypeStruct((NDEV, r, c), v.dtype),
        in_specs=[pl.BlockSpec(memory_space=pl.ANY)], out_specs=pl.BlockSpec(memory_space=pl.ANY),
        scratch_shapes=[pltpu.SemaphoreType.DMA((NDEV - 1,)), pltpu.SemaphoreType.DMA((NDEV - 1,)), pltpu.SemaphoreType.DMA(())],
    )(v)


def _fanout(srcs, dsts, send_sems, recv_sems, local_sems, owners=None):
    me = _me()
    n = len(srcs)
    owns = lambda a, j: True if owners is None or owners[a] is None else owners[a](j)

    def guarded(cond, fn):
        if cond is True:
            fn()
        else:
            pl.when(cond)(fn)

    def copies(with_recvs):
        local = [pltpu.make_async_copy(srcs[a](me), dsts[a](me), local_sems.at[a]) for a in range(n)]
        sends, recvs = [], []
        for k in range(1, NDEV):
            dev, idx = _peer(k)
            for a in range(n):
                s = (k - 1) * n + a
                sends.append((owns(a, idx), pltpu.make_async_remote_copy(
                    src_ref=srcs[a](idx), dst_ref=dsts[a](me), send_sem=send_sems.at[s], recv_sem=recv_sems.at[s],
                    device_id=dev, device_id_type=MESH)))
                if with_recvs:
                    recvs.append((owns(a, me), pltpu.make_async_remote_copy(
                        src_ref=srcs[a](idx), dst_ref=dsts[a](idx), send_sem=send_sems.at[s], recv_sem=recv_sems.at[s],
                        device_id=dev, device_id_type=MESH)))
        return local, sends, recvs

    def start():
        local, sends, _ = copies(False)
        for a, cp in enumerate(local):
            guarded(owns(a, me), cp.start)
        for cond, cp in sends:
            guarded(cond, cp.start)

    def finish():
        local, sends, recvs = copies(True)
        for cond, cp in recvs:
            guarded(cond, cp.wait_recv)
        for cond, cp in sends:
            guarded(cond, cp.wait_send)
        for a, cp in enumerate(local):
            guarded(owns(a, me), cp.wait)

    return start, finish


class _Comm:
    def __init__(self, ins, outs, plan):
        self.ins, self.outs, self.plan = list(ins), list(outs), plan
        self.n = len(self.outs)

    def specs(self):
        hbm = pl.BlockSpec(memory_space=pl.ANY)
        return [hbm] * len(self.ins), [hbm] * self.n, _fanout_sems(self.n) if self.n else []

    def run(self, in_refs, out_refs, sems, nsteps, compute):
        if not self.n:
            compute()
            return

        def hooks():
            srcs, dsts, owners = self.plan(in_refs, out_refs)
            return _fanout(srcs, dsts, sems[0], sems[1], sems[2], owners)

        pl.when(pl.program_id(0) == 0)(lambda: hooks()[0]())
        compute()
        pl.when(pl.program_id(0) == nsteps - 1)(lambda: hooks()[1]())


LATE_MID_STEP = 2
LATE_ADD_ROWS = 256


class _LateComm:
    def __init__(self, blocks, dgt):
        sds = jax.ShapeDtypeStruct
        self.ins = [blocks, dgt]
        self.outs = [sds((D, SHARD), bf16), sds((D, SHARD), bf16), sds((4, D, SHARD), bf16), sds(dgt.shape, f32)]
        self.n = len(self.outs)

    def specs(self):
        hbm = pl.BlockSpec(memory_space=pl.ANY)
        scratch = [pltpu.VMEM((3, LATE_ADD_ROWS, SHARD), bf16), pltpu.SemaphoreType.DMA((2,)), pltpu.SemaphoreType.DMA((4,)),
                   pltpu.SemaphoreType.DMA(())] + _fanout_sems(1)
        return [hbm] * 2, [hbm] * self.n, scratch

    def run(self, in_refs, out_refs, scratch, nsteps, compute):
        late_ref, dgt_ref = in_refs
        sib_ref, pair_ref, parts_ref, gt_ref = out_refs
        vbuf, send_sems, recv_sems, local_sem, g_send, g_recv, g_local = scratch
        x, y, c = lax.axis_index("x"), lax.axis_index("y"), lax.axis_index("c")
        chip = 2 * x + y
        is_owner_chip = chip == 0
        step = pl.program_id(0)

        def to_sibling():
            return pltpu.make_async_remote_copy(src_ref=late_ref.at[1 - c], dst_ref=sib_ref, send_sem=send_sems.at[0],
                                                recv_sem=recv_sems.at[0], device_id=(x, y, 1 - c), device_id_type=MESH)

        def to_owner(k):
            return pltpu.make_async_remote_copy(src_ref=pair_ref, dst_ref=parts_ref.at[k], send_sem=send_sems.at[1],
                                                recv_sem=recv_sems.at[k], device_id=(0, 0, c), device_id_type=MESH)

        def own_copy():
            return pltpu.make_async_copy(pair_ref, parts_ref.at[0], local_sem)

        def gates():
            return _fanout([lambda j: dgt_ref.at[j]], [lambda j: gt_ref.at[j]], g_send, g_recv, g_local)

        @pl.when(step == 0)
        def _():
            to_sibling().start()
            gates()[0]()

        compute()

        @pl.when(step == LATE_MID_STEP)
        def _():
            to_sibling().wait_recv()
            for r0 in range(0, D, LATE_ADD_ROWS):
                rows = pl.ds(r0, LATE_ADD_ROWS)
                pltpu.sync_copy(late_ref.at[c, rows], vbuf.at[0])
                pltpu.sync_copy(sib_ref.at[rows], vbuf.at[1])
                vbuf[2] = (vbuf[0].astype(f32) + vbuf[1].astype(f32)).astype(bf16)
                pltpu.sync_copy(vbuf.at[2], pair_ref.at[rows])
            pl.when(is_owner_chip)(lambda: own_copy().start())
            pl.when(jnp.logical_not(is_owner_chip))(lambda: to_owner(chip).start())

        @pl.when(step == nsteps - 1)
        def _():
            @pl.when(is_owner_chip)
            def _():
                for k in range(1, 4):
                    to_owner(k).wait_recv()
                own_copy().wait()

            pl.when(jnp.logical_not(is_owner_chip))(lambda: to_owner(chip).wait_send())
            to_sibling().wait_send()
            gates()[1]()


def _fanout_sems(n):
    return [pltpu.SemaphoreType.DMA(((NDEV - 1) * n,)), pltpu.SemaphoreType.DMA(((NDEV - 1) * n,)), pltpu.SemaphoreType.DMA((n,))]


def _lanes(j):
    return pl.ds(pl.multiple_of(j * 128, 128), 128)


def _gather_first(c_rows, wi, gate):
    def body(c_ref, wi_ref, g_ref, oc, owi, og, send_sems, recv_sems, local_sems):
        x, y, c = lax.axis_index("x"), lax.axis_index("y"), lax.axis_index("c")
        sibling = (x, y, 1 - c)
        chips = [(1 - x, y), (x, 1 - y), (1 - x, 1 - y)]
        index = lambda px, py, pc: 4 * px + 2 * py + pc
        arrays = ((c_ref, oc), (wi_ref, owi), (g_ref, og))
        n = len(arrays)

        def copy(a, k, block, to, own=False):
            src, out = arrays[a]
            return pltpu.make_async_remote_copy(
                src_ref=src if own else out.at[index(*block)], dst_ref=out.at[index(*block)],
                send_sem=send_sems.at[k * n + a], recv_sem=recv_sems.at[k * n + a], device_id=to, device_id_type=MESH)

        mine = [pltpu.make_async_copy(src, out.at[index(x, y, c)], local_sems.at[a]) for a, (src, out) in enumerate(arrays)]
        first = [copy(a, 0, (x, y, c), sibling, own=True) for a in range(n)]
        first += [copy(a, 1 + j, (x, y, c), (*chip, c), own=True) for j, chip in enumerate(chips) for a in range(n)]
        for cp in mine + first:
            cp.start()
        passed = []
        for j, chip in enumerate(chips):
            for a in range(n):
                copy(a, 1 + j, (*chip, c), (x, y, c)).wait_recv()
            for a in range(n):
                cp = copy(a, 4 + j, (*chip, c), sibling)
                cp.start()
                passed.append(cp)
        for a in range(n):
            copy(a, 0, sibling, (x, y, c)).wait_recv()
        for j, chip in enumerate(chips):
            for a in range(n):
                copy(a, 4 + j, (*chip, 1 - c), (x, y, c)).wait_recv()
        for cp in first + passed:
            cp.wait_send()
        for cp in mine:
            cp.wait()

    hbm = pl.BlockSpec(memory_space=pl.ANY)
    return pl.pallas_call(
        body, name="gather_first",
        out_shape=(jax.ShapeDtypeStruct((NDEV,) + c_rows.shape, f32), jax.ShapeDtypeStruct((NDEV,) + wi.shape, bf16),
                   jax.ShapeDtypeStruct((NDEV,) + gate.shape, f32)),
        in_specs=[hbm] * 3, out_specs=(hbm,) * 3, scratch_shapes=_fanout_sems(3),
    )(c_rows, wi, gate)


SHARD = INW // NDEV
ROWS_RP = 128


def _overlap(lo, hi, a, b):
    s, e = max(lo, a), min(hi, b)
    return (s, e) if s < e else None


def _repack_w(wg):
    segs = ((0, 1024, PQ), (1024, 1024 + 2 * RANK, PLR), (1024 + 2 * RANK, INW, PZB))

    def body(g_ref, o_ref):
        for j in range(NDEV):
            lo, hi = j * SHARD, (j + 1) * SHARD
            for a, b, pad0 in segs:
                ov = _overlap(lo, hi, a, b)
                if ov:
                    s, e = ov
                    o_ref[:, pad0 + s - a:pad0 + e - a] = g_ref[j, :, s - lo:e - lo]
        o_ref[:, PLR + 2 * RANK:PW] = jnp.zeros((ROWS_RP, PW - PLR - 2 * RANK), bf16)

    return pl.pallas_call(
        body, name="repack_w", grid=(D // ROWS_RP,), out_shape=jax.ShapeDtypeStruct((D, PW), bf16),
        in_specs=[pl.BlockSpec((NDEV, ROWS_RP, SHARD), lambda i: (0, i, 0))],
        out_specs=pl.BlockSpec((ROWS_RP, PW), lambda i: (i, 0)), compiler_params=_cp(("arbitrary",)),
    )(wg)


LATE_END = 1024 + 2 * RANK
LATE_DESTS = 2


def _pack_blocks(o_ref, srcs, dests, dtype):
    for n, j in enumerate(dests):
        lo, hi = j * SHARD, (j + 1) * SHARD
        done = lo
        for a, b, src in srcs:
            ov = _overlap(lo, hi, a, b)
            if ov:
                s, e = ov
                if s > done:
                    o_ref[n, :, done - lo:s - lo] = jnp.zeros((ROWS_RP, s - done), dtype)
                o_ref[n, :, s - lo:e - lo] = src[:, s - a:e - a].astype(dtype)
                done = e
        if done < hi:
            o_ref[n, :, done - lo:hi - lo] = jnp.zeros((ROWS_RP, hi - done), dtype)


def _pack_dw_early(dw_zbua, dw_va, dw_zag, row0, nrows, name):
    def body(zbua_ref, va_ref, zag_ref, o_ref):
        _pack_blocks(o_ref, ((LATE_END, 2080, zbua_ref), (2080, 2592, va_ref), (2592, INW, zag_ref)), range(NDEV), bf16)

    row = lambda w: pl.BlockSpec((ROWS_RP, w), lambda i: (i + row0 // ROWS_RP, 0))
    return pl.pallas_call(
        body, name=name, grid=(nrows // ROWS_RP,), out_shape=jax.ShapeDtypeStruct((NDEV, nrows, SHARD), bf16),
        in_specs=[row(2 * AW), row(AW), row(AW + 2 * D)],
        out_specs=pl.BlockSpec((NDEV, ROWS_RP, SHARD), lambda i: (0, i, 0)), compiler_params=_cp(("arbitrary",)),
    )(dw_zbua, dw_va, dw_zag)


def _pack_dw_late(dw_qkv, dw_lr, dwk_c, dwv_c, dwl_c):
    def body(qkv_ref, lr_ref, kc_ref, vc_ref, lc_ref, o_ref):
        qkv = qkv_ref[...] + jnp.concatenate([jnp.zeros((ROWS_RP, KW), f32), kc_ref[...], vc_ref[...]], axis=1)
        lr = lr_ref[...] + lc_ref[...]
        _pack_blocks(o_ref, ((0, 1024, qkv), (1024, LATE_END, lr)), range(LATE_DESTS), bf16)

    row = lambda w: pl.BlockSpec((ROWS_RP, w), lambda i: (i, 0))
    return pl.pallas_call(
        body, name="pack_dw_late", grid=(D // ROWS_RP,), out_shape=jax.ShapeDtypeStruct((LATE_DESTS, D, SHARD), bf16),
        in_specs=[row(2 * KW + VW), row(LRW), row(KW), row(VW), row(LRW)],
        out_specs=pl.BlockSpec((LATE_DESTS, ROWS_RP, SHARD), lambda i: (0, i, 0)), compiler_params=_cp(("arbitrary",)),
    )(dw_qkv, dw_lr, dwk_c, dwv_c, dwl_c)


def _mod_fwd(cs, wm, bm):
    def body(cs_ref, wm_ref, bm_ref, o_ref):
        s, _ = _silu_and_grad(cs_ref[...])
        o_ref[...] = _dot_hi(s, wm_ref[...]) + bm_ref[...]

    return pl.pallas_call(body, name="mod_fwd", out_shape=jax.ShapeDtypeStruct((16, wm.shape[1]), f32),
                          compiler_params=_cp())(cs, wm, bm)


def _mod_bwd(cs, dm, wm):
    def body(cs_ref, dm_ref, wm_ref, gw_ref, gc_ref):
        s, ds = _silu_and_grad(cs_ref[...])
        gw_ref[...] = lax.dot_general(s, dm_ref[...], (((0,), (0,)), ((), ())), preferred_element_type=f32,
                                      precision=lax.Precision.HIGHEST)
        part = lax.dot_general(dm_ref[8:9, :], wm_ref[...], (((1,), (1,)), ((), ())), preferred_element_type=f32,
                               precision=lax.Precision.HIGHEST)
        gc_ref[...] = part * ds[8:9, :]

    return pl.pallas_call(body, name="mod_bwd",
                          out_shape=(jax.ShapeDtypeStruct(wm.shape, f32), jax.ShapeDtypeStruct((1, D), f32)),
                          compiler_params=_cp())(cs, dm, wm)


def _adam_update(g, w_ref, m_ref, v_ref, go_ref, d_ref, mo_ref, vo_ref):
    c1 = 1.0 / (1.0 - ADAM_B1 ** ADAM_STEP)
    c2 = 1.0 / (1.0 - ADAM_B2 ** ADAM_STEP)
    mn = ADAM_B1 * m_ref[...] + (1.0 - ADAM_B1) * g
    vn = ADAM_B2 * v_ref[...] + (1.0 - ADAM_B2) * (g * g)
    go_ref[...] = g
    mo_ref[...] = mn
    vo_ref[...] = vn
    d_ref[...] = -ADAM_LR * ((mn * c1) / (jnp.sqrt(vn * c2) + ADAM_EPS) + ADAM_WD * w_ref[...])


def _adam_w_in(parts_a, parts_b, parts_late, w, m, v):
    na = EARLY_A_ROWS // ROWS_RP

    def body(a_ref, b_ref, l_ref, w_ref, m_ref, v_ref, go_ref, d_ref, mo_ref, vo_ref):
        me = _me()
        first = pl.program_id(0) < na
        early = jnp.where(first, a_ref[0], b_ref[0]).astype(f32)
        for i in range(1, NDEV):
            early = early + jnp.where(first, a_ref[i], b_ref[i]).astype(f32)
        late = l_ref[0].astype(f32)
        for i in range(1, 4):
            late = late + l_ref[i].astype(f32)
        g = jnp.where(me >= LATE_DESTS - 1, early, 0.0) + jnp.where(me < LATE_DESTS, late, 0.0)
        _adam_update(g, w_ref, m_ref, v_ref, go_ref, d_ref, mo_ref, vo_ref)

    blk = pl.BlockSpec((None, ROWS_RP, SHARD), lambda i: (0, i, 0))
    return pl.pallas_call(
        body, name="adam_w_in", grid=(D // ROWS_RP,), out_shape=tuple(jax.ShapeDtypeStruct((1, D, SHARD), f32) for _ in range(4)),
        in_specs=[pl.BlockSpec((NDEV, ROWS_RP, SHARD), lambda i: (0, jnp.minimum(i, na - 1), 0)),
                  pl.BlockSpec((NDEV, ROWS_RP, SHARD), lambda i: (0, jnp.maximum(i - na, 0), 0)),
                  pl.BlockSpec((4, ROWS_RP, SHARD), lambda i: (0, i, 0)), blk, blk, blk],
        out_specs=(blk, blk, blk, blk), compiler_params=_cp(("arbitrary",)),
    )(parts_a, parts_b, parts_late, w, m, v)


def _adam_params(parts, ws, ms, vs, name):
    p = parts.shape[0]
    k = len(ws)
    nrows = [w.size // 128 for w in ws]
    starts = [sum(nrows[:i]) for i in range(k)]

    def shaped(g, shape):
        if len(shape) == 2:
            return jnp.concatenate([g[r:r + 1] for r in range(g.shape[0])], axis=1)
        return g.reshape(shape)

    def body(*refs):
        g_ref = refs[0]
        w_refs, m_refs, v_refs = refs[1:1 + k], refs[1 + k:1 + 2 * k], refs[1 + 2 * k:1 + 3 * k]
        outs = refs[1 + 3 * k:]
        for i in range(k):
            rows = slice(starts[i], starts[i] + nrows[i])
            g = g_ref[0, rows, :]
            for j in range(1, p):
                g = g + g_ref[j, rows, :]
            _adam_update(shaped(g, ws[i].shape), w_refs[i], m_refs[i], v_refs[i], outs[i], outs[k + i], outs[2 * k + i], outs[3 * k + i])

    vm = pl.BlockSpec(memory_space=pltpu.VMEM)
    res = pl.pallas_call(
        body, name=name, out_shape=tuple(jax.ShapeDtypeStruct(w.shape, f32) for _ in range(4) for w in ws),
        in_specs=[vm] * (1 + 3 * k), out_specs=(vm,) * (4 * k), compiler_params=_cp(),
    )(parts, *ws, *ms, *vs)
    return [res[i * k:(i + 1) * k] for i in range(4)]


def _adam(parts, w, m, v, rows, name):
    p, r, c = parts.shape
    lead = w.ndim - 2

    def body(g_ref, w_ref, m_ref, v_ref, go_ref, d_ref, mo_ref, vo_ref):
        g = g_ref[0].astype(f32)
        for i in range(1, p):
            g = g + g_ref[i].astype(f32)
        _adam_update(g, w_ref, m_ref, v_ref, go_ref, d_ref, mo_ref, vo_ref)

    blk = pl.BlockSpec((None,) * lead + (rows, c), lambda i: (0,) * lead + (i, 0))
    return pl.pallas_call(
        body, name=name, grid=(r // rows,), out_shape=tuple(jax.ShapeDtypeStruct(w.shape, f32) for _ in range(4)),
        in_specs=[pl.BlockSpec((p, rows, c), lambda i: (0, i, 0)), blk, blk, blk], out_specs=(blk, blk, blk, blk),
        compiler_params=_cp(("arbitrary",)),
    )(parts, w, m, v)


def _in_proj(x, g, scale, shift, w_pad, tl, comm):
    l = x.shape[0]
    c_in, c_out, c_sems = comm.specs()

    def body(*refs):
        x_ref, g_ref, sc_ref, sh_ref, w_ref = refs[:5]
        cin = refs[5:5 + len(c_in)]
        p_ref, h_ref = refs[5 + len(c_in):7 + len(c_in)]
        cout = refs[7 + len(c_in):7 + len(c_in) + comm.n]
        sems = refs[7 + len(c_in) + comm.n:]

        def compute():
            xv = x_ref[...]
            r = lax.rsqrt(jnp.mean(xv * xv, axis=-1, keepdims=True) + EPS)
            h = (xv * r) * (g_ref[...] * (1.0 + sc_ref[...])) + sh_ref[...]
            hb = h.astype(bf16)
            h_ref[...] = hb
            p_ref[...] = jnp.dot(hb, w_ref[...], preferred_element_type=f32).astype(bf16)

        comm.run(cin, cout, sems, l // tl, compute)

    vec = pl.BlockSpec((1, D), lambda i: (0, 0))
    return pl.pallas_call(
        body, name="in_proj", grid=(l // tl,),
        out_shape=(jax.ShapeDtypeStruct((l, PW), bf16), jax.ShapeDtypeStruct((l, D), bf16)) + tuple(comm.outs),
        in_specs=[pl.BlockSpec((tl, D), lambda i: (i, 0)), vec, vec, vec, pl.BlockSpec((D, PW), lambda i: (0, 0))] + c_in,
        out_specs=(pl.BlockSpec((tl, PW), lambda i: (i, 0)), pl.BlockSpec((tl, D), lambda i: (i, 0))) + tuple(c_out),
        scratch_shapes=c_sems, compiler_params=_cp(("arbitrary",)),
    )(x, g, scale, shift, w_pad, *comm.ins)


def _tri(rev):
    i = lax.broadcasted_iota(jnp.int32, (CH, CH), 0)
    j = lax.broadcasted_iota(jnp.int32, (CH, CH), 1)
    return jnp.where((j >= i) if rev else (j <= i), 1.0, 0.0).astype(f32)


def _head_masks():
    lane = lax.broadcasted_iota(jnp.int32, (1, KW), 1) // HK
    return [jnp.where(lane == h, 1.0, 0.0).astype(f32) for h in range(NH)]


def _block_diag():
    r = lax.broadcasted_iota(jnp.int32, (VW, KW), 0) // HV
    c = lax.broadcasted_iota(jnp.int32, (VW, KW), 1) // HK
    return jnp.where(r == c, 1.0, 0.0).astype(f32)


def _decay(lr, w2, gb, tri, rev):
    logits = _dot(lr, w2) + gb
    a = _log_sigmoid(logits) * (1.0 / TAU)
    c = _dot_hi(tri, a)
    cl = c[0:1, :] if rev else c[CH - 1:CH, :]
    return logits, c, cl


def _stack_heads(t, hm):
    return jnp.concatenate([t * hm[h] for h in range(NH)], axis=0)


def _chunk_fwd(q, k, v, c, cl, st, tri, hm, bd):
    tri4 = jnp.concatenate([tri] * NH, axis=0)
    qd = q * jnp.exp(c) * QSCALE
    kd = k * jnp.exp(-c)
    kdec = k * jnp.exp(cl - c)
    pst = _nt(_stack_heads(qd, hm), kd) * tri4
    intra = jnp.concatenate([_dot(pst[h * CH:(h + 1) * CH], v[:, h * HV:(h + 1) * HV]) for h in range(NH)], axis=1)
    o = _nt(qd, st) + intra
    st_new = st * jnp.exp(cl) + bd * _tn(v, kdec)
    return o, st_new


def _state_fwd(k, v, c, cl, st, bd):
    return st * jnp.exp(cl) + bd * _tn(v, k * jnp.exp(cl - c))


def _chunk_bwd(q, k, v, c, cl, st0, dst, do, tri, trit, hm, bd):
    ecl = jnp.exp(cl)
    edec = jnp.exp(cl - c)
    kdec = k * edec
    dv = _nt(kdec, dst)
    dkdec = _dot(v, dst)
    dcl = jnp.sum(dst * st0, axis=0, keepdims=True) * ecl + jnp.sum(dkdec * kdec, axis=0, keepdims=True)
    dc = -dkdec * kdec
    dk = dkdec * edec
    dst0 = dst * ecl
    dq = None
    if q is not None:
        tri4 = jnp.concatenate([tri] * NH, axis=0)
        ec, enc = jnp.exp(c), jnp.exp(-c)
        qd = q * ec * QSCALE
        kd = k * enc
        qs = _stack_heads(qd, hm)
        pst = _nt(qs, kd) * tri4
        dpst = jnp.concatenate([_nt(do[:, h * HV:(h + 1) * HV], v[:, h * HV:(h + 1) * HV]) for h in range(NH)], axis=0) * tri4
        dv = dv + jnp.concatenate([_tn(pst[h * CH:(h + 1) * CH], do[:, h * HV:(h + 1) * HV]) for h in range(NH)], axis=1)
        dqd = _dot(do, st0)
        for h in range(NH):
            dqd = dqd + hm[h] * _dot(dpst[h * CH:(h + 1) * CH], kd)
        dkd = _tn(dpst, qs)
        dst0 = dst0 + bd * _tn(do, qd)
        dc = dc + dqd * qd - dkd * kd
        dq = dqd * ec * QSCALE
        dk = dk + dkd * enc
    da = _dot_hi(trit, dc) + dcl
    return dq, dk, dv, da, dst0


def _bdot(a, b):
    return lax.dot_general(a.astype(bf16), b.astype(bf16), (((2,), (1,)), ((0,), (0,))), preferred_element_type=f32)


def _bnt(a, b):
    return lax.dot_general(a.astype(bf16), b.astype(bf16), (((2,), (2,)), ((0,), (0,))), preferred_element_type=f32)


def _btn(a, b):
    return lax.dot_general(a.astype(bf16), b.astype(bf16), (((1,), (1,)), ((0,), (0,))), preferred_element_type=f32)


def _scan_chunks(x, rev):
    nc = x.shape[0]
    hi = x.astype(bf16)
    r1 = x - hi.astype(f32)
    mid = r1.astype(bf16)
    lo = (r1 - mid.astype(f32)).astype(bf16)
    terms = jnp.concatenate([hi, mid, lo], axis=1)
    tri3 = jnp.broadcast_to(jnp.concatenate([_tri(rev)] * 3, axis=1).astype(bf16)[None], (nc, CH, 3 * CH))
    return lax.dot_general(tri3, terms, (((2,), (1,)), ((0,), (0,))), preferred_element_type=f32)


class _Tile:
    pass


def _tile_prep(q_ref, k_ref, lr_ref, w2_ref, gb_ref, rev, nc):
    t = _Tile()
    tg = nc * CH
    t.logits = _dot(lr_ref[...], w2_ref[...]) + gb_ref[...]
    c = _scan_chunks((_log_sigmoid(t.logits) * (1.0 / TAU)).reshape(nc, CH, KW), rev)
    cl = c[:, 0:1, :] if rev else c[:, CH - 1:CH, :]
    k = k_ref[...].astype(f32).reshape(nc, CH, KW)
    t.ec, t.enc, t.edec, t.ecl = jnp.exp(c), jnp.exp(-c), jnp.exp(cl - c), jnp.exp(cl)
    t.qd = q_ref[...].astype(f32).reshape(nc, CH, KW) * t.ec * QSCALE
    t.kd = k * t.enc
    t.kdec = k * t.edec
    hm = _head_masks()
    t.tri4 = jnp.concatenate([_tri(rev)] * NH, axis=0)[None]
    t.qs = jnp.concatenate([t.qd * hm[h] for h in range(NH)], axis=1)
    t.pst = _bnt(t.qs, t.kd) * t.tri4
    return t


def _gla_fwd(p, w2p, gb, s0, rev, tg, name):
    l = p.shape[0]
    nb, nc = l // tg, tg // CH

    def body(q_ref, k_ref, v_ref, lr_ref, w2_ref, gb_ref, s0_ref, o_ref, st_ref, st):
        @pl.when(pl.program_id(0) == 0)
        def _():
            st[...] = s0_ref[...]

        t = _tile_prep(q_ref, k_ref, lr_ref, w2_ref, gb_ref, rev, nc)
        v = v_ref[...].reshape(nc, CH, VW)
        intra = jnp.concatenate([_bdot(t.pst[:, h * CH:(h + 1) * CH], v[:, :, h * HV:(h + 1) * HV]) for h in range(NH)], axis=2)
        kv = _btn(v, t.kdec) * _block_diag()[None]
        s = st[...]
        for n in (range(nc - 1, -1, -1) if rev else range(nc)):
            st_ref[n] = s
            s = s * t.ecl[n] + kv[n]
        st[...] = s
        o_ref[...] = (_bnt(t.qd, st_ref[...]) + intra).reshape(tg, VW)

    blk = (lambda i: nb - 1 - i) if rev else (lambda i: i)
    return pl.pallas_call(
        body, name=name, grid=(nb,),
        out_shape=(jax.ShapeDtypeStruct((l, VW), f32), jax.ShapeDtypeStruct((l // CH, VW, KW), f32)),
        in_specs=[pl.BlockSpec((tg, KW), lambda i: (blk(i), PQ // KW)), pl.BlockSpec((tg, KW), lambda i: (blk(i), PK // KW)),
                  pl.BlockSpec((tg, VW), lambda i: (blk(i), PV // VW)), pl.BlockSpec((tg, LRW), lambda i: (blk(i), PLR // LRW)),
                  pl.BlockSpec((LRW, KW), lambda i: (0, 0)), pl.BlockSpec((1, KW), lambda i: (0, 0)),
                  pl.BlockSpec((VW, KW), lambda i: (0, 0))],
        out_specs=(pl.BlockSpec((tg, VW), lambda i: (blk(i), 0)), pl.BlockSpec((nc, VW, KW), lambda i: (blk(i), 0, 0))),
        scratch_shapes=[pltpu.VMEM((VW, KW), f32)],
        compiler_params=_cp(("arbitrary",)),
    )(p, p, p, p, w2p, gb, s0)


def _gla_bwd(p, do, states, w2p, gb, prev, rev, tg, name, comm):
    l = p.shape[0]
    nb, nc = l // tg, tg // CH
    out_dt = f32 if prev is None else bf16
    c_in, c_out, c_sems = comm.specs()

    def body(*refs):
        q_ref, k_ref, v_ref, lr_ref, do_ref, st_ref, w2_ref, gb_ref = refs[:8]
        refs = refs[8:]
        if prev is not None:
            pq_ref, pl_ref = refs[:2]
            refs = refs[2:]
        cin, refs = refs[:len(c_in)], refs[len(c_in):]
        dqkv_ref, dlr_ref, dw2_ref, dgb_ref, ds0_ref = refs[:5]
        cout, dst, ds_buf, sems = refs[5:5 + comm.n], refs[5 + comm.n], refs[6 + comm.n], refs[7 + comm.n:]
        comm.run(cin, cout, sems, nb, lambda: compute(q_ref, k_ref, v_ref, lr_ref, do_ref, st_ref, w2_ref, gb_ref,
                                                      pq_ref if prev is not None else None, pl_ref if prev is not None else None,
                                                      dqkv_ref, dlr_ref, dw2_ref, dgb_ref, ds0_ref, dst, ds_buf))

    def compute(q_ref, k_ref, v_ref, lr_ref, do_ref, st_ref, w2_ref, gb_ref, pq_ref, pl_ref,
                dqkv_ref, dlr_ref, dw2_ref, dgb_ref, ds0_ref, dst, ds_buf):
        @pl.when(pl.program_id(0) == 0)
        def _():
            dst[...] = jnp.zeros_like(dst)
            dw2_ref[...] = jnp.zeros_like(dw2_ref)
            dgb_ref[...] = jnp.zeros_like(dgb_ref)

        t = _tile_prep(q_ref, k_ref, lr_ref, w2_ref, gb_ref, rev, nc)
        hm = _head_masks()
        v = v_ref[...].reshape(nc, CH, VW)
        do = do_ref[...].reshape(nc, CH, VW)
        heads = lambda a, h: a[:, :, h * HV:(h + 1) * HV]
        dpst = jnp.concatenate([_bnt(heads(do, h), heads(v, h)) for h in range(NH)], axis=1) * t.tri4
        dv = jnp.concatenate([_btn(t.pst[:, h * CH:(h + 1) * CH], heads(do, h)) for h in range(NH)], axis=2)
        dqd = _bdot(do, st_ref[...])
        for h in range(NH):
            dqd = dqd + hm[h] * _bdot(dpst[:, h * CH:(h + 1) * CH], t.kd)
        dkd = _btn(dpst, t.qs)
        u = _btn(do, t.qd) * _block_diag()[None]
        d = dst[...]
        for n in (range(nc) if rev else range(nc - 1, -1, -1)):
            ds_buf[n] = d
            d = d * t.ecl[n] + u[n]
        dst[...] = d
        ds0_ref[...] = d
        ds = ds_buf[...]
        dv = dv + _bnt(t.kdec, ds)
        dkdec = _bdot(v, ds)
        dcl = jnp.sum(ds * st_ref[...], axis=1, keepdims=True) * t.ecl + jnp.sum(dkdec * t.kdec, axis=1, keepdims=True)
        dc = dqd * t.qd - dkd * t.kd - dkdec * t.kdec
        da = _scan_chunks(dc, not rev) + dcl
        dq = dqd * t.ec * QSCALE
        dk = dkd * t.enc + dkdec * t.edec
        dlog = da.reshape(tg, KW) * _sigmoid(-t.logits) * (1.0 / TAU)
        dlr = _nt(dlog, w2_ref[...])
        dw2_ref[...] += _tn(lr_ref[...], dlog)
        dgb_ref[...] += jnp.sum(dlog, axis=0, keepdims=True)
        dqkv = jnp.concatenate([dq, dk, dv], axis=2).reshape(tg, 2 * KW + VW)
        if prev is not None:
            dqkv = dqkv + pq_ref[...]
            dlr = dlr + pl_ref[...]
        dqkv_ref[...] = dqkv.astype(out_dt)
        dlr_ref[...] = dlr.astype(out_dt)

    blk = (lambda i: i) if rev else (lambda i: nb - 1 - i)
    in_specs = [pl.BlockSpec((tg, KW), lambda i: (blk(i), PQ // KW)), pl.BlockSpec((tg, KW), lambda i: (blk(i), PK // KW)),
                pl.BlockSpec((tg, VW), lambda i: (blk(i), PV // VW)), pl.BlockSpec((tg, LRW), lambda i: (blk(i), PLR // LRW)),
                pl.BlockSpec((tg, VW), lambda i: (blk(i), 0)), pl.BlockSpec((nc, VW, KW), lambda i: (blk(i), 0, 0)),
                pl.BlockSpec((LRW, KW), lambda i: (0, 0)), pl.BlockSpec((1, KW), lambda i: (0, 0))]
    args = [p, p, p, p, do, states, w2p, gb]
    if prev is not None:
        in_specs += [pl.BlockSpec((tg, 2 * KW + VW), lambda i: (blk(i), 0)), pl.BlockSpec((tg, LRW), lambda i: (blk(i), 0))]
        args += list(prev)
    return pl.pallas_call(
        body, name=name, grid=(nb,),
        out_shape=(jax.ShapeDtypeStruct((l, 2 * KW + VW), out_dt), jax.ShapeDtypeStruct((l, LRW), out_dt),
                   jax.ShapeDtypeStruct((LRW, KW), f32), jax.ShapeDtypeStruct((1, KW), f32), jax.ShapeDtypeStruct((VW, KW), f32))
        + tuple(comm.outs),
        in_specs=in_specs + c_in,
        out_specs=(pl.BlockSpec((tg, 2 * KW + VW), lambda i: (blk(i), 0)), pl.BlockSpec((tg, LRW), lambda i: (blk(i), 0)),
                   pl.BlockSpec((LRW, KW), lambda i: (0, 0)), pl.BlockSpec((1, KW), lambda i: (0, 0)),
                   pl.BlockSpec((VW, KW), lambda i: (0, 0))) + tuple(c_out),
        scratch_shapes=[pltpu.VMEM((VW, KW), f32), pltpu.VMEM((nc, VW, KW), f32)] + c_sems,
        compiler_params=_cp(("arbitrary",)),
    )(*args, *comm.ins)


def _ctx_hidden(ctx_ref, g_ref, sc_ref, sh_ref):
    xv = ctx_ref[...]
    r = lax.rsqrt(jnp.mean(xv * xv, axis=-1, keepdims=True) + EPS)
    xn = xv * r
    return xn, xn * (g_ref[...] * (1.0 + sc_ref[...])) + sh_ref[...]


_CTX_W_SPECS = [pl.BlockSpec((D, KW), lambda i: (0, PK // KW)), pl.BlockSpec((D, VW), lambda i: (0, PV // VW)),
                pl.BlockSpec((D, LRW), lambda i: (0, PLR // LRW))]


def _ctx_fwd(ctx, g, scale, shift, w_pad, w2f, w2b, gbf, gbb):
    ncc = CTX // CH

    def body(ctx_ref, g_ref, sc_ref, sh_ref, wk_ref, wv_ref, wl_ref, w2f_ref, w2b_ref, gbf_ref, gbb_ref, sf_ref, sb_ref):
        _, hc = _ctx_hidden(ctx_ref, g_ref, sc_ref, sh_ref)
        k, v, lr = _dot(hc, wk_ref[...]), _dot(hc, wv_ref[...]), _dot(hc, wl_ref[...])
        bd = _block_diag()
        for rev, w2_ref, gb_ref, out in ((False, w2f_ref, gbf_ref, sf_ref), (True, w2b_ref, gbb_ref, sb_ref)):
            tri = _tri(rev)
            st = jnp.zeros((VW, KW), f32)
            for j in (range(ncc - 1, -1, -1) if rev else range(ncc)):
                rows = slice(j * CH, (j + 1) * CH)
                _, c, cl = _decay(lr[rows], w2_ref[...], gb_ref[...], tri, rev)
                st = _state_fwd(k[rows], v[rows], c, cl, st, bd)
            out[...] = st

    vec = pl.BlockSpec((1, D), lambda i: (0, 0))
    w2s = pl.BlockSpec((LRW, KW), lambda i: (0, 0))
    gbs = pl.BlockSpec((1, KW), lambda i: (0, 0))
    sts = pl.BlockSpec((VW, KW), lambda i: (0, 0))
    return pl.pallas_call(
        body, name="ctx_fwd", grid=(1,), out_shape=(jax.ShapeDtypeStruct((VW, KW), f32),) * 2,
        in_specs=[pl.BlockSpec((CTX, D), lambda i: (0, 0)), vec, vec, vec] + _CTX_W_SPECS + [w2s, w2s, gbs, gbs],
        out_specs=(sts, sts), compiler_params=_cp(("arbitrary",)),
    )(ctx, g, scale, shift, w_pad, w_pad, w_pad, w2f, w2b, gbf, gbb)


def _ctx_bwd(ctx, g, scale, shift, w_pad, w2f, w2b, gbf, gbb, dsf, dsb):
    ncc = CTX // CH

    def body(ctx_ref, g_ref, sc_ref, sh_ref, wk_ref, wv_ref, wl_ref, w2f_ref, w2b_ref, gbf_ref, gbb_ref, dsf_ref, dsb_ref,
             dwk_ref, dwv_ref, dwl_ref, dmod_ref, dg_ref, dw2_ref, dgb_ref):
        xn, hc = _ctx_hidden(ctx_ref, g_ref, sc_ref, sh_ref)
        k, v, lr = _dot(hc, wk_ref[...]), _dot(hc, wv_ref[...]), _dot(hc, wl_ref[...])
        bd = _block_diag()
        dk_rows, dv_rows, dl_rows = [None] * ncc, [None] * ncc, [None] * ncc
        for d, (rev, w2_ref, gb_ref, ds_ref) in enumerate(((False, w2f_ref, gbf_ref, dsf_ref), (True, w2b_ref, gbb_ref, dsb_ref))):
            tri = _tri(rev)
            order = list(range(ncc - 1, -1, -1) if rev else range(ncc))
            st, saved = jnp.zeros((VW, KW), f32), {}
            for j in order:
                rows = slice(j * CH, (j + 1) * CH)
                logits, c, cl = _decay(lr[rows], w2_ref[...], gb_ref[...], tri, rev)
                saved[j] = (logits, c, cl, st)
                st = _state_fwd(k[rows], v[rows], c, cl, st, bd)
            dst = ds_ref[...]
            dw2 = jnp.zeros((LRW, KW), f32)
            dgb = jnp.zeros((1, KW), f32)
            for j in reversed(order):
                rows = slice(j * CH, (j + 1) * CH)
                logits, c, cl, st0 = saved[j]
                _, dk, dv, da, dst = _chunk_bwd(None, k[rows], v[rows], c, cl, st0, dst, None, tri, _tri(not rev), None, bd)
                dlog = da * _sigmoid(-logits) * (1.0 / TAU)
                dl = _nt(dlog, w2_ref[...])
                dw2 = dw2 + _tn(lr[rows], dlog)
                dgb = dgb + jnp.sum(dlog, axis=0, keepdims=True)
                dk_rows[j] = dk if dk_rows[j] is None else dk_rows[j] + dk
                dv_rows[j] = dv if dv_rows[j] is None else dv_rows[j] + dv
                dl_rows[j] = dl if dl_rows[j] is None else dl_rows[j] + dl
            dw2_ref[d] = dw2
            dgb_ref[d] = dgb
        dk, dv, dl = (jnp.concatenate(t, axis=0) for t in (dk_rows, dv_rows, dl_rows))
        dwk_ref[...] = _tn(hc, dk)
        dwv_ref[...] = _tn(hc, dv)
        dwl_ref[...] = _tn(hc, dl)
        dh = _nt(dk, wk_ref[...]) + _nt(dv, wv_ref[...]) + _nt(dl, wl_ref[...])
        gx = dh * xn
        dmod_ref[:, 0:D] = jnp.sum(dh, axis=0, keepdims=True)
        dmod_ref[:, D:2 * D] = jnp.sum(gx, axis=0, keepdims=True) * g_ref[...]
        dg_ref[...] = jnp.sum(gx, axis=0, keepdims=True) * (1.0 + sc_ref[...])

    vec = pl.BlockSpec((1, D), lambda i: (0, 0))
    w2s = pl.BlockSpec((LRW, KW), lambda i: (0, 0))
    gbs = pl.BlockSpec((1, KW), lambda i: (0, 0))
    sts = pl.BlockSpec((VW, KW), lambda i: (0, 0))
    full = lambda *s: pl.BlockSpec(s, lambda i: (0,) * len(s))
    return pl.pallas_call(
        body, name="ctx_bwd", grid=(1,),
        out_shape=(jax.ShapeDtypeStruct((D, KW), f32), jax.ShapeDtypeStruct((D, VW), f32), jax.ShapeDtypeStruct((D, LRW), f32),
                   jax.ShapeDtypeStruct((1, 2 * D), f32), jax.ShapeDtypeStruct((1, D), f32),
                   jax.ShapeDtypeStruct((2, LRW, KW), f32), jax.ShapeDtypeStruct((2, 1, KW), f32)),
        in_specs=[pl.BlockSpec((CTX, D), lambda i: (0, 0)), vec, vec, vec] + _CTX_W_SPECS + [w2s, w2s, gbs, gbs, sts, sts],
        out_specs=(full(D, KW), full(D, VW), full(D, LRW), full(1, 2 * D), full(1, D), full(2, LRW, KW), full(2, 1, KW)),
        compiler_params=_cp(("arbitrary",)),
    )(ctx, g, scale, shift, w_pad, w_pad, w_pad, w2f, w2b, gbf, gbb, dsf, dsb)


def _layernorm(va, g, b):
    mu = jnp.mean(va, axis=-1, keepdims=True)
    xc = va - mu
    rstd = lax.rsqrt(jnp.mean(xc * xc, axis=-1, keepdims=True) + EPS)
    vhat = xc * rstd
    return vhat, rstd, vhat * g + b


def _mix_fwd(p, ln_g, ln_b, ws, bs_t):
    l = p.shape[0]
    nch = l // ACH
    half = AW // 2

    def body(p_ref, g_ref, b_ref, ws_ref, bs_ref, sv_ref, va_buf, col_buf, sem):
        cp = pltpu.make_async_copy(p_ref.at[:, pl.ds(PVA, AW)], va_buf, sem)
        cp.start()
        cp.wait()

        def rows_step(n, carry):
            rows = pl.ds(pl.multiple_of(n * ACH, ACH), ACH)
            _, _, vn = _layernorm(va_buf[rows, :].astype(f32), g_ref[...], b_ref[...])
            for gi in range(2):
                sl = slice(gi * ACH, (gi + 1) * ACH)
                sv_ref[rows, sl] = (_dot(ws_ref[gi], vn[:, sl]) + bs_ref[:, gi:gi + 1]).astype(bf16)
            col_buf[0, rows, :] = vn[:, half:half + ACH]
            col_buf[1, rows, :] = vn[:, half + ACH:]
            return carry

        lax.fori_loop(0, nch, rows_step, 0, unroll=4)

        def cols_step(cidx, carry):
            rows = pl.ds(cidx, ACH, stride=GW)
            for gi in range(2, 4):
                col_buf[gi - 2, rows, :] = _dot(ws_ref[gi], col_buf[gi - 2, rows, :]) + bs_ref[:, gi:gi + 1]
            return carry

        lax.fori_loop(0, GW, cols_step, 0, unroll=4)

        def out_step(n, carry):
            rows = pl.ds(pl.multiple_of(n * ACH, ACH), ACH)
            sv_ref[rows, half:half + ACH] = col_buf[0, rows, :].astype(bf16)
            sv_ref[rows, half + ACH:] = col_buf[1, rows, :].astype(bf16)
            return carry

        lax.fori_loop(0, nch, out_step, 0, unroll=4)

    vm = pl.BlockSpec(memory_space=pltpu.VMEM)
    return pl.pallas_call(
        body, name="mix_fwd", out_shape=jax.ShapeDtypeStruct((l, AW), bf16),
        in_specs=[pl.BlockSpec(memory_space=pl.ANY), vm, vm, vm, vm], out_specs=vm,
        scratch_shapes=[pltpu.VMEM((l, AW), bf16), pltpu.VMEM((2, l, ACH), f32), pltpu.SemaphoreType.DMA(())],
        compiler_params=_cp(),
    )(p, ln_g, ln_b, ws, bs_t)


def _mix_bwd(p, dsv, ln_g, ln_b, ws_t):
    l = p.shape[0]
    nch = l // ACH
    half = AW // 2

    def body(p_ref, dsv_hbm, g_ref, b_ref, wst_ref, dva_ref, dws_ref, dbs_ref, dg_ref, db_ref, va_buf, dsv_buf, vn_col, ds_col, sems):
        cp1 = pltpu.make_async_copy(p_ref.at[:, pl.ds(PVA, AW)], va_buf, sems.at[0])
        cp2 = pltpu.make_async_copy(dsv_hbm, dsv_buf, sems.at[1])
        cp1.start()
        cp2.start()
        dws_ref[...] = jnp.zeros_like(dws_ref)
        dbs_ref[...] = jnp.zeros_like(dbs_ref)
        dg_ref[...] = jnp.zeros_like(dg_ref)
        db_ref[...] = jnp.zeros_like(db_ref)
        cp1.wait()
        cp2.wait()

        def rows_step(n, carry):
            rows = pl.ds(pl.multiple_of(n * ACH, ACH), ACH)
            _, _, vn = _layernorm(va_buf[rows, :].astype(f32), g_ref[...], b_ref[...])
            ds = dsv_buf[rows, :].astype(f32)
            for gi in range(2):
                sl = slice(gi * ACH, (gi + 1) * ACH)
                dws_ref[gi] += _nt(ds[:, sl], vn[:, sl])
                dbs_ref[gi] += ds[:, sl]
            for gi in range(2):
                sl = slice(half + gi * ACH, half + (gi + 1) * ACH)
                vn_col[gi, rows, :] = vn[:, sl]
                ds_col[gi, rows, :] = ds[:, sl]
            return carry

        lax.fori_loop(0, nch, rows_step, 0, unroll=4)

        def cols_step(cidx, carry):
            rows = pl.ds(cidx, ACH, stride=GW)
            for gi in range(2, 4):
                ds = ds_col[gi - 2, rows, :]
                dws_ref[gi] += _nt(ds, vn_col[gi - 2, rows, :])
                dbs_ref[gi] += ds
                ds_col[gi - 2, rows, :] = _dot(wst_ref[gi], ds)
            return carry

        lax.fori_loop(0, GW, cols_step, 0, unroll=4)

        def out_step(n, carry):
            rows = pl.ds(pl.multiple_of(n * ACH, ACH), ACH)
            vhat, rstd, _ = _layernorm(va_buf[rows, :].astype(f32), g_ref[...], b_ref[...])
            ds = dsv_buf[rows, :].astype(f32)
            dvn = jnp.concatenate([_dot(wst_ref[0], ds[:, 0:ACH]), _dot(wst_ref[1], ds[:, ACH:half]), ds_col[0, rows, :], ds_col[1, rows, :]], axis=1)
            dg_ref[...] += jnp.sum(dvn * vhat, axis=0, keepdims=True)
            db_ref[...] += jnp.sum(dvn, axis=0, keepdims=True)
            dvh = dvn * g_ref[...]
            dva = rstd * (dvh - jnp.mean(dvh, axis=-1, keepdims=True) - vhat * jnp.mean(dvh * vhat, axis=-1, keepdims=True))
            dva_ref[rows, :] = dva.astype(bf16)
            return carry

        lax.fori_loop(0, nch, out_step, 0, unroll=4)

    vm = pl.BlockSpec(memory_space=pltpu.VMEM)
    hbm = pl.BlockSpec(memory_space=pl.ANY)
    return pl.pallas_call(
        body, name="mix_bwd",
        out_shape=(jax.ShapeDtypeStruct((l, AW), bf16), jax.ShapeDtypeStruct((4, ACH, ACH), f32), jax.ShapeDtypeStruct((4, ACH, ACH), f32),
                   jax.ShapeDtypeStruct((1, AW), f32), jax.ShapeDtypeStruct((1, AW), f32)),
        in_specs=[hbm, hbm, vm, vm, vm], out_specs=(vm, vm, vm, vm, vm),
        scratch_shapes=[pltpu.VMEM((l, AW), bf16), pltpu.VMEM((l, AW), bf16), pltpu.VMEM((2, l, ACH), f32), pltpu.VMEM((2, l, ACH), f32),
                        pltpu.SemaphoreType.DMA((2,))],
        compiler_params=_cp(),
    )(p, dsv, ln_g, ln_b, ws_t)


def _mid(x, tgt, p, o_f, o_b, sv, gate, gf, gb_norm, w_pa, w_pb, w_out, tl):
    l = x.shape[0]

    def body(x_ref, t_ref, zb_ref, ua_ref, za_ref, g1_ref, g2_ref, of_ref, ob_ref, sv_ref, gate_ref, gf_ref, gbn_ref,
             wpa_ref, wpb_ref, wout_ref,
             dx1_ref, dzbua_ref, dzag_ref, dsv_ref, do_ref, dwout_bf, dwpa_bf, dwpb_bf, dgf_ref, dgate_ref, dgbn_ref, loss_ref,
             dwout_ref, dwpa_ref, dwpb_ref):
        @pl.when(pl.program_id(0) == 0)
        def _():
            for r in (dwout_ref, dwpa_ref, dwpb_ref, dgf_ref, dgate_ref, dgbn_ref, loss_ref):
                r[...] = jnp.zeros_like(r)

        o = of_ref[...] + ob_ref[...]
        rr = jnp.concatenate(
            [jnp.broadcast_to(lax.rsqrt(jnp.mean(o[:, h * HV:(h + 1) * HV] ** 2, axis=-1, keepdims=True) + EPS), (tl, HV))
             for h in range(NH)], axis=1)
        ohat = o * rr
        on = ohat * gbn_ref[...]
        szb, dszb = _silu_and_grad(zb_ref[...].astype(f32))
        tb = on * szb
        u = ua_ref[...].astype(f32)
        svv = sv_ref[...].astype(f32)
        sza, dsza = _silu_and_grad(za_ref[...].astype(f32))
        ta = u * svv * sza
        ya = _dot(ta, wpa_ref[...])
        yb = _dot(tb, wpb_ref[...])
        g1 = _sigmoid(g1_ref[...].astype(f32))
        g2 = _sigmoid(g2_ref[...].astype(f32))
        m = g1 * ya + g2 * yb
        y2 = _dot(m, wout_ref[...])
        x1 = x_ref[...] + gate_ref[...] * y2
        r1 = lax.rsqrt(jnp.mean(x1 * x1, axis=-1, keepdims=True) + EPS)
        x1n = x1 * r1
        err = x1n * gf_ref[...] - t_ref[...]
        loss_ref[...] += jnp.sum(jnp.sum(err * err, axis=-1, keepdims=True), axis=0, keepdims=True) * (0.5 / D)
        dout = err * (1.0 / D)
        dgf_ref[...] += jnp.sum(dout * x1n, axis=0, keepdims=True)
        dx1n = dout * gf_ref[...]
        dx1 = r1 * (dx1n - x1n * jnp.mean(dx1n * x1n, axis=-1, keepdims=True))
        dx1_ref[...] = dx1
        dgate_ref[...] += jnp.sum(dx1 * y2, axis=0, keepdims=True)
        dy2 = dx1 * gate_ref[...]
        dwout_ref[...] += _tn(m, dy2)
        dm = _nt(dy2, wout_ref[...])
        dya = dm * g1
        dyb = dm * g2
        dzag_ref[:, AW:AW + D] = (dm * ya * g1 * (1.0 - g1)).astype(bf16)
        dzag_ref[:, AW + D:] = (dm * yb * g2 * (1.0 - g2)).astype(bf16)
        dwpa_ref[...] += _tn(ta, dya)
        dta = _nt(dya, wpa_ref[...])
        dzbua_ref[:, AW:] = (dta * svv * sza).astype(bf16)
        dsv_ref[...] = (dta * u * sza).astype(bf16)
        dzag_ref[:, 0:AW] = (dta * u * svv * dsza).astype(bf16)
        dwpb_ref[...] += _tn(tb, dyb)
        dtb = _nt(dyb, wpb_ref[...])
        don = dtb * szb
        dzbua_ref[:, 0:AW] = (dtb * on * dszb).astype(bf16)
        dgbn_ref[...] += jnp.sum(don * ohat, axis=0, keepdims=True)
        doh = don * gbn_ref[...]
        prod = doh * ohat
        mh = jnp.concatenate(
            [jnp.broadcast_to(jnp.mean(prod[:, h * HV:(h + 1) * HV], axis=-1, keepdims=True), (tl, HV)) for h in range(NH)], axis=1)
        do_ref[...] = (rr * (doh - ohat * mh)).astype(bf16)

        @pl.when(pl.program_id(0) == l // tl - 1)
        def _():
            for acc, out in ((dwout_ref, dwout_bf), (dwpa_ref, dwpa_bf), (dwpb_ref, dwpb_bf)):
                out[...] = acc[...].astype(bf16)

    row = lambda w, j: pl.BlockSpec((tl, w), lambda i, j=j: (i, j))
    full = lambda *s: pl.BlockSpec(s, lambda i: (0,) * len(s))
    return pl.pallas_call(
        body, name="mid", grid=(l // tl,),
        out_shape=(jax.ShapeDtypeStruct((l, D), f32), jax.ShapeDtypeStruct((l, 2 * AW), bf16), jax.ShapeDtypeStruct((l, AW + 2 * D), bf16),
                   jax.ShapeDtypeStruct((l, AW), bf16), jax.ShapeDtypeStruct((l, VW), bf16),
                   jax.ShapeDtypeStruct((D, D), bf16), jax.ShapeDtypeStruct((AW, D), bf16), jax.ShapeDtypeStruct((VW, D), bf16),
                   jax.ShapeDtypeStruct((1, D), f32), jax.ShapeDtypeStruct((1, D), f32), jax.ShapeDtypeStruct((1, VW), f32),
                   jax.ShapeDtypeStruct((1, 1), f32)),
        scratch_shapes=[pltpu.VMEM((D, D), f32), pltpu.VMEM((AW, D), f32), pltpu.VMEM((VW, D), f32)],
        in_specs=[row(D, 0), row(D, 0), row(AW, PZB // AW), row(AW, PUA // AW), row(AW, PZA // AW), row(D, PG1 // D), row(D, PG2 // D),
                  row(VW, 0), row(VW, 0), row(AW, 0), full(1, D), full(1, D), full(1, VW), full(AW, D), full(VW, D), full(D, D)],
        out_specs=(row(D, 0), row(2 * AW, 0), row(AW + 2 * D, 0), row(AW, 0), row(VW, 0),
                   full(D, D), full(AW, D), full(VW, D), full(1, D), full(1, D), full(1, VW), full(1, 1)),
        compiler_params=_cp(("arbitrary",)),
    )(x, tgt, p, p, p, p, p, o_f, o_b, sv, gate, gf, gb_norm, w_pa, w_pb, w_out)


def _in_bwd(x, dx1, dqkv, dzbua, dva, dzag, dlr, w_pad, g, scale, tl, comm):
    l = x.shape[0]
    c_in, c_out, c_sems = comm.specs()

    def body(*refs):
        cin = refs[14:14 + len(c_in)]
        outs = refs[14 + len(c_in):]
        comm.run(cin, outs[4:4 + comm.n], outs[4 + comm.n:], l // tl, lambda: compute(*refs[:14], *outs[:4]))

    def compute(x_ref, dx1_ref, a_ref, b_ref, c_ref, e_ref, lr_ref, wa_ref, wb_ref, wc_ref, we_ref, wl_ref, g_ref, sc_ref,
                gx_ref, dsh_ref, dsc_ref, dg_ref):
        @pl.when(pl.program_id(0) == 0)
        def _():
            for r in (dsh_ref, dsc_ref, dg_ref):
                r[...] = jnp.zeros_like(r)

        dh = (_nt(a_ref[...], wa_ref[...]) + _nt(b_ref[...], wb_ref[...]) + _nt(c_ref[...], wc_ref[...]) + _nt(e_ref[...], we_ref[...])
              + _nt(lr_ref[...], wl_ref[...]))
        xv = x_ref[...]
        r = lax.rsqrt(jnp.mean(xv * xv, axis=-1, keepdims=True) + EPS)
        xn = xv * r
        gxn = jnp.sum(dh * xn, axis=0, keepdims=True)
        dsh_ref[...] += jnp.sum(dh, axis=0, keepdims=True)
        dsc_ref[...] += gxn * g_ref[...]
        dg_ref[...] += gxn * (1.0 + sc_ref[...])
        dxn = dh * (g_ref[...] * (1.0 + sc_ref[...]))
        gx_ref[...] = dx1_ref[...] + r * (dxn - xn * jnp.mean(dxn * xn, axis=-1, keepdims=True))

    row = lambda w: pl.BlockSpec((tl, w), lambda i: (i, 0))
    wcol = lambda w, j: pl.BlockSpec((D, w), lambda i, j=j: (0, j))
    vec = pl.BlockSpec((1, D), lambda i: (0, 0))
    return pl.pallas_call(
        body, name="in_bwd", grid=(l // tl,),
        out_shape=(jax.ShapeDtypeStruct((l, D), f32),) + (jax.ShapeDtypeStruct((1, D), f32),) * 3 + tuple(comm.outs),
        in_specs=[row(D), row(D), row(2 * KW + VW), row(2 * AW), row(AW), row(AW + 2 * D), row(LRW),
                  wcol(2 * KW + VW, 0), wcol(2 * AW, PZB // (2 * AW)), wcol(AW, PVA // AW), wcol(AW + 2 * D, PZA // (AW + 2 * D)),
                  wcol(LRW, PLR // LRW), vec, vec] + c_in,
        out_specs=(row(D), vec, vec, vec) + tuple(c_out), scratch_shapes=c_sems,
        compiler_params=_cp(("arbitrary",)),
    )(x, dx1, dqkv, dzbua, dva, dzag, dlr, w_pad, w_pad, w_pad, w_pad, w_pad, g, scale, *comm.ins)


def _tn_matmul(a, bs, tl, name, comm=None):
    l, m = a.shape
    k = len(bs)
    comm = comm or _Comm([], [], None)
    c_in, c_out, c_sems = comm.specs()

    def body(a_ref, *refs):
        b_refs, cin = refs[:k], refs[k:k + len(c_in)]
        o_refs = refs[k + len(c_in):2 * k + len(c_in)]
        cout, sems = refs[2 * k + len(c_in):2 * k + len(c_in) + comm.n], refs[2 * k + len(c_in) + comm.n:]

        def compute():
            @pl.when(pl.program_id(0) == 0)
            def _():
                for o_ref in o_refs:
                    o_ref[...] = jnp.zeros_like(o_ref)

            av = a_ref[...]
            for b_ref, o_ref in zip(b_refs, o_refs):
                o_ref[...] += _tn(av, b_ref[...])

        comm.run(cin, cout, sems, l // tl, compute)

    return pl.pallas_call(
        body, name=name, grid=(l // tl,), out_shape=tuple(jax.ShapeDtypeStruct((m, b.shape[1]), f32) for b in bs) + tuple(comm.outs),
        in_specs=[pl.BlockSpec((tl, m), lambda i: (i, 0))] + [pl.BlockSpec((tl, b.shape[1]), lambda i: (i, 0)) for b in bs] + c_in,
        out_specs=tuple(pl.BlockSpec((m, b.shape[1]), lambda i: (0, 0)) for b in bs) + tuple(c_out),
        scratch_shapes=c_sems, compiler_params=_cp(("arbitrary",)),
    )(a, *bs, *comm.ins)


def _pad_gate(w2, gb):
    z = jnp.zeros((RANK, KW), f32)
    tail = jnp.zeros((LRW - 2 * RANK, KW), f32)
    w2f = jnp.concatenate([w2[0], z, tail], axis=0)
    w2b = jnp.concatenate([z, w2[1], tail], axis=0)
    return w2f, w2b, gb[0:1], gb[1:2]


EARLY_A_ROWS = 384


class _NoExchange:
    def __init__(self, w_pa, w_pb, w_out):
        self.weights = (w_pa, w_pb, w_out)

    def gather_proj(self):
        return _Comm([], [], None)

    def proj_weights(self, got):
        return self.weights

    def first(self, dw_out, dw_pa, dw_pb):
        return _Comm([], [], None)

    def early_a(self, small, blocks):
        return _Comm([], [], None)

    def early_b(self, blocks):
        return _Comm([], [], None)

    def late(self, blocks, dgt):
        return _Comm([], [], None)


class _Exchanges:
    def __init__(self, pa, pb, wo):
        self.shards = (pa, pb, wo)

    def gather_proj(self):
        def plan(i, o):
            srcs = [lambda j, r=r: r for r in i]
            dsts = [lambda j: o[0].at[:, _lanes(j)], lambda j: o[1].at[:, _lanes(j)], lambda j: o[2].at[j]]
            return srcs, dsts, None
        sds = jax.ShapeDtypeStruct
        return _Comm(self.shards, [sds((AW, D), bf16), sds((VW, D), bf16), sds((NDEV, 128, D), bf16)], plan)

    def proj_weights(self, got):
        return got[0], got[1], got[2].reshape(D, D)

    def first(self, dw_out, dw_pa, dw_pb):
        def plan(i, o):
            srcs = [lambda j: i[0].at[j], lambda j: i[1].at[:, _lanes(j)], lambda j: i[2].at[:, _lanes(j)]]
            dsts = [lambda j, r=r: r.at[j] for r in o]
            return srcs, dsts, None
        sds = jax.ShapeDtypeStruct
        return _Comm([dw_out.reshape(NDEV, 128, D), dw_pa, dw_pb],
                     [sds((NDEV, 128, D), bf16), sds((NDEV, AW, 128), bf16), sds((NDEV, VW, 128), bf16)], plan)

    def early_a(self, small, blocks):
        def plan(i, o):
            srcs = [lambda j: i[0], lambda j: i[1].at[j]]
            dsts = [lambda j, r=r: r.at[j] for r in o]
            return srcs, dsts, [None, lambda j: j >= LATE_DESTS - 1]
        sds = jax.ShapeDtypeStruct
        return _Comm([small, blocks], [sds((NDEV,) + small.shape, f32), sds(blocks.shape, bf16)], plan)

    def early_b(self, blocks):
        def plan(i, o):
            return [lambda j: i[0].at[j]], [lambda j: o[0].at[j]], [lambda j: j >= LATE_DESTS - 1]
        return _Comm([blocks], [jax.ShapeDtypeStruct(blocks.shape, bf16)], plan)

    def late(self, blocks, dgt):
        return _LateComm(blocks, dgt)


def _local_step(x, ctx, tgt, mod, modc, norm_g, w_pad, ln_g, ln_b, ws, bs, w2, gb, gb_norm, gf, xch):
    shift, scale, gate = mod[:, 0:D], mod[:, D:2 * D], mod[:, 2 * D:]
    shift_c, scale_c = modc[:, 0:D], modc[:, D:]
    w2f, w2b, gbf, gbb = _pad_gate(w2, gb)

    p, h, *got_proj = _in_proj(x, norm_g, scale, shift, w_pad, 512, xch.gather_proj())
    w_pa, w_pb, w_out = xch.proj_weights(got_proj)
    sc_f, sc_b = _ctx_fwd(ctx, norm_g, scale_c, shift_c, w_pad, w2f, w2b, gbf, gbb)
    o_f, st_f = _gla_fwd(p, w2f, gbf, sc_f, False, 512, "gla_fwd_f")
    o_b, st_b = _gla_fwd(p, w2b, gbb, sc_b, True, 512, "gla_fwd_b")
    sv = _mix_fwd(p, ln_g, ln_b, ws.astype(bf16), bs.T)
    (dx1, dzbua, dzag, dsv, do, dw_out, dw_pa, dw_pb, dgf, dgate, dgbn, loss) = _mid(
        x, tgt, p, o_f, o_b, sv, gate, gf, gb_norm, w_pa, w_pb, w_out, 256)
    dva, dws, dbs_acc, dln_g, dln_b = _mix_bwd(p, dsv, ln_g, ln_b, jnp.swapaxes(ws, 1, 2).astype(bf16))
    dw_zbua, dw_zag, *got_first = _tn_matmul(h, [dzbua, dzag], 512, "dw_zbua_zag", xch.first(dw_out, dw_pa, dw_pb))
    (dw_va,) = _tn_matmul(h, [dva], 512, "dw_va")
    blocks_a = _pack_dw_early(dw_zbua, dw_va, dw_zag, 0, EARLY_A_ROWS, "pack_dw_early_a")
    blocks_b = _pack_dw_early(dw_zbua, dw_va, dw_zag, EARLY_A_ROWS, D - EARLY_A_ROWS, "pack_dw_early_b")
    small = _rows128(dln_g, dln_b, dws, jnp.sum(dbs_acc, axis=-1), dgbn, dgf, jnp.broadcast_to(loss, (1, 128)))

    dqkv_f, dlr_f, dw2f, dgbf, dsc_f, *got_a = _gla_bwd(p, do, st_f, w2f, gbf, None, False, 512, "gla_bwd_f",
                                                        xch.early_a(small, blocks_a))
    dqkv, dlr, dw2b, dgbb, dsc_b, *got_b = _gla_bwd(p, do, st_b, w2b, gbb, (dqkv_f, dlr_f), True, 512, "gla_bwd_b",
                                                    xch.early_b(blocks_b))
    dwk_c, dwv_c, dwl_c, dmodc, dg_c, dw2c, dgbc = _ctx_bwd(ctx, norm_g, scale_c, shift_c, w_pad, w2f, w2b, gbf, gbb, dsc_f, dsc_b)
    dw_qkv, dw_lr = _tn_matmul(h, [dqkv, dlr], 512, "dw_qkv_lr")
    blocks_late = _pack_dw_late(dw_qkv, dw_lr, dwk_c, dwv_c, dwl_c)
    dw2 = jnp.stack([dw2f[0:RANK] + dw2c[0, 0:RANK], dw2b[RANK:2 * RANK] + dw2c[1, RANK:2 * RANK]])
    dgb = jnp.concatenate([dgbf + dgbc[0], dgbb + dgbc[1]], axis=0)
    dgt = jnp.concatenate([jnp.transpose(dw2.reshape(2, RANK, NDEV, 32), (2, 0, 1, 3)).reshape(NDEV, 2 * RANK * 32),
                           jnp.transpose(dgb.reshape(2, NDEV, 32), (1, 0, 2)).reshape(NDEV, 64),
                           jnp.zeros((NDEV, 64), f32)], axis=1).reshape(NDEV, 9, 128)
    gx, dshift, dscale, dg, *got_late = _in_bwd(x, dx1, dqkv, dzbua, dva, dzag, dlr, w_pad, norm_g, scale, 512,
                                                xch.late(blocks_late, dgt))
    return dict(loss=loss, gx=gx, dmod=jnp.concatenate([dshift, dscale, dgate], axis=1), dmodc=dmodc, dnorm_g=dg + dg_c,
                small=small, blocks_a=blocks_a, blocks_b=blocks_b, blocks_late=blocks_late, dw2=dw2, dgb=dgb,
                dw_pa=dw_pa, dw_pb=dw_pb, dw_out=dw_out, got_first=got_first, got_a=got_a, got_b=got_b, got_late=got_late)


def _rows128(*vs):
    out = []
    for t in vs:
        t = t.reshape(-1)
        pad = (-t.shape[0]) % 128
        out.append(jnp.pad(t, (0, pad)) if pad else t)
    return jnp.concatenate(out).reshape(-1, 128)


def kernel(x, c, ctx, c_ctx, w_mod, b_mod, norm_g, w_in, a_ln_g, a_ln_b, a_ws, a_bs, b_gate_w2, b_gate_b, b_norm_g, w_proj_a, w_proj_b, w_out, final_norm_g, loss_target, m_c_ctx, m_w_mod, m_b_mod, m_norm_g, m_w_in, m_a_ln_g, m_a_ln_b, m_a_ws, m_a_bs, m_b_gate_w2, m_b_gate_b, m_b_norm_g, m_w_proj_a, m_w_proj_b, m_w_out, m_final_norm_g, v_c_ctx, v_w_mod, v_b_mod, v_norm_g, v_w_in, v_a_ln_g, v_a_ln_b, v_a_ws, v_a_bs, v_b_gate_w2, v_b_gate_b, v_b_norm_g, v_w_proj_a, v_w_proj_b, v_w_out, v_final_norm_g):
    me = _me()
    ncol = w_mod.shape[2]

    gate_mine = _rows128(jnp.concatenate([b_gate_w2.reshape(-1), b_gate_b.reshape(-1)]))
    early, wg, gates = _gather_first(_rows128(c), w_in[0].astype(bf16), gate_mine)
    cs = jnp.concatenate([early.reshape(NDEV, D), c_ctx.reshape(1, D), jnp.zeros((7, D), f32)], axis=0)
    w_pad = _repack_w(wg)
    gflat = gates.reshape(NDEV, 9 * 128)
    w2 = jnp.transpose(gflat[:, 0:2 * RANK * 32].reshape(NDEV, 2, RANK, 32), (1, 2, 0, 3)).reshape(2, RANK, KW)
    gb = jnp.transpose(gflat[:, 2 * RANK * 32:2 * RANK * 32 + 64].reshape(NDEV, 2, 32), (1, 0, 2)).reshape(2, KW)

    bm_mine = lax.dynamic_slice(b_mod, (0, me * ncol), (1, ncol))
    mods = _all_gather(_mod_fwd(cs, w_mod[0], bm_mine), "gather_mod")
    mods = jnp.transpose(mods, (1, 0, 2)).reshape(16, 3 * D)
    mod = lax.dynamic_slice(mods, (me, 0), (1, 3 * D))
    modc = mods[8:9, 0:2 * D]

    xch = _Exchanges(w_proj_a[0].astype(bf16), w_proj_b[0].astype(bf16), w_out[0].astype(bf16))
    r = _local_step(x[0], ctx[0], loss_target[0], mod, modc, norm_g, w_pad, a_ln_g, a_ln_b, a_ws[0], a_bs[0], w2, gb,
                    b_norm_g, final_norm_g.reshape(1, D), xch)
    p_out, p_pa, p_pb = r["got_first"]
    smalls_e, p_in_a = r["got_a"]
    (p_in_b,) = r["got_b"]
    _, _, p_in_late, p_gt = r["got_late"]

    n_e = (AW + AW + 4 * ACH * ACH + AW + VW + D) // 128
    row = lambda t: t.reshape(1, D)
    rep_e = _adam_params(smalls_e, [a_ln_g, a_ln_b, a_ws, a_bs, b_norm_g, row(final_norm_g)],
                         [m_a_ln_g, m_a_ln_b, m_a_ws, m_a_bs, m_b_norm_g, row(m_final_norm_g)],
                         [v_a_ln_g, v_a_ln_b, v_a_ws, v_a_bs, v_b_norm_g, row(v_final_norm_g)], "adam_rep_early")
    rep_e = [t[0:5] + (t[5].reshape(D),) for t in rep_e]
    losses = smalls_e[:, n_e, 0]
    loss = losses[0]
    for i in range(1, NDEV):
        loss = loss + losses[i]

    dbm = r["dmod"] + jnp.concatenate([r["dmodc"], jnp.zeros((1, D), f32)], axis=1)
    smalls_l = _all_gather(_rows128(r["dnorm_g"], dbm, r["dmod"], r["dmodc"]), "gather_small")
    n_l = (D + 3 * D) // 128
    rep_l = _adam_params(smalls_l, [norm_g, b_mod], [m_norm_g, m_b_mod], [v_norm_g, v_b_mod], "adam_rep_late")
    tail = smalls_l[:, n_l:].reshape(NDEV, -1)
    dmods = tail[:, 0:3 * D]
    dmodc_all = tail[:, 3 * D:5 * D]
    dmodc_tot = dmodc_all[0:1]
    for i in range(1, NDEV):
        dmodc_tot = dmodc_tot + dmodc_all[i:i + 1]

    dm_rows = jnp.concatenate([dmods, jnp.concatenate([dmodc_tot, jnp.zeros((1, D), f32)], axis=1), jnp.zeros((7, 3 * D), f32)], axis=0)
    dm_mine = lax.dynamic_slice(dm_rows, (0, me * ncol), (16, ncol))
    g_wmod, gc_part = _mod_bwd(cs, dm_mine, w_mod[0])
    wm = _adam(g_wmod[None], w_mod, m_w_mod, v_w_mod, 256, "adam_wmod")
    gcs = _all_gather(gc_part.reshape(8, 128), "gather_cctx")
    cc = _adam(gcs, c_ctx.reshape(8, 128), m_c_ctx.reshape(8, 128), v_c_ctx.reshape(8, 128), 8, "adam_cctx")

    a_in = _adam_w_in(p_in_a, p_in_b, p_in_late, w_in, m_w_in, v_w_in)
    a_pa = _adam(p_pa, w_proj_a, m_w_proj_a, v_w_proj_a, AW, "adam_w_pa")
    a_pb = _adam(p_pb, w_proj_b, m_w_proj_b, v_w_proj_b, VW, "adam_w_pb")
    a_out = _adam(p_out, w_out, m_w_out, v_w_out, 128, "adam_w_out")
    gate_m = _rows128(jnp.concatenate([m_b_gate_w2.reshape(-1), m_b_gate_b.reshape(-1)]))
    gate_v = _rows128(jnp.concatenate([v_b_gate_w2.reshape(-1), v_b_gate_b.reshape(-1)]))
    a_gt = [t.reshape(-1) for t in _adam(p_gt, gate_mine, gate_m, gate_v, 9, "adam_gate")]
    nw2 = 2 * RANK * 32
    sh = [(a_in[k], a_pa[k], a_pb[k], a_out[k], a_gt[k][0:nw2].reshape(1, 2, RANK, 32),
           a_gt[k][nw2:nw2 + 64].reshape(1, 2, 32)) for k in range(4)]

    outs = [loss, r["gx"][None]]
    for k in range(4):
        lg, lb, aws, abs_, bng, fng = rep_e[k]
        n_g, bmod = rep_l[k]
        s_in, s_pa, s_pb, s_out, s_w2, s_gb = sh[k]
        outs += [cc[k].reshape(D), wm[k], bmod, n_g, s_in, lg, lb, aws, abs_, s_w2, s_gb, bng, s_pa, s_pb, s_out, fng]
    return tuple(outs)
```

```python
import functools

import jax
import jax.numpy as jnp
from jax import lax
from jax.experimental import pallas as pl
from jax.experimental.pallas import tpu as pltpu

f32, bf16 = jnp.float32, jnp.bfloat16

D = 1024
CTX = 256
EPS = 1e-6
AW = 512
ACH = 128
GW = 64
KW = 256
VW = 512
NH = 4
HK = 64
HV = 128
RANK = 16
TAU = 16.0
CH = 64
QSCALE = HK ** -0.5
INW = 5152
NDEV = 8

PQ, PK, PV, PZB, PUA, PVA, PZA, PG1, PG2, PLR, PW = 0, 256, 512, 1024, 1536, 2048, 2560, 3072, 4096, 5120, 5248
LRW = 128

ADAM_LR, ADAM_B1, ADAM_B2, ADAM_EPS, ADAM_WD, ADAM_STEP = 0.001, 0.9, 0.999, 1e-08, 0.01, 10

VMEM_LIMIT = 56 * 1024 * 1024
MESH = pl.DeviceIdType.MESH


def _cp(sem=None):
    return pltpu.CompilerParams(dimension_semantics=sem, vmem_limit_bytes=VMEM_LIMIT)


def _dot(a, b):
    return jnp.dot(a.astype(bf16), b.astype(bf16), preferred_element_type=f32)


def _nt(a, b):
    return lax.dot_general(a.astype(bf16), b.astype(bf16), (((1,), (1,)), ((), ())), preferred_element_type=f32)


def _tn(a, b):
    return lax.dot_general(a.astype(bf16), b.astype(bf16), (((0,), (0,)), ((), ())), preferred_element_type=f32)


def _dot_hi(a, b):
    return jnp.dot(a, b, preferred_element_type=f32, precision=lax.Precision.HIGHEST)


def _sigmoid(x):
    return 1.0 / (1.0 + jnp.exp(-x))


def _log_sigmoid(x):
    return jnp.minimum(x, 0.0) - jnp.log(1.0 + jnp.exp(-jnp.abs(x)))


def _silu_and_grad(z):
    s = _sigmoid(z)
    return z * s, s * (1.0 + z * (1.0 - s))


def _me():
    return 4 * lax.axis_index("x") + 2 * lax.axis_index("y") + lax.axis_index("c")


def _peer(k):
    x, y, c = lax.axis_index("x"), lax.axis_index("y"), lax.axis_index("c")
    px = 1 - x if k & 4 else x
    py = 1 - y if k & 2 else y
    pc = 1 - c if k & 1 else c
    return (px, py, pc), 4 * px + 2 * py + pc


def _all_gather(v, name):
    r, c = v.shape

    def body(v_ref, out_ref, send_sems, recv_sems, local_sem):
        me = _me()
        mine = pltpu.make_async_copy(v_ref, out_ref.at[me], local_sem)
        mine.start()
        sends, recvs = [], []
        for k in range(1, NDEV):
            dev, idx = _peer(k)
            sends.append(pltpu.make_async_remote_copy(
                src_ref=v_ref, dst_ref=out_ref.at[me], send_sem=send_sems.at[k - 1], recv_sem=recv_sems.at[k - 1],
                device_id=dev, device_id_type=MESH))
            recvs.append(pltpu.make_async_remote_copy(
                src_ref=v_ref, dst_ref=out_ref.at[idx], send_sem=send_sems.at[k - 1], recv_sem=recv_sems.at[k - 1],
                device_id=dev, device_id_type=MESH))
        for cp in sends:
            cp.start()
        for cp in recvs:
            cp.wait_recv()
        for cp in sends:
            cp.wait_send()
        mine.wait()

    return pl.pallas_call(
        body, name=name, out_shape=jax.ShapeDtypeStruct((NDEV, r, c), v.dtype),
        in_specs=[pl.BlockSpec(memory_space=pl.ANY)], out_specs=pl.BlockSpec(memory_space=pl.ANY),
        scratch_shapes=[pltpu.SemaphoreType.DMA((NDEV - 1,)), pltpu.SemaphoreType.DMA((NDEV - 1,)), pltpu.SemaphoreType.DMA(())],
    )(v)


def _fanout(srcs, dsts, send_sems, recv_sems, local_sems, owners=None):
    me = _me()
    n = len(srcs)
    owns = lambda a, j: True if owners is None or owners[a] is None else owners[a](j)

    def guarded(cond, fn):
        if cond is True:
            fn()
        else:
            pl.when(cond)(fn)

    def copies(with_recvs):
        local = [pltpu.make_async_copy(srcs[a](me), dsts[a](me), local_sems.at[a]) for a in range(n)]
        sends, recvs = [], []
        for k in range(1, NDEV):
            dev, idx = _peer(k)
            for a in range(n):
                s = (k - 1) * n + a
                sends.append((owns(a, idx), pltpu.make_async_remote_copy(
                    src_ref=srcs[a](idx), dst_ref=dsts[a](me), send_sem=send_sems.at[s], recv_sem=recv_sems.at[s],
                    device_id=dev, device_id_type=MESH)))
                if with_recvs:
                    recvs.append((owns(a, me), pltpu.make_async_remote_copy(
                        src_ref=srcs[a](idx), dst_ref=dsts[a](idx), send_sem=send_sems.at[s], recv_sem=recv_sems.at[s],
                        device_id=dev, device_id_type=MESH)))
        return local, sends, recvs

    def start():
        local, sends, _ = copies(False)
        for a, cp in enumerate(local):
            guarded(owns(a, me), cp.start)
        for cond, cp in sends:
            guarded(cond, cp.start)

    def finish():
        local, sends, recvs = copies(True)
        for cond, cp in recvs:
            guarded(cond, cp.wait_recv)
        for cond, cp in sends:
            guarded(cond, cp.wait_send)
        for a, cp in enumerate(local):
            guarded(owns(a, me), cp.wait)

    return start, finish


class _Comm:
    def __init__(self, ins, outs, plan):
        self.ins, self.outs, self.plan = list(ins), list(outs), plan
        self.n = len(self.outs)

    def specs(self):
        hbm = pl.BlockSpec(memory_space=pl.ANY)
        return [hbm] * len(self.ins), [hbm] * self.n, _fanout_sems(self.n) if self.n else []

    def run(self, in_refs, out_refs, sems, nsteps, compute):
        if not self.n:
            compute()
            return

        def hooks():
            srcs, dsts, owners = self.plan(in_refs, out_refs)
            return _fanout(srcs, dsts, sems[0], sems[1], sems[2], owners)

        pl.when(pl.program_id(0) == 0)(lambda: hooks()[0]())
        compute()
        pl.when(pl.program_id(0) == nsteps - 1)(lambda: hooks()[1]())


LATE_MID_STEP = 2
LATE_ADD_ROWS = 256


class _LateComm:
    def __init__(self, blocks, dgt):
        sds = jax.ShapeDtypeStruct
        self.ins = [blocks, dgt]
        self.outs = [sds((D, SHARD), bf16), sds((D, SHARD), bf16), sds((4, D, SHARD), bf16), sds(dgt.shape, f32)]
        self.n = len(self.outs)

    def specs(self):
        hbm = pl.BlockSpec(memory_space=pl.ANY)
        scratch = [pltpu.VMEM((3, LATE_ADD_ROWS, SHARD), bf16), pltpu.SemaphoreType.DMA((2,)), pltpu.SemaphoreType.DMA((4,)),
                   pltpu.SemaphoreType.DMA(())] + _fanout_sems(1)
        return [hbm] * 2, [hbm] * self.n, scratch

    def run(self, in_refs, out_refs, scratch, nsteps, compute):
        late_ref, dgt_ref = in_refs
        sib_ref, pair_ref, parts_ref, gt_ref = out_refs
        vbuf, send_sems, recv_sems, local_sem, g_send, g_recv, g_local = scratch
        x, y, c = lax.axis_index("x"), lax.axis_index("y"), lax.axis_index("c")
        chip = 2 * x + y
        is_owner_chip = chip == 0
        step = pl.program_id(0)

        def to_sibling():
            return pltpu.make_async_remote_copy(src_ref=late_ref.at[1 - c], dst_ref=sib_ref, send_sem=send_sems.at[0],
                                                recv_sem=recv_sems.at[0], device_id=(x, y, 1 - c), device_id_type=MESH)

        def to_owner(k):
            return pltpu.make_async_remote_copy(src_ref=pair_ref, dst_ref=parts_ref.at[k], send_sem=send_sems.at[1],
                                                recv_sem=recv_sems.at[k], device_id=(0, 0, c), device_id_type=MESH)

        def own_copy():
            return pltpu.make_async_copy(pair_ref, parts_ref.at[0], local_sem)

        def gates():
            return _fanout([lambda j: dgt_ref.at[j]], [lambda j: gt_ref.at[j]], g_send, g_recv, g_local)

        @pl.when(step == 0)
        def _():
            to_sibling().start()
            gates()[0]()

        compute()

        @pl.when(step == LATE_MID_STEP)
        def _():
            to_sibling().wait_recv()
            for r0 in range(0, D, LATE_ADD_ROWS):
                rows = pl.ds(r0, LATE_ADD_ROWS)
                pltpu.sync_copy(late_ref.at[c, rows], vbuf.at[0])
                pltpu.sync_copy(sib_ref.at[rows], vbuf.at[1])
                vbuf[2] = (vbuf[0].astype(f32) + vbuf[1].astype(f32)).astype(bf16)
                pltpu.sync_copy(vbuf.at[2], pair_ref.at[rows])
            pl.when(is_owner_chip)(lambda: own_copy().start())
            pl.when(jnp.logical_not(is_owner_chip))(lambda: to_owner(chip).start())

        @pl.when(step == nsteps - 1)
        def _():
            @pl.when(is_owner_chip)
            def _():
                for k in range(1, 4):
                    to_owner(k).wait_recv()
                own_copy().wait()

            pl.when(jnp.logical_not(is_owner_chip))(lambda: to_owner(chip).wait_send())
            to_sibling().wait_send()
            gates()[1]()


def _fanout_sems(n):
    return [pltpu.SemaphoreType.DMA(((NDEV - 1) * n,)), pltpu.SemaphoreType.DMA(((NDEV - 1) * n,)), pltpu.SemaphoreType.DMA((n,))]


def _lanes(j):
    return pl.ds(pl.multiple_of(j * 128, 128), 128)


def _gather_first(c_row, cctx_row, wm, bm, wi, gate):
    def body(c_ref, cctx_ref, wm_ref, bm_ref, wi_ref, g_ref, cs_ref, mods_ref, owi, og, call_ref, mine_ref,
             send_sems, recv_sems, local_sems, c_send, c_recv, c_local, m_send, m_recv, m_local):
        x, y, c = lax.axis_index("x"), lax.axis_index("y"), lax.axis_index("c")
        sibling = (x, y, 1 - c)
        chips = [(1 - x, y), (x, 1 - y), (1 - x, 1 - y)]
        index = lambda px, py, pc: 4 * px + 2 * py + pc
        arrays = ((wi_ref, owi), (g_ref, og))
        n = len(arrays)

        def copy(a, k, block, to, own=False):
            src, out = arrays[a]
            return pltpu.make_async_remote_copy(
                src_ref=src if own else out.at[index(*block)], dst_ref=out.at[index(*block)],
                send_sem=send_sems.at[k * n + a], recv_sem=recv_sems.at[k * n + a], device_id=to, device_id_type=MESH)

        mine = [pltpu.make_async_copy(src, out.at[index(x, y, c)], local_sems.at[a]) for a, (src, out) in enumerate(arrays)]
        first = [copy(a, 0, (x, y, c), sibling, own=True) for a in range(n)]
        first += [copy(a, 1 + j, (x, y, c), (*chip, c), own=True) for j, chip in enumerate(chips) for a in range(n)]
        for cp in mine + first:
            cp.start()

        c_start, c_finish = _fanout([lambda j: c_ref], [lambda j: call_ref.at[j]], c_send, c_recv, c_local)
        c_start()
        c_finish()
        cs = jnp.concatenate([call_ref[j] for j in range(NDEV)] + [cctx_ref[...], jnp.zeros((16 - NDEV - 1, D), f32)], axis=0)
        cs_ref[...] = cs
        s, _ = _silu_and_grad(cs)
        mine_ref[...] = _dot_hi(s, wm_ref[...]) + bm_ref[...]
        m_start, m_finish = _fanout([lambda j: mine_ref], [lambda j: mods_ref.at[j]], m_send, m_recv, m_local)
        m_start()

        passed = []
        for j, chip in enumerate(chips):
            for a in range(n):
                copy(a, 1 + j, (*chip, c), (x, y, c)).wait_recv()
            for a in range(n):
                cp = copy(a, 4 + j, (*chip, c), sibling)
                cp.start()
                passed.append(cp)
        for a in range(n):
            copy(a, 0, sibling, (x, y, c)).wait_recv()
        for j, chip in enumerate(chips):
            for a in range(n):
                copy(a, 4 + j, (*chip, 1 - c), (x, y, c)).wait_recv()
        for cp in first + passed:
            cp.wait_send()
        for cp in mine:
            cp.wait()
        m_finish()

    hbm = pl.BlockSpec(memory_space=pl.ANY)
    vm = pl.BlockSpec(memory_space=pltpu.VMEM)
    ncol = wm.shape[1]
    return pl.pallas_call(
        body, name="gather_first",
        out_shape=(jax.ShapeDtypeStruct((16, D), f32), jax.ShapeDtypeStruct((NDEV, 16, ncol), f32),
                   jax.ShapeDtypeStruct((NDEV,) + wi.shape, bf16), jax.ShapeDtypeStruct((NDEV,) + gate.shape, f32)),
        in_specs=[vm, vm, vm, vm, hbm, hbm], out_specs=(vm, vm, hbm, hbm),
        scratch_shapes=[pltpu.VMEM((NDEV, 1, D), f32), pltpu.VMEM((16, ncol), f32)] + _fanout_sems(2) + _fanout_sems(1) + _fanout_sems(1),
        compiler_params=_cp(),
    )(c_row, cctx_row, wm, bm, wi, gate)


SHARD = INW // NDEV
ROWS_RP = 128


def _overlap(lo, hi, a, b):
    s, e = max(lo, a), min(hi, b)
    return (s, e) if s < e else None


def _repack_w(wg):
    segs = ((0, 1024, PQ), (1024, 1024 + 2 * RANK, PLR), (1024 + 2 * RANK, INW, PZB))

    def body(g_ref, o_ref):
        for j in range(NDEV):
            lo, hi = j * SHARD, (j + 1) * SHARD
            for a, b, pad0 in segs:
                ov = _overlap(lo, hi, a, b)
                if ov:
                    s, e = ov
                    o_ref[:, pad0 + s - a:pad0 + e - a] = g_ref[j, :, s - lo:e - lo]
        o_ref[:, PLR + 2 * RANK:PW] = jnp.zeros((ROWS_RP, PW - PLR - 2 * RANK), bf16)

    return pl.pallas_call(
        body, name="repack_w", grid=(D // ROWS_RP,), out_shape=jax.ShapeDtypeStruct((D, PW), bf16),
        in_specs=[pl.BlockSpec((NDEV, ROWS_RP, SHARD), lambda i: (0, i, 0))],
        out_specs=pl.BlockSpec((ROWS_RP, PW), lambda i: (i, 0)), compiler_params=_cp(("arbitrary",)),
    )(wg)


LATE_END = 1024 + 2 * RANK
LATE_DESTS = 2


def _pack_blocks(o_ref, srcs, dests, dtype):
    for n, j in enumerate(dests):
        lo, hi = j * SHARD, (j + 1) * SHARD
        done = lo
        for a, b, src in srcs:
            ov = _overlap(lo, hi, a, b)
            if ov:
                s, e = ov
                if s > done:
                    o_ref[n, :, done - lo:s - lo] = jnp.zeros((ROWS_RP, s - done), dtype)
                o_ref[n, :, s - lo:e - lo] = src[:, s - a:e - a].astype(dtype)
                done = e
        if done < hi:
            o_ref[n, :, done - lo:hi - lo] = jnp.zeros((ROWS_RP, hi - done), dtype)


def _pack_dw_early(dw_zbua, dw_va, dw_zag, row0, nrows, name):
    def body(zbua_ref, va_ref, zag_ref, o_ref):
        _pack_blocks(o_ref, ((LATE_END, 2080, zbua_ref), (2080, 2592, va_ref), (2592, INW, zag_ref)), range(NDEV), bf16)

    row = lambda w: pl.BlockSpec((ROWS_RP, w), lambda i: (i + row0 // ROWS_RP, 0))
    return pl.pallas_call(
        body, name=name, grid=(nrows // ROWS_RP,), out_shape=jax.ShapeDtypeStruct((NDEV, nrows, SHARD), bf16),
        in_specs=[row(2 * AW), row(AW), row(AW + 2 * D)],
        out_specs=pl.BlockSpec((NDEV, ROWS_RP, SHARD), lambda i: (0, i, 0)), compiler_params=_cp(("arbitrary",)),
    )(dw_zbua, dw_va, dw_zag)


def _pack_dw_late(dw_qkv, dw_lr, dwk_c, dwv_c, dwl_c):
    def body(qkv_ref, lr_ref, kc_ref, vc_ref, lc_ref, o_ref):
        qkv = qkv_ref[...] + jnp.concatenate([jnp.zeros((ROWS_RP, KW), f32), kc_ref[...], vc_ref[...]], axis=1)
        lr = lr_ref[...] + lc_ref[...]
        _pack_blocks(o_ref, ((0, 1024, qkv), (1024, LATE_END, lr)), range(LATE_DESTS), bf16)

    row = lambda w: pl.BlockSpec((ROWS_RP, w), lambda i: (i, 0))
    return pl.pallas_call(
        body, name="pack_dw_late", grid=(D // ROWS_RP,), out_shape=jax.ShapeDtypeStruct((LATE_DESTS, D, SHARD), bf16),
        in_specs=[row(2 * KW + VW), row(LRW), row(KW), row(VW), row(LRW)],
        out_specs=pl.BlockSpec((LATE_DESTS, ROWS_RP, SHARD), lambda i: (0, i, 0)), compiler_params=_cp(("arbitrary",)),
    )(dw_qkv, dw_lr, dwk_c, dwv_c, dwl_c)


def _mod_bwd(cs, dm, wm):
    def body(cs_ref, dm_ref, wm_ref, gw_ref, gcs_ref, gc_ref, send_sems, recv_sems, local_sems):
        s, ds = _silu_and_grad(cs_ref[...])
        part = lax.dot_general(dm_ref[8:9, :], wm_ref[...], (((1,), (1,)), ((), ())), preferred_element_type=f32,
                               precision=lax.Precision.HIGHEST)
        gc_ref[...] = part * ds[8:9, :]
        start, finish = _fanout([lambda j: gc_ref], [lambda j: gcs_ref.at[j]], send_sems, recv_sems, local_sems)
        start()
        gw_ref[...] = lax.dot_general(s, dm_ref[...], (((0,), (0,)), ((), ())), preferred_element_type=f32,
                                      precision=lax.Precision.HIGHEST)
        finish()

    return pl.pallas_call(body, name="mod_bwd",
                          out_shape=(jax.ShapeDtypeStruct(wm.shape, f32), jax.ShapeDtypeStruct((NDEV, 1, D), f32)),
                          scratch_shapes=[pltpu.VMEM((1, D), f32)] + _fanout_sems(1),
                          compiler_params=_cp())(cs, dm, wm)


def _adam_update(g, w_ref, m_ref, v_ref, go_ref, d_ref, mo_ref, vo_ref):
    c1 = 1.0 / (1.0 - ADAM_B1 ** ADAM_STEP)
    c2 = 1.0 / (1.0 - ADAM_B2 ** ADAM_STEP)
    mn = ADAM_B1 * m_ref[...] + (1.0 - ADAM_B1) * g
    vn = ADAM_B2 * v_ref[...] + (1.0 - ADAM_B2) * (g * g)
    go_ref[...] = g
    mo_ref[...] = mn
    vo_ref[...] = vn
    d_ref[...] = -ADAM_LR * ((mn * c1) / (jnp.sqrt(vn * c2) + ADAM_EPS) + ADAM_WD * w_ref[...])


def _adam_w_in(parts_a, parts_b, parts_late, w, m, v):
    na = EARLY_A_ROWS // ROWS_RP

    def body(a_ref, b_ref, l_ref, w_ref, m_ref, v_ref, go_ref, d_ref, mo_ref, vo_ref):
        me = _me()
        first = pl.program_id(0) < na
        early = jnp.where(first, a_ref[0], b_ref[0]).astype(f32)
        for i in range(1, NDEV):
            early = early + jnp.where(first, a_ref[i], b_ref[i]).astype(f32)
        late = l_ref[0].astype(f32)
        for i in range(1, 4):
            late = late + l_ref[i].astype(f32)
        g = jnp.where(me >= LATE_DESTS - 1, early, 0.0) + jnp.where(me < LATE_DESTS, late, 0.0)
        _adam_update(g, w_ref, m_ref, v_ref, go_ref, d_ref, mo_ref, vo_ref)

    blk = pl.BlockSpec((None, ROWS_RP, SHARD), lambda i: (0, i, 0))
    return pl.pallas_call(
        body, name="adam_w_in", grid=(D // ROWS_RP,), out_shape=tuple(jax.ShapeDtypeStruct((1, D, SHARD), f32) for _ in range(4)),
        in_specs=[pl.BlockSpec((NDEV, ROWS_RP, SHARD), lambda i: (0, jnp.minimum(i, na - 1), 0)),
                  pl.BlockSpec((NDEV, ROWS_RP, SHARD), lambda i: (0, jnp.maximum(i - na, 0), 0)),
                  pl.BlockSpec((4, ROWS_RP, SHARD), lambda i: (0, i, 0)), blk, blk, blk],
        out_specs=(blk, blk, blk, blk), compiler_params=_cp(("arbitrary",)),
    )(parts_a, parts_b, parts_late, w, m, v)


def _adam_params(parts, ws, ms, vs, name):
    p = parts.shape[0]
    k = len(ws)
    nrows = [w.size // 128 for w in ws]
    starts = [sum(nrows[:i]) for i in range(k)]

    def shaped(g, shape):
        if len(shape) == 2:
            return jnp.concatenate([g[r:r + 1] for r in range(g.shape[0])], axis=1)
        return g.reshape(shape)

    def body(*refs):
        g_ref = refs[0]
        w_refs, m_refs, v_refs = refs[1:1 + k], refs[1 + k:1 + 2 * k], refs[1 + 2 * k:1 + 3 * k]
        outs = refs[1 + 3 * k:]
        for i in range(k):
            rows = slice(starts[i], starts[i] + nrows[i])
            g = g_ref[0, rows, :]
            for j in range(1, p):
                g = g + g_ref[j, rows, :]
            _adam_update(shaped(g, ws[i].shape), w_refs[i], m_refs[i], v_refs[i], outs[i], outs[k + i], outs[2 * k + i], outs[3 * k + i])

    vm = pl.BlockSpec(memory_space=pltpu.VMEM)
    res = pl.pallas_call(
        body, name=name, out_shape=tuple(jax.ShapeDtypeStruct(w.shape, f32) for _ in range(4) for w in ws),
        in_specs=[vm] * (1 + 3 * k), out_specs=(vm,) * (4 * k), compiler_params=_cp(),
    )(parts, *ws, *ms, *vs)
    return [res[i * k:(i + 1) * k] for i in range(4)]


def _adam(parts, w, m, v, rows, name):
    p, r, c = parts.shape
    lead = w.ndim - 2

    def body(g_ref, w_ref, m_ref, v_ref, go_ref, d_ref, mo_ref, vo_ref):
        g = g_ref[0].astype(f32)
        for i in range(1, p):
            g = g + g_ref[i].astype(f32)
        _adam_update(g, w_ref, m_ref, v_ref, go_ref, d_ref, mo_ref, vo_ref)

    blk = pl.BlockSpec((None,) * lead + (rows, c), lambda i: (0,) * lead + (i, 0))
    return pl.pallas_call(
        body, name=name, grid=(r // rows,), out_shape=tuple(jax.ShapeDtypeStruct(w.shape, f32) for _ in range(4)),
        in_specs=[pl.BlockSpec((p, rows, c), lambda i: (0, i, 0)), blk, blk, blk], out_specs=(blk, blk, blk, blk),
        compiler_params=_cp(("arbitrary",)),
    )(parts, w, m, v)


def _in_proj(x, g, scale, shift, w_pad, tl, comm):
    l = x.shape[0]
    c_in, c_out, c_sems = comm.specs()

    def body(*refs):
        x_ref, g_ref, sc_ref, sh_ref, w_ref = refs[:5]
        cin = refs[5:5 + len(c_in)]
        p_ref, h_ref = refs[5 + len(c_in):7 + len(c_in)]
        cout = refs[7 + len(c_in):7 + len(c_in) + comm.n]
        sems = refs[7 + len(c_in) + comm.n:]

        def compute():
            xv = x_ref[...]
            r = lax.rsqrt(jnp.mean(xv * xv, axis=-1, keepdims=True) + EPS)
            h = (xv * r) * (g_ref[...] * (1.0 + sc_ref[...])) + sh_ref[...]
            hb = h.astype(bf16)
            h_ref[...] = hb
            p_ref[...] = jnp.dot(hb, w_ref[...], preferred_element_type=f32).astype(bf16)

        comm.run(cin, cout, sems, l // tl, compute)

    vec = pl.BlockSpec((1, D), lambda i: (0, 0))
    return pl.pallas_call(
        body, name="in_proj", grid=(l // tl,),
        out_shape=(jax.ShapeDtypeStruct((l, PW), bf16), jax.ShapeDtypeStruct((l, D), bf16)) + tuple(comm.outs),
        in_specs=[pl.BlockSpec((tl, D), lambda i: (i, 0)), vec, vec, vec, pl.BlockSpec((D, PW), lambda i: (0, 0))] + c_in,
        out_specs=(pl.BlockSpec((tl, PW), lambda i: (i, 0)), pl.BlockSpec((tl, D), lambda i: (i, 0))) + tuple(c_out),
        scratch_shapes=c_sems, compiler_params=_cp(("arbitrary",)),
    )(x, g, scale, shift, w_pad, *comm.ins)


def _tri(rev):
    i = lax.broadcasted_iota(jnp.int32, (CH, CH), 0)
    j = lax.broadcasted_iota(jnp.int32, (CH, CH), 1)
    return jnp.where((j >= i) if rev else (j <= i), 1.0, 0.0).astype(f32)


def _head_masks():
    lane = lax.broadcasted_iota(jnp.int32, (1, KW), 1) // HK
    return [jnp.where(lane == h, 1.0, 0.0).astype(f32) for h in range(NH)]


def _block_diag():
    r = lax.broadcasted_iota(jnp.int32, (VW, KW), 0) // HV
    c = lax.broadcasted_iota(jnp.int32, (VW, KW), 1) // HK
    return jnp.where(r == c, 1.0, 0.0).astype(f32)


def _decay(lr, w2, gb, tri, rev):
    logits = _dot(lr, w2) + gb
    a = _log_sigmoid(logits) * (1.0 / TAU)
    c = _dot_hi(tri, a)
    cl = c[0:1, :] if rev else c[CH - 1:CH, :]
    return logits, c, cl


def _stack_heads(t, hm):
    return jnp.concatenate([t * hm[h] for h in range(NH)], axis=0)


def _chunk_fwd(q, k, v, c, cl, st, tri, hm, bd):
    tri4 = jnp.concatenate([tri] * NH, axis=0)
    qd = q * jnp.exp(c) * QSCALE
    kd = k * jnp.exp(-c)
    kdec = k * jnp.exp(cl - c)
    pst = _nt(_stack_heads(qd, hm), kd) * tri4
    intra = jnp.concatenate([_dot(pst[h * CH:(h + 1) * CH], v[:, h * HV:(h + 1) * HV]) for h in range(NH)], axis=1)
    o = _nt(qd, st) + intra
    st_new = st * jnp.exp(cl) + bd * _tn(v, kdec)
    return o, st_new


def _state_fwd(k, v, c, cl, st, bd):
    return st * jnp.exp(cl) + bd * _tn(v, k * jnp.exp(cl - c))


def _chunk_bwd(q, k, v, c, cl, st0, dst, do, tri, trit, hm, bd):
    ecl = jnp.exp(cl)
    edec = jnp.exp(cl - c)
    kdec = k * edec
    dv = _nt(kdec, dst)
    dkdec = _dot(v, dst)
    dcl = jnp.sum(dst * st0, axis=0, keepdims=True) * ecl + jnp.sum(dkdec * kdec, axis=0, keepdims=True)
    dc = -dkdec * kdec
    dk = dkdec * edec
    dst0 = dst * ecl
    dq = None
    if q is not None:
        tri4 = jnp.concatenate([tri] * NH, axis=0)
        ec, enc = jnp.exp(c), jnp.exp(-c)
        qd = q * ec * QSCALE
        kd = k * enc
        qs = _stack_heads(qd, hm)
        pst = _nt(qs, kd) * tri4
        dpst = jnp.concatenate([_nt(do[:, h * HV:(h + 1) * HV], v[:, h * HV:(h + 1) * HV]) for h in range(NH)], axis=0) * tri4
        dv = dv + jnp.concatenate([_tn(pst[h * CH:(h + 1) * CH], do[:, h * HV:(h + 1) * HV]) for h in range(NH)], axis=1)
        dqd = _dot(do, st0)
        for h in range(NH):
            dqd = dqd + hm[h] * _dot(dpst[h * CH:(h + 1) * CH], kd)
        dkd = _tn(dpst, qs)
        dst0 = dst0 + bd * _tn(do, qd)
        dc = dc + dqd * qd - dkd * kd
        dq = dqd * ec * QSCALE
        dk = dk + dkd * enc
    da = _dot_hi(trit, dc) + dcl
    return dq, dk, dv, da, dst0


def _bdot(a, b):
    return lax.dot_general(a.astype(bf16), b.astype(bf16), (((2,), (1,)), ((0,), (0,))), preferred_element_type=f32)


def _bnt(a, b):
    return lax.dot_general(a.astype(bf16), b.astype(bf16), (((2,), (2,)), ((0,), (0,))), preferred_element_type=f32)


def _btn(a, b):
    return lax.dot_general(a.astype(bf16), b.astype(bf16), (((1,), (1,)), ((0,), (0,))), preferred_element_type=f32)


def _scan_chunks(x, rev):
    nc = x.shape[0]
    hi = x.astype(bf16)
    r1 = x - hi.astype(f32)
    mid = r1.astype(bf16)
    lo = (r1 - mid.astype(f32)).astype(bf16)
    terms = jnp.concatenate([hi, mid, lo], axis=1)
    tri3 = jnp.broadcast_to(jnp.concatenate([_tri(rev)] * 3, axis=1).astype(bf16)[None], (nc, CH, 3 * CH))
    return lax.dot_general(tri3, terms, (((2,), (1,)), ((0,), (0,))), preferred_element_type=f32)


class _Tile:
    pass


def _tile_prep(q_ref, k_ref, lr_ref, w2_ref, gb_ref, rev, nc):
    t = _Tile()
    tg = nc * CH
    t.logits = _dot(lr_ref[...], w2_ref[...]) + gb_ref[...]
    c = _scan_chunks((_log_sigmoid(t.logits) * (1.0 / TAU)).reshape(nc, CH, KW), rev)
    cl = c[:, 0:1, :] if rev else c[:, CH - 1:CH, :]
    k = k_ref[...].astype(f32).reshape(nc, CH, KW)
    t.ec, t.enc, t.edec, t.ecl = jnp.exp(c), jnp.exp(-c), jnp.exp(cl - c), jnp.exp(cl)
    t.qd = q_ref[...].astype(f32).reshape(nc, CH, KW) * t.ec * QSCALE
    t.kd = k * t.enc
    t.kdec = k * t.edec
    hm = _head_masks()
    t.tri4 = jnp.concatenate([_tri(rev)] * NH, axis=0)[None]
    t.qs = jnp.concatenate([t.qd * hm[h] for h in range(NH)], axis=1)
    t.pst = _bnt(t.qs, t.kd) * t.tri4
    return t


def _gla_fwd(p, w2p, gb, s0, rev, tg, name):
    l = p.shape[0]
    nb, nc = l // tg, tg // CH

    def body(q_ref, k_ref, v_ref, lr_ref, w2_ref, gb_ref, s0_ref, o_ref, st_ref, st):
        @pl.when(pl.program_id(0) == 0)
        def _():
            st[...] = s0_ref[...]

        t = _tile_prep(q_ref, k_ref, lr_ref, w2_ref, gb_ref, rev, nc)
        v = v_ref[...].reshape(nc, CH, VW)
        intra = jnp.concatenate([_bdot(t.pst[:, h * CH:(h + 1) * CH], v[:, :, h * HV:(h + 1) * HV]) for h in range(NH)], axis=2)
        kv = _btn(v, t.kdec) * _block_diag()[None]
        s = st[...]
        for n in (range(nc - 1, -1, -1) if rev else range(nc)):
            st_ref[n] = s
            s = s * t.ecl[n] + kv[n]
        st[...] = s
        o_ref[...] = (_bnt(t.qd, st_ref[...]) + intra).reshape(tg, VW)

    blk = (lambda i: nb - 1 - i) if rev else (lambda i: i)
    return pl.pallas_call(
        body, name=name, grid=(nb,),
        out_shape=(jax.ShapeDtypeStruct((l, VW), f32), jax.ShapeDtypeStruct((l // CH, VW, KW), f32)),
        in_specs=[pl.BlockSpec((tg, KW), lambda i: (blk(i), PQ // KW)), pl.BlockSpec((tg, KW), lambda i: (blk(i), PK // KW)),
                  pl.BlockSpec((tg, VW), lambda i: (blk(i), PV // VW)), pl.BlockSpec((tg, LRW), lambda i: (blk(i), PLR // LRW)),
                  pl.BlockSpec((LRW, KW), lambda i: (0, 0)), pl.BlockSpec((1, KW), lambda i: (0, 0)),
                  pl.BlockSpec((VW, KW), lambda i: (0, 0))],
        out_specs=(pl.BlockSpec((tg, VW), lambda i: (blk(i), 0)), pl.BlockSpec((nc, VW, KW), lambda i: (blk(i), 0, 0))),
        scratch_shapes=[pltpu.VMEM((VW, KW), f32)],
        compiler_params=_cp(("arbitrary",)),
    )(p, p, p, p, w2p, gb, s0)


def _gla_bwd(p, do, states, w2p, gb, prev, rev, tg, name, comm):
    l = p.shape[0]
    nb, nc = l // tg, tg // CH
    out_dt = f32 if prev is None else bf16
    c_in, c_out, c_sems = comm.specs()

    def body(*refs):
        q_ref, k_ref, v_ref, lr_ref, do_ref, st_ref, w2_ref, gb_ref = refs[:8]
        refs = refs[8:]
        if prev is not None:
            pq_ref, pl_ref = refs[:2]
            refs = refs[2:]
        cin, refs = refs[:len(c_in)], refs[len(c_in):]
        dqkv_ref, dlr_ref, dw2_ref, dgb_ref, ds0_ref = refs[:5]
        cout, dst, ds_buf, sems = refs[5:5 + comm.n], refs[5 + comm.n], refs[6 + comm.n], refs[7 + comm.n:]
        comm.run(cin, cout, sems, nb, lambda: compute(q_ref, k_ref, v_ref, lr_ref, do_ref, st_ref, w2_ref, gb_ref,
                                                      pq_ref if prev is not None else None, pl_ref if prev is not None else None,
                                                      dqkv_ref, dlr_ref, dw2_ref, dgb_ref, ds0_ref, dst, ds_buf))

    def compute(q_ref, k_ref, v_ref, lr_ref, do_ref, st_ref, w2_ref, gb_ref, pq_ref, pl_ref,
                dqkv_ref, dlr_ref, dw2_ref, dgb_ref, ds0_ref, dst, ds_buf):
        @pl.when(pl.program_id(0) == 0)
        def _():
            dst[...] = jnp.zeros_like(dst)
            dw2_ref[...] = jnp.zeros_like(dw2_ref)
            dgb_ref[...] = jnp.zeros_like(dgb_ref)

        t = _tile_prep(q_ref, k_ref, lr_ref, w2_ref, gb_ref, rev, nc)
        hm = _head_masks()
        v = v_ref[...].reshape(nc, CH, VW)
        do = do_ref[...].reshape(nc, CH, VW)
        heads = lambda a, h: a[:, :, h * HV:(h + 1) * HV]
        dpst = jnp.concatenate([_bnt(heads(do, h), heads(v, h)) for h in range(NH)], axis=1) * t.tri4
        dv = jnp.concatenate([_btn(t.pst[:, h * CH:(h + 1) * CH], heads(do, h)) for h in range(NH)], axis=2)
        dqd = _bdot(do, st_ref[...])
        for h in range(NH):
            dqd = dqd + hm[h] * _bdot(dpst[:, h * CH:(h + 1) * CH], t.kd)
        dkd = _btn(dpst, t.qs)
        u = _btn(do, t.qd) * _block_diag()[None]
        d = dst[...]
        for n in (range(nc) if rev else range(nc - 1, -1, -1)):
            ds_buf[n] = d
            d = d * t.ecl[n] + u[n]
        dst[...] = d
        ds0_ref[...] = d
        ds = ds_buf[...]
        dv = dv + _bnt(t.kdec, ds)
        dkdec = _bdot(v, ds)
        dcl = jnp.sum(ds * st_ref[...], axis=1, keepdims=True) * t.ecl + jnp.sum(dkdec * t.kdec, axis=1, keepdims=True)
        dc = dqd * t.qd - dkd * t.kd - dkdec * t.kdec
        da = _scan_chunks(dc, not rev) + dcl
        dq = dqd * t.ec * QSCALE
        dk = dkd * t.enc + dkdec * t.edec
        dlog = da.reshape(tg, KW) * _sigmoid(-t.logits) * (1.0 / TAU)
        dlr = _nt(dlog, w2_ref[...])
        dw2_ref[...] += _tn(lr_ref[...], dlog)
        dgb_ref[...] += jnp.sum(dlog, axis=0, keepdims=True)
        dqkv = jnp.concatenate([dq, dk, dv], axis=2).reshape(tg, 2 * KW + VW)
        if prev is not None:
            dqkv = dqkv + pq_ref[...]
            dlr = dlr + pl_ref[...]
        dqkv_ref[...] = dqkv.astype(out_dt)
        dlr_ref[...] = dlr.astype(out_dt)

    blk = (lambda i: i) if rev else (lambda i: nb - 1 - i)
    in_specs = [pl.BlockSpec((tg, KW), lambda i: (blk(i), PQ // KW)), pl.BlockSpec((tg, KW), lambda i: (blk(i), PK // KW)),
                pl.BlockSpec((tg, VW), lambda i: (blk(i), PV // VW)), pl.BlockSpec((tg, LRW), lambda i: (blk(i), PLR // LRW)),
                pl.BlockSpec((tg, VW), lambda i: (blk(i), 0)), pl.BlockSpec((nc, VW, KW), lambda i: (blk(i), 0, 0)),
                pl.BlockSpec((LRW, KW), lambda i: (0, 0)), pl.BlockSpec((1, KW), lambda i: (0, 0))]
    args = [p, p, p, p, do, states, w2p, gb]
    if prev is not None:
        in_specs += [pl.BlockSpec((tg, 2 * KW + VW), lambda i: (blk(i), 0)), pl.BlockSpec((tg, LRW), lambda i: (blk(i), 0))]
        args += list(prev)
    return pl.pallas_call(
        body, name=name, grid=(nb,),
        out_shape=(jax.ShapeDtypeStruct((l, 2 * KW + VW), out_dt), jax.ShapeDtypeStruct((l, LRW), out_dt),
                   jax.ShapeDtypeStruct((LRW, KW), f32), jax.ShapeDtypeStruct((1, KW), f32), jax.ShapeDtypeStruct((VW, KW), f32))
        + tuple(comm.outs),
        in_specs=in_specs + c_in,
        out_specs=(pl.BlockSpec((tg, 2 * KW + VW), lambda i: (blk(i), 0)), pl.BlockSpec((tg, LRW), lambda i: (blk(i), 0)),
                   pl.BlockSpec((LRW, KW), lambda i: (0, 0)), pl.BlockSpec((1, KW), lambda i: (0, 0)),
                   pl.BlockSpec((VW, KW), lambda i: (0, 0))) + tuple(c_out),
        scratch_shapes=[pltpu.VMEM((VW, KW), f32), pltpu.VMEM((nc, VW, KW), f32)] + c_sems,
        compiler_params=_cp(("arbitrary",)),
    )(*args, *comm.ins)


def _ctx_hidden(ctx_ref, g_ref, sc_ref, sh_ref):
    xv = ctx_ref[...]
    r = lax.rsqrt(jnp.mean(xv * xv, axis=-1, keepdims=True) + EPS)
    xn = xv * r
    return xn, xn * (g_ref[...] * (1.0 + sc_ref[...])) + sh_ref[...]


_CTX_W_SPECS = [pl.BlockSpec((D, KW), lambda i: (0, PK // KW)), pl.BlockSpec((D, VW), lambda i: (0, PV // VW)),
                pl.BlockSpec((D, LRW), lambda i: (0, PLR // LRW))]


def _ctx_fwd(ctx, g, scale, shift, w_pad, w2f, w2b, gbf, gbb):
    ncc = CTX // CH

    def body(ctx_ref, g_ref, sc_ref, sh_ref, wk_ref, wv_ref, wl_ref, w2f_ref, w2b_ref, gbf_ref, gbb_ref, sf_ref, sb_ref):
        _, hc = _ctx_hidden(ctx_ref, g_ref, sc_ref, sh_ref)
        k, v, lr = _dot(hc, wk_ref[...]), _dot(hc, wv_ref[...]), _dot(hc, wl_ref[...])
        bd = _block_diag()
        for rev, w2_ref, gb_ref, out in ((False, w2f_ref, gbf_ref, sf_ref), (True, w2b_ref, gbb_ref, sb_ref)):
            tri = _tri(rev)
            st = jnp.zeros((VW, KW), f32)
            for j in (range(ncc - 1, -1, -1) if rev else range(ncc)):
                rows = slice(j * CH, (j + 1) * CH)
                _, c, cl = _decay(lr[rows], w2_ref[...], gb_ref[...], tri, rev)
                st = _state_fwd(k[rows], v[rows], c, cl, st, bd)
            out[...] = st

    vec = pl.BlockSpec((1, D), lambda i: (0, 0))
    w2s = pl.BlockSpec((LRW, KW), lambda i: (0, 0))
    gbs = pl.BlockSpec((1, KW), lambda i: (0, 0))
    sts = pl.BlockSpec((VW, KW), lambda i: (0, 0))
    return pl.pallas_call(
        body, name="ctx_fwd", grid=(1,), out_shape=(jax.ShapeDtypeStruct((VW, KW), f32),) * 2,
        in_specs=[pl.BlockSpec((CTX, D), lambda i: (0, 0)), vec, vec, vec] + _CTX_W_SPECS + [w2s, w2s, gbs, gbs],
        out_specs=(sts, sts), compiler_params=_cp(("arbitrary",)),
    )(ctx, g, scale, shift, w_pad, w_pad, w_pad, w2f, w2b, gbf, gbb)


def _ctx_bwd(ctx, g, scale, shift, w_pad, w2f, w2b, gbf, gbb, dsf, dsb):
    ncc = CTX // CH

    def body(ctx_ref, g_ref, sc_ref, sh_ref, wk_ref, wv_ref, wl_ref, w2f_ref, w2b_ref, gbf_ref, gbb_ref, dsf_ref, dsb_ref,
             dwk_ref, dwv_ref, dwl_ref, dmod_ref, dg_ref, dw2_ref, dgb_ref):
        xn, hc = _ctx_hidden(ctx_ref, g_ref, sc_ref, sh_ref)
        k, v, lr = _dot(hc, wk_ref[...]), _dot(hc, wv_ref[...]), _dot(hc, wl_ref[...])
        bd = _block_diag()
        dk_rows, dv_rows, dl_rows = [None] * ncc, [None] * ncc, [None] * ncc
        for d, (rev, w2_ref, gb_ref, ds_ref) in enumerate(((False, w2f_ref, gbf_ref, dsf_ref), (True, w2b_ref, gbb_ref, dsb_ref))):
            tri = _tri(rev)
            order = list(range(ncc - 1, -1, -1) if rev else range(ncc))
            st, saved = jnp.zeros((VW, KW), f32), {}
            for j in order:
                rows = slice(j * CH, (j + 1) * CH)
                logits, c, cl = _decay(lr[rows], w2_ref[...], gb_ref[...], tri, rev)
                saved[j] = (logits, c, cl, st)
                st = _state_fwd(k[rows], v[rows], c, cl, st, bd)
            dst = ds_ref[...]
            dw2 = jnp.zeros((LRW, KW), f32)
            dgb = jnp.zeros((1, KW), f32)
            for j in reversed(order):
                rows = slice(j * CH, (j + 1) * CH)
                logits, c, cl, st0 = saved[j]
                _, dk, dv, da, dst = _chunk_bwd(None, k[rows], v[rows], c, cl, st0, dst, None, tri, _tri(not rev), None, bd)
                dlog = da * _sigmoid(-logits) * (1.0 / TAU)
                dl = _nt(dlog, w2_ref[...])
                dw2 = dw2 + _tn(lr[rows], dlog)
                dgb = dgb + jnp.sum(dlog, axis=0, keepdims=True)
                dk_rows[j] = dk if dk_rows[j] is None else dk_rows[j] + dk
                dv_rows[j] = dv if dv_rows[j] is None else dv_rows[j] + dv
                dl_rows[j] = dl if dl_rows[j] is None else dl_rows[j] + dl
            dw2_ref[d] = dw2
            dgb_ref[d] = dgb
        dk, dv, dl = (jnp.concatenate(t, axis=0) for t in (dk_rows, dv_rows, dl_rows))
        dwk_ref[...] = _tn(hc, dk)
        dwv_ref[...] = _tn(hc, dv)
        dwl_ref[...] = _tn(hc, dl)
        dh = _nt(dk, wk_ref[...]) + _nt(dv, wv_ref[...]) + _nt(dl, wl_ref[...])
        gx = dh * xn
        dmod_ref[:, 0:D] = jnp.sum(dh, axis=0, keepdims=True)
        dmod_ref[:, D:2 * D] = jnp.sum(gx, axis=0, keepdims=True) * g_ref[...]
        dg_ref[...] = jnp.sum(gx, axis=0, keepdims=True) * (1.0 + sc_ref[...])

    vec = pl.BlockSpec((1, D), lambda i: (0, 0))
    w2s = pl.BlockSpec((LRW, KW), lambda i: (0, 0))
    gbs = pl.BlockSpec((1, KW), lambda i: (0, 0))
    sts = pl.BlockSpec((VW, KW), lambda i: (0, 0))
    full = lambda *s: pl.BlockSpec(s, lambda i: (0,) * len(s))
    return pl.pallas_call(
        body, name="ctx_bwd", grid=(1,),
        out_shape=(jax.ShapeDtypeStruct((D, KW), f32), jax.ShapeDtypeStruct((D, VW), f32), jax.ShapeDtypeStruct((D, LRW), f32),
                   jax.ShapeDtypeStruct((1, 2 * D), f32), jax.ShapeDtypeStruct((1, D), f32),
                   jax.ShapeDtypeStruct((2, LRW, KW), f32), jax.ShapeDtypeStruct((2, 1, KW), f32)),
        in_specs=[pl.BlockSpec((CTX, D), lambda i: (0, 0)), vec, vec, vec] + _CTX_W_SPECS + [w2s, w2s, gbs, gbs, sts, sts],
        out_specs=(full(D, KW), full(D, VW), full(D, LRW), full(1, 2 * D), full(1, D), full(2, LRW, KW), full(2, 1, KW)),
        compiler_params=_cp(("arbitrary",)),
    )(ctx, g, scale, shift, w_pad, w_pad, w_pad, w2f, w2b, gbf, gbb, dsf, dsb)


def _layernorm(va, g, b):
    mu = jnp.mean(va, axis=-1, keepdims=True)
    xc = va - mu
    rstd = lax.rsqrt(jnp.mean(xc * xc, axis=-1, keepdims=True) + EPS)
    vhat = xc * rstd
    return vhat, rstd, vhat * g + b


def _mix_fwd(p, ln_g, ln_b, ws, bs_t):
    l = p.shape[0]
    nch = l // ACH
    half = AW // 2

    def body(p_ref, g_ref, b_ref, ws_ref, bs_ref, sv_ref, va_buf, col_buf, sem):
        cp = pltpu.make_async_copy(p_ref.at[:, pl.ds(PVA, AW)], va_buf, sem)
        cp.start()
        cp.wait()

        def rows_step(n, carry):
            rows = pl.ds(pl.multiple_of(n * ACH, ACH), ACH)
            _, _, vn = _layernorm(va_buf[rows, :].astype(f32), g_ref[...], b_ref[...])
            for gi in range(2):
                sl = slice(gi * ACH, (gi + 1) * ACH)
                sv_ref[rows, sl] = (_dot(ws_ref[gi], vn[:, sl]) + bs_ref[:, gi:gi + 1]).astype(bf16)
            col_buf[0, rows, :] = vn[:, half:half + ACH]
            col_buf[1, rows, :] = vn[:, half + ACH:]
            return carry

        lax.fori_loop(0, nch, rows_step, 0, unroll=4)

        def cols_step(cidx, carry):
            rows = pl.ds(cidx, ACH, stride=GW)
            for gi in range(2, 4):
                col_buf[gi - 2, rows, :] = _dot(ws_ref[gi], col_buf[gi - 2, rows, :]) + bs_ref[:, gi:gi + 1]
            return carry

        lax.fori_loop(0, GW, cols_step, 0, unroll=4)

        def out_step(n, carry):
            rows = pl.ds(pl.multiple_of(n * ACH, ACH), ACH)
            sv_ref[rows, half:half + ACH] = col_buf[0, rows, :].astype(bf16)
            sv_ref[rows, half + ACH:] = col_buf[1, rows, :].astype(bf16)
            return carry

        lax.fori_loop(0, nch, out_step, 0, unroll=4)

    vm = pl.BlockSpec(memory_space=pltpu.VMEM)
    return pl.pallas_call(
        body, name="mix_fwd", out_shape=jax.ShapeDtypeStruct((l, AW), bf16),
        in_specs=[pl.BlockSpec(memory_space=pl.ANY), vm, vm, vm, vm], out_specs=vm,
        scratch_shapes=[pltpu.VMEM((l, AW), bf16), pltpu.VMEM((2, l, ACH), f32), pltpu.SemaphoreType.DMA(())],
        compiler_params=_cp(),
    )(p, ln_g, ln_b, ws, bs_t)


def _mix_bwd(p, dsv, ln_g, ln_b, ws_t):
    l = p.shape[0]
    nch = l // ACH
    half = AW // 2

    def body(p_ref, dsv_hbm, g_ref, b_ref, wst_ref, dva_ref, dws_ref, dbs_ref, dg_ref, db_ref, va_buf, dsv_buf, vn_col, ds_col, sems):
        cp1 = pltpu.make_async_copy(p_ref.at[:, pl.ds(PVA, AW)], va_buf, sems.at[0])
        cp2 = pltpu.make_async_copy(dsv_hbm, dsv_buf, sems.at[1])
        cp1.start()
        cp2.start()
        dws_ref[...] = jnp.zeros_like(dws_ref)
        dbs_ref[...] = jnp.zeros_like(dbs_ref)
        dg_ref[...] = jnp.zeros_like(dg_ref)
        db_ref[...] = jnp.zeros_like(db_ref)
        cp1.wait()
        cp2.wait()

        def rows_step(n, carry):
            rows = pl.ds(pl.multiple_of(n * ACH, ACH), ACH)
            _, _, vn = _layernorm(va_buf[rows, :].astype(f32), g_ref[...], b_ref[...])
            ds = dsv_buf[rows, :].astype(f32)
            for gi in range(2):
                sl = slice(gi * ACH, (gi + 1) * ACH)
                dws_ref[gi] += _nt(ds[:, sl], vn[:, sl])
                dbs_ref[gi] += ds[:, sl]
            for gi in range(2):
                sl = slice(half + gi * ACH, half + (gi + 1) * ACH)
                vn_col[gi, rows, :] = vn[:, sl]
                ds_col[gi, rows, :] = ds[:, sl]
            return carry

        lax.fori_loop(0, nch, rows_step, 0, unroll=4)

        def cols_step(cidx, carry):
            rows = pl.ds(cidx, ACH, stride=GW)
            for gi in range(2, 4):
                ds = ds_col[gi - 2, rows, :]
                dws_ref[gi] += _nt(ds, vn_col[gi - 2, rows, :])
                dbs_ref[gi] += ds
                ds_col[gi - 2, rows, :] = _dot(wst_ref[gi], ds)
            return carry

        lax.fori_loop(0, GW, cols_step, 0, unroll=4)

        def out_step(n, carry):
            rows = pl.ds(pl.multiple_of(n * ACH, ACH), ACH)
            vhat, rstd, _ = _layernorm(va_buf[rows, :].astype(f32), g_ref[...], b_ref[...])
            ds = dsv_buf[rows, :].astype(f32)
            dvn = jnp.concatenate([_dot(wst_ref[0], ds[:, 0:ACH]), _dot(wst_ref[1], ds[:, ACH:half]), ds_col[0, rows, :], ds_col[1, rows, :]], axis=1)
            dg_ref[...] += jnp.sum(dvn * vhat, axis=0, keepdims=True)
            db_ref[...] += jnp.sum(dvn, axis=0, keepdims=True)
            dvh = dvn * g_ref[...]
            dva = rstd * (dvh - jnp.mean(dvh, axis=-1, keepdims=True) - vhat * jnp.mean(dvh * vhat, axis=-1, keepdims=True))
            dva_ref[rows, :] = dva.astype(bf16)
            return carry

        lax.fori_loop(0, nch, out_step, 0, unroll=4)

    vm = pl.BlockSpec(memory_space=pltpu.VMEM)
    hbm = pl.BlockSpec(memory_space=pl.ANY)
    return pl.pallas_call(
        body, name="mix_bwd",
        out_shape=(jax.ShapeDtypeStruct((l, AW), bf16), jax.ShapeDtypeStruct((4, ACH, ACH), f32), jax.ShapeDtypeStruct((4, ACH, ACH), f32),
                   jax.ShapeDtypeStruct((1, AW), f32), jax.ShapeDtypeStruct((1, AW), f32)),
        in_specs=[hbm, hbm, vm, vm, vm], out_specs=(vm, vm, vm, vm, vm),
        scratch_shapes=[pltpu.VMEM((l, AW), bf16), pltpu.VMEM((l, AW), bf16), pltpu.VMEM((2, l, ACH), f32), pltpu.VMEM((2, l, ACH), f32),
                        pltpu.SemaphoreType.DMA((2,))],
        compiler_params=_cp(),
    )(p, dsv, ln_g, ln_b, ws_t)


def _mid(x, tgt, p, o_f, o_b, sv, gate, gf, gb_norm, w_pa, w_pb, w_out, tl):
    l = x.shape[0]

    def body(x_ref, t_ref, zb_ref, ua_ref, za_ref, g1_ref, g2_ref, of_ref, ob_ref, sv_ref, gate_ref, gf_ref, gbn_ref,
             wpa_ref, wpb_ref, wout_ref,
             dx1_ref, dzbua_ref, dzag_ref, dsv_ref, do_ref, dwout_bf, dwpa_bf, dwpb_bf, dgf_ref, dgate_ref, dgbn_ref, loss_ref,
             dwout_ref, dwpa_ref, dwpb_ref):
        @pl.when(pl.program_id(0) == 0)
        def _():
            for r in (dwout_ref, dwpa_ref, dwpb_ref, dgf_ref, dgate_ref, dgbn_ref, loss_ref):
                r[...] = jnp.zeros_like(r)

        o = of_ref[...] + ob_ref[...]
        rr = jnp.concatenate(
            [jnp.broadcast_to(lax.rsqrt(jnp.mean(o[:, h * HV:(h + 1) * HV] ** 2, axis=-1, keepdims=True) + EPS), (tl, HV))
             for h in range(NH)], axis=1)
        ohat = o * rr
        on = ohat * gbn_ref[...]
        szb, dszb = _silu_and_grad(zb_ref[...].astype(f32))
        tb = on * szb
        u = ua_ref[...].astype(f32)
        svv = sv_ref[...].astype(f32)
        sza, dsza = _silu_and_grad(za_ref[...].astype(f32))
        ta = u * svv * sza
        ya = _dot(ta, wpa_ref[...])
        yb = _dot(tb, wpb_ref[...])
        g1 = _sigmoid(g1_ref[...].astype(f32))
        g2 = _sigmoid(g2_ref[...].astype(f32))
        m = g1 * ya + g2 * yb
        y2 = _dot(m, wout_ref[...])
        x1 = x_ref[...] + gate_ref[...] * y2
        r1 = lax.rsqrt(jnp.mean(x1 * x1, axis=-1, keepdims=True) + EPS)
        x1n = x1 * r1
        err = x1n * gf_ref[...] - t_ref[...]
        loss_ref[...] += jnp.sum(jnp.sum(err * err, axis=-1, keepdims=True), axis=0, keepdims=True) * (0.5 / D)
        dout = err * (1.0 / D)
        dgf_ref[...] += jnp.sum(dout * x1n, axis=0, keepdims=True)
        dx1n = dout * gf_ref[...]
        dx1 = r1 * (dx1n - x1n * jnp.mean(dx1n * x1n, axis=-1, keepdims=True))
        dx1_ref[...] = dx1
        dgate_ref[...] += jnp.sum(dx1 * y2, axis=0, keepdims=True)
        dy2 = dx1 * gate_ref[...]
        dwout_ref[...] += _tn(m, dy2)
        dm = _nt(dy2, wout_ref[...])
        dya = dm * g1
        dyb = dm * g2
        dzag_ref[:, AW:AW + D] = (dm * ya * g1 * (1.0 - g1)).astype(bf16)
        dzag_ref[:, AW + D:] = (dm * yb * g2 * (1.0 - g2)).astype(bf16)
        dwpa_ref[...] += _tn(ta, dya)
        dta = _nt(dya, wpa_ref[...])
        dzbua_ref[:, AW:] = (dta * svv * sza).astype(bf16)
        dsv_ref[...] = (dta * u * sza).astype(bf16)
        dzag_ref[:, 0:AW] = (dta * u * svv * dsza).astype(bf16)
        dwpb_ref[...] += _tn(tb, dyb)
        dtb = _nt(dyb, wpb_ref[...])
        don = dtb * szb
        dzbua_ref[:, 0:AW] = (dtb * on * dszb).astype(bf16)
        dgbn_ref[...] += jnp.sum(don * ohat, axis=0, keepdims=True)
        doh = don * gbn_ref[...]
        prod = doh * ohat
        mh = jnp.concatenate(
            [jnp.broadcast_to(jnp.mean(prod[:, h * HV:(h + 1) * HV], axis=-1, keepdims=True), (tl, HV)) for h in range(NH)], axis=1)
        do_ref[...] = (rr * (doh - ohat * mh)).astype(bf16)

        @pl.when(pl.program_id(0) == l // tl - 1)
        def _():
            for acc, out in ((dwout_ref, dwout_bf), (dwpa_ref, dwpa_bf), (dwpb_ref, dwpb_bf)):
                out[...] = acc[...].astype(bf16)

    row = lambda w, j: pl.BlockSpec((tl, w), lambda i, j=j: (i, j))
    full = lambda *s: pl.BlockSpec(s, lambda i: (0,) * len(s))
    return pl.pallas_call(
        body, name="mid", grid=(l // tl,),
        out_shape=(jax.ShapeDtypeStruct((l, D), f32), jax.ShapeDtypeStruct((l, 2 * AW), bf16), jax.ShapeDtypeStruct((l, AW + 2 * D), bf16),
                   jax.ShapeDtypeStruct((l, AW), bf16), jax.ShapeDtypeStruct((l, VW), bf16),
                   jax.ShapeDtypeStruct((D, D), bf16), jax.ShapeDtypeStruct((AW, D), bf16), jax.ShapeDtypeStruct((VW, D), bf16),
                   jax.ShapeDtypeStruct((1, D), f32), jax.ShapeDtypeStruct((1, D), f32), jax.ShapeDtypeStruct((1, VW), f32),
                   jax.ShapeDtypeStruct((1, 1), f32)),
        scratch_shapes=[pltpu.VMEM((D, D), f32), pltpu.VMEM((AW, D), f32), pltpu.VMEM((VW, D), f32)],
        in_specs=[row(D, 0), row(D, 0), row(AW, PZB // AW), row(AW, PUA // AW), row(AW, PZA // AW), row(D, PG1 // D), row(D, PG2 // D),
                  row(VW, 0), row(VW, 0), row(AW, 0), full(1, D), full(1, D), full(1, VW), full(AW, D), full(VW, D), full(D, D)],
        out_specs=(row(D, 0), row(2 * AW, 0), row(AW + 2 * D, 0), row(AW, 0), row(VW, 0),
                   full(D, D), full(AW, D), full(VW, D), full(1, D), full(1, D), full(1, VW), full(1, 1)),
        compiler_params=_cp(("arbitrary",)),
    )(x, tgt, p, p, p, p, p, o_f, o_b, sv, gate, gf, gb_norm, w_pa, w_pb, w_out)


def _in_bwd(x, dx1, dqkv, dzbua, dva, dzag, dlr, w_pad, g, scale, tl, comm):
    l = x.shape[0]
    c_in, c_out, c_sems = comm.specs()

    def body(*refs):
        cin = refs[14:14 + len(c_in)]
        outs = refs[14 + len(c_in):]
        comm.run(cin, outs[4:4 + comm.n], outs[4 + comm.n:], l // tl, lambda: compute(*refs[:14], *outs[:4]))

    def compute(x_ref, dx1_ref, a_ref, b_ref, c_ref, e_ref, lr_ref, wa_ref, wb_ref, wc_ref, we_ref, wl_ref, g_ref, sc_ref,
                gx_ref, dsh_ref, dsc_ref, dg_ref):
        @pl.when(pl.program_id(0) == 0)
        def _():
            for r in (dsh_ref, dsc_ref, dg_ref):
                r[...] = jnp.zeros_like(r)

        dh = (_nt(a_ref[...], wa_ref[...]) + _nt(b_ref[...], wb_ref[...]) + _nt(c_ref[...], wc_ref[...]) + _nt(e_ref[...], we_ref[...])
              + _nt(lr_ref[...], wl_ref[...]))
        xv = x_ref[...]
        r = lax.rsqrt(jnp.mean(xv * xv, axis=-1, keepdims=True) + EPS)
        xn = xv * r
        gxn = jnp.sum(dh * xn, axis=0, keepdims=True)
        dsh_ref[...] += jnp.sum(dh, axis=0, keepdims=True)
        dsc_ref[...] += gxn * g_ref[...]
        dg_ref[...] += gxn * (1.0 + sc_ref[...])
        dxn = dh * (g_ref[...] * (1.0 + sc_ref[...]))
        gx_ref[...] = dx1_ref[...] + r * (dxn - xn * jnp.mean(dxn * xn, axis=-1, keepdims=True))

    row = lambda w: pl.BlockSpec((tl, w), lambda i: (i, 0))
    wcol = lambda w, j: pl.BlockSpec((D, w), lambda i, j=j: (0, j))
    vec = pl.BlockSpec((1, D), lambda i: (0, 0))
    return pl.pallas_call(
        body, name="in_bwd", grid=(l // tl,),
        out_shape=(jax.ShapeDtypeStruct((l, D), f32),) + (jax.ShapeDtypeStruct((1, D), f32),) * 3 + tuple(comm.outs),
        in_specs=[row(D), row(D), row(2 * KW + VW), row(2 * AW), row(AW), row(AW + 2 * D), row(LRW),
                  wcol(2 * KW + VW, 0), wcol(2 * AW, PZB // (2 * AW)), wcol(AW, PVA // AW), wcol(AW + 2 * D, PZA // (AW + 2 * D)),
                  wcol(LRW, PLR // LRW), vec, vec] + c_in,
        out_specs=(row(D), vec, vec, vec) + tuple(c_out), scratch_shapes=c_sems,
        compiler_params=_cp(("arbitrary",)),
    )(x, dx1, dqkv, dzbua, dva, dzag, dlr, w_pad, w_pad, w_pad, w_pad, w_pad, g, scale, *comm.ins)


def _tn_matmul(a, bs, tl, name, comm=None):
    l, m = a.shape
    k = len(bs)
    comm = comm or _Comm([], [], None)
    c_in, c_out, c_sems = comm.specs()

    def body(a_ref, *refs):
        b_refs, cin = refs[:k], refs[k:k + len(c_in)]
        o_refs = refs[k + len(c_in):2 * k + len(c_in)]
        cout, sems = refs[2 * k + len(c_in):2 * k + len(c_in) + comm.n], refs[2 * k + len(c_in) + comm.n:]

        def compute():
            @pl.when(pl.program_id(0) == 0)
            def _():
                for o_ref in o_refs:
                    o_ref[...] = jnp.zeros_like(o_ref)

            av = a_ref[...]
            for b_ref, o_ref in zip(b_refs, o_refs):
                o_ref[...] += _tn(av, b_ref[...])

        comm.run(cin, cout, sems, l // tl, compute)

    return pl.pallas_call(
        body, name=name, grid=(l // tl,), out_shape=tuple(jax.ShapeDtypeStruct((m, b.shape[1]), f32) for b in bs) + tuple(comm.outs),
        in_specs=[pl.BlockSpec((tl, m), lambda i: (i, 0))] + [pl.BlockSpec((tl, b.shape[1]), lambda i: (i, 0)) for b in bs] + c_in,
        out_specs=tuple(pl.BlockSpec((m, b.shape[1]), lambda i: (0, 0)) for b in bs) + tuple(c_out),
        scratch_shapes=c_sems, compiler_params=_cp(("arbitrary",)),
    )(a, *bs, *comm.ins)


def _pad_gate(w2, gb):
    z = jnp.zeros((RANK, KW), f32)
    tail = jnp.zeros((LRW - 2 * RANK, KW), f32)
    w2f = jnp.concatenate([w2[0], z, tail], axis=0)
    w2b = jnp.concatenate([z, w2[1], tail], axis=0)
    return w2f, w2b, gb[0:1], gb[1:2]


EARLY_A_ROWS = 384


class _NoExchange:
    def __init__(self, w_pa, w_pb, w_out):
        self.weights = (w_pa, w_pb, w_out)

    def gather_proj(self):
        return _Comm([], [], None)

    def proj_weights(self, got):
        return self.weights

    def first(self, dw_out, dw_pa, dw_pb):
        return _Comm([], [], None)

    def early_a(self, small, blocks):
        return _Comm([], [], None)

    def early_b(self, blocks):
        return _Comm([], [], None)

    def late(self, blocks, dgt):
        return _Comm([], [], None)


class _Exchanges:
    def __init__(self, pa, pb, wo):
        self.shards = (pa, pb, wo)

    def gather_proj(self):
        def plan(i, o):
            srcs = [lambda j, r=r: r for r in i]
            dsts = [lambda j: o[0].at[:, _lanes(j)], lambda j: o[1].at[:, _lanes(j)], lambda j: o[2].at[j]]
            return srcs, dsts, None
        sds = jax.ShapeDtypeStruct
        return _Comm(self.shards, [sds((AW, D), bf16), sds((VW, D), bf16), sds((NDEV, 128, D), bf16)], plan)

    def proj_weights(self, got):
        return got[0], got[1], got[2].reshape(D, D)

    def first(self, dw_out, dw_pa, dw_pb):
        def plan(i, o):
            srcs = [lambda j: i[0].at[j], lambda j: i[1].at[:, _lanes(j)], lambda j: i[2].at[:, _lanes(j)]]
            dsts = [lambda j, r=r: r.at[j] for r in o]
            return srcs, dsts, None
        sds = jax.ShapeDtypeStruct
        return _Comm([dw_out.reshape(NDEV, 128, D), dw_pa, dw_pb],
                     [sds((NDEV, 128, D), bf16), sds((NDEV, AW, 128), bf16), sds((NDEV, VW, 128), bf16)], plan)

    def early_a(self, small, blocks):
        def plan(i, o):
            srcs = [lambda j: i[0], lambda j: i[1].at[j]]
            dsts = [lambda j, r=r: r.at[j] for r in o]
            return srcs, dsts, [None, lambda j: j >= LATE_DESTS - 1]
        sds = jax.ShapeDtypeStruct
        return _Comm([small, blocks], [sds((NDEV,) + small.shape, f32), sds(blocks.shape, bf16)], plan)

    def early_b(self, blocks):
        def plan(i, o):
            return [lambda j: i[0].at[j]], [lambda j: o[0].at[j]], [lambda j: j >= LATE_DESTS - 1]
        return _Comm([blocks], [jax.ShapeDtypeStruct(blocks.shape, bf16)], plan)

    def late(self, blocks, dgt):
        return _LateComm(blocks, dgt)


def _local_step(x, ctx, tgt, mod, modc, norm_g, w_pad, ln_g, ln_b, ws, bs, w2, gb, gb_norm, gf, xch):
    shift, scale, gate = mod[:, 0:D], mod[:, D:2 * D], mod[:, 2 * D:]
    shift_c, scale_c = modc[:, 0:D], modc[:, D:]
    w2f, w2b, gbf, gbb = _pad_gate(w2, gb)

    p, h, *got_proj = _in_proj(x, norm_g, scale, shift, w_pad, 512, xch.gather_proj())
    w_pa, w_pb, w_out = xch.proj_weights(got_proj)
    sc_f, sc_b = _ctx_fwd(ctx, norm_g, scale_c, shift_c, w_pad, w2f, w2b, gbf, gbb)
    o_f, st_f = _gla_fwd(p, w2f, gbf, sc_f, False, 512, "gla_fwd_f")
    o_b, st_b = _gla_fwd(p, w2b, gbb, sc_b, True, 512, "gla_fwd_b")
    sv = _mix_fwd(p, ln_g, ln_b, ws.astype(bf16), bs.T)
    (dx1, dzbua, dzag, dsv, do, dw_out, dw_pa, dw_pb, dgf, dgate, dgbn, loss) = _mid(
        x, tgt, p, o_f, o_b, sv, gate, gf, gb_norm, w_pa, w_pb, w_out, 256)
    dva, dws, dbs_acc, dln_g, dln_b = _mix_bwd(p, dsv, ln_g, ln_b, jnp.swapaxes(ws, 1, 2).astype(bf16))
    dw_zbua, dw_va, dw_zag, *got_first = _tn_matmul(h, [dzbua, dva, dzag], 512, "dw_early", xch.first(dw_out, dw_pa, dw_pb))
    blocks_a = _pack_dw_early(dw_zbua, dw_va, dw_zag, 0, EARLY_A_ROWS, "pack_dw_early_a")
    blocks_b = _pack_dw_early(dw_zbua, dw_va, dw_zag, EARLY_A_ROWS, D - EARLY_A_ROWS, "pack_dw_early_b")
    small = _rows128(dln_g, dln_b, dws, jnp.sum(dbs_acc, axis=-1), dgbn, dgf, jnp.broadcast_to(loss, (1, 128)))

    dqkv_f, dlr_f, dw2f, dgbf, dsc_f, *got_a = _gla_bwd(p, do, st_f, w2f, gbf, None, False, 512, "gla_bwd_f",
                                                        xch.early_a(small, blocks_a))
    dqkv, dlr, dw2b, dgbb, dsc_b, *got_b = _gla_bwd(p, do, st_b, w2b, gbb, (dqkv_f, dlr_f), True, 512, "gla_bwd_b",
                                                    xch.early_b(blocks_b))
    dwk_c, dwv_c, dwl_c, dmodc, dg_c, dw2c, dgbc = _ctx_bwd(ctx, norm_g, scale_c, shift_c, w_pad, w2f, w2b, gbf, gbb, dsc_f, dsc_b)
    dw_qkv, dw_lr = _tn_matmul(h, [dqkv, dlr], 512, "dw_qkv_lr")
    blocks_late = _pack_dw_late(dw_qkv, dw_lr, dwk_c, dwv_c, dwl_c)
    dw2 = jnp.stack([dw2f[0:RANK] + dw2c[0, 0:RANK], dw2b[RANK:2 * RANK] + dw2c[1, RANK:2 * RANK]])
    dgb = jnp.concatenate([dgbf + dgbc[0], dgbb + dgbc[1]], axis=0)
    dgt = jnp.concatenate([jnp.transpose(dw2.reshape(2, RANK, NDEV, 32), (2, 0, 1, 3)).reshape(NDEV, 2 * RANK * 32),
                           jnp.transpose(dgb.reshape(2, NDEV, 32), (1, 0, 2)).reshape(NDEV, 64),
                           jnp.zeros((NDEV, 64), f32)], axis=1).reshape(NDEV, 9, 128)
    gx, dshift, dscale, dg, *got_late = _in_bwd(x, dx1, dqkv, dzbua, dva, dzag, dlr, w_pad, norm_g, scale, 512,
                                                xch.late(blocks_late, dgt))
    return dict(loss=loss, gx=gx, dmod=jnp.concatenate([dshift, dscale, dgate], axis=1), dmodc=dmodc, dnorm_g=dg + dg_c,
                small=small, blocks_a=blocks_a, blocks_b=blocks_b, blocks_late=blocks_late, dw2=dw2, dgb=dgb,
                dw_pa=dw_pa, dw_pb=dw_pb, dw_out=dw_out, got_first=got_first, got_a=got_a, got_b=got_b, got_late=got_late)


def _rows128(*vs):
    out = []
    for t in vs:
        t = t.reshape(-1)
        pad = (-t.shape[0]) % 128
        out.append(jnp.pad(t, (0, pad)) if pad else t)
    return jnp.concatenate(out).reshape(-1, 128)


def kernel(x, c, ctx, c_ctx, w_mod, b_mod, norm_g, w_in, a_ln_g, a_ln_b, a_ws, a_bs, b_gate_w2, b_gate_b, b_norm_g, w_proj_a, w_proj_b, w_out, final_norm_g, loss_target, m_c_ctx, m_w_mod, m_b_mod, m_norm_g, m_w_in, m_a_ln_g, m_a_ln_b, m_a_ws, m_a_bs, m_b_gate_w2, m_b_gate_b, m_b_norm_g, m_w_proj_a, m_w_proj_b, m_w_out, m_final_norm_g, v_c_ctx, v_w_mod, v_b_mod, v_norm_g, v_w_in, v_a_ln_g, v_a_ln_b, v_a_ws, v_a_bs, v_b_gate_w2, v_b_gate_b, v_b_norm_g, v_w_proj_a, v_w_proj_b, v_w_out, v_final_norm_g):
    me = _me()
    ncol = w_mod.shape[2]

    gate_mine = _rows128(jnp.concatenate([b_gate_w2.reshape(-1), b_gate_b.reshape(-1)]))
    bm_mine = lax.dynamic_slice(b_mod, (0, me * ncol), (1, ncol))
    cs, mods, wg, gates = _gather_first(c, c_ctx.reshape(1, D), w_mod[0], bm_mine, w_in[0].astype(bf16), gate_mine)
    w_pad = _repack_w(wg)
    gflat = gates.reshape(NDEV, 9 * 128)
    w2 = jnp.transpose(gflat[:, 0:2 * RANK * 32].reshape(NDEV, 2, RANK, 32), (1, 2, 0, 3)).reshape(2, RANK, KW)
    gb = jnp.transpose(gflat[:, 2 * RANK * 32:2 * RANK * 32 + 64].reshape(NDEV, 2, 32), (1, 0, 2)).reshape(2, KW)

    mods = jnp.transpose(mods, (1, 0, 2)).reshape(16, 3 * D)
    mod = lax.dynamic_slice(mods, (me, 0), (1, 3 * D))
    modc = mods[8:9, 0:2 * D]

    xch = _Exchanges(w_proj_a[0].astype(bf16), w_proj_b[0].astype(bf16), w_out[0].astype(bf16))
    r = _local_step(x[0], ctx[0], loss_target[0], mod, modc, norm_g, w_pad, a_ln_g, a_ln_b, a_ws[0], a_bs[0], w2, gb,
                    b_norm_g, final_norm_g.reshape(1, D), xch)
    p_out, p_pa, p_pb = r["got_first"]
    smalls_e, p_in_a = r["got_a"]
    (p_in_b,) = r["got_b"]
    _, _, p_in_late, p_gt = r["got_late"]

    n_e = (AW + AW + 4 * ACH * ACH + AW + VW + D) // 128
    row = lambda t: t.reshape(1, D)
    rep_e = _adam_params(smalls_e, [a_ln_g, a_ln_b, a_ws, a_bs, b_norm_g, row(final_norm_g)],
                         [m_a_ln_g, m_a_ln_b, m_a_ws, m_a_bs, m_b_norm_g, row(m_final_norm_g)],
                         [v_a_ln_g, v_a_ln_b, v_a_ws, v_a_bs, v_b_norm_g, row(v_final_norm_g)], "adam_rep_early")
    rep_e = [t[0:5] + (t[5].reshape(D),) for t in rep_e]
    losses = smalls_e[:, n_e, 0]
    loss = losses[0]
    for i in range(1, NDEV):
        loss = loss + losses[i]

    dbm = r["dmod"] + jnp.concatenate([r["dmodc"], jnp.zeros((1, D), f32)], axis=1)
    smalls_l = _all_gather(_rows128(r["dnorm_g"], dbm, r["dmod"], r["dmodc"]), "gather_small")
    n_l = (D + 3 * D) // 128
    rep_l = _adam_params(smalls_l, [norm_g, b_mod], [m_norm_g, m_b_mod], [v_norm_g, v_b_mod], "adam_rep_late")
    tail = smalls_l[:, n_l:].reshape(NDEV, -1)
    dmods = tail[:, 0:3 * D]
    dmodc_all = tail[:, 3 * D:5 * D]
    dmodc_tot = dmodc_all[0:1]
    for i in range(1, NDEV):
        dmodc_tot = dmodc_tot + dmodc_all[i:i + 1]

    dm_rows = jnp.concatenate([dmods, jnp.concatenate([dmodc_tot, jnp.zeros((1, D), f32)], axis=1), jnp.zeros((7, 3 * D), f32)], axis=0)
    dm_mine = lax.dynamic_slice(dm_rows, (0, me * ncol), (16, ncol))
    g_wmod, gcs = _mod_bwd(cs, dm_mine, w_mod[0])
    wm = _adam(g_wmod[None], w_mod, m_w_mod, v_w_mod, 256, "adam_wmod")
    cc = _adam(gcs.reshape(NDEV, 8, 128), c_ctx.reshape(8, 128), m_c_ctx.reshape(8, 128), v_c_ctx.reshape(8, 128), 8, "adam_cctx")

    a_in = _adam_w_in(p_in_a, p_in_b, p_in_late, w_in, m_w_in, v_w_in)
    a_pa = _adam(p_pa, w_proj_a, m_w_proj_a, v_w_proj_a, AW, "adam_w_pa")
    a_pb = _adam(p_pb, w_proj_b, m_w_proj_b, v_w_proj_b, VW, "adam_w_pb")
    a_out = _adam(p_out, w_out, m_w_out, v_w_out, 128, "adam_w_out")
    gate_m = _rows128(jnp.concatenate([m_b_gate_w2.reshape(-1), m_b_gate_b.reshape(-1)]))
    gate_v = _rows128(jnp.concatenate([v_b_gate_w2.reshape(-1), v_b_gate_b.reshape(-1)]))
    a_gt = [t.reshape(-1) for t in _adam(p_gt, gate_mine, gate_m, gate_v, 9, "adam_gate")]
    nw2 = 2 * RANK * 32
    sh = [(a_in[k], a_pa[k], a_pb[k], a_out[k], a_gt[k][0:nw2].reshape(1, 2, RANK, 32),
           a_gt[k][nw2:nw2 + 64].reshape(1, 2, 32)) for k in range(4)]

    outs = [loss, r["gx"][None]]
    for k in range(4):
        lg, lb, aws, abs_, bng, fng = rep_e[k]
        n_g, bmod = rep_l[k]
        s_in, s_pa, s_pb, s_out, s_w2, s_gb = sh[k]
        outs += [cc[k].reshape(D), wm[k], bmod, n_g, s_in, lg, lb, aws, abs_, s_w2, s_gb, bng, s_pa, s_pb, s_out, fng]
    return tuple(outs)
```

```python
import functools

import jax
import jax.numpy as jnp
from jax import lax
from jax.experimental import pallas as pl
from jax.experimental.pallas import tpu as pltpu

f32, bf16 = jnp.float32, jnp.bfloat16

D = 1024
CTX = 256
EPS = 1e-6
AW = 512
ACH = 128
GW = 64
KW = 256
VW = 512
NH = 4
HK = 64
HV = 128
RANK = 16
TAU = 16.0
CH = 64
QSCALE = HK ** -0.5
INW = 5152
NDEV = 8

PQ, PK, PV, PZB, PUA, PVA, PZA, PG1, PG2, PLR, PW = 0, 256, 512, 1024, 1536, 2048, 2560, 3072, 4096, 5120, 5248
LRW = 128

ADAM_LR, ADAM_B1, ADAM_B2, ADAM_EPS, ADAM_WD, ADAM_STEP = 0.001, 0.9, 0.999, 1e-08, 0.01, 10

VMEM_LIMIT = 56 * 1024 * 1024
MESH = pl.DeviceIdType.MESH


def _cp(sem=None):
    return pltpu.CompilerParams(dimension_semantics=sem, vmem_limit_bytes=VMEM_LIMIT)


def _dot(a, b):
    return jnp.dot(a.astype(bf16), b.astype(bf16), preferred_element_type=f32)


def _nt(a, b):
    return lax.dot_general(a.astype(bf16), b.astype(bf16), (((1,), (1,)), ((), ())), preferred_element_type=f32)


def _tn(a, b):
    return lax.dot_general(a.astype(bf16), b.astype(bf16), (((0,), (0,)), ((), ())), preferred_element_type=f32)


def _dot_hi(a, b):
    return jnp.dot(a, b, preferred_element_type=f32, precision=lax.Precision.HIGHEST)


def _sigmoid(x):
    return 1.0 / (1.0 + jnp.exp(-x))


def _log_sigmoid(x):
    return jnp.minimum(x, 0.0) - jnp.log(1.0 + jnp.exp(-jnp.abs(x)))


def _silu_and_grad(z):
    s = _sigmoid(z)
    return z * s, s * (1.0 + z * (1.0 - s))


def _me():
    return 4 * lax.axis_index("x") + 2 * lax.axis_index("y") + lax.axis_index("c")


def _peer(k):
    x, y, c = lax.axis_index("x"), lax.axis_index("y"), lax.axis_index("c")
    px = 1 - x if k & 4 else x
    py = 1 - y if k & 2 else y
    pc = 1 - c if k & 1 else c
    return (px, py, pc), 4 * px + 2 * py + pc


def _all_gather(v, name):
    r, c = v.shape

    def body(v_ref, out_ref, send_sems, recv_sems, local_sem):
        me = _me()
        mine = pltpu.make_async_copy(v_ref, out_ref.at[me], local_sem)
        mine.start()
        sends, recvs = [], []
        for k in range(1, NDEV):
            dev, idx = _peer(k)
            sends.append(pltpu.make_async_remote_copy(
                src_ref=v_ref, dst_ref=out_ref.at[me], send_sem=send_sems.at[k - 1], recv_sem=recv_sems.at[k - 1],
                device_id=dev, device_id_type=MESH))
            recvs.append(pltpu.make_async_remote_copy(
                src_ref=v_ref, dst_ref=out_ref.at[idx], send_sem=send_sems.at[k - 1], recv_sem=recv_sems.at[k - 1],
                device_id=dev, device_id_type=MESH))
        for cp in sends:
            cp.start()
        for cp in recvs:
            cp.wait_recv()
        for cp in sends:
            cp.wait_send()
        mine.wait()

    return pl.pallas_call(
        body, name=name, out_shape=jax.ShapeDtypeStruct((NDEV, r, c), v.dtype),
        in_specs=[pl.BlockSpec(memory_space=pl.ANY)], out_specs=pl.BlockSpec(memory_space=pl.ANY),
        scratch_shapes=[pltpu.SemaphoreType.DMA((NDEV - 1,)), pltpu.SemaphoreType.DMA((NDEV - 1,)), pltpu.SemaphoreType.DMA(())],
    )(v)


def _fanout(srcs, dsts, send_sems, recv_sems, local_sems, owners=None):
    me = _me()
    n = len(srcs)
    owns = lambda a, j: True if owners is None or owners[a] is None else owners[a](j)

    def guarded(cond, fn):
        if cond is True:
            fn()
        else:
            pl.when(cond)(fn)

    def copies(with_recvs):
        local = [pltpu.make_async_copy(srcs[a](me), dsts[a](me), local_sems.at[a]) for a in range(n)]
        sends, recvs = [], []
        for k in range(1, NDEV):
            dev, idx = _peer(k)
            for a in range(n):
                s = (k - 1) * n + a
                sends.append((owns(a, idx), pltpu.make_async_remote_copy(
                    src_ref=srcs[a](idx), dst_ref=dsts[a](me), send_sem=send_sems.at[s], recv_sem=recv_sems.at[s],
                    device_id=dev, device_id_type=MESH)))
                if with_recvs:
                    recvs.append((owns(a, me), pltpu.make_async_remote_copy(
                        src_ref=srcs[a](idx), dst_ref=dsts[a](idx), send_sem=send_sems.at[s], recv_sem=recv_sems.at[s],
                        device_id=dev, device_id_type=MESH)))
        return local, sends, recvs

    def start():
        local, sends, _ = copies(False)
        for a, cp in enumerate(local):
            guarded(owns(a, me), cp.start)
        for cond, cp in sends:
            guarded(cond, cp.start)

    def finish():
        local, sends, recvs = copies(True)
        for cond, cp in recvs:
            guarded(cond, cp.wait_recv)
        for cond, cp in sends:
            guarded(cond, cp.wait_send)
        for a, cp in enumerate(local):
            guarded(owns(a, me), cp.wait)

    return start, finish


class _Comm:
    def __init__(self, ins, outs, plan):
        self.ins, self.outs, self.plan = list(ins), list(outs), plan
        self.n = len(self.outs)

    def specs(self):
        hbm = pl.BlockSpec(memory_space=pl.ANY)
        return [hbm] * len(self.ins), [hbm] * self.n, _fanout_sems(self.n) if self.n else []

    def run(self, in_refs, out_refs, sems, nsteps, compute):
        if not self.n:
            compute()
            return

        def hooks():
            srcs, dsts, owners = self.plan(in_refs, out_refs)
            return _fanout(srcs, dsts, sems[0], sems[1], sems[2], owners)

        pl.when(pl.program_id(0) == 0)(lambda: hooks()[0]())
        compute()
        pl.when(pl.program_id(0) == nsteps - 1)(lambda: hooks()[1]())


LATE_MID_STEP = 2
LATE_ADD_ROWS = 256


class _LateComm:
    def __init__(self, blocks, dgt):
        sds = jax.ShapeDtypeStruct
        self.ins = [blocks, dgt]
        self.outs = [sds((D, SHARD), bf16), sds((D, SHARD), bf16), sds((4, D, SHARD), bf16), sds(dgt.shape, f32)]
        self.n = len(self.outs)

    def specs(self):
        hbm = pl.BlockSpec(memory_space=pl.ANY)
        scratch = [pltpu.VMEM((3, LATE_ADD_ROWS, SHARD), bf16), pltpu.SemaphoreType.DMA((2,)), pltpu.SemaphoreType.DMA((4,)),
                   pltpu.SemaphoreType.DMA(())] + _fanout_sems(1)
        return [hbm] * 2, [hbm] * self.n, scratch

    def run(self, in_refs, out_refs, scratch, nsteps, compute):
        late_ref, dgt_ref = in_refs
        sib_ref, pair_ref, parts_ref, gt_ref = out_refs
        vbuf, send_sems, recv_sems, local_sem, g_send, g_recv, g_local = scratch
        x, y, c = lax.axis_index("x"), lax.axis_index("y"), lax.axis_index("c")
        chip = 2 * x + y
        is_owner_chip = chip == 0
        step = pl.program_id(0)

        def to_sibling():
            return pltpu.make_async_remote_copy(src_ref=late_ref.at[1 - c], dst_ref=sib_ref, send_sem=send_sems.at[0],
                                                recv_sem=recv_sems.at[0], device_id=(x, y, 1 - c), device_id_type=MESH)

        def to_owner(k):
            return pltpu.make_async_remote_copy(src_ref=pair_ref, dst_ref=parts_ref.at[k], send_sem=send_sems.at[1],
                                                recv_sem=recv_sems.at[k], device_id=(0, 0, c), device_id_type=MESH)

        def own_copy():
            return pltpu.make_async_copy(pair_ref, parts_ref.at[0], local_sem)

        def gates():
            return _fanout([lambda j: dgt_ref.at[j]], [lambda j: gt_ref.at[j]], g_send, g_recv, g_local)

        @pl.when(step == 0)
        def _():
            to_sibling().start()
            gates()[0]()

        compute()

        @pl.when(step == LATE_MID_STEP)
        def _():
            to_sibling().wait_recv()
            for r0 in range(0, D, LATE_ADD_ROWS):
                rows = pl.ds(r0, LATE_ADD_ROWS)
                pltpu.sync_copy(late_ref.at[c, rows], vbuf.at[0])
                pltpu.sync_copy(sib_ref.at[rows], vbuf.at[1])
                vbuf[2] = (vbuf[0].astype(f32) + vbuf[1].astype(f32)).astype(bf16)
                pltpu.sync_copy(vbuf.at[2], pair_ref.at[rows])
            pl.when(is_owner_chip)(lambda: own_copy().start())
            pl.when(jnp.logical_not(is_owner_chip))(lambda: to_owner(chip).start())

        @pl.when(step == nsteps - 1)
        def _():
            @pl.when(is_owner_chip)
            def _():
                for k in range(1, 4):
                    to_owner(k).wait_recv()
                own_copy().wait()

            pl.when(jnp.logical_not(is_owner_chip))(lambda: to_owner(chip).wait_send())
            to_sibling().wait_send()
            gates()[1]()


def _fanout_sems(n):
    return [pltpu.SemaphoreType.DMA(((NDEV - 1) * n,)), pltpu.SemaphoreType.DMA(((NDEV - 1) * n,)), pltpu.SemaphoreType.DMA((n,))]


def _lanes(j):
    return pl.ds(pl.multiple_of(j * 128, 128), 128)


def _gather_first(c_row, cctx_row, wm, bm, wi, gate):
    def body(c_ref, cctx_ref, wm_ref, bm_ref, wi_ref, g_ref, cs_ref, mods_ref, owi, og, call_ref, mine_ref,
             send_sems, recv_sems, local_sems, c_send, c_recv, c_local, m_send, m_recv, m_local):
        x, y, c = lax.axis_index("x"), lax.axis_index("y"), lax.axis_index("c")
        sibling = (x, y, 1 - c)
        chips = [(1 - x, y), (x, 1 - y), (1 - x, 1 - y)]
        index = lambda px, py, pc: 4 * px + 2 * py + pc
        arrays = ((wi_ref, owi), (g_ref, og))
        n = len(arrays)

        def copy(a, k, block, to, own=False):
            src, out = arrays[a]
            return pltpu.make_async_remote_copy(
                src_ref=src if own else out.at[index(*block)], dst_ref=out.at[index(*block)],
                send_sem=send_sems.at[k * n + a], recv_sem=recv_sems.at[k * n + a], device_id=to, device_id_type=MESH)

        mine = [pltpu.make_async_copy(src, out.at[index(x, y, c)], local_sems.at[a]) for a, (src, out) in enumerate(arrays)]
        first = [copy(a, 0, (x, y, c), sibling, own=True) for a in range(n)]
        first += [copy(a, 1 + j, (x, y, c), (*chip, c), own=True) for j, chip in enumerate(chips) for a in range(n)]
        for cp in mine + first:
            cp.start()

        c_start, c_finish = _fanout([lambda j: c_ref], [lambda j: call_ref.at[j]], c_send, c_recv, c_local)
        c_start()
        c_finish()
        cs = jnp.concatenate([call_ref[j] for j in range(NDEV)] + [cctx_ref[...], jnp.zeros((16 - NDEV - 1, D), f32)], axis=0)
        cs_ref[...] = cs
        s, _ = _silu_and_grad(cs)
        mine_ref[...] = _dot_hi(s, wm_ref[...]) + bm_ref[...]
        m_start, m_finish = _fanout([lambda j: mine_ref], [lambda j: mods_ref.at[j]], m_send, m_recv, m_local)
        m_start()

        passed = []
        for j, chip in enumerate(chips):
            for a in range(n):
                copy(a, 1 + j, (*chip, c), (x, y, c)).wait_recv()
            for a in range(n):
                cp = copy(a, 4 + j, (*chip, c), sibling)
                cp.start()
                passed.append(cp)
        for a in range(n):
            copy(a, 0, sibling, (x, y, c)).wait_recv()
        for j, chip in enumerate(chips):
            for a in range(n):
                copy(a, 4 + j, (*chip, 1 - c), (x, y, c)).wait_recv()
        for cp in first + passed:
            cp.wait_send()
        for cp in mine:
            cp.wait()
        m_finish()

    hbm = pl.BlockSpec(memory_space=pl.ANY)
    vm = pl.BlockSpec(memory_space=pltpu.VMEM)
    ncol = wm.shape[1]
    return pl.pallas_call(
        body, name="gather_first",
        out_shape=(jax.ShapeDtypeStruct((16, D), f32), jax.ShapeDtypeStruct((NDEV, 16, ncol), f32),
                   jax.ShapeDtypeStruct((NDEV,) + wi.shape, bf16), jax.ShapeDtypeStruct((NDEV,) + gate.shape, f32)),
        in_specs=[vm, vm, vm, vm, hbm, hbm], out_specs=(vm, vm, hbm, hbm),
        scratch_shapes=[pltpu.VMEM((NDEV, 1, D), f32), pltpu.VMEM((16, ncol), f32)] + _fanout_sems(2) + _fanout_sems(1) + _fanout_sems(1),
        compiler_params=_cp(),
    )(c_row, cctx_row, wm, bm, wi, gate)


SHARD = INW // NDEV
ROWS_RP = 128


def _overlap(lo, hi, a, b):
    s, e = max(lo, a), min(hi, b)
    return (s, e) if s < e else None


def _repack_w(wg):
    segs = ((0, 1024, PQ), (1024, 1024 + 2 * RANK, PLR), (1024 + 2 * RANK, INW, PZB))

    def body(g_ref, o_ref):
        for j in range(NDEV):
            lo, hi = j * SHARD, (j + 1) * SHARD
            for a, b, pad0 in segs:
                ov = _overlap(lo, hi, a, b)
                if ov:
                    s, e = ov
                    o_ref[:, pad0 + s - a:pad0 + e - a] = g_ref[j, :, s - lo:e - lo]
        o_ref[:, PLR + 2 * RANK:PW] = jnp.zeros((ROWS_RP, PW - PLR - 2 * RANK), bf16)

    return pl.pallas_call(
        body, name="repack_w", grid=(D // ROWS_RP,), out_shape=jax.ShapeDtypeStruct((D, PW), bf16),
        in_specs=[pl.BlockSpec((NDEV, ROWS_RP, SHARD), lambda i: (0, i, 0))],
        out_specs=pl.BlockSpec((ROWS_RP, PW), lambda i: (i, 0)), compiler_params=_cp(("arbitrary",)),
    )(wg)


LATE_END = 1024 + 2 * RANK
LATE_DESTS = 2


def _pack_blocks(o_ref, srcs, dests, dtype):
    for n, j in enumerate(dests):
        lo, hi = j * SHARD, (j + 1) * SHARD
        done = lo
        for a, b, src in srcs:
            ov = _overlap(lo, hi, a, b)
            if ov:
                s, e = ov
                if s > done:
                    o_ref[n, :, done - lo:s - lo] = jnp.zeros((ROWS_RP, s - done), dtype)
                o_ref[n, :, s - lo:e - lo] = src[:, s - a:e - a].astype(dtype)
                done = e
        if done < hi:
            o_ref[n, :, done - lo:hi - lo] = jnp.zeros((ROWS_RP, hi - done), dtype)


def _pack_dw_early(dw_zbua, dw_va, dw_zag, row0, nrows, name):
    def body(zbua_ref, va_ref, zag_ref, o_ref):
        _pack_blocks(o_ref, ((LATE_END, 2080, zbua_ref), (2080, 2592, va_ref), (2592, INW, zag_ref)), range(NDEV), bf16)

    row = lambda w: pl.BlockSpec((ROWS_RP, w), lambda i: (i + row0 // ROWS_RP, 0))
    return pl.pallas_call(
        body, name=name, grid=(nrows // ROWS_RP,), out_shape=jax.ShapeDtypeStruct((NDEV, nrows, SHARD), bf16),
        in_specs=[row(2 * AW), row(AW), row(AW + 2 * D)],
        out_specs=pl.BlockSpec((NDEV, ROWS_RP, SHARD), lambda i: (0, i, 0)), compiler_params=_cp(("arbitrary",)),
    )(dw_zbua, dw_va, dw_zag)


def _pack_dw_late(dw_qkv, dw_lr, dwk_c, dwv_c, dwl_c):
    def body(qkv_ref, lr_ref, kc_ref, vc_ref, lc_ref, o_ref):
        qkv = qkv_ref[...] + jnp.concatenate([jnp.zeros((ROWS_RP, KW), f32), kc_ref[...], vc_ref[...]], axis=1)
        lr = lr_ref[...] + lc_ref[...]
        _pack_blocks(o_ref, ((0, 1024, qkv), (1024, LATE_END, lr)), range(LATE_DESTS), bf16)

    row = lambda w: pl.BlockSpec((ROWS_RP, w), lambda i: (i, 0))
    return pl.pallas_call(
        body, name="pack_dw_late", grid=(D // ROWS_RP,), out_shape=jax.ShapeDtypeStruct((LATE_DESTS, D, SHARD), bf16),
        in_specs=[row(2 * KW + VW), row(LRW), row(KW), row(VW), row(LRW)],
        out_specs=pl.BlockSpec((LATE_DESTS, ROWS_RP, SHARD), lambda i: (0, i, 0)), compiler_params=_cp(("arbitrary",)),
    )(dw_qkv, dw_lr, dwk_c, dwv_c, dwl_c)


def _mod_bwd(cs, dm, wm):
    def body(cs_ref, dm_ref, wm_ref, gw_ref, gcs_ref, gc_ref, send_sems, recv_sems, local_sems):
        s, ds = _silu_and_grad(cs_ref[...])
        part = lax.dot_general(dm_ref[8:9, :], wm_ref[...], (((1,), (1,)), ((), ())), preferred_element_type=f32,
                               precision=lax.Precision.HIGHEST)
        gc_ref[...] = part * ds[8:9, :]
        start, finish = _fanout([lambda j: gc_ref], [lambda j: gcs_ref.at[j]], send_sems, recv_sems, local_sems)
        start()
        gw_ref[...] = lax.dot_general(s, dm_ref[...], (((0,), (0,)), ((), ())), preferred_element_type=f32,
                                      precision=lax.Precision.HIGHEST)
        finish()

    return pl.pallas_call(body, name="mod_bwd",
                          out_shape=(jax.ShapeDtypeStruct(wm.shape, f32), jax.ShapeDtypeStruct((NDEV, 1, D), f32)),
                          scratch_shapes=[pltpu.VMEM((1, D), f32)] + _fanout_sems(1),
                          compiler_params=_cp())(cs, dm, wm)


def _adam_update(g, w_ref, m_ref, v_ref, go_ref, d_ref, mo_ref, vo_ref):
    c1 = 1.0 / (1.0 - ADAM_B1 ** ADAM_STEP)
    c2 = 1.0 / (1.0 - ADAM_B2 ** ADAM_STEP)
    mn = ADAM_B1 * m_ref[...] + (1.0 - ADAM_B1) * g
    vn = ADAM_B2 * v_ref[...] + (1.0 - ADAM_B2) * (g * g)
    go_ref[...] = g
    mo_ref[...] = mn
    vo_ref[...] = vn
    d_ref[...] = -ADAM_LR * ((mn * c1) / (jnp.sqrt(vn * c2) + ADAM_EPS) + ADAM_WD * w_ref[...])


def _adam_w_in(parts_a, parts_b, parts_late, w, m, v):
    na = EARLY_A_ROWS // ROWS_RP

    def body(a_ref, b_ref, l_ref, w_ref, m_ref, v_ref, go_ref, d_ref, mo_ref, vo_ref):
        me = _me()
        first = pl.program_id(0) < na
        early = jnp.where(first, a_ref[0], b_ref[0]).astype(f32)
        for i in range(1, NDEV):
            early = early + jnp.where(first, a_ref[i], b_ref[i]).astype(f32)
        late = l_ref[0].astype(f32)
        for i in range(1, 4):
            late = late + l_ref[i].astype(f32)
        g = jnp.where(me >= LATE_DESTS - 1, early, 0.0) + jnp.where(me < LATE_DESTS, late, 0.0)
        _adam_update(g, w_ref, m_ref, v_ref, go_ref, d_ref, mo_ref, vo_ref)

    blk = pl.BlockSpec((None, ROWS_RP, SHARD), lambda i: (0, i, 0))
    return pl.pallas_call(
        body, name="adam_w_in", grid=(D // ROWS_RP,), out_shape=tuple(jax.ShapeDtypeStruct((1, D, SHARD), f32) for _ in range(4)),
        in_specs=[pl.BlockSpec((NDEV, ROWS_RP, SHARD), lambda i: (0, jnp.minimum(i, na - 1), 0)),
                  pl.BlockSpec((NDEV, ROWS_RP, SHARD), lambda i: (0, jnp.maximum(i - na, 0), 0)),
                  pl.BlockSpec((4, ROWS_RP, SHARD), lambda i: (0, i, 0)), blk, blk, blk],
        out_specs=(blk, blk, blk, blk), compiler_params=_cp(("arbitrary",)),
    )(parts_a, parts_b, parts_late, w, m, v)


def _adam_params(parts, ws, ms, vs, name):
    p = parts.shape[0]
    k = len(ws)
    nrows = [w.size // 128 for w in ws]
    starts = [sum(nrows[:i]) for i in range(k)]

    def shaped(g, shape):
        if len(shape) == 2:
            return jnp.concatenate([g[r:r + 1] for r in range(g.shape[0])], axis=1)
        return g.reshape(shape)

    def body(*refs):
        g_ref = refs[0]
        w_refs, m_refs, v_refs = refs[1:1 + k], refs[1 + k:1 + 2 * k], refs[1 + 2 * k:1 + 3 * k]
        outs = refs[1 + 3 * k:]
        for i in range(k):
            rows = slice(starts[i], starts[i] + nrows[i])
            g = g_ref[0, rows, :]
            for j in range(1, p):
                g = g + g_ref[j, rows, :]
            _adam_update(shaped(g, ws[i].shape), w_refs[i], m_refs[i], v_refs[i], outs[i], outs[k + i], outs[2 * k + i], outs[3 * k + i])

    vm = pl.BlockSpec(memory_space=pltpu.VMEM)
    res = pl.pallas_call(
        body, name=name, out_shape=tuple(jax.ShapeDtypeStruct(w.shape, f32) for _ in range(4) for w in ws),
        in_specs=[vm] * (1 + 3 * k), out_specs=(vm,) * (4 * k), compiler_params=_cp(),
    )(parts, *ws, *ms, *vs)
    return [res[i * k:(i + 1) * k] for i in range(4)]


def _adam(parts, w, m, v, rows, name):
    p, r, c = parts.shape
    lead = w.ndim - 2

    def body(g_ref, w_ref, m_ref, v_ref, go_ref, d_ref, mo_ref, vo_ref):
        g = g_ref[0].astype(f32)
        for i in range(1, p):
            g = g + g_ref[i].astype(f32)
        _adam_update(g, w_ref, m_ref, v_ref, go_ref, d_ref, mo_ref, vo_ref)

    blk = pl.BlockSpec((None,) * lead + (rows, c), lambda i: (0,) * lead + (i, 0))
    return pl.pallas_call(
        body, name=name, grid=(r // rows,), out_shape=tuple(jax.ShapeDtypeStruct(w.shape, f32) for _ in range(4)),
        in_specs=[pl.BlockSpec((p, rows, c), lambda i: (0, i, 0)), blk, blk, blk], out_specs=(blk, blk, blk, blk),
        compiler_params=_cp(("arbitrary",)),
    )(parts, w, m, v)


def _in_proj(x, g, scale, shift, w_pad, tl, comm):
    l = x.shape[0]
    c_in, c_out, c_sems = comm.specs()

    def body(*refs):
        x_ref, g_ref, sc_ref, sh_ref, w_ref = refs[:5]
        cin = refs[5:5 + len(c_in)]
        p_ref, h_ref = refs[5 + len(c_in):7 + len(c_in)]
        cout = refs[7 + len(c_in):7 + len(c_in) + comm.n]
        sems = refs[7 + len(c_in) + comm.n:]

        def compute():
            xv = x_ref[...]
            r = lax.rsqrt(jnp.mean(xv * xv, axis=-1, keepdims=True) + EPS)
            h = (xv * r) * (g_ref[...] * (1.0 + sc_ref[...])) + sh_ref[...]
            hb = h.astype(bf16)
            h_ref[...] = hb
            p_ref[...] = jnp.dot(hb, w_ref[...], preferred_element_type=f32).astype(bf16)

        comm.run(cin, cout, sems, l // tl, compute)

    vec = pl.BlockSpec((1, D), lambda i: (0, 0))
    return pl.pallas_call(
        body, name="in_proj", grid=(l // tl,),
        out_shape=(jax.ShapeDtypeStruct((l, PW), bf16), jax.ShapeDtypeStruct((l, D), bf16)) + tuple(comm.outs),
        in_specs=[pl.BlockSpec((tl, D), lambda i: (i, 0)), vec, vec, vec, pl.BlockSpec((D, PW), lambda i: (0, 0))] + c_in,
        out_specs=(pl.BlockSpec((tl, PW), lambda i: (i, 0)), pl.BlockSpec((tl, D), lambda i: (i, 0))) + tuple(c_out),
        scratch_shapes=c_sems, compiler_params=_cp(("arbitrary",)),
    )(x, g, scale, shift, w_pad, *comm.ins)


def _tri(rev):
    i = lax.broadcasted_iota(jnp.int32, (CH, CH), 0)
    j = lax.broadcasted_iota(jnp.int32, (CH, CH), 1)
    return jnp.where((j >= i) if rev else (j <= i), 1.0, 0.0).astype(f32)


def _head_masks():
    lane = lax.broadcasted_iota(jnp.int32, (1, KW), 1) // HK
    return [jnp.where(lane == h, 1.0, 0.0).astype(f32) for h in range(NH)]


def _block_diag():
    r = lax.broadcasted_iota(jnp.int32, (VW, KW), 0) // HV
    c = lax.broadcasted_iota(jnp.int32, (VW, KW), 1) // HK
    return jnp.where(r == c, 1.0, 0.0).astype(f32)


def _decay(lr, w2, gb, tri, rev):
    logits = _dot(lr, w2) + gb
    a = _log_sigmoid(logits) * (1.0 / TAU)
    c = _dot_hi(tri, a)
    cl = c[0:1, :] if rev else c[CH - 1:CH, :]
    return logits, c, cl


def _stack_heads(t, hm):
    return jnp.concatenate([t * hm[h] for h in range(NH)], axis=0)


def _chunk_fwd(q, k, v, c, cl, st, tri, hm, bd):
    tri4 = jnp.concatenate([tri] * NH, axis=0)
    qd = q * jnp.exp(c) * QSCALE
    kd = k * jnp.exp(-c)
    kdec = k * jnp.exp(cl - c)
    pst = _nt(_stack_heads(qd, hm), kd) * tri4
    intra = jnp.concatenate([_dot(pst[h * CH:(h + 1) * CH], v[:, h * HV:(h + 1) * HV]) for h in range(NH)], axis=1)
    o = _nt(qd, st) + intra
    st_new = st * jnp.exp(cl) + bd * _tn(v, kdec)
    return o, st_new


def _state_fwd(k, v, c, cl, st, bd):
    return st * jnp.exp(cl) + bd * _tn(v, k * jnp.exp(cl - c))


def _chunk_bwd(q, k, v, c, cl, st0, dst, do, tri, trit, hm, bd):
    ecl = jnp.exp(cl)
    edec = jnp.exp(cl - c)
    kdec = k * edec
    dv = _nt(kdec, dst)
    dkdec = _dot(v, dst)
    dcl = jnp.sum(dst * st0, axis=0, keepdims=True) * ecl + jnp.sum(dkdec * kdec, axis=0, keepdims=True)
    dc = -dkdec * kdec
    dk = dkdec * edec
    dst0 = dst * ecl
    dq = None
    if q is not None:
        tri4 = jnp.concatenate([tri] * NH, axis=0)
        ec, enc = jnp.exp(c), jnp.exp(-c)
        qd = q * ec * QSCALE
        kd = k * enc
        qs = _stack_heads(qd, hm)
        pst = _nt(qs, kd) * tri4
        dpst = jnp.concatenate([_nt(do[:, h * HV:(h + 1) * HV], v[:, h * HV:(h + 1) * HV]) for h in range(NH)], axis=0) * tri4
        dv = dv + jnp.concatenate([_tn(pst[h * CH:(h + 1) * CH], do[:, h * HV:(h + 1) * HV]) for h in range(NH)], axis=1)
        dqd = _dot(do, st0)
        for h in range(NH):
            dqd = dqd + hm[h] * _dot(dpst[h * CH:(h + 1) * CH], kd)
        dkd = _tn(dpst, qs)
        dst0 = dst0 + bd * _tn(do, qd)
        dc = dc + dqd * qd - dkd * kd
        dq = dqd * ec * QSCALE
        dk = dk + dkd * enc
    da = _dot_hi(trit, dc) + dcl
    return dq, dk, dv, da, dst0


def _bdot(a, b):
    return lax.dot_general(a.astype(bf16), b.astype(bf16), (((2,), (1,)), ((0,), (0,))), preferred_element_type=f32)


def _bnt(a, b):
    return lax.dot_general(a.astype(bf16), b.astype(bf16), (((2,), (2,)), ((0,), (0,))), preferred_element_type=f32)


def _btn(a, b):
    return lax.dot_general(a.astype(bf16), b.astype(bf16), (((1,), (1,)), ((0,), (0,))), preferred_element_type=f32)


def _scan_chunks(x, rev):
    nc = x.shape[0]
    hi = x.astype(bf16)
    r1 = x - hi.astype(f32)
    mid = r1.astype(bf16)
    lo = (r1 - mid.astype(f32)).astype(bf16)
    terms = jnp.concatenate([hi, mid, lo], axis=1)
    tri3 = jnp.broadcast_to(jnp.concatenate([_tri(rev)] * 3, axis=1).astype(bf16)[None], (nc, CH, 3 * CH))
    return lax.dot_general(tri3, terms, (((2,), (1,)), ((0,), (0,))), preferred_element_type=f32)


class _Tile:
    pass


def _tile_prep(q_ref, k_ref, lr_ref, w2_ref, gb_ref, rev, nc):
    t = _Tile()
    tg = nc * CH
    t.logits = _dot(lr_ref[...], w2_ref[...]) + gb_ref[...]
    c = _scan_chunks((_log_sigmoid(t.logits) * (1.0 / TAU)).reshape(nc, CH, KW), rev)
    cl = c[:, 0:1, :] if rev else c[:, CH - 1:CH, :]
    k = k_ref[...].astype(f32).reshape(nc, CH, KW)
    t.ec, t.enc, t.edec, t.ecl = jnp.exp(c), jnp.exp(-c), jnp.exp(cl - c), jnp.exp(cl)
    t.qd = q_ref[...].astype(f32).reshape(nc, CH, KW) * t.ec * QSCALE
    t.kd = k * t.enc
    t.kdec = k * t.edec
    hm = _head_masks()
    t.tri4 = jnp.concatenate([_tri(rev)] * NH, axis=0)[None]
    t.qs = jnp.concatenate([t.qd * hm[h] for h in range(NH)], axis=1)
    t.pst = _bnt(t.qs, t.kd) * t.tri4
    return t


def _gla_fwd(p, w2p, gb, s0, rev, tg, name):
    l = p.shape[0]
    nb, nc = l // tg, tg // CH

    def body(q_ref, k_ref, v_ref, lr_ref, w2_ref, gb_ref, s0_ref, o_ref, st_ref, st):
        @pl.when(pl.program_id(0) == 0)
        def _():
            st[...] = s0_ref[...]

        t = _tile_prep(q_ref, k_ref, lr_ref, w2_ref, gb_ref, rev, nc)
        v = v_ref[...].reshape(nc, CH, VW)
        intra = jnp.concatenate([_bdot(t.pst[:, h * CH:(h + 1) * CH], v[:, :, h * HV:(h + 1) * HV]) for h in range(NH)], axis=2)
        kv = _btn(v, t.kdec) * _block_diag()[None]
        s = st[...]
        for n in (range(nc - 1, -1, -1) if rev else range(nc)):
            st_ref[n] = s
            s = s * t.ecl[n] + kv[n]
        st[...] = s
        o_ref[...] = (_bnt(t.qd, st_ref[...]) + intra).reshape(tg, VW)

    blk = (lambda i: nb - 1 - i) if rev else (lambda i: i)
    return pl.pallas_call(
        body, name=name, grid=(nb,),
        out_shape=(jax.ShapeDtypeStruct((l, VW), f32), jax.ShapeDtypeStruct((l // CH, VW, KW), f32)),
        in_specs=[pl.BlockSpec((tg, KW), lambda i: (blk(i), PQ // KW)), pl.BlockSpec((tg, KW), lambda i: (blk(i), PK // KW)),
                  pl.BlockSpec((tg, VW), lambda i: (blk(i), PV // VW)), pl.BlockSpec((tg, LRW), lambda i: (blk(i), PLR // LRW)),
                  pl.BlockSpec((LRW, KW), lambda i: (0, 0)), pl.BlockSpec((1, KW), lambda i: (0, 0)),
                  pl.BlockSpec((VW, KW), lambda i: (0, 0))],
        out_specs=(pl.BlockSpec((tg, VW), lambda i: (blk(i), 0)), pl.BlockSpec((nc, VW, KW), lambda i: (blk(i), 0, 0))),
        scratch_shapes=[pltpu.VMEM((VW, KW), f32)],
        compiler_params=_cp(("arbitrary",)),
    )(p, p, p, p, w2p, gb, s0)


def _gla_bwd(p, do, states, w2p, gb, prev, rev, tg, name, comm):
    l = p.shape[0]
    nb, nc = l // tg, tg // CH
    out_dt = f32 if prev is None else bf16
    c_in, c_out, c_sems = comm.specs()

    def body(*refs):
        q_ref, k_ref, v_ref, lr_ref, do_ref, st_ref, w2_ref, gb_ref = refs[:8]
        refs = refs[8:]
        if prev is not None:
            pq_ref, pl_ref = refs[:2]
            refs = refs[2:]
        cin, refs = refs[:len(c_in)], refs[len(c_in):]
        dqkv_ref, dlr_ref, dw2_ref, dgb_ref, ds0_ref = refs[:5]
        cout, dst, ds_buf, sems = refs[5:5 + comm.n], refs[5 + comm.n], refs[6 + comm.n], refs[7 + comm.n:]
        comm.run(cin, cout, sems, nb, lambda: compute(q_ref, k_ref, v_ref, lr_ref, do_ref, st_ref, w2_ref, gb_ref,
                                                      pq_ref if prev is not None else None, pl_ref if prev is not None else None,
                                                      dqkv_ref, dlr_ref, dw2_ref, dgb_ref, ds0_ref, dst, ds_buf))

    def compute(q_ref, k_ref, v_ref, lr_ref, do_ref, st_ref, w2_ref, gb_ref, pq_ref, pl_ref,
                dqkv_ref, dlr_ref, dw2_ref, dgb_ref, ds0_ref, dst, ds_buf):
        @pl.when(pl.program_id(0) == 0)
        def _():
            dst[...] = jnp.zeros_like(dst)
            dw2_ref[...] = jnp.zeros_like(dw2_ref)
            dgb_ref[...] = jnp.zeros_like(dgb_ref)

        t = _tile_prep(q_ref, k_ref, lr_ref, w2_ref, gb_ref, rev, nc)
        hm = _head_masks()
        v = v_ref[...].reshape(nc, CH, VW)
        do = do_ref[...].reshape(nc, CH, VW)
        heads = lambda a, h: a[:, :, h * HV:(h + 1) * HV]
        dpst = jnp.concatenate([_bnt(heads(do, h), heads(v, h)) for h in range(NH)], axis=1) * t.tri4
        dv = jnp.concatenate([_btn(t.pst[:, h * CH:(h + 1) * CH], heads(do, h)) for h in range(NH)], axis=2)
        dqd = _bdot(do, st_ref[...])
        for h in range(NH):
            dqd = dqd + hm[h] * _bdot(dpst[:, h * CH:(h + 1) * CH], t.kd)
        dkd = _btn(dpst, t.qs)
        u = _btn(do, t.qd) * _block_diag()[None]
        d = dst[...]
        for n in (range(nc) if rev else range(nc - 1, -1, -1)):
            ds_buf[n] = d
            d = d * t.ecl[n] + u[n]
        dst[...] = d
        ds0_ref[...] = d
        ds = ds_buf[...]
        dv = dv + _bnt(t.kdec, ds)
        dkdec = _bdot(v, ds)
        dcl = jnp.sum(ds * st_ref[...], axis=1, keepdims=True) * t.ecl + jnp.sum(dkdec * t.kdec, axis=1, keepdims=True)
        dc = dqd * t.qd - dkd * t.kd - dkdec * t.kdec
        da = _scan_chunks(dc, not rev) + dcl
        dq = dqd * t.ec * QSCALE
        dk = dkd * t.enc + dkdec * t.edec
        dlog = da.reshape(tg, KW) * _sigmoid(-t.logits) * (1.0 / TAU)
        dlr = _nt(dlog, w2_ref[...])
        dw2_ref[...] += _tn(lr_ref[...], dlog)
        dgb_ref[...] += jnp.sum(dlog, axis=0, keepdims=True)
        dqkv = jnp.concatenate([dq, dk, dv], axis=2).reshape(tg, 2 * KW + VW)
        if prev is not None:
            dqkv = dqkv + pq_ref[...]
            dlr = dlr + pl_ref[...]
        dqkv_ref[...] = dqkv.astype(out_dt)
        dlr_ref[...] = dlr.astype(out_dt)

    blk = (lambda i: i) if rev else (lambda i: nb - 1 - i)
    in_specs = [pl.BlockSpec((tg, KW), lambda i: (blk(i), PQ // KW)), pl.BlockSpec((tg, KW), lambda i: (blk(i), PK // KW)),
                pl.BlockSpec((tg, VW), lambda i: (blk(i), PV // VW)), pl.BlockSpec((tg, LRW), lambda i: (blk(i), PLR // LRW)),
                pl.BlockSpec((tg, VW), lambda i: (blk(i), 0)), pl.BlockSpec((nc, VW, KW), lambda i: (blk(i), 0, 0)),
                pl.BlockSpec((LRW, KW), lambda i: (0, 0)), pl.BlockSpec((1, KW), lambda i: (0, 0))]
    args = [p, p, p, p, do, states, w2p, gb]
    if prev is not None:
        in_specs += [pl.BlockSpec((tg, 2 * KW + VW), lambda i: (blk(i), 0)), pl.BlockSpec((tg, LRW), lambda i: (blk(i), 0))]
        args += list(prev)
    return pl.pallas_call(
        body, name=name, grid=(nb,),
        out_shape=(jax.ShapeDtypeStruct((l, 2 * KW + VW), out_dt), jax.ShapeDtypeStruct((l, LRW), out_dt),
                   jax.ShapeDtypeStruct((LRW, KW), f32), jax.ShapeDtypeStruct((1, KW), f32), jax.ShapeDtypeStruct((VW, KW), f32))
        + tuple(comm.outs),
        in_specs=in_specs + c_in,
        out_specs=(pl.BlockSpec((tg, 2 * KW + VW), lambda i: (blk(i), 0)), pl.BlockSpec((tg, LRW), lambda i: (blk(i), 0)),
                   pl.BlockSpec((LRW, KW), lambda i: (0, 0)), pl.BlockSpec((1, KW), lambda i: (0, 0)),
                   pl.BlockSpec((VW, KW), lambda i: (0, 0))) + tuple(c_out),
        scratch_shapes=[pltpu.VMEM((VW, KW), f32), pltpu.VMEM((nc, VW, KW), f32)] + c_sems,
        compiler_params=_cp(("arbitrary",)),
    )(*args, *comm.ins)


def _ctx_hidden(ctx_ref, g_ref, sc_ref, sh_ref):
    xv = ctx_ref[...]
    r = lax.rsqrt(jnp.mean(xv * xv, axis=-1, keepdims=True) + EPS)
    xn = xv * r
    return xn, xn * (g_ref[...] * (1.0 + sc_ref[...])) + sh_ref[...]


_CTX_W_SPECS = [pl.BlockSpec((D, KW), lambda i: (0, PK // KW)), pl.BlockSpec((D, VW), lambda i: (0, PV // VW)),
                pl.BlockSpec((D, LRW), lambda i: (0, PLR // LRW))]


def _ctx_fwd(ctx, g, scale, shift, w_pad, w2f, w2b, gbf, gbb):
    ncc = CTX // CH

    def body(ctx_ref, g_ref, sc_ref, sh_ref, wk_ref, wv_ref, wl_ref, w2f_ref, w2b_ref, gbf_ref, gbb_ref, sf_ref, sb_ref):
        _, hc = _ctx_hidden(ctx_ref, g_ref, sc_ref, sh_ref)
        k, v, lr = _dot(hc, wk_ref[...]), _dot(hc, wv_ref[...]), _dot(hc, wl_ref[...])
        bd = _block_diag()
        for rev, w2_ref, gb_ref, out in ((False, w2f_ref, gbf_ref, sf_ref), (True, w2b_ref, gbb_ref, sb_ref)):
            tri = _tri(rev)
            st = jnp.zeros((VW, KW), f32)
            for j in (range(ncc - 1, -1, -1) if rev else range(ncc)):
                rows = slice(j * CH, (j + 1) * CH)
                _, c, cl = _decay(lr[rows], w2_ref[...], gb_ref[...], tri, rev)
                st = _state_fwd(k[rows], v[rows], c, cl, st, bd)
            out[...] = st

    vec = pl.BlockSpec((1, D), lambda i: (0, 0))
    w2s = pl.BlockSpec((LRW, KW), lambda i: (0, 0))
    gbs = pl.BlockSpec((1, KW), lambda i: (0, 0))
    sts = pl.BlockSpec((VW, KW), lambda i: (0, 0))
    return pl.pallas_call(
        body, name="ctx_fwd", grid=(1,), out_shape=(jax.ShapeDtypeStruct((VW, KW), f32),) * 2,
        in_specs=[pl.BlockSpec((CTX, D), lambda i: (0, 0)), vec, vec, vec] + _CTX_W_SPECS + [w2s, w2s, gbs, gbs],
        out_specs=(sts, sts), compiler_params=_cp(("arbitrary",)),
    )(ctx, g, scale, shift, w_pad, w_pad, w_pad, w2f, w2b, gbf, gbb)


def _ctx_bwd(ctx, g, scale, shift, w_pad, w2f, w2b, gbf, gbb, dsf, dsb):
    ncc = CTX // CH

    def body(ctx_ref, g_ref, sc_ref, sh_ref, wk_ref, wv_ref, wl_ref, w2f_ref, w2b_ref, gbf_ref, gbb_ref, dsf_ref, dsb_ref,
             dwk_ref, dwv_ref, dwl_ref, dmod_ref, dg_ref, dw2_ref, dgb_ref):
        xn, hc = _ctx_hidden(ctx_ref, g_ref, sc_ref, sh_ref)
        k, v, lr = _dot(hc, wk_ref[...]), _dot(hc, wv_ref[...]), _dot(hc, wl_ref[...])
        bd = _block_diag()
        dk_rows, dv_rows, dl_rows = [None] * ncc, [None] * ncc, [None] * ncc
        for d, (rev, w2_ref, gb_ref, ds_ref) in enumerate(((False, w2f_ref, gbf_ref, dsf_ref), (True, w2b_ref, gbb_ref, dsb_ref))):
            tri = _tri(rev)
            order = list(range(ncc - 1, -1, -1) if rev else range(ncc))
            st, saved = jnp.zeros((VW, KW), f32), {}
            for j in order:
                rows = slice(j * CH, (j + 1) * CH)
                logits, c, cl = _decay(lr[rows], w2_ref[...], gb_ref[...], tri, rev)
                saved[j] = (logits, c, cl, st)
                st = _state_fwd(k[rows], v[rows], c, cl, st, bd)
            dst = ds_ref[...]
            dw2 = jnp.zeros((LRW, KW), f32)
            dgb = jnp.zeros((1, KW), f32)
            for j in reversed(order):
                rows = slice(j * CH, (j + 1) * CH)
                logits, c, cl, st0 = saved[j]
                _, dk, dv, da, dst = _chunk_bwd(None, k[rows], v[rows], c, cl, st0, dst, None, tri, _tri(not rev), None, bd)
                dlog = da * _sigmoid(-logits) * (1.0 / TAU)
                dl = _nt(dlog, w2_ref[...])
                dw2 = dw2 + _tn(lr[rows], dlog)
                dgb = dgb + jnp.sum(dlog, axis=0, keepdims=True)
                dk_rows[j] = dk if dk_rows[j] is None else dk_rows[j] + dk
                dv_rows[j] = dv if dv_rows[j] is None else dv_rows[j] + dv
                dl_rows[j] = dl if dl_rows[j] is None else dl_rows[j] + dl
            dw2_ref[d] = dw2
            dgb_ref[d] = dgb
        dk, dv, dl = (jnp.concatenate(t, axis=0) for t in (dk_rows, dv_rows, dl_rows))
        dwk_ref[...] = _tn(hc, dk)
        dwv_ref[...] = _tn(hc, dv)
        dwl_ref[...] = _tn(hc, dl)
        dh = _nt(dk, wk_ref[...]) + _nt(dv, wv_ref[...]) + _nt(dl, wl_ref[...])
        gx = dh * xn
        dmod_ref[:, 0:D] = jnp.sum(dh, axis=0, keepdims=True)
        dmod_ref[:, D:2 * D] = jnp.sum(gx, axis=0, keepdims=True) * g_ref[...]
        dg_ref[...] = jnp.sum(gx, axis=0, keepdims=True) * (1.0 + sc_ref[...])

    vec = pl.BlockSpec((1, D), lambda i: (0, 0))
    w2s = pl.BlockSpec((LRW, KW), lambda i: (0, 0))
    gbs = pl.BlockSpec((1, KW), lambda i: (0, 0))
    sts = pl.BlockSpec((VW, KW), lambda i: (0, 0))
    full = lambda *s: pl.BlockSpec(s, lambda i: (0,) * len(s))
    return pl.pallas_call(
        body, name="ctx_bwd", grid=(1,),
        out_shape=(jax.ShapeDtypeStruct((D, KW), f32), jax.ShapeDtypeStruct((D, VW), f32), jax.ShapeDtypeStruct((D, LRW), f32),
                   jax.ShapeDtypeStruct((1, 2 * D), f32), jax.ShapeDtypeStruct((1, D), f32),
                   jax.ShapeDtypeStruct((2, LRW, KW), f32), jax.ShapeDtypeStruct((2, 1, KW), f32)),
        in_specs=[pl.BlockSpec((CTX, D), lambda i: (0, 0)), vec, vec, vec] + _CTX_W_SPECS + [w2s, w2s, gbs, gbs, sts, sts],
        out_specs=(full(D, KW), full(D, VW), full(D, LRW), full(1, 2 * D), full(1, D), full(2, LRW, KW), full(2, 1, KW)),
        compiler_params=_cp(("arbitrary",)),
    )(ctx, g, scale, shift, w_pad, w_pad, w_pad, w2f, w2b, gbf, gbb, dsf, dsb)


def _layernorm(va, g, b):
    mu = jnp.mean(va, axis=-1, keepdims=True)
    xc = va - mu
    rstd = lax.rsqrt(jnp.mean(xc * xc, axis=-1, keepdims=True) + EPS)
    vhat = xc * rstd
    return vhat, rstd, vhat * g + b


def _mix_fwd(p, ln_g, ln_b, ws, bs_t):
    l = p.shape[0]
    nch = l // ACH
    half = AW // 2

    def body(p_ref, g_ref, b_ref, ws_ref, bs_ref, sv_ref, va_buf, col_buf, sem):
        cp = pltpu.make_async_copy(p_ref.at[:, pl.ds(PVA, AW)], va_buf, sem)
        cp.start()
        cp.wait()

        def rows_step(n, carry):
            rows = pl.ds(pl.multiple_of(n * ACH, ACH), ACH)
            _, _, vn = _layernorm(va_buf[rows, :].astype(f32), g_ref[...], b_ref[...])
            for gi in range(2):
                sl = slice(gi * ACH, (gi + 1) * ACH)
                sv_ref[rows, sl] = (_dot(ws_ref[gi], vn[:, sl]) + bs_ref[:, gi:gi + 1]).astype(bf16)
            col_buf[0, rows, :] = vn[:, half:half + ACH]
            col_buf[1, rows, :] = vn[:, half + ACH:]
            return carry

        lax.fori_loop(0, nch, rows_step, 0, unroll=8)

        def cols_step(cidx, carry):
            rows = pl.ds(cidx, ACH, stride=GW)
            for gi in range(2, 4):
                col_buf[gi - 2, rows, :] = _dot(ws_ref[gi], col_buf[gi - 2, rows, :]) + bs_ref[:, gi:gi + 1]
            return carry

        lax.fori_loop(0, GW, cols_step, 0, unroll=8)

        def out_step(n, carry):
            rows = pl.ds(pl.multiple_of(n * ACH, ACH), ACH)
            sv_ref[rows, half:half + ACH] = col_buf[0, rows, :].astype(bf16)
            sv_ref[rows, half + ACH:] = col_buf[1, rows, :].astype(bf16)
            return carry

        lax.fori_loop(0, nch, out_step, 0, unroll=8)

    vm = pl.BlockSpec(memory_space=pltpu.VMEM)
    return pl.pallas_call(
        body, name="mix_fwd", out_shape=jax.ShapeDtypeStruct((l, AW), bf16),
        in_specs=[pl.BlockSpec(memory_space=pl.ANY), vm, vm, vm, vm], out_specs=vm,
        scratch_shapes=[pltpu.VMEM((l, AW), bf16), pltpu.VMEM((2, l, ACH), f32), pltpu.SemaphoreType.DMA(())],
        compiler_params=_cp(),
    )(p, ln_g, ln_b, ws, bs_t)


def _mix_bwd(p, dsv, ln_g, ln_b, ws_t):
    l = p.shape[0]
    nch = l // ACH
    half = AW // 2

    def body(p_ref, dsv_hbm, g_ref, b_ref, wst_ref, dva_ref, dws_ref, dbs_ref, dg_ref, db_ref, va_buf, dsv_buf, vn_col, ds_col, sems):
        cp1 = pltpu.make_async_copy(p_ref.at[:, pl.ds(PVA, AW)], va_buf, sems.at[0])
        cp2 = pltpu.make_async_copy(dsv_hbm, dsv_buf, sems.at[1])
        cp1.start()
        cp2.start()
        dws_ref[...] = jnp.zeros_like(dws_ref)
        dbs_ref[...] = jnp.zeros_like(dbs_ref)
        dg_ref[...] = jnp.zeros_like(dg_ref)
        db_ref[...] = jnp.zeros_like(db_ref)
        cp1.wait()
        cp2.wait()

        def rows_step(n, carry):
            rows = pl.ds(pl.multiple_of(n * ACH, ACH), ACH)
            _, _, vn = _layernorm(va_buf[rows, :].astype(f32), g_ref[...], b_ref[...])
            ds = dsv_buf[rows, :].astype(f32)
            for gi in range(2):
                sl = slice(gi * ACH, (gi + 1) * ACH)
                dws_ref[gi] += _nt(ds[:, sl], vn[:, sl])
                dbs_ref[gi] += ds[:, sl]
            for gi in range(2):
                sl = slice(half + gi * ACH, half + (gi + 1) * ACH)
                vn_col[gi, rows, :] = vn[:, sl]
                ds_col[gi, rows, :] = ds[:, sl]
            return carry

        lax.fori_loop(0, nch, rows_step, 0, unroll=8)

        def cols_step(cidx, carry):
            rows = pl.ds(cidx, ACH, stride=GW)
            for gi in range(2, 4):
                ds = ds_col[gi - 2, rows, :]
                dws_ref[gi] += _nt(ds, vn_col[gi - 2, rows, :])
                dbs_ref[gi] += ds
                ds_col[gi - 2, rows, :] = _dot(wst_ref[gi], ds)
            return carry

        lax.fori_loop(0, GW, cols_step, 0, unroll=8)

        def out_step(n, carry):
            rows = pl.ds(pl.multiple_of(n * ACH, ACH), ACH)
            vhat, rstd, _ = _layernorm(va_buf[rows, :].astype(f32), g_ref[...], b_ref[...])
            ds = dsv_buf[rows, :].astype(f32)
            dvn = jnp.concatenate([_dot(wst_ref[0], ds[:, 0:ACH]), _dot(wst_ref[1], ds[:, ACH:half]), ds_col[0, rows, :], ds_col[1, rows, :]], axis=1)
            dg_ref[...] += jnp.sum(dvn * vhat, axis=0, keepdims=True)
            db_ref[...] += jnp.sum(dvn, axis=0, keepdims=True)
            dvh = dvn * g_ref[...]
            dva = rstd * (dvh - jnp.mean(dvh, axis=-1, keepdims=True) - vhat * jnp.mean(dvh * vhat, axis=-1, keepdims=True))
            dva_ref[rows, :] = dva.astype(bf16)
            return carry

        lax.fori_loop(0, nch, out_step, 0, unroll=8)

    vm = pl.BlockSpec(memory_space=pltpu.VMEM)
    hbm = pl.BlockSpec(memory_space=pl.ANY)
    return pl.pallas_call(
        body, name="mix_bwd",
        out_shape=(jax.ShapeDtypeStruct((l, AW), bf16), jax.ShapeDtypeStruct((4, ACH, ACH), f32), jax.ShapeDtypeStruct((4, ACH, ACH), f32),
                   jax.ShapeDtypeStruct((1, AW), f32), jax.ShapeDtypeStruct((1, AW), f32)),
        in_specs=[hbm, hbm, vm, vm, vm], out_specs=(vm, vm, vm, vm, vm),
        scratch_shapes=[pltpu.VMEM((l, AW), bf16), pltpu.VMEM((l, AW), bf16), pltpu.VMEM((2, l, ACH), f32), pltpu.VMEM((2, l, ACH), f32),
                        pltpu.SemaphoreType.DMA((2,))],
        compiler_params=_cp(),
    )(p, dsv, ln_g, ln_b, ws_t)


def _mid(x, tgt, p, o_f, o_b, sv, gate, gf, gb_norm, w_pa, w_pb, w_out, tl):
    l = x.shape[0]

    def body(x_ref, t_ref, zb_ref, ua_ref, za_ref, g1_ref, g2_ref, of_ref, ob_ref, sv_ref, gate_ref, gf_ref, gbn_ref,
             wpa_ref, wpb_ref, wout_ref,
             dx1_ref, dzbua_ref, dzag_ref, dsv_ref, do_ref, dwout_bf, dwpa_bf, dwpb_bf, dgf_ref, dgate_ref, dgbn_ref, loss_ref,
             dwout_ref, dwpa_ref, dwpb_ref):
        @pl.when(pl.program_id(0) == 0)
        def _():
            for r in (dwout_ref, dwpa_ref, dwpb_ref, dgf_ref, dgate_ref, dgbn_ref, loss_ref):
                r[...] = jnp.zeros_like(r)

        o = of_ref[...] + ob_ref[...]
        rr = jnp.concatenate(
            [jnp.broadcast_to(lax.rsqrt(jnp.mean(o[:, h * HV:(h + 1) * HV] ** 2, axis=-1, keepdims=True) + EPS), (tl, HV))
             for h in range(NH)], axis=1)
        ohat = o * rr
        on = ohat * gbn_ref[...]
        szb, dszb = _silu_and_grad(zb_ref[...].astype(f32))
        tb = on * szb
        u = ua_ref[...].astype(f32)
        svv = sv_ref[...].astype(f32)
        sza, dsza = _silu_and_grad(za_ref[...].astype(f32))
        ta = u * svv * sza
        ya = _dot(ta, wpa_ref[...])
        yb = _dot(tb, wpb_ref[...])
        g1 = _sigmoid(g1_ref[...].astype(f32))
        g2 = _sigmoid(g2_ref[...].astype(f32))
        m = g1 * ya + g2 * yb
        y2 = _dot(m, wout_ref[...])
        x1 = x_ref[...] + gate_ref[...] * y2
        r1 = lax.rsqrt(jnp.mean(x1 * x1, axis=-1, keepdims=True) + EPS)
        x1n = x1 * r1
        err = x1n * gf_ref[...] - t_ref[...]
        loss_ref[...] += jnp.sum(jnp.sum(err * err, axis=-1, keepdims=True), axis=0, keepdims=True) * (0.5 / D)
        dout = err * (1.0 / D)
        dgf_ref[...] += jnp.sum(dout * x1n, axis=0, keepdims=True)
        dx1n = dout * gf_ref[...]
        dx1 = r1 * (dx1n - x1n * jnp.mean(dx1n * x1n, axis=-1, keepdims=True))
        dx1_ref[...] = dx1
        dgate_ref[...] += jnp.sum(dx1 * y2, axis=0, keepdims=True)
        dy2 = dx1 * gate_ref[...]
        dwout_ref[...] += _tn(m, dy2)
        dm = _nt(dy2, wout_ref[...])
        dya = dm * g1
        dyb = dm * g2
        dzag_ref[:, AW:AW + D] = (dm * ya * g1 * (1.0 - g1)).astype(bf16)
        dzag_ref[:, AW + D:] = (dm * yb * g2 * (1.0 - g2)).astype(bf16)
        dwpa_ref[...] += _tn(ta, dya)
        dta = _nt(dya, wpa_ref[...])
        dzbua_ref[:, AW:] = (dta * svv * sza).astype(bf16)
        dsv_ref[...] = (dta * u * sza).astype(bf16)
        dzag_ref[:, 0:AW] = (dta * u * svv * dsza).astype(bf16)
        dwpb_ref[...] += _tn(tb, dyb)
        dtb = _nt(dyb, wpb_ref[...])
        don = dtb * szb
        dzbua_ref[:, 0:AW] = (dtb * on * dszb).astype(bf16)
        dgbn_ref[...] += jnp.sum(don * ohat, axis=0, keepdims=True)
        doh = don * gbn_ref[...]
        prod = doh * ohat
        mh = jnp.concatenate(
            [jnp.broadcast_to(jnp.mean(prod[:, h * HV:(h + 1) * HV], axis=-1, keepdims=True), (tl, HV)) for h in range(NH)], axis=1)
        do_ref[...] = (rr * (doh - ohat * mh)).astype(bf16)

        @pl.when(pl.program_id(0) == l // tl - 1)
        def _():
            for acc, out in ((dwout_ref, dwout_bf), (dwpa_ref, dwpa_bf), (dwpb_ref, dwpb_bf)):
                out[...] = acc[...].astype(bf16)

    row = lambda w, j: pl.BlockSpec((tl, w), lambda i, j=j: (i, j))
    full = lambda *s: pl.BlockSpec(s, lambda i: (0,) * len(s))
    return pl.pallas_call(
        body, name="mid", grid=(l // tl,),
        out_shape=(jax.ShapeDtypeStruct((l, D), f32), jax.ShapeDtypeStruct((l, 2 * AW), bf16), jax.ShapeDtypeStruct((l, AW + 2 * D), bf16),
                   jax.ShapeDtypeStruct((l, AW), bf16), jax.ShapeDtypeStruct((l, VW), bf16),
                   jax.ShapeDtypeStruct((D, D), bf16), jax.ShapeDtypeStruct((AW, D), bf16), jax.ShapeDtypeStruct((VW, D), bf16),
                   jax.ShapeDtypeStruct((1, D), f32), jax.ShapeDtypeStruct((1, D), f32), jax.ShapeDtypeStruct((1, VW), f32),
                   jax.ShapeDtypeStruct((1, 1), f32)),
        scratch_shapes=[pltpu.VMEM((D, D), f32), pltpu.VMEM((AW, D), f32), pltpu.VMEM((VW, D), f32)],
        in_specs=[row(D, 0), row(D, 0), row(AW, PZB // AW), row(AW, PUA // AW), row(AW, PZA // AW), row(D, PG1 // D), row(D, PG2 // D),
                  row(VW, 0), row(VW, 0), row(AW, 0), full(1, D), full(1, D), full(1, VW), full(AW, D), full(VW, D), full(D, D)],
        out_specs=(row(D, 0), row(2 * AW, 0), row(AW + 2 * D, 0), row(AW, 0), row(VW, 0),
                   full(D, D), full(AW, D), full(VW, D), full(1, D), full(1, D), full(1, VW), full(1, 1)),
        compiler_params=_cp(("arbitrary",)),
    )(x, tgt, p, p, p, p, p, o_f, o_b, sv, gate, gf, gb_norm, w_pa, w_pb, w_out)


def _in_bwd(x, dx1, dqkv, dzbua, dva, dzag, dlr, w_pad, g, scale, tl, comm):
    l = x.shape[0]
    c_in, c_out, c_sems = comm.specs()

    def body(*refs):
        cin = refs[14:14 + len(c_in)]
        outs = refs[14 + len(c_in):]
        comm.run(cin, outs[4:4 + comm.n], outs[4 + comm.n:], l // tl, lambda: compute(*refs[:14], *outs[:4]))

    def compute(x_ref, dx1_ref, a_ref, b_ref, c_ref, e_ref, lr_ref, wa_ref, wb_ref, wc_ref, we_ref, wl_ref, g_ref, sc_ref,
                gx_ref, dsh_ref, dsc_ref, dg_ref):
        @pl.when(pl.program_id(0) == 0)
        def _():
            for r in (dsh_ref, dsc_ref, dg_ref):
                r[...] = jnp.zeros_like(r)

        dh = (_nt(a_ref[...], wa_ref[...]) + _nt(b_ref[...], wb_ref[...]) + _nt(c_ref[...], wc_ref[...]) + _nt(e_ref[...], we_ref[...])
              + _nt(lr_ref[...], wl_ref[...]))
        xv = x_ref[...]
        r = lax.rsqrt(jnp.mean(xv * xv, axis=-1, keepdims=True) + EPS)
        xn = xv * r
        gxn = jnp.sum(dh * xn, axis=0, keepdims=True)
        dsh_ref[...] += jnp.sum(dh, axis=0, keepdims=True)
        dsc_ref[...] += gxn * g_ref[...]
        dg_ref[...] += gxn * (1.0 + sc_ref[...])
        dxn = dh * (g_ref[...] * (1.0 + sc_ref[...]))
        gx_ref[...] = dx1_ref[...] + r * (dxn - xn * jnp.mean(dxn * xn, axis=-1, keepdims=True))

    row = lambda w: pl.BlockSpec((tl, w), lambda i: (i, 0))
    wcol = lambda w, j: pl.BlockSpec((D, w), lambda i, j=j: (0, j))
    vec = pl.BlockSpec((1, D), lambda i: (0, 0))
    return pl.pallas_call(
        body, name="in_bwd", grid=(l // tl,),
        out_shape=(jax.ShapeDtypeStruct((l, D), f32),) + (jax.ShapeDtypeStruct((1, D), f32),) * 3 + tuple(comm.outs),
        in_specs=[row(D), row(D), row(2 * KW + VW), row(2 * AW), row(AW), row(AW + 2 * D), row(LRW),
                  wcol(2 * KW + VW, 0), wcol(2 * AW, PZB // (2 * AW)), wcol(AW, PVA // AW), wcol(AW + 2 * D, PZA // (AW + 2 * D)),
                  wcol(LRW, PLR // LRW), vec, vec] + c_in,
        out_specs=(row(D), vec, vec, vec) + tuple(c_out), scratch_shapes=c_sems,
        compiler_params=_cp(("arbitrary",)),
    )(x, dx1, dqkv, dzbua, dva, dzag, dlr, w_pad, w_pad, w_pad, w_pad, w_pad, g, scale, *comm.ins)


def _tn_matmul(a, bs, tl, name, comm=None):
    l, m = a.shape
    k = len(bs)
    comm = comm or _Comm([], [], None)
    c_in, c_out, c_sems = comm.specs()

    def body(a_ref, *refs):
        b_refs, cin = refs[:k], refs[k:k + len(c_in)]
        o_refs = refs[k + len(c_in):2 * k + len(c_in)]
        cout, sems = refs[2 * k + len(c_in):2 * k + len(c_in) + comm.n], refs[2 * k + len(c_in) + comm.n:]

        def compute():
            @pl.when(pl.program_id(0) == 0)
            def _():
                for o_ref in o_refs:
                    o_ref[...] = jnp.zeros_like(o_ref)

            av = a_ref[...]
            for b_ref, o_ref in zip(b_refs, o_refs):
                o_ref[...] += _tn(av, b_ref[...])

        comm.run(cin, cout, sems, l // tl, compute)

    return pl.pallas_call(
        body, name=name, grid=(l // tl,), out_shape=tuple(jax.ShapeDtypeStruct((m, b.shape[1]), f32) for b in bs) + tuple(comm.outs),
        in_specs=[pl.BlockSpec((tl, m), lambda i: (i, 0))] + [pl.BlockSpec((tl, b.shape[1]), lambda i: (i, 0)) for b in bs] + c_in,
        out_specs=tuple(pl.BlockSpec((m, b.shape[1]), lambda i: (0, 0)) for b in bs) + tuple(c_out),
        scratch_shapes=c_sems, compiler_params=_cp(("arbitrary",)),
    )(a, *bs, *comm.ins)


def _pad_gate(w2, gb):
    z = jnp.zeros((RANK, KW), f32)
    tail = jnp.zeros((LRW - 2 * RANK, KW), f32)
    w2f = jnp.concatenate([w2[0], z, tail], axis=0)
    w2b = jnp.concatenate([z, w2[1], tail], axis=0)
    return w2f, w2b, gb[0:1], gb[1:2]


EARLY_A_ROWS = 384


class _NoExchange:
    def __init__(self, w_pa, w_pb, w_out):
        self.weights = (w_pa, w_pb, w_out)

    def gather_proj(self):
        return _Comm([], [], None)

    def proj_weights(self, got):
        return self.weights

    def first(self, dw_out, dw_pa, dw_pb):
        return _Comm([], [], None)

    def early_a(self, small, blocks):
        return _Comm([], [], None)

    def early_b(self, blocks):
        return _Comm([], [], None)

    def late(self, blocks, dgt):
        return _Comm([], [], None)


class _Exchanges:
    def __init__(self, pa, pb, wo):
        self.shards = (pa, pb, wo)

    def gather_proj(self):
        def plan(i, o):
            srcs = [lambda j, r=r: r for r in i]
            dsts = [lambda j: o[0].at[:, _lanes(j)], lambda j: o[1].at[:, _lanes(j)], lambda j: o[2].at[j]]
            return srcs, dsts, None
        sds = jax.ShapeDtypeStruct
        return _Comm(self.shards, [sds((AW, D), bf16), sds((VW, D), bf16), sds((NDEV, 128, D), bf16)], plan)

    def proj_weights(self, got):
        return got[0], got[1], got[2].reshape(D, D)

    def first(self, dw_out, dw_pa, dw_pb):
        def plan(i, o):
            srcs = [lambda j: i[0].at[j], lambda j: i[1].at[:, _lanes(j)], lambda j: i[2].at[:, _lanes(j)]]
            dsts = [lambda j, r=r: r.at[j] for r in o]
            return srcs, dsts, None
        sds = jax.ShapeDtypeStruct
        return _Comm([dw_out.reshape(NDEV, 128, D), dw_pa, dw_pb],
                     [sds((NDEV, 128, D), bf16), sds((NDEV, AW, 128), bf16), sds((NDEV, VW, 128), bf16)], plan)

    def early_a(self, small, blocks):
        def plan(i, o):
            srcs = [lambda j: i[0], lambda j: i[1].at[j]]
            dsts = [lambda j, r=r: r.at[j] for r in o]
            return srcs, dsts, [None, lambda j: j >= LATE_DESTS - 1]
        sds = jax.ShapeDtypeStruct
        return _Comm([small, blocks], [sds((NDEV,) + small.shape, f32), sds(blocks.shape, bf16)], plan)

    def early_b(self, blocks):
        def plan(i, o):
            return [lambda j: i[0].at[j]], [lambda j: o[0].at[j]], [lambda j: j >= LATE_DESTS - 1]
        return _Comm([blocks], [jax.ShapeDtypeStruct(blocks.shape, bf16)], plan)

    def late(self, blocks, dgt):
        return _LateComm(blocks, dgt)


def _local_step(x, ctx, tgt, mod, modc, norm_g, w_pad, ln_g, ln_b, ws, bs, w2, gb, gb_norm, gf, xch):
    shift, scale, gate = mod[:, 0:D], mod[:, D:2 * D], mod[:, 2 * D:]
    shift_c, scale_c = modc[:, 0:D], modc[:, D:]
    w2f, w2b, gbf, gbb = _pad_gate(w2, gb)

    p, h, *got_proj = _in_proj(x, norm_g, scale, shift, w_pad, 512, xch.gather_proj())
    w_pa, w_pb, w_out = xch.proj_weights(got_proj)
    sc_f, sc_b = _ctx_fwd(ctx, norm_g, scale_c, shift_c, w_pad, w2f, w2b, gbf, gbb)
    o_f, st_f = _gla_fwd(p, w2f, gbf, sc_f, False, 512, "gla_fwd_f")
    o_b, st_b = _gla_fwd(p, w2b, gbb, sc_b, True, 512, "gla_fwd_b")
    sv = _mix_fwd(p, ln_g, ln_b, ws.astype(bf16), bs.T)
    (dx1, dzbua, dzag, dsv, do, dw_out, dw_pa, dw_pb, dgf, dgate, dgbn, loss) = _mid(
        x, tgt, p, o_f, o_b, sv, gate, gf, gb_norm, w_pa, w_pb, w_out, 256)
    dva, dws, dbs_acc, dln_g, dln_b = _mix_bwd(p, dsv, ln_g, ln_b, jnp.swapaxes(ws, 1, 2).astype(bf16))
    dw_zbua, dw_va, dw_zag, *got_first = _tn_matmul(h, [dzbua, dva, dzag], 1024, "dw_early", xch.first(dw_out, dw_pa, dw_pb))
    blocks_a = _pack_dw_early(dw_zbua, dw_va, dw_zag, 0, EARLY_A_ROWS, "pack_dw_early_a")
    blocks_b = _pack_dw_early(dw_zbua, dw_va, dw_zag, EARLY_A_ROWS, D - EARLY_A_ROWS, "pack_dw_early_b")
    small = _rows128(dln_g, dln_b, dws, jnp.sum(dbs_acc, axis=-1), dgbn, dgf, jnp.broadcast_to(loss, (1, 128)))

    dqkv_f, dlr_f, dw2f, dgbf, dsc_f, *got_a = _gla_bwd(p, do, st_f, w2f, gbf, None, False, 512, "gla_bwd_f",
                                                        xch.early_a(small, blocks_a))
    dqkv, dlr, dw2b, dgbb, dsc_b, *got_b = _gla_bwd(p, do, st_b, w2b, gbb, (dqkv_f, dlr_f), True, 512, "gla_bwd_b",
                                                    xch.early_b(blocks_b))
    dwk_c, dwv_c, dwl_c, dmodc, dg_c, dw2c, dgbc = _ctx_bwd(ctx, norm_g, scale_c, shift_c, w_pad, w2f, w2b, gbf, gbb, dsc_f, dsc_b)
    dw_qkv, dw_lr = _tn_matmul(h, [dqkv, dlr], 1024, "dw_qkv_lr")
    blocks_late = _pack_dw_late(dw_qkv, dw_lr, dwk_c, dwv_c, dwl_c)
    dw2 = jnp.stack([dw2f[0:RANK] + dw2c[0, 0:RANK], dw2b[RANK:2 * RANK] + dw2c[1, RANK:2 * RANK]])
    dgb = jnp.concatenate([dgbf + dgbc[0], dgbb + dgbc[1]], axis=0)
    dgt = jnp.concatenate([jnp.transpose(dw2.reshape(2, RANK, NDEV, 32), (2, 0, 1, 3)).reshape(NDEV, 2 * RANK * 32),
                           jnp.transpose(dgb.reshape(2, NDEV, 32), (1, 0, 2)).reshape(NDEV, 64),
                           jnp.zeros((NDEV, 64), f32)], axis=1).reshape(NDEV, 9, 128)
    gx, dshift, dscale, dg, *got_late = _in_bwd(x, dx1, dqkv, dzbua, dva, dzag, dlr, w_pad, norm_g, scale, 512,
                                                xch.late(blocks_late, dgt))
    return dict(loss=loss, gx=gx, dmod=jnp.concatenate([dshift, dscale, dgate], axis=1), dmodc=dmodc, dnorm_g=dg + dg_c,
                small=small, blocks_a=blocks_a, blocks_b=blocks_b, blocks_late=blocks_late, dw2=dw2, dgb=dgb,
                dw_pa=dw_pa, dw_pb=dw_pb, dw_out=dw_out, got_first=got_first, got_a=got_a, got_b=got_b, got_late=got_late)


def _rows128(*vs):
    out = []
    for t in vs:
        t = t.reshape(-1)
        pad = (-t.shape[0]) % 128
        out.append(jnp.pad(t, (0, pad)) if pad else t)
    return jnp.concatenate(out).reshape(-1, 128)


def kernel(x, c, ctx, c_ctx, w_mod, b_mod, norm_g, w_in, a_ln_g, a_ln_b, a_ws, a_bs, b_gate_w2, b_gate_b, b_norm_g, w_proj_a, w_proj_b, w_out, final_norm_g, loss_target, m_c_ctx, m_w_mod, m_b_mod, m_norm_g, m_w_in, m_a_ln_g, m_a_ln_b, m_a_ws, m_a_bs, m_b_gate_w2, m_b_gate_b, m_b_norm_g, m_w_proj_a, m_w_proj_b, m_w_out, m_final_norm_g, v_c_ctx, v_w_mod, v_b_mod, v_norm_g, v_w_in, v_a_ln_g, v_a_ln_b, v_a_ws, v_a_bs, v_b_gate_w2, v_b_gate_b, v_b_norm_g, v_w_proj_a, v_w_proj_b, v_w_out, v_final_norm_g):
    me = _me()
    ncol = w_mod.shape[2]

    gate_mine = _rows128(jnp.concatenate([b_gate_w2.reshape(-1), b_gate_b.reshape(-1)]))
    bm_mine = lax.dynamic_slice(b_mod, (0, me * ncol), (1, ncol))
    cs, mods, wg, gates = _gather_first(c, c_ctx.reshape(1, D), w_mod[0], bm_mine, w_in[0].astype(bf16), gate_mine)
    w_pad = _repack_w(wg)
    gflat = gates.reshape(NDEV, 9 * 128)
    w2 = jnp.transpose(gflat[:, 0:2 * RANK * 32].reshape(NDEV, 2, RANK, 32), (1, 2, 0, 3)).reshape(2, RANK, KW)
    gb = jnp.transpose(gflat[:, 2 * RANK * 32:2 * RANK * 32 + 64].reshape(NDEV, 2, 32), (1, 0, 2)).reshape(2, KW)

    mods = jnp.transpose(mods, (1, 0, 2)).reshape(16, 3 * D)
    mod = lax.dynamic_slice(mods, (me, 0), (1, 3 * D))
    modc = mods[8:9, 0:2 * D]

    xch = _Exchanges(w_proj_a[0].astype(bf16), w_proj_b[0].astype(bf16), w_out[0].astype(bf16))
    r = _local_step(x[0], ctx[0], loss_target[0], mod, modc, norm_g, w_pad, a_ln_g, a_ln_b, a_ws[0], a_bs[0], w2, gb,
                    b_norm_g, final_norm_g.reshape(1, D), xch)
    p_out, p_pa, p_pb = r["got_first"]
    smalls_e, p_in_a = r["got_a"]
    (p_in_b,) = r["got_b"]
    _, _, p_in_late, p_gt = r["got_late"]

    n_e = (AW + AW + 4 * ACH * ACH + AW + VW + D) // 128
    row = lambda t: t.reshape(1, D)
    rep_e = _adam_params(smalls_e, [a_ln_g, a_ln_b, a_ws, a_bs, b_norm_g, row(final_norm_g)],
                         [m_a_ln_g, m_a_ln_b, m_a_ws, m_a_bs, m_b_norm_g, row(m_final_norm_g)],
                         [v_a_ln_g, v_a_ln_b, v_a_ws, v_a_bs, v_b_norm_g, row(v_final_norm_g)], "adam_rep_early")
    rep_e = [t[0:5] + (t[5].reshape(D),) for t in rep_e]
    losses = smalls_e[:, n_e, 0]
    loss = losses[0]
    for i in range(1, NDEV):
        loss = loss + losses[i]

    dbm = r["dmod"] + jnp.concatenate([r["dmodc"], jnp.zeros((1, D), f32)], axis=1)
    smalls_l = _all_gather(_rows128(r["dnorm_g"], dbm, r["dmod"], r["dmodc"]), "gather_small")
    n_l = (D + 3 * D) // 128
    rep_l = _adam_params(smalls_l, [norm_g, b_mod], [m_norm_g, m_b_mod], [v_norm_g, v_b_mod], "adam_rep_late")
    tail = smalls_l[:, n_l:].reshape(NDEV, -1)
    dmods = tail[:, 0:3 * D]
    dmodc_all = tail[:, 3 * D:5 * D]
    dmodc_tot = dmodc_all[0:1]
    for i in range(1, NDEV):
        dmodc_tot = dmodc_tot + dmodc_all[i:i + 1]

    dm_rows = jnp.concatenate([dmods, jnp.concatenate([dmodc_tot, jnp.zeros((1, D), f32)], axis=1), jnp.zeros((7, 3 * D), f32)], axis=0)
    dm_mine = lax.dynamic_slice(dm_rows, (0, me * ncol), (16, ncol))
    g_wmod, gcs = _mod_bwd(cs, dm_mine, w_mod[0])
    wm = _adam(g_wmod[None], w_mod, m_w_mod, v_w_mod, 256, "adam_wmod")
    cc = _adam(gcs.reshape(NDEV, 8, 128), c_ctx.reshape(8, 128), m_c_ctx.reshape(8, 128), v_c_ctx.reshape(8, 128), 8, "adam_cctx")

    a_in = _adam_w_in(p_in_a, p_in_b, p_in_late, w_in, m_w_in, v_w_in)
    a_pa = _adam(p_pa, w_proj_a, m_w_proj_a, v_w_proj_a, AW, "adam_w_pa")
    a_pb = _adam(p_pb, w_proj_b, m_w_proj_b, v_w_proj_b, VW, "adam_w_pb")
    a_out = _adam(p_out, w_out, m_w_out, v_w_out, 128, "adam_w_out")
    gate_m = _rows128(jnp.concatenate([m_b_gate_w2.reshape(-1), m_b_gate_b.reshape(-1)]))
    gate_v = _rows128(jnp.concatenate([v_b_gate_w2.reshape(-1), v_b_gate_b.reshape(-1)]))
    a_gt = [t.reshape(-1) for t in _adam(p_gt, gate_mine, gate_m, gate_v, 9, "adam_gate")]
    nw2 = 2 * RANK * 32
    sh = [(a_in[k], a_pa[k], a_pb[k], a_out[k], a_gt[k][0:nw2].reshape(1, 2, RANK, 32),
           a_gt[k][nw2:nw2 + 64].reshape(1, 2, 32)) for k in range(4)]

    outs = [loss, r["gx"][None]]
    for k in range(4):
        lg, lb, aws, abs_, bng, fng = rep_e[k]
        n_g, bmod = rep_l[k]
        s_in, s_pa, s_pb, s_out, s_w2, s_gb = sh[k]
        outs += [cc[k].reshape(D), wm[k], bmod, n_g, s_in, lg, lb, aws, abs_, s_w2, s_gb, bng, s_pa, s_pb, s_out, fng]
    return tuple(outs)
```

```python
import functools

import jax
import jax.numpy as jnp
from jax import lax
from jax.experimental import pallas as pl
from jax.experimental.pallas import tpu as pltpu

f32, bf16 = jnp.float32, jnp.bfloat16

D = 1024
CTX = 256
EPS = 1e-6
AW = 512
ACH = 128
GW = 64
KW = 256
VW = 512
NH = 4
HK = 64
HV = 128
RANK = 16
TAU = 16.0
CH = 64
QSCALE = HK ** -0.5
INW = 5152
NDEV = 8

PQ, PK, PV, PZB, PUA, PVA, PZA, PG1, PG2, PLR, PW = 0, 256, 512, 1024, 1536, 2048, 2560, 3072, 4096, 5120, 5248
LRW = 128

ADAM_LR, ADAM_B1, ADAM_B2, ADAM_EPS, ADAM_WD, ADAM_STEP = 0.001, 0.9, 0.999, 1e-08, 0.01, 10

VMEM_LIMIT = 56 * 1024 * 1024
MESH = pl.DeviceIdType.MESH


def _cp(sem=None):
    return pltpu.CompilerParams(dimension_semantics=sem, vmem_limit_bytes=VMEM_LIMIT)


def _dot(a, b):
    return jnp.dot(a.astype(bf16), b.astype(bf16), preferred_element_type=f32)


def _nt(a, b):
    return lax.dot_general(a.astype(bf16), b.astype(bf16), (((1,), (1,)), ((), ())), preferred_element_type=f32)


def _tn(a, b):
    return lax.dot_general(a.astype(bf16), b.astype(bf16), (((0,), (0,)), ((), ())), preferred_element_type=f32)


def _dot_hi(a, b):
    return jnp.dot(a, b, preferred_element_type=f32, precision=lax.Precision.HIGHEST)


def _sigmoid(x):
    return 1.0 / (1.0 + jnp.exp(-x))


def _log_sigmoid(x):
    return jnp.minimum(x, 0.0) - jnp.log(1.0 + jnp.exp(-jnp.abs(x)))


def _silu_and_grad(z):
    s = _sigmoid(z)
    return z * s, s * (1.0 + z * (1.0 - s))


def _me():
    return 4 * lax.axis_index("x") + 2 * lax.axis_index("y") + lax.axis_index("c")


def _peer(k):
    x, y, c = lax.axis_index("x"), lax.axis_index("y"), lax.axis_index("c")
    px = 1 - x if k & 4 else x
    py = 1 - y if k & 2 else y
    pc = 1 - c if k & 1 else c
    return (px, py, pc), 4 * px + 2 * py + pc


def _all_gather(v, name):
    r, c = v.shape

    def body(v_ref, out_ref, send_sems, recv_sems, local_sem):
        me = _me()
        mine = pltpu.make_async_copy(v_ref, out_ref.at[me], local_sem)
        mine.start()
        sends, recvs = [], []
        for k in range(1, NDEV):
            dev, idx = _peer(k)
            sends.append(pltpu.make_async_remote_copy(
                src_ref=v_ref, dst_ref=out_ref.at[me], send_sem=send_sems.at[k - 1], recv_sem=recv_sems.at[k - 1],
                device_id=dev, device_id_type=MESH))
            recvs.append(pltpu.make_async_remote_copy(
                src_ref=v_ref, dst_ref=out_ref.at[idx], send_sem=send_sems.at[k - 1], recv_sem=recv_sems.at[k - 1],
                device_id=dev, device_id_type=MESH))
        for cp in sends:
            cp.start()
        for cp in recvs:
            cp.wait_recv()
        for cp in sends:
            cp.wait_send()
        mine.wait()

    return pl.pallas_call(
        body, name=name, out_shape=jax.ShapeDtypeStruct((NDEV, r, c), v.dtype),
        in_specs=[pl.BlockSpec(memory_space=pl.ANY)], out_specs=pl.BlockSpec(memory_space=pl.ANY),
        scratch_shapes=[pltpu.SemaphoreType.DMA((NDEV - 1,)), pltpu.SemaphoreType.DMA((NDEV - 1,)), pltpu.SemaphoreType.DMA(())],
    )(v)


def _fanout(srcs, dsts, send_sems, recv_sems, local_sems, owners=None):
    me = _me()
    n = len(srcs)
    owns = lambda a, j: True if owners is None or owners[a] is None else owners[a](j)

    def guarded(cond, fn):
        if cond is True:
            fn()
        else:
            pl.when(cond)(fn)

    def copies(with_recvs):
        local = [pltpu.make_async_copy(srcs[a](me), dsts[a](me), local_sems.at[a]) for a in range(n)]
        sends, recvs = [], []
        for k in range(1, NDEV):
            dev, idx = _peer(k)
            for a in range(n):
                s = (k - 1) * n + a
                sends.append((owns(a, idx), pltpu.make_async_remote_copy(
                    src_ref=srcs[a](idx), dst_ref=dsts[a](me), send_sem=send_sems.at[s], recv_sem=recv_sems.at[s],
                    device_id=dev, device_id_type=MESH)))
                if with_recvs:
                    recvs.append((owns(a, me), pltpu.make_async_remote_copy(
                        src_ref=srcs[a](idx), dst_ref=dsts[a](idx), send_sem=send_sems.at[s], recv_sem=recv_sems.at[s],
                        device_id=dev, device_id_type=MESH)))
        return local, sends, recvs

    def start():
        local, sends, _ = copies(False)
        for a, cp in enumerate(local):
            guarded(owns(a, me), cp.start)
        for cond, cp in sends:
            guarded(cond, cp.start)

    def finish():
        local, sends, recvs = copies(True)
        for cond, cp in recvs:
            guarded(cond, cp.wait_recv)
        for cond, cp in sends:
            guarded(cond, cp.wait_send)
        for a, cp in enumerate(local):
            guarded(owns(a, me), cp.wait)

    return start, finish


class _Comm:
    def __init__(self, ins, outs, plan):
        self.ins, self.outs, self.plan = list(ins), list(outs), plan
        self.n = len(self.outs)

    def specs(self):
        hbm = pl.BlockSpec(memory_space=pl.ANY)
        return [hbm] * len(self.ins), [hbm] * self.n, _fanout_sems(self.n) if self.n else []

    def run(self, in_refs, out_refs, sems, nsteps, compute):
        if not self.n:
            compute()
            return

        def hooks():
            srcs, dsts, owners = self.plan(in_refs, out_refs)
            return _fanout(srcs, dsts, sems[0], sems[1], sems[2], owners)

        pl.when(pl.program_id(0) == 0)(lambda: hooks()[0]())
        compute()
        pl.when(pl.program_id(0) == nsteps - 1)(lambda: hooks()[1]())


LATE_MID_STEP = 1


class _LateComm:
    def __init__(self, blocks, dgt):
        sds = jax.ShapeDtypeStruct
        self.ins = [blocks, dgt]
        self.outs = [sds((D, SHARD), bf16), sds((D, SHARD), bf16), sds((4, D, SHARD), bf16), sds(dgt.shape, f32)]
        self.n = len(self.outs)

    def specs(self):
        hbm = pl.BlockSpec(memory_space=pl.ANY)
        scratch = [pltpu.VMEM((3, D, SHARD), bf16), pltpu.SemaphoreType.DMA((2,)), pltpu.SemaphoreType.DMA((4,)),
                   pltpu.SemaphoreType.DMA((3,))] + _fanout_sems(1)
        return [hbm] * 2, [hbm] * self.n, scratch

    def run(self, in_refs, out_refs, scratch, nsteps, compute):
        late_ref, dgt_ref = in_refs
        sib_ref, pair_ref, parts_ref, gt_ref = out_refs
        vbuf, send_sems, recv_sems, local_sems, g_send, g_recv, g_local = scratch
        x, y, c = lax.axis_index("x"), lax.axis_index("y"), lax.axis_index("c")
        chip = 2 * x + y
        is_owner_chip = chip == 0
        step = pl.program_id(0)

        def to_sibling():
            return pltpu.make_async_remote_copy(src_ref=late_ref.at[1 - c], dst_ref=sib_ref, send_sem=send_sems.at[0],
                                                recv_sem=recv_sems.at[0], device_id=(x, y, 1 - c), device_id_type=MESH)

        def to_owner(k):
            return pltpu.make_async_remote_copy(src_ref=pair_ref, dst_ref=parts_ref.at[k], send_sem=send_sems.at[1],
                                                recv_sem=recv_sems.at[k], device_id=(0, 0, c), device_id_type=MESH)

        def own_copy():
            return pltpu.make_async_copy(pair_ref, parts_ref.at[0], local_sems.at[2])

        def gates():
            return _fanout([lambda j: dgt_ref.at[j]], [lambda j: gt_ref.at[j]], g_send, g_recv, g_local)

        @pl.when(step == 0)
        def _():
            to_sibling().start()
            gates()[0]()

        compute()

        @pl.when(step == LATE_MID_STEP)
        def _():
            mine = pltpu.make_async_copy(late_ref.at[c], vbuf.at[0], local_sems.at[0])
            mine.start()
            to_sibling().wait_recv()
            theirs = pltpu.make_async_copy(sib_ref, vbuf.at[1], local_sems.at[1])
            theirs.start()
            mine.wait()
            theirs.wait()
            vbuf[2] = (vbuf[0].astype(f32) + vbuf[1].astype(f32)).astype(bf16)
            pltpu.sync_copy(vbuf.at[2], pair_ref)
            pl.when(is_owner_chip)(lambda: own_copy().start())
            pl.when(jnp.logical_not(is_owner_chip))(lambda: to_owner(chip).start())

        @pl.when(step == nsteps - 1)
        def _():
            @pl.when(is_owner_chip)
            def _():
                for k in range(1, 4):
                    to_owner(k).wait_recv()
                own_copy().wait()

            pl.when(jnp.logical_not(is_owner_chip))(lambda: to_owner(chip).wait_send())
            to_sibling().wait_send()
            gates()[1]()


def _fanout_sems(n):
    return [pltpu.SemaphoreType.DMA(((NDEV - 1) * n,)), pltpu.SemaphoreType.DMA(((NDEV - 1) * n,)), pltpu.SemaphoreType.DMA((n,))]


def _lanes(j):
    return pl.ds(pl.multiple_of(j * 128, 128), 128)


def _gather_first(c_row, cctx_row, wm, bm, wi, gate):
    def body(c_ref, cctx_ref, wm_ref, bm_ref, wi_ref, g_ref, cs_ref, mods_ref, owi, og, call_ref, mine_ref,
             send_sems, recv_sems, local_sems, c_send, c_recv, c_local, m_send, m_recv, m_local):
        x, y, c = lax.axis_index("x"), lax.axis_index("y"), lax.axis_index("c")
        sibling = (x, y, 1 - c)
        chips = [(1 - x, y), (x, 1 - y), (1 - x, 1 - y)]
        index = lambda px, py, pc: 4 * px + 2 * py + pc
        arrays = ((wi_ref, owi), (g_ref, og))
        n = len(arrays)

        def copy(a, k, block, to, own=False):
            src, out = arrays[a]
            return pltpu.make_async_remote_copy(
                src_ref=src if own else out.at[index(*block)], dst_ref=out.at[index(*block)],
                send_sem=send_sems.at[k * n + a], recv_sem=recv_sems.at[k * n + a], device_id=to, device_id_type=MESH)

        mine = [pltpu.make_async_copy(src, out.at[index(x, y, c)], local_sems.at[a]) for a, (src, out) in enumerate(arrays)]
        first = [copy(a, 0, (x, y, c), sibling, own=True) for a in range(n)]
        first += [copy(a, 1 + j, (x, y, c), (*chip, c), own=True) for j, chip in enumerate(chips) for a in range(n)]
        for cp in mine + first:
            cp.start()

        c_start, c_finish = _fanout([lambda j: c_ref], [lambda j: call_ref.at[j]], c_send, c_recv, c_local)
        c_start()
        c_finish()
        cs = jnp.concatenate([call_ref[j] for j in range(NDEV)] + [cctx_ref[...], jnp.zeros((16 - NDEV - 1, D), f32)], axis=0)
        cs_ref[...] = cs
        s, _ = _silu_and_grad(cs)
        mine_ref[...] = _dot_hi(s, wm_ref[...]) + bm_ref[...]
        m_start, m_finish = _fanout([lambda j: mine_ref], [lambda j: mods_ref.at[j]], m_send, m_recv, m_local)
        m_start()

        passed = []
        for j, chip in enumerate(chips):
            for a in range(n):
                copy(a, 1 + j, (*chip, c), (x, y, c)).wait_recv()
            for a in range(n):
                cp = copy(a, 4 + j, (*chip, c), sibling)
                cp.start()
                passed.append(cp)
        for a in range(n):
            copy(a, 0, sibling, (x, y, c)).wait_recv()
        for j, chip in enumerate(chips):
            for a in range(n):
                copy(a, 4 + j, (*chip, 1 - c), (x, y, c)).wait_recv()
        for cp in first + passed:
            cp.wait_send()
        for cp in mine:
            cp.wait()
        m_finish()

    hbm = pl.BlockSpec(memory_space=pl.ANY)
    vm = pl.BlockSpec(memory_space=pltpu.VMEM)
    ncol = wm.shape[1]
    return pl.pallas_call(
        body, name="gather_first",
        out_shape=(jax.ShapeDtypeStruct((16, D), f32), jax.ShapeDtypeStruct((NDEV, 16, ncol), f32),
                   jax.ShapeDtypeStruct((NDEV,) + wi.shape, bf16), jax.ShapeDtypeStruct((NDEV,) + gate.shape, f32)),
        in_specs=[vm, vm, vm, vm, hbm, hbm], out_specs=(vm, vm, hbm, hbm),
        scratch_shapes=[pltpu.VMEM((NDEV, 1, D), f32), pltpu.VMEM((16, ncol), f32)] + _fanout_sems(2) + _fanout_sems(1) + _fanout_sems(1),
        compiler_params=_cp(),
    )(c_row, cctx_row, wm, bm, wi, gate)


SHARD = INW // NDEV
ROWS_RP = 128


def _overlap(lo, hi, a, b):
    s, e = max(lo, a), min(hi, b)
    return (s, e) if s < e else None


def _repack_w(wg):
    segs = ((0, 1024, PQ), (1024, 1024 + 2 * RANK, PLR), (1024 + 2 * RANK, INW, PZB))

    def body(g_ref, o_ref):
        for j in range(NDEV):
            lo, hi = j * SHARD, (j + 1) * SHARD
            for a, b, pad0 in segs:
                ov = _overlap(lo, hi, a, b)
                if ov:
                    s, e = ov
                    o_ref[:, pad0 + s - a:pad0 + e - a] = g_ref[j, :, s - lo:e - lo]
        o_ref[:, PLR + 2 * RANK:PW] = jnp.zeros((ROWS_RP, PW - PLR - 2 * RANK), bf16)

    return pl.pallas_call(
        body, name="repack_w", grid=(D // ROWS_RP,), out_shape=jax.ShapeDtypeStruct((D, PW), bf16),
        in_specs=[pl.BlockSpec((NDEV, ROWS_RP, SHARD), lambda i: (0, i, 0))],
        out_specs=pl.BlockSpec((ROWS_RP, PW), lambda i: (i, 0)), compiler_params=_cp(("arbitrary",)),
    )(wg)


LATE_END = 1024 + 2 * RANK
LATE_DESTS = 2


def _pack_blocks(o_ref, srcs, dests, dtype):
    for n, j in enumerate(dests):
        lo, hi = j * SHARD, (j + 1) * SHARD
        done = lo
        for a, b, src in srcs:
            ov = _overlap(lo, hi, a, b)
            if ov:
                s, e = ov
                if s > done:
                    o_ref[n, :, done - lo:s - lo] = jnp.zeros((ROWS_RP, s - done), dtype)
                o_ref[n, :, s - lo:e - lo] = src[:, s - a:e - a].astype(dtype)
                done = e
        if done < hi:
            o_ref[n, :, done - lo:hi - lo] = jnp.zeros((ROWS_RP, hi - done), dtype)


def _pack_dw_early(dw_zbua, dw_va, dw_zag, row0, nrows, name):
    def body(zbua_ref, va_ref, zag_ref, o_ref):
        _pack_blocks(o_ref, ((LATE_END, 2080, zbua_ref), (2080, 2592, va_ref), (2592, INW, zag_ref)), range(NDEV), bf16)

    row = lambda w: pl.BlockSpec((ROWS_RP, w), lambda i: (i + row0 // ROWS_RP, 0))
    return pl.pallas_call(
        body, name=name, grid=(nrows // ROWS_RP,), out_shape=jax.ShapeDtypeStruct((NDEV, nrows, SHARD), bf16),
        in_specs=[row(2 * AW), row(AW), row(AW + 2 * D)],
        out_specs=pl.BlockSpec((NDEV, ROWS_RP, SHARD), lambda i: (0, i, 0)), compiler_params=_cp(("arbitrary",)),
    )(dw_zbua, dw_va, dw_zag)


def _pack_dw_late(dw_qkv, dw_lr, dwk_c, dwv_c, dwl_c):
    def body(qkv_ref, lr_ref, kc_ref, vc_ref, lc_ref, o_ref):
        qkv = qkv_ref[...] + jnp.concatenate([jnp.zeros((ROWS_RP, KW), f32), kc_ref[...], vc_ref[...]], axis=1)
        lr = lr_ref[...] + lc_ref[...]
        _pack_blocks(o_ref, ((0, 1024, qkv), (1024, LATE_END, lr)), range(LATE_DESTS), bf16)

    row = lambda w: pl.BlockSpec((ROWS_RP, w), lambda i: (i, 0))
    return pl.pallas_call(
        body, name="pack_dw_late", grid=(D // ROWS_RP,), out_shape=jax.ShapeDtypeStruct((LATE_DESTS, D, SHARD), bf16),
        in_specs=[row(2 * KW + VW), row(LRW), row(KW), row(VW), row(LRW)],
        out_specs=pl.BlockSpec((LATE_DESTS, ROWS_RP, SHARD), lambda i: (0, i, 0)), compiler_params=_cp(("arbitrary",)),
    )(dw_qkv, dw_lr, dwk_c, dwv_c, dwl_c)


def _mod_bwd(cs, dm, wm):
    def body(cs_ref, dm_ref, wm_ref, gw_ref, gcs_ref, gc_ref, send_sems, recv_sems, local_sems):
        s, ds = _silu_and_grad(cs_ref[...])
        part = lax.dot_general(dm_ref[8:9, :], wm_ref[...], (((1,), (1,)), ((), ())), preferred_element_type=f32,
                               precision=lax.Precision.HIGHEST)
        gc_ref[...] = part * ds[8:9, :]
        start, finish = _fanout([lambda j: gc_ref], [lambda j: gcs_ref.at[j]], send_sems, recv_sems, local_sems)
        start()
        gw_ref[...] = lax.dot_general(s, dm_ref[...], (((0,), (0,)), ((), ())), preferred_element_type=f32,
                                      precision=lax.Precision.HIGHEST)
        finish()

    return pl.pallas_call(body, name="mod_bwd",
                          out_shape=(jax.ShapeDtypeStruct(wm.shape, f32), jax.ShapeDtypeStruct((NDEV, 1, D), f32)),
                          scratch_shapes=[pltpu.VMEM((1, D), f32)] + _fanout_sems(1),
                          compiler_params=_cp())(cs, dm, wm)


def _adam_update(g, w_ref, m_ref, v_ref, go_ref, d_ref, mo_ref, vo_ref):
    c1 = 1.0 / (1.0 - ADAM_B1 ** ADAM_STEP)
    c2 = 1.0 / (1.0 - ADAM_B2 ** ADAM_STEP)
    mn = ADAM_B1 * m_ref[...] + (1.0 - ADAM_B1) * g
    vn = ADAM_B2 * v_ref[...] + (1.0 - ADAM_B2) * (g * g)
    go_ref[...] = g
    mo_ref[...] = mn
    vo_ref[...] = vn
    d_ref[...] = -ADAM_LR * ((mn * c1) / (jnp.sqrt(vn * c2) + ADAM_EPS) + ADAM_WD * w_ref[...])


def _adam_w_in(parts_a, parts_b, parts_late, w, m, v):
    na = EARLY_A_ROWS // ROWS_RP

    def body(a_ref, b_ref, l_ref, w_ref, m_ref, v_ref, go_ref, d_ref, mo_ref, vo_ref):
        me = _me()
        first = pl.program_id(0) < na
        early = jnp.where(first, a_ref[0], b_ref[0]).astype(f32)
        for i in range(1, NDEV):
            early = early + jnp.where(first, a_ref[i], b_ref[i]).astype(f32)
        late = l_ref[0].astype(f32)
        for i in range(1, 4):
            late = late + l_ref[i].astype(f32)
        g = jnp.where(me >= LATE_DESTS - 1, early, 0.0) + jnp.where(me < LATE_DESTS, late, 0.0)
        _adam_update(g, w_ref, m_ref, v_ref, go_ref, d_ref, mo_ref, vo_ref)

    blk = pl.BlockSpec((None, ROWS_RP, SHARD), lambda i: (0, i, 0))
    return pl.pallas_call(
        body, name="adam_w_in", grid=(D // ROWS_RP,), out_shape=tuple(jax.ShapeDtypeStruct((1, D, SHARD), f32) for _ in range(4)),
        in_specs=[pl.BlockSpec((NDEV, ROWS_RP, SHARD), lambda i: (0, jnp.minimum(i, na - 1), 0)),
                  pl.BlockSpec((NDEV, ROWS_RP, SHARD), lambda i: (0, jnp.maximum(i - na, 0), 0)),
                  pl.BlockSpec((4, ROWS_RP, SHARD), lambda i: (0, i, 0)), blk, blk, blk],
        out_specs=(blk, blk, blk, blk), compiler_params=_cp(("arbitrary",)),
    )(parts_a, parts_b, parts_late, w, m, v)


def _adam_params(parts, ws, ms, vs, name):
    p = parts.shape[0]
    k = len(ws)
    nrows = [w.size // 128 for w in ws]
    starts = [sum(nrows[:i]) for i in range(k)]

    def shaped(g, shape):
        if len(shape) == 2:
            return jnp.concatenate([g[r:r + 1] for r in range(g.shape[0])], axis=1)
        return g.reshape(shape)

    def body(*refs):
        g_ref = refs[0]
        w_refs, m_refs, v_refs = refs[1:1 + k], refs[1 + k:1 + 2 * k], refs[1 + 2 * k:1 + 3 * k]
        outs = refs[1 + 3 * k:]
        for i in range(k):
            rows = slice(starts[i], starts[i] + nrows[i])
            g = g_ref[0, rows, :]
            for j in range(1, p):
                g = g + g_ref[j, rows, :]
            _adam_update(shaped(g, ws[i].shape), w_refs[i], m_refs[i], v_refs[i], outs[i], outs[k + i], outs[2 * k + i], outs[3 * k + i])

    vm = pl.BlockSpec(memory_space=pltpu.VMEM)
    res = pl.pallas_call(
        body, name=name, out_shape=tuple(jax.ShapeDtypeStruct(w.shape, f32) for _ in range(4) for w in ws),
        in_specs=[vm] * (1 + 3 * k), out_specs=(vm,) * (4 * k), compiler_params=_cp(),
    )(parts, *ws, *ms, *vs)
    return [res[i * k:(i + 1) * k] for i in range(4)]


def _adam(parts, w, m, v, rows, name):
    p, r, c = parts.shape
    lead = w.ndim - 2

    def body(g_ref, w_ref, m_ref, v_ref, go_ref, d_ref, mo_ref, vo_ref):
        g = g_ref[0].astype(f32)
        for i in range(1, p):
            g = g + g_ref[i].astype(f32)
        _adam_update(g, w_ref, m_ref, v_ref, go_ref, d_ref, mo_ref, vo_ref)

    blk = pl.BlockSpec((None,) * lead + (rows, c), lambda i: (0,) * lead + (i, 0))
    return pl.pallas_call(
        body, name=name, grid=(r // rows,), out_shape=tuple(jax.ShapeDtypeStruct(w.shape, f32) for _ in range(4)),
        in_specs=[pl.BlockSpec((p, rows, c), lambda i: (0, i, 0)), blk, blk, blk], out_specs=(blk, blk, blk, blk),
        compiler_params=_cp(("arbitrary",)),
    )(parts, w, m, v)


def _in_proj(x, g, scale, shift, w_pad, tl, comm):
    l = x.shape[0]
    c_in, c_out, c_sems = comm.specs()

    def body(*refs):
        x_ref, g_ref, sc_ref, sh_ref, w_ref = refs[:5]
        cin = refs[5:5 + len(c_in)]
        p_ref, h_ref = refs[5 + len(c_in):7 + len(c_in)]
        cout = refs[7 + len(c_in):7 + len(c_in) + comm.n]
        sems = refs[7 + len(c_in) + comm.n:]

        def compute():
            xv = x_ref[...]
            r = lax.rsqrt(jnp.mean(xv * xv, axis=-1, keepdims=True) + EPS)
            h = (xv * r) * (g_ref[...] * (1.0 + sc_ref[...])) + sh_ref[...]
            hb = h.astype(bf16)
            h_ref[...] = hb
            p_ref[...] = jnp.dot(hb, w_ref[...], preferred_element_type=f32).astype(bf16)

        comm.run(cin, cout, sems, l // tl, compute)

    vec = pl.BlockSpec((1, D), lambda i: (0, 0))
    return pl.pallas_call(
        body, name="in_proj", grid=(l // tl,),
        out_shape=(jax.ShapeDtypeStruct((l, PW), bf16), jax.ShapeDtypeStruct((l, D), bf16)) + tuple(comm.outs),
        in_specs=[pl.BlockSpec((tl, D), lambda i: (i, 0)), vec, vec, vec, pl.BlockSpec((D, PW), lambda i: (0, 0))] + c_in,
        out_specs=(pl.BlockSpec((tl, PW), lambda i: (i, 0)), pl.BlockSpec((tl, D), lambda i: (i, 0))) + tuple(c_out),
        scratch_shapes=c_sems, compiler_params=_cp(("arbitrary",)),
    )(x, g, scale, shift, w_pad, *comm.ins)


def _tri(rev):
    i = lax.broadcasted_iota(jnp.int32, (CH, CH), 0)
    j = lax.broadcasted_iota(jnp.int32, (CH, CH), 1)
    return jnp.where((j >= i) if rev else (j <= i), 1.0, 0.0).astype(f32)


def _head_masks():
    lane = lax.broadcasted_iota(jnp.int32, (1, KW), 1) // HK
    return [jnp.where(lane == h, 1.0, 0.0).astype(f32) for h in range(NH)]


def _block_diag():
    r = lax.broadcasted_iota(jnp.int32, (VW, KW), 0) // HV
    c = lax.broadcasted_iota(jnp.int32, (VW, KW), 1) // HK
    return jnp.where(r == c, 1.0, 0.0).astype(f32)


def _decay(lr, w2, gb, tri, rev):
    logits = _dot(lr, w2) + gb
    a = _log_sigmoid(logits) * (1.0 / TAU)
    c = _dot_hi(tri, a)
    cl = c[0:1, :] if rev else c[CH - 1:CH, :]
    return logits, c, cl


def _stack_heads(t, hm):
    return jnp.concatenate([t * hm[h] for h in range(NH)], axis=0)


def _chunk_fwd(q, k, v, c, cl, st, tri, hm, bd):
    tri4 = jnp.concatenate([tri] * NH, axis=0)
    qd = q * jnp.exp(c) * QSCALE
    kd = k * jnp.exp(-c)
    kdec = k * jnp.exp(cl - c)
    pst = _nt(_stack_heads(qd, hm), kd) * tri4
    intra = jnp.concatenate([_dot(pst[h * CH:(h + 1) * CH], v[:, h * HV:(h + 1) * HV]) for h in range(NH)], axis=1)
    o = _nt(qd, st) + intra
    st_new = st * jnp.exp(cl) + bd * _tn(v, kdec)
    return o, st_new


def _state_fwd(k, v, c, cl, st, bd):
    return st * jnp.exp(cl) + bd * _tn(v, k * jnp.exp(cl - c))


def _chunk_bwd(q, k, v, c, cl, st0, dst, do, tri, trit, hm, bd):
    ecl = jnp.exp(cl)
    edec = jnp.exp(cl - c)
    kdec = k * edec
    dv = _nt(kdec, dst)
    dkdec = _dot(v, dst)
    dcl = jnp.sum(dst * st0, axis=0, keepdims=True) * ecl + jnp.sum(dkdec * kdec, axis=0, keepdims=True)
    dc = -dkdec * kdec
    dk = dkdec * edec
    dst0 = dst * ecl
    dq = None
    if q is not None:
        tri4 = jnp.concatenate([tri] * NH, axis=0)
        ec, enc = jnp.exp(c), jnp.exp(-c)
        qd = q * ec * QSCALE
        kd = k * enc
        qs = _stack_heads(qd, hm)
        pst = _nt(qs, kd) * tri4
        dpst = jnp.concatenate([_nt(do[:, h * HV:(h + 1) * HV], v[:, h * HV:(h + 1) * HV]) for h in range(NH)], axis=0) * tri4
        dv = dv + jnp.concatenate([_tn(pst[h * CH:(h + 1) * CH], do[:, h * HV:(h + 1) * HV]) for h in range(NH)], axis=1)
        dqd = _dot(do, st0)
        for h in range(NH):
            dqd = dqd + hm[h] * _dot(dpst[h * CH:(h + 1) * CH], kd)
        dkd = _tn(dpst, qs)
        dst0 = dst0 + bd * _tn(do, qd)
        dc = dc + dqd * qd - dkd * kd
        dq = dqd * ec * QSCALE
        dk = dk + dkd * enc
    da = _dot_hi(trit, dc) + dcl
    return dq, dk, dv, da, dst0


def _bdot(a, b):
    return lax.dot_general(a.astype(bf16), b.astype(bf16), (((2,), (1,)), ((0,), (0,))), preferred_element_type=f32)


def _bnt(a, b):
    return lax.dot_general(a.astype(bf16), b.astype(bf16), (((2,), (2,)), ((0,), (0,))), preferred_element_type=f32)


def _btn(a, b):
    return lax.dot_general(a.astype(bf16), b.astype(bf16), (((1,), (1,)), ((0,), (0,))), preferred_element_type=f32)


def _scan_chunks(x, rev):
    nc = x.shape[0]
    hi = x.astype(bf16)
    r1 = x - hi.astype(f32)
    mid = r1.astype(bf16)
    lo = (r1 - mid.astype(f32)).astype(bf16)
    terms = jnp.concatenate([hi, mid, lo], axis=1)
    tri3 = jnp.broadcast_to(jnp.concatenate([_tri(rev)] * 3, axis=1).astype(bf16)[None], (nc, CH, 3 * CH))
    return lax.dot_general(tri3, terms, (((2,), (1,)), ((0,), (0,))), preferred_element_type=f32)


class _Tile:
    pass


def _tile_prep(q_ref, k_ref, lr_ref, w2_ref, gb_ref, rev, nc):
    t = _Tile()
    tg = nc * CH
    t.logits = _dot(lr_ref[...], w2_ref[...]) + gb_ref[...]
    c = _scan_chunks((_log_sigmoid(t.logits) * (1.0 / TAU)).reshape(nc, CH, KW), rev)
    cl = c[:, 0:1, :] if rev else c[:, CH - 1:CH, :]
    k = k_ref[...].astype(f32).reshape(nc, CH, KW)
    t.ec, t.enc, t.edec, t.ecl = jnp.exp(c), jnp.exp(-c), jnp.exp(cl - c), jnp.exp(cl)
    t.qd = q_ref[...].astype(f32).reshape(nc, CH, KW) * t.ec * QSCALE
    t.kd = k * t.enc
    t.kdec = k * t.edec
    hm = _head_masks()
    t.tri4 = jnp.concatenate([_tri(rev)] * NH, axis=0)[None]
    t.qs = jnp.concatenate([t.qd * hm[h] for h in range(NH)], axis=1)
    t.pst = _bnt(t.qs, t.kd) * t.tri4
    return t


def _gla_fwd(p, w2p, gb, s0, rev, tg, name):
    l = p.shape[0]
    nb, nc = l // tg, tg // CH

    def body(q_ref, k_ref, v_ref, lr_ref, w2_ref, gb_ref, s0_ref, o_ref, st_ref, st):
        @pl.when(pl.program_id(0) == 0)
        def _():
            st[...] = s0_ref[...]

        t = _tile_prep(q_ref, k_ref, lr_ref, w2_ref, gb_ref, rev, nc)
        v = v_ref[...].reshape(nc, CH, VW)
        intra = jnp.concatenate([_bdot(t.pst[:, h * CH:(h + 1) * CH], v[:, :, h * HV:(h + 1) * HV]) for h in range(NH)], axis=2)
        kv = _btn(v, t.kdec) * _block_diag()[None]
        s = st[...]
        for n in (range(nc - 1, -1, -1) if rev else range(nc)):
            st_ref[n] = s
            s = s * t.ecl[n] + kv[n]
        st[...] = s
        o_ref[...] = (_bnt(t.qd, st_ref[...]) + intra).reshape(tg, VW)

    blk = (lambda i: nb - 1 - i) if rev else (lambda i: i)
    return pl.pallas_call(
        body, name=name, grid=(nb,),
        out_shape=(jax.ShapeDtypeStruct((l, VW), f32), jax.ShapeDtypeStruct((l // CH, VW, KW), f32)),
        in_specs=[pl.BlockSpec((tg, KW), lambda i: (blk(i), PQ // KW)), pl.BlockSpec((tg, KW), lambda i: (blk(i), PK // KW)),
                  pl.BlockSpec((tg, VW), lambda i: (blk(i), PV // VW)), pl.BlockSpec((tg, LRW), lambda i: (blk(i), PLR // LRW)),
                  pl.BlockSpec((LRW, KW), lambda i: (0, 0)), pl.BlockSpec((1, KW), lambda i: (0, 0)),
                  pl.BlockSpec((VW, KW), lambda i: (0, 0))],
        out_specs=(pl.BlockSpec((tg, VW), lambda i: (blk(i), 0)), pl.BlockSpec((nc, VW, KW), lambda i: (blk(i), 0, 0))),
        scratch_shapes=[pltpu.VMEM((VW, KW), f32)],
        compiler_params=_cp(("arbitrary",)),
    )(p, p, p, p, w2p, gb, s0)


def _gla_bwd(p, do, states, w2p, gb, prev, rev, tg, name, comm):
    l = p.shape[0]
    nb, nc = l // tg, tg // CH
    out_dt = f32 if prev is None else bf16
    c_in, c_out, c_sems = comm.specs()

    def body(*refs):
        q_ref, k_ref, v_ref, lr_ref, do_ref, st_ref, w2_ref, gb_ref = refs[:8]
        refs = refs[8:]
        if prev is not None:
            pq_ref, pl_ref = refs[:2]
            refs = refs[2:]
        cin, refs = refs[:len(c_in)], refs[len(c_in):]
        dqkv_ref, dlr_ref, dw2_ref, dgb_ref, ds0_ref = refs[:5]
        cout, dst, ds_buf, sems = refs[5:5 + comm.n], refs[5 + comm.n], refs[6 + comm.n], refs[7 + comm.n:]
        comm.run(cin, cout, sems, nb, lambda: compute(q_ref, k_ref, v_ref, lr_ref, do_ref, st_ref, w2_ref, gb_ref,
                                                      pq_ref if prev is not None else None, pl_ref if prev is not None else None,
                                                      dqkv_ref, dlr_ref, dw2_ref, dgb_ref, ds0_ref, dst, ds_buf))

    def compute(q_ref, k_ref, v_ref, lr_ref, do_ref, st_ref, w2_ref, gb_ref, pq_ref, pl_ref,
                dqkv_ref, dlr_ref, dw2_ref, dgb_ref, ds0_ref, dst, ds_buf):
        @pl.when(pl.program_id(0) == 0)
        def _():
            dst[...] = jnp.zeros_like(dst)
            dw2_ref[...] = jnp.zeros_like(dw2_ref)
            dgb_ref[...] = jnp.zeros_like(dgb_ref)

        t = _tile_prep(q_ref, k_ref, lr_ref, w2_ref, gb_ref, rev, nc)
        hm = _head_masks()
        v = v_ref[...].reshape(nc, CH, VW)
        do = do_ref[...].reshape(nc, CH, VW)
        heads = lambda a, h: a[:, :, h * HV:(h + 1) * HV]
        dpst = jnp.concatenate([_bnt(heads(do, h), heads(v, h)) for h in range(NH)], axis=1) * t.tri4
        dv = jnp.concatenate([_btn(t.pst[:, h * CH:(h + 1) * CH], heads(do, h)) for h in range(NH)], axis=2)
        dqd = _bdot(do, st_ref[...])
        for h in range(NH):
            dqd = dqd + hm[h] * _bdot(dpst[:, h * CH:(h + 1) * CH], t.kd)
        dkd = _btn(dpst, t.qs)
        u = _btn(do, t.qd) * _block_diag()[None]
        d = dst[...]
        for n in (range(nc) if rev else range(nc - 1, -1, -1)):
            ds_buf[n] = d
            d = d * t.ecl[n] + u[n]
        dst[...] = d
        ds0_ref[...] = d
        ds = ds_buf[...]
        dv = dv + _bnt(t.kdec, ds)
        dkdec = _bdot(v, ds)
        dcl = jnp.sum(ds * st_ref[...], axis=1, keepdims=True) * t.ecl + jnp.sum(dkdec * t.kdec, axis=1, keepdims=True)
        dc = dqd * t.qd - dkd * t.kd - dkdec * t.kdec
        da = _scan_chunks(dc, not rev) + dcl
        dq = dqd * t.ec * QSCALE
        dk = dkd * t.enc + dkdec * t.edec
        dlog = da.reshape(tg, KW) * _sigmoid(-t.logits) * (1.0 / TAU)
        dlr = _nt(dlog, w2_ref[...])
        dw2_ref[...] += _tn(lr_ref[...], dlog)
        dgb_ref[...] += jnp.sum(dlog, axis=0, keepdims=True)
        dqkv = jnp.concatenate([dq, dk, dv], axis=2).reshape(tg, 2 * KW + VW)
        if prev is not None:
            dqkv = dqkv + pq_ref[...]
            dlr = dlr + pl_ref[...]
        dqkv_ref[...] = dqkv.astype(out_dt)
        dlr_ref[...] = dlr.astype(out_dt)

    blk = (lambda i: i) if rev else (lambda i: nb - 1 - i)
    in_specs = [pl.BlockSpec((tg, KW), lambda i: (blk(i), PQ // KW)), pl.BlockSpec((tg, KW), lambda i: (blk(i), PK // KW)),
                pl.BlockSpec((tg, VW), lambda i: (blk(i), PV // VW)), pl.BlockSpec((tg, LRW), lambda i: (blk(i), PLR // LRW)),
                pl.BlockSpec((tg, VW), lambda i: (blk(i), 0)), pl.BlockSpec((nc, VW, KW), lambda i: (blk(i), 0, 0)),
                pl.BlockSpec((LRW, KW), lambda i: (0, 0)), pl.BlockSpec((1, KW), lambda i: (0, 0))]
    args = [p, p, p, p, do, states, w2p, gb]
    if prev is not None:
        in_specs += [pl.BlockSpec((tg, 2 * KW + VW), lambda i: (blk(i), 0)), pl.BlockSpec((tg, LRW), lambda i: (blk(i), 0))]
        args += list(prev)
    return pl.pallas_call(
        body, name=name, grid=(nb,),
        out_shape=(jax.ShapeDtypeStruct((l, 2 * KW + VW), out_dt), jax.ShapeDtypeStruct((l, LRW), out_dt),
                   jax.ShapeDtypeStruct((LRW, KW), f32), jax.ShapeDtypeStruct((1, KW), f32), jax.ShapeDtypeStruct((VW, KW), f32))
        + tuple(comm.outs),
        in_specs=in_specs + c_in,
        out_specs=(pl.BlockSpec((tg, 2 * KW + VW), lambda i: (blk(i), 0)), pl.BlockSpec((tg, LRW), lambda i: (blk(i), 0)),
                   pl.BlockSpec((LRW, KW), lambda i: (0, 0)), pl.BlockSpec((1, KW), lambda i: (0, 0)),
                   pl.BlockSpec((VW, KW), lambda i: (0, 0))) + tuple(c_out),
        scratch_shapes=[pltpu.VMEM((VW, KW), f32), pltpu.VMEM((nc, VW, KW), f32)] + c_sems,
        compiler_params=_cp(("arbitrary",)),
    )(*args, *comm.ins)


def _ctx_hidden(ctx_ref, g_ref, sc_ref, sh_ref):
    xv = ctx_ref[...]
    r = lax.rsqrt(jnp.mean(xv * xv, axis=-1, keepdims=True) + EPS)
    xn = xv * r
    return xn, xn * (g_ref[...] * (1.0 + sc_ref[...])) + sh_ref[...]


_CTX_W_SPECS = [pl.BlockSpec((D, KW), lambda i: (0, PK // KW)), pl.BlockSpec((D, VW), lambda i: (0, PV // VW)),
                pl.BlockSpec((D, LRW), lambda i: (0, PLR // LRW))]


def _ctx_fwd(ctx, g, scale, shift, w_pad, w2f, w2b, gbf, gbb):
    ncc = CTX // CH

    def body(ctx_ref, g_ref, sc_ref, sh_ref, wk_ref, wv_ref, wl_ref, w2f_ref, w2b_ref, gbf_ref, gbb_ref, sf_ref, sb_ref):
        _, hc = _ctx_hidden(ctx_ref, g_ref, sc_ref, sh_ref)
        k, v, lr = _dot(hc, wk_ref[...]), _dot(hc, wv_ref[...]), _dot(hc, wl_ref[...])
        bd = _block_diag()
        for rev, w2_ref, gb_ref, out in ((False, w2f_ref, gbf_ref, sf_ref), (True, w2b_ref, gbb_ref, sb_ref)):
            tri = _tri(rev)
            st = jnp.zeros((VW, KW), f32)
            for j in (range(ncc - 1, -1, -1) if rev else range(ncc)):
                rows = slice(j * CH, (j + 1) * CH)
                _, c, cl = _decay(lr[rows], w2_ref[...], gb_ref[...], tri, rev)
                st = _state_fwd(k[rows], v[rows], c, cl, st, bd)
            out[...] = st

    vec = pl.BlockSpec((1, D), lambda i: (0, 0))
    w2s = pl.BlockSpec((LRW, KW), lambda i: (0, 0))
    gbs = pl.BlockSpec((1, KW), lambda i: (0, 0))
    sts = pl.BlockSpec((VW, KW), lambda i: (0, 0))
    return pl.pallas_call(
        body, name="ctx_fwd", grid=(1,), out_shape=(jax.ShapeDtypeStruct((VW, KW), f32),) * 2,
        in_specs=[pl.BlockSpec((CTX, D), lambda i: (0, 0)), vec, vec, vec] + _CTX_W_SPECS + [w2s, w2s, gbs, gbs],
        out_specs=(sts, sts), compiler_params=_cp(("arbitrary",)),
    )(ctx, g, scale, shift, w_pad, w_pad, w_pad, w2f, w2b, gbf, gbb)


def _ctx_bwd(ctx, g, scale, shift, w_pad, w2f, w2b, gbf, gbb, dsf, dsb):
    ncc = CTX // CH

    def body(ctx_ref, g_ref, sc_ref, sh_ref, wk_ref, wv_ref, wl_ref, w2f_ref, w2b_ref, gbf_ref, gbb_ref, dsf_ref, dsb_ref,
             dwk_ref, dwv_ref, dwl_ref, dmod_ref, dg_ref, dw2_ref, dgb_ref):
        xn, hc = _ctx_hidden(ctx_ref, g_ref, sc_ref, sh_ref)
        k, v, lr = _dot(hc, wk_ref[...]), _dot(hc, wv_ref[...]), _dot(hc, wl_ref[...])
        bd = _block_diag()
        dk_rows, dv_rows, dl_rows = [None] * ncc, [None] * ncc, [None] * ncc
        for d, (rev, w2_ref, gb_ref, ds_ref) in enumerate(((False, w2f_ref, gbf_ref, dsf_ref), (True, w2b_ref, gbb_ref, dsb_ref))):
            tri = _tri(rev)
            order = list(range(ncc - 1, -1, -1) if rev else range(ncc))
            st, saved = jnp.zeros((VW, KW), f32), {}
            for j in order:
                rows = slice(j * CH, (j + 1) * CH)
                logits, c, cl = _decay(lr[rows], w2_ref[...], gb_ref[...], tri, rev)
                saved[j] = (logits, c, cl, st)
                st = _state_fwd(k[rows], v[rows], c, cl, st, bd)
            dst = ds_ref[...]
            dw2 = jnp.zeros((LRW, KW), f32)
            dgb = jnp.zeros((1, KW), f32)
            for j in reversed(order):
                rows = slice(j * CH, (j + 1) * CH)
                logits, c, cl, st0 = saved[j]
                _, dk, dv, da, dst = _chunk_bwd(None, k[rows], v[rows], c, cl, st0, dst, None, tri, _tri(not rev), None, bd)
                dlog = da * _sigmoid(-logits) * (1.0 / TAU)
                dl = _nt(dlog, w2_ref[...])
                dw2 = dw2 + _tn(lr[rows], dlog)
                dgb = dgb + jnp.sum(dlog, axis=0, keepdims=True)
                dk_rows[j] = dk if dk_rows[j] is None else dk_rows[j] + dk
                dv_rows[j] = dv if dv_rows[j] is None else dv_rows[j] + dv
                dl_rows[j] = dl if dl_rows[j] is None else dl_rows[j] + dl
            dw2_ref[d] = dw2
            dgb_ref[d] = dgb
        dk, dv, dl = (jnp.concatenate(t, axis=0) for t in (dk_rows, dv_rows, dl_rows))
        dwk_ref[...] = _tn(hc, dk)
        dwv_ref[...] = _tn(hc, dv)
        dwl_ref[...] = _tn(hc, dl)
        dh = _nt(dk, wk_ref[...]) + _nt(dv, wv_ref[...]) + _nt(dl, wl_ref[...])
        gx = dh * xn
        dmod_ref[:, 0:D] = jnp.sum(dh, axis=0, keepdims=True)
        dmod_ref[:, D:2 * D] = jnp.sum(gx, axis=0, keepdims=True) * g_ref[...]
        dg_ref[...] = jnp.sum(gx, axis=0, keepdims=True) * (1.0 + sc_ref[...])

    vec = pl.BlockSpec((1, D), lambda i: (0, 0))
    w2s = pl.BlockSpec((LRW, KW), lambda i: (0, 0))
    gbs = pl.BlockSpec((1, KW), lambda i: (0, 0))
    sts = pl.BlockSpec((VW, KW), lambda i: (0, 0))
    full = lambda *s: pl.BlockSpec(s, lambda i: (0,) * len(s))
    return pl.pallas_call(
        body, name="ctx_bwd", grid=(1,),
        out_shape=(jax.ShapeDtypeStruct((D, KW), f32), jax.ShapeDtypeStruct((D, VW), f32), jax.ShapeDtypeStruct((D, LRW), f32),
                   jax.ShapeDtypeStruct((1, 2 * D), f32), jax.ShapeDtypeStruct((1, D), f32),
                   jax.ShapeDtypeStruct((2, LRW, KW), f32), jax.ShapeDtypeStruct((2, 1, KW), f32)),
        in_specs=[pl.BlockSpec((CTX, D), lambda i: (0, 0)), vec, vec, vec] + _CTX_W_SPECS + [w2s, w2s, gbs, gbs, sts, sts],
        out_specs=(full(D, KW), full(D, VW), full(D, LRW), full(1, 2 * D), full(1, D), full(2, LRW, KW), full(2, 1, KW)),
        compiler_params=_cp(("arbitrary",)),
    )(ctx, g, scale, shift, w_pad, w_pad, w_pad, w2f, w2b, gbf, gbb, dsf, dsb)


def _layernorm(va, g, b):
    mu = jnp.mean(va, axis=-1, keepdims=True)
    xc = va - mu
    rstd = lax.rsqrt(jnp.mean(xc * xc, axis=-1, keepdims=True) + EPS)
    vhat = xc * rstd
    return vhat, rstd, vhat * g + b


def _mix_fwd(p, ln_g, ln_b, ws, bs_t):
    l = p.shape[0]
    nch = l // ACH
    half = AW // 2

    def body(p_ref, g_ref, b_ref, ws_ref, bs_ref, sv_ref, va_buf, col_buf, sem):
        cp = pltpu.make_async_copy(p_ref.at[:, pl.ds(PVA, AW)], va_buf, sem)
        cp.start()
        cp.wait()

        def rows_step(n, carry):
            rows = pl.ds(pl.multiple_of(n * ACH, ACH), ACH)
            _, _, vn = _layernorm(va_buf[rows, :].astype(f32), g_ref[...], b_ref[...])
            for gi in range(2):
                sl = slice(gi * ACH, (gi + 1) * ACH)
                sv_ref[rows, sl] = (_dot(ws_ref[gi], vn[:, sl]) + bs_ref[:, gi:gi + 1]).astype(bf16)
            col_buf[0, rows, :] = vn[:, half:half + ACH]
            col_buf[1, rows, :] = vn[:, half + ACH:]
            return carry

        lax.fori_loop(0, nch, rows_step, 0, unroll=8)

        def cols_step(cidx, carry):
            rows = pl.ds(cidx, ACH, stride=GW)
            for gi in range(2, 4):
                col_buf[gi - 2, rows, :] = _dot(ws_ref[gi], col_buf[gi - 2, rows, :]) + bs_ref[:, gi:gi + 1]
            return carry

        lax.fori_loop(0, GW, cols_step, 0, unroll=8)

        def out_step(n, carry):
            rows = pl.ds(pl.multiple_of(n * ACH, ACH), ACH)
            sv_ref[rows, half:half + ACH] = col_buf[0, rows, :].astype(bf16)
            sv_ref[rows, half + ACH:] = col_buf[1, rows, :].astype(bf16)
            return carry

        lax.fori_loop(0, nch, out_step, 0, unroll=8)

    vm = pl.BlockSpec(memory_space=pltpu.VMEM)
    return pl.pallas_call(
        body, name="mix_fwd", out_shape=jax.ShapeDtypeStruct((l, AW), bf16),
        in_specs=[pl.BlockSpec(memory_space=pl.ANY), vm, vm, vm, vm], out_specs=vm,
        scratch_shapes=[pltpu.VMEM((l, AW), bf16), pltpu.VMEM((2, l, ACH), f32), pltpu.SemaphoreType.DMA(())],
        compiler_params=_cp(),
    )(p, ln_g, ln_b, ws, bs_t)


def _mix_bwd(p, dsv, ln_g, ln_b, ws_t):
    l = p.shape[0]
    nch = l // ACH
    half = AW // 2

    def body(p_ref, dsv_hbm, g_ref, b_ref, wst_ref, dva_ref, dws_ref, dbs_ref, dg_ref, db_ref, va_buf, dsv_buf, vn_col, ds_col, sems):
        cp1 = pltpu.make_async_copy(p_ref.at[:, pl.ds(PVA, AW)], va_buf, sems.at[0])
        cp2 = pltpu.make_async_copy(dsv_hbm, dsv_buf, sems.at[1])
        cp1.start()
        cp2.start()
        dws_ref[...] = jnp.zeros_like(dws_ref)
        dbs_ref[...] = jnp.zeros_like(dbs_ref)
        dg_ref[...] = jnp.zeros_like(dg_ref)
        db_ref[...] = jnp.zeros_like(db_ref)
        cp1.wait()
        cp2.wait()

        def rows_step(n, carry):
            rows = pl.ds(pl.multiple_of(n * ACH, ACH), ACH)
            _, _, vn = _layernorm(va_buf[rows, :].astype(f32), g_ref[...], b_ref[...])
            ds = dsv_buf[rows, :].astype(f32)
            for gi in range(2):
                sl = slice(gi * ACH, (gi + 1) * ACH)
                dws_ref[gi] += _nt(ds[:, sl], vn[:, sl])
                dbs_ref[gi] += ds[:, sl]
            for gi in range(2):
                sl = slice(half + gi * ACH, half + (gi + 1) * ACH)
                vn_col[gi, rows, :] = vn[:, sl]
                ds_col[gi, rows, :] = ds[:, sl]
            return carry

        lax.fori_loop(0, nch, rows_step, 0, unroll=8)

        def cols_step(cidx, carry):
            rows = pl.ds(cidx, ACH, stride=GW)
            for gi in range(2, 4):
                ds = ds_col[gi - 2, rows, :]
                dws_ref[gi] += _nt(ds, vn_col[gi - 2, rows, :])
                dbs_ref[gi] += ds
                ds_col[gi - 2, rows, :] = _dot(wst_ref[gi], ds)
            return carry

        lax.fori_loop(0, GW, cols_step, 0, unroll=8)

        def out_step(n, carry):
            rows = pl.ds(pl.multiple_of(n * ACH, ACH), ACH)
            vhat, rstd, _ = _layernorm(va_buf[rows, :].astype(f32), g_ref[...], b_ref[...])
            ds = dsv_buf[rows, :].astype(f32)
            dvn = jnp.concatenate([_dot(wst_ref[0], ds[:, 0:ACH]), _dot(wst_ref[1], ds[:, ACH:half]), ds_col[0, rows, :], ds_col[1, rows, :]], axis=1)
            dg_ref[...] += jnp.sum(dvn * vhat, axis=0, keepdims=True)
            db_ref[...] += jnp.sum(dvn, axis=0, keepdims=True)
            dvh = dvn * g_ref[...]
            dva = rstd * (dvh - jnp.mean(dvh, axis=-1, keepdims=True) - vhat * jnp.mean(dvh * vhat, axis=-1, keepdims=True))
            dva_ref[rows, :] = dva.astype(bf16)
            return carry

        lax.fori_loop(0, nch, out_step, 0, unroll=8)

    vm = pl.BlockSpec(memory_space=pltpu.VMEM)
    hbm = pl.BlockSpec(memory_space=pl.ANY)
    return pl.pallas_call(
        body, name="mix_bwd",
        out_shape=(jax.ShapeDtypeStruct((l, AW), bf16), jax.ShapeDtypeStruct((4, ACH, ACH), f32), jax.ShapeDtypeStruct((4, ACH, ACH), f32),
                   jax.ShapeDtypeStruct((1, AW), f32), jax.ShapeDtypeStruct((1, AW), f32)),
        in_specs=[hbm, hbm, vm, vm, vm], out_specs=(vm, vm, vm, vm, vm),
        scratch_shapes=[pltpu.VMEM((l, AW), bf16), pltpu.VMEM((l, AW), bf16), pltpu.VMEM((2, l, ACH), f32), pltpu.VMEM((2, l, ACH), f32),
                        pltpu.SemaphoreType.DMA((2,))],
        compiler_params=_cp(),
    )(p, dsv, ln_g, ln_b, ws_t)


def _mid(x, tgt, p, o_f, o_b, sv, gate, gf, gb_norm, w_pa, w_pb, w_out, tl):
    l = x.shape[0]

    def body(x_ref, t_ref, zb_ref, ua_ref, za_ref, g1_ref, g2_ref, of_ref, ob_ref, sv_ref, gate_ref, gf_ref, gbn_ref,
             wpa_ref, wpb_ref, wout_ref,
             dx1_ref, dzbua_ref, dzag_ref, dsv_ref, do_ref, dwout_bf, dwpa_bf, dwpb_bf, dgf_ref, dgate_ref, dgbn_ref, loss_ref,
             dwout_ref, dwpa_ref, dwpb_ref):
        @pl.when(pl.program_id(0) == 0)
        def _():
            for r in (dwout_ref, dwpa_ref, dwpb_ref, dgf_ref, dgate_ref, dgbn_ref, loss_ref):
                r[...] = jnp.zeros_like(r)

        o = of_ref[...] + ob_ref[...]
        rr = jnp.concatenate(
            [jnp.broadcast_to(lax.rsqrt(jnp.mean(o[:, h * HV:(h + 1) * HV] ** 2, axis=-1, keepdims=True) + EPS), (tl, HV))
             for h in range(NH)], axis=1)
        ohat = o * rr
        on = ohat * gbn_ref[...]
        szb, dszb = _silu_and_grad(zb_ref[...].astype(f32))
        tb = on * szb
        u = ua_ref[...].astype(f32)
        svv = sv_ref[...].astype(f32)
        sza, dsza = _silu_and_grad(za_ref[...].astype(f32))
        ta = u * svv * sza
        ya = _dot(ta, wpa_ref[...])
        yb = _dot(tb, wpb_ref[...])
        g1 = _sigmoid(g1_ref[...].astype(f32))
        g2 = _sigmoid(g2_ref[...].astype(f32))
        m = g1 * ya + g2 * yb
        y2 = _dot(m, wout_ref[...])
        x1 = x_ref[...] + gate_ref[...] * y2
        r1 = lax.rsqrt(jnp.mean(x1 * x1, axis=-1, keepdims=True) + EPS)
        x1n = x1 * r1
        err = x1n * gf_ref[...] - t_ref[...]
        loss_ref[...] += jnp.sum(jnp.sum(err * err, axis=-1, keepdims=True), axis=0, keepdims=True) * (0.5 / D)
        dout = err * (1.0 / D)
        dgf_ref[...] += jnp.sum(dout * x1n, axis=0, keepdims=True)
        dx1n = dout * gf_ref[...]
        dx1 = r1 * (dx1n - x1n * jnp.mean(dx1n * x1n, axis=-1, keepdims=True))
        dx1_ref[...] = dx1
        dgate_ref[...] += jnp.sum(dx1 * y2, axis=0, keepdims=True)
        dy2 = dx1 * gate_ref[...]
        dwout_ref[...] += _tn(m, dy2)
        dm = _nt(dy2, wout_ref[...])
        dya = dm * g1
        dyb = dm * g2
        dzag_ref[:, AW:AW + D] = (dm * ya * g1 * (1.0 - g1)).astype(bf16)
        dzag_ref[:, AW + D:] = (dm * yb * g2 * (1.0 - g2)).astype(bf16)
        dwpa_ref[...] += _tn(ta, dya)
        dta = _nt(dya, wpa_ref[...])
        dzbua_ref[:, AW:] = (dta * svv * sza).astype(bf16)
        dsv_ref[...] = (dta * u * sza).astype(bf16)
        dzag_ref[:, 0:AW] = (dta * u * svv * dsza).astype(bf16)
        dwpb_ref[...] += _tn(tb, dyb)
        dtb = _nt(dyb, wpb_ref[...])
        don = dtb * szb
        dzbua_ref[:, 0:AW] = (dtb * on * dszb).astype(bf16)
        dgbn_ref[...] += jnp.sum(don * ohat, axis=0, keepdims=True)
        doh = don * gbn_ref[...]
        prod = doh * ohat
        mh = jnp.concatenate(
            [jnp.broadcast_to(jnp.mean(prod[:, h * HV:(h + 1) * HV], axis=-1, keepdims=True), (tl, HV)) for h in range(NH)], axis=1)
        do_ref[...] = (rr * (doh - ohat * mh)).astype(bf16)

        @pl.when(pl.program_id(0) == l // tl - 1)
        def _():
            for acc, out in ((dwout_ref, dwout_bf), (dwpa_ref, dwpa_bf), (dwpb_ref, dwpb_bf)):
                out[...] = acc[...].astype(bf16)

    row = lambda w, j: pl.BlockSpec((tl, w), lambda i, j=j: (i, j))
    full = lambda *s: pl.BlockSpec(s, lambda i: (0,) * len(s))
    return pl.pallas_call(
        body, name="mid", grid=(l // tl,),
        out_shape=(jax.ShapeDtypeStruct((l, D), f32), jax.ShapeDtypeStruct((l, 2 * AW), bf16), jax.ShapeDtypeStruct((l, AW + 2 * D), bf16),
                   jax.ShapeDtypeStruct((l, AW), bf16), jax.ShapeDtypeStruct((l, VW), bf16),
                   jax.ShapeDtypeStruct((D, D), bf16), jax.ShapeDtypeStruct((AW, D), bf16), jax.ShapeDtypeStruct((VW, D), bf16),
                   jax.ShapeDtypeStruct((1, D), f32), jax.ShapeDtypeStruct((1, D), f32), jax.ShapeDtypeStruct((1, VW), f32),
                   jax.ShapeDtypeStruct((1, 1), f32)),
        scratch_shapes=[pltpu.VMEM((D, D), f32), pltpu.VMEM((AW, D), f32), pltpu.VMEM((VW, D), f32)],
        in_specs=[row(D, 0), row(D, 0), row(AW, PZB // AW), row(AW, PUA // AW), row(AW, PZA // AW), row(D, PG1 // D), row(D, PG2 // D),
                  row(VW, 0), row(VW, 0), row(AW, 0), full(1, D), full(1, D), full(1, VW), full(AW, D), full(VW, D), full(D, D)],
        out_specs=(row(D, 0), row(2 * AW, 0), row(AW + 2 * D, 0), row(AW, 0), row(VW, 0),
                   full(D, D), full(AW, D), full(VW, D), full(1, D), full(1, D), full(1, VW), full(1, 1)),
        compiler_params=_cp(("arbitrary",)),
    )(x, tgt, p, p, p, p, p, o_f, o_b, sv, gate, gf, gb_norm, w_pa, w_pb, w_out)


def _in_bwd(x, dx1, dqkv, dzbua, dva, dzag, dlr, w_pad, g, scale, tl, comm):
    l = x.shape[0]
    c_in, c_out, c_sems = comm.specs()

    def body(*refs):
        cin = refs[14:14 + len(c_in)]
        outs = refs[14 + len(c_in):]
        comm.run(cin, outs[4:4 + comm.n], outs[4 + comm.n:], l // tl, lambda: compute(*refs[:14], *outs[:4]))

    def compute(x_ref, dx1_ref, a_ref, b_ref, c_ref, e_ref, lr_ref, wa_ref, wb_ref, wc_ref, we_ref, wl_ref, g_ref, sc_ref,
                gx_ref, dsh_ref, dsc_ref, dg_ref):
        @pl.when(pl.program_id(0) == 0)
        def _():
            for r in (dsh_ref, dsc_ref, dg_ref):
                r[...] = jnp.zeros_like(r)

        dh = (_nt(a_ref[...], wa_ref[...]) + _nt(b_ref[...], wb_ref[...]) + _nt(c_ref[...], wc_ref[...]) + _nt(e_ref[...], we_ref[...])
              + _nt(lr_ref[...], wl_ref[...]))
        xv = x_ref[...]
        r = lax.rsqrt(jnp.mean(xv * xv, axis=-1, keepdims=True) + EPS)
        xn = xv * r
        gxn = jnp.sum(dh * xn, axis=0, keepdims=True)
        dsh_ref[...] += jnp.sum(dh, axis=0, keepdims=True)
        dsc_ref[...] += gxn * g_ref[...]
        dg_ref[...] += gxn * (1.0 + sc_ref[...])
        dxn = dh * (g_ref[...] * (1.0 + sc_ref[...]))
        gx_ref[...] = dx1_ref[...] + r * (dxn - xn * jnp.mean(dxn * xn, axis=-1, keepdims=True))

    row = lambda w: pl.BlockSpec((tl, w), lambda i: (i, 0))
    wcol = lambda w, j: pl.BlockSpec((D, w), lambda i, j=j: (0, j))
    vec = pl.BlockSpec((1, D), lambda i: (0, 0))
    return pl.pallas_call(
        body, name="in_bwd", grid=(l // tl,),
        out_shape=(jax.ShapeDtypeStruct((l, D), f32),) + (jax.ShapeDtypeStruct((1, D), f32),) * 3 + tuple(comm.outs),
        in_specs=[row(D), row(D), row(2 * KW + VW), row(2 * AW), row(AW), row(AW + 2 * D), row(LRW),
                  wcol(2 * KW + VW, 0), wcol(2 * AW, PZB // (2 * AW)), wcol(AW, PVA // AW), wcol(AW + 2 * D, PZA // (AW + 2 * D)),
                  wcol(LRW, PLR // LRW), vec, vec] + c_in,
        out_specs=(row(D), vec, vec, vec) + tuple(c_out), scratch_shapes=c_sems,
        compiler_params=_cp(("arbitrary",)),
    )(x, dx1, dqkv, dzbua, dva, dzag, dlr, w_pad, w_pad, w_pad, w_pad, w_pad, g, scale, *comm.ins)


def _tn_matmul(a, bs, tl, name, comm=None):
    l, m = a.shape
    k = len(bs)
    comm = comm or _Comm([], [], None)
    c_in, c_out, c_sems = comm.specs()

    def body(a_ref, *refs):
        b_refs, cin = refs[:k], refs[k:k + len(c_in)]
        o_refs = refs[k + len(c_in):2 * k + len(c_in)]
        cout, sems = refs[2 * k + len(c_in):2 * k + len(c_in) + comm.n], refs[2 * k + len(c_in) + comm.n:]

        def compute():
            @pl.when(pl.program_id(0) == 0)
            def _():
                for o_ref in o_refs:
                    o_ref[...] = jnp.zeros_like(o_ref)

            av = a_ref[...]
            for b_ref, o_ref in zip(b_refs, o_refs):
                o_ref[...] += _tn(av, b_ref[...])

        comm.run(cin, cout, sems, l // tl, compute)

    return pl.pallas_call(
        body, name=name, grid=(l // tl,), out_shape=tuple(jax.ShapeDtypeStruct((m, b.shape[1]), f32) for b in bs) + tuple(comm.outs),
        in_specs=[pl.BlockSpec((tl, m), lambda i: (i, 0))] + [pl.BlockSpec((tl, b.shape[1]), lambda i: (i, 0)) for b in bs] + c_in,
        out_specs=tuple(pl.BlockSpec((m, b.shape[1]), lambda i: (0, 0)) for b in bs) + tuple(c_out),
        scratch_shapes=c_sems, compiler_params=_cp(("arbitrary",)),
    )(a, *bs, *comm.ins)


def _pad_gate(w2, gb):
    z = jnp.zeros((RANK, KW), f32)
    tail = jnp.zeros((LRW - 2 * RANK, KW), f32)
    w2f = jnp.concatenate([w2[0], z, tail], axis=0)
    w2b = jnp.concatenate([z, w2[1], tail], axis=0)
    return w2f, w2b, gb[0:1], gb[1:2]


EARLY_A_ROWS = 384


class _NoExchange:
    def __init__(self, w_pa, w_pb, w_out):
        self.weights = (w_pa, w_pb, w_out)

    def gather_proj(self):
        return _Comm([], [], None)

    def proj_weights(self, got):
        return self.weights

    def first(self, dw_out, dw_pa, dw_pb):
        return _Comm([], [], None)

    def early_a(self, small, blocks):
        return _Comm([], [], None)

    def early_b(self, blocks):
        return _Comm([], [], None)

    def late(self, blocks, dgt):
        return _Comm([], [], None)


class _Exchanges:
    def __init__(self, pa, pb, wo):
        self.shards = (pa, pb, wo)

    def gather_proj(self):
        def plan(i, o):
            srcs = [lambda j, r=r: r for r in i]
            dsts = [lambda j: o[0].at[:, _lanes(j)], lambda j: o[1].at[:, _lanes(j)], lambda j: o[2].at[j]]
            return srcs, dsts, None
        sds = jax.ShapeDtypeStruct
        return _Comm(self.shards, [sds((AW, D), bf16), sds((VW, D), bf16), sds((NDEV, 128, D), bf16)], plan)

    def proj_weights(self, got):
        return got[0], got[1], got[2].reshape(D, D)

    def first(self, dw_out, dw_pa, dw_pb):
        def plan(i, o):
            srcs = [lambda j: i[0].at[j], lambda j: i[1].at[:, _lanes(j)], lambda j: i[2].at[:, _lanes(j)]]
            dsts = [lambda j, r=r: r.at[j] for r in o]
            return srcs, dsts, None
        sds = jax.ShapeDtypeStruct
        return _Comm([dw_out.reshape(NDEV, 128, D), dw_pa, dw_pb],
                     [sds((NDEV, 128, D), bf16), sds((NDEV, AW, 128), bf16), sds((NDEV, VW, 128), bf16)], plan)

    def early_a(self, small, blocks):
        def plan(i, o):
            srcs = [lambda j: i[0], lambda j: i[1].at[j]]
            dsts = [lambda j, r=r: r.at[j] for r in o]
            return srcs, dsts, [None, lambda j: j >= LATE_DESTS - 1]
        sds = jax.ShapeDtypeStruct
        return _Comm([small, blocks], [sds((NDEV,) + small.shape, f32), sds(blocks.shape, bf16)], plan)

    def early_b(self, blocks):
        def plan(i, o):
            return [lambda j: i[0].at[j]], [lambda j: o[0].at[j]], [lambda j: j >= LATE_DESTS - 1]
        return _Comm([blocks], [jax.ShapeDtypeStruct(blocks.shape, bf16)], plan)

    def late(self, blocks, dgt):
        return _LateComm(blocks, dgt)


def _local_step(x, ctx, tgt, mod, modc, norm_g, w_pad, ln_g, ln_b, ws, bs, w2, gb, gb_norm, gf, xch):
    shift, scale, gate = mod[:, 0:D], mod[:, D:2 * D], mod[:, 2 * D:]
    shift_c, scale_c = modc[:, 0:D], modc[:, D:]
    w2f, w2b, gbf, gbb = _pad_gate(w2, gb)

    p, h, *got_proj = _in_proj(x, norm_g, scale, shift, w_pad, 512, xch.gather_proj())
    w_pa, w_pb, w_out = xch.proj_weights(got_proj)
    sc_f, sc_b = _ctx_fwd(ctx, norm_g, scale_c, shift_c, w_pad, w2f, w2b, gbf, gbb)
    o_f, st_f = _gla_fwd(p, w2f, gbf, sc_f, False, 512, "gla_fwd_f")
    o_b, st_b = _gla_fwd(p, w2b, gbb, sc_b, True, 512, "gla_fwd_b")
    sv = _mix_fwd(p, ln_g, ln_b, ws.astype(bf16), bs.T)
    (dx1, dzbua, dzag, dsv, do, dw_out, dw_pa, dw_pb, dgf, dgate, dgbn, loss) = _mid(
        x, tgt, p, o_f, o_b, sv, gate, gf, gb_norm, w_pa, w_pb, w_out, 256)
    dva, dws, dbs_acc, dln_g, dln_b = _mix_bwd(p, dsv, ln_g, ln_b, jnp.swapaxes(ws, 1, 2).astype(bf16))
    dw_zbua, dw_va, dw_zag, *got_first = _tn_matmul(h, [dzbua, dva, dzag], 1024, "dw_early", xch.first(dw_out, dw_pa, dw_pb))
    blocks_a = _pack_dw_early(dw_zbua, dw_va, dw_zag, 0, EARLY_A_ROWS, "pack_dw_early_a")
    blocks_b = _pack_dw_early(dw_zbua, dw_va, dw_zag, EARLY_A_ROWS, D - EARLY_A_ROWS, "pack_dw_early_b")
    small = _rows128(dln_g, dln_b, dws, jnp.sum(dbs_acc, axis=-1), dgbn, dgf, jnp.broadcast_to(loss, (1, 128)))

    dqkv_f, dlr_f, dw2f, dgbf, dsc_f, *got_a = _gla_bwd(p, do, st_f, w2f, gbf, None, False, 512, "gla_bwd_f",
                                                        xch.early_a(small, blocks_a))
    dqkv, dlr, dw2b, dgbb, dsc_b, *got_b = _gla_bwd(p, do, st_b, w2b, gbb, (dqkv_f, dlr_f), True, 512, "gla_bwd_b",
                                                    xch.early_b(blocks_b))
    dwk_c, dwv_c, dwl_c, dmodc, dg_c, dw2c, dgbc = _ctx_bwd(ctx, norm_g, scale_c, shift_c, w_pad, w2f, w2b, gbf, gbb, dsc_f, dsc_b)
    dw_qkv, dw_lr = _tn_matmul(h, [dqkv, dlr], 1024, "dw_qkv_lr")
    blocks_late = _pack_dw_late(dw_qkv, dw_lr, dwk_c, dwv_c, dwl_c)
    dw2 = jnp.stack([dw2f[0:RANK] + dw2c[0, 0:RANK], dw2b[RANK:2 * RANK] + dw2c[1, RANK:2 * RANK]])
    dgb = jnp.concatenate([dgbf + dgbc[0], dgbb + dgbc[1]], axis=0)
    dgt = jnp.concatenate([jnp.transpose(dw2.reshape(2, RANK, NDEV, 32), (2, 0, 1, 3)).reshape(NDEV, 2 * RANK * 32),
                           jnp.transpose(dgb.reshape(2, NDEV, 32), (1, 0, 2)).reshape(NDEV, 64),
                           jnp.zeros((NDEV, 64), f32)], axis=1).reshape(NDEV, 9, 128)
    gx, dshift, dscale, dg, *got_late = _in_bwd(x, dx1, dqkv, dzbua, dva, dzag, dlr, w_pad, norm_g, scale, 512,
                                                xch.late(blocks_late, dgt))
    return dict(loss=loss, gx=gx, dmod=jnp.concatenate([dshift, dscale, dgate], axis=1), dmodc=dmodc, dnorm_g=dg + dg_c,
                small=small, blocks_a=blocks_a, blocks_b=blocks_b, blocks_late=blocks_late, dw2=dw2, dgb=dgb,
                dw_pa=dw_pa, dw_pb=dw_pb, dw_out=dw_out, got_first=got_first, got_a=got_a, got_b=got_b, got_late=got_late)


def _rows128(*vs):
    out = []
    for t in vs:
        t = t.reshape(-1)
        pad = (-t.shape[0]) % 128
        out.append(jnp.pad(t, (0, pad)) if pad else t)
    return jnp.concatenate(out).reshape(-1, 128)


def kernel(x, c, ctx, c_ctx, w_mod, b_mod, norm_g, w_in, a_ln_g, a_ln_b, a_ws, a_bs, b_gate_w2, b_gate_b, b_norm_g, w_proj_a, w_proj_b, w_out, final_norm_g, loss_target, m_c_ctx, m_w_mod, m_b_mod, m_norm_g, m_w_in, m_a_ln_g, m_a_ln_b, m_a_ws, m_a_bs, m_b_gate_w2, m_b_gate_b, m_b_norm_g, m_w_proj_a, m_w_proj_b, m_w_out, m_final_norm_g, v_c_ctx, v_w_mod, v_b_mod, v_norm_g, v_w_in, v_a_ln_g, v_a_ln_b, v_a_ws, v_a_bs, v_b_gate_w2, v_b_gate_b, v_b_norm_g, v_w_proj_a, v_w_proj_b, v_w_out, v_final_norm_g):
    me = _me()
    ncol = w_mod.shape[2]

    gate_mine = _rows128(jnp.concatenate([b_gate_w2.reshape(-1), b_gate_b.reshape(-1)]))
    bm_mine = lax.dynamic_slice(b_mod, (0, me * ncol), (1, ncol))
    cs, mods, wg, gates = _gather_first(c, c_ctx.reshape(1, D), w_mod[0], bm_mine, w_in[0].astype(bf16), gate_mine)
    w_pad = _repack_w(wg)
    gflat = gates.reshape(NDEV, 9 * 128)
    w2 = jnp.transpose(gflat[:, 0:2 * RANK * 32].reshape(NDEV, 2, RANK, 32), (1, 2, 0, 3)).reshape(2, RANK, KW)
    gb = jnp.transpose(gflat[:, 2 * RANK * 32:2 * RANK * 32 + 64].reshape(NDEV, 2, 32), (1, 0, 2)).reshape(2, KW)

    mods = jnp.transpose(mods, (1, 0, 2)).reshape(16, 3 * D)
    mod = lax.dynamic_slice(mods, (me, 0), (1, 3 * D))
    modc = mods[8:9, 0:2 * D]

    xch = _Exchanges(w_proj_a[0].astype(bf16), w_proj_b[0].astype(bf16), w_out[0].astype(bf16))
    r = _local_step(x[0], ctx[0], loss_target[0], mod, modc, norm_g, w_pad, a_ln_g, a_ln_b, a_ws[0], a_bs[0], w2, gb,
                    b_norm_g, final_norm_g.reshape(1, D), xch)
    p_out, p_pa, p_pb = r["got_first"]
    smalls_e, p_in_a = r["got_a"]
    (p_in_b,) = r["got_b"]
    _, _, p_in_late, p_gt = r["got_late"]

    n_e = (AW + AW + 4 * ACH * ACH + AW + VW + D) // 128
    row = lambda t: t.reshape(1, D)
    rep_e = _adam_params(smalls_e, [a_ln_g, a_ln_b, a_ws, a_bs, b_norm_g, row(final_norm_g)],
                         [m_a_ln_g, m_a_ln_b, m_a_ws, m_a_bs, m_b_norm_g, row(m_final_norm_g)],
                         [v_a_ln_g, v_a_ln_b, v_a_ws, v_a_bs, v_b_norm_g, row(v_final_norm_g)], "adam_rep_early")
    rep_e = [t[0:5] + (t[5].reshape(D),) for t in rep_e]
    losses = smalls_e[:, n_e, 0]
    loss = losses[0]
    for i in range(1, NDEV):
        loss = loss + losses[i]

    dbm = r["dmod"] + jnp.concatenate([r["dmodc"], jnp.zeros((1, D), f32)], axis=1)
    smalls_l = _all_gather(_rows128(r["dnorm_g"], dbm, r["dmod"], r["dmodc"]), "gather_small")
    n_l = (D + 3 * D) // 128
    rep_l = _adam_params(smalls_l, [norm_g, b_mod], [m_norm_g, m_b_mod], [v_norm_g, v_b_mod], "adam_rep_late")
    tail = smalls_l[:, n_l:].reshape(NDEV, -1)
    dmods = tail[:, 0:3 * D]
    dmodc_all = tail[:, 3 * D:5 * D]
    dmodc_tot = dmodc_all[0:1]
    for i in range(1, NDEV):
        dmodc_tot = dmodc_tot + dmodc_all[i:i + 1]

    dm_rows = jnp.concatenate([dmods, jnp.concatenate([dmodc_tot, jnp.zeros((1, D), f32)], axis=1), jnp.zeros((7, 3 * D), f32)], axis=0)
    dm_mine = lax.dynamic_slice(dm_rows, (0, me * ncol), (16, ncol))
    g_wmod, gcs = _mod_bwd(cs, dm_mine, w_mod[0])
    wm = _adam(g_wmod[None], w_mod, m_w_mod, v_w_mod, 256, "adam_wmod")
    cc = _adam(gcs.reshape(NDEV, 8, 128), c_ctx.reshape(8, 128), m_c_ctx.reshape(8, 128), v_c_ctx.reshape(8, 128), 8, "adam_cctx")

    a_in = _adam_w_in(p_in_a, p_in_b, p_in_late, w_in, m_w_in, v_w_in)
    a_pa = _adam(p_pa, w_proj_a, m_w_proj_a, v_w_proj_a, AW, "adam_w_pa")
    a_pb = _adam(p_pb, w_proj_b, m_w_proj_b, v_w_proj_b, VW, "adam_w_pb")
    a_out = _adam(p_out, w_out, m_w_out, v_w_out, 128, "adam_w_out")
    gate_m = _rows128(jnp.concatenate([m_b_gate_w2.reshape(-1), m_b_gate_b.reshape(-1)]))
    gate_v = _rows128(jnp.concatenate([v_b_gate_w2.reshape(-1), v_b_gate_b.reshape(-1)]))
    a_gt = [t.reshape(-1) for t in _adam(p_gt, gate_mine, gate_m, gate_v, 9, "adam_gate")]
    nw2 = 2 * RANK * 32
    sh = [(a_in[k], a_pa[k], a_pb[k], a_out[k], a_gt[k][0:nw2].reshape(1, 2, RANK, 32),
           a_gt[k][nw2:nw2 + 64].reshape(1, 2, 32)) for k in range(4)]

    outs = [loss, r["gx"][None]]
    for k in range(4):
        lg, lb, aws, abs_, bng, fng = rep_e[k]
        n_g, bmod = rep_l[k]
        s_in, s_pa, s_pb, s_out, s_w2, s_gb = sh[k]
        outs += [cc[k].reshape(D), wm[k], bmod, n_g, s_in, lg, lb, aws, abs_, s_w2, s_gb, bng, s_pa, s_pb, s_out, fng]
    return tuple(outs)
```

```python
import jax
import jax.numpy as jnp
from jax import lax
from jax.experimental import pallas as pl
from jax.experimental.pallas import tpu as pltpu

f32, bf16 = jnp.float32, jnp.bfloat16

D = 1024
CTX = 256
EPS = 1e-6
AW = 512
ACH = 128
GW = 64
KW = 256
VW = 512
NH = 4
HK = 64
HV = 128
RANK = 16
TAU = 16.0
CH = 64
QSCALE = HK ** -0.5
INW = 5152
NDEV = 8

PQ, PK, PV, PZB, PUA, PVA, PZA, PG1, PG2, PLR, PW = 0, 256, 512, 1024, 1536, 2048, 2560, 3072, 4096, 5120, 5248
LRW = 128

ADAM_LR, ADAM_B1, ADAM_B2, ADAM_EPS, ADAM_WD, ADAM_STEP = 0.001, 0.9, 0.999, 1e-08, 0.01, 10

VMEM_LIMIT = 56 * 1024 * 1024
MESH = pl.DeviceIdType.MESH


def _cp(sem=None):
    return pltpu.CompilerParams(dimension_semantics=sem, vmem_limit_bytes=VMEM_LIMIT)


def _dot(a, b):
    return jnp.dot(a.astype(bf16), b.astype(bf16), preferred_element_type=f32)


def _nt(a, b):
    return lax.dot_general(a.astype(bf16), b.astype(bf16), (((1,), (1,)), ((), ())), preferred_element_type=f32)


def _tn(a, b):
    return lax.dot_general(a.astype(bf16), b.astype(bf16), (((0,), (0,)), ((), ())), preferred_element_type=f32)


def _dot_hi(a, b):
    return jnp.dot(a, b, preferred_element_type=f32, precision=lax.Precision.HIGHEST)


def _sigmoid(x):
    return 1.0 / (1.0 + jnp.exp(-x))


def _log_sigmoid(x):
    return jnp.minimum(x, 0.0) - jnp.log(1.0 + jnp.exp(-jnp.abs(x)))


def _silu_and_grad(z):
    s = _sigmoid(z)
    return z * s, s * (1.0 + z * (1.0 - s))


def _me():
    return 4 * lax.axis_index("x") + 2 * lax.axis_index("y") + lax.axis_index("c")


def _peer(k):
    x, y, c = lax.axis_index("x"), lax.axis_index("y"), lax.axis_index("c")
    px = 1 - x if k & 4 else x
    py = 1 - y if k & 2 else y
    pc = 1 - c if k & 1 else c
    return (px, py, pc), 4 * px + 2 * py + pc


def _all_gather(v, name):
    r, c = v.shape

    def body(v_ref, out_ref, send_sems, recv_sems, local_sem):
        me = _me()
        mine = pltpu.make_async_copy(v_ref, out_ref.at[me], local_sem)
        mine.start()
        sends, recvs = [], []
        for k in range(1, NDEV):
            dev, idx = _peer(k)
            sends.append(pltpu.make_async_remote_copy(
                src_ref=v_ref, dst_ref=out_ref.at[me], send_sem=send_sems.at[k - 1], recv_sem=recv_sems.at[k - 1],
                device_id=dev, device_id_type=MESH))
            recvs.append(pltpu.make_async_remote_copy(
                src_ref=v_ref, dst_ref=out_ref.at[idx], send_sem=send_sems.at[k - 1], recv_sem=recv_sems.at[k - 1],
                device_id=dev, device_id_type=MESH))
        for cp in sends:
            cp.start()
        for cp in recvs:
            cp.wait_recv()
        for cp in sends:
            cp.wait_send()
        mine.wait()

    return pl.pallas_call(
        body, name=name, out_shape=jax.ShapeDtypeStruct((NDEV, r, c), v.dtype),
        in_specs=[pl.BlockSpec(memory_space=pl.ANY)], out_specs=pl.BlockSpec(memory_space=pl.ANY),
        scratch_shapes=[pltpu.SemaphoreType.DMA((NDEV - 1,)), pltpu.SemaphoreType.DMA((NDEV - 1,)), pltpu.SemaphoreType.DMA(())],
    )(v)


def _fanout(srcs, dsts, send_sems, recv_sems, local_sems, owners=None):
    me = _me()
    n = len(srcs)
    owns = lambda a, j: True if owners is None or owners[a] is None else owners[a](j)

    def guarded(cond, fn):
        if cond is True:
            fn()
        else:
            pl.when(cond)(fn)

    def copies(with_recvs):
        local = [pltpu.make_async_copy(srcs[a](me), dsts[a](me), local_sems.at[a]) for a in range(n)]
        sends, recvs = [], []
        for k in range(1, NDEV):
            dev, idx = _peer(k)
            for a in range(n):
                s = (k - 1) * n + a
                sends.append((owns(a, idx), pltpu.make_async_remote_copy(
                    src_ref=srcs[a](idx), dst_ref=dsts[a](me), send_sem=send_sems.at[s], recv_sem=recv_sems.at[s],
                    device_id=dev, device_id_type=MESH)))
                if with_recvs:
                    recvs.append((owns(a, me), pltpu.make_async_remote_copy(
                        src_ref=srcs[a](idx), dst_ref=dsts[a](idx), send_sem=send_sems.at[s], recv_sem=recv_sems.at[s],
                        device_id=dev, device_id_type=MESH)))
        return local, sends, recvs

    def start():
        local, sends, _ = copies(False)
        for a, cp in enumerate(local):
            guarded(owns(a, me), cp.start)
        for cond, cp in sends:
            guarded(cond, cp.start)

    def finish():
        local, sends, recvs = copies(True)
        for cond, cp in recvs:
            guarded(cond, cp.wait_recv)
        for cond, cp in sends:
            guarded(cond, cp.wait_send)
        for a, cp in enumerate(local):
            guarded(owns(a, me), cp.wait)

    return start, finish


class _Comm:
    def __init__(self, ins, outs, plan):
        self.ins, self.outs, self.plan = list(ins), list(outs), plan
        self.n = len(self.outs)

    def specs(self):
        hbm = pl.BlockSpec(memory_space=pl.ANY)
        return [hbm] * len(self.ins), [hbm] * self.n, _fanout_sems(self.n) if self.n else []

    def run(self, in_refs, out_refs, sems, nsteps, compute):
        if not self.n:
            compute()
            return

        def hooks():
            srcs, dsts, owners = self.plan(in_refs, out_refs)
            return _fanout(srcs, dsts, sems[0], sems[1], sems[2], owners)

        pl.when(pl.program_id(0) == 0)(lambda: hooks()[0]())
        compute()
        pl.when(pl.program_id(0) == nsteps - 1)(lambda: hooks()[1]())


LATE_MID_STEP = 1


class _LateComm:
    def __init__(self, blocks, dgt):
        sds = jax.ShapeDtypeStruct
        self.ins = [blocks, dgt]
        self.outs = [sds((D, SHARD), bf16), sds((D, SHARD), bf16), sds((4, D, SHARD), bf16), sds(dgt.shape, f32)]
        self.n = len(self.outs)

    def specs(self):
        hbm = pl.BlockSpec(memory_space=pl.ANY)
        scratch = [pltpu.VMEM((3, D, SHARD), bf16), pltpu.SemaphoreType.DMA((2,)), pltpu.SemaphoreType.DMA((4,)),
                   pltpu.SemaphoreType.DMA((3,))] + _fanout_sems(1)
        return [hbm] * 2, [hbm] * self.n, scratch

    def run(self, in_refs, out_refs, scratch, nsteps, compute):
        late_ref, dgt_ref = in_refs
        sib_ref, pair_ref, parts_ref, gt_ref = out_refs
        vbuf, send_sems, recv_sems, local_sems, g_send, g_recv, g_local = scratch
        x, y, c = lax.axis_index("x"), lax.axis_index("y"), lax.axis_index("c")
        chip = 2 * x + y
        is_owner_chip = chip == 0
        step = pl.program_id(0)

        def to_sibling():
            return pltpu.make_async_remote_copy(src_ref=late_ref.at[1 - c], dst_ref=sib_ref, send_sem=send_sems.at[0],
                                                recv_sem=recv_sems.at[0], device_id=(x, y, 1 - c), device_id_type=MESH)

        def to_owner(k):
            return pltpu.make_async_remote_copy(src_ref=pair_ref, dst_ref=parts_ref.at[k], send_sem=send_sems.at[1],
                                                recv_sem=recv_sems.at[k], device_id=(0, 0, c), device_id_type=MESH)

        def own_copy():
            return pltpu.make_async_copy(pair_ref, parts_ref.at[0], local_sems.at[2])

        def gates():
            return _fanout([lambda j: dgt_ref.at[j]], [lambda j: gt_ref.at[j]], g_send, g_recv, g_local)

        @pl.when(step == 0)
        def _():
            to_sibling().start()
            gates()[0]()

        compute()

        @pl.when(step == LATE_MID_STEP)
        def _():
            mine = pltpu.make_async_copy(late_ref.at[c], vbuf.at[0], local_sems.at[0])
            mine.start()
            to_sibling().wait_recv()
            theirs = pltpu.make_async_copy(sib_ref, vbuf.at[1], local_sems.at[1])
            theirs.start()
            mine.wait()
            theirs.wait()
            vbuf[2] = (vbuf[0].astype(f32) + vbuf[1].astype(f32)).astype(bf16)
            pltpu.sync_copy(vbuf.at[2], pair_ref)
            pl.when(is_owner_chip)(lambda: own_copy().start())
            pl.when(jnp.logical_not(is_owner_chip))(lambda: to_owner(chip).start())

        @pl.when(step == nsteps - 1)
        def _():
            @pl.when(is_owner_chip)
            def _():
                for k in range(1, 4):
                    to_owner(k).wait_recv()
                own_copy().wait()

            pl.when(jnp.logical_not(is_owner_chip))(lambda: to_owner(chip).wait_send())
            to_sibling().wait_send()
            gates()[1]()


def _fanout_sems(n):
    return [pltpu.SemaphoreType.DMA(((NDEV - 1) * n,)), pltpu.SemaphoreType.DMA(((NDEV - 1) * n,)), pltpu.SemaphoreType.DMA((n,))]


def _lanes(j):
    return pl.ds(pl.multiple_of(j * 128, 128), 128)


def _gather_first(c_row, cctx_row, wm, bm, wi, gate):
    def body(c_ref, cctx_ref, wm_ref, bm_ref, wi_ref, g_ref, cs_ref, mods_ref, owi, og, call_ref, mine_ref,
             send_sems, recv_sems, local_sems, c_send, c_recv, c_local, m_send, m_recv, m_local):
        x, y, c = lax.axis_index("x"), lax.axis_index("y"), lax.axis_index("c")
        sibling = (x, y, 1 - c)
        chips = [(1 - x, y), (x, 1 - y), (1 - x, 1 - y)]
        index = lambda px, py, pc: 4 * px + 2 * py + pc
        arrays = ((wi_ref, owi), (g_ref, og))
        n = len(arrays)

        def copy(a, k, block, to, own=False):
            src, out = arrays[a]
            return pltpu.make_async_remote_copy(
                src_ref=src if own else out.at[index(*block)], dst_ref=out.at[index(*block)],
                send_sem=send_sems.at[k * n + a], recv_sem=recv_sems.at[k * n + a], device_id=to, device_id_type=MESH)

        c_start, c_finish = _fanout([lambda j: c_ref], [lambda j: call_ref.at[j]], c_send, c_recv, c_local)
        c_start()
        mine = [pltpu.make_async_copy(src, out.at[index(x, y, c)], local_sems.at[a]) for a, (src, out) in enumerate(arrays)]
        first = [copy(a, 0, (x, y, c), sibling, own=True) for a in range(n)]
        first += [copy(a, 1 + j, (x, y, c), (*chip, c), own=True) for j, chip in enumerate(chips) for a in range(n)]
        for cp in mine + first:
            cp.start()

        c_finish()
        cs = jnp.concatenate([call_ref[j] for j in range(NDEV)] + [cctx_ref[...], jnp.zeros((16 - NDEV - 1, D), f32)], axis=0)
        cs_ref[...] = cs
        s, _ = _silu_and_grad(cs)
        mine_ref[...] = _dot_hi(s, wm_ref[...]) + bm_ref[...]
        m_start, m_finish = _fanout([lambda j: mine_ref], [lambda j: mods_ref.at[j]], m_send, m_recv, m_local)
        m_start()

        passed = []
        for j, chip in enumerate(chips):
            for a in range(n):
                copy(a, 1 + j, (*chip, c), (x, y, c)).wait_recv()
            for a in range(n):
                cp = copy(a, 4 + j, (*chip, c), sibling)
                cp.start()
                passed.append(cp)
        for a in range(n):
            copy(a, 0, sibling, (x, y, c)).wait_recv()
        for j, chip in enumerate(chips):
            for a in range(n):
                copy(a, 4 + j, (*chip, 1 - c), (x, y, c)).wait_recv()
        for cp in first + passed:
            cp.wait_send()
        for cp in mine:
            cp.wait()
        m_finish()

    hbm = pl.BlockSpec(memory_space=pl.ANY)
    vm = pl.BlockSpec(memory_space=pltpu.VMEM)
    ncol = wm.shape[1]
    return pl.pallas_call(
        body, name="gather_first",
        out_shape=(jax.ShapeDtypeStruct((16, D), f32), jax.ShapeDtypeStruct((NDEV, 16, ncol), f32),
                   jax.ShapeDtypeStruct((NDEV,) + wi.shape, bf16), jax.ShapeDtypeStruct((NDEV,) + gate.shape, f32)),
        in_specs=[vm, vm, vm, vm, hbm, hbm], out_specs=(vm, vm, hbm, hbm),
        scratch_shapes=[pltpu.VMEM((NDEV, 1, D), f32), pltpu.VMEM((16, ncol), f32)] + _fanout_sems(2) + _fanout_sems(1) + _fanout_sems(1),
        compiler_params=_cp(),
    )(c_row, cctx_row, wm, bm, wi, gate)


SHARD = INW // NDEV
ROWS_RP = 128


def _overlap(lo, hi, a, b):
    s, e = max(lo, a), min(hi, b)
    return (s, e) if s < e else None


def _repack_w(wg):
    segs = ((0, 1024, PQ), (1024, 1024 + 2 * RANK, PLR), (1024 + 2 * RANK, INW, PZB))

    def body(g_ref, o_ref):
        for j in range(NDEV):
            lo, hi = j * SHARD, (j + 1) * SHARD
            for a, b, pad0 in segs:
                ov = _overlap(lo, hi, a, b)
                if ov:
                    s, e = ov
                    o_ref[:, pad0 + s - a:pad0 + e - a] = g_ref[j, :, s - lo:e - lo]
        o_ref[:, PLR + 2 * RANK:PW] = jnp.zeros((ROWS_RP, PW - PLR - 2 * RANK), bf16)

    return pl.pallas_call(
        body, name="repack_w", grid=(D // ROWS_RP,), out_shape=jax.ShapeDtypeStruct((D, PW), bf16),
        in_specs=[pl.BlockSpec((NDEV, ROWS_RP, SHARD), lambda i: (0, i, 0))],
        out_specs=pl.BlockSpec((ROWS_RP, PW), lambda i: (i, 0)), compiler_params=_cp(("arbitrary",)),
    )(wg)


LATE_END = 1024 + 2 * RANK
LATE_DESTS = 2


def _pack_blocks(o_ref, srcs, dests, dtype):
    for n, j in enumerate(dests):
        lo, hi = j * SHARD, (j + 1) * SHARD
        done = lo
        for a, b, src in srcs:
            ov = _overlap(lo, hi, a, b)
            if ov:
                s, e = ov
                if s > done:
                    o_ref[n, :, done - lo:s - lo] = jnp.zeros((ROWS_RP, s - done), dtype)
                o_ref[n, :, s - lo:e - lo] = src[:, s - a:e - a].astype(dtype)
                done = e
        if done < hi:
            o_ref[n, :, done - lo:hi - lo] = jnp.zeros((ROWS_RP, hi - done), dtype)


def _pack_dw_early(dw_zbua, dw_va, dw_zag, row0, nrows, name):
    def body(zbua_ref, va_ref, zag_ref, o_ref):
        _pack_blocks(o_ref, ((LATE_END, 2080, zbua_ref), (2080, 2592, va_ref), (2592, INW, zag_ref)), range(NDEV), bf16)

    row = lambda w: pl.BlockSpec((ROWS_RP, w), lambda i: (i + row0 // ROWS_RP, 0))
    return pl.pallas_call(
        body, name=name, grid=(nrows // ROWS_RP,), out_shape=jax.ShapeDtypeStruct((NDEV, nrows, SHARD), bf16),
        in_specs=[row(2 * AW), row(AW), row(AW + 2 * D)],
        out_specs=pl.BlockSpec((NDEV, ROWS_RP, SHARD), lambda i: (0, i, 0)), compiler_params=_cp(("arbitrary",)),
    )(dw_zbua, dw_va, dw_zag)


def _pack_dw_late(dw_qkv, dw_lr, dwk_c, dwv_c, dwl_c):
    def body(qkv_ref, lr_ref, kc_ref, vc_ref, lc_ref, o_ref):
        qkv = qkv_ref[...] + jnp.concatenate([jnp.zeros((ROWS_RP, KW), f32), kc_ref[...], vc_ref[...]], axis=1)
        lr = lr_ref[...] + lc_ref[...]
        _pack_blocks(o_ref, ((0, 1024, qkv), (1024, LATE_END, lr)), range(LATE_DESTS), bf16)

    row = lambda w: pl.BlockSpec((ROWS_RP, w), lambda i: (i, 0))
    return pl.pallas_call(
        body, name="pack_dw_late", grid=(D // ROWS_RP,), out_shape=jax.ShapeDtypeStruct((LATE_DESTS, D, SHARD), bf16),
        in_specs=[row(2 * KW + VW), row(LRW), row(KW), row(VW), row(LRW)],
        out_specs=pl.BlockSpec((LATE_DESTS, ROWS_RP, SHARD), lambda i: (0, i, 0)), compiler_params=_cp(("arbitrary",)),
    )(dw_qkv, dw_lr, dwk_c, dwv_c, dwl_c)


def _mod_bwd(cs, dm, wm, m, v):
    def body(cs_ref, dm_ref, wm_ref, m_ref, v_ref, gw_ref, d_ref, mo_ref, vo_ref, gcs_ref, gc_ref, send_sems, recv_sems, local_sems):
        s, ds = _silu_and_grad(cs_ref[...])
        part = lax.dot_general(dm_ref[8:9, :], wm_ref[0], (((1,), (1,)), ((), ())), preferred_element_type=f32,
                               precision=lax.Precision.HIGHEST)
        gc_ref[...] = part * ds[8:9, :]
        start, finish = _fanout([lambda j: gc_ref], [lambda j: gcs_ref.at[j]], send_sems, recv_sems, local_sems)
        start()
        g = lax.dot_general(s, dm_ref[...], (((0,), (0,)), ((), ())), preferred_element_type=f32, precision=lax.Precision.HIGHEST)
        _adam_update(g[None], wm_ref, m_ref, v_ref, gw_ref, d_ref, mo_ref, vo_ref)
        finish()

    like = jax.ShapeDtypeStruct(wm.shape, f32)
    return pl.pallas_call(body, name="mod_bwd", out_shape=(like, like, like, like, jax.ShapeDtypeStruct((NDEV, 1, D), f32)),
                          scratch_shapes=[pltpu.VMEM((1, D), f32)] + _fanout_sems(1),
                          compiler_params=_cp())(cs, dm, wm, m, v)


def _adam_update(g, w_ref, m_ref, v_ref, go_ref, d_ref, mo_ref, vo_ref):
    c1 = 1.0 / (1.0 - ADAM_B1 ** ADAM_STEP)
    c2 = 1.0 / (1.0 - ADAM_B2 ** ADAM_STEP)
    mn = ADAM_B1 * m_ref[...] + (1.0 - ADAM_B1) * g
    vn = ADAM_B2 * v_ref[...] + (1.0 - ADAM_B2) * (g * g)
    go_ref[...] = g
    mo_ref[...] = mn
    vo_ref[...] = vn
    d_ref[...] = -ADAM_LR * ((mn * c1) / (jnp.sqrt(vn * c2) + ADAM_EPS) + ADAM_WD * w_ref[...])


def _adam_w_in(parts_a, parts_b, parts_late, w, m, v):
    na = EARLY_A_ROWS // ROWS_RP

    def body(a_ref, b_ref, l_ref, w_ref, m_ref, v_ref, go_ref, d_ref, mo_ref, vo_ref):
        me = _me()
        first = pl.program_id(0) < na
        early = jnp.where(first, a_ref[0], b_ref[0]).astype(f32)
        for i in range(1, NDEV):
            early = early + jnp.where(first, a_ref[i], b_ref[i]).astype(f32)
        late = l_ref[0].astype(f32)
        for i in range(1, 4):
            late = late + l_ref[i].astype(f32)
        g = jnp.where(me >= LATE_DESTS - 1, early, 0.0) + jnp.where(me < LATE_DESTS, late, 0.0)
        _adam_update(g, w_ref, m_ref, v_ref, go_ref, d_ref, mo_ref, vo_ref)

    blk = pl.BlockSpec((None, ROWS_RP, SHARD), lambda i: (0, i, 0))
    return pl.pallas_call(
        body, name="adam_w_in", grid=(D // ROWS_RP,), out_shape=tuple(jax.ShapeDtypeStruct((1, D, SHARD), f32) for _ in range(4)),
        in_specs=[pl.BlockSpec((NDEV, ROWS_RP, SHARD), lambda i: (0, jnp.minimum(i, na - 1), 0)),
                  pl.BlockSpec((NDEV, ROWS_RP, SHARD), lambda i: (0, jnp.maximum(i - na, 0), 0)),
                  pl.BlockSpec((4, ROWS_RP, SHARD), lambda i: (0, i, 0)), blk, blk, blk],
        out_specs=(blk, blk, blk, blk), compiler_params=_cp(("arbitrary",)),
    )(parts_a, parts_b, parts_late, w, m, v)


def _adam_params(parts, ws, ms, vs, name):
    p = parts.shape[0]
    k = len(ws)
    nrows = [w.size // 128 for w in ws]
    starts = [sum(nrows[:i]) for i in range(k)]

    def shaped(g, shape):
        if len(shape) == 2:
            return jnp.concatenate([g[r:r + 1] for r in range(g.shape[0])], axis=1)
        return g.reshape(shape)

    def body(*refs):
        g_ref = refs[0]
        w_refs, m_refs, v_refs = refs[1:1 + k], refs[1 + k:1 + 2 * k], refs[1 + 2 * k:1 + 3 * k]
        outs = refs[1 + 3 * k:]
        for i in range(k):
            rows = slice(starts[i], starts[i] + nrows[i])
            g = g_ref[0, rows, :]
            for j in range(1, p):
                g = g + g_ref[j, rows, :]
            _adam_update(shaped(g, ws[i].shape), w_refs[i], m_refs[i], v_refs[i], outs[i], outs[k + i], outs[2 * k + i], outs[3 * k + i])

    vm = pl.BlockSpec(memory_space=pltpu.VMEM)
    res = pl.pallas_call(
        body, name=name, out_shape=tuple(jax.ShapeDtypeStruct(w.shape, f32) for _ in range(4) for w in ws),
        in_specs=[vm] * (1 + 3 * k), out_specs=(vm,) * (4 * k), compiler_params=_cp(),
    )(parts, *ws, *ms, *vs)
    return [res[i * k:(i + 1) * k] for i in range(4)]


def _adam(parts, w, m, v, rows, name):
    p, r, c = parts.shape
    lead = w.ndim - 2

    def body(g_ref, w_ref, m_ref, v_ref, go_ref, d_ref, mo_ref, vo_ref):
        g = g_ref[0].astype(f32)
        for i in range(1, p):
            g = g + g_ref[i].astype(f32)
        _adam_update(g, w_ref, m_ref, v_ref, go_ref, d_ref, mo_ref, vo_ref)

    blk = pl.BlockSpec((None,) * lead + (rows, c), lambda i: (0,) * lead + (i, 0))
    return pl.pallas_call(
        body, name=name, grid=(r // rows,), out_shape=tuple(jax.ShapeDtypeStruct(w.shape, f32) for _ in range(4)),
        in_specs=[pl.BlockSpec((p, rows, c), lambda i: (0, i, 0)), blk, blk, blk], out_specs=(blk, blk, blk, blk),
        compiler_params=_cp(("arbitrary",)),
    )(parts, w, m, v)


def _in_proj(x, g, scale, shift, w_pad, tl, comm):
    l = x.shape[0]
    c_in, c_out, c_sems = comm.specs()

    def body(*refs):
        x_ref, g_ref, sc_ref, sh_ref, w_ref = refs[:5]
        cin = refs[5:5 + len(c_in)]
        p_ref, h_ref = refs[5 + len(c_in):7 + len(c_in)]
        cout = refs[7 + len(c_in):7 + len(c_in) + comm.n]
        sems = refs[7 + len(c_in) + comm.n:]

        def compute():
            xv = x_ref[...]
            r = lax.rsqrt(jnp.mean(xv * xv, axis=-1, keepdims=True) + EPS)
            h = (xv * r) * (g_ref[...] * (1.0 + sc_ref[...])) + sh_ref[...]
            hb = h.astype(bf16)
            h_ref[...] = hb
            p_ref[...] = jnp.dot(hb, w_ref[...], preferred_element_type=f32).astype(bf16)

        comm.run(cin, cout, sems, l // tl, compute)

    vec = pl.BlockSpec((1, D), lambda i: (0, 0))
    return pl.pallas_call(
        body, name="in_proj", grid=(l // tl,),
        out_shape=(jax.ShapeDtypeStruct((l, PW), bf16), jax.ShapeDtypeStruct((l, D), bf16)) + tuple(comm.outs),
        in_specs=[pl.BlockSpec((tl, D), lambda i: (i, 0)), vec, vec, vec, pl.BlockSpec((D, PW), lambda i: (0, 0))] + c_in,
        out_specs=(pl.BlockSpec((tl, PW), lambda i: (i, 0)), pl.BlockSpec((tl, D), lambda i: (i, 0))) + tuple(c_out),
        scratch_shapes=c_sems, compiler_params=_cp(("arbitrary",)),
    )(x, g, scale, shift, w_pad, *comm.ins)


def _tri(rev):
    i = lax.broadcasted_iota(jnp.int32, (CH, CH), 0)
    j = lax.broadcasted_iota(jnp.int32, (CH, CH), 1)
    return jnp.where((j >= i) if rev else (j <= i), 1.0, 0.0).astype(f32)


def _head_masks():
    lane = lax.broadcasted_iota(jnp.int32, (1, KW), 1) // HK
    return [jnp.where(lane == h, 1.0, 0.0).astype(f32) for h in range(NH)]


def _block_diag():
    r = lax.broadcasted_iota(jnp.int32, (VW, KW), 0) // HV
    c = lax.broadcasted_iota(jnp.int32, (VW, KW), 1) // HK
    return jnp.where(r == c, 1.0, 0.0).astype(f32)


def _decay(lr, w2, gb, tri, rev):
    logits = _dot(lr, w2) + gb
    a = _log_sigmoid(logits) * (1.0 / TAU)
    c = _dot_hi(tri, a)
    cl = c[0:1, :] if rev else c[CH - 1:CH, :]
    return logits, c, cl


def _state_fwd(k, v, c, cl, st, bd):
    return st * jnp.exp(cl) + bd * _tn(v, k * jnp.exp(cl - c))


def _state_bwd(k, v, c, cl, st0, dst, trit):
    ecl = jnp.exp(cl)
    edec = jnp.exp(cl - c)
    kdec = k * edec
    dv = _nt(kdec, dst)
    dkdec = _dot(v, dst)
    dcl = jnp.sum(dst * st0, axis=0, keepdims=True) * ecl + jnp.sum(dkdec * kdec, axis=0, keepdims=True)
    da = _dot_hi(trit, -dkdec * kdec) + dcl
    return dkdec * edec, dv, da, dst * ecl


def _bdot(a, b):
    return lax.dot_general(a.astype(bf16), b.astype(bf16), (((2,), (1,)), ((0,), (0,))), preferred_element_type=f32)


def _bnt(a, b):
    return lax.dot_general(a.astype(bf16), b.astype(bf16), (((2,), (2,)), ((0,), (0,))), preferred_element_type=f32)


def _btn(a, b):
    return lax.dot_general(a.astype(bf16), b.astype(bf16), (((1,), (1,)), ((0,), (0,))), preferred_element_type=f32)


def _scan_chunks(x, rev):
    nc = x.shape[0]
    hi = x.astype(bf16)
    r1 = x - hi.astype(f32)
    mid = r1.astype(bf16)
    lo = (r1 - mid.astype(f32)).astype(bf16)
    terms = jnp.concatenate([hi, mid, lo], axis=1)
    tri3 = jnp.broadcast_to(jnp.concatenate([_tri(rev)] * 3, axis=1).astype(bf16)[None], (nc, CH, 3 * CH))
    return lax.dot_general(tri3, terms, (((2,), (1,)), ((0,), (0,))), preferred_element_type=f32)


class _Tile:
    pass


def _tile_prep(q_ref, k_ref, lr_ref, w2_ref, gb_ref, rev, nc):
    t = _Tile()
    tg = nc * CH
    t.logits = _dot(lr_ref[...], w2_ref[...]) + gb_ref[...]
    c = _scan_chunks((_log_sigmoid(t.logits) * (1.0 / TAU)).reshape(nc, CH, KW), rev)
    cl = c[:, 0:1, :] if rev else c[:, CH - 1:CH, :]
    k = k_ref[...].astype(f32).reshape(nc, CH, KW)
    t.ec, t.enc, t.edec, t.ecl = jnp.exp(c), jnp.exp(-c), jnp.exp(cl - c), jnp.exp(cl)
    t.qd = q_ref[...].astype(f32).reshape(nc, CH, KW) * t.ec * QSCALE
    t.kd = k * t.enc
    t.kdec = k * t.edec
    hm = _head_masks()
    t.tri4 = jnp.concatenate([_tri(rev)] * NH, axis=0)[None]
    t.qs = jnp.concatenate([t.qd * hm[h] for h in range(NH)], axis=1)
    t.pst = _bnt(t.qs, t.kd) * t.tri4
    return t


def _gla_fwd(p, w2p, gb, s0, rev, tg, name):
    l = p.shape[0]
    nb, nc = l // tg, tg // CH

    def body(q_ref, k_ref, v_ref, lr_ref, w2_ref, gb_ref, s0_ref, o_ref, st_ref, st):
        @pl.when(pl.program_id(0) == 0)
        def _():
            st[...] = s0_ref[...]

        t = _tile_prep(q_ref, k_ref, lr_ref, w2_ref, gb_ref, rev, nc)
        v = v_ref[...].reshape(nc, CH, VW)
        intra = jnp.concatenate([_bdot(t.pst[:, h * CH:(h + 1) * CH], v[:, :, h * HV:(h + 1) * HV]) for h in range(NH)], axis=2)
        kv = _btn(v, t.kdec) * _block_diag()[None]
        s = st[...]
        for n in (range(nc - 1, -1, -1) if rev else range(nc)):
            st_ref[n] = s
            s = s * t.ecl[n] + kv[n]
        st[...] = s
        o_ref[...] = (_bnt(t.qd, st_ref[...]) + intra).reshape(tg, VW)

    blk = (lambda i: nb - 1 - i) if rev else (lambda i: i)
    return pl.pallas_call(
        body, name=name, grid=(nb,),
        out_shape=(jax.ShapeDtypeStruct((l, VW), f32), jax.ShapeDtypeStruct((l // CH, VW, KW), f32)),
        in_specs=[pl.BlockSpec((tg, KW), lambda i: (blk(i), PQ // KW)), pl.BlockSpec((tg, KW), lambda i: (blk(i), PK // KW)),
                  pl.BlockSpec((tg, VW), lambda i: (blk(i), PV // VW)), pl.BlockSpec((tg, LRW), lambda i: (blk(i), PLR // LRW)),
                  pl.BlockSpec((LRW, KW), lambda i: (0, 0)), pl.BlockSpec((1, KW), lambda i: (0, 0)),
                  pl.BlockSpec((VW, KW), lambda i: (0, 0))],
        out_specs=(pl.BlockSpec((tg, VW), lambda i: (blk(i), 0)), pl.BlockSpec((nc, VW, KW), lambda i: (blk(i), 0, 0))),
        scratch_shapes=[pltpu.VMEM((VW, KW), f32)],
        compiler_params=_cp(("arbitrary",)),
    )(p, p, p, p, w2p, gb, s0)


def _gla_bwd(p, do, states, w2p, gb, prev, rev, tg, name, comm):
    l = p.shape[0]
    nb, nc = l // tg, tg // CH
    out_dt = f32 if prev is None else bf16
    c_in, c_out, c_sems = comm.specs()

    def body(*refs):
        q_ref, k_ref, v_ref, lr_ref, do_ref, st_ref, w2_ref, gb_ref = refs[:8]
        refs = refs[8:]
        if prev is not None:
            pq_ref, pl_ref = refs[:2]
            refs = refs[2:]
        cin, refs = refs[:len(c_in)], refs[len(c_in):]
        dqkv_ref, dlr_ref, dw2_ref, dgb_ref, ds0_ref = refs[:5]
        cout, dst, ds_buf, sems = refs[5:5 + comm.n], refs[5 + comm.n], refs[6 + comm.n], refs[7 + comm.n:]
        comm.run(cin, cout, sems, nb, lambda: compute(q_ref, k_ref, v_ref, lr_ref, do_ref, st_ref, w2_ref, gb_ref,
                                                      pq_ref if prev is not None else None, pl_ref if prev is not None else None,
                                                      dqkv_ref, dlr_ref, dw2_ref, dgb_ref, ds0_ref, dst, ds_buf))

    def compute(q_ref, k_ref, v_ref, lr_ref, do_ref, st_ref, w2_ref, gb_ref, pq_ref, pl_ref,
                dqkv_ref, dlr_ref, dw2_ref, dgb_ref, ds0_ref, dst, ds_buf):
        @pl.when(pl.program_id(0) == 0)
        def _():
            dst[...] = jnp.zeros_like(dst)
            dw2_ref[...] = jnp.zeros_like(dw2_ref)
            dgb_ref[...] = jnp.zeros_like(dgb_ref)

        t = _tile_prep(q_ref, k_ref, lr_ref, w2_ref, gb_ref, rev, nc)
        hm = _head_masks()
        v = v_ref[...].reshape(nc, CH, VW)
        do = do_ref[...].reshape(nc, CH, VW)
        heads = lambda a, h: a[:, :, h * HV:(h + 1) * HV]
        dpst = jnp.concatenate([_bnt(heads(do, h), heads(v, h)) for h in range(NH)], axis=1) * t.tri4
        dv = jnp.concatenate([_btn(t.pst[:, h * CH:(h + 1) * CH], heads(do, h)) for h in range(NH)], axis=2)
        dqd = _bdot(do, st_ref[...])
        for h in range(NH):
            dqd = dqd + hm[h] * _bdot(dpst[:, h * CH:(h + 1) * CH], t.kd)
        dkd = _btn(dpst, t.qs)
        u = _btn(do, t.qd) * _block_diag()[None]
        d = dst[...]
        for n in (range(nc) if rev else range(nc - 1, -1, -1)):
            ds_buf[n] = d
            d = d * t.ecl[n] + u[n]
        dst[...] = d
        ds0_ref[...] = d
        ds = ds_buf[...]
        dv = dv + _bnt(t.kdec, ds)
        dkdec = _bdot(v, ds)
        dcl = jnp.sum(ds * st_ref[...], axis=1, keepdims=True) * t.ecl + jnp.sum(dkdec * t.kdec, axis=1, keepdims=True)
        dc = dqd * t.qd - dkd * t.kd - dkdec * t.kdec
        da = _scan_chunks(dc, not rev) + dcl
        dq = dqd * t.ec * QSCALE
        dk = dkd * t.enc + dkdec * t.edec
        dlog = da.reshape(tg, KW) * _sigmoid(-t.logits) * (1.0 / TAU)
        dlr = _nt(dlog, w2_ref[...])
        dw2_ref[...] += _tn(lr_ref[...], dlog)
        dgb_ref[...] += jnp.sum(dlog, axis=0, keepdims=True)
        dqkv = jnp.concatenate([dq, dk, dv], axis=2).reshape(tg, 2 * KW + VW)
        if prev is not None:
            dqkv = dqkv + pq_ref[...]
            dlr = dlr + pl_ref[...]
        dqkv_ref[...] = dqkv.astype(out_dt)
        dlr_ref[...] = dlr.astype(out_dt)

    blk = (lambda i: i) if rev else (lambda i: nb - 1 - i)
    in_specs = [pl.BlockSpec((tg, KW), lambda i: (blk(i), PQ // KW)), pl.BlockSpec((tg, KW), lambda i: (blk(i), PK // KW)),
                pl.BlockSpec((tg, VW), lambda i: (blk(i), PV // VW)), pl.BlockSpec((tg, LRW), lambda i: (blk(i), PLR // LRW)),
                pl.BlockSpec((tg, VW), lambda i: (blk(i), 0)), pl.BlockSpec((nc, VW, KW), lambda i: (blk(i), 0, 0)),
                pl.BlockSpec((LRW, KW), lambda i: (0, 0)), pl.BlockSpec((1, KW), lambda i: (0, 0))]
    args = [p, p, p, p, do, states, w2p, gb]
    if prev is not None:
        in_specs += [pl.BlockSpec((tg, 2 * KW + VW), lambda i: (blk(i), 0)), pl.BlockSpec((tg, LRW), lambda i: (blk(i), 0))]
        args += list(prev)
    return pl.pallas_call(
        body, name=name, grid=(nb,),
        out_shape=(jax.ShapeDtypeStruct((l, 2 * KW + VW), out_dt), jax.ShapeDtypeStruct((l, LRW), out_dt),
                   jax.ShapeDtypeStruct((LRW, KW), f32), jax.ShapeDtypeStruct((1, KW), f32), jax.ShapeDtypeStruct((VW, KW), f32))
        + tuple(comm.outs),
        in_specs=in_specs + c_in,
        out_specs=(pl.BlockSpec((tg, 2 * KW + VW), lambda i: (blk(i), 0)), pl.BlockSpec((tg, LRW), lambda i: (blk(i), 0)),
                   pl.BlockSpec((LRW, KW), lambda i: (0, 0)), pl.BlockSpec((1, KW), lambda i: (0, 0)),
                   pl.BlockSpec((VW, KW), lambda i: (0, 0))) + tuple(c_out),
        scratch_shapes=[pltpu.VMEM((VW, KW), f32), pltpu.VMEM((nc, VW, KW), f32)] + c_sems,
        compiler_params=_cp(("arbitrary",)),
    )(*args, *comm.ins)


def _ctx_hidden(ctx_ref, g_ref, sc_ref, sh_ref):
    xv = ctx_ref[...]
    r = lax.rsqrt(jnp.mean(xv * xv, axis=-1, keepdims=True) + EPS)
    xn = xv * r
    return xn, xn * (g_ref[...] * (1.0 + sc_ref[...])) + sh_ref[...]


_CTX_W_SPECS = [pl.BlockSpec((D, KW), lambda i: (0, PK // KW)), pl.BlockSpec((D, VW), lambda i: (0, PV // VW)),
                pl.BlockSpec((D, LRW), lambda i: (0, PLR // LRW))]


def _ctx_fwd(ctx, g, scale, shift, w_pad, w2f, w2b, gbf, gbb):
    ncc = CTX // CH

    def body(ctx_ref, g_ref, sc_ref, sh_ref, wk_ref, wv_ref, wl_ref, w2f_ref, w2b_ref, gbf_ref, gbb_ref, sf_ref, sb_ref):
        _, hc = _ctx_hidden(ctx_ref, g_ref, sc_ref, sh_ref)
        k, v, lr = _dot(hc, wk_ref[...]), _dot(hc, wv_ref[...]), _dot(hc, wl_ref[...])
        bd = _block_diag()
        for rev, w2_ref, gb_ref, out in ((False, w2f_ref, gbf_ref, sf_ref), (True, w2b_ref, gbb_ref, sb_ref)):
            tri = _tri(rev)
            st = jnp.zeros((VW, KW), f32)
            for j in (range(ncc - 1, -1, -1) if rev else range(ncc)):
                rows = slice(j * CH, (j + 1) * CH)
                _, c, cl = _decay(lr[rows], w2_ref[...], gb_ref[...], tri, rev)
                st = _state_fwd(k[rows], v[rows], c, cl, st, bd)
            out[...] = st

    vec = pl.BlockSpec((1, D), lambda i: (0, 0))
    w2s = pl.BlockSpec((LRW, KW), lambda i: (0, 0))
    gbs = pl.BlockSpec((1, KW), lambda i: (0, 0))
    sts = pl.BlockSpec((VW, KW), lambda i: (0, 0))
    return pl.pallas_call(
        body, name="ctx_fwd", grid=(1,), out_shape=(jax.ShapeDtypeStruct((VW, KW), f32),) * 2,
        in_specs=[pl.BlockSpec((CTX, D), lambda i: (0, 0)), vec, vec, vec] + _CTX_W_SPECS + [w2s, w2s, gbs, gbs],
        out_specs=(sts, sts), compiler_params=_cp(("arbitrary",)),
    )(ctx, g, scale, shift, w_pad, w_pad, w_pad, w2f, w2b, gbf, gbb)


def _ctx_bwd(ctx, g, scale, shift, w_pad, w2f, w2b, gbf, gbb, dsf, dsb):
    ncc = CTX // CH

    def body(ctx_ref, g_ref, sc_ref, sh_ref, wk_ref, wv_ref, wl_ref, w2f_ref, w2b_ref, gbf_ref, gbb_ref, dsf_ref, dsb_ref,
             dwk_ref, dwv_ref, dwl_ref, dmod_ref, dg_ref, dw2_ref, dgb_ref):
        xn, hc = _ctx_hidden(ctx_ref, g_ref, sc_ref, sh_ref)
        k, v, lr = _dot(hc, wk_ref[...]), _dot(hc, wv_ref[...]), _dot(hc, wl_ref[...])
        bd = _block_diag()
        dk_rows, dv_rows, dl_rows = [None] * ncc, [None] * ncc, [None] * ncc
        for d, (rev, w2_ref, gb_ref, ds_ref) in enumerate(((False, w2f_ref, gbf_ref, dsf_ref), (True, w2b_ref, gbb_ref, dsb_ref))):
            tri = _tri(rev)
            order = list(range(ncc - 1, -1, -1) if rev else range(ncc))
            st, saved = jnp.zeros((VW, KW), f32), {}
            for j in order:
                rows = slice(j * CH, (j + 1) * CH)
                logits, c, cl = _decay(lr[rows], w2_ref[...], gb_ref[...], tri, rev)
                saved[j] = (logits, c, cl, st)
                st = _state_fwd(k[rows], v[rows], c, cl, st, bd)
            dst = ds_ref[...]
            dw2 = jnp.zeros((LRW, KW), f32)
            dgb = jnp.zeros((1, KW), f32)
            for j in reversed(order):
                rows = slice(j * CH, (j + 1) * CH)
                logits, c, cl, st0 = saved[j]
                dk, dv, da, dst = _state_bwd(k[rows], v[rows], c, cl, st0, dst, _tri(not rev))
                dlog = da * _sigmoid(-logits) * (1.0 / TAU)
                dl = _nt(dlog, w2_ref[...])
                dw2 = dw2 + _tn(lr[rows], dlog)
                dgb = dgb + jnp.sum(dlog, axis=0, keepdims=True)
                dk_rows[j] = dk if dk_rows[j] is None else dk_rows[j] + dk
                dv_rows[j] = dv if dv_rows[j] is None else dv_rows[j] + dv
                dl_rows[j] = dl if dl_rows[j] is None else dl_rows[j] + dl
            dw2_ref[d] = dw2
            dgb_ref[d] = dgb
        dk, dv, dl = (jnp.concatenate(t, axis=0) for t in (dk_rows, dv_rows, dl_rows))
        dwk_ref[...] = _tn(hc, dk)
        dwv_ref[...] = _tn(hc, dv)
        dwl_ref[...] = _tn(hc, dl)
        dh = _nt(dk, wk_ref[...]) + _nt(dv, wv_ref[...]) + _nt(dl, wl_ref[...])
        gx = dh * xn
        dmod_ref[:, 0:D] = jnp.sum(dh, axis=0, keepdims=True)
        dmod_ref[:, D:2 * D] = jnp.sum(gx, axis=0, keepdims=True) * g_ref[...]
        dg_ref[...] = jnp.sum(gx, axis=0, keepdims=True) * (1.0 + sc_ref[...])

    vec = pl.BlockSpec((1, D), lambda i: (0, 0))
    w2s = pl.BlockSpec((LRW, KW), lambda i: (0, 0))
    gbs = pl.BlockSpec((1, KW), lambda i: (0, 0))
    sts = pl.BlockSpec((VW, KW), lambda i: (0, 0))
    full = lambda *s: pl.BlockSpec(s, lambda i: (0,) * len(s))
    return pl.pallas_call(
        body, name="ctx_bwd", grid=(1,),
        out_shape=(jax.ShapeDtypeStruct((D, KW), f32), jax.ShapeDtypeStruct((D, VW), f32), jax.ShapeDtypeStruct((D, LRW), f32),
                   jax.ShapeDtypeStruct((1, 2 * D), f32), jax.ShapeDtypeStruct((1, D), f32),
                   jax.ShapeDtypeStruct((2, LRW, KW), f32), jax.ShapeDtypeStruct((2, 1, KW), f32)),
        in_specs=[pl.BlockSpec((CTX, D), lambda i: (0, 0)), vec, vec, vec] + _CTX_W_SPECS + [w2s, w2s, gbs, gbs, sts, sts],
        out_specs=(full(D, KW), full(D, VW), full(D, LRW), full(1, 2 * D), full(1, D), full(2, LRW, KW), full(2, 1, KW)),
        compiler_params=_cp(("arbitrary",)),
    )(ctx, g, scale, shift, w_pad, w_pad, w_pad, w2f, w2b, gbf, gbb, dsf, dsb)


def _layernorm(va, g, b):
    mu = jnp.mean(va, axis=-1, keepdims=True)
    xc = va - mu
    rstd = lax.rsqrt(jnp.mean(xc * xc, axis=-1, keepdims=True) + EPS)
    vhat = xc * rstd
    return vhat, rstd, vhat * g + b


def _mix_fwd(p, ln_g, ln_b, ws, bs_t):
    l = p.shape[0]
    nch = l // ACH
    half = AW // 2

    def body(p_ref, g_ref, b_ref, ws_ref, bs_ref, sv_ref, va_buf, col_buf, sem):
        cp = pltpu.make_async_copy(p_ref.at[:, pl.ds(PVA, AW)], va_buf, sem)
        cp.start()
        cp.wait()

        def rows_step(n, carry):
            rows = pl.ds(pl.multiple_of(n * ACH, ACH), ACH)
            _, _, vn = _layernorm(va_buf[rows, :].astype(f32), g_ref[...], b_ref[...])
            for gi in range(2):
                sl = slice(gi * ACH, (gi + 1) * ACH)
                sv_ref[rows, sl] = (_dot(ws_ref[gi], vn[:, sl]) + bs_ref[:, gi:gi + 1]).astype(bf16)
            col_buf[0, rows, :] = vn[:, half:half + ACH]
            col_buf[1, rows, :] = vn[:, half + ACH:]
            return carry

        lax.fori_loop(0, nch, rows_step, 0, unroll=8)

        def cols_step(cidx, carry):
            rows = pl.ds(cidx, ACH, stride=GW)
            for gi in range(2, 4):
                col_buf[gi - 2, rows, :] = _dot(ws_ref[gi], col_buf[gi - 2, rows, :]) + bs_ref[:, gi:gi + 1]
            return carry

        lax.fori_loop(0, GW, cols_step, 0, unroll=8)

        def out_step(n, carry):
            rows = pl.ds(pl.multiple_of(n * ACH, ACH), ACH)
            sv_ref[rows, half:half + ACH] = col_buf[0, rows, :].astype(bf16)
            sv_ref[rows, half + ACH:] = col_buf[1, rows, :].astype(bf16)
            return carry

        lax.fori_loop(0, nch, out_step, 0, unroll=8)

    vm = pl.BlockSpec(memory_space=pltpu.VMEM)
    return pl.pallas_call(
        body, name="mix_fwd", out_shape=jax.ShapeDtypeStruct((l, AW), bf16),
        in_specs=[pl.BlockSpec(memory_space=pl.ANY), vm, vm, vm, vm], out_specs=vm,
        scratch_shapes=[pltpu.VMEM((l, AW), bf16), pltpu.VMEM((2, l, ACH), f32), pltpu.SemaphoreType.DMA(())],
        compiler_params=_cp(),
    )(p, ln_g, ln_b, ws, bs_t)


def _mix_bwd(p, dsv, ln_g, ln_b, ws_t):
    l = p.shape[0]
    nch = l // ACH
    half = AW // 2

    def body(p_ref, dsv_hbm, g_ref, b_ref, wst_ref, dva_ref, dws_ref, dbs_ref, dg_ref, db_ref, va_buf, dsv_buf, vn_col, ds_col, sems):
        cp1 = pltpu.make_async_copy(p_ref.at[:, pl.ds(PVA, AW)], va_buf, sems.at[0])
        cp2 = pltpu.make_async_copy(dsv_hbm, dsv_buf, sems.at[1])
        cp1.start()
        cp2.start()
        dws_ref[...] = jnp.zeros_like(dws_ref)
        dbs_ref[...] = jnp.zeros_like(dbs_ref)
        dg_ref[...] = jnp.zeros_like(dg_ref)
        db_ref[...] = jnp.zeros_like(db_ref)
        cp1.wait()
        cp2.wait()

        def rows_step(n, carry):
            rows = pl.ds(pl.multiple_of(n * ACH, ACH), ACH)
            _, _, vn = _layernorm(va_buf[rows, :].astype(f32), g_ref[...], b_ref[...])
            ds = dsv_buf[rows, :].astype(f32)
            for gi in range(2):
                sl = slice(gi * ACH, (gi + 1) * ACH)
                dws_ref[gi] += _nt(ds[:, sl], vn[:, sl])
                dbs_ref[gi] += ds[:, sl]
            for gi in range(2):
                sl = slice(half + gi * ACH, half + (gi + 1) * ACH)
                vn_col[gi, rows, :] = vn[:, sl]
                ds_col[gi, rows, :] = ds[:, sl]
            return carry

        lax.fori_loop(0, nch, rows_step, 0, unroll=8)

        def cols_step(cidx, carry):
            rows = pl.ds(cidx, ACH, stride=GW)
            for gi in range(2, 4):
                ds = ds_col[gi - 2, rows, :]
                dws_ref[gi] += _nt(ds, vn_col[gi - 2, rows, :])
                dbs_ref[gi] += ds
                ds_col[gi - 2, rows, :] = _dot(wst_ref[gi], ds)
            return carry

        lax.fori_loop(0, GW, cols_step, 0, unroll=8)

        def out_step(n, carry):
            rows = pl.ds(pl.multiple_of(n * ACH, ACH), ACH)
            vhat, rstd, _ = _layernorm(va_buf[rows, :].astype(f32), g_ref[...], b_ref[...])
            ds = dsv_buf[rows, :].astype(f32)
            dvn = jnp.concatenate([_dot(wst_ref[0], ds[:, 0:ACH]), _dot(wst_ref[1], ds[:, ACH:half]), ds_col[0, rows, :], ds_col[1, rows, :]], axis=1)
            dg_ref[...] += jnp.sum(dvn * vhat, axis=0, keepdims=True)
            db_ref[...] += jnp.sum(dvn, axis=0, keepdims=True)
            dvh = dvn * g_ref[...]
            dva = rstd * (dvh - jnp.mean(dvh, axis=-1, keepdims=True) - vhat * jnp.mean(dvh * vhat, axis=-1, keepdims=True))
            dva_ref[rows, :] = dva.astype(bf16)
            return carry

        lax.fori_loop(0, nch, out_step, 0, unroll=8)

    vm = pl.BlockSpec(memory_space=pltpu.VMEM)
    hbm = pl.BlockSpec(memory_space=pl.ANY)
    return pl.pallas_call(
        body, name="mix_bwd",
        out_shape=(jax.ShapeDtypeStruct((l, AW), bf16), jax.ShapeDtypeStruct((4, ACH, ACH), f32), jax.ShapeDtypeStruct((4, ACH, ACH), f32),
                   jax.ShapeDtypeStruct((1, AW), f32), jax.ShapeDtypeStruct((1, AW), f32)),
        in_specs=[hbm, hbm, vm, vm, vm], out_specs=(vm, vm, vm, vm, vm),
        scratch_shapes=[pltpu.VMEM((l, AW), bf16), pltpu.VMEM((l, AW), bf16), pltpu.VMEM((2, l, ACH), f32), pltpu.VMEM((2, l, ACH), f32),
                        pltpu.SemaphoreType.DMA((2,))],
        compiler_params=_cp(),
    )(p, dsv, ln_g, ln_b, ws_t)


def _mid(x, tgt, p, o_f, o_b, sv, gate, gf, gb_norm, w_pa, w_pb, w_out, tl):
    l = x.shape[0]

    def body(x_ref, t_ref, zb_ref, ua_ref, za_ref, g1_ref, g2_ref, of_ref, ob_ref, sv_ref, gate_ref, gf_ref, gbn_ref,
             wpa_ref, wpb_ref, wout_ref,
             dx1_ref, dzbua_ref, dzag_ref, dsv_ref, do_ref, dwout_bf, dwpa_bf, dwpb_bf, dgf_ref, dgate_ref, dgbn_ref, loss_ref,
             dwout_ref, dwpa_ref, dwpb_ref):
        @pl.when(pl.program_id(0) == 0)
        def _():
            for r in (dwout_ref, dwpa_ref, dwpb_ref, dgf_ref, dgate_ref, dgbn_ref, loss_ref):
                r[...] = jnp.zeros_like(r)

        o = of_ref[...] + ob_ref[...]
        rr = jnp.concatenate(
            [jnp.broadcast_to(lax.rsqrt(jnp.mean(o[:, h * HV:(h + 1) * HV] ** 2, axis=-1, keepdims=True) + EPS), (tl, HV))
             for h in range(NH)], axis=1)
        ohat = o * rr
        on = ohat * gbn_ref[...]
        szb, dszb = _silu_and_grad(zb_ref[...].astype(f32))
        tb = on * szb
        u = ua_ref[...].astype(f32)
        svv = sv_ref[...].astype(f32)
        sza, dsza = _silu_and_grad(za_ref[...].astype(f32))
        ta = u * svv * sza
        ya = _dot(ta, wpa_ref[...])
        yb = _dot(tb, wpb_ref[...])
        g1 = _sigmoid(g1_ref[...].astype(f32))
        g2 = _sigmoid(g2_ref[...].astype(f32))
        m = g1 * ya + g2 * yb
        y2 = _dot(m, wout_ref[...])
        x1 = x_ref[...] + gate_ref[...] * y2
        r1 = lax.rsqrt(jnp.mean(x1 * x1, axis=-1, keepdims=True) + EPS)
        x1n = x1 * r1
        err = x1n * gf_ref[...] - t_ref[...]
        loss_ref[...] += jnp.sum(jnp.sum(err * err, axis=-1, keepdims=True), axis=0, keepdims=True) * (0.5 / D)
        dout = err * (1.0 / D)
        dgf_ref[...] += jnp.sum(dout * x1n, axis=0, keepdims=True)
        dx1n = dout * gf_ref[...]
        dx1 = r1 * (dx1n - x1n * jnp.mean(dx1n * x1n, axis=-1, keepdims=True))
        dx1_ref[...] = dx1
        dgate_ref[...] += jnp.sum(dx1 * y2, axis=0, keepdims=True)
        dy2 = dx1 * gate_ref[...]
        dwout_ref[...] += _tn(m, dy2)
        dm = _nt(dy2, wout_ref[...])
        dya = dm * g1
        dyb = dm * g2
        dzag_ref[:, AW:AW + D] = (dm * ya * g1 * (1.0 - g1)).astype(bf16)
        dzag_ref[:, AW + D:] = (dm * yb * g2 * (1.0 - g2)).astype(bf16)
        dwpa_ref[...] += _tn(ta, dya)
        dta = _nt(dya, wpa_ref[...])
        dzbua_ref[:, AW:] = (dta * svv * sza).astype(bf16)
        dsv_ref[...] = (dta * u * sza).astype(bf16)
        dzag_ref[:, 0:AW] = (dta * u * svv * dsza).astype(bf16)
        dwpb_ref[...] += _tn(tb, dyb)
        dtb = _nt(dyb, wpb_ref[...])
        don = dtb * szb
        dzbua_ref[:, 0:AW] = (dtb * on * dszb).astype(bf16)
        dgbn_ref[...] += jnp.sum(don * ohat, axis=0, keepdims=True)
        doh = don * gbn_ref[...]
        prod = doh * ohat
        mh = jnp.concatenate(
            [jnp.broadcast_to(jnp.mean(prod[:, h * HV:(h + 1) * HV], axis=-1, keepdims=True), (tl, HV)) for h in range(NH)], axis=1)
        do_ref[...] = (rr * (doh - ohat * mh)).astype(bf16)

        @pl.when(pl.program_id(0) == l // tl - 1)
        def _():
            for acc, out in ((dwout_ref, dwout_bf), (dwpa_ref, dwpa_bf), (dwpb_ref, dwpb_bf)):
                out[...] = acc[...].astype(bf16)

    row = lambda w, j: pl.BlockSpec((tl, w), lambda i, j=j: (i, j))
    full = lambda *s: pl.BlockSpec(s, lambda i: (0,) * len(s))
    return pl.pallas_call(
        body, name="mid", grid=(l // tl,),
        out_shape=(jax.ShapeDtypeStruct((l, D), f32), jax.ShapeDtypeStruct((l, 2 * AW), bf16), jax.ShapeDtypeStruct((l, AW + 2 * D), bf16),
                   jax.ShapeDtypeStruct((l, AW), bf16), jax.ShapeDtypeStruct((l, VW), bf16),
                   jax.ShapeDtypeStruct((D, D), bf16), jax.ShapeDtypeStruct((AW, D), bf16), jax.ShapeDtypeStruct((VW, D), bf16),
                   jax.ShapeDtypeStruct((1, D), f32), jax.ShapeDtypeStruct((1, D), f32), jax.ShapeDtypeStruct((1, VW), f32),
                   jax.ShapeDtypeStruct((1, 1), f32)),
        scratch_shapes=[pltpu.VMEM((D, D), f32), pltpu.VMEM((AW, D), f32), pltpu.VMEM((VW, D), f32)],
        in_specs=[row(D, 0), row(D, 0), row(AW, PZB // AW), row(AW, PUA // AW), row(AW, PZA // AW), row(D, PG1 // D), row(D, PG2 // D),
                  row(VW, 0), row(VW, 0), row(AW, 0), full(1, D), full(1, D), full(1, VW), full(AW, D), full(VW, D), full(D, D)],
        out_specs=(row(D, 0), row(2 * AW, 0), row(AW + 2 * D, 0), row(AW, 0), row(VW, 0),
                   full(D, D), full(AW, D), full(VW, D), full(1, D), full(1, D), full(1, VW), full(1, 1)),
        compiler_params=_cp(("arbitrary",)),
    )(x, tgt, p, p, p, p, p, o_f, o_b, sv, gate, gf, gb_norm, w_pa, w_pb, w_out)


def _in_bwd(x, dx1, dqkv, dzbua, dva, dzag, dlr, w_pad, g, scale, tl, comm):
    l = x.shape[0]
    c_in, c_out, c_sems = comm.specs()

    def body(*refs):
        cin = refs[14:14 + len(c_in)]
        outs = refs[14 + len(c_in):]
        comm.run(cin, outs[4:4 + comm.n], outs[4 + comm.n:], l // tl, lambda: compute(*refs[:14], *outs[:4]))

    def compute(x_ref, dx1_ref, a_ref, b_ref, c_ref, e_ref, lr_ref, wa_ref, wb_ref, wc_ref, we_ref, wl_ref, g_ref, sc_ref,
                gx_ref, dsh_ref, dsc_ref, dg_ref):
        @pl.when(pl.program_id(0) == 0)
        def _():
            for r in (dsh_ref, dsc_ref, dg_ref):
                r[...] = jnp.zeros_like(r)

        dh = (_nt(a_ref[...], wa_ref[...]) + _nt(b_ref[...], wb_ref[...]) + _nt(c_ref[...], wc_ref[...]) + _nt(e_ref[...], we_ref[...])
              + _nt(lr_ref[...], wl_ref[...]))
        xv = x_ref[...]
        r = lax.rsqrt(jnp.mean(xv * xv, axis=-1, keepdims=True) + EPS)
        xn = xv * r
        gxn = jnp.sum(dh * xn, axis=0, keepdims=True)
        dsh_ref[...] += jnp.sum(dh, axis=0, keepdims=True)
        dsc_ref[...] += gxn * g_ref[...]
        dg_ref[...] += gxn * (1.0 + sc_ref[...])
        dxn = dh * (g_ref[...] * (1.0 + sc_ref[...]))
        gx_ref[...] = dx1_ref[...] + r * (dxn - xn * jnp.mean(dxn * xn, axis=-1, keepdims=True))

    row = lambda w: pl.BlockSpec((tl, w), lambda i: (i, 0))
    wcol = lambda w, j: pl.BlockSpec((D, w), lambda i, j=j: (0, j))
    vec = pl.BlockSpec((1, D), lambda i: (0, 0))
    return pl.pallas_call(
        body, name="in_bwd", grid=(l // tl,),
        out_shape=(jax.ShapeDtypeStruct((l, D), f32),) + (jax.ShapeDtypeStruct((1, D), f32),) * 3 + tuple(comm.outs),
        in_specs=[row(D), row(D), row(2 * KW + VW), row(2 * AW), row(AW), row(AW + 2 * D), row(LRW),
                  wcol(2 * KW + VW, 0), wcol(2 * AW, PZB // (2 * AW)), wcol(AW, PVA // AW), wcol(AW + 2 * D, PZA // (AW + 2 * D)),
                  wcol(LRW, PLR // LRW), vec, vec] + c_in,
        out_specs=(row(D), vec, vec, vec) + tuple(c_out), scratch_shapes=c_sems,
        compiler_params=_cp(("arbitrary",)),
    )(x, dx1, dqkv, dzbua, dva, dzag, dlr, w_pad, w_pad, w_pad, w_pad, w_pad, g, scale, *comm.ins)


def _tn_matmul(a, bs, tl, name, comm=None):
    l, m = a.shape
    k = len(bs)
    comm = comm or _Comm([], [], None)
    c_in, c_out, c_sems = comm.specs()

    def body(a_ref, *refs):
        b_refs, cin = refs[:k], refs[k:k + len(c_in)]
        o_refs = refs[k + len(c_in):2 * k + len(c_in)]
        cout, sems = refs[2 * k + len(c_in):2 * k + len(c_in) + comm.n], refs[2 * k + len(c_in) + comm.n:]

        def compute():
            @pl.when(pl.program_id(0) == 0)
            def _():
                for o_ref in o_refs:
                    o_ref[...] = jnp.zeros_like(o_ref)

            av = a_ref[...]
            for b_ref, o_ref in zip(b_refs, o_refs):
                o_ref[...] += _tn(av, b_ref[...])

        comm.run(cin, cout, sems, l // tl, compute)

    return pl.pallas_call(
        body, name=name, grid=(l // tl,), out_shape=tuple(jax.ShapeDtypeStruct((m, b.shape[1]), f32) for b in bs) + tuple(comm.outs),
        in_specs=[pl.BlockSpec((tl, m), lambda i: (i, 0))] + [pl.BlockSpec((tl, b.shape[1]), lambda i: (i, 0)) for b in bs] + c_in,
        out_specs=tuple(pl.BlockSpec((m, b.shape[1]), lambda i: (0, 0)) for b in bs) + tuple(c_out),
        scratch_shapes=c_sems, compiler_params=_cp(("arbitrary",)),
    )(a, *bs, *comm.ins)


def _pad_gate(w2, gb):
    z = jnp.zeros((RANK, KW), f32)
    tail = jnp.zeros((LRW - 2 * RANK, KW), f32)
    w2f = jnp.concatenate([w2[0], z, tail], axis=0)
    w2b = jnp.concatenate([z, w2[1], tail], axis=0)
    return w2f, w2b, gb[0:1], gb[1:2]


EARLY_A_ROWS = 384


class _NoExchange:
    def __init__(self, w_pa, w_pb, w_out):
        self.weights = (w_pa, w_pb, w_out)

    def gather_proj(self):
        return _Comm([], [], None)

    def proj_weights(self, got):
        return self.weights

    def first(self, dw_out, dw_pa, dw_pb):
        return _Comm([], [], None)

    def early_a(self, small, blocks):
        return _Comm([], [], None)

    def early_b(self, blocks):
        return _Comm([], [], None)

    def late(self, blocks, dgt):
        return _Comm([], [], None)


class _Exchanges:
    def __init__(self, pa, pb, wo):
        self.shards = (pa, pb, wo)

    def gather_proj(self):
        def plan(i, o):
            srcs = [lambda j, r=r: r for r in i]
            dsts = [lambda j: o[0].at[:, _lanes(j)], lambda j: o[1].at[:, _lanes(j)], lambda j: o[2].at[j]]
            return srcs, dsts, None
        sds = jax.ShapeDtypeStruct
        return _Comm(self.shards, [sds((AW, D), bf16), sds((VW, D), bf16), sds((NDEV, 128, D), bf16)], plan)

    def proj_weights(self, got):
        return got[0], got[1], got[2].reshape(D, D)

    def first(self, dw_out, dw_pa, dw_pb):
        def plan(i, o):
            srcs = [lambda j: i[0].at[j], lambda j: i[1].at[:, _lanes(j)], lambda j: i[2].at[:, _lanes(j)]]
            dsts = [lambda j, r=r: r.at[j] for r in o]
            return srcs, dsts, None
        sds = jax.ShapeDtypeStruct
        return _Comm([dw_out.reshape(NDEV, 128, D), dw_pa, dw_pb],
                     [sds((NDEV, 128, D), bf16), sds((NDEV, AW, 128), bf16), sds((NDEV, VW, 128), bf16)], plan)

    def early_a(self, small, blocks):
        def plan(i, o):
            srcs = [lambda j: i[0], lambda j: i[1].at[j]]
            dsts = [lambda j, r=r: r.at[j] for r in o]
            return srcs, dsts, [None, lambda j: j >= LATE_DESTS - 1]
        sds = jax.ShapeDtypeStruct
        return _Comm([small, blocks], [sds((NDEV,) + small.shape, f32), sds(blocks.shape, bf16)], plan)

    def early_b(self, blocks):
        def plan(i, o):
            return [lambda j: i[0].at[j]], [lambda j: o[0].at[j]], [lambda j: j >= LATE_DESTS - 1]
        return _Comm([blocks], [jax.ShapeDtypeStruct(blocks.shape, bf16)], plan)

    def late(self, blocks, dgt):
        return _LateComm(blocks, dgt)


def _local_step(x, ctx, tgt, mod, modc, norm_g, w_pad, ln_g, ln_b, ws, bs, w2, gb, gb_norm, gf, xch):
    shift, scale, gate = mod[:, 0:D], mod[:, D:2 * D], mod[:, 2 * D:]
    shift_c, scale_c = modc[:, 0:D], modc[:, D:]
    w2f, w2b, gbf, gbb = _pad_gate(w2, gb)

    p, h, *got_proj = _in_proj(x, norm_g, scale, shift, w_pad, 512, xch.gather_proj())
    w_pa, w_pb, w_out = xch.proj_weights(got_proj)
    sc_f, sc_b = _ctx_fwd(ctx, norm_g, scale_c, shift_c, w_pad, w2f, w2b, gbf, gbb)
    o_f, st_f = _gla_fwd(p, w2f, gbf, sc_f, False, 512, "gla_fwd_f")
    o_b, st_b = _gla_fwd(p, w2b, gbb, sc_b, True, 512, "gla_fwd_b")
    sv = _mix_fwd(p, ln_g, ln_b, ws.astype(bf16), bs.T)
    (dx1, dzbua, dzag, dsv, do, dw_out, dw_pa, dw_pb, dgf, dgate, dgbn, loss) = _mid(
        x, tgt, p, o_f, o_b, sv, gate, gf, gb_norm, w_pa, w_pb, w_out, 256)
    dva, dws, dbs_acc, dln_g, dln_b = _mix_bwd(p, dsv, ln_g, ln_b, jnp.swapaxes(ws, 1, 2).astype(bf16))
    dw_zbua, dw_va, dw_zag, *got_first = _tn_matmul(h, [dzbua, dva, dzag], 1024, "dw_early", xch.first(dw_out, dw_pa, dw_pb))
    blocks_a = _pack_dw_early(dw_zbua, dw_va, dw_zag, 0, EARLY_A_ROWS, "pack_dw_early_a")
    blocks_b = _pack_dw_early(dw_zbua, dw_va, dw_zag, EARLY_A_ROWS, D - EARLY_A_ROWS, "pack_dw_early_b")
    small = _rows128(dln_g, dln_b, dws, jnp.sum(dbs_acc, axis=-1), dgbn, dgf, jnp.broadcast_to(loss, (1, 128)))

    dqkv_f, dlr_f, dw2f, dgbf, dsc_f, *got_a = _gla_bwd(p, do, st_f, w2f, gbf, None, False, 512, "gla_bwd_f",
                                                        xch.early_a(small, blocks_a))
    dqkv, dlr, dw2b, dgbb, dsc_b, *got_b = _gla_bwd(p, do, st_b, w2b, gbb, (dqkv_f, dlr_f), True, 512, "gla_bwd_b",
                                                    xch.early_b(blocks_b))
    dwk_c, dwv_c, dwl_c, dmodc, dg_c, dw2c, dgbc = _ctx_bwd(ctx, norm_g, scale_c, shift_c, w_pad, w2f, w2b, gbf, gbb, dsc_f, dsc_b)
    dw_qkv, dw_lr = _tn_matmul(h, [dqkv, dlr], 1024, "dw_qkv_lr")
    blocks_late = _pack_dw_late(dw_qkv, dw_lr, dwk_c, dwv_c, dwl_c)
    dw2 = jnp.stack([dw2f[0:RANK] + dw2c[0, 0:RANK], dw2b[RANK:2 * RANK] + dw2c[1, RANK:2 * RANK]])
    dgb = jnp.concatenate([dgbf + dgbc[0], dgbb + dgbc[1]], axis=0)
    dgt = jnp.concatenate([jnp.transpose(dw2.reshape(2, RANK, NDEV, 32), (2, 0, 1, 3)).reshape(NDEV, 2 * RANK * 32),
                           jnp.transpose(dgb.reshape(2, NDEV, 32), (1, 0, 2)).reshape(NDEV, 64),
                           jnp.zeros((NDEV, 64), f32)], axis=1).reshape(NDEV, 9, 128)
    gx, dshift, dscale, dg, *got_late = _in_bwd(x, dx1, dqkv, dzbua, dva, dzag, dlr, w_pad, norm_g, scale, 512,
                                                xch.late(blocks_late, dgt))
    return dict(loss=loss, gx=gx, dmod=jnp.concatenate([dshift, dscale, dgate], axis=1), dmodc=dmodc, dnorm_g=dg + dg_c,
                small=small, blocks_a=blocks_a, blocks_b=blocks_b, blocks_late=blocks_late, dw2=dw2, dgb=dgb,
                dw_pa=dw_pa, dw_pb=dw_pb, dw_out=dw_out, got_first=got_first, got_a=got_a, got_b=got_b, got_late=got_late)


def _rows128(*vs):
    out = []
    for t in vs:
        t = t.reshape(-1)
        pad = (-t.shape[0]) % 128
        out.append(jnp.pad(t, (0, pad)) if pad else t)
    return jnp.concatenate(out).reshape(-1, 128)


def kernel(x, c, ctx, c_ctx, w_mod, b_mod, norm_g, w_in, a_ln_g, a_ln_b, a_ws, a_bs, b_gate_w2, b_gate_b, b_norm_g, w_proj_a, w_proj_b, w_out, final_norm_g, loss_target, m_c_ctx, m_w_mod, m_b_mod, m_norm_g, m_w_in, m_a_ln_g, m_a_ln_b, m_a_ws, m_a_bs, m_b_gate_w2, m_b_gate_b, m_b_norm_g, m_w_proj_a, m_w_proj_b, m_w_out, m_final_norm_g, v_c_ctx, v_w_mod, v_b_mod, v_norm_g, v_w_in, v_a_ln_g, v_a_ln_b, v_a_ws, v_a_bs, v_b_gate_w2, v_b_gate_b, v_b_norm_g, v_w_proj_a, v_w_proj_b, v_w_out, v_final_norm_g):
    me = _me()
    ncol = w_mod.shape[2]

    gate_mine = _rows128(jnp.concatenate([b_gate_w2.reshape(-1), b_gate_b.reshape(-1)]))
    bm_mine = lax.dynamic_slice(b_mod, (0, me * ncol), (1, ncol))
    cs, mods, wg, gates = _gather_first(c, c_ctx.reshape(1, D), w_mod[0], bm_mine, w_in[0].astype(bf16), gate_mine)
    w_pad = _repack_w(wg)
    gflat = gates.reshape(NDEV, 9 * 128)
    w2 = jnp.transpose(gflat[:, 0:2 * RANK * 32].reshape(NDEV, 2, RANK, 32), (1, 2, 0, 3)).reshape(2, RANK, KW)
    gb = jnp.transpose(gflat[:, 2 * RANK * 32:2 * RANK * 32 + 64].reshape(NDEV, 2, 32), (1, 0, 2)).reshape(2, KW)

    mods = jnp.transpose(mods, (1, 0, 2)).reshape(16, 3 * D)
    mod = lax.dynamic_slice(mods, (me, 0), (1, 3 * D))
    modc = mods[8:9, 0:2 * D]

    xch = _Exchanges(w_proj_a[0].astype(bf16), w_proj_b[0].astype(bf16), w_out[0].astype(bf16))
    r = _local_step(x[0], ctx[0], loss_target[0], mod, modc, norm_g, w_pad, a_ln_g, a_ln_b, a_ws[0], a_bs[0], w2, gb,
                    b_norm_g, final_norm_g.reshape(1, D), xch)
    p_out, p_pa, p_pb = r["got_first"]
    smalls_e, p_in_a = r["got_a"]
    (p_in_b,) = r["got_b"]
    _, _, p_in_late, p_gt = r["got_late"]

    n_e = (AW + AW + 4 * ACH * ACH + AW + VW + D) // 128
    row = lambda t: t.reshape(1, D)
    rep_e = _adam_params(smalls_e, [a_ln_g, a_ln_b, a_ws, a_bs, b_norm_g, row(final_norm_g)],
                         [m_a_ln_g, m_a_ln_b, m_a_ws, m_a_bs, m_b_norm_g, row(m_final_norm_g)],
                         [v_a_ln_g, v_a_ln_b, v_a_ws, v_a_bs, v_b_norm_g, row(v_final_norm_g)], "adam_rep_early")
    rep_e = [t[0:5] + (t[5].reshape(D),) for t in rep_e]
    losses = smalls_e[:, n_e, 0]
    loss = losses[0]
    for i in range(1, NDEV):
        loss = loss + losses[i]

    dbm = r["dmod"] + jnp.concatenate([r["dmodc"], jnp.zeros((1, D), f32)], axis=1)
    smalls_l = _all_gather(_rows128(r["dnorm_g"], dbm, r["dmod"], r["dmodc"]), "gather_small")
    n_l = (D + 3 * D) // 128
    rep_l = _adam_params(smalls_l, [norm_g, b_mod], [m_norm_g, m_b_mod], [v_norm_g, v_b_mod], "adam_rep_late")
    tail = smalls_l[:, n_l:].reshape(NDEV, -1)
    dmods = tail[:, 0:3 * D]
    dmodc_all = tail[:, 3 * D:5 * D]
    dmodc_tot = dmodc_all[0:1]
    for i in range(1, NDEV):
        dmodc_tot = dmodc_tot + dmodc_all[i:i + 1]

    dm_rows = jnp.concatenate([dmods, jnp.concatenate([dmodc_tot, jnp.zeros((1, D), f32)], axis=1), jnp.zeros((7, 3 * D), f32)], axis=0)
    dm_mine = lax.dynamic_slice(dm_rows, (0, me * ncol), (16, ncol))
    *wm, gcs = _mod_bwd(cs, dm_mine, w_mod, m_w_mod, v_w_mod)
    cc = _adam(gcs.reshape(NDEV, 8, 128), c_ctx.reshape(8, 128), m_c_ctx.reshape(8, 128), v_c_ctx.reshape(8, 128), 8, "adam_cctx")

    a_in = _adam_w_in(p_in_a, p_in_b, p_in_late, w_in, m_w_in, v_w_in)
    a_pa = _adam(p_pa, w_proj_a, m_w_proj_a, v_w_proj_a, AW, "adam_w_pa")
    a_pb = _adam(p_pb, w_proj_b, m_w_proj_b, v_w_proj_b, VW, "adam_w_pb")
    a_out = _adam(p_out, w_out, m_w_out, v_w_out, 128, "adam_w_out")
    gate_m = _rows128(jnp.concatenate([m_b_gate_w2.reshape(-1), m_b_gate_b.reshape(-1)]))
    gate_v = _rows128(jnp.concatenate([v_b_gate_w2.reshape(-1), v_b_gate_b.reshape(-1)]))
    a_gt = [t.reshape(-1) for t in _adam(p_gt, gate_mine, gate_m, gate_v, 9, "adam_gate")]
    nw2 = 2 * RANK * 32
    sh = [(a_in[k], a_pa[k], a_pb[k], a_out[k], a_gt[k][0:nw2].reshape(1, 2, RANK, 32),
           a_gt[k][nw2:nw2 + 64].reshape(1, 2, 32)) for k in range(4)]

    outs = [loss, r["gx"][None]]
    for k in range(4):
        lg, lb, aws, abs_, bng, fng = rep_e[k]
        n_g, bmod = rep_l[k]
        s_in, s_pa, s_pb, s_out, s_w2, s_gb = sh[k]
        outs += [cc[k].reshape(D), wm[k], bmod, n_g, s_in, lg, lb, aws, abs_, s_w2, s_gb, bng, s_pa, s_pb, s_out, fng]
    return tuple(outs)
```

```python
import jax
import jax.numpy as jnp
from jax import lax
from jax.experimental import pallas as pl
from jax.experimental.pallas import tpu as pltpu

f32, bf16 = jnp.float32, jnp.bfloat16

D = 1024
CTX = 256
EPS = 1e-6
AW = 512
ACH = 128
GW = 64
KW = 256
VW = 512
NH = 4
HK = 64
HV = 128
RANK = 16
TAU = 16.0
CH = 64
QSCALE = HK ** -0.5
INW = 5152
NDEV = 8

PQ, PK, PV, PZB, PUA, PVA, PZA, PG1, PG2, PLR, PW = 0, 256, 512, 1024, 1536, 2048, 2560, 3072, 4096, 5120, 5248
LRW = 128

ADAM_LR, ADAM_B1, ADAM_B2, ADAM_EPS, ADAM_WD, ADAM_STEP = 0.001, 0.9, 0.999, 1e-08, 0.01, 10

VMEM_LIMIT = 56 * 1024 * 1024
MESH = pl.DeviceIdType.MESH


def _cp(sem=None):
    return pltpu.CompilerParams(dimension_semantics=sem, vmem_limit_bytes=VMEM_LIMIT)


def _dot(a, b):
    return jnp.dot(a.astype(bf16), b.astype(bf16), preferred_element_type=f32)


def _nt(a, b):
    return lax.dot_general(a.astype(bf16), b.astype(bf16), (((1,), (1,)), ((), ())), preferred_element_type=f32)


def _tn(a, b):
    return lax.dot_general(a.astype(bf16), b.astype(bf16), (((0,), (0,)), ((), ())), preferred_element_type=f32)


def _dot_hi(a, b):
    return jnp.dot(a, b, preferred_element_type=f32, precision=lax.Precision.HIGHEST)


def _sigmoid(x):
    return 1.0 / (1.0 + jnp.exp(-x))


def _log_sigmoid(x):
    return jnp.minimum(x, 0.0) - jnp.log(1.0 + jnp.exp(-jnp.abs(x)))


def _silu_and_grad(z):
    s = _sigmoid(z)
    return z * s, s * (1.0 + z * (1.0 - s))


def _me():
    return 4 * lax.axis_index("x") + 2 * lax.axis_index("y") + lax.axis_index("c")


def _peer(k):
    x, y, c = lax.axis_index("x"), lax.axis_index("y"), lax.axis_index("c")
    px = 1 - x if k & 4 else x
    py = 1 - y if k & 2 else y
    pc = 1 - c if k & 1 else c
    return (px, py, pc), 4 * px + 2 * py + pc


def _all_gather(v, name):
    r, c = v.shape

    def body(v_ref, out_ref, send_sems, recv_sems, local_sem):
        me = _me()
        mine = pltpu.make_async_copy(v_ref, out_ref.at[me], local_sem)
        mine.start()
        sends, recvs = [], []
        for k in range(1, NDEV):
            dev, idx = _peer(k)
            sends.append(pltpu.make_async_remote_copy(
                src_ref=v_ref, dst_ref=out_ref.at[me], send_sem=send_sems.at[k - 1], recv_sem=recv_sems.at[k - 1],
                device_id=dev, device_id_type=MESH))
            recvs.append(pltpu.make_async_remote_copy(
                src_ref=v_ref, dst_ref=out_ref.at[idx], send_sem=send_sems.at[k - 1], recv_sem=recv_sems.at[k - 1],
                device_id=dev, device_id_type=MESH))
        for cp in sends:
            cp.start()
        for cp in recvs:
            cp.wait_recv()
        for cp in sends:
            cp.wait_send()
        mine.wait()

    return pl.pallas_call(
        body, name=name, out_shape=jax.ShapeDtypeStruct((NDEV, r, c), v.dtype),
        in_specs=[pl.BlockSpec(memory_space=pl.ANY)], out_specs=pl.BlockSpec(memory_space=pl.ANY),
        scratch_shapes=[pltpu.SemaphoreType.DMA((NDEV - 1,)), pltpu.SemaphoreType.DMA((NDEV - 1,)), pltpu.SemaphoreType.DMA(())],
    )(v)


def _fanout(srcs, dsts, send_sems, recv_sems, local_sems, owners=None):
    me = _me()
    n = len(srcs)
    owns = lambda a, j: True if owners is None or owners[a] is None else owners[a](j)

    def guarded(cond, fn):
        if cond is True:
            fn()
        else:
            pl.when(cond)(fn)

    def copies(with_recvs):
        local = [pltpu.make_async_copy(srcs[a](me), dsts[a](me), local_sems.at[a]) for a in range(n)]
        sends, recvs = [], []
        for k in range(1, NDEV):
            dev, idx = _peer(k)
            for a in range(n):
                s = (k - 1) * n + a
                sends.append((owns(a, idx), pltpu.make_async_remote_copy(
                    src_ref=srcs[a](idx), dst_ref=dsts[a](me), send_sem=send_sems.at[s], recv_sem=recv_sems.at[s],
                    device_id=dev, device_id_type=MESH)))
                if with_recvs:
                    recvs.append((owns(a, me), pltpu.make_async_remote_copy(
                        src_ref=srcs[a](idx), dst_ref=dsts[a](idx), send_sem=send_sems.at[s], recv_sem=recv_sems.at[s],
                        device_id=dev, device_id_type=MESH)))
        return local, sends, recvs

    def start():
        local, sends, _ = copies(False)
        for a, cp in enumerate(local):
            guarded(owns(a, me), cp.start)
        for cond, cp in sends:
            guarded(cond, cp.start)

    def finish():
        local, sends, recvs = copies(True)
        for cond, cp in recvs:
            guarded(cond, cp.wait_recv)
        for cond, cp in sends:
            guarded(cond, cp.wait_send)
        for a, cp in enumerate(local):
            guarded(owns(a, me), cp.wait)

    return start, finish


class _Comm:
    def __init__(self, ins, outs, plan):
        self.ins, self.outs, self.plan = list(ins), list(outs), plan
        self.n = len(self.outs)

    def specs(self):
        hbm = pl.BlockSpec(memory_space=pl.ANY)
        return [hbm] * len(self.ins), [hbm] * self.n, _fanout_sems(self.n) if self.n else []

    def run(self, in_refs, out_refs, sems, nsteps, compute):
        if not self.n:
            compute()
            return

        def hooks():
            srcs, dsts, owners = self.plan(in_refs, out_refs)
            return _fanout(srcs, dsts, sems[0], sems[1], sems[2], owners)

        pl.when(pl.program_id(0) == 0)(lambda: hooks()[0]())
        compute()
        pl.when(pl.program_id(0) == nsteps - 1)(lambda: hooks()[1]())


LATE_MID_STEP = 1


class _LateComm:
    def __init__(self, blocks, dgt):
        sds = jax.ShapeDtypeStruct
        self.ins = [blocks, dgt]
        self.outs = [sds((D, SHARD), bf16), sds((D, SHARD), bf16), sds((4, D, SHARD), bf16), sds(dgt.shape, f32)]
        self.n = len(self.outs)

    def specs(self):
        hbm = pl.BlockSpec(memory_space=pl.ANY)
        scratch = [pltpu.VMEM((3, D, SHARD), bf16), pltpu.SemaphoreType.DMA((2,)), pltpu.SemaphoreType.DMA((4,)),
                   pltpu.SemaphoreType.DMA((3,))] + _fanout_sems(1)
        return [hbm] * 2, [hbm] * self.n, scratch

    def run(self, in_refs, out_refs, scratch, nsteps, compute):
        late_ref, dgt_ref = in_refs
        sib_ref, pair_ref, parts_ref, gt_ref = out_refs
        vbuf, send_sems, recv_sems, local_sems, g_send, g_recv, g_local = scratch
        x, y, c = lax.axis_index("x"), lax.axis_index("y"), lax.axis_index("c")
        chip = 2 * x + y
        is_owner_chip = chip == 0
        step = pl.program_id(0)

        def to_sibling():
            return pltpu.make_async_remote_copy(src_ref=late_ref.at[1 - c], dst_ref=sib_ref, send_sem=send_sems.at[0],
                                                recv_sem=recv_sems.at[0], device_id=(x, y, 1 - c), device_id_type=MESH)

        def to_owner(k):
            return pltpu.make_async_remote_copy(src_ref=pair_ref, dst_ref=parts_ref.at[k], send_sem=send_sems.at[1],
                                                recv_sem=recv_sems.at[k], device_id=(0, 0, c), device_id_type=MESH)

        def own_copy():
            return pltpu.make_async_copy(pair_ref, parts_ref.at[0], local_sems.at[2])

        def gates():
            return _fanout([lambda j: dgt_ref.at[j]], [lambda j: gt_ref.at[j]], g_send, g_recv, g_local)

        @pl.when(step == 0)
        def _():
            to_sibling().start()
            gates()[0]()

        compute()

        @pl.when(step == LATE_MID_STEP)
        def _():
            mine = pltpu.make_async_copy(late_ref.at[c], vbuf.at[0], local_sems.at[0])
            mine.start()
            to_sibling().wait_recv()
            theirs = pltpu.make_async_copy(sib_ref, vbuf.at[1], local_sems.at[1])
            theirs.start()
            mine.wait()
            theirs.wait()
            vbuf[2] = (vbuf[0].astype(f32) + vbuf[1].astype(f32)).astype(bf16)
            pltpu.sync_copy(vbuf.at[2], pair_ref)
            pl.when(is_owner_chip)(lambda: own_copy().start())
            pl.when(jnp.logical_not(is_owner_chip))(lambda: to_owner(chip).start())

        @pl.when(step == nsteps - 1)
        def _():
            @pl.when(is_owner_chip)
            def _():
                for k in range(1, 4):
                    to_owner(k).wait_recv()
                own_copy().wait()

            pl.when(jnp.logical_not(is_owner_chip))(lambda: to_owner(chip).wait_send())
            to_sibling().wait_send()
            gates()[1]()


def _fanout_sems(n):
    return [pltpu.SemaphoreType.DMA(((NDEV - 1) * n,)), pltpu.SemaphoreType.DMA(((NDEV - 1) * n,)), pltpu.SemaphoreType.DMA((n,))]


def _lanes(j):
    return pl.ds(pl.multiple_of(j * 128, 128), 128)


def _gather_first(c_row, cctx_row, wm, bm, wi, gate):
    def body(c_ref, cctx_ref, wm_ref, bm_ref, wi_ref, g_ref, cs_ref, mods_ref, owi, og, call_ref, mine_ref,
             send_sems, recv_sems, local_sems, c_send, c_recv, c_local, m_send, m_recv, m_local):
        x, y, c = lax.axis_index("x"), lax.axis_index("y"), lax.axis_index("c")
        sibling = (x, y, 1 - c)
        chips = [(1 - x, y), (x, 1 - y), (1 - x, 1 - y)]
        index = lambda px, py, pc: 4 * px + 2 * py + pc
        arrays = ((wi_ref, owi), (g_ref, og))
        n = len(arrays)

        def copy(a, k, block, to, own=False):
            src, out = arrays[a]
            return pltpu.make_async_remote_copy(
                src_ref=src if own else out.at[index(*block)], dst_ref=out.at[index(*block)],
                send_sem=send_sems.at[k * n + a], recv_sem=recv_sems.at[k * n + a], device_id=to, device_id_type=MESH)

        c_start, c_finish = _fanout([lambda j: c_ref], [lambda j: call_ref.at[j]], c_send, c_recv, c_local)
        c_start()
        mine = [pltpu.make_async_copy(src, out.at[index(x, y, c)], local_sems.at[a]) for a, (src, out) in enumerate(arrays)]
        first = [copy(a, 0, (x, y, c), sibling, own=True) for a in range(n)]
        first += [copy(a, 1 + j, (x, y, c), (*chip, c), own=True) for j, chip in enumerate(chips) for a in range(n)]
        for cp in mine + first:
            cp.start()

        c_finish()
        cs = jnp.concatenate([call_ref[j] for j in range(NDEV)] + [cctx_ref[...], jnp.zeros((16 - NDEV - 1, D), f32)], axis=0)
        cs_ref[...] = cs
        s, _ = _silu_and_grad(cs)
        mine_ref[...] = _dot_hi(s, wm_ref[...]) + bm_ref[...]
        m_start, m_finish = _fanout([lambda j: mine_ref], [lambda j: mods_ref.at[j]], m_send, m_recv, m_local)
        m_start()

        passed = []
        for j, chip in enumerate(chips):
            for a in range(n):
                copy(a, 1 + j, (*chip, c), (x, y, c)).wait_recv()
            for a in range(n):
                cp = copy(a, 4 + j, (*chip, c), sibling)
                cp.start()
                passed.append(cp)
        for a in range(n):
            copy(a, 0, sibling, (x, y, c)).wait_recv()
        for j, chip in enumerate(chips):
            for a in range(n):
                copy(a, 4 + j, (*chip, 1 - c), (x, y, c)).wait_recv()
        for cp in first + passed:
            cp.wait_send()
        for cp in mine:
            cp.wait()
        m_finish()

    hbm = pl.BlockSpec(memory_space=pl.ANY)
    vm = pl.BlockSpec(memory_space=pltpu.VMEM)
    ncol = wm.shape[1]
    return pl.pallas_call(
        body, name="gather_first",
        out_shape=(jax.ShapeDtypeStruct((16, D), f32), jax.ShapeDtypeStruct((NDEV, 16, ncol), f32),
                   jax.ShapeDtypeStruct((NDEV,) + wi.shape, bf16), jax.ShapeDtypeStruct((NDEV,) + gate.shape, f32)),
        in_specs=[vm, vm, vm, vm, hbm, hbm], out_specs=(vm, vm, hbm, hbm),
        scratch_shapes=[pltpu.VMEM((NDEV, 1, D), f32), pltpu.VMEM((16, ncol), f32)] + _fanout_sems(2) + _fanout_sems(1) + _fanout_sems(1),
        compiler_params=_cp(),
    )(c_row, cctx_row, wm, bm, wi, gate)


SHARD = INW // NDEV
ROWS_RP = 128


def _overlap(lo, hi, a, b):
    s, e = max(lo, a), min(hi, b)
    return (s, e) if s < e else None


def _repack_w(wg):
    segs = ((0, 1024, PQ), (1024, 1024 + 2 * RANK, PLR), (1024 + 2 * RANK, INW, PZB))

    def body(g_ref, o_ref):
        for j in range(NDEV):
            lo, hi = j * SHARD, (j + 1) * SHARD
            for a, b, pad0 in segs:
                ov = _overlap(lo, hi, a, b)
                if ov:
                    s, e = ov
                    o_ref[:, pad0 + s - a:pad0 + e - a] = g_ref[j, :, s - lo:e - lo]
        o_ref[:, PLR + 2 * RANK:PW] = jnp.zeros((ROWS_RP, PW - PLR - 2 * RANK), bf16)

    return pl.pallas_call(
        body, name="repack_w", grid=(D // ROWS_RP,), out_shape=jax.ShapeDtypeStruct((D, PW), bf16),
        in_specs=[pl.BlockSpec((NDEV, ROWS_RP, SHARD), lambda i: (0, i, 0))],
        out_specs=pl.BlockSpec((ROWS_RP, PW), lambda i: (i, 0)), compiler_params=_cp(("arbitrary",)),
    )(wg)


LATE_END = 1024 + 2 * RANK
LATE_DESTS = 2


def _pack_blocks(o_ref, srcs, dests, dtype):
    for n, j in enumerate(dests):
        lo, hi = j * SHARD, (j + 1) * SHARD
        done = lo
        for a, b, src in srcs:
            ov = _overlap(lo, hi, a, b)
            if ov:
                s, e = ov
                if s > done:
                    o_ref[n, :, done - lo:s - lo] = jnp.zeros((ROWS_RP, s - done), dtype)
                o_ref[n, :, s - lo:e - lo] = src[:, s - a:e - a].astype(dtype)
                done = e
        if done < hi:
            o_ref[n, :, done - lo:hi - lo] = jnp.zeros((ROWS_RP, hi - done), dtype)


def _pack_dw_early(dw_zbua, dw_va, dw_zag, row0, nrows, name):
    def body(zbua_ref, va_ref, zag_ref, o_ref):
        _pack_blocks(o_ref, ((LATE_END, 2080, zbua_ref), (2080, 2592, va_ref), (2592, INW, zag_ref)), range(NDEV), bf16)

    row = lambda w: pl.BlockSpec((ROWS_RP, w), lambda i: (i + row0 // ROWS_RP, 0))
    return pl.pallas_call(
        body, name=name, grid=(nrows // ROWS_RP,), out_shape=jax.ShapeDtypeStruct((NDEV, nrows, SHARD), bf16),
        in_specs=[row(2 * AW), row(AW), row(AW + 2 * D)],
        out_specs=pl.BlockSpec((NDEV, ROWS_RP, SHARD), lambda i: (0, i, 0)), compiler_params=_cp(("arbitrary",)),
    )(dw_zbua, dw_va, dw_zag)


def _pack_dw_late(dw_qkv, dw_lr, dwk_c, dwv_c, dwl_c):
    def body(qkv_ref, lr_ref, kc_ref, vc_ref, lc_ref, o_ref):
        qkv = qkv_ref[...] + jnp.concatenate([jnp.zeros((ROWS_RP, KW), f32), kc_ref[...], vc_ref[...]], axis=1)
        lr = lr_ref[...] + lc_ref[...]
        _pack_blocks(o_ref, ((0, 1024, qkv), (1024, LATE_END, lr)), range(LATE_DESTS), bf16)

    row = lambda w: pl.BlockSpec((ROWS_RP, w), lambda i: (i, 0))
    return pl.pallas_call(
        body, name="pack_dw_late", grid=(D // ROWS_RP,), out_shape=jax.ShapeDtypeStruct((LATE_DESTS, D, SHARD), bf16),
        in_specs=[row(2 * KW + VW), row(LRW), row(KW), row(VW), row(LRW)],
        out_specs=pl.BlockSpec((LATE_DESTS, ROWS_RP, SHARD), lambda i: (0, i, 0)), compiler_params=_cp(("arbitrary",)),
    )(dw_qkv, dw_lr, dwk_c, dwv_c, dwl_c)


def _mod_bwd(cs, dm, wm, m, v):
    def body(cs_ref, dm_ref, wm_ref, m_ref, v_ref, gw_ref, d_ref, mo_ref, vo_ref, gcs_ref, gc_ref, send_sems, recv_sems, local_sems):
        s, ds = _silu_and_grad(cs_ref[...])
        part = lax.dot_general(dm_ref[8:9, :], wm_ref[0], (((1,), (1,)), ((), ())), preferred_element_type=f32,
                               precision=lax.Precision.HIGHEST)
        gc_ref[...] = part * ds[8:9, :]
        start, finish = _fanout([lambda j: gc_ref], [lambda j: gcs_ref.at[j]], send_sems, recv_sems, local_sems)
        start()
        g = lax.dot_general(s, dm_ref[...], (((0,), (0,)), ((), ())), preferred_element_type=f32, precision=lax.Precision.HIGHEST)
        _adam_update(g[None], wm_ref, m_ref, v_ref, gw_ref, d_ref, mo_ref, vo_ref)
        finish()

    like = jax.ShapeDtypeStruct(wm.shape, f32)
    return pl.pallas_call(body, name="mod_bwd", out_shape=(like, like, like, like, jax.ShapeDtypeStruct((NDEV, 1, D), f32)),
                          scratch_shapes=[pltpu.VMEM((1, D), f32)] + _fanout_sems(1),
                          compiler_params=_cp())(cs, dm, wm, m, v)


def _adam_update(g, w_ref, m_ref, v_ref, go_ref, d_ref, mo_ref, vo_ref):
    c1 = 1.0 / (1.0 - ADAM_B1 ** ADAM_STEP)
    c2 = 1.0 / (1.0 - ADAM_B2 ** ADAM_STEP)
    mn = ADAM_B1 * m_ref[...] + (1.0 - ADAM_B1) * g
    vn = ADAM_B2 * v_ref[...] + (1.0 - ADAM_B2) * (g * g)
    go_ref[...] = g
    mo_ref[...] = mn
    vo_ref[...] = vn
    d_ref[...] = -ADAM_LR * ((mn * c1) / (jnp.sqrt(vn * c2) + ADAM_EPS) + ADAM_WD * w_ref[...])


def _adam_w_in(parts_a, parts_b, parts_late, w, m, v):
    na = EARLY_A_ROWS // ROWS_RP

    def body(a_ref, b_ref, l_ref, w_ref, m_ref, v_ref, go_ref, d_ref, mo_ref, vo_ref):
        me = _me()
        first = pl.program_id(0) < na
        early = jnp.where(first, a_ref[0], b_ref[0]).astype(f32)
        for i in range(1, NDEV):
            early = early + jnp.where(first, a_ref[i], b_ref[i]).astype(f32)
        late = l_ref[0].astype(f32)
        for i in range(1, 4):
            late = late + l_ref[i].astype(f32)
        g = jnp.where(me >= LATE_DESTS - 1, early, 0.0) + jnp.where(me < LATE_DESTS, late, 0.0)
        _adam_update(g, w_ref, m_ref, v_ref, go_ref, d_ref, mo_ref, vo_ref)

    blk = pl.BlockSpec((None, ROWS_RP, SHARD), lambda i: (0, i, 0))
    return pl.pallas_call(
        body, name="adam_w_in", grid=(D // ROWS_RP,), out_shape=tuple(jax.ShapeDtypeStruct((1, D, SHARD), f32) for _ in range(4)),
        in_specs=[pl.BlockSpec((NDEV, ROWS_RP, SHARD), lambda i: (0, jnp.minimum(i, na - 1), 0)),
                  pl.BlockSpec((NDEV, ROWS_RP, SHARD), lambda i: (0, jnp.maximum(i - na, 0), 0)),
                  pl.BlockSpec((4, ROWS_RP, SHARD), lambda i: (0, i, 0)), blk, blk, blk],
        out_specs=(blk, blk, blk, blk), compiler_params=_cp(("arbitrary",)),
    )(parts_a, parts_b, parts_late, w, m, v)


def _adam_params(parts, ws, ms, vs, name):
    p = parts.shape[0]
    k = len(ws)
    nrows = [w.size // 128 for w in ws]
    starts = [sum(nrows[:i]) for i in range(k)]

    def shaped(g, shape):
        if len(shape) == 2:
            return jnp.concatenate([g[r:r + 1] for r in range(g.shape[0])], axis=1)
        return g.reshape(shape)

    def body(*refs):
        g_ref = refs[0]
        w_refs, m_refs, v_refs = refs[1:1 + k], refs[1 + k:1 + 2 * k], refs[1 + 2 * k:1 + 3 * k]
        outs = refs[1 + 3 * k:]
        for i in range(k):
            rows = slice(starts[i], starts[i] + nrows[i])
            g = g_ref[0, rows, :]
            for j in range(1, p):
                g = g + g_ref[j, rows, :]
            _adam_update(shaped(g, ws[i].shape), w_refs[i], m_refs[i], v_refs[i], outs[i], outs[k + i], outs[2 * k + i], outs[3 * k + i])

    vm = pl.BlockSpec(memory_space=pltpu.VMEM)
    res = pl.pallas_call(
        body, name=name, out_shape=tuple(jax.ShapeDtypeStruct(w.shape, f32) for _ in range(4) for w in ws),
        in_specs=[vm] * (1 + 3 * k), out_specs=(vm,) * (4 * k), compiler_params=_cp(),
    )(parts, *ws, *ms, *vs)
    return [res[i * k:(i + 1) * k] for i in range(4)]


def _adam(parts, w, m, v, rows, name):
    p, r, c = parts.shape
    lead = w.ndim - 2

    def body(g_ref, w_ref, m_ref, v_ref, go_ref, d_ref, mo_ref, vo_ref):
        g = g_ref[0].astype(f32)
        for i in range(1, p):
            g = g + g_ref[i].astype(f32)
        _adam_update(g, w_ref, m_ref, v_ref, go_ref, d_ref, mo_ref, vo_ref)

    blk = pl.BlockSpec((None,) * lead + (rows, c), lambda i: (0,) * lead + (i, 0))
    return pl.pallas_call(
        body, name=name, grid=(r // rows,), out_shape=tuple(jax.ShapeDtypeStruct(w.shape, f32) for _ in range(4)),
        in_specs=[pl.BlockSpec((p, rows, c), lambda i: (0, i, 0)), blk, blk, blk], out_specs=(blk, blk, blk, blk),
        compiler_params=_cp(("arbitrary",)),
    )(parts, w, m, v)


def _in_proj(x, g, scale, shift, w_pad, tl, comm):
    l = x.shape[0]
    c_in, c_out, c_sems = comm.specs()

    def body(*refs):
        x_ref, g_ref, sc_ref, sh_ref, w_ref = refs[:5]
        cin = refs[5:5 + len(c_in)]
        p_ref, h_ref = refs[5 + len(c_in):7 + len(c_in)]
        cout = refs[7 + len(c_in):7 + len(c_in) + comm.n]
        sems = refs[7 + len(c_in) + comm.n:]

        def compute():
            xv = x_ref[...]
            r = lax.rsqrt(jnp.mean(xv * xv, axis=-1, keepdims=True) + EPS)
            h = (xv * r) * (g_ref[...] * (1.0 + sc_ref[...])) + sh_ref[...]
            hb = h.astype(bf16)
            h_ref[...] = hb
            p_ref[...] = jnp.dot(hb, w_ref[...], preferred_element_type=f32).astype(bf16)

        comm.run(cin, cout, sems, l // tl, compute)

    vec = pl.BlockSpec((1, D), lambda i: (0, 0))
    return pl.pallas_call(
        body, name="in_proj", grid=(l // tl,),
        out_shape=(jax.ShapeDtypeStruct((l, PW), bf16), jax.ShapeDtypeStruct((l, D), bf16)) + tuple(comm.outs),
        in_specs=[pl.BlockSpec((tl, D), lambda i: (i, 0)), vec, vec, vec, pl.BlockSpec((D, PW), lambda i: (0, 0))] + c_in,
        out_specs=(pl.BlockSpec((tl, PW), lambda i: (i, 0)), pl.BlockSpec((tl, D), lambda i: (i, 0))) + tuple(c_out),
        scratch_shapes=c_sems, compiler_params=_cp(("arbitrary",)),
    )(x, g, scale, shift, w_pad, *comm.ins)


def _tri(rev):
    i = lax.broadcasted_iota(jnp.int32, (CH, CH), 0)
    j = lax.broadcasted_iota(jnp.int32, (CH, CH), 1)
    return jnp.where((j >= i) if rev else (j <= i), 1.0, 0.0).astype(f32)


def _head_masks():
    lane = lax.broadcasted_iota(jnp.int32, (1, KW), 1) // HK
    return [jnp.where(lane == h, 1.0, 0.0).astype(f32) for h in range(NH)]


def _block_diag():
    r = lax.broadcasted_iota(jnp.int32, (VW, KW), 0) // HV
    c = lax.broadcasted_iota(jnp.int32, (VW, KW), 1) // HK
    return jnp.where(r == c, 1.0, 0.0).astype(f32)


def _decay(lr, w2, gb, tri, rev):
    logits = _dot(lr, w2) + gb
    a = _log_sigmoid(logits) * (1.0 / TAU)
    c = _dot_hi(tri, a)
    cl = c[0:1, :] if rev else c[CH - 1:CH, :]
    return logits, c, cl


def _state_fwd(k, v, c, cl, st, bd):
    return st * jnp.exp(cl) + bd * _tn(v, k * jnp.exp(cl - c))


def _state_bwd(k, v, c, cl, st0, dst, trit):
    ecl = jnp.exp(cl)
    edec = jnp.exp(cl - c)
    kdec = k * edec
    dv = _nt(kdec, dst)
    dkdec = _dot(v, dst)
    dcl = jnp.sum(dst * st0, axis=0, keepdims=True) * ecl + jnp.sum(dkdec * kdec, axis=0, keepdims=True)
    da = _dot_hi(trit, -dkdec * kdec) + dcl
    return dkdec * edec, dv, da, dst * ecl


def _bdot(a, b):
    return lax.dot_general(a.astype(bf16), b.astype(bf16), (((2,), (1,)), ((0,), (0,))), preferred_element_type=f32)


def _bnt(a, b):
    return lax.dot_general(a.astype(bf16), b.astype(bf16), (((2,), (2,)), ((0,), (0,))), preferred_element_type=f32)


def _btn(a, b):
    return lax.dot_general(a.astype(bf16), b.astype(bf16), (((1,), (1,)), ((0,), (0,))), preferred_element_type=f32)


def _scan_chunks(x, rev):
    nc = x.shape[0]
    hi = x.astype(bf16)
    r1 = x - hi.astype(f32)
    mid = r1.astype(bf16)
    lo = (r1 - mid.astype(f32)).astype(bf16)
    terms = jnp.concatenate([hi, mid, lo], axis=1)
    tri3 = jnp.broadcast_to(jnp.concatenate([_tri(rev)] * 3, axis=1).astype(bf16)[None], (nc, CH, 3 * CH))
    return lax.dot_general(tri3, terms, (((2,), (1,)), ((0,), (0,))), preferred_element_type=f32)


class _Tile:
    pass


def _tile_prep(q_ref, k_ref, lr_ref, w2_ref, gb_ref, rev, nc):
    t = _Tile()
    tg = nc * CH
    t.logits = _dot(lr_ref[...], w2_ref[...]) + gb_ref[...]
    c = _scan_chunks((_log_sigmoid(t.logits) * (1.0 / TAU)).reshape(nc, CH, KW), rev)
    cl = c[:, 0:1, :] if rev else c[:, CH - 1:CH, :]
    k = k_ref[...].astype(f32).reshape(nc, CH, KW)
    t.ec, t.enc, t.edec, t.ecl = jnp.exp(c), jnp.exp(-c), jnp.exp(cl - c), jnp.exp(cl)
    t.qd = q_ref[...].astype(f32).reshape(nc, CH, KW) * t.ec * QSCALE
    t.kd = k * t.enc
    t.kdec = k * t.edec
    hm = _head_masks()
    t.tri4 = jnp.concatenate([_tri(rev)] * NH, axis=0)[None]
    t.qs = jnp.concatenate([t.qd * hm[h] for h in range(NH)], axis=1)
    t.pst = _bnt(t.qs, t.kd) * t.tri4
    return t


def _gla_fwd(p, w2p, gb, s0, rev, tg, name):
    l = p.shape[0]
    nb, nc = l // tg, tg // CH

    def body(q_ref, k_ref, v_ref, lr_ref, w2_ref, gb_ref, s0_ref, o_ref, st_ref, st):
        @pl.when(pl.program_id(0) == 0)
        def _():
            st[...] = s0_ref[...]

        t = _tile_prep(q_ref, k_ref, lr_ref, w2_ref, gb_ref, rev, nc)
        v = v_ref[...].reshape(nc, CH, VW)
        intra = jnp.concatenate([_bdot(t.pst[:, h * CH:(h + 1) * CH], v[:, :, h * HV:(h + 1) * HV]) for h in range(NH)], axis=2)
        kv = _btn(v, t.kdec) * _block_diag()[None]
        s = st[...]
        for n in (range(nc - 1, -1, -1) if rev else range(nc)):
            st_ref[n] = s
            s = s * t.ecl[n] + kv[n]
        st[...] = s
        o_ref[...] = (_bnt(t.qd, st_ref[...]) + intra).reshape(tg, VW)

    blk = (lambda i: nb - 1 - i) if rev else (lambda i: i)
    return pl.pallas_call(
        body, name=name, grid=(nb,),
        out_shape=(jax.ShapeDtypeStruct((l, VW), f32), jax.ShapeDtypeStruct((l // CH, VW, KW), f32)),
        in_specs=[pl.BlockSpec((tg, KW), lambda i: (blk(i), PQ // KW)), pl.BlockSpec((tg, KW), lambda i: (blk(i), PK // KW)),
                  pl.BlockSpec((tg, VW), lambda i: (blk(i), PV // VW)), pl.BlockSpec((tg, LRW), lambda i: (blk(i), PLR // LRW)),
                  pl.BlockSpec((LRW, KW), lambda i: (0, 0)), pl.BlockSpec((1, KW), lambda i: (0, 0)),
                  pl.BlockSpec((VW, KW), lambda i: (0, 0))],
        out_specs=(pl.BlockSpec((tg, VW), lambda i: (blk(i), 0)), pl.BlockSpec((nc, VW, KW), lambda i: (blk(i), 0, 0))),
        scratch_shapes=[pltpu.VMEM((VW, KW), f32)],
        compiler_params=_cp(("arbitrary",)),
    )(p, p, p, p, w2p, gb, s0)


def _gla_bwd(p, do, states, w2p, gb, prev, rev, tg, name, comm):
    l = p.shape[0]
    nb, nc = l // tg, tg // CH
    out_dt = f32 if prev is None else bf16
    c_in, c_out, c_sems = comm.specs()

    def body(*refs):
        q_ref, k_ref, v_ref, lr_ref, do_ref, st_ref, w2_ref, gb_ref = refs[:8]
        refs = refs[8:]
        if prev is not None:
            pq_ref, pl_ref = refs[:2]
            refs = refs[2:]
        cin, refs = refs[:len(c_in)], refs[len(c_in):]
        dqkv_ref, dlr_ref, dw2_ref, dgb_ref, ds0_ref = refs[:5]
        cout, dst, ds_buf, sems = refs[5:5 + comm.n], refs[5 + comm.n], refs[6 + comm.n], refs[7 + comm.n:]
        comm.run(cin, cout, sems, nb, lambda: compute(q_ref, k_ref, v_ref, lr_ref, do_ref, st_ref, w2_ref, gb_ref,
                                                      pq_ref if prev is not None else None, pl_ref if prev is not None else None,
                                                      dqkv_ref, dlr_ref, dw2_ref, dgb_ref, ds0_ref, dst, ds_buf))

    def compute(q_ref, k_ref, v_ref, lr_ref, do_ref, st_ref, w2_ref, gb_ref, pq_ref, pl_ref,
                dqkv_ref, dlr_ref, dw2_ref, dgb_ref, ds0_ref, dst, ds_buf):
        @pl.when(pl.program_id(0) == 0)
        def _():
            dst[...] = jnp.zeros_like(dst)
            dw2_ref[...] = jnp.zeros_like(dw2_ref)
            dgb_ref[...] = jnp.zeros_like(dgb_ref)

        t = _tile_prep(q_ref, k_ref, lr_ref, w2_ref, gb_ref, rev, nc)
        hm = _head_masks()
        v = v_ref[...].reshape(nc, CH, VW)
        do = do_ref[...].reshape(nc, CH, VW)
        heads = lambda a, h: a[:, :, h * HV:(h + 1) * HV]
        dpst = jnp.concatenate([_bnt(heads(do, h), heads(v, h)) for h in range(NH)], axis=1) * t.tri4
        dv = jnp.concatenate([_btn(t.pst[:, h * CH:(h + 1) * CH], heads(do, h)) for h in range(NH)], axis=2)
        dqd = _bdot(do, st_ref[...])
        for h in range(NH):
            dqd = dqd + hm[h] * _bdot(dpst[:, h * CH:(h + 1) * CH], t.kd)
        dkd = _btn(dpst, t.qs)
        u = _btn(do, t.qd) * _block_diag()[None]
        d = dst[...]
        for n in (range(nc) if rev else range(nc - 1, -1, -1)):
            ds_buf[n] = d
            d = d * t.ecl[n] + u[n]
        dst[...] = d
        ds0_ref[...] = d
        ds = ds_buf[...]
        dv = dv + _bnt(t.kdec, ds)
        dkdec = _bdot(v, ds)
        dcl = jnp.sum(ds * st_ref[...], axis=1, keepdims=True) * t.ecl + jnp.sum(dkdec * t.kdec, axis=1, keepdims=True)
        dc = dqd * t.qd - dkd * t.kd - dkdec * t.kdec
        da = _scan_chunks(dc, not rev) + dcl
        dq = dqd * t.ec * QSCALE
        dk = dkd * t.enc + dkdec * t.edec
        dlog = da.reshape(tg, KW) * _sigmoid(-t.logits) * (1.0 / TAU)
        dlr = _nt(dlog, w2_ref[...])
        dw2_ref[...] += _tn(lr_ref[...], dlog)
        dgb_ref[...] += jnp.sum(dlog, axis=0, keepdims=True)
        dqkv = jnp.concatenate([dq, dk, dv], axis=2).reshape(tg, 2 * KW + VW)
        if prev is not None:
            dqkv = dqkv + pq_ref[...]
            dlr = dlr + pl_ref[...]
        dqkv_ref[...] = dqkv.astype(out_dt)
        dlr_ref[...] = dlr.astype(out_dt)

    blk = (lambda i: i) if rev else (lambda i: nb - 1 - i)
    in_specs = [pl.BlockSpec((tg, KW), lambda i: (blk(i), PQ // KW)), pl.BlockSpec((tg, KW), lambda i: (blk(i), PK // KW)),
                pl.BlockSpec((tg, VW), lambda i: (blk(i), PV // VW)), pl.BlockSpec((tg, LRW), lambda i: (blk(i), PLR // LRW)),
                pl.BlockSpec((tg, VW), lambda i: (blk(i), 0)), pl.BlockSpec((nc, VW, KW), lambda i: (blk(i), 0, 0)),
                pl.BlockSpec((LRW, KW), lambda i: (0, 0)), pl.BlockSpec((1, KW), lambda i: (0, 0))]
    args = [p, p, p, p, do, states, w2p, gb]
    if prev is not None:
        in_specs += [pl.BlockSpec((tg, 2 * KW + VW), lambda i: (blk(i), 0)), pl.BlockSpec((tg, LRW), lambda i: (blk(i), 0))]
        args += list(prev)
    return pl.pallas_call(
        body, name=name, grid=(nb,),
        out_shape=(jax.ShapeDtypeStruct((l, 2 * KW + VW), out_dt), jax.ShapeDtypeStruct((l, LRW), out_dt),
                   jax.ShapeDtypeStruct((LRW, KW), f32), jax.ShapeDtypeStruct((1, KW), f32), jax.ShapeDtypeStruct((VW, KW), f32))
        + tuple(comm.outs),
        in_specs=in_specs + c_in,
        out_specs=(pl.BlockSpec((tg, 2 * KW + VW), lambda i: (blk(i), 0)), pl.BlockSpec((tg, LRW), lambda i: (blk(i), 0)),
                   pl.BlockSpec((LRW, KW), lambda i: (0, 0)), pl.BlockSpec((1, KW), lambda i: (0, 0)),
                   pl.BlockSpec((VW, KW), lambda i: (0, 0))) + tuple(c_out),
        scratch_shapes=[pltpu.VMEM((VW, KW), f32), pltpu.VMEM((nc, VW, KW), f32)] + c_sems,
        compiler_params=_cp(("arbitrary",)),
    )(*args, *comm.ins)


def _ctx_hidden(ctx_ref, g_ref, sc_ref, sh_ref):
    xv = ctx_ref[...]
    r = lax.rsqrt(jnp.mean(xv * xv, axis=-1, keepdims=True) + EPS)
    xn = xv * r
    return xn, xn * (g_ref[...] * (1.0 + sc_ref[...])) + sh_ref[...]


_CTX_W_SPECS = [pl.BlockSpec((D, KW), lambda i: (0, PK // KW)), pl.BlockSpec((D, VW), lambda i: (0, PV // VW)),
                pl.BlockSpec((D, LRW), lambda i: (0, PLR // LRW))]


def _ctx_fwd(ctx, g, scale, shift, w_pad, w2f, w2b, gbf, gbb):
    ncc = CTX // CH

    def body(ctx_ref, g_ref, sc_ref, sh_ref, wk_ref, wv_ref, wl_ref, w2f_ref, w2b_ref, gbf_ref, gbb_ref, sf_ref, sb_ref):
        _, hc = _ctx_hidden(ctx_ref, g_ref, sc_ref, sh_ref)
        k, v, lr = _dot(hc, wk_ref[...]), _dot(hc, wv_ref[...]), _dot(hc, wl_ref[...])
        bd = _block_diag()
        for rev, w2_ref, gb_ref, out in ((False, w2f_ref, gbf_ref, sf_ref), (True, w2b_ref, gbb_ref, sb_ref)):
            tri = _tri(rev)
            st = jnp.zeros((VW, KW), f32)
            for j in (range(ncc - 1, -1, -1) if rev else range(ncc)):
                rows = slice(j * CH, (j + 1) * CH)
                _, c, cl = _decay(lr[rows], w2_ref[...], gb_ref[...], tri, rev)
                st = _state_fwd(k[rows], v[rows], c, cl, st, bd)
            out[...] = st

    vec = pl.BlockSpec((1, D), lambda i: (0, 0))
    w2s = pl.BlockSpec((LRW, KW), lambda i: (0, 0))
    gbs = pl.BlockSpec((1, KW), lambda i: (0, 0))
    sts = pl.BlockSpec((VW, KW), lambda i: (0, 0))
    return pl.pallas_call(
        body, name="ctx_fwd", grid=(1,), out_shape=(jax.ShapeDtypeStruct((VW, KW), f32),) * 2,
        in_specs=[pl.BlockSpec((CTX, D), lambda i: (0, 0)), vec, vec, vec] + _CTX_W_SPECS + [w2s, w2s, gbs, gbs],
        out_specs=(sts, sts), compiler_params=_cp(("arbitrary",)),
    )(ctx, g, scale, shift, w_pad, w_pad, w_pad, w2f, w2b, gbf, gbb)


def _ctx_bwd(ctx, g, scale, shift, w_pad, w2f, w2b, gbf, gbb, dsf, dsb):
    ncc = CTX // CH

    def body(ctx_ref, g_ref, sc_ref, sh_ref, wk_ref, wv_ref, wl_ref, w2f_ref, w2b_ref, gbf_ref, gbb_ref, dsf_ref, dsb_ref,
             dwk_ref, dwv_ref, dwl_ref, dmod_ref, dg_ref, dw2_ref, dgb_ref):
        xn, hc = _ctx_hidden(ctx_ref, g_ref, sc_ref, sh_ref)
        k, v, lr = _dot(hc, wk_ref[...]), _dot(hc, wv_ref[...]), _dot(hc, wl_ref[...])
        bd = _block_diag()
        dk_rows, dv_rows, dl_rows = [None] * ncc, [None] * ncc, [None] * ncc
        for d, (rev, w2_ref, gb_ref, ds_ref) in enumerate(((False, w2f_ref, gbf_ref, dsf_ref), (True, w2b_ref, gbb_ref, dsb_ref))):
            tri = _tri(rev)
            order = list(range(ncc - 1, -1, -1) if rev else range(ncc))
            st, saved = jnp.zeros((VW, KW), f32), {}
            for j in order:
                rows = slice(j * CH, (j + 1) * CH)
                logits, c, cl = _decay(lr[rows], w2_ref[...], gb_ref[...], tri, rev)
                saved[j] = (logits, c, cl, st)
                st = _state_fwd(k[rows], v[rows], c, cl, st, bd)
            dst = ds_ref[...]
            dw2 = jnp.zeros((LRW, KW), f32)
            dgb = jnp.zeros((1, KW), f32)
            for j in reversed(order):
                rows = slice(j * CH, (j + 1) * CH)
                logits, c, cl, st0 = saved[j]
                dk, dv, da, dst = _state_bwd(k[rows], v[rows], c, cl, st0, dst, _tri(not rev))
                dlog = da * _sigmoid(-logits) * (1.0 / TAU)
                dl = _nt(dlog, w2_ref[...])
                dw2 = dw2 + _tn(lr[rows], dlog)
                dgb = dgb + jnp.sum(dlog, axis=0, keepdims=True)
                dk_rows[j] = dk if dk_rows[j] is None else dk_rows[j] + dk
                dv_rows[j] = dv if dv_rows[j] is None else dv_rows[j] + dv
                dl_rows[j] = dl if dl_rows[j] is None else dl_rows[j] + dl
            dw2_ref[d] = dw2
            dgb_ref[d] = dgb
        dk, dv, dl = (jnp.concatenate(t, axis=0) for t in (dk_rows, dv_rows, dl_rows))
        dwk_ref[...] = _tn(hc, dk)
        dwv_ref[...] = _tn(hc, dv)
        dwl_ref[...] = _tn(hc, dl)
        dh = _nt(dk, wk_ref[...]) + _nt(dv, wv_ref[...]) + _nt(dl, wl_ref[...])
        gx = dh * xn
        dmod_ref[:, 0:D] = jnp.sum(dh, axis=0, keepdims=True)
        dmod_ref[:, D:2 * D] = jnp.sum(gx, axis=0, keepdims=True) * g_ref[...]
        dg_ref[...] = jnp.sum(gx, axis=0, keepdims=True) * (1.0 + sc_ref[...])

    vec = pl.BlockSpec((1, D), lambda i: (0, 0))
    w2s = pl.BlockSpec((LRW, KW), lambda i: (0, 0))
    gbs = pl.BlockSpec((1, KW), lambda i: (0, 0))
    sts = pl.BlockSpec((VW, KW), lambda i: (0, 0))
    full = lambda *s: pl.BlockSpec(s, lambda i: (0,) * len(s))
    return pl.pallas_call(
        body, name="ctx_bwd", grid=(1,),
        out_shape=(jax.ShapeDtypeStruct((D, KW), f32), jax.ShapeDtypeStruct((D, VW), f32), jax.ShapeDtypeStruct((D, LRW), f32),
                   jax.ShapeDtypeStruct((1, 2 * D), f32), jax.ShapeDtypeStruct((1, D), f32),
                   jax.ShapeDtypeStruct((2, LRW, KW), f32), jax.ShapeDtypeStruct((2, 1, KW), f32)),
        in_specs=[pl.BlockSpec((CTX, D), lambda i: (0, 0)), vec, vec, vec] + _CTX_W_SPECS + [w2s, w2s, gbs, gbs, sts, sts],
        out_specs=(full(D, KW), full(D, VW), full(D, LRW), full(1, 2 * D), full(1, D), full(2, LRW, KW), full(2, 1, KW)),
        compiler_params=_cp(("arbitrary",)),
    )(ctx, g, scale, shift, w_pad, w_pad, w_pad, w2f, w2b, gbf, gbb, dsf, dsb)


def _layernorm(va, g, b):
    mu = jnp.mean(va, axis=-1, keepdims=True)
    xc = va - mu
    rstd = lax.rsqrt(jnp.mean(xc * xc, axis=-1, keepdims=True) + EPS)
    vhat = xc * rstd
    return vhat, rstd, vhat * g + b


def _mix_fwd(p, ln_g, ln_b, ws, bs_t):
    l = p.shape[0]
    nch = l // ACH
    half = AW // 2

    def body(p_ref, g_ref, b_ref, ws_ref, bs_ref, sv_ref, va_buf, col_buf, sem):
        cp = pltpu.make_async_copy(p_ref.at[:, pl.ds(PVA, AW)], va_buf, sem)
        cp.start()
        cp.wait()

        def rows_step(n, carry):
            rows = pl.ds(pl.multiple_of(n * ACH, ACH), ACH)
            _, _, vn = _layernorm(va_buf[rows, :].astype(f32), g_ref[...], b_ref[...])
            for gi in range(2):
                sl = slice(gi * ACH, (gi + 1) * ACH)
                sv_ref[rows, sl] = (_dot(ws_ref[gi], vn[:, sl]) + bs_ref[:, gi:gi + 1]).astype(bf16)
            col_buf[0, rows, :] = vn[:, half:half + ACH]
            col_buf[1, rows, :] = vn[:, half + ACH:]
            return carry

        lax.fori_loop(0, nch, rows_step, 0, unroll=8)

        def cols_step(cidx, carry):
            rows = pl.ds(cidx, ACH, stride=GW)
            for gi in range(2, 4):
                col_buf[gi - 2, rows, :] = _dot(ws_ref[gi], col_buf[gi - 2, rows, :]) + bs_ref[:, gi:gi + 1]
            return carry

        lax.fori_loop(0, GW, cols_step, 0, unroll=8)

        def out_step(n, carry):
            rows = pl.ds(pl.multiple_of(n * ACH, ACH), ACH)
            sv_ref[rows, half:half + ACH] = col_buf[0, rows, :].astype(bf16)
            sv_ref[rows, half + ACH:] = col_buf[1, rows, :].astype(bf16)
            return carry

        lax.fori_loop(0, nch, out_step, 0, unroll=8)

    vm = pl.BlockSpec(memory_space=pltpu.VMEM)
    return pl.pallas_call(
        body, name="mix_fwd", out_shape=jax.ShapeDtypeStruct((l, AW), bf16),
        in_specs=[pl.BlockSpec(memory_space=pl.ANY), vm, vm, vm, vm], out_specs=vm,
        scratch_shapes=[pltpu.VMEM((l, AW), bf16), pltpu.VMEM((2, l, ACH), f32), pltpu.SemaphoreType.DMA(())],
        compiler_params=_cp(),
    )(p, ln_g, ln_b, ws, bs_t)


def _mix_bwd(p, dsv, ln_g, ln_b, ws_t):
    l = p.shape[0]
    nch = l // ACH
    half = AW // 2

    def body(p_ref, dsv_hbm, g_ref, b_ref, wst_ref, dva_ref, dws_ref, dbs_ref, dg_ref, db_ref, va_buf, dsv_buf, vn_col, ds_col, sems):
        cp1 = pltpu.make_async_copy(p_ref.at[:, pl.ds(PVA, AW)], va_buf, sems.at[0])
        cp2 = pltpu.make_async_copy(dsv_hbm, dsv_buf, sems.at[1])
        cp1.start()
        cp2.start()
        dws_ref[...] = jnp.zeros_like(dws_ref)
        dbs_ref[...] = jnp.zeros_like(dbs_ref)
        dg_ref[...] = jnp.zeros_like(dg_ref)
        db_ref[...] = jnp.zeros_like(db_ref)
        cp1.wait()
        cp2.wait()

        def rows_step(n, carry):
            rows = pl.ds(pl.multiple_of(n * ACH, ACH), ACH)
            _, _, vn = _layernorm(va_buf[rows, :].astype(f32), g_ref[...], b_ref[...])
            ds = dsv_buf[rows, :].astype(f32)
            for gi in range(2):
                sl = slice(gi * ACH, (gi + 1) * ACH)
                dws_ref[gi] += _nt(ds[:, sl], vn[:, sl])
                dbs_ref[gi] += ds[:, sl]
            for gi in range(2):
                sl = slice(half + gi * ACH, half + (gi + 1) * ACH)
                vn_col[gi, rows, :] = vn[:, sl]
                ds_col[gi, rows, :] = ds[:, sl]
            return carry

        lax.fori_loop(0, nch, rows_step, 0, unroll=8)

        def cols_step(cidx, carry):
            rows = pl.ds(cidx, ACH, stride=GW)
            for gi in range(2, 4):
                ds = ds_col[gi - 2, rows, :]
                dws_ref[gi] += _nt(ds, vn_col[gi - 2, rows, :])
                dbs_ref[gi] += ds
                ds_col[gi - 2, rows, :] = _dot(wst_ref[gi], ds)
            return carry

        lax.fori_loop(0, GW, cols_step, 0, unroll=8)

        def out_step(n, carry):
            rows = pl.ds(pl.multiple_of(n * ACH, ACH), ACH)
            vhat, rstd, _ = _layernorm(va_buf[rows, :].astype(f32), g_ref[...], b_ref[...])
            ds = dsv_buf[rows, :].astype(f32)
            dvn = jnp.concatenate([_dot(wst_ref[0], ds[:, 0:ACH]), _dot(wst_ref[1], ds[:, ACH:half]), ds_col[0, rows, :], ds_col[1, rows, :]], axis=1)
            dg_ref[...] += jnp.sum(dvn * vhat, axis=0, keepdims=True)
            db_ref[...] += jnp.sum(dvn, axis=0, keepdims=True)
            dvh = dvn * g_ref[...]
            dva = rstd * (dvh - jnp.mean(dvh, axis=-1, keepdims=True) - vhat * jnp.mean(dvh * vhat, axis=-1, keepdims=True))
            dva_ref[rows, :] = dva.astype(bf16)
            return carry

        lax.fori_loop(0, nch, out_step, 0, unroll=8)

    vm = pl.BlockSpec(memory_space=pltpu.VMEM)
    hbm = pl.BlockSpec(memory_space=pl.ANY)
    return pl.pallas_call(
        body, name="mix_bwd",
        out_shape=(jax.ShapeDtypeStruct((l, AW), bf16), jax.ShapeDtypeStruct((4, ACH, ACH), f32), jax.ShapeDtypeStruct((4, ACH, ACH), f32),
                   jax.ShapeDtypeStruct((1, AW), f32), jax.ShapeDtypeStruct((1, AW), f32)),
        in_specs=[hbm, hbm, vm, vm, vm], out_specs=(vm, vm, vm, vm, vm),
        scratch_shapes=[pltpu.VMEM((l, AW), bf16), pltpu.VMEM((l, AW), bf16), pltpu.VMEM((2, l, ACH), f32), pltpu.VMEM((2, l, ACH), f32),
                        pltpu.SemaphoreType.DMA((2,))],
        compiler_params=_cp(),
    )(p, dsv, ln_g, ln_b, ws_t)


def _mid(x, tgt, p, o_f, o_b, sv, gate, gf, gb_norm, w_pa, w_pb, w_out, tl):
    l = x.shape[0]

    def body(x_ref, t_ref, zb_ref, ua_ref, za_ref, g1_ref, g2_ref, of_ref, ob_ref, sv_ref, gate_ref, gf_ref, gbn_ref,
             wpa_ref, wpb_ref, wout_ref,
             dx1_ref, dzbua_ref, dzag_ref, dsv_ref, do_ref, dwout_bf, dwpa_bf, dwpb_bf, dgf_ref, dgate_ref, dgbn_ref, loss_ref,
             dwout_ref, dwpa_ref, dwpb_ref):
        @pl.when(pl.program_id(0) == 0)
        def _():
            for r in (dwout_ref, dwpa_ref, dwpb_ref, dgf_ref, dgate_ref, dgbn_ref, loss_ref):
                r[...] = jnp.zeros_like(r)

        o = of_ref[...] + ob_ref[...]
        rr = jnp.concatenate(
            [jnp.broadcast_to(lax.rsqrt(jnp.mean(o[:, h * HV:(h + 1) * HV] ** 2, axis=-1, keepdims=True) + EPS), (tl, HV))
             for h in range(NH)], axis=1)
        ohat = o * rr
        on = ohat * gbn_ref[...]
        szb, dszb = _silu_and_grad(zb_ref[...].astype(f32))
        tb = on * szb
        u = ua_ref[...].astype(f32)
        svv = sv_ref[...].astype(f32)
        sza, dsza = _silu_and_grad(za_ref[...].astype(f32))
        ta = u * svv * sza
        ya = _dot(ta, wpa_ref[...])
        yb = _dot(tb, wpb_ref[...])
        g1 = _sigmoid(g1_ref[...].astype(f32))
        g2 = _sigmoid(g2_ref[...].astype(f32))
        m = g1 * ya + g2 * yb
        y2 = _dot(m, wout_ref[...])
        x1 = x_ref[...] + gate_ref[...] * y2
        r1 = lax.rsqrt(jnp.mean(x1 * x1, axis=-1, keepdims=True) + EPS)
        x1n = x1 * r1
        err = x1n * gf_ref[...] - t_ref[...]
        loss_ref[...] += jnp.sum(jnp.sum(err * err, axis=-1, keepdims=True), axis=0, keepdims=True) * (0.5 / D)
        dout = err * (1.0 / D)
        dgf_ref[...] += jnp.sum(dout * x1n, axis=0, keepdims=True)
        dx1n = dout * gf_ref[...]
        dx1 = r1 * (dx1n - x1n * jnp.mean(dx1n * x1n, axis=-1, keepdims=True))
        dx1_ref[...] = dx1
        dgate_ref[...] += jnp.sum(dx1 * y2, axis=0, keepdims=True)
        dy2 = dx1 * gate_ref[...]
        dwout_ref[...] += _tn(m, dy2)
        dm = _nt(dy2, wout_ref[...])
        dya = dm * g1
        dyb = dm * g2
        dzag_ref[:, AW:AW + D] = (dm * ya * g1 * (1.0 - g1)).astype(bf16)
        dzag_ref[:, AW + D:] = (dm * yb * g2 * (1.0 - g2)).astype(bf16)
        dwpa_ref[...] += _tn(ta, dya)
        dta = _nt(dya, wpa_ref[...])
        dzbua_ref[:, AW:] = (dta * svv * sza).astype(bf16)
        dsv_ref[...] = (dta * u * sza).astype(bf16)
        dzag_ref[:, 0:AW] = (dta * u * svv * dsza).astype(bf16)
        dwpb_ref[...] += _tn(tb, dyb)
        dtb = _nt(dyb, wpb_ref[...])
        don = dtb * szb
        dzbua_ref[:, 0:AW] = (dtb * on * dszb).astype(bf16)
        dgbn_ref[...] += jnp.sum(don * ohat, axis=0, keepdims=True)
        doh = don * gbn_ref[...]
        prod = doh * ohat
        mh = jnp.concatenate(
            [jnp.broadcast_to(jnp.mean(prod[:, h * HV:(h + 1) * HV], axis=-1, keepdims=True), (tl, HV)) for h in range(NH)], axis=1)
        do_ref[...] = (rr * (doh - ohat * mh)).astype(bf16)

        @pl.when(pl.program_id(0) == l // tl - 1)
        def _():
            for acc, out in ((dwout_ref, dwout_bf), (dwpa_ref, dwpa_bf), (dwpb_ref, dwpb_bf)):
                out[...] = acc[...].astype(bf16)

    row = lambda w, j: pl.BlockSpec((tl, w), lambda i, j=j: (i, j))
    full = lambda *s: pl.BlockSpec(s, lambda i: (0,) * len(s))
    return pl.pallas_call(
        body, name="mid", grid=(l // tl,),
        out_shape=(jax.ShapeDtypeStruct((l, D), f32), jax.ShapeDtypeStruct((l, 2 * AW), bf16), jax.ShapeDtypeStruct((l, AW + 2 * D), bf16),
                   jax.ShapeDtypeStruct((l, AW), bf16), jax.ShapeDtypeStruct((l, VW), bf16),
                   jax.ShapeDtypeStruct((D, D), bf16), jax.ShapeDtypeStruct((AW, D), bf16), jax.ShapeDtypeStruct((VW, D), bf16),
                   jax.ShapeDtypeStruct((1, D), f32), jax.ShapeDtypeStruct((1, D), f32), jax.ShapeDtypeStruct((1, VW), f32),
                   jax.ShapeDtypeStruct((1, 1), f32)),
        scratch_shapes=[pltpu.VMEM((D, D), f32), pltpu.VMEM((AW, D), f32), pltpu.VMEM((VW, D), f32)],
        in_specs=[row(D, 0), row(D, 0), row(AW, PZB // AW), row(AW, PUA // AW), row(AW, PZA // AW), row(D, PG1 // D), row(D, PG2 // D),
                  row(VW, 0), row(VW, 0), row(AW, 0), full(1, D), full(1, D), full(1, VW), full(AW, D), full(VW, D), full(D, D)],
        out_specs=(row(D, 0), row(2 * AW, 0), row(AW + 2 * D, 0), row(AW, 0), row(VW, 0),
                   full(D, D), full(AW, D), full(VW, D), full(1, D), full(1, D), full(1, VW), full(1, 1)),
        compiler_params=_cp(("arbitrary",)),
    )(x, tgt, p, p, p, p, p, o_f, o_b, sv, gate, gf, gb_norm, w_pa, w_pb, w_out)


def _in_bwd(x, dx1, dqkv, dzbua, dva, dzag, dlr, w_pad, g, scale, tl, comm):
    l = x.shape[0]
    c_in, c_out, c_sems = comm.specs()

    def body(*refs):
        cin = refs[14:14 + len(c_in)]
        outs = refs[14 + len(c_in):]
        comm.run(cin, outs[4:4 + comm.n], outs[4 + comm.n:], l // tl, lambda: compute(*refs[:14], *outs[:4]))

    def compute(x_ref, dx1_ref, a_ref, b_ref, c_ref, e_ref, lr_ref, wa_ref, wb_ref, wc_ref, we_ref, wl_ref, g_ref, sc_ref,
                gx_ref, dsh_ref, dsc_ref, dg_ref):
        @pl.when(pl.program_id(0) == 0)
        def _():
            for r in (dsh_ref, dsc_ref, dg_ref):
                r[...] = jnp.zeros_like(r)

        dh = (_nt(a_ref[...], wa_ref[...]) + _nt(b_ref[...], wb_ref[...]) + _nt(c_ref[...], wc_ref[...]) + _nt(e_ref[...], we_ref[...])
              + _nt(lr_ref[...], wl_ref[...]))
        xv = x_ref[...]
        r = lax.rsqrt(jnp.mean(xv * xv, axis=-1, keepdims=True) + EPS)
        xn = xv * r
        gxn = jnp.sum(dh * xn, axis=0, keepdims=True)
        dsh_ref[...] += jnp.sum(dh, axis=0, keepdims=True)
        dsc_ref[...] += gxn * g_ref[...]
        dg_ref[...] += gxn * (1.0 + sc_ref[...])
        dxn = dh * (g_ref[...] * (1.0 + sc_ref[...]))
        gx_ref[...] = dx1_ref[...] + r * (dxn - xn * jnp.mean(dxn * xn, axis=-1, keepdims=True))

    row = lambda w: pl.BlockSpec((tl, w), lambda i: (i, 0))
    wcol = lambda w, j: pl.BlockSpec((D, w), lambda i, j=j: (0, j))
    vec = pl.BlockSpec((1, D), lambda i: (0, 0))
    return pl.pallas_call(
        body, name="in_bwd", grid=(l // tl,),
        out_shape=(jax.ShapeDtypeStruct((l, D), f32),) + (jax.ShapeDtypeStruct((1, D), f32),) * 3 + tuple(comm.outs),
        in_specs=[row(D), row(D), row(2 * KW + VW), row(2 * AW), row(AW), row(AW + 2 * D), row(LRW),
                  wcol(2 * KW + VW, 0), wcol(2 * AW, PZB // (2 * AW)), wcol(AW, PVA // AW), wcol(AW + 2 * D, PZA // (AW + 2 * D)),
                  wcol(LRW, PLR // LRW), vec, vec] + c_in,
        out_specs=(row(D), vec, vec, vec) + tuple(c_out), scratch_shapes=c_sems,
        compiler_params=_cp(("arbitrary",)),
    )(x, dx1, dqkv, dzbua, dva, dzag, dlr, w_pad, w_pad, w_pad, w_pad, w_pad, g, scale, *comm.ins)


def _tn_matmul(a, bs, tl, name, comm=None):
    l, m = a.shape
    k = len(bs)
    comm = comm or _Comm([], [], None)
    c_in, c_out, c_sems = comm.specs()

    def body(a_ref, *refs):
        b_refs, cin = refs[:k], refs[k:k + len(c_in)]
        o_refs = refs[k + len(c_in):2 * k + len(c_in)]
        cout, sems = refs[2 * k + len(c_in):2 * k + len(c_in) + comm.n], refs[2 * k + len(c_in) + comm.n:]

        def compute():
            @pl.when(pl.program_id(0) == 0)
            def _():
                for o_ref in o_refs:
                    o_ref[...] = jnp.zeros_like(o_ref)

            av = a_ref[...]
            for b_ref, o_ref in zip(b_refs, o_refs):
                o_ref[...] += _tn(av, b_ref[...])

        comm.run(cin, cout, sems, l // tl, compute)

    return pl.pallas_call(
        body, name=name, grid=(l // tl,), out_shape=tuple(jax.ShapeDtypeStruct((m, b.shape[1]), f32) for b in bs) + tuple(comm.outs),
        in_specs=[pl.BlockSpec((tl, m), lambda i: (i, 0))] + [pl.BlockSpec((tl, b.shape[1]), lambda i: (i, 0)) for b in bs] + c_in,
        out_specs=tuple(pl.BlockSpec((m, b.shape[1]), lambda i: (0, 0)) for b in bs) + tuple(c_out),
        scratch_shapes=c_sems, compiler_params=_cp(("arbitrary",)),
    )(a, *bs, *comm.ins)


def _pad_gate(w2, gb):
    z = jnp.zeros((RANK, KW), f32)
    tail = jnp.zeros((LRW - 2 * RANK, KW), f32)
    w2f = jnp.concatenate([w2[0], z, tail], axis=0)
    w2b = jnp.concatenate([z, w2[1], tail], axis=0)
    return w2f, w2b, gb[0:1], gb[1:2]


EARLY_A_ROWS = 384


class _NoExchange:
    def __init__(self, w_pa, w_pb, w_out):
        self.weights = (w_pa, w_pb, w_out)

    def gather_proj(self):
        return _Comm([], [], None)

    def proj_weights(self, got):
        return self.weights

    def first(self, dw_out, dw_pa, dw_pb):
        return _Comm([], [], None)

    def early_a(self, small, blocks):
        return _Comm([], [], None)

    def early_b(self, blocks):
        return _Comm([], [], None)

    def late(self, blocks, dgt):
        return _Comm([], [], None)


class _Exchanges:
    def __init__(self, pa, pb, wo):
        self.shards = (pa, pb, wo)

    def gather_proj(self):
        def plan(i, o):
            srcs = [lambda j, r=r: r for r in i]
            dsts = [lambda j: o[0].at[:, _lanes(j)], lambda j: o[1].at[:, _lanes(j)], lambda j: o[2].at[j]]
            return srcs, dsts, None
        sds = jax.ShapeDtypeStruct
        return _Comm(self.shards, [sds((AW, D), bf16), sds((VW, D), bf16), sds((NDEV, 128, D), bf16)], plan)

    def proj_weights(self, got):
        return got[0], got[1], got[2].reshape(D, D)

    def first(self, dw_out, dw_pa, dw_pb):
        def plan(i, o):
            srcs = [lambda j: i[0].at[j], lambda j: i[1].at[:, _lanes(j)], lambda j: i[2].at[:, _lanes(j)]]
            dsts = [lambda j, r=r: r.at[j] for r in o]
            return srcs, dsts, None
        sds = jax.ShapeDtypeStruct
        return _Comm([dw_out.reshape(NDEV, 128, D), dw_pa, dw_pb],
                     [sds((NDEV, 128, D), bf16), sds((NDEV, AW, 128), bf16), sds((NDEV, VW, 128), bf16)], plan)

    def early_a(self, small, blocks):
        def plan(i, o):
            srcs = [lambda j: i[0], lambda j: i[1].at[j]]
            dsts = [lambda j, r=r: r.at[j] for r in o]
            return srcs, dsts, [None, lambda j: j >= LATE_DESTS - 1]
        sds = jax.ShapeDtypeStruct
        return _Comm([small, blocks], [sds((NDEV,) + small.shape, f32), sds(blocks.shape, bf16)], plan)

    def early_b(self, blocks):
        def plan(i, o):
            return [lambda j: i[0].at[j]], [lambda j: o[0].at[j]], [lambda j: j >= LATE_DESTS - 1]
        return _Comm([blocks], [jax.ShapeDtypeStruct(blocks.shape, bf16)], plan)

    def late(self, blocks, dgt):
        return _LateComm(blocks, dgt)


def _local_step(x, ctx, tgt, mod, modc, norm_g, w_pad, ln_g, ln_b, ws, bs, w2, gb, gb_norm, gf, xch):
    shift, scale, gate = mod[:, 0:D], mod[:, D:2 * D], mod[:, 2 * D:]
    shift_c, scale_c = modc[:, 0:D], modc[:, D:]
    w2f, w2b, gbf, gbb = _pad_gate(w2, gb)

    p, h, *got_proj = _in_proj(x, norm_g, scale, shift, w_pad, 1024, xch.gather_proj())
    w_pa, w_pb, w_out = xch.proj_weights(got_proj)
    sc_f, sc_b = _ctx_fwd(ctx, norm_g, scale_c, shift_c, w_pad, w2f, w2b, gbf, gbb)
    o_f, st_f = _gla_fwd(p, w2f, gbf, sc_f, False, 512, "gla_fwd_f")
    o_b, st_b = _gla_fwd(p, w2b, gbb, sc_b, True, 512, "gla_fwd_b")
    sv = _mix_fwd(p, ln_g, ln_b, ws.astype(bf16), bs.T)
    (dx1, dzbua, dzag, dsv, do, dw_out, dw_pa, dw_pb, dgf, dgate, dgbn, loss) = _mid(
        x, tgt, p, o_f, o_b, sv, gate, gf, gb_norm, w_pa, w_pb, w_out, 256)
    dva, dws, dbs_acc, dln_g, dln_b = _mix_bwd(p, dsv, ln_g, ln_b, jnp.swapaxes(ws, 1, 2).astype(bf16))
    dw_zbua, dw_va, dw_zag, *got_first = _tn_matmul(h, [dzbua, dva, dzag], 1024, "dw_early", xch.first(dw_out, dw_pa, dw_pb))
    blocks_a = _pack_dw_early(dw_zbua, dw_va, dw_zag, 0, EARLY_A_ROWS, "pack_dw_early_a")
    blocks_b = _pack_dw_early(dw_zbua, dw_va, dw_zag, EARLY_A_ROWS, D - EARLY_A_ROWS, "pack_dw_early_b")
    small = _rows128(dln_g, dln_b, dws, jnp.sum(dbs_acc, axis=-1), dgbn, dgf, jnp.broadcast_to(loss, (1, 128)))

    dqkv_f, dlr_f, dw2f, dgbf, dsc_f, *got_a = _gla_bwd(p, do, st_f, w2f, gbf, None, False, 512, "gla_bwd_f",
                                                        xch.early_a(small, blocks_a))
    dqkv, dlr, dw2b, dgbb, dsc_b, *got_b = _gla_bwd(p, do, st_b, w2b, gbb, (dqkv_f, dlr_f), True, 512, "gla_bwd_b",
                                                    xch.early_b(blocks_b))
    dwk_c, dwv_c, dwl_c, dmodc, dg_c, dw2c, dgbc = _ctx_bwd(ctx, norm_g, scale_c, shift_c, w_pad, w2f, w2b, gbf, gbb, dsc_f, dsc_b)
    dw_qkv, dw_lr = _tn_matmul(h, [dqkv, dlr], 1024, "dw_qkv_lr")
    blocks_late = _pack_dw_late(dw_qkv, dw_lr, dwk_c, dwv_c, dwl_c)
    dw2 = jnp.stack([dw2f[0:RANK] + dw2c[0, 0:RANK], dw2b[RANK:2 * RANK] + dw2c[1, RANK:2 * RANK]])
    dgb = jnp.concatenate([dgbf + dgbc[0], dgbb + dgbc[1]], axis=0)
    dgt = jnp.concatenate([jnp.transpose(dw2.reshape(2, RANK, NDEV, 32), (2, 0, 1, 3)).reshape(NDEV, 2 * RANK * 32),
                           jnp.transpose(dgb.reshape(2, NDEV, 32), (1, 0, 2)).reshape(NDEV, 64),
                           jnp.zeros((NDEV, 64), f32)], axis=1).reshape(NDEV, 9, 128)
    gx, dshift, dscale, dg, *got_late = _in_bwd(x, dx1, dqkv, dzbua, dva, dzag, dlr, w_pad, norm_g, scale, 512,
                                                xch.late(blocks_late, dgt))
    return dict(loss=loss, gx=gx, dmod=jnp.concatenate([dshift, dscale, dgate], axis=1), dmodc=dmodc, dnorm_g=dg + dg_c,
                small=small, blocks_a=blocks_a, blocks_b=blocks_b, blocks_late=blocks_late, dw2=dw2, dgb=dgb,
                dw_pa=dw_pa, dw_pb=dw_pb, dw_out=dw_out, got_first=got_first, got_a=got_a, got_b=got_b, got_late=got_late)


def _rows128(*vs):
    out = []
    for t in vs:
        t = t.reshape(-1)
        pad = (-t.shape[0]) % 128
        out.append(jnp.pad(t, (0, pad)) if pad else t)
    return jnp.concatenate(out).reshape(-1, 128)


def kernel(x, c, ctx, c_ctx, w_mod, b_mod, norm_g, w_in, a_ln_g, a_ln_b, a_ws, a_bs, b_gate_w2, b_gate_b, b_norm_g, w_proj_a, w_proj_b, w_out, final_norm_g, loss_target, m_c_ctx, m_w_mod, m_b_mod, m_norm_g, m_w_in, m_a_ln_g, m_a_ln_b, m_a_ws, m_a_bs, m_b_gate_w2, m_b_gate_b, m_b_norm_g, m_w_proj_a, m_w_proj_b, m_w_out, m_final_norm_g, v_c_ctx, v_w_mod, v_b_mod, v_norm_g, v_w_in, v_a_ln_g, v_a_ln_b, v_a_ws, v_a_bs, v_b_gate_w2, v_b_gate_b, v_b_norm_g, v_w_proj_a, v_w_proj_b, v_w_out, v_final_norm_g):
    me = _me()
    ncol = w_mod.shape[2]

    gate_mine = _rows128(jnp.concatenate([b_gate_w2.reshape(-1), b_gate_b.reshape(-1)]))
    bm_mine = lax.dynamic_slice(b_mod, (0, me * ncol), (1, ncol))
    cs, mods, wg, gates = _gather_first(c, c_ctx.reshape(1, D), w_mod[0], bm_mine, w_in[0].astype(bf16), gate_mine)
    w_pad = _repack_w(wg)
    gflat = gates.reshape(NDEV, 9 * 128)
    w2 = jnp.transpose(gflat[:, 0:2 * RANK * 32].reshape(NDEV, 2, RANK, 32), (1, 2, 0, 3)).reshape(2, RANK, KW)
    gb = jnp.transpose(gflat[:, 2 * RANK * 32:2 * RANK * 32 + 64].reshape(NDEV, 2, 32), (1, 0, 2)).reshape(2, KW)

    mods = jnp.transpose(mods, (1, 0, 2)).reshape(16, 3 * D)
    mod = lax.dynamic_slice(mods, (me, 0), (1, 3 * D))
    modc = mods[8:9, 0:2 * D]

    xch = _Exchanges(w_proj_a[0].astype(bf16), w_proj_b[0].astype(bf16), w_out[0].astype(bf16))
    r = _local_step(x[0], ctx[0], loss_target[0], mod, modc, norm_g, w_pad, a_ln_g, a_ln_b, a_ws[0], a_bs[0], w2, gb,
                    b_norm_g, final_norm_g.reshape(1, D), xch)
    p_out, p_pa, p_pb = r["got_first"]
    smalls_e, p_in_a = r["got_a"]
    (p_in_b,) = r["got_b"]
    _, _, p_in_late, p_gt = r["got_late"]

    n_e = (AW + AW + 4 * ACH * ACH + AW + VW + D) // 128
    row = lambda t: t.reshape(1, D)
    rep_e = _adam_params(smalls_e, [a_ln_g, a_ln_b, a_ws, a_bs, b_norm_g, row(final_norm_g)],
                         [m_a_ln_g, m_a_ln_b, m_a_ws, m_a_bs, m_b_norm_g, row(m_final_norm_g)],
                         [v_a_ln_g, v_a_ln_b, v_a_ws, v_a_bs, v_b_norm_g, row(v_final_norm_g)], "adam_rep_early")
    rep_e = [t[0:5] + (t[5].reshape(D),) for t in rep_e]
    losses = smalls_e[:, n_e, 0]
    loss = losses[0]
    for i in range(1, NDEV):
        loss = loss + losses[i]

    dbm = r["dmod"] + jnp.concatenate([r["dmodc"], jnp.zeros((1, D), f32)], axis=1)
    smalls_l = _all_gather(_rows128(r["dnorm_g"], dbm, r["dmod"], r["dmodc"]), "gather_small")
    n_l = (D + 3 * D) // 128
    rep_l = _adam_params(smalls_l, [norm_g, b_mod], [m_norm_g, m_b_mod], [v_norm_g, v_b_mod], "adam_rep_late")
    tail = smalls_l[:, n_l:].reshape(NDEV, -1)
    dmods = tail[:, 0:3 * D]
    dmodc_all = tail[:, 3 * D:5 * D]
    dmodc_tot = dmodc_all[0:1]
    for i in range(1, NDEV):
        dmodc_tot = dmodc_tot + dmodc_all[i:i + 1]

    dm_rows = jnp.concatenate([dmods, jnp.concatenate([dmodc_tot, jnp.zeros((1, D), f32)], axis=1), jnp.zeros((7, 3 * D), f32)], axis=0)
    dm_mine = lax.dynamic_slice(dm_rows, (0, me * ncol), (16, ncol))
    *wm, gcs = _mod_bwd(cs, dm_mine, w_mod, m_w_mod, v_w_mod)
    cc = _adam(gcs.reshape(NDEV, 8, 128), c_ctx.reshape(8, 128), m_c_ctx.reshape(8, 128), v_c_ctx.reshape(8, 128), 8, "adam_cctx")

    a_in = _adam_w_in(p_in_a, p_in_b, p_in_late, w_in, m_w_in, v_w_in)
    a_pa = _adam(p_pa, w_proj_a, m_w_proj_a, v_w_proj_a, AW, "adam_w_pa")
    a_pb = _adam(p_pb, w_proj_b, m_w_proj_b, v_w_proj_b, VW, "adam_w_pb")
    a_out = _adam(p_out, w_out, m_w_out, v_w_out, 128, "adam_w_out")
    gate_m = _rows128(jnp.concatenate([m_b_gate_w2.reshape(-1), m_b_gate_b.reshape(-1)]))
    gate_v = _rows128(jnp.concatenate([v_b_gate_w2.reshape(-1), v_b_gate_b.reshape(-1)]))
    a_gt = [t.reshape(-1) for t in _adam(p_gt, gate_mine, gate_m, gate_v, 9, "adam_gate")]
    nw2 = 2 * RANK * 32
    sh = [(a_in[k], a_pa[k], a_pb[k], a_out[k], a_gt[k][0:nw2].reshape(1, 2, RANK, 32),
           a_gt[k][nw2:nw2 + 64].reshape(1, 2, 32)) for k in range(4)]

    outs = [loss, r["gx"][None]]
    for k in range(4):
        lg, lb, aws, abs_, bng, fng = rep_e[k]
        n_g, bmod = rep_l[k]
        s_in, s_pa, s_pb, s_out, s_w2, s_gb = sh[k]
        outs += [cc[k].reshape(D), wm[k], bmod, n_g, s_in, lg, lb, aws, abs_, s_w2, s_gb, bng, s_pa, s_pb, s_out, fng]
    return tuple(outs)
```

```python
import jax
import jax.numpy as jnp
from jax import lax
from jax.experimental import pallas as pl
from jax.experimental.pallas import tpu as pltpu

f32, bf16 = jnp.float32, jnp.bfloat16

D = 1024
CTX = 256
EPS = 1e-6
AW = 512
ACH = 128
GW = 64
KW = 256
VW = 512
NH = 4
HK = 64
HV = 128
RANK = 16
TAU = 16.0
CH = 64
QSCALE = HK ** -0.5
INW = 5152
NDEV = 8

PQ, PK, PV, PZB, PUA, PVA, PZA, PG1, PG2, PLR, PW = 0, 256, 512, 1024, 1536, 2048, 2560, 3072, 4096, 5120, 5248
LRW = 128

ADAM_LR, ADAM_B1, ADAM_B2, ADAM_EPS, ADAM_WD, ADAM_STEP = 0.001, 0.9, 0.999, 1e-08, 0.01, 10

VMEM_LIMIT = 56 * 1024 * 1024
MESH = pl.DeviceIdType.MESH


def _cp(sem=None):
    return pltpu.CompilerParams(dimension_semantics=sem, vmem_limit_bytes=VMEM_LIMIT)


def _dot(a, b):
    return jnp.dot(a.astype(bf16), b.astype(bf16), preferred_element_type=f32)


def _nt(a, b):
    return lax.dot_general(a.astype(bf16), b.astype(bf16), (((1,), (1,)), ((), ())), preferred_element_type=f32)


def _tn(a, b):
    return lax.dot_general(a.astype(bf16), b.astype(bf16), (((0,), (0,)), ((), ())), preferred_element_type=f32)


def _dot_hi(a, b):
    return jnp.dot(a, b, preferred_element_type=f32, precision=lax.Precision.HIGHEST)


def _sigmoid(x):
    return 1.0 / (1.0 + jnp.exp(-x))


def _log_sigmoid(x):
    return jnp.minimum(x, 0.0) - jnp.log(1.0 + jnp.exp(-jnp.abs(x)))


def _silu_and_grad(z):
    s = _sigmoid(z)
    zs = z * s
    return zs, s + zs * (1.0 - s)


def _me():
    return 4 * lax.axis_index("x") + 2 * lax.axis_index("y") + lax.axis_index("c")


def _peer(k):
    x, y, c = lax.axis_index("x"), lax.axis_index("y"), lax.axis_index("c")
    px = 1 - x if k & 4 else x
    py = 1 - y if k & 2 else y
    pc = 1 - c if k & 1 else c
    return (px, py, pc), 4 * px + 2 * py + pc


def _all_gather(v, name):
    r, c = v.shape

    def body(v_ref, out_ref, send_sems, recv_sems, local_sem):
        me = _me()
        mine = pltpu.make_async_copy(v_ref, out_ref.at[me], local_sem)
        mine.start()
        sends, recvs = [], []
        for k in range(1, NDEV):
            dev, idx = _peer(k)
            sends.append(pltpu.make_async_remote_copy(
                src_ref=v_ref, dst_ref=out_ref.at[me], send_sem=send_sems.at[k - 1], recv_sem=recv_sems.at[k - 1],
                device_id=dev, device_id_type=MESH))
            recvs.append(pltpu.make_async_remote_copy(
                src_ref=v_ref, dst_ref=out_ref.at[idx], send_sem=send_sems.at[k - 1], recv_sem=recv_sems.at[k - 1],
                device_id=dev, device_id_type=MESH))
        for cp in sends:
            cp.start()
        for cp in recvs:
            cp.wait_recv()
        for cp in sends:
            cp.wait_send()
        mine.wait()

    return pl.pallas_call(
        body, name=name, out_shape=jax.ShapeDtypeStruct((NDEV, r, c), v.dtype),
        in_specs=[pl.BlockSpec(memory_space=pl.ANY)], out_specs=pl.BlockSpec(memory_space=pl.ANY),
        scratch_shapes=[pltpu.SemaphoreType.DMA((NDEV - 1,)), pltpu.SemaphoreType.DMA((NDEV - 1,)), pltpu.SemaphoreType.DMA(())],
    )(v)


def _fanout(srcs, dsts, send_sems, recv_sems, local_sems, owners=None):
    me = _me()
    n = len(srcs)
    owns = lambda a, j: True if owners is None or owners[a] is None else owners[a](j)

    def guarded(cond, fn):
        if cond is True:
            fn()
        else:
            pl.when(cond)(fn)

    def copies(with_recvs):
        local = [pltpu.make_async_copy(srcs[a](me), dsts[a](me), local_sems.at[a]) for a in range(n)]
        sends, recvs = [], []
        for k in range(1, NDEV):
            dev, idx = _peer(k)
            for a in range(n):
                s = (k - 1) * n + a
                sends.append((owns(a, idx), pltpu.make_async_remote_copy(
                    src_ref=srcs[a](idx), dst_ref=dsts[a](me), send_sem=send_sems.at[s], recv_sem=recv_sems.at[s],
                    device_id=dev, device_id_type=MESH)))
                if with_recvs:
                    recvs.append((owns(a, me), pltpu.make_async_remote_copy(
                        src_ref=srcs[a](idx), dst_ref=dsts[a](idx), send_sem=send_sems.at[s], recv_sem=recv_sems.at[s],
                        device_id=dev, device_id_type=MESH)))
        return local, sends, recvs

    def start():
        local, sends, _ = copies(False)
        for a, cp in enumerate(local):
            guarded(owns(a, me), cp.start)
        for cond, cp in sends:
            guarded(cond, cp.start)

    def finish():
        local, sends, recvs = copies(True)
        for cond, cp in recvs:
            guarded(cond, cp.wait_recv)
        for cond, cp in sends:
            guarded(cond, cp.wait_send)
        for a, cp in enumerate(local):
            guarded(owns(a, me), cp.wait)

    return start, finish


class _Comm:
    def __init__(self, ins, outs, plan):
        self.ins, self.outs, self.plan = list(ins), list(outs), plan
        self.n = len(self.outs)

    def specs(self):
        hbm = pl.BlockSpec(memory_space=pl.ANY)
        return [hbm] * len(self.ins), [hbm] * self.n, _fanout_sems(self.n) if self.n else []

    def run(self, in_refs, out_refs, sems, nsteps, compute):
        if not self.n:
            compute()
            return

        def hooks():
            srcs, dsts, owners = self.plan(in_refs, out_refs)
            return _fanout(srcs, dsts, sems[0], sems[1], sems[2], owners)

        pl.when(pl.program_id(0) == 0)(lambda: hooks()[0]())
        compute()
        pl.when(pl.program_id(0) == nsteps - 1)(lambda: hooks()[1]())


LATE_MID_STEP = 1


class _LateComm:
    def __init__(self, blocks, dgt):
        sds = jax.ShapeDtypeStruct
        self.ins = [blocks, dgt]
        self.outs = [sds((D, SHARD), bf16), sds((D, SHARD), bf16), sds((4, D, SHARD), bf16), sds(dgt.shape, f32)]
        self.n = len(self.outs)

    def specs(self):
        hbm = pl.BlockSpec(memory_space=pl.ANY)
        scratch = [pltpu.VMEM((3, D, SHARD), bf16), pltpu.SemaphoreType.DMA((2,)), pltpu.SemaphoreType.DMA((4,)),
                   pltpu.SemaphoreType.DMA((3,))] + _fanout_sems(1)
        return [hbm] * 2, [hbm] * self.n, scratch

    def run(self, in_refs, out_refs, scratch, nsteps, compute):
        late_ref, dgt_ref = in_refs
        sib_ref, pair_ref, parts_ref, gt_ref = out_refs
        vbuf, send_sems, recv_sems, local_sems, g_send, g_recv, g_local = scratch
        x, y, c = lax.axis_index("x"), lax.axis_index("y"), lax.axis_index("c")
        chip = 2 * x + y
        is_owner_chip = chip == 0
        step = pl.program_id(0)

        def to_sibling():
            return pltpu.make_async_remote_copy(src_ref=late_ref.at[1 - c], dst_ref=sib_ref, send_sem=send_sems.at[0],
                                                recv_sem=recv_sems.at[0], device_id=(x, y, 1 - c), device_id_type=MESH)

        def to_owner(k):
            return pltpu.make_async_remote_copy(src_ref=pair_ref, dst_ref=parts_ref.at[k], send_sem=send_sems.at[1],
                                                recv_sem=recv_sems.at[k], device_id=(0, 0, c), device_id_type=MESH)

        def own_copy():
            return pltpu.make_async_copy(pair_ref, parts_ref.at[0], local_sems.at[2])

        def gates():
            return _fanout([lambda j: dgt_ref.at[j]], [lambda j: gt_ref.at[j]], g_send, g_recv, g_local)

        @pl.when(step == 0)
        def _():
            to_sibling().start()
            gates()[0]()

        compute()

        @pl.when(step == LATE_MID_STEP)
        def _():
            mine = pltpu.make_async_copy(late_ref.at[c], vbuf.at[0], local_sems.at[0])
            mine.start()
            to_sibling().wait_recv()
            theirs = pltpu.make_async_copy(sib_ref, vbuf.at[1], local_sems.at[1])
            theirs.start()
            mine.wait()
            theirs.wait()
            vbuf[2] = (vbuf[0].astype(f32) + vbuf[1].astype(f32)).astype(bf16)
            pltpu.sync_copy(vbuf.at[2], pair_ref)
            pl.when(is_owner_chip)(lambda: own_copy().start())
            pl.when(jnp.logical_not(is_owner_chip))(lambda: to_owner(chip).start())

        @pl.when(step == nsteps - 1)
        def _():
            @pl.when(is_owner_chip)
            def _():
                for k in range(1, 4):
                    to_owner(k).wait_recv()
                own_copy().wait()

            pl.when(jnp.logical_not(is_owner_chip))(lambda: to_owner(chip).wait_send())
            to_sibling().wait_send()
            gates()[1]()


def _fanout_sems(n):
    return [pltpu.SemaphoreType.DMA(((NDEV - 1) * n,)), pltpu.SemaphoreType.DMA(((NDEV - 1) * n,)), pltpu.SemaphoreType.DMA((n,))]


def _lanes(j):
    return pl.ds(pl.multiple_of(j * 128, 128), 128)


def _gather_first(c_row, cctx_row, wm, bm, wi, gate):
    def body(c_ref, cctx_ref, wm_ref, bm_ref, wi_ref, g_ref, cs_ref, mods_ref, owi, og, call_ref, mine_ref,
             send_sems, recv_sems, local_sems, c_send, c_recv, c_local, m_send, m_recv, m_local):
        x, y, c = lax.axis_index("x"), lax.axis_index("y"), lax.axis_index("c")
        sibling = (x, y, 1 - c)
        chips = [(1 - x, y), (x, 1 - y), (1 - x, 1 - y)]
        index = lambda px, py, pc: 4 * px + 2 * py + pc
        arrays = ((wi_ref, owi), (g_ref, og))
        n = len(arrays)

        def copy(a, k, block, to, own=False):
            src, out = arrays[a]
            return pltpu.make_async_remote_copy(
                src_ref=src if own else out.at[index(*block)], dst_ref=out.at[index(*block)],
                send_sem=send_sems.at[k * n + a], recv_sem=recv_sems.at[k * n + a], device_id=to, device_id_type=MESH)

        c_start, c_finish = _fanout([lambda j: c_ref], [lambda j: call_ref.at[j]], c_send, c_recv, c_local)
        c_start()
        mine = [pltpu.make_async_copy(src, out.at[index(x, y, c)], local_sems.at[a]) for a, (src, out) in enumerate(arrays)]
        first = [copy(a, 0, (x, y, c), sibling, own=True) for a in range(n)]
        first += [copy(a, 1 + j, (x, y, c), (*chip, c), own=True) for j, chip in enumerate(chips) for a in range(n)]
        for cp in mine + first:
            cp.start()

        c_finish()
        cs = jnp.concatenate([call_ref[j] for j in range(NDEV)] + [cctx_ref[...], jnp.zeros((16 - NDEV - 1, D), f32)], axis=0)
        cs_ref[...] = cs
        s, _ = _silu_and_grad(cs)
        mine_ref[...] = _dot_hi(s, wm_ref[...]) + bm_ref[...]
        m_start, m_finish = _fanout([lambda j: mine_ref], [lambda j: mods_ref.at[j]], m_send, m_recv, m_local)
        m_start()

        passed = []
        for j, chip in enumerate(chips):
            for a in range(n):
                copy(a, 1 + j, (*chip, c), (x, y, c)).wait_recv()
            for a in range(n):
                cp = copy(a, 4 + j, (*chip, c), sibling)
                cp.start()
                passed.append(cp)
        for a in range(n):
            copy(a, 0, sibling, (x, y, c)).wait_recv()
        for j, chip in enumerate(chips):
            for a in range(n):
                copy(a, 4 + j, (*chip, 1 - c), (x, y, c)).wait_recv()
        for cp in first + passed:
            cp.wait_send()
        for cp in mine:
            cp.wait()
        m_finish()

    hbm = pl.BlockSpec(memory_space=pl.ANY)
    vm = pl.BlockSpec(memory_space=pltpu.VMEM)
    ncol = wm.shape[1]
    return pl.pallas_call(
        body, name="gather_first",
        out_shape=(jax.ShapeDtypeStruct((16, D), f32), jax.ShapeDtypeStruct((NDEV, 16, ncol), f32),
                   jax.ShapeDtypeStruct((NDEV,) + wi.shape, bf16), jax.ShapeDtypeStruct((NDEV,) + gate.shape, f32)),
        in_specs=[vm, vm, vm, vm, hbm, hbm], out_specs=(vm, vm, hbm, hbm),
        scratch_shapes=[pltpu.VMEM((NDEV, 1, D), f32), pltpu.VMEM((16, ncol), f32)] + _fanout_sems(2) + _fanout_sems(1) + _fanout_sems(1),
        compiler_params=_cp(),
    )(c_row, cctx_row, wm, bm, wi, gate)


SHARD = INW // NDEV
ROWS_RP = 128


def _overlap(lo, hi, a, b):
    s, e = max(lo, a), min(hi, b)
    return (s, e) if s < e else None


def _repack_w(wg):
    segs = ((0, 1024, PQ), (1024, 1024 + 2 * RANK, PLR), (1024 + 2 * RANK, INW, PZB))

    def body(g_ref, o_ref):
        for j in range(NDEV):
            lo, hi = j * SHARD, (j + 1) * SHARD
            for a, b, pad0 in segs:
                ov = _overlap(lo, hi, a, b)
                if ov:
                    s, e = ov
                    o_ref[:, pad0 + s - a:pad0 + e - a] = g_ref[j, :, s - lo:e - lo]
        o_ref[:, PLR + 2 * RANK:PW] = jnp.zeros((ROWS_RP, PW - PLR - 2 * RANK), bf16)

    return pl.pallas_call(
        body, name="repack_w", grid=(D // ROWS_RP,), out_shape=jax.ShapeDtypeStruct((D, PW), bf16),
        in_specs=[pl.BlockSpec((NDEV, ROWS_RP, SHARD), lambda i: (0, i, 0))],
        out_specs=pl.BlockSpec((ROWS_RP, PW), lambda i: (i, 0)), compiler_params=_cp(("arbitrary",)),
    )(wg)


LATE_END = 1024 + 2 * RANK
LATE_DESTS = 2


def _pack_blocks(o_ref, srcs, dests, dtype):
    for n, j in enumerate(dests):
        lo, hi = j * SHARD, (j + 1) * SHARD
        done = lo
        for a, b, src in srcs:
            ov = _overlap(lo, hi, a, b)
            if ov:
                s, e = ov
                if s > done:
                    o_ref[n, :, done - lo:s - lo] = jnp.zeros((ROWS_RP, s - done), dtype)
                o_ref[n, :, s - lo:e - lo] = src[:, s - a:e - a].astype(dtype)
                done = e
        if done < hi:
            o_ref[n, :, done - lo:hi - lo] = jnp.zeros((ROWS_RP, hi - done), dtype)


def _pack_dw_early(dw_zbua, dw_va, dw_zag, row0, nrows, name):
    def body(zbua_ref, va_ref, zag_ref, o_ref):
        _pack_blocks(o_ref, ((LATE_END, 2080, zbua_ref), (2080, 2592, va_ref), (2592, INW, zag_ref)), range(NDEV), bf16)

    row = lambda w: pl.BlockSpec((ROWS_RP, w), lambda i: (i + row0 // ROWS_RP, 0))
    return pl.pallas_call(
        body, name=name, grid=(nrows // ROWS_RP,), out_shape=jax.ShapeDtypeStruct((NDEV, nrows, SHARD), bf16),
        in_specs=[row(2 * AW), row(AW), row(AW + 2 * D)],
        out_specs=pl.BlockSpec((NDEV, ROWS_RP, SHARD), lambda i: (0, i, 0)), compiler_params=_cp(("arbitrary",)),
    )(dw_zbua, dw_va, dw_zag)


def _pack_dw_late(dw_qkv, dw_lr, dwk_c, dwv_c, dwl_c):
    def body(qkv_ref, lr_ref, kc_ref, vc_ref, lc_ref, o_ref):
        qkv = qkv_ref[...] + jnp.concatenate([jnp.zeros((ROWS_RP, KW), f32), kc_ref[...], vc_ref[...]], axis=1)
        lr = lr_ref[...] + lc_ref[...]
        _pack_blocks(o_ref, ((0, 1024, qkv), (1024, LATE_END, lr)), range(LATE_DESTS), bf16)

    row = lambda w: pl.BlockSpec((ROWS_RP, w), lambda i: (i, 0))
    return pl.pallas_call(
        body, name="pack_dw_late", grid=(D // ROWS_RP,), out_shape=jax.ShapeDtypeStruct((LATE_DESTS, D, SHARD), bf16),
        in_specs=[row(2 * KW + VW), row(LRW), row(KW), row(VW), row(LRW)],
        out_specs=pl.BlockSpec((LATE_DESTS, ROWS_RP, SHARD), lambda i: (0, i, 0)), compiler_params=_cp(("arbitrary",)),
    )(dw_qkv, dw_lr, dwk_c, dwv_c, dwl_c)


def _mod_bwd(cs, dm, wm, m, v):
    def body(cs_ref, dm_ref, wm_ref, m_ref, v_ref, gw_ref, d_ref, mo_ref, vo_ref, gcs_ref, gc_ref, send_sems, recv_sems, local_sems):
        s, ds = _silu_and_grad(cs_ref[...])
        part = lax.dot_general(dm_ref[8:9, :], wm_ref[0], (((1,), (1,)), ((), ())), preferred_element_type=f32,
                               precision=lax.Precision.HIGHEST)
        gc_ref[...] = part * ds[8:9, :]
        start, finish = _fanout([lambda j: gc_ref], [lambda j: gcs_ref.at[j]], send_sems, recv_sems, local_sems)
        start()
        g = lax.dot_general(s, dm_ref[...], (((0,), (0,)), ((), ())), preferred_element_type=f32, precision=lax.Precision.HIGHEST)
        _adam_update(g[None], wm_ref, m_ref, v_ref, gw_ref, d_ref, mo_ref, vo_ref)
        finish()

    like = jax.ShapeDtypeStruct(wm.shape, f32)
    return pl.pallas_call(body, name="mod_bwd", out_shape=(like, like, like, like, jax.ShapeDtypeStruct((NDEV, 1, D), f32)),
                          scratch_shapes=[pltpu.VMEM((1, D), f32)] + _fanout_sems(1),
                          compiler_params=_cp())(cs, dm, wm, m, v)


def _adam_update(g, w_ref, m_ref, v_ref, go_ref, d_ref, mo_ref, vo_ref):
    c1 = 1.0 / (1.0 - ADAM_B1 ** ADAM_STEP)
    c2 = 1.0 / (1.0 - ADAM_B2 ** ADAM_STEP)
    mn = ADAM_B1 * m_ref[...] + (1.0 - ADAM_B1) * g
    vn = ADAM_B2 * v_ref[...] + (1.0 - ADAM_B2) * (g * g)
    go_ref[...] = g
    mo_ref[...] = mn
    vo_ref[...] = vn
    d_ref[...] = -ADAM_LR * ((mn * c1) / (jnp.sqrt(vn * c2) + ADAM_EPS) + ADAM_WD * w_ref[...])


def _adam_w_in(parts_a, parts_b, parts_late, w, m, v):
    na = EARLY_A_ROWS // ROWS_RP

    def body(a_ref, b_ref, l_ref, w_ref, m_ref, v_ref, go_ref, d_ref, mo_ref, vo_ref):
        me = _me()
        first = pl.program_id(0) < na
        early = jnp.where(first, a_ref[0], b_ref[0]).astype(f32)
        for i in range(1, NDEV):
            early = early + jnp.where(first, a_ref[i], b_ref[i]).astype(f32)
        late = l_ref[0].astype(f32)
        for i in range(1, 4):
            late = late + l_ref[i].astype(f32)
        g = jnp.where(me >= LATE_DESTS - 1, early, 0.0) + jnp.where(me < LATE_DESTS, late, 0.0)
        _adam_update(g, w_ref, m_ref, v_ref, go_ref, d_ref, mo_ref, vo_ref)

    blk = pl.BlockSpec((None, ROWS_RP, SHARD), lambda i: (0, i, 0))
    return pl.pallas_call(
        body, name="adam_w_in", grid=(D // ROWS_RP,), out_shape=tuple(jax.ShapeDtypeStruct((1, D, SHARD), f32) for _ in range(4)),
        in_specs=[pl.BlockSpec((NDEV, ROWS_RP, SHARD), lambda i: (0, jnp.minimum(i, na - 1), 0)),
                  pl.BlockSpec((NDEV, ROWS_RP, SHARD), lambda i: (0, jnp.maximum(i - na, 0), 0)),
                  pl.BlockSpec((4, ROWS_RP, SHARD), lambda i: (0, i, 0)), blk, blk, blk],
        out_specs=(blk, blk, blk, blk), compiler_params=_cp(("arbitrary",)),
    )(parts_a, parts_b, parts_late, w, m, v)


def _adam_params(parts, ws, ms, vs, name):
    p = parts.shape[0]
    k = len(ws)
    nrows = [w.size // 128 for w in ws]
    starts = [sum(nrows[:i]) for i in range(k)]

    def shaped(g, shape):
        if len(shape) == 2:
            return jnp.concatenate([g[r:r + 1] for r in range(g.shape[0])], axis=1)
        return g.reshape(shape)

    def body(*refs):
        g_ref = refs[0]
        w_refs, m_refs, v_refs = refs[1:1 + k], refs[1 + k:1 + 2 * k], refs[1 + 2 * k:1 + 3 * k]
        outs = refs[1 + 3 * k:]
        for i in range(k):
            rows = slice(starts[i], starts[i] + nrows[i])
            g = g_ref[0, rows, :]
            for j in range(1, p):
                g = g + g_ref[j, rows, :]
            _adam_update(shaped(g, ws[i].shape), w_refs[i], m_refs[i], v_refs[i], outs[i], outs[k + i], outs[2 * k + i], outs[3 * k + i])

    vm = pl.BlockSpec(memory_space=pltpu.VMEM)
    res = pl.pallas_call(
        body, name=name, out_shape=tuple(jax.ShapeDtypeStruct(w.shape, f32) for _ in range(4) for w in ws),
        in_specs=[vm] * (1 + 3 * k), out_specs=(vm,) * (4 * k), compiler_params=_cp(),
    )(parts, *ws, *ms, *vs)
    return [res[i * k:(i + 1) * k] for i in range(4)]


def _adam(parts, w, m, v, rows, name):
    p, r, c = parts.shape
    lead = w.ndim - 2

    def body(g_ref, w_ref, m_ref, v_ref, go_ref, d_ref, mo_ref, vo_ref):
        g = g_ref[0].astype(f32)
        for i in range(1, p):
            g = g + g_ref[i].astype(f32)
        _adam_update(g, w_ref, m_ref, v_ref, go_ref, d_ref, mo_ref, vo_ref)

    blk = pl.BlockSpec((None,) * lead + (rows, c), lambda i: (0,) * lead + (i, 0))
    return pl.pallas_call(
        body, name=name, grid=(r // rows,), out_shape=tuple(jax.ShapeDtypeStruct(w.shape, f32) for _ in range(4)),
        in_specs=[pl.BlockSpec((p, rows, c), lambda i: (0, i, 0)), blk, blk, blk], out_specs=(blk, blk, blk, blk),
        compiler_params=_cp(("arbitrary",)),
    )(parts, w, m, v)


def _in_proj(x, g, scale, shift, w_pad, tl, comm):
    l = x.shape[0]
    c_in, c_out, c_sems = comm.specs()

    def body(*refs):
        x_ref, g_ref, sc_ref, sh_ref, w_ref = refs[:5]
        cin = refs[5:5 + len(c_in)]
        p_ref, h_ref = refs[5 + len(c_in):7 + len(c_in)]
        cout = refs[7 + len(c_in):7 + len(c_in) + comm.n]
        sems = refs[7 + len(c_in) + comm.n:]

        def compute():
            xv = x_ref[...]
            r = lax.rsqrt(jnp.mean(xv * xv, axis=-1, keepdims=True) + EPS)
            h = (xv * r) * (g_ref[...] * (1.0 + sc_ref[...])) + sh_ref[...]
            hb = h.astype(bf16)
            h_ref[...] = hb
            p_ref[...] = jnp.dot(hb, w_ref[...], preferred_element_type=f32).astype(bf16)

        comm.run(cin, cout, sems, l // tl, compute)

    vec = pl.BlockSpec((1, D), lambda i: (0, 0))
    return pl.pallas_call(
        body, name="in_proj", grid=(l // tl,),
        out_shape=(jax.ShapeDtypeStruct((l, PW), bf16), jax.ShapeDtypeStruct((l, D), bf16)) + tuple(comm.outs),
        in_specs=[pl.BlockSpec((tl, D), lambda i: (i, 0)), vec, vec, vec, pl.BlockSpec((D, PW), lambda i: (0, 0))] + c_in,
        out_specs=(pl.BlockSpec((tl, PW), lambda i: (i, 0)), pl.BlockSpec((tl, D), lambda i: (i, 0))) + tuple(c_out),
        scratch_shapes=c_sems, compiler_params=_cp(("arbitrary",)),
    )(x, g, scale, shift, w_pad, *comm.ins)


def _tri(rev):
    i = lax.broadcasted_iota(jnp.int32, (CH, CH), 0)
    j = lax.broadcasted_iota(jnp.int32, (CH, CH), 1)
    return jnp.where((j >= i) if rev else (j <= i), 1.0, 0.0).astype(f32)


def _head_masks():
    lane = lax.broadcasted_iota(jnp.int32, (1, KW), 1) // HK
    return [jnp.where(lane == h, 1.0, 0.0).astype(f32) for h in range(NH)]


def _block_diag():
    r = lax.broadcasted_iota(jnp.int32, (VW, KW), 0) // HV
    c = lax.broadcasted_iota(jnp.int32, (VW, KW), 1) // HK
    return jnp.where(r == c, 1.0, 0.0).astype(f32)


def _decay(lr, w2, gb, tri, rev):
    logits = _dot(lr, w2) + gb
    a = _log_sigmoid(logits) * (1.0 / TAU)
    c = _dot_hi(tri, a)
    cl = c[0:1, :] if rev else c[CH - 1:CH, :]
    return logits, c, cl


def _state_fwd(k, v, c, cl, st, bd):
    return st * jnp.exp(cl) + bd * _tn(v, k * jnp.exp(cl - c))


def _state_bwd(k, v, c, cl, st0, dst, trit):
    ecl = jnp.exp(cl)
    edec = jnp.exp(cl - c)
    kdec = k * edec
    dv = _nt(kdec, dst)
    dkdec = _dot(v, dst)
    dcl = jnp.sum(dst * st0, axis=0, keepdims=True) * ecl + jnp.sum(dkdec * kdec, axis=0, keepdims=True)
    da = _dot_hi(trit, -dkdec * kdec) + dcl
    return dkdec * edec, dv, da, dst * ecl


def _bdot(a, b):
    return lax.dot_general(a.astype(bf16), b.astype(bf16), (((2,), (1,)), ((0,), (0,))), preferred_element_type=f32)


def _bnt(a, b):
    return lax.dot_general(a.astype(bf16), b.astype(bf16), (((2,), (2,)), ((0,), (0,))), preferred_element_type=f32)


def _btn(a, b):
    return lax.dot_general(a.astype(bf16), b.astype(bf16), (((1,), (1,)), ((0,), (0,))), preferred_element_type=f32)


def _scan_chunks(x, rev):
    nc = x.shape[0]
    hi = x.astype(bf16)
    r1 = x - hi.astype(f32)
    mid = r1.astype(bf16)
    lo = (r1 - mid.astype(f32)).astype(bf16)
    terms = jnp.concatenate([hi, mid, lo], axis=1)
    tri3 = jnp.broadcast_to(jnp.concatenate([_tri(rev)] * 3, axis=1).astype(bf16)[None], (nc, CH, 3 * CH))
    return lax.dot_general(tri3, terms, (((2,), (1,)), ((0,), (0,))), preferred_element_type=f32)


class _Tile:
    pass


def _tile_prep(q_ref, k_ref, lr_ref, w2_ref, gb_ref, rev, nc):
    t = _Tile()
    tg = nc * CH
    t.logits = _dot(lr_ref[...], w2_ref[...]) + gb_ref[...]
    c = _scan_chunks((_log_sigmoid(t.logits) * (1.0 / TAU)).reshape(nc, CH, KW), rev)
    cl = c[:, 0:1, :] if rev else c[:, CH - 1:CH, :]
    k = k_ref[...].astype(f32).reshape(nc, CH, KW)
    t.ec, t.enc, t.edec, t.ecl = jnp.exp(c), jnp.exp(-c), jnp.exp(cl - c), jnp.exp(cl)
    t.qd = q_ref[...].astype(f32).reshape(nc, CH, KW) * t.ec * QSCALE
    t.kd = k * t.enc
    t.kdec = k * t.edec
    hm = _head_masks()
    t.tri4 = jnp.concatenate([_tri(rev)] * NH, axis=0)[None]
    t.qs = jnp.concatenate([t.qd * hm[h] for h in range(NH)], axis=1)
    t.pst = _bnt(t.qs, t.kd) * t.tri4
    return t


def _gla_fwd(p, w2p, gb, s0, rev, tg, name):
    l = p.shape[0]
    nb, nc = l // tg, tg // CH

    def body(q_ref, k_ref, v_ref, lr_ref, w2_ref, gb_ref, s0_ref, o_ref, st_ref, st):
        @pl.when(pl.program_id(0) == 0)
        def _():
            st[...] = s0_ref[...]

        t = _tile_prep(q_ref, k_ref, lr_ref, w2_ref, gb_ref, rev, nc)
        v = v_ref[...].reshape(nc, CH, VW)
        intra = jnp.concatenate([_bdot(t.pst[:, h * CH:(h + 1) * CH], v[:, :, h * HV:(h + 1) * HV]) for h in range(NH)], axis=2)
        kv = _btn(v, t.kdec) * _block_diag()[None]
        s = st[...]
        for n in (range(nc - 1, -1, -1) if rev else range(nc)):
            st_ref[n] = s
            s = s * t.ecl[n] + kv[n]
        st[...] = s
        o_ref[...] = (_bnt(t.qd, st_ref[...]) + intra).reshape(tg, VW)

    blk = (lambda i: nb - 1 - i) if rev else (lambda i: i)
    return pl.pallas_call(
        body, name=name, grid=(nb,),
        out_shape=(jax.ShapeDtypeStruct((l, VW), f32), jax.ShapeDtypeStruct((l // CH, VW, KW), f32)),
        in_specs=[pl.BlockSpec((tg, KW), lambda i: (blk(i), PQ // KW)), pl.BlockSpec((tg, KW), lambda i: (blk(i), PK // KW)),
                  pl.BlockSpec((tg, VW), lambda i: (blk(i), PV // VW)), pl.BlockSpec((tg, LRW), lambda i: (blk(i), PLR // LRW)),
                  pl.BlockSpec((LRW, KW), lambda i: (0, 0)), pl.BlockSpec((1, KW), lambda i: (0, 0)),
                  pl.BlockSpec((VW, KW), lambda i: (0, 0))],
        out_specs=(pl.BlockSpec((tg, VW), lambda i: (blk(i), 0)), pl.BlockSpec((nc, VW, KW), lambda i: (blk(i), 0, 0))),
        scratch_shapes=[pltpu.VMEM((VW, KW), f32)],
        compiler_params=_cp(("arbitrary",)),
    )(p, p, p, p, w2p, gb, s0)


def _gla_bwd(p, do, states, w2p, gb, prev, rev, tg, name, comm):
    l = p.shape[0]
    nb, nc = l // tg, tg // CH
    out_dt = f32 if prev is None else bf16
    c_in, c_out, c_sems = comm.specs()

    def body(*refs):
        q_ref, k_ref, v_ref, lr_ref, do_ref, st_ref, w2_ref, gb_ref = refs[:8]
        refs = refs[8:]
        if prev is not None:
            pq_ref, pl_ref = refs[:2]
            refs = refs[2:]
        cin, refs = refs[:len(c_in)], refs[len(c_in):]
        dqkv_ref, dlr_ref, dw2_ref, dgb_ref, ds0_ref = refs[:5]
        cout, dst, ds_buf, sems = refs[5:5 + comm.n], refs[5 + comm.n], refs[6 + comm.n], refs[7 + comm.n:]
        comm.run(cin, cout, sems, nb, lambda: compute(q_ref, k_ref, v_ref, lr_ref, do_ref, st_ref, w2_ref, gb_ref,
                                                      pq_ref if prev is not None else None, pl_ref if prev is not None else None,
                                                      dqkv_ref, dlr_ref, dw2_ref, dgb_ref, ds0_ref, dst, ds_buf))

    def compute(q_ref, k_ref, v_ref, lr_ref, do_ref, st_ref, w2_ref, gb_ref, pq_ref, pl_ref,
                dqkv_ref, dlr_ref, dw2_ref, dgb_ref, ds0_ref, dst, ds_buf):
        @pl.when(pl.program_id(0) == 0)
        def _():
            dst[...] = jnp.zeros_like(dst)
            dw2_ref[...] = jnp.zeros_like(dw2_ref)
            dgb_ref[...] = jnp.zeros_like(dgb_ref)

        t = _tile_prep(q_ref, k_ref, lr_ref, w2_ref, gb_ref, rev, nc)
        hm = _head_masks()
        v = v_ref[...].reshape(nc, CH, VW)
        do = do_ref[...].reshape(nc, CH, VW)
        heads = lambda a, h: a[:, :, h * HV:(h + 1) * HV]
        dpst = jnp.concatenate([_bnt(heads(do, h), heads(v, h)) for h in range(NH)], axis=1) * t.tri4
        dv = jnp.concatenate([_btn(t.pst[:, h * CH:(h + 1) * CH], heads(do, h)) for h in range(NH)], axis=2)
        dqd = _bdot(do, st_ref[...])
        for h in range(NH):
            dqd = dqd + hm[h] * _bdot(dpst[:, h * CH:(h + 1) * CH], t.kd)
        dkd = _btn(dpst, t.qs)
        u = _btn(do, t.qd) * _block_diag()[None]
        d = dst[...]
        for n in (range(nc) if rev else range(nc - 1, -1, -1)):
            ds_buf[n] = d
            d = d * t.ecl[n] + u[n]
        dst[...] = d
        ds0_ref[...] = d
        ds = ds_buf[...]
        dv = dv + _bnt(t.kdec, ds)
        dkdec = _bdot(v, ds)
        dcl = jnp.sum(ds * st_ref[...], axis=1, keepdims=True) * t.ecl + jnp.sum(dkdec * t.kdec, axis=1, keepdims=True)
        dc = dqd * t.qd - dkd * t.kd - dkdec * t.kdec
        da = _scan_chunks(dc, not rev) + dcl
        dq = dqd * t.ec * QSCALE
        dk = dkd * t.enc + dkdec * t.edec
        dlog = da.reshape(tg, KW) * _sigmoid(-t.logits) * (1.0 / TAU)
        dlr = _nt(dlog, w2_ref[...])
        dw2_ref[...] += _tn(lr_ref[...], dlog)
        dgb_ref[...] += jnp.sum(dlog, axis=0, keepdims=True)
        dqkv = jnp.concatenate([dq, dk, dv], axis=2).reshape(tg, 2 * KW + VW)
        if prev is not None:
            dqkv = dqkv + pq_ref[...]
            dlr = dlr + pl_ref[...]
        dqkv_ref[...] = dqkv.astype(out_dt)
        dlr_ref[...] = dlr.astype(out_dt)

    blk = (lambda i: i) if rev else (lambda i: nb - 1 - i)
    in_specs = [pl.BlockSpec((tg, KW), lambda i: (blk(i), PQ // KW)), pl.BlockSpec((tg, KW), lambda i: (blk(i), PK // KW)),
                pl.BlockSpec((tg, VW), lambda i: (blk(i), PV // VW)), pl.BlockSpec((tg, LRW), lambda i: (blk(i), PLR // LRW)),
                pl.BlockSpec((tg, VW), lambda i: (blk(i), 0)), pl.BlockSpec((nc, VW, KW), lambda i: (blk(i), 0, 0)),
                pl.BlockSpec((LRW, KW), lambda i: (0, 0)), pl.BlockSpec((1, KW), lambda i: (0, 0))]
    args = [p, p, p, p, do, states, w2p, gb]
    if prev is not None:
        in_specs += [pl.BlockSpec((tg, 2 * KW + VW), lambda i: (blk(i), 0)), pl.BlockSpec((tg, LRW), lambda i: (blk(i), 0))]
        args += list(prev)
    return pl.pallas_call(
        body, name=name, grid=(nb,),
        out_shape=(jax.ShapeDtypeStruct((l, 2 * KW + VW), out_dt), jax.ShapeDtypeStruct((l, LRW), out_dt),
                   jax.ShapeDtypeStruct((LRW, KW), f32), jax.ShapeDtypeStruct((1, KW), f32), jax.ShapeDtypeStruct((VW, KW), f32))
        + tuple(comm.outs),
        in_specs=in_specs + c_in,
        out_specs=(pl.BlockSpec((tg, 2 * KW + VW), lambda i: (blk(i), 0)), pl.BlockSpec((tg, LRW), lambda i: (blk(i), 0)),
                   pl.BlockSpec((LRW, KW), lambda i: (0, 0)), pl.BlockSpec((1, KW), lambda i: (0, 0)),
                   pl.BlockSpec((VW, KW), lambda i: (0, 0))) + tuple(c_out),
        scratch_shapes=[pltpu.VMEM((VW, KW), f32), pltpu.VMEM((nc, VW, KW), f32)] + c_sems,
        compiler_params=_cp(("arbitrary",)),
    )(*args, *comm.ins)


def _ctx_hidden(ctx_ref, g_ref, sc_ref, sh_ref):
    xv = ctx_ref[...]
    r = lax.rsqrt(jnp.mean(xv * xv, axis=-1, keepdims=True) + EPS)
    xn = xv * r
    return xn, xn * (g_ref[...] * (1.0 + sc_ref[...])) + sh_ref[...]


_CTX_W_SPECS = [pl.BlockSpec((D, KW), lambda i: (0, PK // KW)), pl.BlockSpec((D, VW), lambda i: (0, PV // VW)),
                pl.BlockSpec((D, LRW), lambda i: (0, PLR // LRW))]


def _ctx_fwd(ctx, g, scale, shift, w_pad, w2f, w2b, gbf, gbb):
    ncc = CTX // CH

    def body(ctx_ref, g_ref, sc_ref, sh_ref, wk_ref, wv_ref, wl_ref, w2f_ref, w2b_ref, gbf_ref, gbb_ref, sf_ref, sb_ref):
        _, hc = _ctx_hidden(ctx_ref, g_ref, sc_ref, sh_ref)
        k, v, lr = _dot(hc, wk_ref[...]), _dot(hc, wv_ref[...]), _dot(hc, wl_ref[...])
        bd = _block_diag()
        for rev, w2_ref, gb_ref, out in ((False, w2f_ref, gbf_ref, sf_ref), (True, w2b_ref, gbb_ref, sb_ref)):
            tri = _tri(rev)
            st = jnp.zeros((VW, KW), f32)
            for j in (range(ncc - 1, -1, -1) if rev else range(ncc)):
                rows = slice(j * CH, (j + 1) * CH)
                _, c, cl = _decay(lr[rows], w2_ref[...], gb_ref[...], tri, rev)
                st = _state_fwd(k[rows], v[rows], c, cl, st, bd)
            out[...] = st

    vec = pl.BlockSpec((1, D), lambda i: (0, 0))
    w2s = pl.BlockSpec((LRW, KW), lambda i: (0, 0))
    gbs = pl.BlockSpec((1, KW), lambda i: (0, 0))
    sts = pl.BlockSpec((VW, KW), lambda i: (0, 0))
    return pl.pallas_call(
        body, name="ctx_fwd", grid=(1,), out_shape=(jax.ShapeDtypeStruct((VW, KW), f32),) * 2,
        in_specs=[pl.BlockSpec((CTX, D), lambda i: (0, 0)), vec, vec, vec] + _CTX_W_SPECS + [w2s, w2s, gbs, gbs],
        out_specs=(sts, sts), compiler_params=_cp(("arbitrary",)),
    )(ctx, g, scale, shift, w_pad, w_pad, w_pad, w2f, w2b, gbf, gbb)


def _ctx_bwd(ctx, g, scale, shift, w_pad, w2f, w2b, gbf, gbb, dsf, dsb):
    ncc = CTX // CH

    def body(ctx_ref, g_ref, sc_ref, sh_ref, wk_ref, wv_ref, wl_ref, w2f_ref, w2b_ref, gbf_ref, gbb_ref, dsf_ref, dsb_ref,
             dwk_ref, dwv_ref, dwl_ref, dmod_ref, dg_ref, dw2_ref, dgb_ref):
        xn, hc = _ctx_hidden(ctx_ref, g_ref, sc_ref, sh_ref)
        k, v, lr = _dot(hc, wk_ref[...]), _dot(hc, wv_ref[...]), _dot(hc, wl_ref[...])
        bd = _block_diag()
        dk_rows, dv_rows, dl_rows = [None] * ncc, [None] * ncc, [None] * ncc
        for d, (rev, w2_ref, gb_ref, ds_ref) in enumerate(((False, w2f_ref, gbf_ref, dsf_ref), (True, w2b_ref, gbb_ref, dsb_ref))):
            tri = _tri(rev)
            order = list(range(ncc - 1, -1, -1) if rev else range(ncc))
            st, saved = jnp.zeros((VW, KW), f32), {}
            for j in order:
                rows = slice(j * CH, (j + 1) * CH)
                logits, c, cl = _decay(lr[rows], w2_ref[...], gb_ref[...], tri, rev)
                saved[j] = (logits, c, cl, st)
                st = _state_fwd(k[rows], v[rows], c, cl, st, bd)
            dst = ds_ref[...]
            dw2 = jnp.zeros((LRW, KW), f32)
            dgb = jnp.zeros((1, KW), f32)
            for j in reversed(order):
                rows = slice(j * CH, (j + 1) * CH)
                logits, c, cl, st0 = saved[j]
                dk, dv, da, dst = _state_bwd(k[rows], v[rows], c, cl, st0, dst, _tri(not rev))
                dlog = da * _sigmoid(-logits) * (1.0 / TAU)
                dl = _nt(dlog, w2_ref[...])
                dw2 = dw2 + _tn(lr[rows], dlog)
                dgb = dgb + jnp.sum(dlog, axis=0, keepdims=True)
                dk_rows[j] = dk if dk_rows[j] is None else dk_rows[j] + dk
                dv_rows[j] = dv if dv_rows[j] is None else dv_rows[j] + dv
                dl_rows[j] = dl if dl_rows[j] is None else dl_rows[j] + dl
            dw2_ref[d] = dw2
            dgb_ref[d] = dgb
        dk, dv, dl = (jnp.concatenate(t, axis=0) for t in (dk_rows, dv_rows, dl_rows))
        dwk_ref[...] = _tn(hc, dk)
        dwv_ref[...] = _tn(hc, dv)
        dwl_ref[...] = _tn(hc, dl)
        dh = _nt(dk, wk_ref[...]) + _nt(dv, wv_ref[...]) + _nt(dl, wl_ref[...])
        gx = dh * xn
        dmod_ref[:, 0:D] = jnp.sum(dh, axis=0, keepdims=True)
        dmod_ref[:, D:2 * D] = jnp.sum(gx, axis=0, keepdims=True) * g_ref[...]
        dg_ref[...] = jnp.sum(gx, axis=0, keepdims=True) * (1.0 + sc_ref[...])

    vec = pl.BlockSpec((1, D), lambda i: (0, 0))
    w2s = pl.BlockSpec((LRW, KW), lambda i: (0, 0))
    gbs = pl.BlockSpec((1, KW), lambda i: (0, 0))
    sts = pl.BlockSpec((VW, KW), lambda i: (0, 0))
    full = lambda *s: pl.BlockSpec(s, lambda i: (0,) * len(s))
    return pl.pallas_call(
        body, name="ctx_bwd", grid=(1,),
        out_shape=(jax.ShapeDtypeStruct((D, KW), f32), jax.ShapeDtypeStruct((D, VW), f32), jax.ShapeDtypeStruct((D, LRW), f32),
                   jax.ShapeDtypeStruct((1, 2 * D), f32), jax.ShapeDtypeStruct((1, D), f32),
                   jax.ShapeDtypeStruct((2, LRW, KW), f32), jax.ShapeDtypeStruct((2, 1, KW), f32)),
        in_specs=[pl.BlockSpec((CTX, D), lambda i: (0, 0)), vec, vec, vec] + _CTX_W_SPECS + [w2s, w2s, gbs, gbs, sts, sts],
        out_specs=(full(D, KW), full(D, VW), full(D, LRW), full(1, 2 * D), full(1, D), full(2, LRW, KW), full(2, 1, KW)),
        compiler_params=_cp(("arbitrary",)),
    )(ctx, g, scale, shift, w_pad, w_pad, w_pad, w2f, w2b, gbf, gbb, dsf, dsb)


def _layernorm(va, g, b):
    mu = jnp.mean(va, axis=-1, keepdims=True)
    xc = va - mu
    rstd = lax.rsqrt(jnp.mean(xc * xc, axis=-1, keepdims=True) + EPS)
    vhat = xc * rstd
    return vhat, rstd, vhat * g + b


def _mix_fwd(p, ln_g, ln_b, ws, bs_t):
    l = p.shape[0]
    nch = l // ACH
    half = AW // 2

    def body(p_ref, g_ref, b_ref, ws_ref, bs_ref, sv_ref, va_buf, col_buf, sem):
        cp = pltpu.make_async_copy(p_ref.at[:, pl.ds(PVA, AW)], va_buf, sem)
        cp.start()
        cp.wait()

        def rows_step(n, carry):
            rows = pl.ds(pl.multiple_of(n * ACH, ACH), ACH)
            _, _, vn = _layernorm(va_buf[rows, :].astype(f32), g_ref[...], b_ref[...])
            for gi in range(2):
                sl = slice(gi * ACH, (gi + 1) * ACH)
                sv_ref[rows, sl] = (_dot(ws_ref[gi], vn[:, sl]) + bs_ref[:, gi:gi + 1]).astype(bf16)
            col_buf[0, rows, :] = vn[:, half:half + ACH]
            col_buf[1, rows, :] = vn[:, half + ACH:]
            return carry

        lax.fori_loop(0, nch, rows_step, 0, unroll=8)

        def cols_step(cidx, carry):
            rows = pl.ds(cidx, ACH, stride=GW)
            for gi in range(2, 4):
                col_buf[gi - 2, rows, :] = _dot(ws_ref[gi], col_buf[gi - 2, rows, :]) + bs_ref[:, gi:gi + 1]
            return carry

        lax.fori_loop(0, GW, cols_step, 0, unroll=8)

        def out_step(n, carry):
            rows = pl.ds(pl.multiple_of(n * ACH, ACH), ACH)
            sv_ref[rows, half:half + ACH] = col_buf[0, rows, :].astype(bf16)
            sv_ref[rows, half + ACH:] = col_buf[1, rows, :].astype(bf16)
            return carry

        lax.fori_loop(0, nch, out_step, 0, unroll=8)

    vm = pl.BlockSpec(memory_space=pltpu.VMEM)
    return pl.pallas_call(
        body, name="mix_fwd", out_shape=jax.ShapeDtypeStruct((l, AW), bf16),
        in_specs=[pl.BlockSpec(memory_space=pl.ANY), vm, vm, vm, vm], out_specs=vm,
        scratch_shapes=[pltpu.VMEM((l, AW), bf16), pltpu.VMEM((2, l, ACH), f32), pltpu.SemaphoreType.DMA(())],
        compiler_params=_cp(),
    )(p, ln_g, ln_b, ws, bs_t)


def _mix_bwd(p, dsv, ln_g, ln_b, ws_t):
    l = p.shape[0]
    nch = l // ACH
    half = AW // 2

    def body(p_ref, dsv_hbm, g_ref, b_ref, wst_ref, dva_ref, dws_ref, dbs_ref, dg_ref, db_ref, va_buf, dsv_buf, vn_col, ds_col, sems):
        cp1 = pltpu.make_async_copy(p_ref.at[:, pl.ds(PVA, AW)], va_buf, sems.at[0])
        cp2 = pltpu.make_async_copy(dsv_hbm, dsv_buf, sems.at[1])
        cp1.start()
        cp2.start()
        dws_ref[...] = jnp.zeros_like(dws_ref)
        dbs_ref[...] = jnp.zeros_like(dbs_ref)
        dg_ref[...] = jnp.zeros_like(dg_ref)
        db_ref[...] = jnp.zeros_like(db_ref)
        cp1.wait()
        cp2.wait()

        def rows_step(n, carry):
            rows = pl.ds(pl.multiple_of(n * ACH, ACH), ACH)
            _, _, vn = _layernorm(va_buf[rows, :].astype(f32), g_ref[...], b_ref[...])
            ds = dsv_buf[rows, :].astype(f32)
            for gi in range(2):
                sl = slice(gi * ACH, (gi + 1) * ACH)
                dws_ref[gi] += _nt(ds[:, sl], vn[:, sl])
                dbs_ref[gi] += ds[:, sl]
            for gi in range(2):
                sl = slice(half + gi * ACH, half + (gi + 1) * ACH)
                vn_col[gi, rows, :] = vn[:, sl]
                ds_col[gi, rows, :] = ds[:, sl]
            return carry

        lax.fori_loop(0, nch, rows_step, 0, unroll=8)

        def cols_step(cidx, carry):
            rows = pl.ds(cidx, ACH, stride=GW)
            for gi in range(2, 4):
                ds = ds_col[gi - 2, rows, :]
                dws_ref[gi] += _nt(ds, vn_col[gi - 2, rows, :])
                dbs_ref[gi] += ds
                ds_col[gi - 2, rows, :] = _dot(wst_ref[gi], ds)
            return carry

        lax.fori_loop(0, GW, cols_step, 0, unroll=8)

        def out_step(n, carry):
            rows = pl.ds(pl.multiple_of(n * ACH, ACH), ACH)
            vhat, rstd, _ = _layernorm(va_buf[rows, :].astype(f32), g_ref[...], b_ref[...])
            ds = dsv_buf[rows, :].astype(f32)
            dvn = jnp.concatenate([_dot(wst_ref[0], ds[:, 0:ACH]), _dot(wst_ref[1], ds[:, ACH:half]), ds_col[0, rows, :], ds_col[1, rows, :]], axis=1)
            dg_ref[...] += jnp.sum(dvn * vhat, axis=0, keepdims=True)
            db_ref[...] += jnp.sum(dvn, axis=0, keepdims=True)
            dvh = dvn * g_ref[...]
            dva = rstd * (dvh - jnp.mean(dvh, axis=-1, keepdims=True) - vhat * jnp.mean(dvh * vhat, axis=-1, keepdims=True))
            dva_ref[rows, :] = dva.astype(bf16)
            return carry

        lax.fori_loop(0, nch, out_step, 0, unroll=8)

    vm = pl.BlockSpec(memory_space=pltpu.VMEM)
    hbm = pl.BlockSpec(memory_space=pl.ANY)
    return pl.pallas_call(
        body, name="mix_bwd",
        out_shape=(jax.ShapeDtypeStruct((l, AW), bf16), jax.ShapeDtypeStruct((4, ACH, ACH), f32), jax.ShapeDtypeStruct((4, ACH, ACH), f32),
                   jax.ShapeDtypeStruct((1, AW), f32), jax.ShapeDtypeStruct((1, AW), f32)),
        in_specs=[hbm, hbm, vm, vm, vm], out_specs=(vm, vm, vm, vm, vm),
        scratch_shapes=[pltpu.VMEM((l, AW), bf16), pltpu.VMEM((l, AW), bf16), pltpu.VMEM((2, l, ACH), f32), pltpu.VMEM((2, l, ACH), f32),
                        pltpu.SemaphoreType.DMA((2,))],
        compiler_params=_cp(),
    )(p, dsv, ln_g, ln_b, ws_t)


def _mid(x, tgt, p, o_f, o_b, sv, gate, gf, gb_norm, w_pa, w_pb, w_out, tl):
    l = x.shape[0]

    def body(x_ref, t_ref, zb_ref, ua_ref, za_ref, g1_ref, g2_ref, of_ref, ob_ref, sv_ref, gate_ref, gf_ref, gbn_ref,
             wpa_ref, wpb_ref, wout_ref,
             dx1_ref, dzbua_ref, dzag_ref, dsv_ref, do_ref, dwout_bf, dwpa_bf, dwpb_bf, dgf_ref, dgate_ref, dgbn_ref, loss_ref,
             dwout_ref, dwpa_ref, dwpb_ref, m_p, dy2_p, ta_p, dya_p, tb_p, dyb_p):
        @pl.when(pl.program_id(0) == 0)
        def _():
            for r in (dwout_ref, dwpa_ref, dwpb_ref, dgf_ref, dgate_ref, dgbn_ref, loss_ref, m_p, dy2_p, ta_p, dya_p, tb_p, dyb_p):
                r[...] = jnp.zeros_like(r)

        o = of_ref[...] + ob_ref[...]
        rr = jnp.concatenate(
            [jnp.broadcast_to(lax.rsqrt(jnp.mean(o[:, h * HV:(h + 1) * HV] ** 2, axis=-1, keepdims=True) + EPS), (tl, HV))
             for h in range(NH)], axis=1)
        ohat = o * rr
        on = ohat * gbn_ref[...]
        szb, dszb = _silu_and_grad(zb_ref[...].astype(f32))
        tb = on * szb
        u = ua_ref[...].astype(f32)
        svv = sv_ref[...].astype(f32)
        sza, dsza = _silu_and_grad(za_ref[...].astype(f32))
        ta = u * svv * sza
        ya = _dot(ta, wpa_ref[...])
        yb = _dot(tb, wpb_ref[...])
        dwout_ref[...] += _tn(m_p[...], dy2_p[...])
        g1 = _sigmoid(g1_ref[...].astype(f32))
        g2 = _sigmoid(g2_ref[...].astype(f32))
        m = g1 * ya + g2 * yb
        y2 = _dot(m, wout_ref[...])
        dwpa_ref[...] += _tn(ta_p[...], dya_p[...])
        x1 = x_ref[...] + gate_ref[...] * y2
        r1 = lax.rsqrt(jnp.mean(x1 * x1, axis=-1, keepdims=True) + EPS)
        x1n = x1 * r1
        err = x1n * gf_ref[...] - t_ref[...]
        loss_ref[...] += jnp.sum(jnp.sum(err * err, axis=-1, keepdims=True), axis=0, keepdims=True) * (0.5 / D)
        dout = err * (1.0 / D)
        dgf_ref[...] += jnp.sum(dout * x1n, axis=0, keepdims=True)
        dx1n = dout * gf_ref[...]
        dx1 = r1 * (dx1n - x1n * jnp.mean(dx1n * x1n, axis=-1, keepdims=True))
        dx1_ref[...] = dx1
        dgate_ref[...] += jnp.sum(dx1 * y2, axis=0, keepdims=True)
        dy2 = dx1 * gate_ref[...]
        dm = _nt(dy2, wout_ref[...])
        dwpb_ref[...] += _tn(tb_p[...], dyb_p[...])
        dya = dm * g1
        dyb = dm * g2
        dzag_ref[:, AW:AW + D] = (dya * ya * (1.0 - g1)).astype(bf16)
        dzag_ref[:, AW + D:] = (dyb * yb * (1.0 - g2)).astype(bf16)
        dta = _nt(dya, wpa_ref[...])
        dzbua_ref[:, AW:] = (dta * svv * sza).astype(bf16)
        dsv_ref[...] = (dta * u * sza).astype(bf16)
        dzag_ref[:, 0:AW] = (dta * u * svv * dsza).astype(bf16)
        dtb = _nt(dyb, wpb_ref[...])
        don = dtb * szb
        dzbua_ref[:, 0:AW] = (dtb * on * dszb).astype(bf16)
        dgbn_ref[...] += jnp.sum(don * ohat, axis=0, keepdims=True)
        doh = don * gbn_ref[...]
        prod = doh * ohat
        mh = jnp.concatenate(
            [jnp.broadcast_to(jnp.mean(prod[:, h * HV:(h + 1) * HV], axis=-1, keepdims=True), (tl, HV)) for h in range(NH)], axis=1)
        do_ref[...] = (rr * (doh - ohat * mh)).astype(bf16)
        for kept, val in ((m_p, m), (dy2_p, dy2), (ta_p, ta), (dya_p, dya), (tb_p, tb), (dyb_p, dyb)):
            kept[...] = val.astype(bf16)

        @pl.when(pl.program_id(0) == l // tl - 1)
        def _():
            dwout_bf[...] = (dwout_ref[...] + _tn(m_p[...], dy2_p[...])).astype(bf16)
            dwpa_bf[...] = (dwpa_ref[...] + _tn(ta_p[...], dya_p[...])).astype(bf16)
            dwpb_bf[...] = (dwpb_ref[...] + _tn(tb_p[...], dyb_p[...])).astype(bf16)

    row = lambda w, j: pl.BlockSpec((tl, w), lambda i, j=j: (i, j))
    full = lambda *s: pl.BlockSpec(s, lambda i: (0,) * len(s))
    return pl.pallas_call(
        body, name="mid", grid=(l // tl,),
        out_shape=(jax.ShapeDtypeStruct((l, D), f32), jax.ShapeDtypeStruct((l, 2 * AW), bf16), jax.ShapeDtypeStruct((l, AW + 2 * D), bf16),
                   jax.ShapeDtypeStruct((l, AW), bf16), jax.ShapeDtypeStruct((l, VW), bf16),
                   jax.ShapeDtypeStruct((D, D), bf16), jax.ShapeDtypeStruct((AW, D), bf16), jax.ShapeDtypeStruct((VW, D), bf16),
                   jax.ShapeDtypeStruct((1, D), f32), jax.ShapeDtypeStruct((1, D), f32), jax.ShapeDtypeStruct((1, VW), f32),
                   jax.ShapeDtypeStruct((1, 1), f32)),
        scratch_shapes=[pltpu.VMEM((D, D), f32), pltpu.VMEM((AW, D), f32), pltpu.VMEM((VW, D), f32),
                        pltpu.VMEM((tl, D), bf16), pltpu.VMEM((tl, D), bf16), pltpu.VMEM((tl, AW), bf16), pltpu.VMEM((tl, D), bf16),
                        pltpu.VMEM((tl, VW), bf16), pltpu.VMEM((tl, D), bf16)],
        in_specs=[row(D, 0), row(D, 0), row(AW, PZB // AW), row(AW, PUA // AW), row(AW, PZA // AW), row(D, PG1 // D), row(D, PG2 // D),
                  row(VW, 0), row(VW, 0), row(AW, 0), full(1, D), full(1, D), full(1, VW), full(AW, D), full(VW, D), full(D, D)],
        out_specs=(row(D, 0), row(2 * AW, 0), row(AW + 2 * D, 0), row(AW, 0), row(VW, 0),
                   full(D, D), full(AW, D), full(VW, D), full(1, D), full(1, D), full(1, VW), full(1, 1)),
        compiler_params=_cp(("arbitrary",)),
    )(x, tgt, p, p, p, p, p, o_f, o_b, sv, gate, gf, gb_norm, w_pa, w_pb, w_out)


def _in_bwd(x, dx1, dqkv, dzbua, dva, dzag, dlr, w_pad, g, scale, tl, comm):
    l = x.shape[0]
    c_in, c_out, c_sems = comm.specs()

    def body(*refs):
        cin = refs[14:14 + len(c_in)]
        outs = refs[14 + len(c_in):]
        comm.run(cin, outs[4:4 + comm.n], outs[4 + comm.n:], l // tl, lambda: compute(*refs[:14], *outs[:4]))

    def compute(x_ref, dx1_ref, a_ref, b_ref, c_ref, e_ref, lr_ref, wa_ref, wb_ref, wc_ref, we_ref, wl_ref, g_ref, sc_ref,
                gx_ref, dsh_ref, dsc_ref, dg_ref):
        @pl.when(pl.program_id(0) == 0)
        def _():
            for r in (dsh_ref, dsc_ref, dg_ref):
                r[...] = jnp.zeros_like(r)

        dh = (_nt(a_ref[...], wa_ref[...]) + _nt(b_ref[...], wb_ref[...]) + _nt(c_ref[...], wc_ref[...]) + _nt(e_ref[...], we_ref[...])
              + _nt(lr_ref[...], wl_ref[...]))
        xv = x_ref[...]
        r = lax.rsqrt(jnp.mean(xv * xv, axis=-1, keepdims=True) + EPS)
        xn = xv * r
        gxn = jnp.sum(dh * xn, axis=0, keepdims=True)
        dsh_ref[...] += jnp.sum(dh, axis=0, keepdims=True)
        dsc_ref[...] += gxn * g_ref[...]
        dg_ref[...] += gxn * (1.0 + sc_ref[...])
        dxn = dh * (g_ref[...] * (1.0 + sc_ref[...]))
        gx_ref[...] = dx1_ref[...] + r * (dxn - xn * jnp.mean(dxn * xn, axis=-1, keepdims=True))

    row = lambda w: pl.BlockSpec((tl, w), lambda i: (i, 0))
    wcol = lambda w, j: pl.BlockSpec((D, w), lambda i, j=j: (0, j))
    vec = pl.BlockSpec((1, D), lambda i: (0, 0))
    return pl.pallas_call(
        body, name="in_bwd", grid=(l // tl,),
        out_shape=(jax.ShapeDtypeStruct((l, D), f32),) + (jax.ShapeDtypeStruct((1, D), f32),) * 3 + tuple(comm.outs),
        in_specs=[row(D), row(D), row(2 * KW + VW), row(2 * AW), row(AW), row(AW + 2 * D), row(LRW),
                  wcol(2 * KW + VW, 0), wcol(2 * AW, PZB // (2 * AW)), wcol(AW, PVA // AW), wcol(AW + 2 * D, PZA // (AW + 2 * D)),
                  wcol(LRW, PLR // LRW), vec, vec] + c_in,
        out_specs=(row(D), vec, vec, vec) + tuple(c_out), scratch_shapes=c_sems,
        compiler_params=_cp(("arbitrary",)),
    )(x, dx1, dqkv, dzbua, dva, dzag, dlr, w_pad, w_pad, w_pad, w_pad, w_pad, g, scale, *comm.ins)


def _tn_matmul(a, bs, tl, name, comm=None):
    l, m = a.shape
    k = len(bs)
    comm = comm or _Comm([], [], None)
    c_in, c_out, c_sems = comm.specs()

    def body(a_ref, *refs):
        b_refs, cin = refs[:k], refs[k:k + len(c_in)]
        o_refs = refs[k + len(c_in):2 * k + len(c_in)]
        cout, sems = refs[2 * k + len(c_in):2 * k + len(c_in) + comm.n], refs[2 * k + len(c_in) + comm.n:]

        def compute():
            @pl.when(pl.program_id(0) == 0)
            def _():
                for o_ref in o_refs:
                    o_ref[...] = jnp.zeros_like(o_ref)

            av = a_ref[...]
            for b_ref, o_ref in zip(b_refs, o_refs):
                o_ref[...] += _tn(av, b_ref[...])

        comm.run(cin, cout, sems, l // tl, compute)

    return pl.pallas_call(
        body, name=name, grid=(l // tl,), out_shape=tuple(jax.ShapeDtypeStruct((m, b.shape[1]), f32) for b in bs) + tuple(comm.outs),
        in_specs=[pl.BlockSpec((tl, m), lambda i: (i, 0))] + [pl.BlockSpec((tl, b.shape[1]), lambda i: (i, 0)) for b in bs] + c_in,
        out_specs=tuple(pl.BlockSpec((m, b.shape[1]), lambda i: (0, 0)) for b in bs) + tuple(c_out),
        scratch_shapes=c_sems, compiler_params=_cp(("arbitrary",)),
    )(a, *bs, *comm.ins)


def _pad_gate(w2, gb):
    z = jnp.zeros((RANK, KW), f32)
    tail = jnp.zeros((LRW - 2 * RANK, KW), f32)
    w2f = jnp.concatenate([w2[0], z, tail], axis=0)
    w2b = jnp.concatenate([z, w2[1], tail], axis=0)
    return w2f, w2b, gb[0:1], gb[1:2]


EARLY_A_ROWS = 384


class _NoExchange:
    def __init__(self, w_pa, w_pb, w_out):
        self.weights = (w_pa, w_pb, w_out)

    def gather_proj(self):
        return _Comm([], [], None)

    def proj_weights(self, got):
        return self.weights

    def first(self, dw_out, dw_pa, dw_pb):
        return _Comm([], [], None)

    def early_a(self, small, blocks):
        return _Comm([], [], None)

    def early_b(self, blocks):
        return _Comm([], [], None)

    def late(self, blocks, dgt):
        return _Comm([], [], None)


class _Exchanges:
    def __init__(self, pa, pb, wo):
        self.shards = (pa, pb, wo)

    def gather_proj(self):
        def plan(i, o):
            srcs = [lambda j, r=r: r for r in i]
            dsts = [lambda j: o[0].at[:, _lanes(j)], lambda j: o[1].at[:, _lanes(j)], lambda j: o[2].at[j]]
            return srcs, dsts, None
        sds = jax.ShapeDtypeStruct
        return _Comm(self.shards, [sds((AW, D), bf16), sds((VW, D), bf16), sds((NDEV, 128, D), bf16)], plan)

    def proj_weights(self, got):
        return got[0], got[1], got[2].reshape(D, D)

    def first(self, dw_out, dw_pa, dw_pb):
        def plan(i, o):
            srcs = [lambda j: i[0].at[j], lambda j: i[1].at[:, _lanes(j)], lambda j: i[2].at[:, _lanes(j)]]
            dsts = [lambda j, r=r: r.at[j] for r in o]
            return srcs, dsts, None
        sds = jax.ShapeDtypeStruct
        return _Comm([dw_out.reshape(NDEV, 128, D), dw_pa, dw_pb],
                     [sds((NDEV, 128, D), bf16), sds((NDEV, AW, 128), bf16), sds((NDEV, VW, 128), bf16)], plan)

    def early_a(self, small, blocks):
        def plan(i, o):
            srcs = [lambda j: i[0], lambda j: i[1].at[j]]
            dsts = [lambda j, r=r: r.at[j] for r in o]
            return srcs, dsts, [None, lambda j: j >= LATE_DESTS - 1]
        sds = jax.ShapeDtypeStruct
        return _Comm([small, blocks], [sds((NDEV,) + small.shape, f32), sds(blocks.shape, bf16)], plan)

    def early_b(self, blocks):
        def plan(i, o):
            return [lambda j: i[0].at[j]], [lambda j: o[0].at[j]], [lambda j: j >= LATE_DESTS - 1]
        return _Comm([blocks], [jax.ShapeDtypeStruct(blocks.shape, bf16)], plan)

    def late(self, blocks, dgt):
        return _LateComm(blocks, dgt)


def _local_step(x, ctx, tgt, mod, modc, norm_g, w_pad, ln_g, ln_b, ws, bs, w2, gb, gb_norm, gf, xch):
    shift, scale, gate = mod[:, 0:D], mod[:, D:2 * D], mod[:, 2 * D:]
    shift_c, scale_c = modc[:, 0:D], modc[:, D:]
    w2f, w2b, gbf, gbb = _pad_gate(w2, gb)

    p, h, *got_proj = _in_proj(x, norm_g, scale, shift, w_pad, 512, xch.gather_proj())
    w_pa, w_pb, w_out = xch.proj_weights(got_proj)
    sc_f, sc_b = _ctx_fwd(ctx, norm_g, scale_c, shift_c, w_pad, w2f, w2b, gbf, gbb)
    o_f, st_f = _gla_fwd(p, w2f, gbf, sc_f, False, 512, "gla_fwd_f")
    o_b, st_b = _gla_fwd(p, w2b, gbb, sc_b, True, 512, "gla_fwd_b")
    sv = _mix_fwd(p, ln_g, ln_b, ws.astype(bf16), bs.T)
    (dx1, dzbua, dzag, dsv, do, dw_out, dw_pa, dw_pb, dgf, dgate, dgbn, loss) = _mid(
        x, tgt, p, o_f, o_b, sv, gate, gf, gb_norm, w_pa, w_pb, w_out, 256)
    dva, dws, dbs_acc, dln_g, dln_b = _mix_bwd(p, dsv, ln_g, ln_b, jnp.swapaxes(ws, 1, 2).astype(bf16))
    dw_zbua, dw_va, dw_zag, *got_first = _tn_matmul(h, [dzbua, dva, dzag], 1024, "dw_early", xch.first(dw_out, dw_pa, dw_pb))
    blocks_a = _pack_dw_early(dw_zbua, dw_va, dw_zag, 0, EARLY_A_ROWS, "pack_dw_early_a")
    blocks_b = _pack_dw_early(dw_zbua, dw_va, dw_zag, EARLY_A_ROWS, D - EARLY_A_ROWS, "pack_dw_early_b")
    small = _rows128(dln_g, dln_b, dws, jnp.sum(dbs_acc, axis=-1), dgbn, dgf, jnp.broadcast_to(loss, (1, 128)))

    dqkv_f, dlr_f, dw2f, dgbf, dsc_f, *got_a = _gla_bwd(p, do, st_f, w2f, gbf, None, False, 512, "gla_bwd_f",
                                                        xch.early_a(small, blocks_a))
    dqkv, dlr, dw2b, dgbb, dsc_b, *got_b = _gla_bwd(p, do, st_b, w2b, gbb, (dqkv_f, dlr_f), True, 512, "gla_bwd_b",
                                                    xch.early_b(blocks_b))
    dwk_c, dwv_c, dwl_c, dmodc, dg_c, dw2c, dgbc = _ctx_bwd(ctx, norm_g, scale_c, shift_c, w_pad, w2f, w2b, gbf, gbb, dsc_f, dsc_b)
    dw_qkv, dw_lr = _tn_matmul(h, [dqkv, dlr], 1024, "dw_qkv_lr")
    blocks_late = _pack_dw_late(dw_qkv, dw_lr, dwk_c, dwv_c, dwl_c)
    dw2 = jnp.stack([dw2f[0:RANK] + dw2c[0, 0:RANK], dw2b[RANK:2 * RANK] + dw2c[1, RANK:2 * RANK]])
    dgb = jnp.concatenate([dgbf + dgbc[0], dgbb + dgbc[1]], axis=0)
    dgt = jnp.concatenate([jnp.transpose(dw2.reshape(2, RANK, NDEV, 32), (2, 0, 1, 3)).reshape(NDEV, 2 * RANK * 32),
                           jnp.transpose(dgb.reshape(2, NDEV, 32), (1, 0, 2)).reshape(NDEV, 64),
                           jnp.zeros((NDEV, 64), f32)], axis=1).reshape(NDEV, 9, 128)
    gx, dshift, dscale, dg, *got_late = _in_bwd(x, dx1, dqkv, dzbua, dva, dzag, dlr, w_pad, norm_g, scale, 512,
                                                xch.late(blocks_late, dgt))
    return dict(loss=loss, gx=gx, dmod=jnp.concatenate([dshift, dscale, dgate], axis=1), dmodc=dmodc, dnorm_g=dg + dg_c,
                small=small, blocks_a=blocks_a, blocks_b=blocks_b, blocks_late=blocks_late, dw2=dw2, dgb=dgb,
                dw_pa=dw_pa, dw_pb=dw_pb, dw_out=dw_out, got_first=got_first, got_a=got_a, got_b=got_b, got_late=got_late)


def _rows128(*vs):
    out = []
    for t in vs:
        t = t.reshape(-1)
        pad = (-t.shape[0]) % 128
        out.append(jnp.pad(t, (0, pad)) if pad else t)
    return jnp.concatenate(out).reshape(-1, 128)


def kernel(x, c, ctx, c_ctx, w_mod, b_mod, norm_g, w_in, a_ln_g, a_ln_b, a_ws, a_bs, b_gate_w2, b_gate_b, b_norm_g, w_proj_a, w_proj_b, w_out, final_norm_g, loss_target, m_c_ctx, m_w_mod, m_b_mod, m_norm_g, m_w_in, m_a_ln_g, m_a_ln_b, m_a_ws, m_a_bs, m_b_gate_w2, m_b_gate_b, m_b_norm_g, m_w_proj_a, m_w_proj_b, m_w_out, m_final_norm_g, v_c_ctx, v_w_mod, v_b_mod, v_norm_g, v_w_in, v_a_ln_g, v_a_ln_b, v_a_ws, v_a_bs, v_b_gate_w2, v_b_gate_b, v_b_norm_g, v_w_proj_a, v_w_proj_b, v_w_out, v_final_norm_g):
    me = _me()
    ncol = w_mod.shape[2]

    gate_mine = _rows128(jnp.concatenate([b_gate_w2.reshape(-1), b_gate_b.reshape(-1)]))
    bm_mine = lax.dynamic_slice(b_mod, (0, me * ncol), (1, ncol))
    cs, mods, wg, gates = _gather_first(c, c_ctx.reshape(1, D), w_mod[0], bm_mine, w_in[0].astype(bf16), gate_mine)
    w_pad = _repack_w(wg)
    gflat = gates.reshape(NDEV, 9 * 128)
    w2 = jnp.transpose(gflat[:, 0:2 * RANK * 32].reshape(NDEV, 2, RANK, 32), (1, 2, 0, 3)).reshape(2, RANK, KW)
    gb = jnp.transpose(gflat[:, 2 * RANK * 32:2 * RANK * 32 + 64].reshape(NDEV, 2, 32), (1, 0, 2)).reshape(2, KW)

    mods = jnp.transpose(mods, (1, 0, 2)).reshape(16, 3 * D)
    mod = lax.dynamic_slice(mods, (me, 0), (1, 3 * D))
    modc = mods[8:9, 0:2 * D]

    xch = _Exchanges(w_proj_a[0].astype(bf16), w_proj_b[0].astype(bf16), w_out[0].astype(bf16))
    r = _local_step(x[0], ctx[0], loss_target[0], mod, modc, norm_g, w_pad, a_ln_g, a_ln_b, a_ws[0], a_bs[0], w2, gb,
                    b_norm_g, final_norm_g.reshape(1, D), xch)
    p_out, p_pa, p_pb = r["got_first"]
    smalls_e, p_in_a = r["got_a"]
    (p_in_b,) = r["got_b"]
    _, _, p_in_late, p_gt = r["got_late"]

    n_e = (AW + AW + 4 * ACH * ACH + AW + VW + D) // 128
    row = lambda t: t.reshape(1, D)
    rep_e = _adam_params(smalls_e, [a_ln_g, a_ln_b, a_ws, a_bs, b_norm_g, row(final_norm_g)],
                         [m_a_ln_g, m_a_ln_b, m_a_ws, m_a_bs, m_b_norm_g, row(m_final_norm_g)],
                         [v_a_ln_g, v_a_ln_b, v_a_ws, v_a_bs, v_b_norm_g, row(v_final_norm_g)], "adam_rep_early")
    rep_e = [t[0:5] + (t[5].reshape(D),) for t in rep_e]
    losses = smalls_e[:, n_e, 0]
    loss = losses[0]
    for i in range(1, NDEV):
        loss = loss + losses[i]

    dbm = r["dmod"] + jnp.concatenate([r["dmodc"], jnp.zeros((1, D), f32)], axis=1)
    smalls_l = _all_gather(_rows128(r["dnorm_g"], dbm, r["dmod"], r["dmodc"]), "gather_small")
    n_l = (D + 3 * D) // 128
    rep_l = _adam_params(smalls_l, [norm_g, b_mod], [m_norm_g, m_b_mod], [v_norm_g, v_b_mod], "adam_rep_late")
    tail = smalls_l[:, n_l:].reshape(NDEV, -1)
    dmods = tail[:, 0:3 * D]
    dmodc_all = tail[:, 3 * D:5 * D]
    dmodc_tot = dmodc_all[0:1]
    for i in range(1, NDEV):
        dmodc_tot = dmodc_tot + dmodc_all[i:i + 1]

    dm_rows = jnp.concatenate([dmods, jnp.concatenate([dmodc_tot, jnp.zeros((1, D), f32)], axis=1), jnp.zeros((7, 3 * D), f32)], axis=0)
    dm_mine = lax.dynamic_slice(dm_rows, (0, me * ncol), (16, ncol))
    *wm, gcs = _mod_bwd(cs, dm_mine, w_mod, m_w_mod, v_w_mod)
    cc = _adam(gcs.reshape(NDEV, 8, 128), c_ctx.reshape(8, 128), m_c_ctx.reshape(8, 128), v_c_ctx.reshape(8, 128), 8, "adam_cctx")

    a_in = _adam_w_in(p_in_a, p_in_b, p_in_late, w_in, m_w_in, v_w_in)
    a_pa = _adam(p_pa, w_proj_a, m_w_proj_a, v_w_proj_a, AW, "adam_w_pa")
    a_pb = _adam(p_pb, w_proj_b, m_w_proj_b, v_w_proj_b, VW, "adam_w_pb")
    a_out = _adam(p_out, w_out, m_w_out, v_w_out, 128, "adam_w_out")
    gate_m = _rows128(jnp.concatenate([m_b_gate_w2.reshape(-1), m_b_gate_b.reshape(-1)]))
    gate_v = _rows128(jnp.concatenate([v_b_gate_w2.reshape(-1), v_b_gate_b.reshape(-1)]))
    a_gt = [t.reshape(-1) for t in _adam(p_gt, gate_mine, gate_m, gate_v, 9, "adam_gate")]
    nw2 = 2 * RANK * 32
    sh = [(a_in[k], a_pa[k], a_pb[k], a_out[k], a_gt[k][0:nw2].reshape(1, 2, RANK, 32),
           a_gt[k][nw2:nw2 + 64].reshape(1, 2, 32)) for k in range(4)]

    outs = [loss, r["gx"][None]]
    for k in range(4):
        lg, lb, aws, abs_, bng, fng = rep_e[k]
        n_g, bmod = rep_l[k]
        s_in, s_pa, s_pb, s_out, s_w2, s_gb = sh[k]
        outs += [cc[k].reshape(D), wm[k], bmod, n_g, s_in, lg, lb, aws, abs_, s_w2, s_gb, bng, s_pa, s_pb, s_out, fng]
    return tuple(outs)
```

```python
import jax
import jax.numpy as jnp
from jax import lax
from jax.experimental import pallas as pl
from jax.experimental.pallas import tpu as pltpu

f32, bf16 = jnp.float32, jnp.bfloat16

D = 1024
CTX = 256
EPS = 1e-6
AW = 512
ACH = 128
GW = 64
KW = 256
VW = 512
NH = 4
HK = 64
HV = 128
RANK = 16
TAU = 16.0
CH = 64
QSCALE = HK ** -0.5
INW = 5152
NDEV = 8

PQ, PK, PV, PZB, PUA, PVA, PZA, PG1, PG2, PLR, PW = 0, 256, 512, 1024, 1536, 2048, 2560, 3072, 4096, 5120, 5248
LRW = 128

ADAM_LR, ADAM_B1, ADAM_B2, ADAM_EPS, ADAM_WD, ADAM_STEP = 0.001, 0.9, 0.999, 1e-08, 0.01, 10

VMEM_LIMIT = 56 * 1024 * 1024
MESH = pl.DeviceIdType.MESH


def _cp(sem=None):
    return pltpu.CompilerParams(dimension_semantics=sem, vmem_limit_bytes=VMEM_LIMIT)


def _dot(a, b):
    return jnp.dot(a.astype(bf16), b.astype(bf16), preferred_element_type=f32)


def _nt(a, b):
    return lax.dot_general(a.astype(bf16), b.astype(bf16), (((1,), (1,)), ((), ())), preferred_element_type=f32)


def _tn(a, b):
    return lax.dot_general(a.astype(bf16), b.astype(bf16), (((0,), (0,)), ((), ())), preferred_element_type=f32)


def _dot_hi(a, b):
    return jnp.dot(a, b, preferred_element_type=f32, precision=lax.Precision.HIGHEST)


def _sigmoid(x):
    return 1.0 / (1.0 + jnp.exp(-x))


def _log_sigmoid(x):
    return jnp.minimum(x, 0.0) - jnp.log(1.0 + jnp.exp(-jnp.abs(x)))


def _silu_and_grad(z):
    s = _sigmoid(z)
    return z * s, s * (1.0 + z * (1.0 - s))


def _me():
    return 4 * lax.axis_index("x") + 2 * lax.axis_index("y") + lax.axis_index("c")


def _peer(k):
    x, y, c = lax.axis_index("x"), lax.axis_index("y"), lax.axis_index("c")
    px = 1 - x if k & 4 else x
    py = 1 - y if k & 2 else y
    pc = 1 - c if k & 1 else c
    return (px, py, pc), 4 * px + 2 * py + pc


def _fanout(srcs, dsts, send_sems, recv_sems, local_sems, owners=None):
    me = _me()
    n = len(srcs)
    owns = lambda a, j: True if owners is None or owners[a] is None else owners[a](j)

    def guarded(cond, fn):
        if cond is True:
            fn()
        else:
            pl.when(cond)(fn)

    def copies(with_recvs):
        local = [pltpu.make_async_copy(srcs[a](me), dsts[a](me), local_sems.at[a]) for a in range(n)]
        sends, recvs = [], []
        for k in range(1, NDEV):
            dev, idx = _peer(k)
            for a in range(n):
                s = (k - 1) * n + a
                sends.append((owns(a, idx), pltpu.make_async_remote_copy(
                    src_ref=srcs[a](idx), dst_ref=dsts[a](me), send_sem=send_sems.at[s], recv_sem=recv_sems.at[s],
                    device_id=dev, device_id_type=MESH)))
                if with_recvs:
                    recvs.append((owns(a, me), pltpu.make_async_remote_copy(
                        src_ref=srcs[a](idx), dst_ref=dsts[a](idx), send_sem=send_sems.at[s], recv_sem=recv_sems.at[s],
                        device_id=dev, device_id_type=MESH)))
        return local, sends, recvs

    def start():
        local, sends, _ = copies(False)
        for a, cp in enumerate(local):
            guarded(owns(a, me), cp.start)
        for cond, cp in sends:
            guarded(cond, cp.start)

    def finish():
        local, sends, recvs = copies(True)
        for cond, cp in recvs:
            guarded(cond, cp.wait_recv)
        for cond, cp in sends:
            guarded(cond, cp.wait_send)
        for a, cp in enumerate(local):
            guarded(owns(a, me), cp.wait)

    return start, finish


class _Comm:
    def __init__(self, ins, outs, plan):
        self.ins, self.outs, self.plan = list(ins), list(outs), plan
        self.n = len(self.outs)

    def specs(self):
        hbm = pl.BlockSpec(memory_space=pl.ANY)
        return [hbm] * len(self.ins), [hbm] * self.n, _fanout_sems(self.n) if self.n else []

    def run(self, in_refs, out_refs, sems, nsteps, compute):
        if not self.n:
            compute()
            return

        def hooks():
            srcs, dsts, owners = self.plan(in_refs, out_refs)
            return _fanout(srcs, dsts, sems[0], sems[1], sems[2], owners)

        pl.when(pl.program_id(0) == 0)(lambda: hooks()[0]())
        compute()
        pl.when(pl.program_id(0) == nsteps - 1)(lambda: hooks()[1]())


LATE_MID_STEP = 1


class _LateComm:
    def __init__(self, blocks, dgt):
        sds = jax.ShapeDtypeStruct
        self.ins = [blocks, dgt]
        self.outs = [sds((D, SHARD), bf16), sds((D, SHARD), bf16), sds((4, D, SHARD), bf16), sds(dgt.shape, f32)]
        self.n = len(self.outs)

    def specs(self):
        hbm = pl.BlockSpec(memory_space=pl.ANY)
        scratch = [pltpu.VMEM((3, D, SHARD), bf16), pltpu.SemaphoreType.DMA((2,)), pltpu.SemaphoreType.DMA((4,)),
                   pltpu.SemaphoreType.DMA((3,))] + _fanout_sems(1)
        return [hbm] * 2, [hbm] * self.n, scratch

    def run(self, in_refs, out_refs, scratch, nsteps, compute):
        late_ref, dgt_ref = in_refs
        sib_ref, pair_ref, parts_ref, gt_ref = out_refs
        vbuf, send_sems, recv_sems, local_sems, g_send, g_recv, g_local = scratch
        x, y, c = lax.axis_index("x"), lax.axis_index("y"), lax.axis_index("c")
        chip = 2 * x + y
        is_owner_chip = chip == 0
        step = pl.program_id(0)

        def to_sibling():
            return pltpu.make_async_remote_copy(src_ref=late_ref.at[1 - c], dst_ref=sib_ref, send_sem=send_sems.at[0],
                                                recv_sem=recv_sems.at[0], device_id=(x, y, 1 - c), device_id_type=MESH)

        def to_owner(k):
            return pltpu.make_async_remote_copy(src_ref=pair_ref, dst_ref=parts_ref.at[k], send_sem=send_sems.at[1],
                                                recv_sem=recv_sems.at[k], device_id=(0, 0, c), device_id_type=MESH)

        def own_copy():
            return pltpu.make_async_copy(pair_ref, parts_ref.at[0], local_sems.at[2])

        def gates():
            return _fanout([lambda j: dgt_ref.at[j]], [lambda j: gt_ref.at[j]], g_send, g_recv, g_local)

        @pl.when(step == 0)
        def _():
            to_sibling().start()
            gates()[0]()

        compute()

        @pl.when(step == LATE_MID_STEP)
        def _():
            mine = pltpu.make_async_copy(late_ref.at[c], vbuf.at[0], local_sems.at[0])
            mine.start()
            to_sibling().wait_recv()
            theirs = pltpu.make_async_copy(sib_ref, vbuf.at[1], local_sems.at[1])
            theirs.start()
            mine.wait()
            theirs.wait()
            vbuf[2] = (vbuf[0].astype(f32) + vbuf[1].astype(f32)).astype(bf16)
            pltpu.sync_copy(vbuf.at[2], pair_ref)
            pl.when(is_owner_chip)(lambda: own_copy().start())
            pl.when(jnp.logical_not(is_owner_chip))(lambda: to_owner(chip).start())

        @pl.when(step == nsteps - 1)
        def _():
            @pl.when(is_owner_chip)
            def _():
                for k in range(1, 4):
                    to_owner(k).wait_recv()
                own_copy().wait()

            pl.when(jnp.logical_not(is_owner_chip))(lambda: to_owner(chip).wait_send())
            to_sibling().wait_send()
            gates()[1]()


def _fanout_sems(n):
    return [pltpu.SemaphoreType.DMA(((NDEV - 1) * n,)), pltpu.SemaphoreType.DMA(((NDEV - 1) * n,)), pltpu.SemaphoreType.DMA((n,))]


def _lanes(j):
    return pl.ds(pl.multiple_of(j * 128, 128), 128)


def _gather_first(c_row, cctx_row, wm, bm, wi, gate):
    def body(c_ref, cctx_ref, wm_ref, bm_ref, wi_ref, g_ref, cs_ref, mods_ref, owi, og, call_ref, mine_ref,
             send_sems, recv_sems, local_sems, c_send, c_recv, c_local, m_send, m_recv, m_local):
        x, y, c = lax.axis_index("x"), lax.axis_index("y"), lax.axis_index("c")
        sibling = (x, y, 1 - c)
        chips = [(1 - x, y), (x, 1 - y), (1 - x, 1 - y)]
        index = lambda px, py, pc: 4 * px + 2 * py + pc
        arrays = ((wi_ref, owi), (g_ref, og))
        n = len(arrays)

        def copy(a, k, block, to, own=False):
            src, out = arrays[a]
            return pltpu.make_async_remote_copy(
                src_ref=src if own else out.at[index(*block)], dst_ref=out.at[index(*block)],
                send_sem=send_sems.at[k * n + a], recv_sem=recv_sems.at[k * n + a], device_id=to, device_id_type=MESH)

        c_start, c_finish = _fanout([lambda j: c_ref], [lambda j: call_ref.at[j]], c_send, c_recv, c_local)
        c_start()
        mine = [pltpu.make_async_copy(src, out.at[index(x, y, c)], local_sems.at[a]) for a, (src, out) in enumerate(arrays)]
        first = [copy(a, 0, (x, y, c), sibling, own=True) for a in range(n)]
        first += [copy(a, 1 + j, (x, y, c), (*chip, c), own=True) for j, chip in enumerate(chips) for a in range(n)]
        for cp in mine + first:
            cp.start()

        c_finish()
        cs = jnp.concatenate([call_ref[j] for j in range(NDEV)] + [cctx_ref[...], jnp.zeros((16 - NDEV - 1, D), f32)], axis=0)
        cs_ref[...] = cs
        s, _ = _silu_and_grad(cs)
        mine_ref[...] = _dot_hi(s, wm_ref[...]) + bm_ref[...]
        m_start, m_finish = _fanout([lambda j: mine_ref], [lambda j: mods_ref.at[j]], m_send, m_recv, m_local)
        m_start()

        passed = []
        for j, chip in enumerate(chips):
            for a in range(n):
                copy(a, 1 + j, (*chip, c), (x, y, c)).wait_recv()
            for a in range(n):
                cp = copy(a, 4 + j, (*chip, c), sibling)
                cp.start()
                passed.append(cp)
        for a in range(n):
            copy(a, 0, sibling, (x, y, c)).wait_recv()
        for j, chip in enumerate(chips):
            for a in range(n):
                copy(a, 4 + j, (*chip, 1 - c), (x, y, c)).wait_recv()
        for cp in first + passed:
            cp.wait_send()
        for cp in mine:
            cp.wait()
        m_finish()

    hbm = pl.BlockSpec(memory_space=pl.ANY)
    vm = pl.BlockSpec(memory_space=pltpu.VMEM)
    ncol = wm.shape[1]
    return pl.pallas_call(
        body, name="gather_first",
        out_shape=(jax.ShapeDtypeStruct((16, D), f32), jax.ShapeDtypeStruct((NDEV, 16, ncol), f32),
                   jax.ShapeDtypeStruct((NDEV,) + wi.shape, bf16), jax.ShapeDtypeStruct((NDEV,) + gate.shape, f32)),
        in_specs=[vm, vm, vm, vm, hbm, hbm], out_specs=(vm, vm, hbm, hbm),
        scratch_shapes=[pltpu.VMEM((NDEV, 1, D), f32), pltpu.VMEM((16, ncol), f32)] + _fanout_sems(2) + _fanout_sems(1) + _fanout_sems(1),
        compiler_params=_cp(),
    )(c_row, cctx_row, wm, bm, wi, gate)


SHARD = INW // NDEV
ROWS_RP = 128


def _overlap(lo, hi, a, b):
    s, e = max(lo, a), min(hi, b)
    return (s, e) if s < e else None


def _repack_w(wg):
    segs = ((0, 1024, PQ), (1024, 1024 + 2 * RANK, PLR), (1024 + 2 * RANK, INW, PZB))

    def body(g_ref, o_ref):
        for j in range(NDEV):
            lo, hi = j * SHARD, (j + 1) * SHARD
            for a, b, pad0 in segs:
                ov = _overlap(lo, hi, a, b)
                if ov:
                    s, e = ov
                    o_ref[:, pad0 + s - a:pad0 + e - a] = g_ref[j, :, s - lo:e - lo]
        o_ref[:, PLR + 2 * RANK:PW] = jnp.zeros((ROWS_RP, PW - PLR - 2 * RANK), bf16)

    return pl.pallas_call(
        body, name="repack_w", grid=(D // ROWS_RP,), out_shape=jax.ShapeDtypeStruct((D, PW), bf16),
        in_specs=[pl.BlockSpec((NDEV, ROWS_RP, SHARD), lambda i: (0, i, 0))],
        out_specs=pl.BlockSpec((ROWS_RP, PW), lambda i: (i, 0)), compiler_params=_cp(("arbitrary",)),
    )(wg)


LATE_END = 1024 + 2 * RANK
LATE_DESTS = 2


def _pack_blocks(o_ref, srcs, dests, dtype):
    for n, j in enumerate(dests):
        lo, hi = j * SHARD, (j + 1) * SHARD
        done = lo
        for a, b, src in srcs:
            ov = _overlap(lo, hi, a, b)
            if ov:
                s, e = ov
                if s > done:
                    o_ref[n, :, done - lo:s - lo] = jnp.zeros((ROWS_RP, s - done), dtype)
                o_ref[n, :, s - lo:e - lo] = src[:, s - a:e - a].astype(dtype)
                done = e
        if done < hi:
            o_ref[n, :, done - lo:hi - lo] = jnp.zeros((ROWS_RP, hi - done), dtype)


def _pack_dw_early(dw_zbua, dw_va, dw_zag, row0, nrows, name):
    def body(zbua_ref, va_ref, zag_ref, o_ref):
        _pack_blocks(o_ref, ((LATE_END, 2080, zbua_ref), (2080, 2592, va_ref), (2592, INW, zag_ref)), range(NDEV), bf16)

    row = lambda w: pl.BlockSpec((ROWS_RP, w), lambda i: (i + row0 // ROWS_RP, 0))
    return pl.pallas_call(
        body, name=name, grid=(nrows // ROWS_RP,), out_shape=jax.ShapeDtypeStruct((NDEV, nrows, SHARD), bf16),
        in_specs=[row(2 * AW), row(AW), row(AW + 2 * D)],
        out_specs=pl.BlockSpec((NDEV, ROWS_RP, SHARD), lambda i: (0, i, 0)), compiler_params=_cp(("arbitrary",)),
    )(dw_zbua, dw_va, dw_zag)


def _pack_dw_late(dw_qkv, dw_lr, dwk_c, dwv_c, dwl_c):
    def body(qkv_ref, lr_ref, kc_ref, vc_ref, lc_ref, o_ref):
        qkv = qkv_ref[...] + jnp.concatenate([jnp.zeros((ROWS_RP, KW), f32), kc_ref[...], vc_ref[...]], axis=1)
        lr = lr_ref[...] + lc_ref[...]
        _pack_blocks(o_ref, ((0, 1024, qkv), (1024, LATE_END, lr)), range(LATE_DESTS), bf16)

    row = lambda w: pl.BlockSpec((ROWS_RP, w), lambda i: (i, 0))
    return pl.pallas_call(
        body, name="pack_dw_late", grid=(D // ROWS_RP,), out_shape=jax.ShapeDtypeStruct((LATE_DESTS, D, SHARD), bf16),
        in_specs=[row(2 * KW + VW), row(LRW), row(KW), row(VW), row(LRW)],
        out_specs=pl.BlockSpec((LATE_DESTS, ROWS_RP, SHARD), lambda i: (0, i, 0)), compiler_params=_cp(("arbitrary",)),
    )(dw_qkv, dw_lr, dwk_c, dwv_c, dwl_c)


def _mod_tail(dnorm, dbm, dmod, dmodc, cs, wm, m_wm, v_wm, norm_g, m_ng, v_ng, b_mod, m_bm, v_bm, c_ctx, m_cc, v_cc):
    ncol = wm.shape[2]

    def body(dn_ref, dbm_ref, dmod_ref, dmodc_ref, cs_ref, wm_ref, mwm_ref, vwm_ref, ng_ref, mng_ref, vng_ref,
             bm_ref, mbm_ref, vbm_ref, cc_ref, mcc_ref, vcc_ref, *rest):
        o_wm, o_ng, o_bm, o_cc = rest[0:4], rest[4:8], rest[8:12], rest[12:16]
        a_dn, a_dbm, a_dmod, a_dmodc, dm_rows, gc_ref, a_gc = rest[16:23]
        s1, r1, l1, s2, r2, l2 = rest[23:29]
        start, finish = _fanout([lambda j: dn_ref, lambda j: dbm_ref, lambda j: dmod_ref, lambda j: dmodc_ref],
                                [lambda j, r=r: r.at[j] for r in (a_dn, a_dbm, a_dmod, a_dmodc)], s1, r1, l1)
        start()
        finish()

        def total(ref):
            t = ref[0]
            for j in range(1, NDEV):
                t = t + ref[j]
            return t

        _adam_update(total(a_dn), ng_ref, mng_ref, vng_ref, *o_ng)
        _adam_update(total(a_dbm), bm_ref, mbm_ref, vbm_ref, *o_bm)
        dmodc_tot = jnp.concatenate([total(a_dmodc), jnp.zeros((1, D), f32)], axis=1)
        dm_rows[...] = jnp.concatenate([a_dmod[j] for j in range(NDEV)] + [dmodc_tot, jnp.zeros((16 - NDEV - 1, 3 * D), f32)], axis=0)
        dm = dm_rows[:, pl.ds(pl.multiple_of(_me() * ncol, 128), ncol)]
        s, ds = _silu_and_grad(cs_ref[...])
        part = lax.dot_general(dm[8:9, :], wm_ref[0], (((1,), (1,)), ((), ())), preferred_element_type=f32,
                               precision=lax.Precision.HIGHEST)
        gc_ref[...] = part * ds[8:9, :]
        start2, finish2 = _fanout([lambda j: gc_ref], [lambda j: a_gc.at[j]], s2, r2, l2)
        start2()
        g = lax.dot_general(s, dm, (((0,), (0,)), ((), ())), preferred_element_type=f32, precision=lax.Precision.HIGHEST)
        _adam_update(g[None], wm_ref, mwm_ref, vwm_ref, *o_wm)
        finish2()
        _adam_update(total(a_gc), cc_ref, mcc_ref, vcc_ref, *o_cc)

    like = lambda a: [jax.ShapeDtypeStruct(a.shape, f32)] * 4
    row = lambda n: pltpu.VMEM((NDEV, 1, n), f32)
    res = pl.pallas_call(
        body, name="mod_tail", out_shape=tuple(like(wm) + like(norm_g) + like(b_mod) + like(c_ctx)),
        scratch_shapes=[row(D), row(3 * D), row(3 * D), row(2 * D), pltpu.VMEM((16, 3 * D), f32), pltpu.VMEM((1, D), f32), row(D)]
        + _fanout_sems(4) + _fanout_sems(1),
        compiler_params=_cp(),
    )(dnorm, dbm, dmod, dmodc, cs, wm, m_wm, v_wm, norm_g, m_ng, v_ng, b_mod, m_bm, v_bm, c_ctx, m_cc, v_cc)
    return res[0:4], res[4:8], res[8:12], res[12:16]


def _adam_update(g, w_ref, m_ref, v_ref, go_ref, d_ref, mo_ref, vo_ref):
    c1 = 1.0 / (1.0 - ADAM_B1 ** ADAM_STEP)
    c2 = 1.0 / (1.0 - ADAM_B2 ** ADAM_STEP)
    mn = ADAM_B1 * m_ref[...] + (1.0 - ADAM_B1) * g
    vn = ADAM_B2 * v_ref[...] + (1.0 - ADAM_B2) * (g * g)
    go_ref[...] = g
    mo_ref[...] = mn
    vo_ref[...] = vn
    d_ref[...] = -ADAM_LR * ((mn * c1) / (jnp.sqrt(vn * c2) + ADAM_EPS) + ADAM_WD * w_ref[...])


def _adam_w_in(parts_a, parts_b, parts_late, w, m, v):
    na = EARLY_A_ROWS // ROWS_RP

    def body(a_ref, b_ref, l_ref, w_ref, m_ref, v_ref, go_ref, d_ref, mo_ref, vo_ref):
        me = _me()
        first = pl.program_id(0) < na
        early = jnp.where(first, a_ref[0], b_ref[0]).astype(f32)
        for i in range(1, NDEV):
            early = early + jnp.where(first, a_ref[i], b_ref[i]).astype(f32)
        late = l_ref[0].astype(f32)
        for i in range(1, 4):
            late = late + l_ref[i].astype(f32)
        g = jnp.where(me >= LATE_DESTS - 1, early, 0.0) + jnp.where(me < LATE_DESTS, late, 0.0)
        _adam_update(g, w_ref, m_ref, v_ref, go_ref, d_ref, mo_ref, vo_ref)

    blk = pl.BlockSpec((None, ROWS_RP, SHARD), lambda i: (0, i, 0))
    return pl.pallas_call(
        body, name="adam_w_in", grid=(D // ROWS_RP,), out_shape=tuple(jax.ShapeDtypeStruct((1, D, SHARD), f32) for _ in range(4)),
        in_specs=[pl.BlockSpec((NDEV, ROWS_RP, SHARD), lambda i: (0, jnp.minimum(i, na - 1), 0)),
                  pl.BlockSpec((NDEV, ROWS_RP, SHARD), lambda i: (0, jnp.maximum(i - na, 0), 0)),
                  pl.BlockSpec((4, ROWS_RP, SHARD), lambda i: (0, i, 0)), blk, blk, blk],
        out_specs=(blk, blk, blk, blk), compiler_params=_cp(("arbitrary",)),
    )(parts_a, parts_b, parts_late, w, m, v)


def _adam_params(parts, ws, ms, vs, name):
    p = parts.shape[0]
    k = len(ws)
    nrows = [w.size // 128 for w in ws]
    starts = [sum(nrows[:i]) for i in range(k)]

    def shaped(g, shape):
        if len(shape) == 2:
            return jnp.concatenate([g[r:r + 1] for r in range(g.shape[0])], axis=1)
        return g.reshape(shape)

    def body(*refs):
        g_ref = refs[0]
        w_refs, m_refs, v_refs = refs[1:1 + k], refs[1 + k:1 + 2 * k], refs[1 + 2 * k:1 + 3 * k]
        outs = refs[1 + 3 * k:]
        for i in range(k):
            rows = slice(starts[i], starts[i] + nrows[i])
            g = g_ref[0, rows, :]
            for j in range(1, p):
                g = g + g_ref[j, rows, :]
            _adam_update(shaped(g, ws[i].shape), w_refs[i], m_refs[i], v_refs[i], outs[i], outs[k + i], outs[2 * k + i], outs[3 * k + i])

    vm = pl.BlockSpec(memory_space=pltpu.VMEM)
    res = pl.pallas_call(
        body, name=name, out_shape=tuple(jax.ShapeDtypeStruct(w.shape, f32) for _ in range(4) for w in ws),
        in_specs=[vm] * (1 + 3 * k), out_specs=(vm,) * (4 * k), compiler_params=_cp(),
    )(parts, *ws, *ms, *vs)
    return [res[i * k:(i + 1) * k] for i in range(4)]


def _adam_blocks(parts, ws, ms, vs, name):
    k = len(ws)

    def body(*refs):
        g_refs, w_refs, m_refs, v_refs = refs[0:k], refs[k:2 * k], refs[2 * k:3 * k], refs[3 * k:4 * k]
        outs = refs[4 * k:]
        for i in range(k):
            g = g_refs[i][0].astype(f32)
            for j in range(1, parts[i].shape[0]):
                g = g + g_refs[i][j].astype(f32)
            _adam_update(g[None], w_refs[i], m_refs[i], v_refs[i], outs[i], outs[k + i], outs[2 * k + i], outs[3 * k + i])

    vm = pl.BlockSpec(memory_space=pltpu.VMEM)
    res = pl.pallas_call(
        body, name=name, out_shape=tuple(jax.ShapeDtypeStruct(w.shape, f32) for _ in range(4) for w in ws),
        in_specs=[vm] * (4 * k), out_specs=(vm,) * (4 * k), compiler_params=_cp(),
    )(*parts, *ws, *ms, *vs)
    return [res[i * k:(i + 1) * k] for i in range(4)]


def _adam(parts, w, m, v, rows, name):
    p, r, c = parts.shape
    lead = w.ndim - 2

    def body(g_ref, w_ref, m_ref, v_ref, go_ref, d_ref, mo_ref, vo_ref):
        g = g_ref[0].astype(f32)
        for i in range(1, p):
            g = g + g_ref[i].astype(f32)
        _adam_update(g, w_ref, m_ref, v_ref, go_ref, d_ref, mo_ref, vo_ref)

    blk = pl.BlockSpec((None,) * lead + (rows, c), lambda i: (0,) * lead + (i, 0))
    return pl.pallas_call(
        body, name=name, grid=(r // rows,), out_shape=tuple(jax.ShapeDtypeStruct(w.shape, f32) for _ in range(4)),
        in_specs=[pl.BlockSpec((p, rows, c), lambda i: (0, i, 0)), blk, blk, blk], out_specs=(blk, blk, blk, blk),
        compiler_params=_cp(("arbitrary",)),
    )(parts, w, m, v)


def _in_proj(x, g, scale, shift, w_pad, tl, comm):
    l = x.shape[0]
    c_in, c_out, c_sems = comm.specs()

    def body(*refs):
        x_ref, g_ref, sc_ref, sh_ref, w_ref = refs[:5]
        cin = refs[5:5 + len(c_in)]
        p_ref, h_ref = refs[5 + len(c_in):7 + len(c_in)]
        cout = refs[7 + len(c_in):7 + len(c_in) + comm.n]
        sems = refs[7 + len(c_in) + comm.n:]

        def compute():
            xv = x_ref[...]
            r = lax.rsqrt(jnp.mean(xv * xv, axis=-1, keepdims=True) + EPS)
            h = (xv * r) * (g_ref[...] * (1.0 + sc_ref[...])) + sh_ref[...]
            hb = h.astype(bf16)
            h_ref[...] = hb
            p_ref[...] = jnp.dot(hb, w_ref[...], preferred_element_type=f32).astype(bf16)

        comm.run(cin, cout, sems, l // tl, compute)

    vec = pl.BlockSpec((1, D), lambda i: (0, 0))
    return pl.pallas_call(
        body, name="in_proj", grid=(l // tl,),
        out_shape=(jax.ShapeDtypeStruct((l, PW), bf16), jax.ShapeDtypeStruct((l, D), bf16)) + tuple(comm.outs),
        in_specs=[pl.BlockSpec((tl, D), lambda i: (i, 0)), vec, vec, vec, pl.BlockSpec((D, PW), lambda i: (0, 0))] + c_in,
        out_specs=(pl.BlockSpec((tl, PW), lambda i: (i, 0)), pl.BlockSpec((tl, D), lambda i: (i, 0))) + tuple(c_out),
        scratch_shapes=c_sems, compiler_params=_cp(("arbitrary",)),
    )(x, g, scale, shift, w_pad, *comm.ins)


def _tri(rev):
    i = lax.broadcasted_iota(jnp.int32, (CH, CH), 0)
    j = lax.broadcasted_iota(jnp.int32, (CH, CH), 1)
    return jnp.where((j >= i) if rev else (j <= i), 1.0, 0.0).astype(f32)


def _head_masks():
    lane = lax.broadcasted_iota(jnp.int32, (1, KW), 1) // HK
    return [jnp.where(lane == h, 1.0, 0.0).astype(f32) for h in range(NH)]


def _block_diag():
    r = lax.broadcasted_iota(jnp.int32, (VW, KW), 0) // HV
    c = lax.broadcasted_iota(jnp.int32, (VW, KW), 1) // HK
    return jnp.where(r == c, 1.0, 0.0).astype(f32)


def _decay(lr, w2, gb, tri, rev):
    logits = _dot(lr, w2) + gb
    a = _log_sigmoid(logits) * (1.0 / TAU)
    c = _dot_hi(tri, a)
    cl = c[0:1, :] if rev else c[CH - 1:CH, :]
    return logits, c, cl


def _state_fwd(k, v, c, cl, st, bd):
    return st * jnp.exp(cl) + bd * _tn(v, k * jnp.exp(cl - c))


def _state_bwd(k, v, c, cl, st0, dst, trit):
    ecl = jnp.exp(cl)
    edec = jnp.exp(cl - c)
    kdec = k * edec
    dv = _nt(kdec, dst)
    dkdec = _dot(v, dst)
    dcl = jnp.sum(dst * st0, axis=0, keepdims=True) * ecl + jnp.sum(dkdec * kdec, axis=0, keepdims=True)
    da = _dot_hi(trit, -dkdec * kdec) + dcl
    return dkdec * edec, dv, da, dst * ecl


def _bdot(a, b):
    return lax.dot_general(a.astype(bf16), b.astype(bf16), (((2,), (1,)), ((0,), (0,))), preferred_element_type=f32)


def _bnt(a, b):
    return lax.dot_general(a.astype(bf16), b.astype(bf16), (((2,), (2,)), ((0,), (0,))), preferred_element_type=f32)


def _btn(a, b):
    return lax.dot_general(a.astype(bf16), b.astype(bf16), (((1,), (1,)), ((0,), (0,))), preferred_element_type=f32)


def _scan_chunks(x, rev):
    nc = x.shape[0]
    hi = x.astype(bf16)
    r1 = x - hi.astype(f32)
    mid = r1.astype(bf16)
    lo = (r1 - mid.astype(f32)).astype(bf16)
    terms = jnp.concatenate([hi, mid, lo], axis=1)
    tri3 = jnp.broadcast_to(jnp.concatenate([_tri(rev)] * 3, axis=1).astype(bf16)[None], (nc, CH, 3 * CH))
    return lax.dot_general(tri3, terms, (((2,), (1,)), ((0,), (0,))), preferred_element_type=f32)


class _Tile:
    pass


def _tile_prep(q_ref, k_ref, lr_ref, w2_ref, gb_ref, rev, nc):
    t = _Tile()
    tg = nc * CH
    t.logits = _dot(lr_ref[...], w2_ref[...]) + gb_ref[...]
    c = _scan_chunks((_log_sigmoid(t.logits) * (1.0 / TAU)).reshape(nc, CH, KW), rev)
    cl = c[:, 0:1, :] if rev else c[:, CH - 1:CH, :]
    k = k_ref[...].astype(f32).reshape(nc, CH, KW)
    t.ec, t.enc, t.edec, t.ecl = jnp.exp(c), jnp.exp(-c), jnp.exp(cl - c), jnp.exp(cl)
    t.qd = q_ref[...].astype(f32).reshape(nc, CH, KW) * t.ec * QSCALE
    t.kd = k * t.enc
    t.kdec = k * t.edec
    hm = _head_masks()
    t.tri4 = jnp.concatenate([_tri(rev)] * NH, axis=0)[None]
    t.qs = jnp.concatenate([t.qd * hm[h] for h in range(NH)], axis=1)
    t.pst = _bnt(t.qs, t.kd) * t.tri4
    return t


def _gla_fwd(p, w2p, gb, s0, rev, tg, name):
    l = p.shape[0]
    nb, nc = l // tg, tg // CH

    def body(q_ref, k_ref, v_ref, lr_ref, w2_ref, gb_ref, s0_ref, o_ref, st_ref, st):
        @pl.when(pl.program_id(0) == 0)
        def _():
            st[...] = s0_ref[...]

        t = _tile_prep(q_ref, k_ref, lr_ref, w2_ref, gb_ref, rev, nc)
        v = v_ref[...].reshape(nc, CH, VW)
        intra = jnp.concatenate([_bdot(t.pst[:, h * CH:(h + 1) * CH], v[:, :, h * HV:(h + 1) * HV]) for h in range(NH)], axis=2)
        kv = _btn(v, t.kdec) * _block_diag()[None]
        s = st[...]
        for n in (range(nc - 1, -1, -1) if rev else range(nc)):
            st_ref[n] = s
            s = s * t.ecl[n] + kv[n]
        st[...] = s
        o_ref[...] = (_bnt(t.qd, st_ref[...]) + intra).reshape(tg, VW)

    blk = (lambda i: nb - 1 - i) if rev else (lambda i: i)
    return pl.pallas_call(
        body, name=name, grid=(nb,),
        out_shape=(jax.ShapeDtypeStruct((l, VW), f32), jax.ShapeDtypeStruct((l // CH, VW, KW), f32)),
        in_specs=[pl.BlockSpec((tg, KW), lambda i: (blk(i), PQ // KW)), pl.BlockSpec((tg, KW), lambda i: (blk(i), PK // KW)),
                  pl.BlockSpec((tg, VW), lambda i: (blk(i), PV // VW)), pl.BlockSpec((tg, LRW), lambda i: (blk(i), PLR // LRW)),
                  pl.BlockSpec((LRW, KW), lambda i: (0, 0)), pl.BlockSpec((1, KW), lambda i: (0, 0)),
                  pl.BlockSpec((VW, KW), lambda i: (0, 0))],
        out_specs=(pl.BlockSpec((tg, VW), lambda i: (blk(i), 0)), pl.BlockSpec((nc, VW, KW), lambda i: (blk(i), 0, 0))),
        scratch_shapes=[pltpu.VMEM((VW, KW), f32)],
        compiler_params=_cp(("arbitrary",)),
    )(p, p, p, p, w2p, gb, s0)


def _gla_bwd(p, do, states, w2p, gb, prev, rev, tg, name, comm):
    l = p.shape[0]
    nb, nc = l // tg, tg // CH
    out_dt = f32 if prev is None else bf16
    c_in, c_out, c_sems = comm.specs()

    def body(*refs):
        q_ref, k_ref, v_ref, lr_ref, do_ref, st_ref, w2_ref, gb_ref = refs[:8]
        refs = refs[8:]
        if prev is not None:
            pq_ref, pl_ref = refs[:2]
            refs = refs[2:]
        cin, refs = refs[:len(c_in)], refs[len(c_in):]
        dqkv_ref, dlr_ref, dw2_ref, dgb_ref, ds0_ref = refs[:5]
        cout, dst, ds_buf, sems = refs[5:5 + comm.n], refs[5 + comm.n], refs[6 + comm.n], refs[7 + comm.n:]
        comm.run(cin, cout, sems, nb, lambda: compute(q_ref, k_ref, v_ref, lr_ref, do_ref, st_ref, w2_ref, gb_ref,
                                                      pq_ref if prev is not None else None, pl_ref if prev is not None else None,
                                                      dqkv_ref, dlr_ref, dw2_ref, dgb_ref, ds0_ref, dst, ds_buf))

    def compute(q_ref, k_ref, v_ref, lr_ref, do_ref, st_ref, w2_ref, gb_ref, pq_ref, pl_ref,
                dqkv_ref, dlr_ref, dw2_ref, dgb_ref, ds0_ref, dst, ds_buf):
        @pl.when(pl.program_id(0) == 0)
        def _():
            dst[...] = jnp.zeros_like(dst)
            dw2_ref[...] = jnp.zeros_like(dw2_ref)
            dgb_ref[...] = jnp.zeros_like(dgb_ref)

        t = _tile_prep(q_ref, k_ref, lr_ref, w2_ref, gb_ref, rev, nc)
        hm = _head_masks()
        v = v_ref[...].reshape(nc, CH, VW)
        do = do_ref[...].reshape(nc, CH, VW)
        heads = lambda a, h: a[:, :, h * HV:(h + 1) * HV]
        dpst = jnp.concatenate([_bnt(heads(do, h), heads(v, h)) for h in range(NH)], axis=1) * t.tri4
        dv = jnp.concatenate([_btn(t.pst[:, h * CH:(h + 1) * CH], heads(do, h)) for h in range(NH)], axis=2)
        dqd = _bdot(do, st_ref[...])
        for h in range(NH):
            dqd = dqd + hm[h] * _bdot(dpst[:, h * CH:(h + 1) * CH], t.kd)
        dkd = _btn(dpst, t.qs)
        u = _btn(do, t.qd) * _block_diag()[None]
        d = dst[...]
        for n in (range(nc) if rev else range(nc - 1, -1, -1)):
            ds_buf[n] = d
            d = d * t.ecl[n] + u[n]
        dst[...] = d
        ds0_ref[...] = d
        ds = ds_buf[...]
        dv = dv + _bnt(t.kdec, ds)
        dkdec = _bdot(v, ds)
        dcl = jnp.sum(ds * st_ref[...], axis=1, keepdims=True) * t.ecl + jnp.sum(dkdec * t.kdec, axis=1, keepdims=True)
        dc = dqd * t.qd - dkd * t.kd - dkdec * t.kdec
        da = _scan_chunks(dc, not rev) + dcl
        dq = dqd * t.ec * QSCALE
        dk = dkd * t.enc + dkdec * t.edec
        dlog = da.reshape(tg, KW) * _sigmoid(-t.logits) * (1.0 / TAU)
        dlr = _nt(dlog, w2_ref[...])
        dw2_ref[...] += _tn(lr_ref[...], dlog)
        dgb_ref[...] += jnp.sum(dlog, axis=0, keepdims=True)
        dqkv = jnp.concatenate([dq, dk, dv], axis=2).reshape(tg, 2 * KW + VW)
        if prev is not None:
            dqkv = dqkv + pq_ref[...]
            dlr = dlr + pl_ref[...]
        dqkv_ref[...] = dqkv.astype(out_dt)
        dlr_ref[...] = dlr.astype(out_dt)

    blk = (lambda i: i) if rev else (lambda i: nb - 1 - i)
    in_specs = [pl.BlockSpec((tg, KW), lambda i: (blk(i), PQ // KW)), pl.BlockSpec((tg, KW), lambda i: (blk(i), PK // KW)),
                pl.BlockSpec((tg, VW), lambda i: (blk(i), PV // VW)), pl.BlockSpec((tg, LRW), lambda i: (blk(i), PLR // LRW)),
                pl.BlockSpec((tg, VW), lambda i: (blk(i), 0)), pl.BlockSpec((nc, VW, KW), lambda i: (blk(i), 0, 0)),
                pl.BlockSpec((LRW, KW), lambda i: (0, 0)), pl.BlockSpec((1, KW), lambda i: (0, 0))]
    args = [p, p, p, p, do, states, w2p, gb]
    if prev is not None:
        in_specs += [pl.BlockSpec((tg, 2 * KW + VW), lambda i: (blk(i), 0)), pl.BlockSpec((tg, LRW), lambda i: (blk(i), 0))]
        args += list(prev)
    return pl.pallas_call(
        body, name=name, grid=(nb,),
        out_shape=(jax.ShapeDtypeStruct((l, 2 * KW + VW), out_dt), jax.ShapeDtypeStruct((l, LRW), out_dt),
                   jax.ShapeDtypeStruct((LRW, KW), f32), jax.ShapeDtypeStruct((1, KW), f32), jax.ShapeDtypeStruct((VW, KW), f32))
        + tuple(comm.outs),
        in_specs=in_specs + c_in,
        out_specs=(pl.BlockSpec((tg, 2 * KW + VW), lambda i: (blk(i), 0)), pl.BlockSpec((tg, LRW), lambda i: (blk(i), 0)),
                   pl.BlockSpec((LRW, KW), lambda i: (0, 0)), pl.BlockSpec((1, KW), lambda i: (0, 0)),
                   pl.BlockSpec((VW, KW), lambda i: (0, 0))) + tuple(c_out),
        scratch_shapes=[pltpu.VMEM((VW, KW), f32), pltpu.VMEM((nc, VW, KW), f32)] + c_sems,
        compiler_params=_cp(("arbitrary",)),
    )(*args, *comm.ins)


def _ctx_hidden(ctx_ref, g_ref, sc_ref, sh_ref):
    xv = ctx_ref[...]
    r = lax.rsqrt(jnp.mean(xv * xv, axis=-1, keepdims=True) + EPS)
    xn = xv * r
    return xn, xn * (g_ref[...] * (1.0 + sc_ref[...])) + sh_ref[...]


_CTX_W_SPECS = [pl.BlockSpec((D, KW), lambda i: (0, PK // KW)), pl.BlockSpec((D, VW), lambda i: (0, PV // VW)),
                pl.BlockSpec((D, LRW), lambda i: (0, PLR // LRW))]


def _ctx_fwd(ctx, g, scale, shift, w_pad, w2f, w2b, gbf, gbb):
    ncc = CTX // CH

    def body(ctx_ref, g_ref, sc_ref, sh_ref, wk_ref, wv_ref, wl_ref, w2f_ref, w2b_ref, gbf_ref, gbb_ref, sf_ref, sb_ref):
        _, hc = _ctx_hidden(ctx_ref, g_ref, sc_ref, sh_ref)
        k, v, lr = _dot(hc, wk_ref[...]), _dot(hc, wv_ref[...]), _dot(hc, wl_ref[...])
        bd = _block_diag()
        for rev, w2_ref, gb_ref, out in ((False, w2f_ref, gbf_ref, sf_ref), (True, w2b_ref, gbb_ref, sb_ref)):
            tri = _tri(rev)
            st = jnp.zeros((VW, KW), f32)
            for j in (range(ncc - 1, -1, -1) if rev else range(ncc)):
                rows = slice(j * CH, (j + 1) * CH)
                _, c, cl = _decay(lr[rows], w2_ref[...], gb_ref[...], tri, rev)
                st = _state_fwd(k[rows], v[rows], c, cl, st, bd)
            out[...] = st

    vec = pl.BlockSpec((1, D), lambda i: (0, 0))
    w2s = pl.BlockSpec((LRW, KW), lambda i: (0, 0))
    gbs = pl.BlockSpec((1, KW), lambda i: (0, 0))
    sts = pl.BlockSpec((VW, KW), lambda i: (0, 0))
    return pl.pallas_call(
        body, name="ctx_fwd", grid=(1,), out_shape=(jax.ShapeDtypeStruct((VW, KW), f32),) * 2,
        in_specs=[pl.BlockSpec((CTX, D), lambda i: (0, 0)), vec, vec, vec] + _CTX_W_SPECS + [w2s, w2s, gbs, gbs],
        out_specs=(sts, sts), compiler_params=_cp(("arbitrary",)),
    )(ctx, g, scale, shift, w_pad, w_pad, w_pad, w2f, w2b, gbf, gbb)


def _ctx_bwd(ctx, g, scale, shift, w_pad, w2f, w2b, gbf, gbb, dsf, dsb):
    ncc = CTX // CH

    def body(ctx_ref, g_ref, sc_ref, sh_ref, wk_ref, wv_ref, wl_ref, w2f_ref, w2b_ref, gbf_ref, gbb_ref, dsf_ref, dsb_ref,
             dwk_ref, dwv_ref, dwl_ref, dmod_ref, dg_ref, dw2_ref, dgb_ref):
        xn, hc = _ctx_hidden(ctx_ref, g_ref, sc_ref, sh_ref)
        k, v, lr = _dot(hc, wk_ref[...]), _dot(hc, wv_ref[...]), _dot(hc, wl_ref[...])
        bd = _block_diag()
        dk_rows, dv_rows, dl_rows = [None] * ncc, [None] * ncc, [None] * ncc
        for d, (rev, w2_ref, gb_ref, ds_ref) in enumerate(((False, w2f_ref, gbf_ref, dsf_ref), (True, w2b_ref, gbb_ref, dsb_ref))):
            tri = _tri(rev)
            order = list(range(ncc - 1, -1, -1) if rev else range(ncc))
            st, saved = jnp.zeros((VW, KW), f32), {}
            for j in order:
                rows = slice(j * CH, (j + 1) * CH)
                logits, c, cl = _decay(lr[rows], w2_ref[...], gb_ref[...], tri, rev)
                saved[j] = (logits, c, cl, st)
                st = _state_fwd(k[rows], v[rows], c, cl, st, bd)
            dst = ds_ref[...]
            dw2 = jnp.zeros((LRW, KW), f32)
            dgb = jnp.zeros((1, KW), f32)
            for j in reversed(order):
                rows = slice(j * CH, (j + 1) * CH)
                logits, c, cl, st0 = saved[j]
                dk, dv, da, dst = _state_bwd(k[rows], v[rows], c, cl, st0, dst, _tri(not rev))
                dlog = da * _sigmoid(-logits) * (1.0 / TAU)
                dl = _nt(dlog, w2_ref[...])
                dw2 = dw2 + _tn(lr[rows], dlog)
                dgb = dgb + jnp.sum(dlog, axis=0, keepdims=True)
                dk_rows[j] = dk if dk_rows[j] is None else dk_rows[j] + dk
                dv_rows[j] = dv if dv_rows[j] is None else dv_rows[j] + dv
                dl_rows[j] = dl if dl_rows[j] is None else dl_rows[j] + dl
            dw2_ref[d] = dw2
            dgb_ref[d] = dgb
        dk, dv, dl = (jnp.concatenate(t, axis=0) for t in (dk_rows, dv_rows, dl_rows))
        dwk_ref[...] = _tn(hc, dk)
        dwv_ref[...] = _tn(hc, dv)
        dwl_ref[...] = _tn(hc, dl)
        dh = _nt(dk, wk_ref[...]) + _nt(dv, wv_ref[...]) + _nt(dl, wl_ref[...])
        gx = dh * xn
        dmod_ref[:, 0:D] = jnp.sum(dh, axis=0, keepdims=True)
        dmod_ref[:, D:2 * D] = jnp.sum(gx, axis=0, keepdims=True) * g_ref[...]
        dg_ref[...] = jnp.sum(gx, axis=0, keepdims=True) * (1.0 + sc_ref[...])

    vec = pl.BlockSpec((1, D), lambda i: (0, 0))
    w2s = pl.BlockSpec((LRW, KW), lambda i: (0, 0))
    gbs = pl.BlockSpec((1, KW), lambda i: (0, 0))
    sts = pl.BlockSpec((VW, KW), lambda i: (0, 0))
    full = lambda *s: pl.BlockSpec(s, lambda i: (0,) * len(s))
    return pl.pallas_call(
        body, name="ctx_bwd", grid=(1,),
        out_shape=(jax.ShapeDtypeStruct((D, KW), f32), jax.ShapeDtypeStruct((D, VW), f32), jax.ShapeDtypeStruct((D, LRW), f32),
                   jax.ShapeDtypeStruct((1, 2 * D), f32), jax.ShapeDtypeStruct((1, D), f32),
                   jax.ShapeDtypeStruct((2, LRW, KW), f32), jax.ShapeDtypeStruct((2, 1, KW), f32)),
        in_specs=[pl.BlockSpec((CTX, D), lambda i: (0, 0)), vec, vec, vec] + _CTX_W_SPECS + [w2s, w2s, gbs, gbs, sts, sts],
        out_specs=(full(D, KW), full(D, VW), full(D, LRW), full(1, 2 * D), full(1, D), full(2, LRW, KW), full(2, 1, KW)),
        compiler_params=_cp(("arbitrary",)),
    )(ctx, g, scale, shift, w_pad, w_pad, w_pad, w2f, w2b, gbf, gbb, dsf, dsb)


def _layernorm(va, g, b):
    mu = jnp.mean(va, axis=-1, keepdims=True)
    xc = va - mu
    rstd = lax.rsqrt(jnp.mean(xc * xc, axis=-1, keepdims=True) + EPS)
    vhat = xc * rstd
    return vhat, rstd, vhat * g + b


def _mix_fwd(p, ln_g, ln_b, ws, bs_t):
    l = p.shape[0]
    nch = l // ACH
    half = AW // 2

    def body(p_ref, g_ref, b_ref, ws_ref, bs_ref, sv_ref, va_buf, col_buf, sem):
        cp = pltpu.make_async_copy(p_ref.at[:, pl.ds(PVA, AW)], va_buf, sem)
        cp.start()
        cp.wait()

        def rows_step(n, carry):
            rows = pl.ds(pl.multiple_of(n * ACH, ACH), ACH)
            _, _, vn = _layernorm(va_buf[rows, :].astype(f32), g_ref[...], b_ref[...])
            for gi in range(2):
                sl = slice(gi * ACH, (gi + 1) * ACH)
                sv_ref[rows, sl] = (_dot(ws_ref[gi], vn[:, sl]) + bs_ref[:, gi:gi + 1]).astype(bf16)
            col_buf[0, rows, :] = vn[:, half:half + ACH]
            col_buf[1, rows, :] = vn[:, half + ACH:]
            return carry

        lax.fori_loop(0, nch, rows_step, 0, unroll=8)

        def cols_step(cidx, carry):
            rows = pl.ds(cidx, ACH, stride=GW)
            for gi in range(2, 4):
                col_buf[gi - 2, rows, :] = _dot(ws_ref[gi], col_buf[gi - 2, rows, :]) + bs_ref[:, gi:gi + 1]
            return carry

        lax.fori_loop(0, GW, cols_step, 0, unroll=8)

        def out_step(n, carry):
            rows = pl.ds(pl.multiple_of(n * ACH, ACH), ACH)
            sv_ref[rows, half:half + ACH] = col_buf[0, rows, :].astype(bf16)
            sv_ref[rows, half + ACH:] = col_buf[1, rows, :].astype(bf16)
            return carry

        lax.fori_loop(0, nch, out_step, 0, unroll=8)

    vm = pl.BlockSpec(memory_space=pltpu.VMEM)
    return pl.pallas_call(
        body, name="mix_fwd", out_shape=jax.ShapeDtypeStruct((l, AW), bf16),
        in_specs=[pl.BlockSpec(memory_space=pl.ANY), vm, vm, vm, vm], out_specs=vm,
        scratch_shapes=[pltpu.VMEM((l, AW), bf16), pltpu.VMEM((2, l, ACH), f32), pltpu.SemaphoreType.DMA(())],
        compiler_params=_cp(),
    )(p, ln_g, ln_b, ws, bs_t)


def _mix_bwd(p, dsv, ln_g, ln_b, ws_t):
    l = p.shape[0]
    nch = l // ACH
    half = AW // 2

    def body(p_ref, dsv_hbm, g_ref, b_ref, wst_ref, dva_ref, dws_ref, dbs_ref, dg_ref, db_ref, va_buf, dsv_buf, vn_col, ds_col, sems):
        cp1 = pltpu.make_async_copy(p_ref.at[:, pl.ds(PVA, AW)], va_buf, sems.at[0])
        cp2 = pltpu.make_async_copy(dsv_hbm, dsv_buf, sems.at[1])
        cp1.start()
        cp2.start()
        dws_ref[...] = jnp.zeros_like(dws_ref)
        dbs_ref[...] = jnp.zeros_like(dbs_ref)
        dg_ref[...] = jnp.zeros_like(dg_ref)
        db_ref[...] = jnp.zeros_like(db_ref)
        cp1.wait()
        cp2.wait()

        def rows_step(n, carry):
            rows = pl.ds(pl.multiple_of(n * ACH, ACH), ACH)
            _, _, vn = _layernorm(va_buf[rows, :].astype(f32), g_ref[...], b_ref[...])
            ds = dsv_buf[rows, :].astype(f32)
            for gi in range(2):
                sl = slice(gi * ACH, (gi + 1) * ACH)
                dws_ref[gi] += _nt(ds[:, sl], vn[:, sl])
                dbs_ref[gi] += ds[:, sl]
            for gi in range(2):
                sl = slice(half + gi * ACH, half + (gi + 1) * ACH)
                vn_col[gi, rows, :] = vn[:, sl]
                ds_col[gi, rows, :] = ds[:, sl]
            return carry

        lax.fori_loop(0, nch, rows_step, 0, unroll=8)

        def cols_step(cidx, carry):
            rows = pl.ds(cidx, ACH, stride=GW)
            for gi in range(2, 4):
                ds = ds_col[gi - 2, rows, :]
                dws_ref[gi] += _nt(ds, vn_col[gi - 2, rows, :])
                dbs_ref[gi] += ds
                ds_col[gi - 2, rows, :] = _dot(wst_ref[gi], ds)
            return carry

        lax.fori_loop(0, GW, cols_step, 0, unroll=8)

        def out_step(n, carry):
            rows = pl.ds(pl.multiple_of(n * ACH, ACH), ACH)
            vhat, rstd, _ = _layernorm(va_buf[rows, :].astype(f32), g_ref[...], b_ref[...])
            ds = dsv_buf[rows, :].astype(f32)
            dvn = jnp.concatenate([_dot(wst_ref[0], ds[:, 0:ACH]), _dot(wst_ref[1], ds[:, ACH:half]), ds_col[0, rows, :], ds_col[1, rows, :]], axis=1)
            dg_ref[...] += jnp.sum(dvn * vhat, axis=0, keepdims=True)
            db_ref[...] += jnp.sum(dvn, axis=0, keepdims=True)
            dvh = dvn * g_ref[...]
            dva = rstd * (dvh - jnp.mean(dvh, axis=-1, keepdims=True) - vhat * jnp.mean(dvh * vhat, axis=-1, keepdims=True))
            dva_ref[rows, :] = dva.astype(bf16)
            return carry

        lax.fori_loop(0, nch, out_step, 0, unroll=8)

    vm = pl.BlockSpec(memory_space=pltpu.VMEM)
    hbm = pl.BlockSpec(memory_space=pl.ANY)
    return pl.pallas_call(
        body, name="mix_bwd",
        out_shape=(jax.ShapeDtypeStruct((l, AW), bf16), jax.ShapeDtypeStruct((4, ACH, ACH), f32), jax.ShapeDtypeStruct((4, ACH, ACH), f32),
                   jax.ShapeDtypeStruct((1, AW), f32), jax.ShapeDtypeStruct((1, AW), f32)),
        in_specs=[hbm, hbm, vm, vm, vm], out_specs=(vm, vm, vm, vm, vm),
        scratch_shapes=[pltpu.VMEM((l, AW), bf16), pltpu.VMEM((l, AW), bf16), pltpu.VMEM((2, l, ACH), f32), pltpu.VMEM((2, l, ACH), f32),
                        pltpu.SemaphoreType.DMA((2,))],
        compiler_params=_cp(),
    )(p, dsv, ln_g, ln_b, ws_t)


def _mid(x, tgt, p, o_f, o_b, sv, gate, gf, gb_norm, w_pa, w_pb, w_out, tl):
    l = x.shape[0]

    def body(x_ref, t_ref, zb_ref, ua_ref, za_ref, g1_ref, g2_ref, of_ref, ob_ref, sv_ref, gate_ref, gf_ref, gbn_ref,
             wpa_ref, wpb_ref, wout_ref,
             dx1_ref, dzbua_ref, dzag_ref, dsv_ref, do_ref, dwout_bf, dwpa_bf, dwpb_bf, dgf_ref, dgate_ref, dgbn_ref, loss_ref,
             dwout_ref, dwpa_ref, dwpb_ref):
        @pl.when(pl.program_id(0) == 0)
        def _():
            for r in (dwout_ref, dwpa_ref, dwpb_ref, dgf_ref, dgate_ref, dgbn_ref, loss_ref):
                r[...] = jnp.zeros_like(r)

        o = of_ref[...] + ob_ref[...]
        rr = jnp.concatenate(
            [jnp.broadcast_to(lax.rsqrt(jnp.mean(o[:, h * HV:(h + 1) * HV] ** 2, axis=-1, keepdims=True) + EPS), (tl, HV))
             for h in range(NH)], axis=1)
        ohat = o * rr
        on = ohat * gbn_ref[...]
        szb, dszb = _silu_and_grad(zb_ref[...].astype(f32))
        tb = on * szb
        u = ua_ref[...].astype(f32)
        svv = sv_ref[...].astype(f32)
        sza, dsza = _silu_and_grad(za_ref[...].astype(f32))
        ta = u * svv * sza
        ya = _dot(ta, wpa_ref[...])
        yb = _dot(tb, wpb_ref[...])
        g1 = _sigmoid(g1_ref[...].astype(f32))
        g2 = _sigmoid(g2_ref[...].astype(f32))
        m = g1 * ya + g2 * yb
        y2 = _dot(m, wout_ref[...])
        x1 = x_ref[...] + gate_ref[...] * y2
        r1 = lax.rsqrt(jnp.mean(x1 * x1, axis=-1, keepdims=True) + EPS)
        x1n = x1 * r1
        err = x1n * gf_ref[...] - t_ref[...]
        loss_ref[...] += jnp.sum(jnp.sum(err * err, axis=-1, keepdims=True), axis=0, keepdims=True) * (0.5 / D)
        dout = err * (1.0 / D)
        dgf_ref[...] += jnp.sum(dout * x1n, axis=0, keepdims=True)
        dx1n = dout * gf_ref[...]
        dx1 = r1 * (dx1n - x1n * jnp.mean(dx1n * x1n, axis=-1, keepdims=True))
        dx1_ref[...] = dx1
        dgate_ref[...] += jnp.sum(dx1 * y2, axis=0, keepdims=True)
        dy2 = dx1 * gate_ref[...]
        dwout_ref[...] += _tn(m, dy2)
        dm = _nt(dy2, wout_ref[...])
        dya = dm * g1
        dyb = dm * g2
        dzag_ref[:, AW:AW + D] = (dm * ya * g1 * (1.0 - g1)).astype(bf16)
        dzag_ref[:, AW + D:] = (dm * yb * g2 * (1.0 - g2)).astype(bf16)
        dwpa_ref[...] += _tn(ta, dya)
        dta = _nt(dya, wpa_ref[...])
        dzbua_ref[:, AW:] = (dta * svv * sza).astype(bf16)
        dsv_ref[...] = (dta * u * sza).astype(bf16)
        dzag_ref[:, 0:AW] = (dta * u * svv * dsza).astype(bf16)
        dwpb_ref[...] += _tn(tb, dyb)
        dtb = _nt(dyb, wpb_ref[...])
        don = dtb * szb
        dzbua_ref[:, 0:AW] = (dtb * on * dszb).astype(bf16)
        dgbn_ref[...] += jnp.sum(don * ohat, axis=0, keepdims=True)
        doh = don * gbn_ref[...]
        prod = doh * ohat
        mh = jnp.concatenate(
            [jnp.broadcast_to(jnp.mean(prod[:, h * HV:(h + 1) * HV], axis=-1, keepdims=True), (tl, HV)) for h in range(NH)], axis=1)
        do_ref[...] = (rr * (doh - ohat * mh)).astype(bf16)

        @pl.when(pl.program_id(0) == l // tl - 1)
        def _():
            for acc, out in ((dwout_ref, dwout_bf), (dwpa_ref, dwpa_bf), (dwpb_ref, dwpb_bf)):
                out[...] = acc[...].astype(bf16)

    row = lambda w, j: pl.BlockSpec((tl, w), lambda i, j=j: (i, j))
    full = lambda *s: pl.BlockSpec(s, lambda i: (0,) * len(s))
    return pl.pallas_call(
        body, name="mid", grid=(l // tl,),
        out_shape=(jax.ShapeDtypeStruct((l, D), f32), jax.ShapeDtypeStruct((l, 2 * AW), bf16), jax.ShapeDtypeStruct((l, AW + 2 * D), bf16),
                   jax.ShapeDtypeStruct((l, AW), bf16), jax.ShapeDtypeStruct((l, VW), bf16),
                   jax.ShapeDtypeStruct((D, D), bf16), jax.ShapeDtypeStruct((AW, D), bf16), jax.ShapeDtypeStruct((VW, D), bf16),
                   jax.ShapeDtypeStruct((1, D), f32), jax.ShapeDtypeStruct((1, D), f32), jax.ShapeDtypeStruct((1, VW), f32),
                   jax.ShapeDtypeStruct((1, 1), f32)),
        scratch_shapes=[pltpu.VMEM((D, D), f32), pltpu.VMEM((AW, D), f32), pltpu.VMEM((VW, D), f32)],
        in_specs=[row(D, 0), row(D, 0), row(AW, PZB // AW), row(AW, PUA // AW), row(AW, PZA // AW), row(D, PG1 // D), row(D, PG2 // D),
                  row(VW, 0), row(VW, 0), row(AW, 0), full(1, D), full(1, D), full(1, VW), full(AW, D), full(VW, D), full(D, D)],
        out_specs=(row(D, 0), row(2 * AW, 0), row(AW + 2 * D, 0), row(AW, 0), row(VW, 0),
                   full(D, D), full(AW, D), full(VW, D), full(1, D), full(1, D), full(1, VW), full(1, 1)),
        compiler_params=_cp(("arbitrary",)),
    )(x, tgt, p, p, p, p, p, o_f, o_b, sv, gate, gf, gb_norm, w_pa, w_pb, w_out)


def _in_bwd(x, dx1, dqkv, dzbua, dva, dzag, dlr, w_pad, g, scale, tl, comm):
    l = x.shape[0]
    c_in, c_out, c_sems = comm.specs()

    def body(*refs):
        cin = refs[14:14 + len(c_in)]
        outs = refs[14 + len(c_in):]
        comm.run(cin, outs[4:4 + comm.n], outs[4 + comm.n:], l // tl, lambda: compute(*refs[:14], *outs[:4]))

    def compute(x_ref, dx1_ref, a_ref, b_ref, c_ref, e_ref, lr_ref, wa_ref, wb_ref, wc_ref, we_ref, wl_ref, g_ref, sc_ref,
                gx_ref, dsh_ref, dsc_ref, dg_ref):
        @pl.when(pl.program_id(0) == 0)
        def _():
            for r in (dsh_ref, dsc_ref, dg_ref):
                r[...] = jnp.zeros_like(r)

        dh = (_nt(a_ref[...], wa_ref[...]) + _nt(b_ref[...], wb_ref[...]) + _nt(c_ref[...], wc_ref[...]) + _nt(e_ref[...], we_ref[...])
              + _nt(lr_ref[...], wl_ref[...]))
        xv = x_ref[...]
        r = lax.rsqrt(jnp.mean(xv * xv, axis=-1, keepdims=True) + EPS)
        xn = xv * r
        gxn = jnp.sum(dh * xn, axis=0, keepdims=True)
        dsh_ref[...] += jnp.sum(dh, axis=0, keepdims=True)
        dsc_ref[...] += gxn * g_ref[...]
        dg_ref[...] += gxn * (1.0 + sc_ref[...])
        dxn = dh * (g_ref[...] * (1.0 + sc_ref[...]))
        gx_ref[...] = dx1_ref[...] + r * (dxn - xn * jnp.mean(dxn * xn, axis=-1, keepdims=True))

    row = lambda w: pl.BlockSpec((tl, w), lambda i: (i, 0))
    wcol = lambda w, j: pl.BlockSpec((D, w), lambda i, j=j: (0, j))
    vec = pl.BlockSpec((1, D), lambda i: (0, 0))
    return pl.pallas_call(
        body, name="in_bwd", grid=(l // tl,),
        out_shape=(jax.ShapeDtypeStruct((l, D), f32),) + (jax.ShapeDtypeStruct((1, D), f32),) * 3 + tuple(comm.outs),
        in_specs=[row(D), row(D), row(2 * KW + VW), row(2 * AW), row(AW), row(AW + 2 * D), row(LRW),
                  wcol(2 * KW + VW, 0), wcol(2 * AW, PZB // (2 * AW)), wcol(AW, PVA // AW), wcol(AW + 2 * D, PZA // (AW + 2 * D)),
                  wcol(LRW, PLR // LRW), vec, vec] + c_in,
        out_specs=(row(D), vec, vec, vec) + tuple(c_out), scratch_shapes=c_sems,
        compiler_params=_cp(("arbitrary",)),
    )(x, dx1, dqkv, dzbua, dva, dzag, dlr, w_pad, w_pad, w_pad, w_pad, w_pad, g, scale, *comm.ins)


def _tn_matmul(a, bs, tl, name, comm=None):
    l, m = a.shape
    k = len(bs)
    comm = comm or _Comm([], [], None)
    c_in, c_out, c_sems = comm.specs()

    def body(a_ref, *refs):
        b_refs, cin = refs[:k], refs[k:k + len(c_in)]
        o_refs = refs[k + len(c_in):2 * k + len(c_in)]
        cout, sems = refs[2 * k + len(c_in):2 * k + len(c_in) + comm.n], refs[2 * k + len(c_in) + comm.n:]

        def compute():
            @pl.when(pl.program_id(0) == 0)
            def _():
                for o_ref in o_refs:
                    o_ref[...] = jnp.zeros_like(o_ref)

            av = a_ref[...]
            for b_ref, o_ref in zip(b_refs, o_refs):
                o_ref[...] += _tn(av, b_ref[...])

        comm.run(cin, cout, sems, l // tl, compute)

    return pl.pallas_call(
        body, name=name, grid=(l // tl,), out_shape=tuple(jax.ShapeDtypeStruct((m, b.shape[1]), f32) for b in bs) + tuple(comm.outs),
        in_specs=[pl.BlockSpec((tl, m), lambda i: (i, 0))] + [pl.BlockSpec((tl, b.shape[1]), lambda i: (i, 0)) for b in bs] + c_in,
        out_specs=tuple(pl.BlockSpec((m, b.shape[1]), lambda i: (0, 0)) for b in bs) + tuple(c_out),
        scratch_shapes=c_sems, compiler_params=_cp(("arbitrary",)),
    )(a, *bs, *comm.ins)


def _pad_gate(w2, gb):
    z = jnp.zeros((RANK, KW), f32)
    tail = jnp.zeros((LRW - 2 * RANK, KW), f32)
    w2f = jnp.concatenate([w2[0], z, tail], axis=0)
    w2b = jnp.concatenate([z, w2[1], tail], axis=0)
    return w2f, w2b, gb[0:1], gb[1:2]


EARLY_A_ROWS = 384


class _NoExchange:
    def __init__(self, w_pa, w_pb, w_out):
        self.weights = (w_pa, w_pb, w_out)

    def gather_proj(self):
        return _Comm([], [], None)

    def proj_weights(self, got):
        return self.weights

    def first(self, dw_out, dw_pa, dw_pb):
        return _Comm([], [], None)

    def early_a(self, small, blocks):
        return _Comm([], [], None)

    def early_b(self, blocks):
        return _Comm([], [], None)

    def late(self, blocks, dgt):
        return _Comm([], [], None)


class _Exchanges:
    def __init__(self, pa, pb, wo):
        self.shards = (pa, pb, wo)

    def gather_proj(self):
        def plan(i, o):
            srcs = [lambda j, r=r: r for r in i]
            dsts = [lambda j: o[0].at[:, _lanes(j)], lambda j: o[1].at[:, _lanes(j)], lambda j: o[2].at[j]]
            return srcs, dsts, None
        sds = jax.ShapeDtypeStruct
        return _Comm(self.shards, [sds((AW, D), bf16), sds((VW, D), bf16), sds((NDEV, 128, D), bf16)], plan)

    def proj_weights(self, got):
        return got[0], got[1], got[2].reshape(D, D)

    def first(self, dw_out, dw_pa, dw_pb):
        def plan(i, o):
            srcs = [lambda j: i[0].at[j], lambda j: i[1].at[:, _lanes(j)], lambda j: i[2].at[:, _lanes(j)]]
            dsts = [lambda j, r=r: r.at[j] for r in o]
            return srcs, dsts, None
        sds = jax.ShapeDtypeStruct
        return _Comm([dw_out.reshape(NDEV, 128, D), dw_pa, dw_pb],
                     [sds((NDEV, 128, D), bf16), sds((NDEV, AW, 128), bf16), sds((NDEV, VW, 128), bf16)], plan)

    def early_a(self, small, blocks):
        def plan(i, o):
            srcs = [lambda j: i[0], lambda j: i[1].at[j]]
            dsts = [lambda j, r=r: r.at[j] for r in o]
            return srcs, dsts, [None, lambda j: j >= LATE_DESTS - 1]
        sds = jax.ShapeDtypeStruct
        return _Comm([small, blocks], [sds((NDEV,) + small.shape, f32), sds(blocks.shape, bf16)], plan)

    def early_b(self, blocks):
        def plan(i, o):
            return [lambda j: i[0].at[j]], [lambda j: o[0].at[j]], [lambda j: j >= LATE_DESTS - 1]
        return _Comm([blocks], [jax.ShapeDtypeStruct(blocks.shape, bf16)], plan)

    def late(self, blocks, dgt):
        return _LateComm(blocks, dgt)


def _local_step(x, ctx, tgt, mod, modc, norm_g, w_pad, ln_g, ln_b, ws, bs, w2, gb, gb_norm, gf, xch):
    shift, scale, gate = mod[:, 0:D], mod[:, D:2 * D], mod[:, 2 * D:]
    shift_c, scale_c = modc[:, 0:D], modc[:, D:]
    w2f, w2b, gbf, gbb = _pad_gate(w2, gb)

    p, h, *got_proj = _in_proj(x, norm_g, scale, shift, w_pad, 512, xch.gather_proj())
    w_pa, w_pb, w_out = xch.proj_weights(got_proj)
    sc_f, sc_b = _ctx_fwd(ctx, norm_g, scale_c, shift_c, w_pad, w2f, w2b, gbf, gbb)
    o_f, st_f = _gla_fwd(p, w2f, gbf, sc_f, False, 512, "gla_fwd_f")
    o_b, st_b = _gla_fwd(p, w2b, gbb, sc_b, True, 512, "gla_fwd_b")
    sv = _mix_fwd(p, ln_g, ln_b, ws.astype(bf16), bs.T)
    (dx1, dzbua, dzag, dsv, do, dw_out, dw_pa, dw_pb, dgf, dgate, dgbn, loss) = _mid(
        x, tgt, p, o_f, o_b, sv, gate, gf, gb_norm, w_pa, w_pb, w_out, 256)
    dva, dws, dbs_acc, dln_g, dln_b = _mix_bwd(p, dsv, ln_g, ln_b, jnp.swapaxes(ws, 1, 2).astype(bf16))
    dw_zbua, dw_va, dw_zag, *got_first = _tn_matmul(h, [dzbua, dva, dzag], 1024, "dw_early", xch.first(dw_out, dw_pa, dw_pb))
    blocks_a = _pack_dw_early(dw_zbua, dw_va, dw_zag, 0, EARLY_A_ROWS, "pack_dw_early_a")
    blocks_b = _pack_dw_early(dw_zbua, dw_va, dw_zag, EARLY_A_ROWS, D - EARLY_A_ROWS, "pack_dw_early_b")
    small = _rows128(dln_g, dln_b, dws, jnp.sum(dbs_acc, axis=-1), dgbn, dgf, jnp.broadcast_to(loss, (1, 128)))

    dqkv_f, dlr_f, dw2f, dgbf, dsc_f, *got_a = _gla_bwd(p, do, st_f, w2f, gbf, None, False, 512, "gla_bwd_f",
                                                        xch.early_a(small, blocks_a))
    dqkv, dlr, dw2b, dgbb, dsc_b, *got_b = _gla_bwd(p, do, st_b, w2b, gbb, (dqkv_f, dlr_f), True, 512, "gla_bwd_b",
                                                    xch.early_b(blocks_b))
    dwk_c, dwv_c, dwl_c, dmodc, dg_c, dw2c, dgbc = _ctx_bwd(ctx, norm_g, scale_c, shift_c, w_pad, w2f, w2b, gbf, gbb, dsc_f, dsc_b)
    dw_qkv, dw_lr = _tn_matmul(h, [dqkv, dlr], 1024, "dw_qkv_lr")
    blocks_late = _pack_dw_late(dw_qkv, dw_lr, dwk_c, dwv_c, dwl_c)
    dw2 = jnp.stack([dw2f[0:RANK] + dw2c[0, 0:RANK], dw2b[RANK:2 * RANK] + dw2c[1, RANK:2 * RANK]])
    dgb = jnp.concatenate([dgbf + dgbc[0], dgbb + dgbc[1]], axis=0)
    dgt = jnp.concatenate([jnp.transpose(dw2.reshape(2, RANK, NDEV, 32), (2, 0, 1, 3)).reshape(NDEV, 2 * RANK * 32),
                           jnp.transpose(dgb.reshape(2, NDEV, 32), (1, 0, 2)).reshape(NDEV, 64),
                           jnp.zeros((NDEV, 64), f32)], axis=1).reshape(NDEV, 9, 128)
    gx, dshift, dscale, dg, *got_late = _in_bwd(x, dx1, dqkv, dzbua, dva, dzag, dlr, w_pad, norm_g, scale, 512,
                                                xch.late(blocks_late, dgt))
    return dict(loss=loss, gx=gx, dmod=jnp.concatenate([dshift, dscale, dgate], axis=1), dmodc=dmodc, dnorm_g=dg + dg_c,
                small=small, blocks_a=blocks_a, blocks_b=blocks_b, blocks_late=blocks_late, dw2=dw2, dgb=dgb,
                dw_pa=dw_pa, dw_pb=dw_pb, dw_out=dw_out, got_first=got_first, got_a=got_a, got_b=got_b, got_late=got_late)


def _rows128(*vs):
    out = []
    for t in vs:
        t = t.reshape(-1)
        pad = (-t.shape[0]) % 128
        out.append(jnp.pad(t, (0, pad)) if pad else t)
    return jnp.concatenate(out).reshape(-1, 128)


def kernel(x, c, ctx, c_ctx, w_mod, b_mod, norm_g, w_in, a_ln_g, a_ln_b, a_ws, a_bs, b_gate_w2, b_gate_b, b_norm_g, w_proj_a, w_proj_b, w_out, final_norm_g, loss_target, m_c_ctx, m_w_mod, m_b_mod, m_norm_g, m_w_in, m_a_ln_g, m_a_ln_b, m_a_ws, m_a_bs, m_b_gate_w2, m_b_gate_b, m_b_norm_g, m_w_proj_a, m_w_proj_b, m_w_out, m_final_norm_g, v_c_ctx, v_w_mod, v_b_mod, v_norm_g, v_w_in, v_a_ln_g, v_a_ln_b, v_a_ws, v_a_bs, v_b_gate_w2, v_b_gate_b, v_b_norm_g, v_w_proj_a, v_w_proj_b, v_w_out, v_final_norm_g):
    me = _me()
    ncol = w_mod.shape[2]

    gate_mine = _rows128(jnp.concatenate([b_gate_w2.reshape(-1), b_gate_b.reshape(-1)]))
    bm_mine = lax.dynamic_slice(b_mod, (0, me * ncol), (1, ncol))
    cs, mods, wg, gates = _gather_first(c, c_ctx.reshape(1, D), w_mod[0], bm_mine, w_in[0].astype(bf16), gate_mine)
    w_pad = _repack_w(wg)
    gflat = gates.reshape(NDEV, 9 * 128)
    w2 = jnp.transpose(gflat[:, 0:2 * RANK * 32].reshape(NDEV, 2, RANK, 32), (1, 2, 0, 3)).reshape(2, RANK, KW)
    gb = jnp.transpose(gflat[:, 2 * RANK * 32:2 * RANK * 32 + 64].reshape(NDEV, 2, 32), (1, 0, 2)).reshape(2, KW)

    mods = jnp.transpose(mods, (1, 0, 2)).reshape(16, 3 * D)
    mod = lax.dynamic_slice(mods, (me, 0), (1, 3 * D))
    modc = mods[8:9, 0:2 * D]

    xch = _Exchanges(w_proj_a[0].astype(bf16), w_proj_b[0].astype(bf16), w_out[0].astype(bf16))
    r = _local_step(x[0], ctx[0], loss_target[0], mod, modc, norm_g, w_pad, a_ln_g, a_ln_b, a_ws[0], a_bs[0], w2, gb,
                    b_norm_g, final_norm_g.reshape(1, D), xch)
    p_out, p_pa, p_pb = r["got_first"]
    smalls_e, p_in_a = r["got_a"]
    (p_in_b,) = r["got_b"]
    _, _, p_in_late, p_gt = r["got_late"]

    n_e = (AW + AW + 4 * ACH * ACH + AW + VW + D) // 128
    row = lambda t: t.reshape(1, D)
    rep_e = _adam_params(smalls_e, [a_ln_g, a_ln_b, a_ws, a_bs, b_norm_g, row(final_norm_g)],
                         [m_a_ln_g, m_a_ln_b, m_a_ws, m_a_bs, m_b_norm_g, row(m_final_norm_g)],
                         [v_a_ln_g, v_a_ln_b, v_a_ws, v_a_bs, v_b_norm_g, row(v_final_norm_g)], "adam_rep_early")
    rep_e = [t[0:5] + (t[5].reshape(D),) for t in rep_e]
    losses = smalls_e[:, n_e, 0]
    loss = losses[0]
    for i in range(1, NDEV):
        loss = loss + losses[i]

    dbm = r["dmod"] + jnp.concatenate([r["dmodc"], jnp.zeros((1, D), f32)], axis=1)
    wm, ng, bmod_r, cc = _mod_tail(r["dnorm_g"], dbm, r["dmod"], r["dmodc"], cs, w_mod, m_w_mod, v_w_mod,
                                   norm_g, m_norm_g, v_norm_g, b_mod, m_b_mod, v_b_mod,
                                   c_ctx.reshape(1, D), m_c_ctx.reshape(1, D), v_c_ctx.reshape(1, D))

    a_in = _adam_w_in(p_in_a, p_in_b, p_in_late, w_in, m_w_in, v_w_in)
    blk = _adam_blocks([p_pa, p_pb, p_out], [w_proj_a, w_proj_b, w_out], [m_w_proj_a, m_w_proj_b, m_w_out],
                       [v_w_proj_a, v_w_proj_b, v_w_out], "adam_proj_out")
    a_pa, a_pb, a_out = ([t[i] for t in blk] for i in range(3))
    gate_m = _rows128(jnp.concatenate([m_b_gate_w2.reshape(-1), m_b_gate_b.reshape(-1)]))
    gate_v = _rows128(jnp.concatenate([v_b_gate_w2.reshape(-1), v_b_gate_b.reshape(-1)]))
    a_gt = [t.reshape(-1) for t in _adam(p_gt, gate_mine, gate_m, gate_v, 9, "adam_gate")]
    nw2 = 2 * RANK * 32
    sh = [(a_in[k], a_pa[k], a_pb[k], a_out[k], a_gt[k][0:nw2].reshape(1, 2, RANK, 32),
           a_gt[k][nw2:nw2 + 64].reshape(1, 2, 32)) for k in range(4)]

    outs = [loss, r["gx"][None]]
    for k in range(4):
        lg, lb, aws, abs_, bng, fng = rep_e[k]
        n_g, bmod = ng[k], bmod_r[k]
        s_in, s_pa, s_pb, s_out, s_w2, s_gb = sh[k]
        outs += [cc[k].reshape(D), wm[k], bmod, n_g, s_in, lg, lb, aws, abs_, s_w2, s_gb, bng, s_pa, s_pb, s_out, fng]
    return tuple(outs)
```

```python
import jax
import jax.numpy as jnp
from jax import lax
from jax.experimental import pallas as pl
from jax.experimental.pallas import tpu as pltpu

f32, bf16 = jnp.float32, jnp.bfloat16

D = 1024
CTX = 256
EPS = 1e-6
AW = 512
ACH = 128
GW = 64
KW = 256
VW = 512
NH = 4
HK = 64
HV = 128
RANK = 16
TAU = 16.0
CH = 64
QSCALE = HK ** -0.5
INW = 5152
NDEV = 8

PQ, PK, PV, PZB, PUA, PVA, PZA, PG1, PG2, PLR, PW = 0, 256, 512, 1024, 1536, 2048, 2560, 3072, 4096, 5120, 5248
LRW = 128

ADAM_LR, ADAM_B1, ADAM_B2, ADAM_EPS, ADAM_WD, ADAM_STEP = 0.001, 0.9, 0.999, 1e-08, 0.01, 10

VMEM_LIMIT = 56 * 1024 * 1024
MESH = pl.DeviceIdType.MESH


def _cp(sem=None):
    return pltpu.CompilerParams(dimension_semantics=sem, vmem_limit_bytes=VMEM_LIMIT)


def _dot(a, b):
    return jnp.dot(a.astype(bf16), b.astype(bf16), preferred_element_type=f32)


def _nt(a, b):
    return lax.dot_general(a.astype(bf16), b.astype(bf16), (((1,), (1,)), ((), ())), preferred_element_type=f32)


def _tn(a, b):
    return lax.dot_general(a.astype(bf16), b.astype(bf16), (((0,), (0,)), ((), ())), preferred_element_type=f32)


def _dot_hi(a, b):
    return jnp.dot(a, b, preferred_element_type=f32, precision=lax.Precision.HIGHEST)


def _sigmoid(x):
    return 1.0 / (1.0 + jnp.exp(-x))


def _log_sigmoid(x):
    return jnp.minimum(x, 0.0) - jnp.log(1.0 + jnp.exp(-jnp.abs(x)))


def _silu_and_grad(z):
    s = _sigmoid(z)
    return z * s, s * (1.0 + z * (1.0 - s))


def _me():
    return 4 * lax.axis_index("x") + 2 * lax.axis_index("y") + lax.axis_index("c")


def _peer(k):
    x, y, c = lax.axis_index("x"), lax.axis_index("y"), lax.axis_index("c")
    px = 1 - x if k & 4 else x
    py = 1 - y if k & 2 else y
    pc = 1 - c if k & 1 else c
    return (px, py, pc), 4 * px + 2 * py + pc


def _fanout(srcs, dsts, send_sems, recv_sems, local_sems, owners=None):
    me = _me()
    n = len(srcs)
    owns = lambda a, j: True if owners is None or owners[a] is None else owners[a](j)

    def guarded(cond, fn):
        if cond is True:
            fn()
        else:
            pl.when(cond)(fn)

    def copies(with_recvs):
        local = [pltpu.make_async_copy(srcs[a](me), dsts[a](me), local_sems.at[a]) for a in range(n)]
        sends, recvs = [], []
        for k in range(1, NDEV):
            dev, idx = _peer(k)
            for a in range(n):
                s = (k - 1) * n + a
                sends.append((owns(a, idx), pltpu.make_async_remote_copy(
                    src_ref=srcs[a](idx), dst_ref=dsts[a](me), send_sem=send_sems.at[s], recv_sem=recv_sems.at[s],
                    device_id=dev, device_id_type=MESH)))
                if with_recvs:
                    recvs.append((owns(a, me), pltpu.make_async_remote_copy(
                        src_ref=srcs[a](idx), dst_ref=dsts[a](idx), send_sem=send_sems.at[s], recv_sem=recv_sems.at[s],
                        device_id=dev, device_id_type=MESH)))
        return local, sends, recvs

    def start():
        local, sends, _ = copies(False)
        for a, cp in enumerate(local):
            guarded(owns(a, me), cp.start)
        for cond, cp in sends:
            guarded(cond, cp.start)

    def finish():
        local, sends, recvs = copies(True)
        for cond, cp in recvs:
            guarded(cond, cp.wait_recv)
        for cond, cp in sends:
            guarded(cond, cp.wait_send)
        for a, cp in enumerate(local):
            guarded(owns(a, me), cp.wait)

    return start, finish


class _Comm:
    def __init__(self, ins, outs, plan):
        self.ins, self.outs, self.plan = list(ins), list(outs), plan
        self.n = len(self.outs)

    def specs(self):
        hbm = pl.BlockSpec(memory_space=pl.ANY)
        return [hbm] * len(self.ins), [hbm] * self.n, _fanout_sems(self.n) if self.n else []

    def run(self, in_refs, out_refs, sems, nsteps, compute):
        if not self.n:
            compute()
            return

        def hooks():
            srcs, dsts, owners = self.plan(in_refs, out_refs)
            return _fanout(srcs, dsts, sems[0], sems[1], sems[2], owners)

        pl.when(pl.program_id(0) == 0)(lambda: hooks()[0]())
        compute()
        pl.when(pl.program_id(0) == nsteps - 1)(lambda: hooks()[1]())


LATE_MID_STEP = 1


class _LateComm:
    def __init__(self, blocks, dgt):
        sds = jax.ShapeDtypeStruct
        self.ins = [blocks, dgt]
        self.outs = [sds((D, SHARD), bf16), sds((D, SHARD), bf16), sds((4, D, SHARD), bf16), sds(dgt.shape, f32)]
        self.n = len(self.outs)

    def specs(self):
        hbm = pl.BlockSpec(memory_space=pl.ANY)
        scratch = [pltpu.VMEM((3, D, SHARD), bf16), pltpu.SemaphoreType.DMA((2,)), pltpu.SemaphoreType.DMA((4,)),
                   pltpu.SemaphoreType.DMA((3,))] + _fanout_sems(1)
        return [hbm] * 2, [hbm] * self.n, scratch

    def run(self, in_refs, out_refs, scratch, nsteps, compute):
        late_ref, dgt_ref = in_refs
        sib_ref, pair_ref, parts_ref, gt_ref = out_refs
        vbuf, send_sems, recv_sems, local_sems, g_send, g_recv, g_local = scratch
        x, y, c = lax.axis_index("x"), lax.axis_index("y"), lax.axis_index("c")
        chip = 2 * x + y
        is_owner_chip = chip == 0
        step = pl.program_id(0)

        def to_sibling():
            return pltpu.make_async_remote_copy(src_ref=late_ref.at[1 - c], dst_ref=sib_ref, send_sem=send_sems.at[0],
                                                recv_sem=recv_sems.at[0], device_id=(x, y, 1 - c), device_id_type=MESH)

        def to_owner(k):
            return pltpu.make_async_remote_copy(src_ref=pair_ref, dst_ref=parts_ref.at[k], send_sem=send_sems.at[1],
                                                recv_sem=recv_sems.at[k], device_id=(0, 0, c), device_id_type=MESH)

        def own_copy():
            return pltpu.make_async_copy(pair_ref, parts_ref.at[0], local_sems.at[2])

        def gates():
            return _fanout([lambda j: dgt_ref.at[j]], [lambda j: gt_ref.at[j]], g_send, g_recv, g_local)

        @pl.when(step == 0)
        def _():
            to_sibling().start()
            gates()[0]()

        compute()

        @pl.when(step == LATE_MID_STEP)
        def _():
            mine = pltpu.make_async_copy(late_ref.at[c], vbuf.at[0], local_sems.at[0])
            mine.start()
            to_sibling().wait_recv()
            theirs = pltpu.make_async_copy(sib_ref, vbuf.at[1], local_sems.at[1])
            theirs.start()
            mine.wait()
            theirs.wait()
            vbuf[2] = (vbuf[0].astype(f32) + vbuf[1].astype(f32)).astype(bf16)
            pltpu.sync_copy(vbuf.at[2], pair_ref)
            pl.when(is_owner_chip)(lambda: own_copy().start())
            pl.when(jnp.logical_not(is_owner_chip))(lambda: to_owner(chip).start())

        @pl.when(step == nsteps - 1)
        def _():
            @pl.when(is_owner_chip)
            def _():
                for k in range(1, 4):
                    to_owner(k).wait_recv()
                own_copy().wait()

            pl.when(jnp.logical_not(is_owner_chip))(lambda: to_owner(chip).wait_send())
            to_sibling().wait_send()
            gates()[1]()


def _fanout_sems(n):
    return [pltpu.SemaphoreType.DMA(((NDEV - 1) * n,)), pltpu.SemaphoreType.DMA(((NDEV - 1) * n,)), pltpu.SemaphoreType.DMA((n,))]


def _lanes(j):
    return pl.ds(pl.multiple_of(j * 128, 128), 128)


def _gather_first(c_row, cctx_row, wm, bm, wi, gate):
    def body(c_ref, cctx_ref, wm_ref, bm_ref, wi_ref, g_ref, cs_ref, mods_ref, owi, og, call_ref, mine_ref,
             send_sems, recv_sems, local_sems, c_send, c_recv, c_local, m_send, m_recv, m_local):
        x, y, c = lax.axis_index("x"), lax.axis_index("y"), lax.axis_index("c")
        sibling = (x, y, 1 - c)
        chips = [(1 - x, y), (x, 1 - y), (1 - x, 1 - y)]
        index = lambda px, py, pc: 4 * px + 2 * py + pc
        arrays = ((wi_ref, owi), (g_ref, og))
        n = len(arrays)

        def copy(a, k, block, to, own=False):
            src, out = arrays[a]
            return pltpu.make_async_remote_copy(
                src_ref=src if own else out.at[index(*block)], dst_ref=out.at[index(*block)],
                send_sem=send_sems.at[k * n + a], recv_sem=recv_sems.at[k * n + a], device_id=to, device_id_type=MESH)

        c_start, c_finish = _fanout([lambda j: c_ref], [lambda j: call_ref.at[j]], c_send, c_recv, c_local)
        c_start()
        mine = [pltpu.make_async_copy(src, out.at[index(x, y, c)], local_sems.at[a]) for a, (src, out) in enumerate(arrays)]
        first = [copy(a, 0, (x, y, c), sibling, own=True) for a in range(n)]
        first += [copy(a, 1 + j, (x, y, c), (*chip, c), own=True) for j, chip in enumerate(chips) for a in range(n)]
        for cp in mine + first:
            cp.start()

        c_finish()
        cs = jnp.concatenate([call_ref[j] for j in range(NDEV)] + [cctx_ref[...], jnp.zeros((16 - NDEV - 1, D), f32)], axis=0)
        cs_ref[...] = cs
        s, _ = _silu_and_grad(cs)
        mine_ref[...] = _dot_hi(s, wm_ref[...]) + bm_ref[...]
        m_start, m_finish = _fanout([lambda j: mine_ref], [lambda j: mods_ref.at[j]], m_send, m_recv, m_local)
        m_start()

        passed = []
        for j, chip in enumerate(chips):
            for a in range(n):
                copy(a, 1 + j, (*chip, c), (x, y, c)).wait_recv()
            for a in range(n):
                cp = copy(a, 4 + j, (*chip, c), sibling)
                cp.start()
                passed.append(cp)
        for a in range(n):
            copy(a, 0, sibling, (x, y, c)).wait_recv()
        for j, chip in enumerate(chips):
            for a in range(n):
                copy(a, 4 + j, (*chip, 1 - c), (x, y, c)).wait_recv()
        for cp in first + passed:
            cp.wait_send()
        for cp in mine:
            cp.wait()
        m_finish()

    hbm = pl.BlockSpec(memory_space=pl.ANY)
    vm = pl.BlockSpec(memory_space=pltpu.VMEM)
    ncol = wm.shape[1]
    return pl.pallas_call(
        body, name="gather_first",
        out_shape=(jax.ShapeDtypeStruct((16, D), f32), jax.ShapeDtypeStruct((NDEV, 16, ncol), f32),
                   jax.ShapeDtypeStruct((NDEV,) + wi.shape, bf16), jax.ShapeDtypeStruct((NDEV,) + gate.shape, f32)),
        in_specs=[vm, vm, vm, vm, hbm, hbm], out_specs=(vm, vm, hbm, hbm),
        scratch_shapes=[pltpu.VMEM((NDEV, 1, D), f32), pltpu.VMEM((16, ncol), f32)] + _fanout_sems(2) + _fanout_sems(1) + _fanout_sems(1),
        compiler_params=_cp(),
    )(c_row, cctx_row, wm, bm, wi, gate)


SHARD = INW // NDEV
ROWS_RP = 128


def _overlap(lo, hi, a, b):
    s, e = max(lo, a), min(hi, b)
    return (s, e) if s < e else None


def _repack_w(wg):
    segs = ((0, 1024, PQ), (1024, 1024 + 2 * RANK, PLR), (1024 + 2 * RANK, INW, PZB))

    def body(g_ref, o_ref):
        for j in range(NDEV):
            lo, hi = j * SHARD, (j + 1) * SHARD
            for a, b, pad0 in segs:
                ov = _overlap(lo, hi, a, b)
                if ov:
                    s, e = ov
                    o_ref[:, pad0 + s - a:pad0 + e - a] = g_ref[j, :, s - lo:e - lo]
        o_ref[:, PLR + 2 * RANK:PW] = jnp.zeros((ROWS_RP, PW - PLR - 2 * RANK), bf16)

    return pl.pallas_call(
        body, name="repack_w", grid=(D // ROWS_RP,), out_shape=jax.ShapeDtypeStruct((D, PW), bf16),
        in_specs=[pl.BlockSpec((NDEV, ROWS_RP, SHARD), lambda i: (0, i, 0))],
        out_specs=pl.BlockSpec((ROWS_RP, PW), lambda i: (i, 0)), compiler_params=_cp(("arbitrary",)),
    )(wg)


LATE_END = 1024 + 2 * RANK
LATE_DESTS = 2


def _pack_blocks(o_ref, srcs, dests, dtype):
    for n, j in enumerate(dests):
        lo, hi = j * SHARD, (j + 1) * SHARD
        done = lo
        for a, b, src in srcs:
            ov = _overlap(lo, hi, a, b)
            if ov:
                s, e = ov
                if s > done:
                    o_ref[n, :, done - lo:s - lo] = jnp.zeros((ROWS_RP, s - done), dtype)
                o_ref[n, :, s - lo:e - lo] = src[:, s - a:e - a].astype(dtype)
                done = e
        if done < hi:
            o_ref[n, :, done - lo:hi - lo] = jnp.zeros((ROWS_RP, hi - done), dtype)


def _pack_dw_early(dw_zbua, dw_va, dw_zag, row0, nrows, name):
    def body(zbua_ref, va_ref, zag_ref, o_ref):
        _pack_blocks(o_ref, ((LATE_END, 2080, zbua_ref), (2080, 2592, va_ref), (2592, INW, zag_ref)), range(NDEV), bf16)

    row = lambda w: pl.BlockSpec((ROWS_RP, w), lambda i: (i + row0 // ROWS_RP, 0))
    return pl.pallas_call(
        body, name=name, grid=(nrows // ROWS_RP,), out_shape=jax.ShapeDtypeStruct((NDEV, nrows, SHARD), bf16),
        in_specs=[row(2 * AW), row(AW), row(AW + 2 * D)],
        out_specs=pl.BlockSpec((NDEV, ROWS_RP, SHARD), lambda i: (0, i, 0)), compiler_params=_cp(("arbitrary",)),
    )(dw_zbua, dw_va, dw_zag)


def _pack_dw_late(dw_qkv, dw_lr, dwk_c, dwv_c, dwl_c):
    def body(qkv_ref, lr_ref, kc_ref, vc_ref, lc_ref, o_ref):
        qkv = qkv_ref[...] + jnp.concatenate([jnp.zeros((ROWS_RP, KW), f32), kc_ref[...], vc_ref[...]], axis=1)
        lr = lr_ref[...] + lc_ref[...]
        _pack_blocks(o_ref, ((0, 1024, qkv), (1024, LATE_END, lr)), range(LATE_DESTS), bf16)

    row = lambda w: pl.BlockSpec((ROWS_RP, w), lambda i: (i, 0))
    return pl.pallas_call(
        body, name="pack_dw_late", grid=(D // ROWS_RP,), out_shape=jax.ShapeDtypeStruct((LATE_DESTS, D, SHARD), bf16),
        in_specs=[row(2 * KW + VW), row(LRW), row(KW), row(VW), row(LRW)],
        out_specs=pl.BlockSpec((LATE_DESTS, ROWS_RP, SHARD), lambda i: (0, i, 0)), compiler_params=_cp(("arbitrary",)),
    )(dw_qkv, dw_lr, dwk_c, dwv_c, dwl_c)


def _mod_tail(dnorm, dbm, dmod, dmodc, cs, wm, m_wm, v_wm, norm_g, m_ng, v_ng, b_mod, m_bm, v_bm, c_ctx, m_cc, v_cc):
    ncol = wm.shape[2]

    def body(dn_ref, dbm_ref, dmod_ref, dmodc_ref, cs_ref, wm_ref, mwm_ref, vwm_ref, ng_ref, mng_ref, vng_ref,
             bm_ref, mbm_ref, vbm_ref, cc_ref, mcc_ref, vcc_ref, *rest):
        o_wm, o_ng, o_bm, o_cc = rest[0:4], rest[4:8], rest[8:12], rest[12:16]
        a_dn, a_dbm, a_dmod, a_dmodc, dm_rows, gc_ref, a_gc = rest[16:23]
        s1, r1, l1, s2, r2, l2 = rest[23:29]
        start, finish = _fanout([lambda j: dn_ref, lambda j: dbm_ref, lambda j: dmod_ref, lambda j: dmodc_ref],
                                [lambda j, r=r: r.at[j] for r in (a_dn, a_dbm, a_dmod, a_dmodc)], s1, r1, l1)
        start()
        finish()

        def total(ref):
            t = ref[0]
            for j in range(1, NDEV):
                t = t + ref[j]
            return t

        _adam_update(total(a_dn), ng_ref, mng_ref, vng_ref, *o_ng)
        _adam_update(total(a_dbm), bm_ref, mbm_ref, vbm_ref, *o_bm)
        dmodc_tot = jnp.concatenate([total(a_dmodc), jnp.zeros((1, D), f32)], axis=1)
        dm_rows[...] = jnp.concatenate([a_dmod[j] for j in range(NDEV)] + [dmodc_tot, jnp.zeros((16 - NDEV - 1, 3 * D), f32)], axis=0)
        dm = dm_rows[:, pl.ds(pl.multiple_of(_me() * ncol, 128), ncol)]
        s, ds = _silu_and_grad(cs_ref[...])
        part = lax.dot_general(dm[8:9, :], wm_ref[0], (((1,), (1,)), ((), ())), preferred_element_type=f32,
                               precision=lax.Precision.HIGHEST)
        gc_ref[...] = part * ds[8:9, :]
        start2, finish2 = _fanout([lambda j: gc_ref], [lambda j: a_gc.at[j]], s2, r2, l2)
        start2()
        g = lax.dot_general(s, dm, (((0,), (0,)), ((), ())), preferred_element_type=f32, precision=lax.Precision.HIGHEST)
        _adam_update(g[None], wm_ref, mwm_ref, vwm_ref, *o_wm)
        finish2()
        _adam_update(total(a_gc), cc_ref, mcc_ref, vcc_ref, *o_cc)

    like = lambda a: [jax.ShapeDtypeStruct(a.shape, f32)] * 4
    row = lambda n: pltpu.VMEM((NDEV, 1, n), f32)
    res = pl.pallas_call(
        body, name="mod_tail", out_shape=tuple(like(wm) + like(norm_g) + like(b_mod) + like(c_ctx)),
        scratch_shapes=[row(D), row(3 * D), row(3 * D), row(2 * D), pltpu.VMEM((16, 3 * D), f32), pltpu.VMEM((1, D), f32), row(D)]
        + _fanout_sems(4) + _fanout_sems(1),
        compiler_params=_cp(),
    )(dnorm, dbm, dmod, dmodc, cs, wm, m_wm, v_wm, norm_g, m_ng, v_ng, b_mod, m_bm, v_bm, c_ctx, m_cc, v_cc)
    return res[0:4], res[4:8], res[8:12], res[12:16]


def _adam_update(g, w_ref, m_ref, v_ref, go_ref, d_ref, mo_ref, vo_ref):
    c1 = 1.0 / (1.0 - ADAM_B1 ** ADAM_STEP)
    c2 = 1.0 / (1.0 - ADAM_B2 ** ADAM_STEP)
    mn = ADAM_B1 * m_ref[...] + (1.0 - ADAM_B1) * g
    vn = ADAM_B2 * v_ref[...] + (1.0 - ADAM_B2) * (g * g)
    go_ref[...] = g
    mo_ref[...] = mn
    vo_ref[...] = vn
    d_ref[...] = -ADAM_LR * ((mn * c1) / (jnp.sqrt(vn * c2) + ADAM_EPS) + ADAM_WD * w_ref[...])


def _adam_w_in(parts_a, parts_b, parts_late, w, m, v):
    na = EARLY_A_ROWS // ROWS_RP

    def body(a_ref, b_ref, l_ref, w_ref, m_ref, v_ref, go_ref, d_ref, mo_ref, vo_ref):
        me = _me()
        first = pl.program_id(0) < na
        early = jnp.where(first, a_ref[0], b_ref[0]).astype(f32)
        for i in range(1, NDEV):
            early = early + jnp.where(first, a_ref[i], b_ref[i]).astype(f32)
        late = l_ref[0].astype(f32)
        for i in range(1, 4):
            late = late + l_ref[i].astype(f32)
        g = jnp.where(me >= LATE_DESTS - 1, early, 0.0) + jnp.where(me < LATE_DESTS, late, 0.0)
        _adam_update(g, w_ref, m_ref, v_ref, go_ref, d_ref, mo_ref, vo_ref)

    blk = pl.BlockSpec((None, ROWS_RP, SHARD), lambda i: (0, i, 0))
    return pl.pallas_call(
        body, name="adam_w_in", grid=(D // ROWS_RP,), out_shape=tuple(jax.ShapeDtypeStruct((1, D, SHARD), f32) for _ in range(4)),
        in_specs=[pl.BlockSpec((NDEV, ROWS_RP, SHARD), lambda i: (0, jnp.minimum(i, na - 1), 0)),
                  pl.BlockSpec((NDEV, ROWS_RP, SHARD), lambda i: (0, jnp.maximum(i - na, 0), 0)),
                  pl.BlockSpec((4, ROWS_RP, SHARD), lambda i: (0, i, 0)), blk, blk, blk],
        out_specs=(blk, blk, blk, blk), compiler_params=_cp(("arbitrary",)),
    )(parts_a, parts_b, parts_late, w, m, v)


def _adam_params(parts, ws, ms, vs, name):
    p = parts.shape[0]
    k = len(ws)
    nrows = [w.size // 128 for w in ws]
    starts = [sum(nrows[:i]) for i in range(k)]

    def shaped(g, shape):
        if len(shape) == 2:
            return jnp.concatenate([g[r:r + 1] for r in range(g.shape[0])], axis=1)
        return g.reshape(shape)

    def body(*refs):
        g_ref = refs[0]
        w_refs, m_refs, v_refs = refs[1:1 + k], refs[1 + k:1 + 2 * k], refs[1 + 2 * k:1 + 3 * k]
        outs = refs[1 + 3 * k:]
        for i in range(k):
            rows = slice(starts[i], starts[i] + nrows[i])
            g = g_ref[0, rows, :]
            for j in range(1, p):
                g = g + g_ref[j, rows, :]
            _adam_update(shaped(g, ws[i].shape), w_refs[i], m_refs[i], v_refs[i], outs[i], outs[k + i], outs[2 * k + i], outs[3 * k + i])

    vm = pl.BlockSpec(memory_space=pltpu.VMEM)
    res = pl.pallas_call(
        body, name=name, out_shape=tuple(jax.ShapeDtypeStruct(w.shape, f32) for _ in range(4) for w in ws),
        in_specs=[vm] * (1 + 3 * k), out_specs=(vm,) * (4 * k), compiler_params=_cp(),
    )(parts, *ws, *ms, *vs)
    return [res[i * k:(i + 1) * k] for i in range(4)]


def _adam_blocks(parts, ws, ms, vs, name):
    k = len(ws)

    def body(*refs):
        g_refs, w_refs, m_refs, v_refs = refs[0:k], refs[k:2 * k], refs[2 * k:3 * k], refs[3 * k:4 * k]
        outs = refs[4 * k:]
        for i in range(k):
            g = g_refs[i][0].astype(f32)
            for j in range(1, parts[i].shape[0]):
                g = g + g_refs[i][j].astype(f32)
            _adam_update(g[None], w_refs[i], m_refs[i], v_refs[i], outs[i], outs[k + i], outs[2 * k + i], outs[3 * k + i])

    vm = pl.BlockSpec(memory_space=pltpu.VMEM)
    res = pl.pallas_call(
        body, name=name, out_shape=tuple(jax.ShapeDtypeStruct(w.shape, f32) for _ in range(4) for w in ws),
        in_specs=[vm] * (4 * k), out_specs=(vm,) * (4 * k), compiler_params=_cp(),
    )(*parts, *ws, *ms, *vs)
    return [res[i * k:(i + 1) * k] for i in range(4)]


def _adam(parts, w, m, v, rows, name):
    p, r, c = parts.shape
    lead = w.ndim - 2

    def body(g_ref, w_ref, m_ref, v_ref, go_ref, d_ref, mo_ref, vo_ref):
        g = g_ref[0].astype(f32)
        for i in range(1, p):
            g = g + g_ref[i].astype(f32)
        _adam_update(g, w_ref, m_ref, v_ref, go_ref, d_ref, mo_ref, vo_ref)

    blk = pl.BlockSpec((None,) * lead + (rows, c), lambda i: (0,) * lead + (i, 0))
    return pl.pallas_call(
        body, name=name, grid=(r // rows,), out_shape=tuple(jax.ShapeDtypeStruct(w.shape, f32) for _ in range(4)),
        in_specs=[pl.BlockSpec((p, rows, c), lambda i: (0, i, 0)), blk, blk, blk], out_specs=(blk, blk, blk, blk),
        compiler_params=_cp(("arbitrary",)),
    )(parts, w, m, v)


def _in_proj(x, g, scale, shift, w_pad, tl, comm):
    l = x.shape[0]
    c_in, c_out, c_sems = comm.specs()

    def body(*refs):
        x_ref, g_ref, sc_ref, sh_ref, w_ref = refs[:5]
        cin = refs[5:5 + len(c_in)]
        p_ref, h_ref = refs[5 + len(c_in):7 + len(c_in)]
        cout = refs[7 + len(c_in):7 + len(c_in) + comm.n]
        sems = refs[7 + len(c_in) + comm.n:]

        def compute():
            xv = x_ref[...]
            r = lax.rsqrt(jnp.mean(xv * xv, axis=-1, keepdims=True) + EPS)
            h = (xv * r) * (g_ref[...] * (1.0 + sc_ref[...])) + sh_ref[...]
            hb = h.astype(bf16)
            h_ref[...] = hb
            p_ref[...] = jnp.dot(hb, w_ref[...], preferred_element_type=f32).astype(bf16)

        comm.run(cin, cout, sems, l // tl, compute)

    vec = pl.BlockSpec((1, D), lambda i: (0, 0))
    return pl.pallas_call(
        body, name="in_proj", grid=(l // tl,),
        out_shape=(jax.ShapeDtypeStruct((l, PW), bf16), jax.ShapeDtypeStruct((l, D), bf16)) + tuple(comm.outs),
        in_specs=[pl.BlockSpec((tl, D), lambda i: (i, 0)), vec, vec, vec, pl.BlockSpec((D, PW), lambda i: (0, 0))] + c_in,
        out_specs=(pl.BlockSpec((tl, PW), lambda i: (i, 0)), pl.BlockSpec((tl, D), lambda i: (i, 0))) + tuple(c_out),
        scratch_shapes=c_sems, compiler_params=_cp(("arbitrary",)),
    )(x, g, scale, shift, w_pad, *comm.ins)


def _tri(rev):
    i = lax.broadcasted_iota(jnp.int32, (CH, CH), 0)
    j = lax.broadcasted_iota(jnp.int32, (CH, CH), 1)
    return jnp.where((j >= i) if rev else (j <= i), 1.0, 0.0).astype(f32)


def _head_masks():
    lane = lax.broadcasted_iota(jnp.int32, (1, KW), 1) // HK
    return [jnp.where(lane == h, 1.0, 0.0).astype(f32) for h in range(NH)]


def _block_diag():
    r = lax.broadcasted_iota(jnp.int32, (VW, KW), 0) // HV
    c = lax.broadcasted_iota(jnp.int32, (VW, KW), 1) // HK
    return jnp.where(r == c, 1.0, 0.0).astype(f32)


def _decay(lr, w2, gb, tri, rev):
    logits = _dot(lr, w2) + gb
    a = _log_sigmoid(logits) * (1.0 / TAU)
    c = _dot_hi(tri, a)
    cl = c[0:1, :] if rev else c[CH - 1:CH, :]
    return logits, c, cl


def _state_fwd(k, v, c, cl, st, bd):
    return st * jnp.exp(cl) + bd * _tn(v, k * jnp.exp(cl - c))


def _state_bwd(k, v, c, cl, st0, dst, trit):
    ecl = jnp.exp(cl)
    edec = jnp.exp(cl - c)
    kdec = k * edec
    dv = _nt(kdec, dst)
    dkdec = _dot(v, dst)
    dcl = jnp.sum(dst * st0, axis=0, keepdims=True) * ecl + jnp.sum(dkdec * kdec, axis=0, keepdims=True)
    da = _dot_hi(trit, -dkdec * kdec) + dcl
    return dkdec * edec, dv, da, dst * ecl


def _bdot(a, b):
    return lax.dot_general(a.astype(bf16), b.astype(bf16), (((2,), (1,)), ((0,), (0,))), preferred_element_type=f32)


def _bnt(a, b):
    return lax.dot_general(a.astype(bf16), b.astype(bf16), (((2,), (2,)), ((0,), (0,))), preferred_element_type=f32)


def _btn(a, b):
    return lax.dot_general(a.astype(bf16), b.astype(bf16), (((1,), (1,)), ((0,), (0,))), preferred_element_type=f32)


def _scan_chunks(x, rev):
    nc = x.shape[0]
    hi = x.astype(bf16)
    r1 = x - hi.astype(f32)
    mid = r1.astype(bf16)
    lo = (r1 - mid.astype(f32)).astype(bf16)
    terms = jnp.concatenate([hi, mid, lo], axis=1)
    tri3 = jnp.broadcast_to(jnp.concatenate([_tri(rev)] * 3, axis=1).astype(bf16)[None], (nc, CH, 3 * CH))
    return lax.dot_general(tri3, terms, (((2,), (1,)), ((0,), (0,))), preferred_element_type=f32)


class _Tile:
    pass


def _tile_prep(q_ref, k_ref, lr_ref, w2_ref, gb_ref, rev, nc):
    t = _Tile()
    tg = nc * CH
    t.logits = _dot(lr_ref[...], w2_ref[...]) + gb_ref[...]
    c = _scan_chunks((_log_sigmoid(t.logits) * (1.0 / TAU)).reshape(nc, CH, KW), rev)
    cl = c[:, 0:1, :] if rev else c[:, CH - 1:CH, :]
    k = k_ref[...].astype(f32).reshape(nc, CH, KW)
    t.ec, t.enc, t.edec, t.ecl = jnp.exp(c), jnp.exp(-c), jnp.exp(cl - c), jnp.exp(cl)
    t.qd = q_ref[...].astype(f32).reshape(nc, CH, KW) * t.ec * QSCALE
    t.kd = k * t.enc
    t.kdec = k * t.edec
    hm = _head_masks()
    t.tri4 = jnp.concatenate([_tri(rev)] * NH, axis=0)[None]
    t.qs = jnp.concatenate([t.qd * hm[h] for h in range(NH)], axis=1)
    t.pst = _bnt(t.qs, t.kd) * t.tri4
    return t


def _gla_fwd(p, w2p, gb, s0, rev, tg, name):
    l = p.shape[0]
    nb, nc = l // tg, tg // CH

    def body(q_ref, k_ref, v_ref, lr_ref, w2_ref, gb_ref, s0_ref, o_ref, st_ref, st):
        @pl.when(pl.program_id(0) == 0)
        def _():
            st[...] = s0_ref[...]

        t = _tile_prep(q_ref, k_ref, lr_ref, w2_ref, gb_ref, rev, nc)
        v = v_ref[...].reshape(nc, CH, VW)
        intra = jnp.concatenate([_bdot(t.pst[:, h * CH:(h + 1) * CH], v[:, :, h * HV:(h + 1) * HV]) for h in range(NH)], axis=2)
        kv = _btn(v, t.kdec) * _block_diag()[None]
        s = st[...]
        for n in (range(nc - 1, -1, -1) if rev else range(nc)):
            st_ref[n] = s.astype(bf16)
            s = s * t.ecl[n] + kv[n]
        st[...] = s
        o_ref[...] = (_bnt(t.qd, st_ref[...]) + intra).reshape(tg, VW).astype(bf16)

    blk = (lambda i: nb - 1 - i) if rev else (lambda i: i)
    return pl.pallas_call(
        body, name=name, grid=(nb,),
        out_shape=(jax.ShapeDtypeStruct((l, VW), bf16), jax.ShapeDtypeStruct((l // CH, VW, KW), bf16)),
        in_specs=[pl.BlockSpec((tg, KW), lambda i: (blk(i), PQ // KW)), pl.BlockSpec((tg, KW), lambda i: (blk(i), PK // KW)),
                  pl.BlockSpec((tg, VW), lambda i: (blk(i), PV // VW)), pl.BlockSpec((tg, LRW), lambda i: (blk(i), PLR // LRW)),
                  pl.BlockSpec((LRW, KW), lambda i: (0, 0)), pl.BlockSpec((1, KW), lambda i: (0, 0)),
                  pl.BlockSpec((VW, KW), lambda i: (0, 0))],
        out_specs=(pl.BlockSpec((tg, VW), lambda i: (blk(i), 0)), pl.BlockSpec((nc, VW, KW), lambda i: (blk(i), 0, 0))),
        scratch_shapes=[pltpu.VMEM((VW, KW), f32)],
        compiler_params=_cp(("arbitrary",)),
    )(p, p, p, p, w2p, gb, s0)


def _gla_bwd(p, do, states, w2p, gb, prev, rev, tg, name, comm):
    l = p.shape[0]
    nb, nc = l // tg, tg // CH
    out_dt = bf16
    c_in, c_out, c_sems = comm.specs()

    def body(*refs):
        q_ref, k_ref, v_ref, lr_ref, do_ref, st_ref, w2_ref, gb_ref = refs[:8]
        refs = refs[8:]
        if prev is not None:
            pq_ref, pl_ref = refs[:2]
            refs = refs[2:]
        cin, refs = refs[:len(c_in)], refs[len(c_in):]
        dqkv_ref, dlr_ref, dw2_ref, dgb_ref, ds0_ref = refs[:5]
        cout, dst, ds_buf, sems = refs[5:5 + comm.n], refs[5 + comm.n], refs[6 + comm.n], refs[7 + comm.n:]
        comm.run(cin, cout, sems, nb, lambda: compute(q_ref, k_ref, v_ref, lr_ref, do_ref, st_ref, w2_ref, gb_ref,
                                                      pq_ref if prev is not None else None, pl_ref if prev is not None else None,
                                                      dqkv_ref, dlr_ref, dw2_ref, dgb_ref, ds0_ref, dst, ds_buf))

    def compute(q_ref, k_ref, v_ref, lr_ref, do_ref, st_ref, w2_ref, gb_ref, pq_ref, pl_ref,
                dqkv_ref, dlr_ref, dw2_ref, dgb_ref, ds0_ref, dst, ds_buf):
        @pl.when(pl.program_id(0) == 0)
        def _():
            dst[...] = jnp.zeros_like(dst)
            dw2_ref[...] = jnp.zeros_like(dw2_ref)
            dgb_ref[...] = jnp.zeros_like(dgb_ref)

        t = _tile_prep(q_ref, k_ref, lr_ref, w2_ref, gb_ref, rev, nc)
        hm = _head_masks()
        v = v_ref[...].reshape(nc, CH, VW)
        do = do_ref[...].reshape(nc, CH, VW)
        heads = lambda a, h: a[:, :, h * HV:(h + 1) * HV]
        dpst = jnp.concatenate([_bnt(heads(do, h), heads(v, h)) for h in range(NH)], axis=1) * t.tri4
        dv = jnp.concatenate([_btn(t.pst[:, h * CH:(h + 1) * CH], heads(do, h)) for h in range(NH)], axis=2)
        dqd = _bdot(do, st_ref[...])
        for h in range(NH):
            dqd = dqd + hm[h] * _bdot(dpst[:, h * CH:(h + 1) * CH], t.kd)
        dkd = _btn(dpst, t.qs)
        u = _btn(do, t.qd) * _block_diag()[None]
        d = dst[...]
        for n in (range(nc) if rev else range(nc - 1, -1, -1)):
            ds_buf[n] = d
            d = d * t.ecl[n] + u[n]
        dst[...] = d
        ds0_ref[...] = d
        ds = ds_buf[...]
        dv = dv + _bnt(t.kdec, ds)
        dkdec = _bdot(v, ds)
        dcl = jnp.sum(ds * st_ref[...].astype(f32), axis=1, keepdims=True) * t.ecl + jnp.sum(dkdec * t.kdec, axis=1, keepdims=True)
        dc = dqd * t.qd - dkd * t.kd - dkdec * t.kdec
        da = _scan_chunks(dc, not rev) + dcl
        dq = dqd * t.ec * QSCALE
        dk = dkd * t.enc + dkdec * t.edec
        dlog = da.reshape(tg, KW) * _sigmoid(-t.logits) * (1.0 / TAU)
        dlr = _nt(dlog, w2_ref[...])
        dw2_ref[...] += _tn(lr_ref[...], dlog)
        dgb_ref[...] += jnp.sum(dlog, axis=0, keepdims=True)
        dqkv = jnp.concatenate([dq, dk, dv], axis=2).reshape(tg, 2 * KW + VW)
        if prev is not None:
            dqkv = dqkv + pq_ref[...]
            dlr = dlr + pl_ref[...]
        dqkv_ref[...] = dqkv.astype(out_dt)
        dlr_ref[...] = dlr.astype(out_dt)

    blk = (lambda i: i) if rev else (lambda i: nb - 1 - i)
    in_specs = [pl.BlockSpec((tg, KW), lambda i: (blk(i), PQ // KW)), pl.BlockSpec((tg, KW), lambda i: (blk(i), PK // KW)),
                pl.BlockSpec((tg, VW), lambda i: (blk(i), PV // VW)), pl.BlockSpec((tg, LRW), lambda i: (blk(i), PLR // LRW)),
                pl.BlockSpec((tg, VW), lambda i: (blk(i), 0)), pl.BlockSpec((nc, VW, KW), lambda i: (blk(i), 0, 0)),
                pl.BlockSpec((LRW, KW), lambda i: (0, 0)), pl.BlockSpec((1, KW), lambda i: (0, 0))]
    args = [p, p, p, p, do, states, w2p, gb]
    if prev is not None:
        in_specs += [pl.BlockSpec((tg, 2 * KW + VW), lambda i: (blk(i), 0)), pl.BlockSpec((tg, LRW), lambda i: (blk(i), 0))]
        args += list(prev)
    return pl.pallas_call(
        body, name=name, grid=(nb,),
        out_shape=(jax.ShapeDtypeStruct((l, 2 * KW + VW), out_dt), jax.ShapeDtypeStruct((l, LRW), out_dt),
                   jax.ShapeDtypeStruct((LRW, KW), f32), jax.ShapeDtypeStruct((1, KW), f32), jax.ShapeDtypeStruct((VW, KW), f32))
        + tuple(comm.outs),
        in_specs=in_specs + c_in,
        out_specs=(pl.BlockSpec((tg, 2 * KW + VW), lambda i: (blk(i), 0)), pl.BlockSpec((tg, LRW), lambda i: (blk(i), 0)),
                   pl.BlockSpec((LRW, KW), lambda i: (0, 0)), pl.BlockSpec((1, KW), lambda i: (0, 0)),
                   pl.BlockSpec((VW, KW), lambda i: (0, 0))) + tuple(c_out),
        scratch_shapes=[pltpu.VMEM((VW, KW), f32), pltpu.VMEM((nc, VW, KW), f32)] + c_sems,
        compiler_params=_cp(("arbitrary",)),
    )(*args, *comm.ins)


def _ctx_hidden(ctx_ref, g_ref, sc_ref, sh_ref):
    xv = ctx_ref[...]
    r = lax.rsqrt(jnp.mean(xv * xv, axis=-1, keepdims=True) + EPS)
    xn = xv * r
    return xn, xn * (g_ref[...] * (1.0 + sc_ref[...])) + sh_ref[...]


_CTX_W_SPECS = [pl.BlockSpec((D, KW), lambda i: (0, PK // KW)), pl.BlockSpec((D, VW), lambda i: (0, PV // VW)),
                pl.BlockSpec((D, LRW), lambda i: (0, PLR // LRW))]


def _ctx_fwd(ctx, g, scale, shift, w_pad, w2f, w2b, gbf, gbb):
    ncc = CTX // CH

    def body(ctx_ref, g_ref, sc_ref, sh_ref, wk_ref, wv_ref, wl_ref, w2f_ref, w2b_ref, gbf_ref, gbb_ref, sf_ref, sb_ref):
        _, hc = _ctx_hidden(ctx_ref, g_ref, sc_ref, sh_ref)
        k, v, lr = _dot(hc, wk_ref[...]), _dot(hc, wv_ref[...]), _dot(hc, wl_ref[...])
        bd = _block_diag()
        for rev, w2_ref, gb_ref, out in ((False, w2f_ref, gbf_ref, sf_ref), (True, w2b_ref, gbb_ref, sb_ref)):
            tri = _tri(rev)
            st = jnp.zeros((VW, KW), f32)
            for j in (range(ncc - 1, -1, -1) if rev else range(ncc)):
                rows = slice(j * CH, (j + 1) * CH)
                _, c, cl = _decay(lr[rows], w2_ref[...], gb_ref[...], tri, rev)
                st = _state_fwd(k[rows], v[rows], c, cl, st, bd)
            out[...] = st

    vec = pl.BlockSpec((1, D), lambda i: (0, 0))
    w2s = pl.BlockSpec((LRW, KW), lambda i: (0, 0))
    gbs = pl.BlockSpec((1, KW), lambda i: (0, 0))
    sts = pl.BlockSpec((VW, KW), lambda i: (0, 0))
    return pl.pallas_call(
        body, name="ctx_fwd", grid=(1,), out_shape=(jax.ShapeDtypeStruct((VW, KW), f32),) * 2,
        in_specs=[pl.BlockSpec((CTX, D), lambda i: (0, 0)), vec, vec, vec] + _CTX_W_SPECS + [w2s, w2s, gbs, gbs],
        out_specs=(sts, sts), compiler_params=_cp(("arbitrary",)),
    )(ctx, g, scale, shift, w_pad, w_pad, w_pad, w2f, w2b, gbf, gbb)


def _ctx_bwd(ctx, g, scale, shift, w_pad, w2f, w2b, gbf, gbb, dsf, dsb):
    ncc = CTX // CH

    def body(ctx_ref, g_ref, sc_ref, sh_ref, wk_ref, wv_ref, wl_ref, w2f_ref, w2b_ref, gbf_ref, gbb_ref, dsf_ref, dsb_ref,
             dwk_ref, dwv_ref, dwl_ref, dmod_ref, dg_ref, dw2_ref, dgb_ref):
        xn, hc = _ctx_hidden(ctx_ref, g_ref, sc_ref, sh_ref)
        k, v, lr = _dot(hc, wk_ref[...]), _dot(hc, wv_ref[...]), _dot(hc, wl_ref[...])
        bd = _block_diag()
        dk_rows, dv_rows, dl_rows = [None] * ncc, [None] * ncc, [None] * ncc
        for d, (rev, w2_ref, gb_ref, ds_ref) in enumerate(((False, w2f_ref, gbf_ref, dsf_ref), (True, w2b_ref, gbb_ref, dsb_ref))):
            tri = _tri(rev)
            order = list(range(ncc - 1, -1, -1) if rev else range(ncc))
            st, saved = jnp.zeros((VW, KW), f32), {}
            for j in order:
                rows = slice(j * CH, (j + 1) * CH)
                logits, c, cl = _decay(lr[rows], w2_ref[...], gb_ref[...], tri, rev)
                saved[j] = (logits, c, cl, st)
                st = _state_fwd(k[rows], v[rows], c, cl, st, bd)
            dst = ds_ref[...]
            dw2 = jnp.zeros((LRW, KW), f32)
            dgb = jnp.zeros((1, KW), f32)
            for j in reversed(order):
                rows = slice(j * CH, (j + 1) * CH)
                logits, c, cl, st0 = saved[j]
                dk, dv, da, dst = _state_bwd(k[rows], v[rows], c, cl, st0, dst, _tri(not rev))
                dlog = da * _sigmoid(-logits) * (1.0 / TAU)
                dl = _nt(dlog, w2_ref[...])
                dw2 = dw2 + _tn(lr[rows], dlog)
                dgb = dgb + jnp.sum(dlog, axis=0, keepdims=True)
                dk_rows[j] = dk if dk_rows[j] is None else dk_rows[j] + dk
                dv_rows[j] = dv if dv_rows[j] is None else dv_rows[j] + dv
                dl_rows[j] = dl if dl_rows[j] is None else dl_rows[j] + dl
            dw2_ref[d] = dw2
            dgb_ref[d] = dgb
        dk, dv, dl = (jnp.concatenate(t, axis=0) for t in (dk_rows, dv_rows, dl_rows))
        dwk_ref[...] = _tn(hc, dk)
        dwv_ref[...] = _tn(hc, dv)
        dwl_ref[...] = _tn(hc, dl)
        dh = _nt(dk, wk_ref[...]) + _nt(dv, wv_ref[...]) + _nt(dl, wl_ref[...])
        gx = dh * xn
        dmod_ref[:, 0:D] = jnp.sum(dh, axis=0, keepdims=True)
        dmod_ref[:, D:2 * D] = jnp.sum(gx, axis=0, keepdims=True) * g_ref[...]
        dg_ref[...] = jnp.sum(gx, axis=0, keepdims=True) * (1.0 + sc_ref[...])

    vec = pl.BlockSpec((1, D), lambda i: (0, 0))
    w2s = pl.BlockSpec((LRW, KW), lambda i: (0, 0))
    gbs = pl.BlockSpec((1, KW), lambda i: (0, 0))
    sts = pl.BlockSpec((VW, KW), lambda i: (0, 0))
    full = lambda *s: pl.BlockSpec(s, lambda i: (0,) * len(s))
    return pl.pallas_call(
        body, name="ctx_bwd", grid=(1,),
        out_shape=(jax.ShapeDtypeStruct((D, KW), f32), jax.ShapeDtypeStruct((D, VW), f32), jax.ShapeDtypeStruct((D, LRW), f32),
                   jax.ShapeDtypeStruct((1, 2 * D), f32), jax.ShapeDtypeStruct((1, D), f32),
                   jax.ShapeDtypeStruct((2, LRW, KW), f32), jax.ShapeDtypeStruct((2, 1, KW), f32)),
        in_specs=[pl.BlockSpec((CTX, D), lambda i: (0, 0)), vec, vec, vec] + _CTX_W_SPECS + [w2s, w2s, gbs, gbs, sts, sts],
        out_specs=(full(D, KW), full(D, VW), full(D, LRW), full(1, 2 * D), full(1, D), full(2, LRW, KW), full(2, 1, KW)),
        compiler_params=_cp(("arbitrary",)),
    )(ctx, g, scale, shift, w_pad, w_pad, w_pad, w2f, w2b, gbf, gbb, dsf, dsb)


def _layernorm(va, g, b):
    mu = jnp.mean(va, axis=-1, keepdims=True)
    xc = va - mu
    rstd = lax.rsqrt(jnp.mean(xc * xc, axis=-1, keepdims=True) + EPS)
    vhat = xc * rstd
    return vhat, rstd, vhat * g + b


def _mix_fwd(p, ln_g, ln_b, ws, bs_t):
    l = p.shape[0]
    nch = l // ACH
    half = AW // 2

    def body(p_ref, g_ref, b_ref, ws_ref, bs_ref, sv_ref, va_buf, col_buf, sem):
        cp = pltpu.make_async_copy(p_ref.at[:, pl.ds(PVA, AW)], va_buf, sem)
        cp.start()
        cp.wait()

        def rows_step(n, carry):
            rows = pl.ds(pl.multiple_of(n * ACH, ACH), ACH)
            _, _, vn = _layernorm(va_buf[rows, :].astype(f32), g_ref[...], b_ref[...])
            for gi in range(2):
                sl = slice(gi * ACH, (gi + 1) * ACH)
                sv_ref[rows, sl] = (_dot(ws_ref[gi], vn[:, sl]) + bs_ref[:, gi:gi + 1]).astype(bf16)
            col_buf[0, rows, :] = vn[:, half:half + ACH]
            col_buf[1, rows, :] = vn[:, half + ACH:]
            return carry

        lax.fori_loop(0, nch, rows_step, 0, unroll=8)

        def cols_step(cidx, carry):
            rows = pl.ds(cidx, ACH, stride=GW)
            for gi in range(2, 4):
                col_buf[gi - 2, rows, :] = _dot(ws_ref[gi], col_buf[gi - 2, rows, :]) + bs_ref[:, gi:gi + 1]
            return carry

        lax.fori_loop(0, GW, cols_step, 0, unroll=8)

        def out_step(n, carry):
            rows = pl.ds(pl.multiple_of(n * ACH, ACH), ACH)
            sv_ref[rows, half:half + ACH] = col_buf[0, rows, :].astype(bf16)
            sv_ref[rows, half + ACH:] = col_buf[1, rows, :].astype(bf16)
            return carry

        lax.fori_loop(0, nch, out_step, 0, unroll=8)

    vm = pl.BlockSpec(memory_space=pltpu.VMEM)
    return pl.pallas_call(
        body, name="mix_fwd", out_shape=jax.ShapeDtypeStruct((l, AW), bf16),
        in_specs=[pl.BlockSpec(memory_space=pl.ANY), vm, vm, vm, vm], out_specs=vm,
        scratch_shapes=[pltpu.VMEM((l, AW), bf16), pltpu.VMEM((2, l, ACH), f32), pltpu.SemaphoreType.DMA(())],
        compiler_params=_cp(),
    )(p, ln_g, ln_b, ws, bs_t)


def _mix_bwd(p, dsv, ln_g, ln_b, ws_t):
    l = p.shape[0]
    nch = l // ACH
    half = AW // 2

    def body(p_ref, dsv_hbm, g_ref, b_ref, wst_ref, dva_ref, dws_ref, dbs_ref, dg_ref, db_ref, va_buf, dsv_buf, vn_col, ds_col, sems):
        cp1 = pltpu.make_async_copy(p_ref.at[:, pl.ds(PVA, AW)], va_buf, sems.at[0])
        cp2 = pltpu.make_async_copy(dsv_hbm, dsv_buf, sems.at[1])
        cp1.start()
        cp2.start()
        dws_ref[...] = jnp.zeros_like(dws_ref)
        dbs_ref[...] = jnp.zeros_like(dbs_ref)
        dg_ref[...] = jnp.zeros_like(dg_ref)
        db_ref[...] = jnp.zeros_like(db_ref)
        cp1.wait()
        cp2.wait()

        def rows_step(n, carry):
            rows = pl.ds(pl.multiple_of(n * ACH, ACH), ACH)
            _, _, vn = _layernorm(va_buf[rows, :].astype(f32), g_ref[...], b_ref[...])
            ds = dsv_buf[rows, :].astype(f32)
            for gi in range(2):
                sl = slice(gi * ACH, (gi + 1) * ACH)
                dws_ref[gi] += _nt(ds[:, sl], vn[:, sl])
                dbs_ref[gi] += ds[:, sl]
            for gi in range(2):
                sl = slice(half + gi * ACH, half + (gi + 1) * ACH)
                vn_col[gi, rows, :] = vn[:, sl]
                ds_col[gi, rows, :] = ds[:, sl]
            return carry

        lax.fori_loop(0, nch, rows_step, 0, unroll=8)

        def cols_step(cidx, carry):
            rows = pl.ds(cidx, ACH, stride=GW)
            for gi in range(2, 4):
                ds = ds_col[gi - 2, rows, :]
                dws_ref[gi] += _nt(ds, vn_col[gi - 2, rows, :])
                dbs_ref[gi] += ds
                ds_col[gi - 2, rows, :] = _dot(wst_ref[gi], ds)
            return carry

        lax.fori_loop(0, GW, cols_step, 0, unroll=8)

        def out_step(n, carry):
            rows = pl.ds(pl.multiple_of(n * ACH, ACH), ACH)
            vhat, rstd, _ = _layernorm(va_buf[rows, :].astype(f32), g_ref[...], b_ref[...])
            ds = dsv_buf[rows, :].astype(f32)
            dvn = jnp.concatenate([_dot(wst_ref[0], ds[:, 0:ACH]), _dot(wst_ref[1], ds[:, ACH:half]), ds_col[0, rows, :], ds_col[1, rows, :]], axis=1)
            dg_ref[...] += jnp.sum(dvn * vhat, axis=0, keepdims=True)
            db_ref[...] += jnp.sum(dvn, axis=0, keepdims=True)
            dvh = dvn * g_ref[...]
            dva = rstd * (dvh - jnp.mean(dvh, axis=-1, keepdims=True) - vhat * jnp.mean(dvh * vhat, axis=-1, keepdims=True))
            dva_ref[rows, :] = dva.astype(bf16)
            return carry

        lax.fori_loop(0, nch, out_step, 0, unroll=8)

    vm = pl.BlockSpec(memory_space=pltpu.VMEM)
    hbm = pl.BlockSpec(memory_space=pl.ANY)
    return pl.pallas_call(
        body, name="mix_bwd",
        out_shape=(jax.ShapeDtypeStruct((l, AW), bf16), jax.ShapeDtypeStruct((4, ACH, ACH), f32), jax.ShapeDtypeStruct((4, ACH, ACH), f32),
                   jax.ShapeDtypeStruct((1, AW), f32), jax.ShapeDtypeStruct((1, AW), f32)),
        in_specs=[hbm, hbm, vm, vm, vm], out_specs=(vm, vm, vm, vm, vm),
        scratch_shapes=[pltpu.VMEM((l, AW), bf16), pltpu.VMEM((l, AW), bf16), pltpu.VMEM((2, l, ACH), f32), pltpu.VMEM((2, l, ACH), f32),
                        pltpu.SemaphoreType.DMA((2,))],
        compiler_params=_cp(),
    )(p, dsv, ln_g, ln_b, ws_t)


def _mid(x, tgt, p, o_f, o_b, sv, gate, gf, gb_norm, w_pa, w_pb, w_out, tl):
    l = x.shape[0]

    def body(x_ref, t_ref, zb_ref, ua_ref, za_ref, g1_ref, g2_ref, of_ref, ob_ref, sv_ref, gate_ref, gf_ref, gbn_ref,
             wpa_ref, wpb_ref, wout_ref,
             dx1_ref, dzbua_ref, dzag_ref, dsv_ref, do_ref, dwout_bf, dwpa_bf, dwpb_bf, dgf_ref, dgate_ref, dgbn_ref, loss_ref,
             dwout_ref, dwpa_ref, dwpb_ref):
        @pl.when(pl.program_id(0) == 0)
        def _():
            for r in (dwout_ref, dwpa_ref, dwpb_ref, dgf_ref, dgate_ref, dgbn_ref, loss_ref):
                r[...] = jnp.zeros_like(r)

        o = of_ref[...].astype(f32) + ob_ref[...].astype(f32)
        rr = jnp.concatenate(
            [jnp.broadcast_to(lax.rsqrt(jnp.mean(o[:, h * HV:(h + 1) * HV] ** 2, axis=-1, keepdims=True) + EPS), (tl, HV))
             for h in range(NH)], axis=1)
        ohat = o * rr
        on = ohat * gbn_ref[...]
        szb, dszb = _silu_and_grad(zb_ref[...].astype(f32))
        tb = on * szb
        u = ua_ref[...].astype(f32)
        svv = sv_ref[...].astype(f32)
        sza, dsza = _silu_and_grad(za_ref[...].astype(f32))
        ta = u * svv * sza
        ya = _dot(ta, wpa_ref[...])
        yb = _dot(tb, wpb_ref[...])
        g1 = _sigmoid(g1_ref[...].astype(f32))
        g2 = _sigmoid(g2_ref[...].astype(f32))
        m = g1 * ya + g2 * yb
        y2 = _dot(m, wout_ref[...])
        x1 = x_ref[...] + gate_ref[...] * y2
        r1 = lax.rsqrt(jnp.mean(x1 * x1, axis=-1, keepdims=True) + EPS)
        x1n = x1 * r1
        err = x1n * gf_ref[...] - t_ref[...]
        loss_ref[...] += jnp.sum(jnp.sum(err * err, axis=-1, keepdims=True), axis=0, keepdims=True) * (0.5 / D)
        dout = err * (1.0 / D)
        dgf_ref[...] += jnp.sum(dout * x1n, axis=0, keepdims=True)
        dx1n = dout * gf_ref[...]
        dx1 = r1 * (dx1n - x1n * jnp.mean(dx1n * x1n, axis=-1, keepdims=True))
        dx1_ref[...] = dx1
        dgate_ref[...] += jnp.sum(dx1 * y2, axis=0, keepdims=True)
        dy2 = dx1 * gate_ref[...]
        dwout_ref[...] += _tn(m, dy2)
        dm = _nt(dy2, wout_ref[...])
        dya = dm * g1
        dyb = dm * g2
        dzag_ref[:, AW:AW + D] = (dm * ya * g1 * (1.0 - g1)).astype(bf16)
        dzag_ref[:, AW + D:] = (dm * yb * g2 * (1.0 - g2)).astype(bf16)
        dwpa_ref[...] += _tn(ta, dya)
        dta = _nt(dya, wpa_ref[...])
        dzbua_ref[:, AW:] = (dta * svv * sza).astype(bf16)
        dsv_ref[...] = (dta * u * sza).astype(bf16)
        dzag_ref[:, 0:AW] = (dta * u * svv * dsza).astype(bf16)
        dwpb_ref[...] += _tn(tb, dyb)
        dtb = _nt(dyb, wpb_ref[...])
        don = dtb * szb
        dzbua_ref[:, 0:AW] = (dtb * on * dszb).astype(bf16)
        dgbn_ref[...] += jnp.sum(don * ohat, axis=0, keepdims=True)
        doh = don * gbn_ref[...]
        prod = doh * ohat
        mh = jnp.concatenate(
            [jnp.broadcast_to(jnp.mean(prod[:, h * HV:(h + 1) * HV], axis=-1, keepdims=True), (tl, HV)) for h in range(NH)], axis=1)
        do_ref[...] = (rr * (doh - ohat * mh)).astype(bf16)

        @pl.when(pl.program_id(0) == l // tl - 1)
        def _():
            for acc, out in ((dwout_ref, dwout_bf), (dwpa_ref, dwpa_bf), (dwpb_ref, dwpb_bf)):
                out[...] = acc[...].astype(bf16)

    row = lambda w, j: pl.BlockSpec((tl, w), lambda i, j=j: (i, j))
    full = lambda *s: pl.BlockSpec(s, lambda i: (0,) * len(s))
    return pl.pallas_call(
        body, name="mid", grid=(l // tl,),
        out_shape=(jax.ShapeDtypeStruct((l, D), f32), jax.ShapeDtypeStruct((l, 2 * AW), bf16), jax.ShapeDtypeStruct((l, AW + 2 * D), bf16),
                   jax.ShapeDtypeStruct((l, AW), bf16), jax.ShapeDtypeStruct((l, VW), bf16),
                   jax.ShapeDtypeStruct((D, D), bf16), jax.ShapeDtypeStruct((AW, D), bf16), jax.ShapeDtypeStruct((VW, D), bf16),
                   jax.ShapeDtypeStruct((1, D), f32), jax.ShapeDtypeStruct((1, D), f32), jax.ShapeDtypeStruct((1, VW), f32),
                   jax.ShapeDtypeStruct((1, 1), f32)),
        scratch_shapes=[pltpu.VMEM((D, D), f32), pltpu.VMEM((AW, D), f32), pltpu.VMEM((VW, D), f32)],
        in_specs=[row(D, 0), row(D, 0), row(AW, PZB // AW), row(AW, PUA // AW), row(AW, PZA // AW), row(D, PG1 // D), row(D, PG2 // D),
                  row(VW, 0), row(VW, 0), row(AW, 0), full(1, D), full(1, D), full(1, VW), full(AW, D), full(VW, D), full(D, D)],
        out_specs=(row(D, 0), row(2 * AW, 0), row(AW + 2 * D, 0), row(AW, 0), row(VW, 0),
                   full(D, D), full(AW, D), full(VW, D), full(1, D), full(1, D), full(1, VW), full(1, 1)),
        compiler_params=_cp(("arbitrary",)),
    )(x, tgt, p, p, p, p, p, o_f, o_b, sv, gate, gf, gb_norm, w_pa, w_pb, w_out)


def _in_bwd(x, dx1, dqkv, dzbua, dva, dzag, dlr, w_pad, g, scale, tl, comm):
    l = x.shape[0]
    c_in, c_out, c_sems = comm.specs()

    def body(*refs):
        cin = refs[14:14 + len(c_in)]
        outs = refs[14 + len(c_in):]
        comm.run(cin, outs[4:4 + comm.n], outs[4 + comm.n:], l // tl, lambda: compute(*refs[:14], *outs[:4]))

    def compute(x_ref, dx1_ref, a_ref, b_ref, c_ref, e_ref, lr_ref, wa_ref, wb_ref, wc_ref, we_ref, wl_ref, g_ref, sc_ref,
                gx_ref, dsh_ref, dsc_ref, dg_ref):
        @pl.when(pl.program_id(0) == 0)
        def _():
            for r in (dsh_ref, dsc_ref, dg_ref):
                r[...] = jnp.zeros_like(r)

        dh = (_nt(a_ref[...], wa_ref[...]) + _nt(b_ref[...], wb_ref[...]) + _nt(c_ref[...], wc_ref[...]) + _nt(e_ref[...], we_ref[...])
              + _nt(lr_ref[...], wl_ref[...]))
        xv = x_ref[...]
        r = lax.rsqrt(jnp.mean(xv * xv, axis=-1, keepdims=True) + EPS)
        xn = xv * r
        gxn = jnp.sum(dh * xn, axis=0, keepdims=True)
        dsh_ref[...] += jnp.sum(dh, axis=0, keepdims=True)
        dsc_ref[...] += gxn * g_ref[...]
        dg_ref[...] += gxn * (1.0 + sc_ref[...])
        dxn = dh * (g_ref[...] * (1.0 + sc_ref[...]))
        gx_ref[...] = dx1_ref[...] + r * (dxn - xn * jnp.mean(dxn * xn, axis=-1, keepdims=True))

    row = lambda w: pl.BlockSpec((tl, w), lambda i: (i, 0))
    wcol = lambda w, j: pl.BlockSpec((D, w), lambda i, j=j: (0, j))
    vec = pl.BlockSpec((1, D), lambda i: (0, 0))
    return pl.pallas_call(
        body, name="in_bwd", grid=(l // tl,),
        out_shape=(jax.ShapeDtypeStruct((l, D), f32),) + (jax.ShapeDtypeStruct((1, D), f32),) * 3 + tuple(comm.outs),
        in_specs=[row(D), row(D), row(2 * KW + VW), row(2 * AW), row(AW), row(AW + 2 * D), row(LRW),
                  wcol(2 * KW + VW, 0), wcol(2 * AW, PZB // (2 * AW)), wcol(AW, PVA // AW), wcol(AW + 2 * D, PZA // (AW + 2 * D)),
                  wcol(LRW, PLR // LRW), vec, vec] + c_in,
        out_specs=(row(D), vec, vec, vec) + tuple(c_out), scratch_shapes=c_sems,
        compiler_params=_cp(("arbitrary",)),
    )(x, dx1, dqkv, dzbua, dva, dzag, dlr, w_pad, w_pad, w_pad, w_pad, w_pad, g, scale, *comm.ins)


def _tn_matmul(a, bs, tl, name, comm=None):
    l, m = a.shape
    k = len(bs)
    comm = comm or _Comm([], [], None)
    c_in, c_out, c_sems = comm.specs()

    def body(a_ref, *refs):
        b_refs, cin = refs[:k], refs[k:k + len(c_in)]
        o_refs = refs[k + len(c_in):2 * k + len(c_in)]
        cout, sems = refs[2 * k + len(c_in):2 * k + len(c_in) + comm.n], refs[2 * k + len(c_in) + comm.n:]

        def compute():
            @pl.when(pl.program_id(0) == 0)
            def _():
                for o_ref in o_refs:
                    o_ref[...] = jnp.zeros_like(o_ref)

            av = a_ref[...]
            for b_ref, o_ref in zip(b_refs, o_refs):
                o_ref[...] += _tn(av, b_ref[...])

        comm.run(cin, cout, sems, l // tl, compute)

    return pl.pallas_call(
        body, name=name, grid=(l // tl,), out_shape=tuple(jax.ShapeDtypeStruct((m, b.shape[1]), f32) for b in bs) + tuple(comm.outs),
        in_specs=[pl.BlockSpec((tl, m), lambda i: (i, 0))] + [pl.BlockSpec((tl, b.shape[1]), lambda i: (i, 0)) for b in bs] + c_in,
        out_specs=tuple(pl.BlockSpec((m, b.shape[1]), lambda i: (0, 0)) for b in bs) + tuple(c_out),
        scratch_shapes=c_sems, compiler_params=_cp(("arbitrary",)),
    )(a, *bs, *comm.ins)


def _pad_gate(w2, gb):
    z = jnp.zeros((RANK, KW), f32)
    tail = jnp.zeros((LRW - 2 * RANK, KW), f32)
    w2f = jnp.concatenate([w2[0], z, tail], axis=0)
    w2b = jnp.concatenate([z, w2[1], tail], axis=0)
    return w2f, w2b, gb[0:1], gb[1:2]


EARLY_A_ROWS = 384


class _NoExchange:
    def __init__(self, w_pa, w_pb, w_out):
        self.weights = (w_pa, w_pb, w_out)

    def gather_proj(self):
        return _Comm([], [], None)

    def proj_weights(self, got):
        return self.weights

    def first(self, dw_out, dw_pa, dw_pb):
        return _Comm([], [], None)

    def early_a(self, small, blocks):
        return _Comm([], [], None)

    def early_b(self, blocks):
        return _Comm([], [], None)

    def late(self, blocks, dgt):
        return _Comm([], [], None)


class _Exchanges:
    def __init__(self, pa, pb, wo):
        self.shards = (pa, pb, wo)

    def gather_proj(self):
        def plan(i, o):
            srcs = [lambda j, r=r: r for r in i]
            dsts = [lambda j: o[0].at[:, _lanes(j)], lambda j: o[1].at[:, _lanes(j)], lambda j: o[2].at[j]]
            return srcs, dsts, None
        sds = jax.ShapeDtypeStruct
        return _Comm(self.shards, [sds((AW, D), bf16), sds((VW, D), bf16), sds((NDEV, 128, D), bf16)], plan)

    def proj_weights(self, got):
        return got[0], got[1], got[2].reshape(D, D)

    def first(self, dw_out, dw_pa, dw_pb):
        def plan(i, o):
            srcs = [lambda j: i[0].at[j], lambda j: i[1].at[:, _lanes(j)], lambda j: i[2].at[:, _lanes(j)]]
            dsts = [lambda j, r=r: r.at[j] for r in o]
            return srcs, dsts, None
        sds = jax.ShapeDtypeStruct
        return _Comm([dw_out.reshape(NDEV, 128, D), dw_pa, dw_pb],
                     [sds((NDEV, 128, D), bf16), sds((NDEV, AW, 128), bf16), sds((NDEV, VW, 128), bf16)], plan)

    def early_a(self, small, blocks):
        def plan(i, o):
            srcs = [lambda j: i[0], lambda j: i[1].at[j]]
            dsts = [lambda j, r=r: r.at[j] for r in o]
            return srcs, dsts, [None, lambda j: j >= LATE_DESTS - 1]
        sds = jax.ShapeDtypeStruct
        return _Comm([small, blocks], [sds((NDEV,) + small.shape, f32), sds(blocks.shape, bf16)], plan)

    def early_b(self, blocks):
        def plan(i, o):
            return [lambda j: i[0].at[j]], [lambda j: o[0].at[j]], [lambda j: j >= LATE_DESTS - 1]
        return _Comm([blocks], [jax.ShapeDtypeStruct(blocks.shape, bf16)], plan)

    def late(self, blocks, dgt):
        return _LateComm(blocks, dgt)


def _local_step(x, ctx, tgt, mod, modc, norm_g, w_pad, ln_g, ln_b, ws, bs, w2, gb, gb_norm, gf, xch):
    shift, scale, gate = mod[:, 0:D], mod[:, D:2 * D], mod[:, 2 * D:]
    shift_c, scale_c = modc[:, 0:D], modc[:, D:]
    w2f, w2b, gbf, gbb = _pad_gate(w2, gb)

    p, h, *got_proj = _in_proj(x, norm_g, scale, shift, w_pad, 512, xch.gather_proj())
    w_pa, w_pb, w_out = xch.proj_weights(got_proj)
    sc_f, sc_b = _ctx_fwd(ctx, norm_g, scale_c, shift_c, w_pad, w2f, w2b, gbf, gbb)
    o_f, st_f = _gla_fwd(p, w2f, gbf, sc_f, False, 512, "gla_fwd_f")
    o_b, st_b = _gla_fwd(p, w2b, gbb, sc_b, True, 512, "gla_fwd_b")
    sv = _mix_fwd(p, ln_g, ln_b, ws.astype(bf16), bs.T)
    (dx1, dzbua, dzag, dsv, do, dw_out, dw_pa, dw_pb, dgf, dgate, dgbn, loss) = _mid(
        x, tgt, p, o_f, o_b, sv, gate, gf, gb_norm, w_pa, w_pb, w_out, 256)
    dva, dws, dbs_acc, dln_g, dln_b = _mix_bwd(p, dsv, ln_g, ln_b, jnp.swapaxes(ws, 1, 2).astype(bf16))
    dw_zbua, dw_va, dw_zag, *got_first = _tn_matmul(h, [dzbua, dva, dzag], 1024, "dw_early", xch.first(dw_out, dw_pa, dw_pb))
    blocks_a = _pack_dw_early(dw_zbua, dw_va, dw_zag, 0, EARLY_A_ROWS, "pack_dw_early_a")
    blocks_b = _pack_dw_early(dw_zbua, dw_va, dw_zag, EARLY_A_ROWS, D - EARLY_A_ROWS, "pack_dw_early_b")
    small = _rows128(dln_g, dln_b, dws, jnp.sum(dbs_acc, axis=-1), dgbn, dgf, jnp.broadcast_to(loss, (1, 128)))

    dqkv_f, dlr_f, dw2f, dgbf, dsc_f, *got_a = _gla_bwd(p, do, st_f, w2f, gbf, None, False, 512, "gla_bwd_f",
                                                        xch.early_a(small, blocks_a))
    dqkv, dlr, dw2b, dgbb, dsc_b, *got_b = _gla_bwd(p, do, st_b, w2b, gbb, (dqkv_f, dlr_f), True, 512, "gla_bwd_b",
                                                    xch.early_b(blocks_b))
    dwk_c, dwv_c, dwl_c, dmodc, dg_c, dw2c, dgbc = _ctx_bwd(ctx, norm_g, scale_c, shift_c, w_pad, w2f, w2b, gbf, gbb, dsc_f, dsc_b)
    dw_qkv, dw_lr = _tn_matmul(h, [dqkv, dlr], 1024, "dw_qkv_lr")
    blocks_late = _pack_dw_late(dw_qkv, dw_lr, dwk_c, dwv_c, dwl_c)
    dw2 = jnp.stack([dw2f[0:RANK] + dw2c[0, 0:RANK], dw2b[RANK:2 * RANK] + dw2c[1, RANK:2 * RANK]])
    dgb = jnp.concatenate([dgbf + dgbc[0], dgbb + dgbc[1]], axis=0)
    dgt = jnp.concatenate([jnp.transpose(dw2.reshape(2, RANK, NDEV, 32), (2, 0, 1, 3)).reshape(NDEV, 2 * RANK * 32),
                           jnp.transpose(dgb.reshape(2, NDEV, 32), (1, 0, 2)).reshape(NDEV, 64),
                           jnp.zeros((NDEV, 64), f32)], axis=1).reshape(NDEV, 9, 128)
    gx, dshift, dscale, dg, *got_late = _in_bwd(x, dx1, dqkv, dzbua, dva, dzag, dlr, w_pad, norm_g, scale, 512,
                                                xch.late(blocks_late, dgt))
    return dict(loss=loss, gx=gx, dmod=jnp.concatenate([dshift, dscale, dgate], axis=1), dmodc=dmodc, dnorm_g=dg + dg_c,
                small=small, blocks_a=blocks_a, blocks_b=blocks_b, blocks_late=blocks_late, dw2=dw2, dgb=dgb,
                dw_pa=dw_pa, dw_pb=dw_pb, dw_out=dw_out, got_first=got_first, got_a=got_a, got_b=got_b, got_late=got_late)


def _rows128(*vs):
    out = []
    for t in vs:
        t = t.reshape(-1)
        pad = (-t.shape[0]) % 128
        out.append(jnp.pad(t, (0, pad)) if pad else t)
    return jnp.concatenate(out).reshape(-1, 128)


def kernel(x, c, ctx, c_ctx, w_mod, b_mod, norm_g, w_in, a_ln_g, a_ln_b, a_ws, a_bs, b_gate_w2, b_gate_b, b_norm_g, w_proj_a, w_proj_b, w_out, final_norm_g, loss_target, m_c_ctx, m_w_mod, m_b_mod, m_norm_g, m_w_in, m_a_ln_g, m_a_ln_b, m_a_ws, m_a_bs, m_b_gate_w2, m_b_gate_b, m_b_norm_g, m_w_proj_a, m_w_proj_b, m_w_out, m_final_norm_g, v_c_ctx, v_w_mod, v_b_mod, v_norm_g, v_w_in, v_a_ln_g, v_a_ln_b, v_a_ws, v_a_bs, v_b_gate_w2, v_b_gate_b, v_b_norm_g, v_w_proj_a, v_w_proj_b, v_w_out, v_final_norm_g):
    me = _me()
    ncol = w_mod.shape[2]

    gate_mine = _rows128(jnp.concatenate([b_gate_w2.reshape(-1), b_gate_b.reshape(-1)]))
    bm_mine = lax.dynamic_slice(b_mod, (0, me * ncol), (1, ncol))
    cs, mods, wg, gates = _gather_first(c, c_ctx.reshape(1, D), w_mod[0], bm_mine, w_in[0].astype(bf16), gate_mine)
    w_pad = _repack_w(wg)
    gflat = gates.reshape(NDEV, 9 * 128)
    w2 = jnp.transpose(gflat[:, 0:2 * RANK * 32].reshape(NDEV, 2, RANK, 32), (1, 2, 0, 3)).reshape(2, RANK, KW)
    gb = jnp.transpose(gflat[:, 2 * RANK * 32:2 * RANK * 32 + 64].reshape(NDEV, 2, 32), (1, 0, 2)).reshape(2, KW)

    mods = jnp.transpose(mods, (1, 0, 2)).reshape(16, 3 * D)
    mod = lax.dynamic_slice(mods, (me, 0), (1, 3 * D))
    modc = mods[8:9, 0:2 * D]

    xch = _Exchanges(w_proj_a[0].astype(bf16), w_proj_b[0].astype(bf16), w_out[0].astype(bf16))
    r = _local_step(x[0], ctx[0], loss_target[0], mod, modc, norm_g, w_pad, a_ln_g, a_ln_b, a_ws[0], a_bs[0], w2, gb,
                    b_norm_g, final_norm_g.reshape(1, D), xch)
    p_out, p_pa, p_pb = r["got_first"]
    smalls_e, p_in_a = r["got_a"]
    (p_in_b,) = r["got_b"]
    _, _, p_in_late, p_gt = r["got_late"]

    n_e = (AW + AW + 4 * ACH * ACH + AW + VW + D) // 128
    row = lambda t: t.reshape(1, D)
    rep_e = _adam_params(smalls_e, [a_ln_g, a_ln_b, a_ws, a_bs, b_norm_g, row(final_norm_g)],
                         [m_a_ln_g, m_a_ln_b, m_a_ws, m_a_bs, m_b_norm_g, row(m_final_norm_g)],
                         [v_a_ln_g, v_a_ln_b, v_a_ws, v_a_bs, v_b_norm_g, row(v_final_norm_g)], "adam_rep_early")
    rep_e = [t[0:5] + (t[5].reshape(D),) for t in rep_e]
    losses = smalls_e[:, n_e, 0]
    loss = losses[0]
    for i in range(1, NDEV):
        loss = loss + losses[i]

    dbm = r["dmod"] + jnp.concatenate([r["dmodc"], jnp.zeros((1, D), f32)], axis=1)
    wm, ng, bmod_r, cc = _mod_tail(r["dnorm_g"], dbm, r["dmod"], r["dmodc"], cs, w_mod, m_w_mod, v_w_mod,
                                   norm_g, m_norm_g, v_norm_g, b_mod, m_b_mod, v_b_mod,
                                   c_ctx.reshape(1, D), m_c_ctx.reshape(1, D), v_c_ctx.reshape(1, D))

    a_in = _adam_w_in(p_in_a, p_in_b, p_in_late, w_in, m_w_in, v_w_in)
    blk = _adam_blocks([p_pa, p_pb, p_out], [w_proj_a, w_proj_b, w_out], [m_w_proj_a, m_w_proj_b, m_w_out],
                       [v_w_proj_a, v_w_proj_b, v_w_out], "adam_proj_out")
    a_pa, a_pb, a_out = ([t[i] for t in blk] for i in range(3))
    gate_m = _rows128(jnp.concatenate([m_b_gate_w2.reshape(-1), m_b_gate_b.reshape(-1)]))
    gate_v = _rows128(jnp.concatenate([v_b_gate_w2.reshape(-1), v_b_gate_b.reshape(-1)]))
    a_gt = [t.reshape(-1) for t in _adam(p_gt, gate_mine, gate_m, gate_v, 9, "adam_gate")]
    nw2 = 2 * RANK * 32
    sh = [(a_in[k], a_pa[k], a_pb[k], a_out[k], a_gt[k][0:nw2].reshape(1, 2, RANK, 32),
           a_gt[k][nw2:nw2 + 64].reshape(1, 2, 32)) for k in range(4)]

    outs = [loss, r["gx"][None]]
    for k in range(4):
        lg, lb, aws, abs_, bng, fng = rep_e[k]
        n_g, bmod = ng[k], bmod_r[k]
        s_in, s_pa, s_pb, s_out, s_w2, s_gb = sh[k]
        outs += [cc[k].reshape(D), wm[k], bmod, n_g, s_in, lg, lb, aws, abs_, s_w2, s_gb, bng, s_pa, s_pb, s_out, fng]
    return tuple(outs)
```

```python
import jax
import jax.numpy as jnp
from jax import lax
from jax.experimental import pallas as pl
from jax.experimental.pallas import tpu as pltpu

f32, bf16 = jnp.float32, jnp.bfloat16

D = 1024
CTX = 256
EPS = 1e-6
AW = 512
ACH = 128
GW = 64
KW = 256
VW = 512
NH = 4
HK = 64
HV = 128
RANK = 16
TAU = 16.0
CH = 64
QSCALE = HK ** -0.5
INW = 5152
NDEV = 8

PQ, PK, PV, PZB, PUA, PVA, PZA, PG1, PG2, PLR, PW = 0, 256, 512, 1024, 1536, 2048, 2560, 3072, 4096, 5120, 5248
LRW = 128

ADAM_LR, ADAM_B1, ADAM_B2, ADAM_EPS, ADAM_WD, ADAM_STEP = 0.001, 0.9, 0.999, 1e-08, 0.01, 10

VMEM_LIMIT = 56 * 1024 * 1024
MESH = pl.DeviceIdType.MESH


def _cp(sem=None):
    return pltpu.CompilerParams(dimension_semantics=sem, vmem_limit_bytes=VMEM_LIMIT)


def _dot(a, b):
    return jnp.dot(a.astype(bf16), b.astype(bf16), preferred_element_type=f32)


def _nt(a, b):
    return lax.dot_general(a.astype(bf16), b.astype(bf16), (((1,), (1,)), ((), ())), preferred_element_type=f32)


def _tn(a, b):
    return lax.dot_general(a.astype(bf16), b.astype(bf16), (((0,), (0,)), ((), ())), preferred_element_type=f32)


def _dot_hi(a, b):
    return jnp.dot(a, b, preferred_element_type=f32, precision=lax.Precision.HIGHEST)


def _sigmoid(x):
    return 1.0 / (1.0 + jnp.exp(-x))


def _log_sigmoid(x):
    return jnp.minimum(x, 0.0) - jnp.log(1.0 + jnp.exp(-jnp.abs(x)))


def _silu_and_grad(z):
    s = _sigmoid(z)
    return z * s, s * (1.0 + z * (1.0 - s))


def _me():
    return 4 * lax.axis_index("x") + 2 * lax.axis_index("y") + lax.axis_index("c")


def _peer(k):
    x, y, c = lax.axis_index("x"), lax.axis_index("y"), lax.axis_index("c")
    px = 1 - x if k & 4 else x
    py = 1 - y if k & 2 else y
    pc = 1 - c if k & 1 else c
    return (px, py, pc), 4 * px + 2 * py + pc


def _fanout(srcs, dsts, send_sems, recv_sems, local_sems, owners=None):
    me = _me()
    n = len(srcs)
    owns = lambda a, j: True if owners is None or owners[a] is None else owners[a](j)

    def guarded(cond, fn):
        if cond is True:
            fn()
        else:
            pl.when(cond)(fn)

    def copies(with_recvs):
        local = [pltpu.make_async_copy(srcs[a](me), dsts[a](me), local_sems.at[a]) for a in range(n)]
        sends, recvs = [], []
        for k in range(1, NDEV):
            dev, idx = _peer(k)
            for a in range(n):
                s = (k - 1) * n + a
                sends.append((owns(a, idx), pltpu.make_async_remote_copy(
                    src_ref=srcs[a](idx), dst_ref=dsts[a](me), send_sem=send_sems.at[s], recv_sem=recv_sems.at[s],
                    device_id=dev, device_id_type=MESH)))
                if with_recvs:
                    recvs.append((owns(a, me), pltpu.make_async_remote_copy(
                        src_ref=srcs[a](idx), dst_ref=dsts[a](idx), send_sem=send_sems.at[s], recv_sem=recv_sems.at[s],
                        device_id=dev, device_id_type=MESH)))
        return local, sends, recvs

    def start():
        local, sends, _ = copies(False)
        for a, cp in enumerate(local):
            guarded(owns(a, me), cp.start)
        for cond, cp in sends:
            guarded(cond, cp.start)

    def finish():
        local, sends, recvs = copies(True)
        for cond, cp in recvs:
            guarded(cond, cp.wait_recv)
        for cond, cp in sends:
            guarded(cond, cp.wait_send)
        for a, cp in enumerate(local):
            guarded(owns(a, me), cp.wait)

    return start, finish


class _Comm:
    def __init__(self, ins, outs, plan):
        self.ins, self.outs, self.plan = list(ins), list(outs), plan
        self.n = len(self.outs)

    def specs(self):
        hbm = pl.BlockSpec(memory_space=pl.ANY)
        return [hbm] * len(self.ins), [hbm] * self.n, _fanout_sems(self.n) if self.n else []

    def run(self, in_refs, out_refs, sems, nsteps, compute):
        if not self.n:
            compute()
            return

        def hooks():
            srcs, dsts, owners = self.plan(in_refs, out_refs)
            return _fanout(srcs, dsts, sems[0], sems[1], sems[2], owners)

        pl.when(pl.program_id(0) == 0)(lambda: hooks()[0]())
        compute()
        pl.when(pl.program_id(0) == nsteps - 1)(lambda: hooks()[1]())


LATE_MID_STEP = 1


class _LateComm:
    def __init__(self, blocks, dgt):
        sds = jax.ShapeDtypeStruct
        self.ins = [blocks, dgt]
        self.outs = [sds((D, SHARD), bf16), sds((D, SHARD), bf16), sds((4, D, SHARD), bf16), sds(dgt.shape, f32)]
        self.n = len(self.outs)

    def specs(self):
        hbm = pl.BlockSpec(memory_space=pl.ANY)
        scratch = [pltpu.VMEM((3, D, SHARD), bf16), pltpu.SemaphoreType.DMA((2,)), pltpu.SemaphoreType.DMA((4,)),
                   pltpu.SemaphoreType.DMA((3,))] + _fanout_sems(1)
        return [hbm] * 2, [hbm] * self.n, scratch

    def run(self, in_refs, out_refs, scratch, nsteps, compute):
        late_ref, dgt_ref = in_refs
        sib_ref, pair_ref, parts_ref, gt_ref = out_refs
        vbuf, send_sems, recv_sems, local_sems, g_send, g_recv, g_local = scratch
        x, y, c = lax.axis_index("x"), lax.axis_index("y"), lax.axis_index("c")
        chip = 2 * x + y
        is_owner_chip = chip == 0
        step = pl.program_id(0)

        def to_sibling():
            return pltpu.make_async_remote_copy(src_ref=late_ref.at[1 - c], dst_ref=sib_ref, send_sem=send_sems.at[0],
                                                recv_sem=recv_sems.at[0], device_id=(x, y, 1 - c), device_id_type=MESH)

        def to_owner(k):
            return pltpu.make_async_remote_copy(src_ref=pair_ref, dst_ref=parts_ref.at[k], send_sem=send_sems.at[1],
                                                recv_sem=recv_sems.at[k], device_id=(0, 0, c), device_id_type=MESH)

        def own_copy():
            return pltpu.make_async_copy(pair_ref, parts_ref.at[0], local_sems.at[2])

        def gates():
            return _fanout([lambda j: dgt_ref.at[j]], [lambda j: gt_ref.at[j]], g_send, g_recv, g_local)

        @pl.when(step == 0)
        def _():
            to_sibling().start()
            gates()[0]()

        compute()

        @pl.when(step == LATE_MID_STEP)
        def _():
            mine = pltpu.make_async_copy(late_ref.at[c], vbuf.at[0], local_sems.at[0])
            mine.start()
            to_sibling().wait_recv()
            theirs = pltpu.make_async_copy(sib_ref, vbuf.at[1], local_sems.at[1])
            theirs.start()
            mine.wait()
            theirs.wait()
            vbuf[2] = (vbuf[0].astype(f32) + vbuf[1].astype(f32)).astype(bf16)
            pltpu.sync_copy(vbuf.at[2], pair_ref)
            pl.when(is_owner_chip)(lambda: own_copy().start())
            pl.when(jnp.logical_not(is_owner_chip))(lambda: to_owner(chip).start())

        @pl.when(step == nsteps - 1)
        def _():
            @pl.when(is_owner_chip)
            def _():
                for k in range(1, 4):
                    to_owner(k).wait_recv()
                own_copy().wait()

            pl.when(jnp.logical_not(is_owner_chip))(lambda: to_owner(chip).wait_send())
            to_sibling().wait_send()
            gates()[1]()


def _fanout_sems(n):
    return [pltpu.SemaphoreType.DMA(((NDEV - 1) * n,)), pltpu.SemaphoreType.DMA(((NDEV - 1) * n,)), pltpu.SemaphoreType.DMA((n,))]


def _lanes(j):
    return pl.ds(pl.multiple_of(j * 128, 128), 128)


def _gather_first(c_row, cctx_row, wm, bm, wi, gate):
    def body(c_ref, cctx_ref, wm_ref, bm_ref, wi_ref, g_ref, cs_ref, mods_ref, owi, og, call_ref, mine_ref,
             send_sems, recv_sems, local_sems, c_send, c_recv, c_local, m_send, m_recv, m_local):
        x, y, c = lax.axis_index("x"), lax.axis_index("y"), lax.axis_index("c")
        sibling = (x, y, 1 - c)
        chips = [(1 - x, y), (x, 1 - y), (1 - x, 1 - y)]
        index = lambda px, py, pc: 4 * px + 2 * py + pc
        arrays = ((wi_ref, owi), (g_ref, og))
        n = len(arrays)

        def copy(a, k, block, to, own=False):
            src, out = arrays[a]
            return pltpu.make_async_remote_copy(
                src_ref=src if own else out.at[index(*block)], dst_ref=out.at[index(*block)],
                send_sem=send_sems.at[k * n + a], recv_sem=recv_sems.at[k * n + a], device_id=to, device_id_type=MESH)

        c_start, c_finish = _fanout([lambda j: c_ref], [lambda j: call_ref.at[j]], c_send, c_recv, c_local)
        c_start()
        mine = [pltpu.make_async_copy(src, out.at[index(x, y, c)], local_sems.at[a]) for a, (src, out) in enumerate(arrays)]
        first = [copy(a, 0, (x, y, c), sibling, own=True) for a in range(n)]
        first += [copy(a, 1 + j, (x, y, c), (*chip, c), own=True) for j, chip in enumerate(chips) for a in range(n)]
        for cp in mine + first:
            cp.start()

        c_finish()
        cs = jnp.concatenate([call_ref[j] for j in range(NDEV)] + [cctx_ref[...], jnp.zeros((16 - NDEV - 1, D), f32)], axis=0)
        cs_ref[...] = cs
        s, _ = _silu_and_grad(cs)
        mine_ref[...] = _dot_hi(s, wm_ref[...]) + bm_ref[...]
        m_start, m_finish = _fanout([lambda j: mine_ref], [lambda j: mods_ref.at[j]], m_send, m_recv, m_local)
        m_start()

        passed = []
        for j, chip in enumerate(chips):
            for a in range(n):
                copy(a, 1 + j, (*chip, c), (x, y, c)).wait_recv()
            for a in range(n):
                cp = copy(a, 4 + j, (*chip, c), sibling)
                cp.start()
                passed.append(cp)
        for a in range(n):
            copy(a, 0, sibling, (x, y, c)).wait_recv()
        for j, chip in enumerate(chips):
            for a in range(n):
                copy(a, 4 + j, (*chip, 1 - c), (x, y, c)).wait_recv()
        for cp in first + passed:
            cp.wait_send()
        for cp in mine:
            cp.wait()
        m_finish()

    hbm = pl.BlockSpec(memory_space=pl.ANY)
    vm = pl.BlockSpec(memory_space=pltpu.VMEM)
    ncol = wm.shape[1]
    return pl.pallas_call(
        body, name="gather_first",
        out_shape=(jax.ShapeDtypeStruct((16, D), f32), jax.ShapeDtypeStruct((NDEV, 16, ncol), f32),
                   jax.ShapeDtypeStruct((NDEV,) + wi.shape, bf16), jax.ShapeDtypeStruct((NDEV,) + gate.shape, f32)),
        in_specs=[vm, vm, vm, vm, hbm, hbm], out_specs=(vm, vm, hbm, hbm),
        scratch_shapes=[pltpu.VMEM((NDEV, 1, D), f32), pltpu.VMEM((16, ncol), f32)] + _fanout_sems(2) + _fanout_sems(1) + _fanout_sems(1),
        compiler_params=_cp(),
    )(c_row, cctx_row, wm, bm, wi, gate)


SHARD = INW // NDEV
ROWS_RP = 128


def _overlap(lo, hi, a, b):
    s, e = max(lo, a), min(hi, b)
    return (s, e) if s < e else None


def _repack_w(wg):
    segs = ((0, 1024, PQ), (1024, 1024 + 2 * RANK, PLR), (1024 + 2 * RANK, INW, PZB))

    def body(g_ref, o_ref):
        for j in range(NDEV):
            lo, hi = j * SHARD, (j + 1) * SHARD
            for a, b, pad0 in segs:
                ov = _overlap(lo, hi, a, b)
                if ov:
                    s, e = ov
                    o_ref[:, pad0 + s - a:pad0 + e - a] = g_ref[j, :, s - lo:e - lo]
        o_ref[:, PLR + 2 * RANK:PW] = jnp.zeros((ROWS_RP, PW - PLR - 2 * RANK), bf16)

    return pl.pallas_call(
        body, name="repack_w", grid=(D // ROWS_RP,), out_shape=jax.ShapeDtypeStruct((D, PW), bf16),
        in_specs=[pl.BlockSpec((NDEV, ROWS_RP, SHARD), lambda i: (0, i, 0))],
        out_specs=pl.BlockSpec((ROWS_RP, PW), lambda i: (i, 0)), compiler_params=_cp(("arbitrary",)),
    )(wg)


LATE_END = 1024 + 2 * RANK
LATE_DESTS = 2


def _pack_blocks(o_ref, srcs, dests, dtype):
    for n, j in enumerate(dests):
        lo, hi = j * SHARD, (j + 1) * SHARD
        done = lo
        for a, b, src in srcs:
            ov = _overlap(lo, hi, a, b)
            if ov:
                s, e = ov
                if s > done:
                    o_ref[n, :, done - lo:s - lo] = jnp.zeros((ROWS_RP, s - done), dtype)
                o_ref[n, :, s - lo:e - lo] = src[:, s - a:e - a].astype(dtype)
                done = e
        if done < hi:
            o_ref[n, :, done - lo:hi - lo] = jnp.zeros((ROWS_RP, hi - done), dtype)


def _pack_dw_early(dw_zbua, dw_va, dw_zag, row0, nrows, name):
    def body(zbua_ref, va_ref, zag_ref, o_ref):
        _pack_blocks(o_ref, ((LATE_END, 2080, zbua_ref), (2080, 2592, va_ref), (2592, INW, zag_ref)), range(NDEV), bf16)

    row = lambda w: pl.BlockSpec((ROWS_RP, w), lambda i: (i + row0 // ROWS_RP, 0))
    return pl.pallas_call(
        body, name=name, grid=(nrows // ROWS_RP,), out_shape=jax.ShapeDtypeStruct((NDEV, nrows, SHARD), bf16),
        in_specs=[row(2 * AW), row(AW), row(AW + 2 * D)],
        out_specs=pl.BlockSpec((NDEV, ROWS_RP, SHARD), lambda i: (0, i, 0)), compiler_params=_cp(("arbitrary",)),
    )(dw_zbua, dw_va, dw_zag)


def _pack_dw_late(dw_qkv, dw_lr, dwk_c, dwv_c, dwl_c):
    def body(qkv_ref, lr_ref, kc_ref, vc_ref, lc_ref, o_ref):
        qkv = qkv_ref[...] + jnp.concatenate([jnp.zeros((ROWS_RP, KW), f32), kc_ref[...], vc_ref[...]], axis=1)
        lr = lr_ref[...] + lc_ref[...]
        _pack_blocks(o_ref, ((0, 1024, qkv), (1024, LATE_END, lr)), range(LATE_DESTS), bf16)

    row = lambda w: pl.BlockSpec((ROWS_RP, w), lambda i: (i, 0))
    return pl.pallas_call(
        body, name="pack_dw_late", grid=(D // ROWS_RP,), out_shape=jax.ShapeDtypeStruct((LATE_DESTS, D, SHARD), bf16),
        in_specs=[row(2 * KW + VW), row(LRW), row(KW), row(VW), row(LRW)],
        out_specs=pl.BlockSpec((LATE_DESTS, ROWS_RP, SHARD), lambda i: (0, i, 0)), compiler_params=_cp(("arbitrary",)),
    )(dw_qkv, dw_lr, dwk_c, dwv_c, dwl_c)


def _mod_tail(dnorm, dbm, dmod, dmodc, cs, wm, m_wm, v_wm, norm_g, m_ng, v_ng, b_mod, m_bm, v_bm, c_ctx, m_cc, v_cc):
    ncol = wm.shape[2]

    def body(dn_ref, dbm_ref, dmod_ref, dmodc_ref, cs_ref, wm_ref, mwm_ref, vwm_ref, ng_ref, mng_ref, vng_ref,
             bm_ref, mbm_ref, vbm_ref, cc_ref, mcc_ref, vcc_ref, *rest):
        o_wm, o_ng, o_bm, o_cc = rest[0:4], rest[4:8], rest[8:12], rest[12:16]
        a_dn, a_dbm, a_dmod, a_dmodc, dm_rows, gc_ref, a_gc = rest[16:23]
        s1, r1, l1, s2, r2, l2 = rest[23:29]
        start, finish = _fanout([lambda j: dn_ref, lambda j: dbm_ref, lambda j: dmod_ref, lambda j: dmodc_ref],
                                [lambda j, r=r: r.at[j] for r in (a_dn, a_dbm, a_dmod, a_dmodc)], s1, r1, l1)
        start()
        finish()

        def total(ref):
            t = ref[0]
            for j in range(1, NDEV):
                t = t + ref[j]
            return t

        _adam_update(total(a_dn), ng_ref, mng_ref, vng_ref, *o_ng)
        _adam_update(total(a_dbm), bm_ref, mbm_ref, vbm_ref, *o_bm)
        dmodc_tot = jnp.concatenate([total(a_dmodc), jnp.zeros((1, D), f32)], axis=1)
        dm_rows[...] = jnp.concatenate([a_dmod[j] for j in range(NDEV)] + [dmodc_tot, jnp.zeros((16 - NDEV - 1, 3 * D), f32)], axis=0)
        dm = dm_rows[:, pl.ds(pl.multiple_of(_me() * ncol, 128), ncol)]
        s, ds = _silu_and_grad(cs_ref[...])
        part = lax.dot_general(dm[8:9, :], wm_ref[0], (((1,), (1,)), ((), ())), preferred_element_type=f32,
                               precision=lax.Precision.HIGHEST)
        gc_ref[...] = part * ds[8:9, :]
        start2, finish2 = _fanout([lambda j: gc_ref], [lambda j: a_gc.at[j]], s2, r2, l2)
        start2()
        g = lax.dot_general(s, dm, (((0,), (0,)), ((), ())), preferred_element_type=f32, precision=lax.Precision.HIGHEST)
        _adam_update(g[None], wm_ref, mwm_ref, vwm_ref, *o_wm)
        finish2()
        _adam_update(total(a_gc), cc_ref, mcc_ref, vcc_ref, *o_cc)

    like = lambda a: [jax.ShapeDtypeStruct(a.shape, f32)] * 4
    row = lambda n: pltpu.VMEM((NDEV, 1, n), f32)
    res = pl.pallas_call(
        body, name="mod_tail", out_shape=tuple(like(wm) + like(norm_g) + like(b_mod) + like(c_ctx)),
        scratch_shapes=[row(D), row(3 * D), row(3 * D), row(2 * D), pltpu.VMEM((16, 3 * D), f32), pltpu.VMEM((1, D), f32), row(D)]
        + _fanout_sems(4) + _fanout_sems(1),
        compiler_params=_cp(),
    )(dnorm, dbm, dmod, dmodc, cs, wm, m_wm, v_wm, norm_g, m_ng, v_ng, b_mod, m_bm, v_bm, c_ctx, m_cc, v_cc)
    return res[0:4], res[4:8], res[8:12], res[12:16]


def _adam_update(g, w_ref, m_ref, v_ref, go_ref, d_ref, mo_ref, vo_ref):
    c1 = 1.0 / (1.0 - ADAM_B1 ** ADAM_STEP)
    c2 = 1.0 / (1.0 - ADAM_B2 ** ADAM_STEP)
    mn = ADAM_B1 * m_ref[...] + (1.0 - ADAM_B1) * g
    vn = ADAM_B2 * v_ref[...] + (1.0 - ADAM_B2) * (g * g)
    go_ref[...] = g
    mo_ref[...] = mn
    vo_ref[...] = vn
    d_ref[...] = -ADAM_LR * ((mn * c1) / (jnp.sqrt(vn * c2) + ADAM_EPS) + ADAM_WD * w_ref[...])


def _adam_w_in(parts_a, parts_b, parts_late, w, m, v):
    na = EARLY_A_ROWS // ROWS_RP

    def body(a_ref, b_ref, l_ref, w_ref, m_ref, v_ref, go_ref, d_ref, mo_ref, vo_ref):
        me = _me()
        first = pl.program_id(0) < na
        early = jnp.where(first, a_ref[0], b_ref[0]).astype(f32)
        for i in range(1, NDEV):
            early = early + jnp.where(first, a_ref[i], b_ref[i]).astype(f32)
        late = l_ref[0].astype(f32)
        for i in range(1, 4):
            late = late + l_ref[i].astype(f32)
        g = jnp.where(me >= LATE_DESTS - 1, early, 0.0) + jnp.where(me < LATE_DESTS, late, 0.0)
        _adam_update(g, w_ref, m_ref, v_ref, go_ref, d_ref, mo_ref, vo_ref)

    blk = pl.BlockSpec((None, ROWS_RP, SHARD), lambda i: (0, i, 0))
    return pl.pallas_call(
        body, name="adam_w_in", grid=(D // ROWS_RP,), out_shape=tuple(jax.ShapeDtypeStruct((1, D, SHARD), f32) for _ in range(4)),
        in_specs=[pl.BlockSpec((NDEV, ROWS_RP, SHARD), lambda i: (0, jnp.minimum(i, na - 1), 0)),
                  pl.BlockSpec((NDEV, ROWS_RP, SHARD), lambda i: (0, jnp.maximum(i - na, 0), 0)),
                  pl.BlockSpec((4, ROWS_RP, SHARD), lambda i: (0, i, 0)), blk, blk, blk],
        out_specs=(blk, blk, blk, blk), compiler_params=_cp(("arbitrary",)),
    )(parts_a, parts_b, parts_late, w, m, v)


def _adam_params(parts, ws, ms, vs, name):
    p = parts.shape[0]
    k = len(ws)
    nrows = [w.size // 128 for w in ws]
    starts = [sum(nrows[:i]) for i in range(k)]

    def shaped(g, shape):
        if len(shape) == 2:
            return jnp.concatenate([g[r:r + 1] for r in range(g.shape[0])], axis=1)
        return g.reshape(shape)

    def body(*refs):
        g_ref = refs[0]
        w_refs, m_refs, v_refs = refs[1:1 + k], refs[1 + k:1 + 2 * k], refs[1 + 2 * k:1 + 3 * k]
        outs = refs[1 + 3 * k:]
        for i in range(k):
            rows = slice(starts[i], starts[i] + nrows[i])
            g = g_ref[0, rows, :]
            for j in range(1, p):
                g = g + g_ref[j, rows, :]
            _adam_update(shaped(g, ws[i].shape), w_refs[i], m_refs[i], v_refs[i], outs[i], outs[k + i], outs[2 * k + i], outs[3 * k + i])

    vm = pl.BlockSpec(memory_space=pltpu.VMEM)
    res = pl.pallas_call(
        body, name=name, out_shape=tuple(jax.ShapeDtypeStruct(w.shape, f32) for _ in range(4) for w in ws),
        in_specs=[vm] * (1 + 3 * k), out_specs=(vm,) * (4 * k), compiler_params=_cp(),
    )(parts, *ws, *ms, *vs)
    return [res[i * k:(i + 1) * k] for i in range(4)]


def _adam_blocks(parts, ws, ms, vs, name):
    k = len(ws)

    def body(*refs):
        g_refs, w_refs, m_refs, v_refs = refs[0:k], refs[k:2 * k], refs[2 * k:3 * k], refs[3 * k:4 * k]
        outs = refs[4 * k:]
        for i in range(k):
            g = g_refs[i][0].astype(f32)
            for j in range(1, parts[i].shape[0]):
                g = g + g_refs[i][j].astype(f32)
            _adam_update(g[None], w_refs[i], m_refs[i], v_refs[i], outs[i], outs[k + i], outs[2 * k + i], outs[3 * k + i])

    vm = pl.BlockSpec(memory_space=pltpu.VMEM)
    res = pl.pallas_call(
        body, name=name, out_shape=tuple(jax.ShapeDtypeStruct(w.shape, f32) for _ in range(4) for w in ws),
        in_specs=[vm] * (4 * k), out_specs=(vm,) * (4 * k), compiler_params=_cp(),
    )(*parts, *ws, *ms, *vs)
    return [res[i * k:(i + 1) * k] for i in range(4)]


def _adam(parts, w, m, v, rows, name):
    p, r, c = parts.shape
    lead = w.ndim - 2

    def body(g_ref, w_ref, m_ref, v_ref, go_ref, d_ref, mo_ref, vo_ref):
        g = g_ref[0].astype(f32)
        for i in range(1, p):
            g = g + g_ref[i].astype(f32)
        _adam_update(g, w_ref, m_ref, v_ref, go_ref, d_ref, mo_ref, vo_ref)

    blk = pl.BlockSpec((None,) * lead + (rows, c), lambda i: (0,) * lead + (i, 0))
    return pl.pallas_call(
        body, name=name, grid=(r // rows,), out_shape=tuple(jax.ShapeDtypeStruct(w.shape, f32) for _ in range(4)),
        in_specs=[pl.BlockSpec((p, rows, c), lambda i: (0, i, 0)), blk, blk, blk], out_specs=(blk, blk, blk, blk),
        compiler_params=_cp(("arbitrary",)),
    )(parts, w, m, v)


def _in_proj(x, g, scale, shift, w_pad, tl, comm):
    l = x.shape[0]
    c_in, c_out, c_sems = comm.specs()

    def body(*refs):
        x_ref, g_ref, sc_ref, sh_ref, w_ref = refs[:5]
        cin = refs[5:5 + len(c_in)]
        p_ref, h_ref = refs[5 + len(c_in):7 + len(c_in)]
        cout = refs[7 + len(c_in):7 + len(c_in) + comm.n]
        sems = refs[7 + len(c_in) + comm.n:]

        def compute():
            xv = x_ref[...]
            r = lax.rsqrt(jnp.mean(xv * xv, axis=-1, keepdims=True) + EPS)
            h = (xv * r) * (g_ref[...] * (1.0 + sc_ref[...])) + sh_ref[...]
            hb = h.astype(bf16)
            h_ref[...] = hb
            p_ref[...] = jnp.dot(hb, w_ref[...], preferred_element_type=f32).astype(bf16)

        comm.run(cin, cout, sems, l // tl, compute)

    vec = pl.BlockSpec((1, D), lambda i: (0, 0))
    return pl.pallas_call(
        body, name="in_proj", grid=(l // tl,),
        out_shape=(jax.ShapeDtypeStruct((l, PW), bf16), jax.ShapeDtypeStruct((l, D), bf16)) + tuple(comm.outs),
        in_specs=[pl.BlockSpec((tl, D), lambda i: (i, 0)), vec, vec, vec, pl.BlockSpec((D, PW), lambda i: (0, 0))] + c_in,
        out_specs=(pl.BlockSpec((tl, PW), lambda i: (i, 0)), pl.BlockSpec((tl, D), lambda i: (i, 0))) + tuple(c_out),
        scratch_shapes=c_sems, compiler_params=_cp(("arbitrary",)),
    )(x, g, scale, shift, w_pad, *comm.ins)


def _tri(rev):
    i = lax.broadcasted_iota(jnp.int32, (CH, CH), 0)
    j = lax.broadcasted_iota(jnp.int32, (CH, CH), 1)
    return jnp.where((j >= i) if rev else (j <= i), 1.0, 0.0).astype(f32)


def _head_masks():
    lane = lax.broadcasted_iota(jnp.int32, (1, KW), 1) // HK
    return [jnp.where(lane == h, 1.0, 0.0).astype(f32) for h in range(NH)]


def _block_diag():
    r = lax.broadcasted_iota(jnp.int32, (VW, KW), 0) // HV
    c = lax.broadcasted_iota(jnp.int32, (VW, KW), 1) // HK
    return jnp.where(r == c, 1.0, 0.0).astype(f32)


def _decay(lr, w2, gb, tri, rev):
    logits = _dot(lr, w2) + gb
    a = _log_sigmoid(logits) * (1.0 / TAU)
    c = _dot_hi(tri, a)
    cl = c[0:1, :] if rev else c[CH - 1:CH, :]
    return logits, c, cl


def _state_fwd(k, v, c, cl, st, bd):
    return st * jnp.exp(cl) + bd * _tn(v, k * jnp.exp(cl - c))


def _state_bwd(k, v, c, cl, st0, dst, trit):
    ecl = jnp.exp(cl)
    edec = jnp.exp(cl - c)
    kdec = k * edec
    dv = _nt(kdec, dst)
    dkdec = _dot(v, dst)
    dcl = jnp.sum(dst * st0, axis=0, keepdims=True) * ecl + jnp.sum(dkdec * kdec, axis=0, keepdims=True)
    da = _dot_hi(trit, -dkdec * kdec) + dcl
    return dkdec * edec, dv, da, dst * ecl


def _bdot(a, b):
    return lax.dot_general(a.astype(bf16), b.astype(bf16), (((2,), (1,)), ((0,), (0,))), preferred_element_type=f32)


def _bnt(a, b):
    return lax.dot_general(a.astype(bf16), b.astype(bf16), (((2,), (2,)), ((0,), (0,))), preferred_element_type=f32)


def _btn(a, b):
    return lax.dot_general(a.astype(bf16), b.astype(bf16), (((1,), (1,)), ((0,), (0,))), preferred_element_type=f32)


def _scan_chunks(x, rev):
    nc = x.shape[0]
    hi = x.astype(bf16)
    r1 = x - hi.astype(f32)
    mid = r1.astype(bf16)
    lo = (r1 - mid.astype(f32)).astype(bf16)
    terms = jnp.concatenate([hi, mid, lo], axis=1)
    tri3 = jnp.broadcast_to(jnp.concatenate([_tri(rev)] * 3, axis=1).astype(bf16)[None], (nc, CH, 3 * CH))
    return lax.dot_general(tri3, terms, (((2,), (1,)), ((0,), (0,))), preferred_element_type=f32)


class _Tile:
    pass


def _tile_prep(q_ref, k_ref, lr_ref, w2_ref, gb_ref, rev, nc):
    t = _Tile()
    tg = nc * CH
    t.logits = _dot(lr_ref[...], w2_ref[...]) + gb_ref[...]
    c = _scan_chunks((_log_sigmoid(t.logits) * (1.0 / TAU)).reshape(nc, CH, KW), rev)
    cl = c[:, 0:1, :] if rev else c[:, CH - 1:CH, :]
    k = k_ref[...].astype(f32).reshape(nc, CH, KW)
    t.ec, t.enc, t.edec, t.ecl = jnp.exp(c), jnp.exp(-c), jnp.exp(cl - c), jnp.exp(cl)
    t.qd = q_ref[...].astype(f32).reshape(nc, CH, KW) * t.ec * QSCALE
    t.kd = k * t.enc
    t.kdec = k * t.edec
    hm = _head_masks()
    t.tri4 = jnp.concatenate([_tri(rev)] * NH, axis=0)[None]
    t.qs = jnp.concatenate([t.qd * hm[h] for h in range(NH)], axis=1)
    t.pst = _bnt(t.qs, t.kd) * t.tri4
    return t


def _gla_fwd(p, w2p, gb, s0, rev, tg, name):
    l = p.shape[0]
    nb, nc = l // tg, tg // CH

    def body(q_ref, k_ref, v_ref, lr_ref, w2_ref, gb_ref, s0_ref, o_ref, st_ref, st):
        @pl.when(pl.program_id(0) == 0)
        def _():
            st[...] = s0_ref[...]

        t = _tile_prep(q_ref, k_ref, lr_ref, w2_ref, gb_ref, rev, nc)
        v = v_ref[...].reshape(nc, CH, VW)
        intra = jnp.concatenate([_bdot(t.pst[:, h * CH:(h + 1) * CH], v[:, :, h * HV:(h + 1) * HV]) for h in range(NH)], axis=2)
        kv = _btn(v, t.kdec) * _block_diag()[None]
        s = st[...]
        for n in (range(nc - 1, -1, -1) if rev else range(nc)):
            st_ref[n] = s.astype(bf16)
            s = s * t.ecl[n] + kv[n]
        st[...] = s
        o_ref[...] = (_bnt(t.qd, st_ref[...]) + intra).reshape(tg, VW).astype(bf16)

    blk = (lambda i: nb - 1 - i) if rev else (lambda i: i)
    return pl.pallas_call(
        body, name=name, grid=(nb,),
        out_shape=(jax.ShapeDtypeStruct((l, VW), bf16), jax.ShapeDtypeStruct((l // CH, VW, KW), bf16)),
        in_specs=[pl.BlockSpec((tg, KW), lambda i: (blk(i), PQ // KW)), pl.BlockSpec((tg, KW), lambda i: (blk(i), PK // KW)),
                  pl.BlockSpec((tg, VW), lambda i: (blk(i), PV // VW)), pl.BlockSpec((tg, LRW), lambda i: (blk(i), PLR // LRW)),
                  pl.BlockSpec((LRW, KW), lambda i: (0, 0)), pl.BlockSpec((1, KW), lambda i: (0, 0)),
                  pl.BlockSpec((VW, KW), lambda i: (0, 0))],
        out_specs=(pl.BlockSpec((tg, VW), lambda i: (blk(i), 0)), pl.BlockSpec((nc, VW, KW), lambda i: (blk(i), 0, 0))),
        scratch_shapes=[pltpu.VMEM((VW, KW), f32)],
        compiler_params=_cp(("arbitrary",)),
    )(p, p, p, p, w2p, gb, s0)


def _gla_bwd(p, do, states, w2p, gb, prev, rev, tg, name, comm):
    l = p.shape[0]
    nb, nc = l // tg, tg // CH
    out_dt = bf16
    c_in, c_out, c_sems = comm.specs()

    def body(*refs):
        q_ref, k_ref, v_ref, lr_ref, do_ref, st_ref, w2_ref, gb_ref = refs[:8]
        refs = refs[8:]
        if prev is not None:
            pq_ref, pl_ref = refs[:2]
            refs = refs[2:]
        cin, refs = refs[:len(c_in)], refs[len(c_in):]
        dqkv_ref, dlr_ref, dw2_ref, dgb_ref, ds0_ref = refs[:5]
        cout, dst, ds_buf, sems = refs[5:5 + comm.n], refs[5 + comm.n], refs[6 + comm.n], refs[7 + comm.n:]
        comm.run(cin, cout, sems, nb, lambda: compute(q_ref, k_ref, v_ref, lr_ref, do_ref, st_ref, w2_ref, gb_ref,
                                                      pq_ref if prev is not None else None, pl_ref if prev is not None else None,
                                                      dqkv_ref, dlr_ref, dw2_ref, dgb_ref, ds0_ref, dst, ds_buf))

    def compute(q_ref, k_ref, v_ref, lr_ref, do_ref, st_ref, w2_ref, gb_ref, pq_ref, pl_ref,
                dqkv_ref, dlr_ref, dw2_ref, dgb_ref, ds0_ref, dst, ds_buf):
        @pl.when(pl.program_id(0) == 0)
        def _():
            dst[...] = jnp.zeros_like(dst)
            dw2_ref[...] = jnp.zeros_like(dw2_ref)
            dgb_ref[...] = jnp.zeros_like(dgb_ref)

        t = _tile_prep(q_ref, k_ref, lr_ref, w2_ref, gb_ref, rev, nc)
        hm = _head_masks()
        v = v_ref[...].reshape(nc, CH, VW)
        do = do_ref[...].reshape(nc, CH, VW)
        heads = lambda a, h: a[:, :, h * HV:(h + 1) * HV]
        dpst = jnp.concatenate([_bnt(heads(do, h), heads(v, h)) for h in range(NH)], axis=1) * t.tri4
        dv = jnp.concatenate([_btn(t.pst[:, h * CH:(h + 1) * CH], heads(do, h)) for h in range(NH)], axis=2)
        dqd = _bdot(do, st_ref[...])
        for h in range(NH):
            dqd = dqd + hm[h] * _bdot(dpst[:, h * CH:(h + 1) * CH], t.kd)
        dkd = _btn(dpst, t.qs)
        u = _btn(do, t.qd) * _block_diag()[None]
        d = dst[...]
        for n in (range(nc) if rev else range(nc - 1, -1, -1)):
            ds_buf[n] = d
            d = d * t.ecl[n] + u[n]
        dst[...] = d
        ds0_ref[...] = d
        ds = ds_buf[...]
        dv = dv + _bnt(t.kdec, ds)
        dkdec = _bdot(v, ds)
        dcl = jnp.sum(ds * st_ref[...].astype(f32), axis=1, keepdims=True) * t.ecl + jnp.sum(dkdec * t.kdec, axis=1, keepdims=True)
        dc = dqd * t.qd - dkd * t.kd - dkdec * t.kdec
        da = _scan_chunks(dc, not rev) + dcl
        dq = dqd * t.ec * QSCALE
        dk = dkd * t.enc + dkdec * t.edec
        dlog = da.reshape(tg, KW) * _sigmoid(-t.logits) * (1.0 / TAU)
        dlr = _nt(dlog, w2_ref[...])
        dw2_ref[...] += _tn(lr_ref[...], dlog)
        dgb_ref[...] += jnp.sum(dlog, axis=0, keepdims=True)
        dqkv = jnp.concatenate([dq, dk, dv], axis=2).reshape(tg, 2 * KW + VW)
        if prev is not None:
            dqkv = dqkv + pq_ref[...]
            dlr = dlr + pl_ref[...]
        dqkv_ref[...] = dqkv.astype(out_dt)
        dlr_ref[...] = dlr.astype(out_dt)

    blk = (lambda i: i) if rev else (lambda i: nb - 1 - i)
    in_specs = [pl.BlockSpec((tg, KW), lambda i: (blk(i), PQ // KW)), pl.BlockSpec((tg, KW), lambda i: (blk(i), PK // KW)),
                pl.BlockSpec((tg, VW), lambda i: (blk(i), PV // VW)), pl.BlockSpec((tg, LRW), lambda i: (blk(i), PLR // LRW)),
                pl.BlockSpec((tg, VW), lambda i: (blk(i), 0)), pl.BlockSpec((nc, VW, KW), lambda i: (blk(i), 0, 0)),
                pl.BlockSpec((LRW, KW), lambda i: (0, 0)), pl.BlockSpec((1, KW), lambda i: (0, 0))]
    args = [p, p, p, p, do, states, w2p, gb]
    if prev is not None:
        in_specs += [pl.BlockSpec((tg, 2 * KW + VW), lambda i: (blk(i), 0)), pl.BlockSpec((tg, LRW), lambda i: (blk(i), 0))]
        args += list(prev)
    return pl.pallas_call(
        body, name=name, grid=(nb,),
        out_shape=(jax.ShapeDtypeStruct((l, 2 * KW + VW), out_dt), jax.ShapeDtypeStruct((l, LRW), out_dt),
                   jax.ShapeDtypeStruct((LRW, KW), f32), jax.ShapeDtypeStruct((1, KW), f32), jax.ShapeDtypeStruct((VW, KW), f32))
        + tuple(comm.outs),
        in_specs=in_specs + c_in,
        out_specs=(pl.BlockSpec((tg, 2 * KW + VW), lambda i: (blk(i), 0)), pl.BlockSpec((tg, LRW), lambda i: (blk(i), 0)),
                   pl.BlockSpec((LRW, KW), lambda i: (0, 0)), pl.BlockSpec((1, KW), lambda i: (0, 0)),
                   pl.BlockSpec((VW, KW), lambda i: (0, 0))) + tuple(c_out),
        scratch_shapes=[pltpu.VMEM((VW, KW), f32), pltpu.VMEM((nc, VW, KW), f32)] + c_sems,
        compiler_params=_cp(("arbitrary",)),
    )(*args, *comm.ins)


def _ctx_hidden(ctx_ref, g_ref, sc_ref, sh_ref):
    xv = ctx_ref[...]
    r = lax.rsqrt(jnp.mean(xv * xv, axis=-1, keepdims=True) + EPS)
    xn = xv * r
    return xn, xn * (g_ref[...] * (1.0 + sc_ref[...])) + sh_ref[...]


_CTX_W_SPECS = [pl.BlockSpec((D, KW), lambda i: (0, PK // KW)), pl.BlockSpec((D, VW), lambda i: (0, PV // VW)),
                pl.BlockSpec((D, LRW), lambda i: (0, PLR // LRW))]


def _ctx_fwd(ctx, g, scale, shift, w_pad, w2f, w2b, gbf, gbb):
    ncc = CTX // CH

    def body(ctx_ref, g_ref, sc_ref, sh_ref, wk_ref, wv_ref, wl_ref, w2f_ref, w2b_ref, gbf_ref, gbb_ref, sf_ref, sb_ref):
        _, hc = _ctx_hidden(ctx_ref, g_ref, sc_ref, sh_ref)
        k, v, lr = _dot(hc, wk_ref[...]), _dot(hc, wv_ref[...]), _dot(hc, wl_ref[...])
        bd = _block_diag()
        for rev, w2_ref, gb_ref, out in ((False, w2f_ref, gbf_ref, sf_ref), (True, w2b_ref, gbb_ref, sb_ref)):
            tri = _tri(rev)
            st = jnp.zeros((VW, KW), f32)
            for j in (range(ncc - 1, -1, -1) if rev else range(ncc)):
                rows = slice(j * CH, (j + 1) * CH)
                _, c, cl = _decay(lr[rows], w2_ref[...], gb_ref[...], tri, rev)
                st = _state_fwd(k[rows], v[rows], c, cl, st, bd)
            out[...] = st

    vec = pl.BlockSpec((1, D), lambda i: (0, 0))
    w2s = pl.BlockSpec((LRW, KW), lambda i: (0, 0))
    gbs = pl.BlockSpec((1, KW), lambda i: (0, 0))
    sts = pl.BlockSpec((VW, KW), lambda i: (0, 0))
    return pl.pallas_call(
        body, name="ctx_fwd", grid=(1,), out_shape=(jax.ShapeDtypeStruct((VW, KW), f32),) * 2,
        in_specs=[pl.BlockSpec((CTX, D), lambda i: (0, 0)), vec, vec, vec] + _CTX_W_SPECS + [w2s, w2s, gbs, gbs],
        out_specs=(sts, sts), compiler_params=_cp(("arbitrary",)),
    )(ctx, g, scale, shift, w_pad, w_pad, w_pad, w2f, w2b, gbf, gbb)


def _ctx_bwd(ctx, g, scale, shift, w_pad, w2f, w2b, gbf, gbb, dsf, dsb):
    ncc = CTX // CH

    def body(ctx_ref, g_ref, sc_ref, sh_ref, wk_ref, wv_ref, wl_ref, w2f_ref, w2b_ref, gbf_ref, gbb_ref, dsf_ref, dsb_ref,
             dwk_ref, dwv_ref, dwl_ref, dmod_ref, dg_ref, dw2_ref, dgb_ref):
        xn, hc = _ctx_hidden(ctx_ref, g_ref, sc_ref, sh_ref)
        k, v, lr = _dot(hc, wk_ref[...]), _dot(hc, wv_ref[...]), _dot(hc, wl_ref[...])
        bd = _block_diag()
        dk_rows, dv_rows, dl_rows = [None] * ncc, [None] * ncc, [None] * ncc
        for d, (rev, w2_ref, gb_ref, ds_ref) in enumerate(((False, w2f_ref, gbf_ref, dsf_ref), (True, w2b_ref, gbb_ref, dsb_ref))):
            tri = _tri(rev)
            order = list(range(ncc - 1, -1, -1) if rev else range(ncc))
            st, saved = jnp.zeros((VW, KW), f32), {}
            for j in order:
                rows = slice(j * CH, (j + 1) * CH)
                logits, c, cl = _decay(lr[rows], w2_ref[...], gb_ref[...], tri, rev)
                saved[j] = (logits, c, cl, st)
                st = _state_fwd(k[rows], v[rows], c, cl, st, bd)
            dst = ds_ref[...]
            dw2 = jnp.zeros((LRW, KW), f32)
            dgb = jnp.zeros((1, KW), f32)
            for j in reversed(order):
                rows = slice(j * CH, (j + 1) * CH)
                logits, c, cl, st0 = saved[j]
                dk, dv, da, dst = _state_bwd(k[rows], v[rows], c, cl, st0, dst, _tri(not rev))
                dlog = da * _sigmoid(-logits) * (1.0 / TAU)
                dl = _nt(dlog, w2_ref[...])
                dw2 = dw2 + _tn(lr[rows], dlog)
                dgb = dgb + jnp.sum(dlog, axis=0, keepdims=True)
                dk_rows[j] = dk if dk_rows[j] is None else dk_rows[j] + dk
                dv_rows[j] = dv if dv_rows[j] is None else dv_rows[j] + dv
                dl_rows[j] = dl if dl_rows[j] is None else dl_rows[j] + dl
            dw2_ref[d] = dw2
            dgb_ref[d] = dgb
        dk, dv, dl = (jnp.concatenate(t, axis=0) for t in (dk_rows, dv_rows, dl_rows))
        dwk_ref[...] = _tn(hc, dk)
        dwv_ref[...] = _tn(hc, dv)
        dwl_ref[...] = _tn(hc, dl)
        dh = _nt(dk, wk_ref[...]) + _nt(dv, wv_ref[...]) + _nt(dl, wl_ref[...])
        gx = dh * xn
        dmod_ref[:, 0:D] = jnp.sum(dh, axis=0, keepdims=True)
        dmod_ref[:, D:2 * D] = jnp.sum(gx, axis=0, keepdims=True) * g_ref[...]
        dg_ref[...] = jnp.sum(gx, axis=0, keepdims=True) * (1.0 + sc_ref[...])

    vec = pl.BlockSpec((1, D), lambda i: (0, 0))
    w2s = pl.BlockSpec((LRW, KW), lambda i: (0, 0))
    gbs = pl.BlockSpec((1, KW), lambda i: (0, 0))
    sts = pl.BlockSpec((VW, KW), lambda i: (0, 0))
    full = lambda *s: pl.BlockSpec(s, lambda i: (0,) * len(s))
    return pl.pallas_call(
        body, name="ctx_bwd", grid=(1,),
        out_shape=(jax.ShapeDtypeStruct((D, KW), f32), jax.ShapeDtypeStruct((D, VW), f32), jax.ShapeDtypeStruct((D, LRW), f32),
                   jax.ShapeDtypeStruct((1, 2 * D), f32), jax.ShapeDtypeStruct((1, D), f32),
                   jax.ShapeDtypeStruct((2, LRW, KW), f32), jax.ShapeDtypeStruct((2, 1, KW), f32)),
        in_specs=[pl.BlockSpec((CTX, D), lambda i: (0, 0)), vec, vec, vec] + _CTX_W_SPECS + [w2s, w2s, gbs, gbs, sts, sts],
        out_specs=(full(D, KW), full(D, VW), full(D, LRW), full(1, 2 * D), full(1, D), full(2, LRW, KW), full(2, 1, KW)),
        compiler_params=_cp(("arbitrary",)),
    )(ctx, g, scale, shift, w_pad, w_pad, w_pad, w2f, w2b, gbf, gbb, dsf, dsb)


def _layernorm(va, g, b):
    mu = jnp.mean(va, axis=-1, keepdims=True)
    xc = va - mu
    rstd = lax.rsqrt(jnp.mean(xc * xc, axis=-1, keepdims=True) + EPS)
    vhat = xc * rstd
    return vhat, rstd, vhat * g + b


MIX_PIECES = 8


def _mix_fwd(p, ln_g, ln_b, ws, bs_t):
    l = p.shape[0]
    half = AW // 2
    rp = l // MIX_PIECES
    cpp = rp // ACH

    def body(p_ref, g_ref, b_ref, ws_ref, bs_ref, sv_hbm, va_buf, col_buf, sv_buf, in_sems, out_sems):
        piece = lambda i: pl.ds(pl.multiple_of(i * rp, rp), rp)
        load = lambda i: pltpu.make_async_copy(p_ref.at[piece(i), pl.ds(PVA, AW)], va_buf.at[piece(i)], in_sems.at[i])
        store = lambda i: pltpu.make_async_copy(sv_buf.at[piece(i)], sv_hbm.at[piece(i)], out_sems.at[i])
        for i in range(MIX_PIECES):
            load(i).start()

        def rows_piece(i, carry):
            load(i).wait()
            for j in range(cpp):
                rows = pl.ds(pl.multiple_of(i * rp + j * ACH, ACH), ACH)
                _, _, vn = _layernorm(va_buf[rows, :].astype(f32), g_ref[...], b_ref[...])
                for gi in range(2):
                    sl = slice(gi * ACH, (gi + 1) * ACH)
                    sv_buf[rows, sl] = (_dot(ws_ref[gi], vn[:, sl]) + bs_ref[:, gi:gi + 1]).astype(bf16)
                col_buf[0, rows, :] = vn[:, half:half + ACH]
                col_buf[1, rows, :] = vn[:, half + ACH:]
            return carry

        lax.fori_loop(0, MIX_PIECES, rows_piece, 0)

        def cols_step(cidx, carry):
            rows = pl.ds(cidx, ACH, stride=GW)
            for gi in range(2, 4):
                col_buf[gi - 2, rows, :] = _dot(ws_ref[gi], col_buf[gi - 2, rows, :]) + bs_ref[:, gi:gi + 1]
            return carry

        lax.fori_loop(0, GW, cols_step, 0, unroll=8)

        def out_piece(i, carry):
            for j in range(cpp):
                rows = pl.ds(pl.multiple_of(i * rp + j * ACH, ACH), ACH)
                sv_buf[rows, half:half + ACH] = col_buf[0, rows, :].astype(bf16)
                sv_buf[rows, half + ACH:] = col_buf[1, rows, :].astype(bf16)
            store(i).start()
            return carry

        lax.fori_loop(0, MIX_PIECES, out_piece, 0)
        for i in range(MIX_PIECES):
            store(i).wait()

    vm = pl.BlockSpec(memory_space=pltpu.VMEM)
    hbm = pl.BlockSpec(memory_space=pl.ANY)
    return pl.pallas_call(
        body, name="mix_fwd", out_shape=jax.ShapeDtypeStruct((l, AW), bf16),
        in_specs=[hbm, vm, vm, vm, vm], out_specs=hbm,
        scratch_shapes=[pltpu.VMEM((l, AW), bf16), pltpu.VMEM((2, l, ACH), f32), pltpu.VMEM((l, AW), bf16),
                        pltpu.SemaphoreType.DMA((MIX_PIECES,)), pltpu.SemaphoreType.DMA((MIX_PIECES,))],
        compiler_params=_cp(),
    )(p, ln_g, ln_b, ws, bs_t)


def _mix_bwd(p, dsv, ln_g, ln_b, ws_t):
    l = p.shape[0]
    half = AW // 2
    rp = l // MIX_PIECES
    cpp = rp // ACH

    def body(p_ref, dsv_hbm, g_ref, b_ref, wst_ref, dva_hbm, dws_ref, dbs_ref, dg_ref, db_ref,
             va_buf, dsv_buf, vn_col, ds_col, dva_buf, va_sems, ds_sems, out_sems):
        piece = lambda i: pl.ds(pl.multiple_of(i * rp, rp), rp)
        load_va = lambda i: pltpu.make_async_copy(p_ref.at[piece(i), pl.ds(PVA, AW)], va_buf.at[piece(i)], va_sems.at[i])
        load_ds = lambda i: pltpu.make_async_copy(dsv_hbm.at[piece(i)], dsv_buf.at[piece(i)], ds_sems.at[i])
        store = lambda i: pltpu.make_async_copy(dva_buf.at[piece(i)], dva_hbm.at[piece(i)], out_sems.at[i])
        for i in range(MIX_PIECES):
            load_va(i).start()
            load_ds(i).start()
        dws_ref[...] = jnp.zeros_like(dws_ref)
        dbs_ref[...] = jnp.zeros_like(dbs_ref)
        dg_ref[...] = jnp.zeros_like(dg_ref)
        db_ref[...] = jnp.zeros_like(db_ref)

        def rows_piece(i, carry):
            load_va(i).wait()
            load_ds(i).wait()
            for j in range(cpp):
                rows = pl.ds(pl.multiple_of(i * rp + j * ACH, ACH), ACH)
                _, _, vn = _layernorm(va_buf[rows, :].astype(f32), g_ref[...], b_ref[...])
                ds = dsv_buf[rows, :].astype(f32)
                for gi in range(2):
                    sl = slice(gi * ACH, (gi + 1) * ACH)
                    dws_ref[gi] += _nt(ds[:, sl], vn[:, sl])
                    dbs_ref[gi] += ds[:, sl]
                for gi in range(2):
                    sl = slice(half + gi * ACH, half + (gi + 1) * ACH)
                    vn_col[gi, rows, :] = vn[:, sl]
                    ds_col[gi, rows, :] = ds[:, sl]
            return carry

        lax.fori_loop(0, MIX_PIECES, rows_piece, 0)

        def cols_step(cidx, carry):
            rows = pl.ds(cidx, ACH, stride=GW)
            for gi in range(2, 4):
                ds = ds_col[gi - 2, rows, :]
                dws_ref[gi] += _nt(ds, vn_col[gi - 2, rows, :])
                dbs_ref[gi] += ds
                ds_col[gi - 2, rows, :] = _dot(wst_ref[gi], ds)
            return carry

        lax.fori_loop(0, GW, cols_step, 0, unroll=8)

        def out_piece(i, carry):
            for j in range(cpp):
                rows = pl.ds(pl.multiple_of(i * rp + j * ACH, ACH), ACH)
                vhat, rstd, _ = _layernorm(va_buf[rows, :].astype(f32), g_ref[...], b_ref[...])
                ds = dsv_buf[rows, :].astype(f32)
                dvn = jnp.concatenate([_dot(wst_ref[0], ds[:, 0:ACH]), _dot(wst_ref[1], ds[:, ACH:half]),
                                       ds_col[0, rows, :], ds_col[1, rows, :]], axis=1)
                dg_ref[...] += jnp.sum(dvn * vhat, axis=0, keepdims=True)
                db_ref[...] += jnp.sum(dvn, axis=0, keepdims=True)
                dvh = dvn * g_ref[...]
                dva = rstd * (dvh - jnp.mean(dvh, axis=-1, keepdims=True) - vhat * jnp.mean(dvh * vhat, axis=-1, keepdims=True))
                dva_buf[rows, :] = dva.astype(bf16)
            store(i).start()
            return carry

        lax.fori_loop(0, MIX_PIECES, out_piece, 0)
        for i in range(MIX_PIECES):
            store(i).wait()

    vm = pl.BlockSpec(memory_space=pltpu.VMEM)
    hbm = pl.BlockSpec(memory_space=pl.ANY)
    dma = lambda: pltpu.SemaphoreType.DMA((MIX_PIECES,))
    return pl.pallas_call(
        body, name="mix_bwd",
        out_shape=(jax.ShapeDtypeStruct((l, AW), bf16), jax.ShapeDtypeStruct((4, ACH, ACH), f32), jax.ShapeDtypeStruct((4, ACH, ACH), f32),
                   jax.ShapeDtypeStruct((1, AW), f32), jax.ShapeDtypeStruct((1, AW), f32)),
        in_specs=[hbm, hbm, vm, vm, vm], out_specs=(hbm, vm, vm, vm, vm),
        scratch_shapes=[pltpu.VMEM((l, AW), bf16), pltpu.VMEM((l, AW), bf16), pltpu.VMEM((2, l, ACH), f32), pltpu.VMEM((2, l, ACH), f32),
                        pltpu.VMEM((l, AW), bf16), dma(), dma(), dma()],
        compiler_params=_cp(),
    )(p, dsv, ln_g, ln_b, ws_t)


def _mid(x, tgt, p, o_f, o_b, sv, gate, gf, gb_norm, w_pa, w_pb, w_out, tl):
    l = x.shape[0]

    def body(x_ref, t_ref, zb_ref, ua_ref, za_ref, g1_ref, g2_ref, of_ref, ob_ref, sv_ref, gate_ref, gf_ref, gbn_ref,
             wpa_ref, wpb_ref, wout_ref,
             dx1_ref, dzbua_ref, dzag_ref, dsv_ref, do_ref, dwout_bf, dwpa_bf, dwpb_bf, dgf_ref, dgate_ref, dgbn_ref, loss_ref,
             dwout_ref, dwpa_ref, dwpb_ref):
        @pl.when(pl.program_id(0) == 0)
        def _():
            for r in (dwout_ref, dwpa_ref, dwpb_ref, dgf_ref, dgate_ref, dgbn_ref, loss_ref):
                r[...] = jnp.zeros_like(r)

        o = of_ref[...].astype(f32) + ob_ref[...].astype(f32)
        rr = jnp.concatenate(
            [jnp.broadcast_to(lax.rsqrt(jnp.mean(o[:, h * HV:(h + 1) * HV] ** 2, axis=-1, keepdims=True) + EPS), (tl, HV))
             for h in range(NH)], axis=1)
        ohat = o * rr
        on = ohat * gbn_ref[...]
        szb, dszb = _silu_and_grad(zb_ref[...].astype(f32))
        tb = on * szb
        u = ua_ref[...].astype(f32)
        svv = sv_ref[...].astype(f32)
        sza, dsza = _silu_and_grad(za_ref[...].astype(f32))
        ta = u * svv * sza
        ya = _dot(ta, wpa_ref[...])
        yb = _dot(tb, wpb_ref[...])
        g1 = _sigmoid(g1_ref[...].astype(f32))
        g2 = _sigmoid(g2_ref[...].astype(f32))
        m = g1 * ya + g2 * yb
        y2 = _dot(m, wout_ref[...])
        x1 = x_ref[...] + gate_ref[...] * y2
        r1 = lax.rsqrt(jnp.mean(x1 * x1, axis=-1, keepdims=True) + EPS)
        x1n = x1 * r1
        err = x1n * gf_ref[...] - t_ref[...]
        loss_ref[...] += jnp.sum(jnp.sum(err * err, axis=-1, keepdims=True), axis=0, keepdims=True) * (0.5 / D)
        dout = err * (1.0 / D)
        dgf_ref[...] += jnp.sum(dout * x1n, axis=0, keepdims=True)
        dx1n = dout * gf_ref[...]
        dx1 = r1 * (dx1n - x1n * jnp.mean(dx1n * x1n, axis=-1, keepdims=True))
        dx1_ref[...] = dx1
        dgate_ref[...] += jnp.sum(dx1 * y2, axis=0, keepdims=True)
        dy2 = dx1 * gate_ref[...]
        dwout_ref[...] += _tn(m, dy2)
        dm = _nt(dy2, wout_ref[...])
        dya = dm * g1
        dyb = dm * g2
        dzag_ref[:, AW:AW + D] = (dm * ya * g1 * (1.0 - g1)).astype(bf16)
        dzag_ref[:, AW + D:] = (dm * yb * g2 * (1.0 - g2)).astype(bf16)
        dwpa_ref[...] += _tn(ta, dya)
        dta = _nt(dya, wpa_ref[...])
        dzbua_ref[:, AW:] = (dta * svv * sza).astype(bf16)
        dsv_ref[...] = (dta * u * sza).astype(bf16)
        dzag_ref[:, 0:AW] = (dta * u * svv * dsza).astype(bf16)
        dwpb_ref[...] += _tn(tb, dyb)
        dtb = _nt(dyb, wpb_ref[...])
        don = dtb * szb
        dzbua_ref[:, 0:AW] = (dtb * on * dszb).astype(bf16)
        dgbn_ref[...] += jnp.sum(don * ohat, axis=0, keepdims=True)
        doh = don * gbn_ref[...]
        prod = doh * ohat
        mh = jnp.concatenate(
            [jnp.broadcast_to(jnp.mean(prod[:, h * HV:(h + 1) * HV], axis=-1, keepdims=True), (tl, HV)) for h in range(NH)], axis=1)
        do_ref[...] = (rr * (doh - ohat * mh)).astype(bf16)

        @pl.when(pl.program_id(0) == l // tl - 1)
        def _():
            for acc, out in ((dwout_ref, dwout_bf), (dwpa_ref, dwpa_bf), (dwpb_ref, dwpb_bf)):
                out[...] = acc[...].astype(bf16)

    row = lambda w, j: pl.BlockSpec((tl, w), lambda i, j=j: (i, j))
    full = lambda *s: pl.BlockSpec(s, lambda i: (0,) * len(s))
    return pl.pallas_call(
        body, name="mid", grid=(l // tl,),
        out_shape=(jax.ShapeDtypeStruct((l, D), f32), jax.ShapeDtypeStruct((l, 2 * AW), bf16), jax.ShapeDtypeStruct((l, AW + 2 * D), bf16),
                   jax.ShapeDtypeStruct((l, AW), bf16), jax.ShapeDtypeStruct((l, VW), bf16),
                   jax.ShapeDtypeStruct((D, D), bf16), jax.ShapeDtypeStruct((AW, D), bf16), jax.ShapeDtypeStruct((VW, D), bf16),
                   jax.ShapeDtypeStruct((1, D), f32), jax.ShapeDtypeStruct((1, D), f32), jax.ShapeDtypeStruct((1, VW), f32),
                   jax.ShapeDtypeStruct((1, 1), f32)),
        scratch_shapes=[pltpu.VMEM((D, D), f32), pltpu.VMEM((AW, D), f32), pltpu.VMEM((VW, D), f32)],
        in_specs=[row(D, 0), row(D, 0), row(AW, PZB // AW), row(AW, PUA // AW), row(AW, PZA // AW), row(D, PG1 // D), row(D, PG2 // D),
                  row(VW, 0), row(VW, 0), row(AW, 0), full(1, D), full(1, D), full(1, VW), full(AW, D), full(VW, D), full(D, D)],
        out_specs=(row(D, 0), row(2 * AW, 0), row(AW + 2 * D, 0), row(AW, 0), row(VW, 0),
                   full(D, D), full(AW, D), full(VW, D), full(1, D), full(1, D), full(1, VW), full(1, 1)),
        compiler_params=_cp(("arbitrary",)),
    )(x, tgt, p, p, p, p, p, o_f, o_b, sv, gate, gf, gb_norm, w_pa, w_pb, w_out)


def _in_bwd(x, dx1, dqkv, dzbua, dva, dzag, dlr, w_pad, g, scale, tl, comm):
    l = x.shape[0]
    c_in, c_out, c_sems = comm.specs()

    def body(*refs):
        cin = refs[14:14 + len(c_in)]
        outs = refs[14 + len(c_in):]
        comm.run(cin, outs[4:4 + comm.n], outs[4 + comm.n:], l // tl, lambda: compute(*refs[:14], *outs[:4]))

    def compute(x_ref, dx1_ref, a_ref, b_ref, c_ref, e_ref, lr_ref, wa_ref, wb_ref, wc_ref, we_ref, wl_ref, g_ref, sc_ref,
                gx_ref, dsh_ref, dsc_ref, dg_ref):
        @pl.when(pl.program_id(0) == 0)
        def _():
            for r in (dsh_ref, dsc_ref, dg_ref):
                r[...] = jnp.zeros_like(r)

        dh = (_nt(a_ref[...], wa_ref[...]) + _nt(b_ref[...], wb_ref[...]) + _nt(c_ref[...], wc_ref[...]) + _nt(e_ref[...], we_ref[...])
              + _nt(lr_ref[...], wl_ref[...]))
        xv = x_ref[...]
        r = lax.rsqrt(jnp.mean(xv * xv, axis=-1, keepdims=True) + EPS)
        xn = xv * r
        gxn = jnp.sum(dh * xn, axis=0, keepdims=True)
        dsh_ref[...] += jnp.sum(dh, axis=0, keepdims=True)
        dsc_ref[...] += gxn * g_ref[...]
        dg_ref[...] += gxn * (1.0 + sc_ref[...])
        dxn = dh * (g_ref[...] * (1.0 + sc_ref[...]))
        gx_ref[...] = dx1_ref[...] + r * (dxn - xn * jnp.mean(dxn * xn, axis=-1, keepdims=True))

    row = lambda w: pl.BlockSpec((tl, w), lambda i: (i, 0))
    wcol = lambda w, j: pl.BlockSpec((D, w), lambda i, j=j: (0, j))
    vec = pl.BlockSpec((1, D), lambda i: (0, 0))
    return pl.pallas_call(
        body, name="in_bwd", grid=(l // tl,),
        out_shape=(jax.ShapeDtypeStruct((l, D), f32),) + (jax.ShapeDtypeStruct((1, D), f32),) * 3 + tuple(comm.outs),
        in_specs=[row(D), row(D), row(2 * KW + VW), row(2 * AW), row(AW), row(AW + 2 * D), row(LRW),
                  wcol(2 * KW + VW, 0), wcol(2 * AW, PZB // (2 * AW)), wcol(AW, PVA // AW), wcol(AW + 2 * D, PZA // (AW + 2 * D)),
                  wcol(LRW, PLR // LRW), vec, vec] + c_in,
        out_specs=(row(D), vec, vec, vec) + tuple(c_out), scratch_shapes=c_sems,
        compiler_params=_cp(("arbitrary",)),
    )(x, dx1, dqkv, dzbua, dva, dzag, dlr, w_pad, w_pad, w_pad, w_pad, w_pad, g, scale, *comm.ins)


def _tn_matmul(a, bs, tl, name, comm=None):
    l, m = a.shape
    k = len(bs)
    comm = comm or _Comm([], [], None)
    c_in, c_out, c_sems = comm.specs()

    def body(a_ref, *refs):
        b_refs, cin = refs[:k], refs[k:k + len(c_in)]
        o_refs = refs[k + len(c_in):2 * k + len(c_in)]
        cout, sems = refs[2 * k + len(c_in):2 * k + len(c_in) + comm.n], refs[2 * k + len(c_in) + comm.n:]

        def compute():
            @pl.when(pl.program_id(0) == 0)
            def _():
                for o_ref in o_refs:
                    o_ref[...] = jnp.zeros_like(o_ref)

            av = a_ref[...]
            for b_ref, o_ref in zip(b_refs, o_refs):
                o_ref[...] += _tn(av, b_ref[...])

        comm.run(cin, cout, sems, l // tl, compute)

    return pl.pallas_call(
        body, name=name, grid=(l // tl,), out_shape=tuple(jax.ShapeDtypeStruct((m, b.shape[1]), f32) for b in bs) + tuple(comm.outs),
        in_specs=[pl.BlockSpec((tl, m), lambda i: (i, 0))] + [pl.BlockSpec((tl, b.shape[1]), lambda i: (i, 0)) for b in bs] + c_in,
        out_specs=tuple(pl.BlockSpec((m, b.shape[1]), lambda i: (0, 0)) for b in bs) + tuple(c_out),
        scratch_shapes=c_sems, compiler_params=_cp(("arbitrary",)),
    )(a, *bs, *comm.ins)


def _pad_gate(w2, gb):
    z = jnp.zeros((RANK, KW), f32)
    tail = jnp.zeros((LRW - 2 * RANK, KW), f32)
    w2f = jnp.concatenate([w2[0], z, tail], axis=0)
    w2b = jnp.concatenate([z, w2[1], tail], axis=0)
    return w2f, w2b, gb[0:1], gb[1:2]


EARLY_A_ROWS = 384


class _NoExchange:
    def __init__(self, w_pa, w_pb, w_out):
        self.weights = (w_pa, w_pb, w_out)

    def gather_proj(self):
        return _Comm([], [], None)

    def proj_weights(self, got):
        return self.weights

    def first(self, dw_out, dw_pa, dw_pb):
        return _Comm([], [], None)

    def early_a(self, small, blocks):
        return _Comm([], [], None)

    def early_b(self, blocks):
        return _Comm([], [], None)

    def late(self, blocks, dgt):
        return _Comm([], [], None)


class _Exchanges:
    def __init__(self, pa, pb, wo):
        self.shards = (pa, pb, wo)

    def gather_proj(self):
        def plan(i, o):
            srcs = [lambda j, r=r: r for r in i]
            dsts = [lambda j: o[0].at[:, _lanes(j)], lambda j: o[1].at[:, _lanes(j)], lambda j: o[2].at[j]]
            return srcs, dsts, None
        sds = jax.ShapeDtypeStruct
        return _Comm(self.shards, [sds((AW, D), bf16), sds((VW, D), bf16), sds((NDEV, 128, D), bf16)], plan)

    def proj_weights(self, got):
        return got[0], got[1], got[2].reshape(D, D)

    def first(self, dw_out, dw_pa, dw_pb):
        def plan(i, o):
            srcs = [lambda j: i[0].at[j], lambda j: i[1].at[:, _lanes(j)], lambda j: i[2].at[:, _lanes(j)]]
            dsts = [lambda j, r=r: r.at[j] for r in o]
            return srcs, dsts, None
        sds = jax.ShapeDtypeStruct
        return _Comm([dw_out.reshape(NDEV, 128, D), dw_pa, dw_pb],
                     [sds((NDEV, 128, D), bf16), sds((NDEV, AW, 128), bf16), sds((NDEV, VW, 128), bf16)], plan)

    def early_a(self, small, blocks):
        def plan(i, o):
            srcs = [lambda j: i[0], lambda j: i[1].at[j]]
            dsts = [lambda j, r=r: r.at[j] for r in o]
            return srcs, dsts, [None, lambda j: j >= LATE_DESTS - 1]
        sds = jax.ShapeDtypeStruct
        return _Comm([small, blocks], [sds((NDEV,) + small.shape, f32), sds(blocks.shape, bf16)], plan)

    def early_b(self, blocks):
        def plan(i, o):
            return [lambda j: i[0].at[j]], [lambda j: o[0].at[j]], [lambda j: j >= LATE_DESTS - 1]
        return _Comm([blocks], [jax.ShapeDtypeStruct(blocks.shape, bf16)], plan)

    def late(self, blocks, dgt):
        return _LateComm(blocks, dgt)


def _local_step(x, ctx, tgt, mod, modc, norm_g, w_pad, ln_g, ln_b, ws, bs, w2, gb, gb_norm, gf, xch):
    shift, scale, gate = mod[:, 0:D], mod[:, D:2 * D], mod[:, 2 * D:]
    shift_c, scale_c = modc[:, 0:D], modc[:, D:]
    w2f, w2b, gbf, gbb = _pad_gate(w2, gb)

    p, h, *got_proj = _in_proj(x, norm_g, scale, shift, w_pad, 512, xch.gather_proj())
    w_pa, w_pb, w_out = xch.proj_weights(got_proj)
    sc_f, sc_b = _ctx_fwd(ctx, norm_g, scale_c, shift_c, w_pad, w2f, w2b, gbf, gbb)
    o_f, st_f = _gla_fwd(p, w2f, gbf, sc_f, False, 512, "gla_fwd_f")
    o_b, st_b = _gla_fwd(p, w2b, gbb, sc_b, True, 512, "gla_fwd_b")
    sv = _mix_fwd(p, ln_g, ln_b, ws.astype(bf16), bs.T)
    (dx1, dzbua, dzag, dsv, do, dw_out, dw_pa, dw_pb, dgf, dgate, dgbn, loss) = _mid(
        x, tgt, p, o_f, o_b, sv, gate, gf, gb_norm, w_pa, w_pb, w_out, 256)
    dva, dws, dbs_acc, dln_g, dln_b = _mix_bwd(p, dsv, ln_g, ln_b, jnp.swapaxes(ws, 1, 2).astype(bf16))
    dw_zbua, dw_va, dw_zag, *got_first = _tn_matmul(h, [dzbua, dva, dzag], 1024, "dw_early", xch.first(dw_out, dw_pa, dw_pb))
    blocks_a = _pack_dw_early(dw_zbua, dw_va, dw_zag, 0, EARLY_A_ROWS, "pack_dw_early_a")
    blocks_b = _pack_dw_early(dw_zbua, dw_va, dw_zag, EARLY_A_ROWS, D - EARLY_A_ROWS, "pack_dw_early_b")
    small = _rows128(dln_g, dln_b, dws, jnp.sum(dbs_acc, axis=-1), dgbn, dgf, jnp.broadcast_to(loss, (1, 128)))

    dqkv_f, dlr_f, dw2f, dgbf, dsc_f, *got_a = _gla_bwd(p, do, st_f, w2f, gbf, None, False, 512, "gla_bwd_f",
                                                        xch.early_a(small, blocks_a))
    dqkv, dlr, dw2b, dgbb, dsc_b, *got_b = _gla_bwd(p, do, st_b, w2b, gbb, (dqkv_f, dlr_f), True, 512, "gla_bwd_b",
                                                    xch.early_b(blocks_b))
    dwk_c, dwv_c, dwl_c, dmodc, dg_c, dw2c, dgbc = _ctx_bwd(ctx, norm_g, scale_c, shift_c, w_pad, w2f, w2b, gbf, gbb, dsc_f, dsc_b)
    dw_qkv, dw_lr = _tn_matmul(h, [dqkv, dlr], 1024, "dw_qkv_lr")
    blocks_late = _pack_dw_late(dw_qkv, dw_lr, dwk_c, dwv_c, dwl_c)
    dw2 = jnp.stack([dw2f[0:RANK] + dw2c[0, 0:RANK], dw2b[RANK:2 * RANK] + dw2c[1, RANK:2 * RANK]])
    dgb = jnp.concatenate([dgbf + dgbc[0], dgbb + dgbc[1]], axis=0)
    dgt = jnp.concatenate([jnp.transpose(dw2.reshape(2, RANK, NDEV, 32), (2, 0, 1, 3)).reshape(NDEV, 2 * RANK * 32),
                           jnp.transpose(dgb.reshape(2, NDEV, 32), (1, 0, 2)).reshape(NDEV, 64),
                           jnp.zeros((NDEV, 64), f32)], axis=1).reshape(NDEV, 9, 128)
    gx, dshift, dscale, dg, *got_late = _in_bwd(x, dx1, dqkv, dzbua, dva, dzag, dlr, w_pad, norm_g, scale, 512,
                                                xch.late(blocks_late, dgt))
    return dict(loss=loss, gx=gx, dmod=jnp.concatenate([dshift, dscale, dgate], axis=1), dmodc=dmodc, dnorm_g=dg + dg_c,
                small=small, blocks_a=blocks_a, blocks_b=blocks_b, blocks_late=blocks_late, dw2=dw2, dgb=dgb,
                dw_pa=dw_pa, dw_pb=dw_pb, dw_out=dw_out, got_first=got_first, got_a=got_a, got_b=got_b, got_late=got_late)


def _rows128(*vs):
    out = []
    for t in vs:
        t = t.reshape(-1)
        pad = (-t.shape[0]) % 128
        out.append(jnp.pad(t, (0, pad)) if pad else t)
    return jnp.concatenate(out).reshape(-1, 128)


def kernel(x, c, ctx, c_ctx, w_mod, b_mod, norm_g, w_in, a_ln_g, a_ln_b, a_ws, a_bs, b_gate_w2, b_gate_b, b_norm_g, w_proj_a, w_proj_b, w_out, final_norm_g, loss_target, m_c_ctx, m_w_mod, m_b_mod, m_norm_g, m_w_in, m_a_ln_g, m_a_ln_b, m_a_ws, m_a_bs, m_b_gate_w2, m_b_gate_b, m_b_norm_g, m_w_proj_a, m_w_proj_b, m_w_out, m_final_norm_g, v_c_ctx, v_w_mod, v_b_mod, v_norm_g, v_w_in, v_a_ln_g, v_a_ln_b, v_a_ws, v_a_bs, v_b_gate_w2, v_b_gate_b, v_b_norm_g, v_w_proj_a, v_w_proj_b, v_w_out, v_final_norm_g):
    me = _me()
    ncol = w_mod.shape[2]

    gate_mine = _rows128(jnp.concatenate([b_gate_w2.reshape(-1), b_gate_b.reshape(-1)]))
    bm_mine = lax.dynamic_slice(b_mod, (0, me * ncol), (1, ncol))
    cs, mods, wg, gates = _gather_first(c, c_ctx.reshape(1, D), w_mod[0], bm_mine, w_in[0].astype(bf16), gate_mine)
    w_pad = _repack_w(wg)
    gflat = gates.reshape(NDEV, 9 * 128)
    w2 = jnp.transpose(gflat[:, 0:2 * RANK * 32].reshape(NDEV, 2, RANK, 32), (1, 2, 0, 3)).reshape(2, RANK, KW)
    gb = jnp.transpose(gflat[:, 2 * RANK * 32:2 * RANK * 32 + 64].reshape(NDEV, 2, 32), (1, 0, 2)).reshape(2, KW)

    mods = jnp.transpose(mods, (1, 0, 2)).reshape(16, 3 * D)
    mod = lax.dynamic_slice(mods, (me, 0), (1, 3 * D))
    modc = mods[8:9, 0:2 * D]

    xch = _Exchanges(w_proj_a[0].astype(bf16), w_proj_b[0].astype(bf16), w_out[0].astype(bf16))
    r = _local_step(x[0], ctx[0], loss_target[0], mod, modc, norm_g, w_pad, a_ln_g, a_ln_b, a_ws[0], a_bs[0], w2, gb,
                    b_norm_g, final_norm_g.reshape(1, D), xch)
    p_out, p_pa, p_pb = r["got_first"]
    smalls_e, p_in_a = r["got_a"]
    (p_in_b,) = r["got_b"]
    _, _, p_in_late, p_gt = r["got_late"]

    n_e = (AW + AW + 4 * ACH * ACH + AW + VW + D) // 128
    row = lambda t: t.reshape(1, D)
    rep_e = _adam_params(smalls_e, [a_ln_g, a_ln_b, a_ws, a_bs, b_norm_g, row(final_norm_g)],
                         [m_a_ln_g, m_a_ln_b, m_a_ws, m_a_bs, m_b_norm_g, row(m_final_norm_g)],
                         [v_a_ln_g, v_a_ln_b, v_a_ws, v_a_bs, v_b_norm_g, row(v_final_norm_g)], "adam_rep_early")
    rep_e = [t[0:5] + (t[5].reshape(D),) for t in rep_e]
    losses = smalls_e[:, n_e, 0]
    loss = losses[0]
    for i in range(1, NDEV):
        loss = loss + losses[i]

    dbm = r["dmod"] + jnp.concatenate([r["dmodc"], jnp.zeros((1, D), f32)], axis=1)
    wm, ng, bmod_r, cc = _mod_tail(r["dnorm_g"], dbm, r["dmod"], r["dmodc"], cs, w_mod, m_w_mod, v_w_mod,
                                   norm_g, m_norm_g, v_norm_g, b_mod, m_b_mod, v_b_mod,
                                   c_ctx.reshape(1, D), m_c_ctx.reshape(1, D), v_c_ctx.reshape(1, D))

    a_in = _adam_w_in(p_in_a, p_in_b, p_in_late, w_in, m_w_in, v_w_in)
    blk = _adam_blocks([p_pa, p_pb, p_out], [w_proj_a, w_proj_b, w_out], [m_w_proj_a, m_w_proj_b, m_w_out],
                       [v_w_proj_a, v_w_proj_b, v_w_out], "adam_proj_out")
    a_pa, a_pb, a_out = ([t[i] for t in blk] for i in range(3))
    gate_m = _rows128(jnp.concatenate([m_b_gate_w2.reshape(-1), m_b_gate_b.reshape(-1)]))
    gate_v = _rows128(jnp.concatenate([v_b_gate_w2.reshape(-1), v_b_gate_b.reshape(-1)]))
    a_gt = [t.reshape(-1) for t in _adam(p_gt, gate_mine, gate_m, gate_v, 9, "adam_gate")]
    nw2 = 2 * RANK * 32
    sh = [(a_in[k], a_pa[k], a_pb[k], a_out[k], a_gt[k][0:nw2].reshape(1, 2, RANK, 32),
           a_gt[k][nw2:nw2 + 64].reshape(1, 2, 32)) for k in range(4)]

    outs = [loss, r["gx"][None]]
    for k in range(4):
        lg, lb, aws, abs_, bng, fng = rep_e[k]
        n_g, bmod = ng[k], bmod_r[k]
        s_in, s_pa, s_pb, s_out, s_w2, s_gb = sh[k]
        outs += [cc[k].reshape(D), wm[k], bmod, n_g, s_in, lg, lb, aws, abs_, s_w2, s_gb, bng, s_pa, s_pb, s_out, fng]
    return tuple(outs)
```

```python
import jax
import jax.numpy as jnp
from jax import lax
from jax.experimental import pallas as pl
from jax.experimental.pallas import tpu as pltpu

f32, bf16 = jnp.float32, jnp.bfloat16

D = 1024
CTX = 256
EPS = 1e-6
AW = 512
ACH = 128
GW = 64
KW = 256
VW = 512
NH = 4
HK = 64
HV = 128
RANK = 16
TAU = 16.0
CH = 64
QSCALE = HK ** -0.5
INW = 5152
NDEV = 8

PQ, PK, PV, PZB, PUA, PVA, PZA, PG1, PG2, PLR, PW = 0, 256, 512, 1024, 1536, 2048, 2560, 3072, 4096, 5120, 5248
LRW = 128

ADAM_LR, ADAM_B1, ADAM_B2, ADAM_EPS, ADAM_WD, ADAM_STEP = 0.001, 0.9, 0.999, 1e-08, 0.01, 10

VMEM_LIMIT = 56 * 1024 * 1024
MESH = pl.DeviceIdType.MESH


def _cp(sem=None):
    return pltpu.CompilerParams(dimension_semantics=sem, vmem_limit_bytes=VMEM_LIMIT)


def _dot(a, b):
    return jnp.dot(a.astype(bf16), b.astype(bf16), preferred_element_type=f32)


def _nt(a, b):
    return lax.dot_general(a.astype(bf16), b.astype(bf16), (((1,), (1,)), ((), ())), preferred_element_type=f32)


def _tn(a, b):
    return lax.dot_general(a.astype(bf16), b.astype(bf16), (((0,), (0,)), ((), ())), preferred_element_type=f32)


def _dot_hi(a, b):
    return jnp.dot(a, b, preferred_element_type=f32, precision=lax.Precision.HIGHEST)


def _sigmoid(x):
    return 1.0 / (1.0 + jnp.exp(-x))


def _log_sigmoid(x):
    return jnp.minimum(x, 0.0) - jnp.log(1.0 + jnp.exp(-jnp.abs(x)))


def _silu_and_grad(z):
    s = _sigmoid(z)
    return z * s, s * (1.0 + z * (1.0 - s))


def _me():
    return 4 * lax.axis_index("x") + 2 * lax.axis_index("y") + lax.axis_index("c")


def _peer(k):
    x, y, c = lax.axis_index("x"), lax.axis_index("y"), lax.axis_index("c")
    px = 1 - x if k & 4 else x
    py = 1 - y if k & 2 else y
    pc = 1 - c if k & 1 else c
    return (px, py, pc), 4 * px + 2 * py + pc


def _fanout(srcs, dsts, send_sems, recv_sems, local_sems, owners=None):
    me = _me()
    n = len(srcs)
    owns = lambda a, j: True if owners is None or owners[a] is None else owners[a](j)

    def guarded(cond, fn):
        if cond is True:
            fn()
        else:
            pl.when(cond)(fn)

    def copies(with_recvs):
        local = [pltpu.make_async_copy(srcs[a](me), dsts[a](me), local_sems.at[a]) for a in range(n)]
        sends, recvs = [], []
        for k in range(1, NDEV):
            dev, idx = _peer(k)
            for a in range(n):
                s = (k - 1) * n + a
                sends.append((owns(a, idx), pltpu.make_async_remote_copy(
                    src_ref=srcs[a](idx), dst_ref=dsts[a](me), send_sem=send_sems.at[s], recv_sem=recv_sems.at[s],
                    device_id=dev, device_id_type=MESH)))
                if with_recvs:
                    recvs.append((owns(a, me), pltpu.make_async_remote_copy(
                        src_ref=srcs[a](idx), dst_ref=dsts[a](idx), send_sem=send_sems.at[s], recv_sem=recv_sems.at[s],
                        device_id=dev, device_id_type=MESH)))
        return local, sends, recvs

    def start():
        local, sends, _ = copies(False)
        for a, cp in enumerate(local):
            guarded(owns(a, me), cp.start)
        for cond, cp in sends:
            guarded(cond, cp.start)

    def finish():
        local, sends, recvs = copies(True)
        for cond, cp in recvs:
            guarded(cond, cp.wait_recv)
        for cond, cp in sends:
            guarded(cond, cp.wait_send)
        for a, cp in enumerate(local):
            guarded(owns(a, me), cp.wait)

    return start, finish


class _Comm:
    def __init__(self, ins, outs, plan):
        self.ins, self.outs, self.plan = list(ins), list(outs), plan
        self.n = len(self.outs)

    def specs(self):
        hbm = pl.BlockSpec(memory_space=pl.ANY)
        return [hbm] * len(self.ins), [hbm] * self.n, _fanout_sems(self.n) if self.n else []

    def run(self, in_refs, out_refs, sems, nsteps, compute):
        if not self.n:
            compute()
            return

        def hooks():
            srcs, dsts, owners = self.plan(in_refs, out_refs)
            return _fanout(srcs, dsts, sems[0], sems[1], sems[2], owners)

        pl.when(pl.program_id(0) == 0)(lambda: hooks()[0]())
        compute()
        pl.when(pl.program_id(0) == nsteps - 1)(lambda: hooks()[1]())


LATE_MID_STEP = 1


class _LateComm:
    def __init__(self, blocks, dgt):
        sds = jax.ShapeDtypeStruct
        self.ins = [blocks, dgt]
        self.outs = [sds((D, SHARD), bf16), sds((D, SHARD), bf16), sds((4, D, SHARD), bf16), sds(dgt.shape, f32)]
        self.n = len(self.outs)

    def specs(self):
        hbm = pl.BlockSpec(memory_space=pl.ANY)
        scratch = [pltpu.VMEM((3, D, SHARD), bf16), pltpu.SemaphoreType.DMA((2,)), pltpu.SemaphoreType.DMA((4,)),
                   pltpu.SemaphoreType.DMA((3,))] + _fanout_sems(1)
        return [hbm] * 2, [hbm] * self.n, scratch

    def run(self, in_refs, out_refs, scratch, nsteps, compute):
        late_ref, dgt_ref = in_refs
        sib_ref, pair_ref, parts_ref, gt_ref = out_refs
        vbuf, send_sems, recv_sems, local_sems, g_send, g_recv, g_local = scratch
        x, y, c = lax.axis_index("x"), lax.axis_index("y"), lax.axis_index("c")
        chip = 2 * x + y
        is_owner_chip = chip == 0
        step = pl.program_id(0)

        def to_sibling():
            return pltpu.make_async_remote_copy(src_ref=late_ref.at[1 - c], dst_ref=sib_ref, send_sem=send_sems.at[0],
                                                recv_sem=recv_sems.at[0], device_id=(x, y, 1 - c), device_id_type=MESH)

        def to_owner(k):
            return pltpu.make_async_remote_copy(src_ref=pair_ref, dst_ref=parts_ref.at[k], send_sem=send_sems.at[1],
                                                recv_sem=recv_sems.at[k], device_id=(0, 0, c), device_id_type=MESH)

        def own_copy():
            return pltpu.make_async_copy(pair_ref, parts_ref.at[0], local_sems.at[2])

        def gates():
            return _fanout([lambda j: dgt_ref.at[j]], [lambda j: gt_ref.at[j]], g_send, g_recv, g_local)

        @pl.when(step == 0)
        def _():
            to_sibling().start()
            gates()[0]()

        compute()

        @pl.when(step == LATE_MID_STEP)
        def _():
            mine = pltpu.make_async_copy(late_ref.at[c], vbuf.at[0], local_sems.at[0])
            mine.start()
            to_sibling().wait_recv()
            theirs = pltpu.make_async_copy(sib_ref, vbuf.at[1], local_sems.at[1])
            theirs.start()
            mine.wait()
            theirs.wait()
            vbuf[2] = (vbuf[0].astype(f32) + vbuf[1].astype(f32)).astype(bf16)
            pltpu.sync_copy(vbuf.at[2], pair_ref)
            pl.when(is_owner_chip)(lambda: own_copy().start())
            pl.when(jnp.logical_not(is_owner_chip))(lambda: to_owner(chip).start())

        @pl.when(step == nsteps - 1)
        def _():
            @pl.when(is_owner_chip)
            def _():
                for k in range(1, 4):
                    to_owner(k).wait_recv()
                own_copy().wait()

            pl.when(jnp.logical_not(is_owner_chip))(lambda: to_owner(chip).wait_send())
            to_sibling().wait_send()
            gates()[1]()


def _fanout_sems(n):
    return [pltpu.SemaphoreType.DMA(((NDEV - 1) * n,)), pltpu.SemaphoreType.DMA(((NDEV - 1) * n,)), pltpu.SemaphoreType.DMA((n,))]


def _lanes(j):
    return pl.ds(pl.multiple_of(j * 128, 128), 128)


def _gather_first(c_row, cctx_row, wm, bm, wi, gate):
    def body(c_ref, cctx_ref, wm_ref, bm_ref, wi_ref, g_ref, cs_ref, mods_ref, owi, og, call_ref, mine_ref,
             send_sems, recv_sems, local_sems, c_send, c_recv, c_local, m_send, m_recv, m_local):
        x, y, c = lax.axis_index("x"), lax.axis_index("y"), lax.axis_index("c")
        sibling = (x, y, 1 - c)
        chips = [(1 - x, y), (x, 1 - y), (1 - x, 1 - y)]
        index = lambda px, py, pc: 4 * px + 2 * py + pc
        arrays = ((wi_ref, owi), (g_ref, og))
        n = len(arrays)

        def copy(a, k, block, to, own=False):
            src, out = arrays[a]
            return pltpu.make_async_remote_copy(
                src_ref=src if own else out.at[index(*block)], dst_ref=out.at[index(*block)],
                send_sem=send_sems.at[k * n + a], recv_sem=recv_sems.at[k * n + a], device_id=to, device_id_type=MESH)

        c_start, c_finish = _fanout([lambda j: c_ref], [lambda j: call_ref.at[j]], c_send, c_recv, c_local)
        c_start()
        mine = [pltpu.make_async_copy(src, out.at[index(x, y, c)], local_sems.at[a]) for a, (src, out) in enumerate(arrays)]
        first = [copy(a, 0, (x, y, c), sibling, own=True) for a in range(n)]
        first += [copy(a, 1 + j, (x, y, c), (*chip, c), own=True) for j, chip in enumerate(chips) for a in range(n)]
        for cp in mine + first:
            cp.start()

        c_finish()
        cs = jnp.concatenate([call_ref[j] for j in range(NDEV)] + [cctx_ref[...], jnp.zeros((16 - NDEV - 1, D), f32)], axis=0)
        cs_ref[...] = cs
        s, _ = _silu_and_grad(cs)
        mine_ref[...] = _dot_hi(s, wm_ref[...]) + bm_ref[...]
        m_start, m_finish = _fanout([lambda j: mine_ref], [lambda j: mods_ref.at[j]], m_send, m_recv, m_local)
        m_start()

        passed = []
        for j, chip in enumerate(chips):
            for a in range(n):
                copy(a, 1 + j, (*chip, c), (x, y, c)).wait_recv()
            for a in range(n):
                cp = copy(a, 4 + j, (*chip, c), sibling)
                cp.start()
                passed.append(cp)
        for a in range(n):
            copy(a, 0, sibling, (x, y, c)).wait_recv()
        for j, chip in enumerate(chips):
            for a in range(n):
                copy(a, 4 + j, (*chip, 1 - c), (x, y, c)).wait_recv()
        for cp in first + passed:
            cp.wait_send()
        for cp in mine:
            cp.wait()
        m_finish()

    hbm = pl.BlockSpec(memory_space=pl.ANY)
    vm = pl.BlockSpec(memory_space=pltpu.VMEM)
    ncol = wm.shape[1]
    return pl.pallas_call(
        body, name="gather_first",
        out_shape=(jax.ShapeDtypeStruct((16, D), f32), jax.ShapeDtypeStruct((NDEV, 16, ncol), f32),
                   jax.ShapeDtypeStruct((NDEV,) + wi.shape, bf16), jax.ShapeDtypeStruct((NDEV,) + gate.shape, f32)),
        in_specs=[vm, vm, vm, vm, hbm, hbm], out_specs=(vm, vm, hbm, hbm),
        scratch_shapes=[pltpu.VMEM((NDEV, 1, D), f32), pltpu.VMEM((16, ncol), f32)] + _fanout_sems(2) + _fanout_sems(1) + _fanout_sems(1),
        compiler_params=_cp(),
    )(c_row, cctx_row, wm, bm, wi, gate)


SHARD = INW // NDEV
ROWS_RP = 128


def _overlap(lo, hi, a, b):
    s, e = max(lo, a), min(hi, b)
    return (s, e) if s < e else None


def _repack_w(wg):
    segs = ((0, 1024, PQ), (1024, 1024 + 2 * RANK, PLR), (1024 + 2 * RANK, INW, PZB))

    def body(g_ref, o_ref):
        for j in range(NDEV):
            lo, hi = j * SHARD, (j + 1) * SHARD
            for a, b, pad0 in segs:
                ov = _overlap(lo, hi, a, b)
                if ov:
                    s, e = ov
                    o_ref[:, pad0 + s - a:pad0 + e - a] = g_ref[j, :, s - lo:e - lo]
        o_ref[:, PLR + 2 * RANK:PW] = jnp.zeros((ROWS_RP, PW - PLR - 2 * RANK), bf16)

    return pl.pallas_call(
        body, name="repack_w", grid=(D // ROWS_RP,), out_shape=jax.ShapeDtypeStruct((D, PW), bf16),
        in_specs=[pl.BlockSpec((NDEV, ROWS_RP, SHARD), lambda i: (0, i, 0))],
        out_specs=pl.BlockSpec((ROWS_RP, PW), lambda i: (i, 0)), compiler_params=_cp(("arbitrary",)),
    )(wg)


LATE_END = 1024 + 2 * RANK
LATE_DESTS = 2


def _pack_blocks(o_ref, srcs, dests, dtype):
    for n, j in enumerate(dests):
        lo, hi = j * SHARD, (j + 1) * SHARD
        done = lo
        for a, b, src in srcs:
            ov = _overlap(lo, hi, a, b)
            if ov:
                s, e = ov
                if s > done:
                    o_ref[n, :, done - lo:s - lo] = jnp.zeros((ROWS_RP, s - done), dtype)
                o_ref[n, :, s - lo:e - lo] = src[:, s - a:e - a].astype(dtype)
                done = e
        if done < hi:
            o_ref[n, :, done - lo:hi - lo] = jnp.zeros((ROWS_RP, hi - done), dtype)


def _pack_dw_early(dw_zbua, dw_va, dw_zag, row0, nrows, name):
    def body(zbua_ref, va_ref, zag_ref, o_ref):
        _pack_blocks(o_ref, ((LATE_END, 2080, zbua_ref), (2080, 2592, va_ref), (2592, INW, zag_ref)), range(NDEV), bf16)

    row = lambda w: pl.BlockSpec((ROWS_RP, w), lambda i: (i + row0 // ROWS_RP, 0))
    return pl.pallas_call(
        body, name=name, grid=(nrows // ROWS_RP,), out_shape=jax.ShapeDtypeStruct((NDEV, nrows, SHARD), bf16),
        in_specs=[row(2 * AW), row(AW), row(AW + 2 * D)],
        out_specs=pl.BlockSpec((NDEV, ROWS_RP, SHARD), lambda i: (0, i, 0)), compiler_params=_cp(("arbitrary",)),
    )(dw_zbua, dw_va, dw_zag)


def _pack_dw_late(dw_qkv, dw_lr, dwk_c, dwv_c, dwl_c):
    def body(qkv_ref, lr_ref, kc_ref, vc_ref, lc_ref, o_ref):
        qkv = qkv_ref[...] + jnp.concatenate([jnp.zeros((ROWS_RP, KW), f32), kc_ref[...], vc_ref[...]], axis=1)
        lr = lr_ref[...] + lc_ref[...]
        _pack_blocks(o_ref, ((0, 1024, qkv), (1024, LATE_END, lr)), range(LATE_DESTS), bf16)

    row = lambda w: pl.BlockSpec((ROWS_RP, w), lambda i: (i, 0))
    return pl.pallas_call(
        body, name="pack_dw_late", grid=(D // ROWS_RP,), out_shape=jax.ShapeDtypeStruct((LATE_DESTS, D, SHARD), bf16),
        in_specs=[row(2 * KW + VW), row(LRW), row(KW), row(VW), row(LRW)],
        out_specs=pl.BlockSpec((LATE_DESTS, ROWS_RP, SHARD), lambda i: (0, i, 0)), compiler_params=_cp(("arbitrary",)),
    )(dw_qkv, dw_lr, dwk_c, dwv_c, dwl_c)


def _mod_tail(dnorm, dbm, dmod, dmodc, cs, wm, m_wm, v_wm, norm_g, m_ng, v_ng, b_mod, m_bm, v_bm, c_ctx, m_cc, v_cc):
    ncol = wm.shape[2]

    def body(dn_ref, dbm_ref, dmod_ref, dmodc_ref, cs_ref, wm_ref, mwm_ref, vwm_ref, ng_ref, mng_ref, vng_ref,
             bm_ref, mbm_ref, vbm_ref, cc_ref, mcc_ref, vcc_ref, *rest):
        o_wm, o_ng, o_bm, o_cc = rest[0:4], rest[4:8], rest[8:12], rest[12:16]
        a_dn, a_dbm, a_dmod, a_dmodc, dm_rows, gc_ref, a_gc = rest[16:23]
        s1, r1, l1, s2, r2, l2 = rest[23:29]
        start, finish = _fanout([lambda j: dn_ref, lambda j: dbm_ref, lambda j: dmod_ref, lambda j: dmodc_ref],
                                [lambda j, r=r: r.at[j] for r in (a_dn, a_dbm, a_dmod, a_dmodc)], s1, r1, l1)
        start()
        finish()

        def total(ref):
            t = ref[0]
            for j in range(1, NDEV):
                t = t + ref[j]
            return t

        _adam_update(total(a_dn), ng_ref, mng_ref, vng_ref, *o_ng)
        _adam_update(total(a_dbm), bm_ref, mbm_ref, vbm_ref, *o_bm)
        dmodc_tot = jnp.concatenate([total(a_dmodc), jnp.zeros((1, D), f32)], axis=1)
        dm_rows[...] = jnp.concatenate([a_dmod[j] for j in range(NDEV)] + [dmodc_tot, jnp.zeros((16 - NDEV - 1, 3 * D), f32)], axis=0)
        dm = dm_rows[:, pl.ds(pl.multiple_of(_me() * ncol, 128), ncol)]
        s, ds = _silu_and_grad(cs_ref[...])
        part = lax.dot_general(dm[8:9, :], wm_ref[0], (((1,), (1,)), ((), ())), preferred_element_type=f32,
                               precision=lax.Precision.HIGHEST)
        gc_ref[...] = part * ds[8:9, :]
        start2, finish2 = _fanout([lambda j: gc_ref], [lambda j: a_gc.at[j]], s2, r2, l2)
        start2()
        g = lax.dot_general(s, dm, (((0,), (0,)), ((), ())), preferred_element_type=f32, precision=lax.Precision.HIGHEST)
        _adam_update(g[None], wm_ref, mwm_ref, vwm_ref, *o_wm)
        finish2()
        _adam_update(total(a_gc), cc_ref, mcc_ref, vcc_ref, *o_cc)

    like = lambda a: [jax.ShapeDtypeStruct(a.shape, f32)] * 4
    row = lambda n: pltpu.VMEM((NDEV, 1, n), f32)
    res = pl.pallas_call(
        body, name="mod_tail", out_shape=tuple(like(wm) + like(norm_g) + like(b_mod) + like(c_ctx)),
        scratch_shapes=[row(D), row(3 * D), row(3 * D), row(2 * D), pltpu.VMEM((16, 3 * D), f32), pltpu.VMEM((1, D), f32), row(D)]
        + _fanout_sems(4) + _fanout_sems(1),
        compiler_params=_cp(),
    )(dnorm, dbm, dmod, dmodc, cs, wm, m_wm, v_wm, norm_g, m_ng, v_ng, b_mod, m_bm, v_bm, c_ctx, m_cc, v_cc)
    return res[0:4], res[4:8], res[8:12], res[12:16]


def _adam_update(g, w_ref, m_ref, v_ref, go_ref, d_ref, mo_ref, vo_ref):
    c1 = 1.0 / (1.0 - ADAM_B1 ** ADAM_STEP)
    c2 = 1.0 / (1.0 - ADAM_B2 ** ADAM_STEP)
    mn = ADAM_B1 * m_ref[...] + (1.0 - ADAM_B1) * g
    vn = ADAM_B2 * v_ref[...] + (1.0 - ADAM_B2) * (g * g)
    go_ref[...] = g
    mo_ref[...] = mn
    vo_ref[...] = vn
    d_ref[...] = -ADAM_LR * ((mn * c1) / (jnp.sqrt(vn * c2) + ADAM_EPS) + ADAM_WD * w_ref[...])


def _adam_w_in(parts_a, parts_b, parts_late, w, m, v):
    na = EARLY_A_ROWS // ROWS_RP

    def body(a_ref, b_ref, l_ref, w_ref, m_ref, v_ref, go_ref, d_ref, mo_ref, vo_ref):
        me = _me()
        first = pl.program_id(0) < na
        early = jnp.where(first, a_ref[0], b_ref[0]).astype(f32)
        for i in range(1, NDEV):
            early = early + jnp.where(first, a_ref[i], b_ref[i]).astype(f32)
        late = l_ref[0].astype(f32)
        for i in range(1, 4):
            late = late + l_ref[i].astype(f32)
        g = jnp.where(me >= LATE_DESTS - 1, early, 0.0) + jnp.where(me < LATE_DESTS, late, 0.0)
        _adam_update(g, w_ref, m_ref, v_ref, go_ref, d_ref, mo_ref, vo_ref)

    blk = pl.BlockSpec((None, ROWS_RP, SHARD), lambda i: (0, i, 0))
    return pl.pallas_call(
        body, name="adam_w_in", grid=(D // ROWS_RP,), out_shape=tuple(jax.ShapeDtypeStruct((1, D, SHARD), f32) for _ in range(4)),
        in_specs=[pl.BlockSpec((NDEV, ROWS_RP, SHARD), lambda i: (0, jnp.minimum(i, na - 1), 0)),
                  pl.BlockSpec((NDEV, ROWS_RP, SHARD), lambda i: (0, jnp.maximum(i - na, 0), 0)),
                  pl.BlockSpec((4, ROWS_RP, SHARD), lambda i: (0, i, 0)), blk, blk, blk],
        out_specs=(blk, blk, blk, blk), compiler_params=_cp(("arbitrary",)),
    )(parts_a, parts_b, parts_late, w, m, v)


def _adam_params(parts, ws, ms, vs, name):
    p = parts.shape[0]
    k = len(ws)
    nrows = [w.size // 128 for w in ws]
    starts = [sum(nrows[:i]) for i in range(k)]

    def shaped(g, shape):
        if len(shape) == 2:
            return jnp.concatenate([g[r:r + 1] for r in range(g.shape[0])], axis=1)
        return g.reshape(shape)

    def body(*refs):
        g_ref = refs[0]
        w_refs, m_refs, v_refs = refs[1:1 + k], refs[1 + k:1 + 2 * k], refs[1 + 2 * k:1 + 3 * k]
        outs = refs[1 + 3 * k:]
        for i in range(k):
            rows = slice(starts[i], starts[i] + nrows[i])
            g = g_ref[0, rows, :]
            for j in range(1, p):
                g = g + g_ref[j, rows, :]
            _adam_update(shaped(g, ws[i].shape), w_refs[i], m_refs[i], v_refs[i], outs[i], outs[k + i], outs[2 * k + i], outs[3 * k + i])

    vm = pl.BlockSpec(memory_space=pltpu.VMEM)
    res = pl.pallas_call(
        body, name=name, out_shape=tuple(jax.ShapeDtypeStruct(w.shape, f32) for _ in range(4) for w in ws),
        in_specs=[vm] * (1 + 3 * k), out_specs=(vm,) * (4 * k), compiler_params=_cp(),
    )(parts, *ws, *ms, *vs)
    return [res[i * k:(i + 1) * k] for i in range(4)]


def _adam_blocks(parts, ws, ms, vs, name):
    k = len(ws)

    def body(*refs):
        g_refs, w_refs, m_refs, v_refs = refs[0:k], refs[k:2 * k], refs[2 * k:3 * k], refs[3 * k:4 * k]
        outs = refs[4 * k:]
        for i in range(k):
            g = g_refs[i][0].astype(f32)
            for j in range(1, parts[i].shape[0]):
                g = g + g_refs[i][j].astype(f32)
            _adam_update(g[None], w_refs[i], m_refs[i], v_refs[i], outs[i], outs[k + i], outs[2 * k + i], outs[3 * k + i])

    vm = pl.BlockSpec(memory_space=pltpu.VMEM)
    res = pl.pallas_call(
        body, name=name, out_shape=tuple(jax.ShapeDtypeStruct(w.shape, f32) for _ in range(4) for w in ws),
        in_specs=[vm] * (4 * k), out_specs=(vm,) * (4 * k), compiler_params=_cp(),
    )(*parts, *ws, *ms, *vs)
    return [res[i * k:(i + 1) * k] for i in range(4)]


def _adam(parts, w, m, v, rows, name):
    p, r, c = parts.shape
    lead = w.ndim - 2

    def body(g_ref, w_ref, m_ref, v_ref, go_ref, d_ref, mo_ref, vo_ref):
        g = g_ref[0].astype(f32)
        for i in range(1, p):
            g = g + g_ref[i].astype(f32)
        _adam_update(g, w_ref, m_ref, v_ref, go_ref, d_ref, mo_ref, vo_ref)

    blk = pl.BlockSpec((None,) * lead + (rows, c), lambda i: (0,) * lead + (i, 0))
    return pl.pallas_call(
        body, name=name, grid=(r // rows,), out_shape=tuple(jax.ShapeDtypeStruct(w.shape, f32) for _ in range(4)),
        in_specs=[pl.BlockSpec((p, rows, c), lambda i: (0, i, 0)), blk, blk, blk], out_specs=(blk, blk, blk, blk),
        compiler_params=_cp(("arbitrary",)),
    )(parts, w, m, v)


def _in_proj(x, g, scale, shift, w_pad, tl, comm):
    l = x.shape[0]
    c_in, c_out, c_sems = comm.specs()

    def body(*refs):
        x_ref, g_ref, sc_ref, sh_ref, w_ref = refs[:5]
        cin = refs[5:5 + len(c_in)]
        p_ref, h_ref = refs[5 + len(c_in):7 + len(c_in)]
        cout = refs[7 + len(c_in):7 + len(c_in) + comm.n]
        sems = refs[7 + len(c_in) + comm.n:]

        def compute():
            xv = x_ref[...]
            r = lax.rsqrt(jnp.mean(xv * xv, axis=-1, keepdims=True) + EPS)
            h = (xv * r) * (g_ref[...] * (1.0 + sc_ref[...])) + sh_ref[...]
            hb = h.astype(bf16)
            h_ref[...] = hb
            p_ref[...] = jnp.dot(hb, w_ref[...], preferred_element_type=f32).astype(bf16)

        comm.run(cin, cout, sems, l // tl, compute)

    vec = pl.BlockSpec((1, D), lambda i: (0, 0))
    return pl.pallas_call(
        body, name="in_proj", grid=(l // tl,),
        out_shape=(jax.ShapeDtypeStruct((l, PW), bf16), jax.ShapeDtypeStruct((l, D), bf16)) + tuple(comm.outs),
        in_specs=[pl.BlockSpec((tl, D), lambda i: (i, 0)), vec, vec, vec, pl.BlockSpec((D, PW), lambda i: (0, 0))] + c_in,
        out_specs=(pl.BlockSpec((tl, PW), lambda i: (i, 0)), pl.BlockSpec((tl, D), lambda i: (i, 0))) + tuple(c_out),
        scratch_shapes=c_sems, compiler_params=_cp(("arbitrary",)),
    )(x, g, scale, shift, w_pad, *comm.ins)


def _tri(rev):
    i = lax.broadcasted_iota(jnp.int32, (CH, CH), 0)
    j = lax.broadcasted_iota(jnp.int32, (CH, CH), 1)
    return jnp.where((j >= i) if rev else (j <= i), 1.0, 0.0).astype(f32)


def _head_masks():
    lane = lax.broadcasted_iota(jnp.int32, (1, KW), 1) // HK
    return [jnp.where(lane == h, 1.0, 0.0).astype(f32) for h in range(NH)]


def _block_diag():
    r = lax.broadcasted_iota(jnp.int32, (VW, KW), 0) // HV
    c = lax.broadcasted_iota(jnp.int32, (VW, KW), 1) // HK
    return jnp.where(r == c, 1.0, 0.0).astype(f32)


def _decay(lr, w2, gb, tri, rev):
    logits = _dot(lr, w2) + gb
    a = _log_sigmoid(logits) * (1.0 / TAU)
    c = _dot_hi(tri, a)
    cl = c[0:1, :] if rev else c[CH - 1:CH, :]
    return logits, c, cl


def _state_fwd(k, v, c, cl, st, bd):
    return st * jnp.exp(cl) + bd * _tn(v, k * jnp.exp(cl - c))


def _state_bwd(k, v, c, cl, st0, dst, trit):
    ecl = jnp.exp(cl)
    edec = jnp.exp(cl - c)
    kdec = k * edec
    dv = _nt(kdec, dst)
    dkdec = _dot(v, dst)
    dcl = jnp.sum(dst * st0, axis=0, keepdims=True) * ecl + jnp.sum(dkdec * kdec, axis=0, keepdims=True)
    da = _dot_hi(trit, -dkdec * kdec) + dcl
    return dkdec * edec, dv, da, dst * ecl


def _bdot(a, b):
    return lax.dot_general(a.astype(bf16), b.astype(bf16), (((2,), (1,)), ((0,), (0,))), preferred_element_type=f32)


def _bnt(a, b):
    return lax.dot_general(a.astype(bf16), b.astype(bf16), (((2,), (2,)), ((0,), (0,))), preferred_element_type=f32)


def _btn(a, b):
    return lax.dot_general(a.astype(bf16), b.astype(bf16), (((1,), (1,)), ((0,), (0,))), preferred_element_type=f32)


def _scan_chunks(x, rev):
    nc = x.shape[0]
    hi = x.astype(bf16)
    r1 = x - hi.astype(f32)
    mid = r1.astype(bf16)
    lo = (r1 - mid.astype(f32)).astype(bf16)
    terms = jnp.concatenate([hi, mid, lo], axis=1)
    tri3 = jnp.broadcast_to(jnp.concatenate([_tri(rev)] * 3, axis=1).astype(bf16)[None], (nc, CH, 3 * CH))
    return lax.dot_general(tri3, terms, (((2,), (1,)), ((0,), (0,))), preferred_element_type=f32)


class _Tile:
    pass


def _tile_prep(q_ref, k_ref, lr_ref, w2_ref, gb_ref, rev, nc):
    t = _Tile()
    tg = nc * CH
    t.logits = _dot(lr_ref[...], w2_ref[...]) + gb_ref[...]
    c = _scan_chunks((_log_sigmoid(t.logits) * (1.0 / TAU)).reshape(nc, CH, KW), rev)
    cl = c[:, 0:1, :] if rev else c[:, CH - 1:CH, :]
    k = k_ref[...].astype(f32).reshape(nc, CH, KW)
    t.ec, t.enc, t.edec, t.ecl = jnp.exp(c), jnp.exp(-c), jnp.exp(cl - c), jnp.exp(cl)
    t.qd = q_ref[...].astype(f32).reshape(nc, CH, KW) * t.ec * QSCALE
    t.kd = k * t.enc
    t.kdec = k * t.edec
    hm = _head_masks()
    t.tri4 = jnp.concatenate([_tri(rev)] * NH, axis=0)[None]
    t.qs = jnp.concatenate([t.qd * hm[h] for h in range(NH)], axis=1)
    t.pst = _bnt(t.qs, t.kd) * t.tri4
    return t


def _gla_fwd(p, w2p, gb, s0, rev, tg, name):
    l = p.shape[0]
    nb, nc = l // tg, tg // CH

    def body(q_ref, k_ref, v_ref, lr_ref, w2_ref, gb_ref, s0_ref, o_ref, st_ref, st):
        @pl.when(pl.program_id(0) == 0)
        def _():
            st[...] = s0_ref[...]

        t = _tile_prep(q_ref, k_ref, lr_ref, w2_ref, gb_ref, rev, nc)
        v = v_ref[...].reshape(nc, CH, VW)
        intra = jnp.concatenate([_bdot(t.pst[:, h * CH:(h + 1) * CH], v[:, :, h * HV:(h + 1) * HV]) for h in range(NH)], axis=2)
        bd = _block_diag()
        s = st[...]
        for n in (range(nc - 1, -1, -1) if rev else range(nc)):
            st_ref[n] = s.astype(bf16)
            s = s * t.ecl[n] + bd * _tn(v[n], t.kdec[n])
        st[...] = s
        o_ref[...] = (_bnt(t.qd, st_ref[...]) + intra).reshape(tg, VW).astype(bf16)

    blk = (lambda i: nb - 1 - i) if rev else (lambda i: i)
    return pl.pallas_call(
        body, name=name, grid=(nb,),
        out_shape=(jax.ShapeDtypeStruct((l, VW), bf16), jax.ShapeDtypeStruct((l // CH, VW, KW), bf16)),
        in_specs=[pl.BlockSpec((tg, KW), lambda i: (blk(i), PQ // KW)), pl.BlockSpec((tg, KW), lambda i: (blk(i), PK // KW)),
                  pl.BlockSpec((tg, VW), lambda i: (blk(i), PV // VW)), pl.BlockSpec((tg, LRW), lambda i: (blk(i), PLR // LRW)),
                  pl.BlockSpec((LRW, KW), lambda i: (0, 0)), pl.BlockSpec((1, KW), lambda i: (0, 0)),
                  pl.BlockSpec((VW, KW), lambda i: (0, 0))],
        out_specs=(pl.BlockSpec((tg, VW), lambda i: (blk(i), 0)), pl.BlockSpec((nc, VW, KW), lambda i: (blk(i), 0, 0))),
        scratch_shapes=[pltpu.VMEM((VW, KW), f32)],
        compiler_params=_cp(("arbitrary",)),
    )(p, p, p, p, w2p, gb, s0)


def _gla_bwd(p, do, states, w2p, gb, prev, rev, tg, name, comm):
    l = p.shape[0]
    nb, nc = l // tg, tg // CH
    out_dt = bf16
    c_in, c_out, c_sems = comm.specs()

    def body(*refs):
        q_ref, k_ref, v_ref, lr_ref, do_ref, st_ref, w2_ref, gb_ref = refs[:8]
        refs = refs[8:]
        if prev is not None:
            pq_ref, pl_ref = refs[:2]
            refs = refs[2:]
        cin, refs = refs[:len(c_in)], refs[len(c_in):]
        dqkv_ref, dlr_ref, dw2_ref, dgb_ref, ds0_ref = refs[:5]
        cout, dst, ds_buf, sems = refs[5:5 + comm.n], refs[5 + comm.n], refs[6 + comm.n], refs[7 + comm.n:]
        comm.run(cin, cout, sems, nb, lambda: compute(q_ref, k_ref, v_ref, lr_ref, do_ref, st_ref, w2_ref, gb_ref,
                                                      pq_ref if prev is not None else None, pl_ref if prev is not None else None,
                                                      dqkv_ref, dlr_ref, dw2_ref, dgb_ref, ds0_ref, dst, ds_buf))

    def compute(q_ref, k_ref, v_ref, lr_ref, do_ref, st_ref, w2_ref, gb_ref, pq_ref, pl_ref,
                dqkv_ref, dlr_ref, dw2_ref, dgb_ref, ds0_ref, dst, ds_buf):
        @pl.when(pl.program_id(0) == 0)
        def _():
            dst[...] = jnp.zeros_like(dst)
            dw2_ref[...] = jnp.zeros_like(dw2_ref)
            dgb_ref[...] = jnp.zeros_like(dgb_ref)

        t = _tile_prep(q_ref, k_ref, lr_ref, w2_ref, gb_ref, rev, nc)
        hm = _head_masks()
        v = v_ref[...].reshape(nc, CH, VW)
        do = do_ref[...].reshape(nc, CH, VW)
        heads = lambda a, h: a[:, :, h * HV:(h + 1) * HV]
        dpst = jnp.concatenate([_bnt(heads(do, h), heads(v, h)) for h in range(NH)], axis=1) * t.tri4
        dv = jnp.concatenate([_btn(t.pst[:, h * CH:(h + 1) * CH], heads(do, h)) for h in range(NH)], axis=2)
        dqd = _bdot(do, st_ref[...])
        for h in range(NH):
            dqd = dqd + hm[h] * _bdot(dpst[:, h * CH:(h + 1) * CH], t.kd)
        dkd = _btn(dpst, t.qs)
        bd = _block_diag()
        d = dst[...]
        for n in (range(nc) if rev else range(nc - 1, -1, -1)):
            ds_buf[n] = d
            d = d * t.ecl[n] + bd * _tn(do[n], t.qd[n])
        dst[...] = d
        ds0_ref[...] = d
        ds = ds_buf[...]
        dv = dv + _bnt(t.kdec, ds)
        dkdec = _bdot(v, ds)
        dcl = jnp.sum(ds * st_ref[...].astype(f32), axis=1, keepdims=True) * t.ecl + jnp.sum(dkdec * t.kdec, axis=1, keepdims=True)
        dc = dqd * t.qd - dkd * t.kd - dkdec * t.kdec
        da = _scan_chunks(dc, not rev) + dcl
        dq = dqd * t.ec * QSCALE
        dk = dkd * t.enc + dkdec * t.edec
        dlog = da.reshape(tg, KW) * _sigmoid(-t.logits) * (1.0 / TAU)
        dlr = _nt(dlog, w2_ref[...])
        dw2_ref[...] += _tn(lr_ref[...], dlog)
        dgb_ref[...] += jnp.sum(dlog, axis=0, keepdims=True)
        dqkv = jnp.concatenate([dq, dk, dv], axis=2).reshape(tg, 2 * KW + VW)
        if prev is not None:
            dqkv = dqkv + pq_ref[...]
            dlr = dlr + pl_ref[...]
        dqkv_ref[...] = dqkv.astype(out_dt)
        dlr_ref[...] = dlr.astype(out_dt)

    blk = (lambda i: i) if rev else (lambda i: nb - 1 - i)
    in_specs = [pl.BlockSpec((tg, KW), lambda i: (blk(i), PQ // KW)), pl.BlockSpec((tg, KW), lambda i: (blk(i), PK // KW)),
                pl.BlockSpec((tg, VW), lambda i: (blk(i), PV // VW)), pl.BlockSpec((tg, LRW), lambda i: (blk(i), PLR // LRW)),
                pl.BlockSpec((tg, VW), lambda i: (blk(i), 0)), pl.BlockSpec((nc, VW, KW), lambda i: (blk(i), 0, 0)),
                pl.BlockSpec((LRW, KW), lambda i: (0, 0)), pl.BlockSpec((1, KW), lambda i: (0, 0))]
    args = [p, p, p, p, do, states, w2p, gb]
    if prev is not None:
        in_specs += [pl.BlockSpec((tg, 2 * KW + VW), lambda i: (blk(i), 0)), pl.BlockSpec((tg, LRW), lambda i: (blk(i), 0))]
        args += list(prev)
    return pl.pallas_call(
        body, name=name, grid=(nb,),
        out_shape=(jax.ShapeDtypeStruct((l, 2 * KW + VW), out_dt), jax.ShapeDtypeStruct((l, LRW), out_dt),
                   jax.ShapeDtypeStruct((LRW, KW), f32), jax.ShapeDtypeStruct((1, KW), f32), jax.ShapeDtypeStruct((VW, KW), f32))
        + tuple(comm.outs),
        in_specs=in_specs + c_in,
        out_specs=(pl.BlockSpec((tg, 2 * KW + VW), lambda i: (blk(i), 0)), pl.BlockSpec((tg, LRW), lambda i: (blk(i), 0)),
                   pl.BlockSpec((LRW, KW), lambda i: (0, 0)), pl.BlockSpec((1, KW), lambda i: (0, 0)),
                   pl.BlockSpec((VW, KW), lambda i: (0, 0))) + tuple(c_out),
        scratch_shapes=[pltpu.VMEM((VW, KW), f32), pltpu.VMEM((nc, VW, KW), f32)] + c_sems,
        compiler_params=_cp(("arbitrary",)),
    )(*args, *comm.ins)


def _ctx_hidden(ctx_ref, g_ref, sc_ref, sh_ref):
    xv = ctx_ref[...]
    r = lax.rsqrt(jnp.mean(xv * xv, axis=-1, keepdims=True) + EPS)
    xn = xv * r
    return xn, xn * (g_ref[...] * (1.0 + sc_ref[...])) + sh_ref[...]


_CTX_W_SPECS = [pl.BlockSpec((D, KW), lambda i: (0, PK // KW)), pl.BlockSpec((D, VW), lambda i: (0, PV // VW)),
                pl.BlockSpec((D, LRW), lambda i: (0, PLR // LRW))]


def _ctx_fwd(ctx, g, scale, shift, w_pad, w2f, w2b, gbf, gbb):
    ncc = CTX // CH

    def body(ctx_ref, g_ref, sc_ref, sh_ref, wk_ref, wv_ref, wl_ref, w2f_ref, w2b_ref, gbf_ref, gbb_ref, sf_ref, sb_ref):
        _, hc = _ctx_hidden(ctx_ref, g_ref, sc_ref, sh_ref)
        k, v, lr = _dot(hc, wk_ref[...]), _dot(hc, wv_ref[...]), _dot(hc, wl_ref[...])
        bd = _block_diag()
        for rev, w2_ref, gb_ref, out in ((False, w2f_ref, gbf_ref, sf_ref), (True, w2b_ref, gbb_ref, sb_ref)):
            tri = _tri(rev)
            st = jnp.zeros((VW, KW), f32)
            for j in (range(ncc - 1, -1, -1) if rev else range(ncc)):
                rows = slice(j * CH, (j + 1) * CH)
                _, c, cl = _decay(lr[rows], w2_ref[...], gb_ref[...], tri, rev)
                st = _state_fwd(k[rows], v[rows], c, cl, st, bd)
            out[...] = st

    vec = pl.BlockSpec((1, D), lambda i: (0, 0))
    w2s = pl.BlockSpec((LRW, KW), lambda i: (0, 0))
    gbs = pl.BlockSpec((1, KW), lambda i: (0, 0))
    sts = pl.BlockSpec((VW, KW), lambda i: (0, 0))
    return pl.pallas_call(
        body, name="ctx_fwd", grid=(1,), out_shape=(jax.ShapeDtypeStruct((VW, KW), f32),) * 2,
        in_specs=[pl.BlockSpec((CTX, D), lambda i: (0, 0)), vec, vec, vec] + _CTX_W_SPECS + [w2s, w2s, gbs, gbs],
        out_specs=(sts, sts), compiler_params=_cp(("arbitrary",)),
    )(ctx, g, scale, shift, w_pad, w_pad, w_pad, w2f, w2b, gbf, gbb)


def _ctx_bwd(ctx, g, scale, shift, w_pad, w2f, w2b, gbf, gbb, dsf, dsb):
    ncc = CTX // CH

    def body(ctx_ref, g_ref, sc_ref, sh_ref, wk_ref, wv_ref, wl_ref, w2f_ref, w2b_ref, gbf_ref, gbb_ref, dsf_ref, dsb_ref,
             dwk_ref, dwv_ref, dwl_ref, dmod_ref, dg_ref, dw2_ref, dgb_ref):
        xn, hc = _ctx_hidden(ctx_ref, g_ref, sc_ref, sh_ref)
        k, v, lr = _dot(hc, wk_ref[...]), _dot(hc, wv_ref[...]), _dot(hc, wl_ref[...])
        bd = _block_diag()
        dk_rows, dv_rows, dl_rows = [None] * ncc, [None] * ncc, [None] * ncc
        for d, (rev, w2_ref, gb_ref, ds_ref) in enumerate(((False, w2f_ref, gbf_ref, dsf_ref), (True, w2b_ref, gbb_ref, dsb_ref))):
            tri = _tri(rev)
            order = list(range(ncc - 1, -1, -1) if rev else range(ncc))
            st, saved = jnp.zeros((VW, KW), f32), {}
            for j in order:
                rows = slice(j * CH, (j + 1) * CH)
                logits, c, cl = _decay(lr[rows], w2_ref[...], gb_ref[...], tri, rev)
                saved[j] = (logits, c, cl, st)
                st = _state_fwd(k[rows], v[rows], c, cl, st, bd)
            dst = ds_ref[...]
            dw2 = jnp.zeros((LRW, KW), f32)
            dgb = jnp.zeros((1, KW), f32)
            for j in reversed(order):
                rows = slice(j * CH, (j + 1) * CH)
                logits, c, cl, st0 = saved[j]
                dk, dv, da, dst = _state_bwd(k[rows], v[rows], c, cl, st0, dst, _tri(not rev))
                dlog = da * _sigmoid(-logits) * (1.0 / TAU)
                dl = _nt(dlog, w2_ref[...])
                dw2 = dw2 + _tn(lr[rows], dlog)
                dgb = dgb + jnp.sum(dlog, axis=0, keepdims=True)
                dk_rows[j] = dk if dk_rows[j] is None else dk_rows[j] + dk
                dv_rows[j] = dv if dv_rows[j] is None else dv_rows[j] + dv
                dl_rows[j] = dl if dl_rows[j] is None else dl_rows[j] + dl
            dw2_ref[d] = dw2
            dgb_ref[d] = dgb
        dk, dv, dl = (jnp.concatenate(t, axis=0) for t in (dk_rows, dv_rows, dl_rows))
        dwk_ref[...] = _tn(hc, dk)
        dwv_ref[...] = _tn(hc, dv)
        dwl_ref[...] = _tn(hc, dl)
        dh = _nt(dk, wk_ref[...]) + _nt(dv, wv_ref[...]) + _nt(dl, wl_ref[...])
        gx = dh * xn
        dmod_ref[:, 0:D] = jnp.sum(dh, axis=0, keepdims=True)
        dmod_ref[:, D:2 * D] = jnp.sum(gx, axis=0, keepdims=True) * g_ref[...]
        dg_ref[...] = jnp.sum(gx, axis=0, keepdims=True) * (1.0 + sc_ref[...])

    vec = pl.BlockSpec((1, D), lambda i: (0, 0))
    w2s = pl.BlockSpec((LRW, KW), lambda i: (0, 0))
    gbs = pl.BlockSpec((1, KW), lambda i: (0, 0))
    sts = pl.BlockSpec((VW, KW), lambda i: (0, 0))
    full = lambda *s: pl.BlockSpec(s, lambda i: (0,) * len(s))
    return pl.pallas_call(
        body, name="ctx_bwd", grid=(1,),
        out_shape=(jax.ShapeDtypeStruct((D, KW), f32), jax.ShapeDtypeStruct((D, VW), f32), jax.ShapeDtypeStruct((D, LRW), f32),
                   jax.ShapeDtypeStruct((1, 2 * D), f32), jax.ShapeDtypeStruct((1, D), f32),
                   jax.ShapeDtypeStruct((2, LRW, KW), f32), jax.ShapeDtypeStruct((2, 1, KW), f32)),
        in_specs=[pl.BlockSpec((CTX, D), lambda i: (0, 0)), vec, vec, vec] + _CTX_W_SPECS + [w2s, w2s, gbs, gbs, sts, sts],
        out_specs=(full(D, KW), full(D, VW), full(D, LRW), full(1, 2 * D), full(1, D), full(2, LRW, KW), full(2, 1, KW)),
        compiler_params=_cp(("arbitrary",)),
    )(ctx, g, scale, shift, w_pad, w_pad, w_pad, w2f, w2b, gbf, gbb, dsf, dsb)


def _layernorm(va, g, b):
    mu = jnp.mean(va, axis=-1, keepdims=True)
    xc = va - mu
    rstd = lax.rsqrt(jnp.mean(xc * xc, axis=-1, keepdims=True) + EPS)
    vhat = xc * rstd
    return vhat, rstd, vhat * g + b


MIX_PIECES = 8


def _mix_fwd(p, ln_g, ln_b, ws, bs_t):
    l = p.shape[0]
    half = AW // 2
    rp = l // MIX_PIECES
    cpp = rp // ACH

    def body(p_ref, g_ref, b_ref, ws_ref, bs_ref, sv_hbm, va_buf, col_buf, sv_buf, in_sems, out_sems):
        piece = lambda i: pl.ds(pl.multiple_of(i * rp, rp), rp)
        load = lambda i: pltpu.make_async_copy(p_ref.at[piece(i), pl.ds(PVA, AW)], va_buf.at[piece(i)], in_sems.at[i])
        store = lambda i: pltpu.make_async_copy(sv_buf.at[piece(i)], sv_hbm.at[piece(i)], out_sems.at[i])
        for i in range(MIX_PIECES):
            load(i).start()

        def rows_piece(i, carry):
            load(i).wait()
            for j in range(cpp):
                rows = pl.ds(pl.multiple_of(i * rp + j * ACH, ACH), ACH)
                _, _, vn = _layernorm(va_buf[rows, :].astype(f32), g_ref[...], b_ref[...])
                for gi in range(2):
                    sl = slice(gi * ACH, (gi + 1) * ACH)
                    sv_buf[rows, sl] = (_dot(ws_ref[gi], vn[:, sl]) + bs_ref[:, gi:gi + 1]).astype(bf16)
                col_buf[0, rows, :] = vn[:, half:half + ACH]
                col_buf[1, rows, :] = vn[:, half + ACH:]
            return carry

        lax.fori_loop(0, MIX_PIECES, rows_piece, 0)

        def cols_step(cidx, carry):
            rows = pl.ds(cidx, ACH, stride=GW)
            for gi in range(2, 4):
                col_buf[gi - 2, rows, :] = _dot(ws_ref[gi], col_buf[gi - 2, rows, :]) + bs_ref[:, gi:gi + 1]
            return carry

        lax.fori_loop(0, GW, cols_step, 0, unroll=8)

        def out_piece(i, carry):
            for j in range(cpp):
                rows = pl.ds(pl.multiple_of(i * rp + j * ACH, ACH), ACH)
                sv_buf[rows, half:half + ACH] = col_buf[0, rows, :].astype(bf16)
                sv_buf[rows, half + ACH:] = col_buf[1, rows, :].astype(bf16)
            store(i).start()
            return carry

        lax.fori_loop(0, MIX_PIECES, out_piece, 0)
        for i in range(MIX_PIECES):
            store(i).wait()

    vm = pl.BlockSpec(memory_space=pltpu.VMEM)
    hbm = pl.BlockSpec(memory_space=pl.ANY)
    return pl.pallas_call(
        body, name="mix_fwd", out_shape=jax.ShapeDtypeStruct((l, AW), bf16),
        in_specs=[hbm, vm, vm, vm, vm], out_specs=hbm,
        scratch_shapes=[pltpu.VMEM((l, AW), bf16), pltpu.VMEM((2, l, ACH), f32), pltpu.VMEM((l, AW), bf16),
                        pltpu.SemaphoreType.DMA((MIX_PIECES,)), pltpu.SemaphoreType.DMA((MIX_PIECES,))],
        compiler_params=_cp(),
    )(p, ln_g, ln_b, ws, bs_t)


def _mix_bwd(p, dsv, ln_g, ln_b, ws_t):
    l = p.shape[0]
    half = AW // 2
    rp = l // MIX_PIECES
    cpp = rp // ACH

    def body(p_ref, dsv_hbm, g_ref, b_ref, wst_ref, dva_hbm, dws_ref, dbs_ref, dg_ref, db_ref,
             va_buf, dsv_buf, vn_col, ds_col, dva_buf, va_sems, ds_sems, out_sems):
        piece = lambda i: pl.ds(pl.multiple_of(i * rp, rp), rp)
        load_va = lambda i: pltpu.make_async_copy(p_ref.at[piece(i), pl.ds(PVA, AW)], va_buf.at[piece(i)], va_sems.at[i])
        load_ds = lambda i: pltpu.make_async_copy(dsv_hbm.at[piece(i)], dsv_buf.at[piece(i)], ds_sems.at[i])
        store = lambda i: pltpu.make_async_copy(dva_buf.at[piece(i)], dva_hbm.at[piece(i)], out_sems.at[i])
        for i in range(MIX_PIECES):
            load_va(i).start()
            load_ds(i).start()
        dws_ref[...] = jnp.zeros_like(dws_ref)
        dbs_ref[...] = jnp.zeros_like(dbs_ref)
        dg_ref[...] = jnp.zeros_like(dg_ref)
        db_ref[...] = jnp.zeros_like(db_ref)

        def rows_piece(i, carry):
            load_va(i).wait()
            load_ds(i).wait()
            for j in range(cpp):
                rows = pl.ds(pl.multiple_of(i * rp + j * ACH, ACH), ACH)
                _, _, vn = _layernorm(va_buf[rows, :].astype(f32), g_ref[...], b_ref[...])
                ds = dsv_buf[rows, :].astype(f32)
                for gi in range(2):
                    sl = slice(gi * ACH, (gi + 1) * ACH)
                    dws_ref[gi] += _nt(ds[:, sl], vn[:, sl])
                    dbs_ref[gi] += ds[:, sl]
                for gi in range(2):
                    sl = slice(half + gi * ACH, half + (gi + 1) * ACH)
                    vn_col[gi, rows, :] = vn[:, sl]
                    ds_col[gi, rows, :] = ds[:, sl]
            return carry

        lax.fori_loop(0, MIX_PIECES, rows_piece, 0)

        def cols_step(cidx, carry):
            rows = pl.ds(cidx, ACH, stride=GW)
            for gi in range(2, 4):
                ds = ds_col[gi - 2, rows, :]
                dws_ref[gi] += _nt(ds, vn_col[gi - 2, rows, :])
                dbs_ref[gi] += ds
                ds_col[gi - 2, rows, :] = _dot(wst_ref[gi], ds)
            return carry

        lax.fori_loop(0, GW, cols_step, 0, unroll=8)

        def out_piece(i, carry):
            for j in range(cpp):
                rows = pl.ds(pl.multiple_of(i * rp + j * ACH, ACH), ACH)
                vhat, rstd, _ = _layernorm(va_buf[rows, :].astype(f32), g_ref[...], b_ref[...])
                ds = dsv_buf[rows, :].astype(f32)
                dvn = jnp.concatenate([_dot(wst_ref[0], ds[:, 0:ACH]), _dot(wst_ref[1], ds[:, ACH:half]),
                                       ds_col[0, rows, :], ds_col[1, rows, :]], axis=1)
                dg_ref[...] += jnp.sum(dvn * vhat, axis=0, keepdims=True)
                db_ref[...] += jnp.sum(dvn, axis=0, keepdims=True)
                dvh = dvn * g_ref[...]
                dva = rstd * (dvh - jnp.mean(dvh, axis=-1, keepdims=True) - vhat * jnp.mean(dvh * vhat, axis=-1, keepdims=True))
                dva_buf[rows, :] = dva.astype(bf16)
            store(i).start()
            return carry

        lax.fori_loop(0, MIX_PIECES, out_piece, 0)
        for i in range(MIX_PIECES):
            store(i).wait()

    vm = pl.BlockSpec(memory_space=pltpu.VMEM)
    hbm = pl.BlockSpec(memory_space=pl.ANY)
    dma = lambda: pltpu.SemaphoreType.DMA((MIX_PIECES,))
    return pl.pallas_call(
        body, name="mix_bwd",
        out_shape=(jax.ShapeDtypeStruct((l, AW), bf16), jax.ShapeDtypeStruct((4, ACH, ACH), f32), jax.ShapeDtypeStruct((4, ACH, ACH), f32),
                   jax.ShapeDtypeStruct((1, AW), f32), jax.ShapeDtypeStruct((1, AW), f32)),
        in_specs=[hbm, hbm, vm, vm, vm], out_specs=(hbm, vm, vm, vm, vm),
        scratch_shapes=[pltpu.VMEM((l, AW), bf16), pltpu.VMEM((l, AW), bf16), pltpu.VMEM((2, l, ACH), f32), pltpu.VMEM((2, l, ACH), f32),
                        pltpu.VMEM((l, AW), bf16), dma(), dma(), dma()],
        compiler_params=_cp(),
    )(p, dsv, ln_g, ln_b, ws_t)


def _mid(x, tgt, p, o_f, o_b, sv, gate, gf, gb_norm, w_pa, w_pb, w_out, tl):
    l = x.shape[0]

    def body(x_ref, t_ref, zb_ref, ua_ref, za_ref, g1_ref, g2_ref, of_ref, ob_ref, sv_ref, gate_ref, gf_ref, gbn_ref,
             wpa_ref, wpb_ref, wout_ref,
             dx1_ref, dzbua_ref, dzag_ref, dsv_ref, do_ref, dwout_bf, dwpa_bf, dwpb_bf, dgf_ref, dgate_ref, dgbn_ref, loss_ref,
             dwout_ref, dwpa_ref, dwpb_ref):
        @pl.when(pl.program_id(0) == 0)
        def _():
            for r in (dwout_ref, dwpa_ref, dwpb_ref, dgf_ref, dgate_ref, dgbn_ref, loss_ref):
                r[...] = jnp.zeros_like(r)

        o = of_ref[...].astype(f32) + ob_ref[...].astype(f32)
        rr = jnp.concatenate(
            [jnp.broadcast_to(lax.rsqrt(jnp.mean(o[:, h * HV:(h + 1) * HV] ** 2, axis=-1, keepdims=True) + EPS), (tl, HV))
             for h in range(NH)], axis=1)
        ohat = o * rr
        on = ohat * gbn_ref[...]
        szb, dszb = _silu_and_grad(zb_ref[...].astype(f32))
        tb = on * szb
        u = ua_ref[...].astype(f32)
        svv = sv_ref[...].astype(f32)
        sza, dsza = _silu_and_grad(za_ref[...].astype(f32))
        ta = u * svv * sza
        ya = _dot(ta, wpa_ref[...])
        yb = _dot(tb, wpb_ref[...])
        g1 = _sigmoid(g1_ref[...].astype(f32))
        g2 = _sigmoid(g2_ref[...].astype(f32))
        m = g1 * ya + g2 * yb
        y2 = _dot(m, wout_ref[...])
        x1 = x_ref[...] + gate_ref[...] * y2
        r1 = lax.rsqrt(jnp.mean(x1 * x1, axis=-1, keepdims=True) + EPS)
        x1n = x1 * r1
        err = x1n * gf_ref[...] - t_ref[...]
        loss_ref[...] += jnp.sum(jnp.sum(err * err, axis=-1, keepdims=True), axis=0, keepdims=True) * (0.5 / D)
        dout = err * (1.0 / D)
        dgf_ref[...] += jnp.sum(dout * x1n, axis=0, keepdims=True)
        dx1n = dout * gf_ref[...]
        dx1 = r1 * (dx1n - x1n * jnp.mean(dx1n * x1n, axis=-1, keepdims=True))
        dx1_ref[...] = dx1
        dgate_ref[...] += jnp.sum(dx1 * y2, axis=0, keepdims=True)
        dy2 = dx1 * gate_ref[...]
        dwout_ref[...] += _tn(m, dy2)
        dm = _nt(dy2, wout_ref[...])
        dya = dm * g1
        dyb = dm * g2
        dzag_ref[:, AW:AW + D] = (dm * ya * g1 * (1.0 - g1)).astype(bf16)
        dzag_ref[:, AW + D:] = (dm * yb * g2 * (1.0 - g2)).astype(bf16)
        dwpa_ref[...] += _tn(ta, dya)
        dta = _nt(dya, wpa_ref[...])
        dzbua_ref[:, AW:] = (dta * svv * sza).astype(bf16)
        dsv_ref[...] = (dta * u * sza).astype(bf16)
        dzag_ref[:, 0:AW] = (dta * u * svv * dsza).astype(bf16)
        dwpb_ref[...] += _tn(tb, dyb)
        dtb = _nt(dyb, wpb_ref[...])
        don = dtb * szb
        dzbua_ref[:, 0:AW] = (dtb * on * dszb).astype(bf16)
        dgbn_ref[...] += jnp.sum(don * ohat, axis=0, keepdims=True)
        doh = don * gbn_ref[...]
        prod = doh * ohat
        mh = jnp.concatenate(
            [jnp.broadcast_to(jnp.mean(prod[:, h * HV:(h + 1) * HV], axis=-1, keepdims=True), (tl, HV)) for h in range(NH)], axis=1)
        do_ref[...] = (rr * (doh - ohat * mh)).astype(bf16)

        @pl.when(pl.program_id(0) == l // tl - 1)
        def _():
            for acc, out in ((dwout_ref, dwout_bf), (dwpa_ref, dwpa_bf), (dwpb_ref, dwpb_bf)):
                out[...] = acc[...].astype(bf16)

    row = lambda w, j: pl.BlockSpec((tl, w), lambda i, j=j: (i, j))
    full = lambda *s: pl.BlockSpec(s, lambda i: (0,) * len(s))
    return pl.pallas_call(
        body, name="mid", grid=(l // tl,),
        out_shape=(jax.ShapeDtypeStruct((l, D), f32), jax.ShapeDtypeStruct((l, 2 * AW), bf16), jax.ShapeDtypeStruct((l, AW + 2 * D), bf16),
                   jax.ShapeDtypeStruct((l, AW), bf16), jax.ShapeDtypeStruct((l, VW), bf16),
                   jax.ShapeDtypeStruct((D, D), bf16), jax.ShapeDtypeStruct((AW, D), bf16), jax.ShapeDtypeStruct((VW, D), bf16),
                   jax.ShapeDtypeStruct((1, D), f32), jax.ShapeDtypeStruct((1, D), f32), jax.ShapeDtypeStruct((1, VW), f32),
                   jax.ShapeDtypeStruct((1, 1), f32)),
        scratch_shapes=[pltpu.VMEM((D, D), f32), pltpu.VMEM((AW, D), f32), pltpu.VMEM((VW, D), f32)],
        in_specs=[row(D, 0), row(D, 0), row(AW, PZB // AW), row(AW, PUA // AW), row(AW, PZA // AW), row(D, PG1 // D), row(D, PG2 // D),
                  row(VW, 0), row(VW, 0), row(AW, 0), full(1, D), full(1, D), full(1, VW), full(AW, D), full(VW, D), full(D, D)],
        out_specs=(row(D, 0), row(2 * AW, 0), row(AW + 2 * D, 0), row(AW, 0), row(VW, 0),
                   full(D, D), full(AW, D), full(VW, D), full(1, D), full(1, D), full(1, VW), full(1, 1)),
        compiler_params=_cp(("arbitrary",)),
    )(x, tgt, p, p, p, p, p, o_f, o_b, sv, gate, gf, gb_norm, w_pa, w_pb, w_out)


def _in_bwd(x, dx1, dqkv, dzbua, dva, dzag, dlr, w_pad, g, scale, tl, comm):
    l = x.shape[0]
    c_in, c_out, c_sems = comm.specs()

    def body(*refs):
        cin = refs[14:14 + len(c_in)]
        outs = refs[14 + len(c_in):]
        comm.run(cin, outs[4:4 + comm.n], outs[4 + comm.n:], l // tl, lambda: compute(*refs[:14], *outs[:4]))

    def compute(x_ref, dx1_ref, a_ref, b_ref, c_ref, e_ref, lr_ref, wa_ref, wb_ref, wc_ref, we_ref, wl_ref, g_ref, sc_ref,
                gx_ref, dsh_ref, dsc_ref, dg_ref):
        @pl.when(pl.program_id(0) == 0)
        def _():
            for r in (dsh_ref, dsc_ref, dg_ref):
                r[...] = jnp.zeros_like(r)

        dh = (_nt(a_ref[...], wa_ref[...]) + _nt(b_ref[...], wb_ref[...]) + _nt(c_ref[...], wc_ref[...]) + _nt(e_ref[...], we_ref[...])
              + _nt(lr_ref[...], wl_ref[...]))
        xv = x_ref[...]
        r = lax.rsqrt(jnp.mean(xv * xv, axis=-1, keepdims=True) + EPS)
        xn = xv * r
        gxn = jnp.sum(dh * xn, axis=0, keepdims=True)
        dsh_ref[...] += jnp.sum(dh, axis=0, keepdims=True)
        dsc_ref[...] += gxn * g_ref[...]
        dg_ref[...] += gxn * (1.0 + sc_ref[...])
        dxn = dh * (g_ref[...] * (1.0 + sc_ref[...]))
        gx_ref[...] = dx1_ref[...] + r * (dxn - xn * jnp.mean(dxn * xn, axis=-1, keepdims=True))

    row = lambda w: pl.BlockSpec((tl, w), lambda i: (i, 0))
    wcol = lambda w, j: pl.BlockSpec((D, w), lambda i, j=j: (0, j))
    vec = pl.BlockSpec((1, D), lambda i: (0, 0))
    return pl.pallas_call(
        body, name="in_bwd", grid=(l // tl,),
        out_shape=(jax.ShapeDtypeStruct((l, D), f32),) + (jax.ShapeDtypeStruct((1, D), f32),) * 3 + tuple(comm.outs),
        in_specs=[row(D), row(D), row(2 * KW + VW), row(2 * AW), row(AW), row(AW + 2 * D), row(LRW),
                  wcol(2 * KW + VW, 0), wcol(2 * AW, PZB // (2 * AW)), wcol(AW, PVA // AW), wcol(AW + 2 * D, PZA // (AW + 2 * D)),
                  wcol(LRW, PLR // LRW), vec, vec] + c_in,
        out_specs=(row(D), vec, vec, vec) + tuple(c_out), scratch_shapes=c_sems,
        compiler_params=_cp(("arbitrary",)),
    )(x, dx1, dqkv, dzbua, dva, dzag, dlr, w_pad, w_pad, w_pad, w_pad, w_pad, g, scale, *comm.ins)


def _tn_matmul(a, bs, tl, name, comm=None):
    l, m = a.shape
    k = len(bs)
    comm = comm or _Comm([], [], None)
    c_in, c_out, c_sems = comm.specs()

    def body(a_ref, *refs):
        b_refs, cin = refs[:k], refs[k:k + len(c_in)]
        o_refs = refs[k + len(c_in):2 * k + len(c_in)]
        cout, sems = refs[2 * k + len(c_in):2 * k + len(c_in) + comm.n], refs[2 * k + len(c_in) + comm.n:]

        def compute():
            @pl.when(pl.program_id(0) == 0)
            def _():
                for o_ref in o_refs:
                    o_ref[...] = jnp.zeros_like(o_ref)

            av = a_ref[...]
            for b_ref, o_ref in zip(b_refs, o_refs):
                o_ref[...] += _tn(av, b_ref[...])

        comm.run(cin, cout, sems, l // tl, compute)

    return pl.pallas_call(
        body, name=name, grid=(l // tl,), out_shape=tuple(jax.ShapeDtypeStruct((m, b.shape[1]), f32) for b in bs) + tuple(comm.outs),
        in_specs=[pl.BlockSpec((tl, m), lambda i: (i, 0))] + [pl.BlockSpec((tl, b.shape[1]), lambda i: (i, 0)) for b in bs] + c_in,
        out_specs=tuple(pl.BlockSpec((m, b.shape[1]), lambda i: (0, 0)) for b in bs) + tuple(c_out),
        scratch_shapes=c_sems, compiler_params=_cp(("arbitrary",)),
    )(a, *bs, *comm.ins)


def _pad_gate(w2, gb):
    z = jnp.zeros((RANK, KW), f32)
    tail = jnp.zeros((LRW - 2 * RANK, KW), f32)
    w2f = jnp.concatenate([w2[0], z, tail], axis=0)
    w2b = jnp.concatenate([z, w2[1], tail], axis=0)
    return w2f, w2b, gb[0:1], gb[1:2]


EARLY_A_ROWS = 384


class _NoExchange:
    def __init__(self, w_pa, w_pb, w_out):
        self.weights = (w_pa, w_pb, w_out)

    def gather_proj(self):
        return _Comm([], [], None)

    def proj_weights(self, got):
        return self.weights

    def first(self, dw_out, dw_pa, dw_pb):
        return _Comm([], [], None)

    def early_a(self, small, blocks):
        return _Comm([], [], None)

    def early_b(self, blocks):
        return _Comm([], [], None)

    def late(self, blocks, dgt):
        return _Comm([], [], None)


class _Exchanges:
    def __init__(self, pa, pb, wo):
        self.shards = (pa, pb, wo)

    def gather_proj(self):
        def plan(i, o):
            srcs = [lambda j, r=r: r for r in i]
            dsts = [lambda j: o[0].at[:, _lanes(j)], lambda j: o[1].at[:, _lanes(j)], lambda j: o[2].at[j]]
            return srcs, dsts, None
        sds = jax.ShapeDtypeStruct
        return _Comm(self.shards, [sds((AW, D), bf16), sds((VW, D), bf16), sds((NDEV, 128, D), bf16)], plan)

    def proj_weights(self, got):
        return got[0], got[1], got[2].reshape(D, D)

    def first(self, dw_out, dw_pa, dw_pb):
        def plan(i, o):
            srcs = [lambda j: i[0].at[j], lambda j: i[1].at[:, _lanes(j)], lambda j: i[2].at[:, _lanes(j)]]
            dsts = [lambda j, r=r: r.at[j] for r in o]
            return srcs, dsts, None
        sds = jax.ShapeDtypeStruct
        return _Comm([dw_out.reshape(NDEV, 128, D), dw_pa, dw_pb],
                     [sds((NDEV, 128, D), bf16), sds((NDEV, AW, 128), bf16), sds((NDEV, VW, 128), bf16)], plan)

    def early_a(self, small, blocks):
        def plan(i, o):
            srcs = [lambda j: i[0], lambda j: i[1].at[j]]
            dsts = [lambda j, r=r: r.at[j] for r in o]
            return srcs, dsts, [None, lambda j: j >= LATE_DESTS - 1]
        sds = jax.ShapeDtypeStruct
        return _Comm([small, blocks], [sds((NDEV,) + small.shape, f32), sds(blocks.shape, bf16)], plan)

    def early_b(self, blocks):
        def plan(i, o):
            return [lambda j: i[0].at[j]], [lambda j: o[0].at[j]], [lambda j: j >= LATE_DESTS - 1]
        return _Comm([blocks], [jax.ShapeDtypeStruct(blocks.shape, bf16)], plan)

    def late(self, blocks, dgt):
        return _LateComm(blocks, dgt)


def _local_step(x, ctx, tgt, mod, modc, norm_g, w_pad, ln_g, ln_b, ws, bs, w2, gb, gb_norm, gf, xch):
    shift, scale, gate = mod[:, 0:D], mod[:, D:2 * D], mod[:, 2 * D:]
    shift_c, scale_c = modc[:, 0:D], modc[:, D:]
    w2f, w2b, gbf, gbb = _pad_gate(w2, gb)

    p, h, *got_proj = _in_proj(x, norm_g, scale, shift, w_pad, 512, xch.gather_proj())
    w_pa, w_pb, w_out = xch.proj_weights(got_proj)
    sc_f, sc_b = _ctx_fwd(ctx, norm_g, scale_c, shift_c, w_pad, w2f, w2b, gbf, gbb)
    o_f, st_f = _gla_fwd(p, w2f, gbf, sc_f, False, 512, "gla_fwd_f")
    o_b, st_b = _gla_fwd(p, w2b, gbb, sc_b, True, 512, "gla_fwd_b")
    sv = _mix_fwd(p, ln_g, ln_b, ws.astype(bf16), bs.T)
    (dx1, dzbua, dzag, dsv, do, dw_out, dw_pa, dw_pb, dgf, dgate, dgbn, loss) = _mid(
        x, tgt, p, o_f, o_b, sv, gate, gf, gb_norm, w_pa, w_pb, w_out, 256)
    dva, dws, dbs_acc, dln_g, dln_b = _mix_bwd(p, dsv, ln_g, ln_b, jnp.swapaxes(ws, 1, 2).astype(bf16))
    dw_zbua, dw_va, dw_zag, *got_first = _tn_matmul(h, [dzbua, dva, dzag], 1024, "dw_early", xch.first(dw_out, dw_pa, dw_pb))
    blocks_a = _pack_dw_early(dw_zbua, dw_va, dw_zag, 0, EARLY_A_ROWS, "pack_dw_early_a")
    blocks_b = _pack_dw_early(dw_zbua, dw_va, dw_zag, EARLY_A_ROWS, D - EARLY_A_ROWS, "pack_dw_early_b")
    small = _rows128(dln_g, dln_b, dws, jnp.sum(dbs_acc, axis=-1), dgbn, dgf, jnp.broadcast_to(loss, (1, 128)))

    dqkv_f, dlr_f, dw2f, dgbf, dsc_f, *got_a = _gla_bwd(p, do, st_f, w2f, gbf, None, False, 512, "gla_bwd_f",
                                                        xch.early_a(small, blocks_a))
    dqkv, dlr, dw2b, dgbb, dsc_b, *got_b = _gla_bwd(p, do, st_b, w2b, gbb, (dqkv_f, dlr_f), True, 512, "gla_bwd_b",
                                                    xch.early_b(blocks_b))
    dwk_c, dwv_c, dwl_c, dmodc, dg_c, dw2c, dgbc = _ctx_bwd(ctx, norm_g, scale_c, shift_c, w_pad, w2f, w2b, gbf, gbb, dsc_f, dsc_b)
    dw_qkv, dw_lr = _tn_matmul(h, [dqkv, dlr], 1024, "dw_qkv_lr")
    blocks_late = _pack_dw_late(dw_qkv, dw_lr, dwk_c, dwv_c, dwl_c)
    dw2 = jnp.stack([dw2f[0:RANK] + dw2c[0, 0:RANK], dw2b[RANK:2 * RANK] + dw2c[1, RANK:2 * RANK]])
    dgb = jnp.concatenate([dgbf + dgbc[0], dgbb + dgbc[1]], axis=0)
    dgt = jnp.concatenate([jnp.transpose(dw2.reshape(2, RANK, NDEV, 32), (2, 0, 1, 3)).reshape(NDEV, 2 * RANK * 32),
                           jnp.transpose(dgb.reshape(2, NDEV, 32), (1, 0, 2)).reshape(NDEV, 64),
                           jnp.zeros((NDEV, 64), f32)], axis=1).reshape(NDEV, 9, 128)
    gx, dshift, dscale, dg, *got_late = _in_bwd(x, dx1, dqkv, dzbua, dva, dzag, dlr, w_pad, norm_g, scale, 512,
                                                xch.late(blocks_late, dgt))
    return dict(loss=loss, gx=gx, dmod=jnp.concatenate([dshift, dscale, dgate], axis=1), dmodc=dmodc, dnorm_g=dg + dg_c,
                small=small, blocks_a=blocks_a, blocks_b=blocks_b, blocks_late=blocks_late, dw2=dw2, dgb=dgb,
                dw_pa=dw_pa, dw_pb=dw_pb, dw_out=dw_out, got_first=got_first, got_a=got_a, got_b=got_b, got_late=got_late)


def _rows128(*vs):
    out = []
    for t in vs:
        t = t.reshape(-1)
        pad = (-t.shape[0]) % 128
        out.append(jnp.pad(t, (0, pad)) if pad else t)
    return jnp.concatenate(out).reshape(-1, 128)


def kernel(x, c, ctx, c_ctx, w_mod, b_mod, norm_g, w_in, a_ln_g, a_ln_b, a_ws, a_bs, b_gate_w2, b_gate_b, b_norm_g, w_proj_a, w_proj_b, w_out, final_norm_g, loss_target, m_c_ctx, m_w_mod, m_b_mod, m_norm_g, m_w_in, m_a_ln_g, m_a_ln_b, m_a_ws, m_a_bs, m_b_gate_w2, m_b_gate_b, m_b_norm_g, m_w_proj_a, m_w_proj_b, m_w_out, m_final_norm_g, v_c_ctx, v_w_mod, v_b_mod, v_norm_g, v_w_in, v_a_ln_g, v_a_ln_b, v_a_ws, v_a_bs, v_b_gate_w2, v_b_gate_b, v_b_norm_g, v_w_proj_a, v_w_proj_b, v_w_out, v_final_norm_g):
    me = _me()
    ncol = w_mod.shape[2]

    gate_mine = _rows128(jnp.concatenate([b_gate_w2.reshape(-1), b_gate_b.reshape(-1)]))
    bm_mine = lax.dynamic_slice(b_mod, (0, me * ncol), (1, ncol))
    cs, mods, wg, gates = _gather_first(c, c_ctx.reshape(1, D), w_mod[0], bm_mine, w_in[0].astype(bf16), gate_mine)
    w_pad = _repack_w(wg)
    gflat = gates.reshape(NDEV, 9 * 128)
    w2 = jnp.transpose(gflat[:, 0:2 * RANK * 32].reshape(NDEV, 2, RANK, 32), (1, 2, 0, 3)).reshape(2, RANK, KW)
    gb = jnp.transpose(gflat[:, 2 * RANK * 32:2 * RANK * 32 + 64].reshape(NDEV, 2, 32), (1, 0, 2)).reshape(2, KW)

    mods = jnp.transpose(mods, (1, 0, 2)).reshape(16, 3 * D)
    mod = lax.dynamic_slice(mods, (me, 0), (1, 3 * D))
    modc = mods[8:9, 0:2 * D]

    xch = _Exchanges(w_proj_a[0].astype(bf16), w_proj_b[0].astype(bf16), w_out[0].astype(bf16))
    r = _local_step(x[0], ctx[0], loss_target[0], mod, modc, norm_g, w_pad, a_ln_g, a_ln_b, a_ws[0], a_bs[0], w2, gb,
                    b_norm_g, final_norm_g.reshape(1, D), xch)
    p_out, p_pa, p_pb = r["got_first"]
    smalls_e, p_in_a = r["got_a"]
    (p_in_b,) = r["got_b"]
    _, _, p_in_late, p_gt = r["got_late"]

    n_e = (AW + AW + 4 * ACH * ACH + AW + VW + D) // 128
    row = lambda t: t.reshape(1, D)
    rep_e = _adam_params(smalls_e, [a_ln_g, a_ln_b, a_ws, a_bs, b_norm_g, row(final_norm_g)],
                         [m_a_ln_g, m_a_ln_b, m_a_ws, m_a_bs, m_b_norm_g, row(m_final_norm_g)],
                         [v_a_ln_g, v_a_ln_b, v_a_ws, v_a_bs, v_b_norm_g, row(v_final_norm_g)], "adam_rep_early")
    rep_e = [t[0:5] + (t[5].reshape(D),) for t in rep_e]
    losses = smalls_e[:, n_e, 0]
    loss = losses[0]
    for i in range(1, NDEV):
        loss = loss + losses[i]

    dbm = r["dmod"] + jnp.concatenate([r["dmodc"], jnp.zeros((1, D), f32)], axis=1)
    wm, ng, bmod_r, cc = _mod_tail(r["dnorm_g"], dbm, r["dmod"], r["dmodc"], cs, w_mod, m_w_mod, v_w_mod,
                                   norm_g, m_norm_g, v_norm_g, b_mod, m_b_mod, v_b_mod,
                                   c_ctx.reshape(1, D), m_c_ctx.reshape(1, D), v_c_ctx.reshape(1, D))

    a_in = _adam_w_in(p_in_a, p_in_b, p_in_late, w_in, m_w_in, v_w_in)
    blk = _adam_blocks([p_pa, p_pb, p_out], [w_proj_a, w_proj_b, w_out], [m_w_proj_a, m_w_proj_b, m_w_out],
                       [v_w_proj_a, v_w_proj_b, v_w_out], "adam_proj_out")
    a_pa, a_pb, a_out = ([t[i] for t in blk] for i in range(3))
    gate_m = _rows128(jnp.concatenate([m_b_gate_w2.reshape(-1), m_b_gate_b.reshape(-1)]))
    gate_v = _rows128(jnp.concatenate([v_b_gate_w2.reshape(-1), v_b_gate_b.reshape(-1)]))
    a_gt = [t.reshape(-1) for t in _adam(p_gt, gate_mine, gate_m, gate_v, 9, "adam_gate")]
    nw2 = 2 * RANK * 32
    sh = [(a_in[k], a_pa[k], a_pb[k], a_out[k], a_gt[k][0:nw2].reshape(1, 2, RANK, 32),
           a_gt[k][nw2:nw2 + 64].reshape(1, 2, 32)) for k in range(4)]

    outs = [loss, r["gx"][None]]
    for k in range(4):
        lg, lb, aws, abs_, bng, fng = rep_e[k]
        n_g, bmod = ng[k], bmod_r[k]
        s_in, s_pa, s_pb, s_out, s_w2, s_gb = sh[k]
        outs += [cc[k].reshape(D), wm[k], bmod, n_g, s_in, lg, lb, aws, abs_, s_w2, s_gb, bng, s_pa, s_pb, s_out, fng]
    return tuple(outs)
```

```python
import jax
import jax.numpy as jnp
from jax import lax
from jax.experimental import pallas as pl
from jax.experimental.pallas import tpu as pltpu

f32, bf16 = jnp.float32, jnp.bfloat16

D = 1024
CTX = 256
EPS = 1e-6
AW = 512
ACH = 128
GW = 64
KW = 256
VW = 512
NH = 4
HK = 64
HV = 128
RANK = 16
TAU = 16.0
CH = 64
QSCALE = HK ** -0.5
INW = 5152
NDEV = 8

PQ, PK, PV, PZB, PUA, PVA, PZA, PG1, PG2, PLR, PW = 0, 256, 512, 1024, 1536, 2048, 2560, 3072, 4096, 5120, 5248
LRW = 128

ADAM_LR, ADAM_B1, ADAM_B2, ADAM_EPS, ADAM_WD, ADAM_STEP = 0.001, 0.9, 0.999, 1e-08, 0.01, 10

VMEM_LIMIT = 56 * 1024 * 1024
MESH = pl.DeviceIdType.MESH


def _cp(sem=None):
    return pltpu.CompilerParams(dimension_semantics=sem, vmem_limit_bytes=VMEM_LIMIT)


def _dot(a, b):
    return jnp.dot(a.astype(bf16), b.astype(bf16), preferred_element_type=f32)


def _nt(a, b):
    return lax.dot_general(a.astype(bf16), b.astype(bf16), (((1,), (1,)), ((), ())), preferred_element_type=f32)


def _tn(a, b):
    return lax.dot_general(a.astype(bf16), b.astype(bf16), (((0,), (0,)), ((), ())), preferred_element_type=f32)


def _dot_hi(a, b):
    return jnp.dot(a, b, preferred_element_type=f32, precision=lax.Precision.HIGHEST)


def _sigmoid(x):
    return 1.0 / (1.0 + jnp.exp(-x))


def _log_sigmoid(x):
    return jnp.minimum(x, 0.0) - jnp.log(1.0 + jnp.exp(-jnp.abs(x)))


def _silu_and_grad(z):
    s = _sigmoid(z)
    return z * s, s * (1.0 + z * (1.0 - s))


def _me():
    return 4 * lax.axis_index("x") + 2 * lax.axis_index("y") + lax.axis_index("c")


def _peer(k):
    x, y, c = lax.axis_index("x"), lax.axis_index("y"), lax.axis_index("c")
    px = 1 - x if k & 4 else x
    py = 1 - y if k & 2 else y
    pc = 1 - c if k & 1 else c
    return (px, py, pc), 4 * px + 2 * py + pc


def _fanout(srcs, dsts, send_sems, recv_sems, local_sems, owners=None):
    me = _me()
    n = len(srcs)
    owns = lambda a, j: True if owners is None or owners[a] is None else owners[a](j)

    def guarded(cond, fn):
        if cond is True:
            fn()
        else:
            pl.when(cond)(fn)

    def copies(with_recvs):
        local = [pltpu.make_async_copy(srcs[a](me), dsts[a](me), local_sems.at[a]) for a in range(n)]
        sends, recvs = [], []
        for k in range(1, NDEV):
            dev, idx = _peer(k)
            for a in range(n):
                s = (k - 1) * n + a
                sends.append((owns(a, idx), pltpu.make_async_remote_copy(
                    src_ref=srcs[a](idx), dst_ref=dsts[a](me), send_sem=send_sems.at[s], recv_sem=recv_sems.at[s],
                    device_id=dev, device_id_type=MESH)))
                if with_recvs:
                    recvs.append((owns(a, me), pltpu.make_async_remote_copy(
                        src_ref=srcs[a](idx), dst_ref=dsts[a](idx), send_sem=send_sems.at[s], recv_sem=recv_sems.at[s],
                        device_id=dev, device_id_type=MESH)))
        return local, sends, recvs

    def start():
        local, sends, _ = copies(False)
        for a, cp in enumerate(local):
            guarded(owns(a, me), cp.start)
        for cond, cp in sends:
            guarded(cond, cp.start)

    def finish():
        local, sends, recvs = copies(True)
        for cond, cp in recvs:
            guarded(cond, cp.wait_recv)
        for cond, cp in sends:
            guarded(cond, cp.wait_send)
        for a, cp in enumerate(local):
            guarded(owns(a, me), cp.wait)

    return start, finish


class _Comm:
    def __init__(self, ins, outs, plan):
        self.ins, self.outs, self.plan = list(ins), list(outs), plan
        self.n = len(self.outs)

    def specs(self):
        hbm = pl.BlockSpec(memory_space=pl.ANY)
        return [hbm] * len(self.ins), [hbm] * self.n, _fanout_sems(self.n) if self.n else []

    def run(self, in_refs, out_refs, sems, nsteps, compute):
        if not self.n:
            compute()
            return

        def hooks():
            srcs, dsts, owners = self.plan(in_refs, out_refs)
            return _fanout(srcs, dsts, sems[0], sems[1], sems[2], owners)

        pl.when(pl.program_id(0) == 0)(lambda: hooks()[0]())
        compute()
        pl.when(pl.program_id(0) == nsteps - 1)(lambda: hooks()[1]())


LATE_MID_STEP = 1


class _LateComm:
    def __init__(self, blocks, dgt):
        sds = jax.ShapeDtypeStruct
        self.ins = [blocks, dgt]
        self.outs = [sds((D, SHARD), bf16), sds((D, SHARD), bf16), sds((4, D, SHARD), bf16), sds(dgt.shape, f32)]
        self.n = len(self.outs)

    def specs(self):
        hbm = pl.BlockSpec(memory_space=pl.ANY)
        scratch = [pltpu.VMEM((3, D, SHARD), bf16), pltpu.SemaphoreType.DMA((2,)), pltpu.SemaphoreType.DMA((4,)),
                   pltpu.SemaphoreType.DMA((3,))] + _fanout_sems(1)
        return [hbm] * 2, [hbm] * self.n, scratch

    def run(self, in_refs, out_refs, scratch, nsteps, compute):
        late_ref, dgt_ref = in_refs
        sib_ref, pair_ref, parts_ref, gt_ref = out_refs
        vbuf, send_sems, recv_sems, local_sems, g_send, g_recv, g_local = scratch
        x, y, c = lax.axis_index("x"), lax.axis_index("y"), lax.axis_index("c")
        chip = 2 * x + y
        is_owner_chip = chip == 0
        step = pl.program_id(0)

        def to_sibling():
            return pltpu.make_async_remote_copy(src_ref=late_ref.at[1 - c], dst_ref=sib_ref, send_sem=send_sems.at[0],
                                                recv_sem=recv_sems.at[0], device_id=(x, y, 1 - c), device_id_type=MESH)

        def to_owner(k):
            return pltpu.make_async_remote_copy(src_ref=pair_ref, dst_ref=parts_ref.at[k], send_sem=send_sems.at[1],
                                                recv_sem=recv_sems.at[k], device_id=(0, 0, c), device_id_type=MESH)

        def own_copy():
            return pltpu.make_async_copy(pair_ref, parts_ref.at[0], local_sems.at[2])

        def gates():
            return _fanout([lambda j: dgt_ref.at[j]], [lambda j: gt_ref.at[j]], g_send, g_recv, g_local)

        @pl.when(step == 0)
        def _():
            to_sibling().start()
            gates()[0]()

        compute()

        @pl.when(step == LATE_MID_STEP)
        def _():
            mine = pltpu.make_async_copy(late_ref.at[c], vbuf.at[0], local_sems.at[0])
            mine.start()
            to_sibling().wait_recv()
            theirs = pltpu.make_async_copy(sib_ref, vbuf.at[1], local_sems.at[1])
            theirs.start()
            mine.wait()
            theirs.wait()
            vbuf[2] = (vbuf[0].astype(f32) + vbuf[1].astype(f32)).astype(bf16)
            pltpu.sync_copy(vbuf.at[2], pair_ref)
            pl.when(is_owner_chip)(lambda: own_copy().start())
            pl.when(jnp.logical_not(is_owner_chip))(lambda: to_owner(chip).start())

        @pl.when(step == nsteps - 1)
        def _():
            @pl.when(is_owner_chip)
            def _():
                for k in range(1, 4):
                    to_owner(k).wait_recv()
                own_copy().wait()

            pl.when(jnp.logical_not(is_owner_chip))(lambda: to_owner(chip).wait_send())
            to_sibling().wait_send()
            gates()[1]()


def _fanout_sems(n):
    return [pltpu.SemaphoreType.DMA(((NDEV - 1) * n,)), pltpu.SemaphoreType.DMA(((NDEV - 1) * n,)), pltpu.SemaphoreType.DMA((n,))]


def _lanes(j):
    return pl.ds(pl.multiple_of(j * 128, 128), 128)


def _gather_first(c_row, cctx_row, wm, bm, wi, gate):
    def body(c_ref, cctx_ref, wm_ref, bm_ref, wi_ref, g_ref, cs_ref, mods_ref, owi, og, call_ref, mine_ref,
             send_sems, recv_sems, local_sems, c_send, c_recv, c_local, m_send, m_recv, m_local):
        x, y, c = lax.axis_index("x"), lax.axis_index("y"), lax.axis_index("c")
        sibling = (x, y, 1 - c)
        chips = [(1 - x, y), (x, 1 - y), (1 - x, 1 - y)]
        index = lambda px, py, pc: 4 * px + 2 * py + pc
        arrays = ((wi_ref, owi), (g_ref, og))
        n = len(arrays)

        def copy(a, k, block, to, own=False):
            src, out = arrays[a]
            return pltpu.make_async_remote_copy(
                src_ref=src if own else out.at[index(*block)], dst_ref=out.at[index(*block)],
                send_sem=send_sems.at[k * n + a], recv_sem=recv_sems.at[k * n + a], device_id=to, device_id_type=MESH)

        c_start, c_finish = _fanout([lambda j: c_ref], [lambda j: call_ref.at[j]], c_send, c_recv, c_local)
        c_start()
        mine = [pltpu.make_async_copy(src, out.at[index(x, y, c)], local_sems.at[a]) for a, (src, out) in enumerate(arrays)]
        first = [copy(a, 0, (x, y, c), sibling, own=True) for a in range(n)]
        first += [copy(a, 1 + j, (x, y, c), (*chip, c), own=True) for j, chip in enumerate(chips) for a in range(n)]
        for cp in mine + first:
            cp.start()

        c_finish()
        cs = jnp.concatenate([call_ref[j] for j in range(NDEV)] + [cctx_ref[...], jnp.zeros((16 - NDEV - 1, D), f32)], axis=0)
        cs_ref[...] = cs
        s, _ = _silu_and_grad(cs)
        mine_ref[...] = _dot_hi(s, wm_ref[...]) + bm_ref[...]
        m_start, m_finish = _fanout([lambda j: mine_ref], [lambda j: mods_ref.at[j]], m_send, m_recv, m_local)
        m_start()

        passed = []
        for j, chip in enumerate(chips):
            for a in range(n):
                copy(a, 1 + j, (*chip, c), (x, y, c)).wait_recv()
            for a in range(n):
                cp = copy(a, 4 + j, (*chip, c), sibling)
                cp.start()
                passed.append(cp)
        for a in range(n):
            copy(a, 0, sibling, (x, y, c)).wait_recv()
        for j, chip in enumerate(chips):
            for a in range(n):
                copy(a, 4 + j, (*chip, 1 - c), (x, y, c)).wait_recv()
        for cp in first + passed:
            cp.wait_send()
        for cp in mine:
            cp.wait()
        m_finish()

    hbm = pl.BlockSpec(memory_space=pl.ANY)
    vm = pl.BlockSpec(memory_space=pltpu.VMEM)
    ncol = wm.shape[1]
    return pl.pallas_call(
        body, name="gather_first",
        out_shape=(jax.ShapeDtypeStruct((16, D), f32), jax.ShapeDtypeStruct((NDEV, 16, ncol), f32),
                   jax.ShapeDtypeStruct((NDEV,) + wi.shape, bf16), jax.ShapeDtypeStruct((NDEV,) + gate.shape, f32)),
        in_specs=[vm, vm, vm, vm, hbm, hbm], out_specs=(vm, vm, hbm, hbm),
        scratch_shapes=[pltpu.VMEM((NDEV, 1, D), f32), pltpu.VMEM((16, ncol), f32)] + _fanout_sems(2) + _fanout_sems(1) + _fanout_sems(1),
        compiler_params=_cp(),
    )(c_row, cctx_row, wm, bm, wi, gate)


SHARD = INW // NDEV
ROWS_RP = 128


def _overlap(lo, hi, a, b):
    s, e = max(lo, a), min(hi, b)
    return (s, e) if s < e else None


def _repack_w(wg):
    segs = ((0, 1024, PQ), (1024, 1024 + 2 * RANK, PLR), (1024 + 2 * RANK, INW, PZB))

    def body(g_ref, o_ref):
        for j in range(NDEV):
            lo, hi = j * SHARD, (j + 1) * SHARD
            for a, b, pad0 in segs:
                ov = _overlap(lo, hi, a, b)
                if ov:
                    s, e = ov
                    o_ref[:, pad0 + s - a:pad0 + e - a] = g_ref[j, :, s - lo:e - lo]
        o_ref[:, PLR + 2 * RANK:PW] = jnp.zeros((ROWS_RP, PW - PLR - 2 * RANK), bf16)

    return pl.pallas_call(
        body, name="repack_w", grid=(D // ROWS_RP,), out_shape=jax.ShapeDtypeStruct((D, PW), bf16),
        in_specs=[pl.BlockSpec((NDEV, ROWS_RP, SHARD), lambda i: (0, i, 0))],
        out_specs=pl.BlockSpec((ROWS_RP, PW), lambda i: (i, 0)), compiler_params=_cp(("arbitrary",)),
    )(wg)


LATE_END = 1024 + 2 * RANK
LATE_DESTS = 2


def _pack_blocks(o_ref, srcs, dests, dtype):
    for n, j in enumerate(dests):
        lo, hi = j * SHARD, (j + 1) * SHARD
        done = lo
        for a, b, src in srcs:
            ov = _overlap(lo, hi, a, b)
            if ov:
                s, e = ov
                if s > done:
                    o_ref[n, :, done - lo:s - lo] = jnp.zeros((ROWS_RP, s - done), dtype)
                o_ref[n, :, s - lo:e - lo] = src[:, s - a:e - a].astype(dtype)
                done = e
        if done < hi:
            o_ref[n, :, done - lo:hi - lo] = jnp.zeros((ROWS_RP, hi - done), dtype)


def _pack_dw_early(dw_zbua, dw_va, dw_zag, row0, nrows, name):
    def body(zbua_ref, va_ref, zag_ref, o_ref):
        _pack_blocks(o_ref, ((LATE_END, 2080, zbua_ref), (2080, 2592, va_ref), (2592, INW, zag_ref)), range(NDEV), bf16)

    row = lambda w: pl.BlockSpec((ROWS_RP, w), lambda i: (i + row0 // ROWS_RP, 0))
    return pl.pallas_call(
        body, name=name, grid=(nrows // ROWS_RP,), out_shape=jax.ShapeDtypeStruct((NDEV, nrows, SHARD), bf16),
        in_specs=[row(2 * AW), row(AW), row(AW + 2 * D)],
        out_specs=pl.BlockSpec((NDEV, ROWS_RP, SHARD), lambda i: (0, i, 0)), compiler_params=_cp(("arbitrary",)),
    )(dw_zbua, dw_va, dw_zag)


def _pack_dw_late(dw_qkv, dw_lr, dwk_c, dwv_c, dwl_c):
    def body(qkv_ref, lr_ref, kc_ref, vc_ref, lc_ref, o_ref):
        qkv = qkv_ref[...] + jnp.concatenate([jnp.zeros((ROWS_RP, KW), f32), kc_ref[...], vc_ref[...]], axis=1)
        lr = lr_ref[...] + lc_ref[...]
        _pack_blocks(o_ref, ((0, 1024, qkv), (1024, LATE_END, lr)), range(LATE_DESTS), bf16)

    row = lambda w: pl.BlockSpec((ROWS_RP, w), lambda i: (i, 0))
    return pl.pallas_call(
        body, name="pack_dw_late", grid=(D // ROWS_RP,), out_shape=jax.ShapeDtypeStruct((LATE_DESTS, D, SHARD), bf16),
        in_specs=[row(2 * KW + VW), row(LRW), row(KW), row(VW), row(LRW)],
        out_specs=pl.BlockSpec((LATE_DESTS, ROWS_RP, SHARD), lambda i: (0, i, 0)), compiler_params=_cp(("arbitrary",)),
    )(dw_qkv, dw_lr, dwk_c, dwv_c, dwl_c)


def _mod_tail(dnorm, dbm, dmod, dmodc, cs, wm, m_wm, v_wm, norm_g, m_ng, v_ng, b_mod, m_bm, v_bm, c_ctx, m_cc, v_cc):
    ncol = wm.shape[2]

    def body(dn_ref, dbm_ref, dmod_ref, dmodc_ref, cs_ref, wm_ref, mwm_ref, vwm_ref, ng_ref, mng_ref, vng_ref,
             bm_ref, mbm_ref, vbm_ref, cc_ref, mcc_ref, vcc_ref, *rest):
        o_wm, o_ng, o_bm, o_cc = rest[0:4], rest[4:8], rest[8:12], rest[12:16]
        a_dn, a_dbm, a_dmod, a_dmodc, dm_rows, gc_ref, a_gc = rest[16:23]
        s1, r1, l1, s2, r2, l2 = rest[23:29]
        start, finish = _fanout([lambda j: dn_ref, lambda j: dbm_ref, lambda j: dmod_ref, lambda j: dmodc_ref],
                                [lambda j, r=r: r.at[j] for r in (a_dn, a_dbm, a_dmod, a_dmodc)], s1, r1, l1)
        start()
        finish()

        def total(ref):
            t = ref[0]
            for j in range(1, NDEV):
                t = t + ref[j]
            return t

        _adam_update(total(a_dn), ng_ref, mng_ref, vng_ref, *o_ng)
        _adam_update(total(a_dbm), bm_ref, mbm_ref, vbm_ref, *o_bm)
        dmodc_tot = jnp.concatenate([total(a_dmodc), jnp.zeros((1, D), f32)], axis=1)
        dm_rows[...] = jnp.concatenate([a_dmod[j] for j in range(NDEV)] + [dmodc_tot, jnp.zeros((16 - NDEV - 1, 3 * D), f32)], axis=0)
        dm = dm_rows[:, pl.ds(pl.multiple_of(_me() * ncol, 128), ncol)]
        s, ds = _silu_and_grad(cs_ref[...])
        part = lax.dot_general(dm[8:9, :], wm_ref[0], (((1,), (1,)), ((), ())), preferred_element_type=f32,
                               precision=lax.Precision.HIGHEST)
        gc_ref[...] = part * ds[8:9, :]
        start2, finish2 = _fanout([lambda j: gc_ref], [lambda j: a_gc.at[j]], s2, r2, l2)
        start2()
        g = lax.dot_general(s, dm, (((0,), (0,)), ((), ())), preferred_element_type=f32, precision=lax.Precision.HIGHEST)
        _adam_update(g[None], wm_ref, mwm_ref, vwm_ref, *o_wm)
        finish2()
        _adam_update(total(a_gc), cc_ref, mcc_ref, vcc_ref, *o_cc)

    like = lambda a: [jax.ShapeDtypeStruct(a.shape, f32)] * 4
    row = lambda n: pltpu.VMEM((NDEV, 1, n), f32)
    res = pl.pallas_call(
        body, name="mod_tail", out_shape=tuple(like(wm) + like(norm_g) + like(b_mod) + like(c_ctx)),
        scratch_shapes=[row(D), row(3 * D), row(3 * D), row(2 * D), pltpu.VMEM((16, 3 * D), f32), pltpu.VMEM((1, D), f32), row(D)]
        + _fanout_sems(4) + _fanout_sems(1),
        compiler_params=_cp(),
    )(dnorm, dbm, dmod, dmodc, cs, wm, m_wm, v_wm, norm_g, m_ng, v_ng, b_mod, m_bm, v_bm, c_ctx, m_cc, v_cc)
    return res[0:4], res[4:8], res[8:12], res[12:16]


def _adam_update(g, w_ref, m_ref, v_ref, go_ref, d_ref, mo_ref, vo_ref):
    c1 = 1.0 / (1.0 - ADAM_B1 ** ADAM_STEP)
    c2 = 1.0 / (1.0 - ADAM_B2 ** ADAM_STEP)
    mn = ADAM_B1 * m_ref[...] + (1.0 - ADAM_B1) * g
    vn = ADAM_B2 * v_ref[...] + (1.0 - ADAM_B2) * (g * g)
    go_ref[...] = g
    mo_ref[...] = mn
    vo_ref[...] = vn
    d_ref[...] = -ADAM_LR * ((mn * c1) / (jnp.sqrt(vn * c2) + ADAM_EPS) + ADAM_WD * w_ref[...])


def _adam_w_in(parts_a, parts_b, parts_late, w, m, v):
    na = EARLY_A_ROWS // ROWS_RP

    def body(a_ref, b_ref, l_ref, w_ref, m_ref, v_ref, go_ref, d_ref, mo_ref, vo_ref):
        me = _me()
        first = pl.program_id(0) < na
        early = jnp.where(first, a_ref[0], b_ref[0]).astype(f32)
        for i in range(1, NDEV):
            early = early + jnp.where(first, a_ref[i], b_ref[i]).astype(f32)
        late = l_ref[0].astype(f32)
        for i in range(1, 4):
            late = late + l_ref[i].astype(f32)
        g = jnp.where(me >= LATE_DESTS - 1, early, 0.0) + jnp.where(me < LATE_DESTS, late, 0.0)
        _adam_update(g, w_ref, m_ref, v_ref, go_ref, d_ref, mo_ref, vo_ref)

    blk = pl.BlockSpec((None, ROWS_RP, SHARD), lambda i: (0, i, 0))
    return pl.pallas_call(
        body, name="adam_w_in", grid=(D // ROWS_RP,), out_shape=tuple(jax.ShapeDtypeStruct((1, D, SHARD), f32) for _ in range(4)),
        in_specs=[pl.BlockSpec((NDEV, ROWS_RP, SHARD), lambda i: (0, jnp.minimum(i, na - 1), 0)),
                  pl.BlockSpec((NDEV, ROWS_RP, SHARD), lambda i: (0, jnp.maximum(i - na, 0), 0)),
                  pl.BlockSpec((4, ROWS_RP, SHARD), lambda i: (0, i, 0)), blk, blk, blk],
        out_specs=(blk, blk, blk, blk), compiler_params=_cp(("arbitrary",)),
    )(parts_a, parts_b, parts_late, w, m, v)


def _adam_params(parts, ws, ms, vs, name):
    p = parts.shape[0]
    k = len(ws)
    nrows = [w.size // 128 for w in ws]
    starts = [sum(nrows[:i]) for i in range(k)]

    def shaped(g, shape):
        if len(shape) == 2:
            return jnp.concatenate([g[r:r + 1] for r in range(g.shape[0])], axis=1)
        return g.reshape(shape)

    def body(*refs):
        g_ref = refs[0]
        w_refs, m_refs, v_refs = refs[1:1 + k], refs[1 + k:1 + 2 * k], refs[1 + 2 * k:1 + 3 * k]
        outs = refs[1 + 3 * k:]
        for i in range(k):
            rows = slice(starts[i], starts[i] + nrows[i])
            g = g_ref[0, rows, :]
            for j in range(1, p):
                g = g + g_ref[j, rows, :]
            _adam_update(shaped(g, ws[i].shape), w_refs[i], m_refs[i], v_refs[i], outs[i], outs[k + i], outs[2 * k + i], outs[3 * k + i])

    vm = pl.BlockSpec(memory_space=pltpu.VMEM)
    res = pl.pallas_call(
        body, name=name, out_shape=tuple(jax.ShapeDtypeStruct(w.shape, f32) for _ in range(4) for w in ws),
        in_specs=[vm] * (1 + 3 * k), out_specs=(vm,) * (4 * k), compiler_params=_cp(),
    )(parts, *ws, *ms, *vs)
    return [res[i * k:(i + 1) * k] for i in range(4)]


def _adam_blocks(parts, ws, ms, vs, name):
    k = len(ws)

    def body(*refs):
        g_refs, w_refs, m_refs, v_refs = refs[0:k], refs[k:2 * k], refs[2 * k:3 * k], refs[3 * k:4 * k]
        outs = refs[4 * k:]
        for i in range(k):
            g = g_refs[i][0].astype(f32)
            for j in range(1, parts[i].shape[0]):
                g = g + g_refs[i][j].astype(f32)
            _adam_update(g[None], w_refs[i], m_refs[i], v_refs[i], outs[i], outs[k + i], outs[2 * k + i], outs[3 * k + i])

    vm = pl.BlockSpec(memory_space=pltpu.VMEM)
    res = pl.pallas_call(
        body, name=name, out_shape=tuple(jax.ShapeDtypeStruct(w.shape, f32) for _ in range(4) for w in ws),
        in_specs=[vm] * (4 * k), out_specs=(vm,) * (4 * k), compiler_params=_cp(),
    )(*parts, *ws, *ms, *vs)
    return [res[i * k:(i + 1) * k] for i in range(4)]


def _adam(parts, w, m, v, rows, name):
    p, r, c = parts.shape
    lead = w.ndim - 2

    def body(g_ref, w_ref, m_ref, v_ref, go_ref, d_ref, mo_ref, vo_ref):
        g = g_ref[0].astype(f32)
        for i in range(1, p):
            g = g + g_ref[i].astype(f32)
        _adam_update(g, w_ref, m_ref, v_ref, go_ref, d_ref, mo_ref, vo_ref)

    blk = pl.BlockSpec((None,) * lead + (rows, c), lambda i: (0,) * lead + (i, 0))
    return pl.pallas_call(
        body, name=name, grid=(r // rows,), out_shape=tuple(jax.ShapeDtypeStruct(w.shape, f32) for _ in range(4)),
        in_specs=[pl.BlockSpec((p, rows, c), lambda i: (0, i, 0)), blk, blk, blk], out_specs=(blk, blk, blk, blk),
        compiler_params=_cp(("arbitrary",)),
    )(parts, w, m, v)


def _in_proj(x, g, scale, shift, w_pad, tl, comm):
    l = x.shape[0]
    c_in, c_out, c_sems = comm.specs()

    def body(*refs):
        x_ref, g_ref, sc_ref, sh_ref, w_ref = refs[:5]
        cin = refs[5:5 + len(c_in)]
        p_ref, h_ref = refs[5 + len(c_in):7 + len(c_in)]
        cout = refs[7 + len(c_in):7 + len(c_in) + comm.n]
        sems = refs[7 + len(c_in) + comm.n:]

        def compute():
            xv = x_ref[...]
            r = lax.rsqrt(jnp.mean(xv * xv, axis=-1, keepdims=True) + EPS)
            h = (xv * r) * (g_ref[...] * (1.0 + sc_ref[...])) + sh_ref[...]
            hb = h.astype(bf16)
            h_ref[...] = hb
            p_ref[...] = jnp.dot(hb, w_ref[...], preferred_element_type=f32).astype(bf16)

        comm.run(cin, cout, sems, l // tl, compute)

    vec = pl.BlockSpec((1, D), lambda i: (0, 0))
    return pl.pallas_call(
        body, name="in_proj", grid=(l // tl,),
        out_shape=(jax.ShapeDtypeStruct((l, PW), bf16), jax.ShapeDtypeStruct((l, D), bf16)) + tuple(comm.outs),
        in_specs=[pl.BlockSpec((tl, D), lambda i: (i, 0)), vec, vec, vec, pl.BlockSpec((D, PW), lambda i: (0, 0))] + c_in,
        out_specs=(pl.BlockSpec((tl, PW), lambda i: (i, 0)), pl.BlockSpec((tl, D), lambda i: (i, 0))) + tuple(c_out),
        scratch_shapes=c_sems, compiler_params=_cp(("arbitrary",)),
    )(x, g, scale, shift, w_pad, *comm.ins)


def _tri(rev):
    i = lax.broadcasted_iota(jnp.int32, (CH, CH), 0)
    j = lax.broadcasted_iota(jnp.int32, (CH, CH), 1)
    return jnp.where((j >= i) if rev else (j <= i), 1.0, 0.0).astype(f32)


def _head_masks():
    lane = lax.broadcasted_iota(jnp.int32, (1, KW), 1) // HK
    return [jnp.where(lane == h, 1.0, 0.0).astype(f32) for h in range(NH)]


def _block_diag():
    r = lax.broadcasted_iota(jnp.int32, (VW, KW), 0) // HV
    c = lax.broadcasted_iota(jnp.int32, (VW, KW), 1) // HK
    return jnp.where(r == c, 1.0, 0.0).astype(f32)


def _decay(lr, w2, gb, tri, rev):
    logits = _dot(lr, w2) + gb
    a = _log_sigmoid(logits) * (1.0 / TAU)
    c = _dot_hi(tri, a)
    cl = c[0:1, :] if rev else c[CH - 1:CH, :]
    return logits, c, cl


def _state_fwd(k, v, c, cl, st, bd):
    return st * jnp.exp(cl) + bd * _tn(v, k * jnp.exp(cl - c))


def _state_bwd(k, v, c, cl, st0, dst, trit):
    ecl = jnp.exp(cl)
    edec = jnp.exp(cl - c)
    kdec = k * edec
    dv = _nt(kdec, dst)
    dkdec = _dot(v, dst)
    dcl = jnp.sum(dst * st0, axis=0, keepdims=True) * ecl + jnp.sum(dkdec * kdec, axis=0, keepdims=True)
    da = _dot_hi(trit, -dkdec * kdec) + dcl
    return dkdec * edec, dv, da, dst * ecl


def _bdot(a, b):
    return lax.dot_general(a.astype(bf16), b.astype(bf16), (((2,), (1,)), ((0,), (0,))), preferred_element_type=f32)


def _bnt(a, b):
    return lax.dot_general(a.astype(bf16), b.astype(bf16), (((2,), (2,)), ((0,), (0,))), preferred_element_type=f32)


def _btn(a, b):
    return lax.dot_general(a.astype(bf16), b.astype(bf16), (((1,), (1,)), ((0,), (0,))), preferred_element_type=f32)


def _scan_chunks(x, rev):
    nc = x.shape[0]
    hi = x.astype(bf16)
    r1 = x - hi.astype(f32)
    mid = r1.astype(bf16)
    lo = (r1 - mid.astype(f32)).astype(bf16)
    terms = jnp.concatenate([hi, mid, lo], axis=1)
    tri3 = jnp.broadcast_to(jnp.concatenate([_tri(rev)] * 3, axis=1).astype(bf16)[None], (nc, CH, 3 * CH))
    return lax.dot_general(tri3, terms, (((2,), (1,)), ((0,), (0,))), preferred_element_type=f32)


class _Tile:
    pass


def _tile_prep(q_ref, k_ref, lr_ref, w2_ref, gb_ref, rev, nc):
    t = _Tile()
    tg = nc * CH
    t.logits = _dot(lr_ref[...], w2_ref[...]) + gb_ref[...]
    c = _scan_chunks((_log_sigmoid(t.logits) * (1.0 / TAU)).reshape(nc, CH, KW), rev)
    cl = c[:, 0:1, :] if rev else c[:, CH - 1:CH, :]
    k = k_ref[...].astype(f32).reshape(nc, CH, KW)
    t.ec, t.enc, t.edec, t.ecl = jnp.exp(c), jnp.exp(-c), jnp.exp(cl - c), jnp.exp(cl)
    t.qd = q_ref[...].astype(f32).reshape(nc, CH, KW) * t.ec * QSCALE
    t.kd = k * t.enc
    t.kdec = k * t.edec
    hm = _head_masks()
    t.tri4 = jnp.concatenate([_tri(rev)] * NH, axis=0)[None]
    t.qs = jnp.concatenate([t.qd * hm[h] for h in range(NH)], axis=1)
    t.pst = _bnt(t.qs, t.kd) * t.tri4
    return t


def _gla_fwd(p, w2p, gb, s0, rev, tg, name):
    l = p.shape[0]
    nb, nc = l // tg, tg // CH

    def body(q_ref, k_ref, v_ref, lr_ref, w2_ref, gb_ref, s0_ref, o_ref, st_ref, st):
        @pl.when(pl.program_id(0) == 0)
        def _():
            st[...] = s0_ref[...]

        t = _tile_prep(q_ref, k_ref, lr_ref, w2_ref, gb_ref, rev, nc)
        v = v_ref[...].reshape(nc, CH, VW)
        intra = jnp.concatenate([_bdot(t.pst[:, h * CH:(h + 1) * CH], v[:, :, h * HV:(h + 1) * HV]) for h in range(NH)], axis=2)
        bd = _block_diag()
        s = st[...]
        for n in (range(nc - 1, -1, -1) if rev else range(nc)):
            st_ref[n] = s.astype(bf16)
            s = s * t.ecl[n] + bd * _tn(v[n], t.kdec[n])
        st[...] = s
        o_ref[...] = (_bnt(t.qd, st_ref[...]) + intra).reshape(tg, VW).astype(bf16)

    blk = (lambda i: nb - 1 - i) if rev else (lambda i: i)
    return pl.pallas_call(
        body, name=name, grid=(nb,),
        out_shape=(jax.ShapeDtypeStruct((l, VW), bf16), jax.ShapeDtypeStruct((l // CH, VW, KW), bf16)),
        in_specs=[pl.BlockSpec((tg, KW), lambda i: (blk(i), PQ // KW)), pl.BlockSpec((tg, KW), lambda i: (blk(i), PK // KW)),
                  pl.BlockSpec((tg, VW), lambda i: (blk(i), PV // VW)), pl.BlockSpec((tg, LRW), lambda i: (blk(i), PLR // LRW)),
                  pl.BlockSpec((LRW, KW), lambda i: (0, 0)), pl.BlockSpec((1, KW), lambda i: (0, 0)),
                  pl.BlockSpec((VW, KW), lambda i: (0, 0))],
        out_specs=(pl.BlockSpec((tg, VW), lambda i: (blk(i), 0)), pl.BlockSpec((nc, VW, KW), lambda i: (blk(i), 0, 0))),
        scratch_shapes=[pltpu.VMEM((VW, KW), f32)],
        compiler_params=_cp(("arbitrary",)),
    )(p, p, p, p, w2p, gb, s0)


def _gla_bwd(p, do, states, w2p, gb, prev, rev, tg, name, comm):
    l = p.shape[0]
    nb, nc = l // tg, tg // CH
    out_dt = bf16
    c_in, c_out, c_sems = comm.specs()

    def body(*refs):
        q_ref, k_ref, v_ref, lr_ref, do_ref, st_ref, w2_ref, gb_ref = refs[:8]
        refs = refs[8:]
        if prev is not None:
            pq_ref, pl_ref = refs[:2]
            refs = refs[2:]
        cin, refs = refs[:len(c_in)], refs[len(c_in):]
        dqkv_ref, dlr_ref, dw2_ref, dgb_ref, ds0_ref = refs[:5]
        cout, dst, ds_buf, sems = refs[5:5 + comm.n], refs[5 + comm.n], refs[6 + comm.n], refs[7 + comm.n:]
        comm.run(cin, cout, sems, nb, lambda: compute(q_ref, k_ref, v_ref, lr_ref, do_ref, st_ref, w2_ref, gb_ref,
                                                      pq_ref if prev is not None else None, pl_ref if prev is not None else None,
                                                      dqkv_ref, dlr_ref, dw2_ref, dgb_ref, ds0_ref, dst, ds_buf))

    def compute(q_ref, k_ref, v_ref, lr_ref, do_ref, st_ref, w2_ref, gb_ref, pq_ref, pl_ref,
                dqkv_ref, dlr_ref, dw2_ref, dgb_ref, ds0_ref, dst, ds_buf):
        @pl.when(pl.program_id(0) == 0)
        def _():
            dst[...] = jnp.zeros_like(dst)
            dw2_ref[...] = jnp.zeros_like(dw2_ref)
            dgb_ref[...] = jnp.zeros_like(dgb_ref)

        t = _tile_prep(q_ref, k_ref, lr_ref, w2_ref, gb_ref, rev, nc)
        hm = _head_masks()
        v = v_ref[...].reshape(nc, CH, VW)
        do = do_ref[...].reshape(nc, CH, VW)
        heads = lambda a, h: a[:, :, h * HV:(h + 1) * HV]
        dpst = jnp.concatenate([_bnt(heads(do, h), heads(v, h)) for h in range(NH)], axis=1) * t.tri4
        dv = jnp.concatenate([_btn(t.pst[:, h * CH:(h + 1) * CH], heads(do, h)) for h in range(NH)], axis=2)
        dqd = _bdot(do, st_ref[...])
        for h in range(NH):
            dqd = dqd + hm[h] * _bdot(dpst[:, h * CH:(h + 1) * CH], t.kd)
        dkd = _btn(dpst, t.qs)
        bd = _block_diag()
        d = dst[...]
        for n in (range(nc) if rev else range(nc - 1, -1, -1)):
            ds_buf[n] = d
            d = d * t.ecl[n] + bd * _tn(do[n], t.qd[n])
        dst[...] = d
        ds0_ref[...] = d
        ds = ds_buf[...]
        dv = dv + _bnt(t.kdec, ds)
        dkdec = _bdot(v, ds)
        dcl = jnp.sum(ds * st_ref[...].astype(f32), axis=1, keepdims=True) * t.ecl + jnp.sum(dkdec * t.kdec, axis=1, keepdims=True)
        dc = dqd * t.qd - dkd * t.kd - dkdec * t.kdec
        da = _scan_chunks(dc, not rev) + dcl
        dq = dqd * t.ec * QSCALE
        dk = dkd * t.enc + dkdec * t.edec
        dlog = da.reshape(tg, KW) * _sigmoid(-t.logits) * (1.0 / TAU)
        dlr = _nt(dlog, w2_ref[...])
        dw2_ref[...] += _tn(lr_ref[...], dlog)
        dgb_ref[...] += jnp.sum(dlog, axis=0, keepdims=True)
        dqkv = jnp.concatenate([dq, dk, dv], axis=2).reshape(tg, 2 * KW + VW)
        if prev is not None:
            dqkv = dqkv + pq_ref[...]
            dlr = dlr + pl_ref[...]
        dqkv_ref[...] = dqkv.astype(out_dt)
        dlr_ref[...] = dlr.astype(out_dt)

    blk = (lambda i: i) if rev else (lambda i: nb - 1 - i)
    in_specs = [pl.BlockSpec((tg, KW), lambda i: (blk(i), PQ // KW)), pl.BlockSpec((tg, KW), lambda i: (blk(i), PK // KW)),
                pl.BlockSpec((tg, VW), lambda i: (blk(i), PV // VW)), pl.BlockSpec((tg, LRW), lambda i: (blk(i), PLR // LRW)),
                pl.BlockSpec((tg, VW), lambda i: (blk(i), 0)), pl.BlockSpec((nc, VW, KW), lambda i: (blk(i), 0, 0)),
                pl.BlockSpec((LRW, KW), lambda i: (0, 0)), pl.BlockSpec((1, KW), lambda i: (0, 0))]
    args = [p, p, p, p, do, states, w2p, gb]
    if prev is not None:
        in_specs += [pl.BlockSpec((tg, 2 * KW + VW), lambda i: (blk(i), 0)), pl.BlockSpec((tg, LRW), lambda i: (blk(i), 0))]
        args += list(prev)
    return pl.pallas_call(
        body, name=name, grid=(nb,),
        out_shape=(jax.ShapeDtypeStruct((l, 2 * KW + VW), out_dt), jax.ShapeDtypeStruct((l, LRW), out_dt),
                   jax.ShapeDtypeStruct((LRW, KW), f32), jax.ShapeDtypeStruct((1, KW), f32), jax.ShapeDtypeStruct((VW, KW), f32))
        + tuple(comm.outs),
        in_specs=in_specs + c_in,
        out_specs=(pl.BlockSpec((tg, 2 * KW + VW), lambda i: (blk(i), 0)), pl.BlockSpec((tg, LRW), lambda i: (blk(i), 0)),
                   pl.BlockSpec((LRW, KW), lambda i: (0, 0)), pl.BlockSpec((1, KW), lambda i: (0, 0)),
                   pl.BlockSpec((VW, KW), lambda i: (0, 0))) + tuple(c_out),
        scratch_shapes=[pltpu.VMEM((VW, KW), f32), pltpu.VMEM((nc, VW, KW), f32)] + c_sems,
        compiler_params=_cp(("arbitrary",)),
    )(*args, *comm.ins)


def _ctx_hidden(ctx_ref, g_ref, sc_ref, sh_ref):
    xv = ctx_ref[...]
    r = lax.rsqrt(jnp.mean(xv * xv, axis=-1, keepdims=True) + EPS)
    xn = xv * r
    return xn, xn * (g_ref[...] * (1.0 + sc_ref[...])) + sh_ref[...]


_CTX_W_SPECS = [pl.BlockSpec((D, KW), lambda i: (0, PK // KW)), pl.BlockSpec((D, VW), lambda i: (0, PV // VW)),
                pl.BlockSpec((D, LRW), lambda i: (0, PLR // LRW))]


def _ctx_fwd(ctx, g, scale, shift, w_pad, w2f, w2b, gbf, gbb):
    ncc = CTX // CH

    def body(ctx_ref, g_ref, sc_ref, sh_ref, wk_ref, wv_ref, wl_ref, w2f_ref, w2b_ref, gbf_ref, gbb_ref, sf_ref, sb_ref):
        _, hc = _ctx_hidden(ctx_ref, g_ref, sc_ref, sh_ref)
        k, v, lr = _dot(hc, wk_ref[...]), _dot(hc, wv_ref[...]), _dot(hc, wl_ref[...])
        bd = _block_diag()
        for rev, w2_ref, gb_ref, out in ((False, w2f_ref, gbf_ref, sf_ref), (True, w2b_ref, gbb_ref, sb_ref)):
            tri = _tri(rev)
            st = jnp.zeros((VW, KW), f32)
            for j in (range(ncc - 1, -1, -1) if rev else range(ncc)):
                rows = slice(j * CH, (j + 1) * CH)
                _, c, cl = _decay(lr[rows], w2_ref[...], gb_ref[...], tri, rev)
                st = _state_fwd(k[rows], v[rows], c, cl, st, bd)
            out[...] = st

    vec = pl.BlockSpec((1, D), lambda i: (0, 0))
    w2s = pl.BlockSpec((LRW, KW), lambda i: (0, 0))
    gbs = pl.BlockSpec((1, KW), lambda i: (0, 0))
    sts = pl.BlockSpec((VW, KW), lambda i: (0, 0))
    return pl.pallas_call(
        body, name="ctx_fwd", grid=(1,), out_shape=(jax.ShapeDtypeStruct((VW, KW), f32),) * 2,
        in_specs=[pl.BlockSpec((CTX, D), lambda i: (0, 0)), vec, vec, vec] + _CTX_W_SPECS + [w2s, w2s, gbs, gbs],
        out_specs=(sts, sts), compiler_params=_cp(("arbitrary",)),
    )(ctx, g, scale, shift, w_pad, w_pad, w_pad, w2f, w2b, gbf, gbb)


def _ctx_bwd(ctx, g, scale, shift, w_pad, w2f, w2b, gbf, gbb, dsf, dsb):
    ncc = CTX // CH

    def body(ctx_ref, g_ref, sc_ref, sh_ref, wk_ref, wv_ref, wl_ref, w2f_ref, w2b_ref, gbf_ref, gbb_ref, dsf_ref, dsb_ref,
             dwk_ref, dwv_ref, dwl_ref, dmod_ref, dg_ref, dw2_ref, dgb_ref):
        xn, hc = _ctx_hidden(ctx_ref, g_ref, sc_ref, sh_ref)
        k, v, lr = _dot(hc, wk_ref[...]), _dot(hc, wv_ref[...]), _dot(hc, wl_ref[...])
        bd = _block_diag()
        dk_rows, dv_rows, dl_rows = [None] * ncc, [None] * ncc, [None] * ncc
        for d, (rev, w2_ref, gb_ref, ds_ref) in enumerate(((False, w2f_ref, gbf_ref, dsf_ref), (True, w2b_ref, gbb_ref, dsb_ref))):
            tri = _tri(rev)
            order = list(range(ncc - 1, -1, -1) if rev else range(ncc))
            st, saved = jnp.zeros((VW, KW), f32), {}
            for j in order:
                rows = slice(j * CH, (j + 1) * CH)
                logits, c, cl = _decay(lr[rows], w2_ref[...], gb_ref[...], tri, rev)
                saved[j] = (logits, c, cl, st)
                st = _state_fwd(k[rows], v[rows], c, cl, st, bd)
            dst = ds_ref[...]
            dw2 = jnp.zeros((LRW, KW), f32)
            dgb = jnp.zeros((1, KW), f32)
            for j in reversed(order):
                rows = slice(j * CH, (j + 1) * CH)
                logits, c, cl, st0 = saved[j]
                dk, dv, da, dst = _state_bwd(k[rows], v[rows], c, cl, st0, dst, _tri(not rev))
                dlog = da * _sigmoid(-logits) * (1.0 / TAU)
                dl = _nt(dlog, w2_ref[...])
                dw2 = dw2 + _tn(lr[rows], dlog)
                dgb = dgb + jnp.sum(dlog, axis=0, keepdims=True)
                dk_rows[j] = dk if dk_rows[j] is None else dk_rows[j] + dk
                dv_rows[j] = dv if dv_rows[j] is None else dv_rows[j] + dv
                dl_rows[j] = dl if dl_rows[j] is None else dl_rows[j] + dl
            dw2_ref[d] = dw2
            dgb_ref[d] = dgb
        dk, dv, dl = (jnp.concatenate(t, axis=0) for t in (dk_rows, dv_rows, dl_rows))
        dwk_ref[...] = _tn(hc, dk)
        dwv_ref[...] = _tn(hc, dv)
        dwl_ref[...] = _tn(hc, dl)
        dh = _nt(dk, wk_ref[...]) + _nt(dv, wv_ref[...]) + _nt(dl, wl_ref[...])
        gx = dh * xn
        dmod_ref[:, 0:D] = jnp.sum(dh, axis=0, keepdims=True)
        dmod_ref[:, D:2 * D] = jnp.sum(gx, axis=0, keepdims=True) * g_ref[...]
        dg_ref[...] = jnp.sum(gx, axis=0, keepdims=True) * (1.0 + sc_ref[...])

    vec = pl.BlockSpec((1, D), lambda i: (0, 0))
    w2s = pl.BlockSpec((LRW, KW), lambda i: (0, 0))
    gbs = pl.BlockSpec((1, KW), lambda i: (0, 0))
    sts = pl.BlockSpec((VW, KW), lambda i: (0, 0))
    full = lambda *s: pl.BlockSpec(s, lambda i: (0,) * len(s))
    return pl.pallas_call(
        body, name="ctx_bwd", grid=(1,),
        out_shape=(jax.ShapeDtypeStruct((D, KW), f32), jax.ShapeDtypeStruct((D, VW), f32), jax.ShapeDtypeStruct((D, LRW), f32),
                   jax.ShapeDtypeStruct((1, 2 * D), f32), jax.ShapeDtypeStruct((1, D), f32),
                   jax.ShapeDtypeStruct((2, LRW, KW), f32), jax.ShapeDtypeStruct((2, 1, KW), f32)),
        in_specs=[pl.BlockSpec((CTX, D), lambda i: (0, 0)), vec, vec, vec] + _CTX_W_SPECS + [w2s, w2s, gbs, gbs, sts, sts],
        out_specs=(full(D, KW), full(D, VW), full(D, LRW), full(1, 2 * D), full(1, D), full(2, LRW, KW), full(2, 1, KW)),
        compiler_params=_cp(("arbitrary",)),
    )(ctx, g, scale, shift, w_pad, w_pad, w_pad, w2f, w2b, gbf, gbb, dsf, dsb)


def _layernorm(va, g, b):
    mu = jnp.mean(va, axis=-1, keepdims=True)
    xc = va - mu
    rstd = lax.rsqrt(jnp.mean(xc * xc, axis=-1, keepdims=True) + EPS)
    vhat = xc * rstd
    return vhat, rstd, vhat * g + b


MIX_PIECES = 8


def _mix_fwd(p, ln_g, ln_b, ws, bs_t):
    l = p.shape[0]
    half = AW // 2
    rp = l // MIX_PIECES
    cpp = rp // ACH

    def body(p_ref, g_ref, b_ref, ws_ref, bs_ref, sv_hbm, va_buf, col_buf, sv_buf, in_sems, out_sems):
        piece = lambda i: pl.ds(pl.multiple_of(i * rp, rp), rp)
        load = lambda i: pltpu.make_async_copy(p_ref.at[piece(i), pl.ds(PVA, AW)], va_buf.at[piece(i)], in_sems.at[i])
        store = lambda i: pltpu.make_async_copy(sv_buf.at[piece(i)], sv_hbm.at[piece(i)], out_sems.at[i])
        for i in range(MIX_PIECES):
            load(i).start()

        def rows_piece(i, carry):
            load(i).wait()
            for j in range(cpp):
                rows = pl.ds(pl.multiple_of(i * rp + j * ACH, ACH), ACH)
                _, _, vn = _layernorm(va_buf[rows, :].astype(f32), g_ref[...], b_ref[...])
                for gi in range(2):
                    sl = slice(gi * ACH, (gi + 1) * ACH)
                    sv_buf[rows, sl] = (_dot(ws_ref[gi], vn[:, sl]) + bs_ref[:, gi:gi + 1]).astype(bf16)
                col_buf[0, rows, :] = vn[:, half:half + ACH]
                col_buf[1, rows, :] = vn[:, half + ACH:]
            return carry

        lax.fori_loop(0, MIX_PIECES, rows_piece, 0)

        def cols_step(cidx, carry):
            rows = pl.ds(cidx, ACH, stride=GW)
            for gi in range(2, 4):
                col_buf[gi - 2, rows, :] = _dot(ws_ref[gi], col_buf[gi - 2, rows, :]) + bs_ref[:, gi:gi + 1]
            return carry

        lax.fori_loop(0, GW, cols_step, 0, unroll=8)

        def out_piece(i, carry):
            for j in range(cpp):
                rows = pl.ds(pl.multiple_of(i * rp + j * ACH, ACH), ACH)
                sv_buf[rows, half:half + ACH] = col_buf[0, rows, :].astype(bf16)
                sv_buf[rows, half + ACH:] = col_buf[1, rows, :].astype(bf16)
            store(i).start()
            return carry

        lax.fori_loop(0, MIX_PIECES, out_piece, 0)
        for i in range(MIX_PIECES):
            store(i).wait()

    vm = pl.BlockSpec(memory_space=pltpu.VMEM)
    hbm = pl.BlockSpec(memory_space=pl.ANY)
    return pl.pallas_call(
        body, name="mix_fwd", out_shape=jax.ShapeDtypeStruct((l, AW), bf16),
        in_specs=[hbm, vm, vm, vm, vm], out_specs=hbm,
        scratch_shapes=[pltpu.VMEM((l, AW), bf16), pltpu.VMEM((2, l, ACH), f32), pltpu.VMEM((l, AW), bf16),
                        pltpu.SemaphoreType.DMA((MIX_PIECES,)), pltpu.SemaphoreType.DMA((MIX_PIECES,))],
        compiler_params=_cp(),
    )(p, ln_g, ln_b, ws, bs_t)


def _mix_bwd(p, dsv, ln_g, ln_b, ws_t):
    l = p.shape[0]
    half = AW // 2
    rp = l // MIX_PIECES
    cpp = rp // ACH

    def body(p_ref, dsv_hbm, g_ref, b_ref, wst_ref, dva_hbm, dws_ref, dbs_ref, dg_ref, db_ref,
             va_buf, dsv_buf, vn_col, ds_col, dva_buf, va_sems, ds_sems, out_sems):
        piece = lambda i: pl.ds(pl.multiple_of(i * rp, rp), rp)
        load_va = lambda i: pltpu.make_async_copy(p_ref.at[piece(i), pl.ds(PVA, AW)], va_buf.at[piece(i)], va_sems.at[i])
        load_ds = lambda i: pltpu.make_async_copy(dsv_hbm.at[piece(i)], dsv_buf.at[piece(i)], ds_sems.at[i])
        store = lambda i: pltpu.make_async_copy(dva_buf.at[piece(i)], dva_hbm.at[piece(i)], out_sems.at[i])
        for i in range(MIX_PIECES):
            load_va(i).start()
            load_ds(i).start()
        dws_ref[...] = jnp.zeros_like(dws_ref)
        dbs_ref[...] = jnp.zeros_like(dbs_ref)
        dg_ref[...] = jnp.zeros_like(dg_ref)
        db_ref[...] = jnp.zeros_like(db_ref)

        def rows_piece(i, carry):
            load_va(i).wait()
            load_ds(i).wait()
            for j in range(cpp):
                rows = pl.ds(pl.multiple_of(i * rp + j * ACH, ACH), ACH)
                _, _, vn = _layernorm(va_buf[rows, :].astype(f32), g_ref[...], b_ref[...])
                ds = dsv_buf[rows, :].astype(f32)
                for gi in range(2):
                    sl = slice(gi * ACH, (gi + 1) * ACH)
                    dws_ref[gi] += _nt(ds[:, sl], vn[:, sl])
                    dbs_ref[gi] += ds[:, sl]
                for gi in range(2):
                    sl = slice(half + gi * ACH, half + (gi + 1) * ACH)
                    vn_col[gi, rows, :] = vn[:, sl]
                    ds_col[gi, rows, :] = ds[:, sl]
            return carry

        lax.fori_loop(0, MIX_PIECES, rows_piece, 0)

        def cols_step(cidx, carry):
            rows = pl.ds(cidx, ACH, stride=GW)
            for gi in range(2, 4):
                ds = ds_col[gi - 2, rows, :]
                dws_ref[gi] += _nt(ds, vn_col[gi - 2, rows, :])
                dbs_ref[gi] += ds
                ds_col[gi - 2, rows, :] = _dot(wst_ref[gi], ds)
            return carry

        lax.fori_loop(0, GW, cols_step, 0, unroll=8)

        def out_piece(i, carry):
            for j in range(cpp):
                rows = pl.ds(pl.multiple_of(i * rp + j * ACH, ACH), ACH)
                vhat, rstd, _ = _layernorm(va_buf[rows, :].astype(f32), g_ref[...], b_ref[...])
                ds = dsv_buf[rows, :].astype(f32)
                dvn = jnp.concatenate([_dot(wst_ref[0], ds[:, 0:ACH]), _dot(wst_ref[1], ds[:, ACH:half]),
                                       ds_col[0, rows, :], ds_col[1, rows, :]], axis=1)
                dg_ref[...] += jnp.sum(dvn * vhat, axis=0, keepdims=True)
                db_ref[...] += jnp.sum(dvn, axis=0, keepdims=True)
                dvh = dvn * g_ref[...]
                dva = rstd * (dvh - jnp.mean(dvh, axis=-1, keepdims=True) - vhat * jnp.mean(dvh * vhat, axis=-1, keepdims=True))
                dva_buf[rows, :] = dva.astype(bf16)
            store(i).start()
            return carry

        lax.fori_loop(0, MIX_PIECES, out_piece, 0)
        for i in range(MIX_PIECES):
            store(i).wait()

    vm = pl.BlockSpec(memory_space=pltpu.VMEM)
    hbm = pl.BlockSpec(memory_space=pl.ANY)
    dma = lambda: pltpu.SemaphoreType.DMA((MIX_PIECES,))
    return pl.pallas_call(
        body, name="mix_bwd",
        out_shape=(jax.ShapeDtypeStruct((l, AW), bf16), jax.ShapeDtypeStruct((4, ACH, ACH), f32), jax.ShapeDtypeStruct((4, ACH, ACH), f32),
                   jax.ShapeDtypeStruct((1, AW), f32), jax.ShapeDtypeStruct((1, AW), f32)),
        in_specs=[hbm, hbm, vm, vm, vm], out_specs=(hbm, vm, vm, vm, vm),
        scratch_shapes=[pltpu.VMEM((l, AW), bf16), pltpu.VMEM((l, AW), bf16), pltpu.VMEM((2, l, ACH), f32), pltpu.VMEM((2, l, ACH), f32),
                        pltpu.VMEM((l, AW), bf16), dma(), dma(), dma()],
        compiler_params=_cp(),
    )(p, dsv, ln_g, ln_b, ws_t)


def _mid(x, tgt, p, o_f, o_b, sv, gate, gf, gb_norm, w_pa, w_pb, w_out, tl):
    l = x.shape[0]

    def body(x_ref, t_ref, zb_ref, ua_ref, za_ref, g1_ref, g2_ref, of_ref, ob_ref, sv_ref, gate_ref, gf_ref, gbn_ref,
             wpa_ref, wpb_ref, wout_ref,
             dx1_ref, dzbua_ref, dzag_ref, dsv_ref, do_ref, dwout_bf, dwpa_bf, dwpb_bf, dgf_ref, dgate_ref, dgbn_ref, loss_ref,
             dwout_ref, dwpa_ref, dwpb_ref):
        @pl.when(pl.program_id(0) == 0)
        def _():
            for r in (dwout_ref, dwpa_ref, dwpb_ref, dgf_ref, dgate_ref, dgbn_ref, loss_ref):
                r[...] = jnp.zeros_like(r)

        o = of_ref[...].astype(f32) + ob_ref[...].astype(f32)
        rr = jnp.concatenate(
            [jnp.broadcast_to(lax.rsqrt(jnp.mean(o[:, h * HV:(h + 1) * HV] ** 2, axis=-1, keepdims=True) + EPS), (tl, HV))
             for h in range(NH)], axis=1)
        ohat = o * rr
        on = ohat * gbn_ref[...]
        szb, dszb = _silu_and_grad(zb_ref[...].astype(f32))
        tb = on * szb
        u = ua_ref[...].astype(f32)
        svv = sv_ref[...].astype(f32)
        sza, dsza = _silu_and_grad(za_ref[...].astype(f32))
        ta = u * svv * sza
        ya = _dot(ta, wpa_ref[...])
        yb = _dot(tb, wpb_ref[...])
        g1 = _sigmoid(g1_ref[...].astype(f32))
        g2 = _sigmoid(g2_ref[...].astype(f32))
        m = g1 * ya + g2 * yb
        y2 = _dot(m, wout_ref[...])
        x1 = x_ref[...] + gate_ref[...] * y2
        r1 = lax.rsqrt(jnp.mean(x1 * x1, axis=-1, keepdims=True) + EPS)
        x1n = x1 * r1
        err = x1n * gf_ref[...] - t_ref[...]
        loss_ref[...] += jnp.sum(jnp.sum(err * err, axis=-1, keepdims=True), axis=0, keepdims=True) * (0.5 / D)
        dout = err * (1.0 / D)
        dgf_ref[...] += jnp.sum(dout * x1n, axis=0, keepdims=True)
        dx1n = dout * gf_ref[...]
        dx1 = r1 * (dx1n - x1n * jnp.mean(dx1n * x1n, axis=-1, keepdims=True))
        dx1_ref[...] = dx1
        dgate_ref[...] += jnp.sum(dx1 * y2, axis=0, keepdims=True)
        dy2 = dx1 * gate_ref[...]
        dwout_ref[...] += _tn(m, dy2)
        dm = _nt(dy2, wout_ref[...])
        dya = dm * g1
        dyb = dm * g2
        dzag_ref[:, AW:AW + D] = (dm * ya * g1 * (1.0 - g1)).astype(bf16)
        dzag_ref[:, AW + D:] = (dm * yb * g2 * (1.0 - g2)).astype(bf16)
        dwpa_ref[...] += _tn(ta, dya)
        dta = _nt(dya, wpa_ref[...])
        dzbua_ref[:, AW:] = (dta * svv * sza).astype(bf16)
        dsv_ref[...] = (dta * u * sza).astype(bf16)
        dzag_ref[:, 0:AW] = (dta * u * svv * dsza).astype(bf16)
        dwpb_ref[...] += _tn(tb, dyb)
        dtb = _nt(dyb, wpb_ref[...])
        don = dtb * szb
        dzbua_ref[:, 0:AW] = (dtb * on * dszb).astype(bf16)
        dgbn_ref[...] += jnp.sum(don * ohat, axis=0, keepdims=True)
        doh = don * gbn_ref[...]
        prod = doh * ohat
        mh = jnp.concatenate(
            [jnp.broadcast_to(jnp.mean(prod[:, h * HV:(h + 1) * HV], axis=-1, keepdims=True), (tl, HV)) for h in range(NH)], axis=1)
        do_ref[...] = (rr * (doh - ohat * mh)).astype(bf16)

        @pl.when(pl.program_id(0) == l // tl - 1)
        def _():
            for acc, out in ((dwout_ref, dwout_bf), (dwpa_ref, dwpa_bf), (dwpb_ref, dwpb_bf)):
                out[...] = acc[...].astype(bf16)

    row = lambda w, j: pl.BlockSpec((tl, w), lambda i, j=j: (i, j))
    full = lambda *s: pl.BlockSpec(s, lambda i: (0,) * len(s))
    return pl.pallas_call(
        body, name="mid", grid=(l // tl,),
        out_shape=(jax.ShapeDtypeStruct((l, D), f32), jax.ShapeDtypeStruct((l, 2 * AW), bf16), jax.ShapeDtypeStruct((l, AW + 2 * D), bf16),
                   jax.ShapeDtypeStruct((l, AW), bf16), jax.ShapeDtypeStruct((l, VW), bf16),
                   jax.ShapeDtypeStruct((D, D), bf16), jax.ShapeDtypeStruct((AW, D), bf16), jax.ShapeDtypeStruct((VW, D), bf16),
                   jax.ShapeDtypeStruct((1, D), f32), jax.ShapeDtypeStruct((1, D), f32), jax.ShapeDtypeStruct((1, VW), f32),
                   jax.ShapeDtypeStruct((1, 1), f32)),
        scratch_shapes=[pltpu.VMEM((D, D), f32), pltpu.VMEM((AW, D), f32), pltpu.VMEM((VW, D), f32)],
        in_specs=[row(D, 0), row(D, 0), row(AW, PZB // AW), row(AW, PUA // AW), row(AW, PZA // AW), row(D, PG1 // D), row(D, PG2 // D),
                  row(VW, 0), row(VW, 0), row(AW, 0), full(1, D), full(1, D), full(1, VW), full(AW, D), full(VW, D), full(D, D)],
        out_specs=(row(D, 0), row(2 * AW, 0), row(AW + 2 * D, 0), row(AW, 0), row(VW, 0),
                   full(D, D), full(AW, D), full(VW, D), full(1, D), full(1, D), full(1, VW), full(1, 1)),
        compiler_params=_cp(("arbitrary",)),
    )(x, tgt, p, p, p, p, p, o_f, o_b, sv, gate, gf, gb_norm, w_pa, w_pb, w_out)


def _in_bwd(x, dx1, dqkv, dzbua, dva, dzag, dlr, w_pad, g, scale, tl, comm):
    l = x.shape[0]
    c_in, c_out, c_sems = comm.specs()

    def body(*refs):
        cin = refs[14:14 + len(c_in)]
        outs = refs[14 + len(c_in):]
        comm.run(cin, outs[4:4 + comm.n], outs[4 + comm.n:], l // tl, lambda: compute(*refs[:14], *outs[:4]))

    def compute(x_ref, dx1_ref, a_ref, b_ref, c_ref, e_ref, lr_ref, wa_ref, wb_ref, wc_ref, we_ref, wl_ref, g_ref, sc_ref,
                gx_ref, dsh_ref, dsc_ref, dg_ref):
        @pl.when(pl.program_id(0) == 0)
        def _():
            for r in (dsh_ref, dsc_ref, dg_ref):
                r[...] = jnp.zeros_like(r)

        dh = (_nt(a_ref[...], wa_ref[...]) + _nt(b_ref[...], wb_ref[...]) + _nt(c_ref[...], wc_ref[...]) + _nt(e_ref[...], we_ref[...])
              + _nt(lr_ref[...], wl_ref[...]))
        xv = x_ref[...]
        r = lax.rsqrt(jnp.mean(xv * xv, axis=-1, keepdims=True) + EPS)
        xn = xv * r
        gxn = jnp.sum(dh * xn, axis=0, keepdims=True)
        dsh_ref[...] += jnp.sum(dh, axis=0, keepdims=True)
        dsc_ref[...] += gxn * g_ref[...]
        dg_ref[...] += gxn * (1.0 + sc_ref[...])
        dxn = dh * (g_ref[...] * (1.0 + sc_ref[...]))
        gx_ref[...] = dx1_ref[...] + r * (dxn - xn * jnp.mean(dxn * xn, axis=-1, keepdims=True))

    row = lambda w: pl.BlockSpec((tl, w), lambda i: (i, 0))
    wcol = lambda w, j: pl.BlockSpec((D, w), lambda i, j=j: (0, j))
    vec = pl.BlockSpec((1, D), lambda i: (0, 0))
    return pl.pallas_call(
        body, name="in_bwd", grid=(l // tl,),
        out_shape=(jax.ShapeDtypeStruct((l, D), f32),) + (jax.ShapeDtypeStruct((1, D), f32),) * 3 + tuple(comm.outs),
        in_specs=[row(D), row(D), row(2 * KW + VW), row(2 * AW), row(AW), row(AW + 2 * D), row(LRW),
                  wcol(2 * KW + VW, 0), wcol(2 * AW, PZB // (2 * AW)), wcol(AW, PVA // AW), wcol(AW + 2 * D, PZA // (AW + 2 * D)),
                  wcol(LRW, PLR // LRW), vec, vec] + c_in,
        out_specs=(row(D), vec, vec, vec) + tuple(c_out), scratch_shapes=c_sems,
        compiler_params=_cp(("arbitrary",)),
    )(x, dx1, dqkv, dzbua, dva, dzag, dlr, w_pad, w_pad, w_pad, w_pad, w_pad, g, scale, *comm.ins)


def _tn_matmul(a, bs, tl, name, comm=None):
    l, m = a.shape
    k = len(bs)
    comm = comm or _Comm([], [], None)
    c_in, c_out, c_sems = comm.specs()

    def body(a_ref, *refs):
        b_refs, cin = refs[:k], refs[k:k + len(c_in)]
        o_refs = refs[k + len(c_in):2 * k + len(c_in)]
        cout, sems = refs[2 * k + len(c_in):2 * k + len(c_in) + comm.n], refs[2 * k + len(c_in) + comm.n:]

        def compute():
            @pl.when(pl.program_id(0) == 0)
            def _():
                for o_ref in o_refs:
                    o_ref[...] = jnp.zeros_like(o_ref)

            av = a_ref[...]
            for b_ref, o_ref in zip(b_refs, o_refs):
                o_ref[...] += _tn(av, b_ref[...])

        comm.run(cin, cout, sems, l // tl, compute)

    return pl.pallas_call(
        body, name=name, grid=(l // tl,), out_shape=tuple(jax.ShapeDtypeStruct((m, b.shape[1]), f32) for b in bs) + tuple(comm.outs),
        in_specs=[pl.BlockSpec((tl, m), lambda i: (i, 0))] + [pl.BlockSpec((tl, b.shape[1]), lambda i: (i, 0)) for b in bs] + c_in,
        out_specs=tuple(pl.BlockSpec((m, b.shape[1]), lambda i: (0, 0)) for b in bs) + tuple(c_out),
        scratch_shapes=c_sems, compiler_params=_cp(("arbitrary",)),
    )(a, *bs, *comm.ins)


def _pad_gate(w2, gb):
    z = jnp.zeros((RANK, KW), f32)
    tail = jnp.zeros((LRW - 2 * RANK, KW), f32)
    w2f = jnp.concatenate([w2[0], z, tail], axis=0)
    w2b = jnp.concatenate([z, w2[1], tail], axis=0)
    return w2f, w2b, gb[0:1], gb[1:2]


EARLY_A_ROWS = 512


class _NoExchange:
    def __init__(self, w_pa, w_pb, w_out):
        self.weights = (w_pa, w_pb, w_out)

    def gather_proj(self):
        return _Comm([], [], None)

    def proj_weights(self, got):
        return self.weights

    def first(self, dw_out, dw_pa, dw_pb, small):
        return _Comm([], [], None)

    def early_a(self, blocks):
        return _Comm([], [], None)

    def early_b(self, blocks):
        return _Comm([], [], None)

    def late(self, blocks, dgt):
        return _Comm([], [], None)


class _Exchanges:
    def __init__(self, pa, pb, wo):
        self.shards = (pa, pb, wo)

    def gather_proj(self):
        def plan(i, o):
            srcs = [lambda j, r=r: r for r in i]
            dsts = [lambda j: o[0].at[:, _lanes(j)], lambda j: o[1].at[:, _lanes(j)], lambda j: o[2].at[j]]
            return srcs, dsts, None
        sds = jax.ShapeDtypeStruct
        return _Comm(self.shards, [sds((AW, D), bf16), sds((VW, D), bf16), sds((NDEV, 128, D), bf16)], plan)

    def proj_weights(self, got):
        return got[0], got[1], got[2].reshape(D, D)

    def first(self, dw_out, dw_pa, dw_pb, small):
        def plan(i, o):
            srcs = [lambda j: i[0].at[j], lambda j: i[1].at[:, _lanes(j)], lambda j: i[2].at[:, _lanes(j)], lambda j: i[3]]
            dsts = [lambda j, r=r: r.at[j] for r in o]
            return srcs, dsts, None
        sds = jax.ShapeDtypeStruct
        return _Comm([dw_out.reshape(NDEV, 128, D), dw_pa, dw_pb, small],
                     [sds((NDEV, 128, D), bf16), sds((NDEV, AW, 128), bf16), sds((NDEV, VW, 128), bf16),
                      sds((NDEV,) + small.shape, f32)], plan)

    def early_a(self, blocks):
        def plan(i, o):
            return [lambda j: i[0].at[j]], [lambda j: o[0].at[j]], [lambda j: j >= LATE_DESTS - 1]
        return _Comm([blocks], [jax.ShapeDtypeStruct(blocks.shape, bf16)], plan)

    def early_b(self, blocks):
        def plan(i, o):
            return [lambda j: i[0].at[j]], [lambda j: o[0].at[j]], [lambda j: j >= LATE_DESTS - 1]
        return _Comm([blocks], [jax.ShapeDtypeStruct(blocks.shape, bf16)], plan)

    def late(self, blocks, dgt):
        return _LateComm(blocks, dgt)


def _local_step(x, ctx, tgt, mod, modc, norm_g, w_pad, ln_g, ln_b, ws, bs, w2, gb, gb_norm, gf, xch):
    shift, scale, gate = mod[:, 0:D], mod[:, D:2 * D], mod[:, 2 * D:]
    shift_c, scale_c = modc[:, 0:D], modc[:, D:]
    w2f, w2b, gbf, gbb = _pad_gate(w2, gb)

    p, h, *got_proj = _in_proj(x, norm_g, scale, shift, w_pad, 512, xch.gather_proj())
    w_pa, w_pb, w_out = xch.proj_weights(got_proj)
    sc_f, sc_b = _ctx_fwd(ctx, norm_g, scale_c, shift_c, w_pad, w2f, w2b, gbf, gbb)
    o_f, st_f = _gla_fwd(p, w2f, gbf, sc_f, False, 512, "gla_fwd_f")
    o_b, st_b = _gla_fwd(p, w2b, gbb, sc_b, True, 512, "gla_fwd_b")
    sv = _mix_fwd(p, ln_g, ln_b, ws.astype(bf16), bs.T)
    (dx1, dzbua, dzag, dsv, do, dw_out, dw_pa, dw_pb, dgf, dgate, dgbn, loss) = _mid(
        x, tgt, p, o_f, o_b, sv, gate, gf, gb_norm, w_pa, w_pb, w_out, 256)
    dva, dws, dbs_acc, dln_g, dln_b = _mix_bwd(p, dsv, ln_g, ln_b, jnp.swapaxes(ws, 1, 2).astype(bf16))
    small = _rows128(dln_g, dln_b, dws, jnp.sum(dbs_acc, axis=-1), dgbn, dgf, jnp.broadcast_to(loss, (1, 128)))
    dw_zbua, dw_va, dw_zag, *got_first = _tn_matmul(h, [dzbua, dva, dzag], 1024, "dw_early", xch.first(dw_out, dw_pa, dw_pb, small))
    blocks_a = _pack_dw_early(dw_zbua, dw_va, dw_zag, 0, EARLY_A_ROWS, "pack_dw_early_a")
    blocks_b = _pack_dw_early(dw_zbua, dw_va, dw_zag, EARLY_A_ROWS, D - EARLY_A_ROWS, "pack_dw_early_b")

    dqkv_f, dlr_f, dw2f, dgbf, dsc_f, *got_a = _gla_bwd(p, do, st_f, w2f, gbf, None, False, 512, "gla_bwd_f",
                                                        xch.early_a(blocks_a))
    dqkv, dlr, dw2b, dgbb, dsc_b, *got_b = _gla_bwd(p, do, st_b, w2b, gbb, (dqkv_f, dlr_f), True, 512, "gla_bwd_b",
                                                    xch.early_b(blocks_b))
    dwk_c, dwv_c, dwl_c, dmodc, dg_c, dw2c, dgbc = _ctx_bwd(ctx, norm_g, scale_c, shift_c, w_pad, w2f, w2b, gbf, gbb, dsc_f, dsc_b)
    dw_qkv, dw_lr = _tn_matmul(h, [dqkv, dlr], 1024, "dw_qkv_lr")
    blocks_late = _pack_dw_late(dw_qkv, dw_lr, dwk_c, dwv_c, dwl_c)
    dw2 = jnp.stack([dw2f[0:RANK] + dw2c[0, 0:RANK], dw2b[RANK:2 * RANK] + dw2c[1, RANK:2 * RANK]])
    dgb = jnp.concatenate([dgbf + dgbc[0], dgbb + dgbc[1]], axis=0)
    dgt = jnp.concatenate([jnp.transpose(dw2.reshape(2, RANK, NDEV, 32), (2, 0, 1, 3)).reshape(NDEV, 2 * RANK * 32),
                           jnp.transpose(dgb.reshape(2, NDEV, 32), (1, 0, 2)).reshape(NDEV, 64),
                           jnp.zeros((NDEV, 64), f32)], axis=1).reshape(NDEV, 9, 128)
    gx, dshift, dscale, dg, *got_late = _in_bwd(x, dx1, dqkv, dzbua, dva, dzag, dlr, w_pad, norm_g, scale, 512,
                                                xch.late(blocks_late, dgt))
    return dict(loss=loss, gx=gx, dmod=jnp.concatenate([dshift, dscale, dgate], axis=1), dmodc=dmodc, dnorm_g=dg + dg_c,
                small=small, blocks_a=blocks_a, blocks_b=blocks_b, blocks_late=blocks_late, dw2=dw2, dgb=dgb,
                dw_pa=dw_pa, dw_pb=dw_pb, dw_out=dw_out, got_first=got_first, got_a=got_a, got_b=got_b, got_late=got_late)


def _rows128(*vs):
    out = []
    for t in vs:
        t = t.reshape(-1)
        pad = (-t.shape[0]) % 128
        out.append(jnp.pad(t, (0, pad)) if pad else t)
    return jnp.concatenate(out).reshape(-1, 128)


def kernel(x, c, ctx, c_ctx, w_mod, b_mod, norm_g, w_in, a_ln_g, a_ln_b, a_ws, a_bs, b_gate_w2, b_gate_b, b_norm_g, w_proj_a, w_proj_b, w_out, final_norm_g, loss_target, m_c_ctx, m_w_mod, m_b_mod, m_norm_g, m_w_in, m_a_ln_g, m_a_ln_b, m_a_ws, m_a_bs, m_b_gate_w2, m_b_gate_b, m_b_norm_g, m_w_proj_a, m_w_proj_b, m_w_out, m_final_norm_g, v_c_ctx, v_w_mod, v_b_mod, v_norm_g, v_w_in, v_a_ln_g, v_a_ln_b, v_a_ws, v_a_bs, v_b_gate_w2, v_b_gate_b, v_b_norm_g, v_w_proj_a, v_w_proj_b, v_w_out, v_final_norm_g):
    me = _me()
    ncol = w_mod.shape[2]

    gate_mine = _rows128(jnp.concatenate([b_gate_w2.reshape(-1), b_gate_b.reshape(-1)]))
    bm_mine = lax.dynamic_slice(b_mod, (0, me * ncol), (1, ncol))
    cs, mods, wg, gates = _gather_first(c, c_ctx.reshape(1, D), w_mod[0], bm_mine, w_in[0].astype(bf16), gate_mine)
    w_pad = _repack_w(wg)
    gflat = gates.reshape(NDEV, 9 * 128)
    w2 = jnp.transpose(gflat[:, 0:2 * RANK * 32].reshape(NDEV, 2, RANK, 32), (1, 2, 0, 3)).reshape(2, RANK, KW)
    gb = jnp.transpose(gflat[:, 2 * RANK * 32:2 * RANK * 32 + 64].reshape(NDEV, 2, 32), (1, 0, 2)).reshape(2, KW)

    mods = jnp.transpose(mods, (1, 0, 2)).reshape(16, 3 * D)
    mod = lax.dynamic_slice(mods, (me, 0), (1, 3 * D))
    modc = mods[8:9, 0:2 * D]

    xch = _Exchanges(w_proj_a[0].astype(bf16), w_proj_b[0].astype(bf16), w_out[0].astype(bf16))
    r = _local_step(x[0], ctx[0], loss_target[0], mod, modc, norm_g, w_pad, a_ln_g, a_ln_b, a_ws[0], a_bs[0], w2, gb,
                    b_norm_g, final_norm_g.reshape(1, D), xch)
    p_out, p_pa, p_pb, smalls_e = r["got_first"]
    (p_in_a,) = r["got_a"]
    (p_in_b,) = r["got_b"]
    _, _, p_in_late, p_gt = r["got_late"]

    n_e = (AW + AW + 4 * ACH * ACH + AW + VW + D) // 128
    row = lambda t: t.reshape(1, D)
    rep_e = _adam_params(smalls_e, [a_ln_g, a_ln_b, a_ws, a_bs, b_norm_g, row(final_norm_g)],
                         [m_a_ln_g, m_a_ln_b, m_a_ws, m_a_bs, m_b_norm_g, row(m_final_norm_g)],
                         [v_a_ln_g, v_a_ln_b, v_a_ws, v_a_bs, v_b_norm_g, row(v_final_norm_g)], "adam_rep_early")
    rep_e = [t[0:5] + (t[5].reshape(D),) for t in rep_e]
    losses = smalls_e[:, n_e, 0]
    loss = losses[0]
    for i in range(1, NDEV):
        loss = loss + losses[i]

    dbm = r["dmod"] + jnp.concatenate([r["dmodc"], jnp.zeros((1, D), f32)], axis=1)
    wm, ng, bmod_r, cc = _mod_tail(r["dnorm_g"], dbm, r["dmod"], r["dmodc"], cs, w_mod, m_w_mod, v_w_mod,
                                   norm_g, m_norm_g, v_norm_g, b_mod, m_b_mod, v_b_mod,
                                   c_ctx.reshape(1, D), m_c_ctx.reshape(1, D), v_c_ctx.reshape(1, D))

    a_in = _adam_w_in(p_in_a, p_in_b, p_in_late, w_in, m_w_in, v_w_in)
    blk = _adam_blocks([p_pa, p_pb, p_out], [w_proj_a, w_proj_b, w_out], [m_w_proj_a, m_w_proj_b, m_w_out],
                       [v_w_proj_a, v_w_proj_b, v_w_out], "adam_proj_out")
    a_pa, a_pb, a_out = ([t[i] for t in blk] for i in range(3))
    gate_m = _rows128(jnp.concatenate([m_b_gate_w2.reshape(-1), m_b_gate_b.reshape(-1)]))
    gate_v = _rows128(jnp.concatenate([v_b_gate_w2.reshape(-1), v_b_gate_b.reshape(-1)]))
    a_gt = [t.reshape(-1) for t in _adam(p_gt, gate_mine, gate_m, gate_v, 9, "adam_gate")]
    nw2 = 2 * RANK * 32
    sh = [(a_in[k], a_pa[k], a_pb[k], a_out[k], a_gt[k][0:nw2].reshape(1, 2, RANK, 32),
           a_gt[k][nw2:nw2 + 64].reshape(1, 2, 32)) for k in range(4)]

    outs = [loss, r["gx"][None]]
    for k in range(4):
        lg, lb, aws, abs_, bng, fng = rep_e[k]
        n_g, bmod = ng[k], bmod_r[k]
        s_in, s_pa, s_pb, s_out, s_w2, s_gb = sh[k]
        outs += [cc[k].reshape(D), wm[k], bmod, n_g, s_in, lg, lb, aws, abs_, s_w2, s_gb, bng, s_pa, s_pb, s_out, fng]
    return tuple(outs)
```

```python
import jax
import jax.numpy as jnp
from jax import lax
from jax.experimental import pallas as pl
from jax.experimental.pallas import tpu as pltpu

f32, bf16 = jnp.float32, jnp.bfloat16

D = 1024
CTX = 256
EPS = 1e-6
AW = 512
ACH = 128
GW = 64
KW = 256
VW = 512
NH = 4
HK = 64
HV = 128
RANK = 16
TAU = 16.0
CH = 64
QSCALE = HK ** -0.5
INW = 5152
NDEV = 8

PQ, PK, PV, PZB, PUA, PVA, PZA, PG1, PG2, PLR, PW = 0, 256, 512, 1024, 1536, 2048, 2560, 3072, 4096, 5120, 5248
LRW = 128

ADAM_LR, ADAM_B1, ADAM_B2, ADAM_EPS, ADAM_WD, ADAM_STEP = 0.001, 0.9, 0.999, 1e-08, 0.01, 10

VMEM_LIMIT = 56 * 1024 * 1024
MESH = pl.DeviceIdType.MESH


def _cp(sem=None):
    return pltpu.CompilerParams(dimension_semantics=sem, vmem_limit_bytes=VMEM_LIMIT)


def _dot(a, b):
    return jnp.dot(a.astype(bf16), b.astype(bf16), preferred_element_type=f32)


def _nt(a, b):
    return lax.dot_general(a.astype(bf16), b.astype(bf16), (((1,), (1,)), ((), ())), preferred_element_type=f32)


def _tn(a, b):
    return lax.dot_general(a.astype(bf16), b.astype(bf16), (((0,), (0,)), ((), ())), preferred_element_type=f32)


def _dot_hi(a, b):
    return jnp.dot(a, b, preferred_element_type=f32, precision=lax.Precision.HIGHEST)


def _sigmoid(x):
    return 1.0 / (1.0 + jnp.exp(-x))


def _log_sigmoid(x):
    return jnp.minimum(x, 0.0) - jnp.log(1.0 + jnp.exp(-jnp.abs(x)))


def _silu_and_grad(z):
    s = _sigmoid(z)
    return z * s, s * (1.0 + z * (1.0 - s))


def _me():
    return 4 * lax.axis_index("x") + 2 * lax.axis_index("y") + lax.axis_index("c")


def _peer(k):
    x, y, c = lax.axis_index("x"), lax.axis_index("y"), lax.axis_index("c")
    px = 1 - x if k & 4 else x
    py = 1 - y if k & 2 else y
    pc = 1 - c if k & 1 else c
    return (px, py, pc), 4 * px + 2 * py + pc


def _fanout(srcs, dsts, send_sems, recv_sems, local_sems, owners=None):
    me = _me()
    n = len(srcs)
    owns = lambda a, j: True if owners is None or owners[a] is None else owners[a](j)

    def guarded(cond, fn):
        if cond is True:
            fn()
        else:
            pl.when(cond)(fn)

    def copies(with_recvs):
        local = [pltpu.make_async_copy(srcs[a](me), dsts[a](me), local_sems.at[a]) for a in range(n)]
        sends, recvs = [], []
        for k in range(1, NDEV):
            dev, idx = _peer(k)
            for a in range(n):
                s = (k - 1) * n + a
                sends.append((owns(a, idx), pltpu.make_async_remote_copy(
                    src_ref=srcs[a](idx), dst_ref=dsts[a](me), send_sem=send_sems.at[s], recv_sem=recv_sems.at[s],
                    device_id=dev, device_id_type=MESH)))
                if with_recvs:
                    recvs.append((owns(a, me), pltpu.make_async_remote_copy(
                        src_ref=srcs[a](idx), dst_ref=dsts[a](idx), send_sem=send_sems.at[s], recv_sem=recv_sems.at[s],
                        device_id=dev, device_id_type=MESH)))
        return local, sends, recvs

    def start():
        local, sends, _ = copies(False)
        for a, cp in enumerate(local):
            guarded(owns(a, me), cp.start)
        for cond, cp in sends:
            guarded(cond, cp.start)

    def finish():
        local, sends, recvs = copies(True)
        for cond, cp in recvs:
            guarded(cond, cp.wait_recv)
        for cond, cp in sends:
            guarded(cond, cp.wait_send)
        for a, cp in enumerate(local):
            guarded(owns(a, me), cp.wait)

    return start, finish


class _Comm:
    def __init__(self, ins, outs, plan):
        self.ins, self.outs, self.plan = list(ins), list(outs), plan
        self.n = len(self.outs)

    def specs(self):
        hbm = pl.BlockSpec(memory_space=pl.ANY)
        return [hbm] * len(self.ins), [hbm] * self.n, _fanout_sems(self.n) if self.n else []

    def run(self, in_refs, out_refs, sems, nsteps, compute):
        if not self.n:
            compute()
            return

        def hooks():
            srcs, dsts, owners = self.plan(in_refs, out_refs)
            return _fanout(srcs, dsts, sems[0], sems[1], sems[2], owners)

        pl.when(pl.program_id(0) == 0)(lambda: hooks()[0]())
        compute()
        pl.when(pl.program_id(0) == nsteps - 1)(lambda: hooks()[1]())


LATE_MID_STEP = 1


class _LateComm:
    def __init__(self, blocks, dgt):
        sds = jax.ShapeDtypeStruct
        self.ins = [blocks, dgt]
        self.outs = [sds((D, SHARD), bf16), sds((D, SHARD), bf16), sds((4, D, SHARD), bf16), sds(dgt.shape, f32)]
        self.n = len(self.outs)

    def specs(self):
        hbm = pl.BlockSpec(memory_space=pl.ANY)
        scratch = [pltpu.VMEM((3, D, SHARD), bf16), pltpu.SemaphoreType.DMA((2,)), pltpu.SemaphoreType.DMA((4,)),
                   pltpu.SemaphoreType.DMA((3,))] + _fanout_sems(1)
        return [hbm] * 2, [hbm] * self.n, scratch

    def run(self, in_refs, out_refs, scratch, nsteps, compute):
        late_ref, dgt_ref = in_refs
        sib_ref, pair_ref, parts_ref, gt_ref = out_refs
        vbuf, send_sems, recv_sems, local_sems, g_send, g_recv, g_local = scratch
        x, y, c = lax.axis_index("x"), lax.axis_index("y"), lax.axis_index("c")
        chip = 2 * x + y
        is_owner_chip = chip == 0
        step = pl.program_id(0)

        def to_sibling():
            return pltpu.make_async_remote_copy(src_ref=late_ref.at[1 - c], dst_ref=sib_ref, send_sem=send_sems.at[0],
                                                recv_sem=recv_sems.at[0], device_id=(x, y, 1 - c), device_id_type=MESH)

        def to_owner(k):
            return pltpu.make_async_remote_copy(src_ref=pair_ref, dst_ref=parts_ref.at[k], send_sem=send_sems.at[1],
                                                recv_sem=recv_sems.at[k], device_id=(0, 0, c), device_id_type=MESH)

        def own_copy():
            return pltpu.make_async_copy(pair_ref, parts_ref.at[0], local_sems.at[2])

        def gates():
            return _fanout([lambda j: dgt_ref.at[j]], [lambda j: gt_ref.at[j]], g_send, g_recv, g_local)

        @pl.when(step == 0)
        def _():
            to_sibling().start()
            gates()[0]()

        compute()

        @pl.when(step == LATE_MID_STEP)
        def _():
            mine = pltpu.make_async_copy(late_ref.at[c], vbuf.at[0], local_sems.at[0])
            mine.start()
            to_sibling().wait_recv()
            theirs = pltpu.make_async_copy(sib_ref, vbuf.at[1], local_sems.at[1])
            theirs.start()
            mine.wait()
            theirs.wait()
            vbuf[2] = (vbuf[0].astype(f32) + vbuf[1].astype(f32)).astype(bf16)
            pltpu.sync_copy(vbuf.at[2], pair_ref)
            pl.when(is_owner_chip)(lambda: own_copy().start())
            pl.when(jnp.logical_not(is_owner_chip))(lambda: to_owner(chip).start())

        @pl.when(step == nsteps - 1)
        def _():
            @pl.when(is_owner_chip)
            def _():
                for k in range(1, 4):
                    to_owner(k).wait_recv()
                own_copy().wait()

            pl.when(jnp.logical_not(is_owner_chip))(lambda: to_owner(chip).wait_send())
            to_sibling().wait_send()
            gates()[1]()


def _fanout_sems(n):
    return [pltpu.SemaphoreType.DMA(((NDEV - 1) * n,)), pltpu.SemaphoreType.DMA(((NDEV - 1) * n,)), pltpu.SemaphoreType.DMA((n,))]


def _lanes(j):
    return pl.ds(pl.multiple_of(j * 128, 128), 128)


def _gather_first(c_row, cctx_row, wm, bm, wi, gate):
    def body(c_ref, cctx_ref, wm_ref, bm_ref, wi_ref, g_ref, cs_ref, mods_ref, owi, og, call_ref, mine_ref,
             send_sems, recv_sems, local_sems, c_send, c_recv, c_local, m_send, m_recv, m_local):
        x, y, c = lax.axis_index("x"), lax.axis_index("y"), lax.axis_index("c")
        sibling = (x, y, 1 - c)
        chips = [(1 - x, y), (x, 1 - y), (1 - x, 1 - y)]
        index = lambda px, py, pc: 4 * px + 2 * py + pc
        arrays = ((wi_ref, owi), (g_ref, og))
        n = len(arrays)

        def copy(a, k, block, to, own=False):
            src, out = arrays[a]
            return pltpu.make_async_remote_copy(
                src_ref=src if own else out.at[index(*block)], dst_ref=out.at[index(*block)],
                send_sem=send_sems.at[k * n + a], recv_sem=recv_sems.at[k * n + a], device_id=to, device_id_type=MESH)

        c_start, c_finish = _fanout([lambda j: c_ref], [lambda j: call_ref.at[j]], c_send, c_recv, c_local)
        c_start()
        mine = [pltpu.make_async_copy(src, out.at[index(x, y, c)], local_sems.at[a]) for a, (src, out) in enumerate(arrays)]
        first = [copy(a, 0, (x, y, c), sibling, own=True) for a in range(n)]
        first += [copy(a, 1 + j, (x, y, c), (*chip, c), own=True) for j, chip in enumerate(chips) for a in range(n)]
        for cp in mine + first:
            cp.start()

        c_finish()
        cs = jnp.concatenate([call_ref[j] for j in range(NDEV)] + [cctx_ref[...], jnp.zeros((16 - NDEV - 1, D), f32)], axis=0)
        cs_ref[...] = cs
        s, _ = _silu_and_grad(cs)
        mine_ref[...] = _dot_hi(s, wm_ref[...]) + bm_ref[...]
        m_start, m_finish = _fanout([lambda j: mine_ref], [lambda j: mods_ref.at[j]], m_send, m_recv, m_local)
        m_start()

        passed = []
        for j, chip in enumerate(chips):
            for a in range(n):
                copy(a, 1 + j, (*chip, c), (x, y, c)).wait_recv()
            for a in range(n):
                cp = copy(a, 4 + j, (*chip, c), sibling)
                cp.start()
                passed.append(cp)
        for a in range(n):
            copy(a, 0, sibling, (x, y, c)).wait_recv()
        for j, chip in enumerate(chips):
            for a in range(n):
                copy(a, 4 + j, (*chip, 1 - c), (x, y, c)).wait_recv()
        for cp in first + passed:
            cp.wait_send()
        for cp in mine:
            cp.wait()
        m_finish()

    hbm = pl.BlockSpec(memory_space=pl.ANY)
    vm = pl.BlockSpec(memory_space=pltpu.VMEM)
    ncol = wm.shape[1]
    return pl.pallas_call(
        body, name="gather_first",
        out_shape=(jax.ShapeDtypeStruct((16, D), f32), jax.ShapeDtypeStruct((NDEV, 16, ncol), f32),
                   jax.ShapeDtypeStruct((NDEV,) + wi.shape, bf16), jax.ShapeDtypeStruct((NDEV,) + gate.shape, f32)),
        in_specs=[vm, vm, vm, vm, hbm, hbm], out_specs=(vm, vm, hbm, hbm),
        scratch_shapes=[pltpu.VMEM((NDEV, 1, D), f32), pltpu.VMEM((16, ncol), f32)] + _fanout_sems(2) + _fanout_sems(1) + _fanout_sems(1),
        compiler_params=_cp(),
    )(c_row, cctx_row, wm, bm, wi, gate)


SHARD = INW // NDEV
ROWS_RP = 128


def _overlap(lo, hi, a, b):
    s, e = max(lo, a), min(hi, b)
    return (s, e) if s < e else None


def _repack_w(wg):
    segs = ((0, 1024, PQ), (1024, 1024 + 2 * RANK, PLR), (1024 + 2 * RANK, INW, PZB))

    def body(g_ref, o_ref):
        for j in range(NDEV):
            lo, hi = j * SHARD, (j + 1) * SHARD
            for a, b, pad0 in segs:
                ov = _overlap(lo, hi, a, b)
                if ov:
                    s, e = ov
                    o_ref[:, pad0 + s - a:pad0 + e - a] = g_ref[j, :, s - lo:e - lo]
        o_ref[:, PLR + 2 * RANK:PW] = jnp.zeros((ROWS_RP, PW - PLR - 2 * RANK), bf16)

    return pl.pallas_call(
        body, name="repack_w", grid=(D // ROWS_RP,), out_shape=jax.ShapeDtypeStruct((D, PW), bf16),
        in_specs=[pl.BlockSpec((NDEV, ROWS_RP, SHARD), lambda i: (0, i, 0))],
        out_specs=pl.BlockSpec((ROWS_RP, PW), lambda i: (i, 0)), compiler_params=_cp(("arbitrary",)),
    )(wg)


LATE_END = 1024 + 2 * RANK
LATE_DESTS = 2


def _pack_blocks(o_ref, srcs, dests, dtype):
    for n, j in enumerate(dests):
        lo, hi = j * SHARD, (j + 1) * SHARD
        done = lo
        for a, b, src in srcs:
            ov = _overlap(lo, hi, a, b)
            if ov:
                s, e = ov
                if s > done:
                    o_ref[n, :, done - lo:s - lo] = jnp.zeros((ROWS_RP, s - done), dtype)
                o_ref[n, :, s - lo:e - lo] = src[:, s - a:e - a].astype(dtype)
                done = e
        if done < hi:
            o_ref[n, :, done - lo:hi - lo] = jnp.zeros((ROWS_RP, hi - done), dtype)


class _Pack:
    def __init__(self, extra, outs, fn):
        self.extra, self.outs, self.fn = list(extra), list(outs), fn


def _pack_early():
    def fn(accs, extra, outs):
        zbua, va, zag = accs
        for r0 in range(0, D, ROWS_RP):
            rows = pl.ds(r0, ROWS_RP)
            out, o0 = (outs[0], r0) if r0 < EARLY_A_ROWS else (outs[1], r0 - EARLY_A_ROWS)
            _pack_blocks(out.at[:, pl.ds(o0, ROWS_RP)], ((LATE_END, 2080, zbua.at[rows]), (2080, 2592, va.at[rows]), (2592, INW, zag.at[rows])),
                         range(NDEV), bf16)

    sds = jax.ShapeDtypeStruct
    return _Pack([], [sds((NDEV, EARLY_A_ROWS, SHARD), bf16), sds((NDEV, D - EARLY_A_ROWS, SHARD), bf16)], fn)


def _pack_late(dwk_c, dwv_c, dwl_c):
    def fn(accs, extra, outs):
        qkv_ref, lr_ref = accs
        kc_ref, vc_ref, lc_ref = extra
        for r0 in range(0, D, ROWS_RP):
            rows = pl.ds(r0, ROWS_RP)
            qkv = qkv_ref[rows, :] + jnp.concatenate([jnp.zeros((ROWS_RP, KW), f32), kc_ref[rows, :], vc_ref[rows, :]], axis=1)
            lr = lr_ref[rows, :] + lc_ref[rows, :]
            _pack_blocks(outs[0].at[:, rows], ((0, 1024, qkv), (1024, LATE_END, lr)), range(LATE_DESTS), bf16)

    return _Pack([dwk_c, dwv_c, dwl_c], [jax.ShapeDtypeStruct((LATE_DESTS, D, SHARD), bf16)], fn)


def _mod_tail(dnorm, dbm, dmod, dmodc, cs, wm, m_wm, v_wm, norm_g, m_ng, v_ng, b_mod, m_bm, v_bm, c_ctx, m_cc, v_cc):
    ncol = wm.shape[2]

    def body(dn_ref, dbm_ref, dmod_ref, dmodc_ref, cs_ref, wm_ref, mwm_ref, vwm_ref, ng_ref, mng_ref, vng_ref,
             bm_ref, mbm_ref, vbm_ref, cc_ref, mcc_ref, vcc_ref, *rest):
        o_wm, o_ng, o_bm, o_cc = rest[0:4], rest[4:8], rest[8:12], rest[12:16]
        a_dn, a_dbm, a_dmod, a_dmodc, dm_rows, gc_ref, a_gc = rest[16:23]
        s1, r1, l1, s2, r2, l2 = rest[23:29]
        start, finish = _fanout([lambda j: dn_ref, lambda j: dbm_ref, lambda j: dmod_ref, lambda j: dmodc_ref],
                                [lambda j, r=r: r.at[j] for r in (a_dn, a_dbm, a_dmod, a_dmodc)], s1, r1, l1)
        start()
        finish()

        def total(ref):
            t = ref[0]
            for j in range(1, NDEV):
                t = t + ref[j]
            return t

        _adam_update(total(a_dn), ng_ref, mng_ref, vng_ref, *o_ng)
        _adam_update(total(a_dbm), bm_ref, mbm_ref, vbm_ref, *o_bm)
        dmodc_tot = jnp.concatenate([total(a_dmodc), jnp.zeros((1, D), f32)], axis=1)
        dm_rows[...] = jnp.concatenate([a_dmod[j] for j in range(NDEV)] + [dmodc_tot, jnp.zeros((16 - NDEV - 1, 3 * D), f32)], axis=0)
        dm = dm_rows[:, pl.ds(pl.multiple_of(_me() * ncol, 128), ncol)]
        s, ds = _silu_and_grad(cs_ref[...])
        part = lax.dot_general(dm[8:9, :], wm_ref[0], (((1,), (1,)), ((), ())), preferred_element_type=f32,
                               precision=lax.Precision.HIGHEST)
        gc_ref[...] = part * ds[8:9, :]
        start2, finish2 = _fanout([lambda j: gc_ref], [lambda j: a_gc.at[j]], s2, r2, l2)
        start2()
        g = lax.dot_general(s, dm, (((0,), (0,)), ((), ())), preferred_element_type=f32, precision=lax.Precision.HIGHEST)
        _adam_update(g[None], wm_ref, mwm_ref, vwm_ref, *o_wm)
        finish2()
        _adam_update(total(a_gc), cc_ref, mcc_ref, vcc_ref, *o_cc)

    like = lambda a: [jax.ShapeDtypeStruct(a.shape, f32)] * 4
    row = lambda n: pltpu.VMEM((NDEV, 1, n), f32)
    res = pl.pallas_call(
        body, name="mod_tail", out_shape=tuple(like(wm) + like(norm_g) + like(b_mod) + like(c_ctx)),
        scratch_shapes=[row(D), row(3 * D), row(3 * D), row(2 * D), pltpu.VMEM((16, 3 * D), f32), pltpu.VMEM((1, D), f32), row(D)]
        + _fanout_sems(4) + _fanout_sems(1),
        compiler_params=_cp(),
    )(dnorm, dbm, dmod, dmodc, cs, wm, m_wm, v_wm, norm_g, m_ng, v_ng, b_mod, m_bm, v_bm, c_ctx, m_cc, v_cc)
    return res[0:4], res[4:8], res[8:12], res[12:16]


def _adam_update(g, w_ref, m_ref, v_ref, go_ref, d_ref, mo_ref, vo_ref):
    c1 = 1.0 / (1.0 - ADAM_B1 ** ADAM_STEP)
    c2 = 1.0 / (1.0 - ADAM_B2 ** ADAM_STEP)
    mn = ADAM_B1 * m_ref[...] + (1.0 - ADAM_B1) * g
    vn = ADAM_B2 * v_ref[...] + (1.0 - ADAM_B2) * (g * g)
    go_ref[...] = g
    mo_ref[...] = mn
    vo_ref[...] = vn
    d_ref[...] = -ADAM_LR * ((mn * c1) / (jnp.sqrt(vn * c2) + ADAM_EPS) + ADAM_WD * w_ref[...])


def _adam_w_in(parts_a, parts_b, parts_late, w, m, v):
    na = EARLY_A_ROWS // ROWS_RP

    def body(a_ref, b_ref, l_ref, w_ref, m_ref, v_ref, go_ref, d_ref, mo_ref, vo_ref):
        me = _me()
        first = pl.program_id(0) < na
        early = jnp.where(first, a_ref[0], b_ref[0]).astype(f32)
        for i in range(1, NDEV):
            early = early + jnp.where(first, a_ref[i], b_ref[i]).astype(f32)
        late = l_ref[0].astype(f32)
        for i in range(1, 4):
            late = late + l_ref[i].astype(f32)
        g = jnp.where(me >= LATE_DESTS - 1, early, 0.0) + jnp.where(me < LATE_DESTS, late, 0.0)
        _adam_update(g, w_ref, m_ref, v_ref, go_ref, d_ref, mo_ref, vo_ref)

    blk = pl.BlockSpec((None, ROWS_RP, SHARD), lambda i: (0, i, 0))
    return pl.pallas_call(
        body, name="adam_w_in", grid=(D // ROWS_RP,), out_shape=tuple(jax.ShapeDtypeStruct((1, D, SHARD), f32) for _ in range(4)),
        in_specs=[pl.BlockSpec((NDEV, ROWS_RP, SHARD), lambda i: (0, jnp.minimum(i, na - 1), 0)),
                  pl.BlockSpec((NDEV, ROWS_RP, SHARD), lambda i: (0, jnp.maximum(i - na, 0), 0)),
                  pl.BlockSpec((4, ROWS_RP, SHARD), lambda i: (0, i, 0)), blk, blk, blk],
        out_specs=(blk, blk, blk, blk), compiler_params=_cp(("arbitrary",)),
    )(parts_a, parts_b, parts_late, w, m, v)


def _adam_params(parts, ws, ms, vs, name):
    p = parts.shape[0]
    k = len(ws)
    nrows = [w.size // 128 for w in ws]
    starts = [sum(nrows[:i]) for i in range(k)]

    def shaped(g, shape):
        if len(shape) == 2:
            return jnp.concatenate([g[r:r + 1] for r in range(g.shape[0])], axis=1)
        return g.reshape(shape)

    def body(*refs):
        g_ref = refs[0]
        w_refs, m_refs, v_refs = refs[1:1 + k], refs[1 + k:1 + 2 * k], refs[1 + 2 * k:1 + 3 * k]
        outs = refs[1 + 3 * k:]
        for i in range(k):
            rows = slice(starts[i], starts[i] + nrows[i])
            g = g_ref[0, rows, :]
            for j in range(1, p):
                g = g + g_ref[j, rows, :]
            _adam_update(shaped(g, ws[i].shape), w_refs[i], m_refs[i], v_refs[i], outs[i], outs[k + i], outs[2 * k + i], outs[3 * k + i])

    vm = pl.BlockSpec(memory_space=pltpu.VMEM)
    res = pl.pallas_call(
        body, name=name, out_shape=tuple(jax.ShapeDtypeStruct(w.shape, f32) for _ in range(4) for w in ws),
        in_specs=[vm] * (1 + 3 * k), out_specs=(vm,) * (4 * k), compiler_params=_cp(),
    )(parts, *ws, *ms, *vs)
    return [res[i * k:(i + 1) * k] for i in range(4)]


def _adam_blocks(parts, ws, ms, vs, name):
    k = len(ws)

    def body(*refs):
        g_refs, w_refs, m_refs, v_refs = refs[0:k], refs[k:2 * k], refs[2 * k:3 * k], refs[3 * k:4 * k]
        outs = refs[4 * k:]
        for i in range(k):
            g = g_refs[i][0].astype(f32)
            for j in range(1, parts[i].shape[0]):
                g = g + g_refs[i][j].astype(f32)
            _adam_update(g[None], w_refs[i], m_refs[i], v_refs[i], outs[i], outs[k + i], outs[2 * k + i], outs[3 * k + i])

    vm = pl.BlockSpec(memory_space=pltpu.VMEM)
    res = pl.pallas_call(
        body, name=name, out_shape=tuple(jax.ShapeDtypeStruct(w.shape, f32) for _ in range(4) for w in ws),
        in_specs=[vm] * (4 * k), out_specs=(vm,) * (4 * k), compiler_params=_cp(),
    )(*parts, *ws, *ms, *vs)
    return [res[i * k:(i + 1) * k] for i in range(4)]


def _adam(parts, w, m, v, rows, name):
    p, r, c = parts.shape
    lead = w.ndim - 2

    def body(g_ref, w_ref, m_ref, v_ref, go_ref, d_ref, mo_ref, vo_ref):
        g = g_ref[0].astype(f32)
        for i in range(1, p):
            g = g + g_ref[i].astype(f32)
        _adam_update(g, w_ref, m_ref, v_ref, go_ref, d_ref, mo_ref, vo_ref)

    blk = pl.BlockSpec((None,) * lead + (rows, c), lambda i: (0,) * lead + (i, 0))
    return pl.pallas_call(
        body, name=name, grid=(r // rows,), out_shape=tuple(jax.ShapeDtypeStruct(w.shape, f32) for _ in range(4)),
        in_specs=[pl.BlockSpec((p, rows, c), lambda i: (0, i, 0)), blk, blk, blk], out_specs=(blk, blk, blk, blk),
        compiler_params=_cp(("arbitrary",)),
    )(parts, w, m, v)


def _in_proj(x, g, scale, shift, w_pad, tl, comm):
    l = x.shape[0]
    c_in, c_out, c_sems = comm.specs()

    def body(*refs):
        x_ref, g_ref, sc_ref, sh_ref, w_ref = refs[:5]
        cin = refs[5:5 + len(c_in)]
        p_ref, h_ref = refs[5 + len(c_in):7 + len(c_in)]
        cout = refs[7 + len(c_in):7 + len(c_in) + comm.n]
        sems = refs[7 + len(c_in) + comm.n:]

        def compute():
            xv = x_ref[...]
            r = lax.rsqrt(jnp.mean(xv * xv, axis=-1, keepdims=True) + EPS)
            h = (xv * r) * (g_ref[...] * (1.0 + sc_ref[...])) + sh_ref[...]
            hb = h.astype(bf16)
            h_ref[...] = hb
            p_ref[...] = jnp.dot(hb, w_ref[...], preferred_element_type=f32).astype(bf16)

        comm.run(cin, cout, sems, l // tl, compute)

    vec = pl.BlockSpec((1, D), lambda i: (0, 0))
    return pl.pallas_call(
        body, name="in_proj", grid=(l // tl,),
        out_shape=(jax.ShapeDtypeStruct((l, PW), bf16), jax.ShapeDtypeStruct((l, D), bf16)) + tuple(comm.outs),
        in_specs=[pl.BlockSpec((tl, D), lambda i: (i, 0)), vec, vec, vec, pl.BlockSpec((D, PW), lambda i: (0, 0))] + c_in,
        out_specs=(pl.BlockSpec((tl, PW), lambda i: (i, 0)), pl.BlockSpec((tl, D), lambda i: (i, 0))) + tuple(c_out),
        scratch_shapes=c_sems, compiler_params=_cp(("arbitrary",)),
    )(x, g, scale, shift, w_pad, *comm.ins)


def _tri(rev):
    i = lax.broadcasted_iota(jnp.int32, (CH, CH), 0)
    j = lax.broadcasted_iota(jnp.int32, (CH, CH), 1)
    return jnp.where((j >= i) if rev else (j <= i), 1.0, 0.0).astype(f32)


def _head_masks():
    lane = lax.broadcasted_iota(jnp.int32, (1, KW), 1) // HK
    return [jnp.where(lane == h, 1.0, 0.0).astype(f32) for h in range(NH)]


def _block_diag():
    r = lax.broadcasted_iota(jnp.int32, (VW, KW), 0) // HV
    c = lax.broadcasted_iota(jnp.int32, (VW, KW), 1) // HK
    return jnp.where(r == c, 1.0, 0.0).astype(f32)


def _decay(lr, w2, gb, tri, rev):
    logits = _dot(lr, w2) + gb
    a = _log_sigmoid(logits) * (1.0 / TAU)
    c = _dot_hi(tri, a)
    cl = c[0:1, :] if rev else c[CH - 1:CH, :]
    return logits, c, cl


def _state_fwd(k, v, c, cl, st, bd):
    return st * jnp.exp(cl) + bd * _tn(v, k * jnp.exp(cl - c))


def _state_bwd(k, v, c, cl, st0, dst, trit):
    ecl = jnp.exp(cl)
    edec = jnp.exp(cl - c)
    kdec = k * edec
    dv = _nt(kdec, dst)
    dkdec = _dot(v, dst)
    dcl = jnp.sum(dst * st0, axis=0, keepdims=True) * ecl + jnp.sum(dkdec * kdec, axis=0, keepdims=True)
    da = _dot_hi(trit, -dkdec * kdec) + dcl
    return dkdec * edec, dv, da, dst * ecl


def _bdot(a, b):
    return lax.dot_general(a.astype(bf16), b.astype(bf16), (((2,), (1,)), ((0,), (0,))), preferred_element_type=f32)


def _bnt(a, b):
    return lax.dot_general(a.astype(bf16), b.astype(bf16), (((2,), (2,)), ((0,), (0,))), preferred_element_type=f32)


def _btn(a, b):
    return lax.dot_general(a.astype(bf16), b.astype(bf16), (((1,), (1,)), ((0,), (0,))), preferred_element_type=f32)


def _scan_chunks(x, rev):
    nc = x.shape[0]
    hi = x.astype(bf16)
    r1 = x - hi.astype(f32)
    mid = r1.astype(bf16)
    lo = (r1 - mid.astype(f32)).astype(bf16)
    terms = jnp.concatenate([hi, mid, lo], axis=1)
    tri3 = jnp.broadcast_to(jnp.concatenate([_tri(rev)] * 3, axis=1).astype(bf16)[None], (nc, CH, 3 * CH))
    return lax.dot_general(tri3, terms, (((2,), (1,)), ((0,), (0,))), preferred_element_type=f32)


class _Tile:
    pass


def _tile_prep(q_ref, k_ref, lr_ref, w2_ref, gb_ref, rev, nc):
    t = _Tile()
    tg = nc * CH
    t.logits = _dot(lr_ref[...], w2_ref[...]) + gb_ref[...]
    c = _scan_chunks((_log_sigmoid(t.logits) * (1.0 / TAU)).reshape(nc, CH, KW), rev)
    cl = c[:, 0:1, :] if rev else c[:, CH - 1:CH, :]
    k = k_ref[...].astype(f32).reshape(nc, CH, KW)
    t.ec, t.enc, t.edec, t.ecl = jnp.exp(c), jnp.exp(-c), jnp.exp(cl - c), jnp.exp(cl)
    t.qd = q_ref[...].astype(f32).reshape(nc, CH, KW) * t.ec * QSCALE
    t.kd = k * t.enc
    t.kdec = k * t.edec
    hm = _head_masks()
    t.tri4 = jnp.concatenate([_tri(rev)] * NH, axis=0)[None]
    t.qs = jnp.concatenate([t.qd * hm[h] for h in range(NH)], axis=1)
    t.pst = _bnt(t.qs, t.kd) * t.tri4
    return t


def _gla_fwd(p, w2p, gb, s0, rev, tg, name):
    l = p.shape[0]
    nb, nc = l // tg, tg // CH

    def body(q_ref, k_ref, v_ref, lr_ref, w2_ref, gb_ref, s0_ref, o_ref, st_ref, st):
        @pl.when(pl.program_id(0) == 0)
        def _():
            st[...] = s0_ref[...]

        t = _tile_prep(q_ref, k_ref, lr_ref, w2_ref, gb_ref, rev, nc)
        v = v_ref[...].reshape(nc, CH, VW)
        intra = jnp.concatenate([_bdot(t.pst[:, h * CH:(h + 1) * CH], v[:, :, h * HV:(h + 1) * HV]) for h in range(NH)], axis=2)
        bd = _block_diag()
        s = st[...]
        for n in (range(nc - 1, -1, -1) if rev else range(nc)):
            st_ref[n] = s.astype(bf16)
            s = s * t.ecl[n] + bd * _tn(v[n], t.kdec[n])
        st[...] = s
        o_ref[...] = (_bnt(t.qd, st_ref[...]) + intra).reshape(tg, VW).astype(bf16)

    blk = (lambda i: nb - 1 - i) if rev else (lambda i: i)
    return pl.pallas_call(
        body, name=name, grid=(nb,),
        out_shape=(jax.ShapeDtypeStruct((l, VW), bf16), jax.ShapeDtypeStruct((l // CH, VW, KW), bf16)),
        in_specs=[pl.BlockSpec((tg, KW), lambda i: (blk(i), PQ // KW)), pl.BlockSpec((tg, KW), lambda i: (blk(i), PK // KW)),
                  pl.BlockSpec((tg, VW), lambda i: (blk(i), PV // VW)), pl.BlockSpec((tg, LRW), lambda i: (blk(i), PLR // LRW)),
                  pl.BlockSpec((LRW, KW), lambda i: (0, 0)), pl.BlockSpec((1, KW), lambda i: (0, 0)),
                  pl.BlockSpec((VW, KW), lambda i: (0, 0))],
        out_specs=(pl.BlockSpec((tg, VW), lambda i: (blk(i), 0)), pl.BlockSpec((nc, VW, KW), lambda i: (blk(i), 0, 0))),
        scratch_shapes=[pltpu.VMEM((VW, KW), f32)],
        compiler_params=_cp(("arbitrary",)),
    )(p, p, p, p, w2p, gb, s0)


def _gla_bwd(p, do, states, w2p, gb, prev, rev, tg, name, comm):
    l = p.shape[0]
    nb, nc = l // tg, tg // CH
    out_dt = bf16
    c_in, c_out, c_sems = comm.specs()

    def body(*refs):
        q_ref, k_ref, v_ref, lr_ref, do_ref, st_ref, w2_ref, gb_ref = refs[:8]
        refs = refs[8:]
        if prev is not None:
            pq_ref, pl_ref = refs[:2]
            refs = refs[2:]
        cin, refs = refs[:len(c_in)], refs[len(c_in):]
        dqkv_ref, dlr_ref, dw2_ref, dgb_ref, ds0_ref = refs[:5]
        cout, dst, ds_buf, sems = refs[5:5 + comm.n], refs[5 + comm.n], refs[6 + comm.n], refs[7 + comm.n:]
        comm.run(cin, cout, sems, nb, lambda: compute(q_ref, k_ref, v_ref, lr_ref, do_ref, st_ref, w2_ref, gb_ref,
                                                      pq_ref if prev is not None else None, pl_ref if prev is not None else None,
                                                      dqkv_ref, dlr_ref, dw2_ref, dgb_ref, ds0_ref, dst, ds_buf))

    def compute(q_ref, k_ref, v_ref, lr_ref, do_ref, st_ref, w2_ref, gb_ref, pq_ref, pl_ref,
                dqkv_ref, dlr_ref, dw2_ref, dgb_ref, ds0_ref, dst, ds_buf):
        @pl.when(pl.program_id(0) == 0)
        def _():
            dst[...] = jnp.zeros_like(dst)
            dw2_ref[...] = jnp.zeros_like(dw2_ref)
            dgb_ref[...] = jnp.zeros_like(dgb_ref)

        t = _tile_prep(q_ref, k_ref, lr_ref, w2_ref, gb_ref, rev, nc)
        hm = _head_masks()
        v = v_ref[...].reshape(nc, CH, VW)
        do = do_ref[...].reshape(nc, CH, VW)
        heads = lambda a, h: a[:, :, h * HV:(h + 1) * HV]
        dpst = jnp.concatenate([_bnt(heads(do, h), heads(v, h)) for h in range(NH)], axis=1) * t.tri4
        dv = jnp.concatenate([_btn(t.pst[:, h * CH:(h + 1) * CH], heads(do, h)) for h in range(NH)], axis=2)
        dqd = _bdot(do, st_ref[...])
        for h in range(NH):
            dqd = dqd + hm[h] * _bdot(dpst[:, h * CH:(h + 1) * CH], t.kd)
        dkd = _btn(dpst, t.qs)
        bd = _block_diag()
        d = dst[...]
        for n in (range(nc) if rev else range(nc - 1, -1, -1)):
            ds_buf[n] = d
            d = d * t.ecl[n] + bd * _tn(do[n], t.qd[n])
        dst[...] = d
        ds0_ref[...] = d
        ds = ds_buf[...]
        dv = dv + _bnt(t.kdec, ds)
        dkdec = _bdot(v, ds)
        dcl = jnp.sum(ds * st_ref[...].astype(f32), axis=1, keepdims=True) * t.ecl + jnp.sum(dkdec * t.kdec, axis=1, keepdims=True)
        dc = dqd * t.qd - dkd * t.kd - dkdec * t.kdec
        da = _scan_chunks(dc, not rev) + dcl
        dq = dqd * t.ec * QSCALE
        dk = dkd * t.enc + dkdec * t.edec
        dlog = da.reshape(tg, KW) * _sigmoid(-t.logits) * (1.0 / TAU)
        dlr = _nt(dlog, w2_ref[...])
        dw2_ref[...] += _tn(lr_ref[...], dlog)
        dgb_ref[...] += jnp.sum(dlog, axis=0, keepdims=True)
        dqkv = jnp.concatenate([dq, dk, dv], axis=2).reshape(tg, 2 * KW + VW)
        if prev is not None:
            dqkv = dqkv + pq_ref[...]
            dlr = dlr + pl_ref[...]
        dqkv_ref[...] = dqkv.astype(out_dt)
        dlr_ref[...] = dlr.astype(out_dt)

    blk = (lambda i: i) if rev else (lambda i: nb - 1 - i)
    in_specs = [pl.BlockSpec((tg, KW), lambda i: (blk(i), PQ // KW)), pl.BlockSpec((tg, KW), lambda i: (blk(i), PK // KW)),
                pl.BlockSpec((tg, VW), lambda i: (blk(i), PV // VW)), pl.BlockSpec((tg, LRW), lambda i: (blk(i), PLR // LRW)),
                pl.BlockSpec((tg, VW), lambda i: (blk(i), 0)), pl.BlockSpec((nc, VW, KW), lambda i: (blk(i), 0, 0)),
                pl.BlockSpec((LRW, KW), lambda i: (0, 0)), pl.BlockSpec((1, KW), lambda i: (0, 0))]
    args = [p, p, p, p, do, states, w2p, gb]
    if prev is not None:
        in_specs += [pl.BlockSpec((tg, 2 * KW + VW), lambda i: (blk(i), 0)), pl.BlockSpec((tg, LRW), lambda i: (blk(i), 0))]
        args += list(prev)
    return pl.pallas_call(
        body, name=name, grid=(nb,),
        out_shape=(jax.ShapeDtypeStruct((l, 2 * KW + VW), out_dt), jax.ShapeDtypeStruct((l, LRW), out_dt),
                   jax.ShapeDtypeStruct((LRW, KW), f32), jax.ShapeDtypeStruct((1, KW), f32), jax.ShapeDtypeStruct((VW, KW), f32))
        + tuple(comm.outs),
        in_specs=in_specs + c_in,
        out_specs=(pl.BlockSpec((tg, 2 * KW + VW), lambda i: (blk(i), 0)), pl.BlockSpec((tg, LRW), lambda i: (blk(i), 0)),
                   pl.BlockSpec((LRW, KW), lambda i: (0, 0)), pl.BlockSpec((1, KW), lambda i: (0, 0)),
                   pl.BlockSpec((VW, KW), lambda i: (0, 0))) + tuple(c_out),
        scratch_shapes=[pltpu.VMEM((VW, KW), f32), pltpu.VMEM((nc, VW, KW), f32)] + c_sems,
        compiler_params=_cp(("arbitrary",)),
    )(*args, *comm.ins)


def _ctx_hidden(ctx_ref, g_ref, sc_ref, sh_ref):
    xv = ctx_ref[...]
    r = lax.rsqrt(jnp.mean(xv * xv, axis=-1, keepdims=True) + EPS)
    xn = xv * r
    return xn, xn * (g_ref[...] * (1.0 + sc_ref[...])) + sh_ref[...]


_CTX_W_SPECS = [pl.BlockSpec((D, KW), lambda i: (0, PK // KW)), pl.BlockSpec((D, VW), lambda i: (0, PV // VW)),
                pl.BlockSpec((D, LRW), lambda i: (0, PLR // LRW))]


def _ctx_fwd(ctx, g, scale, shift, w_pad, w2f, w2b, gbf, gbb):
    ncc = CTX // CH

    def body(ctx_ref, g_ref, sc_ref, sh_ref, wk_ref, wv_ref, wl_ref, w2f_ref, w2b_ref, gbf_ref, gbb_ref, sf_ref, sb_ref):
        _, hc = _ctx_hidden(ctx_ref, g_ref, sc_ref, sh_ref)
        k, v, lr = _dot(hc, wk_ref[...]), _dot(hc, wv_ref[...]), _dot(hc, wl_ref[...])
        bd = _block_diag()
        for rev, w2_ref, gb_ref, out in ((False, w2f_ref, gbf_ref, sf_ref), (True, w2b_ref, gbb_ref, sb_ref)):
            tri = _tri(rev)
            st = jnp.zeros((VW, KW), f32)
            for j in (range(ncc - 1, -1, -1) if rev else range(ncc)):
                rows = slice(j * CH, (j + 1) * CH)
                _, c, cl = _decay(lr[rows], w2_ref[...], gb_ref[...], tri, rev)
                st = _state_fwd(k[rows], v[rows], c, cl, st, bd)
            out[...] = st

    vec = pl.BlockSpec((1, D), lambda i: (0, 0))
    w2s = pl.BlockSpec((LRW, KW), lambda i: (0, 0))
    gbs = pl.BlockSpec((1, KW), lambda i: (0, 0))
    sts = pl.BlockSpec((VW, KW), lambda i: (0, 0))
    return pl.pallas_call(
        body, name="ctx_fwd", grid=(1,), out_shape=(jax.ShapeDtypeStruct((VW, KW), f32),) * 2,
        in_specs=[pl.BlockSpec((CTX, D), lambda i: (0, 0)), vec, vec, vec] + _CTX_W_SPECS + [w2s, w2s, gbs, gbs],
        out_specs=(sts, sts), compiler_params=_cp(("arbitrary",)),
    )(ctx, g, scale, shift, w_pad, w_pad, w_pad, w2f, w2b, gbf, gbb)


def _ctx_bwd(ctx, g, scale, shift, w_pad, w2f, w2b, gbf, gbb, dsf, dsb):
    ncc = CTX // CH

    def body(ctx_ref, g_ref, sc_ref, sh_ref, wk_ref, wv_ref, wl_ref, w2f_ref, w2b_ref, gbf_ref, gbb_ref, dsf_ref, dsb_ref,
             dwk_ref, dwv_ref, dwl_ref, dmod_ref, dg_ref, dw2_ref, dgb_ref):
        xn, hc = _ctx_hidden(ctx_ref, g_ref, sc_ref, sh_ref)
        k, v, lr = _dot(hc, wk_ref[...]), _dot(hc, wv_ref[...]), _dot(hc, wl_ref[...])
        bd = _block_diag()
        dk_rows, dv_rows, dl_rows = [None] * ncc, [None] * ncc, [None] * ncc
        for d, (rev, w2_ref, gb_ref, ds_ref) in enumerate(((False, w2f_ref, gbf_ref, dsf_ref), (True, w2b_ref, gbb_ref, dsb_ref))):
            tri = _tri(rev)
            order = list(range(ncc - 1, -1, -1) if rev else range(ncc))
            st, saved = jnp.zeros((VW, KW), f32), {}
            for j in order:
                rows = slice(j * CH, (j + 1) * CH)
                logits, c, cl = _decay(lr[rows], w2_ref[...], gb_ref[...], tri, rev)
                saved[j] = (logits, c, cl, st)
                st = _state_fwd(k[rows], v[rows], c, cl, st, bd)
            dst = ds_ref[...]
            dw2 = jnp.zeros((LRW, KW), f32)
            dgb = jnp.zeros((1, KW), f32)
            for j in reversed(order):
                rows = slice(j * CH, (j + 1) * CH)
                logits, c, cl, st0 = saved[j]
                dk, dv, da, dst = _state_bwd(k[rows], v[rows], c, cl, st0, dst, _tri(not rev))
                dlog = da * _sigmoid(-logits) * (1.0 / TAU)
                dl = _nt(dlog, w2_ref[...])
                dw2 = dw2 + _tn(lr[rows], dlog)
                dgb = dgb + jnp.sum(dlog, axis=0, keepdims=True)
                dk_rows[j] = dk if dk_rows[j] is None else dk_rows[j] + dk
                dv_rows[j] = dv if dv_rows[j] is None else dv_rows[j] + dv
                dl_rows[j] = dl if dl_rows[j] is None else dl_rows[j] + dl
            dw2_ref[d] = dw2
            dgb_ref[d] = dgb
        dk, dv, dl = (jnp.concatenate(t, axis=0) for t in (dk_rows, dv_rows, dl_rows))
        dwk_ref[...] = _tn(hc, dk)
        dwv_ref[...] = _tn(hc, dv)
        dwl_ref[...] = _tn(hc, dl)
        dh = _nt(dk, wk_ref[...]) + _nt(dv, wv_ref[...]) + _nt(dl, wl_ref[...])
        gx = dh * xn
        dmod_ref[:, 0:D] = jnp.sum(dh, axis=0, keepdims=True)
        dmod_ref[:, D:2 * D] = jnp.sum(gx, axis=0, keepdims=True) * g_ref[...]
        dg_ref[...] = jnp.sum(gx, axis=0, keepdims=True) * (1.0 + sc_ref[...])

    vec = pl.BlockSpec((1, D), lambda i: (0, 0))
    w2s = pl.BlockSpec((LRW, KW), lambda i: (0, 0))
    gbs = pl.BlockSpec((1, KW), lambda i: (0, 0))
    sts = pl.BlockSpec((VW, KW), lambda i: (0, 0))
    full = lambda *s: pl.BlockSpec(s, lambda i: (0,) * len(s))
    return pl.pallas_call(
        body, name="ctx_bwd", grid=(1,),
        out_shape=(jax.ShapeDtypeStruct((D, KW), f32), jax.ShapeDtypeStruct((D, VW), f32), jax.ShapeDtypeStruct((D, LRW), f32),
                   jax.ShapeDtypeStruct((1, 2 * D), f32), jax.ShapeDtypeStruct((1, D), f32),
                   jax.ShapeDtypeStruct((2, LRW, KW), f32), jax.ShapeDtypeStruct((2, 1, KW), f32)),
        in_specs=[pl.BlockSpec((CTX, D), lambda i: (0, 0)), vec, vec, vec] + _CTX_W_SPECS + [w2s, w2s, gbs, gbs, sts, sts],
        out_specs=(full(D, KW), full(D, VW), full(D, LRW), full(1, 2 * D), full(1, D), full(2, LRW, KW), full(2, 1, KW)),
        compiler_params=_cp(("arbitrary",)),
    )(ctx, g, scale, shift, w_pad, w_pad, w_pad, w2f, w2b, gbf, gbb, dsf, dsb)


def _layernorm(va, g, b):
    mu = jnp.mean(va, axis=-1, keepdims=True)
    xc = va - mu
    rstd = lax.rsqrt(jnp.mean(xc * xc, axis=-1, keepdims=True) + EPS)
    vhat = xc * rstd
    return vhat, rstd, vhat * g + b


MIX_PIECES = 8


def _mix_fwd(p, ln_g, ln_b, ws, bs_t):
    l = p.shape[0]
    half = AW // 2
    rp = l // MIX_PIECES
    cpp = rp // ACH

    def body(p_ref, g_ref, b_ref, ws_ref, bs_ref, sv_hbm, va_buf, col_buf, sv_buf, in_sems, out_sems):
        piece = lambda i: pl.ds(pl.multiple_of(i * rp, rp), rp)
        load = lambda i: pltpu.make_async_copy(p_ref.at[piece(i), pl.ds(PVA, AW)], va_buf.at[piece(i)], in_sems.at[i])
        store = lambda i: pltpu.make_async_copy(sv_buf.at[piece(i)], sv_hbm.at[piece(i)], out_sems.at[i])
        for i in range(MIX_PIECES):
            load(i).start()

        def rows_piece(i, carry):
            load(i).wait()
            for j in range(cpp):
                rows = pl.ds(pl.multiple_of(i * rp + j * ACH, ACH), ACH)
                _, _, vn = _layernorm(va_buf[rows, :].astype(f32), g_ref[...], b_ref[...])
                for gi in range(2):
                    sl = slice(gi * ACH, (gi + 1) * ACH)
                    sv_buf[rows, sl] = (_dot(ws_ref[gi], vn[:, sl]) + bs_ref[:, gi:gi + 1]).astype(bf16)
                col_buf[0, rows, :] = vn[:, half:half + ACH]
                col_buf[1, rows, :] = vn[:, half + ACH:]
            return carry

        lax.fori_loop(0, MIX_PIECES, rows_piece, 0)

        def cols_step(cidx, carry):
            rows = pl.ds(cidx, ACH, stride=GW)
            for gi in range(2, 4):
                col_buf[gi - 2, rows, :] = _dot(ws_ref[gi], col_buf[gi - 2, rows, :]) + bs_ref[:, gi:gi + 1]
            return carry

        lax.fori_loop(0, GW, cols_step, 0, unroll=8)

        def out_piece(i, carry):
            for j in range(cpp):
                rows = pl.ds(pl.multiple_of(i * rp + j * ACH, ACH), ACH)
                sv_buf[rows, half:half + ACH] = col_buf[0, rows, :].astype(bf16)
                sv_buf[rows, half + ACH:] = col_buf[1, rows, :].astype(bf16)
            store(i).start()
            return carry

        lax.fori_loop(0, MIX_PIECES, out_piece, 0)
        for i in range(MIX_PIECES):
            store(i).wait()

    vm = pl.BlockSpec(memory_space=pltpu.VMEM)
    hbm = pl.BlockSpec(memory_space=pl.ANY)
    return pl.pallas_call(
        body, name="mix_fwd", out_shape=jax.ShapeDtypeStruct((l, AW), bf16),
        in_specs=[hbm, vm, vm, vm, vm], out_specs=hbm,
        scratch_shapes=[pltpu.VMEM((l, AW), bf16), pltpu.VMEM((2, l, ACH), f32), pltpu.VMEM((l, AW), bf16),
                        pltpu.SemaphoreType.DMA((MIX_PIECES,)), pltpu.SemaphoreType.DMA((MIX_PIECES,))],
        compiler_params=_cp(),
    )(p, ln_g, ln_b, ws, bs_t)


def _mix_bwd(p, dsv, ln_g, ln_b, ws_t):
    l = p.shape[0]
    half = AW // 2
    rp = l // MIX_PIECES
    cpp = rp // ACH

    def body(p_ref, dsv_hbm, g_ref, b_ref, wst_ref, dva_hbm, dws_ref, dbs_ref, dg_ref, db_ref,
             va_buf, dsv_buf, vn_col, ds_col, dva_buf, va_sems, ds_sems, out_sems):
        piece = lambda i: pl.ds(pl.multiple_of(i * rp, rp), rp)
        load_va = lambda i: pltpu.make_async_copy(p_ref.at[piece(i), pl.ds(PVA, AW)], va_buf.at[piece(i)], va_sems.at[i])
        load_ds = lambda i: pltpu.make_async_copy(dsv_hbm.at[piece(i)], dsv_buf.at[piece(i)], ds_sems.at[i])
        store = lambda i: pltpu.make_async_copy(dva_buf.at[piece(i)], dva_hbm.at[piece(i)], out_sems.at[i])
        for i in range(MIX_PIECES):
            load_va(i).start()
            load_ds(i).start()
        dws_ref[...] = jnp.zeros_like(dws_ref)
        dbs_ref[...] = jnp.zeros_like(dbs_ref)
        dg_ref[...] = jnp.zeros_like(dg_ref)
        db_ref[...] = jnp.zeros_like(db_ref)

        def rows_piece(i, carry):
            load_va(i).wait()
            load_ds(i).wait()
            for j in range(cpp):
                rows = pl.ds(pl.multiple_of(i * rp + j * ACH, ACH), ACH)
                _, _, vn = _layernorm(va_buf[rows, :].astype(f32), g_ref[...], b_ref[...])
                ds = dsv_buf[rows, :].astype(f32)
                for gi in range(2):
                    sl = slice(gi * ACH, (gi + 1) * ACH)
                    dws_ref[gi] += _nt(ds[:, sl], vn[:, sl])
                    dbs_ref[gi] += ds[:, sl]
                for gi in range(2):
                    sl = slice(half + gi * ACH, half + (gi + 1) * ACH)
                    vn_col[gi, rows, :] = vn[:, sl]
                    ds_col[gi, rows, :] = ds[:, sl]
            return carry

        lax.fori_loop(0, MIX_PIECES, rows_piece, 0)

        def cols_step(cidx, carry):
            rows = pl.ds(cidx, ACH, stride=GW)
            for gi in range(2, 4):
                ds = ds_col[gi - 2, rows, :]
                dws_ref[gi] += _nt(ds, vn_col[gi - 2, rows, :])
                dbs_ref[gi] += ds
                ds_col[gi - 2, rows, :] = _dot(wst_ref[gi], ds)
            return carry

        lax.fori_loop(0, GW, cols_step, 0, unroll=8)

        def out_piece(i, carry):
            for j in range(cpp):
                rows = pl.ds(pl.multiple_of(i * rp + j * ACH, ACH), ACH)
                vhat, rstd, _ = _layernorm(va_buf[rows, :].astype(f32), g_ref[...], b_ref[...])
                ds = dsv_buf[rows, :].astype(f32)
                dvn = jnp.concatenate([_dot(wst_ref[0], ds[:, 0:ACH]), _dot(wst_ref[1], ds[:, ACH:half]),
                                       ds_col[0, rows, :], ds_col[1, rows, :]], axis=1)
                dg_ref[...] += jnp.sum(dvn * vhat, axis=0, keepdims=True)
                db_ref[...] += jnp.sum(dvn, axis=0, keepdims=True)
                dvh = dvn * g_ref[...]
                dva = rstd * (dvh - jnp.mean(dvh, axis=-1, keepdims=True) - vhat * jnp.mean(dvh * vhat, axis=-1, keepdims=True))
                dva_buf[rows, :] = dva.astype(bf16)
            store(i).start()
            return carry

        lax.fori_loop(0, MIX_PIECES, out_piece, 0)
        for i in range(MIX_PIECES):
            store(i).wait()

    vm = pl.BlockSpec(memory_space=pltpu.VMEM)
    hbm = pl.BlockSpec(memory_space=pl.ANY)
    dma = lambda: pltpu.SemaphoreType.DMA((MIX_PIECES,))
    return pl.pallas_call(
        body, name="mix_bwd",
        out_shape=(jax.ShapeDtypeStruct((l, AW), bf16), jax.ShapeDtypeStruct((4, ACH, ACH), f32), jax.ShapeDtypeStruct((4, ACH, ACH), f32),
                   jax.ShapeDtypeStruct((1, AW), f32), jax.ShapeDtypeStruct((1, AW), f32)),
        in_specs=[hbm, hbm, vm, vm, vm], out_specs=(hbm, vm, vm, vm, vm),
        scratch_shapes=[pltpu.VMEM((l, AW), bf16), pltpu.VMEM((l, AW), bf16), pltpu.VMEM((2, l, ACH), f32), pltpu.VMEM((2, l, ACH), f32),
                        pltpu.VMEM((l, AW), bf16), dma(), dma(), dma()],
        compiler_params=_cp(),
    )(p, dsv, ln_g, ln_b, ws_t)


def _mid(x, tgt, p, o_f, o_b, sv, gate, gf, gb_norm, w_pa, w_pb, w_out, tl):
    l = x.shape[0]

    def body(x_ref, t_ref, zb_ref, ua_ref, za_ref, g1_ref, g2_ref, of_ref, ob_ref, sv_ref, gate_ref, gf_ref, gbn_ref,
             wpa_ref, wpb_ref, wout_ref,
             dx1_ref, dzbua_ref, dzag_ref, dsv_ref, do_ref, dwout_bf, dwpa_bf, dwpb_bf, dgf_ref, dgate_ref, dgbn_ref, loss_ref,
             dwout_ref, dwpa_ref, dwpb_ref):
        @pl.when(pl.program_id(0) == 0)
        def _():
            for r in (dwout_ref, dwpa_ref, dwpb_ref, dgf_ref, dgate_ref, dgbn_ref, loss_ref):
                r[...] = jnp.zeros_like(r)

        o = of_ref[...].astype(f32) + ob_ref[...].astype(f32)
        rr = jnp.concatenate(
            [jnp.broadcast_to(lax.rsqrt(jnp.mean(o[:, h * HV:(h + 1) * HV] ** 2, axis=-1, keepdims=True) + EPS), (tl, HV))
             for h in range(NH)], axis=1)
        ohat = o * rr
        on = ohat * gbn_ref[...]
        szb, dszb = _silu_and_grad(zb_ref[...].astype(f32))
        tb = on * szb
        u = ua_ref[...].astype(f32)
        svv = sv_ref[...].astype(f32)
        sza, dsza = _silu_and_grad(za_ref[...].astype(f32))
        ta = u * svv * sza
        ya = _dot(ta, wpa_ref[...])
        yb = _dot(tb, wpb_ref[...])
        g1 = _sigmoid(g1_ref[...].astype(f32))
        g2 = _sigmoid(g2_ref[...].astype(f32))
        m = g1 * ya + g2 * yb
        y2 = _dot(m, wout_ref[...])
        x1 = x_ref[...] + gate_ref[...] * y2
        r1 = lax.rsqrt(jnp.mean(x1 * x1, axis=-1, keepdims=True) + EPS)
        x1n = x1 * r1
        err = x1n * gf_ref[...] - t_ref[...]
        loss_ref[...] += jnp.sum(jnp.sum(err * err, axis=-1, keepdims=True), axis=0, keepdims=True) * (0.5 / D)
        dout = err * (1.0 / D)
        dgf_ref[...] += jnp.sum(dout * x1n, axis=0, keepdims=True)
        dx1n = dout * gf_ref[...]
        dx1 = r1 * (dx1n - x1n * jnp.mean(dx1n * x1n, axis=-1, keepdims=True))
        dx1_ref[...] = dx1
        dgate_ref[...] += jnp.sum(dx1 * y2, axis=0, keepdims=True)
        dy2 = dx1 * gate_ref[...]
        dwout_ref[...] += _tn(m, dy2)
        dm = _nt(dy2, wout_ref[...])
        dya = dm * g1
        dyb = dm * g2
        dzag_ref[:, AW:AW + D] = (dm * ya * g1 * (1.0 - g1)).astype(bf16)
        dzag_ref[:, AW + D:] = (dm * yb * g2 * (1.0 - g2)).astype(bf16)
        dwpa_ref[...] += _tn(ta, dya)
        dta = _nt(dya, wpa_ref[...])
        dzbua_ref[:, AW:] = (dta * svv * sza).astype(bf16)
        dsv_ref[...] = (dta * u * sza).astype(bf16)
        dzag_ref[:, 0:AW] = (dta * u * svv * dsza).astype(bf16)
        dwpb_ref[...] += _tn(tb, dyb)
        dtb = _nt(dyb, wpb_ref[...])
        don = dtb * szb
        dzbua_ref[:, 0:AW] = (dtb * on * dszb).astype(bf16)
        dgbn_ref[...] += jnp.sum(don * ohat, axis=0, keepdims=True)
        doh = don * gbn_ref[...]
        prod = doh * ohat
        mh = jnp.concatenate(
            [jnp.broadcast_to(jnp.mean(prod[:, h * HV:(h + 1) * HV], axis=-1, keepdims=True), (tl, HV)) for h in range(NH)], axis=1)
        do_ref[...] = (rr * (doh - ohat * mh)).astype(bf16)

        @pl.when(pl.program_id(0) == l // tl - 1)
        def _():
            for acc, out in ((dwout_ref, dwout_bf), (dwpa_ref, dwpa_bf), (dwpb_ref, dwpb_bf)):
                out[...] = acc[...].astype(bf16)

    row = lambda w, j: pl.BlockSpec((tl, w), lambda i, j=j: (i, j))
    full = lambda *s: pl.BlockSpec(s, lambda i: (0,) * len(s))
    return pl.pallas_call(
        body, name="mid", grid=(l // tl,),
        out_shape=(jax.ShapeDtypeStruct((l, D), f32), jax.ShapeDtypeStruct((l, 2 * AW), bf16), jax.ShapeDtypeStruct((l, AW + 2 * D), bf16),
                   jax.ShapeDtypeStruct((l, AW), bf16), jax.ShapeDtypeStruct((l, VW), bf16),
                   jax.ShapeDtypeStruct((D, D), bf16), jax.ShapeDtypeStruct((AW, D), bf16), jax.ShapeDtypeStruct((VW, D), bf16),
                   jax.ShapeDtypeStruct((1, D), f32), jax.ShapeDtypeStruct((1, D), f32), jax.ShapeDtypeStruct((1, VW), f32),
                   jax.ShapeDtypeStruct((1, 1), f32)),
        scratch_shapes=[pltpu.VMEM((D, D), f32), pltpu.VMEM((AW, D), f32), pltpu.VMEM((VW, D), f32)],
        in_specs=[row(D, 0), row(D, 0), row(AW, PZB // AW), row(AW, PUA // AW), row(AW, PZA // AW), row(D, PG1 // D), row(D, PG2 // D),
                  row(VW, 0), row(VW, 0), row(AW, 0), full(1, D), full(1, D), full(1, VW), full(AW, D), full(VW, D), full(D, D)],
        out_specs=(row(D, 0), row(2 * AW, 0), row(AW + 2 * D, 0), row(AW, 0), row(VW, 0),
                   full(D, D), full(AW, D), full(VW, D), full(1, D), full(1, D), full(1, VW), full(1, 1)),
        compiler_params=_cp(("arbitrary",)),
    )(x, tgt, p, p, p, p, p, o_f, o_b, sv, gate, gf, gb_norm, w_pa, w_pb, w_out)


def _in_bwd(x, dx1, dqkv, dzbua, dva, dzag, dlr, w_pad, g, scale, tl, comm):
    l = x.shape[0]
    c_in, c_out, c_sems = comm.specs()

    def body(*refs):
        cin = refs[14:14 + len(c_in)]
        outs = refs[14 + len(c_in):]
        comm.run(cin, outs[4:4 + comm.n], outs[4 + comm.n:], l // tl, lambda: compute(*refs[:14], *outs[:4]))

    def compute(x_ref, dx1_ref, a_ref, b_ref, c_ref, e_ref, lr_ref, wa_ref, wb_ref, wc_ref, we_ref, wl_ref, g_ref, sc_ref,
                gx_ref, dsh_ref, dsc_ref, dg_ref):
        @pl.when(pl.program_id(0) == 0)
        def _():
            for r in (dsh_ref, dsc_ref, dg_ref):
                r[...] = jnp.zeros_like(r)

        dh = (_nt(a_ref[...], wa_ref[...]) + _nt(b_ref[...], wb_ref[...]) + _nt(c_ref[...], wc_ref[...]) + _nt(e_ref[...], we_ref[...])
              + _nt(lr_ref[...], wl_ref[...]))
        xv = x_ref[...]
        r = lax.rsqrt(jnp.mean(xv * xv, axis=-1, keepdims=True) + EPS)
        xn = xv * r
        gxn = jnp.sum(dh * xn, axis=0, keepdims=True)
        dsh_ref[...] += jnp.sum(dh, axis=0, keepdims=True)
        dsc_ref[...] += gxn * g_ref[...]
        dg_ref[...] += gxn * (1.0 + sc_ref[...])
        dxn = dh * (g_ref[...] * (1.0 + sc_ref[...]))
        gx_ref[...] = dx1_ref[...] + r * (dxn - xn * jnp.mean(dxn * xn, axis=-1, keepdims=True))

    row = lambda w: pl.BlockSpec((tl, w), lambda i: (i, 0))
    wcol = lambda w, j: pl.BlockSpec((D, w), lambda i, j=j: (0, j))
    vec = pl.BlockSpec((1, D), lambda i: (0, 0))
    return pl.pallas_call(
        body, name="in_bwd", grid=(l // tl,),
        out_shape=(jax.ShapeDtypeStruct((l, D), f32),) + (jax.ShapeDtypeStruct((1, D), f32),) * 3 + tuple(comm.outs),
        in_specs=[row(D), row(D), row(2 * KW + VW), row(2 * AW), row(AW), row(AW + 2 * D), row(LRW),
                  wcol(2 * KW + VW, 0), wcol(2 * AW, PZB // (2 * AW)), wcol(AW, PVA // AW), wcol(AW + 2 * D, PZA // (AW + 2 * D)),
                  wcol(LRW, PLR // LRW), vec, vec] + c_in,
        out_specs=(row(D), vec, vec, vec) + tuple(c_out), scratch_shapes=c_sems,
        compiler_params=_cp(("arbitrary",)),
    )(x, dx1, dqkv, dzbua, dva, dzag, dlr, w_pad, w_pad, w_pad, w_pad, w_pad, g, scale, *comm.ins)


def _tn_matmul(a, bs, tl, name, comm=None, pack=None):
    l, m = a.shape
    k = len(bs)
    comm = comm or _Comm([], [], None)
    c_in, c_out, c_sems = comm.specs()
    acc_shapes = [(m, b.shape[1]) for b in bs]
    n_extra = len(pack.extra) if pack else 0
    n_res = len(pack.outs) if pack else k

    def body(a_ref, *refs):
        b_refs, refs = refs[:k], refs[k:]
        extra, refs = refs[:n_extra], refs[n_extra:]
        cin, refs = refs[:len(c_in)], refs[len(c_in):]
        res, refs = refs[:n_res], refs[n_res:]
        cout, refs = refs[:comm.n], refs[comm.n:]
        accs, sems = (refs[:k], refs[k:]) if pack else (res, refs)

        def compute():
            @pl.when(pl.program_id(0) == 0)
            def _():
                for o_ref in accs:
                    o_ref[...] = jnp.zeros_like(o_ref)

            av = a_ref[...]
            for b_ref, o_ref in zip(b_refs, accs):
                o_ref[...] += _tn(av, b_ref[...])
            if pack:
                pl.when(pl.program_id(0) == l // tl - 1)(lambda: pack.fn(accs, extra, res))

        comm.run(cin, cout, sems, l // tl, compute)

    full = lambda shape: pl.BlockSpec(shape, lambda i: (0,) * len(shape))
    res_shapes = list(pack.outs) if pack else [jax.ShapeDtypeStruct(sh, f32) for sh in acc_shapes]
    return pl.pallas_call(
        body, name=name, grid=(l // tl,), out_shape=tuple(res_shapes) + tuple(comm.outs),
        in_specs=[pl.BlockSpec((tl, m), lambda i: (i, 0))] + [pl.BlockSpec((tl, b.shape[1]), lambda i: (i, 0)) for b in bs]
        + [full(e.shape) for e in (pack.extra if pack else [])] + c_in,
        out_specs=tuple(full(r.shape) for r in res_shapes) + tuple(c_out),
        scratch_shapes=([pltpu.VMEM(sh, f32) for sh in acc_shapes] if pack else []) + c_sems, compiler_params=_cp(("arbitrary",)),
    )(a, *bs, *(pack.extra if pack else []), *comm.ins)


def _pad_gate(w2, gb):
    z = jnp.zeros((RANK, KW), f32)
    tail = jnp.zeros((LRW - 2 * RANK, KW), f32)
    w2f = jnp.concatenate([w2[0], z, tail], axis=0)
    w2b = jnp.concatenate([z, w2[1], tail], axis=0)
    return w2f, w2b, gb[0:1], gb[1:2]


EARLY_A_ROWS = 512


class _NoExchange:
    def __init__(self, w_pa, w_pb, w_out):
        self.weights = (w_pa, w_pb, w_out)

    def gather_proj(self):
        return _Comm([], [], None)

    def proj_weights(self, got):
        return self.weights

    def first(self, dw_out, dw_pa, dw_pb, small):
        return _Comm([], [], None)

    def early_a(self, blocks):
        return _Comm([], [], None)

    def early_b(self, blocks):
        return _Comm([], [], None)

    def late(self, blocks, dgt):
        return _Comm([], [], None)


class _Exchanges:
    def __init__(self, pa, pb, wo):
        self.shards = (pa, pb, wo)

    def gather_proj(self):
        def plan(i, o):
            srcs = [lambda j, r=r: r for r in i]
            dsts = [lambda j: o[0].at[:, _lanes(j)], lambda j: o[1].at[:, _lanes(j)], lambda j: o[2].at[j]]
            return srcs, dsts, None
        sds = jax.ShapeDtypeStruct
        return _Comm(self.shards, [sds((AW, D), bf16), sds((VW, D), bf16), sds((NDEV, 128, D), bf16)], plan)

    def proj_weights(self, got):
        return got[0], got[1], got[2].reshape(D, D)

    def first(self, dw_out, dw_pa, dw_pb, small):
        def plan(i, o):
            srcs = [lambda j: i[0].at[j], lambda j: i[1].at[:, _lanes(j)], lambda j: i[2].at[:, _lanes(j)], lambda j: i[3]]
            dsts = [lambda j, r=r: r.at[j] for r in o]
            return srcs, dsts, None
        sds = jax.ShapeDtypeStruct
        return _Comm([dw_out.reshape(NDEV, 128, D), dw_pa, dw_pb, small],
                     [sds((NDEV, 128, D), bf16), sds((NDEV, AW, 128), bf16), sds((NDEV, VW, 128), bf16),
                      sds((NDEV,) + small.shape, f32)], plan)

    def early_a(self, blocks):
        def plan(i, o):
            return [lambda j: i[0].at[j]], [lambda j: o[0].at[j]], [lambda j: j >= LATE_DESTS - 1]
        return _Comm([blocks], [jax.ShapeDtypeStruct(blocks.shape, bf16)], plan)

    def early_b(self, blocks):
        def plan(i, o):
            return [lambda j: i[0].at[j]], [lambda j: o[0].at[j]], [lambda j: j >= LATE_DESTS - 1]
        return _Comm([blocks], [jax.ShapeDtypeStruct(blocks.shape, bf16)], plan)

    def late(self, blocks, dgt):
        return _LateComm(blocks, dgt)


def _local_step(x, ctx, tgt, mod, modc, norm_g, w_pad, ln_g, ln_b, ws, bs, w2, gb, gb_norm, gf, xch):
    shift, scale, gate = mod[:, 0:D], mod[:, D:2 * D], mod[:, 2 * D:]
    shift_c, scale_c = modc[:, 0:D], modc[:, D:]
    w2f, w2b, gbf, gbb = _pad_gate(w2, gb)

    p, h, *got_proj = _in_proj(x, norm_g, scale, shift, w_pad, 512, xch.gather_proj())
    w_pa, w_pb, w_out = xch.proj_weights(got_proj)
    sc_f, sc_b = _ctx_fwd(ctx, norm_g, scale_c, shift_c, w_pad, w2f, w2b, gbf, gbb)
    o_f, st_f = _gla_fwd(p, w2f, gbf, sc_f, False, 512, "gla_fwd_f")
    o_b, st_b = _gla_fwd(p, w2b, gbb, sc_b, True, 512, "gla_fwd_b")
    sv = _mix_fwd(p, ln_g, ln_b, ws.astype(bf16), bs.T)
    (dx1, dzbua, dzag, dsv, do, dw_out, dw_pa, dw_pb, dgf, dgate, dgbn, loss) = _mid(
        x, tgt, p, o_f, o_b, sv, gate, gf, gb_norm, w_pa, w_pb, w_out, 256)
    dva, dws, dbs_acc, dln_g, dln_b = _mix_bwd(p, dsv, ln_g, ln_b, jnp.swapaxes(ws, 1, 2).astype(bf16))
    small = _rows128(dln_g, dln_b, dws, jnp.sum(dbs_acc, axis=-1), dgbn, dgf, jnp.broadcast_to(loss, (1, 128)))
    blocks_a, blocks_b, *got_first = _tn_matmul(h, [dzbua, dva, dzag], 1024, "dw_early", xch.first(dw_out, dw_pa, dw_pb, small),
                                               _pack_early())

    dqkv_f, dlr_f, dw2f, dgbf, dsc_f, *got_a = _gla_bwd(p, do, st_f, w2f, gbf, None, False, 512, "gla_bwd_f",
                                                        xch.early_a(blocks_a))
    dqkv, dlr, dw2b, dgbb, dsc_b, *got_b = _gla_bwd(p, do, st_b, w2b, gbb, (dqkv_f, dlr_f), True, 512, "gla_bwd_b",
                                                    xch.early_b(blocks_b))
    dwk_c, dwv_c, dwl_c, dmodc, dg_c, dw2c, dgbc = _ctx_bwd(ctx, norm_g, scale_c, shift_c, w_pad, w2f, w2b, gbf, gbb, dsc_f, dsc_b)
    (blocks_late,) = _tn_matmul(h, [dqkv, dlr], 1024, "dw_qkv_lr", pack=_pack_late(dwk_c, dwv_c, dwl_c))
    dw2 = jnp.stack([dw2f[0:RANK] + dw2c[0, 0:RANK], dw2b[RANK:2 * RANK] + dw2c[1, RANK:2 * RANK]])
    dgb = jnp.concatenate([dgbf + dgbc[0], dgbb + dgbc[1]], axis=0)
    dgt = jnp.concatenate([jnp.transpose(dw2.reshape(2, RANK, NDEV, 32), (2, 0, 1, 3)).reshape(NDEV, 2 * RANK * 32),
                           jnp.transpose(dgb.reshape(2, NDEV, 32), (1, 0, 2)).reshape(NDEV, 64),
                           jnp.zeros((NDEV, 64), f32)], axis=1).reshape(NDEV, 9, 128)
    gx, dshift, dscale, dg, *got_late = _in_bwd(x, dx1, dqkv, dzbua, dva, dzag, dlr, w_pad, norm_g, scale, 512,
                                                xch.late(blocks_late, dgt))
    return dict(loss=loss, gx=gx, dmod=jnp.concatenate([dshift, dscale, dgate], axis=1), dmodc=dmodc, dnorm_g=dg + dg_c,
                small=small, blocks_a=blocks_a, blocks_b=blocks_b, blocks_late=blocks_late, dw2=dw2, dgb=dgb,
                dw_pa=dw_pa, dw_pb=dw_pb, dw_out=dw_out, got_first=got_first, got_a=got_a, got_b=got_b, got_late=got_late)


def _rows128(*vs):
    out = []
    for t in vs:
        t = t.reshape(-1)
        pad = (-t.shape[0]) % 128
        out.append(jnp.pad(t, (0, pad)) if pad else t)
    return jnp.concatenate(out).reshape(-1, 128)


def kernel(x, c, ctx, c_ctx, w_mod, b_mod, norm_g, w_in, a_ln_g, a_ln_b, a_ws, a_bs, b_gate_w2, b_gate_b, b_norm_g, w_proj_a, w_proj_b, w_out, final_norm_g, loss_target, m_c_ctx, m_w_mod, m_b_mod, m_norm_g, m_w_in, m_a_ln_g, m_a_ln_b, m_a_ws, m_a_bs, m_b_gate_w2, m_b_gate_b, m_b_norm_g, m_w_proj_a, m_w_proj_b, m_w_out, m_final_norm_g, v_c_ctx, v_w_mod, v_b_mod, v_norm_g, v_w_in, v_a_ln_g, v_a_ln_b, v_a_ws, v_a_bs, v_b_gate_w2, v_b_gate_b, v_b_norm_g, v_w_proj_a, v_w_proj_b, v_w_out, v_final_norm_g):
    me = _me()
    ncol = w_mod.shape[2]

    gate_mine = _rows128(jnp.concatenate([b_gate_w2.reshape(-1), b_gate_b.reshape(-1)]))
    bm_mine = lax.dynamic_slice(b_mod, (0, me * ncol), (1, ncol))
    cs, mods, wg, gates = _gather_first(c, c_ctx.reshape(1, D), w_mod[0], bm_mine, w_in[0].astype(bf16), gate_mine)
    w_pad = _repack_w(wg)
    gflat = gates.reshape(NDEV, 9 * 128)
    w2 = jnp.transpose(gflat[:, 0:2 * RANK * 32].reshape(NDEV, 2, RANK, 32), (1, 2, 0, 3)).reshape(2, RANK, KW)
    gb = jnp.transpose(gflat[:, 2 * RANK * 32:2 * RANK * 32 + 64].reshape(NDEV, 2, 32), (1, 0, 2)).reshape(2, KW)

    mods = jnp.transpose(mods, (1, 0, 2)).reshape(16, 3 * D)
    mod = lax.dynamic_slice(mods, (me, 0), (1, 3 * D))
    modc = mods[8:9, 0:2 * D]

    xch = _Exchanges(w_proj_a[0].astype(bf16), w_proj_b[0].astype(bf16), w_out[0].astype(bf16))
    r = _local_step(x[0], ctx[0], loss_target[0], mod, modc, norm_g, w_pad, a_ln_g, a_ln_b, a_ws[0], a_bs[0], w2, gb,
                    b_norm_g, final_norm_g.reshape(1, D), xch)
    p_out, p_pa, p_pb, smalls_e = r["got_first"]
    (p_in_a,) = r["got_a"]
    (p_in_b,) = r["got_b"]
    _, _, p_in_late, p_gt = r["got_late"]

    n_e = (AW + AW + 4 * ACH * ACH + AW + VW + D) // 128
    row = lambda t: t.reshape(1, D)
    rep_e = _adam_params(smalls_e, [a_ln_g, a_ln_b, a_ws, a_bs, b_norm_g, row(final_norm_g)],
                         [m_a_ln_g, m_a_ln_b, m_a_ws, m_a_bs, m_b_norm_g, row(m_final_norm_g)],
                         [v_a_ln_g, v_a_ln_b, v_a_ws, v_a_bs, v_b_norm_g, row(v_final_norm_g)], "adam_rep_early")
    rep_e = [t[0:5] + (t[5].reshape(D),) for t in rep_e]
    losses = smalls_e[:, n_e, 0]
    loss = losses[0]
    for i in range(1, NDEV):
        loss = loss + losses[i]

    dbm = r["dmod"] + jnp.concatenate([r["dmodc"], jnp.zeros((1, D), f32)], axis=1)
    wm, ng, bmod_r, cc = _mod_tail(r["dnorm_g"], dbm, r["dmod"], r["dmodc"], cs, w_mod, m_w_mod, v_w_mod,
                                   norm_g, m_norm_g, v_norm_g, b_mod, m_b_mod, v_b_mod,
                                   c_ctx.reshape(1, D), m_c_ctx.reshape(1, D), v_c_ctx.reshape(1, D))

    a_in = _adam_w_in(p_in_a, p_in_b, p_in_late, w_in, m_w_in, v_w_in)
    blk = _adam_blocks([p_pa, p_pb, p_out], [w_proj_a, w_proj_b, w_out], [m_w_proj_a, m_w_proj_b, m_w_out],
                       [v_w_proj_a, v_w_proj_b, v_w_out], "adam_proj_out")
    a_pa, a_pb, a_out = ([t[i] for t in blk] for i in range(3))
    gate_m = _rows128(jnp.concatenate([m_b_gate_w2.reshape(-1), m_b_gate_b.reshape(-1)]))
    gate_v = _rows128(jnp.concatenate([v_b_gate_w2.reshape(-1), v_b_gate_b.reshape(-1)]))
    a_gt = [t.reshape(-1) for t in _adam(p_gt, gate_mine, gate_m, gate_v, 9, "adam_gate")]
    nw2 = 2 * RANK * 32
    sh = [(a_in[k], a_pa[k], a_pb[k], a_out[k], a_gt[k][0:nw2].reshape(1, 2, RANK, 32),
           a_gt[k][nw2:nw2 + 64].reshape(1, 2, 32)) for k in range(4)]

    outs = [loss, r["gx"][None]]
    for k in range(4):
        lg, lb, aws, abs_, bng, fng = rep_e[k]
        n_g, bmod = ng[k], bmod_r[k]
        s_in, s_pa, s_pb, s_out, s_w2, s_gb = sh[k]
        outs += [cc[k].reshape(D), wm[k], bmod, n_g, s_in, lg, lb, aws, abs_, s_w2, s_gb, bng, s_pa, s_pb, s_out, fng]
    return tuple(outs)
```

```python
import jax
import jax.numpy as jnp
from jax import lax
from jax.experimental import pallas as pl
from jax.experimental.pallas import tpu as pltpu

f32, bf16 = jnp.float32, jnp.bfloat16

D = 1024
CTX = 256
EPS = 1e-6
AW = 512
ACH = 128
GW = 64
KW = 256
VW = 512
NH = 4
HK = 64
HV = 128
RANK = 16
TAU = 16.0
CH = 64
QSCALE = HK ** -0.5
INW = 5152
NDEV = 8

PQ, PK, PV, PZB, PUA, PVA, PZA, PG1, PG2, PLR, PW = 0, 256, 512, 1024, 1536, 2048, 2560, 3072, 4096, 5120, 5248
LRW = 128

ADAM_LR, ADAM_B1, ADAM_B2, ADAM_EPS, ADAM_WD, ADAM_STEP = 0.001, 0.9, 0.999, 1e-08, 0.01, 10

VMEM_LIMIT = 56 * 1024 * 1024
MESH = pl.DeviceIdType.MESH


def _cp(sem=None):
    return pltpu.CompilerParams(dimension_semantics=sem, vmem_limit_bytes=VMEM_LIMIT)


def _dot(a, b):
    return jnp.dot(a.astype(bf16), b.astype(bf16), preferred_element_type=f32)


def _nt(a, b):
    return lax.dot_general(a.astype(bf16), b.astype(bf16), (((1,), (1,)), ((), ())), preferred_element_type=f32)


def _tn(a, b):
    return lax.dot_general(a.astype(bf16), b.astype(bf16), (((0,), (0,)), ((), ())), preferred_element_type=f32)


def _dot_hi(a, b):
    return jnp.dot(a, b, preferred_element_type=f32, precision=lax.Precision.HIGHEST)


def _sigmoid(x):
    return 1.0 / (1.0 + jnp.exp(-x))


def _log_sigmoid(x):
    return jnp.minimum(x, 0.0) - jnp.log(1.0 + jnp.exp(-jnp.abs(x)))


def _silu_and_grad(z):
    s = _sigmoid(z)
    return z * s, s * (1.0 + z * (1.0 - s))


def _me():
    return 4 * lax.axis_index("x") + 2 * lax.axis_index("y") + lax.axis_index("c")


def _peer(k):
    x, y, c = lax.axis_index("x"), lax.axis_index("y"), lax.axis_index("c")
    px = 1 - x if k & 4 else x
    py = 1 - y if k & 2 else y
    pc = 1 - c if k & 1 else c
    return (px, py, pc), 4 * px + 2 * py + pc


def _fanout(srcs, dsts, send_sems, recv_sems, local_sems, owners=None):
    me = _me()
    n = len(srcs)
    owns = lambda a, j: True if owners is None or owners[a] is None else owners[a](j)

    def guarded(cond, fn):
        if cond is True:
            fn()
        else:
            pl.when(cond)(fn)

    def copies(with_recvs):
        local = [pltpu.make_async_copy(srcs[a](me), dsts[a](me), local_sems.at[a]) for a in range(n)]
        sends, recvs = [], []
        for k in range(1, NDEV):
            dev, idx = _peer(k)
            for a in range(n):
                s = (k - 1) * n + a
                sends.append((owns(a, idx), pltpu.make_async_remote_copy(
                    src_ref=srcs[a](idx), dst_ref=dsts[a](me), send_sem=send_sems.at[s], recv_sem=recv_sems.at[s],
                    device_id=dev, device_id_type=MESH)))
                if with_recvs:
                    recvs.append((owns(a, me), pltpu.make_async_remote_copy(
                        src_ref=srcs[a](idx), dst_ref=dsts[a](idx), send_sem=send_sems.at[s], recv_sem=recv_sems.at[s],
                        device_id=dev, device_id_type=MESH)))
        return local, sends, recvs

    def start():
        local, sends, _ = copies(False)
        for a, cp in enumerate(local):
            guarded(owns(a, me), cp.start)
        for cond, cp in sends:
            guarded(cond, cp.start)

    def finish():
        local, sends, recvs = copies(True)
        for cond, cp in recvs:
            guarded(cond, cp.wait_recv)
        for cond, cp in sends:
            guarded(cond, cp.wait_send)
        for a, cp in enumerate(local):
            guarded(owns(a, me), cp.wait)

    return start, finish


class _Comm:
    def __init__(self, ins, outs, plan):
        self.ins, self.outs, self.plan = list(ins), list(outs), plan
        self.n = len(self.outs)

    def specs(self):
        hbm = pl.BlockSpec(memory_space=pl.ANY)
        return [hbm] * len(self.ins), [hbm] * self.n, _fanout_sems(self.n) if self.n else []

    def run(self, in_refs, out_refs, sems, nsteps, compute):
        if not self.n:
            compute()
            return

        def hooks():
            srcs, dsts, owners = self.plan(in_refs, out_refs)
            return _fanout(srcs, dsts, sems[0], sems[1], sems[2], owners)

        pl.when(pl.program_id(0) == 0)(lambda: hooks()[0]())
        compute()
        pl.when(pl.program_id(0) == nsteps - 1)(lambda: hooks()[1]())


LATE_MID_STEP = 1


class _LateComm:
    def __init__(self, blocks, dgt):
        sds = jax.ShapeDtypeStruct
        self.ins = [blocks, dgt]
        self.outs = [sds((D, SHARD), bf16), sds((D, SHARD), bf16), sds((4, D, SHARD), bf16), sds(dgt.shape, f32)]
        self.n = len(self.outs)

    def specs(self):
        hbm = pl.BlockSpec(memory_space=pl.ANY)
        scratch = [pltpu.VMEM((3, D, SHARD), bf16), pltpu.SemaphoreType.DMA((2,)), pltpu.SemaphoreType.DMA((4,)),
                   pltpu.SemaphoreType.DMA((3,))] + _fanout_sems(1)
        return [hbm] * 2, [hbm] * self.n, scratch

    def run(self, in_refs, out_refs, scratch, nsteps, compute):
        late_ref, dgt_ref = in_refs
        sib_ref, pair_ref, parts_ref, gt_ref = out_refs
        vbuf, send_sems, recv_sems, local_sems, g_send, g_recv, g_local = scratch
        x, y, c = lax.axis_index("x"), lax.axis_index("y"), lax.axis_index("c")
        chip = 2 * x + y
        is_owner_chip = chip == 0
        step = pl.program_id(0)

        def to_sibling():
            return pltpu.make_async_remote_copy(src_ref=late_ref.at[1 - c], dst_ref=sib_ref, send_sem=send_sems.at[0],
                                                recv_sem=recv_sems.at[0], device_id=(x, y, 1 - c), device_id_type=MESH)

        def to_owner(k):
            return pltpu.make_async_remote_copy(src_ref=pair_ref, dst_ref=parts_ref.at[k], send_sem=send_sems.at[1],
                                                recv_sem=recv_sems.at[k], device_id=(0, 0, c), device_id_type=MESH)

        def own_copy():
            return pltpu.make_async_copy(pair_ref, parts_ref.at[0], local_sems.at[2])

        def gates():
            return _fanout([lambda j: dgt_ref.at[j]], [lambda j: gt_ref.at[j]], g_send, g_recv, g_local)

        @pl.when(step == 0)
        def _():
            to_sibling().start()
            gates()[0]()

        compute()

        @pl.when(step == LATE_MID_STEP)
        def _():
            mine = pltpu.make_async_copy(late_ref.at[c], vbuf.at[0], local_sems.at[0])
            mine.start()
            to_sibling().wait_recv()
            theirs = pltpu.make_async_copy(sib_ref, vbuf.at[1], local_sems.at[1])
            theirs.start()
            mine.wait()
            theirs.wait()
            vbuf[2] = (vbuf[0].astype(f32) + vbuf[1].astype(f32)).astype(bf16)
            pltpu.sync_copy(vbuf.at[2], pair_ref)
            pl.when(is_owner_chip)(lambda: own_copy().start())
            pl.when(jnp.logical_not(is_owner_chip))(lambda: to_owner(chip).start())

        @pl.when(step == nsteps - 1)
        def _():
            @pl.when(is_owner_chip)
            def _():
                for k in range(1, 4):
                    to_owner(k).wait_recv()
                own_copy().wait()

            pl.when(jnp.logical_not(is_owner_chip))(lambda: to_owner(chip).wait_send())
            to_sibling().wait_send()
            gates()[1]()


def _fanout_sems(n):
    return [pltpu.SemaphoreType.DMA(((NDEV - 1) * n,)), pltpu.SemaphoreType.DMA(((NDEV - 1) * n,)), pltpu.SemaphoreType.DMA((n,))]


def _lanes(j):
    return pl.ds(pl.multiple_of(j * 128, 128), 128)


def _gather_first(c_row, cctx_row, wm, bm, wi, gate):
    def body(c_ref, cctx_ref, wm_ref, bm_ref, wi_ref, g_ref, cs_ref, mods_ref, owi, og, call_ref, mine_ref,
             send_sems, recv_sems, local_sems, c_send, c_recv, c_local, m_send, m_recv, m_local):
        x, y, c = lax.axis_index("x"), lax.axis_index("y"), lax.axis_index("c")
        sibling = (x, y, 1 - c)
        chips = [(1 - x, y), (x, 1 - y), (1 - x, 1 - y)]
        index = lambda px, py, pc: 4 * px + 2 * py + pc
        arrays = ((wi_ref, owi), (g_ref, og))
        n = len(arrays)

        def copy(a, k, block, to, own=False):
            src, out = arrays[a]
            return pltpu.make_async_remote_copy(
                src_ref=src if own else out.at[index(*block)], dst_ref=out.at[index(*block)],
                send_sem=send_sems.at[k * n + a], recv_sem=recv_sems.at[k * n + a], device_id=to, device_id_type=MESH)

        c_start, c_finish = _fanout([lambda j: c_ref], [lambda j: call_ref.at[j]], c_send, c_recv, c_local)
        c_start()
        mine = [pltpu.make_async_copy(src, out.at[index(x, y, c)], local_sems.at[a]) for a, (src, out) in enumerate(arrays)]
        first = [copy(a, 0, (x, y, c), sibling, own=True) for a in range(n)]
        first += [copy(a, 1 + j, (x, y, c), (*chip, c), own=True) for j, chip in enumerate(chips) for a in range(n)]
        for cp in mine + first:
            cp.start()

        c_finish()
        cs = jnp.concatenate([call_ref[j] for j in range(NDEV)] + [cctx_ref[...], jnp.zeros((16 - NDEV - 1, D), f32)], axis=0)
        cs_ref[...] = cs
        s, _ = _silu_and_grad(cs)
        mine_ref[...] = _dot_hi(s, wm_ref[...]) + bm_ref[...]
        m_start, m_finish = _fanout([lambda j: mine_ref], [lambda j: mods_ref.at[j]], m_send, m_recv, m_local)
        m_start()

        passed = []
        for j, chip in enumerate(chips):
            for a in range(n):
                copy(a, 1 + j, (*chip, c), (x, y, c)).wait_recv()
            for a in range(n):
                cp = copy(a, 4 + j, (*chip, c), sibling)
                cp.start()
                passed.append(cp)
        for a in range(n):
            copy(a, 0, sibling, (x, y, c)).wait_recv()
        for j, chip in enumerate(chips):
            for a in range(n):
                copy(a, 4 + j, (*chip, 1 - c), (x, y, c)).wait_recv()
        for cp in first + passed:
            cp.wait_send()
        for cp in mine:
            cp.wait()
        m_finish()

    hbm = pl.BlockSpec(memory_space=pl.ANY)
    vm = pl.BlockSpec(memory_space=pltpu.VMEM)
    ncol = wm.shape[1]
    return pl.pallas_call(
        body, name="gather_first",
        out_shape=(jax.ShapeDtypeStruct((16, D), f32), jax.ShapeDtypeStruct((NDEV, 16, ncol), f32),
                   jax.ShapeDtypeStruct((NDEV,) + wi.shape, bf16), jax.ShapeDtypeStruct((NDEV,) + gate.shape, f32)),
        in_specs=[vm, vm, vm, vm, hbm, hbm], out_specs=(vm, vm, hbm, hbm),
        scratch_shapes=[pltpu.VMEM((NDEV, 1, D), f32), pltpu.VMEM((16, ncol), f32)] + _fanout_sems(2) + _fanout_sems(1) + _fanout_sems(1),
        compiler_params=_cp(),
    )(c_row, cctx_row, wm, bm, wi, gate)


SHARD = INW // NDEV
ROWS_RP = 128


def _overlap(lo, hi, a, b):
    s, e = max(lo, a), min(hi, b)
    return (s, e) if s < e else None


def _repack_w(wg):
    segs = ((0, 1024, PQ), (1024, 1024 + 2 * RANK, PLR), (1024 + 2 * RANK, INW, PZB))

    def body(g_ref, o_ref):
        for j in range(NDEV):
            lo, hi = j * SHARD, (j + 1) * SHARD
            for a, b, pad0 in segs:
                ov = _overlap(lo, hi, a, b)
                if ov:
                    s, e = ov
                    o_ref[:, pad0 + s - a:pad0 + e - a] = g_ref[j, :, s - lo:e - lo]
        o_ref[:, PLR + 2 * RANK:PW] = jnp.zeros((ROWS_RP, PW - PLR - 2 * RANK), bf16)

    return pl.pallas_call(
        body, name="repack_w", grid=(D // ROWS_RP,), out_shape=jax.ShapeDtypeStruct((D, PW), bf16),
        in_specs=[pl.BlockSpec((NDEV, ROWS_RP, SHARD), lambda i: (0, i, 0))],
        out_specs=pl.BlockSpec((ROWS_RP, PW), lambda i: (i, 0)), compiler_params=_cp(("arbitrary",)),
    )(wg)


LATE_END = 1024 + 2 * RANK
LATE_DESTS = 2


def _pack_blocks(o_ref, srcs, dests, dtype):
    for n, j in enumerate(dests):
        lo, hi = j * SHARD, (j + 1) * SHARD
        done = lo
        for a, b, src in srcs:
            ov = _overlap(lo, hi, a, b)
            if ov:
                s, e = ov
                if s > done:
                    o_ref[n, :, done - lo:s - lo] = jnp.zeros((ROWS_RP, s - done), dtype)
                o_ref[n, :, s - lo:e - lo] = src[:, s - a:e - a].astype(dtype)
                done = e
        if done < hi:
            o_ref[n, :, done - lo:hi - lo] = jnp.zeros((ROWS_RP, hi - done), dtype)


class _Pack:
    def __init__(self, extra, outs, fn):
        self.extra, self.outs, self.fn = list(extra), list(outs), fn


def _pack_early():
    def fn(accs, extra, outs):
        zbua, va, zag = accs
        for r0 in range(0, D, ROWS_RP):
            rows = pl.ds(r0, ROWS_RP)
            out, o0 = (outs[0], r0) if r0 < EARLY_A_ROWS else (outs[1], r0 - EARLY_A_ROWS)
            _pack_blocks(out.at[:, pl.ds(o0, ROWS_RP)], ((LATE_END, 2080, zbua.at[rows]), (2080, 2592, va.at[rows]), (2592, INW, zag.at[rows])),
                         range(NDEV), bf16)

    sds = jax.ShapeDtypeStruct
    return _Pack([], [sds((NDEV, EARLY_A_ROWS, SHARD), bf16), sds((NDEV, D - EARLY_A_ROWS, SHARD), bf16)], fn)


def _pack_late(dwk_c, dwv_c, dwl_c):
    def fn(accs, extra, outs):
        qkv_ref, lr_ref = accs
        kc_ref, vc_ref, lc_ref = extra
        for r0 in range(0, D, ROWS_RP):
            rows = pl.ds(r0, ROWS_RP)
            qkv = qkv_ref[rows, :] + jnp.concatenate([jnp.zeros((ROWS_RP, KW), f32), kc_ref[rows, :], vc_ref[rows, :]], axis=1)
            lr = lr_ref[rows, :] + lc_ref[rows, :]
            _pack_blocks(outs[0].at[:, rows], ((0, 1024, qkv), (1024, LATE_END, lr)), range(LATE_DESTS), bf16)

    return _Pack([dwk_c, dwv_c, dwl_c], [jax.ShapeDtypeStruct((LATE_DESTS, D, SHARD), bf16)], fn)


def _mod_tail(dnorm, dbm, dmod, dmodc, cs, wm, m_wm, v_wm, norm_g, m_ng, v_ng, b_mod, m_bm, v_bm, c_ctx, m_cc, v_cc):
    ncol = wm.shape[2]

    def body(dn_ref, dbm_ref, dmod_ref, dmodc_ref, cs_ref, wm_ref, mwm_ref, vwm_ref, ng_ref, mng_ref, vng_ref,
             bm_ref, mbm_ref, vbm_ref, cc_ref, mcc_ref, vcc_ref, *rest):
        o_wm, o_ng, o_bm, o_cc = rest[0:4], rest[4:8], rest[8:12], rest[12:16]
        a_dn, a_dbm, a_dmod, a_dmodc, dm_rows, gc_ref, a_gc = rest[16:23]
        s1, r1, l1, s2, r2, l2 = rest[23:29]
        start, finish = _fanout([lambda j: dn_ref, lambda j: dbm_ref, lambda j: dmod_ref, lambda j: dmodc_ref],
                                [lambda j, r=r: r.at[j] for r in (a_dn, a_dbm, a_dmod, a_dmodc)], s1, r1, l1)
        start()
        finish()

        def total(ref):
            t = ref[0]
            for j in range(1, NDEV):
                t = t + ref[j]
            return t

        _adam_update(total(a_dn), ng_ref, mng_ref, vng_ref, *o_ng)
        _adam_update(total(a_dbm), bm_ref, mbm_ref, vbm_ref, *o_bm)
        dmodc_tot = jnp.concatenate([total(a_dmodc), jnp.zeros((1, D), f32)], axis=1)
        dm_rows[...] = jnp.concatenate([a_dmod[j] for j in range(NDEV)] + [dmodc_tot, jnp.zeros((16 - NDEV - 1, 3 * D), f32)], axis=0)
        dm = dm_rows[:, pl.ds(pl.multiple_of(_me() * ncol, 128), ncol)]
        s, ds = _silu_and_grad(cs_ref[...])
        part = lax.dot_general(dm[8:9, :], wm_ref[0], (((1,), (1,)), ((), ())), preferred_element_type=f32,
                               precision=lax.Precision.HIGHEST)
        gc_ref[...] = part * ds[8:9, :]
        start2, finish2 = _fanout([lambda j: gc_ref], [lambda j: a_gc.at[j]], s2, r2, l2)
        start2()
        g = lax.dot_general(s, dm, (((0,), (0,)), ((), ())), preferred_element_type=f32, precision=lax.Precision.HIGHEST)
        _adam_update(g[None], wm_ref, mwm_ref, vwm_ref, *o_wm)
        finish2()
        _adam_update(total(a_gc), cc_ref, mcc_ref, vcc_ref, *o_cc)

    like = lambda a: [jax.ShapeDtypeStruct(a.shape, f32)] * 4
    row = lambda n: pltpu.VMEM((NDEV, 1, n), f32)
    res = pl.pallas_call(
        body, name="mod_tail", out_shape=tuple(like(wm) + like(norm_g) + like(b_mod) + like(c_ctx)),
        scratch_shapes=[row(D), row(3 * D), row(3 * D), row(2 * D), pltpu.VMEM((16, 3 * D), f32), pltpu.VMEM((1, D), f32), row(D)]
        + _fanout_sems(4) + _fanout_sems(1),
        compiler_params=_cp(),
    )(dnorm, dbm, dmod, dmodc, cs, wm, m_wm, v_wm, norm_g, m_ng, v_ng, b_mod, m_bm, v_bm, c_ctx, m_cc, v_cc)
    return res[0:4], res[4:8], res[8:12], res[12:16]


def _adam_update(g, w_ref, m_ref, v_ref, go_ref, d_ref, mo_ref, vo_ref):
    c1 = 1.0 / (1.0 - ADAM_B1 ** ADAM_STEP)
    c2 = 1.0 / (1.0 - ADAM_B2 ** ADAM_STEP)
    mn = ADAM_B1 * m_ref[...] + (1.0 - ADAM_B1) * g
    vn = ADAM_B2 * v_ref[...] + (1.0 - ADAM_B2) * (g * g)
    go_ref[...] = g
    mo_ref[...] = mn
    vo_ref[...] = vn
    d_ref[...] = -ADAM_LR * ((mn * c1) / (jnp.sqrt(vn * c2) + ADAM_EPS) + ADAM_WD * w_ref[...])


def _adam_w_in(parts_a, parts_b, parts_late, w, m, v):
    na = EARLY_A_ROWS // ROWS_RP

    def body(a_ref, b_ref, l_ref, w_ref, m_ref, v_ref, go_ref, d_ref, mo_ref, vo_ref):
        me = _me()
        first = pl.program_id(0) < na
        early = jnp.where(first, a_ref[0], b_ref[0]).astype(f32)
        for i in range(1, NDEV):
            early = early + jnp.where(first, a_ref[i], b_ref[i]).astype(f32)
        late = l_ref[0].astype(f32)
        for i in range(1, 4):
            late = late + l_ref[i].astype(f32)
        g = jnp.where(me >= LATE_DESTS - 1, early, 0.0) + jnp.where(me < LATE_DESTS, late, 0.0)
        _adam_update(g, w_ref, m_ref, v_ref, go_ref, d_ref, mo_ref, vo_ref)

    blk = pl.BlockSpec((None, ROWS_RP, SHARD), lambda i: (0, i, 0))
    return pl.pallas_call(
        body, name="adam_w_in", grid=(D // ROWS_RP,), out_shape=tuple(jax.ShapeDtypeStruct((1, D, SHARD), f32) for _ in range(4)),
        in_specs=[pl.BlockSpec((NDEV, ROWS_RP, SHARD), lambda i: (0, jnp.minimum(i, na - 1), 0)),
                  pl.BlockSpec((NDEV, ROWS_RP, SHARD), lambda i: (0, jnp.maximum(i - na, 0), 0)),
                  pl.BlockSpec((4, ROWS_RP, SHARD), lambda i: (0, i, 0)), blk, blk, blk],
        out_specs=(blk, blk, blk, blk), compiler_params=_cp(("arbitrary",)),
    )(parts_a, parts_b, parts_late, w, m, v)


def _adam_params(parts, ws, ms, vs, name):
    p = parts.shape[0]
    k = len(ws)
    nrows = [w.size // 128 for w in ws]
    starts = [sum(nrows[:i]) for i in range(k)]

    def shaped(g, shape):
        if len(shape) == 2:
            return jnp.concatenate([g[r:r + 1] for r in range(g.shape[0])], axis=1)
        return g.reshape(shape)

    def body(*refs):
        g_ref = refs[0]
        w_refs, m_refs, v_refs = refs[1:1 + k], refs[1 + k:1 + 2 * k], refs[1 + 2 * k:1 + 3 * k]
        outs = refs[1 + 3 * k:]
        for i in range(k):
            rows = slice(starts[i], starts[i] + nrows[i])
            g = g_ref[0, rows, :]
            for j in range(1, p):
                g = g + g_ref[j, rows, :]
            _adam_update(shaped(g, ws[i].shape), w_refs[i], m_refs[i], v_refs[i], outs[i], outs[k + i], outs[2 * k + i], outs[3 * k + i])

    vm = pl.BlockSpec(memory_space=pltpu.VMEM)
    res = pl.pallas_call(
        body, name=name, out_shape=tuple(jax.ShapeDtypeStruct(w.shape, f32) for _ in range(4) for w in ws),
        in_specs=[vm] * (1 + 3 * k), out_specs=(vm,) * (4 * k), compiler_params=_cp(),
    )(parts, *ws, *ms, *vs)
    return [res[i * k:(i + 1) * k] for i in range(4)]


def _adam_blocks(parts, ws, ms, vs, name):
    k = len(ws)

    def body(*refs):
        g_refs, w_refs, m_refs, v_refs = refs[0:k], refs[k:2 * k], refs[2 * k:3 * k], refs[3 * k:4 * k]
        outs = refs[4 * k:]
        for i in range(k):
            g = g_refs[i][0].astype(f32)
            for j in range(1, parts[i].shape[0]):
                g = g + g_refs[i][j].astype(f32)
            _adam_update(g[None], w_refs[i], m_refs[i], v_refs[i], outs[i], outs[k + i], outs[2 * k + i], outs[3 * k + i])

    vm = pl.BlockSpec(memory_space=pltpu.VMEM)
    res = pl.pallas_call(
        body, name=name, out_shape=tuple(jax.ShapeDtypeStruct(w.shape, f32) for _ in range(4) for w in ws),
        in_specs=[vm] * (4 * k), out_specs=(vm,) * (4 * k), compiler_params=_cp(),
    )(*parts, *ws, *ms, *vs)
    return [res[i * k:(i + 1) * k] for i in range(4)]


def _adam(parts, w, m, v, rows, name):
    p, r, c = parts.shape
    lead = w.ndim - 2

    def body(g_ref, w_ref, m_ref, v_ref, go_ref, d_ref, mo_ref, vo_ref):
        g = g_ref[0].astype(f32)
        for i in range(1, p):
            g = g + g_ref[i].astype(f32)
        _adam_update(g, w_ref, m_ref, v_ref, go_ref, d_ref, mo_ref, vo_ref)

    blk = pl.BlockSpec((None,) * lead + (rows, c), lambda i: (0,) * lead + (i, 0))
    return pl.pallas_call(
        body, name=name, grid=(r // rows,), out_shape=tuple(jax.ShapeDtypeStruct(w.shape, f32) for _ in range(4)),
        in_specs=[pl.BlockSpec((p, rows, c), lambda i: (0, i, 0)), blk, blk, blk], out_specs=(blk, blk, blk, blk),
        compiler_params=_cp(("arbitrary",)),
    )(parts, w, m, v)


def _in_proj(x, g, scale, shift, w_pad, tl, comm):
    l = x.shape[0]
    c_in, c_out, c_sems = comm.specs()

    def body(*refs):
        x_ref, g_ref, sc_ref, sh_ref, w_ref = refs[:5]
        cin = refs[5:5 + len(c_in)]
        p_ref, h_ref = refs[5 + len(c_in):7 + len(c_in)]
        cout = refs[7 + len(c_in):7 + len(c_in) + comm.n]
        sems = refs[7 + len(c_in) + comm.n:]

        def compute():
            xv = x_ref[...]
            r = lax.rsqrt(jnp.mean(xv * xv, axis=-1, keepdims=True) + EPS)
            h = (xv * r) * (g_ref[...] * (1.0 + sc_ref[...])) + sh_ref[...]
            hb = h.astype(bf16)
            h_ref[...] = hb
            p_ref[...] = jnp.dot(hb, w_ref[...], preferred_element_type=f32).astype(bf16)

        comm.run(cin, cout, sems, l // tl, compute)

    vec = pl.BlockSpec((1, D), lambda i: (0, 0))
    return pl.pallas_call(
        body, name="in_proj", grid=(l // tl,),
        out_shape=(jax.ShapeDtypeStruct((l, PW), bf16), jax.ShapeDtypeStruct((l, D), bf16)) + tuple(comm.outs),
        in_specs=[pl.BlockSpec((tl, D), lambda i: (i, 0)), vec, vec, vec, pl.BlockSpec((D, PW), lambda i: (0, 0))] + c_in,
        out_specs=(pl.BlockSpec((tl, PW), lambda i: (i, 0)), pl.BlockSpec((tl, D), lambda i: (i, 0))) + tuple(c_out),
        scratch_shapes=c_sems, compiler_params=_cp(("arbitrary",)),
    )(x, g, scale, shift, w_pad, *comm.ins)


def _tri(rev):
    i = lax.broadcasted_iota(jnp.int32, (CH, CH), 0)
    j = lax.broadcasted_iota(jnp.int32, (CH, CH), 1)
    return jnp.where((j >= i) if rev else (j <= i), 1.0, 0.0).astype(f32)


def _head_masks():
    lane = lax.broadcasted_iota(jnp.int32, (1, KW), 1) // HK
    return [jnp.where(lane == h, 1.0, 0.0).astype(f32) for h in range(NH)]


def _block_diag():
    r = lax.broadcasted_iota(jnp.int32, (VW, KW), 0) // HV
    c = lax.broadcasted_iota(jnp.int32, (VW, KW), 1) // HK
    return jnp.where(r == c, 1.0, 0.0).astype(f32)


def _decay(lr, w2, gb, tri, rev):
    logits = _dot(lr, w2) + gb
    a = _log_sigmoid(logits) * (1.0 / TAU)
    c = _dot_hi(tri, a)
    cl = c[0:1, :] if rev else c[CH - 1:CH, :]
    return logits, c, cl


def _state_fwd(k, v, c, cl, st, bd):
    return st * jnp.exp(cl) + bd * _tn(v, k * jnp.exp(cl - c))


def _state_bwd(k, v, c, cl, st0, dst, trit):
    ecl = jnp.exp(cl)
    edec = jnp.exp(cl - c)
    kdec = k * edec
    dv = _nt(kdec, dst)
    dkdec = _dot(v, dst)
    dcl = jnp.sum(dst * st0, axis=0, keepdims=True) * ecl + jnp.sum(dkdec * kdec, axis=0, keepdims=True)
    da = _dot_hi(trit, -dkdec * kdec) + dcl
    return dkdec * edec, dv, da, dst * ecl


def _bdot(a, b):
    return lax.dot_general(a.astype(bf16), b.astype(bf16), (((2,), (1,)), ((0,), (0,))), preferred_element_type=f32)


def _bnt(a, b):
    return lax.dot_general(a.astype(bf16), b.astype(bf16), (((2,), (2,)), ((0,), (0,))), preferred_element_type=f32)


def _btn(a, b):
    return lax.dot_general(a.astype(bf16), b.astype(bf16), (((1,), (1,)), ((0,), (0,))), preferred_element_type=f32)


def _scan_chunks(x, rev):
    nc = x.shape[0]
    hi = x.astype(bf16)
    r1 = x - hi.astype(f32)
    mid = r1.astype(bf16)
    lo = (r1 - mid.astype(f32)).astype(bf16)
    terms = jnp.concatenate([hi, mid, lo], axis=1)
    tri3 = jnp.broadcast_to(jnp.concatenate([_tri(rev)] * 3, axis=1).astype(bf16)[None], (nc, CH, 3 * CH))
    return lax.dot_general(tri3, terms, (((2,), (1,)), ((0,), (0,))), preferred_element_type=f32)


class _Tile:
    pass


def _tile_prep(q_ref, k_ref, lr_ref, w2_ref, gb_ref, rev, nc):
    t = _Tile()
    tg = nc * CH
    t.logits = _dot(lr_ref[...], w2_ref[...]) + gb_ref[...]
    c = _scan_chunks((_log_sigmoid(t.logits) * (1.0 / TAU)).reshape(nc, CH, KW), rev)
    cl = c[:, 0:1, :] if rev else c[:, CH - 1:CH, :]
    k = k_ref[...].astype(f32).reshape(nc, CH, KW)
    t.ec, t.enc, t.edec, t.ecl = jnp.exp(c), jnp.exp(-c), jnp.exp(cl - c), jnp.exp(cl)
    t.qd = q_ref[...].astype(f32).reshape(nc, CH, KW) * t.ec * QSCALE
    t.kd = k * t.enc
    t.kdec = k * t.edec
    hm = _head_masks()
    t.tri4 = jnp.concatenate([_tri(rev)] * NH, axis=0)[None]
    t.qs = jnp.concatenate([t.qd * hm[h] for h in range(NH)], axis=1)
    t.pst = _bnt(t.qs, t.kd) * t.tri4
    return t


def _gla_fwd(p, w2p, gb, s0, rev, tg, name):
    l = p.shape[0]
    nb, nc = l // tg, tg // CH

    def body(q_ref, k_ref, v_ref, lr_ref, w2_ref, gb_ref, s0_ref, o_ref, st_ref, st):
        @pl.when(pl.program_id(0) == 0)
        def _():
            st[...] = s0_ref[...]

        t = _tile_prep(q_ref, k_ref, lr_ref, w2_ref, gb_ref, rev, nc)
        v = v_ref[...].reshape(nc, CH, VW)
        intra = jnp.concatenate([_bdot(t.pst[:, h * CH:(h + 1) * CH], v[:, :, h * HV:(h + 1) * HV]) for h in range(NH)], axis=2)
        bd = _block_diag()
        s = st[...]
        for n in (range(nc - 1, -1, -1) if rev else range(nc)):
            st_ref[n] = s.astype(bf16)
            s = s * t.ecl[n] + bd * _tn(v[n], t.kdec[n])
        st[...] = s
        o_ref[...] = (_bnt(t.qd, st_ref[...]) + intra).reshape(tg, VW).astype(bf16)

    blk = (lambda i: nb - 1 - i) if rev else (lambda i: i)
    return pl.pallas_call(
        body, name=name, grid=(nb,),
        out_shape=(jax.ShapeDtypeStruct((l, VW), bf16), jax.ShapeDtypeStruct((l // CH, VW, KW), bf16)),
        in_specs=[pl.BlockSpec((tg, KW), lambda i: (blk(i), PQ // KW)), pl.BlockSpec((tg, KW), lambda i: (blk(i), PK // KW)),
                  pl.BlockSpec((tg, VW), lambda i: (blk(i), PV // VW)), pl.BlockSpec((tg, LRW), lambda i: (blk(i), PLR // LRW)),
                  pl.BlockSpec((LRW, KW), lambda i: (0, 0)), pl.BlockSpec((1, KW), lambda i: (0, 0)),
                  pl.BlockSpec((VW, KW), lambda i: (0, 0))],
        out_specs=(pl.BlockSpec((tg, VW), lambda i: (blk(i), 0)), pl.BlockSpec((nc, VW, KW), lambda i: (blk(i), 0, 0))),
        scratch_shapes=[pltpu.VMEM((VW, KW), f32)],
        compiler_params=_cp(("arbitrary",)),
    )(p, p, p, p, w2p, gb, s0)


def _gla_bwd(p, do, states, w2p, gb, prev, rev, tg, name, comm):
    l = p.shape[0]
    nb, nc = l // tg, tg // CH
    out_dt = bf16
    c_in, c_out, c_sems = comm.specs()

    def body(*refs):
        q_ref, k_ref, v_ref, lr_ref, do_ref, st_ref, w2_ref, gb_ref = refs[:8]
        refs = refs[8:]
        if prev is not None:
            pq_ref, pl_ref = refs[:2]
            refs = refs[2:]
        cin, refs = refs[:len(c_in)], refs[len(c_in):]
        dqkv_ref, dlr_ref, dw2_ref, dgb_ref, ds0_ref = refs[:5]
        cout, dst, ds_buf, sems = refs[5:5 + comm.n], refs[5 + comm.n], refs[6 + comm.n], refs[7 + comm.n:]
        comm.run(cin, cout, sems, nb, lambda: compute(q_ref, k_ref, v_ref, lr_ref, do_ref, st_ref, w2_ref, gb_ref,
                                                      pq_ref if prev is not None else None, pl_ref if prev is not None else None,
                                                      dqkv_ref, dlr_ref, dw2_ref, dgb_ref, ds0_ref, dst, ds_buf))

    def compute(q_ref, k_ref, v_ref, lr_ref, do_ref, st_ref, w2_ref, gb_ref, pq_ref, pl_ref,
                dqkv_ref, dlr_ref, dw2_ref, dgb_ref, ds0_ref, dst, ds_buf):
        @pl.when(pl.program_id(0) == 0)
        def _():
            dst[...] = jnp.zeros_like(dst)
            dw2_ref[...] = jnp.zeros_like(dw2_ref)
            dgb_ref[...] = jnp.zeros_like(dgb_ref)

        t = _tile_prep(q_ref, k_ref, lr_ref, w2_ref, gb_ref, rev, nc)
        hm = _head_masks()
        v = v_ref[...].reshape(nc, CH, VW)
        do = do_ref[...].reshape(nc, CH, VW)
        heads = lambda a, h: a[:, :, h * HV:(h + 1) * HV]
        dpst = jnp.concatenate([_bnt(heads(do, h), heads(v, h)) for h in range(NH)], axis=1) * t.tri4
        dv = jnp.concatenate([_btn(t.pst[:, h * CH:(h + 1) * CH], heads(do, h)) for h in range(NH)], axis=2)
        dqd = _bdot(do, st_ref[...])
        for h in range(NH):
            dqd = dqd + hm[h] * _bdot(dpst[:, h * CH:(h + 1) * CH], t.kd)
        dkd = _btn(dpst, t.qs)
        bd = _block_diag()
        d = dst[...]
        for n in (range(nc) if rev else range(nc - 1, -1, -1)):
            ds_buf[n] = d
            d = d * t.ecl[n] + bd * _tn(do[n], t.qd[n])
        dst[...] = d
        ds0_ref[...] = d
        ds = ds_buf[...]
        dv = dv + _bnt(t.kdec, ds)
        dkdec = _bdot(v, ds)
        dcl = jnp.sum(ds * st_ref[...].astype(f32), axis=1, keepdims=True) * t.ecl + jnp.sum(dkdec * t.kdec, axis=1, keepdims=True)
        dc = dqd * t.qd - dkd * t.kd - dkdec * t.kdec
        da = _scan_chunks(dc, not rev) + dcl
        dq = dqd * t.ec * QSCALE
        dk = dkd * t.enc + dkdec * t.edec
        dlog = da.reshape(tg, KW) * _sigmoid(-t.logits) * (1.0 / TAU)
        dlr = _nt(dlog, w2_ref[...])
        dw2_ref[...] += _tn(lr_ref[...], dlog)
        dgb_ref[...] += jnp.sum(dlog, axis=0, keepdims=True)
        dqkv = jnp.concatenate([dq, dk, dv], axis=2).reshape(tg, 2 * KW + VW)
        if prev is not None:
            dqkv = dqkv + pq_ref[...]
            dlr = dlr + pl_ref[...]
        dqkv_ref[...] = dqkv.astype(out_dt)
        dlr_ref[...] = dlr.astype(out_dt)

    blk = (lambda i: i) if rev else (lambda i: nb - 1 - i)
    in_specs = [pl.BlockSpec((tg, KW), lambda i: (blk(i), PQ // KW)), pl.BlockSpec((tg, KW), lambda i: (blk(i), PK // KW)),
                pl.BlockSpec((tg, VW), lambda i: (blk(i), PV // VW)), pl.BlockSpec((tg, LRW), lambda i: (blk(i), PLR // LRW)),
                pl.BlockSpec((tg, VW), lambda i: (blk(i), 0)), pl.BlockSpec((nc, VW, KW), lambda i: (blk(i), 0, 0)),
                pl.BlockSpec((LRW, KW), lambda i: (0, 0)), pl.BlockSpec((1, KW), lambda i: (0, 0))]
    args = [p, p, p, p, do, states, w2p, gb]
    if prev is not None:
        in_specs += [pl.BlockSpec((tg, 2 * KW + VW), lambda i: (blk(i), 0)), pl.BlockSpec((tg, LRW), lambda i: (blk(i), 0))]
        args += list(prev)
    return pl.pallas_call(
        body, name=name, grid=(nb,),
        out_shape=(jax.ShapeDtypeStruct((l, 2 * KW + VW), out_dt), jax.ShapeDtypeStruct((l, LRW), out_dt),
                   jax.ShapeDtypeStruct((LRW, KW), f32), jax.ShapeDtypeStruct((1, KW), f32), jax.ShapeDtypeStruct((VW, KW), f32))
        + tuple(comm.outs),
        in_specs=in_specs + c_in,
        out_specs=(pl.BlockSpec((tg, 2 * KW + VW), lambda i: (blk(i), 0)), pl.BlockSpec((tg, LRW), lambda i: (blk(i), 0)),
                   pl.BlockSpec((LRW, KW), lambda i: (0, 0)), pl.BlockSpec((1, KW), lambda i: (0, 0)),
                   pl.BlockSpec((VW, KW), lambda i: (0, 0))) + tuple(c_out),
        scratch_shapes=[pltpu.VMEM((VW, KW), f32), pltpu.VMEM((nc, VW, KW), f32)] + c_sems,
        compiler_params=_cp(("arbitrary",)),
    )(*args, *comm.ins)


def _ctx_hidden(ctx_ref, g_ref, sc_ref, sh_ref):
    xv = ctx_ref[...]
    r = lax.rsqrt(jnp.mean(xv * xv, axis=-1, keepdims=True) + EPS)
    xn = xv * r
    return xn, xn * (g_ref[...] * (1.0 + sc_ref[...])) + sh_ref[...]


_CTX_W_SPECS = [pl.BlockSpec((D, KW), lambda i: (0, PK // KW)), pl.BlockSpec((D, VW), lambda i: (0, PV // VW)),
                pl.BlockSpec((D, LRW), lambda i: (0, PLR // LRW))]


def _ctx_fwd(ctx, g, scale, shift, w_pad, w2f, w2b, gbf, gbb):
    ncc = CTX // CH

    def body(ctx_ref, g_ref, sc_ref, sh_ref, wk_ref, wv_ref, wl_ref, w2f_ref, w2b_ref, gbf_ref, gbb_ref, sf_ref, sb_ref):
        _, hc = _ctx_hidden(ctx_ref, g_ref, sc_ref, sh_ref)
        k, v, lr = _dot(hc, wk_ref[...]), _dot(hc, wv_ref[...]), _dot(hc, wl_ref[...])
        bd = _block_diag()
        for rev, w2_ref, gb_ref, out in ((False, w2f_ref, gbf_ref, sf_ref), (True, w2b_ref, gbb_ref, sb_ref)):
            tri = _tri(rev)
            st = jnp.zeros((VW, KW), f32)
            for j in (range(ncc - 1, -1, -1) if rev else range(ncc)):
                rows = slice(j * CH, (j + 1) * CH)
                _, c, cl = _decay(lr[rows], w2_ref[...], gb_ref[...], tri, rev)
                st = _state_fwd(k[rows], v[rows], c, cl, st, bd)
            out[...] = st

    vec = pl.BlockSpec((1, D), lambda i: (0, 0))
    w2s = pl.BlockSpec((LRW, KW), lambda i: (0, 0))
    gbs = pl.BlockSpec((1, KW), lambda i: (0, 0))
    sts = pl.BlockSpec((VW, KW), lambda i: (0, 0))
    return pl.pallas_call(
        body, name="ctx_fwd", grid=(1,), out_shape=(jax.ShapeDtypeStruct((VW, KW), f32),) * 2,
        in_specs=[pl.BlockSpec((CTX, D), lambda i: (0, 0)), vec, vec, vec] + _CTX_W_SPECS + [w2s, w2s, gbs, gbs],
        out_specs=(sts, sts), compiler_params=_cp(("arbitrary",)),
    )(ctx, g, scale, shift, w_pad, w_pad, w_pad, w2f, w2b, gbf, gbb)


def _ctx_bwd(ctx, g, scale, shift, w_pad, w2f, w2b, gbf, gbb, dsf, dsb):
    ncc = CTX // CH

    def body(ctx_ref, g_ref, sc_ref, sh_ref, wk_ref, wv_ref, wl_ref, w2f_ref, w2b_ref, gbf_ref, gbb_ref, dsf_ref, dsb_ref,
             dwk_ref, dwv_ref, dwl_ref, dmod_ref, dg_ref, dw2_ref, dgb_ref):
        xn, hc = _ctx_hidden(ctx_ref, g_ref, sc_ref, sh_ref)
        k, v, lr = _dot(hc, wk_ref[...]), _dot(hc, wv_ref[...]), _dot(hc, wl_ref[...])
        bd = _block_diag()
        dk_rows, dv_rows, dl_rows = [None] * ncc, [None] * ncc, [None] * ncc
        for d, (rev, w2_ref, gb_ref, ds_ref) in enumerate(((False, w2f_ref, gbf_ref, dsf_ref), (True, w2b_ref, gbb_ref, dsb_ref))):
            tri = _tri(rev)
            order = list(range(ncc - 1, -1, -1) if rev else range(ncc))
            st, saved = jnp.zeros((VW, KW), f32), {}
            for j in order:
                rows = slice(j * CH, (j + 1) * CH)
                logits, c, cl = _decay(lr[rows], w2_ref[...], gb_ref[...], tri, rev)
                saved[j] = (logits, c, cl, st)
                st = _state_fwd(k[rows], v[rows], c, cl, st, bd)
            dst = ds_ref[...]
            dw2 = jnp.zeros((LRW, KW), f32)
            dgb = jnp.zeros((1, KW), f32)
            for j in reversed(order):
                rows = slice(j * CH, (j + 1) * CH)
                logits, c, cl, st0 = saved[j]
                dk, dv, da, dst = _state_bwd(k[rows], v[rows], c, cl, st0, dst, _tri(not rev))
                dlog = da * _sigmoid(-logits) * (1.0 / TAU)
                dl = _nt(dlog, w2_ref[...])
                dw2 = dw2 + _tn(lr[rows], dlog)
                dgb = dgb + jnp.sum(dlog, axis=0, keepdims=True)
                dk_rows[j] = dk if dk_rows[j] is None else dk_rows[j] + dk
                dv_rows[j] = dv if dv_rows[j] is None else dv_rows[j] + dv
                dl_rows[j] = dl if dl_rows[j] is None else dl_rows[j] + dl
            dw2_ref[d] = dw2
            dgb_ref[d] = dgb
        dk, dv, dl = (jnp.concatenate(t, axis=0) for t in (dk_rows, dv_rows, dl_rows))
        dwk_ref[...] = _tn(hc, dk)
        dwv_ref[...] = _tn(hc, dv)
        dwl_ref[...] = _tn(hc, dl)
        dh = _nt(dk, wk_ref[...]) + _nt(dv, wv_ref[...]) + _nt(dl, wl_ref[...])
        gx = dh * xn
        dmod_ref[:, 0:D] = jnp.sum(dh, axis=0, keepdims=True)
        dmod_ref[:, D:2 * D] = jnp.sum(gx, axis=0, keepdims=True) * g_ref[...]
        dg_ref[...] = jnp.sum(gx, axis=0, keepdims=True) * (1.0 + sc_ref[...])

    vec = pl.BlockSpec((1, D), lambda i: (0, 0))
    w2s = pl.BlockSpec((LRW, KW), lambda i: (0, 0))
    gbs = pl.BlockSpec((1, KW), lambda i: (0, 0))
    sts = pl.BlockSpec((VW, KW), lambda i: (0, 0))
    full = lambda *s: pl.BlockSpec(s, lambda i: (0,) * len(s))
    return pl.pallas_call(
        body, name="ctx_bwd", grid=(1,),
        out_shape=(jax.ShapeDtypeStruct((D, KW), f32), jax.ShapeDtypeStruct((D, VW), f32), jax.ShapeDtypeStruct((D, LRW), f32),
                   jax.ShapeDtypeStruct((1, 2 * D), f32), jax.ShapeDtypeStruct((1, D), f32),
                   jax.ShapeDtypeStruct((2, LRW, KW), f32), jax.ShapeDtypeStruct((2, 1, KW), f32)),
        in_specs=[pl.BlockSpec((CTX, D), lambda i: (0, 0)), vec, vec, vec] + _CTX_W_SPECS + [w2s, w2s, gbs, gbs, sts, sts],
        out_specs=(full(D, KW), full(D, VW), full(D, LRW), full(1, 2 * D), full(1, D), full(2, LRW, KW), full(2, 1, KW)),
        compiler_params=_cp(("arbitrary",)),
    )(ctx, g, scale, shift, w_pad, w_pad, w_pad, w2f, w2b, gbf, gbb, dsf, dsb)


def _layernorm(va, g, b):
    mu = jnp.mean(va, axis=-1, keepdims=True)
    xc = va - mu
    rstd = lax.rsqrt(jnp.mean(xc * xc, axis=-1, keepdims=True) + EPS)
    vhat = xc * rstd
    return vhat, rstd, vhat * g + b


MIX_PIECES = 8


def _mix_fwd(p, ln_g, ln_b, ws, bs_t):
    l = p.shape[0]
    half = AW // 2
    rp = l // MIX_PIECES
    cpp = rp // ACH

    def body(p_ref, g_ref, b_ref, ws_ref, bs_ref, sv_hbm, va_buf, col_buf, sv_buf, in_sems, out_sems):
        piece = lambda i: pl.ds(pl.multiple_of(i * rp, rp), rp)
        load = lambda i: pltpu.make_async_copy(p_ref.at[piece(i), pl.ds(PVA, AW)], va_buf.at[piece(i)], in_sems.at[i])
        store = lambda i: pltpu.make_async_copy(sv_buf.at[piece(i)], sv_hbm.at[piece(i)], out_sems.at[i])
        for i in range(MIX_PIECES):
            load(i).start()

        def rows_piece(i, carry):
            load(i).wait()
            for j in range(cpp):
                rows = pl.ds(pl.multiple_of(i * rp + j * ACH, ACH), ACH)
                _, _, vn = _layernorm(va_buf[rows, :].astype(f32), g_ref[...], b_ref[...])
                for gi in range(2):
                    sl = slice(gi * ACH, (gi + 1) * ACH)
                    sv_buf[rows, sl] = (_dot(ws_ref[gi], vn[:, sl]) + bs_ref[:, gi:gi + 1]).astype(bf16)
                col_buf[0, rows, :] = vn[:, half:half + ACH]
                col_buf[1, rows, :] = vn[:, half + ACH:]
            return carry

        lax.fori_loop(0, MIX_PIECES, rows_piece, 0)

        def cols_step(cidx, carry):
            rows = pl.ds(cidx, ACH, stride=GW)
            for gi in range(2, 4):
                col_buf[gi - 2, rows, :] = _dot(ws_ref[gi], col_buf[gi - 2, rows, :]) + bs_ref[:, gi:gi + 1]
            return carry

        lax.fori_loop(0, GW, cols_step, 0, unroll=8)

        def out_piece(i, carry):
            for j in range(cpp):
                rows = pl.ds(pl.multiple_of(i * rp + j * ACH, ACH), ACH)
                sv_buf[rows, half:half + ACH] = col_buf[0, rows, :].astype(bf16)
                sv_buf[rows, half + ACH:] = col_buf[1, rows, :].astype(bf16)
            store(i).start()
            return carry

        lax.fori_loop(0, MIX_PIECES, out_piece, 0)
        for i in range(MIX_PIECES):
            store(i).wait()

    vm = pl.BlockSpec(memory_space=pltpu.VMEM)
    hbm = pl.BlockSpec(memory_space=pl.ANY)
    return pl.pallas_call(
        body, name="mix_fwd", out_shape=jax.ShapeDtypeStruct((l, AW), bf16),
        in_specs=[hbm, vm, vm, vm, vm], out_specs=hbm,
        scratch_shapes=[pltpu.VMEM((l, AW), bf16), pltpu.VMEM((2, l, ACH), f32), pltpu.VMEM((l, AW), bf16),
                        pltpu.SemaphoreType.DMA((MIX_PIECES,)), pltpu.SemaphoreType.DMA((MIX_PIECES,))],
        compiler_params=_cp(),
    )(p, ln_g, ln_b, ws, bs_t)


def _mix_bwd(p, dsv, ln_g, ln_b, ws_t):
    l = p.shape[0]
    half = AW // 2
    rp = l // MIX_PIECES
    cpp = rp // ACH

    def body(p_ref, dsv_hbm, g_ref, b_ref, wst_ref, dva_hbm, dws_ref, dbs_ref, dg_ref, db_ref,
             va_buf, dsv_buf, vn_col, ds_col, dva_buf, va_sems, ds_sems, out_sems):
        piece = lambda i: pl.ds(pl.multiple_of(i * rp, rp), rp)
        load_va = lambda i: pltpu.make_async_copy(p_ref.at[piece(i), pl.ds(PVA, AW)], va_buf.at[piece(i)], va_sems.at[i])
        load_ds = lambda i: pltpu.make_async_copy(dsv_hbm.at[piece(i)], dsv_buf.at[piece(i)], ds_sems.at[i])
        store = lambda i: pltpu.make_async_copy(dva_buf.at[piece(i)], dva_hbm.at[piece(i)], out_sems.at[i])
        for i in range(MIX_PIECES):
            load_va(i).start()
            load_ds(i).start()
        dws_ref[...] = jnp.zeros_like(dws_ref)
        dbs_ref[...] = jnp.zeros_like(dbs_ref)
        dg_ref[...] = jnp.zeros_like(dg_ref)
        db_ref[...] = jnp.zeros_like(db_ref)

        def rows_piece(i, carry):
            load_va(i).wait()
            load_ds(i).wait()
            for j in range(cpp):
                rows = pl.ds(pl.multiple_of(i * rp + j * ACH, ACH), ACH)
                _, _, vn = _layernorm(va_buf[rows, :].astype(f32), g_ref[...], b_ref[...])
                ds = dsv_buf[rows, :].astype(f32)
                for gi in range(2):
                    sl = slice(gi * ACH, (gi + 1) * ACH)
                    dws_ref[gi] += _nt(ds[:, sl], vn[:, sl])
                    dbs_ref[gi] += ds[:, sl]
                for gi in range(2):
                    sl = slice(half + gi * ACH, half + (gi + 1) * ACH)
                    vn_col[gi, rows, :] = vn[:, sl]
                    ds_col[gi, rows, :] = ds[:, sl]
            return carry

        lax.fori_loop(0, MIX_PIECES, rows_piece, 0)

        def cols_step(cidx, carry):
            rows = pl.ds(cidx, ACH, stride=GW)
            for gi in range(2, 4):
                ds = ds_col[gi - 2, rows, :]
                dws_ref[gi] += _nt(ds, vn_col[gi - 2, rows, :])
                dbs_ref[gi] += ds
                ds_col[gi - 2, rows, :] = _dot(wst_ref[gi], ds)
            return carry

        lax.fori_loop(0, GW, cols_step, 0, unroll=8)

        def out_piece(i, carry):
            for j in range(cpp):
                rows = pl.ds(pl.multiple_of(i * rp + j * ACH, ACH), ACH)
                vhat, rstd, _ = _layernorm(va_buf[rows, :].astype(f32), g_ref[...], b_ref[...])
                ds = dsv_buf[rows, :].astype(f32)
                dvn = jnp.concatenate([_dot(wst_ref[0], ds[:, 0:ACH]), _dot(wst_ref[1], ds[:, ACH:half]),
                                       ds_col[0, rows, :], ds_col[1, rows, :]], axis=1)
                dg_ref[...] += jnp.sum(dvn * vhat, axis=0, keepdims=True)
                db_ref[...] += jnp.sum(dvn, axis=0, keepdims=True)
                dvh = dvn * g_ref[...]
                dva = rstd * (dvh - jnp.mean(dvh, axis=-1, keepdims=True) - vhat * jnp.mean(dvh * vhat, axis=-1, keepdims=True))
                dva_buf[rows, :] = dva.astype(bf16)
            store(i).start()
            return carry

        lax.fori_loop(0, MIX_PIECES, out_piece, 0)
        for i in range(MIX_PIECES):
            store(i).wait()

    vm = pl.BlockSpec(memory_space=pltpu.VMEM)
    hbm = pl.BlockSpec(memory_space=pl.ANY)
    dma = lambda: pltpu.SemaphoreType.DMA((MIX_PIECES,))
    return pl.pallas_call(
        body, name="mix_bwd",
        out_shape=(jax.ShapeDtypeStruct((l, AW), bf16), jax.ShapeDtypeStruct((4, ACH, ACH), f32), jax.ShapeDtypeStruct((4, ACH, ACH), f32),
                   jax.ShapeDtypeStruct((1, AW), f32), jax.ShapeDtypeStruct((1, AW), f32)),
        in_specs=[hbm, hbm, vm, vm, vm], out_specs=(hbm, vm, vm, vm, vm),
        scratch_shapes=[pltpu.VMEM((l, AW), bf16), pltpu.VMEM((l, AW), bf16), pltpu.VMEM((2, l, ACH), f32), pltpu.VMEM((2, l, ACH), f32),
                        pltpu.VMEM((l, AW), bf16), dma(), dma(), dma()],
        compiler_params=_cp(),
    )(p, dsv, ln_g, ln_b, ws_t)


def _mid(x, tgt, p, o_f, o_b, sv, gate, gf, gb_norm, w_pa, w_pb, w_out, tl):
    l = x.shape[0]

    def body(x_ref, t_ref, zb_ref, ua_ref, za_ref, g1_ref, g2_ref, of_ref, ob_ref, sv_ref, gate_ref, gf_ref, gbn_ref,
             wpa_ref, wpb_ref, wout_ref,
             dx1_ref, dzbua_ref, dzag_ref, dsv_ref, do_ref, dwout_bf, dwpa_bf, dwpb_bf, dgf_ref, dgate_ref, dgbn_ref, loss_ref,
             dwout_ref, dwpa_ref, dwpb_ref):
        @pl.when(pl.program_id(0) == 0)
        def _():
            for r in (dwout_ref, dwpa_ref, dwpb_ref, dgf_ref, dgate_ref, dgbn_ref, loss_ref):
                r[...] = jnp.zeros_like(r)

        o = of_ref[...].astype(f32) + ob_ref[...].astype(f32)
        rr = jnp.concatenate(
            [jnp.broadcast_to(lax.rsqrt(jnp.mean(o[:, h * HV:(h + 1) * HV] ** 2, axis=-1, keepdims=True) + EPS), (tl, HV))
             for h in range(NH)], axis=1)
        ohat = o * rr
        on = ohat * gbn_ref[...]
        szb, dszb = _silu_and_grad(zb_ref[...].astype(f32))
        tb = on * szb
        u = ua_ref[...].astype(f32)
        svv = sv_ref[...].astype(f32)
        sza, dsza = _silu_and_grad(za_ref[...].astype(f32))
        ta = u * svv * sza
        ya = _dot(ta, wpa_ref[...])
        yb = _dot(tb, wpb_ref[...])
        g1 = _sigmoid(g1_ref[...].astype(f32))
        g2 = _sigmoid(g2_ref[...].astype(f32))
        m = g1 * ya + g2 * yb
        y2 = _dot(m, wout_ref[...])
        x1 = x_ref[...] + gate_ref[...] * y2
        r1 = lax.rsqrt(jnp.mean(x1 * x1, axis=-1, keepdims=True) + EPS)
        x1n = x1 * r1
        err = x1n * gf_ref[...] - t_ref[...]
        loss_ref[...] += jnp.sum(jnp.sum(err * err, axis=-1, keepdims=True), axis=0, keepdims=True) * (0.5 / D)
        dout = err * (1.0 / D)
        dgf_ref[...] += jnp.sum(dout * x1n, axis=0, keepdims=True)
        dx1n = dout * gf_ref[...]
        dx1 = r1 * (dx1n - x1n * jnp.mean(dx1n * x1n, axis=-1, keepdims=True))
        dx1_ref[...] = dx1
        dgate_ref[...] += jnp.sum(dx1 * y2, axis=0, keepdims=True)
        dy2 = dx1 * gate_ref[...]
        dwout_ref[...] += _tn(m, dy2)
        dm = _nt(dy2, wout_ref[...])
        dya = dm * g1
        dyb = dm * g2
        dzag_ref[:, AW:AW + D] = (dm * ya * g1 * (1.0 - g1)).astype(bf16)
        dzag_ref[:, AW + D:] = (dm * yb * g2 * (1.0 - g2)).astype(bf16)
        dwpa_ref[...] += _tn(ta, dya)
        dta = _nt(dya, wpa_ref[...])
        dzbua_ref[:, AW:] = (dta * svv * sza).astype(bf16)
        dsv_ref[...] = (dta * u * sza).astype(bf16)
        dzag_ref[:, 0:AW] = (dta * u * svv * dsza).astype(bf16)
        dwpb_ref[...] += _tn(tb, dyb)
        dtb = _nt(dyb, wpb_ref[...])
        don = dtb * szb
        dzbua_ref[:, 0:AW] = (dtb * on * dszb).astype(bf16)
        dgbn_ref[...] += jnp.sum(don * ohat, axis=0, keepdims=True)
        doh = don * gbn_ref[...]
        prod = doh * ohat
        mh = jnp.concatenate(
            [jnp.broadcast_to(jnp.mean(prod[:, h * HV:(h + 1) * HV], axis=-1, keepdims=True), (tl, HV)) for h in range(NH)], axis=1)
        do_ref[...] = (rr * (doh - ohat * mh)).astype(bf16)

        @pl.when(pl.program_id(0) == l // tl - 1)
        def _():
            for acc, out in ((dwout_ref, dwout_bf), (dwpa_ref, dwpa_bf), (dwpb_ref, dwpb_bf)):
                out[...] = acc[...].astype(bf16)

    row = lambda w, j: pl.BlockSpec((tl, w), lambda i, j=j: (i, j))
    full = lambda *s: pl.BlockSpec(s, lambda i: (0,) * len(s))
    return pl.pallas_call(
        body, name="mid", grid=(l // tl,),
        out_shape=(jax.ShapeDtypeStruct((l, D), f32), jax.ShapeDtypeStruct((l, 2 * AW), bf16), jax.ShapeDtypeStruct((l, AW + 2 * D), bf16),
                   jax.ShapeDtypeStruct((l, AW), bf16), jax.ShapeDtypeStruct((l, VW), bf16),
                   jax.ShapeDtypeStruct((D, D), bf16), jax.ShapeDtypeStruct((AW, D), bf16), jax.ShapeDtypeStruct((VW, D), bf16),
                   jax.ShapeDtypeStruct((1, D), f32), jax.ShapeDtypeStruct((1, D), f32), jax.ShapeDtypeStruct((1, VW), f32),
                   jax.ShapeDtypeStruct((1, 1), f32)),
        scratch_shapes=[pltpu.VMEM((D, D), f32), pltpu.VMEM((AW, D), f32), pltpu.VMEM((VW, D), f32)],
        in_specs=[row(D, 0), row(D, 0), row(AW, PZB // AW), row(AW, PUA // AW), row(AW, PZA // AW), row(D, PG1 // D), row(D, PG2 // D),
                  row(VW, 0), row(VW, 0), row(AW, 0), full(1, D), full(1, D), full(1, VW), full(AW, D), full(VW, D), full(D, D)],
        out_specs=(row(D, 0), row(2 * AW, 0), row(AW + 2 * D, 0), row(AW, 0), row(VW, 0),
                   full(D, D), full(AW, D), full(VW, D), full(1, D), full(1, D), full(1, VW), full(1, 1)),
        compiler_params=_cp(("arbitrary",)),
    )(x, tgt, p, p, p, p, p, o_f, o_b, sv, gate, gf, gb_norm, w_pa, w_pb, w_out)


IN_DH_TILES = 10


def _in_dh(dqkv, dzbua, dva, dzag, dlr, w_pad, rows, tl, comm):
    c_in, c_out, c_sems = comm.specs()

    def body(*refs):
        cin = refs[10:10 + len(c_in)]
        outs = refs[10 + len(c_in):]
        comm.run(cin, outs[1:1 + comm.n], outs[1 + comm.n:], rows // tl, lambda: compute(*refs[:10], outs[0]))

    def compute(a_ref, b_ref, c_ref, e_ref, lr_ref, wa_ref, wb_ref, wc_ref, we_ref, wl_ref, dh_ref):
        dh_ref[...] = (_nt(a_ref[...], wa_ref[...]) + _nt(b_ref[...], wb_ref[...]) + _nt(c_ref[...], wc_ref[...])
                       + _nt(e_ref[...], we_ref[...]) + _nt(lr_ref[...], wl_ref[...])).astype(bf16)

    row = lambda w: pl.BlockSpec((tl, w), lambda i: (i, 0))
    wcol = lambda w, j: pl.BlockSpec((D, w), lambda i, j=j: (0, j))
    return pl.pallas_call(
        body, name="in_dh", grid=(rows // tl,), out_shape=(jax.ShapeDtypeStruct((rows, D), bf16),) + tuple(comm.outs),
        in_specs=[row(2 * KW + VW), row(2 * AW), row(AW), row(AW + 2 * D), row(LRW),
                  wcol(2 * KW + VW, 0), wcol(2 * AW, PZB // (2 * AW)), wcol(AW, PVA // AW), wcol(AW + 2 * D, PZA // (AW + 2 * D)),
                  wcol(LRW, PLR // LRW)] + c_in,
        out_specs=(row(D),) + tuple(c_out), scratch_shapes=c_sems, compiler_params=_cp(("arbitrary",)),
    )(dqkv, dzbua, dva, dzag, dlr, w_pad, w_pad, w_pad, w_pad, w_pad, *comm.ins)


def _in_bwd(x, dx1, dh_head, dqkv, dzbua, dva, dzag, dlr, w_pad, g, scale, tl):
    l = x.shape[0]
    na = dh_head.shape[0] // tl

    def body(x_ref, dx1_ref, dh_ref, a_ref, b_ref, c_ref, e_ref, lr_ref, wa_ref, wb_ref, wc_ref, we_ref, wl_ref, g_ref, sc_ref,
             gx_ref, dsh_ref, dsc_ref, dg_ref):
        @pl.when(pl.program_id(0) == 0)
        def _():
            for r in (dsh_ref, dsc_ref, dg_ref):
                r[...] = jnp.zeros_like(r)

        def through_norm(dh):
            xv = x_ref[...]
            r = lax.rsqrt(jnp.mean(xv * xv, axis=-1, keepdims=True) + EPS)
            xn = xv * r
            gxn = jnp.sum(dh * xn, axis=0, keepdims=True)
            dsh_ref[...] += jnp.sum(dh, axis=0, keepdims=True)
            dsc_ref[...] += gxn * g_ref[...]
            dg_ref[...] += gxn * (1.0 + sc_ref[...])
            dxn = dh * (g_ref[...] * (1.0 + sc_ref[...]))
            gx_ref[...] = dx1_ref[...] + r * (dxn - xn * jnp.mean(dxn * xn, axis=-1, keepdims=True))

        pl.when(pl.program_id(0) < na)(lambda: through_norm(dh_ref[...].astype(f32)))
        pl.when(pl.program_id(0) >= na)(lambda: through_norm(
            _nt(a_ref[...], wa_ref[...]) + _nt(b_ref[...], wb_ref[...]) + _nt(c_ref[...], wc_ref[...]) + _nt(e_ref[...], we_ref[...])
            + _nt(lr_ref[...], wl_ref[...])))

    row = lambda w: pl.BlockSpec((tl, w), lambda i: (i, 0))
    tail = lambda w: pl.BlockSpec((tl, w), lambda i: (jnp.maximum(i, na), 0))
    wcol = lambda w, j: pl.BlockSpec((D, w), lambda i, j=j: (0, j))
    vec = pl.BlockSpec((1, D), lambda i: (0, 0))
    return pl.pallas_call(
        body, name="in_bwd", grid=(l // tl,),
        out_shape=(jax.ShapeDtypeStruct((l, D), f32),) + (jax.ShapeDtypeStruct((1, D), f32),) * 3,
        in_specs=[row(D), row(D), pl.BlockSpec((tl, D), lambda i: (jnp.minimum(i, na - 1), 0)),
                  tail(2 * KW + VW), tail(2 * AW), tail(AW), tail(AW + 2 * D), tail(LRW),
                  wcol(2 * KW + VW, 0), wcol(2 * AW, PZB // (2 * AW)), wcol(AW, PVA // AW), wcol(AW + 2 * D, PZA // (AW + 2 * D)),
                  wcol(LRW, PLR // LRW), vec, vec],
        out_specs=(row(D), vec, vec, vec), compiler_params=_cp(("arbitrary",)),
    )(x, dx1, dh_head, dqkv, dzbua, dva, dzag, dlr, w_pad, w_pad, w_pad, w_pad, w_pad, g, scale)


def _tn_matmul(a, bs, tl, name, comm=None, pack=None):
    l, m = a.shape
    k = len(bs)
    comm = comm or _Comm([], [], None)
    c_in, c_out, c_sems = comm.specs()
    acc_shapes = [(m, b.shape[1]) for b in bs]
    n_extra = len(pack.extra) if pack else 0
    n_res = len(pack.outs) if pack else k

    def body(a_ref, *refs):
        b_refs, refs = refs[:k], refs[k:]
        extra, refs = refs[:n_extra], refs[n_extra:]
        cin, refs = refs[:len(c_in)], refs[len(c_in):]
        res, refs = refs[:n_res], refs[n_res:]
        cout, refs = refs[:comm.n], refs[comm.n:]
        accs, sems = (refs[:k], refs[k:]) if pack else (res, refs)

        def compute():
            @pl.when(pl.program_id(0) == 0)
            def _():
                for o_ref in accs:
                    o_ref[...] = jnp.zeros_like(o_ref)

            av = a_ref[...]
            for b_ref, o_ref in zip(b_refs, accs):
                o_ref[...] += _tn(av, b_ref[...])
            if pack:
                pl.when(pl.program_id(0) == l // tl - 1)(lambda: pack.fn(accs, extra, res))

        comm.run(cin, cout, sems, l // tl, compute)

    full = lambda shape: pl.BlockSpec(shape, lambda i: (0,) * len(shape))
    res_shapes = list(pack.outs) if pack else [jax.ShapeDtypeStruct(sh, f32) for sh in acc_shapes]
    return pl.pallas_call(
        body, name=name, grid=(l // tl,), out_shape=tuple(res_shapes) + tuple(comm.outs),
        in_specs=[pl.BlockSpec((tl, m), lambda i: (i, 0))] + [pl.BlockSpec((tl, b.shape[1]), lambda i: (i, 0)) for b in bs]
        + [full(e.shape) for e in (pack.extra if pack else [])] + c_in,
        out_specs=tuple(full(r.shape) for r in res_shapes) + tuple(c_out),
        scratch_shapes=([pltpu.VMEM(sh, f32) for sh in acc_shapes] if pack else []) + c_sems, compiler_params=_cp(("arbitrary",)),
    )(a, *bs, *(pack.extra if pack else []), *comm.ins)


def _pad_gate(w2, gb):
    z = jnp.zeros((RANK, KW), f32)
    tail = jnp.zeros((LRW - 2 * RANK, KW), f32)
    w2f = jnp.concatenate([w2[0], z, tail], axis=0)
    w2b = jnp.concatenate([z, w2[1], tail], axis=0)
    return w2f, w2b, gb[0:1], gb[1:2]


EARLY_A_ROWS = 512


class _NoExchange:
    def __init__(self, w_pa, w_pb, w_out):
        self.weights = (w_pa, w_pb, w_out)

    def gather_proj(self):
        return _Comm([], [], None)

    def proj_weights(self, got):
        return self.weights

    def first(self, dw_out, dw_pa, dw_pb, small):
        return _Comm([], [], None)

    def early_a(self, blocks):
        return _Comm([], [], None)

    def early_b(self, blocks):
        return _Comm([], [], None)

    def late(self, blocks, dgt):
        return _Comm([], [], None)


class _Exchanges:
    def __init__(self, pa, pb, wo):
        self.shards = (pa, pb, wo)

    def gather_proj(self):
        def plan(i, o):
            srcs = [lambda j, r=r: r for r in i]
            dsts = [lambda j: o[0].at[:, _lanes(j)], lambda j: o[1].at[:, _lanes(j)], lambda j: o[2].at[j]]
            return srcs, dsts, None
        sds = jax.ShapeDtypeStruct
        return _Comm(self.shards, [sds((AW, D), bf16), sds((VW, D), bf16), sds((NDEV, 128, D), bf16)], plan)

    def proj_weights(self, got):
        return got[0], got[1], got[2].reshape(D, D)

    def first(self, dw_out, dw_pa, dw_pb, small):
        def plan(i, o):
            srcs = [lambda j: i[0].at[j], lambda j: i[1].at[:, _lanes(j)], lambda j: i[2].at[:, _lanes(j)], lambda j: i[3]]
            dsts = [lambda j, r=r: r.at[j] for r in o]
            return srcs, dsts, None
        sds = jax.ShapeDtypeStruct
        return _Comm([dw_out.reshape(NDEV, 128, D), dw_pa, dw_pb, small],
                     [sds((NDEV, 128, D), bf16), sds((NDEV, AW, 128), bf16), sds((NDEV, VW, 128), bf16),
                      sds((NDEV,) + small.shape, f32)], plan)

    def early_a(self, blocks):
        def plan(i, o):
            return [lambda j: i[0].at[j]], [lambda j: o[0].at[j]], [lambda j: j >= LATE_DESTS - 1]
        return _Comm([blocks], [jax.ShapeDtypeStruct(blocks.shape, bf16)], plan)

    def early_b(self, blocks):
        def plan(i, o):
            return [lambda j: i[0].at[j]], [lambda j: o[0].at[j]], [lambda j: j >= LATE_DESTS - 1]
        return _Comm([blocks], [jax.ShapeDtypeStruct(blocks.shape, bf16)], plan)

    def late(self, blocks, dgt):
        return _LateComm(blocks, dgt)


def _local_step(x, ctx, tgt, mod, modc, norm_g, w_pad, ln_g, ln_b, ws, bs, w2, gb, gb_norm, gf, xch):
    shift, scale, gate = mod[:, 0:D], mod[:, D:2 * D], mod[:, 2 * D:]
    shift_c, scale_c = modc[:, 0:D], modc[:, D:]
    w2f, w2b, gbf, gbb = _pad_gate(w2, gb)

    p, h, *got_proj = _in_proj(x, norm_g, scale, shift, w_pad, 512, xch.gather_proj())
    w_pa, w_pb, w_out = xch.proj_weights(got_proj)
    sc_f, sc_b = _ctx_fwd(ctx, norm_g, scale_c, shift_c, w_pad, w2f, w2b, gbf, gbb)
    o_f, st_f = _gla_fwd(p, w2f, gbf, sc_f, False, 512, "gla_fwd_f")
    o_b, st_b = _gla_fwd(p, w2b, gbb, sc_b, True, 512, "gla_fwd_b")
    sv = _mix_fwd(p, ln_g, ln_b, ws.astype(bf16), bs.T)
    (dx1, dzbua, dzag, dsv, do, dw_out, dw_pa, dw_pb, dgf, dgate, dgbn, loss) = _mid(
        x, tgt, p, o_f, o_b, sv, gate, gf, gb_norm, w_pa, w_pb, w_out, 256)
    dva, dws, dbs_acc, dln_g, dln_b = _mix_bwd(p, dsv, ln_g, ln_b, jnp.swapaxes(ws, 1, 2).astype(bf16))
    small = _rows128(dln_g, dln_b, dws, jnp.sum(dbs_acc, axis=-1), dgbn, dgf, jnp.broadcast_to(loss, (1, 128)))
    blocks_a, blocks_b, *got_first = _tn_matmul(h, [dzbua, dva, dzag], 1024, "dw_early", xch.first(dw_out, dw_pa, dw_pb, small),
                                               _pack_early())

    dqkv_f, dlr_f, dw2f, dgbf, dsc_f, *got_a = _gla_bwd(p, do, st_f, w2f, gbf, None, False, 512, "gla_bwd_f",
                                                        xch.early_a(blocks_a))
    dqkv, dlr, dw2b, dgbb, dsc_b, *got_b = _gla_bwd(p, do, st_b, w2b, gbb, (dqkv_f, dlr_f), True, 512, "gla_bwd_b",
                                                    xch.early_b(blocks_b))
    dwk_c, dwv_c, dwl_c, dmodc, dg_c, dw2c, dgbc = _ctx_bwd(ctx, norm_g, scale_c, shift_c, w_pad, w2f, w2b, gbf, gbb, dsc_f, dsc_b)
    (blocks_late,) = _tn_matmul(h, [dqkv, dlr], 1024, "dw_qkv_lr", pack=_pack_late(dwk_c, dwv_c, dwl_c))
    dw2 = jnp.stack([dw2f[0:RANK] + dw2c[0, 0:RANK], dw2b[RANK:2 * RANK] + dw2c[1, RANK:2 * RANK]])
    dgb = jnp.concatenate([dgbf + dgbc[0], dgbb + dgbc[1]], axis=0)
    dgt = jnp.concatenate([jnp.transpose(dw2.reshape(2, RANK, NDEV, 32), (2, 0, 1, 3)).reshape(NDEV, 2 * RANK * 32),
                           jnp.transpose(dgb.reshape(2, NDEV, 32), (1, 0, 2)).reshape(NDEV, 64),
                           jnp.zeros((NDEV, 64), f32)], axis=1).reshape(NDEV, 9, 128)
    dh_head, *got_late = _in_dh(dqkv, dzbua, dva, dzag, dlr, w_pad, IN_DH_TILES * 512, 512, xch.late(blocks_late, dgt))
    gx, dshift, dscale, dg = _in_bwd(x, dx1, dh_head, dqkv, dzbua, dva, dzag, dlr, w_pad, norm_g, scale, 512)
    return dict(loss=loss, gx=gx, dmod=jnp.concatenate([dshift, dscale, dgate], axis=1), dmodc=dmodc, dnorm_g=dg + dg_c,
                small=small, blocks_a=blocks_a, blocks_b=blocks_b, blocks_late=blocks_late, dw2=dw2, dgb=dgb,
                dw_pa=dw_pa, dw_pb=dw_pb, dw_out=dw_out, got_first=got_first, got_a=got_a, got_b=got_b, got_late=got_late)


def _rows128(*vs):
    out = []
    for t in vs:
        t = t.reshape(-1)
        pad = (-t.shape[0]) % 128
        out.append(jnp.pad(t, (0, pad)) if pad else t)
    return jnp.concatenate(out).reshape(-1, 128)


def kernel(x, c, ctx, c_ctx, w_mod, b_mod, norm_g, w_in, a_ln_g, a_ln_b, a_ws, a_bs, b_gate_w2, b_gate_b, b_norm_g, w_proj_a, w_proj_b, w_out, final_norm_g, loss_target, m_c_ctx, m_w_mod, m_b_mod, m_norm_g, m_w_in, m_a_ln_g, m_a_ln_b, m_a_ws, m_a_bs, m_b_gate_w2, m_b_gate_b, m_b_norm_g, m_w_proj_a, m_w_proj_b, m_w_out, m_final_norm_g, v_c_ctx, v_w_mod, v_b_mod, v_norm_g, v_w_in, v_a_ln_g, v_a_ln_b, v_a_ws, v_a_bs, v_b_gate_w2, v_b_gate_b, v_b_norm_g, v_w_proj_a, v_w_proj_b, v_w_out, v_final_norm_g):
    me = _me()
    ncol = w_mod.shape[2]

    gate_mine = _rows128(jnp.concatenate([b_gate_w2.reshape(-1), b_gate_b.reshape(-1)]))
    bm_mine = lax.dynamic_slice(b_mod, (0, me * ncol), (1, ncol))
    cs, mods, wg, gates = _gather_first(c, c_ctx.reshape(1, D), w_mod[0], bm_mine, w_in[0].astype(bf16), gate_mine)
    w_pad = _repack_w(wg)
    gflat = gates.reshape(NDEV, 9 * 128)
    w2 = jnp.transpose(gflat[:, 0:2 * RANK * 32].reshape(NDEV, 2, RANK, 32), (1, 2, 0, 3)).reshape(2, RANK, KW)
    gb = jnp.transpose(gflat[:, 2 * RANK * 32:2 * RANK * 32 + 64].reshape(NDEV, 2, 32), (1, 0, 2)).reshape(2, KW)

    mods = jnp.transpose(mods, (1, 0, 2)).reshape(16, 3 * D)
    mod = lax.dynamic_slice(mods, (me, 0), (1, 3 * D))
    modc = mods[8:9, 0:2 * D]

    xch = _Exchanges(w_proj_a[0].astype(bf16), w_proj_b[0].astype(bf16), w_out[0].astype(bf16))
    r = _local_step(x[0], ctx[0], loss_target[0], mod, modc, norm_g, w_pad, a_ln_g, a_ln_b, a_ws[0], a_bs[0], w2, gb,
                    b_norm_g, final_norm_g.reshape(1, D), xch)
    p_out, p_pa, p_pb, smalls_e = r["got_first"]
    (p_in_a,) = r["got_a"]
    (p_in_b,) = r["got_b"]
    _, _, p_in_late, p_gt = r["got_late"]

    n_e = (AW + AW + 4 * ACH * ACH + AW + VW + D) // 128
    row = lambda t: t.reshape(1, D)
    rep_e = _adam_params(smalls_e, [a_ln_g, a_ln_b, a_ws, a_bs, b_norm_g, row(final_norm_g)],
                         [m_a_ln_g, m_a_ln_b, m_a_ws, m_a_bs, m_b_norm_g, row(m_final_norm_g)],
                         [v_a_ln_g, v_a_ln_b, v_a_ws, v_a_bs, v_b_norm_g, row(v_final_norm_g)], "adam_rep_early")
    rep_e = [t[0:5] + (t[5].reshape(D),) for t in rep_e]
    losses = smalls_e[:, n_e, 0]
    loss = losses[0]
    for i in range(1, NDEV):
        loss = loss + losses[i]

    dbm = r["dmod"] + jnp.concatenate([r["dmodc"], jnp.zeros((1, D), f32)], axis=1)
    wm, ng, bmod_r, cc = _mod_tail(r["dnorm_g"], dbm, r["dmod"], r["dmodc"], cs, w_mod, m_w_mod, v_w_mod,
                                   norm_g, m_norm_g, v_norm_g, b_mod, m_b_mod, v_b_mod,
                                   c_ctx.reshape(1, D), m_c_ctx.reshape(1, D), v_c_ctx.reshape(1, D))

    a_in = _adam_w_in(p_in_a, p_in_b, p_in_late, w_in, m_w_in, v_w_in)
    blk = _adam_blocks([p_pa, p_pb, p_out], [w_proj_a, w_proj_b, w_out], [m_w_proj_a, m_w_proj_b, m_w_out],
                       [v_w_proj_a, v_w_proj_b, v_w_out], "adam_proj_out")
    a_pa, a_pb, a_out = ([t[i] for t in blk] for i in range(3))
    gate_m = _rows128(jnp.concatenate([m_b_gate_w2.reshape(-1), m_b_gate_b.reshape(-1)]))
    gate_v = _rows128(jnp.concatenate([v_b_gate_w2.reshape(-1), v_b_gate_b.reshape(-1)]))
    a_gt = [t.reshape(-1) for t in _adam(p_gt, gate_mine, gate_m, gate_v, 9, "adam_gate")]
    nw2 = 2 * RANK * 32
    sh = [(a_in[k], a_pa[k], a_pb[k], a_out[k], a_gt[k][0:nw2].reshape(1, 2, RANK, 32),
           a_gt[k][nw2:nw2 + 64].reshape(1, 2, 32)) for k in range(4)]

    outs = [loss, r["gx"][None]]
    for k in range(4):
        lg, lb, aws, abs_, bng, fng = rep_e[k]
        n_g, bmod = ng[k], bmod_r[k]
        s_in, s_pa, s_pb, s_out, s_w2, s_gb = sh[k]
        outs += [cc[k].reshape(D), wm[k], bmod, n_g, s_in, lg, lb, aws, abs_, s_w2, s_gb, bng, s_pa, s_pb, s_out, fng]
    return tuple(outs)
```

```python
import jax
import jax.numpy as jnp
from jax import lax
from jax.experimental import pallas as pl
from jax.experimental.pallas import tpu as pltpu

f32, bf16 = jnp.float32, jnp.bfloat16

D = 1024
CTX = 256
EPS = 1e-6
AW = 512
ACH = 128
GW = 64
KW = 256
VW = 512
NH = 4
HK = 64
HV = 128
RANK = 16
TAU = 16.0
CH = 64
QSCALE = HK ** -0.5
INW = 5152
NDEV = 8

PQ, PK, PV, PZB, PUA, PVA, PZA, PG1, PG2, PLR, PW = 0, 256, 512, 1024, 1536, 2048, 2560, 3072, 4096, 5120, 5248
LRW = 128

ADAM_LR, ADAM_B1, ADAM_B2, ADAM_EPS, ADAM_WD, ADAM_STEP = 0.001, 0.9, 0.999, 1e-08, 0.01, 10

VMEM_LIMIT = 56 * 1024 * 1024
MESH = pl.DeviceIdType.MESH


def _cp(sem=None):
    return pltpu.CompilerParams(dimension_semantics=sem, vmem_limit_bytes=VMEM_LIMIT)


def _dot(a, b):
    return jnp.dot(a.astype(bf16), b.astype(bf16), preferred_element_type=f32)


def _nt(a, b):
    return lax.dot_general(a.astype(bf16), b.astype(bf16), (((1,), (1,)), ((), ())), preferred_element_type=f32)


def _tn(a, b):
    return lax.dot_general(a.astype(bf16), b.astype(bf16), (((0,), (0,)), ((), ())), preferred_element_type=f32)


def _dot_hi(a, b):
    return jnp.dot(a, b, preferred_element_type=f32, precision=lax.Precision.HIGHEST)


def _sigmoid(x):
    return 1.0 / (1.0 + jnp.exp(-x))


def _log_sigmoid(x):
    return jnp.minimum(x, 0.0) - jnp.log(1.0 + jnp.exp(-jnp.abs(x)))


def _silu_and_grad(z):
    s = _sigmoid(z)
    return z * s, s * (1.0 + z * (1.0 - s))


def _me():
    return 4 * lax.axis_index("x") + 2 * lax.axis_index("y") + lax.axis_index("c")


def _peer(k):
    x, y, c = lax.axis_index("x"), lax.axis_index("y"), lax.axis_index("c")
    px = 1 - x if k & 4 else x
    py = 1 - y if k & 2 else y
    pc = 1 - c if k & 1 else c
    return (px, py, pc), 4 * px + 2 * py + pc


def _fanout(srcs, dsts, send_sems, recv_sems, local_sems, owners=None):
    me = _me()
    n = len(srcs)
    owns = lambda a, j: True if owners is None or owners[a] is None else owners[a](j)

    def guarded(cond, fn):
        if cond is True:
            fn()
        else:
            pl.when(cond)(fn)

    def copies(with_recvs):
        local = [pltpu.make_async_copy(srcs[a](me), dsts[a](me), local_sems.at[a]) for a in range(n)]
        sends, recvs = [], []
        for k in range(1, NDEV):
            dev, idx = _peer(k)
            for a in range(n):
                s = (k - 1) * n + a
                sends.append((owns(a, idx), pltpu.make_async_remote_copy(
                    src_ref=srcs[a](idx), dst_ref=dsts[a](me), send_sem=send_sems.at[s], recv_sem=recv_sems.at[s],
                    device_id=dev, device_id_type=MESH)))
                if with_recvs:
                    recvs.append((owns(a, me), pltpu.make_async_remote_copy(
                        src_ref=srcs[a](idx), dst_ref=dsts[a](idx), send_sem=send_sems.at[s], recv_sem=recv_sems.at[s],
                        device_id=dev, device_id_type=MESH)))
        return local, sends, recvs

    def start():
        local, sends, _ = copies(False)
        for a, cp in enumerate(local):
            guarded(owns(a, me), cp.start)
        for cond, cp in sends:
            guarded(cond, cp.start)

    def finish():
        local, sends, recvs = copies(True)
        for cond, cp in recvs:
            guarded(cond, cp.wait_recv)
        for cond, cp in sends:
            guarded(cond, cp.wait_send)
        for a, cp in enumerate(local):
            guarded(owns(a, me), cp.wait)

    return start, finish


class _Comm:
    def __init__(self, ins, outs, plan):
        self.ins, self.outs, self.plan = list(ins), list(outs), plan
        self.n = len(self.outs)

    def specs(self):
        hbm = pl.BlockSpec(memory_space=pl.ANY)
        return [hbm] * len(self.ins), [hbm] * self.n, _fanout_sems(self.n) if self.n else []

    def run(self, in_refs, out_refs, sems, nsteps, compute):
        if not self.n:
            compute()
            return

        def hooks():
            srcs, dsts, owners = self.plan(in_refs, out_refs)
            return _fanout(srcs, dsts, sems[0], sems[1], sems[2], owners)

        pl.when(pl.program_id(0) == 0)(lambda: hooks()[0]())
        compute()
        pl.when(pl.program_id(0) == nsteps - 1)(lambda: hooks()[1]())


LATE_MID_STEP = 1


class _LateComm:
    def __init__(self, blocks, dgt):
        sds = jax.ShapeDtypeStruct
        self.ins = [blocks, dgt]
        self.outs = [sds((D, SHARD), bf16), sds((D, SHARD), bf16), sds((4, D, SHARD), bf16), sds(dgt.shape, f32)]
        self.n = len(self.outs)

    def specs(self):
        hbm = pl.BlockSpec(memory_space=pl.ANY)
        scratch = [pltpu.VMEM((3, D, SHARD), bf16), pltpu.SemaphoreType.DMA((2,)), pltpu.SemaphoreType.DMA((4,)),
                   pltpu.SemaphoreType.DMA((3,))] + _fanout_sems(1)
        return [hbm] * 2, [hbm] * self.n, scratch

    def run(self, in_refs, out_refs, scratch, nsteps, compute):
        late_ref, dgt_ref = in_refs
        sib_ref, pair_ref, parts_ref, gt_ref = out_refs
        vbuf, send_sems, recv_sems, local_sems, g_send, g_recv, g_local = scratch
        x, y, c = lax.axis_index("x"), lax.axis_index("y"), lax.axis_index("c")
        chip = 2 * x + y
        is_owner_chip = chip == 0
        step = pl.program_id(0)

        def to_sibling():
            return pltpu.make_async_remote_copy(src_ref=late_ref.at[1 - c], dst_ref=sib_ref, send_sem=send_sems.at[0],
                                                recv_sem=recv_sems.at[0], device_id=(x, y, 1 - c), device_id_type=MESH)

        def to_owner(k):
            return pltpu.make_async_remote_copy(src_ref=pair_ref, dst_ref=parts_ref.at[k], send_sem=send_sems.at[1],
                                                recv_sem=recv_sems.at[k], device_id=(0, 0, c), device_id_type=MESH)

        def own_copy():
            return pltpu.make_async_copy(pair_ref, parts_ref.at[0], local_sems.at[2])

        def gates():
            return _fanout([lambda j: dgt_ref.at[j]], [lambda j: gt_ref.at[j]], g_send, g_recv, g_local)

        @pl.when(step == 0)
        def _():
            to_sibling().start()
            gates()[0]()

        compute()

        @pl.when(step == LATE_MID_STEP)
        def _():
            mine = pltpu.make_async_copy(late_ref.at[c], vbuf.at[0], local_sems.at[0])
            mine.start()
            to_sibling().wait_recv()
            theirs = pltpu.make_async_copy(sib_ref, vbuf.at[1], local_sems.at[1])
            theirs.start()
            mine.wait()
            theirs.wait()
            vbuf[2] = (vbuf[0].astype(f32) + vbuf[1].astype(f32)).astype(bf16)
            pltpu.sync_copy(vbuf.at[2], pair_ref)
            pl.when(is_owner_chip)(lambda: own_copy().start())
            pl.when(jnp.logical_not(is_owner_chip))(lambda: to_owner(chip).start())

        @pl.when(step == nsteps - 1)
        def _():
            @pl.when(is_owner_chip)
            def _():
                for k in range(1, 4):
                    to_owner(k).wait_recv()
                own_copy().wait()

            pl.when(jnp.logical_not(is_owner_chip))(lambda: to_owner(chip).wait_send())
            to_sibling().wait_send()
            gates()[1]()


def _fanout_sems(n):
    return [pltpu.SemaphoreType.DMA(((NDEV - 1) * n,)), pltpu.SemaphoreType.DMA(((NDEV - 1) * n,)), pltpu.SemaphoreType.DMA((n,))]


def _lanes(j):
    return pl.ds(pl.multiple_of(j * 128, 128), 128)


def _gather_first(c_row, cctx_row, wm, bm, wi, gate):
    def body(c_ref, cctx_ref, wm_ref, bm_ref, wi_ref, g_ref, cs_ref, mods_ref, owi, og, call_ref, mine_ref,
             send_sems, recv_sems, local_sems, c_send, c_recv, c_local, m_send, m_recv, m_local):
        x, y, c = lax.axis_index("x"), lax.axis_index("y"), lax.axis_index("c")
        sibling = (x, y, 1 - c)
        chips = [(1 - x, y), (x, 1 - y), (1 - x, 1 - y)]
        index = lambda px, py, pc: 4 * px + 2 * py + pc
        arrays = ((wi_ref, owi), (g_ref, og))
        n = len(arrays)

        def copy(a, k, block, to, own=False):
            src, out = arrays[a]
            return pltpu.make_async_remote_copy(
                src_ref=src if own else out.at[index(*block)], dst_ref=out.at[index(*block)],
                send_sem=send_sems.at[k * n + a], recv_sem=recv_sems.at[k * n + a], device_id=to, device_id_type=MESH)

        c_start, c_finish = _fanout([lambda j: c_ref], [lambda j: call_ref.at[j]], c_send, c_recv, c_local)
        c_start()
        mine = [pltpu.make_async_copy(src, out.at[index(x, y, c)], local_sems.at[a]) for a, (src, out) in enumerate(arrays)]
        first = [copy(a, 0, (x, y, c), sibling, own=True) for a in range(n)]
        first += [copy(a, 1 + j, (x, y, c), (*chip, c), own=True) for j, chip in enumerate(chips) for a in range(n)]
        for cp in mine + first:
            cp.start()

        c_finish()
        cs = jnp.concatenate([call_ref[j] for j in range(NDEV)] + [cctx_ref[...], jnp.zeros((16 - NDEV - 1, D), f32)], axis=0)
        cs_ref[...] = cs
        s, _ = _silu_and_grad(cs)
        mine_ref[...] = _dot_hi(s, wm_ref[...]) + bm_ref[...]
        m_start, m_finish = _fanout([lambda j: mine_ref], [lambda j: mods_ref.at[j]], m_send, m_recv, m_local)
        m_start()

        passed = []
        for j, chip in enumerate(chips):
            for a in range(n):
                copy(a, 1 + j, (*chip, c), (x, y, c)).wait_recv()
            for a in range(n):
                cp = copy(a, 4 + j, (*chip, c), sibling)
                cp.start()
                passed.append(cp)
        for a in range(n):
            copy(a, 0, sibling, (x, y, c)).wait_recv()
        for j, chip in enumerate(chips):
            for a in range(n):
                copy(a, 4 + j, (*chip, 1 - c), (x, y, c)).wait_recv()
        for cp in first + passed:
            cp.wait_send()
        for cp in mine:
            cp.wait()
        m_finish()

    hbm = pl.BlockSpec(memory_space=pl.ANY)
    vm = pl.BlockSpec(memory_space=pltpu.VMEM)
    ncol = wm.shape[1]
    return pl.pallas_call(
        body, name="gather_first",
        out_shape=(jax.ShapeDtypeStruct((16, D), f32), jax.ShapeDtypeStruct((NDEV, 16, ncol), f32),
                   jax.ShapeDtypeStruct((NDEV,) + wi.shape, bf16), jax.ShapeDtypeStruct((NDEV,) + gate.shape, f32)),
        in_specs=[vm, vm, vm, vm, hbm, hbm], out_specs=(vm, vm, hbm, hbm),
        scratch_shapes=[pltpu.VMEM((NDEV, 1, D), f32), pltpu.VMEM((16, ncol), f32)] + _fanout_sems(2) + _fanout_sems(1) + _fanout_sems(1),
        compiler_params=_cp(),
    )(c_row, cctx_row, wm, bm, wi, gate)


SHARD = INW // NDEV
ROWS_RP = 128


def _overlap(lo, hi, a, b):
    s, e = max(lo, a), min(hi, b)
    return (s, e) if s < e else None


def _repack_w(wg):
    segs = ((0, 1024, PQ), (1024, 1024 + 2 * RANK, PLR), (1024 + 2 * RANK, INW, PZB))

    def body(g_ref, o_ref):
        for j in range(NDEV):
            lo, hi = j * SHARD, (j + 1) * SHARD
            for a, b, pad0 in segs:
                ov = _overlap(lo, hi, a, b)
                if ov:
                    s, e = ov
                    o_ref[:, pad0 + s - a:pad0 + e - a] = g_ref[j, :, s - lo:e - lo]
        o_ref[:, PLR + 2 * RANK:PW] = jnp.zeros((ROWS_RP, PW - PLR - 2 * RANK), bf16)

    return pl.pallas_call(
        body, name="repack_w", grid=(D // ROWS_RP,), out_shape=jax.ShapeDtypeStruct((D, PW), bf16),
        in_specs=[pl.BlockSpec((NDEV, ROWS_RP, SHARD), lambda i: (0, i, 0))],
        out_specs=pl.BlockSpec((ROWS_RP, PW), lambda i: (i, 0)), compiler_params=_cp(("arbitrary",)),
    )(wg)


LATE_END = 1024 + 2 * RANK
LATE_DESTS = 2


def _pack_blocks(o_ref, srcs, dests, dtype):
    for n, j in enumerate(dests):
        lo, hi = j * SHARD, (j + 1) * SHARD
        done = lo
        for a, b, src in srcs:
            ov = _overlap(lo, hi, a, b)
            if ov:
                s, e = ov
                if s > done:
                    o_ref[n, :, done - lo:s - lo] = jnp.zeros((ROWS_RP, s - done), dtype)
                o_ref[n, :, s - lo:e - lo] = src[:, s - a:e - a].astype(dtype)
                done = e
        if done < hi:
            o_ref[n, :, done - lo:hi - lo] = jnp.zeros((ROWS_RP, hi - done), dtype)


class _Pack:
    def __init__(self, extra, outs, fn):
        self.extra, self.outs, self.fn = list(extra), list(outs), fn


def _pack_early():
    def fn(accs, extra, outs):
        zbua, va, zag = accs
        for r0 in range(0, D, ROWS_RP):
            rows = pl.ds(r0, ROWS_RP)
            out, o0 = (outs[0], r0) if r0 < EARLY_A_ROWS else (outs[1], r0 - EARLY_A_ROWS)
            _pack_blocks(out.at[:, pl.ds(o0, ROWS_RP)], ((LATE_END, 2080, zbua.at[rows]), (2080, 2592, va.at[rows]), (2592, INW, zag.at[rows])),
                         range(NDEV), bf16)

    sds = jax.ShapeDtypeStruct
    return _Pack([], [sds((NDEV, EARLY_A_ROWS, SHARD), bf16), sds((NDEV, D - EARLY_A_ROWS, SHARD), bf16)], fn)


def _pack_late(dwk_c, dwv_c, dwl_c):
    def fn(accs, extra, outs):
        qkv_ref, lr_ref = accs
        kc_ref, vc_ref, lc_ref = extra
        for r0 in range(0, D, ROWS_RP):
            rows = pl.ds(r0, ROWS_RP)
            qkv = qkv_ref[rows, :] + jnp.concatenate([jnp.zeros((ROWS_RP, KW), f32), kc_ref[rows, :], vc_ref[rows, :]], axis=1)
            lr = lr_ref[rows, :] + lc_ref[rows, :]
            _pack_blocks(outs[0].at[:, rows], ((0, 1024, qkv), (1024, LATE_END, lr)), range(LATE_DESTS), bf16)

    return _Pack([dwk_c, dwv_c, dwl_c], [jax.ShapeDtypeStruct((LATE_DESTS, D, SHARD), bf16)], fn)


def _mod_tail(dnorm, dbm, dmod, dmodc, cs, wm):
    ncol = wm.shape[2]

    def body(dn_ref, dbm_ref, dmod_ref, dmodc_ref, cs_ref, wm_ref, g_wm, g_ng, g_bm, g_cc, *rest):
        a_dn, a_dbm, a_dmod, a_dmodc, dm_rows, gc_ref, a_gc = rest[0:7]
        s1, r1, l1, s2, r2, l2 = rest[7:13]
        start, finish = _fanout([lambda j: dn_ref, lambda j: dbm_ref, lambda j: dmod_ref, lambda j: dmodc_ref],
                                [lambda j, r=r: r.at[j] for r in (a_dn, a_dbm, a_dmod, a_dmodc)], s1, r1, l1)
        start()
        finish()

        def total(ref):
            t = ref[0]
            for j in range(1, NDEV):
                t = t + ref[j]
            return t

        g_ng[...] = total(a_dn)
        g_bm[...] = total(a_dbm)
        dmodc_tot = jnp.concatenate([total(a_dmodc), jnp.zeros((1, D), f32)], axis=1)
        dm_rows[...] = jnp.concatenate([a_dmod[j] for j in range(NDEV)] + [dmodc_tot, jnp.zeros((16 - NDEV - 1, 3 * D), f32)], axis=0)
        dm = dm_rows[:, pl.ds(pl.multiple_of(_me() * ncol, 128), ncol)]
        s, ds = _silu_and_grad(cs_ref[...])
        part = lax.dot_general(dm[8:9, :], wm_ref[0], (((1,), (1,)), ((), ())), preferred_element_type=f32,
                               precision=lax.Precision.HIGHEST)
        gc_ref[...] = part * ds[8:9, :]
        start2, finish2 = _fanout([lambda j: gc_ref], [lambda j: a_gc.at[j]], s2, r2, l2)
        start2()
        g = lax.dot_general(s, dm, (((0,), (0,)), ((), ())), preferred_element_type=f32, precision=lax.Precision.HIGHEST)
        g_wm[...] = g[None]
        finish2()
        g_cc[...] = total(a_gc)

    sds = jax.ShapeDtypeStruct
    row = lambda n: pltpu.VMEM((NDEV, 1, n), f32)
    return pl.pallas_call(
        body, name="mod_tail", out_shape=(sds(wm.shape, f32), sds((1, D), f32), sds((1, 3 * D), f32), sds((1, D), f32)),
        scratch_shapes=[row(D), row(3 * D), row(3 * D), row(2 * D), pltpu.VMEM((16, 3 * D), f32), pltpu.VMEM((1, D), f32), row(D)]
        + _fanout_sems(4) + _fanout_sems(1),
        compiler_params=_cp(),
    )(dnorm, dbm, dmod, dmodc, cs, wm)


def _adam_update(g, w_ref, m_ref, v_ref, go_ref, d_ref, mo_ref, vo_ref):
    c1 = 1.0 / (1.0 - ADAM_B1 ** ADAM_STEP)
    c2 = 1.0 / (1.0 - ADAM_B2 ** ADAM_STEP)
    mn = ADAM_B1 * m_ref[...] + (1.0 - ADAM_B1) * g
    vn = ADAM_B2 * v_ref[...] + (1.0 - ADAM_B2) * (g * g)
    go_ref[...] = g
    mo_ref[...] = mn
    vo_ref[...] = vn
    d_ref[...] = -ADAM_LR * ((mn * c1) / (jnp.sqrt(vn * c2) + ADAM_EPS) + ADAM_WD * w_ref[...])


def _adam_w_in(parts_a, parts_b, parts_late, w, m, v):
    na = EARLY_A_ROWS // ROWS_RP

    def body(a_ref, b_ref, l_ref, w_ref, m_ref, v_ref, go_ref, d_ref, mo_ref, vo_ref):
        me = _me()
        first = pl.program_id(0) < na
        early = jnp.where(first, a_ref[0], b_ref[0]).astype(f32)
        for i in range(1, NDEV):
            early = early + jnp.where(first, a_ref[i], b_ref[i]).astype(f32)
        late = l_ref[0].astype(f32)
        for i in range(1, 4):
            late = late + l_ref[i].astype(f32)
        g = jnp.where(me >= LATE_DESTS - 1, early, 0.0) + jnp.where(me < LATE_DESTS, late, 0.0)
        _adam_update(g, w_ref, m_ref, v_ref, go_ref, d_ref, mo_ref, vo_ref)

    blk = pl.BlockSpec((None, ROWS_RP, SHARD), lambda i: (0, i, 0))
    return pl.pallas_call(
        body, name="adam_w_in", grid=(D // ROWS_RP,), out_shape=tuple(jax.ShapeDtypeStruct((1, D, SHARD), f32) for _ in range(4)),
        in_specs=[pl.BlockSpec((NDEV, ROWS_RP, SHARD), lambda i: (0, jnp.minimum(i, na - 1), 0)),
                  pl.BlockSpec((NDEV, ROWS_RP, SHARD), lambda i: (0, jnp.maximum(i - na, 0), 0)),
                  pl.BlockSpec((4, ROWS_RP, SHARD), lambda i: (0, i, 0)), blk, blk, blk],
        out_specs=(blk, blk, blk, blk), compiler_params=_cp(("arbitrary",)),
    )(parts_a, parts_b, parts_late, w, m, v)


def _adam_params(parts, ws, ms, vs, name):
    p = parts.shape[0]
    k = len(ws)
    nrows = [w.size // 128 for w in ws]
    starts = [sum(nrows[:i]) for i in range(k)]

    def shaped(g, shape):
        if len(shape) == 2:
            return jnp.concatenate([g[r:r + 1] for r in range(g.shape[0])], axis=1)
        return g.reshape(shape)

    def body(*refs):
        g_ref = refs[0]
        w_refs, m_refs, v_refs = refs[1:1 + k], refs[1 + k:1 + 2 * k], refs[1 + 2 * k:1 + 3 * k]
        outs = refs[1 + 3 * k:]
        for i in range(k):
            rows = slice(starts[i], starts[i] + nrows[i])
            g = g_ref[0, rows, :]
            for j in range(1, p):
                g = g + g_ref[j, rows, :]
            _adam_update(shaped(g, ws[i].shape), w_refs[i], m_refs[i], v_refs[i], outs[i], outs[k + i], outs[2 * k + i], outs[3 * k + i])

    vm = pl.BlockSpec(memory_space=pltpu.VMEM)
    res = pl.pallas_call(
        body, name=name, out_shape=tuple(jax.ShapeDtypeStruct(w.shape, f32) for _ in range(4) for w in ws),
        in_specs=[vm] * (1 + 3 * k), out_specs=(vm,) * (4 * k), compiler_params=_cp(),
    )(parts, *ws, *ms, *vs)
    return [res[i * k:(i + 1) * k] for i in range(4)]


def _adam_blocks(parts, ws, ms, vs, name):
    k = len(ws)

    def body(*refs):
        g_refs, w_refs, m_refs, v_refs = refs[0:k], refs[k:2 * k], refs[2 * k:3 * k], refs[3 * k:4 * k]
        outs = refs[4 * k:]
        for i in range(k):
            g = g_refs[i][0].astype(f32)
            for j in range(1, parts[i].shape[0]):
                g = g + g_refs[i][j].astype(f32)
            _adam_update(g[None], w_refs[i], m_refs[i], v_refs[i], outs[i], outs[k + i], outs[2 * k + i], outs[3 * k + i])

    vm = pl.BlockSpec(memory_space=pltpu.VMEM)
    res = pl.pallas_call(
        body, name=name, out_shape=tuple(jax.ShapeDtypeStruct(w.shape, f32) for _ in range(4) for w in ws),
        in_specs=[vm] * (4 * k), out_specs=(vm,) * (4 * k), compiler_params=_cp(),
    )(*parts, *ws, *ms, *vs)
    return [res[i * k:(i + 1) * k] for i in range(4)]


def _adam_plain(gs, ws, ms, vs, name):
    k = len(ws)

    def body(*refs):
        g_refs, w_refs, m_refs, v_refs = refs[0:k], refs[k:2 * k], refs[2 * k:3 * k], refs[3 * k:4 * k]
        outs = refs[4 * k:]
        for i in range(k):
            _adam_update(g_refs[i][...], w_refs[i], m_refs[i], v_refs[i], outs[i], outs[k + i], outs[2 * k + i], outs[3 * k + i])

    vm = pl.BlockSpec(memory_space=pltpu.VMEM)
    res = pl.pallas_call(
        body, name=name, out_shape=tuple(jax.ShapeDtypeStruct(w.shape, f32) for _ in range(4) for w in ws),
        in_specs=[vm] * (4 * k), out_specs=(vm,) * (4 * k), compiler_params=_cp(),
    )(*gs, *ws, *ms, *vs)
    return [res[i * k:(i + 1) * k] for i in range(4)]


def _adam_gate(parts, w2, gb, m_w2, m_gb, v_w2, v_gb):
    per = 128 // 32

    def body(p_ref, w2_ref, gb_ref, mw_ref, mb_ref, vw_ref, vb_ref, *outs):
        g = p_ref[0]
        for j in range(1, parts.shape[0]):
            g = g + p_ref[j]
        for d in range(2):
            gd = jnp.concatenate([g[(d * RANK + r) // per:(d * RANK + r) // per + 1, (r % per) * 32:(r % per + 1) * 32]
                                  for r in range(RANK)], axis=0)
            _adam_update(gd, *(t.at[0, d] for t in (w2_ref, mw_ref, vw_ref) + outs[0:4]))
        last = 2 * RANK // per
        gbv = jnp.concatenate([g[last:last + 1, d * 32:(d + 1) * 32] for d in range(2)], axis=0)
        _adam_update(gbv, *(t.at[0] for t in (gb_ref, mb_ref, vb_ref) + outs[4:8]))

    vm = pl.BlockSpec(memory_space=pltpu.VMEM)
    return pl.pallas_call(
        body, name="adam_gate", out_shape=tuple(jax.ShapeDtypeStruct(w.shape, f32) for w in (w2, gb) for _ in range(4)),
        in_specs=[vm] * 7, out_specs=(vm,) * 8, compiler_params=_cp(),
    )(parts, w2, gb, m_w2, m_gb, v_w2, v_gb)


def _in_proj(x, g, scale, shift, w_pad, tl, comm):
    l = x.shape[0]
    c_in, c_out, c_sems = comm.specs()

    def body(*refs):
        x_ref, g_ref, sc_ref, sh_ref, w_ref = refs[:5]
        cin = refs[5:5 + len(c_in)]
        p_ref, h_ref = refs[5 + len(c_in):7 + len(c_in)]
        cout = refs[7 + len(c_in):7 + len(c_in) + comm.n]
        sems = refs[7 + len(c_in) + comm.n:]

        def compute():
            xv = x_ref[...]
            r = lax.rsqrt(jnp.mean(xv * xv, axis=-1, keepdims=True) + EPS)
            h = (xv * r) * (g_ref[...] * (1.0 + sc_ref[...])) + sh_ref[...]
            hb = h.astype(bf16)
            h_ref[...] = hb
            p_ref[...] = jnp.dot(hb, w_ref[...], preferred_element_type=f32).astype(bf16)

        comm.run(cin, cout, sems, l // tl, compute)

    vec = pl.BlockSpec((1, D), lambda i: (0, 0))
    return pl.pallas_call(
        body, name="in_proj", grid=(l // tl,),
        out_shape=(jax.ShapeDtypeStruct((l, PW), bf16), jax.ShapeDtypeStruct((l, D), bf16)) + tuple(comm.outs),
        in_specs=[pl.BlockSpec((tl, D), lambda i: (i, 0)), vec, vec, vec, pl.BlockSpec((D, PW), lambda i: (0, 0))] + c_in,
        out_specs=(pl.BlockSpec((tl, PW), lambda i: (i, 0)), pl.BlockSpec((tl, D), lambda i: (i, 0))) + tuple(c_out),
        scratch_shapes=c_sems, compiler_params=_cp(("arbitrary",)),
    )(x, g, scale, shift, w_pad, *comm.ins)


def _tri(rev):
    i = lax.broadcasted_iota(jnp.int32, (CH, CH), 0)
    j = lax.broadcasted_iota(jnp.int32, (CH, CH), 1)
    return jnp.where((j >= i) if rev else (j <= i), 1.0, 0.0).astype(f32)


def _head_masks():
    lane = lax.broadcasted_iota(jnp.int32, (1, KW), 1) // HK
    return [jnp.where(lane == h, 1.0, 0.0).astype(f32) for h in range(NH)]


def _block_diag():
    r = lax.broadcasted_iota(jnp.int32, (VW, KW), 0) // HV
    c = lax.broadcasted_iota(jnp.int32, (VW, KW), 1) // HK
    return jnp.where(r == c, 1.0, 0.0).astype(f32)


def _decay(lr, w2, gb, tri, rev):
    logits = _dot(lr, w2) + gb
    a = _log_sigmoid(logits) * (1.0 / TAU)
    c = _dot_hi(tri, a)
    cl = c[0:1, :] if rev else c[CH - 1:CH, :]
    return logits, c, cl


def _state_fwd(k, v, c, cl, st, bd):
    return st * jnp.exp(cl) + bd * _tn(v, k * jnp.exp(cl - c))


def _state_bwd(k, v, c, cl, st0, dst, trit):
    ecl = jnp.exp(cl)
    edec = jnp.exp(cl - c)
    kdec = k * edec
    dv = _nt(kdec, dst)
    dkdec = _dot(v, dst)
    dcl = jnp.sum(dst * st0, axis=0, keepdims=True) * ecl + jnp.sum(dkdec * kdec, axis=0, keepdims=True)
    da = _dot_hi(trit, -dkdec * kdec) + dcl
    return dkdec * edec, dv, da, dst * ecl


def _bdot(a, b):
    return lax.dot_general(a.astype(bf16), b.astype(bf16), (((2,), (1,)), ((0,), (0,))), preferred_element_type=f32)


def _bnt(a, b):
    return lax.dot_general(a.astype(bf16), b.astype(bf16), (((2,), (2,)), ((0,), (0,))), preferred_element_type=f32)


def _btn(a, b):
    return lax.dot_general(a.astype(bf16), b.astype(bf16), (((1,), (1,)), ((0,), (0,))), preferred_element_type=f32)


def _scan_chunks(x, rev):
    nc = x.shape[0]
    hi = x.astype(bf16)
    r1 = x - hi.astype(f32)
    mid = r1.astype(bf16)
    lo = (r1 - mid.astype(f32)).astype(bf16)
    terms = jnp.concatenate([hi, mid, lo], axis=1)
    tri3 = jnp.broadcast_to(jnp.concatenate([_tri(rev)] * 3, axis=1).astype(bf16)[None], (nc, CH, 3 * CH))
    return lax.dot_general(tri3, terms, (((2,), (1,)), ((0,), (0,))), preferred_element_type=f32)


class _Tile:
    pass


def _tile_prep(q_ref, k_ref, lr_ref, w2_ref, gb_ref, rev, nc):
    t = _Tile()
    tg = nc * CH
    t.logits = _dot(lr_ref[...], w2_ref[...]) + gb_ref[...]
    c = _scan_chunks((_log_sigmoid(t.logits) * (1.0 / TAU)).reshape(nc, CH, KW), rev)
    cl = c[:, 0:1, :] if rev else c[:, CH - 1:CH, :]
    k = k_ref[...].astype(f32).reshape(nc, CH, KW)
    t.ec, t.enc, t.edec, t.ecl = jnp.exp(c), jnp.exp(-c), jnp.exp(cl - c), jnp.exp(cl)
    t.qd = q_ref[...].astype(f32).reshape(nc, CH, KW) * t.ec * QSCALE
    t.kd = k * t.enc
    t.kdec = k * t.edec
    hm = _head_masks()
    t.tri4 = jnp.concatenate([_tri(rev)] * NH, axis=0)[None]
    t.qs = jnp.concatenate([t.qd * hm[h] for h in range(NH)], axis=1)
    t.pst = _bnt(t.qs, t.kd) * t.tri4
    return t


def _gla_fwd(p, w2p, gb, s0, rev, tg, name):
    l = p.shape[0]
    nb, nc = l // tg, tg // CH

    def body(q_ref, k_ref, v_ref, lr_ref, w2_ref, gb_ref, s0_ref, o_ref, st_ref, st):
        @pl.when(pl.program_id(0) == 0)
        def _():
            st[...] = s0_ref[...]

        t = _tile_prep(q_ref, k_ref, lr_ref, w2_ref, gb_ref, rev, nc)
        v = v_ref[...].reshape(nc, CH, VW)
        intra = jnp.concatenate([_bdot(t.pst[:, h * CH:(h + 1) * CH], v[:, :, h * HV:(h + 1) * HV]) for h in range(NH)], axis=2)
        bd = _block_diag()
        s = st[...]
        for n in (range(nc - 1, -1, -1) if rev else range(nc)):
            st_ref[n] = s.astype(bf16)
            s = s * t.ecl[n] + bd * _tn(v[n], t.kdec[n])
        st[...] = s
        o_ref[...] = (_bnt(t.qd, st_ref[...]) + intra).reshape(tg, VW).astype(bf16)

    blk = (lambda i: nb - 1 - i) if rev else (lambda i: i)
    return pl.pallas_call(
        body, name=name, grid=(nb,),
        out_shape=(jax.ShapeDtypeStruct((l, VW), bf16), jax.ShapeDtypeStruct((l // CH, VW, KW), bf16)),
        in_specs=[pl.BlockSpec((tg, KW), lambda i: (blk(i), PQ // KW)), pl.BlockSpec((tg, KW), lambda i: (blk(i), PK // KW)),
                  pl.BlockSpec((tg, VW), lambda i: (blk(i), PV // VW)), pl.BlockSpec((tg, LRW), lambda i: (blk(i), PLR // LRW)),
                  pl.BlockSpec((LRW, KW), lambda i: (0, 0)), pl.BlockSpec((1, KW), lambda i: (0, 0)),
                  pl.BlockSpec((VW, KW), lambda i: (0, 0))],
        out_specs=(pl.BlockSpec((tg, VW), lambda i: (blk(i), 0)), pl.BlockSpec((nc, VW, KW), lambda i: (blk(i), 0, 0))),
        scratch_shapes=[pltpu.VMEM((VW, KW), f32)],
        compiler_params=_cp(("arbitrary",)),
    )(p, p, p, p, w2p, gb, s0)


def _gla_bwd(p, do, states, w2p, gb, prev, rev, tg, name, comm):
    l = p.shape[0]
    nb, nc = l // tg, tg // CH
    out_dt = bf16
    c_in, c_out, c_sems = comm.specs()

    def body(*refs):
        q_ref, k_ref, v_ref, lr_ref, do_ref, st_ref, w2_ref, gb_ref = refs[:8]
        refs = refs[8:]
        if prev is not None:
            pq_ref, pl_ref = refs[:2]
            refs = refs[2:]
        cin, refs = refs[:len(c_in)], refs[len(c_in):]
        dqkv_ref, dlr_ref, dw2_ref, dgb_ref, ds0_ref = refs[:5]
        cout, dst, ds_buf, sems = refs[5:5 + comm.n], refs[5 + comm.n], refs[6 + comm.n], refs[7 + comm.n:]
        comm.run(cin, cout, sems, nb, lambda: compute(q_ref, k_ref, v_ref, lr_ref, do_ref, st_ref, w2_ref, gb_ref,
                                                      pq_ref if prev is not None else None, pl_ref if prev is not None else None,
                                                      dqkv_ref, dlr_ref, dw2_ref, dgb_ref, ds0_ref, dst, ds_buf))

    def compute(q_ref, k_ref, v_ref, lr_ref, do_ref, st_ref, w2_ref, gb_ref, pq_ref, pl_ref,
                dqkv_ref, dlr_ref, dw2_ref, dgb_ref, ds0_ref, dst, ds_buf):
        @pl.when(pl.program_id(0) == 0)
        def _():
            dst[...] = jnp.zeros_like(dst)
            dw2_ref[...] = jnp.zeros_like(dw2_ref)
            dgb_ref[...] = jnp.zeros_like(dgb_ref)

        t = _tile_prep(q_ref, k_ref, lr_ref, w2_ref, gb_ref, rev, nc)
        hm = _head_masks()
        v = v_ref[...].reshape(nc, CH, VW)
        do = do_ref[...].reshape(nc, CH, VW)
        heads = lambda a, h: a[:, :, h * HV:(h + 1) * HV]
        dpst = jnp.concatenate([_bnt(heads(do, h), heads(v, h)) for h in range(NH)], axis=1) * t.tri4
        dv = jnp.concatenate([_btn(t.pst[:, h * CH:(h + 1) * CH], heads(do, h)) for h in range(NH)], axis=2)
        dqd = _bdot(do, st_ref[...])
        for h in range(NH):
            dqd = dqd + hm[h] * _bdot(dpst[:, h * CH:(h + 1) * CH], t.kd)
        dkd = _btn(dpst, t.qs)
        bd = _block_diag()
        d = dst[...]
        for n in (range(nc) if rev else range(nc - 1, -1, -1)):
            ds_buf[n] = d
            d = d * t.ecl[n] + bd * _tn(do[n], t.qd[n])
        dst[...] = d
        ds0_ref[...] = d
        ds = ds_buf[...]
        dv = dv + _bnt(t.kdec, ds)
        dkdec = _bdot(v, ds)
        dcl = jnp.sum(ds * st_ref[...].astype(f32), axis=1, keepdims=True) * t.ecl + jnp.sum(dkdec * t.kdec, axis=1, keepdims=True)
        dc = dqd * t.qd - dkd * t.kd - dkdec * t.kdec
        da = _scan_chunks(dc, not rev) + dcl
        dq = dqd * t.ec * QSCALE
        dk = dkd * t.enc + dkdec * t.edec
        dlog = da.reshape(tg, KW) * _sigmoid(-t.logits) * (1.0 / TAU)
        dlr = _nt(dlog, w2_ref[...])
        dw2_ref[...] += _tn(lr_ref[...], dlog)
        dgb_ref[...] += jnp.sum(dlog, axis=0, keepdims=True)
        dqkv = jnp.concatenate([dq, dk, dv], axis=2).reshape(tg, 2 * KW + VW)
        if prev is not None:
            dqkv = dqkv + pq_ref[...]
            dlr = dlr + pl_ref[...]
        dqkv_ref[...] = dqkv.astype(out_dt)
        dlr_ref[...] = dlr.astype(out_dt)

    blk = (lambda i: i) if rev else (lambda i: nb - 1 - i)
    in_specs = [pl.BlockSpec((tg, KW), lambda i: (blk(i), PQ // KW)), pl.BlockSpec((tg, KW), lambda i: (blk(i), PK // KW)),
                pl.BlockSpec((tg, VW), lambda i: (blk(i), PV // VW)), pl.BlockSpec((tg, LRW), lambda i: (blk(i), PLR // LRW)),
                pl.BlockSpec((tg, VW), lambda i: (blk(i), 0)), pl.BlockSpec((nc, VW, KW), lambda i: (blk(i), 0, 0)),
                pl.BlockSpec((LRW, KW), lambda i: (0, 0)), pl.BlockSpec((1, KW), lambda i: (0, 0))]
    args = [p, p, p, p, do, states, w2p, gb]
    if prev is not None:
        in_specs += [pl.BlockSpec((tg, 2 * KW + VW), lambda i: (blk(i), 0)), pl.BlockSpec((tg, LRW), lambda i: (blk(i), 0))]
        args += list(prev)
    return pl.pallas_call(
        body, name=name, grid=(nb,),
        out_shape=(jax.ShapeDtypeStruct((l, 2 * KW + VW), out_dt), jax.ShapeDtypeStruct((l, LRW), out_dt),
                   jax.ShapeDtypeStruct((LRW, KW), f32), jax.ShapeDtypeStruct((1, KW), f32), jax.ShapeDtypeStruct((VW, KW), f32))
        + tuple(comm.outs),
        in_specs=in_specs + c_in,
        out_specs=(pl.BlockSpec((tg, 2 * KW + VW), lambda i: (blk(i), 0)), pl.BlockSpec((tg, LRW), lambda i: (blk(i), 0)),
                   pl.BlockSpec((LRW, KW), lambda i: (0, 0)), pl.BlockSpec((1, KW), lambda i: (0, 0)),
                   pl.BlockSpec((VW, KW), lambda i: (0, 0))) + tuple(c_out),
        scratch_shapes=[pltpu.VMEM((VW, KW), f32), pltpu.VMEM((nc, VW, KW), f32)] + c_sems,
        compiler_params=_cp(("arbitrary",)),
    )(*args, *comm.ins)


def _ctx_hidden(ctx_ref, g_ref, sc_ref, sh_ref):
    xv = ctx_ref[...]
    r = lax.rsqrt(jnp.mean(xv * xv, axis=-1, keepdims=True) + EPS)
    xn = xv * r
    return xn, xn * (g_ref[...] * (1.0 + sc_ref[...])) + sh_ref[...]


_CTX_W_SPECS = [pl.BlockSpec((D, KW), lambda i: (0, PK // KW)), pl.BlockSpec((D, VW), lambda i: (0, PV // VW)),
                pl.BlockSpec((D, LRW), lambda i: (0, PLR // LRW))]


def _ctx_fwd(ctx, g, scale, shift, w_pad, w2f, w2b, gbf, gbb):
    ncc = CTX // CH

    def body(ctx_ref, g_ref, sc_ref, sh_ref, wk_ref, wv_ref, wl_ref, w2f_ref, w2b_ref, gbf_ref, gbb_ref, sf_ref, sb_ref):
        _, hc = _ctx_hidden(ctx_ref, g_ref, sc_ref, sh_ref)
        k, v, lr = _dot(hc, wk_ref[...]), _dot(hc, wv_ref[...]), _dot(hc, wl_ref[...])
        bd = _block_diag()
        for rev, w2_ref, gb_ref, out in ((False, w2f_ref, gbf_ref, sf_ref), (True, w2b_ref, gbb_ref, sb_ref)):
            tri = _tri(rev)
            st = jnp.zeros((VW, KW), f32)
            for j in (range(ncc - 1, -1, -1) if rev else range(ncc)):
                rows = slice(j * CH, (j + 1) * CH)
                _, c, cl = _decay(lr[rows], w2_ref[...], gb_ref[...], tri, rev)
                st = _state_fwd(k[rows], v[rows], c, cl, st, bd)
            out[...] = st

    vec = pl.BlockSpec((1, D), lambda i: (0, 0))
    w2s = pl.BlockSpec((LRW, KW), lambda i: (0, 0))
    gbs = pl.BlockSpec((1, KW), lambda i: (0, 0))
    sts = pl.BlockSpec((VW, KW), lambda i: (0, 0))
    return pl.pallas_call(
        body, name="ctx_fwd", grid=(1,), out_shape=(jax.ShapeDtypeStruct((VW, KW), f32),) * 2,
        in_specs=[pl.BlockSpec((CTX, D), lambda i: (0, 0)), vec, vec, vec] + _CTX_W_SPECS + [w2s, w2s, gbs, gbs],
        out_specs=(sts, sts), compiler_params=_cp(("arbitrary",)),
    )(ctx, g, scale, shift, w_pad, w_pad, w_pad, w2f, w2b, gbf, gbb)


def _ctx_bwd(ctx, g, scale, shift, w_pad, w2f, w2b, gbf, gbb, dsf, dsb):
    ncc = CTX // CH

    def body(ctx_ref, g_ref, sc_ref, sh_ref, wk_ref, wv_ref, wl_ref, w2f_ref, w2b_ref, gbf_ref, gbb_ref, dsf_ref, dsb_ref,
             dwk_ref, dwv_ref, dwl_ref, dmod_ref, dg_ref, dw2_ref, dgb_ref):
        xn, hc = _ctx_hidden(ctx_ref, g_ref, sc_ref, sh_ref)
        k, v, lr = _dot(hc, wk_ref[...]), _dot(hc, wv_ref[...]), _dot(hc, wl_ref[...])
        bd = _block_diag()
        dk_rows, dv_rows, dl_rows = [None] * ncc, [None] * ncc, [None] * ncc
        for d, (rev, w2_ref, gb_ref, ds_ref) in enumerate(((False, w2f_ref, gbf_ref, dsf_ref), (True, w2b_ref, gbb_ref, dsb_ref))):
            tri = _tri(rev)
            order = list(range(ncc - 1, -1, -1) if rev else range(ncc))
            st, saved = jnp.zeros((VW, KW), f32), {}
            for j in order:
                rows = slice(j * CH, (j + 1) * CH)
                logits, c, cl = _decay(lr[rows], w2_ref[...], gb_ref[...], tri, rev)
                saved[j] = (logits, c, cl, st)
                st = _state_fwd(k[rows], v[rows], c, cl, st, bd)
            dst = ds_ref[...]
            dw2 = jnp.zeros((LRW, KW), f32)
            dgb = jnp.zeros((1, KW), f32)
            for j in reversed(order):
                rows = slice(j * CH, (j + 1) * CH)
                logits, c, cl, st0 = saved[j]
                dk, dv, da, dst = _state_bwd(k[rows], v[rows], c, cl, st0, dst, _tri(not rev))
                dlog = da * _sigmoid(-logits) * (1.0 / TAU)
                dl = _nt(dlog, w2_ref[...])
                dw2 = dw2 + _tn(lr[rows], dlog)
                dgb = dgb + jnp.sum(dlog, axis=0, keepdims=True)
                dk_rows[j] = dk if dk_rows[j] is None else dk_rows[j] + dk
                dv_rows[j] = dv if dv_rows[j] is None else dv_rows[j] + dv
                dl_rows[j] = dl if dl_rows[j] is None else dl_rows[j] + dl
            dw2_ref[d] = dw2
            dgb_ref[d] = dgb
        dk, dv, dl = (jnp.concatenate(t, axis=0) for t in (dk_rows, dv_rows, dl_rows))
        dwk_ref[...] = _tn(hc, dk)
        dwv_ref[...] = _tn(hc, dv)
        dwl_ref[...] = _tn(hc, dl)
        dh = _nt(dk, wk_ref[...]) + _nt(dv, wv_ref[...]) + _nt(dl, wl_ref[...])
        gx = dh * xn
        dmod_ref[:, 0:D] = jnp.sum(dh, axis=0, keepdims=True)
        dmod_ref[:, D:2 * D] = jnp.sum(gx, axis=0, keepdims=True) * g_ref[...]
        dg_ref[...] = jnp.sum(gx, axis=0, keepdims=True) * (1.0 + sc_ref[...])

    vec = pl.BlockSpec((1, D), lambda i: (0, 0))
    w2s = pl.BlockSpec((LRW, KW), lambda i: (0, 0))
    gbs = pl.BlockSpec((1, KW), lambda i: (0, 0))
    sts = pl.BlockSpec((VW, KW), lambda i: (0, 0))
    full = lambda *s: pl.BlockSpec(s, lambda i: (0,) * len(s))
    return pl.pallas_call(
        body, name="ctx_bwd", grid=(1,),
        out_shape=(jax.ShapeDtypeStruct((D, KW), f32), jax.ShapeDtypeStruct((D, VW), f32), jax.ShapeDtypeStruct((D, LRW), f32),
                   jax.ShapeDtypeStruct((1, 2 * D), f32), jax.ShapeDtypeStruct((1, D), f32),
                   jax.ShapeDtypeStruct((2, LRW, KW), f32), jax.ShapeDtypeStruct((2, 1, KW), f32)),
        in_specs=[pl.BlockSpec((CTX, D), lambda i: (0, 0)), vec, vec, vec] + _CTX_W_SPECS + [w2s, w2s, gbs, gbs, sts, sts],
        out_specs=(full(D, KW), full(D, VW), full(D, LRW), full(1, 2 * D), full(1, D), full(2, LRW, KW), full(2, 1, KW)),
        compiler_params=_cp(("arbitrary",)),
    )(ctx, g, scale, shift, w_pad, w_pad, w_pad, w2f, w2b, gbf, gbb, dsf, dsb)


def _layernorm(va, g, b):
    mu = jnp.mean(va, axis=-1, keepdims=True)
    xc = va - mu
    rstd = lax.rsqrt(jnp.mean(xc * xc, axis=-1, keepdims=True) + EPS)
    vhat = xc * rstd
    return vhat, rstd, vhat * g + b


MIX_PIECES = 8


def _mix_fwd(p, ln_g, ln_b, ws, bs_t):
    l = p.shape[0]
    half = AW // 2
    rp = l // MIX_PIECES
    cpp = rp // ACH

    def body(p_ref, g_ref, b_ref, ws_ref, bs_ref, sv_hbm, va_buf, col_buf, sv_buf, in_sems, out_sems):
        piece = lambda i: pl.ds(pl.multiple_of(i * rp, rp), rp)
        load = lambda i: pltpu.make_async_copy(p_ref.at[piece(i), pl.ds(PVA, AW)], va_buf.at[piece(i)], in_sems.at[i])
        store = lambda i: pltpu.make_async_copy(sv_buf.at[piece(i)], sv_hbm.at[piece(i)], out_sems.at[i])
        for i in range(MIX_PIECES):
            load(i).start()

        def rows_piece(i, carry):
            load(i).wait()
            for j in range(cpp):
                rows = pl.ds(pl.multiple_of(i * rp + j * ACH, ACH), ACH)
                _, _, vn = _layernorm(va_buf[rows, :].astype(f32), g_ref[...], b_ref[...])
                for gi in range(2):
                    sl = slice(gi * ACH, (gi + 1) * ACH)
                    sv_buf[rows, sl] = (_dot(ws_ref[gi], vn[:, sl]) + bs_ref[:, gi:gi + 1]).astype(bf16)
                col_buf[0, rows, :] = vn[:, half:half + ACH]
                col_buf[1, rows, :] = vn[:, half + ACH:]
            return carry

        lax.fori_loop(0, MIX_PIECES, rows_piece, 0)

        def cols_step(cidx, carry):
            rows = pl.ds(cidx, ACH, stride=GW)
            for gi in range(2, 4):
                col_buf[gi - 2, rows, :] = _dot(ws_ref[gi], col_buf[gi - 2, rows, :]) + bs_ref[:, gi:gi + 1]
            return carry

        lax.fori_loop(0, GW, cols_step, 0, unroll=8)

        def out_piece(i, carry):
            for j in range(cpp):
                rows = pl.ds(pl.multiple_of(i * rp + j * ACH, ACH), ACH)
                sv_buf[rows, half:half + ACH] = col_buf[0, rows, :].astype(bf16)
                sv_buf[rows, half + ACH:] = col_buf[1, rows, :].astype(bf16)
            store(i).start()
            return carry

        lax.fori_loop(0, MIX_PIECES, out_piece, 0)
        for i in range(MIX_PIECES):
            store(i).wait()

    vm = pl.BlockSpec(memory_space=pltpu.VMEM)
    hbm = pl.BlockSpec(memory_space=pl.ANY)
    return pl.pallas_call(
        body, name="mix_fwd", out_shape=jax.ShapeDtypeStruct((l, AW), bf16),
        in_specs=[hbm, vm, vm, vm, vm], out_specs=hbm,
        scratch_shapes=[pltpu.VMEM((l, AW), bf16), pltpu.VMEM((2, l, ACH), f32), pltpu.VMEM((l, AW), bf16),
                        pltpu.SemaphoreType.DMA((MIX_PIECES,)), pltpu.SemaphoreType.DMA((MIX_PIECES,))],
        compiler_params=_cp(),
    )(p, ln_g, ln_b, ws, bs_t)


def _mix_bwd(p, dsv, ln_g, ln_b, ws_t):
    l = p.shape[0]
    half = AW // 2
    rp = l // MIX_PIECES
    cpp = rp // ACH

    def body(p_ref, dsv_hbm, g_ref, b_ref, wst_ref, dva_hbm, dws_ref, dbs_ref, dg_ref, db_ref,
             va_buf, dsv_buf, vn_col, ds_col, dva_buf, va_sems, ds_sems, out_sems):
        piece = lambda i: pl.ds(pl.multiple_of(i * rp, rp), rp)
        load_va = lambda i: pltpu.make_async_copy(p_ref.at[piece(i), pl.ds(PVA, AW)], va_buf.at[piece(i)], va_sems.at[i])
        load_ds = lambda i: pltpu.make_async_copy(dsv_hbm.at[piece(i)], dsv_buf.at[piece(i)], ds_sems.at[i])
        store = lambda i: pltpu.make_async_copy(dva_buf.at[piece(i)], dva_hbm.at[piece(i)], out_sems.at[i])
        for i in range(MIX_PIECES):
            load_va(i).start()
            load_ds(i).start()
        dws_ref[...] = jnp.zeros_like(dws_ref)
        dbs_ref[...] = jnp.zeros_like(dbs_ref)
        dg_ref[...] = jnp.zeros_like(dg_ref)
        db_ref[...] = jnp.zeros_like(db_ref)

        def rows_piece(i, carry):
            load_va(i).wait()
            load_ds(i).wait()
            for j in range(cpp):
                rows = pl.ds(pl.multiple_of(i * rp + j * ACH, ACH), ACH)
                _, _, vn = _layernorm(va_buf[rows, :].astype(f32), g_ref[...], b_ref[...])
                ds = dsv_buf[rows, :].astype(f32)
                for gi in range(2):
                    sl = slice(gi * ACH, (gi + 1) * ACH)
                    dws_ref[gi] += _nt(ds[:, sl], vn[:, sl])
                    dbs_ref[gi] += ds[:, sl]
                for gi in range(2):
                    sl = slice(half + gi * ACH, half + (gi + 1) * ACH)
                    vn_col[gi, rows, :] = vn[:, sl]
                    ds_col[gi, rows, :] = ds[:, sl]
            return carry

        lax.fori_loop(0, MIX_PIECES, rows_piece, 0)

        def cols_step(cidx, carry):
            rows = pl.ds(cidx, ACH, stride=GW)
            for gi in range(2, 4):
                ds = ds_col[gi - 2, rows, :]
                dws_ref[gi] += _nt(ds, vn_col[gi - 2, rows, :])
                dbs_ref[gi] += ds
                ds_col[gi - 2, rows, :] = _dot(wst_ref[gi], ds)
            return carry

        lax.fori_loop(0, GW, cols_step, 0, unroll=8)

        def out_piece(i, carry):
            for j in range(cpp):
                rows = pl.ds(pl.multiple_of(i * rp + j * ACH, ACH), ACH)
                vhat, rstd, _ = _layernorm(va_buf[rows, :].astype(f32), g_ref[...], b_ref[...])
                ds = dsv_buf[rows, :].astype(f32)
                dvn = jnp.concatenate([_dot(wst_ref[0], ds[:, 0:ACH]), _dot(wst_ref[1], ds[:, ACH:half]),
                                       ds_col[0, rows, :], ds_col[1, rows, :]], axis=1)
                dg_ref[...] += jnp.sum(dvn * vhat, axis=0, keepdims=True)
                db_ref[...] += jnp.sum(dvn, axis=0, keepdims=True)
                dvh = dvn * g_ref[...]
                dva = rstd * (dvh - jnp.mean(dvh, axis=-1, keepdims=True) - vhat * jnp.mean(dvh * vhat, axis=-1, keepdims=True))
                dva_buf[rows, :] = dva.astype(bf16)
            store(i).start()
            return carry

        lax.fori_loop(0, MIX_PIECES, out_piece, 0)
        for i in range(MIX_PIECES):
            store(i).wait()

    vm = pl.BlockSpec(memory_space=pltpu.VMEM)
    hbm = pl.BlockSpec(memory_space=pl.ANY)
    dma = lambda: pltpu.SemaphoreType.DMA((MIX_PIECES,))
    return pl.pallas_call(
        body, name="mix_bwd",
        out_shape=(jax.ShapeDtypeStruct((l, AW), bf16), jax.ShapeDtypeStruct((4, ACH, ACH), f32), jax.ShapeDtypeStruct((4, ACH, ACH), f32),
                   jax.ShapeDtypeStruct((1, AW), f32), jax.ShapeDtypeStruct((1, AW), f32)),
        in_specs=[hbm, hbm, vm, vm, vm], out_specs=(hbm, vm, vm, vm, vm),
        scratch_shapes=[pltpu.VMEM((l, AW), bf16), pltpu.VMEM((l, AW), bf16), pltpu.VMEM((2, l, ACH), f32), pltpu.VMEM((2, l, ACH), f32),
                        pltpu.VMEM((l, AW), bf16), dma(), dma(), dma()],
        compiler_params=_cp(),
    )(p, dsv, ln_g, ln_b, ws_t)


def _mid(x, tgt, p, o_f, o_b, sv, gate, gf, gb_norm, w_pa, w_pb, w_out, tl):
    l = x.shape[0]

    def body(x_ref, t_ref, zb_ref, ua_ref, za_ref, g1_ref, g2_ref, of_ref, ob_ref, sv_ref, gate_ref, gf_ref, gbn_ref,
             wpa_ref, wpb_ref, wout_ref,
             dx1_ref, dzbua_ref, dzag_ref, dsv_ref, do_ref, dwout_bf, dwpa_bf, dwpb_bf, dgf_ref, dgate_ref, dgbn_ref, loss_ref,
             dwout_ref, dwpa_ref, dwpb_ref):
        @pl.when(pl.program_id(0) == 0)
        def _():
            for r in (dwout_ref, dwpa_ref, dwpb_ref, dgf_ref, dgate_ref, dgbn_ref, loss_ref):
                r[...] = jnp.zeros_like(r)

        o = of_ref[...].astype(f32) + ob_ref[...].astype(f32)
        rr = jnp.concatenate(
            [jnp.broadcast_to(lax.rsqrt(jnp.mean(o[:, h * HV:(h + 1) * HV] ** 2, axis=-1, keepdims=True) + EPS), (tl, HV))
             for h in range(NH)], axis=1)
        ohat = o * rr
        on = ohat * gbn_ref[...]
        szb, dszb = _silu_and_grad(zb_ref[...].astype(f32))
        tb = on * szb
        u = ua_ref[...].astype(f32)
        svv = sv_ref[...].astype(f32)
        sza, dsza = _silu_and_grad(za_ref[...].astype(f32))
        ta = u * svv * sza
        ya = _dot(ta, wpa_ref[...])
        yb = _dot(tb, wpb_ref[...])
        g1 = _sigmoid(g1_ref[...].astype(f32))
        g2 = _sigmoid(g2_ref[...].astype(f32))
        m = g1 * ya + g2 * yb
        y2 = _dot(m, wout_ref[...])
        x1 = x_ref[...] + gate_ref[...] * y2
        r1 = lax.rsqrt(jnp.mean(x1 * x1, axis=-1, keepdims=True) + EPS)
        x1n = x1 * r1
        err = x1n * gf_ref[...] - t_ref[...]
        loss_ref[...] += jnp.sum(jnp.sum(err * err, axis=-1, keepdims=True), axis=0, keepdims=True) * (0.5 / D)
        dout = err * (1.0 / D)
        dgf_ref[...] += jnp.sum(dout * x1n, axis=0, keepdims=True)
        dx1n = dout * gf_ref[...]
        dx1 = r1 * (dx1n - x1n * jnp.mean(dx1n * x1n, axis=-1, keepdims=True))
        dx1_ref[...] = dx1
        dgate_ref[...] += jnp.sum(dx1 * y2, axis=0, keepdims=True)
        dy2 = dx1 * gate_ref[...]
        dwout_ref[...] += _tn(m, dy2)
        dm = _nt(dy2, wout_ref[...])
        dya = dm * g1
        dyb = dm * g2
        dzag_ref[:, AW:AW + D] = (dm * ya * g1 * (1.0 - g1)).astype(bf16)
        dzag_ref[:, AW + D:] = (dm * yb * g2 * (1.0 - g2)).astype(bf16)
        dwpa_ref[...] += _tn(ta, dya)
        dta = _nt(dya, wpa_ref[...])
        dzbua_ref[:, AW:] = (dta * svv * sza).astype(bf16)
        dsv_ref[...] = (dta * u * sza).astype(bf16)
        dzag_ref[:, 0:AW] = (dta * u * svv * dsza).astype(bf16)
        dwpb_ref[...] += _tn(tb, dyb)
        dtb = _nt(dyb, wpb_ref[...])
        don = dtb * szb
        dzbua_ref[:, 0:AW] = (dtb * on * dszb).astype(bf16)
        dgbn_ref[...] += jnp.sum(don * ohat, axis=0, keepdims=True)
        doh = don * gbn_ref[...]
        prod = doh * ohat
        mh = jnp.concatenate(
            [jnp.broadcast_to(jnp.mean(prod[:, h * HV:(h + 1) * HV], axis=-1, keepdims=True), (tl, HV)) for h in range(NH)], axis=1)
        do_ref[...] = (rr * (doh - ohat * mh)).astype(bf16)

        @pl.when(pl.program_id(0) == l // tl - 1)
        def _():
            for acc, out in ((dwout_ref, dwout_bf), (dwpa_ref, dwpa_bf), (dwpb_ref, dwpb_bf)):
                out[...] = acc[...].astype(bf16)

    row = lambda w, j: pl.BlockSpec((tl, w), lambda i, j=j: (i, j))
    full = lambda *s: pl.BlockSpec(s, lambda i: (0,) * len(s))
    return pl.pallas_call(
        body, name="mid", grid=(l // tl,),
        out_shape=(jax.ShapeDtypeStruct((l, D), f32), jax.ShapeDtypeStruct((l, 2 * AW), bf16), jax.ShapeDtypeStruct((l, AW + 2 * D), bf16),
                   jax.ShapeDtypeStruct((l, AW), bf16), jax.ShapeDtypeStruct((l, VW), bf16),
                   jax.ShapeDtypeStruct((D, D), bf16), jax.ShapeDtypeStruct((AW, D), bf16), jax.ShapeDtypeStruct((VW, D), bf16),
                   jax.ShapeDtypeStruct((1, D), f32), jax.ShapeDtypeStruct((1, D), f32), jax.ShapeDtypeStruct((1, VW), f32),
                   jax.ShapeDtypeStruct((1, 1), f32)),
        scratch_shapes=[pltpu.VMEM((D, D), f32), pltpu.VMEM((AW, D), f32), pltpu.VMEM((VW, D), f32)],
        in_specs=[row(D, 0), row(D, 0), row(AW, PZB // AW), row(AW, PUA // AW), row(AW, PZA // AW), row(D, PG1 // D), row(D, PG2 // D),
                  row(VW, 0), row(VW, 0), row(AW, 0), full(1, D), full(1, D), full(1, VW), full(AW, D), full(VW, D), full(D, D)],
        out_specs=(row(D, 0), row(2 * AW, 0), row(AW + 2 * D, 0), row(AW, 0), row(VW, 0),
                   full(D, D), full(AW, D), full(VW, D), full(1, D), full(1, D), full(1, VW), full(1, 1)),
        compiler_params=_cp(("arbitrary",)),
    )(x, tgt, p, p, p, p, p, o_f, o_b, sv, gate, gf, gb_norm, w_pa, w_pb, w_out)


def _in_bwd(x, dx1, dqkv, dzbua, dva, dzag, dlr, w_pad, g, scale, tl, comm):
    l = x.shape[0]
    c_in, c_out, c_sems = comm.specs()

    def body(*refs):
        cin = refs[14:14 + len(c_in)]
        outs = refs[14 + len(c_in):]
        comm.run(cin, outs[4:4 + comm.n], outs[4 + comm.n:], l // tl, lambda: compute(*refs[:14], *outs[:4]))

    def compute(x_ref, dx1_ref, a_ref, b_ref, c_ref, e_ref, lr_ref, wa_ref, wb_ref, wc_ref, we_ref, wl_ref, g_ref, sc_ref,
                gx_ref, dsh_ref, dsc_ref, dg_ref):
        @pl.when(pl.program_id(0) == 0)
        def _():
            for r in (dsh_ref, dsc_ref, dg_ref):
                r[...] = jnp.zeros_like(r)

        dh = (_nt(a_ref[...], wa_ref[...]) + _nt(b_ref[...], wb_ref[...]) + _nt(c_ref[...], wc_ref[...]) + _nt(e_ref[...], we_ref[...])
              + _nt(lr_ref[...], wl_ref[...]))
        xv = x_ref[...]
        r = lax.rsqrt(jnp.mean(xv * xv, axis=-1, keepdims=True) + EPS)
        xn = xv * r
        gxn = jnp.sum(dh * xn, axis=0, keepdims=True)
        dsh_ref[...] += jnp.sum(dh, axis=0, keepdims=True)
        dsc_ref[...] += gxn * g_ref[...]
        dg_ref[...] += gxn * (1.0 + sc_ref[...])
        dxn = dh * (g_ref[...] * (1.0 + sc_ref[...]))
        gx_ref[...] = dx1_ref[...] + r * (dxn - xn * jnp.mean(dxn * xn, axis=-1, keepdims=True))

    row = lambda w: pl.BlockSpec((tl, w), lambda i: (i, 0))
    wcol = lambda w, j: pl.BlockSpec((D, w), lambda i, j=j: (0, j))
    vec = pl.BlockSpec((1, D), lambda i: (0, 0))
    return pl.pallas_call(
        body, name="in_bwd", grid=(l // tl,),
        out_shape=(jax.ShapeDtypeStruct((l, D), f32),) + (jax.ShapeDtypeStruct((1, D), f32),) * 3 + tuple(comm.outs),
        in_specs=[row(D), row(D), row(2 * KW + VW), row(2 * AW), row(AW), row(AW + 2 * D), row(LRW),
                  wcol(2 * KW + VW, 0), wcol(2 * AW, PZB // (2 * AW)), wcol(AW, PVA // AW), wcol(AW + 2 * D, PZA // (AW + 2 * D)),
                  wcol(LRW, PLR // LRW), vec, vec] + c_in,
        out_specs=(row(D), vec, vec, vec) + tuple(c_out), scratch_shapes=c_sems,
        compiler_params=_cp(("arbitrary",)),
    )(x, dx1, dqkv, dzbua, dva, dzag, dlr, w_pad, w_pad, w_pad, w_pad, w_pad, g, scale, *comm.ins)


def _tn_matmul(a, bs, tl, name, comm=None, pack=None):
    l, m = a.shape
    k = len(bs)
    comm = comm or _Comm([], [], None)
    c_in, c_out, c_sems = comm.specs()
    acc_shapes = [(m, b.shape[1]) for b in bs]
    n_extra = len(pack.extra) if pack else 0
    n_res = len(pack.outs) if pack else k

    def body(a_ref, *refs):
        b_refs, refs = refs[:k], refs[k:]
        extra, refs = refs[:n_extra], refs[n_extra:]
        cin, refs = refs[:len(c_in)], refs[len(c_in):]
        res, refs = refs[:n_res], refs[n_res:]
        cout, refs = refs[:comm.n], refs[comm.n:]
        accs, sems = (refs[:k], refs[k:]) if pack else (res, refs)

        def compute():
            @pl.when(pl.program_id(0) == 0)
            def _():
                for o_ref in accs:
                    o_ref[...] = jnp.zeros_like(o_ref)

            av = a_ref[...]
            for b_ref, o_ref in zip(b_refs, accs):
                o_ref[...] += _tn(av, b_ref[...])
            if pack:
                pl.when(pl.program_id(0) == l // tl - 1)(lambda: pack.fn(accs, extra, res))

        comm.run(cin, cout, sems, l // tl, compute)

    full = lambda shape: pl.BlockSpec(shape, lambda i: (0,) * len(shape))
    res_shapes = list(pack.outs) if pack else [jax.ShapeDtypeStruct(sh, f32) for sh in acc_shapes]
    return pl.pallas_call(
        body, name=name, grid=(l // tl,), out_shape=tuple(res_shapes) + tuple(comm.outs),
        in_specs=[pl.BlockSpec((tl, m), lambda i: (i, 0))] + [pl.BlockSpec((tl, b.shape[1]), lambda i: (i, 0)) for b in bs]
        + [full(e.shape) for e in (pack.extra if pack else [])] + c_in,
        out_specs=tuple(full(r.shape) for r in res_shapes) + tuple(c_out),
        scratch_shapes=([pltpu.VMEM(sh, f32) for sh in acc_shapes] if pack else []) + c_sems, compiler_params=_cp(("arbitrary",)),
    )(a, *bs, *(pack.extra if pack else []), *comm.ins)


def _pad_gate(w2, gb):
    z = jnp.zeros((RANK, KW), f32)
    tail = jnp.zeros((LRW - 2 * RANK, KW), f32)
    w2f = jnp.concatenate([w2[0], z, tail], axis=0)
    w2b = jnp.concatenate([z, w2[1], tail], axis=0)
    return w2f, w2b, gb[0:1], gb[1:2]


EARLY_A_ROWS = 512


class _NoExchange:
    def __init__(self, w_pa, w_pb, w_out):
        self.weights = (w_pa, w_pb, w_out)

    def gather_proj(self):
        return _Comm([], [], None)

    def proj_weights(self, got):
        return self.weights

    def first(self, dw_out, dw_pa, dw_pb, small):
        return _Comm([], [], None)

    def early_a(self, blocks):
        return _Comm([], [], None)

    def early_b(self, blocks):
        return _Comm([], [], None)

    def late(self, blocks, dgt):
        return _Comm([], [], None)


class _Exchanges:
    def __init__(self, pa, pb, wo):
        self.shards = (pa, pb, wo)

    def gather_proj(self):
        def plan(i, o):
            srcs = [lambda j, r=r: r for r in i]
            dsts = [lambda j: o[0].at[:, _lanes(j)], lambda j: o[1].at[:, _lanes(j)], lambda j: o[2].at[j]]
            return srcs, dsts, None
        sds = jax.ShapeDtypeStruct
        return _Comm(self.shards, [sds((AW, D), bf16), sds((VW, D), bf16), sds((NDEV, 128, D), bf16)], plan)

    def proj_weights(self, got):
        return got[0], got[1], got[2].reshape(D, D)

    def first(self, dw_out, dw_pa, dw_pb, small):
        def plan(i, o):
            srcs = [lambda j: i[0].at[j], lambda j: i[1].at[:, _lanes(j)], lambda j: i[2].at[:, _lanes(j)], lambda j: i[3]]
            dsts = [lambda j, r=r: r.at[j] for r in o]
            return srcs, dsts, None
        sds = jax.ShapeDtypeStruct
        return _Comm([dw_out.reshape(NDEV, 128, D), dw_pa, dw_pb, small],
                     [sds((NDEV, 128, D), bf16), sds((NDEV, AW, 128), bf16), sds((NDEV, VW, 128), bf16),
                      sds((NDEV,) + small.shape, f32)], plan)

    def early_a(self, blocks):
        def plan(i, o):
            return [lambda j: i[0].at[j]], [lambda j: o[0].at[j]], [lambda j: j >= LATE_DESTS - 1]
        return _Comm([blocks], [jax.ShapeDtypeStruct(blocks.shape, bf16)], plan)

    def early_b(self, blocks):
        def plan(i, o):
            return [lambda j: i[0].at[j]], [lambda j: o[0].at[j]], [lambda j: j >= LATE_DESTS - 1]
        return _Comm([blocks], [jax.ShapeDtypeStruct(blocks.shape, bf16)], plan)

    def late(self, blocks, dgt):
        return _LateComm(blocks, dgt)


def _local_step(x, ctx, tgt, mod, modc, norm_g, w_pad, ln_g, ln_b, ws, bs, w2, gb, gb_norm, gf, xch):
    shift, scale, gate = mod[:, 0:D], mod[:, D:2 * D], mod[:, 2 * D:]
    shift_c, scale_c = modc[:, 0:D], modc[:, D:]
    w2f, w2b, gbf, gbb = _pad_gate(w2, gb)

    p, h, *got_proj = _in_proj(x, norm_g, scale, shift, w_pad, 512, xch.gather_proj())
    w_pa, w_pb, w_out = xch.proj_weights(got_proj)
    sc_f, sc_b = _ctx_fwd(ctx, norm_g, scale_c, shift_c, w_pad, w2f, w2b, gbf, gbb)
    o_f, st_f = _gla_fwd(p, w2f, gbf, sc_f, False, 512, "gla_fwd_f")
    o_b, st_b = _gla_fwd(p, w2b, gbb, sc_b, True, 512, "gla_fwd_b")
    sv = _mix_fwd(p, ln_g, ln_b, ws.astype(bf16), bs.T)
    (dx1, dzbua, dzag, dsv, do, dw_out, dw_pa, dw_pb, dgf, dgate, dgbn, loss) = _mid(
        x, tgt, p, o_f, o_b, sv, gate, gf, gb_norm, w_pa, w_pb, w_out, 256)
    dva, dws, dbs_acc, dln_g, dln_b = _mix_bwd(p, dsv, ln_g, ln_b, jnp.swapaxes(ws, 1, 2).astype(bf16))
    small = _rows128(dln_g, dln_b, dws, jnp.sum(dbs_acc, axis=-1), dgbn, dgf, jnp.broadcast_to(loss, (1, 128)))
    blocks_a, blocks_b, *got_first = _tn_matmul(h, [dzbua, dva, dzag], 1024, "dw_early", xch.first(dw_out, dw_pa, dw_pb, small),
                                               _pack_early())

    dqkv_f, dlr_f, dw2f, dgbf, dsc_f, *got_a = _gla_bwd(p, do, st_f, w2f, gbf, None, False, 512, "gla_bwd_f",
                                                        xch.early_a(blocks_a))
    dqkv, dlr, dw2b, dgbb, dsc_b, *got_b = _gla_bwd(p, do, st_b, w2b, gbb, (dqkv_f, dlr_f), True, 512, "gla_bwd_b",
                                                    xch.early_b(blocks_b))
    dwk_c, dwv_c, dwl_c, dmodc, dg_c, dw2c, dgbc = _ctx_bwd(ctx, norm_g, scale_c, shift_c, w_pad, w2f, w2b, gbf, gbb, dsc_f, dsc_b)
    (blocks_late,) = _tn_matmul(h, [dqkv, dlr], 1024, "dw_qkv_lr", pack=_pack_late(dwk_c, dwv_c, dwl_c))
    dw2 = jnp.stack([dw2f[0:RANK] + dw2c[0, 0:RANK], dw2b[RANK:2 * RANK] + dw2c[1, RANK:2 * RANK]])
    dgb = jnp.concatenate([dgbf + dgbc[0], dgbb + dgbc[1]], axis=0)
    dgt = jnp.concatenate([jnp.transpose(dw2.reshape(2, RANK, NDEV, 32), (2, 0, 1, 3)).reshape(NDEV, 2 * RANK * 32),
                           jnp.transpose(dgb.reshape(2, NDEV, 32), (1, 0, 2)).reshape(NDEV, 64),
                           jnp.zeros((NDEV, 64), f32)], axis=1).reshape(NDEV, 9, 128)
    gx, dshift, dscale, dg, *got_late = _in_bwd(x, dx1, dqkv, dzbua, dva, dzag, dlr, w_pad, norm_g, scale, 512,
                                                xch.late(blocks_late, dgt))
    return dict(loss=loss, gx=gx, dmod=jnp.concatenate([dshift, dscale, dgate], axis=1), dmodc=dmodc, dnorm_g=dg + dg_c,
                small=small, blocks_a=blocks_a, blocks_b=blocks_b, blocks_late=blocks_late, dw2=dw2, dgb=dgb,
                dw_pa=dw_pa, dw_pb=dw_pb, dw_out=dw_out, got_first=got_first, got_a=got_a, got_b=got_b, got_late=got_late)


def _rows128(*vs):
    out = []
    for t in vs:
        t = t.reshape(-1)
        pad = (-t.shape[0]) % 128
        out.append(jnp.pad(t, (0, pad)) if pad else t)
    return jnp.concatenate(out).reshape(-1, 128)


def kernel(x, c, ctx, c_ctx, w_mod, b_mod, norm_g, w_in, a_ln_g, a_ln_b, a_ws, a_bs, b_gate_w2, b_gate_b, b_norm_g, w_proj_a, w_proj_b, w_out, final_norm_g, loss_target, m_c_ctx, m_w_mod, m_b_mod, m_norm_g, m_w_in, m_a_ln_g, m_a_ln_b, m_a_ws, m_a_bs, m_b_gate_w2, m_b_gate_b, m_b_norm_g, m_w_proj_a, m_w_proj_b, m_w_out, m_final_norm_g, v_c_ctx, v_w_mod, v_b_mod, v_norm_g, v_w_in, v_a_ln_g, v_a_ln_b, v_a_ws, v_a_bs, v_b_gate_w2, v_b_gate_b, v_b_norm_g, v_w_proj_a, v_w_proj_b, v_w_out, v_final_norm_g):
    me = _me()
    ncol = w_mod.shape[2]

    gate_mine = _rows128(jnp.concatenate([b_gate_w2.reshape(-1), b_gate_b.reshape(-1)]))
    bm_mine = lax.dynamic_slice(b_mod, (0, me * ncol), (1, ncol))
    cs, mods, wg, gates = _gather_first(c, c_ctx.reshape(1, D), w_mod[0], bm_mine, w_in[0].astype(bf16), gate_mine)
    w_pad = _repack_w(wg)
    gflat = gates.reshape(NDEV, 9 * 128)
    w2 = jnp.transpose(gflat[:, 0:2 * RANK * 32].reshape(NDEV, 2, RANK, 32), (1, 2, 0, 3)).reshape(2, RANK, KW)
    gb = jnp.transpose(gflat[:, 2 * RANK * 32:2 * RANK * 32 + 64].reshape(NDEV, 2, 32), (1, 0, 2)).reshape(2, KW)

    mods = jnp.transpose(mods, (1, 0, 2)).reshape(16, 3 * D)
    mod = lax.dynamic_slice(mods, (me, 0), (1, 3 * D))
    modc = mods[8:9, 0:2 * D]

    xch = _Exchanges(w_proj_a[0].astype(bf16), w_proj_b[0].astype(bf16), w_out[0].astype(bf16))
    r = _local_step(x[0], ctx[0], loss_target[0], mod, modc, norm_g, w_pad, a_ln_g, a_ln_b, a_ws[0], a_bs[0], w2, gb,
                    b_norm_g, final_norm_g.reshape(1, D), xch)
    p_out, p_pa, p_pb, smalls_e = r["got_first"]
    (p_in_a,) = r["got_a"]
    (p_in_b,) = r["got_b"]
    _, _, p_in_late, p_gt = r["got_late"]

    n_e = (AW + AW + 4 * ACH * ACH + AW + VW + D) // 128
    row = lambda t: t.reshape(1, D)
    rep_e = _adam_params(smalls_e, [a_ln_g, a_ln_b, a_ws, a_bs, b_norm_g, row(final_norm_g)],
                         [m_a_ln_g, m_a_ln_b, m_a_ws, m_a_bs, m_b_norm_g, row(m_final_norm_g)],
                         [v_a_ln_g, v_a_ln_b, v_a_ws, v_a_bs, v_b_norm_g, row(v_final_norm_g)], "adam_rep_early")
    rep_e = [t[0:5] + (t[5].reshape(D),) for t in rep_e]
    losses = smalls_e[:, n_e, 0]
    loss = losses[0]
    for i in range(1, NDEV):
        loss = loss + losses[i]

    dbm = r["dmod"] + jnp.concatenate([r["dmodc"], jnp.zeros((1, D), f32)], axis=1)
    mod_grads = _mod_tail(r["dnorm_g"], dbm, r["dmod"], r["dmodc"], cs, w_mod)
    mod_res = _adam_plain(mod_grads, [w_mod, norm_g, b_mod, row(c_ctx)], [m_w_mod, m_norm_g, m_b_mod, row(m_c_ctx)],
                          [v_w_mod, v_norm_g, v_b_mod, row(v_c_ctx)], "adam_mod")
    wm, ng, bmod_r, cc = ([t[i] for t in mod_res] for i in range(4))

    a_in = _adam_w_in(p_in_a, p_in_b, p_in_late, w_in, m_w_in, v_w_in)
    blk = _adam_blocks([p_pa, p_pb, p_out], [w_proj_a, w_proj_b, w_out], [m_w_proj_a, m_w_proj_b, m_w_out],
                       [v_w_proj_a, v_w_proj_b, v_w_out], "adam_proj_out")
    a_pa, a_pb, a_out = ([t[i] for t in blk] for i in range(3))
    a_gt = _adam_gate(p_gt, b_gate_w2, b_gate_b, m_b_gate_w2, m_b_gate_b, v_b_gate_w2, v_b_gate_b)
    sh = [(a_in[k], a_pa[k], a_pb[k], a_out[k], a_gt[k], a_gt[4 + k]) for k in range(4)]

    outs = [loss, r["gx"][None]]
    for k in range(4):
        lg, lb, aws, abs_, bng, fng = rep_e[k]
        n_g, bmod = ng[k], bmod_r[k]
        s_in, s_pa, s_pb, s_out, s_w2, s_gb = sh[k]
        outs += [cc[k].reshape(D), wm[k], bmod, n_g, s_in, lg, lb, aws, abs_, s_w2, s_gb, bng, s_pa, s_pb, s_out, fng]
    return tuple(outs)
```

```python
import jax
import jax.numpy as jnp
from jax import lax
from jax.experimental import pallas as pl
from jax.experimental.pallas import tpu as pltpu

f32, bf16 = jnp.float32, jnp.bfloat16

D = 1024
CTX = 256
EPS = 1e-6
AW = 512
ACH = 128
GW = 64
KW = 256
VW = 512
NH = 4
HK = 64
HV = 128
RANK = 16
TAU = 16.0
CH = 64
QSCALE = HK ** -0.5
INW = 5152
NDEV = 8

PQ, PK, PV, PZB, PUA, PVA, PZA, PG1, PG2, PLR, PW = 0, 256, 512, 1024, 1536, 2048, 2560, 3072, 4096, 5120, 5248
LRW = 128

ADAM_LR, ADAM_B1, ADAM_B2, ADAM_EPS, ADAM_WD, ADAM_STEP = 0.001, 0.9, 0.999, 1e-08, 0.01, 10

VMEM_LIMIT = 56 * 1024 * 1024
MESH = pl.DeviceIdType.MESH


def _cp(sem=None):
    return pltpu.CompilerParams(dimension_semantics=sem, vmem_limit_bytes=VMEM_LIMIT)


def _dot(a, b):
    return jnp.dot(a.astype(bf16), b.astype(bf16), preferred_element_type=f32)


def _nt(a, b):
    return lax.dot_general(a.astype(bf16), b.astype(bf16), (((1,), (1,)), ((), ())), preferred_element_type=f32)


def _tn(a, b):
    return lax.dot_general(a.astype(bf16), b.astype(bf16), (((0,), (0,)), ((), ())), preferred_element_type=f32)


def _dot_hi(a, b):
    return jnp.dot(a, b, preferred_element_type=f32, precision=lax.Precision.HIGHEST)


def _sigmoid(x):
    return 1.0 / (1.0 + jnp.exp(-x))


def _log_sigmoid(x):
    return jnp.minimum(x, 0.0) - jnp.log(1.0 + jnp.exp(-jnp.abs(x)))


def _silu_and_grad(z):
    s = _sigmoid(z)
    return z * s, s * (1.0 + z * (1.0 - s))


def _me():
    return 4 * lax.axis_index("x") + 2 * lax.axis_index("y") + lax.axis_index("c")


def _peer(k):
    x, y, c = lax.axis_index("x"), lax.axis_index("y"), lax.axis_index("c")
    px = 1 - x if k & 4 else x
    py = 1 - y if k & 2 else y
    pc = 1 - c if k & 1 else c
    return (px, py, pc), 4 * px + 2 * py + pc


def _fanout(srcs, dsts, send_sems, recv_sems, local_sems, owners=None):
    me = _me()
    n = len(srcs)
    owns = lambda a, j: True if owners is None or owners[a] is None else owners[a](j)

    def guarded(cond, fn):
        if cond is True:
            fn()
        else:
            pl.when(cond)(fn)

    def copies(with_recvs):
        local = [pltpu.make_async_copy(srcs[a](me), dsts[a](me), local_sems.at[a]) for a in range(n)]
        sends, recvs = [], []
        for k in range(1, NDEV):
            dev, idx = _peer(k)
            for a in range(n):
                s = (k - 1) * n + a
                sends.append((owns(a, idx), pltpu.make_async_remote_copy(
                    src_ref=srcs[a](idx), dst_ref=dsts[a](me), send_sem=send_sems.at[s], recv_sem=recv_sems.at[s],
                    device_id=dev, device_id_type=MESH)))
                if with_recvs:
                    recvs.append((owns(a, me), pltpu.make_async_remote_copy(
                        src_ref=srcs[a](idx), dst_ref=dsts[a](idx), send_sem=send_sems.at[s], recv_sem=recv_sems.at[s],
                        device_id=dev, device_id_type=MESH)))
        return local, sends, recvs

    def start():
        local, sends, _ = copies(False)
        for a, cp in enumerate(local):
            guarded(owns(a, me), cp.start)
        for cond, cp in sends:
            guarded(cond, cp.start)

    def finish():
        local, sends, recvs = copies(True)
        for cond, cp in recvs:
            guarded(cond, cp.wait_recv)
        for cond, cp in sends:
            guarded(cond, cp.wait_send)
        for a, cp in enumerate(local):
            guarded(owns(a, me), cp.wait)

    return start, finish


class _Comm:
    def __init__(self, ins, outs, plan):
        self.ins, self.outs, self.plan = list(ins), list(outs), plan
        self.n = len(self.outs)

    def specs(self):
        hbm = pl.BlockSpec(memory_space=pl.ANY)
        return [hbm] * len(self.ins), [hbm] * self.n, _fanout_sems(self.n) if self.n else []

    def run(self, in_refs, out_refs, sems, nsteps, compute):
        if not self.n:
            compute()
            return

        def hooks():
            srcs, dsts, owners = self.plan(in_refs, out_refs)
            return _fanout(srcs, dsts, sems[0], sems[1], sems[2], owners)

        pl.when(pl.program_id(0) == 0)(lambda: hooks()[0]())
        compute()
        pl.when(pl.program_id(0) == nsteps - 1)(lambda: hooks()[1]())


LATE_MID_STEP = 1


class _LateComm:
    def __init__(self, blocks, dgt):
        sds = jax.ShapeDtypeStruct
        self.ins = [blocks, dgt]
        self.outs = [sds((D, SHARD), bf16), sds((D, SHARD), bf16), sds((4, D, SHARD), bf16), sds(dgt.shape, f32)]
        self.n = len(self.outs)

    def specs(self):
        hbm = pl.BlockSpec(memory_space=pl.ANY)
        scratch = [pltpu.VMEM((3, D, SHARD), bf16), pltpu.SemaphoreType.DMA((2,)), pltpu.SemaphoreType.DMA((4,)),
                   pltpu.SemaphoreType.DMA((3,))] + _fanout_sems(1)
        return [hbm] * 2, [hbm] * self.n, scratch

    def run(self, in_refs, out_refs, scratch, nsteps, compute):
        late_ref, dgt_ref = in_refs
        sib_ref, pair_ref, parts_ref, gt_ref = out_refs
        vbuf, send_sems, recv_sems, local_sems, g_send, g_recv, g_local = scratch
        x, y, c = lax.axis_index("x"), lax.axis_index("y"), lax.axis_index("c")
        chip = 2 * x + y
        is_owner_chip = chip == 0
        step = pl.program_id(0)

        def to_sibling():
            return pltpu.make_async_remote_copy(src_ref=late_ref.at[1 - c], dst_ref=sib_ref, send_sem=send_sems.at[0],
                                                recv_sem=recv_sems.at[0], device_id=(x, y, 1 - c), device_id_type=MESH)

        def to_owner(k):
            return pltpu.make_async_remote_copy(src_ref=pair_ref, dst_ref=parts_ref.at[k], send_sem=send_sems.at[1],
                                                recv_sem=recv_sems.at[k], device_id=(0, 0, c), device_id_type=MESH)

        def own_copy():
            return pltpu.make_async_copy(pair_ref, parts_ref.at[0], local_sems.at[2])

        def gates():
            return _fanout([lambda j: dgt_ref.at[j]], [lambda j: gt_ref.at[j]], g_send, g_recv, g_local)

        @pl.when(step == 0)
        def _():
            to_sibling().start()
            gates()[0]()

        compute()

        @pl.when(step == LATE_MID_STEP)
        def _():
            mine = pltpu.make_async_copy(late_ref.at[c], vbuf.at[0], local_sems.at[0])
            mine.start()
            to_sibling().wait_recv()
            theirs = pltpu.make_async_copy(sib_ref, vbuf.at[1], local_sems.at[1])
            theirs.start()
            mine.wait()
            theirs.wait()
            vbuf[2] = (vbuf[0].astype(f32) + vbuf[1].astype(f32)).astype(bf16)
            pltpu.sync_copy(vbuf.at[2], pair_ref)
            pl.when(is_owner_chip)(lambda: own_copy().start())
            pl.when(jnp.logical_not(is_owner_chip))(lambda: to_owner(chip).start())

        @pl.when(step == nsteps - 1)
        def _():
            @pl.when(is_owner_chip)
            def _():
                for k in range(1, 4):
                    to_owner(k).wait_recv()
                own_copy().wait()

            pl.when(jnp.logical_not(is_owner_chip))(lambda: to_owner(chip).wait_send())
            to_sibling().wait_send()
            gates()[1]()


def _fanout_sems(n):
    return [pltpu.SemaphoreType.DMA(((NDEV - 1) * n,)), pltpu.SemaphoreType.DMA(((NDEV - 1) * n,)), pltpu.SemaphoreType.DMA((n,))]


def _lanes(j):
    return pl.ds(pl.multiple_of(j * 128, 128), 128)


def _gather_first(c_row, cctx_row, wm, bm, wi, gate):
    def body(c_ref, cctx_ref, wm_ref, bm_ref, wi_ref, g_ref, cs_ref, mods_ref, owi, og, call_ref, mine_ref,
             send_sems, recv_sems, local_sems, c_send, c_recv, c_local, m_send, m_recv, m_local):
        x, y, c = lax.axis_index("x"), lax.axis_index("y"), lax.axis_index("c")
        sibling = (x, y, 1 - c)
        chips = [(1 - x, y), (x, 1 - y), (1 - x, 1 - y)]
        index = lambda px, py, pc: 4 * px + 2 * py + pc
        arrays = ((wi_ref, owi), (g_ref, og))
        n = len(arrays)

        def copy(a, k, block, to, own=False):
            src, out = arrays[a]
            return pltpu.make_async_remote_copy(
                src_ref=src if own else out.at[index(*block)], dst_ref=out.at[index(*block)],
                send_sem=send_sems.at[k * n + a], recv_sem=recv_sems.at[k * n + a], device_id=to, device_id_type=MESH)

        c_start, c_finish = _fanout([lambda j: c_ref], [lambda j: call_ref.at[j]], c_send, c_recv, c_local)
        c_start()
        mine = [pltpu.make_async_copy(src, out.at[index(x, y, c)], local_sems.at[a]) for a, (src, out) in enumerate(arrays)]
        first = [copy(a, 0, (x, y, c), sibling, own=True) for a in range(n)]
        first += [copy(a, 1 + j, (x, y, c), (*chip, c), own=True) for j, chip in enumerate(chips) for a in range(n)]
        for cp in mine + first:
            cp.start()

        c_finish()
        cs = jnp.concatenate([call_ref[j] for j in range(NDEV)] + [cctx_ref[...], jnp.zeros((16 - NDEV - 1, D), f32)], axis=0)
        cs_ref[...] = cs
        s, _ = _silu_and_grad(cs)
        mine_ref[...] = _dot_hi(s, wm_ref[...]) + bm_ref[...]
        m_start, m_finish = _fanout([lambda j: mine_ref], [lambda j: mods_ref.at[j]], m_send, m_recv, m_local)
        m_start()

        passed = []
        for j, chip in enumerate(chips):
            for a in range(n):
                copy(a, 1 + j, (*chip, c), (x, y, c)).wait_recv()
            for a in range(n):
                cp = copy(a, 4 + j, (*chip, c), sibling)
                cp.start()
                passed.append(cp)
        for a in range(n):
            copy(a, 0, sibling, (x, y, c)).wait_recv()
        for j, chip in enumerate(chips):
            for a in range(n):
                copy(a, 4 + j, (*chip, 1 - c), (x, y, c)).wait_recv()
        for cp in first + passed:
            cp.wait_send()
        for cp in mine:
            cp.wait()
        m_finish()

    hbm = pl.BlockSpec(memory_space=pl.ANY)
    vm = pl.BlockSpec(memory_space=pltpu.VMEM)
    ncol = wm.shape[1]
    return pl.pallas_call(
        body, name="gather_first",
        out_shape=(jax.ShapeDtypeStruct((16, D), f32), jax.ShapeDtypeStruct((NDEV, 16, ncol), f32),
                   jax.ShapeDtypeStruct((NDEV,) + wi.shape, bf16), jax.ShapeDtypeStruct((NDEV,) + gate.shape, f32)),
        in_specs=[vm, vm, vm, vm, hbm, hbm], out_specs=(vm, vm, hbm, hbm),
        scratch_shapes=[pltpu.VMEM((NDEV, 1, D), f32), pltpu.VMEM((16, ncol), f32)] + _fanout_sems(2) + _fanout_sems(1) + _fanout_sems(1),
        compiler_params=_cp(),
    )(c_row, cctx_row, wm, bm, wi, gate)


SHARD = INW // NDEV
ROWS_RP = 128


def _overlap(lo, hi, a, b):
    s, e = max(lo, a), min(hi, b)
    return (s, e) if s < e else None


def _repack_w(wg):
    segs = ((0, 1024, PQ), (1024, 1024 + 2 * RANK, PLR), (1024 + 2 * RANK, INW, PZB))

    def body(g_ref, o_ref):
        for j in range(NDEV):
            lo, hi = j * SHARD, (j + 1) * SHARD
            for a, b, pad0 in segs:
                ov = _overlap(lo, hi, a, b)
                if ov:
                    s, e = ov
                    o_ref[:, pad0 + s - a:pad0 + e - a] = g_ref[j, :, s - lo:e - lo]
        o_ref[:, PLR + 2 * RANK:PW] = jnp.zeros((ROWS_RP, PW - PLR - 2 * RANK), bf16)

    return pl.pallas_call(
        body, name="repack_w", grid=(D // ROWS_RP,), out_shape=jax.ShapeDtypeStruct((D, PW), bf16),
        in_specs=[pl.BlockSpec((NDEV, ROWS_RP, SHARD), lambda i: (0, i, 0))],
        out_specs=pl.BlockSpec((ROWS_RP, PW), lambda i: (i, 0)), compiler_params=_cp(("arbitrary",)),
    )(wg)


LATE_END = 1024 + 2 * RANK
LATE_DESTS = 2


def _pack_blocks(o_ref, srcs, dests, dtype):
    for n, j in enumerate(dests):
        lo, hi = j * SHARD, (j + 1) * SHARD
        done = lo
        for a, b, src in srcs:
            ov = _overlap(lo, hi, a, b)
            if ov:
                s, e = ov
                if s > done:
                    o_ref[n, :, done - lo:s - lo] = jnp.zeros((ROWS_RP, s - done), dtype)
                o_ref[n, :, s - lo:e - lo] = src[:, s - a:e - a].astype(dtype)
                done = e
        if done < hi:
            o_ref[n, :, done - lo:hi - lo] = jnp.zeros((ROWS_RP, hi - done), dtype)


class _Pack:
    def __init__(self, extra, outs, fn):
        self.extra, self.outs, self.fn = list(extra), list(outs), fn


def _pack_early():
    def fn(accs, extra, outs):
        zbua, va, zag = accs
        for r0 in range(0, D, ROWS_RP):
            rows = pl.ds(r0, ROWS_RP)
            out, o0 = (outs[0], r0) if r0 < EARLY_A_ROWS else (outs[1], r0 - EARLY_A_ROWS)
            _pack_blocks(out.at[:, pl.ds(o0, ROWS_RP)], ((LATE_END, 2080, zbua.at[rows]), (2080, 2592, va.at[rows]), (2592, INW, zag.at[rows])),
                         range(NDEV), bf16)

    sds = jax.ShapeDtypeStruct
    return _Pack([], [sds((NDEV, EARLY_A_ROWS, SHARD), bf16), sds((NDEV, D - EARLY_A_ROWS, SHARD), bf16)], fn)


def _pack_late(dwk_c, dwv_c, dwl_c):
    def fn(accs, extra, outs):
        qkv_ref, lr_ref = accs
        kc_ref, vc_ref, lc_ref = extra
        for r0 in range(0, D, ROWS_RP):
            rows = pl.ds(r0, ROWS_RP)
            qkv = qkv_ref[rows, :] + jnp.concatenate([jnp.zeros((ROWS_RP, KW), f32), kc_ref[rows, :], vc_ref[rows, :]], axis=1)
            lr = lr_ref[rows, :] + lc_ref[rows, :]
            _pack_blocks(outs[0].at[:, rows], ((0, 1024, qkv), (1024, LATE_END, lr)), range(LATE_DESTS), bf16)

    return _Pack([dwk_c, dwv_c, dwl_c], [jax.ShapeDtypeStruct((LATE_DESTS, D, SHARD), bf16)], fn)


def _mod_tail(dnorm, dbm, dmod, dmodc, cs, wm):
    ncol = wm.shape[2]

    def body(dn_ref, dbm_ref, dmod_ref, dmodc_ref, cs_ref, wm_ref, g_wm, g_ng, g_bm, g_cc, *rest):
        a_dn, a_dbm, a_dmod, a_dmodc, dm_rows, gc_ref, a_gc = rest[0:7]
        s1, r1, l1, s2, r2, l2 = rest[7:13]
        start, finish = _fanout([lambda j: dn_ref, lambda j: dbm_ref, lambda j: dmod_ref, lambda j: dmodc_ref],
                                [lambda j, r=r: r.at[j] for r in (a_dn, a_dbm, a_dmod, a_dmodc)], s1, r1, l1)
        start()
        finish()

        def total(ref):
            t = ref[0]
            for j in range(1, NDEV):
                t = t + ref[j]
            return t

        g_ng[...] = total(a_dn)
        g_bm[...] = total(a_dbm)
        dmodc_tot = jnp.concatenate([total(a_dmodc), jnp.zeros((1, D), f32)], axis=1)
        dm_rows[...] = jnp.concatenate([a_dmod[j] for j in range(NDEV)] + [dmodc_tot, jnp.zeros((16 - NDEV - 1, 3 * D), f32)], axis=0)
        dm = dm_rows[:, pl.ds(pl.multiple_of(_me() * ncol, 128), ncol)]
        s, ds = _silu_and_grad(cs_ref[...])
        part = lax.dot_general(dm[8:9, :], wm_ref[0], (((1,), (1,)), ((), ())), preferred_element_type=f32,
                               precision=lax.Precision.HIGHEST)
        gc_ref[...] = part * ds[8:9, :]
        start2, finish2 = _fanout([lambda j: gc_ref], [lambda j: a_gc.at[j]], s2, r2, l2)
        start2()
        g = lax.dot_general(s, dm, (((0,), (0,)), ((), ())), preferred_element_type=f32, precision=lax.Precision.HIGHEST)
        g_wm[...] = g[None]
        finish2()
        g_cc[...] = total(a_gc)

    sds = jax.ShapeDtypeStruct
    row = lambda n: pltpu.VMEM((NDEV, 1, n), f32)
    return pl.pallas_call(
        body, name="mod_tail", out_shape=(sds(wm.shape, f32), sds((1, D), f32), sds((1, 3 * D), f32), sds((1, D), f32)),
        scratch_shapes=[row(D), row(3 * D), row(3 * D), row(2 * D), pltpu.VMEM((16, 3 * D), f32), pltpu.VMEM((1, D), f32), row(D)]
        + _fanout_sems(4) + _fanout_sems(1),
        compiler_params=_cp(),
    )(dnorm, dbm, dmod, dmodc, cs, wm)


def _adam_update(g, w_ref, m_ref, v_ref, go_ref, d_ref, mo_ref, vo_ref):
    c1 = 1.0 / (1.0 - ADAM_B1 ** ADAM_STEP)
    c2 = 1.0 / (1.0 - ADAM_B2 ** ADAM_STEP)
    mn = ADAM_B1 * m_ref[...] + (1.0 - ADAM_B1) * g
    vn = ADAM_B2 * v_ref[...] + (1.0 - ADAM_B2) * (g * g)
    go_ref[...] = g
    mo_ref[...] = mn
    vo_ref[...] = vn
    d_ref[...] = -ADAM_LR * ((mn * c1) / (jnp.sqrt(vn * c2) + ADAM_EPS) + ADAM_WD * w_ref[...])


def _adam_w_in(parts_a, parts_b, parts_late, w, m, v):
    na = EARLY_A_ROWS // ROWS_RP
    whole = SHARD // 128 * 128

    def body(a_ref, b_ref, l_ref, w_ref, m_ref, v_ref, go_ref, d_ref, mo_ref, vo_ref):
        me = _me()
        first = pl.program_id(0) < na
        early = jnp.where(first, a_ref[0], b_ref[0]).astype(f32)
        for i in range(1, NDEV):
            early = early + jnp.where(first, a_ref[i], b_ref[i]).astype(f32)
        late = l_ref[0].astype(f32)
        for i in range(1, 4):
            late = late + l_ref[i].astype(f32)
        g = jnp.where(me >= LATE_DESTS - 1, early, 0.0) + jnp.where(me < LATE_DESTS, late, 0.0)
        gt = jnp.concatenate([g[:, c:c + 128].T for c in range(0, whole, 128)] + [g[:, SHARD - 128:].T[128 - (SHARD - whole):]],
                             axis=0)
        _adam_update(gt, w_ref, m_ref, v_ref, go_ref, d_ref, mo_ref, vo_ref)

    blk = pl.BlockSpec((SHARD, ROWS_RP), lambda i: (0, i))
    res = pl.pallas_call(
        body, name="adam_w_in", grid=(D // ROWS_RP,), out_shape=tuple(jax.ShapeDtypeStruct((SHARD, D), f32) for _ in range(4)),
        in_specs=[pl.BlockSpec((NDEV, ROWS_RP, SHARD), lambda i: (0, jnp.minimum(i, na - 1), 0)),
                  pl.BlockSpec((NDEV, ROWS_RP, SHARD), lambda i: (0, jnp.maximum(i - na, 0), 0)),
                  pl.BlockSpec((4, ROWS_RP, SHARD), lambda i: (0, i, 0)), blk, blk, blk],
        out_specs=(blk, blk, blk, blk), compiler_params=_cp(("arbitrary",)),
    )(parts_a, parts_b, parts_late, jnp.transpose(w[0]), jnp.transpose(m[0]), jnp.transpose(v[0]))
    return tuple(jnp.transpose(t)[None] for t in res)


def _adam_params(parts, ws, ms, vs, name):
    p = parts.shape[0]
    k = len(ws)
    nrows = [w.size // 128 for w in ws]
    starts = [sum(nrows[:i]) for i in range(k)]

    def shaped(g, shape):
        if len(shape) == 2:
            return jnp.concatenate([g[r:r + 1] for r in range(g.shape[0])], axis=1)
        return g.reshape(shape)

    def body(*refs):
        g_ref = refs[0]
        w_refs, m_refs, v_refs = refs[1:1 + k], refs[1 + k:1 + 2 * k], refs[1 + 2 * k:1 + 3 * k]
        outs = refs[1 + 3 * k:]
        for i in range(k):
            rows = slice(starts[i], starts[i] + nrows[i])
            g = g_ref[0, rows, :]
            for j in range(1, p):
                g = g + g_ref[j, rows, :]
            _adam_update(shaped(g, ws[i].shape), w_refs[i], m_refs[i], v_refs[i], outs[i], outs[k + i], outs[2 * k + i], outs[3 * k + i])

    vm = pl.BlockSpec(memory_space=pltpu.VMEM)
    res = pl.pallas_call(
        body, name=name, out_shape=tuple(jax.ShapeDtypeStruct(w.shape, f32) for _ in range(4) for w in ws),
        in_specs=[vm] * (1 + 3 * k), out_specs=(vm,) * (4 * k), compiler_params=_cp(),
    )(parts, *ws, *ms, *vs)
    return [res[i * k:(i + 1) * k] for i in range(4)]


def _adam_blocks(parts, ws, ms, vs, name):
    k = len(ws)

    def body(*refs):
        g_refs, w_refs, m_refs, v_refs = refs[0:k], refs[k:2 * k], refs[2 * k:3 * k], refs[3 * k:4 * k]
        outs = refs[4 * k:]
        for i in range(k):
            g = g_refs[i][0].astype(f32)
            for j in range(1, parts[i].shape[0]):
                g = g + g_refs[i][j].astype(f32)
            _adam_update(g[None], w_refs[i], m_refs[i], v_refs[i], outs[i], outs[k + i], outs[2 * k + i], outs[3 * k + i])

    vm = pl.BlockSpec(memory_space=pltpu.VMEM)
    res = pl.pallas_call(
        body, name=name, out_shape=tuple(jax.ShapeDtypeStruct(w.shape, f32) for _ in range(4) for w in ws),
        in_specs=[vm] * (4 * k), out_specs=(vm,) * (4 * k), compiler_params=_cp(),
    )(*parts, *ws, *ms, *vs)
    return [res[i * k:(i + 1) * k] for i in range(4)]


def _adam_plain(gs, ws, ms, vs, name):
    k = len(ws)

    def body(*refs):
        g_refs, w_refs, m_refs, v_refs = refs[0:k], refs[k:2 * k], refs[2 * k:3 * k], refs[3 * k:4 * k]
        outs = refs[4 * k:]
        for i in range(k):
            _adam_update(g_refs[i][...], w_refs[i], m_refs[i], v_refs[i], outs[i], outs[k + i], outs[2 * k + i], outs[3 * k + i])

    vm = pl.BlockSpec(memory_space=pltpu.VMEM)
    res = pl.pallas_call(
        body, name=name, out_shape=tuple(jax.ShapeDtypeStruct(w.shape, f32) for _ in range(4) for w in ws),
        in_specs=[vm] * (4 * k), out_specs=(vm,) * (4 * k), compiler_params=_cp(),
    )(*gs, *ws, *ms, *vs)
    return [res[i * k:(i + 1) * k] for i in range(4)]


def _adam_gate(parts, w2, gb, m_w2, m_gb, v_w2, v_gb):
    per = 128 // 32

    def body(p_ref, w2_ref, gb_ref, mw_ref, mb_ref, vw_ref, vb_ref, *outs):
        g = p_ref[0]
        for j in range(1, parts.shape[0]):
            g = g + p_ref[j]
        for d in range(2):
            gd = jnp.concatenate([g[(d * RANK + r) // per:(d * RANK + r) // per + 1, (r % per) * 32:(r % per + 1) * 32]
                                  for r in range(RANK)], axis=0)
            _adam_update(gd, *(t.at[0, d] for t in (w2_ref, mw_ref, vw_ref) + outs[0:4]))
        last = 2 * RANK // per
        gbv = jnp.concatenate([g[last:last + 1, d * 32:(d + 1) * 32] for d in range(2)], axis=0)
        _adam_update(gbv, *(t.at[0] for t in (gb_ref, mb_ref, vb_ref) + outs[4:8]))

    vm = pl.BlockSpec(memory_space=pltpu.VMEM)
    return pl.pallas_call(
        body, name="adam_gate", out_shape=tuple(jax.ShapeDtypeStruct(w.shape, f32) for w in (w2, gb) for _ in range(4)),
        in_specs=[vm] * 7, out_specs=(vm,) * 8, compiler_params=_cp(),
    )(parts, w2, gb, m_w2, m_gb, v_w2, v_gb)


def _in_proj(x, g, scale, shift, w_pad, tl, comm):
    l = x.shape[0]
    c_in, c_out, c_sems = comm.specs()

    def body(*refs):
        x_ref, g_ref, sc_ref, sh_ref, w_ref = refs[:5]
        cin = refs[5:5 + len(c_in)]
        p_ref, h_ref = refs[5 + len(c_in):7 + len(c_in)]
        cout = refs[7 + len(c_in):7 + len(c_in) + comm.n]
        sems = refs[7 + len(c_in) + comm.n:]

        def compute():
            xv = x_ref[...]
            r = lax.rsqrt(jnp.mean(xv * xv, axis=-1, keepdims=True) + EPS)
            h = (xv * r) * (g_ref[...] * (1.0 + sc_ref[...])) + sh_ref[...]
            hb = h.astype(bf16)
            h_ref[...] = hb
            p_ref[...] = jnp.dot(hb, w_ref[...], preferred_element_type=f32).astype(bf16)

        comm.run(cin, cout, sems, l // tl, compute)

    vec = pl.BlockSpec((1, D), lambda i: (0, 0))
    return pl.pallas_call(
        body, name="in_proj", grid=(l // tl,),
        out_shape=(jax.ShapeDtypeStruct((l, PW), bf16), jax.ShapeDtypeStruct((l, D), bf16)) + tuple(comm.outs),
        in_specs=[pl.BlockSpec((tl, D), lambda i: (i, 0)), vec, vec, vec, pl.BlockSpec((D, PW), lambda i: (0, 0))] + c_in,
        out_specs=(pl.BlockSpec((tl, PW), lambda i: (i, 0)), pl.BlockSpec((tl, D), lambda i: (i, 0))) + tuple(c_out),
        scratch_shapes=c_sems, compiler_params=_cp(("arbitrary",)),
    )(x, g, scale, shift, w_pad, *comm.ins)


def _tri(rev):
    i = lax.broadcasted_iota(jnp.int32, (CH, CH), 0)
    j = lax.broadcasted_iota(jnp.int32, (CH, CH), 1)
    return jnp.where((j >= i) if rev else (j <= i), 1.0, 0.0).astype(f32)


def _head_masks():
    lane = lax.broadcasted_iota(jnp.int32, (1, KW), 1) // HK
    return [jnp.where(lane == h, 1.0, 0.0).astype(f32) for h in range(NH)]


def _block_diag():
    r = lax.broadcasted_iota(jnp.int32, (VW, KW), 0) // HV
    c = lax.broadcasted_iota(jnp.int32, (VW, KW), 1) // HK
    return jnp.where(r == c, 1.0, 0.0).astype(f32)


def _decay(lr, w2, gb, tri, rev):
    logits = _dot(lr, w2) + gb
    a = _log_sigmoid(logits) * (1.0 / TAU)
    c = _dot_hi(tri, a)
    cl = c[0:1, :] if rev else c[CH - 1:CH, :]
    return logits, c, cl


def _state_fwd(k, v, c, cl, st, bd):
    return st * jnp.exp(cl) + bd * _tn(v, k * jnp.exp(cl - c))


def _state_bwd(k, v, c, cl, st0, dst, trit):
    ecl = jnp.exp(cl)
    edec = jnp.exp(cl - c)
    kdec = k * edec
    dv = _nt(kdec, dst)
    dkdec = _dot(v, dst)
    dcl = jnp.sum(dst * st0, axis=0, keepdims=True) * ecl + jnp.sum(dkdec * kdec, axis=0, keepdims=True)
    da = _dot_hi(trit, -dkdec * kdec) + dcl
    return dkdec * edec, dv, da, dst * ecl


def _bdot(a, b):
    return lax.dot_general(a.astype(bf16), b.astype(bf16), (((2,), (1,)), ((0,), (0,))), preferred_element_type=f32)


def _bnt(a, b):
    return lax.dot_general(a.astype(bf16), b.astype(bf16), (((2,), (2,)), ((0,), (0,))), preferred_element_type=f32)


def _btn(a, b):
    return lax.dot_general(a.astype(bf16), b.astype(bf16), (((1,), (1,)), ((0,), (0,))), preferred_element_type=f32)


def _scan_chunks(x, rev):
    nc = x.shape[0]
    hi = x.astype(bf16)
    r1 = x - hi.astype(f32)
    mid = r1.astype(bf16)
    lo = (r1 - mid.astype(f32)).astype(bf16)
    terms = jnp.concatenate([hi, mid, lo], axis=1)
    tri3 = jnp.broadcast_to(jnp.concatenate([_tri(rev)] * 3, axis=1).astype(bf16)[None], (nc, CH, 3 * CH))
    return lax.dot_general(tri3, terms, (((2,), (1,)), ((0,), (0,))), preferred_element_type=f32)


class _Tile:
    pass


def _tile_prep(q_ref, k_ref, lr_ref, w2_ref, gb_ref, rev, nc):
    t = _Tile()
    tg = nc * CH
    t.logits = _dot(lr_ref[...], w2_ref[...]) + gb_ref[...]
    c = _scan_chunks((_log_sigmoid(t.logits) * (1.0 / TAU)).reshape(nc, CH, KW), rev)
    cl = c[:, 0:1, :] if rev else c[:, CH - 1:CH, :]
    k = k_ref[...].astype(f32).reshape(nc, CH, KW)
    t.ec, t.enc, t.edec, t.ecl = jnp.exp(c), jnp.exp(-c), jnp.exp(cl - c), jnp.exp(cl)
    t.qd = q_ref[...].astype(f32).reshape(nc, CH, KW) * t.ec * QSCALE
    t.kd = k * t.enc
    t.kdec = k * t.edec
    hm = _head_masks()
    t.tri4 = jnp.concatenate([_tri(rev)] * NH, axis=0)[None]
    t.qs = jnp.concatenate([t.qd * hm[h] for h in range(NH)], axis=1)
    t.pst = _bnt(t.qs, t.kd) * t.tri4
    return t


def _gla_fwd(p, w2p, gb, s0, rev, tg, name):
    l = p.shape[0]
    nb, nc = l // tg, tg // CH

    def body(q_ref, k_ref, v_ref, lr_ref, w2_ref, gb_ref, s0_ref, o_ref, st_ref, st):
        @pl.when(pl.program_id(0) == 0)
        def _():
            st[...] = s0_ref[...]

        t = _tile_prep(q_ref, k_ref, lr_ref, w2_ref, gb_ref, rev, nc)
        v = v_ref[...].reshape(nc, CH, VW)
        intra = jnp.concatenate([_bdot(t.pst[:, h * CH:(h + 1) * CH], v[:, :, h * HV:(h + 1) * HV]) for h in range(NH)], axis=2)
        bd = _block_diag()
        s = st[...]
        for n in (range(nc - 1, -1, -1) if rev else range(nc)):
            st_ref[n] = s.astype(bf16)
            s = s * t.ecl[n] + bd * _tn(v[n], t.kdec[n])
        st[...] = s
        o_ref[...] = (_bnt(t.qd, st_ref[...]) + intra).reshape(tg, VW).astype(bf16)

    blk = (lambda i: nb - 1 - i) if rev else (lambda i: i)
    return pl.pallas_call(
        body, name=name, grid=(nb,),
        out_shape=(jax.ShapeDtypeStruct((l, VW), bf16), jax.ShapeDtypeStruct((l // CH, VW, KW), bf16)),
        in_specs=[pl.BlockSpec((tg, KW), lambda i: (blk(i), PQ // KW)), pl.BlockSpec((tg, KW), lambda i: (blk(i), PK // KW)),
                  pl.BlockSpec((tg, VW), lambda i: (blk(i), PV // VW)), pl.BlockSpec((tg, LRW), lambda i: (blk(i), PLR // LRW)),
                  pl.BlockSpec((LRW, KW), lambda i: (0, 0)), pl.BlockSpec((1, KW), lambda i: (0, 0)),
                  pl.BlockSpec((VW, KW), lambda i: (0, 0))],
        out_specs=(pl.BlockSpec((tg, VW), lambda i: (blk(i), 0)), pl.BlockSpec((nc, VW, KW), lambda i: (blk(i), 0, 0))),
        scratch_shapes=[pltpu.VMEM((VW, KW), f32)],
        compiler_params=_cp(("arbitrary",)),
    )(p, p, p, p, w2p, gb, s0)


def _gla_bwd(p, do, states, w2p, gb, prev, rev, tg, name, comm):
    l = p.shape[0]
    nb, nc = l // tg, tg // CH
    out_dt = bf16
    c_in, c_out, c_sems = comm.specs()

    def body(*refs):
        q_ref, k_ref, v_ref, lr_ref, do_ref, st_ref, w2_ref, gb_ref = refs[:8]
        refs = refs[8:]
        if prev is not None:
            pq_ref, pl_ref = refs[:2]
            refs = refs[2:]
        cin, refs = refs[:len(c_in)], refs[len(c_in):]
        dqkv_ref, dlr_ref, dw2_ref, dgb_ref, ds0_ref = refs[:5]
        cout, dst, ds_buf, sems = refs[5:5 + comm.n], refs[5 + comm.n], refs[6 + comm.n], refs[7 + comm.n:]
        comm.run(cin, cout, sems, nb, lambda: compute(q_ref, k_ref, v_ref, lr_ref, do_ref, st_ref, w2_ref, gb_ref,
                                                      pq_ref if prev is not None else None, pl_ref if prev is not None else None,
                                                      dqkv_ref, dlr_ref, dw2_ref, dgb_ref, ds0_ref, dst, ds_buf))

    def compute(q_ref, k_ref, v_ref, lr_ref, do_ref, st_ref, w2_ref, gb_ref, pq_ref, pl_ref,
                dqkv_ref, dlr_ref, dw2_ref, dgb_ref, ds0_ref, dst, ds_buf):
        @pl.when(pl.program_id(0) == 0)
        def _():
            dst[...] = jnp.zeros_like(dst)
            dw2_ref[...] = jnp.zeros_like(dw2_ref)
            dgb_ref[...] = jnp.zeros_like(dgb_ref)

        t = _tile_prep(q_ref, k_ref, lr_ref, w2_ref, gb_ref, rev, nc)
        hm = _head_masks()
        v = v_ref[...].reshape(nc, CH, VW)
        do = do_ref[...].reshape(nc, CH, VW)
        heads = lambda a, h: a[:, :, h * HV:(h + 1) * HV]
        dpst = jnp.concatenate([_bnt(heads(do, h), heads(v, h)) for h in range(NH)], axis=1) * t.tri4
        dv = jnp.concatenate([_btn(t.pst[:, h * CH:(h + 1) * CH], heads(do, h)) for h in range(NH)], axis=2)
        dqd = _bdot(do, st_ref[...])
        for h in range(NH):
            dqd = dqd + hm[h] * _bdot(dpst[:, h * CH:(h + 1) * CH], t.kd)
        dkd = _btn(dpst, t.qs)
        bd = _block_diag()
        d = dst[...]
        for n in (range(nc) if rev else range(nc - 1, -1, -1)):
            ds_buf[n] = d
            d = d * t.ecl[n] + bd * _tn(do[n], t.qd[n])
        dst[...] = d
        ds0_ref[...] = d
        ds = ds_buf[...]
        dv = dv + _bnt(t.kdec, ds)
        dkdec = _bdot(v, ds)
        dcl = jnp.sum(ds * st_ref[...].astype(f32), axis=1, keepdims=True) * t.ecl + jnp.sum(dkdec * t.kdec, axis=1, keepdims=True)
        dc = dqd * t.qd - dkd * t.kd - dkdec * t.kdec
        da = _scan_chunks(dc, not rev) + dcl
        dq = dqd * t.ec * QSCALE
        dk = dkd * t.enc + dkdec * t.edec
        dlog = da.reshape(tg, KW) * _sigmoid(-t.logits) * (1.0 / TAU)
        dlr = _nt(dlog, w2_ref[...])
        dw2_ref[...] += _tn(lr_ref[...], dlog)
        dgb_ref[...] += jnp.sum(dlog, axis=0, keepdims=True)
        dqkv = jnp.concatenate([dq, dk, dv], axis=2).reshape(tg, 2 * KW + VW)
        if prev is not None:
            dqkv = dqkv + pq_ref[...]
            dlr = dlr + pl_ref[...]
        dqkv_ref[...] = dqkv.astype(out_dt)
        dlr_ref[...] = dlr.astype(out_dt)

    blk = (lambda i: i) if rev else (lambda i: nb - 1 - i)
    in_specs = [pl.BlockSpec((tg, KW), lambda i: (blk(i), PQ // KW)), pl.BlockSpec((tg, KW), lambda i: (blk(i), PK // KW)),
                pl.BlockSpec((tg, VW), lambda i: (blk(i), PV // VW)), pl.BlockSpec((tg, LRW), lambda i: (blk(i), PLR // LRW)),
                pl.BlockSpec((tg, VW), lambda i: (blk(i), 0)), pl.BlockSpec((nc, VW, KW), lambda i: (blk(i), 0, 0)),
                pl.BlockSpec((LRW, KW), lambda i: (0, 0)), pl.BlockSpec((1, KW), lambda i: (0, 0))]
    args = [p, p, p, p, do, states, w2p, gb]
    if prev is not None:
        in_specs += [pl.BlockSpec((tg, 2 * KW + VW), lambda i: (blk(i), 0)), pl.BlockSpec((tg, LRW), lambda i: (blk(i), 0))]
        args += list(prev)
    return pl.pallas_call(
        body, name=name, grid=(nb,),
        out_shape=(jax.ShapeDtypeStruct((l, 2 * KW + VW), out_dt), jax.ShapeDtypeStruct((l, LRW), out_dt),
                   jax.ShapeDtypeStruct((LRW, KW), f32), jax.ShapeDtypeStruct((1, KW), f32), jax.ShapeDtypeStruct((VW, KW), f32))
        + tuple(comm.outs),
        in_specs=in_specs + c_in,
        out_specs=(pl.BlockSpec((tg, 2 * KW + VW), lambda i: (blk(i), 0)), pl.BlockSpec((tg, LRW), lambda i: (blk(i), 0)),
                   pl.BlockSpec((LRW, KW), lambda i: (0, 0)), pl.BlockSpec((1, KW), lambda i: (0, 0)),
                   pl.BlockSpec((VW, KW), lambda i: (0, 0))) + tuple(c_out),
        scratch_shapes=[pltpu.VMEM((VW, KW), f32), pltpu.VMEM((nc, VW, KW), f32)] + c_sems,
        compiler_params=_cp(("arbitrary",)),
    )(*args, *comm.ins)


def _ctx_hidden(ctx_ref, g_ref, sc_ref, sh_ref):
    xv = ctx_ref[...]
    r = lax.rsqrt(jnp.mean(xv * xv, axis=-1, keepdims=True) + EPS)
    xn = xv * r
    return xn, xn * (g_ref[...] * (1.0 + sc_ref[...])) + sh_ref[...]


_CTX_W_SPECS = [pl.BlockSpec((D, KW), lambda i: (0, PK // KW)), pl.BlockSpec((D, VW), lambda i: (0, PV // VW)),
                pl.BlockSpec((D, LRW), lambda i: (0, PLR // LRW))]


def _ctx_fwd(ctx, g, scale, shift, w_pad, w2f, w2b, gbf, gbb):
    ncc = CTX // CH

    def body(ctx_ref, g_ref, sc_ref, sh_ref, wk_ref, wv_ref, wl_ref, w2f_ref, w2b_ref, gbf_ref, gbb_ref, sf_ref, sb_ref):
        _, hc = _ctx_hidden(ctx_ref, g_ref, sc_ref, sh_ref)
        k, v, lr = _dot(hc, wk_ref[...]), _dot(hc, wv_ref[...]), _dot(hc, wl_ref[...])
        bd = _block_diag()
        for rev, w2_ref, gb_ref, out in ((False, w2f_ref, gbf_ref, sf_ref), (True, w2b_ref, gbb_ref, sb_ref)):
            tri = _tri(rev)
            st = jnp.zeros((VW, KW), f32)
            for j in (range(ncc - 1, -1, -1) if rev else range(ncc)):
                rows = slice(j * CH, (j + 1) * CH)
                _, c, cl = _decay(lr[rows], w2_ref[...], gb_ref[...], tri, rev)
                st = _state_fwd(k[rows], v[rows], c, cl, st, bd)
            out[...] = st

    vec = pl.BlockSpec((1, D), lambda i: (0, 0))
    w2s = pl.BlockSpec((LRW, KW), lambda i: (0, 0))
    gbs = pl.BlockSpec((1, KW), lambda i: (0, 0))
    sts = pl.BlockSpec((VW, KW), lambda i: (0, 0))
    return pl.pallas_call(
        body, name="ctx_fwd", grid=(1,), out_shape=(jax.ShapeDtypeStruct((VW, KW), f32),) * 2,
        in_specs=[pl.BlockSpec((CTX, D), lambda i: (0, 0)), vec, vec, vec] + _CTX_W_SPECS + [w2s, w2s, gbs, gbs],
        out_specs=(sts, sts), compiler_params=_cp(("arbitrary",)),
    )(ctx, g, scale, shift, w_pad, w_pad, w_pad, w2f, w2b, gbf, gbb)


def _ctx_bwd(ctx, g, scale, shift, w_pad, w2f, w2b, gbf, gbb, dsf, dsb):
    ncc = CTX // CH

    def body(ctx_ref, g_ref, sc_ref, sh_ref, wk_ref, wv_ref, wl_ref, w2f_ref, w2b_ref, gbf_ref, gbb_ref, dsf_ref, dsb_ref,
             dwk_ref, dwv_ref, dwl_ref, dmod_ref, dg_ref, dw2_ref, dgb_ref):
        xn, hc = _ctx_hidden(ctx_ref, g_ref, sc_ref, sh_ref)
        k, v, lr = _dot(hc, wk_ref[...]), _dot(hc, wv_ref[...]), _dot(hc, wl_ref[...])
        bd = _block_diag()
        dk_rows, dv_rows, dl_rows = [None] * ncc, [None] * ncc, [None] * ncc
        for d, (rev, w2_ref, gb_ref, ds_ref) in enumerate(((False, w2f_ref, gbf_ref, dsf_ref), (True, w2b_ref, gbb_ref, dsb_ref))):
            tri = _tri(rev)
            order = list(range(ncc - 1, -1, -1) if rev else range(ncc))
            st, saved = jnp.zeros((VW, KW), f32), {}
            for j in order:
                rows = slice(j * CH, (j + 1) * CH)
                logits, c, cl = _decay(lr[rows], w2_ref[...], gb_ref[...], tri, rev)
                saved[j] = (logits, c, cl, st)
                st = _state_fwd(k[rows], v[rows], c, cl, st, bd)
            dst = ds_ref[...]
            dw2 = jnp.zeros((LRW, KW), f32)
            dgb = jnp.zeros((1, KW), f32)
            for j in reversed(order):
                rows = slice(j * CH, (j + 1) * CH)
                logits, c, cl, st0 = saved[j]
                dk, dv, da, dst = _state_bwd(k[rows], v[rows], c, cl, st0, dst, _tri(not rev))
                dlog = da * _sigmoid(-logits) * (1.0 / TAU)
                dl = _nt(dlog, w2_ref[...])
                dw2 = dw2 + _tn(lr[rows], dlog)
                dgb = dgb + jnp.sum(dlog, axis=0, keepdims=True)
                dk_rows[j] = dk if dk_rows[j] is None else dk_rows[j] + dk
                dv_rows[j] = dv if dv_rows[j] is None else dv_rows[j] + dv
                dl_rows[j] = dl if dl_rows[j] is None else dl_rows[j] + dl
            dw2_ref[d] = dw2
            dgb_ref[d] = dgb
        dk, dv, dl = (jnp.concatenate(t, axis=0) for t in (dk_rows, dv_rows, dl_rows))
        dwk_ref[...] = _tn(hc, dk)
        dwv_ref[...] = _tn(hc, dv)
        dwl_ref[...] = _tn(hc, dl)
        dh = _nt(dk, wk_ref[...]) + _nt(dv, wv_ref[...]) + _nt(dl, wl_ref[...])
        gx = dh * xn
        dmod_ref[:, 0:D] = jnp.sum(dh, axis=0, keepdims=True)
        dmod_ref[:, D:2 * D] = jnp.sum(gx, axis=0, keepdims=True) * g_ref[...]
        dg_ref[...] = jnp.sum(gx, axis=0, keepdims=True) * (1.0 + sc_ref[...])

    vec = pl.BlockSpec((1, D), lambda i: (0, 0))
    w2s = pl.BlockSpec((LRW, KW), lambda i: (0, 0))
    gbs = pl.BlockSpec((1, KW), lambda i: (0, 0))
    sts = pl.BlockSpec((VW, KW), lambda i: (0, 0))
    full = lambda *s: pl.BlockSpec(s, lambda i: (0,) * len(s))
    return pl.pallas_call(
        body, name="ctx_bwd", grid=(1,),
        out_shape=(jax.ShapeDtypeStruct((D, KW), f32), jax.ShapeDtypeStruct((D, VW), f32), jax.ShapeDtypeStruct((D, LRW), f32),
                   jax.ShapeDtypeStruct((1, 2 * D), f32), jax.ShapeDtypeStruct((1, D), f32),
                   jax.ShapeDtypeStruct((2, LRW, KW), f32), jax.ShapeDtypeStruct((2, 1, KW), f32)),
        in_specs=[pl.BlockSpec((CTX, D), lambda i: (0, 0)), vec, vec, vec] + _CTX_W_SPECS + [w2s, w2s, gbs, gbs, sts, sts],
        out_specs=(full(D, KW), full(D, VW), full(D, LRW), full(1, 2 * D), full(1, D), full(2, LRW, KW), full(2, 1, KW)),
        compiler_params=_cp(("arbitrary",)),
    )(ctx, g, scale, shift, w_pad, w_pad, w_pad, w2f, w2b, gbf, gbb, dsf, dsb)


def _layernorm(va, g, b):
    mu = jnp.mean(va, axis=-1, keepdims=True)
    xc = va - mu
    rstd = lax.rsqrt(jnp.mean(xc * xc, axis=-1, keepdims=True) + EPS)
    vhat = xc * rstd
    return vhat, rstd, vhat * g + b


MIX_PIECES = 8


def _mix_fwd(p, ln_g, ln_b, ws, bs_t):
    l = p.shape[0]
    half = AW // 2
    rp = l // MIX_PIECES
    cpp = rp // ACH

    def body(p_ref, g_ref, b_ref, ws_ref, bs_ref, sv_hbm, va_buf, col_buf, sv_buf, in_sems, out_sems):
        piece = lambda i: pl.ds(pl.multiple_of(i * rp, rp), rp)
        load = lambda i: pltpu.make_async_copy(p_ref.at[piece(i), pl.ds(PVA, AW)], va_buf.at[piece(i)], in_sems.at[i])
        store = lambda i: pltpu.make_async_copy(sv_buf.at[piece(i)], sv_hbm.at[piece(i)], out_sems.at[i])
        for i in range(MIX_PIECES):
            load(i).start()

        def rows_piece(i, carry):
            load(i).wait()
            for j in range(cpp):
                rows = pl.ds(pl.multiple_of(i * rp + j * ACH, ACH), ACH)
                _, _, vn = _layernorm(va_buf[rows, :].astype(f32), g_ref[...], b_ref[...])
                for gi in range(2):
                    sl = slice(gi * ACH, (gi + 1) * ACH)
                    sv_buf[rows, sl] = (_dot(ws_ref[gi], vn[:, sl]) + bs_ref[:, gi:gi + 1]).astype(bf16)
                col_buf[0, rows, :] = vn[:, half:half + ACH]
                col_buf[1, rows, :] = vn[:, half + ACH:]
            return carry

        lax.fori_loop(0, MIX_PIECES, rows_piece, 0)

        def cols_step(cidx, carry):
            rows = pl.ds(cidx, ACH, stride=GW)
            for gi in range(2, 4):
                col_buf[gi - 2, rows, :] = _dot(ws_ref[gi], col_buf[gi - 2, rows, :]) + bs_ref[:, gi:gi + 1]
            return carry

        lax.fori_loop(0, GW, cols_step, 0, unroll=8)

        def out_piece(i, carry):
            for j in range(cpp):
                rows = pl.ds(pl.multiple_of(i * rp + j * ACH, ACH), ACH)
                sv_buf[rows, half:half + ACH] = col_buf[0, rows, :].astype(bf16)
                sv_buf[rows, half + ACH:] = col_buf[1, rows, :].astype(bf16)
            store(i).start()
            return carry

        lax.fori_loop(0, MIX_PIECES, out_piece, 0)
        for i in range(MIX_PIECES):
            store(i).wait()

    vm = pl.BlockSpec(memory_space=pltpu.VMEM)
    hbm = pl.BlockSpec(memory_space=pl.ANY)
    return pl.pallas_call(
        body, name="mix_fwd", out_shape=jax.ShapeDtypeStruct((l, AW), bf16),
        in_specs=[hbm, vm, vm, vm, vm], out_specs=hbm,
        scratch_shapes=[pltpu.VMEM((l, AW), bf16), pltpu.VMEM((2, l, ACH), f32), pltpu.VMEM((l, AW), bf16),
                        pltpu.SemaphoreType.DMA((MIX_PIECES,)), pltpu.SemaphoreType.DMA((MIX_PIECES,))],
        compiler_params=_cp(),
    )(p, ln_g, ln_b, ws, bs_t)


def _mix_bwd(p, dsv, ln_g, ln_b, ws_t):
    l = p.shape[0]
    half = AW // 2
    rp = l // MIX_PIECES
    cpp = rp // ACH

    def body(p_ref, dsv_hbm, g_ref, b_ref, wst_ref, dva_hbm, dws_ref, dbs_ref, dg_ref, db_ref,
             va_buf, dsv_buf, vn_col, ds_col, dva_buf, va_sems, ds_sems, out_sems):
        piece = lambda i: pl.ds(pl.multiple_of(i * rp, rp), rp)
        load_va = lambda i: pltpu.make_async_copy(p_ref.at[piece(i), pl.ds(PVA, AW)], va_buf.at[piece(i)], va_sems.at[i])
        load_ds = lambda i: pltpu.make_async_copy(dsv_hbm.at[piece(i)], dsv_buf.at[piece(i)], ds_sems.at[i])
        store = lambda i: pltpu.make_async_copy(dva_buf.at[piece(i)], dva_hbm.at[piece(i)], out_sems.at[i])
        for i in range(MIX_PIECES):
            load_va(i).start()
            load_ds(i).start()
        dws_ref[...] = jnp.zeros_like(dws_ref)
        dbs_ref[...] = jnp.zeros_like(dbs_ref)
        dg_ref[...] = jnp.zeros_like(dg_ref)
        db_ref[...] = jnp.zeros_like(db_ref)

        def rows_piece(i, carry):
            load_va(i).wait()
            load_ds(i).wait()
            for j in range(cpp):
                rows = pl.ds(pl.multiple_of(i * rp + j * ACH, ACH), ACH)
                _, _, vn = _layernorm(va_buf[rows, :].astype(f32), g_ref[...], b_ref[...])
                ds = dsv_buf[rows, :].astype(f32)
                for gi in range(2):
                    sl = slice(gi * ACH, (gi + 1) * ACH)
                    dws_ref[gi] += _nt(ds[:, sl], vn[:, sl])
                    dbs_ref[gi] += ds[:, sl]
                for gi in range(2):
                    sl = slice(half + gi * ACH, half + (gi + 1) * ACH)
                    vn_col[gi, rows, :] = vn[:, sl]
                    ds_col[gi, rows, :] = ds[:, sl]
            return carry

        lax.fori_loop(0, MIX_PIECES, rows_piece, 0)

        def cols_step(cidx, carry):
            rows = pl.ds(cidx, ACH, stride=GW)
            for gi in range(2, 4):
                ds = ds_col[gi - 2, rows, :]
                dws_ref[gi] += _nt(ds, vn_col[gi - 2, rows, :])
                dbs_ref[gi] += ds
                ds_col[gi - 2, rows, :] = _dot(wst_ref[gi], ds)
            return carry

        lax.fori_loop(0, GW, cols_step, 0, unroll=8)

        def out_piece(i, carry):
            for j in range(cpp):
                rows = pl.ds(pl.multiple_of(i * rp + j * ACH, ACH), ACH)
                vhat, rstd, _ = _layernorm(va_buf[rows, :].astype(f32), g_ref[...], b_ref[...])
                ds = dsv_buf[rows, :].astype(f32)
                dvn = jnp.concatenate([_dot(wst_ref[0], ds[:, 0:ACH]), _dot(wst_ref[1], ds[:, ACH:half]),
                                       ds_col[0, rows, :], ds_col[1, rows, :]], axis=1)
                dg_ref[...] += jnp.sum(dvn * vhat, axis=0, keepdims=True)
                db_ref[...] += jnp.sum(dvn, axis=0, keepdims=True)
                dvh = dvn * g_ref[...]
                dva = rstd * (dvh - jnp.mean(dvh, axis=-1, keepdims=True) - vhat * jnp.mean(dvh * vhat, axis=-1, keepdims=True))
                dva_buf[rows, :] = dva.astype(bf16)
            store(i).start()
            return carry

        lax.fori_loop(0, MIX_PIECES, out_piece, 0)
        for i in range(MIX_PIECES):
            store(i).wait()

    vm = pl.BlockSpec(memory_space=pltpu.VMEM)
    hbm = pl.BlockSpec(memory_space=pl.ANY)
    dma = lambda: pltpu.SemaphoreType.DMA((MIX_PIECES,))
    return pl.pallas_call(
        body, name="mix_bwd",
        out_shape=(jax.ShapeDtypeStruct((l, AW), bf16), jax.ShapeDtypeStruct((4, ACH, ACH), f32), jax.ShapeDtypeStruct((4, ACH, ACH), f32),
                   jax.ShapeDtypeStruct((1, AW), f32), jax.ShapeDtypeStruct((1, AW), f32)),
        in_specs=[hbm, hbm, vm, vm, vm], out_specs=(hbm, vm, vm, vm, vm),
        scratch_shapes=[pltpu.VMEM((l, AW), bf16), pltpu.VMEM((l, AW), bf16), pltpu.VMEM((2, l, ACH), f32), pltpu.VMEM((2, l, ACH), f32),
                        pltpu.VMEM((l, AW), bf16), dma(), dma(), dma()],
        compiler_params=_cp(),
    )(p, dsv, ln_g, ln_b, ws_t)


def _mid(x, tgt, p, o_f, o_b, sv, gate, gf, gb_norm, w_pa, w_pb, w_out, tl):
    l = x.shape[0]

    def body(x_ref, t_ref, zb_ref, ua_ref, za_ref, g1_ref, g2_ref, of_ref, ob_ref, sv_ref, gate_ref, gf_ref, gbn_ref,
             wpa_ref, wpb_ref, wout_ref,
             dx1_ref, dzbua_ref, dzag_ref, dsv_ref, do_ref, dwout_bf, dwpa_bf, dwpb_bf, dgf_ref, dgate_ref, dgbn_ref, loss_ref,
             dwout_ref, dwpa_ref, dwpb_ref):
        @pl.when(pl.program_id(0) == 0)
        def _():
            for r in (dwout_ref, dwpa_ref, dwpb_ref, dgf_ref, dgate_ref, dgbn_ref, loss_ref):
                r[...] = jnp.zeros_like(r)

        o = of_ref[...].astype(f32) + ob_ref[...].astype(f32)
        rr = jnp.concatenate(
            [jnp.broadcast_to(lax.rsqrt(jnp.mean(o[:, h * HV:(h + 1) * HV] ** 2, axis=-1, keepdims=True) + EPS), (tl, HV))
             for h in range(NH)], axis=1)
        ohat = o * rr
        on = ohat * gbn_ref[...]
        szb, dszb = _silu_and_grad(zb_ref[...].astype(f32))
        tb = on * szb
        u = ua_ref[...].astype(f32)
        svv = sv_ref[...].astype(f32)
        sza, dsza = _silu_and_grad(za_ref[...].astype(f32))
        ta = u * svv * sza
        ya = _dot(ta, wpa_ref[...])
        yb = _dot(tb, wpb_ref[...])
        g1 = _sigmoid(g1_ref[...].astype(f32))
        g2 = _sigmoid(g2_ref[...].astype(f32))
        m = g1 * ya + g2 * yb
        y2 = _dot(m, wout_ref[...])
        x1 = x_ref[...] + gate_ref[...] * y2
        r1 = lax.rsqrt(jnp.mean(x1 * x1, axis=-1, keepdims=True) + EPS)
        x1n = x1 * r1
        err = x1n * gf_ref[...] - t_ref[...]
        loss_ref[...] += jnp.sum(jnp.sum(err * err, axis=-1, keepdims=True), axis=0, keepdims=True) * (0.5 / D)
        dout = err * (1.0 / D)
        dgf_ref[...] += jnp.sum(dout * x1n, axis=0, keepdims=True)
        dx1n = dout * gf_ref[...]
        dx1 = r1 * (dx1n - x1n * jnp.mean(dx1n * x1n, axis=-1, keepdims=True))
        dx1_ref[...] = dx1
        dgate_ref[...] += jnp.sum(dx1 * y2, axis=0, keepdims=True)
        dy2 = dx1 * gate_ref[...]
        dwout_ref[...] += _tn(m, dy2)
        dm = _nt(dy2, wout_ref[...])
        dya = dm * g1
        dyb = dm * g2
        dzag_ref[:, AW:AW + D] = (dm * ya * g1 * (1.0 - g1)).astype(bf16)
        dzag_ref[:, AW + D:] = (dm * yb * g2 * (1.0 - g2)).astype(bf16)
        dwpa_ref[...] += _tn(ta, dya)
        dta = _nt(dya, wpa_ref[...])
        dzbua_ref[:, AW:] = (dta * svv * sza).astype(bf16)
        dsv_ref[...] = (dta * u * sza).astype(bf16)
        dzag_ref[:, 0:AW] = (dta * u * svv * dsza).astype(bf16)
        dwpb_ref[...] += _tn(tb, dyb)
        dtb = _nt(dyb, wpb_ref[...])
        don = dtb * szb
        dzbua_ref[:, 0:AW] = (dtb * on * dszb).astype(bf16)
        dgbn_ref[...] += jnp.sum(don * ohat, axis=0, keepdims=True)
        doh = don * gbn_ref[...]
        prod = doh * ohat
        mh = jnp.concatenate(
            [jnp.broadcast_to(jnp.mean(prod[:, h * HV:(h + 1) * HV], axis=-1, keepdims=True), (tl, HV)) for h in range(NH)], axis=1)
        do_ref[...] = (rr * (doh - ohat * mh)).astype(bf16)

        @pl.when(pl.program_id(0) == l // tl - 1)
        def _():
            for acc, out in ((dwout_ref, dwout_bf), (dwpa_ref, dwpa_bf), (dwpb_ref, dwpb_bf)):
                out[...] = acc[...].astype(bf16)

    row = lambda w, j: pl.BlockSpec((tl, w), lambda i, j=j: (i, j))
    full = lambda *s: pl.BlockSpec(s, lambda i: (0,) * len(s))
    return pl.pallas_call(
        body, name="mid", grid=(l // tl,),
        out_shape=(jax.ShapeDtypeStruct((l, D), f32), jax.ShapeDtypeStruct((l, 2 * AW), bf16), jax.ShapeDtypeStruct((l, AW + 2 * D), bf16),
                   jax.ShapeDtypeStruct((l, AW), bf16), jax.ShapeDtypeStruct((l, VW), bf16),
                   jax.ShapeDtypeStruct((D, D), bf16), jax.ShapeDtypeStruct((AW, D), bf16), jax.ShapeDtypeStruct((VW, D), bf16),
                   jax.ShapeDtypeStruct((1, D), f32), jax.ShapeDtypeStruct((1, D), f32), jax.ShapeDtypeStruct((1, VW), f32),
                   jax.ShapeDtypeStruct((1, 1), f32)),
        scratch_shapes=[pltpu.VMEM((D, D), f32), pltpu.VMEM((AW, D), f32), pltpu.VMEM((VW, D), f32)],
        in_specs=[row(D, 0), row(D, 0), row(AW, PZB // AW), row(AW, PUA // AW), row(AW, PZA // AW), row(D, PG1 // D), row(D, PG2 // D),
                  row(VW, 0), row(VW, 0), row(AW, 0), full(1, D), full(1, D), full(1, VW), full(AW, D), full(VW, D), full(D, D)],
        out_specs=(row(D, 0), row(2 * AW, 0), row(AW + 2 * D, 0), row(AW, 0), row(VW, 0),
                   full(D, D), full(AW, D), full(VW, D), full(1, D), full(1, D), full(1, VW), full(1, 1)),
        compiler_params=_cp(("arbitrary",)),
    )(x, tgt, p, p, p, p, p, o_f, o_b, sv, gate, gf, gb_norm, w_pa, w_pb, w_out)


def _in_bwd(x, dx1, dqkv, dzbua, dva, dzag, dlr, w_pad, g, scale, tl, comm):
    l = x.shape[0]
    c_in, c_out, c_sems = comm.specs()

    def body(*refs):
        cin = refs[14:14 + len(c_in)]
        outs = refs[14 + len(c_in):]
        comm.run(cin, outs[4:4 + comm.n], outs[4 + comm.n:], l // tl, lambda: compute(*refs[:14], *outs[:4]))

    def compute(x_ref, dx1_ref, a_ref, b_ref, c_ref, e_ref, lr_ref, wa_ref, wb_ref, wc_ref, we_ref, wl_ref, g_ref, sc_ref,
                gx_ref, dsh_ref, dsc_ref, dg_ref):
        @pl.when(pl.program_id(0) == 0)
        def _():
            for r in (dsh_ref, dsc_ref, dg_ref):
                r[...] = jnp.zeros_like(r)

        dh = (_nt(a_ref[...], wa_ref[...]) + _nt(b_ref[...], wb_ref[...]) + _nt(c_ref[...], wc_ref[...]) + _nt(e_ref[...], we_ref[...])
              + _nt(lr_ref[...], wl_ref[...]))
        xv = x_ref[...]
        r = lax.rsqrt(jnp.mean(xv * xv, axis=-1, keepdims=True) + EPS)
        xn = xv * r
        gxn = jnp.sum(dh * xn, axis=0, keepdims=True)
        dsh_ref[...] += jnp.sum(dh, axis=0, keepdims=True)
        dsc_ref[...] += gxn * g_ref[...]
        dg_ref[...] += gxn * (1.0 + sc_ref[...])
        dxn = dh * (g_ref[...] * (1.0 + sc_ref[...]))
        gx_ref[...] = dx1_ref[...] + r * (dxn - xn * jnp.mean(dxn * xn, axis=-1, keepdims=True))

    row = lambda w: pl.BlockSpec((tl, w), lambda i: (i, 0))
    wcol = lambda w, j: pl.BlockSpec((D, w), lambda i, j=j: (0, j))
    vec = pl.BlockSpec((1, D), lambda i: (0, 0))
    return pl.pallas_call(
        body, name="in_bwd", grid=(l // tl,),
        out_shape=(jax.ShapeDtypeStruct((l, D), f32),) + (jax.ShapeDtypeStruct((1, D), f32),) * 3 + tuple(comm.outs),
        in_specs=[row(D), row(D), row(2 * KW + VW), row(2 * AW), row(AW), row(AW + 2 * D), row(LRW),
                  wcol(2 * KW + VW, 0), wcol(2 * AW, PZB // (2 * AW)), wcol(AW, PVA // AW), wcol(AW + 2 * D, PZA // (AW + 2 * D)),
                  wcol(LRW, PLR // LRW), vec, vec] + c_in,
        out_specs=(row(D), vec, vec, vec) + tuple(c_out), scratch_shapes=c_sems,
        compiler_params=_cp(("arbitrary",)),
    )(x, dx1, dqkv, dzbua, dva, dzag, dlr, w_pad, w_pad, w_pad, w_pad, w_pad, g, scale, *comm.ins)


def _tn_matmul(a, bs, tl, name, comm=None, pack=None):
    l, m = a.shape
    k = len(bs)
    comm = comm or _Comm([], [], None)
    c_in, c_out, c_sems = comm.specs()
    acc_shapes = [(m, b.shape[1]) for b in bs]
    n_extra = len(pack.extra) if pack else 0
    n_res = len(pack.outs) if pack else k

    def body(a_ref, *refs):
        b_refs, refs = refs[:k], refs[k:]
        extra, refs = refs[:n_extra], refs[n_extra:]
        cin, refs = refs[:len(c_in)], refs[len(c_in):]
        res, refs = refs[:n_res], refs[n_res:]
        cout, refs = refs[:comm.n], refs[comm.n:]
        accs, sems = (refs[:k], refs[k:]) if pack else (res, refs)

        def compute():
            @pl.when(pl.program_id(0) == 0)
            def _():
                for o_ref in accs:
                    o_ref[...] = jnp.zeros_like(o_ref)

            av = a_ref[...]
            for b_ref, o_ref in zip(b_refs, accs):
                o_ref[...] += _tn(av, b_ref[...])
            if pack:
                pl.when(pl.program_id(0) == l // tl - 1)(lambda: pack.fn(accs, extra, res))

        comm.run(cin, cout, sems, l // tl, compute)

    full = lambda shape: pl.BlockSpec(shape, lambda i: (0,) * len(shape))
    res_shapes = list(pack.outs) if pack else [jax.ShapeDtypeStruct(sh, f32) for sh in acc_shapes]
    return pl.pallas_call(
        body, name=name, grid=(l // tl,), out_shape=tuple(res_shapes) + tuple(comm.outs),
        in_specs=[pl.BlockSpec((tl, m), lambda i: (i, 0))] + [pl.BlockSpec((tl, b.shape[1]), lambda i: (i, 0)) for b in bs]
        + [full(e.shape) for e in (pack.extra if pack else [])] + c_in,
        out_specs=tuple(full(r.shape) for r in res_shapes) + tuple(c_out),
        scratch_shapes=([pltpu.VMEM(sh, f32) for sh in acc_shapes] if pack else []) + c_sems, compiler_params=_cp(("arbitrary",)),
    )(a, *bs, *(pack.extra if pack else []), *comm.ins)


def _pad_gate(w2, gb):
    z = jnp.zeros((RANK, KW), f32)
    tail = jnp.zeros((LRW - 2 * RANK, KW), f32)
    w2f = jnp.concatenate([w2[0], z, tail], axis=0)
    w2b = jnp.concatenate([z, w2[1], tail], axis=0)
    return w2f, w2b, gb[0:1], gb[1:2]


EARLY_A_ROWS = 512


class _NoExchange:
    def __init__(self, w_pa, w_pb, w_out):
        self.weights = (w_pa, w_pb, w_out)

    def gather_proj(self):
        return _Comm([], [], None)

    def proj_weights(self, got):
        return self.weights

    def first(self, dw_out, dw_pa, dw_pb, small):
        return _Comm([], [], None)

    def early_a(self, blocks):
        return _Comm([], [], None)

    def early_b(self, blocks):
        return _Comm([], [], None)

    def late(self, blocks, dgt):
        return _Comm([], [], None)


class _Exchanges:
    def __init__(self, pa, pb, wo):
        self.shards = (pa, pb, wo)

    def gather_proj(self):
        def plan(i, o):
            srcs = [lambda j, r=r: r for r in i]
            dsts = [lambda j: o[0].at[:, _lanes(j)], lambda j: o[1].at[:, _lanes(j)], lambda j: o[2].at[j]]
            return srcs, dsts, None
        sds = jax.ShapeDtypeStruct
        return _Comm(self.shards, [sds((AW, D), bf16), sds((VW, D), bf16), sds((NDEV, 128, D), bf16)], plan)

    def proj_weights(self, got):
        return got[0], got[1], got[2].reshape(D, D)

    def first(self, dw_out, dw_pa, dw_pb, small):
        def plan(i, o):
            srcs = [lambda j: i[0].at[j], lambda j: i[1].at[:, _lanes(j)], lambda j: i[2].at[:, _lanes(j)], lambda j: i[3]]
            dsts = [lambda j, r=r: r.at[j] for r in o]
            return srcs, dsts, None
        sds = jax.ShapeDtypeStruct
        return _Comm([dw_out.reshape(NDEV, 128, D), dw_pa, dw_pb, small],
                     [sds((NDEV, 128, D), bf16), sds((NDEV, AW, 128), bf16), sds((NDEV, VW, 128), bf16),
                      sds((NDEV,) + small.shape, f32)], plan)

    def early_a(self, blocks):
        def plan(i, o):
            return [lambda j: i[0].at[j]], [lambda j: o[0].at[j]], [lambda j: j >= LATE_DESTS - 1]
        return _Comm([blocks], [jax.ShapeDtypeStruct(blocks.shape, bf16)], plan)

    def early_b(self, blocks):
        def plan(i, o):
            return [lambda j: i[0].at[j]], [lambda j: o[0].at[j]], [lambda j: j >= LATE_DESTS - 1]
        return _Comm([blocks], [jax.ShapeDtypeStruct(blocks.shape, bf16)], plan)

    def late(self, blocks, dgt):
        return _LateComm(blocks, dgt)


def _local_step(x, ctx, tgt, mod, modc, norm_g, w_pad, ln_g, ln_b, ws, bs, w2, gb, gb_norm, gf, xch):
    shift, scale, gate = mod[:, 0:D], mod[:, D:2 * D], mod[:, 2 * D:]
    shift_c, scale_c = modc[:, 0:D], modc[:, D:]
    w2f, w2b, gbf, gbb = _pad_gate(w2, gb)

    p, h, *got_proj = _in_proj(x, norm_g, scale, shift, w_pad, 512, xch.gather_proj())
    w_pa, w_pb, w_out = xch.proj_weights(got_proj)
    sc_f, sc_b = _ctx_fwd(ctx, norm_g, scale_c, shift_c, w_pad, w2f, w2b, gbf, gbb)
    o_f, st_f = _gla_fwd(p, w2f, gbf, sc_f, False, 512, "gla_fwd_f")
    o_b, st_b = _gla_fwd(p, w2b, gbb, sc_b, True, 512, "gla_fwd_b")
    sv = _mix_fwd(p, ln_g, ln_b, ws.astype(bf16), bs.T)
    (dx1, dzbua, dzag, dsv, do, dw_out, dw_pa, dw_pb, dgf, dgate, dgbn, loss) = _mid(
        x, tgt, p, o_f, o_b, sv, gate, gf, gb_norm, w_pa, w_pb, w_out, 256)
    dva, dws, dbs_acc, dln_g, dln_b = _mix_bwd(p, dsv, ln_g, ln_b, jnp.swapaxes(ws, 1, 2).astype(bf16))
    small = _rows128(dln_g, dln_b, dws, jnp.sum(dbs_acc, axis=-1), dgbn, dgf, jnp.broadcast_to(loss, (1, 128)))
    blocks_a, blocks_b, *got_first = _tn_matmul(h, [dzbua, dva, dzag], 1024, "dw_early", xch.first(dw_out, dw_pa, dw_pb, small),
                                               _pack_early())

    dqkv_f, dlr_f, dw2f, dgbf, dsc_f, *got_a = _gla_bwd(p, do, st_f, w2f, gbf, None, False, 512, "gla_bwd_f",
                                                        xch.early_a(blocks_a))
    dqkv, dlr, dw2b, dgbb, dsc_b, *got_b = _gla_bwd(p, do, st_b, w2b, gbb, (dqkv_f, dlr_f), True, 512, "gla_bwd_b",
                                                    xch.early_b(blocks_b))
    dwk_c, dwv_c, dwl_c, dmodc, dg_c, dw2c, dgbc = _ctx_bwd(ctx, norm_g, scale_c, shift_c, w_pad, w2f, w2b, gbf, gbb, dsc_f, dsc_b)
    (blocks_late,) = _tn_matmul(h, [dqkv, dlr], 1024, "dw_qkv_lr", pack=_pack_late(dwk_c, dwv_c, dwl_c))
    dw2 = jnp.stack([dw2f[0:RANK] + dw2c[0, 0:RANK], dw2b[RANK:2 * RANK] + dw2c[1, RANK:2 * RANK]])
    dgb = jnp.concatenate([dgbf + dgbc[0], dgbb + dgbc[1]], axis=0)
    dgt = jnp.concatenate([jnp.transpose(dw2.reshape(2, RANK, NDEV, 32), (2, 0, 1, 3)).reshape(NDEV, 2 * RANK * 32),
                           jnp.transpose(dgb.reshape(2, NDEV, 32), (1, 0, 2)).reshape(NDEV, 64),
                           jnp.zeros((NDEV, 64), f32)], axis=1).reshape(NDEV, 9, 128)
    gx, dshift, dscale, dg, *got_late = _in_bwd(x, dx1, dqkv, dzbua, dva, dzag, dlr, w_pad, norm_g, scale, 512,
                                                xch.late(blocks_late, dgt))
    return dict(loss=loss, gx=gx, dmod=jnp.concatenate([dshift, dscale, dgate], axis=1), dmodc=dmodc, dnorm_g=dg + dg_c,
                small=small, blocks_a=blocks_a, blocks_b=blocks_b, blocks_late=blocks_late, dw2=dw2, dgb=dgb,
                dw_pa=dw_pa, dw_pb=dw_pb, dw_out=dw_out, got_first=got_first, got_a=got_a, got_b=got_b, got_late=got_late)


def _rows128(*vs):
    out = []
    for t in vs:
        t = t.reshape(-1)
        pad = (-t.shape[0]) % 128
        out.append(jnp.pad(t, (0, pad)) if pad else t)
    return jnp.concatenate(out).reshape(-1, 128)


def kernel(x, c, ctx, c_ctx, w_mod, b_mod, norm_g, w_in, a_ln_g, a_ln_b, a_ws, a_bs, b_gate_w2, b_gate_b, b_norm_g, w_proj_a, w_proj_b, w_out, final_norm_g, loss_target, m_c_ctx, m_w_mod, m_b_mod, m_norm_g, m_w_in, m_a_ln_g, m_a_ln_b, m_a_ws, m_a_bs, m_b_gate_w2, m_b_gate_b, m_b_norm_g, m_w_proj_a, m_w_proj_b, m_w_out, m_final_norm_g, v_c_ctx, v_w_mod, v_b_mod, v_norm_g, v_w_in, v_a_ln_g, v_a_ln_b, v_a_ws, v_a_bs, v_b_gate_w2, v_b_gate_b, v_b_norm_g, v_w_proj_a, v_w_proj_b, v_w_out, v_final_norm_g):
    me = _me()
    ncol = w_mod.shape[2]

    gate_mine = _rows128(jnp.concatenate([b_gate_w2.reshape(-1), b_gate_b.reshape(-1)]))
    bm_mine = lax.dynamic_slice(b_mod, (0, me * ncol), (1, ncol))
    cs, mods, wg, gates = _gather_first(c, c_ctx.reshape(1, D), w_mod[0], bm_mine, w_in[0].astype(bf16), gate_mine)
    w_pad = _repack_w(wg)
    gflat = gates.reshape(NDEV, 9 * 128)
    w2 = jnp.transpose(gflat[:, 0:2 * RANK * 32].reshape(NDEV, 2, RANK, 32), (1, 2, 0, 3)).reshape(2, RANK, KW)
    gb = jnp.transpose(gflat[:, 2 * RANK * 32:2 * RANK * 32 + 64].reshape(NDEV, 2, 32), (1, 0, 2)).reshape(2, KW)

    mods = jnp.transpose(mods, (1, 0, 2)).reshape(16, 3 * D)
    mod = lax.dynamic_slice(mods, (me, 0), (1, 3 * D))
    modc = mods[8:9, 0:2 * D]

    xch = _Exchanges(w_proj_a[0].astype(bf16), w_proj_b[0].astype(bf16), w_out[0].astype(bf16))
    r = _local_step(x[0], ctx[0], loss_target[0], mod, modc, norm_g, w_pad, a_ln_g, a_ln_b, a_ws[0], a_bs[0], w2, gb,
                    b_norm_g, final_norm_g.reshape(1, D), xch)
    p_out, p_pa, p_pb, smalls_e = r["got_first"]
    (p_in_a,) = r["got_a"]
    (p_in_b,) = r["got_b"]
    _, _, p_in_late, p_gt = r["got_late"]

    n_e = (AW + AW + 4 * ACH * ACH + AW + VW + D) // 128
    row = lambda t: t.reshape(1, D)
    rep_e = _adam_params(smalls_e, [a_ln_g, a_ln_b, a_ws, a_bs, b_norm_g, row(final_norm_g)],
                         [m_a_ln_g, m_a_ln_b, m_a_ws, m_a_bs, m_b_norm_g, row(m_final_norm_g)],
                         [v_a_ln_g, v_a_ln_b, v_a_ws, v_a_bs, v_b_norm_g, row(v_final_norm_g)], "adam_rep_early")
    rep_e = [t[0:5] + (t[5].reshape(D),) for t in rep_e]
    losses = smalls_e[:, n_e, 0]
    loss = losses[0]
    for i in range(1, NDEV):
        loss = loss + losses[i]

    dbm = r["dmod"] + jnp.concatenate([r["dmodc"], jnp.zeros((1, D), f32)], axis=1)
    mod_grads = _mod_tail(r["dnorm_g"], dbm, r["dmod"], r["dmodc"], cs, w_mod)
    mod_res = _adam_plain(mod_grads, [w_mod, norm_g, b_mod, row(c_ctx)], [m_w_mod, m_norm_g, m_b_mod, row(m_c_ctx)],
                          [v_w_mod, v_norm_g, v_b_mod, row(v_c_ctx)], "adam_mod")
    wm, ng, bmod_r, cc = ([t[i] for t in mod_res] for i in range(4))

    a_in = _adam_w_in(p_in_a, p_in_b, p_in_late, w_in, m_w_in, v_w_in)
    blk = _adam_blocks([p_pa, p_pb, p_out], [w_proj_a, w_proj_b, w_out], [m_w_proj_a, m_w_proj_b, m_w_out],
                       [v_w_proj_a, v_w_proj_b, v_w_out], "adam_proj_out")
    a_pa, a_pb, a_out = ([t[i] for t in blk] for i in range(3))
    a_gt = _adam_gate(p_gt, b_gate_w2, b_gate_b, m_b_gate_w2, m_b_gate_b, v_b_gate_w2, v_b_gate_b)
    sh = [(a_in[k], a_pa[k], a_pb[k], a_out[k], a_gt[k], a_gt[4 + k]) for k in range(4)]

    outs = [loss, r["gx"][None]]
    for k in range(4):
        lg, lb, aws, abs_, bng, fng = rep_e[k]
        n_g, bmod = ng[k], bmod_r[k]
        s_in, s_pa, s_pb, s_out, s_w2, s_gb = sh[k]
        outs += [cc[k].reshape(D), wm[k], bmod, n_g, s_in, lg, lb, aws, abs_, s_w2, s_gb, bng, s_pa, s_pb, s_out, fng]
    return tuple(outs)
```

```python
import jax
import jax.numpy as jnp
from jax import lax
from jax.experimental import pallas as pl
from jax.experimental.pallas import tpu as pltpu

f32, bf16 = jnp.float32, jnp.bfloat16

D = 1024
CTX = 256
EPS = 1e-6
AW = 512
ACH = 128
GW = 64
KW = 256
VW = 512
NH = 4
HK = 64
HV = 128
RANK = 16
TAU = 16.0
CH = 64
QSCALE = HK ** -0.5
INW = 5152
NDEV = 8

PQ, PK, PV, PZB, PUA, PVA, PZA, PG1, PG2, PLR, PW = 0, 256, 512, 1024, 1536, 2048, 2560, 3072, 4096, 5120, 5248
LRW = 128

ADAM_LR, ADAM_B1, ADAM_B2, ADAM_EPS, ADAM_WD, ADAM_STEP = 0.001, 0.9, 0.999, 1e-08, 0.01, 10

VMEM_LIMIT = 56 * 1024 * 1024
MESH = pl.DeviceIdType.MESH


def _cp(sem=None):
    return pltpu.CompilerParams(dimension_semantics=sem, vmem_limit_bytes=VMEM_LIMIT)


def _dot(a, b):
    return jnp.dot(a.astype(bf16), b.astype(bf16), preferred_element_type=f32)


def _nt(a, b):
    return lax.dot_general(a.astype(bf16), b.astype(bf16), (((1,), (1,)), ((), ())), preferred_element_type=f32)


def _tn(a, b):
    return lax.dot_general(a.astype(bf16), b.astype(bf16), (((0,), (0,)), ((), ())), preferred_element_type=f32)


def _dot_hi(a, b):
    return jnp.dot(a, b, preferred_element_type=f32, precision=lax.Precision.HIGHEST)


def _sigmoid(x):
    return 1.0 / (1.0 + jnp.exp(-x))


def _log_sigmoid(x):
    return jnp.minimum(x, 0.0) - jnp.log(1.0 + jnp.exp(-jnp.abs(x)))


def _silu_and_grad(z):
    s = _sigmoid(z)
    return z * s, s * (1.0 + z * (1.0 - s))


def _me():
    return 4 * lax.axis_index("x") + 2 * lax.axis_index("y") + lax.axis_index("c")


def _peer(k):
    x, y, c = lax.axis_index("x"), lax.axis_index("y"), lax.axis_index("c")
    px = 1 - x if k & 4 else x
    py = 1 - y if k & 2 else y
    pc = 1 - c if k & 1 else c
    return (px, py, pc), 4 * px + 2 * py + pc


def _fanout(srcs, dsts, send_sems, recv_sems, local_sems, owners=None):
    me = _me()
    n = len(srcs)
    owns = lambda a, j: True if owners is None or owners[a] is None else owners[a](j)

    def guarded(cond, fn):
        if cond is True:
            fn()
        else:
            pl.when(cond)(fn)

    def copies(with_recvs):
        local = [pltpu.make_async_copy(srcs[a](me), dsts[a](me), local_sems.at[a]) for a in range(n)]
        sends, recvs = [], []
        for k in range(1, NDEV):
            dev, idx = _peer(k)
            for a in range(n):
                s = (k - 1) * n + a
                sends.append((owns(a, idx), pltpu.make_async_remote_copy(
                    src_ref=srcs[a](idx), dst_ref=dsts[a](me), send_sem=send_sems.at[s], recv_sem=recv_sems.at[s],
                    device_id=dev, device_id_type=MESH)))
                if with_recvs:
                    recvs.append((owns(a, me), pltpu.make_async_remote_copy(
                        src_ref=srcs[a](idx), dst_ref=dsts[a](idx), send_sem=send_sems.at[s], recv_sem=recv_sems.at[s],
                        device_id=dev, device_id_type=MESH)))
        return local, sends, recvs

    def start():
        local, sends, _ = copies(False)
        for a, cp in enumerate(local):
            guarded(owns(a, me), cp.start)
        for cond, cp in sends:
            guarded(cond, cp.start)

    def finish():
        local, sends, recvs = copies(True)
        for cond, cp in recvs:
            guarded(cond, cp.wait_recv)
        for cond, cp in sends:
            guarded(cond, cp.wait_send)
        for a, cp in enumerate(local):
            guarded(owns(a, me), cp.wait)

    return start, finish


class _Comm:
    def __init__(self, ins, outs, plan):
        self.ins, self.outs, self.plan = list(ins), list(outs), plan
        self.n = len(self.outs)

    def specs(self):
        hbm = pl.BlockSpec(memory_space=pl.ANY)
        return [hbm] * len(self.ins), [hbm] * self.n, _fanout_sems(self.n) if self.n else []

    def run(self, in_refs, out_refs, sems, nsteps, compute):
        if not self.n:
            compute()
            return

        def hooks():
            srcs, dsts, owners = self.plan(in_refs, out_refs)
            return _fanout(srcs, dsts, sems[0], sems[1], sems[2], owners)

        pl.when(pl.program_id(0) == 0)(lambda: hooks()[0]())
        compute()
        pl.when(pl.program_id(0) == nsteps - 1)(lambda: hooks()[1]())


LATE_MID_STEP = 1


class _LateComm:
    def __init__(self, blocks, dgt):
        sds = jax.ShapeDtypeStruct
        self.ins = [blocks, dgt]
        self.outs = [sds((D, SHARD), bf16), sds((D, SHARD), bf16), sds((4, D, SHARD), bf16), sds(dgt.shape, f32)]
        self.n = len(self.outs)

    def specs(self):
        hbm = pl.BlockSpec(memory_space=pl.ANY)
        scratch = [pltpu.VMEM((3, D, SHARD), bf16), pltpu.SemaphoreType.DMA((2,)), pltpu.SemaphoreType.DMA((4,)),
                   pltpu.SemaphoreType.DMA((3,))] + _fanout_sems(1)
        return [hbm] * 2, [hbm] * self.n, scratch

    def run(self, in_refs, out_refs, scratch, nsteps, compute):
        late_ref, dgt_ref = in_refs
        sib_ref, pair_ref, parts_ref, gt_ref = out_refs
        vbuf, send_sems, recv_sems, local_sems, g_send, g_recv, g_local = scratch
        x, y, c = lax.axis_index("x"), lax.axis_index("y"), lax.axis_index("c")
        chip = 2 * x + y
        is_owner_chip = chip == 0
        step = pl.program_id(0)

        def to_sibling():
            return pltpu.make_async_remote_copy(src_ref=late_ref.at[1 - c], dst_ref=sib_ref, send_sem=send_sems.at[0],
                                                recv_sem=recv_sems.at[0], device_id=(x, y, 1 - c), device_id_type=MESH)

        def to_owner(k):
            return pltpu.make_async_remote_copy(src_ref=pair_ref, dst_ref=parts_ref.at[k], send_sem=send_sems.at[1],
                                                recv_sem=recv_sems.at[k], device_id=(0, 0, c), device_id_type=MESH)

        def own_copy():
            return pltpu.make_async_copy(pair_ref, parts_ref.at[0], local_sems.at[2])

        def gates():
            return _fanout([lambda j: dgt_ref.at[j]], [lambda j: gt_ref.at[j]], g_send, g_recv, g_local)

        @pl.when(step == 0)
        def _():
            to_sibling().start()
            gates()[0]()

        compute()

        @pl.when(step == LATE_MID_STEP)
        def _():
            mine = pltpu.make_async_copy(late_ref.at[c], vbuf.at[0], local_sems.at[0])
            mine.start()
            to_sibling().wait_recv()
            theirs = pltpu.make_async_copy(sib_ref, vbuf.at[1], local_sems.at[1])
            theirs.start()
            mine.wait()
            theirs.wait()
            vbuf[2] = (vbuf[0].astype(f32) + vbuf[1].astype(f32)).astype(bf16)
            pltpu.sync_copy(vbuf.at[2], pair_ref)
            pl.when(is_owner_chip)(lambda: own_copy().start())
            pl.when(jnp.logical_not(is_owner_chip))(lambda: to_owner(chip).start())

        @pl.when(step == nsteps - 1)
        def _():
            @pl.when(is_owner_chip)
            def _():
                for k in range(1, 4):
                    to_owner(k).wait_recv()
                own_copy().wait()

            pl.when(jnp.logical_not(is_owner_chip))(lambda: to_owner(chip).wait_send())
            to_sibling().wait_send()
            gates()[1]()


def _fanout_sems(n):
    return [pltpu.SemaphoreType.DMA(((NDEV - 1) * n,)), pltpu.SemaphoreType.DMA(((NDEV - 1) * n,)), pltpu.SemaphoreType.DMA((n,))]


def _lanes(j):
    return pl.ds(pl.multiple_of(j * 128, 128), 128)


def _gather_first(c_row, cctx_row, wm, bm, wi, gate):
    def body(c_ref, cctx_ref, wm_ref, bm_ref, wi_ref, g_ref, cs_ref, mods_ref, owi, og, call_ref, mine_ref,
             send_sems, recv_sems, local_sems, c_send, c_recv, c_local, m_send, m_recv, m_local):
        x, y, c = lax.axis_index("x"), lax.axis_index("y"), lax.axis_index("c")
        sibling = (x, y, 1 - c)
        chips = [(1 - x, y), (x, 1 - y), (1 - x, 1 - y)]
        index = lambda px, py, pc: 4 * px + 2 * py + pc
        arrays = ((wi_ref, owi), (g_ref, og))
        n = len(arrays)

        def copy(a, k, block, to, own=False):
            src, out = arrays[a]
            return pltpu.make_async_remote_copy(
                src_ref=src if own else out.at[index(*block)], dst_ref=out.at[index(*block)],
                send_sem=send_sems.at[k * n + a], recv_sem=recv_sems.at[k * n + a], device_id=to, device_id_type=MESH)

        c_start, c_finish = _fanout([lambda j: c_ref], [lambda j: call_ref.at[j]], c_send, c_recv, c_local)
        c_start()
        mine = [pltpu.make_async_copy(src, out.at[index(x, y, c)], local_sems.at[a]) for a, (src, out) in enumerate(arrays)]
        first = [copy(a, 0, (x, y, c), sibling, own=True) for a in range(n)]
        first += [copy(a, 1 + j, (x, y, c), (*chip, c), own=True) for j, chip in enumerate(chips) for a in range(n)]
        for cp in mine + first:
            cp.start()

        c_finish()
        cs = jnp.concatenate([call_ref[j] for j in range(NDEV)] + [cctx_ref[...], jnp.zeros((16 - NDEV - 1, D), f32)], axis=0)
        cs_ref[...] = cs
        s, _ = _silu_and_grad(cs)
        mine_ref[...] = _dot_hi(s, wm_ref[...]) + bm_ref[...]
        m_start, m_finish = _fanout([lambda j: mine_ref], [lambda j: mods_ref.at[j]], m_send, m_recv, m_local)
        m_start()

        passed = []
        for j, chip in enumerate(chips):
            for a in range(n):
                copy(a, 1 + j, (*chip, c), (x, y, c)).wait_recv()
            for a in range(n):
                cp = copy(a, 4 + j, (*chip, c), sibling)
                cp.start()
                passed.append(cp)
        for a in range(n):
            copy(a, 0, sibling, (x, y, c)).wait_recv()
        for j, chip in enumerate(chips):
            for a in range(n):
                copy(a, 4 + j, (*chip, 1 - c), (x, y, c)).wait_recv()
        for cp in first + passed:
            cp.wait_send()
        for cp in mine:
            cp.wait()
        m_finish()

    hbm = pl.BlockSpec(memory_space=pl.ANY)
    vm = pl.BlockSpec(memory_space=pltpu.VMEM)
    ncol = wm.shape[1]
    return pl.pallas_call(
        body, name="gather_first",
        out_shape=(jax.ShapeDtypeStruct((16, D), f32), jax.ShapeDtypeStruct((NDEV, 16, ncol), f32),
                   jax.ShapeDtypeStruct((NDEV,) + wi.shape, bf16), jax.ShapeDtypeStruct((NDEV,) + gate.shape, f32)),
        in_specs=[vm, vm, vm, vm, hbm, hbm], out_specs=(vm, vm, hbm, hbm),
        scratch_shapes=[pltpu.VMEM((NDEV, 1, D), f32), pltpu.VMEM((16, ncol), f32)] + _fanout_sems(2) + _fanout_sems(1) + _fanout_sems(1),
        compiler_params=_cp(),
    )(c_row, cctx_row, wm, bm, wi, gate)


SHARD = INW // NDEV
ROWS_RP = 128


def _overlap(lo, hi, a, b):
    s, e = max(lo, a), min(hi, b)
    return (s, e) if s < e else None


def _repack_w(wg):
    segs = ((0, 1024, PQ), (1024, 1024 + 2 * RANK, PLR), (1024 + 2 * RANK, INW, PZB))

    def body(g_ref, o_ref):
        for j in range(NDEV):
            lo, hi = j * SHARD, (j + 1) * SHARD
            for a, b, pad0 in segs:
                ov = _overlap(lo, hi, a, b)
                if ov:
                    s, e = ov
                    o_ref[:, pad0 + s - a:pad0 + e - a] = g_ref[j, :, s - lo:e - lo]
        o_ref[:, PLR + 2 * RANK:PW] = jnp.zeros((ROWS_RP, PW - PLR - 2 * RANK), bf16)

    return pl.pallas_call(
        body, name="repack_w", grid=(D // ROWS_RP,), out_shape=jax.ShapeDtypeStruct((D, PW), bf16),
        in_specs=[pl.BlockSpec((NDEV, ROWS_RP, SHARD), lambda i: (0, i, 0))],
        out_specs=pl.BlockSpec((ROWS_RP, PW), lambda i: (i, 0)), compiler_params=_cp(("arbitrary",)),
    )(wg)


LATE_END = 1024 + 2 * RANK
LATE_DESTS = 2


def _pack_blocks(o_ref, srcs, dests, dtype):
    for n, j in enumerate(dests):
        lo, hi = j * SHARD, (j + 1) * SHARD
        done = lo
        for a, b, src in srcs:
            ov = _overlap(lo, hi, a, b)
            if ov:
                s, e = ov
                if s > done:
                    o_ref[n, :, done - lo:s - lo] = jnp.zeros((ROWS_RP, s - done), dtype)
                o_ref[n, :, s - lo:e - lo] = src[:, s - a:e - a].astype(dtype)
                done = e
        if done < hi:
            o_ref[n, :, done - lo:hi - lo] = jnp.zeros((ROWS_RP, hi - done), dtype)


class _Pack:
    def __init__(self, extra, outs, fn):
        self.extra, self.outs, self.fn = list(extra), list(outs), fn


def _pack_early():
    def fn(accs, extra, outs):
        zbua, va, zag = accs
        for r0 in range(0, D, ROWS_RP):
            rows = pl.ds(r0, ROWS_RP)
            out, o0 = (outs[0], r0) if r0 < EARLY_A_ROWS else (outs[1], r0 - EARLY_A_ROWS)
            _pack_blocks(out.at[:, pl.ds(o0, ROWS_RP)], ((LATE_END, 2080, zbua.at[rows]), (2080, 2592, va.at[rows]), (2592, INW, zag.at[rows])),
                         range(NDEV), bf16)

    sds = jax.ShapeDtypeStruct
    return _Pack([], [sds((NDEV, EARLY_A_ROWS, SHARD), bf16), sds((NDEV, D - EARLY_A_ROWS, SHARD), bf16)], fn)


def _pack_late(dwk_c, dwv_c, dwl_c):
    def fn(accs, extra, outs):
        qkv_ref, lr_ref = accs
        kc_ref, vc_ref, lc_ref = extra
        for r0 in range(0, D, ROWS_RP):
            rows = pl.ds(r0, ROWS_RP)
            qkv = qkv_ref[rows, :] + jnp.concatenate([jnp.zeros((ROWS_RP, KW), f32), kc_ref[rows, :], vc_ref[rows, :]], axis=1)
            lr = lr_ref[rows, :] + lc_ref[rows, :]
            _pack_blocks(outs[0].at[:, rows], ((0, 1024, qkv), (1024, LATE_END, lr)), range(LATE_DESTS), bf16)

    return _Pack([dwk_c, dwv_c, dwl_c], [jax.ShapeDtypeStruct((LATE_DESTS, D, SHARD), bf16)], fn)


def _mod_tail(dnorm, dbm, dmod, dmodc, cs, wm):
    ncol = wm.shape[2]

    def body(dn_ref, dbm_ref, dmod_ref, dmodc_ref, cs_ref, wm_ref, g_wm, g_ng, g_bm, g_cc, *rest):
        a_dn, a_dbm, a_dmod, a_dmodc, dm_rows, gc_ref, a_gc = rest[0:7]
        s1, r1, l1, s2, r2, l2 = rest[7:13]
        start, finish = _fanout([lambda j: dn_ref, lambda j: dbm_ref, lambda j: dmod_ref, lambda j: dmodc_ref],
                                [lambda j, r=r: r.at[j] for r in (a_dn, a_dbm, a_dmod, a_dmodc)], s1, r1, l1)
        start()
        finish()

        def total(ref):
            t = ref[0]
            for j in range(1, NDEV):
                t = t + ref[j]
            return t

        g_ng[...] = total(a_dn)
        g_bm[...] = total(a_dbm)
        dmodc_tot = jnp.concatenate([total(a_dmodc), jnp.zeros((1, D), f32)], axis=1)
        dm_rows[...] = jnp.concatenate([a_dmod[j] for j in range(NDEV)] + [dmodc_tot, jnp.zeros((16 - NDEV - 1, 3 * D), f32)], axis=0)
        dm = dm_rows[:, pl.ds(pl.multiple_of(_me() * ncol, 128), ncol)]
        s, ds = _silu_and_grad(cs_ref[...])
        part = lax.dot_general(dm[8:9, :], wm_ref[0], (((1,), (1,)), ((), ())), preferred_element_type=f32,
                               precision=lax.Precision.HIGHEST)
        gc_ref[...] = part * ds[8:9, :]
        start2, finish2 = _fanout([lambda j: gc_ref], [lambda j: a_gc.at[j]], s2, r2, l2)
        start2()
        g = lax.dot_general(s, dm, (((0,), (0,)), ((), ())), preferred_element_type=f32, precision=lax.Precision.HIGHEST)
        g_wm[...] = g[None]
        finish2()
        g_cc[...] = total(a_gc)

    sds = jax.ShapeDtypeStruct
    row = lambda n: pltpu.VMEM((NDEV, 1, n), f32)
    return pl.pallas_call(
        body, name="mod_tail", out_shape=(sds(wm.shape, f32), sds((1, D), f32), sds((1, 3 * D), f32), sds((1, D), f32)),
        scratch_shapes=[row(D), row(3 * D), row(3 * D), row(2 * D), pltpu.VMEM((16, 3 * D), f32), pltpu.VMEM((1, D), f32), row(D)]
        + _fanout_sems(4) + _fanout_sems(1),
        compiler_params=_cp(),
    )(dnorm, dbm, dmod, dmodc, cs, wm)


def _adam_update(g, w_ref, m_ref, v_ref, go_ref, d_ref, mo_ref, vo_ref):
    c1 = 1.0 / (1.0 - ADAM_B1 ** ADAM_STEP)
    c2 = 1.0 / (1.0 - ADAM_B2 ** ADAM_STEP)
    mn = ADAM_B1 * m_ref[...] + (1.0 - ADAM_B1) * g
    vn = ADAM_B2 * v_ref[...] + (1.0 - ADAM_B2) * (g * g)
    go_ref[...] = g
    mo_ref[...] = mn
    vo_ref[...] = vn
    d_ref[...] = -ADAM_LR * ((mn * c1) / (jnp.sqrt(vn * c2) + ADAM_EPS) + ADAM_WD * w_ref[...])


def _adam_w_in(parts_a, parts_b, parts_late, w, m, v):
    na = EARLY_A_ROWS // ROWS_RP
    whole = SHARD // 128 * 128

    def body(a_ref, b_ref, l_ref, w_ref, m_ref, v_ref, go_ref, d_ref, mo_ref, vo_ref):
        me = _me()
        first = pl.program_id(0) < na
        early = jnp.where(first, a_ref[0], b_ref[0]).astype(f32)
        for i in range(1, NDEV):
            early = early + jnp.where(first, a_ref[i], b_ref[i]).astype(f32)
        late = l_ref[0].astype(f32)
        for i in range(1, 4):
            late = late + l_ref[i].astype(f32)
        g = jnp.where(me >= LATE_DESTS - 1, early, 0.0) + jnp.where(me < LATE_DESTS, late, 0.0)
        gt = jnp.concatenate([g[:, c:c + 128].T for c in range(0, whole, 128)] + [g[:, SHARD - 128:].T[128 - (SHARD - whole):]],
                             axis=0)
        _adam_update(gt, w_ref, m_ref, v_ref, go_ref, d_ref, mo_ref, vo_ref)

    blk = pl.BlockSpec((SHARD, None, ROWS_RP), lambda i: (0, 0, i))
    res = pl.pallas_call(
        body, name="adam_w_in", grid=(D // ROWS_RP,), out_shape=tuple(jax.ShapeDtypeStruct((SHARD, 1, D), f32) for _ in range(4)),
        in_specs=[pl.BlockSpec((NDEV, ROWS_RP, SHARD), lambda i: (0, jnp.minimum(i, na - 1), 0)),
                  pl.BlockSpec((NDEV, ROWS_RP, SHARD), lambda i: (0, jnp.maximum(i - na, 0), 0)),
                  pl.BlockSpec((4, ROWS_RP, SHARD), lambda i: (0, i, 0)), blk, blk, blk],
        out_specs=(blk, blk, blk, blk), compiler_params=_cp(("arbitrary",)),
    )(parts_a, parts_b, parts_late, *(jnp.transpose(t, (2, 0, 1)) for t in (w, m, v)))
    return tuple(jnp.transpose(t, (1, 2, 0)) for t in res)


def _adam_params(parts, ws, ms, vs, name):
    p = parts.shape[0]
    k = len(ws)
    nrows = [w.size // 128 for w in ws]
    starts = [sum(nrows[:i]) for i in range(k)]

    def shaped(g, shape):
        if len(shape) == 2:
            return jnp.concatenate([g[r:r + 1] for r in range(g.shape[0])], axis=1)
        return g.reshape(shape)

    def body(*refs):
        g_ref = refs[0]
        w_refs, m_refs, v_refs = refs[1:1 + k], refs[1 + k:1 + 2 * k], refs[1 + 2 * k:1 + 3 * k]
        outs = refs[1 + 3 * k:]
        for i in range(k):
            rows = slice(starts[i], starts[i] + nrows[i])
            g = g_ref[0, rows, :]
            for j in range(1, p):
                g = g + g_ref[j, rows, :]
            _adam_update(shaped(g, ws[i].shape), w_refs[i], m_refs[i], v_refs[i], outs[i], outs[k + i], outs[2 * k + i], outs[3 * k + i])

    vm = pl.BlockSpec(memory_space=pltpu.VMEM)
    res = pl.pallas_call(
        body, name=name, out_shape=tuple(jax.ShapeDtypeStruct(w.shape, f32) for _ in range(4) for w in ws),
        in_specs=[vm] * (1 + 3 * k), out_specs=(vm,) * (4 * k), compiler_params=_cp(),
    )(parts, *ws, *ms, *vs)
    return [res[i * k:(i + 1) * k] for i in range(4)]


def _adam_blocks(parts, ws, ms, vs, name):
    k = len(ws)

    def body(*refs):
        g_refs, w_refs, m_refs, v_refs = refs[0:k], refs[k:2 * k], refs[2 * k:3 * k], refs[3 * k:4 * k]
        outs = refs[4 * k:]
        for i in range(k):
            g = g_refs[i][0].astype(f32)
            for j in range(1, parts[i].shape[0]):
                g = g + g_refs[i][j].astype(f32)
            _adam_update(g[None], w_refs[i], m_refs[i], v_refs[i], outs[i], outs[k + i], outs[2 * k + i], outs[3 * k + i])

    vm = pl.BlockSpec(memory_space=pltpu.VMEM)
    res = pl.pallas_call(
        body, name=name, out_shape=tuple(jax.ShapeDtypeStruct(w.shape, f32) for _ in range(4) for w in ws),
        in_specs=[vm] * (4 * k), out_specs=(vm,) * (4 * k), compiler_params=_cp(),
    )(*parts, *ws, *ms, *vs)
    return [res[i * k:(i + 1) * k] for i in range(4)]


def _adam_plain(gs, ws, ms, vs, name):
    k = len(ws)

    def body(*refs):
        g_refs, w_refs, m_refs, v_refs = refs[0:k], refs[k:2 * k], refs[2 * k:3 * k], refs[3 * k:4 * k]
        outs = refs[4 * k:]
        for i in range(k):
            _adam_update(g_refs[i][...], w_refs[i], m_refs[i], v_refs[i], outs[i], outs[k + i], outs[2 * k + i], outs[3 * k + i])

    vm = pl.BlockSpec(memory_space=pltpu.VMEM)
    res = pl.pallas_call(
        body, name=name, out_shape=tuple(jax.ShapeDtypeStruct(w.shape, f32) for _ in range(4) for w in ws),
        in_specs=[vm] * (4 * k), out_specs=(vm,) * (4 * k), compiler_params=_cp(),
    )(*gs, *ws, *ms, *vs)
    return [res[i * k:(i + 1) * k] for i in range(4)]


def _adam_gate(parts, w2, gb, m_w2, m_gb, v_w2, v_gb):
    per = 128 // 32

    def body(p_ref, w2_ref, gb_ref, mw_ref, mb_ref, vw_ref, vb_ref, *outs):
        g = p_ref[0]
        for j in range(1, parts.shape[0]):
            g = g + p_ref[j]
        for d in range(2):
            gd = jnp.concatenate([g[(d * RANK + r) // per:(d * RANK + r) // per + 1, (r % per) * 32:(r % per + 1) * 32]
                                  for r in range(RANK)], axis=0)
            _adam_update(gd, *(t.at[0, d] for t in (w2_ref, mw_ref, vw_ref) + outs[0:4]))
        last = 2 * RANK // per
        gbv = jnp.concatenate([g[last:last + 1, d * 32:(d + 1) * 32] for d in range(2)], axis=0)
        _adam_update(gbv, *(t.at[0] for t in (gb_ref, mb_ref, vb_ref) + outs[4:8]))

    vm = pl.BlockSpec(memory_space=pltpu.VMEM)
    return pl.pallas_call(
        body, name="adam_gate", out_shape=tuple(jax.ShapeDtypeStruct(w.shape, f32) for w in (w2, gb) for _ in range(4)),
        in_specs=[vm] * 7, out_specs=(vm,) * 8, compiler_params=_cp(),
    )(parts, w2, gb, m_w2, m_gb, v_w2, v_gb)


def _in_proj(x, g, scale, shift, w_pad, tl, comm):
    l = x.shape[0]
    c_in, c_out, c_sems = comm.specs()

    def body(*refs):
        x_ref, g_ref, sc_ref, sh_ref, w_ref = refs[:5]
        cin = refs[5:5 + len(c_in)]
        p_ref, h_ref = refs[5 + len(c_in):7 + len(c_in)]
        cout = refs[7 + len(c_in):7 + len(c_in) + comm.n]
        sems = refs[7 + len(c_in) + comm.n:]

        def compute():
            xv = x_ref[...]
            r = lax.rsqrt(jnp.mean(xv * xv, axis=-1, keepdims=True) + EPS)
            h = (xv * r) * (g_ref[...] * (1.0 + sc_ref[...])) + sh_ref[...]
            hb = h.astype(bf16)
            h_ref[...] = hb
            p_ref[...] = jnp.dot(hb, w_ref[...], preferred_element_type=f32).astype(bf16)

        comm.run(cin, cout, sems, l // tl, compute)

    vec = pl.BlockSpec((1, D), lambda i: (0, 0))
    return pl.pallas_call(
        body, name="in_proj", grid=(l // tl,),
        out_shape=(jax.ShapeDtypeStruct((l, PW), bf16), jax.ShapeDtypeStruct((l, D), bf16)) + tuple(comm.outs),
        in_specs=[pl.BlockSpec((tl, D), lambda i: (i, 0)), vec, vec, vec, pl.BlockSpec((D, PW), lambda i: (0, 0))] + c_in,
        out_specs=(pl.BlockSpec((tl, PW), lambda i: (i, 0)), pl.BlockSpec((tl, D), lambda i: (i, 0))) + tuple(c_out),
        scratch_shapes=c_sems, compiler_params=_cp(("arbitrary",)),
    )(x, g, scale, shift, w_pad, *comm.ins)


def _tri(rev):
    i = lax.broadcasted_iota(jnp.int32, (CH, CH), 0)
    j = lax.broadcasted_iota(jnp.int32, (CH, CH), 1)
    return jnp.where((j >= i) if rev else (j <= i), 1.0, 0.0).astype(f32)


def _head_masks():
    lane = lax.broadcasted_iota(jnp.int32, (1, KW), 1) // HK
    return [jnp.where(lane == h, 1.0, 0.0).astype(f32) for h in range(NH)]


def _block_diag():
    r = lax.broadcasted_iota(jnp.int32, (VW, KW), 0) // HV
    c = lax.broadcasted_iota(jnp.int32, (VW, KW), 1) // HK
    return jnp.where(r == c, 1.0, 0.0).astype(f32)


def _decay(lr, w2, gb, tri, rev):
    logits = _dot(lr, w2) + gb
    a = _log_sigmoid(logits) * (1.0 / TAU)
    c = _dot_hi(tri, a)
    cl = c[0:1, :] if rev else c[CH - 1:CH, :]
    return logits, c, cl


def _state_fwd(k, v, c, cl, st, bd):
    return st * jnp.exp(cl) + bd * _tn(v, k * jnp.exp(cl - c))


def _state_bwd(k, v, c, cl, st0, dst, trit):
    ecl = jnp.exp(cl)
    edec = jnp.exp(cl - c)
    kdec = k * edec
    dv = _nt(kdec, dst)
    dkdec = _dot(v, dst)
    dcl = jnp.sum(dst * st0, axis=0, keepdims=True) * ecl + jnp.sum(dkdec * kdec, axis=0, keepdims=True)
    da = _dot_hi(trit, -dkdec * kdec) + dcl
    return dkdec * edec, dv, da, dst * ecl


def _bdot(a, b):
    return lax.dot_general(a.astype(bf16), b.astype(bf16), (((2,), (1,)), ((0,), (0,))), preferred_element_type=f32)


def _bnt(a, b):
    return lax.dot_general(a.astype(bf16), b.astype(bf16), (((2,), (2,)), ((0,), (0,))), preferred_element_type=f32)


def _btn(a, b):
    return lax.dot_general(a.astype(bf16), b.astype(bf16), (((1,), (1,)), ((0,), (0,))), preferred_element_type=f32)


def _scan_chunks(x, rev):
    nc = x.shape[0]
    hi = x.astype(bf16)
    r1 = x - hi.astype(f32)
    mid = r1.astype(bf16)
    lo = (r1 - mid.astype(f32)).astype(bf16)
    terms = jnp.concatenate([hi, mid, lo], axis=1)
    tri3 = jnp.broadcast_to(jnp.concatenate([_tri(rev)] * 3, axis=1).astype(bf16)[None], (nc, CH, 3 * CH))
    return lax.dot_general(tri3, terms, (((2,), (1,)), ((0,), (0,))), preferred_element_type=f32)


class _Tile:
    pass


def _tile_prep(q_ref, k_ref, lr_ref, w2_ref, gb_ref, rev, nc):
    t = _Tile()
    tg = nc * CH
    t.logits = _dot(lr_ref[...], w2_ref[...]) + gb_ref[...]
    c = _scan_chunks((_log_sigmoid(t.logits) * (1.0 / TAU)).reshape(nc, CH, KW), rev)
    cl = c[:, 0:1, :] if rev else c[:, CH - 1:CH, :]
    k = k_ref[...].astype(f32).reshape(nc, CH, KW)
    t.ec, t.enc, t.edec, t.ecl = jnp.exp(c), jnp.exp(-c), jnp.exp(cl - c), jnp.exp(cl)
    t.qd = q_ref[...].astype(f32).reshape(nc, CH, KW) * t.ec * QSCALE
    t.kd = k * t.enc
    t.kdec = k * t.edec
    hm = _head_masks()
    t.tri4 = jnp.concatenate([_tri(rev)] * NH, axis=0)[None]
    t.qs = jnp.concatenate([t.qd * hm[h] for h in range(NH)], axis=1)
    t.pst = _bnt(t.qs, t.kd) * t.tri4
    return t


def _gla_fwd(p, w2p, gb, s0, rev, tg, name):
    l = p.shape[0]
    nb, nc = l // tg, tg // CH

    def body(q_ref, k_ref, v_ref, lr_ref, w2_ref, gb_ref, s0_ref, o_ref, st_ref, st):
        @pl.when(pl.program_id(0) == 0)
        def _():
            st[...] = s0_ref[...]

        t = _tile_prep(q_ref, k_ref, lr_ref, w2_ref, gb_ref, rev, nc)
        v = v_ref[...].reshape(nc, CH, VW)
        intra = jnp.concatenate([_bdot(t.pst[:, h * CH:(h + 1) * CH], v[:, :, h * HV:(h + 1) * HV]) for h in range(NH)], axis=2)
        bd = _block_diag()
        s = st[...]
        for n in (range(nc - 1, -1, -1) if rev else range(nc)):
            st_ref[n] = s.astype(bf16)
            s = s * t.ecl[n] + bd * _tn(v[n], t.kdec[n])
        st[...] = s
        o_ref[...] = (_bnt(t.qd, st_ref[...]) + intra).reshape(tg, VW).astype(bf16)

    blk = (lambda i: nb - 1 - i) if rev else (lambda i: i)
    return pl.pallas_call(
        body, name=name, grid=(nb,),
        out_shape=(jax.ShapeDtypeStruct((l, VW), bf16), jax.ShapeDtypeStruct((l // CH, VW, KW), bf16)),
        in_specs=[pl.BlockSpec((tg, KW), lambda i: (blk(i), PQ // KW)), pl.BlockSpec((tg, KW), lambda i: (blk(i), PK // KW)),
                  pl.BlockSpec((tg, VW), lambda i: (blk(i), PV // VW)), pl.BlockSpec((tg, LRW), lambda i: (blk(i), PLR // LRW)),
                  pl.BlockSpec((LRW, KW), lambda i: (0, 0)), pl.BlockSpec((1, KW), lambda i: (0, 0)),
                  pl.BlockSpec((VW, KW), lambda i: (0, 0))],
        out_specs=(pl.BlockSpec((tg, VW), lambda i: (blk(i), 0)), pl.BlockSpec((nc, VW, KW), lambda i: (blk(i), 0, 0))),
        scratch_shapes=[pltpu.VMEM((VW, KW), f32)],
        compiler_params=_cp(("arbitrary",)),
    )(p, p, p, p, w2p, gb, s0)


def _gla_bwd(p, do, states, w2p, gb, prev, rev, tg, name, comm):
    l = p.shape[0]
    nb, nc = l // tg, tg // CH
    out_dt = bf16
    c_in, c_out, c_sems = comm.specs()

    def body(*refs):
        q_ref, k_ref, v_ref, lr_ref, do_ref, st_ref, w2_ref, gb_ref = refs[:8]
        refs = refs[8:]
        if prev is not None:
            pq_ref, pl_ref = refs[:2]
            refs = refs[2:]
        cin, refs = refs[:len(c_in)], refs[len(c_in):]
        dqkv_ref, dlr_ref, dw2_ref, dgb_ref, ds0_ref = refs[:5]
        cout, dst, ds_buf, sems = refs[5:5 + comm.n], refs[5 + comm.n], refs[6 + comm.n], refs[7 + comm.n:]
        comm.run(cin, cout, sems, nb, lambda: compute(q_ref, k_ref, v_ref, lr_ref, do_ref, st_ref, w2_ref, gb_ref,
                                                      pq_ref if prev is not None else None, pl_ref if prev is not None else None,
                                                      dqkv_ref, dlr_ref, dw2_ref, dgb_ref, ds0_ref, dst, ds_buf))

    def compute(q_ref, k_ref, v_ref, lr_ref, do_ref, st_ref, w2_ref, gb_ref, pq_ref, pl_ref,
                dqkv_ref, dlr_ref, dw2_ref, dgb_ref, ds0_ref, dst, ds_buf):
        @pl.when(pl.program_id(0) == 0)
        def _():
            dst[...] = jnp.zeros_like(dst)
            dw2_ref[...] = jnp.zeros_like(dw2_ref)
            dgb_ref[...] = jnp.zeros_like(dgb_ref)

        t = _tile_prep(q_ref, k_ref, lr_ref, w2_ref, gb_ref, rev, nc)
        hm = _head_masks()
        v = v_ref[...].reshape(nc, CH, VW)
        do = do_ref[...].reshape(nc, CH, VW)
        heads = lambda a, h: a[:, :, h * HV:(h + 1) * HV]
        dpst = jnp.concatenate([_bnt(heads(do, h), heads(v, h)) for h in range(NH)], axis=1) * t.tri4
        dv = jnp.concatenate([_btn(t.pst[:, h * CH:(h + 1) * CH], heads(do, h)) for h in range(NH)], axis=2)
        dqd = _bdot(do, st_ref[...])
        for h in range(NH):
            dqd = dqd + hm[h] * _bdot(dpst[:, h * CH:(h + 1) * CH], t.kd)
        dkd = _btn(dpst, t.qs)
        bd = _block_diag()
        d = dst[...]
        for n in (range(nc) if rev else range(nc - 1, -1, -1)):
            ds_buf[n] = d
            d = d * t.ecl[n] + bd * _tn(do[n], t.qd[n])
        dst[...] = d
        ds0_ref[...] = d
        ds = ds_buf[...]
        dv = dv + _bnt(t.kdec, ds)
        dkdec = _bdot(v, ds)
        dcl = jnp.sum(ds * st_ref[...].astype(f32), axis=1, keepdims=True) * t.ecl + jnp.sum(dkdec * t.kdec, axis=1, keepdims=True)
        dc = dqd * t.qd - dkd * t.kd - dkdec * t.kdec
        da = _scan_chunks(dc, not rev) + dcl
        dq = dqd * t.ec * QSCALE
        dk = dkd * t.enc + dkdec * t.edec
        dlog = da.reshape(tg, KW) * _sigmoid(-t.logits) * (1.0 / TAU)
        dlr = _nt(dlog, w2_ref[...])
        dw2_ref[...] += _tn(lr_ref[...], dlog)
        dgb_ref[...] += jnp.sum(dlog, axis=0, keepdims=True)
        dqkv = jnp.concatenate([dq, dk, dv], axis=2).reshape(tg, 2 * KW + VW)
        if prev is not None:
            dqkv = dqkv + pq_ref[...]
            dlr = dlr + pl_ref[...]
        dqkv_ref[...] = dqkv.astype(out_dt)
        dlr_ref[...] = dlr.astype(out_dt)

    blk = (lambda i: i) if rev else (lambda i: nb - 1 - i)
    in_specs = [pl.BlockSpec((tg, KW), lambda i: (blk(i), PQ // KW)), pl.BlockSpec((tg, KW), lambda i: (blk(i), PK // KW)),
                pl.BlockSpec((tg, VW), lambda i: (blk(i), PV // VW)), pl.BlockSpec((tg, LRW), lambda i: (blk(i), PLR // LRW)),
                pl.BlockSpec((tg, VW), lambda i: (blk(i), 0)), pl.BlockSpec((nc, VW, KW), lambda i: (blk(i), 0, 0)),
                pl.BlockSpec((LRW, KW), lambda i: (0, 0)), pl.BlockSpec((1, KW), lambda i: (0, 0))]
    args = [p, p, p, p, do, states, w2p, gb]
    if prev is not None:
        in_specs += [pl.BlockSpec((tg, 2 * KW + VW), lambda i: (blk(i), 0)), pl.BlockSpec((tg, LRW), lambda i: (blk(i), 0))]
        args += list(prev)
    return pl.pallas_call(
        body, name=name, grid=(nb,),
        out_shape=(jax.ShapeDtypeStruct((l, 2 * KW + VW), out_dt), jax.ShapeDtypeStruct((l, LRW), out_dt),
                   jax.ShapeDtypeStruct((LRW, KW), f32), jax.ShapeDtypeStruct((1, KW), f32), jax.ShapeDtypeStruct((VW, KW), f32))
        + tuple(comm.outs),
        in_specs=in_specs + c_in,
        out_specs=(pl.BlockSpec((tg, 2 * KW + VW), lambda i: (blk(i), 0)), pl.BlockSpec((tg, LRW), lambda i: (blk(i), 0)),
                   pl.BlockSpec((LRW, KW), lambda i: (0, 0)), pl.BlockSpec((1, KW), lambda i: (0, 0)),
                   pl.BlockSpec((VW, KW), lambda i: (0, 0))) + tuple(c_out),
        scratch_shapes=[pltpu.VMEM((VW, KW), f32), pltpu.VMEM((nc, VW, KW), f32)] + c_sems,
        compiler_params=_cp(("arbitrary",)),
    )(*args, *comm.ins)


def _ctx_hidden(ctx_ref, g_ref, sc_ref, sh_ref):
    xv = ctx_ref[...]
    r = lax.rsqrt(jnp.mean(xv * xv, axis=-1, keepdims=True) + EPS)
    xn = xv * r
    return xn, xn * (g_ref[...] * (1.0 + sc_ref[...])) + sh_ref[...]


_CTX_W_SPECS = [pl.BlockSpec((D, KW), lambda i: (0, PK // KW)), pl.BlockSpec((D, VW), lambda i: (0, PV // VW)),
                pl.BlockSpec((D, LRW), lambda i: (0, PLR // LRW))]


def _ctx_fwd(ctx, g, scale, shift, w_pad, w2f, w2b, gbf, gbb):
    ncc = CTX // CH

    def body(ctx_ref, g_ref, sc_ref, sh_ref, wk_ref, wv_ref, wl_ref, w2f_ref, w2b_ref, gbf_ref, gbb_ref, sf_ref, sb_ref):
        _, hc = _ctx_hidden(ctx_ref, g_ref, sc_ref, sh_ref)
        k, v, lr = _dot(hc, wk_ref[...]), _dot(hc, wv_ref[...]), _dot(hc, wl_ref[...])
        bd = _block_diag()
        for rev, w2_ref, gb_ref, out in ((False, w2f_ref, gbf_ref, sf_ref), (True, w2b_ref, gbb_ref, sb_ref)):
            tri = _tri(rev)
            st = jnp.zeros((VW, KW), f32)
            for j in (range(ncc - 1, -1, -1) if rev else range(ncc)):
                rows = slice(j * CH, (j + 1) * CH)
                _, c, cl = _decay(lr[rows], w2_ref[...], gb_ref[...], tri, rev)
                st = _state_fwd(k[rows], v[rows], c, cl, st, bd)
            out[...] = st

    vec = pl.BlockSpec((1, D), lambda i: (0, 0))
    w2s = pl.BlockSpec((LRW, KW), lambda i: (0, 0))
    gbs = pl.BlockSpec((1, KW), lambda i: (0, 0))
    sts = pl.BlockSpec((VW, KW), lambda i: (0, 0))
    return pl.pallas_call(
        body, name="ctx_fwd", grid=(1,), out_shape=(jax.ShapeDtypeStruct((VW, KW), f32),) * 2,
        in_specs=[pl.BlockSpec((CTX, D), lambda i: (0, 0)), vec, vec, vec] + _CTX_W_SPECS + [w2s, w2s, gbs, gbs],
        out_specs=(sts, sts), compiler_params=_cp(("arbitrary",)),
    )(ctx, g, scale, shift, w_pad, w_pad, w_pad, w2f, w2b, gbf, gbb)


def _ctx_bwd(ctx, g, scale, shift, w_pad, w2f, w2b, gbf, gbb, dsf, dsb):
    ncc = CTX // CH

    def body(ctx_ref, g_ref, sc_ref, sh_ref, wk_ref, wv_ref, wl_ref, w2f_ref, w2b_ref, gbf_ref, gbb_ref, dsf_ref, dsb_ref,
             dwk_ref, dwv_ref, dwl_ref, dmod_ref, dg_ref, dw2_ref, dgb_ref):
        xn, hc = _ctx_hidden(ctx_ref, g_ref, sc_ref, sh_ref)
        k, v, lr = _dot(hc, wk_ref[...]), _dot(hc, wv_ref[...]), _dot(hc, wl_ref[...])
        bd = _block_diag()
        dk_rows, dv_rows, dl_rows = [None] * ncc, [None] * ncc, [None] * ncc
        for d, (rev, w2_ref, gb_ref, ds_ref) in enumerate(((False, w2f_ref, gbf_ref, dsf_ref), (True, w2b_ref, gbb_ref, dsb_ref))):
            tri = _tri(rev)
            order = list(range(ncc - 1, -1, -1) if rev else range(ncc))
            st, saved = jnp.zeros((VW, KW), f32), {}
            for j in order:
                rows = slice(j * CH, (j + 1) * CH)
                logits, c, cl = _decay(lr[rows], w2_ref[...], gb_ref[...], tri, rev)
                saved[j] = (logits, c, cl, st)
                st = _state_fwd(k[rows], v[rows], c, cl, st, bd)
            dst = ds_ref[...]
            dw2 = jnp.zeros((LRW, KW), f32)
            dgb = jnp.zeros((1, KW), f32)
            for j in reversed(order):
                rows = slice(j * CH, (j + 1) * CH)
                logits, c, cl, st0 = saved[j]
                dk, dv, da, dst = _state_bwd(k[rows], v[rows], c, cl, st0, dst, _tri(not rev))
                dlog = da * _sigmoid(-logits) * (1.0 / TAU)
                dl = _nt(dlog, w2_ref[...])
                dw2 = dw2 + _tn(lr[rows], dlog)
                dgb = dgb + jnp.sum(dlog, axis=0, keepdims=True)
                dk_rows[j] = dk if dk_rows[j] is None else dk_rows[j] + dk
                dv_rows[j] = dv if dv_rows[j] is None else dv_rows[j] + dv
                dl_rows[j] = dl if dl_rows[j] is None else dl_rows[j] + dl
            dw2_ref[d] = dw2
            dgb_ref[d] = dgb
        dk, dv, dl = (jnp.concatenate(t, axis=0) for t in (dk_rows, dv_rows, dl_rows))
        dwk_ref[...] = _tn(hc, dk)
        dwv_ref[...] = _tn(hc, dv)
        dwl_ref[...] = _tn(hc, dl)
        dh = _nt(dk, wk_ref[...]) + _nt(dv, wv_ref[...]) + _nt(dl, wl_ref[...])
        gx = dh * xn
        dmod_ref[:, 0:D] = jnp.sum(dh, axis=0, keepdims=True)
        dmod_ref[:, D:2 * D] = jnp.sum(gx, axis=0, keepdims=True) * g_ref[...]
        dg_ref[...] = jnp.sum(gx, axis=0, keepdims=True) * (1.0 + sc_ref[...])

    vec = pl.BlockSpec((1, D), lambda i: (0, 0))
    w2s = pl.BlockSpec((LRW, KW), lambda i: (0, 0))
    gbs = pl.BlockSpec((1, KW), lambda i: (0, 0))
    sts = pl.BlockSpec((VW, KW), lambda i: (0, 0))
    full = lambda *s: pl.BlockSpec(s, lambda i: (0,) * len(s))
    return pl.pallas_call(
        body, name="ctx_bwd", grid=(1,),
        out_shape=(jax.ShapeDtypeStruct((D, KW), f32), jax.ShapeDtypeStruct((D, VW), f32), jax.ShapeDtypeStruct((D, LRW), f32),
                   jax.ShapeDtypeStruct((1, 2 * D), f32), jax.ShapeDtypeStruct((1, D), f32),
                   jax.ShapeDtypeStruct((2, LRW, KW), f32), jax.ShapeDtypeStruct((2, 1, KW), f32)),
        in_specs=[pl.BlockSpec((CTX, D), lambda i: (0, 0)), vec, vec, vec] + _CTX_W_SPECS + [w2s, w2s, gbs, gbs, sts, sts],
        out_specs=(full(D, KW), full(D, VW), full(D, LRW), full(1, 2 * D), full(1, D), full(2, LRW, KW), full(2, 1, KW)),
        compiler_params=_cp(("arbitrary",)),
    )(ctx, g, scale, shift, w_pad, w_pad, w_pad, w2f, w2b, gbf, gbb, dsf, dsb)


def _layernorm(va, g, b):
    mu = jnp.mean(va, axis=-1, keepdims=True)
    xc = va - mu
    rstd = lax.rsqrt(jnp.mean(xc * xc, axis=-1, keepdims=True) + EPS)
    vhat = xc * rstd
    return vhat, rstd, vhat * g + b


MIX_PIECES = 8


def _mix_fwd(p, ln_g, ln_b, ws, bs_t):
    l = p.shape[0]
    half = AW // 2
    rp = l // MIX_PIECES
    cpp = rp // ACH

    def body(p_ref, g_ref, b_ref, ws_ref, bs_ref, sv_hbm, va_buf, col_buf, sv_buf, in_sems, out_sems):
        piece = lambda i: pl.ds(pl.multiple_of(i * rp, rp), rp)
        load = lambda i: pltpu.make_async_copy(p_ref.at[piece(i), pl.ds(PVA, AW)], va_buf.at[piece(i)], in_sems.at[i])
        store = lambda i: pltpu.make_async_copy(sv_buf.at[piece(i)], sv_hbm.at[piece(i)], out_sems.at[i])
        for i in range(MIX_PIECES):
            load(i).start()

        def rows_piece(i, carry):
            load(i).wait()
            for j in range(cpp):
                rows = pl.ds(pl.multiple_of(i * rp + j * ACH, ACH), ACH)
                _, _, vn = _layernorm(va_buf[rows, :].astype(f32), g_ref[...], b_ref[...])
                for gi in range(2):
                    sl = slice(gi * ACH, (gi + 1) * ACH)
                    sv_buf[rows, sl] = (_dot(ws_ref[gi], vn[:, sl]) + bs_ref[:, gi:gi + 1]).astype(bf16)
                col_buf[0, rows, :] = vn[:, half:half + ACH]
                col_buf[1, rows, :] = vn[:, half + ACH:]
            return carry

        lax.fori_loop(0, MIX_PIECES, rows_piece, 0)

        def cols_step(cidx, carry):
            rows = pl.ds(cidx, ACH, stride=GW)
            for gi in range(2, 4):
                col_buf[gi - 2, rows, :] = _dot(ws_ref[gi], col_buf[gi - 2, rows, :]) + bs_ref[:, gi:gi + 1]
            return carry

        lax.fori_loop(0, GW, cols_step, 0, unroll=8)

        def out_piece(i, carry):
            for j in range(cpp):
                rows = pl.ds(pl.multiple_of(i * rp + j * ACH, ACH), ACH)
                sv_buf[rows, half:half + ACH] = col_buf[0, rows, :].astype(bf16)
                sv_buf[rows, half + ACH:] = col_buf[1, rows, :].astype(bf16)
            store(i).start()
            return carry

        lax.fori_loop(0, MIX_PIECES, out_piece, 0)
        for i in range(MIX_PIECES):
            store(i).wait()

    vm = pl.BlockSpec(memory_space=pltpu.VMEM)
    hbm = pl.BlockSpec(memory_space=pl.ANY)
    return pl.pallas_call(
        body, name="mix_fwd", out_shape=jax.ShapeDtypeStruct((l, AW), bf16),
        in_specs=[hbm, vm, vm, vm, vm], out_specs=hbm,
        scratch_shapes=[pltpu.VMEM((l, AW), bf16), pltpu.VMEM((2, l, ACH), f32), pltpu.VMEM((l, AW), bf16),
                        pltpu.SemaphoreType.DMA((MIX_PIECES,)), pltpu.SemaphoreType.DMA((MIX_PIECES,))],
        compiler_params=_cp(),
    )(p, ln_g, ln_b, ws, bs_t)


def _mix_bwd(p, dsv, ln_g, ln_b, ws_t):
    l = p.shape[0]
    half = AW // 2
    rp = l // MIX_PIECES
    cpp = rp // ACH

    def body(p_ref, dsv_hbm, g_ref, b_ref, wst_ref, dva_hbm, dws_ref, dbs_ref, dg_ref, db_ref,
             va_buf, dsv_buf, vn_col, ds_col, dva_buf, va_sems, ds_sems, out_sems):
        piece = lambda i: pl.ds(pl.multiple_of(i * rp, rp), rp)
        load_va = lambda i: pltpu.make_async_copy(p_ref.at[piece(i), pl.ds(PVA, AW)], va_buf.at[piece(i)], va_sems.at[i])
        load_ds = lambda i: pltpu.make_async_copy(dsv_hbm.at[piece(i)], dsv_buf.at[piece(i)], ds_sems.at[i])
        store = lambda i: pltpu.make_async_copy(dva_buf.at[piece(i)], dva_hbm.at[piece(i)], out_sems.at[i])
        for i in range(MIX_PIECES):
            load_va(i).start()
            load_ds(i).start()
        dws_ref[...] = jnp.zeros_like(dws_ref)
        dbs_ref[...] = jnp.zeros_like(dbs_ref)
        dg_ref[...] = jnp.zeros_like(dg_ref)
        db_ref[...] = jnp.zeros_like(db_ref)

        def rows_piece(i, carry):
            load_va(i).wait()
            load_ds(i).wait()
            for j in range(cpp):
                rows = pl.ds(pl.multiple_of(i * rp + j * ACH, ACH), ACH)
                _, _, vn = _layernorm(va_buf[rows, :].astype(f32), g_ref[...], b_ref[...])
                ds = dsv_buf[rows, :].astype(f32)
                for gi in range(2):
                    sl = slice(gi * ACH, (gi + 1) * ACH)
                    dws_ref[gi] += _nt(ds[:, sl], vn[:, sl])
                    dbs_ref[gi] += ds[:, sl]
                for gi in range(2):
                    sl = slice(half + gi * ACH, half + (gi + 1) * ACH)
                    vn_col[gi, rows, :] = vn[:, sl]
                    ds_col[gi, rows, :] = ds[:, sl]
            return carry

        lax.fori_loop(0, MIX_PIECES, rows_piece, 0)

        def cols_step(cidx, carry):
            rows = pl.ds(cidx, ACH, stride=GW)
            for gi in range(2, 4):
                ds = ds_col[gi - 2, rows, :]
                dws_ref[gi] += _nt(ds, vn_col[gi - 2, rows, :])
                dbs_ref[gi] += ds
                ds_col[gi - 2, rows, :] = _dot(wst_ref[gi], ds)
            return carry

        lax.fori_loop(0, GW, cols_step, 0, unroll=8)

        def out_piece(i, carry):
            for j in range(cpp):
                rows = pl.ds(pl.multiple_of(i * rp + j * ACH, ACH), ACH)
                vhat, rstd, _ = _layernorm(va_buf[rows, :].astype(f32), g_ref[...], b_ref[...])
                ds = dsv_buf[rows, :].astype(f32)
                dvn = jnp.concatenate([_dot(wst_ref[0], ds[:, 0:ACH]), _dot(wst_ref[1], ds[:, ACH:half]),
                                       ds_col[0, rows, :], ds_col[1, rows, :]], axis=1)
                dg_ref[...] += jnp.sum(dvn * vhat, axis=0, keepdims=True)
                db_ref[...] += jnp.sum(dvn, axis=0, keepdims=True)
                dvh = dvn * g_ref[...]
                dva = rstd * (dvh - jnp.mean(dvh, axis=-1, keepdims=True) - vhat * jnp.mean(dvh * vhat, axis=-1, keepdims=True))
                dva_buf[rows, :] = dva.astype(bf16)
            store(i).start()
            return carry

        lax.fori_loop(0, MIX_PIECES, out_piece, 0)
        for i in range(MIX_PIECES):
            store(i).wait()

    vm = pl.BlockSpec(memory_space=pltpu.VMEM)
    hbm = pl.BlockSpec(memory_space=pl.ANY)
    dma = lambda: pltpu.SemaphoreType.DMA((MIX_PIECES,))
    return pl.pallas_call(
        body, name="mix_bwd",
        out_shape=(jax.ShapeDtypeStruct((l, AW), bf16), jax.ShapeDtypeStruct((4, ACH, ACH), f32), jax.ShapeDtypeStruct((4, ACH, ACH), f32),
                   jax.ShapeDtypeStruct((1, AW), f32), jax.ShapeDtypeStruct((1, AW), f32)),
        in_specs=[hbm, hbm, vm, vm, vm], out_specs=(hbm, vm, vm, vm, vm),
        scratch_shapes=[pltpu.VMEM((l, AW), bf16), pltpu.VMEM((l, AW), bf16), pltpu.VMEM((2, l, ACH), f32), pltpu.VMEM((2, l, ACH), f32),
                        pltpu.VMEM((l, AW), bf16), dma(), dma(), dma()],
        compiler_params=_cp(),
    )(p, dsv, ln_g, ln_b, ws_t)


def _mid(x, tgt, p, o_f, o_b, sv, gate, gf, gb_norm, w_pa, w_pb, w_out, tl):
    l = x.shape[0]

    def body(x_ref, t_ref, zb_ref, ua_ref, za_ref, g1_ref, g2_ref, of_ref, ob_ref, sv_ref, gate_ref, gf_ref, gbn_ref,
             wpa_ref, wpb_ref, wout_ref,
             dx1_ref, dzbua_ref, dzag_ref, dsv_ref, do_ref, dwout_bf, dwpa_bf, dwpb_bf, dgf_ref, dgate_ref, dgbn_ref, loss_ref,
             dwout_ref, dwpa_ref, dwpb_ref):
        @pl.when(pl.program_id(0) == 0)
        def _():
            for r in (dwout_ref, dwpa_ref, dwpb_ref, dgf_ref, dgate_ref, dgbn_ref, loss_ref):
                r[...] = jnp.zeros_like(r)

        o = of_ref[...].astype(f32) + ob_ref[...].astype(f32)
        rr = jnp.concatenate(
            [jnp.broadcast_to(lax.rsqrt(jnp.mean(o[:, h * HV:(h + 1) * HV] ** 2, axis=-1, keepdims=True) + EPS), (tl, HV))
             for h in range(NH)], axis=1)
        ohat = o * rr
        on = ohat * gbn_ref[...]
        szb, dszb = _silu_and_grad(zb_ref[...].astype(f32))
        tb = on * szb
        u = ua_ref[...].astype(f32)
        svv = sv_ref[...].astype(f32)
        sza, dsza = _silu_and_grad(za_ref[...].astype(f32))
        ta = u * svv * sza
        ya = _dot(ta, wpa_ref[...])
        yb = _dot(tb, wpb_ref[...])
        g1 = _sigmoid(g1_ref[...].astype(f32))
        g2 = _sigmoid(g2_ref[...].astype(f32))
        m = g1 * ya + g2 * yb
        y2 = _dot(m, wout_ref[...])
        x1 = x_ref[...] + gate_ref[...] * y2
        r1 = lax.rsqrt(jnp.mean(x1 * x1, axis=-1, keepdims=True) + EPS)
        x1n = x1 * r1
        err = x1n * gf_ref[...] - t_ref[...]
        loss_ref[...] += jnp.sum(jnp.sum(err * err, axis=-1, keepdims=True), axis=0, keepdims=True) * (0.5 / D)
        dout = err * (1.0 / D)
        dgf_ref[...] += jnp.sum(dout * x1n, axis=0, keepdims=True)
        dx1n = dout * gf_ref[...]
        dx1 = r1 * (dx1n - x1n * jnp.mean(dx1n * x1n, axis=-1, keepdims=True))
        dx1_ref[...] = dx1
        dgate_ref[...] += jnp.sum(dx1 * y2, axis=0, keepdims=True)
        dy2 = dx1 * gate_ref[...]
        dwout_ref[...] += _tn(m, dy2)
        dm = _nt(dy2, wout_ref[...])
        dya = dm * g1
        dyb = dm * g2
        dzag_ref[:, AW:AW + D] = (dm * ya * g1 * (1.0 - g1)).astype(bf16)
        dzag_ref[:, AW + D:] = (dm * yb * g2 * (1.0 - g2)).astype(bf16)
        dwpa_ref[...] += _tn(ta, dya)
        dta = _nt(dya, wpa_ref[...])
        dzbua_ref[:, AW:] = (dta * svv * sza).astype(bf16)
        dsv_ref[...] = (dta * u * sza).astype(bf16)
        dzag_ref[:, 0:AW] = (dta * u * svv * dsza).astype(bf16)
        dwpb_ref[...] += _tn(tb, dyb)
        dtb = _nt(dyb, wpb_ref[...])
        don = dtb * szb
        dzbua_ref[:, 0:AW] = (dtb * on * dszb).astype(bf16)
        dgbn_ref[...] += jnp.sum(don * ohat, axis=0, keepdims=True)
        doh = don * gbn_ref[...]
        prod = doh * ohat
        mh = jnp.concatenate(
            [jnp.broadcast_to(jnp.mean(prod[:, h * HV:(h + 1) * HV], axis=-1, keepdims=True), (tl, HV)) for h in range(NH)], axis=1)
        do_ref[...] = (rr * (doh - ohat * mh)).astype(bf16)

        @pl.when(pl.program_id(0) == l // tl - 1)
        def _():
            for acc, out in ((dwout_ref, dwout_bf), (dwpa_ref, dwpa_bf), (dwpb_ref, dwpb_bf)):
                out[...] = acc[...].astype(bf16)

    row = lambda w, j: pl.BlockSpec((tl, w), lambda i, j=j: (i, j))
    full = lambda *s: pl.BlockSpec(s, lambda i: (0,) * len(s))
    return pl.pallas_call(
        body, name="mid", grid=(l // tl,),
        out_shape=(jax.ShapeDtypeStruct((l, D), f32), jax.ShapeDtypeStruct((l, 2 * AW), bf16), jax.ShapeDtypeStruct((l, AW + 2 * D), bf16),
                   jax.ShapeDtypeStruct((l, AW), bf16), jax.ShapeDtypeStruct((l, VW), bf16),
                   jax.ShapeDtypeStruct((D, D), bf16), jax.ShapeDtypeStruct((AW, D), bf16), jax.ShapeDtypeStruct((VW, D), bf16),
                   jax.ShapeDtypeStruct((1, D), f32), jax.ShapeDtypeStruct((1, D), f32), jax.ShapeDtypeStruct((1, VW), f32),
                   jax.ShapeDtypeStruct((1, 1), f32)),
        scratch_shapes=[pltpu.VMEM((D, D), f32), pltpu.VMEM((AW, D), f32), pltpu.VMEM((VW, D), f32)],
        in_specs=[row(D, 0), row(D, 0), row(AW, PZB // AW), row(AW, PUA // AW), row(AW, PZA // AW), row(D, PG1 // D), row(D, PG2 // D),
                  row(VW, 0), row(VW, 0), row(AW, 0), full(1, D), full(1, D), full(1, VW), full(AW, D), full(VW, D), full(D, D)],
        out_specs=(row(D, 0), row(2 * AW, 0), row(AW + 2 * D, 0), row(AW, 0), row(VW, 0),
                   full(D, D), full(AW, D), full(VW, D), full(1, D), full(1, D), full(1, VW), full(1, 1)),
        compiler_params=_cp(("arbitrary",)),
    )(x, tgt, p, p, p, p, p, o_f, o_b, sv, gate, gf, gb_norm, w_pa, w_pb, w_out)


def _in_bwd(x, dx1, dqkv, dzbua, dva, dzag, dlr, w_pad, g, scale, tl, comm):
    l = x.shape[0]
    c_in, c_out, c_sems = comm.specs()

    def body(*refs):
        cin = refs[14:14 + len(c_in)]
        outs = refs[14 + len(c_in):]
        comm.run(cin, outs[4:4 + comm.n], outs[4 + comm.n:], l // tl, lambda: compute(*refs[:14], *outs[:4]))

    def compute(x_ref, dx1_ref, a_ref, b_ref, c_ref, e_ref, lr_ref, wa_ref, wb_ref, wc_ref, we_ref, wl_ref, g_ref, sc_ref,
                gx_ref, dsh_ref, dsc_ref, dg_ref):
        @pl.when(pl.program_id(0) == 0)
        def _():
            for r in (dsh_ref, dsc_ref, dg_ref):
                r[...] = jnp.zeros_like(r)

        dh = (_nt(a_ref[...], wa_ref[...]) + _nt(b_ref[...], wb_ref[...]) + _nt(c_ref[...], wc_ref[...]) + _nt(e_ref[...], we_ref[...])
              + _nt(lr_ref[...], wl_ref[...]))
        xv = x_ref[...]
        r = lax.rsqrt(jnp.mean(xv * xv, axis=-1, keepdims=True) + EPS)
        xn = xv * r
        gxn = jnp.sum(dh * xn, axis=0, keepdims=True)
        dsh_ref[...] += jnp.sum(dh, axis=0, keepdims=True)
        dsc_ref[...] += gxn * g_ref[...]
        dg_ref[...] += gxn * (1.0 + sc_ref[...])
        dxn = dh * (g_ref[...] * (1.0 + sc_ref[...]))
        gx_ref[...] = dx1_ref[...] + r * (dxn - xn * jnp.mean(dxn * xn, axis=-1, keepdims=True))

    row = lambda w: pl.BlockSpec((tl, w), lambda i: (i, 0))
    wcol = lambda w, j: pl.BlockSpec((D, w), lambda i, j=j: (0, j))
    vec = pl.BlockSpec((1, D), lambda i: (0, 0))
    return pl.pallas_call(
        body, name="in_bwd", grid=(l // tl,),
        out_shape=(jax.ShapeDtypeStruct((l, D), f32),) + (jax.ShapeDtypeStruct((1, D), f32),) * 3 + tuple(comm.outs),
        in_specs=[row(D), row(D), row(2 * KW + VW), row(2 * AW), row(AW), row(AW + 2 * D), row(LRW),
                  wcol(2 * KW + VW, 0), wcol(2 * AW, PZB // (2 * AW)), wcol(AW, PVA // AW), wcol(AW + 2 * D, PZA // (AW + 2 * D)),
                  wcol(LRW, PLR // LRW), vec, vec] + c_in,
        out_specs=(row(D), vec, vec, vec) + tuple(c_out), scratch_shapes=c_sems,
        compiler_params=_cp(("arbitrary",)),
    )(x, dx1, dqkv, dzbua, dva, dzag, dlr, w_pad, w_pad, w_pad, w_pad, w_pad, g, scale, *comm.ins)


def _tn_matmul(a, bs, tl, name, comm=None, pack=None):
    l, m = a.shape
    k = len(bs)
    comm = comm or _Comm([], [], None)
    c_in, c_out, c_sems = comm.specs()
    acc_shapes = [(m, b.shape[1]) for b in bs]
    n_extra = len(pack.extra) if pack else 0
    n_res = len(pack.outs) if pack else k

    def body(a_ref, *refs):
        b_refs, refs = refs[:k], refs[k:]
        extra, refs = refs[:n_extra], refs[n_extra:]
        cin, refs = refs[:len(c_in)], refs[len(c_in):]
        res, refs = refs[:n_res], refs[n_res:]
        cout, refs = refs[:comm.n], refs[comm.n:]
        accs, sems = (refs[:k], refs[k:]) if pack else (res, refs)

        def compute():
            @pl.when(pl.program_id(0) == 0)
            def _():
                for o_ref in accs:
                    o_ref[...] = jnp.zeros_like(o_ref)

            av = a_ref[...]
            for b_ref, o_ref in zip(b_refs, accs):
                o_ref[...] += _tn(av, b_ref[...])
            if pack:
                pl.when(pl.program_id(0) == l // tl - 1)(lambda: pack.fn(accs, extra, res))

        comm.run(cin, cout, sems, l // tl, compute)

    full = lambda shape: pl.BlockSpec(shape, lambda i: (0,) * len(shape))
    res_shapes = list(pack.outs) if pack else [jax.ShapeDtypeStruct(sh, f32) for sh in acc_shapes]
    return pl.pallas_call(
        body, name=name, grid=(l // tl,), out_shape=tuple(res_shapes) + tuple(comm.outs),
        in_specs=[pl.BlockSpec((tl, m), lambda i: (i, 0))] + [pl.BlockSpec((tl, b.shape[1]), lambda i: (i, 0)) for b in bs]
        + [full(e.shape) for e in (pack.extra if pack else [])] + c_in,
        out_specs=tuple(full(r.shape) for r in res_shapes) + tuple(c_out),
        scratch_shapes=([pltpu.VMEM(sh, f32) for sh in acc_shapes] if pack else []) + c_sems, compiler_params=_cp(("arbitrary",)),
    )(a, *bs, *(pack.extra if pack else []), *comm.ins)


def _pad_gate(w2, gb):
    z = jnp.zeros((RANK, KW), f32)
    tail = jnp.zeros((LRW - 2 * RANK, KW), f32)
    w2f = jnp.concatenate([w2[0], z, tail], axis=0)
    w2b = jnp.concatenate([z, w2[1], tail], axis=0)
    return w2f, w2b, gb[0:1], gb[1:2]


EARLY_A_ROWS = 512


class _NoExchange:
    def __init__(self, w_pa, w_pb, w_out):
        self.weights = (w_pa, w_pb, w_out)

    def gather_proj(self):
        return _Comm([], [], None)

    def proj_weights(self, got):
        return self.weights

    def first(self, dw_out, dw_pa, dw_pb, small):
        return _Comm([], [], None)

    def early_a(self, blocks):
        return _Comm([], [], None)

    def early_b(self, blocks):
        return _Comm([], [], None)

    def late(self, blocks, dgt):
        return _Comm([], [], None)


class _Exchanges:
    def __init__(self, pa, pb, wo):
        self.shards = (pa, pb, wo)

    def gather_proj(self):
        def plan(i, o):
            srcs = [lambda j, r=r: r for r in i]
            dsts = [lambda j: o[0].at[:, _lanes(j)], lambda j: o[1].at[:, _lanes(j)], lambda j: o[2].at[j]]
            return srcs, dsts, None
        sds = jax.ShapeDtypeStruct
        return _Comm(self.shards, [sds((AW, D), bf16), sds((VW, D), bf16), sds((NDEV, 128, D), bf16)], plan)

    def proj_weights(self, got):
        return got[0], got[1], got[2].reshape(D, D)

    def first(self, dw_out, dw_pa, dw_pb, small):
        def plan(i, o):
            srcs = [lambda j: i[0].at[j], lambda j: i[1].at[:, _lanes(j)], lambda j: i[2].at[:, _lanes(j)], lambda j: i[3]]
            dsts = [lambda j, r=r: r.at[j] for r in o]
            return srcs, dsts, None
        sds = jax.ShapeDtypeStruct
        return _Comm([dw_out.reshape(NDEV, 128, D), dw_pa, dw_pb, small],
                     [sds((NDEV, 128, D), bf16), sds((NDEV, AW, 128), bf16), sds((NDEV, VW, 128), bf16),
                      sds((NDEV,) + small.shape, f32)], plan)

    def early_a(self, blocks):
        def plan(i, o):
            return [lambda j: i[0].at[j]], [lambda j: o[0].at[j]], [lambda j: j >= LATE_DESTS - 1]
        return _Comm([blocks], [jax.ShapeDtypeStruct(blocks.shape, bf16)], plan)

    def early_b(self, blocks):
        def plan(i, o):
            return [lambda j: i[0].at[j]], [lambda j: o[0].at[j]], [lambda j: j >= LATE_DESTS - 1]
        return _Comm([blocks], [jax.ShapeDtypeStruct(blocks.shape, bf16)], plan)

    def late(self, blocks, dgt):
        return _LateComm(blocks, dgt)


def _local_step(x, ctx, tgt, mod, modc, norm_g, w_pad, ln_g, ln_b, ws, bs, w2, gb, gb_norm, gf, xch):
    shift, scale, gate = mod[:, 0:D], mod[:, D:2 * D], mod[:, 2 * D:]
    shift_c, scale_c = modc[:, 0:D], modc[:, D:]
    w2f, w2b, gbf, gbb = _pad_gate(w2, gb)

    p, h, *got_proj = _in_proj(x, norm_g, scale, shift, w_pad, 512, xch.gather_proj())
    w_pa, w_pb, w_out = xch.proj_weights(got_proj)
    sc_f, sc_b = _ctx_fwd(ctx, norm_g, scale_c, shift_c, w_pad, w2f, w2b, gbf, gbb)
    o_f, st_f = _gla_fwd(p, w2f, gbf, sc_f, False, 512, "gla_fwd_f")
    o_b, st_b = _gla_fwd(p, w2b, gbb, sc_b, True, 512, "gla_fwd_b")
    sv = _mix_fwd(p, ln_g, ln_b, ws.astype(bf16), bs.T)
    (dx1, dzbua, dzag, dsv, do, dw_out, dw_pa, dw_pb, dgf, dgate, dgbn, loss) = _mid(
        x, tgt, p, o_f, o_b, sv, gate, gf, gb_norm, w_pa, w_pb, w_out, 256)
    dva, dws, dbs_acc, dln_g, dln_b = _mix_bwd(p, dsv, ln_g, ln_b, jnp.swapaxes(ws, 1, 2).astype(bf16))
    small = _rows128(dln_g, dln_b, dws, jnp.sum(dbs_acc, axis=-1), dgbn, dgf, jnp.broadcast_to(loss, (1, 128)))
    blocks_a, blocks_b, *got_first = _tn_matmul(h, [dzbua, dva, dzag], 1024, "dw_early", xch.first(dw_out, dw_pa, dw_pb, small),
                                               _pack_early())

    dqkv_f, dlr_f, dw2f, dgbf, dsc_f, *got_a = _gla_bwd(p, do, st_f, w2f, gbf, None, False, 512, "gla_bwd_f",
                                                        xch.early_a(blocks_a))
    dqkv, dlr, dw2b, dgbb, dsc_b, *got_b = _gla_bwd(p, do, st_b, w2b, gbb, (dqkv_f, dlr_f), True, 512, "gla_bwd_b",
                                                    xch.early_b(blocks_b))
    dwk_c, dwv_c, dwl_c, dmodc, dg_c, dw2c, dgbc = _ctx_bwd(ctx, norm_g, scale_c, shift_c, w_pad, w2f, w2b, gbf, gbb, dsc_f, dsc_b)
    (blocks_late,) = _tn_matmul(h, [dqkv, dlr], 1024, "dw_qkv_lr", pack=_pack_late(dwk_c, dwv_c, dwl_c))
    dw2 = jnp.stack([dw2f[0:RANK] + dw2c[0, 0:RANK], dw2b[RANK:2 * RANK] + dw2c[1, RANK:2 * RANK]])
    dgb = jnp.concatenate([dgbf + dgbc[0], dgbb + dgbc[1]], axis=0)
    dgt = jnp.concatenate([jnp.transpose(dw2.reshape(2, RANK, NDEV, 32), (2, 0, 1, 3)).reshape(NDEV, 2 * RANK * 32),
                           jnp.transpose(dgb.reshape(2, NDEV, 32), (1, 0, 2)).reshape(NDEV, 64),
                           jnp.zeros((NDEV, 64), f32)], axis=1).reshape(NDEV, 9, 128)
    gx, dshift, dscale, dg, *got_late = _in_bwd(x, dx1, dqkv, dzbua, dva, dzag, dlr, w_pad, norm_g, scale, 512,
                                                xch.late(blocks_late, dgt))
    return dict(loss=loss, gx=gx, dmod=jnp.concatenate([dshift, dscale, dgate], axis=1), dmodc=dmodc, dnorm_g=dg + dg_c,
                small=small, blocks_a=blocks_a, blocks_b=blocks_b, blocks_late=blocks_late, dw2=dw2, dgb=dgb,
                dw_pa=dw_pa, dw_pb=dw_pb, dw_out=dw_out, got_first=got_first, got_a=got_a, got_b=got_b, got_late=got_late)


def _rows128(*vs):
    out = []
    for t in vs:
        t = t.reshape(-1)
        pad = (-t.shape[0]) % 128
        out.append(jnp.pad(t, (0, pad)) if pad else t)
    return jnp.concatenate(out).reshape(-1, 128)


def kernel(x, c, ctx, c_ctx, w_mod, b_mod, norm_g, w_in, a_ln_g, a_ln_b, a_ws, a_bs, b_gate_w2, b_gate_b, b_norm_g, w_proj_a, w_proj_b, w_out, final_norm_g, loss_target, m_c_ctx, m_w_mod, m_b_mod, m_norm_g, m_w_in, m_a_ln_g, m_a_ln_b, m_a_ws, m_a_bs, m_b_gate_w2, m_b_gate_b, m_b_norm_g, m_w_proj_a, m_w_proj_b, m_w_out, m_final_norm_g, v_c_ctx, v_w_mod, v_b_mod, v_norm_g, v_w_in, v_a_ln_g, v_a_ln_b, v_a_ws, v_a_bs, v_b_gate_w2, v_b_gate_b, v_b_norm_g, v_w_proj_a, v_w_proj_b, v_w_out, v_final_norm_g):
    me = _me()
    ncol = w_mod.shape[2]

    gate_mine = _rows128(jnp.concatenate([b_gate_w2.reshape(-1), b_gate_b.reshape(-1)]))
    bm_mine = lax.dynamic_slice(b_mod, (0, me * ncol), (1, ncol))
    cs, mods, wg, gates = _gather_first(c, c_ctx.reshape(1, D), w_mod[0], bm_mine, w_in[0].astype(bf16), gate_mine)
    w_pad = _repack_w(wg)
    gflat = gates.reshape(NDEV, 9 * 128)
    w2 = jnp.transpose(gflat[:, 0:2 * RANK * 32].reshape(NDEV, 2, RANK, 32), (1, 2, 0, 3)).reshape(2, RANK, KW)
    gb = jnp.transpose(gflat[:, 2 * RANK * 32:2 * RANK * 32 + 64].reshape(NDEV, 2, 32), (1, 0, 2)).reshape(2, KW)

    mods = jnp.transpose(mods, (1, 0, 2)).reshape(16, 3 * D)
    mod = lax.dynamic_slice(mods, (me, 0), (1, 3 * D))
    modc = mods[8:9, 0:2 * D]

    xch = _Exchanges(w_proj_a[0].astype(bf16), w_proj_b[0].astype(bf16), w_out[0].astype(bf16))
    r = _local_step(x[0], ctx[0], loss_target[0], mod, modc, norm_g, w_pad, a_ln_g, a_ln_b, a_ws[0], a_bs[0], w2, gb,
                    b_norm_g, final_norm_g.reshape(1, D), xch)
    p_out, p_pa, p_pb, smalls_e = r["got_first"]
    (p_in_a,) = r["got_a"]
    (p_in_b,) = r["got_b"]
    _, _, p_in_late, p_gt = r["got_late"]

    n_e = (AW + AW + 4 * ACH * ACH + AW + VW + D) // 128
    row = lambda t: t.reshape(1, D)
    rep_e = _adam_params(smalls_e, [a_ln_g, a_ln_b, a_ws, a_bs, b_norm_g, row(final_norm_g)],
                         [m_a_ln_g, m_a_ln_b, m_a_ws, m_a_bs, m_b_norm_g, row(m_final_norm_g)],
                         [v_a_ln_g, v_a_ln_b, v_a_ws, v_a_bs, v_b_norm_g, row(v_final_norm_g)], "adam_rep_early")
    rep_e = [t[0:5] + (t[5].reshape(D),) for t in rep_e]
    losses = smalls_e[:, n_e, 0]
    loss = losses[0]
    for i in range(1, NDEV):
        loss = loss + losses[i]

    dbm = r["dmod"] + jnp.concatenate([r["dmodc"], jnp.zeros((1, D), f32)], axis=1)
    mod_grads = _mod_tail(r["dnorm_g"], dbm, r["dmod"], r["dmodc"], cs, w_mod)
    mod_res = _adam_plain(mod_grads, [w_mod, norm_g, b_mod, row(c_ctx)], [m_w_mod, m_norm_g, m_b_mod, row(m_c_ctx)],
                          [v_w_mod, v_norm_g, v_b_mod, row(v_c_ctx)], "adam_mod")
    wm, ng, bmod_r, cc = ([t[i] for t in mod_res] for i in range(4))

    a_in = _adam_w_in(p_in_a, p_in_b, p_in_late, w_in, m_w_in, v_w_in)
    blk = _adam_blocks([p_pa, p_pb, p_out], [w_proj_a, w_proj_b, w_out], [m_w_proj_a, m_w_proj_b, m_w_out],
                       [v_w_proj_a, v_w_proj_b, v_w_out], "adam_proj_out")
    a_pa, a_pb, a_out = ([t[i] for t in blk] for i in range(3))
    a_gt = _adam_gate(p_gt, b_gate_w2, b_gate_b, m_b_gate_w2, m_b_gate_b, v_b_gate_w2, v_b_gate_b)
    sh = [(a_in[k], a_pa[k], a_pb[k], a_out[k], a_gt[k], a_gt[4 + k]) for k in range(4)]

    outs = [loss, r["gx"][None]]
    for k in range(4):
        lg, lb, aws, abs_, bng, fng = rep_e[k]
        n_g, bmod = ng[k], bmod_r[k]
        s_in, s_pa, s_pb, s_out, s_w2, s_gb = sh[k]
        outs += [cc[k].reshape(D), wm[k], bmod, n_g, s_in, lg, lb, aws, abs_, s_w2, s_gb, bng, s_pa, s_pb, s_out, fng]
    return tuple(outs)
```

```python
import jax
import jax.numpy as jnp
from jax import lax
from jax.experimental import pallas as pl
from jax.experimental.pallas import tpu as pltpu

f32, bf16 = jnp.float32, jnp.bfloat16

D = 1024
CTX = 256
EPS = 1e-6
AW = 512
ACH = 128
GW = 64
KW = 256
VW = 512
NH = 4
HK = 64
HV = 128
RANK = 16
TAU = 16.0
CH = 64
QSCALE = HK ** -0.5
INW = 5152
NDEV = 8

PQ, PK, PV, PZB, PUA, PVA, PZA, PG1, PG2, PLR, PW = 0, 256, 512, 1024, 1536, 2048, 2560, 3072, 4096, 5120, 5248
LRW = 128

ADAM_LR, ADAM_B1, ADAM_B2, ADAM_EPS, ADAM_WD, ADAM_STEP = 0.001, 0.9, 0.999, 1e-08, 0.01, 10

VMEM_LIMIT = 56 * 1024 * 1024
MESH = pl.DeviceIdType.MESH


def _cp(sem=None):
    return pltpu.CompilerParams(dimension_semantics=sem, vmem_limit_bytes=VMEM_LIMIT)


def _dot(a, b):
    return jnp.dot(a.astype(bf16), b.astype(bf16), preferred_element_type=f32)


def _nt(a, b):
    return lax.dot_general(a.astype(bf16), b.astype(bf16), (((1,), (1,)), ((), ())), preferred_element_type=f32)


def _tn(a, b):
    return lax.dot_general(a.astype(bf16), b.astype(bf16), (((0,), (0,)), ((), ())), preferred_element_type=f32)


def _dot_hi(a, b):
    return jnp.dot(a, b, preferred_element_type=f32, precision=lax.Precision.HIGHEST)


def _sigmoid(x):
    return 1.0 / (1.0 + jnp.exp(-x))


def _log_sigmoid(x):
    return jnp.minimum(x, 0.0) - jnp.log(1.0 + jnp.exp(-jnp.abs(x)))


def _silu_and_grad(z):
    s = _sigmoid(z)
    return z * s, s * (1.0 + z * (1.0 - s))


def _me():
    return 4 * lax.axis_index("x") + 2 * lax.axis_index("y") + lax.axis_index("c")


def _peer(k):
    x, y, c = lax.axis_index("x"), lax.axis_index("y"), lax.axis_index("c")
    px = 1 - x if k & 4 else x
    py = 1 - y if k & 2 else y
    pc = 1 - c if k & 1 else c
    return (px, py, pc), 4 * px + 2 * py + pc


def _fanout(srcs, dsts, send_sems, recv_sems, local_sems, owners=None):
    me = _me()
    n = len(srcs)
    owns = lambda a, j: True if owners is None or owners[a] is None else owners[a](j)

    def guarded(cond, fn):
        if cond is True:
            fn()
        else:
            pl.when(cond)(fn)

    def copies(with_recvs):
        local = [pltpu.make_async_copy(srcs[a](me), dsts[a](me), local_sems.at[a]) for a in range(n)]
        sends, recvs = [], []
        for k in range(1, NDEV):
            dev, idx = _peer(k)
            for a in range(n):
                s = (k - 1) * n + a
                sends.append((owns(a, idx), pltpu.make_async_remote_copy(
                    src_ref=srcs[a](idx), dst_ref=dsts[a](me), send_sem=send_sems.at[s], recv_sem=recv_sems.at[s],
                    device_id=dev, device_id_type=MESH)))
                if with_recvs:
                    recvs.append((owns(a, me), pltpu.make_async_remote_copy(
                        src_ref=srcs[a](idx), dst_ref=dsts[a](idx), send_sem=send_sems.at[s], recv_sem=recv_sems.at[s],
                        device_id=dev, device_id_type=MESH)))
        return local, sends, recvs

    def start():
        local, sends, _ = copies(False)
        for a, cp in enumerate(local):
            guarded(owns(a, me), cp.start)
        for cond, cp in sends:
            guarded(cond, cp.start)

    def finish():
        local, sends, recvs = copies(True)
        for cond, cp in recvs:
            guarded(cond, cp.wait_recv)
        for cond, cp in sends:
            guarded(cond, cp.wait_send)
        for a, cp in enumerate(local):
            guarded(owns(a, me), cp.wait)

    return start, finish


class _Comm:
    def __init__(self, ins, outs, plan):
        self.ins, self.outs, self.plan = list(ins), list(outs), plan
        self.n = len(self.outs)

    def specs(self):
        hbm = pl.BlockSpec(memory_space=pl.ANY)
        return [hbm] * len(self.ins), [hbm] * self.n, _fanout_sems(self.n) if self.n else []

    def run(self, in_refs, out_refs, sems, nsteps, compute):
        if not self.n:
            compute()
            return

        def hooks():
            srcs, dsts, owners = self.plan(in_refs, out_refs)
            return _fanout(srcs, dsts, sems[0], sems[1], sems[2], owners)

        pl.when(pl.program_id(0) == 0)(lambda: hooks()[0]())
        compute()
        pl.when(pl.program_id(0) == nsteps - 1)(lambda: hooks()[1]())


LATE_MID_STEP = 1


class _LateComm:
    def __init__(self, blocks, dgt):
        sds = jax.ShapeDtypeStruct
        self.ins = [blocks, dgt]
        self.outs = [sds((D, SHARD), bf16), sds((D, SHARD), bf16), sds((4, D, SHARD), bf16), sds(dgt.shape, f32)]
        self.n = len(self.outs)

    def specs(self):
        hbm = pl.BlockSpec(memory_space=pl.ANY)
        scratch = [pltpu.VMEM((3, D, SHARD), bf16), pltpu.SemaphoreType.DMA((2,)), pltpu.SemaphoreType.DMA((4,)),
                   pltpu.SemaphoreType.DMA((3,))] + _fanout_sems(1)
        return [hbm] * 2, [hbm] * self.n, scratch

    def run(self, in_refs, out_refs, scratch, nsteps, compute):
        late_ref, dgt_ref = in_refs
        sib_ref, pair_ref, parts_ref, gt_ref = out_refs
        vbuf, send_sems, recv_sems, local_sems, g_send, g_recv, g_local = scratch
        x, y, c = lax.axis_index("x"), lax.axis_index("y"), lax.axis_index("c")
        chip = 2 * x + y
        is_owner_chip = chip == 0
        step = pl.program_id(0)

        def to_sibling():
            return pltpu.make_async_remote_copy(src_ref=late_ref.at[1 - c], dst_ref=sib_ref, send_sem=send_sems.at[0],
                                                recv_sem=recv_sems.at[0], device_id=(x, y, 1 - c), device_id_type=MESH)

        def to_owner(k):
            return pltpu.make_async_remote_copy(src_ref=pair_ref, dst_ref=parts_ref.at[k], send_sem=send_sems.at[1],
                                                recv_sem=recv_sems.at[k], device_id=(0, 0, c), device_id_type=MESH)

        def own_copy():
            return pltpu.make_async_copy(pair_ref, parts_ref.at[0], local_sems.at[2])

        def gates():
            return _fanout([lambda j: dgt_ref.at[j]], [lambda j: gt_ref.at[j]], g_send, g_recv, g_local)

        @pl.when(step == 0)
        def _():
            to_sibling().start()
            gates()[0]()

        compute()

        @pl.when(step == LATE_MID_STEP)
        def _():
            mine = pltpu.make_async_copy(late_ref.at[c], vbuf.at[0], local_sems.at[0])
            mine.start()
            to_sibling().wait_recv()
            theirs = pltpu.make_async_copy(sib_ref, vbuf.at[1], local_sems.at[1])
            theirs.start()
            mine.wait()
            theirs.wait()
            vbuf[2] = (vbuf[0].astype(f32) + vbuf[1].astype(f32)).astype(bf16)
            pltpu.sync_copy(vbuf.at[2], pair_ref)
            pl.when(is_owner_chip)(lambda: own_copy().start())
            pl.when(jnp.logical_not(is_owner_chip))(lambda: to_owner(chip).start())

        @pl.when(step == nsteps - 1)
        def _():
            @pl.when(is_owner_chip)
            def _():
                for k in range(1, 4):
                    to_owner(k).wait_recv()
                own_copy().wait()

            pl.when(jnp.logical_not(is_owner_chip))(lambda: to_owner(chip).wait_send())
            to_sibling().wait_send()
            gates()[1]()


def _fanout_sems(n):
    return [pltpu.SemaphoreType.DMA(((NDEV - 1) * n,)), pltpu.SemaphoreType.DMA(((NDEV - 1) * n,)), pltpu.SemaphoreType.DMA((n,))]


def _lanes(j):
    return pl.ds(pl.multiple_of(j * 128, 128), 128)


def _gather_first(c_row, cctx_row, wm, bm, wi, gate):
    def body(c_ref, cctx_ref, wm_ref, bm_ref, wi_ref, g_ref, cs_ref, mods_ref, owi, og, call_ref, mine_ref,
             send_sems, recv_sems, local_sems, c_send, c_recv, c_local, m_send, m_recv, m_local):
        x, y, c = lax.axis_index("x"), lax.axis_index("y"), lax.axis_index("c")
        sibling = (x, y, 1 - c)
        chips = [(1 - x, y), (x, 1 - y), (1 - x, 1 - y)]
        index = lambda px, py, pc: 4 * px + 2 * py + pc
        arrays = ((wi_ref, owi), (g_ref, og))
        n = len(arrays)

        def copy(a, k, block, to, own=False):
            src, out = arrays[a]
            return pltpu.make_async_remote_copy(
                src_ref=src if own else out.at[index(*block)], dst_ref=out.at[index(*block)],
                send_sem=send_sems.at[k * n + a], recv_sem=recv_sems.at[k * n + a], device_id=to, device_id_type=MESH)

        c_start, c_finish = _fanout([lambda j: c_ref], [lambda j: call_ref.at[j]], c_send, c_recv, c_local)
        c_start()
        mine = [pltpu.make_async_copy(src, out.at[index(x, y, c)], local_sems.at[a]) for a, (src, out) in enumerate(arrays)]
        first = [copy(a, 0, (x, y, c), sibling, own=True) for a in range(n)]
        first += [copy(a, 1 + j, (x, y, c), (*chip, c), own=True) for j, chip in enumerate(chips) for a in range(n)]
        for cp in mine + first:
            cp.start()

        c_finish()
        cs = jnp.concatenate([call_ref[j] for j in range(NDEV)] + [cctx_ref[...], jnp.zeros((16 - NDEV - 1, D), f32)], axis=0)
        cs_ref[...] = cs
        s, _ = _silu_and_grad(cs)
        mine_ref[...] = _dot_hi(s, wm_ref[...]) + bm_ref[...]
        m_start, m_finish = _fanout([lambda j: mine_ref], [lambda j: mods_ref.at[j]], m_send, m_recv, m_local)
        m_start()

        passed = []
        for j, chip in enumerate(chips):
            for a in range(n):
                copy(a, 1 + j, (*chip, c), (x, y, c)).wait_recv()
            for a in range(n):
                cp = copy(a, 4 + j, (*chip, c), sibling)
                cp.start()
                passed.append(cp)
        for a in range(n):
            copy(a, 0, sibling, (x, y, c)).wait_recv()
        for j, chip in enumerate(chips):
            for a in range(n):
                copy(a, 4 + j, (*chip, 1 - c), (x, y, c)).wait_recv()
        for cp in first + passed:
            cp.wait_send()
        for cp in mine:
            cp.wait()
        m_finish()

    hbm = pl.BlockSpec(memory_space=pl.ANY)
    vm = pl.BlockSpec(memory_space=pltpu.VMEM)
    ncol = wm.shape[1]
    return pl.pallas_call(
        body, name="gather_first",
        out_shape=(jax.ShapeDtypeStruct((16, D), f32), jax.ShapeDtypeStruct((NDEV, 16, ncol), f32),
                   jax.ShapeDtypeStruct((NDEV,) + wi.shape, bf16), jax.ShapeDtypeStruct((NDEV,) + gate.shape, f32)),
        in_specs=[vm, vm, vm, vm, hbm, hbm], out_specs=(vm, vm, hbm, hbm),
        scratch_shapes=[pltpu.VMEM((NDEV, 1, D), f32), pltpu.VMEM((16, ncol), f32)] + _fanout_sems(2) + _fanout_sems(1) + _fanout_sems(1),
        compiler_params=_cp(),
    )(c_row, cctx_row, wm, bm, wi, gate)


SHARD = INW // NDEV
ROWS_RP = 128


def _overlap(lo, hi, a, b):
    s, e = max(lo, a), min(hi, b)
    return (s, e) if s < e else None


def _repack_w(wg):
    segs = ((0, 1024, PQ), (1024, 1024 + 2 * RANK, PLR), (1024 + 2 * RANK, INW, PZB))

    def body(g_ref, o_ref):
        for j in range(NDEV):
            lo, hi = j * SHARD, (j + 1) * SHARD
            for a, b, pad0 in segs:
                ov = _overlap(lo, hi, a, b)
                if ov:
                    s, e = ov
                    o_ref[:, pad0 + s - a:pad0 + e - a] = g_ref[j, :, s - lo:e - lo]
        o_ref[:, PLR + 2 * RANK:PW] = jnp.zeros((ROWS_RP, PW - PLR - 2 * RANK), bf16)

    return pl.pallas_call(
        body, name="repack_w", grid=(D // ROWS_RP,), out_shape=jax.ShapeDtypeStruct((D, PW), bf16),
        in_specs=[pl.BlockSpec((NDEV, ROWS_RP, SHARD), lambda i: (0, i, 0))],
        out_specs=pl.BlockSpec((ROWS_RP, PW), lambda i: (i, 0)), compiler_params=_cp(("arbitrary",)),
    )(wg)


LATE_END = 1024 + 2 * RANK
LATE_DESTS = 2


def _pack_blocks(o_ref, srcs, dests, dtype):
    for n, j in enumerate(dests):
        lo, hi = j * SHARD, (j + 1) * SHARD
        done = lo
        for a, b, src in srcs:
            ov = _overlap(lo, hi, a, b)
            if ov:
                s, e = ov
                if s > done:
                    o_ref[n, :, done - lo:s - lo] = jnp.zeros((ROWS_RP, s - done), dtype)
                o_ref[n, :, s - lo:e - lo] = src[:, s - a:e - a].astype(dtype)
                done = e
        if done < hi:
            o_ref[n, :, done - lo:hi - lo] = jnp.zeros((ROWS_RP, hi - done), dtype)


class _Pack:
    def __init__(self, extra, outs, fn):
        self.extra, self.outs, self.fn = list(extra), list(outs), fn


def _pack_early():
    def fn(accs, extra, outs):
        zbua, va, zag = accs
        for r0 in range(0, D, ROWS_RP):
            rows = pl.ds(r0, ROWS_RP)
            out, o0 = (outs[0], r0) if r0 < EARLY_A_ROWS else (outs[1], r0 - EARLY_A_ROWS)
            _pack_blocks(out.at[:, pl.ds(o0, ROWS_RP)], ((LATE_END, 2080, zbua.at[rows]), (2080, 2592, va.at[rows]), (2592, INW, zag.at[rows])),
                         range(NDEV), bf16)

    sds = jax.ShapeDtypeStruct
    return _Pack([], [sds((NDEV, EARLY_A_ROWS, SHARD), bf16), sds((NDEV, D - EARLY_A_ROWS, SHARD), bf16)], fn)


def _pack_late(dwk_c, dwv_c, dwl_c):
    def fn(accs, extra, outs):
        qkv_ref, lr_ref = accs
        kc_ref, vc_ref, lc_ref = extra
        for r0 in range(0, D, ROWS_RP):
            rows = pl.ds(r0, ROWS_RP)
            qkv = qkv_ref[rows, :] + jnp.concatenate([jnp.zeros((ROWS_RP, KW), f32), kc_ref[rows, :], vc_ref[rows, :]], axis=1)
            lr = lr_ref[rows, :] + lc_ref[rows, :]
            _pack_blocks(outs[0].at[:, rows], ((0, 1024, qkv), (1024, LATE_END, lr)), range(LATE_DESTS), bf16)

    return _Pack([dwk_c, dwv_c, dwl_c], [jax.ShapeDtypeStruct((LATE_DESTS, D, SHARD), bf16)], fn)


def _mod_tail(dnorm, dbm, dmod, dmodc, cs, wm):
    ncol = wm.shape[2]

    def body(dn_ref, dbm_ref, dmod_ref, dmodc_ref, cs_ref, wm_ref, g_wm, g_ng, g_bm, g_cc, *rest):
        a_dn, a_dbm, a_dmod, a_dmodc, dm_rows, gc_ref, a_gc = rest[0:7]
        s1, r1, l1, s2, r2, l2 = rest[7:13]
        start, finish = _fanout([lambda j: dn_ref, lambda j: dbm_ref, lambda j: dmod_ref, lambda j: dmodc_ref],
                                [lambda j, r=r: r.at[j] for r in (a_dn, a_dbm, a_dmod, a_dmodc)], s1, r1, l1)
        start()
        finish()

        def total(ref):
            t = ref[0]
            for j in range(1, NDEV):
                t = t + ref[j]
            return t

        g_ng[...] = total(a_dn)
        g_bm[...] = total(a_dbm)
        dmodc_tot = jnp.concatenate([total(a_dmodc), jnp.zeros((1, D), f32)], axis=1)
        dm_rows[...] = jnp.concatenate([a_dmod[j] for j in range(NDEV)] + [dmodc_tot, jnp.zeros((16 - NDEV - 1, 3 * D), f32)], axis=0)
        dm = dm_rows[:, pl.ds(pl.multiple_of(_me() * ncol, 128), ncol)]
        s, ds = _silu_and_grad(cs_ref[...])
        part = lax.dot_general(dm[8:9, :], wm_ref[0], (((1,), (1,)), ((), ())), preferred_element_type=f32,
                               precision=lax.Precision.HIGHEST)
        gc_ref[...] = part * ds[8:9, :]
        start2, finish2 = _fanout([lambda j: gc_ref], [lambda j: a_gc.at[j]], s2, r2, l2)
        start2()
        g = lax.dot_general(s, dm, (((0,), (0,)), ((), ())), preferred_element_type=f32, precision=lax.Precision.HIGHEST)
        g_wm[...] = g[None]
        finish2()
        g_cc[...] = total(a_gc)

    sds = jax.ShapeDtypeStruct
    row = lambda n: pltpu.VMEM((NDEV, 1, n), f32)
    return pl.pallas_call(
        body, name="mod_tail", out_shape=(sds(wm.shape, f32), sds((1, D), f32), sds((1, 3 * D), f32), sds((1, D), f32)),
        scratch_shapes=[row(D), row(3 * D), row(3 * D), row(2 * D), pltpu.VMEM((16, 3 * D), f32), pltpu.VMEM((1, D), f32), row(D)]
        + _fanout_sems(4) + _fanout_sems(1),
        compiler_params=_cp(),
    )(dnorm, dbm, dmod, dmodc, cs, wm)


def _adam_update(g, w_ref, m_ref, v_ref, go_ref, d_ref, mo_ref, vo_ref):
    c1 = 1.0 / (1.0 - ADAM_B1 ** ADAM_STEP)
    c2 = 1.0 / (1.0 - ADAM_B2 ** ADAM_STEP)
    mn = ADAM_B1 * m_ref[...] + (1.0 - ADAM_B1) * g
    vn = ADAM_B2 * v_ref[...] + (1.0 - ADAM_B2) * (g * g)
    go_ref[...] = g
    mo_ref[...] = mn
    vo_ref[...] = vn
    d_ref[...] = -ADAM_LR * ((mn * c1) / (jnp.sqrt(vn * c2) + ADAM_EPS) + ADAM_WD * w_ref[...])


def _adam_w_in(parts_a, parts_b, parts_late, w, m, v):
    na = EARLY_A_ROWS // ROWS_RP
    n = D // ROWS_RP
    whole = SHARD // 128 * 128

    def body(a_ref, b_ref, l_ref, w_hbm, m_hbm, v_hbm, go_hbm, d_hbm, mo_hbm, vo_hbm, ibuf, obuf, isem, osem):
        step = pl.program_id(0)

        def cols(t):
            return pl.ds(pl.multiple_of(t * ROWS_RP, ROWS_RP), ROWS_RP)

        def fetch(t):
            return [pltpu.make_async_copy(r.at[:, 0, cols(t)], ibuf.at[t % 2, k], isem.at[t % 2, k])
                    for k, r in enumerate((w_hbm, m_hbm, v_hbm))]

        def flush(t):
            return [pltpu.make_async_copy(obuf.at[t % 2, k], r.at[:, 0, cols(t)], osem.at[t % 2, k])
                    for k, r in enumerate((go_hbm, d_hbm, mo_hbm, vo_hbm))]

        def start(cps):
            for cp in cps:
                cp.start()

        def wait(cps):
            for cp in cps:
                cp.wait()

        pl.when(step == 0)(lambda: start(fetch(step)))
        pl.when(step + 1 < n)(lambda: start(fetch(step + 1)))
        me = _me()
        first = step < na
        early = jnp.where(first, a_ref[0], b_ref[0]).astype(f32)
        for i in range(1, NDEV):
            early = early + jnp.where(first, a_ref[i], b_ref[i]).astype(f32)
        late = l_ref[0].astype(f32)
        for i in range(1, 4):
            late = late + l_ref[i].astype(f32)
        g = jnp.where(me >= LATE_DESTS - 1, early, 0.0) + jnp.where(me < LATE_DESTS, late, 0.0)
        gt = jnp.concatenate([g[:, c:c + 128].T for c in range(0, whole, 128)] + [g[:, SHARD - 128:].T[128 - (SHARD - whole):]],
                             axis=0)
        wait(fetch(step))
        pl.when(step >= 2)(lambda: wait(flush(step - 2)))
        slot = step % 2
        _adam_update(gt, *(ibuf.at[slot, k] for k in range(3)), *(obuf.at[slot, k] for k in range(4)))
        start(flush(step))

        @pl.when(step == n - 1)
        def _():
            wait(flush(step - 1))
            wait(flush(step))

    hbm = pl.BlockSpec(memory_space=pl.ANY)
    res = pl.pallas_call(
        body, name="adam_w_in", grid=(n,), out_shape=tuple(jax.ShapeDtypeStruct((SHARD, 1, D), f32) for _ in range(4)),
        in_specs=[pl.BlockSpec((NDEV, ROWS_RP, SHARD), lambda i: (0, jnp.minimum(i, na - 1), 0)),
                  pl.BlockSpec((NDEV, ROWS_RP, SHARD), lambda i: (0, jnp.maximum(i - na, 0), 0)),
                  pl.BlockSpec((4, ROWS_RP, SHARD), lambda i: (0, i, 0)), hbm, hbm, hbm],
        out_specs=(hbm, hbm, hbm, hbm),
        scratch_shapes=[pltpu.VMEM((2, 3, SHARD, ROWS_RP), f32), pltpu.VMEM((2, 4, SHARD, ROWS_RP), f32),
                        pltpu.SemaphoreType.DMA((2, 3)), pltpu.SemaphoreType.DMA((2, 4))],
        compiler_params=_cp(("arbitrary",)),
    )(parts_a, parts_b, parts_late, *(jnp.transpose(t, (2, 0, 1)) for t in (w, m, v)))
    return tuple(jnp.transpose(t, (1, 2, 0)) for t in res)


def _adam_params(parts, ws, ms, vs, name):
    p = parts.shape[0]
    k = len(ws)
    nrows = [w.size // 128 for w in ws]
    starts = [sum(nrows[:i]) for i in range(k)]

    def shaped(g, shape):
        if len(shape) == 2:
            return jnp.concatenate([g[r:r + 1] for r in range(g.shape[0])], axis=1)
        return g.reshape(shape)

    def body(*refs):
        g_ref = refs[0]
        w_refs, m_refs, v_refs = refs[1:1 + k], refs[1 + k:1 + 2 * k], refs[1 + 2 * k:1 + 3 * k]
        outs = refs[1 + 3 * k:]
        for i in range(k):
            rows = slice(starts[i], starts[i] + nrows[i])
            g = g_ref[0, rows, :]
            for j in range(1, p):
                g = g + g_ref[j, rows, :]
            _adam_update(shaped(g, ws[i].shape), w_refs[i], m_refs[i], v_refs[i], outs[i], outs[k + i], outs[2 * k + i], outs[3 * k + i])

    vm = pl.BlockSpec(memory_space=pltpu.VMEM)
    res = pl.pallas_call(
        body, name=name, out_shape=tuple(jax.ShapeDtypeStruct(w.shape, f32) for _ in range(4) for w in ws),
        in_specs=[vm] * (1 + 3 * k), out_specs=(vm,) * (4 * k), compiler_params=_cp(),
    )(parts, *ws, *ms, *vs)
    return [res[i * k:(i + 1) * k] for i in range(4)]


def _adam_blocks(parts, ws, ms, vs, name):
    k = len(ws)

    def body(*refs):
        g_refs, w_refs, m_refs, v_refs = refs[0:k], refs[k:2 * k], refs[2 * k:3 * k], refs[3 * k:4 * k]
        outs = refs[4 * k:]
        for i in range(k):
            g = g_refs[i][0].astype(f32)
            for j in range(1, parts[i].shape[0]):
                g = g + g_refs[i][j].astype(f32)
            _adam_update(g[None], w_refs[i], m_refs[i], v_refs[i], outs[i], outs[k + i], outs[2 * k + i], outs[3 * k + i])

    vm = pl.BlockSpec(memory_space=pltpu.VMEM)
    res = pl.pallas_call(
        body, name=name, out_shape=tuple(jax.ShapeDtypeStruct(w.shape, f32) for _ in range(4) for w in ws),
        in_specs=[vm] * (4 * k), out_specs=(vm,) * (4 * k), compiler_params=_cp(),
    )(*parts, *ws, *ms, *vs)
    return [res[i * k:(i + 1) * k] for i in range(4)]


def _adam_plain(gs, ws, ms, vs, name):
    k = len(ws)

    def body(*refs):
        g_refs, w_refs, m_refs, v_refs = refs[0:k], refs[k:2 * k], refs[2 * k:3 * k], refs[3 * k:4 * k]
        outs = refs[4 * k:]
        for i in range(k):
            _adam_update(g_refs[i][...], w_refs[i], m_refs[i], v_refs[i], outs[i], outs[k + i], outs[2 * k + i], outs[3 * k + i])

    vm = pl.BlockSpec(memory_space=pltpu.VMEM)
    res = pl.pallas_call(
        body, name=name, out_shape=tuple(jax.ShapeDtypeStruct(w.shape, f32) for _ in range(4) for w in ws),
        in_specs=[vm] * (4 * k), out_specs=(vm,) * (4 * k), compiler_params=_cp(),
    )(*gs, *ws, *ms, *vs)
    return [res[i * k:(i + 1) * k] for i in range(4)]


def _adam_gate(parts, w2, gb, m_w2, m_gb, v_w2, v_gb):
    per = 128 // 32

    def body(p_ref, w2_ref, gb_ref, mw_ref, mb_ref, vw_ref, vb_ref, *outs):
        g = p_ref[0]
        for j in range(1, parts.shape[0]):
            g = g + p_ref[j]
        for d in range(2):
            gd = jnp.concatenate([g[(d * RANK + r) // per:(d * RANK + r) // per + 1, (r % per) * 32:(r % per + 1) * 32]
                                  for r in range(RANK)], axis=0)
            _adam_update(gd, *(t.at[0, d] for t in (w2_ref, mw_ref, vw_ref) + outs[0:4]))
        last = 2 * RANK // per
        gbv = jnp.concatenate([g[last:last + 1, d * 32:(d + 1) * 32] for d in range(2)], axis=0)
        _adam_update(gbv, *(t.at[0] for t in (gb_ref, mb_ref, vb_ref) + outs[4:8]))

    vm = pl.BlockSpec(memory_space=pltpu.VMEM)
    return pl.pallas_call(
        body, name="adam_gate", out_shape=tuple(jax.ShapeDtypeStruct(w.shape, f32) for w in (w2, gb) for _ in range(4)),
        in_specs=[vm] * 7, out_specs=(vm,) * 8, compiler_params=_cp(),
    )(parts, w2, gb, m_w2, m_gb, v_w2, v_gb)


def _in_proj(x, g, scale, shift, w_pad, tl, comm):
    l = x.shape[0]
    c_in, c_out, c_sems = comm.specs()

    def body(*refs):
        x_ref, g_ref, sc_ref, sh_ref, w_ref = refs[:5]
        cin = refs[5:5 + len(c_in)]
        p_ref, h_ref = refs[5 + len(c_in):7 + len(c_in)]
        cout = refs[7 + len(c_in):7 + len(c_in) + comm.n]
        sems = refs[7 + len(c_in) + comm.n:]

        def compute():
            xv = x_ref[...]
            r = lax.rsqrt(jnp.mean(xv * xv, axis=-1, keepdims=True) + EPS)
            h = (xv * r) * (g_ref[...] * (1.0 + sc_ref[...])) + sh_ref[...]
            hb = h.astype(bf16)
            h_ref[...] = hb
            p_ref[...] = jnp.dot(hb, w_ref[...], preferred_element_type=f32).astype(bf16)

        comm.run(cin, cout, sems, l // tl, compute)

    vec = pl.BlockSpec((1, D), lambda i: (0, 0))
    return pl.pallas_call(
        body, name="in_proj", grid=(l // tl,),
        out_shape=(jax.ShapeDtypeStruct((l, PW), bf16), jax.ShapeDtypeStruct((l, D), bf16)) + tuple(comm.outs),
        in_specs=[pl.BlockSpec((tl, D), lambda i: (i, 0)), vec, vec, vec, pl.BlockSpec((D, PW), lambda i: (0, 0))] + c_in,
        out_specs=(pl.BlockSpec((tl, PW), lambda i: (i, 0)), pl.BlockSpec((tl, D), lambda i: (i, 0))) + tuple(c_out),
        scratch_shapes=c_sems, compiler_params=_cp(("arbitrary",)),
    )(x, g, scale, shift, w_pad, *comm.ins)


def _tri(rev):
    i = lax.broadcasted_iota(jnp.int32, (CH, CH), 0)
    j = lax.broadcasted_iota(jnp.int32, (CH, CH), 1)
    return jnp.where((j >= i) if rev else (j <= i), 1.0, 0.0).astype(f32)


def _head_masks():
    lane = lax.broadcasted_iota(jnp.int32, (1, KW), 1) // HK
    return [jnp.where(lane == h, 1.0, 0.0).astype(f32) for h in range(NH)]


def _block_diag():
    r = lax.broadcasted_iota(jnp.int32, (VW, KW), 0) // HV
    c = lax.broadcasted_iota(jnp.int32, (VW, KW), 1) // HK
    return jnp.where(r == c, 1.0, 0.0).astype(f32)


def _decay(lr, w2, gb, tri, rev):
    logits = _dot(lr, w2) + gb
    a = _log_sigmoid(logits) * (1.0 / TAU)
    c = _dot_hi(tri, a)
    cl = c[0:1, :] if rev else c[CH - 1:CH, :]
    return logits, c, cl


def _state_fwd(k, v, c, cl, st, bd):
    return st * jnp.exp(cl) + bd * _tn(v, k * jnp.exp(cl - c))


def _state_bwd(k, v, c, cl, st0, dst, trit):
    ecl = jnp.exp(cl)
    edec = jnp.exp(cl - c)
    kdec = k * edec
    dv = _nt(kdec, dst)
    dkdec = _dot(v, dst)
    dcl = jnp.sum(dst * st0, axis=0, keepdims=True) * ecl + jnp.sum(dkdec * kdec, axis=0, keepdims=True)
    da = _dot_hi(trit, -dkdec * kdec) + dcl
    return dkdec * edec, dv, da, dst * ecl


def _bdot(a, b):
    return lax.dot_general(a.astype(bf16), b.astype(bf16), (((2,), (1,)), ((0,), (0,))), preferred_element_type=f32)


def _bnt(a, b):
    return lax.dot_general(a.astype(bf16), b.astype(bf16), (((2,), (2,)), ((0,), (0,))), preferred_element_type=f32)


def _btn(a, b):
    return lax.dot_general(a.astype(bf16), b.astype(bf16), (((1,), (1,)), ((0,), (0,))), preferred_element_type=f32)


def _scan_chunks(x, rev):
    nc = x.shape[0]
    hi = x.astype(bf16)
    r1 = x - hi.astype(f32)
    mid = r1.astype(bf16)
    lo = (r1 - mid.astype(f32)).astype(bf16)
    terms = jnp.concatenate([hi, mid, lo], axis=1)
    tri3 = jnp.broadcast_to(jnp.concatenate([_tri(rev)] * 3, axis=1).astype(bf16)[None], (nc, CH, 3 * CH))
    return lax.dot_general(tri3, terms, (((2,), (1,)), ((0,), (0,))), preferred_element_type=f32)


class _Tile:
    pass


def _tile_prep(q_ref, k_ref, lr_ref, w2_ref, gb_ref, rev, nc):
    t = _Tile()
    tg = nc * CH
    t.logits = _dot(lr_ref[...], w2_ref[...]) + gb_ref[...]
    c = _scan_chunks((_log_sigmoid(t.logits) * (1.0 / TAU)).reshape(nc, CH, KW), rev)
    cl = c[:, 0:1, :] if rev else c[:, CH - 1:CH, :]
    k = k_ref[...].astype(f32).reshape(nc, CH, KW)
    t.ec, t.enc, t.edec, t.ecl = jnp.exp(c), jnp.exp(-c), jnp.exp(cl - c), jnp.exp(cl)
    t.qd = q_ref[...].astype(f32).reshape(nc, CH, KW) * t.ec * QSCALE
    t.kd = k * t.enc
    t.kdec = k * t.edec
    hm = _head_masks()
    t.tri4 = jnp.concatenate([_tri(rev)] * NH, axis=0)[None]
    t.qs = jnp.concatenate([t.qd * hm[h] for h in range(NH)], axis=1)
    t.pst = _bnt(t.qs, t.kd) * t.tri4
    return t


def _gla_fwd(p, w2p, gb, s0, rev, tg, name):
    l = p.shape[0]
    nb, nc = l // tg, tg // CH

    def body(q_ref, k_ref, v_ref, lr_ref, w2_ref, gb_ref, s0_ref, o_ref, st_ref, st):
        @pl.when(pl.program_id(0) == 0)
        def _():
            st[...] = s0_ref[...]

        t = _tile_prep(q_ref, k_ref, lr_ref, w2_ref, gb_ref, rev, nc)
        v = v_ref[...].reshape(nc, CH, VW)
        intra = jnp.concatenate([_bdot(t.pst[:, h * CH:(h + 1) * CH], v[:, :, h * HV:(h + 1) * HV]) for h in range(NH)], axis=2)
        bd = _block_diag()
        s = st[...]
        for n in (range(nc - 1, -1, -1) if rev else range(nc)):
            st_ref[n] = s.astype(bf16)
            s = s * t.ecl[n] + bd * _tn(v[n], t.kdec[n])
        st[...] = s
        o_ref[...] = (_bnt(t.qd, st_ref[...]) + intra).reshape(tg, VW).astype(bf16)

    blk = (lambda i: nb - 1 - i) if rev else (lambda i: i)
    return pl.pallas_call(
        body, name=name, grid=(nb,),
        out_shape=(jax.ShapeDtypeStruct((l, VW), bf16), jax.ShapeDtypeStruct((l // CH, VW, KW), bf16)),
        in_specs=[pl.BlockSpec((tg, KW), lambda i: (blk(i), PQ // KW)), pl.BlockSpec((tg, KW), lambda i: (blk(i), PK // KW)),
                  pl.BlockSpec((tg, VW), lambda i: (blk(i), PV // VW)), pl.BlockSpec((tg, LRW), lambda i: (blk(i), PLR // LRW)),
                  pl.BlockSpec((LRW, KW), lambda i: (0, 0)), pl.BlockSpec((1, KW), lambda i: (0, 0)),
                  pl.BlockSpec((VW, KW), lambda i: (0, 0))],
        out_specs=(pl.BlockSpec((tg, VW), lambda i: (blk(i), 0)), pl.BlockSpec((nc, VW, KW), lambda i: (blk(i), 0, 0))),
        scratch_shapes=[pltpu.VMEM((VW, KW), f32)],
        compiler_params=_cp(("arbitrary",)),
    )(p, p, p, p, w2p, gb, s0)


def _gla_bwd(p, do, states, w2p, gb, prev, rev, tg, name, comm):
    l = p.shape[0]
    nb, nc = l // tg, tg // CH
    out_dt = bf16
    c_in, c_out, c_sems = comm.specs()

    def body(*refs):
        q_ref, k_ref, v_ref, lr_ref, do_ref, st_ref, w2_ref, gb_ref = refs[:8]
        refs = refs[8:]
        if prev is not None:
            pq_ref, pl_ref = refs[:2]
            refs = refs[2:]
        cin, refs = refs[:len(c_in)], refs[len(c_in):]
        dqkv_ref, dlr_ref, dw2_ref, dgb_ref, ds0_ref = refs[:5]
        cout, dst, ds_buf, sems = refs[5:5 + comm.n], refs[5 + comm.n], refs[6 + comm.n], refs[7 + comm.n:]
        comm.run(cin, cout, sems, nb, lambda: compute(q_ref, k_ref, v_ref, lr_ref, do_ref, st_ref, w2_ref, gb_ref,
                                                      pq_ref if prev is not None else None, pl_ref if prev is not None else None,
                                                      dqkv_ref, dlr_ref, dw2_ref, dgb_ref, ds0_ref, dst, ds_buf))

    def compute(q_ref, k_ref, v_ref, lr_ref, do_ref, st_ref, w2_ref, gb_ref, pq_ref, pl_ref,
                dqkv_ref, dlr_ref, dw2_ref, dgb_ref, ds0_ref, dst, ds_buf):
        @pl.when(pl.program_id(0) == 0)
        def _():
            dst[...] = jnp.zeros_like(dst)
            dw2_ref[...] = jnp.zeros_like(dw2_ref)
            dgb_ref[...] = jnp.zeros_like(dgb_ref)

        t = _tile_prep(q_ref, k_ref, lr_ref, w2_ref, gb_ref, rev, nc)
        hm = _head_masks()
        v = v_ref[...].reshape(nc, CH, VW)
        do = do_ref[...].reshape(nc, CH, VW)
        heads = lambda a, h: a[:, :, h * HV:(h + 1) * HV]
        dpst = jnp.concatenate([_bnt(heads(do, h), heads(v, h)) for h in range(NH)], axis=1) * t.tri4
        dv = jnp.concatenate([_btn(t.pst[:, h * CH:(h + 1) * CH], heads(do, h)) for h in range(NH)], axis=2)
        dqd = _bdot(do, st_ref[...])
        for h in range(NH):
            dqd = dqd + hm[h] * _bdot(dpst[:, h * CH:(h + 1) * CH], t.kd)
        dkd = _btn(dpst, t.qs)
        bd = _block_diag()
        d = dst[...]
        for n in (range(nc) if rev else range(nc - 1, -1, -1)):
            ds_buf[n] = d
            d = d * t.ecl[n] + bd * _tn(do[n], t.qd[n])
        dst[...] = d
        ds0_ref[...] = d
        ds = ds_buf[...]
        dv = dv + _bnt(t.kdec, ds)
        dkdec = _bdot(v, ds)
        dcl = jnp.sum(ds * st_ref[...].astype(f32), axis=1, keepdims=True) * t.ecl + jnp.sum(dkdec * t.kdec, axis=1, keepdims=True)
        dc = dqd * t.qd - dkd * t.kd - dkdec * t.kdec
        da = _scan_chunks(dc, not rev) + dcl
        dq = dqd * t.ec * QSCALE
        dk = dkd * t.enc + dkdec * t.edec
        dlog = da.reshape(tg, KW) * _sigmoid(-t.logits) * (1.0 / TAU)
        dlr = _nt(dlog, w2_ref[...])
        dw2_ref[...] += _tn(lr_ref[...], dlog)
        dgb_ref[...] += jnp.sum(dlog, axis=0, keepdims=True)
        dqkv = jnp.concatenate([dq, dk, dv], axis=2).reshape(tg, 2 * KW + VW)
        if prev is not None:
            dqkv = dqkv + pq_ref[...]
            dlr = dlr + pl_ref[...]
        dqkv_ref[...] = dqkv.astype(out_dt)
        dlr_ref[...] = dlr.astype(out_dt)

    blk = (lambda i: i) if rev else (lambda i: nb - 1 - i)
    in_specs = [pl.BlockSpec((tg, KW), lambda i: (blk(i), PQ // KW)), pl.BlockSpec((tg, KW), lambda i: (blk(i), PK // KW)),
                pl.BlockSpec((tg, VW), lambda i: (blk(i), PV // VW)), pl.BlockSpec((tg, LRW), lambda i: (blk(i), PLR // LRW)),
                pl.BlockSpec((tg, VW), lambda i: (blk(i), 0)), pl.BlockSpec((nc, VW, KW), lambda i: (blk(i), 0, 0)),
                pl.BlockSpec((LRW, KW), lambda i: (0, 0)), pl.BlockSpec((1, KW), lambda i: (0, 0))]
    args = [p, p, p, p, do, states, w2p, gb]
    if prev is not None:
        in_specs += [pl.BlockSpec((tg, 2 * KW + VW), lambda i: (blk(i), 0)), pl.BlockSpec((tg, LRW), lambda i: (blk(i), 0))]
        args += list(prev)
    return pl.pallas_call(
        body, name=name, grid=(nb,),
        out_shape=(jax.ShapeDtypeStruct((l, 2 * KW + VW), out_dt), jax.ShapeDtypeStruct((l, LRW), out_dt),
                   jax.ShapeDtypeStruct((LRW, KW), f32), jax.ShapeDtypeStruct((1, KW), f32), jax.ShapeDtypeStruct((VW, KW), f32))
        + tuple(comm.outs),
        in_specs=in_specs + c_in,
        out_specs=(pl.BlockSpec((tg, 2 * KW + VW), lambda i: (blk(i), 0)), pl.BlockSpec((tg, LRW), lambda i: (blk(i), 0)),
                   pl.BlockSpec((LRW, KW), lambda i: (0, 0)), pl.BlockSpec((1, KW), lambda i: (0, 0)),
                   pl.BlockSpec((VW, KW), lambda i: (0, 0))) + tuple(c_out),
        scratch_shapes=[pltpu.VMEM((VW, KW), f32), pltpu.VMEM((nc, VW, KW), f32)] + c_sems,
        compiler_params=_cp(("arbitrary",)),
    )(*args, *comm.ins)


def _ctx_hidden(ctx_ref, g_ref, sc_ref, sh_ref):
    xv = ctx_ref[...]
    r = lax.rsqrt(jnp.mean(xv * xv, axis=-1, keepdims=True) + EPS)
    xn = xv * r
    return xn, xn * (g_ref[...] * (1.0 + sc_ref[...])) + sh_ref[...]


_CTX_W_SPECS = [pl.BlockSpec((D, KW), lambda i: (0, PK // KW)), pl.BlockSpec((D, VW), lambda i: (0, PV // VW)),
                pl.BlockSpec((D, LRW), lambda i: (0, PLR // LRW))]


def _ctx_fwd(ctx, g, scale, shift, w_pad, w2f, w2b, gbf, gbb):
    ncc = CTX // CH

    def body(ctx_ref, g_ref, sc_ref, sh_ref, wk_ref, wv_ref, wl_ref, w2f_ref, w2b_ref, gbf_ref, gbb_ref, sf_ref, sb_ref):
        _, hc = _ctx_hidden(ctx_ref, g_ref, sc_ref, sh_ref)
        k, v, lr = _dot(hc, wk_ref[...]), _dot(hc, wv_ref[...]), _dot(hc, wl_ref[...])
        bd = _block_diag()
        for rev, w2_ref, gb_ref, out in ((False, w2f_ref, gbf_ref, sf_ref), (True, w2b_ref, gbb_ref, sb_ref)):
            tri = _tri(rev)
            st = jnp.zeros((VW, KW), f32)
            for j in (range(ncc - 1, -1, -1) if rev else range(ncc)):
                rows = slice(j * CH, (j + 1) * CH)
                _, c, cl = _decay(lr[rows], w2_ref[...], gb_ref[...], tri, rev)
                st = _state_fwd(k[rows], v[rows], c, cl, st, bd)
            out[...] = st

    vec = pl.BlockSpec((1, D), lambda i: (0, 0))
    w2s = pl.BlockSpec((LRW, KW), lambda i: (0, 0))
    gbs = pl.BlockSpec((1, KW), lambda i: (0, 0))
    sts = pl.BlockSpec((VW, KW), lambda i: (0, 0))
    return pl.pallas_call(
        body, name="ctx_fwd", grid=(1,), out_shape=(jax.ShapeDtypeStruct((VW, KW), f32),) * 2,
        in_specs=[pl.BlockSpec((CTX, D), lambda i: (0, 0)), vec, vec, vec] + _CTX_W_SPECS + [w2s, w2s, gbs, gbs],
        out_specs=(sts, sts), compiler_params=_cp(("arbitrary",)),
    )(ctx, g, scale, shift, w_pad, w_pad, w_pad, w2f, w2b, gbf, gbb)


def _ctx_bwd(ctx, g, scale, shift, w_pad, w2f, w2b, gbf, gbb, dsf, dsb):
    ncc = CTX // CH

    def body(ctx_ref, g_ref, sc_ref, sh_ref, wk_ref, wv_ref, wl_ref, w2f_ref, w2b_ref, gbf_ref, gbb_ref, dsf_ref, dsb_ref,
             dwk_ref, dwv_ref, dwl_ref, dmod_ref, dg_ref, dw2_ref, dgb_ref):
        xn, hc = _ctx_hidden(ctx_ref, g_ref, sc_ref, sh_ref)
        k, v, lr = _dot(hc, wk_ref[...]), _dot(hc, wv_ref[...]), _dot(hc, wl_ref[...])
        bd = _block_diag()
        dk_rows, dv_rows, dl_rows = [None] * ncc, [None] * ncc, [None] * ncc
        for d, (rev, w2_ref, gb_ref, ds_ref) in enumerate(((False, w2f_ref, gbf_ref, dsf_ref), (True, w2b_ref, gbb_ref, dsb_ref))):
            tri = _tri(rev)
            order = list(range(ncc - 1, -1, -1) if rev else range(ncc))
            st, saved = jnp.zeros((VW, KW), f32), {}
            for j in order:
                rows = slice(j * CH, (j + 1) * CH)
                logits, c, cl = _decay(lr[rows], w2_ref[...], gb_ref[...], tri, rev)
                saved[j] = (logits, c, cl, st)
                st = _state_fwd(k[rows], v[rows], c, cl, st, bd)
            dst = ds_ref[...]
            dw2 = jnp.zeros((LRW, KW), f32)
            dgb = jnp.zeros((1, KW), f32)
            for j in reversed(order):
                rows = slice(j * CH, (j + 1) * CH)
                logits, c, cl, st0 = saved[j]
                dk, dv, da, dst = _state_bwd(k[rows], v[rows], c, cl, st0, dst, _tri(not rev))
                dlog = da * _sigmoid(-logits) * (1.0 / TAU)
                dl = _nt(dlog, w2_ref[...])
                dw2 = dw2 + _tn(lr[rows], dlog)
                dgb = dgb + jnp.sum(dlog, axis=0, keepdims=True)
                dk_rows[j] = dk if dk_rows[j] is None else dk_rows[j] + dk
                dv_rows[j] = dv if dv_rows[j] is None else dv_rows[j] + dv
                dl_rows[j] = dl if dl_rows[j] is None else dl_rows[j] + dl
            dw2_ref[d] = dw2
            dgb_ref[d] = dgb
        dk, dv, dl = (jnp.concatenate(t, axis=0) for t in (dk_rows, dv_rows, dl_rows))
        dwk_ref[...] = _tn(hc, dk)
        dwv_ref[...] = _tn(hc, dv)
        dwl_ref[...] = _tn(hc, dl)
        dh = _nt(dk, wk_ref[...]) + _nt(dv, wv_ref[...]) + _nt(dl, wl_ref[...])
        gx = dh * xn
        dmod_ref[:, 0:D] = jnp.sum(dh, axis=0, keepdims=True)
        dmod_ref[:, D:2 * D] = jnp.sum(gx, axis=0, keepdims=True) * g_ref[...]
        dg_ref[...] = jnp.sum(gx, axis=0, keepdims=True) * (1.0 + sc_ref[...])

    vec = pl.BlockSpec((1, D), lambda i: (0, 0))
    w2s = pl.BlockSpec((LRW, KW), lambda i: (0, 0))
    gbs = pl.BlockSpec((1, KW), lambda i: (0, 0))
    sts = pl.BlockSpec((VW, KW), lambda i: (0, 0))
    full = lambda *s: pl.BlockSpec(s, lambda i: (0,) * len(s))
    return pl.pallas_call(
        body, name="ctx_bwd", grid=(1,),
        out_shape=(jax.ShapeDtypeStruct((D, KW), f32), jax.ShapeDtypeStruct((D, VW), f32), jax.ShapeDtypeStruct((D, LRW), f32),
                   jax.ShapeDtypeStruct((1, 2 * D), f32), jax.ShapeDtypeStruct((1, D), f32),
                   jax.ShapeDtypeStruct((2, LRW, KW), f32), jax.ShapeDtypeStruct((2, 1, KW), f32)),
        in_specs=[pl.BlockSpec((CTX, D), lambda i: (0, 0)), vec, vec, vec] + _CTX_W_SPECS + [w2s, w2s, gbs, gbs, sts, sts],
        out_specs=(full(D, KW), full(D, VW), full(D, LRW), full(1, 2 * D), full(1, D), full(2, LRW, KW), full(2, 1, KW)),
        compiler_params=_cp(("arbitrary",)),
    )(ctx, g, scale, shift, w_pad, w_pad, w_pad, w2f, w2b, gbf, gbb, dsf, dsb)


def _layernorm(va, g, b):
    mu = jnp.mean(va, axis=-1, keepdims=True)
    xc = va - mu
    rstd = lax.rsqrt(jnp.mean(xc * xc, axis=-1, keepdims=True) + EPS)
    vhat = xc * rstd
    return vhat, rstd, vhat * g + b


MIX_PIECES = 8


def _mix_fwd(p, ln_g, ln_b, ws, bs_t):
    l = p.shape[0]
    half = AW // 2
    rp = l // MIX_PIECES
    cpp = rp // ACH

    def body(p_ref, g_ref, b_ref, ws_ref, bs_ref, sv_hbm, va_buf, col_buf, sv_buf, in_sems, out_sems):
        piece = lambda i: pl.ds(pl.multiple_of(i * rp, rp), rp)
        load = lambda i: pltpu.make_async_copy(p_ref.at[piece(i), pl.ds(PVA, AW)], va_buf.at[piece(i)], in_sems.at[i])
        store = lambda i: pltpu.make_async_copy(sv_buf.at[piece(i)], sv_hbm.at[piece(i)], out_sems.at[i])
        for i in range(MIX_PIECES):
            load(i).start()

        def rows_piece(i, carry):
            load(i).wait()
            for j in range(cpp):
                rows = pl.ds(pl.multiple_of(i * rp + j * ACH, ACH), ACH)
                _, _, vn = _layernorm(va_buf[rows, :].astype(f32), g_ref[...], b_ref[...])
                for gi in range(2):
                    sl = slice(gi * ACH, (gi + 1) * ACH)
                    sv_buf[rows, sl] = (_dot(ws_ref[gi], vn[:, sl]) + bs_ref[:, gi:gi + 1]).astype(bf16)
                col_buf[0, rows, :] = vn[:, half:half + ACH]
                col_buf[1, rows, :] = vn[:, half + ACH:]
            return carry

        lax.fori_loop(0, MIX_PIECES, rows_piece, 0)

        def cols_step(cidx, carry):
            rows = pl.ds(cidx, ACH, stride=GW)
            for gi in range(2, 4):
                col_buf[gi - 2, rows, :] = _dot(ws_ref[gi], col_buf[gi - 2, rows, :]) + bs_ref[:, gi:gi + 1]
            return carry

        lax.fori_loop(0, GW, cols_step, 0, unroll=8)

        def out_piece(i, carry):
            for j in range(cpp):
                rows = pl.ds(pl.multiple_of(i * rp + j * ACH, ACH), ACH)
                sv_buf[rows, half:half + ACH] = col_buf[0, rows, :].astype(bf16)
                sv_buf[rows, half + ACH:] = col_buf[1, rows, :].astype(bf16)
            store(i).start()
            return carry

        lax.fori_loop(0, MIX_PIECES, out_piece, 0)
        for i in range(MIX_PIECES):
            store(i).wait()

    vm = pl.BlockSpec(memory_space=pltpu.VMEM)
    hbm = pl.BlockSpec(memory_space=pl.ANY)
    return pl.pallas_call(
        body, name="mix_fwd", out_shape=jax.ShapeDtypeStruct((l, AW), bf16),
        in_specs=[hbm, vm, vm, vm, vm], out_specs=hbm,
        scratch_shapes=[pltpu.VMEM((l, AW), bf16), pltpu.VMEM((2, l, ACH), f32), pltpu.VMEM((l, AW), bf16),
                        pltpu.SemaphoreType.DMA((MIX_PIECES,)), pltpu.SemaphoreType.DMA((MIX_PIECES,))],
        compiler_params=_cp(),
    )(p, ln_g, ln_b, ws, bs_t)


def _mix_bwd(p, dsv, ln_g, ln_b, ws_t):
    l = p.shape[0]
    half = AW // 2
    rp = l // MIX_PIECES
    cpp = rp // ACH

    def body(p_ref, dsv_hbm, g_ref, b_ref, wst_ref, dva_hbm, dws_ref, dbs_ref, dg_ref, db_ref,
             va_buf, dsv_buf, vn_col, ds_col, dva_buf, va_sems, ds_sems, out_sems):
        piece = lambda i: pl.ds(pl.multiple_of(i * rp, rp), rp)
        load_va = lambda i: pltpu.make_async_copy(p_ref.at[piece(i), pl.ds(PVA, AW)], va_buf.at[piece(i)], va_sems.at[i])
        load_ds = lambda i: pltpu.make_async_copy(dsv_hbm.at[piece(i)], dsv_buf.at[piece(i)], ds_sems.at[i])
        store = lambda i: pltpu.make_async_copy(dva_buf.at[piece(i)], dva_hbm.at[piece(i)], out_sems.at[i])
        for i in range(MIX_PIECES):
            load_va(i).start()
            load_ds(i).start()
        dws_ref[...] = jnp.zeros_like(dws_ref)
        dbs_ref[...] = jnp.zeros_like(dbs_ref)
        dg_ref[...] = jnp.zeros_like(dg_ref)
        db_ref[...] = jnp.zeros_like(db_ref)

        def rows_piece(i, carry):
            load_va(i).wait()
            load_ds(i).wait()
            for j in range(cpp):
                rows = pl.ds(pl.multiple_of(i * rp + j * ACH, ACH), ACH)
                _, _, vn = _layernorm(va_buf[rows, :].astype(f32), g_ref[...], b_ref[...])
                ds = dsv_buf[rows, :].astype(f32)
                for gi in range(2):
                    sl = slice(gi * ACH, (gi + 1) * ACH)
                    dws_ref[gi] += _nt(ds[:, sl], vn[:, sl])
                    dbs_ref[gi] += ds[:, sl]
                for gi in range(2):
                    sl = slice(half + gi * ACH, half + (gi + 1) * ACH)
                    vn_col[gi, rows, :] = vn[:, sl]
                    ds_col[gi, rows, :] = ds[:, sl]
            return carry

        lax.fori_loop(0, MIX_PIECES, rows_piece, 0)

        def cols_step(cidx, carry):
            rows = pl.ds(cidx, ACH, stride=GW)
            for gi in range(2, 4):
                ds = ds_col[gi - 2, rows, :]
                dws_ref[gi] += _nt(ds, vn_col[gi - 2, rows, :])
                dbs_ref[gi] += ds
                ds_col[gi - 2, rows, :] = _dot(wst_ref[gi], ds)
            return carry

        lax.fori_loop(0, GW, cols_step, 0, unroll=8)

        def out_piece(i, carry):
            for j in range(cpp):
                rows = pl.ds(pl.multiple_of(i * rp + j * ACH, ACH), ACH)
                vhat, rstd, _ = _layernorm(va_buf[rows, :].astype(f32), g_ref[...], b_ref[...])
                ds = dsv_buf[rows, :].astype(f32)
                dvn = jnp.concatenate([_dot(wst_ref[0], ds[:, 0:ACH]), _dot(wst_ref[1], ds[:, ACH:half]),
                                       ds_col[0, rows, :], ds_col[1, rows, :]], axis=1)
                dg_ref[...] += jnp.sum(dvn * vhat, axis=0, keepdims=True)
                db_ref[...] += jnp.sum(dvn, axis=0, keepdims=True)
                dvh = dvn * g_ref[...]
                dva = rstd * (dvh - jnp.mean(dvh, axis=-1, keepdims=True) - vhat * jnp.mean(dvh * vhat, axis=-1, keepdims=True))
                dva_buf[rows, :] = dva.astype(bf16)
            store(i).start()
            return carry

        lax.fori_loop(0, MIX_PIECES, out_piece, 0)
        for i in range(MIX_PIECES):
            store(i).wait()

    vm = pl.BlockSpec(memory_space=pltpu.VMEM)
    hbm = pl.BlockSpec(memory_space=pl.ANY)
    dma = lambda: pltpu.SemaphoreType.DMA((MIX_PIECES,))
    return pl.pallas_call(
        body, name="mix_bwd",
        out_shape=(jax.ShapeDtypeStruct((l, AW), bf16), jax.ShapeDtypeStruct((4, ACH, ACH), f32), jax.ShapeDtypeStruct((4, ACH, ACH), f32),
                   jax.ShapeDtypeStruct((1, AW), f32), jax.ShapeDtypeStruct((1, AW), f32)),
        in_specs=[hbm, hbm, vm, vm, vm], out_specs=(hbm, vm, vm, vm, vm),
        scratch_shapes=[pltpu.VMEM((l, AW), bf16), pltpu.VMEM((l, AW), bf16), pltpu.VMEM((2, l, ACH), f32), pltpu.VMEM((2, l, ACH), f32),
                        pltpu.VMEM((l, AW), bf16), dma(), dma(), dma()],
        compiler_params=_cp(),
    )(p, dsv, ln_g, ln_b, ws_t)


def _mid(x, tgt, p, o_f, o_b, sv, gate, gf, gb_norm, w_pa, w_pb, w_out, tl):
    l = x.shape[0]

    def body(x_ref, t_ref, zb_ref, ua_ref, za_ref, g1_ref, g2_ref, of_ref, ob_ref, sv_ref, gate_ref, gf_ref, gbn_ref,
             wpa_ref, wpb_ref, wout_ref,
             dx1_ref, dzbua_ref, dzag_ref, dsv_ref, do_ref, dwout_bf, dwpa_bf, dwpb_bf, dgf_ref, dgate_ref, dgbn_ref, loss_ref,
             dwout_ref, dwpa_ref, dwpb_ref):
        @pl.when(pl.program_id(0) == 0)
        def _():
            for r in (dwout_ref, dwpa_ref, dwpb_ref, dgf_ref, dgate_ref, dgbn_ref, loss_ref):
                r[...] = jnp.zeros_like(r)

        o = of_ref[...].astype(f32) + ob_ref[...].astype(f32)
        rr = jnp.concatenate(
            [jnp.broadcast_to(lax.rsqrt(jnp.mean(o[:, h * HV:(h + 1) * HV] ** 2, axis=-1, keepdims=True) + EPS), (tl, HV))
             for h in range(NH)], axis=1)
        ohat = o * rr
        on = ohat * gbn_ref[...]
        szb, dszb = _silu_and_grad(zb_ref[...].astype(f32))
        tb = on * szb
        u = ua_ref[...].astype(f32)
        svv = sv_ref[...].astype(f32)
        sza, dsza = _silu_and_grad(za_ref[...].astype(f32))
        ta = u * svv * sza
        ya = _dot(ta, wpa_ref[...])
        yb = _dot(tb, wpb_ref[...])
        g1 = _sigmoid(g1_ref[...].astype(f32))
        g2 = _sigmoid(g2_ref[...].astype(f32))
        m = g1 * ya + g2 * yb
        y2 = _dot(m, wout_ref[...])
        x1 = x_ref[...] + gate_ref[...] * y2
        r1 = lax.rsqrt(jnp.mean(x1 * x1, axis=-1, keepdims=True) + EPS)
        x1n = x1 * r1
        err = x1n * gf_ref[...] - t_ref[...]
        loss_ref[...] += jnp.sum(jnp.sum(err * err, axis=-1, keepdims=True), axis=0, keepdims=True) * (0.5 / D)
        dout = err * (1.0 / D)
        dgf_ref[...] += jnp.sum(dout * x1n, axis=0, keepdims=True)
        dx1n = dout * gf_ref[...]
        dx1 = r1 * (dx1n - x1n * jnp.mean(dx1n * x1n, axis=-1, keepdims=True))
        dx1_ref[...] = dx1
        dgate_ref[...] += jnp.sum(dx1 * y2, axis=0, keepdims=True)
        dy2 = dx1 * gate_ref[...]
        dwout_ref[...] += _tn(m, dy2)
        dm = _nt(dy2, wout_ref[...])
        dya = dm * g1
        dyb = dm * g2
        dzag_ref[:, AW:AW + D] = (dm * ya * g1 * (1.0 - g1)).astype(bf16)
        dzag_ref[:, AW + D:] = (dm * yb * g2 * (1.0 - g2)).astype(bf16)
        dwpa_ref[...] += _tn(ta, dya)
        dta = _nt(dya, wpa_ref[...])
        dzbua_ref[:, AW:] = (dta * svv * sza).astype(bf16)
        dsv_ref[...] = (dta * u * sza).astype(bf16)
        dzag_ref[:, 0:AW] = (dta * u * svv * dsza).astype(bf16)
        dwpb_ref[...] += _tn(tb, dyb)
        dtb = _nt(dyb, wpb_ref[...])
        don = dtb * szb
        dzbua_ref[:, 0:AW] = (dtb * on * dszb).astype(bf16)
        dgbn_ref[...] += jnp.sum(don * ohat, axis=0, keepdims=True)
        doh = don * gbn_ref[...]
        prod = doh * ohat
        mh = jnp.concatenate(
            [jnp.broadcast_to(jnp.mean(prod[:, h * HV:(h + 1) * HV], axis=-1, keepdims=True), (tl, HV)) for h in range(NH)], axis=1)
        do_ref[...] = (rr * (doh - ohat * mh)).astype(bf16)

        @pl.when(pl.program_id(0) == l // tl - 1)
        def _():
            for acc, out in ((dwout_ref, dwout_bf), (dwpa_ref, dwpa_bf), (dwpb_ref, dwpb_bf)):
                out[...] = acc[...].astype(bf16)

    row = lambda w, j: pl.BlockSpec((tl, w), lambda i, j=j: (i, j))
    full = lambda *s: pl.BlockSpec(s, lambda i: (0,) * len(s))
    return pl.pallas_call(
        body, name="mid", grid=(l // tl,),
        out_shape=(jax.ShapeDtypeStruct((l, D), f32), jax.ShapeDtypeStruct((l, 2 * AW), bf16), jax.ShapeDtypeStruct((l, AW + 2 * D), bf16),
                   jax.ShapeDtypeStruct((l, AW), bf16), jax.ShapeDtypeStruct((l, VW), bf16),
                   jax.ShapeDtypeStruct((D, D), bf16), jax.ShapeDtypeStruct((AW, D), bf16), jax.ShapeDtypeStruct((VW, D), bf16),
                   jax.ShapeDtypeStruct((1, D), f32), jax.ShapeDtypeStruct((1, D), f32), jax.ShapeDtypeStruct((1, VW), f32),
                   jax.ShapeDtypeStruct((1, 1), f32)),
        scratch_shapes=[pltpu.VMEM((D, D), f32), pltpu.VMEM((AW, D), f32), pltpu.VMEM((VW, D), f32)],
        in_specs=[row(D, 0), row(D, 0), row(AW, PZB // AW), row(AW, PUA // AW), row(AW, PZA // AW), row(D, PG1 // D), row(D, PG2 // D),
                  row(VW, 0), row(VW, 0), row(AW, 0), full(1, D), full(1, D), full(1, VW), full(AW, D), full(VW, D), full(D, D)],
        out_specs=(row(D, 0), row(2 * AW, 0), row(AW + 2 * D, 0), row(AW, 0), row(VW, 0),
                   full(D, D), full(AW, D), full(VW, D), full(1, D), full(1, D), full(1, VW), full(1, 1)),
        compiler_params=_cp(("arbitrary",)),
    )(x, tgt, p, p, p, p, p, o_f, o_b, sv, gate, gf, gb_norm, w_pa, w_pb, w_out)


def _in_bwd(x, dx1, dqkv, dzbua, dva, dzag, dlr, w_pad, g, scale, tl, comm):
    l = x.shape[0]
    c_in, c_out, c_sems = comm.specs()

    def body(*refs):
        cin = refs[14:14 + len(c_in)]
        outs = refs[14 + len(c_in):]
        comm.run(cin, outs[4:4 + comm.n], outs[4 + comm.n:], l // tl, lambda: compute(*refs[:14], *outs[:4]))

    def compute(x_ref, dx1_ref, a_ref, b_ref, c_ref, e_ref, lr_ref, wa_ref, wb_ref, wc_ref, we_ref, wl_ref, g_ref, sc_ref,
                gx_ref, dsh_ref, dsc_ref, dg_ref):
        @pl.when(pl.program_id(0) == 0)
        def _():
            for r in (dsh_ref, dsc_ref, dg_ref):
                r[...] = jnp.zeros_like(r)

        dh = (_nt(a_ref[...], wa_ref[...]) + _nt(b_ref[...], wb_ref[...]) + _nt(c_ref[...], wc_ref[...]) + _nt(e_ref[...], we_ref[...])
              + _nt(lr_ref[...], wl_ref[...]))
        xv = x_ref[...]
        r = lax.rsqrt(jnp.mean(xv * xv, axis=-1, keepdims=True) + EPS)
        xn = xv * r
        gxn = jnp.sum(dh * xn, axis=0, keepdims=True)
        dsh_ref[...] += jnp.sum(dh, axis=0, keepdims=True)
        dsc_ref[...] += gxn * g_ref[...]
        dg_ref[...] += gxn * (1.0 + sc_ref[...])
        dxn = dh * (g_ref[...] * (1.0 + sc_ref[...]))
        gx_ref[...] = dx1_ref[...] + r * (dxn - xn * jnp.mean(dxn * xn, axis=-1, keepdims=True))

    row = lambda w: pl.BlockSpec((tl, w), lambda i: (i, 0))
    wcol = lambda w, j: pl.BlockSpec((D, w), lambda i, j=j: (0, j))
    vec = pl.BlockSpec((1, D), lambda i: (0, 0))
    return pl.pallas_call(
        body, name="in_bwd", grid=(l // tl,),
        out_shape=(jax.ShapeDtypeStruct((l, D), f32),) + (jax.ShapeDtypeStruct((1, D), f32),) * 3 + tuple(comm.outs),
        in_specs=[row(D), row(D), row(2 * KW + VW), row(2 * AW), row(AW), row(AW + 2 * D), row(LRW),
                  wcol(2 * KW + VW, 0), wcol(2 * AW, PZB // (2 * AW)), wcol(AW, PVA // AW), wcol(AW + 2 * D, PZA // (AW + 2 * D)),
                  wcol(LRW, PLR // LRW), vec, vec] + c_in,
        out_specs=(row(D), vec, vec, vec) + tuple(c_out), scratch_shapes=c_sems,
        compiler_params=_cp(("arbitrary",)),
    )(x, dx1, dqkv, dzbua, dva, dzag, dlr, w_pad, w_pad, w_pad, w_pad, w_pad, g, scale, *comm.ins)


def _tn_matmul(a, bs, tl, name, comm=None, pack=None):
    l, m = a.shape
    k = len(bs)
    comm = comm or _Comm([], [], None)
    c_in, c_out, c_sems = comm.specs()
    acc_shapes = [(m, b.shape[1]) for b in bs]
    n_extra = len(pack.extra) if pack else 0
    n_res = len(pack.outs) if pack else k

    def body(a_ref, *refs):
        b_refs, refs = refs[:k], refs[k:]
        extra, refs = refs[:n_extra], refs[n_extra:]
        cin, refs = refs[:len(c_in)], refs[len(c_in):]
        res, refs = refs[:n_res], refs[n_res:]
        cout, refs = refs[:comm.n], refs[comm.n:]
        accs, sems = (refs[:k], refs[k:]) if pack else (res, refs)

        def compute():
            @pl.when(pl.program_id(0) == 0)
            def _():
                for o_ref in accs:
                    o_ref[...] = jnp.zeros_like(o_ref)

            av = a_ref[...]
            for b_ref, o_ref in zip(b_refs, accs):
                o_ref[...] += _tn(av, b_ref[...])
            if pack:
                pl.when(pl.program_id(0) == l // tl - 1)(lambda: pack.fn(accs, extra, res))

        comm.run(cin, cout, sems, l // tl, compute)

    full = lambda shape: pl.BlockSpec(shape, lambda i: (0,) * len(shape))
    res_shapes = list(pack.outs) if pack else [jax.ShapeDtypeStruct(sh, f32) for sh in acc_shapes]
    return pl.pallas_call(
        body, name=name, grid=(l // tl,), out_shape=tuple(res_shapes) + tuple(comm.outs),
        in_specs=[pl.BlockSpec((tl, m), lambda i: (i, 0))] + [pl.BlockSpec((tl, b.shape[1]), lambda i: (i, 0)) for b in bs]
        + [full(e.shape) for e in (pack.extra if pack else [])] + c_in,
        out_specs=tuple(full(r.shape) for r in res_shapes) + tuple(c_out),
        scratch_shapes=([pltpu.VMEM(sh, f32) for sh in acc_shapes] if pack else []) + c_sems, compiler_params=_cp(("arbitrary",)),
    )(a, *bs, *(pack.extra if pack else []), *comm.ins)


def _pad_gate(w2, gb):
    z = jnp.zeros((RANK, KW), f32)
    tail = jnp.zeros((LRW - 2 * RANK, KW), f32)
    w2f = jnp.concatenate([w2[0], z, tail], axis=0)
    w2b = jnp.concatenate([z, w2[1], tail], axis=0)
    return w2f, w2b, gb[0:1], gb[1:2]


EARLY_A_ROWS = 512


class _NoExchange:
    def __init__(self, w_pa, w_pb, w_out):
        self.weights = (w_pa, w_pb, w_out)

    def gather_proj(self):
        return _Comm([], [], None)

    def proj_weights(self, got):
        return self.weights

    def first(self, dw_out, dw_pa, dw_pb, small):
        return _Comm([], [], None)

    def early_a(self, blocks):
        return _Comm([], [], None)

    def early_b(self, blocks):
        return _Comm([], [], None)

    def late(self, blocks, dgt):
        return _Comm([], [], None)


class _Exchanges:
    def __init__(self, pa, pb, wo):
        self.shards = (pa, pb, wo)

    def gather_proj(self):
        def plan(i, o):
            srcs = [lambda j, r=r: r for r in i]
            dsts = [lambda j: o[0].at[:, _lanes(j)], lambda j: o[1].at[:, _lanes(j)], lambda j: o[2].at[j]]
            return srcs, dsts, None
        sds = jax.ShapeDtypeStruct
        return _Comm(self.shards, [sds((AW, D), bf16), sds((VW, D), bf16), sds((NDEV, 128, D), bf16)], plan)

    def proj_weights(self, got):
        return got[0], got[1], got[2].reshape(D, D)

    def first(self, dw_out, dw_pa, dw_pb, small):
        def plan(i, o):
            srcs = [lambda j: i[0].at[j], lambda j: i[1].at[:, _lanes(j)], lambda j: i[2].at[:, _lanes(j)], lambda j: i[3]]
            dsts = [lambda j, r=r: r.at[j] for r in o]
            return srcs, dsts, None
        sds = jax.ShapeDtypeStruct
        return _Comm([dw_out.reshape(NDEV, 128, D), dw_pa, dw_pb, small],
                     [sds((NDEV, 128, D), bf16), sds((NDEV, AW, 128), bf16), sds((NDEV, VW, 128), bf16),
                      sds((NDEV,) + small.shape, f32)], plan)

    def early_a(self, blocks):
        def plan(i, o):
            return [lambda j: i[0].at[j]], [lambda j: o[0].at[j]], [lambda j: j >= LATE_DESTS - 1]
        return _Comm([blocks], [jax.ShapeDtypeStruct(blocks.shape, bf16)], plan)

    def early_b(self, blocks):
        def plan(i, o):
            return [lambda j: i[0].at[j]], [lambda j: o[0].at[j]], [lambda j: j >= LATE_DESTS - 1]
        return _Comm([blocks], [jax.ShapeDtypeStruct(blocks.shape, bf16)], plan)

    def late(self, blocks, dgt):
        return _LateComm(blocks, dgt)


def _local_step(x, ctx, tgt, mod, modc, norm_g, w_pad, ln_g, ln_b, ws, bs, w2, gb, gb_norm, gf, xch):
    shift, scale, gate = mod[:, 0:D], mod[:, D:2 * D], mod[:, 2 * D:]
    shift_c, scale_c = modc[:, 0:D], modc[:, D:]
    w2f, w2b, gbf, gbb = _pad_gate(w2, gb)

    p, h, *got_proj = _in_proj(x, norm_g, scale, shift, w_pad, 512, xch.gather_proj())
    w_pa, w_pb, w_out = xch.proj_weights(got_proj)
    sc_f, sc_b = _ctx_fwd(ctx, norm_g, scale_c, shift_c, w_pad, w2f, w2b, gbf, gbb)
    o_f, st_f = _gla_fwd(p, w2f, gbf, sc_f, False, 512, "gla_fwd_f")
    o_b, st_b = _gla_fwd(p, w2b, gbb, sc_b, True, 512, "gla_fwd_b")
    sv = _mix_fwd(p, ln_g, ln_b, ws.astype(bf16), bs.T)
    (dx1, dzbua, dzag, dsv, do, dw_out, dw_pa, dw_pb, dgf, dgate, dgbn, loss) = _mid(
        x, tgt, p, o_f, o_b, sv, gate, gf, gb_norm, w_pa, w_pb, w_out, 256)
    dva, dws, dbs_acc, dln_g, dln_b = _mix_bwd(p, dsv, ln_g, ln_b, jnp.swapaxes(ws, 1, 2).astype(bf16))
    small = _rows128(dln_g, dln_b, dws, jnp.sum(dbs_acc, axis=-1), dgbn, dgf, jnp.broadcast_to(loss, (1, 128)))
    blocks_a, blocks_b, *got_first = _tn_matmul(h, [dzbua, dva, dzag], 1024, "dw_early", xch.first(dw_out, dw_pa, dw_pb, small),
                                               _pack_early())

    dqkv_f, dlr_f, dw2f, dgbf, dsc_f, *got_a = _gla_bwd(p, do, st_f, w2f, gbf, None, False, 512, "gla_bwd_f",
                                                        xch.early_a(blocks_a))
    dqkv, dlr, dw2b, dgbb, dsc_b, *got_b = _gla_bwd(p, do, st_b, w2b, gbb, (dqkv_f, dlr_f), True, 512, "gla_bwd_b",
                                                    xch.early_b(blocks_b))
    dwk_c, dwv_c, dwl_c, dmodc, dg_c, dw2c, dgbc = _ctx_bwd(ctx, norm_g, scale_c, shift_c, w_pad, w2f, w2b, gbf, gbb, dsc_f, dsc_b)
    (blocks_late,) = _tn_matmul(h, [dqkv, dlr], 1024, "dw_qkv_lr", pack=_pack_late(dwk_c, dwv_c, dwl_c))
    dw2 = jnp.stack([dw2f[0:RANK] + dw2c[0, 0:RANK], dw2b[RANK:2 * RANK] + dw2c[1, RANK:2 * RANK]])
    dgb = jnp.concatenate([dgbf + dgbc[0], dgbb + dgbc[1]], axis=0)
    dgt = jnp.concatenate([jnp.transpose(dw2.reshape(2, RANK, NDEV, 32), (2, 0, 1, 3)).reshape(NDEV, 2 * RANK * 32),
                           jnp.transpose(dgb.reshape(2, NDEV, 32), (1, 0, 2)).reshape(NDEV, 64),
                           jnp.zeros((NDEV, 64), f32)], axis=1).reshape(NDEV, 9, 128)
    gx, dshift, dscale, dg, *got_late = _in_bwd(x, dx1, dqkv, dzbua, dva, dzag, dlr, w_pad, norm_g, scale, 512,
                                                xch.late(blocks_late, dgt))
    return dict(loss=loss, gx=gx, dmod=jnp.concatenate([dshift, dscale, dgate], axis=1), dmodc=dmodc, dnorm_g=dg + dg_c,
                small=small, blocks_a=blocks_a, blocks_b=blocks_b, blocks_late=blocks_late, dw2=dw2, dgb=dgb,
                dw_pa=dw_pa, dw_pb=dw_pb, dw_out=dw_out, got_first=got_first, got_a=got_a, got_b=got_b, got_late=got_late)


def _rows128(*vs):
    out = []
    for t in vs:
        t = t.reshape(-1)
        pad = (-t.shape[0]) % 128
        out.append(jnp.pad(t, (0, pad)) if pad else t)
    return jnp.concatenate(out).reshape(-1, 128)


def kernel(x, c, ctx, c_ctx, w_mod, b_mod, norm_g, w_in, a_ln_g, a_ln_b, a_ws, a_bs, b_gate_w2, b_gate_b, b_norm_g, w_proj_a, w_proj_b, w_out, final_norm_g, loss_target, m_c_ctx, m_w_mod, m_b_mod, m_norm_g, m_w_in, m_a_ln_g, m_a_ln_b, m_a_ws, m_a_bs, m_b_gate_w2, m_b_gate_b, m_b_norm_g, m_w_proj_a, m_w_proj_b, m_w_out, m_final_norm_g, v_c_ctx, v_w_mod, v_b_mod, v_norm_g, v_w_in, v_a_ln_g, v_a_ln_b, v_a_ws, v_a_bs, v_b_gate_w2, v_b_gate_b, v_b_norm_g, v_w_proj_a, v_w_proj_b, v_w_out, v_final_norm_g):
    me = _me()
    ncol = w_mod.shape[2]

    gate_mine = _rows128(jnp.concatenate([b_gate_w2.reshape(-1), b_gate_b.reshape(-1)]))
    bm_mine = lax.dynamic_slice(b_mod, (0, me * ncol), (1, ncol))
    cs, mods, wg, gates = _gather_first(c, c_ctx.reshape(1, D), w_mod[0], bm_mine, w_in[0].astype(bf16), gate_mine)
    w_pad = _repack_w(wg)
    gflat = gates.reshape(NDEV, 9 * 128)
    w2 = jnp.transpose(gflat[:, 0:2 * RANK * 32].reshape(NDEV, 2, RANK, 32), (1, 2, 0, 3)).reshape(2, RANK, KW)
    gb = jnp.transpose(gflat[:, 2 * RANK * 32:2 * RANK * 32 + 64].reshape(NDEV, 2, 32), (1, 0, 2)).reshape(2, KW)

    mods = jnp.transpose(mods, (1, 0, 2)).reshape(16, 3 * D)
    mod = lax.dynamic_slice(mods, (me, 0), (1, 3 * D))
    modc = mods[8:9, 0:2 * D]

    xch = _Exchanges(w_proj_a[0].astype(bf16), w_proj_b[0].astype(bf16), w_out[0].astype(bf16))
    r = _local_step(x[0], ctx[0], loss_target[0], mod, modc, norm_g, w_pad, a_ln_g, a_ln_b, a_ws[0], a_bs[0], w2, gb,
                    b_norm_g, final_norm_g.reshape(1, D), xch)
    p_out, p_pa, p_pb, smalls_e = r["got_first"]
    (p_in_a,) = r["got_a"]
    (p_in_b,) = r["got_b"]
    _, _, p_in_late, p_gt = r["got_late"]

    n_e = (AW + AW + 4 * ACH * ACH + AW + VW + D) // 128
    row = lambda t: t.reshape(1, D)
    rep_e = _adam_params(smalls_e, [a_ln_g, a_ln_b, a_ws, a_bs, b_norm_g, row(final_norm_g)],
                         [m_a_ln_g, m_a_ln_b, m_a_ws, m_a_bs, m_b_norm_g, row(m_final_norm_g)],
                         [v_a_ln_g, v_a_ln_b, v_a_ws, v_a_bs, v_b_norm_g, row(v_final_norm_g)], "adam_rep_early")
    rep_e = [t[0:5] + (t[5].reshape(D),) for t in rep_e]
    losses = smalls_e[:, n_e, 0]
    loss = losses[0]
    for i in range(1, NDEV):
        loss = loss + losses[i]

    dbm = r["dmod"] + jnp.concatenate([r["dmodc"], jnp.zeros((1, D), f32)], axis=1)
    mod_grads = _mod_tail(r["dnorm_g"], dbm, r["dmod"], r["dmodc"], cs, w_mod)
    mod_res = _adam_plain(mod_grads, [w_mod, norm_g, b_mod, row(c_ctx)], [m_w_mod, m_norm_g, m_b_mod, row(m_c_ctx)],
                          [v_w_mod, v_norm_g, v_b_mod, row(v_c_ctx)], "adam_mod")
    wm, ng, bmod_r, cc = ([t[i] for t in mod_res] for i in range(4))

    a_in = _adam_w_in(p_in_a, p_in_b, p_in_late, w_in, m_w_in, v_w_in)
    blk = _adam_blocks([p_pa, p_pb, p_out], [w_proj_a, w_proj_b, w_out], [m_w_proj_a, m_w_proj_b, m_w_out],
                       [v_w_proj_a, v_w_proj_b, v_w_out], "adam_proj_out")
    a_pa, a_pb, a_out = ([t[i] for t in blk] for i in range(3))
    a_gt = _adam_gate(p_gt, b_gate_w2, b_gate_b, m_b_gate_w2, m_b_gate_b, v_b_gate_w2, v_b_gate_b)
    sh = [(a_in[k], a_pa[k], a_pb[k], a_out[k], a_gt[k], a_gt[4 + k]) for k in range(4)]

    outs = [loss, r["gx"][None]]
    for k in range(4):
        lg, lb, aws, abs_, bng, fng = rep_e[k]
        n_g, bmod = ng[k], bmod_r[k]
        s_in, s_pa, s_pb, s_out, s_w2, s_gb = sh[k]
        outs += [cc[k].reshape(D), wm[k], bmod, n_g, s_in, lg, lb, aws, abs_, s_w2, s_gb, bng, s_pa, s_pb, s_out, fng]
    return tuple(outs)
```

```python
import jax
import jax.numpy as jnp
from jax import lax
from jax.experimental import pallas as pl
from jax.experimental.pallas import tpu as pltpu

f32, bf16 = jnp.float32, jnp.bfloat16

D = 1024
CTX = 256
EPS = 1e-6
AW = 512
ACH = 128
GW = 64
KW = 256
VW = 512
NH = 4
HK = 64
HV = 128
RANK = 16
TAU = 16.0
CH = 64
QSCALE = HK ** -0.5
INW = 5152
NDEV = 8

PQ, PK, PV, PZB, PUA, PVA, PZA, PG1, PG2, PLR, PW = 0, 256, 512, 1024, 1536, 2048, 2560, 3072, 4096, 5120, 5248
LRW = 128

ADAM_LR, ADAM_B1, ADAM_B2, ADAM_EPS, ADAM_WD, ADAM_STEP = 0.001, 0.9, 0.999, 1e-08, 0.01, 10

VMEM_LIMIT = 56 * 1024 * 1024
MESH = pl.DeviceIdType.MESH


def _cp(sem=None):
    return pltpu.CompilerParams(dimension_semantics=sem, vmem_limit_bytes=VMEM_LIMIT)


def _dot(a, b):
    return jnp.dot(a.astype(bf16), b.astype(bf16), preferred_element_type=f32)


def _nt(a, b):
    return lax.dot_general(a.astype(bf16), b.astype(bf16), (((1,), (1,)), ((), ())), preferred_element_type=f32)


def _tn(a, b):
    return lax.dot_general(a.astype(bf16), b.astype(bf16), (((0,), (0,)), ((), ())), preferred_element_type=f32)


def _dot_hi(a, b):
    return jnp.dot(a, b, preferred_element_type=f32, precision=lax.Precision.HIGHEST)


def _sigmoid(x):
    return 1.0 / (1.0 + jnp.exp(-x))


def _log_sigmoid(x):
    return jnp.minimum(x, 0.0) - jnp.log(1.0 + jnp.exp(-jnp.abs(x)))


def _silu_and_grad(z):
    s = _sigmoid(z)
    return z * s, s * (1.0 + z * (1.0 - s))


def _me():
    return 4 * lax.axis_index("x") + 2 * lax.axis_index("y") + lax.axis_index("c")


def _peer(k):
    x, y, c = lax.axis_index("x"), lax.axis_index("y"), lax.axis_index("c")
    px = 1 - x if k & 4 else x
    py = 1 - y if k & 2 else y
    pc = 1 - c if k & 1 else c
    return (px, py, pc), 4 * px + 2 * py + pc


def _fanout(srcs, dsts, send_sems, recv_sems, local_sems, owners=None):
    me = _me()
    n = len(srcs)
    owns = lambda a, j: True if owners is None or owners[a] is None else owners[a](j)

    def guarded(cond, fn):
        if cond is True:
            fn()
        else:
            pl.when(cond)(fn)

    def copies(with_recvs):
        local = [pltpu.make_async_copy(srcs[a](me), dsts[a](me), local_sems.at[a]) for a in range(n)]
        sends, recvs = [], []
        for k in range(1, NDEV):
            dev, idx = _peer(k)
            for a in range(n):
                s = (k - 1) * n + a
                sends.append((owns(a, idx), pltpu.make_async_remote_copy(
                    src_ref=srcs[a](idx), dst_ref=dsts[a](me), send_sem=send_sems.at[s], recv_sem=recv_sems.at[s],
                    device_id=dev, device_id_type=MESH)))
                if with_recvs:
                    recvs.append((owns(a, me), pltpu.make_async_remote_copy(
                        src_ref=srcs[a](idx), dst_ref=dsts[a](idx), send_sem=send_sems.at[s], recv_sem=recv_sems.at[s],
                        device_id=dev, device_id_type=MESH)))
        return local, sends, recvs

    def start():
        local, sends, _ = copies(False)
        for a, cp in enumerate(local):
            guarded(owns(a, me), cp.start)
        for cond, cp in sends:
            guarded(cond, cp.start)

    def finish():
        local, sends, recvs = copies(True)
        for cond, cp in recvs:
            guarded(cond, cp.wait_recv)
        for cond, cp in sends:
            guarded(cond, cp.wait_send)
        for a, cp in enumerate(local):
            guarded(owns(a, me), cp.wait)

    return start, finish


class _Comm:
    def __init__(self, ins, outs, plan):
        self.ins, self.outs, self.plan = list(ins), list(outs), plan
        self.n = len(self.outs)

    def specs(self):
        hbm = pl.BlockSpec(memory_space=pl.ANY)
        return [hbm] * len(self.ins), [hbm] * self.n, _fanout_sems(self.n) if self.n else []

    def run(self, in_refs, out_refs, sems, nsteps, compute):
        if not self.n:
            compute()
            return

        def hooks():
            srcs, dsts, owners = self.plan(in_refs, out_refs)
            return _fanout(srcs, dsts, sems[0], sems[1], sems[2], owners)

        pl.when(pl.program_id(0) == 0)(lambda: hooks()[0]())
        compute()
        pl.when(pl.program_id(0) == nsteps - 1)(lambda: hooks()[1]())


LATE_MID_STEP = 1


class _LateComm:
    def __init__(self, blocks, dgt):
        sds = jax.ShapeDtypeStruct
        self.ins = [blocks, dgt]
        self.outs = [sds((D, SHARD), bf16), sds((D, SHARD), bf16), sds((4, D, SHARD), bf16), sds(dgt.shape, f32)]
        self.n = len(self.outs)

    def specs(self):
        hbm = pl.BlockSpec(memory_space=pl.ANY)
        scratch = [pltpu.VMEM((3, D, SHARD), bf16), pltpu.SemaphoreType.DMA((2,)), pltpu.SemaphoreType.DMA((4,)),
                   pltpu.SemaphoreType.DMA((3,))] + _fanout_sems(1)
        return [hbm] * 2, [hbm] * self.n, scratch

    def run(self, in_refs, out_refs, scratch, nsteps, compute):
        late_ref, dgt_ref = in_refs
        sib_ref, pair_ref, parts_ref, gt_ref = out_refs
        vbuf, send_sems, recv_sems, local_sems, g_send, g_recv, g_local = scratch
        x, y, c = lax.axis_index("x"), lax.axis_index("y"), lax.axis_index("c")
        chip = 2 * x + y
        is_owner_chip = chip == 0
        step = pl.program_id(0)

        def to_sibling():
            return pltpu.make_async_remote_copy(src_ref=late_ref.at[1 - c], dst_ref=sib_ref, send_sem=send_sems.at[0],
                                                recv_sem=recv_sems.at[0], device_id=(x, y, 1 - c), device_id_type=MESH)

        def to_owner(k):
            return pltpu.make_async_remote_copy(src_ref=pair_ref, dst_ref=parts_ref.at[k], send_sem=send_sems.at[1],
                                                recv_sem=recv_sems.at[k], device_id=(0, 0, c), device_id_type=MESH)

        def own_copy():
            return pltpu.make_async_copy(pair_ref, parts_ref.at[0], local_sems.at[2])

        def gates():
            return _fanout([lambda j: dgt_ref.at[j]], [lambda j: gt_ref.at[j]], g_send, g_recv, g_local)

        @pl.when(step == 0)
        def _():
            to_sibling().start()
            gates()[0]()

        compute()

        @pl.when(step == LATE_MID_STEP)
        def _():
            mine = pltpu.make_async_copy(late_ref.at[c], vbuf.at[0], local_sems.at[0])
            mine.start()
            to_sibling().wait_recv()
            theirs = pltpu.make_async_copy(sib_ref, vbuf.at[1], local_sems.at[1])
            theirs.start()
            mine.wait()
            theirs.wait()
            vbuf[2] = (vbuf[0].astype(f32) + vbuf[1].astype(f32)).astype(bf16)
            pltpu.sync_copy(vbuf.at[2], pair_ref)
            pl.when(is_owner_chip)(lambda: own_copy().start())
            pl.when(jnp.logical_not(is_owner_chip))(lambda: to_owner(chip).start())

        @pl.when(step == nsteps - 1)
        def _():
            @pl.when(is_owner_chip)
            def _():
                for k in range(1, 4):
                    to_owner(k).wait_recv()
                own_copy().wait()

            pl.when(jnp.logical_not(is_owner_chip))(lambda: to_owner(chip).wait_send())
            to_sibling().wait_send()
            gates()[1]()


def _fanout_sems(n):
    return [pltpu.SemaphoreType.DMA(((NDEV - 1) * n,)), pltpu.SemaphoreType.DMA(((NDEV - 1) * n,)), pltpu.SemaphoreType.DMA((n,))]


def _lanes(j):
    return pl.ds(pl.multiple_of(j * 128, 128), 128)


def _gather_first(c_row, cctx_row, wm, bm, wi, gate):
    def body(c_ref, cctx_ref, wm_ref, bm_ref, wi_ref, g_ref, cs_ref, mods_ref, owi, og, call_ref, mine_ref,
             send_sems, recv_sems, local_sems, c_send, c_recv, c_local, m_send, m_recv, m_local):
        x, y, c = lax.axis_index("x"), lax.axis_index("y"), lax.axis_index("c")
        sibling = (x, y, 1 - c)
        chips = [(1 - x, y), (x, 1 - y), (1 - x, 1 - y)]
        index = lambda px, py, pc: 4 * px + 2 * py + pc
        arrays = ((wi_ref, owi), (g_ref, og))
        n = len(arrays)

        def copy(a, k, block, to, own=False):
            src, out = arrays[a]
            return pltpu.make_async_remote_copy(
                src_ref=src if own else out.at[index(*block)], dst_ref=out.at[index(*block)],
                send_sem=send_sems.at[k * n + a], recv_sem=recv_sems.at[k * n + a], device_id=to, device_id_type=MESH)

        c_start, c_finish = _fanout([lambda j: c_ref], [lambda j: call_ref.at[j]], c_send, c_recv, c_local)
        c_start()
        mine = [pltpu.make_async_copy(src, out.at[index(x, y, c)], local_sems.at[a]) for a, (src, out) in enumerate(arrays)]
        first = [copy(a, 0, (x, y, c), sibling, own=True) for a in range(n)]
        first += [copy(a, 1 + j, (x, y, c), (*chip, c), own=True) for j, chip in enumerate(chips) for a in range(n)]
        for cp in mine + first:
            cp.start()

        c_finish()
        cs = jnp.concatenate([call_ref[j] for j in range(NDEV)] + [cctx_ref[...], jnp.zeros((16 - NDEV - 1, D), f32)], axis=0)
        cs_ref[...] = cs
        s, _ = _silu_and_grad(cs)
        mine_ref[...] = _dot_hi(s, wm_ref[...]) + bm_ref[:, pl.ds(pl.multiple_of(_me() * ncol, 128), ncol)]
        m_start, m_finish = _fanout([lambda j: mine_ref], [lambda j: mods_ref.at[j]], m_send, m_recv, m_local)
        m_start()

        passed = []
        for j, chip in enumerate(chips):
            for a in range(n):
                copy(a, 1 + j, (*chip, c), (x, y, c)).wait_recv()
            for a in range(n):
                cp = copy(a, 4 + j, (*chip, c), sibling)
                cp.start()
                passed.append(cp)
        for a in range(n):
            copy(a, 0, sibling, (x, y, c)).wait_recv()
        for j, chip in enumerate(chips):
            for a in range(n):
                copy(a, 4 + j, (*chip, 1 - c), (x, y, c)).wait_recv()
        for cp in first + passed:
            cp.wait_send()
        for cp in mine:
            cp.wait()
        m_finish()

    hbm = pl.BlockSpec(memory_space=pl.ANY)
    vm = pl.BlockSpec(memory_space=pltpu.VMEM)
    ncol = wm.shape[1]
    return pl.pallas_call(
        body, name="gather_first",
        out_shape=(jax.ShapeDtypeStruct((16, D), f32), jax.ShapeDtypeStruct((NDEV, 16, ncol), f32),
                   jax.ShapeDtypeStruct((NDEV,) + wi.shape, bf16), jax.ShapeDtypeStruct((NDEV,) + gate.shape, f32)),
        in_specs=[vm, vm, vm, vm, hbm, hbm], out_specs=(vm, vm, hbm, hbm),
        scratch_shapes=[pltpu.VMEM((NDEV, 1, D), f32), pltpu.VMEM((16, ncol), f32)] + _fanout_sems(2) + _fanout_sems(1) + _fanout_sems(1),
        compiler_params=_cp(),
    )(c_row, cctx_row, wm, bm, wi, gate)


SHARD = INW // NDEV
ROWS_RP = 128


def _overlap(lo, hi, a, b):
    s, e = max(lo, a), min(hi, b)
    return (s, e) if s < e else None


def _repack_w(wg):
    segs = ((0, 1024, PQ), (1024, 1024 + 2 * RANK, PLR), (1024 + 2 * RANK, INW, PZB))

    def body(g_ref, o_ref):
        for j in range(NDEV):
            lo, hi = j * SHARD, (j + 1) * SHARD
            for a, b, pad0 in segs:
                ov = _overlap(lo, hi, a, b)
                if ov:
                    s, e = ov
                    o_ref[:, pad0 + s - a:pad0 + e - a] = g_ref[j, :, s - lo:e - lo]
        o_ref[:, PLR + 2 * RANK:PW] = jnp.zeros((ROWS_RP, PW - PLR - 2 * RANK), bf16)

    return pl.pallas_call(
        body, name="repack_w", grid=(D // ROWS_RP,), out_shape=jax.ShapeDtypeStruct((D, PW), bf16),
        in_specs=[pl.BlockSpec((NDEV, ROWS_RP, SHARD), lambda i: (0, i, 0))],
        out_specs=pl.BlockSpec((ROWS_RP, PW), lambda i: (i, 0)), compiler_params=_cp(("arbitrary",)),
    )(wg)


LATE_END = 1024 + 2 * RANK
LATE_DESTS = 2


def _pack_blocks(o_ref, srcs, dests, dtype):
    for n, j in enumerate(dests):
        lo, hi = j * SHARD, (j + 1) * SHARD
        done = lo
        for a, b, src in srcs:
            ov = _overlap(lo, hi, a, b)
            if ov:
                s, e = ov
                if s > done:
                    o_ref[n, :, done - lo:s - lo] = jnp.zeros((ROWS_RP, s - done), dtype)
                o_ref[n, :, s - lo:e - lo] = src[:, s - a:e - a].astype(dtype)
                done = e
        if done < hi:
            o_ref[n, :, done - lo:hi - lo] = jnp.zeros((ROWS_RP, hi - done), dtype)


class _Pack:
    def __init__(self, extra, outs, fn):
        self.extra, self.outs, self.fn = list(extra), list(outs), fn


def _pack_early():
    def fn(accs, extra, outs):
        zbua, va, zag = accs
        for r0 in range(0, D, ROWS_RP):
            rows = pl.ds(r0, ROWS_RP)
            out, o0 = (outs[0], r0) if r0 < EARLY_A_ROWS else (outs[1], r0 - EARLY_A_ROWS)
            _pack_blocks(out.at[:, pl.ds(o0, ROWS_RP)], ((LATE_END, 2080, zbua.at[rows]), (2080, 2592, va.at[rows]), (2592, INW, zag.at[rows])),
                         range(NDEV), bf16)

    sds = jax.ShapeDtypeStruct
    return _Pack([], [sds((NDEV, EARLY_A_ROWS, SHARD), bf16), sds((NDEV, D - EARLY_A_ROWS, SHARD), bf16)], fn)


def _pack_late(dwk_c, dwv_c, dwl_c):
    def fn(accs, extra, outs):
        qkv_ref, lr_ref = accs
        kc_ref, vc_ref, lc_ref = extra
        for r0 in range(0, D, ROWS_RP):
            rows = pl.ds(r0, ROWS_RP)
            qkv = qkv_ref[rows, :] + jnp.concatenate([jnp.zeros((ROWS_RP, KW), f32), kc_ref[rows, :], vc_ref[rows, :]], axis=1)
            lr = lr_ref[rows, :] + lc_ref[rows, :]
            _pack_blocks(outs[0].at[:, rows], ((0, 1024, qkv), (1024, LATE_END, lr)), range(LATE_DESTS), bf16)

    return _Pack([dwk_c, dwv_c, dwl_c], [jax.ShapeDtypeStruct((LATE_DESTS, D, SHARD), bf16)], fn)


def _mod_tail(dnorm, dbm, dmod, dmodc, cs, wm):
    ncol = wm.shape[2]

    def body(dn_ref, dbm_ref, dmod_ref, dmodc_ref, cs_ref, wm_ref, g_wm, g_ng, g_bm, g_cc, *rest):
        a_dn, a_dbm, a_dmod, a_dmodc, dm_rows, gc_ref, a_gc = rest[0:7]
        s1, r1, l1, s2, r2, l2 = rest[7:13]
        start, finish = _fanout([lambda j: dn_ref, lambda j: dbm_ref, lambda j: dmod_ref, lambda j: dmodc_ref],
                                [lambda j, r=r: r.at[j] for r in (a_dn, a_dbm, a_dmod, a_dmodc)], s1, r1, l1)
        start()
        finish()

        def total(ref):
            t = ref[0]
            for j in range(1, NDEV):
                t = t + ref[j]
            return t

        g_ng[...] = total(a_dn)
        g_bm[...] = total(a_dbm)
        dmodc_tot = jnp.concatenate([total(a_dmodc), jnp.zeros((1, D), f32)], axis=1)
        dm_rows[...] = jnp.concatenate([a_dmod[j] for j in range(NDEV)] + [dmodc_tot, jnp.zeros((16 - NDEV - 1, 3 * D), f32)], axis=0)
        dm = dm_rows[:, pl.ds(pl.multiple_of(_me() * ncol, 128), ncol)]
        s, ds = _silu_and_grad(cs_ref[...])
        part = lax.dot_general(dm[8:9, :], wm_ref[0], (((1,), (1,)), ((), ())), preferred_element_type=f32,
                               precision=lax.Precision.HIGHEST)
        gc_ref[...] = part * ds[8:9, :]
        start2, finish2 = _fanout([lambda j: gc_ref], [lambda j: a_gc.at[j]], s2, r2, l2)
        start2()
        g = lax.dot_general(s, dm, (((0,), (0,)), ((), ())), preferred_element_type=f32, precision=lax.Precision.HIGHEST)
        g_wm[...] = g[None]
        finish2()
        g_cc[...] = total(a_gc)

    sds = jax.ShapeDtypeStruct
    row = lambda n: pltpu.VMEM((NDEV, 1, n), f32)
    return pl.pallas_call(
        body, name="mod_tail", out_shape=(sds(wm.shape, f32), sds((1, D), f32), sds((1, 3 * D), f32), sds((1, D), f32)),
        scratch_shapes=[row(D), row(3 * D), row(3 * D), row(2 * D), pltpu.VMEM((16, 3 * D), f32), pltpu.VMEM((1, D), f32), row(D)]
        + _fanout_sems(4) + _fanout_sems(1),
        compiler_params=_cp(),
    )(dnorm, dbm, dmod, dmodc, cs, wm)


def _adam_update(g, w_ref, m_ref, v_ref, go_ref, d_ref, mo_ref, vo_ref):
    c1 = 1.0 / (1.0 - ADAM_B1 ** ADAM_STEP)
    c2 = 1.0 / (1.0 - ADAM_B2 ** ADAM_STEP)
    mn = ADAM_B1 * m_ref[...] + (1.0 - ADAM_B1) * g
    vn = ADAM_B2 * v_ref[...] + (1.0 - ADAM_B2) * (g * g)
    go_ref[...] = g
    mo_ref[...] = mn
    vo_ref[...] = vn
    d_ref[...] = -ADAM_LR * ((mn * c1) / (jnp.sqrt(vn * c2) + ADAM_EPS) + ADAM_WD * w_ref[...])


def _adam_w_in(parts_a, parts_b, parts_late, w, m, v):
    na = EARLY_A_ROWS // ROWS_RP
    n = D // ROWS_RP
    whole = SHARD // 128 * 128

    def body(a_ref, b_ref, l_ref, w_hbm, m_hbm, v_hbm, go_hbm, d_hbm, mo_hbm, vo_hbm, ibuf, obuf, isem, osem):
        step = pl.program_id(0)

        def cols(t):
            return pl.ds(pl.multiple_of(t * ROWS_RP, ROWS_RP), ROWS_RP)

        def fetch(t):
            return [pltpu.make_async_copy(r.at[:, 0, cols(t)], ibuf.at[t % 2, k], isem.at[t % 2, k])
                    for k, r in enumerate((w_hbm, m_hbm, v_hbm))]

        def flush(t):
            return [pltpu.make_async_copy(obuf.at[t % 2, k], r.at[:, 0, cols(t)], osem.at[t % 2, k])
                    for k, r in enumerate((go_hbm, d_hbm, mo_hbm, vo_hbm))]

        def start(cps):
            for cp in cps:
                cp.start()

        def wait(cps):
            for cp in cps:
                cp.wait()

        pl.when(step == 0)(lambda: start(fetch(step)))
        pl.when(step + 1 < n)(lambda: start(fetch(step + 1)))
        me = _me()
        first = step < na
        early = jnp.where(first, a_ref[0], b_ref[0]).astype(f32)
        for i in range(1, NDEV):
            early = early + jnp.where(first, a_ref[i], b_ref[i]).astype(f32)
        late = l_ref[0].astype(f32)
        for i in range(1, 4):
            late = late + l_ref[i].astype(f32)
        g = jnp.where(me >= LATE_DESTS - 1, early, 0.0) + jnp.where(me < LATE_DESTS, late, 0.0)
        gt = jnp.concatenate([g[:, c:c + 128].T for c in range(0, whole, 128)] + [g[:, SHARD - 128:].T[128 - (SHARD - whole):]],
                             axis=0)
        wait(fetch(step))
        pl.when(step >= 2)(lambda: wait(flush(step - 2)))
        slot = step % 2
        _adam_update(gt, *(ibuf.at[slot, k] for k in range(3)), *(obuf.at[slot, k] for k in range(4)))
        start(flush(step))

        @pl.when(step == n - 1)
        def _():
            wait(flush(step - 1))
            wait(flush(step))

    hbm = pl.BlockSpec(memory_space=pl.ANY)
    res = pl.pallas_call(
        body, name="adam_w_in", grid=(n,), out_shape=tuple(jax.ShapeDtypeStruct((SHARD, 1, D), f32) for _ in range(4)),
        in_specs=[pl.BlockSpec((NDEV, ROWS_RP, SHARD), lambda i: (0, jnp.minimum(i, na - 1), 0)),
                  pl.BlockSpec((NDEV, ROWS_RP, SHARD), lambda i: (0, jnp.maximum(i - na, 0), 0)),
                  pl.BlockSpec((4, ROWS_RP, SHARD), lambda i: (0, i, 0)), hbm, hbm, hbm],
        out_specs=(hbm, hbm, hbm, hbm),
        scratch_shapes=[pltpu.VMEM((2, 3, SHARD, ROWS_RP), f32), pltpu.VMEM((2, 4, SHARD, ROWS_RP), f32),
                        pltpu.SemaphoreType.DMA((2, 3)), pltpu.SemaphoreType.DMA((2, 4))],
        compiler_params=_cp(("arbitrary",)),
    )(parts_a, parts_b, parts_late, *(jnp.transpose(t, (2, 0, 1)) for t in (w, m, v)))
    return tuple(jnp.transpose(t, (1, 2, 0)) for t in res)


def _adam_params(parts, ws, ms, vs, name):
    p = parts.shape[0]
    k = len(ws)
    nrows = [w.size // 128 for w in ws]
    starts = [sum(nrows[:i]) for i in range(k)]

    def shaped(g, shape):
        if len(shape) == 2:
            return jnp.concatenate([g[r:r + 1] for r in range(g.shape[0])], axis=1)
        return g.reshape(shape)

    def body(*refs):
        g_ref = refs[0]
        w_refs, m_refs, v_refs = refs[1:1 + k], refs[1 + k:1 + 2 * k], refs[1 + 2 * k:1 + 3 * k]
        outs = refs[1 + 3 * k:]
        for i in range(k):
            rows = slice(starts[i], starts[i] + nrows[i])
            g = g_ref[0, rows, :]
            for j in range(1, p):
                g = g + g_ref[j, rows, :]
            _adam_update(shaped(g, ws[i].shape), w_refs[i], m_refs[i], v_refs[i], outs[i], outs[k + i], outs[2 * k + i], outs[3 * k + i])

    vm = pl.BlockSpec(memory_space=pltpu.VMEM)
    res = pl.pallas_call(
        body, name=name, out_shape=tuple(jax.ShapeDtypeStruct(w.shape, f32) for _ in range(4) for w in ws),
        in_specs=[vm] * (1 + 3 * k), out_specs=(vm,) * (4 * k), compiler_params=_cp(),
    )(parts, *ws, *ms, *vs)
    return [res[i * k:(i + 1) * k] for i in range(4)]


def _adam_blocks(parts, ws, ms, vs, name):
    k = len(ws)

    def body(*refs):
        g_refs, w_refs, m_refs, v_refs = refs[0:k], refs[k:2 * k], refs[2 * k:3 * k], refs[3 * k:4 * k]
        outs = refs[4 * k:]
        for i in range(k):
            g = g_refs[i][0].astype(f32)
            for j in range(1, parts[i].shape[0]):
                g = g + g_refs[i][j].astype(f32)
            _adam_update(g[None], w_refs[i], m_refs[i], v_refs[i], outs[i], outs[k + i], outs[2 * k + i], outs[3 * k + i])

    vm = pl.BlockSpec(memory_space=pltpu.VMEM)
    res = pl.pallas_call(
        body, name=name, out_shape=tuple(jax.ShapeDtypeStruct(w.shape, f32) for _ in range(4) for w in ws),
        in_specs=[vm] * (4 * k), out_specs=(vm,) * (4 * k), compiler_params=_cp(),
    )(*parts, *ws, *ms, *vs)
    return [res[i * k:(i + 1) * k] for i in range(4)]


def _adam_plain(gs, ws, ms, vs, name):
    k = len(ws)

    def body(*refs):
        g_refs, w_refs, m_refs, v_refs = refs[0:k], refs[k:2 * k], refs[2 * k:3 * k], refs[3 * k:4 * k]
        outs = refs[4 * k:]
        for i in range(k):
            _adam_update(g_refs[i][...], w_refs[i], m_refs[i], v_refs[i], outs[i], outs[k + i], outs[2 * k + i], outs[3 * k + i])

    vm = pl.BlockSpec(memory_space=pltpu.VMEM)
    res = pl.pallas_call(
        body, name=name, out_shape=tuple(jax.ShapeDtypeStruct(w.shape, f32) for _ in range(4) for w in ws),
        in_specs=[vm] * (4 * k), out_specs=(vm,) * (4 * k), compiler_params=_cp(),
    )(*gs, *ws, *ms, *vs)
    return [res[i * k:(i + 1) * k] for i in range(4)]


def _adam_gate(parts, w2, gb, m_w2, m_gb, v_w2, v_gb):
    per = 128 // 32

    def body(p_ref, w2_ref, gb_ref, mw_ref, mb_ref, vw_ref, vb_ref, *outs):
        g = p_ref[0]
        for j in range(1, parts.shape[0]):
            g = g + p_ref[j]
        for d in range(2):
            gd = jnp.concatenate([g[(d * RANK + r) // per:(d * RANK + r) // per + 1, (r % per) * 32:(r % per + 1) * 32]
                                  for r in range(RANK)], axis=0)
            _adam_update(gd, *(t.at[0, d] for t in (w2_ref, mw_ref, vw_ref) + outs[0:4]))
        last = 2 * RANK // per
        gbv = jnp.concatenate([g[last:last + 1, d * 32:(d + 1) * 32] for d in range(2)], axis=0)
        _adam_update(gbv, *(t.at[0] for t in (gb_ref, mb_ref, vb_ref) + outs[4:8]))

    vm = pl.BlockSpec(memory_space=pltpu.VMEM)
    return pl.pallas_call(
        body, name="adam_gate", out_shape=tuple(jax.ShapeDtypeStruct(w.shape, f32) for w in (w2, gb) for _ in range(4)),
        in_specs=[vm] * 7, out_specs=(vm,) * 8, compiler_params=_cp(),
    )(parts, w2, gb, m_w2, m_gb, v_w2, v_gb)


def _in_proj(x, g, scale, shift, w_pad, tl, comm):
    l = x.shape[0]
    c_in, c_out, c_sems = comm.specs()

    def body(*refs):
        x_ref, g_ref, sc_ref, sh_ref, w_ref = refs[:5]
        cin = refs[5:5 + len(c_in)]
        p_ref, h_ref = refs[5 + len(c_in):7 + len(c_in)]
        cout = refs[7 + len(c_in):7 + len(c_in) + comm.n]
        sems = refs[7 + len(c_in) + comm.n:]

        def compute():
            xv = x_ref[...]
            r = lax.rsqrt(jnp.mean(xv * xv, axis=-1, keepdims=True) + EPS)
            h = (xv * r) * (g_ref[...] * (1.0 + sc_ref[...])) + sh_ref[...]
            hb = h.astype(bf16)
            h_ref[...] = hb
            p_ref[...] = jnp.dot(hb, w_ref[...], preferred_element_type=f32).astype(bf16)

        comm.run(cin, cout, sems, l // tl, compute)

    vec = pl.BlockSpec((1, D), lambda i: (0, 0))
    return pl.pallas_call(
        body, name="in_proj", grid=(l // tl,),
        out_shape=(jax.ShapeDtypeStruct((l, PW), bf16), jax.ShapeDtypeStruct((l, D), bf16)) + tuple(comm.outs),
        in_specs=[pl.BlockSpec((tl, D), lambda i: (i, 0)), vec, vec, vec, pl.BlockSpec((D, PW), lambda i: (0, 0))] + c_in,
        out_specs=(pl.BlockSpec((tl, PW), lambda i: (i, 0)), pl.BlockSpec((tl, D), lambda i: (i, 0))) + tuple(c_out),
        scratch_shapes=c_sems, compiler_params=_cp(("arbitrary",)),
    )(x, g, scale, shift, w_pad, *comm.ins)


def _tri(rev):
    i = lax.broadcasted_iota(jnp.int32, (CH, CH), 0)
    j = lax.broadcasted_iota(jnp.int32, (CH, CH), 1)
    return jnp.where((j >= i) if rev else (j <= i), 1.0, 0.0).astype(f32)


def _head_masks():
    lane = lax.broadcasted_iota(jnp.int32, (1, KW), 1) // HK
    return [jnp.where(lane == h, 1.0, 0.0).astype(f32) for h in range(NH)]


def _block_diag():
    r = lax.broadcasted_iota(jnp.int32, (VW, KW), 0) // HV
    c = lax.broadcasted_iota(jnp.int32, (VW, KW), 1) // HK
    return jnp.where(r == c, 1.0, 0.0).astype(f32)


def _decay(lr, w2, gb, tri, rev):
    logits = _dot(lr, w2) + gb
    a = _log_sigmoid(logits) * (1.0 / TAU)
    c = _dot_hi(tri, a)
    cl = c[0:1, :] if rev else c[CH - 1:CH, :]
    return logits, c, cl


def _state_fwd(k, v, c, cl, st, bd):
    return st * jnp.exp(cl) + bd * _tn(v, k * jnp.exp(cl - c))


def _state_bwd(k, v, c, cl, st0, dst, trit):
    ecl = jnp.exp(cl)
    edec = jnp.exp(cl - c)
    kdec = k * edec
    dv = _nt(kdec, dst)
    dkdec = _dot(v, dst)
    dcl = jnp.sum(dst * st0, axis=0, keepdims=True) * ecl + jnp.sum(dkdec * kdec, axis=0, keepdims=True)
    da = _dot_hi(trit, -dkdec * kdec) + dcl
    return dkdec * edec, dv, da, dst * ecl


def _bdot(a, b):
    return lax.dot_general(a.astype(bf16), b.astype(bf16), (((2,), (1,)), ((0,), (0,))), preferred_element_type=f32)


def _bnt(a, b):
    return lax.dot_general(a.astype(bf16), b.astype(bf16), (((2,), (2,)), ((0,), (0,))), preferred_element_type=f32)


def _btn(a, b):
    return lax.dot_general(a.astype(bf16), b.astype(bf16), (((1,), (1,)), ((0,), (0,))), preferred_element_type=f32)


def _scan_chunks(x, rev):
    nc = x.shape[0]
    hi = x.astype(bf16)
    r1 = x - hi.astype(f32)
    mid = r1.astype(bf16)
    lo = (r1 - mid.astype(f32)).astype(bf16)
    terms = jnp.concatenate([hi, mid, lo], axis=1)
    tri3 = jnp.broadcast_to(jnp.concatenate([_tri(rev)] * 3, axis=1).astype(bf16)[None], (nc, CH, 3 * CH))
    return lax.dot_general(tri3, terms, (((2,), (1,)), ((0,), (0,))), preferred_element_type=f32)


class _Tile:
    pass


def _tile_prep(q_ref, k_ref, lr_ref, w2_ref, gb_ref, rev, nc):
    t = _Tile()
    tg = nc * CH
    t.logits = _dot(lr_ref[...], w2_ref[...]) + gb_ref[...]
    c = _scan_chunks((_log_sigmoid(t.logits) * (1.0 / TAU)).reshape(nc, CH, KW), rev)
    cl = c[:, 0:1, :] if rev else c[:, CH - 1:CH, :]
    k = k_ref[...].astype(f32).reshape(nc, CH, KW)
    t.ec, t.enc, t.edec, t.ecl = jnp.exp(c), jnp.exp(-c), jnp.exp(cl - c), jnp.exp(cl)
    t.qd = q_ref[...].astype(f32).reshape(nc, CH, KW) * t.ec * QSCALE
    t.kd = k * t.enc
    t.kdec = k * t.edec
    hm = _head_masks()
    t.tri4 = jnp.concatenate([_tri(rev)] * NH, axis=0)[None]
    t.qs = jnp.concatenate([t.qd * hm[h] for h in range(NH)], axis=1)
    t.pst = _bnt(t.qs, t.kd) * t.tri4
    return t


def _gla_fwd(p, w2p, gb, s0, rev, tg, name):
    l = p.shape[0]
    nb, nc = l // tg, tg // CH

    def body(q_ref, k_ref, v_ref, lr_ref, w2_ref, gb_ref, s0_ref, o_ref, st_ref, st):
        @pl.when(pl.program_id(0) == 0)
        def _():
            st[...] = s0_ref[...]

        t = _tile_prep(q_ref, k_ref, lr_ref, w2_ref, gb_ref, rev, nc)
        v = v_ref[...].reshape(nc, CH, VW)
        intra = jnp.concatenate([_bdot(t.pst[:, h * CH:(h + 1) * CH], v[:, :, h * HV:(h + 1) * HV]) for h in range(NH)], axis=2)
        bd = _block_diag()
        s = st[...]
        for n in (range(nc - 1, -1, -1) if rev else range(nc)):
            st_ref[n] = s.astype(bf16)
            s = s * t.ecl[n] + bd * _tn(v[n], t.kdec[n])
        st[...] = s
        o_ref[...] = (_bnt(t.qd, st_ref[...]) + intra).reshape(tg, VW).astype(bf16)

    blk = (lambda i: nb - 1 - i) if rev else (lambda i: i)
    return pl.pallas_call(
        body, name=name, grid=(nb,),
        out_shape=(jax.ShapeDtypeStruct((l, VW), bf16), jax.ShapeDtypeStruct((l // CH, VW, KW), bf16)),
        in_specs=[pl.BlockSpec((tg, KW), lambda i: (blk(i), PQ // KW)), pl.BlockSpec((tg, KW), lambda i: (blk(i), PK // KW)),
                  pl.BlockSpec((tg, VW), lambda i: (blk(i), PV // VW)), pl.BlockSpec((tg, LRW), lambda i: (blk(i), PLR // LRW)),
                  pl.BlockSpec((LRW, KW), lambda i: (0, 0)), pl.BlockSpec((1, KW), lambda i: (0, 0)),
                  pl.BlockSpec((VW, KW), lambda i: (0, 0))],
        out_specs=(pl.BlockSpec((tg, VW), lambda i: (blk(i), 0)), pl.BlockSpec((nc, VW, KW), lambda i: (blk(i), 0, 0))),
        scratch_shapes=[pltpu.VMEM((VW, KW), f32)],
        compiler_params=_cp(("arbitrary",)),
    )(p, p, p, p, w2p, gb, s0)


def _gla_bwd(p, do, states, w2p, gb, prev, rev, tg, name, comm):
    l = p.shape[0]
    nb, nc = l // tg, tg // CH
    out_dt = bf16
    c_in, c_out, c_sems = comm.specs()

    def body(*refs):
        q_ref, k_ref, v_ref, lr_ref, do_ref, st_ref, w2_ref, gb_ref = refs[:8]
        refs = refs[8:]
        if prev is not None:
            pq_ref, pl_ref = refs[:2]
            refs = refs[2:]
        cin, refs = refs[:len(c_in)], refs[len(c_in):]
        dqkv_ref, dlr_ref, dw2_ref, dgb_ref, ds0_ref = refs[:5]
        cout, dst, ds_buf, sems = refs[5:5 + comm.n], refs[5 + comm.n], refs[6 + comm.n], refs[7 + comm.n:]
        comm.run(cin, cout, sems, nb, lambda: compute(q_ref, k_ref, v_ref, lr_ref, do_ref, st_ref, w2_ref, gb_ref,
                                                      pq_ref if prev is not None else None, pl_ref if prev is not None else None,
                                                      dqkv_ref, dlr_ref, dw2_ref, dgb_ref, ds0_ref, dst, ds_buf))

    def compute(q_ref, k_ref, v_ref, lr_ref, do_ref, st_ref, w2_ref, gb_ref, pq_ref, pl_ref,
                dqkv_ref, dlr_ref, dw2_ref, dgb_ref, ds0_ref, dst, ds_buf):
        @pl.when(pl.program_id(0) == 0)
        def _():
            dst[...] = jnp.zeros_like(dst)
            dw2_ref[...] = jnp.zeros_like(dw2_ref)
            dgb_ref[...] = jnp.zeros_like(dgb_ref)

        t = _tile_prep(q_ref, k_ref, lr_ref, w2_ref, gb_ref, rev, nc)
        hm = _head_masks()
        v = v_ref[...].reshape(nc, CH, VW)
        do = do_ref[...].reshape(nc, CH, VW)
        heads = lambda a, h: a[:, :, h * HV:(h + 1) * HV]
        dpst = jnp.concatenate([_bnt(heads(do, h), heads(v, h)) for h in range(NH)], axis=1) * t.tri4
        dv = jnp.concatenate([_btn(t.pst[:, h * CH:(h + 1) * CH], heads(do, h)) for h in range(NH)], axis=2)
        dqd = _bdot(do, st_ref[...])
        for h in range(NH):
            dqd = dqd + hm[h] * _bdot(dpst[:, h * CH:(h + 1) * CH], t.kd)
        dkd = _btn(dpst, t.qs)
        bd = _block_diag()
        d = dst[...]
        for n in (range(nc) if rev else range(nc - 1, -1, -1)):
            ds_buf[n] = d
            d = d * t.ecl[n] + bd * _tn(do[n], t.qd[n])
        dst[...] = d
        ds0_ref[...] = d
        ds = ds_buf[...]
        dv = dv + _bnt(t.kdec, ds)
        dkdec = _bdot(v, ds)
        dcl = jnp.sum(ds * st_ref[...].astype(f32), axis=1, keepdims=True) * t.ecl + jnp.sum(dkdec * t.kdec, axis=1, keepdims=True)
        dc = dqd * t.qd - dkd * t.kd - dkdec * t.kdec
        da = _scan_chunks(dc, not rev) + dcl
        dq = dqd * t.ec * QSCALE
        dk = dkd * t.enc + dkdec * t.edec
        dlog = da.reshape(tg, KW) * _sigmoid(-t.logits) * (1.0 / TAU)
        dlr = _nt(dlog, w2_ref[...])
        dw2_ref[...] += _tn(lr_ref[...], dlog)
        dgb_ref[...] += jnp.sum(dlog, axis=0, keepdims=True)
        dqkv = jnp.concatenate([dq, dk, dv], axis=2).reshape(tg, 2 * KW + VW)
        if prev is not None:
            dqkv = dqkv + pq_ref[...]
            dlr = dlr + pl_ref[...]
        dqkv_ref[...] = dqkv.astype(out_dt)
        dlr_ref[...] = dlr.astype(out_dt)

    blk = (lambda i: i) if rev else (lambda i: nb - 1 - i)
    in_specs = [pl.BlockSpec((tg, KW), lambda i: (blk(i), PQ // KW)), pl.BlockSpec((tg, KW), lambda i: (blk(i), PK // KW)),
                pl.BlockSpec((tg, VW), lambda i: (blk(i), PV // VW)), pl.BlockSpec((tg, LRW), lambda i: (blk(i), PLR // LRW)),
                pl.BlockSpec((tg, VW), lambda i: (blk(i), 0)), pl.BlockSpec((nc, VW, KW), lambda i: (blk(i), 0, 0)),
                pl.BlockSpec((LRW, KW), lambda i: (0, 0)), pl.BlockSpec((1, KW), lambda i: (0, 0))]
    args = [p, p, p, p, do, states, w2p, gb]
    if prev is not None:
        in_specs += [pl.BlockSpec((tg, 2 * KW + VW), lambda i: (blk(i), 0)), pl.BlockSpec((tg, LRW), lambda i: (blk(i), 0))]
        args += list(prev)
    return pl.pallas_call(
        body, name=name, grid=(nb,),
        out_shape=(jax.ShapeDtypeStruct((l, 2 * KW + VW), out_dt), jax.ShapeDtypeStruct((l, LRW), out_dt),
                   jax.ShapeDtypeStruct((LRW, KW), f32), jax.ShapeDtypeStruct((1, KW), f32), jax.ShapeDtypeStruct((VW, KW), f32))
        + tuple(comm.outs),
        in_specs=in_specs + c_in,
        out_specs=(pl.BlockSpec((tg, 2 * KW + VW), lambda i: (blk(i), 0)), pl.BlockSpec((tg, LRW), lambda i: (blk(i), 0)),
                   pl.BlockSpec((LRW, KW), lambda i: (0, 0)), pl.BlockSpec((1, KW), lambda i: (0, 0)),
                   pl.BlockSpec((VW, KW), lambda i: (0, 0))) + tuple(c_out),
        scratch_shapes=[pltpu.VMEM((VW, KW), f32), pltpu.VMEM((nc, VW, KW), f32)] + c_sems,
        compiler_params=_cp(("arbitrary",)),
    )(*args, *comm.ins)


def _ctx_hidden(ctx_ref, g_ref, sc_ref, sh_ref):
    xv = ctx_ref[...]
    r = lax.rsqrt(jnp.mean(xv * xv, axis=-1, keepdims=True) + EPS)
    xn = xv * r
    return xn, xn * (g_ref[...] * (1.0 + sc_ref[...])) + sh_ref[...]


_CTX_W_SPECS = [pl.BlockSpec((D, KW), lambda i: (0, PK // KW)), pl.BlockSpec((D, VW), lambda i: (0, PV // VW)),
                pl.BlockSpec((D, LRW), lambda i: (0, PLR // LRW))]


def _ctx_fwd(ctx, g, scale, shift, w_pad, w2f, w2b, gbf, gbb):
    ncc = CTX // CH

    def body(ctx_ref, g_ref, sc_ref, sh_ref, wk_ref, wv_ref, wl_ref, w2f_ref, w2b_ref, gbf_ref, gbb_ref, sf_ref, sb_ref):
        _, hc = _ctx_hidden(ctx_ref, g_ref, sc_ref, sh_ref)
        k, v, lr = _dot(hc, wk_ref[...]), _dot(hc, wv_ref[...]), _dot(hc, wl_ref[...])
        bd = _block_diag()
        for rev, w2_ref, gb_ref, out in ((False, w2f_ref, gbf_ref, sf_ref), (True, w2b_ref, gbb_ref, sb_ref)):
            tri = _tri(rev)
            st = jnp.zeros((VW, KW), f32)
            for j in (range(ncc - 1, -1, -1) if rev else range(ncc)):
                rows = slice(j * CH, (j + 1) * CH)
                _, c, cl = _decay(lr[rows], w2_ref[...], gb_ref[...], tri, rev)
                st = _state_fwd(k[rows], v[rows], c, cl, st, bd)
            out[...] = st

    vec = pl.BlockSpec((1, D), lambda i: (0, 0))
    w2s = pl.BlockSpec((LRW, KW), lambda i: (0, 0))
    gbs = pl.BlockSpec((1, KW), lambda i: (0, 0))
    sts = pl.BlockSpec((VW, KW), lambda i: (0, 0))
    return pl.pallas_call(
        body, name="ctx_fwd", grid=(1,), out_shape=(jax.ShapeDtypeStruct((VW, KW), f32),) * 2,
        in_specs=[pl.BlockSpec((CTX, D), lambda i: (0, 0)), vec, vec, vec] + _CTX_W_SPECS + [w2s, w2s, gbs, gbs],
        out_specs=(sts, sts), compiler_params=_cp(("arbitrary",)),
    )(ctx, g, scale, shift, w_pad, w_pad, w_pad, w2f, w2b, gbf, gbb)


def _ctx_bwd(ctx, g, scale, shift, w_pad, w2f, w2b, gbf, gbb, dsf, dsb):
    ncc = CTX // CH

    def body(ctx_ref, g_ref, sc_ref, sh_ref, wk_ref, wv_ref, wl_ref, w2f_ref, w2b_ref, gbf_ref, gbb_ref, dsf_ref, dsb_ref,
             dwk_ref, dwv_ref, dwl_ref, dmod_ref, dg_ref, dw2_ref, dgb_ref):
        xn, hc = _ctx_hidden(ctx_ref, g_ref, sc_ref, sh_ref)
        k, v, lr = _dot(hc, wk_ref[...]), _dot(hc, wv_ref[...]), _dot(hc, wl_ref[...])
        bd = _block_diag()
        dk_rows, dv_rows, dl_rows = [None] * ncc, [None] * ncc, [None] * ncc
        for d, (rev, w2_ref, gb_ref, ds_ref) in enumerate(((False, w2f_ref, gbf_ref, dsf_ref), (True, w2b_ref, gbb_ref, dsb_ref))):
            tri = _tri(rev)
            order = list(range(ncc - 1, -1, -1) if rev else range(ncc))
            st, saved = jnp.zeros((VW, KW), f32), {}
            for j in order:
                rows = slice(j * CH, (j + 1) * CH)
                logits, c, cl = _decay(lr[rows], w2_ref[...], gb_ref[...], tri, rev)
                saved[j] = (logits, c, cl, st)
                st = _state_fwd(k[rows], v[rows], c, cl, st, bd)
            dst = ds_ref[...]
            dw2 = jnp.zeros((LRW, KW), f32)
            dgb = jnp.zeros((1, KW), f32)
            for j in reversed(order):
                rows = slice(j * CH, (j + 1) * CH)
                logits, c, cl, st0 = saved[j]
                dk, dv, da, dst = _state_bwd(k[rows], v[rows], c, cl, st0, dst, _tri(not rev))
                dlog = da * _sigmoid(-logits) * (1.0 / TAU)
                dl = _nt(dlog, w2_ref[...])
                dw2 = dw2 + _tn(lr[rows], dlog)
                dgb = dgb + jnp.sum(dlog, axis=0, keepdims=True)
                dk_rows[j] = dk if dk_rows[j] is None else dk_rows[j] + dk
                dv_rows[j] = dv if dv_rows[j] is None else dv_rows[j] + dv
                dl_rows[j] = dl if dl_rows[j] is None else dl_rows[j] + dl
            dw2_ref[d] = dw2
            dgb_ref[d] = dgb
        dk, dv, dl = (jnp.concatenate(t, axis=0) for t in (dk_rows, dv_rows, dl_rows))
        dwk_ref[...] = _tn(hc, dk)
        dwv_ref[...] = _tn(hc, dv)
        dwl_ref[...] = _tn(hc, dl)
        dh = _nt(dk, wk_ref[...]) + _nt(dv, wv_ref[...]) + _nt(dl, wl_ref[...])
        gx = dh * xn
        dmod_ref[:, 0:D] = jnp.sum(dh, axis=0, keepdims=True)
        dmod_ref[:, D:2 * D] = jnp.sum(gx, axis=0, keepdims=True) * g_ref[...]
        dg_ref[...] = jnp.sum(gx, axis=0, keepdims=True) * (1.0 + sc_ref[...])

    vec = pl.BlockSpec((1, D), lambda i: (0, 0))
    w2s = pl.BlockSpec((LRW, KW), lambda i: (0, 0))
    gbs = pl.BlockSpec((1, KW), lambda i: (0, 0))
    sts = pl.BlockSpec((VW, KW), lambda i: (0, 0))
    full = lambda *s: pl.BlockSpec(s, lambda i: (0,) * len(s))
    return pl.pallas_call(
        body, name="ctx_bwd", grid=(1,),
        out_shape=(jax.ShapeDtypeStruct((D, KW), f32), jax.ShapeDtypeStruct((D, VW), f32), jax.ShapeDtypeStruct((D, LRW), f32),
                   jax.ShapeDtypeStruct((1, 2 * D), f32), jax.ShapeDtypeStruct((1, D), f32),
                   jax.ShapeDtypeStruct((2, LRW, KW), f32), jax.ShapeDtypeStruct((2, 1, KW), f32)),
        in_specs=[pl.BlockSpec((CTX, D), lambda i: (0, 0)), vec, vec, vec] + _CTX_W_SPECS + [w2s, w2s, gbs, gbs, sts, sts],
        out_specs=(full(D, KW), full(D, VW), full(D, LRW), full(1, 2 * D), full(1, D), full(2, LRW, KW), full(2, 1, KW)),
        compiler_params=_cp(("arbitrary",)),
    )(ctx, g, scale, shift, w_pad, w_pad, w_pad, w2f, w2b, gbf, gbb, dsf, dsb)


def _layernorm(va, g, b):
    mu = jnp.mean(va, axis=-1, keepdims=True)
    xc = va - mu
    rstd = lax.rsqrt(jnp.mean(xc * xc, axis=-1, keepdims=True) + EPS)
    vhat = xc * rstd
    return vhat, rstd, vhat * g + b


MIX_PIECES = 8


def _mix_fwd(p, ln_g, ln_b, ws, bs_t):
    l = p.shape[0]
    half = AW // 2
    rp = l // MIX_PIECES
    cpp = rp // ACH

    def body(p_ref, g_ref, b_ref, ws_ref, bs_ref, sv_hbm, va_buf, col_buf, sv_buf, in_sems, out_sems):
        piece = lambda i: pl.ds(pl.multiple_of(i * rp, rp), rp)
        load = lambda i: pltpu.make_async_copy(p_ref.at[piece(i), pl.ds(PVA, AW)], va_buf.at[piece(i)], in_sems.at[i])
        store = lambda i: pltpu.make_async_copy(sv_buf.at[piece(i)], sv_hbm.at[piece(i)], out_sems.at[i])
        for i in range(MIX_PIECES):
            load(i).start()

        def rows_piece(i, carry):
            load(i).wait()
            for j in range(cpp):
                rows = pl.ds(pl.multiple_of(i * rp + j * ACH, ACH), ACH)
                _, _, vn = _layernorm(va_buf[rows, :].astype(f32), g_ref[...], b_ref[...])
                for gi in range(2):
                    sl = slice(gi * ACH, (gi + 1) * ACH)
                    sv_buf[rows, sl] = (_dot(ws_ref[gi], vn[:, sl]) + bs_ref[:, gi:gi + 1]).astype(bf16)
                col_buf[0, rows, :] = vn[:, half:half + ACH]
                col_buf[1, rows, :] = vn[:, half + ACH:]
            return carry

        lax.fori_loop(0, MIX_PIECES, rows_piece, 0)

        def cols_step(cidx, carry):
            rows = pl.ds(cidx, ACH, stride=GW)
            for gi in range(2, 4):
                col_buf[gi - 2, rows, :] = _dot(ws_ref[gi], col_buf[gi - 2, rows, :]) + bs_ref[:, gi:gi + 1]
            return carry

        lax.fori_loop(0, GW, cols_step, 0, unroll=8)

        def out_piece(i, carry):
            for j in range(cpp):
                rows = pl.ds(pl.multiple_of(i * rp + j * ACH, ACH), ACH)
                sv_buf[rows, half:half + ACH] = col_buf[0, rows, :].astype(bf16)
                sv_buf[rows, half + ACH:] = col_buf[1, rows, :].astype(bf16)
            store(i).start()
            return carry

        lax.fori_loop(0, MIX_PIECES, out_piece, 0)
        for i in range(MIX_PIECES):
            store(i).wait()

    vm = pl.BlockSpec(memory_space=pltpu.VMEM)
    hbm = pl.BlockSpec(memory_space=pl.ANY)
    return pl.pallas_call(
        body, name="mix_fwd", out_shape=jax.ShapeDtypeStruct((l, AW), bf16),
        in_specs=[hbm, vm, vm, vm, vm], out_specs=hbm,
        scratch_shapes=[pltpu.VMEM((l, AW), bf16), pltpu.VMEM((2, l, ACH), f32), pltpu.VMEM((l, AW), bf16),
                        pltpu.SemaphoreType.DMA((MIX_PIECES,)), pltpu.SemaphoreType.DMA((MIX_PIECES,))],
        compiler_params=_cp(),
    )(p, ln_g, ln_b, ws, bs_t)


def _mix_bwd(p, dsv, ln_g, ln_b, ws_t):
    l = p.shape[0]
    half = AW // 2
    rp = l // MIX_PIECES
    cpp = rp // ACH

    def body(p_ref, dsv_hbm, g_ref, b_ref, wst_ref, dva_hbm, dws_ref, dbs_ref, dg_ref, db_ref,
             va_buf, dsv_buf, vn_col, ds_col, dva_buf, va_sems, ds_sems, out_sems):
        piece = lambda i: pl.ds(pl.multiple_of(i * rp, rp), rp)
        load_va = lambda i: pltpu.make_async_copy(p_ref.at[piece(i), pl.ds(PVA, AW)], va_buf.at[piece(i)], va_sems.at[i])
        load_ds = lambda i: pltpu.make_async_copy(dsv_hbm.at[piece(i)], dsv_buf.at[piece(i)], ds_sems.at[i])
        store = lambda i: pltpu.make_async_copy(dva_buf.at[piece(i)], dva_hbm.at[piece(i)], out_sems.at[i])
        for i in range(MIX_PIECES):
            load_va(i).start()
            load_ds(i).start()
        dws_ref[...] = jnp.zeros_like(dws_ref)
        dbs_ref[...] = jnp.zeros_like(dbs_ref)
        dg_ref[...] = jnp.zeros_like(dg_ref)
        db_ref[...] = jnp.zeros_like(db_ref)

        def rows_piece(i, carry):
            load_va(i).wait()
            load_ds(i).wait()
            for j in range(cpp):
                rows = pl.ds(pl.multiple_of(i * rp + j * ACH, ACH), ACH)
                _, _, vn = _layernorm(va_buf[rows, :].astype(f32), g_ref[...], b_ref[...])
                ds = dsv_buf[rows, :].astype(f32)
                for gi in range(2):
                    sl = slice(gi * ACH, (gi + 1) * ACH)
                    dws_ref[gi] += _nt(ds[:, sl], vn[:, sl])
                    dbs_ref[gi] += ds[:, sl]
                for gi in range(2):
                    sl = slice(half + gi * ACH, half + (gi + 1) * ACH)
                    vn_col[gi, rows, :] = vn[:, sl]
                    ds_col[gi, rows, :] = ds[:, sl]
            return carry

        lax.fori_loop(0, MIX_PIECES, rows_piece, 0)

        def cols_step(cidx, carry):
            rows = pl.ds(cidx, ACH, stride=GW)
            for gi in range(2, 4):
                ds = ds_col[gi - 2, rows, :]
                dws_ref[gi] += _nt(ds, vn_col[gi - 2, rows, :])
                dbs_ref[gi] += ds
                ds_col[gi - 2, rows, :] = _dot(wst_ref[gi], ds)
            return carry

        lax.fori_loop(0, GW, cols_step, 0, unroll=8)

        def out_piece(i, carry):
            for j in range(cpp):
                rows = pl.ds(pl.multiple_of(i * rp + j * ACH, ACH), ACH)
                vhat, rstd, _ = _layernorm(va_buf[rows, :].astype(f32), g_ref[...], b_ref[...])
                ds = dsv_buf[rows, :].astype(f32)
                dvn = jnp.concatenate([_dot(wst_ref[0], ds[:, 0:ACH]), _dot(wst_ref[1], ds[:, ACH:half]),
                                       ds_col[0, rows, :], ds_col[1, rows, :]], axis=1)
                dg_ref[...] += jnp.sum(dvn * vhat, axis=0, keepdims=True)
                db_ref[...] += jnp.sum(dvn, axis=0, keepdims=True)
                dvh = dvn * g_ref[...]
                dva = rstd * (dvh - jnp.mean(dvh, axis=-1, keepdims=True) - vhat * jnp.mean(dvh * vhat, axis=-1, keepdims=True))
                dva_buf[rows, :] = dva.astype(bf16)
            store(i).start()
            return carry

        lax.fori_loop(0, MIX_PIECES, out_piece, 0)
        for i in range(MIX_PIECES):
            store(i).wait()

    vm = pl.BlockSpec(memory_space=pltpu.VMEM)
    hbm = pl.BlockSpec(memory_space=pl.ANY)
    dma = lambda: pltpu.SemaphoreType.DMA((MIX_PIECES,))
    return pl.pallas_call(
        body, name="mix_bwd",
        out_shape=(jax.ShapeDtypeStruct((l, AW), bf16), jax.ShapeDtypeStruct((4, ACH, ACH), f32), jax.ShapeDtypeStruct((4, ACH, ACH), f32),
                   jax.ShapeDtypeStruct((1, AW), f32), jax.ShapeDtypeStruct((1, AW), f32)),
        in_specs=[hbm, hbm, vm, vm, vm], out_specs=(hbm, vm, vm, vm, vm),
        scratch_shapes=[pltpu.VMEM((l, AW), bf16), pltpu.VMEM((l, AW), bf16), pltpu.VMEM((2, l, ACH), f32), pltpu.VMEM((2, l, ACH), f32),
                        pltpu.VMEM((l, AW), bf16), dma(), dma(), dma()],
        compiler_params=_cp(),
    )(p, dsv, ln_g, ln_b, ws_t)


def _mid(x, tgt, p, o_f, o_b, sv, gate, gf, gb_norm, w_pa, w_pb, w_out, tl):
    l = x.shape[0]

    def body(x_ref, t_ref, zb_ref, ua_ref, za_ref, g1_ref, g2_ref, of_ref, ob_ref, sv_ref, gate_ref, gf_ref, gbn_ref,
             wpa_ref, wpb_ref, wout_ref,
             dx1_ref, dzbua_ref, dzag_ref, dsv_ref, do_ref, dwout_bf, dwpa_bf, dwpb_bf, dgf_ref, dgate_ref, dgbn_ref, loss_ref,
             dwout_ref, dwpa_ref, dwpb_ref):
        @pl.when(pl.program_id(0) == 0)
        def _():
            for r in (dwout_ref, dwpa_ref, dwpb_ref, dgf_ref, dgate_ref, dgbn_ref, loss_ref):
                r[...] = jnp.zeros_like(r)

        o = of_ref[...].astype(f32) + ob_ref[...].astype(f32)
        rr = jnp.concatenate(
            [jnp.broadcast_to(lax.rsqrt(jnp.mean(o[:, h * HV:(h + 1) * HV] ** 2, axis=-1, keepdims=True) + EPS), (tl, HV))
             for h in range(NH)], axis=1)
        ohat = o * rr
        on = ohat * gbn_ref[...]
        szb, dszb = _silu_and_grad(zb_ref[...].astype(f32))
        tb = on * szb
        u = ua_ref[...].astype(f32)
        svv = sv_ref[...].astype(f32)
        sza, dsza = _silu_and_grad(za_ref[...].astype(f32))
        ta = u * svv * sza
        ya = _dot(ta, wpa_ref[...])
        yb = _dot(tb, wpb_ref[...])
        g1 = _sigmoid(g1_ref[...].astype(f32))
        g2 = _sigmoid(g2_ref[...].astype(f32))
        m = g1 * ya + g2 * yb
        y2 = _dot(m, wout_ref[...])
        x1 = x_ref[...] + gate_ref[...] * y2
        r1 = lax.rsqrt(jnp.mean(x1 * x1, axis=-1, keepdims=True) + EPS)
        x1n = x1 * r1
        err = x1n * gf_ref[...] - t_ref[...]
        loss_ref[...] += jnp.sum(jnp.sum(err * err, axis=-1, keepdims=True), axis=0, keepdims=True) * (0.5 / D)
        dout = err * (1.0 / D)
        dgf_ref[...] += jnp.sum(dout * x1n, axis=0, keepdims=True)
        dx1n = dout * gf_ref[...]
        dx1 = r1 * (dx1n - x1n * jnp.mean(dx1n * x1n, axis=-1, keepdims=True))
        dx1_ref[...] = dx1
        dgate_ref[...] += jnp.sum(dx1 * y2, axis=0, keepdims=True)
        dy2 = dx1 * gate_ref[...]
        dwout_ref[...] += _tn(m, dy2)
        dm = _nt(dy2, wout_ref[...])
        dya = dm * g1
        dyb = dm * g2
        dzag_ref[:, AW:AW + D] = (dm * ya * g1 * (1.0 - g1)).astype(bf16)
        dzag_ref[:, AW + D:] = (dm * yb * g2 * (1.0 - g2)).astype(bf16)
        dwpa_ref[...] += _tn(ta, dya)
        dta = _nt(dya, wpa_ref[...])
        dzbua_ref[:, AW:] = (dta * svv * sza).astype(bf16)
        dsv_ref[...] = (dta * u * sza).astype(bf16)
        dzag_ref[:, 0:AW] = (dta * u * svv * dsza).astype(bf16)
        dwpb_ref[...] += _tn(tb, dyb)
        dtb = _nt(dyb, wpb_ref[...])
        don = dtb * szb
        dzbua_ref[:, 0:AW] = (dtb * on * dszb).astype(bf16)
        dgbn_ref[...] += jnp.sum(don * ohat, axis=0, keepdims=True)
        doh = don * gbn_ref[...]
        prod = doh * ohat
        mh = jnp.concatenate(
            [jnp.broadcast_to(jnp.mean(prod[:, h * HV:(h + 1) * HV], axis=-1, keepdims=True), (tl, HV)) for h in range(NH)], axis=1)
        do_ref[...] = (rr * (doh - ohat * mh)).astype(bf16)

        @pl.when(pl.program_id(0) == l // tl - 1)
        def _():
            for acc, out in ((dwout_ref, dwout_bf), (dwpa_ref, dwpa_bf), (dwpb_ref, dwpb_bf)):
                out[...] = acc[...].astype(bf16)

    row = lambda w, j: pl.BlockSpec((tl, w), lambda i, j=j: (i, j))
    full = lambda *s: pl.BlockSpec(s, lambda i: (0,) * len(s))
    return pl.pallas_call(
        body, name="mid", grid=(l // tl,),
        out_shape=(jax.ShapeDtypeStruct((l, D), f32), jax.ShapeDtypeStruct((l, 2 * AW), bf16), jax.ShapeDtypeStruct((l, AW + 2 * D), bf16),
                   jax.ShapeDtypeStruct((l, AW), bf16), jax.ShapeDtypeStruct((l, VW), bf16),
                   jax.ShapeDtypeStruct((D, D), bf16), jax.ShapeDtypeStruct((AW, D), bf16), jax.ShapeDtypeStruct((VW, D), bf16),
                   jax.ShapeDtypeStruct((1, D), f32), jax.ShapeDtypeStruct((1, D), f32), jax.ShapeDtypeStruct((1, VW), f32),
                   jax.ShapeDtypeStruct((1, 1), f32)),
        scratch_shapes=[pltpu.VMEM((D, D), f32), pltpu.VMEM((AW, D), f32), pltpu.VMEM((VW, D), f32)],
        in_specs=[row(D, 0), row(D, 0), row(AW, PZB // AW), row(AW, PUA // AW), row(AW, PZA // AW), row(D, PG1 // D), row(D, PG2 // D),
                  row(VW, 0), row(VW, 0), row(AW, 0), full(1, D), full(1, D), full(1, VW), full(AW, D), full(VW, D), full(D, D)],
        out_specs=(row(D, 0), row(2 * AW, 0), row(AW + 2 * D, 0), row(AW, 0), row(VW, 0),
                   full(D, D), full(AW, D), full(VW, D), full(1, D), full(1, D), full(1, VW), full(1, 1)),
        compiler_params=_cp(("arbitrary",)),
    )(x, tgt, p, p, p, p, p, o_f, o_b, sv, gate, gf, gb_norm, w_pa, w_pb, w_out)


def _in_bwd(x, dx1, dqkv, dzbua, dva, dzag, dlr, w_pad, g, scale, tl, comm):
    l = x.shape[0]
    c_in, c_out, c_sems = comm.specs()

    def body(*refs):
        cin = refs[14:14 + len(c_in)]
        outs = refs[14 + len(c_in):]
        comm.run(cin, outs[4:4 + comm.n], outs[4 + comm.n:], l // tl, lambda: compute(*refs[:14], *outs[:4]))

    def compute(x_ref, dx1_ref, a_ref, b_ref, c_ref, e_ref, lr_ref, wa_ref, wb_ref, wc_ref, we_ref, wl_ref, g_ref, sc_ref,
                gx_ref, dsh_ref, dsc_ref, dg_ref):
        @pl.when(pl.program_id(0) == 0)
        def _():
            for r in (dsh_ref, dsc_ref, dg_ref):
                r[...] = jnp.zeros_like(r)

        dh = (_nt(a_ref[...], wa_ref[...]) + _nt(b_ref[...], wb_ref[...]) + _nt(c_ref[...], wc_ref[...]) + _nt(e_ref[...], we_ref[...])
              + _nt(lr_ref[...], wl_ref[...]))
        xv = x_ref[...]
        r = lax.rsqrt(jnp.mean(xv * xv, axis=-1, keepdims=True) + EPS)
        xn = xv * r
        gxn = jnp.sum(dh * xn, axis=0, keepdims=True)
        dsh_ref[...] += jnp.sum(dh, axis=0, keepdims=True)
        dsc_ref[...] += gxn * g_ref[...]
        dg_ref[...] += gxn * (1.0 + sc_ref[...])
        dxn = dh * (g_ref[...] * (1.0 + sc_ref[...]))
        gx_ref[...] = dx1_ref[...] + r * (dxn - xn * jnp.mean(dxn * xn, axis=-1, keepdims=True))

    row = lambda w: pl.BlockSpec((tl, w), lambda i: (i, 0))
    wcol = lambda w, j: pl.BlockSpec((D, w), lambda i, j=j: (0, j))
    vec = pl.BlockSpec((1, D), lambda i: (0, 0))
    return pl.pallas_call(
        body, name="in_bwd", grid=(l // tl,),
        out_shape=(jax.ShapeDtypeStruct((l, D), f32),) + (jax.ShapeDtypeStruct((1, D), f32),) * 3 + tuple(comm.outs),
        in_specs=[row(D), row(D), row(2 * KW + VW), row(2 * AW), row(AW), row(AW + 2 * D), row(LRW),
                  wcol(2 * KW + VW, 0), wcol(2 * AW, PZB // (2 * AW)), wcol(AW, PVA // AW), wcol(AW + 2 * D, PZA // (AW + 2 * D)),
                  wcol(LRW, PLR // LRW), vec, vec] + c_in,
        out_specs=(row(D), vec, vec, vec) + tuple(c_out), scratch_shapes=c_sems,
        compiler_params=_cp(("arbitrary",)),
    )(x, dx1, dqkv, dzbua, dva, dzag, dlr, w_pad, w_pad, w_pad, w_pad, w_pad, g, scale, *comm.ins)


def _tn_matmul(a, bs, tl, name, comm=None, pack=None):
    l, m = a.shape
    k = len(bs)
    comm = comm or _Comm([], [], None)
    c_in, c_out, c_sems = comm.specs()
    acc_shapes = [(m, b.shape[1]) for b in bs]
    n_extra = len(pack.extra) if pack else 0
    n_res = len(pack.outs) if pack else k

    def body(a_ref, *refs):
        b_refs, refs = refs[:k], refs[k:]
        extra, refs = refs[:n_extra], refs[n_extra:]
        cin, refs = refs[:len(c_in)], refs[len(c_in):]
        res, refs = refs[:n_res], refs[n_res:]
        cout, refs = refs[:comm.n], refs[comm.n:]
        accs, sems = (refs[:k], refs[k:]) if pack else (res, refs)

        def compute():
            @pl.when(pl.program_id(0) == 0)
            def _():
                for o_ref in accs:
                    o_ref[...] = jnp.zeros_like(o_ref)

            av = a_ref[...]
            for b_ref, o_ref in zip(b_refs, accs):
                o_ref[...] += _tn(av, b_ref[...])
            if pack:
                pl.when(pl.program_id(0) == l // tl - 1)(lambda: pack.fn(accs, extra, res))

        comm.run(cin, cout, sems, l // tl, compute)

    full = lambda shape: pl.BlockSpec(shape, lambda i: (0,) * len(shape))
    res_shapes = list(pack.outs) if pack else [jax.ShapeDtypeStruct(sh, f32) for sh in acc_shapes]
    return pl.pallas_call(
        body, name=name, grid=(l // tl,), out_shape=tuple(res_shapes) + tuple(comm.outs),
        in_specs=[pl.BlockSpec((tl, m), lambda i: (i, 0))] + [pl.BlockSpec((tl, b.shape[1]), lambda i: (i, 0)) for b in bs]
        + [full(e.shape) for e in (pack.extra if pack else [])] + c_in,
        out_specs=tuple(full(r.shape) for r in res_shapes) + tuple(c_out),
        scratch_shapes=([pltpu.VMEM(sh, f32) for sh in acc_shapes] if pack else []) + c_sems, compiler_params=_cp(("arbitrary",)),
    )(a, *bs, *(pack.extra if pack else []), *comm.ins)


def _pad_gate(w2, gb):
    z = jnp.zeros((RANK, KW), f32)
    tail = jnp.zeros((LRW - 2 * RANK, KW), f32)
    w2f = jnp.concatenate([w2[0], z, tail], axis=0)
    w2b = jnp.concatenate([z, w2[1], tail], axis=0)
    return w2f, w2b, gb[0:1], gb[1:2]


EARLY_A_ROWS = 512


class _NoExchange:
    def __init__(self, w_pa, w_pb, w_out):
        self.weights = (w_pa, w_pb, w_out)

    def gather_proj(self):
        return _Comm([], [], None)

    def proj_weights(self, got):
        return self.weights

    def first(self, dw_out, dw_pa, dw_pb, small):
        return _Comm([], [], None)

    def early_a(self, blocks):
        return _Comm([], [], None)

    def early_b(self, blocks):
        return _Comm([], [], None)

    def late(self, blocks, dgt):
        return _Comm([], [], None)


class _Exchanges:
    def __init__(self, pa, pb, wo):
        self.shards = (pa, pb, wo)

    def gather_proj(self):
        def plan(i, o):
            srcs = [lambda j, r=r: r for r in i]
            dsts = [lambda j: o[0].at[:, _lanes(j)], lambda j: o[1].at[:, _lanes(j)], lambda j: o[2].at[j]]
            return srcs, dsts, None
        sds = jax.ShapeDtypeStruct
        return _Comm(self.shards, [sds((AW, D), bf16), sds((VW, D), bf16), sds((NDEV, 128, D), bf16)], plan)

    def proj_weights(self, got):
        return got[0], got[1], got[2].reshape(D, D)

    def first(self, dw_out, dw_pa, dw_pb, small):
        def plan(i, o):
            srcs = [lambda j: i[0].at[j], lambda j: i[1].at[:, _lanes(j)], lambda j: i[2].at[:, _lanes(j)], lambda j: i[3]]
            dsts = [lambda j, r=r: r.at[j] for r in o]
            return srcs, dsts, None
        sds = jax.ShapeDtypeStruct
        return _Comm([dw_out.reshape(NDEV, 128, D), dw_pa, dw_pb, small],
                     [sds((NDEV, 128, D), bf16), sds((NDEV, AW, 128), bf16), sds((NDEV, VW, 128), bf16),
                      sds((NDEV,) + small.shape, f32)], plan)

    def early_a(self, blocks):
        def plan(i, o):
            return [lambda j: i[0].at[j]], [lambda j: o[0].at[j]], [lambda j: j >= LATE_DESTS - 1]
        return _Comm([blocks], [jax.ShapeDtypeStruct(blocks.shape, bf16)], plan)

    def early_b(self, blocks):
        def plan(i, o):
            return [lambda j: i[0].at[j]], [lambda j: o[0].at[j]], [lambda j: j >= LATE_DESTS - 1]
        return _Comm([blocks], [jax.ShapeDtypeStruct(blocks.shape, bf16)], plan)

    def late(self, blocks, dgt):
        return _LateComm(blocks, dgt)


def _local_step(x, ctx, tgt, mod, modc, norm_g, w_pad, ln_g, ln_b, ws, bs, w2, gb, gb_norm, gf, xch):
    shift, scale, gate = mod[:, 0:D], mod[:, D:2 * D], mod[:, 2 * D:]
    shift_c, scale_c = modc[:, 0:D], modc[:, D:]
    w2f, w2b, gbf, gbb = _pad_gate(w2, gb)

    p, h, *got_proj = _in_proj(x, norm_g, scale, shift, w_pad, 512, xch.gather_proj())
    w_pa, w_pb, w_out = xch.proj_weights(got_proj)
    sc_f, sc_b = _ctx_fwd(ctx, norm_g, scale_c, shift_c, w_pad, w2f, w2b, gbf, gbb)
    o_f, st_f = _gla_fwd(p, w2f, gbf, sc_f, False, 512, "gla_fwd_f")
    o_b, st_b = _gla_fwd(p, w2b, gbb, sc_b, True, 512, "gla_fwd_b")
    sv = _mix_fwd(p, ln_g, ln_b, ws.astype(bf16), bs.T)
    (dx1, dzbua, dzag, dsv, do, dw_out, dw_pa, dw_pb, dgf, dgate, dgbn, loss) = _mid(
        x, tgt, p, o_f, o_b, sv, gate, gf, gb_norm, w_pa, w_pb, w_out, 256)
    dva, dws, dbs_acc, dln_g, dln_b = _mix_bwd(p, dsv, ln_g, ln_b, jnp.swapaxes(ws, 1, 2).astype(bf16))
    small = _rows128(dln_g, dln_b, dws, jnp.sum(dbs_acc, axis=-1), dgbn, dgf, jnp.broadcast_to(loss, (1, 128)))
    blocks_a, blocks_b, *got_first = _tn_matmul(h, [dzbua, dva, dzag], 1024, "dw_early", xch.first(dw_out, dw_pa, dw_pb, small),
                                               _pack_early())

    dqkv_f, dlr_f, dw2f, dgbf, dsc_f, *got_a = _gla_bwd(p, do, st_f, w2f, gbf, None, False, 512, "gla_bwd_f",
                                                        xch.early_a(blocks_a))
    dqkv, dlr, dw2b, dgbb, dsc_b, *got_b = _gla_bwd(p, do, st_b, w2b, gbb, (dqkv_f, dlr_f), True, 512, "gla_bwd_b",
                                                    xch.early_b(blocks_b))
    dwk_c, dwv_c, dwl_c, dmodc, dg_c, dw2c, dgbc = _ctx_bwd(ctx, norm_g, scale_c, shift_c, w_pad, w2f, w2b, gbf, gbb, dsc_f, dsc_b)
    (blocks_late,) = _tn_matmul(h, [dqkv, dlr], 1024, "dw_qkv_lr", pack=_pack_late(dwk_c, dwv_c, dwl_c))
    dw2 = jnp.stack([dw2f[0:RANK] + dw2c[0, 0:RANK], dw2b[RANK:2 * RANK] + dw2c[1, RANK:2 * RANK]])
    dgb = jnp.concatenate([dgbf + dgbc[0], dgbb + dgbc[1]], axis=0)
    dgt = jnp.concatenate([jnp.transpose(dw2.reshape(2, RANK, NDEV, 32), (2, 0, 1, 3)).reshape(NDEV, 2 * RANK * 32),
                           jnp.transpose(dgb.reshape(2, NDEV, 32), (1, 0, 2)).reshape(NDEV, 64),
                           jnp.zeros((NDEV, 64), f32)], axis=1).reshape(NDEV, 9, 128)
    gx, dshift, dscale, dg, *got_late = _in_bwd(x, dx1, dqkv, dzbua, dva, dzag, dlr, w_pad, norm_g, scale, 512,
                                                xch.late(blocks_late, dgt))
    return dict(loss=loss, gx=gx, dmod=jnp.concatenate([dshift, dscale, dgate], axis=1), dmodc=dmodc, dnorm_g=dg + dg_c,
                small=small, blocks_a=blocks_a, blocks_b=blocks_b, blocks_late=blocks_late, dw2=dw2, dgb=dgb,
                dw_pa=dw_pa, dw_pb=dw_pb, dw_out=dw_out, got_first=got_first, got_a=got_a, got_b=got_b, got_late=got_late)


def _rows128(*vs):
    out = []
    for t in vs:
        t = t.reshape(-1)
        pad = (-t.shape[0]) % 128
        out.append(jnp.pad(t, (0, pad)) if pad else t)
    return jnp.concatenate(out).reshape(-1, 128)


def kernel(x, c, ctx, c_ctx, w_mod, b_mod, norm_g, w_in, a_ln_g, a_ln_b, a_ws, a_bs, b_gate_w2, b_gate_b, b_norm_g, w_proj_a, w_proj_b, w_out, final_norm_g, loss_target, m_c_ctx, m_w_mod, m_b_mod, m_norm_g, m_w_in, m_a_ln_g, m_a_ln_b, m_a_ws, m_a_bs, m_b_gate_w2, m_b_gate_b, m_b_norm_g, m_w_proj_a, m_w_proj_b, m_w_out, m_final_norm_g, v_c_ctx, v_w_mod, v_b_mod, v_norm_g, v_w_in, v_a_ln_g, v_a_ln_b, v_a_ws, v_a_bs, v_b_gate_w2, v_b_gate_b, v_b_norm_g, v_w_proj_a, v_w_proj_b, v_w_out, v_final_norm_g):
    me = _me()

    gate_mine = _rows128(jnp.concatenate([b_gate_w2.reshape(-1), b_gate_b.reshape(-1)]))
    cs, mods, wg, gates = _gather_first(c, c_ctx.reshape(1, D), w_mod[0], b_mod, w_in[0].astype(bf16), gate_mine)
    w_pad = _repack_w(wg)
    gflat = gates.reshape(NDEV, 9 * 128)
    w2 = jnp.transpose(gflat[:, 0:2 * RANK * 32].reshape(NDEV, 2, RANK, 32), (1, 2, 0, 3)).reshape(2, RANK, KW)
    gb = jnp.transpose(gflat[:, 2 * RANK * 32:2 * RANK * 32 + 64].reshape(NDEV, 2, 32), (1, 0, 2)).reshape(2, KW)

    mods = jnp.transpose(mods, (1, 0, 2)).reshape(16, 3 * D)
    mod = lax.dynamic_slice(mods, (me, 0), (1, 3 * D))
    modc = mods[8:9, 0:2 * D]

    xch = _Exchanges(w_proj_a[0].astype(bf16), w_proj_b[0].astype(bf16), w_out[0].astype(bf16))
    r = _local_step(x[0], ctx[0], loss_target[0], mod, modc, norm_g, w_pad, a_ln_g, a_ln_b, a_ws[0], a_bs[0], w2, gb,
                    b_norm_g, final_norm_g.reshape(1, D), xch)
    p_out, p_pa, p_pb, smalls_e = r["got_first"]
    (p_in_a,) = r["got_a"]
    (p_in_b,) = r["got_b"]
    _, _, p_in_late, p_gt = r["got_late"]

    n_e = (AW + AW + 4 * ACH * ACH + AW + VW + D) // 128
    row = lambda t: t.reshape(1, D)
    rep_e = _adam_params(smalls_e, [a_ln_g, a_ln_b, a_ws, a_bs, b_norm_g, row(final_norm_g)],
                         [m_a_ln_g, m_a_ln_b, m_a_ws, m_a_bs, m_b_norm_g, row(m_final_norm_g)],
                         [v_a_ln_g, v_a_ln_b, v_a_ws, v_a_bs, v_b_norm_g, row(v_final_norm_g)], "adam_rep_early")
    rep_e = [t[0:5] + (t[5].reshape(D),) for t in rep_e]
    loss = jnp.sum(smalls_e[:, n_e, 0])

    dbm = r["dmod"] + jnp.concatenate([r["dmodc"], jnp.zeros((1, D), f32)], axis=1)
    mod_grads = _mod_tail(r["dnorm_g"], dbm, r["dmod"], r["dmodc"], cs, w_mod)
    mod_res = _adam_plain(mod_grads, [w_mod, norm_g, b_mod, row(c_ctx)], [m_w_mod, m_norm_g, m_b_mod, row(m_c_ctx)],
                          [v_w_mod, v_norm_g, v_b_mod, row(v_c_ctx)], "adam_mod")
    wm, ng, bmod_r, cc = ([t[i] for t in mod_res] for i in range(4))

    a_in = _adam_w_in(p_in_a, p_in_b, p_in_late, w_in, m_w_in, v_w_in)
    blk = _adam_blocks([p_pa, p_pb, p_out], [w_proj_a, w_proj_b, w_out], [m_w_proj_a, m_w_proj_b, m_w_out],
                       [v_w_proj_a, v_w_proj_b, v_w_out], "adam_proj_out")
    a_pa, a_pb, a_out = ([t[i] for t in blk] for i in range(3))
    a_gt = _adam_gate(p_gt, b_gate_w2, b_gate_b, m_b_gate_w2, m_b_gate_b, v_b_gate_w2, v_b_gate_b)
    sh = [(a_in[k], a_pa[k], a_pb[k], a_out[k], a_gt[k], a_gt[4 + k]) for k in range(4)]

    outs = [loss, r["gx"][None]]
    for k in range(4):
        lg, lb, aws, abs_, bng, fng = rep_e[k]
        n_g, bmod = ng[k], bmod_r[k]
        s_in, s_pa, s_pb, s_out, s_w2, s_gb = sh[k]
        outs += [cc[k].reshape(D), wm[k], bmod, n_g, s_in, lg, lb, aws, abs_, s_w2, s_gb, bng, s_pa, s_pb, s_out, fng]
    return tuple(outs)
```

```python
import jax
import jax.numpy as jnp
from jax import lax
from jax.experimental import pallas as pl
from jax.experimental.pallas import tpu as pltpu

f32, bf16 = jnp.float32, jnp.bfloat16

D = 1024
CTX = 256
EPS = 1e-6
AW = 512
ACH = 128
GW = 64
KW = 256
VW = 512
NH = 4
HK = 64
HV = 128
RANK = 16
TAU = 16.0
CH = 64
QSCALE = HK ** -0.5
INW = 5152
NDEV = 8

PQ, PK, PV, PZB, PUA, PVA, PZA, PG1, PG2, PLR, PW = 0, 256, 512, 1024, 1536, 2048, 2560, 3072, 4096, 5120, 5248
LRW = 128

ADAM_LR, ADAM_B1, ADAM_B2, ADAM_EPS, ADAM_WD, ADAM_STEP = 0.001, 0.9, 0.999, 1e-08, 0.01, 10

VMEM_LIMIT = 56 * 1024 * 1024
MESH = pl.DeviceIdType.MESH


def _cp(sem=None):
    return pltpu.CompilerParams(dimension_semantics=sem, vmem_limit_bytes=VMEM_LIMIT)


def _dot(a, b):
    return jnp.dot(a.astype(bf16), b.astype(bf16), preferred_element_type=f32)


def _nt(a, b):
    return lax.dot_general(a.astype(bf16), b.astype(bf16), (((1,), (1,)), ((), ())), preferred_element_type=f32)


def _tn(a, b):
    return lax.dot_general(a.astype(bf16), b.astype(bf16), (((0,), (0,)), ((), ())), preferred_element_type=f32)


def _dot_hi(a, b):
    return jnp.dot(a, b, preferred_element_type=f32, precision=lax.Precision.HIGHEST)


def _sigmoid(x):
    return 1.0 / (1.0 + jnp.exp(-x))


def _log_sigmoid(x):
    return jnp.minimum(x, 0.0) - jnp.log(1.0 + jnp.exp(-jnp.abs(x)))


def _silu_and_grad(z):
    s = _sigmoid(z)
    return z * s, s * (1.0 + z * (1.0 - s))


def _me():
    return 4 * lax.axis_index("x") + 2 * lax.axis_index("y") + lax.axis_index("c")


def _peer(k):
    x, y, c = lax.axis_index("x"), lax.axis_index("y"), lax.axis_index("c")
    px = 1 - x if k & 4 else x
    py = 1 - y if k & 2 else y
    pc = 1 - c if k & 1 else c
    return (px, py, pc), 4 * px + 2 * py + pc


def _fanout(srcs, dsts, send_sems, recv_sems, local_sems, owners=None):
    me = _me()
    n = len(srcs)
    owns = lambda a, j: True if owners is None or owners[a] is None else owners[a](j)

    def guarded(cond, fn):
        if cond is True:
            fn()
        else:
            pl.when(cond)(fn)

    def copies(with_recvs):
        local = [pltpu.make_async_copy(srcs[a](me), dsts[a](me), local_sems.at[a]) for a in range(n)]
        sends, recvs = [], []
        for k in range(1, NDEV):
            dev, idx = _peer(k)
            for a in range(n):
                s = (k - 1) * n + a
                sends.append((owns(a, idx), pltpu.make_async_remote_copy(
                    src_ref=srcs[a](idx), dst_ref=dsts[a](me), send_sem=send_sems.at[s], recv_sem=recv_sems.at[s],
                    device_id=dev, device_id_type=MESH)))
                if with_recvs:
                    recvs.append((owns(a, me), pltpu.make_async_remote_copy(
                        src_ref=srcs[a](idx), dst_ref=dsts[a](idx), send_sem=send_sems.at[s], recv_sem=recv_sems.at[s],
                        device_id=dev, device_id_type=MESH)))
        return local, sends, recvs

    def start():
        local, sends, _ = copies(False)
        for a, cp in enumerate(local):
            guarded(owns(a, me), cp.start)
        for cond, cp in sends:
            guarded(cond, cp.start)

    def finish():
        local, sends, recvs = copies(True)
        for cond, cp in recvs:
            guarded(cond, cp.wait_recv)
        for cond, cp in sends:
            guarded(cond, cp.wait_send)
        for a, cp in enumerate(local):
            guarded(owns(a, me), cp.wait)

    return start, finish


class _Comm:
    def __init__(self, ins, outs, plan):
        self.ins, self.outs, self.plan = list(ins), list(outs), plan
        self.n = len(self.outs)

    def specs(self):
        hbm = pl.BlockSpec(memory_space=pl.ANY)
        return [hbm] * len(self.ins), [hbm] * self.n, _fanout_sems(self.n) if self.n else []

    def run(self, in_refs, out_refs, sems, nsteps, compute):
        if not self.n:
            compute()
            return

        def hooks():
            srcs, dsts, owners = self.plan(in_refs, out_refs)
            return _fanout(srcs, dsts, sems[0], sems[1], sems[2], owners)

        pl.when(pl.program_id(0) == 0)(lambda: hooks()[0]())
        compute()
        pl.when(pl.program_id(0) == nsteps - 1)(lambda: hooks()[1]())


LATE_MID_STEP = 1


class _LateComm:
    def __init__(self, blocks, dgt):
        sds = jax.ShapeDtypeStruct
        self.ins = [blocks, dgt]
        self.outs = [sds((D, SHARD), bf16), sds((D, SHARD), bf16), sds((4, D, SHARD), bf16), sds(dgt.shape, f32)]
        self.n = len(self.outs)

    def specs(self):
        hbm = pl.BlockSpec(memory_space=pl.ANY)
        scratch = [pltpu.VMEM((3, D, SHARD), bf16), pltpu.SemaphoreType.DMA((2,)), pltpu.SemaphoreType.DMA((4,)),
                   pltpu.SemaphoreType.DMA((3,))] + _fanout_sems(1)
        return [hbm] * 2, [hbm] * self.n, scratch

    def run(self, in_refs, out_refs, scratch, nsteps, compute):
        late_ref, dgt_ref = in_refs
        sib_ref, pair_ref, parts_ref, gt_ref = out_refs
        vbuf, send_sems, recv_sems, local_sems, g_send, g_recv, g_local = scratch
        x, y, c = lax.axis_index("x"), lax.axis_index("y"), lax.axis_index("c")
        chip = 2 * x + y
        is_owner_chip = chip == 0
        step = pl.program_id(0)

        def to_sibling():
            return pltpu.make_async_remote_copy(src_ref=late_ref.at[1 - c], dst_ref=sib_ref, send_sem=send_sems.at[0],
                                                recv_sem=recv_sems.at[0], device_id=(x, y, 1 - c), device_id_type=MESH)

        def to_owner(k):
            return pltpu.make_async_remote_copy(src_ref=pair_ref, dst_ref=parts_ref.at[k], send_sem=send_sems.at[1],
                                                recv_sem=recv_sems.at[k], device_id=(0, 0, c), device_id_type=MESH)

        def own_copy():
            return pltpu.make_async_copy(pair_ref, parts_ref.at[0], local_sems.at[2])

        def gates():
            return _fanout([lambda j: dgt_ref.at[j]], [lambda j: gt_ref.at[j]], g_send, g_recv, g_local)

        @pl.when(step == 0)
        def _():
            to_sibling().start()
            gates()[0]()

        compute()

        @pl.when(step == LATE_MID_STEP)
        def _():
            mine = pltpu.make_async_copy(late_ref.at[c], vbuf.at[0], local_sems.at[0])
            mine.start()
            to_sibling().wait_recv()
            theirs = pltpu.make_async_copy(sib_ref, vbuf.at[1], local_sems.at[1])
            theirs.start()
            mine.wait()
            theirs.wait()
            vbuf[2] = (vbuf[0].astype(f32) + vbuf[1].astype(f32)).astype(bf16)
            pltpu.sync_copy(vbuf.at[2], pair_ref)
            pl.when(is_owner_chip)(lambda: own_copy().start())
            pl.when(jnp.logical_not(is_owner_chip))(lambda: to_owner(chip).start())

        @pl.when(step == nsteps - 1)
        def _():
            @pl.when(is_owner_chip)
            def _():
                for k in range(1, 4):
                    to_owner(k).wait_recv()
                own_copy().wait()

            pl.when(jnp.logical_not(is_owner_chip))(lambda: to_owner(chip).wait_send())
            to_sibling().wait_send()
            gates()[1]()


def _fanout_sems(n):
    return [pltpu.SemaphoreType.DMA(((NDEV - 1) * n,)), pltpu.SemaphoreType.DMA(((NDEV - 1) * n,)), pltpu.SemaphoreType.DMA((n,))]


def _lanes(j):
    return pl.ds(pl.multiple_of(j * 128, 128), 128)


def _gather_first(c_row, cctx_row, wm, bm, wi, gate):
    def body(c_ref, cctx_ref, wm_ref, bm_ref, wi_ref, g_ref, cs_ref, mods_ref, owi, og, call_ref, mine_ref,
             send_sems, recv_sems, local_sems, c_send, c_recv, c_local, m_send, m_recv, m_local):
        x, y, c = lax.axis_index("x"), lax.axis_index("y"), lax.axis_index("c")
        sibling = (x, y, 1 - c)
        chips = [(1 - x, y), (x, 1 - y), (1 - x, 1 - y)]
        index = lambda px, py, pc: 4 * px + 2 * py + pc
        arrays = ((wi_ref, owi), (g_ref, og))
        n = len(arrays)

        def copy(a, k, block, to, own=False):
            src, out = arrays[a]
            return pltpu.make_async_remote_copy(
                src_ref=src if own else out.at[index(*block)], dst_ref=out.at[index(*block)],
                send_sem=send_sems.at[k * n + a], recv_sem=recv_sems.at[k * n + a], device_id=to, device_id_type=MESH)

        c_start, c_finish = _fanout([lambda j: c_ref], [lambda j: call_ref.at[j]], c_send, c_recv, c_local)
        c_start()
        mine = [pltpu.make_async_copy(src, out.at[index(x, y, c)], local_sems.at[a]) for a, (src, out) in enumerate(arrays)]
        first = [copy(a, 0, (x, y, c), sibling, own=True) for a in range(n)]
        first += [copy(a, 1 + j, (x, y, c), (*chip, c), own=True) for j, chip in enumerate(chips) for a in range(n)]
        for cp in mine + first:
            cp.start()

        c_finish()
        cs = jnp.concatenate([call_ref[j] for j in range(NDEV)] + [cctx_ref[...], jnp.zeros((16 - NDEV - 1, D), f32)], axis=0)
        cs_ref[...] = cs
        s, _ = _silu_and_grad(cs)
        mine_ref[...] = _dot_hi(s, wm_ref[...]) + bm_ref[:, pl.ds(pl.multiple_of(_me() * ncol, 128), ncol)]
        m_start, m_finish = _fanout([lambda j: mine_ref], [lambda j: mods_ref.at[j]], m_send, m_recv, m_local)
        m_start()

        passed = []
        for j, chip in enumerate(chips):
            for a in range(n):
                copy(a, 1 + j, (*chip, c), (x, y, c)).wait_recv()
            for a in range(n):
                cp = copy(a, 4 + j, (*chip, c), sibling)
                cp.start()
                passed.append(cp)
        for a in range(n):
            copy(a, 0, sibling, (x, y, c)).wait_recv()
        for j, chip in enumerate(chips):
            for a in range(n):
                copy(a, 4 + j, (*chip, 1 - c), (x, y, c)).wait_recv()
        for cp in first + passed:
            cp.wait_send()
        for cp in mine:
            cp.wait()
        m_finish()

    hbm = pl.BlockSpec(memory_space=pl.ANY)
    vm = pl.BlockSpec(memory_space=pltpu.VMEM)
    ncol = wm.shape[1]
    return pl.pallas_call(
        body, name="gather_first",
        out_shape=(jax.ShapeDtypeStruct((16, D), f32), jax.ShapeDtypeStruct((NDEV, 16, ncol), f32),
                   jax.ShapeDtypeStruct((NDEV,) + wi.shape, bf16), jax.ShapeDtypeStruct((NDEV,) + gate.shape, f32)),
        in_specs=[vm, vm, vm, vm, hbm, hbm], out_specs=(vm, vm, hbm, hbm),
        scratch_shapes=[pltpu.VMEM((NDEV, 1, D), f32), pltpu.VMEM((16, ncol), f32)] + _fanout_sems(2) + _fanout_sems(1) + _fanout_sems(1),
        compiler_params=_cp(),
    )(c_row, cctx_row, wm, bm, wi, gate)


SHARD = INW // NDEV
ROWS_RP = 128


def _overlap(lo, hi, a, b):
    s, e = max(lo, a), min(hi, b)
    return (s, e) if s < e else None


def _repack_w(wg):
    segs = ((0, 1024, PQ), (1024, 1024 + 2 * RANK, PLR), (1024 + 2 * RANK, INW, PZB))

    def body(g_ref, o_ref):
        for j in range(NDEV):
            lo, hi = j * SHARD, (j + 1) * SHARD
            for a, b, pad0 in segs:
                ov = _overlap(lo, hi, a, b)
                if ov:
                    s, e = ov
                    o_ref[:, pad0 + s - a:pad0 + e - a] = g_ref[j, :, s - lo:e - lo]
        o_ref[:, PLR + 2 * RANK:PW] = jnp.zeros((ROWS_RP, PW - PLR - 2 * RANK), bf16)

    return pl.pallas_call(
        body, name="repack_w", grid=(D // ROWS_RP,), out_shape=jax.ShapeDtypeStruct((D, PW), bf16),
        in_specs=[pl.BlockSpec((NDEV, ROWS_RP, SHARD), lambda i: (0, i, 0))],
        out_specs=pl.BlockSpec((ROWS_RP, PW), lambda i: (i, 0)), compiler_params=_cp(("arbitrary",)),
    )(wg)


LATE_END = 1024 + 2 * RANK
LATE_DESTS = 2


def _pack_blocks(o_ref, srcs, dests, dtype):
    for n, j in enumerate(dests):
        lo, hi = j * SHARD, (j + 1) * SHARD
        done = lo
        for a, b, src in srcs:
            ov = _overlap(lo, hi, a, b)
            if ov:
                s, e = ov
                if s > done:
                    o_ref[n, :, done - lo:s - lo] = jnp.zeros((ROWS_RP, s - done), dtype)
                o_ref[n, :, s - lo:e - lo] = src[:, s - a:e - a].astype(dtype)
                done = e
        if done < hi:
            o_ref[n, :, done - lo:hi - lo] = jnp.zeros((ROWS_RP, hi - done), dtype)


class _Pack:
    def __init__(self, extra, outs, fn):
        self.extra, self.outs, self.fn = list(extra), list(outs), fn


def _pack_early():
    def fn(accs, extra, outs):
        zbua, va, zag = accs
        for r0 in range(0, D, ROWS_RP):
            rows = pl.ds(r0, ROWS_RP)
            out, o0 = (outs[0], r0) if r0 < EARLY_A_ROWS else (outs[1], r0 - EARLY_A_ROWS)
            _pack_blocks(out.at[:, pl.ds(o0, ROWS_RP)], ((LATE_END, 2080, zbua.at[rows]), (2080, 2592, va.at[rows]), (2592, INW, zag.at[rows])),
                         range(NDEV), bf16)

    sds = jax.ShapeDtypeStruct
    return _Pack([], [sds((NDEV, EARLY_A_ROWS, SHARD), bf16), sds((NDEV, D - EARLY_A_ROWS, SHARD), bf16)], fn)


def _pack_late(dwk_c, dwv_c, dwl_c):
    def fn(accs, extra, outs):
        qkv_ref, lr_ref = accs
        kc_ref, vc_ref, lc_ref = extra
        for r0 in range(0, D, ROWS_RP):
            rows = pl.ds(r0, ROWS_RP)
            qkv = qkv_ref[rows, :] + jnp.concatenate([jnp.zeros((ROWS_RP, KW), f32), kc_ref[rows, :], vc_ref[rows, :]], axis=1)
            lr = lr_ref[rows, :] + lc_ref[rows, :]
            _pack_blocks(outs[0].at[:, rows], ((0, 1024, qkv), (1024, LATE_END, lr)), range(LATE_DESTS), bf16)

    return _Pack([dwk_c, dwv_c, dwl_c], [jax.ShapeDtypeStruct((LATE_DESTS, D, SHARD), bf16)], fn)


def _mod_tail(dnorm, dbm, dmod, dmodc, cs, wm):
    ncol = wm.shape[2]

    def body(dn_ref, dbm_ref, dmod_ref, dmodc_ref, cs_ref, wm_ref, g_wm, g_ng, g_bm, g_cc, *rest):
        a_dn, a_dbm, a_dmod, a_dmodc, dm_rows, gc_ref, a_gc = rest[0:7]
        s1, r1, l1, s2, r2, l2 = rest[7:13]
        start, finish = _fanout([lambda j: dn_ref, lambda j: dbm_ref, lambda j: dmod_ref, lambda j: dmodc_ref],
                                [lambda j, r=r: r.at[j] for r in (a_dn, a_dbm, a_dmod, a_dmodc)], s1, r1, l1)
        start()
        finish()

        def total(ref):
            t = ref[0]
            for j in range(1, NDEV):
                t = t + ref[j]
            return t

        g_ng[...] = total(a_dn)
        g_bm[...] = total(a_dbm)
        dmodc_tot = jnp.concatenate([total(a_dmodc), jnp.zeros((1, D), f32)], axis=1)
        dm_rows[...] = jnp.concatenate([a_dmod[j] for j in range(NDEV)] + [dmodc_tot, jnp.zeros((16 - NDEV - 1, 3 * D), f32)], axis=0)
        dm = dm_rows[:, pl.ds(pl.multiple_of(_me() * ncol, 128), ncol)]
        s, ds = _silu_and_grad(cs_ref[...])
        part = lax.dot_general(dm[8:9, :], wm_ref[0], (((1,), (1,)), ((), ())), preferred_element_type=f32,
                               precision=lax.Precision.HIGHEST)
        gc_ref[...] = part * ds[8:9, :]
        start2, finish2 = _fanout([lambda j: gc_ref], [lambda j: a_gc.at[j]], s2, r2, l2)
        start2()
        g = lax.dot_general(s, dm, (((0,), (0,)), ((), ())), preferred_element_type=f32, precision=lax.Precision.HIGHEST)
        g_wm[...] = g[None]
        finish2()
        g_cc[...] = total(a_gc)

    sds = jax.ShapeDtypeStruct
    row = lambda n: pltpu.VMEM((NDEV, 1, n), f32)
    return pl.pallas_call(
        body, name="mod_tail", out_shape=(sds(wm.shape, f32), sds((1, D), f32), sds((1, 3 * D), f32), sds((1, D), f32)),
        scratch_shapes=[row(D), row(3 * D), row(3 * D), row(2 * D), pltpu.VMEM((16, 3 * D), f32), pltpu.VMEM((1, D), f32), row(D)]
        + _fanout_sems(4) + _fanout_sems(1),
        compiler_params=_cp(),
    )(dnorm, dbm, dmod, dmodc, cs, wm)


def _adam_update(g, w_ref, m_ref, v_ref, go_ref, d_ref, mo_ref, vo_ref):
    c1 = 1.0 / (1.0 - ADAM_B1 ** ADAM_STEP)
    c2 = 1.0 / (1.0 - ADAM_B2 ** ADAM_STEP)
    mn = ADAM_B1 * m_ref[...] + (1.0 - ADAM_B1) * g
    vn = ADAM_B2 * v_ref[...] + (1.0 - ADAM_B2) * (g * g)
    go_ref[...] = g
    mo_ref[...] = mn
    vo_ref[...] = vn
    d_ref[...] = -ADAM_LR * ((mn * c1) / (jnp.sqrt(vn * c2) + ADAM_EPS) + ADAM_WD * w_ref[...])


def _adam_w_in(parts_a, parts_b, parts_late, w, m, v):
    na = EARLY_A_ROWS // ROWS_RP
    n = D // ROWS_RP
    whole = SHARD // 128 * 128

    def body(a_ref, b_ref, l_ref, w_hbm, m_hbm, v_hbm, go_hbm, d_hbm, mo_hbm, vo_hbm, ibuf, obuf, isem, osem):
        step = pl.program_id(0)

        def cols(t):
            return pl.ds(pl.multiple_of(t * ROWS_RP, ROWS_RP), ROWS_RP)

        def fetch(t):
            return [pltpu.make_async_copy(r.at[:, 0, cols(t)], ibuf.at[t % 2, k], isem.at[t % 2, k])
                    for k, r in enumerate((w_hbm, m_hbm, v_hbm))]

        def flush(t):
            return [pltpu.make_async_copy(obuf.at[t % 2, k], r.at[:, 0, cols(t)], osem.at[t % 2, k])
                    for k, r in enumerate((go_hbm, d_hbm, mo_hbm, vo_hbm))]

        def start(cps):
            for cp in cps:
                cp.start()

        def wait(cps):
            for cp in cps:
                cp.wait()

        pl.when(step == 0)(lambda: start(fetch(step)))
        pl.when(step + 1 < n)(lambda: start(fetch(step + 1)))
        me = _me()
        first = step < na
        early = jnp.where(first, a_ref[0], b_ref[0]).astype(f32)
        for i in range(1, NDEV):
            early = early + jnp.where(first, a_ref[i], b_ref[i]).astype(f32)
        late = l_ref[0].astype(f32)
        for i in range(1, 4):
            late = late + l_ref[i].astype(f32)
        g = jnp.where(me >= LATE_DESTS - 1, early, 0.0) + jnp.where(me < LATE_DESTS, late, 0.0)
        gt = jnp.concatenate([g[:, c:c + 128].T for c in range(0, whole, 128)] + [g[:, SHARD - 128:].T[128 - (SHARD - whole):]],
                             axis=0)
        wait(fetch(step))
        pl.when(step >= 2)(lambda: wait(flush(step - 2)))
        slot = step % 2
        _adam_update(gt, *(ibuf.at[slot, k] for k in range(3)), *(obuf.at[slot, k] for k in range(4)))
        start(flush(step))

        @pl.when(step == n - 1)
        def _():
            wait(flush(step - 1))
            wait(flush(step))

    hbm = pl.BlockSpec(memory_space=pl.ANY)
    res = pl.pallas_call(
        body, name="adam_w_in", grid=(n,), out_shape=tuple(jax.ShapeDtypeStruct((SHARD, 1, D), f32) for _ in range(4)),
        in_specs=[pl.BlockSpec((NDEV, ROWS_RP, SHARD), lambda i: (0, jnp.minimum(i, na - 1), 0)),
                  pl.BlockSpec((NDEV, ROWS_RP, SHARD), lambda i: (0, jnp.maximum(i - na, 0), 0)),
                  pl.BlockSpec((4, ROWS_RP, SHARD), lambda i: (0, i, 0)), hbm, hbm, hbm],
        out_specs=(hbm, hbm, hbm, hbm),
        scratch_shapes=[pltpu.VMEM((2, 3, SHARD, ROWS_RP), f32), pltpu.VMEM((2, 4, SHARD, ROWS_RP), f32),
                        pltpu.SemaphoreType.DMA((2, 3)), pltpu.SemaphoreType.DMA((2, 4))],
        compiler_params=_cp(("arbitrary",)),
    )(parts_a, parts_b, parts_late, *(jnp.transpose(t, (2, 0, 1)) for t in (w, m, v)))
    return tuple(jnp.transpose(t, (1, 2, 0)) for t in res)


def _adam_params(parts, ws, ms, vs, name):
    p = parts.shape[0]
    k = len(ws)
    nrows = [w.size // 128 for w in ws]
    starts = [sum(nrows[:i]) for i in range(k)]

    def shaped(g, shape):
        if len(shape) == 2:
            return jnp.concatenate([g[r:r + 1] for r in range(g.shape[0])], axis=1)
        return g.reshape(shape)

    def body(*refs):
        g_ref = refs[0]
        w_refs, m_refs, v_refs = refs[1:1 + k], refs[1 + k:1 + 2 * k], refs[1 + 2 * k:1 + 3 * k]
        outs = refs[1 + 3 * k:]
        for i in range(k):
            rows = slice(starts[i], starts[i] + nrows[i])
            g = g_ref[0, rows, :]
            for j in range(1, p):
                g = g + g_ref[j, rows, :]
            _adam_update(shaped(g, ws[i].shape), w_refs[i], m_refs[i], v_refs[i], outs[i], outs[k + i], outs[2 * k + i], outs[3 * k + i])

    vm = pl.BlockSpec(memory_space=pltpu.VMEM)
    res = pl.pallas_call(
        body, name=name, out_shape=tuple(jax.ShapeDtypeStruct(w.shape, f32) for _ in range(4) for w in ws),
        in_specs=[vm] * (1 + 3 * k), out_specs=(vm,) * (4 * k), compiler_params=_cp(),
    )(parts, *ws, *ms, *vs)
    return [res[i * k:(i + 1) * k] for i in range(4)]


def _adam_blocks(parts, ws, ms, vs, name):
    k = len(ws)

    def body(*refs):
        g_refs, w_refs, m_refs, v_refs = refs[0:k], refs[k:2 * k], refs[2 * k:3 * k], refs[3 * k:4 * k]
        outs = refs[4 * k:]
        for i in range(k):
            g = g_refs[i][0].astype(f32)
            for j in range(1, parts[i].shape[0]):
                g = g + g_refs[i][j].astype(f32)
            _adam_update(g[None], w_refs[i], m_refs[i], v_refs[i], outs[i], outs[k + i], outs[2 * k + i], outs[3 * k + i])

    vm = pl.BlockSpec(memory_space=pltpu.VMEM)
    res = pl.pallas_call(
        body, name=name, out_shape=tuple(jax.ShapeDtypeStruct(w.shape, f32) for _ in range(4) for w in ws),
        in_specs=[vm] * (4 * k), out_specs=(vm,) * (4 * k), compiler_params=_cp(),
    )(*parts, *ws, *ms, *vs)
    return [res[i * k:(i + 1) * k] for i in range(4)]


def _adam_plain(gs, ws, ms, vs, name):
    k = len(ws)

    def body(*refs):
        g_refs, w_refs, m_refs, v_refs = refs[0:k], refs[k:2 * k], refs[2 * k:3 * k], refs[3 * k:4 * k]
        outs = refs[4 * k:]
        for i in range(k):
            _adam_update(g_refs[i][...], w_refs[i], m_refs[i], v_refs[i], outs[i], outs[k + i], outs[2 * k + i], outs[3 * k + i])

    vm = pl.BlockSpec(memory_space=pltpu.VMEM)
    res = pl.pallas_call(
        body, name=name, out_shape=tuple(jax.ShapeDtypeStruct(w.shape, f32) for _ in range(4) for w in ws),
        in_specs=[vm] * (4 * k), out_specs=(vm,) * (4 * k), compiler_params=_cp(),
    )(*gs, *ws, *ms, *vs)
    return [res[i * k:(i + 1) * k] for i in range(4)]


def _adam_gate(parts, w2, gb, m_w2, m_gb, v_w2, v_gb):
    per = 128 // 32

    def body(p_ref, w2_ref, gb_ref, mw_ref, mb_ref, vw_ref, vb_ref, *outs):
        g = p_ref[0]
        for j in range(1, parts.shape[0]):
            g = g + p_ref[j]
        for d in range(2):
            gd = jnp.concatenate([g[(d * RANK + r) // per:(d * RANK + r) // per + 1, (r % per) * 32:(r % per + 1) * 32]
                                  for r in range(RANK)], axis=0)
            _adam_update(gd, *(t.at[0, d] for t in (w2_ref, mw_ref, vw_ref) + outs[0:4]))
        last = 2 * RANK // per
        gbv = jnp.concatenate([g[last:last + 1, d * 32:(d + 1) * 32] for d in range(2)], axis=0)
        _adam_update(gbv, *(t.at[0] for t in (gb_ref, mb_ref, vb_ref) + outs[4:8]))

    vm = pl.BlockSpec(memory_space=pltpu.VMEM)
    return pl.pallas_call(
        body, name="adam_gate", out_shape=tuple(jax.ShapeDtypeStruct(w.shape, f32) for w in (w2, gb) for _ in range(4)),
        in_specs=[vm] * 7, out_specs=(vm,) * 8, compiler_params=_cp(),
    )(parts, w2, gb, m_w2, m_gb, v_w2, v_gb)


def _in_proj(x, g, scale, shift, w_pad, tl, comm):
    l = x.shape[0]
    c_in, c_out, c_sems = comm.specs()

    def body(*refs):
        x_ref, g_ref, sc_ref, sh_ref, w_ref = refs[:5]
        cin = refs[5:5 + len(c_in)]
        p_ref, h_ref = refs[5 + len(c_in):7 + len(c_in)]
        cout = refs[7 + len(c_in):7 + len(c_in) + comm.n]
        sems = refs[7 + len(c_in) + comm.n:]

        def compute():
            xv = x_ref[...]
            r = lax.rsqrt(jnp.mean(xv * xv, axis=-1, keepdims=True) + EPS)
            h = (xv * r) * (g_ref[...] * (1.0 + sc_ref[...])) + sh_ref[...]
            hb = h.astype(bf16)
            h_ref[...] = hb
            p_ref[...] = jnp.dot(hb, w_ref[...], preferred_element_type=f32).astype(bf16)

        comm.run(cin, cout, sems, l // tl, compute)

    vec = pl.BlockSpec((1, D), lambda i: (0, 0))
    return pl.pallas_call(
        body, name="in_proj", grid=(l // tl,),
        out_shape=(jax.ShapeDtypeStruct((l, PW), bf16), jax.ShapeDtypeStruct((l, D), bf16)) + tuple(comm.outs),
        in_specs=[pl.BlockSpec((tl, D), lambda i: (i, 0)), vec, vec, vec, pl.BlockSpec((D, PW), lambda i: (0, 0))] + c_in,
        out_specs=(pl.BlockSpec((tl, PW), lambda i: (i, 0)), pl.BlockSpec((tl, D), lambda i: (i, 0))) + tuple(c_out),
        scratch_shapes=c_sems, compiler_params=_cp(("arbitrary",)),
    )(x, g, scale, shift, w_pad, *comm.ins)


def _tri(rev):
    i = lax.broadcasted_iota(jnp.int32, (CH, CH), 0)
    j = lax.broadcasted_iota(jnp.int32, (CH, CH), 1)
    return jnp.where((j >= i) if rev else (j <= i), 1.0, 0.0).astype(f32)


def _head_masks():
    lane = lax.broadcasted_iota(jnp.int32, (1, KW), 1) // HK
    return [jnp.where(lane == h, 1.0, 0.0).astype(f32) for h in range(NH)]


def _block_diag():
    r = lax.broadcasted_iota(jnp.int32, (VW, KW), 0) // HV
    c = lax.broadcasted_iota(jnp.int32, (VW, KW), 1) // HK
    return jnp.where(r == c, 1.0, 0.0).astype(f32)


def _decay(lr, w2, gb, tri, rev):
    logits = _dot(lr, w2) + gb
    a = _log_sigmoid(logits) * (1.0 / TAU)
    c = _dot_hi(tri, a)
    cl = c[0:1, :] if rev else c[CH - 1:CH, :]
    return logits, c, cl


def _state_fwd(k, v, c, cl, st, bd):
    return st * jnp.exp(cl) + bd * _tn(v, k * jnp.exp(cl - c))


def _state_bwd(k, v, c, cl, st0, dst, trit):
    ecl = jnp.exp(cl)
    edec = jnp.exp(cl - c)
    kdec = k * edec
    dv = _nt(kdec, dst)
    dkdec = _dot(v, dst)
    dcl = jnp.sum(dst * st0, axis=0, keepdims=True) * ecl + jnp.sum(dkdec * kdec, axis=0, keepdims=True)
    da = _dot_hi(trit, -dkdec * kdec) + dcl
    return dkdec * edec, dv, da, dst * ecl


def _bdot(a, b):
    return lax.dot_general(a.astype(bf16), b.astype(bf16), (((2,), (1,)), ((0,), (0,))), preferred_element_type=f32)


def _bnt(a, b):
    return lax.dot_general(a.astype(bf16), b.astype(bf16), (((2,), (2,)), ((0,), (0,))), preferred_element_type=f32)


def _btn(a, b):
    return lax.dot_general(a.astype(bf16), b.astype(bf16), (((1,), (1,)), ((0,), (0,))), preferred_element_type=f32)


def _scan_chunks(x, rev):
    nc = x.shape[0]
    hi = x.astype(bf16)
    r1 = x - hi.astype(f32)
    mid = r1.astype(bf16)
    lo = (r1 - mid.astype(f32)).astype(bf16)
    terms = jnp.concatenate([hi, mid, lo], axis=1)
    tri3 = jnp.broadcast_to(jnp.concatenate([_tri(rev)] * 3, axis=1).astype(bf16)[None], (nc, CH, 3 * CH))
    return lax.dot_general(tri3, terms, (((2,), (1,)), ((0,), (0,))), preferred_element_type=f32)


class _Tile:
    pass


def _tile_prep(q_ref, k_ref, lr_ref, w2_ref, gb_ref, rev, nc):
    t = _Tile()
    tg = nc * CH
    t.logits = _dot(lr_ref[...], w2_ref[...]) + gb_ref[...]
    c = _scan_chunks((_log_sigmoid(t.logits) * (1.0 / TAU)).reshape(nc, CH, KW), rev)
    cl = c[:, 0:1, :] if rev else c[:, CH - 1:CH, :]
    k = k_ref[...].astype(f32).reshape(nc, CH, KW)
    t.ec, t.enc, t.edec, t.ecl = jnp.exp(c), jnp.exp(-c), jnp.exp(cl - c), jnp.exp(cl)
    t.qd = q_ref[...].astype(f32).reshape(nc, CH, KW) * t.ec * QSCALE
    t.kd = k * t.enc
    t.kdec = k * t.edec
    hm = _head_masks()
    t.tri4 = jnp.concatenate([_tri(rev)] * NH, axis=0)[None]
    t.qs = jnp.concatenate([t.qd * hm[h] for h in range(NH)], axis=1)
    t.pst = _bnt(t.qs, t.kd) * t.tri4
    return t


def _gla_fwd(p, w2p, gb, s0, rev, tg, name):
    l = p.shape[0]
    nb, nc = l // tg, tg // CH

    def body(q_ref, k_ref, v_ref, lr_ref, w2_ref, gb_ref, s0_ref, o_ref, st_ref, st):
        @pl.when(pl.program_id(0) == 0)
        def _():
            st[...] = s0_ref[...]

        t = _tile_prep(q_ref, k_ref, lr_ref, w2_ref, gb_ref, rev, nc)
        v = v_ref[...].reshape(nc, CH, VW)
        intra = jnp.concatenate([_bdot(t.pst[:, h * CH:(h + 1) * CH], v[:, :, h * HV:(h + 1) * HV]) for h in range(NH)], axis=2)
        bd = _block_diag()
        s = st[...]
        for n in (range(nc - 1, -1, -1) if rev else range(nc)):
            st_ref[n] = s.astype(bf16)
            s = s * t.ecl[n] + bd * _tn(v[n], t.kdec[n])
        st[...] = s
        o_ref[...] = (_bnt(t.qd, st_ref[...]) + intra).reshape(tg, VW).astype(bf16)

    blk = (lambda i: nb - 1 - i) if rev else (lambda i: i)
    return pl.pallas_call(
        body, name=name, grid=(nb,),
        out_shape=(jax.ShapeDtypeStruct((l, VW), bf16), jax.ShapeDtypeStruct((l // CH, VW, KW), bf16)),
        in_specs=[pl.BlockSpec((tg, KW), lambda i: (blk(i), PQ // KW)), pl.BlockSpec((tg, KW), lambda i: (blk(i), PK // KW)),
                  pl.BlockSpec((tg, VW), lambda i: (blk(i), PV // VW)), pl.BlockSpec((tg, LRW), lambda i: (blk(i), PLR // LRW)),
                  pl.BlockSpec((LRW, KW), lambda i: (0, 0)), pl.BlockSpec((1, KW), lambda i: (0, 0)),
                  pl.BlockSpec((VW, KW), lambda i: (0, 0))],
        out_specs=(pl.BlockSpec((tg, VW), lambda i: (blk(i), 0)), pl.BlockSpec((nc, VW, KW), lambda i: (blk(i), 0, 0))),
        scratch_shapes=[pltpu.VMEM((VW, KW), f32)],
        compiler_params=_cp(("arbitrary",)),
    )(p, p, p, p, w2p, gb, s0)


def _gla_bwd(p, do, states, w2p, gb, prev, rev, tg, name, comm):
    l = p.shape[0]
    nb, nc = l // tg, tg // CH
    out_dt = bf16
    c_in, c_out, c_sems = comm.specs()

    def body(*refs):
        q_ref, k_ref, v_ref, lr_ref, do_ref, st_ref, w2_ref, gb_ref = refs[:8]
        refs = refs[8:]
        if prev is not None:
            pq_ref, pl_ref = refs[:2]
            refs = refs[2:]
        cin, refs = refs[:len(c_in)], refs[len(c_in):]
        dqkv_ref, dlr_ref, dw2_ref, dgb_ref, ds0_ref = refs[:5]
        cout, dst, ds_buf, sems = refs[5:5 + comm.n], refs[5 + comm.n], refs[6 + comm.n], refs[7 + comm.n:]
        comm.run(cin, cout, sems, nb, lambda: compute(q_ref, k_ref, v_ref, lr_ref, do_ref, st_ref, w2_ref, gb_ref,
                                                      pq_ref if prev is not None else None, pl_ref if prev is not None else None,
                                                      dqkv_ref, dlr_ref, dw2_ref, dgb_ref, ds0_ref, dst, ds_buf))

    def compute(q_ref, k_ref, v_ref, lr_ref, do_ref, st_ref, w2_ref, gb_ref, pq_ref, pl_ref,
                dqkv_ref, dlr_ref, dw2_ref, dgb_ref, ds0_ref, dst, ds_buf):
        @pl.when(pl.program_id(0) == 0)
        def _():
            dst[...] = jnp.zeros_like(dst)
            dw2_ref[...] = jnp.zeros_like(dw2_ref)
            dgb_ref[...] = jnp.zeros_like(dgb_ref)

        t = _tile_prep(q_ref, k_ref, lr_ref, w2_ref, gb_ref, rev, nc)
        hm = _head_masks()
        v = v_ref[...].reshape(nc, CH, VW)
        do = do_ref[...].reshape(nc, CH, VW)
        heads = lambda a, h: a[:, :, h * HV:(h + 1) * HV]
        dpst = jnp.concatenate([_bnt(heads(do, h), heads(v, h)) for h in range(NH)], axis=1) * t.tri4
        dv = jnp.concatenate([_btn(t.pst[:, h * CH:(h + 1) * CH], heads(do, h)) for h in range(NH)], axis=2)
        dqd = _bdot(do, st_ref[...])
        for h in range(NH):
            dqd = dqd + hm[h] * _bdot(dpst[:, h * CH:(h + 1) * CH], t.kd)
        dkd = _btn(dpst, t.qs)
        bd = _block_diag()
        d = dst[...]
        for n in (range(nc) if rev else range(nc - 1, -1, -1)):
            ds_buf[n] = d
            d = d * t.ecl[n] + bd * _tn(do[n], t.qd[n])
        dst[...] = d
        ds0_ref[...] = d
        ds = ds_buf[...]
        dv = dv + _bnt(t.kdec, ds)
        dkdec = _bdot(v, ds)
        dcl = jnp.sum(ds * st_ref[...].astype(f32), axis=1, keepdims=True) * t.ecl + jnp.sum(dkdec * t.kdec, axis=1, keepdims=True)
        dc = dqd * t.qd - dkd * t.kd - dkdec * t.kdec
        da = _scan_chunks(dc, not rev) + dcl
        dq = dqd * t.ec * QSCALE
        dk = dkd * t.enc + dkdec * t.edec
        dlog = da.reshape(tg, KW) * _sigmoid(-t.logits) * (1.0 / TAU)
        dlr = _nt(dlog, w2_ref[...])
        dw2_ref[...] += _tn(lr_ref[...], dlog)
        dgb_ref[...] += jnp.sum(dlog, axis=0, keepdims=True)
        dqkv = jnp.concatenate([dq, dk, dv], axis=2).reshape(tg, 2 * KW + VW)
        if prev is not None:
            dqkv = dqkv + pq_ref[...]
            dlr = dlr + pl_ref[...]
        dqkv_ref[...] = dqkv.astype(out_dt)
        dlr_ref[...] = dlr.astype(out_dt)

    blk = (lambda i: i) if rev else (lambda i: nb - 1 - i)
    in_specs = [pl.BlockSpec((tg, KW), lambda i: (blk(i), PQ // KW)), pl.BlockSpec((tg, KW), lambda i: (blk(i), PK // KW)),
                pl.BlockSpec((tg, VW), lambda i: (blk(i), PV // VW)), pl.BlockSpec((tg, LRW), lambda i: (blk(i), PLR // LRW)),
                pl.BlockSpec((tg, VW), lambda i: (blk(i), 0)), pl.BlockSpec((nc, VW, KW), lambda i: (blk(i), 0, 0)),
                pl.BlockSpec((LRW, KW), lambda i: (0, 0)), pl.BlockSpec((1, KW), lambda i: (0, 0))]
    args = [p, p, p, p, do, states, w2p, gb]
    if prev is not None:
        in_specs += [pl.BlockSpec((tg, 2 * KW + VW), lambda i: (blk(i), 0)), pl.BlockSpec((tg, LRW), lambda i: (blk(i), 0))]
        args += list(prev)
    return pl.pallas_call(
        body, name=name, grid=(nb,),
        out_shape=(jax.ShapeDtypeStruct((l, 2 * KW + VW), out_dt), jax.ShapeDtypeStruct((l, LRW), out_dt),
                   jax.ShapeDtypeStruct((LRW, KW), f32), jax.ShapeDtypeStruct((1, KW), f32), jax.ShapeDtypeStruct((VW, KW), f32))
        + tuple(comm.outs),
        in_specs=in_specs + c_in,
        out_specs=(pl.BlockSpec((tg, 2 * KW + VW), lambda i: (blk(i), 0)), pl.BlockSpec((tg, LRW), lambda i: (blk(i), 0)),
                   pl.BlockSpec((LRW, KW), lambda i: (0, 0)), pl.BlockSpec((1, KW), lambda i: (0, 0)),
                   pl.BlockSpec((VW, KW), lambda i: (0, 0))) + tuple(c_out),
        scratch_shapes=[pltpu.VMEM((VW, KW), f32), pltpu.VMEM((nc, VW, KW), f32)] + c_sems,
        compiler_params=_cp(("arbitrary",)),
    )(*args, *comm.ins)


def _ctx_hidden(ctx_ref, g_ref, sc_ref, sh_ref):
    xv = ctx_ref[...]
    r = lax.rsqrt(jnp.mean(xv * xv, axis=-1, keepdims=True) + EPS)
    xn = xv * r
    return xn, xn * (g_ref[...] * (1.0 + sc_ref[...])) + sh_ref[...]


_CTX_W_SPECS = [pl.BlockSpec((D, KW), lambda i: (0, PK // KW)), pl.BlockSpec((D, VW), lambda i: (0, PV // VW)),
                pl.BlockSpec((D, LRW), lambda i: (0, PLR // LRW))]


def _ctx_fwd(ctx, g, scale, shift, w_pad, w2f, w2b, gbf, gbb):
    ncc = CTX // CH

    def body(ctx_ref, g_ref, sc_ref, sh_ref, wk_ref, wv_ref, wl_ref, w2f_ref, w2b_ref, gbf_ref, gbb_ref, sf_ref, sb_ref):
        _, hc = _ctx_hidden(ctx_ref, g_ref, sc_ref, sh_ref)
        k, v, lr = _dot(hc, wk_ref[...]), _dot(hc, wv_ref[...]), _dot(hc, wl_ref[...])
        bd = _block_diag()
        for rev, w2_ref, gb_ref, out in ((False, w2f_ref, gbf_ref, sf_ref), (True, w2b_ref, gbb_ref, sb_ref)):
            tri = _tri(rev)
            st = jnp.zeros((VW, KW), f32)
            for j in (range(ncc - 1, -1, -1) if rev else range(ncc)):
                rows = slice(j * CH, (j + 1) * CH)
                _, c, cl = _decay(lr[rows], w2_ref[...], gb_ref[...], tri, rev)
                st = _state_fwd(k[rows], v[rows], c, cl, st, bd)
            out[...] = st

    vec = pl.BlockSpec((1, D), lambda i: (0, 0))
    w2s = pl.BlockSpec((LRW, KW), lambda i: (0, 0))
    gbs = pl.BlockSpec((1, KW), lambda i: (0, 0))
    sts = pl.BlockSpec((VW, KW), lambda i: (0, 0))
    return pl.pallas_call(
        body, name="ctx_fwd", grid=(1,), out_shape=(jax.ShapeDtypeStruct((VW, KW), f32),) * 2,
        in_specs=[pl.BlockSpec((CTX, D), lambda i: (0, 0)), vec, vec, vec] + _CTX_W_SPECS + [w2s, w2s, gbs, gbs],
        out_specs=(sts, sts), compiler_params=_cp(("arbitrary",)),
    )(ctx, g, scale, shift, w_pad, w_pad, w_pad, w2f, w2b, gbf, gbb)


def _ctx_bwd(ctx, g, scale, shift, w_pad, w2f, w2b, gbf, gbb, dsf, dsb):
    ncc = CTX // CH

    def body(ctx_ref, g_ref, sc_ref, sh_ref, wk_ref, wv_ref, wl_ref, w2f_ref, w2b_ref, gbf_ref, gbb_ref, dsf_ref, dsb_ref,
             dwk_ref, dwv_ref, dwl_ref, dmod_ref, dg_ref, dw2_ref, dgb_ref):
        xn, hc = _ctx_hidden(ctx_ref, g_ref, sc_ref, sh_ref)
        k, v, lr = _dot(hc, wk_ref[...]), _dot(hc, wv_ref[...]), _dot(hc, wl_ref[...])
        bd = _block_diag()
        dk_rows, dv_rows, dl_rows = [None] * ncc, [None] * ncc, [None] * ncc
        for d, (rev, w2_ref, gb_ref, ds_ref) in enumerate(((False, w2f_ref, gbf_ref, dsf_ref), (True, w2b_ref, gbb_ref, dsb_ref))):
            tri = _tri(rev)
            order = list(range(ncc - 1, -1, -1) if rev else range(ncc))
            st, saved = jnp.zeros((VW, KW), f32), {}
            for j in order:
                rows = slice(j * CH, (j + 1) * CH)
                logits, c, cl = _decay(lr[rows], w2_ref[...], gb_ref[...], tri, rev)
                saved[j] = (logits, c, cl, st)
                st = _state_fwd(k[rows], v[rows], c, cl, st, bd)
            dst = ds_ref[...]
            dw2 = jnp.zeros((LRW, KW), f32)
            dgb = jnp.zeros((1, KW), f32)
            for j in reversed(order):
                rows = slice(j * CH, (j + 1) * CH)
                logits, c, cl, st0 = saved[j]
                dk, dv, da, dst = _state_bwd(k[rows], v[rows], c, cl, st0, dst, _tri(not rev))
                dlog = da * _sigmoid(-logits) * (1.0 / TAU)
                dl = _nt(dlog, w2_ref[...])
                dw2 = dw2 + _tn(lr[rows], dlog)
                dgb = dgb + jnp.sum(dlog, axis=0, keepdims=True)
                dk_rows[j] = dk if dk_rows[j] is None else dk_rows[j] + dk
                dv_rows[j] = dv if dv_rows[j] is None else dv_rows[j] + dv
                dl_rows[j] = dl if dl_rows[j] is None else dl_rows[j] + dl
            dw2_ref[d] = dw2
            dgb_ref[d] = dgb
        dk, dv, dl = (jnp.concatenate(t, axis=0) for t in (dk_rows, dv_rows, dl_rows))
        dwk_ref[...] = _tn(hc, dk)
        dwv_ref[...] = _tn(hc, dv)
        dwl_ref[...] = _tn(hc, dl)
        dh = _nt(dk, wk_ref[...]) + _nt(dv, wv_ref[...]) + _nt(dl, wl_ref[...])
        gx = dh * xn
        dmod_ref[:, 0:D] = jnp.sum(dh, axis=0, keepdims=True)
        dmod_ref[:, D:2 * D] = jnp.sum(gx, axis=0, keepdims=True) * g_ref[...]
        dg_ref[...] = jnp.sum(gx, axis=0, keepdims=True) * (1.0 + sc_ref[...])

    vec = pl.BlockSpec((1, D), lambda i: (0, 0))
    w2s = pl.BlockSpec((LRW, KW), lambda i: (0, 0))
    gbs = pl.BlockSpec((1, KW), lambda i: (0, 0))
    sts = pl.BlockSpec((VW, KW), lambda i: (0, 0))
    full = lambda *s: pl.BlockSpec(s, lambda i: (0,) * len(s))
    return pl.pallas_call(
        body, name="ctx_bwd", grid=(1,),
        out_shape=(jax.ShapeDtypeStruct((D, KW), f32), jax.ShapeDtypeStruct((D, VW), f32), jax.ShapeDtypeStruct((D, LRW), f32),
                   jax.ShapeDtypeStruct((1, 2 * D), f32), jax.ShapeDtypeStruct((1, D), f32),
                   jax.ShapeDtypeStruct((2, LRW, KW), f32), jax.ShapeDtypeStruct((2, 1, KW), f32)),
        in_specs=[pl.BlockSpec((CTX, D), lambda i: (0, 0)), vec, vec, vec] + _CTX_W_SPECS + [w2s, w2s, gbs, gbs, sts, sts],
        out_specs=(full(D, KW), full(D, VW), full(D, LRW), full(1, 2 * D), full(1, D), full(2, LRW, KW), full(2, 1, KW)),
        compiler_params=_cp(("arbitrary",)),
    )(ctx, g, scale, shift, w_pad, w_pad, w_pad, w2f, w2b, gbf, gbb, dsf, dsb)


def _layernorm(va, g, b):
    mu = jnp.mean(va, axis=-1, keepdims=True)
    xc = va - mu
    rstd = lax.rsqrt(jnp.mean(xc * xc, axis=-1, keepdims=True) + EPS)
    vhat = xc * rstd
    return vhat, rstd, vhat * g + b


MIX_PIECES = 8


def _mix_fwd(p, ln_g, ln_b, ws, bs_t):
    l = p.shape[0]
    half = AW // 2
    rp = l // MIX_PIECES
    cpp = rp // ACH

    def body(p_ref, g_ref, b_ref, ws_ref, bs_ref, sv_hbm, va_buf, col_buf, sv_buf, in_sems, out_sems):
        piece = lambda i: pl.ds(pl.multiple_of(i * rp, rp), rp)
        load = lambda i: pltpu.make_async_copy(p_ref.at[piece(i), pl.ds(PVA, AW)], va_buf.at[piece(i)], in_sems.at[i])
        store = lambda i: pltpu.make_async_copy(sv_buf.at[piece(i)], sv_hbm.at[piece(i)], out_sems.at[i])
        for i in range(MIX_PIECES):
            load(i).start()

        def rows_piece(i, carry):
            load(i).wait()
            for j in range(cpp):
                rows = pl.ds(pl.multiple_of(i * rp + j * ACH, ACH), ACH)
                _, _, vn = _layernorm(va_buf[rows, :].astype(f32), g_ref[...], b_ref[...])
                for gi in range(2):
                    sl = slice(gi * ACH, (gi + 1) * ACH)
                    sv_buf[rows, sl] = (_dot(ws_ref[gi], vn[:, sl]) + bs_ref[:, gi:gi + 1]).astype(bf16)
                col_buf[0, rows, :] = vn[:, half:half + ACH]
                col_buf[1, rows, :] = vn[:, half + ACH:]
            return carry

        lax.fori_loop(0, MIX_PIECES, rows_piece, 0)

        def cols_step(cidx, carry):
            rows = pl.ds(cidx, ACH, stride=GW)
            for gi in range(2, 4):
                col_buf[gi - 2, rows, :] = _dot(ws_ref[gi], col_buf[gi - 2, rows, :]) + bs_ref[:, gi:gi + 1]
            return carry

        lax.fori_loop(0, GW, cols_step, 0, unroll=8)

        def out_piece(i, carry):
            for j in range(cpp):
                rows = pl.ds(pl.multiple_of(i * rp + j * ACH, ACH), ACH)
                sv_buf[rows, half:half + ACH] = col_buf[0, rows, :].astype(bf16)
                sv_buf[rows, half + ACH:] = col_buf[1, rows, :].astype(bf16)
            store(i).start()
            return carry

        lax.fori_loop(0, MIX_PIECES, out_piece, 0)
        for i in range(MIX_PIECES):
            store(i).wait()

    vm = pl.BlockSpec(memory_space=pltpu.VMEM)
    hbm = pl.BlockSpec(memory_space=pl.ANY)
    return pl.pallas_call(
        body, name="mix_fwd", out_shape=jax.ShapeDtypeStruct((l, AW), bf16),
        in_specs=[hbm, vm, vm, vm, vm], out_specs=hbm,
        scratch_shapes=[pltpu.VMEM((l, AW), bf16), pltpu.VMEM((2, l, ACH), f32), pltpu.VMEM((l, AW), bf16),
                        pltpu.SemaphoreType.DMA((MIX_PIECES,)), pltpu.SemaphoreType.DMA((MIX_PIECES,))],
        compiler_params=_cp(),
    )(p, ln_g, ln_b, ws, bs_t)


def _mix_bwd(p, dsv, ln_g, ln_b, ws_t):
    l = p.shape[0]
    half = AW // 2
    rp = l // MIX_PIECES
    cpp = rp // ACH

    def body(p_ref, dsv_hbm, g_ref, b_ref, wst_ref, dva_hbm, dws_ref, dbs_ref, dg_ref, db_ref,
             va_buf, dsv_buf, vn_col, ds_col, dva_buf, va_sems, ds_sems, out_sems):
        piece = lambda i: pl.ds(pl.multiple_of(i * rp, rp), rp)
        load_va = lambda i: pltpu.make_async_copy(p_ref.at[piece(i), pl.ds(PVA, AW)], va_buf.at[piece(i)], va_sems.at[i])
        load_ds = lambda i: pltpu.make_async_copy(dsv_hbm.at[piece(i)], dsv_buf.at[piece(i)], ds_sems.at[i])
        store = lambda i: pltpu.make_async_copy(dva_buf.at[piece(i)], dva_hbm.at[piece(i)], out_sems.at[i])
        for i in range(MIX_PIECES):
            load_va(i).start()
            load_ds(i).start()
        dws_ref[...] = jnp.zeros_like(dws_ref)
        dbs_ref[...] = jnp.zeros_like(dbs_ref)
        dg_ref[...] = jnp.zeros_like(dg_ref)
        db_ref[...] = jnp.zeros_like(db_ref)

        def rows_piece(i, carry):
            load_va(i).wait()
            load_ds(i).wait()
            for j in range(cpp):
                rows = pl.ds(pl.multiple_of(i * rp + j * ACH, ACH), ACH)
                _, _, vn = _layernorm(va_buf[rows, :].astype(f32), g_ref[...], b_ref[...])
                ds = dsv_buf[rows, :].astype(f32)
                for gi in range(2):
                    sl = slice(gi * ACH, (gi + 1) * ACH)
                    dws_ref[gi] += _nt(ds[:, sl], vn[:, sl])
                    dbs_ref[gi] += ds[:, sl]
                for gi in range(2):
                    sl = slice(half + gi * ACH, half + (gi + 1) * ACH)
                    vn_col[gi, rows, :] = vn[:, sl]
                    ds_col[gi, rows, :] = ds[:, sl]
            return carry

        lax.fori_loop(0, MIX_PIECES, rows_piece, 0)

        def cols_step(cidx, carry):
            rows = pl.ds(cidx, ACH, stride=GW)
            for gi in range(2, 4):
                ds = ds_col[gi - 2, rows, :]
                dws_ref[gi] += _nt(ds, vn_col[gi - 2, rows, :])
                dbs_ref[gi] += ds
                ds_col[gi - 2, rows, :] = _dot(wst_ref[gi], ds)
            return carry

        lax.fori_loop(0, GW, cols_step, 0, unroll=8)

        def out_piece(i, carry):
            for j in range(cpp):
                rows = pl.ds(pl.multiple_of(i * rp + j * ACH, ACH), ACH)
                vhat, rstd, _ = _layernorm(va_buf[rows, :].astype(f32), g_ref[...], b_ref[...])
                ds = dsv_buf[rows, :].astype(f32)
                dvn = jnp.concatenate([_dot(wst_ref[0], ds[:, 0:ACH]), _dot(wst_ref[1], ds[:, ACH:half]),
                                       ds_col[0, rows, :], ds_col[1, rows, :]], axis=1)
                dg_ref[...] += jnp.sum(dvn * vhat, axis=0, keepdims=True)
                db_ref[...] += jnp.sum(dvn, axis=0, keepdims=True)
                dvh = dvn * g_ref[...]
                dva = rstd * (dvh - jnp.mean(dvh, axis=-1, keepdims=True) - vhat * jnp.mean(dvh * vhat, axis=-1, keepdims=True))
                dva_buf[rows, :] = dva.astype(bf16)
            store(i).start()
            return carry

        lax.fori_loop(0, MIX_PIECES, out_piece, 0)
        for i in range(MIX_PIECES):
            store(i).wait()

    vm = pl.BlockSpec(memory_space=pltpu.VMEM)
    hbm = pl.BlockSpec(memory_space=pl.ANY)
    dma = lambda: pltpu.SemaphoreType.DMA((MIX_PIECES,))
    return pl.pallas_call(
        body, name="mix_bwd",
        out_shape=(jax.ShapeDtypeStruct((l, AW), bf16), jax.ShapeDtypeStruct((4, ACH, ACH), f32), jax.ShapeDtypeStruct((4, ACH, ACH), f32),
                   jax.ShapeDtypeStruct((1, AW), f32), jax.ShapeDtypeStruct((1, AW), f32)),
        in_specs=[hbm, hbm, vm, vm, vm], out_specs=(hbm, vm, vm, vm, vm),
        scratch_shapes=[pltpu.VMEM((l, AW), bf16), pltpu.VMEM((l, AW), bf16), pltpu.VMEM((2, l, ACH), f32), pltpu.VMEM((2, l, ACH), f32),
                        pltpu.VMEM((l, AW), bf16), dma(), dma(), dma()],
        compiler_params=_cp(),
    )(p, dsv, ln_g, ln_b, ws_t)


def _mid(x, tgt, p, o_f, o_b, sv, gate, gf, gb_norm, w_pa, w_pb, w_out, tl):
    l = x.shape[0]

    def body(x_ref, t_ref, zb_ref, ua_ref, za_ref, g1_ref, g2_ref, of_ref, ob_ref, sv_ref, gate_ref, gf_ref, gbn_ref,
             wpa_ref, wpb_ref, wout_ref,
             dx1_ref, dzbua_ref, dzag_ref, dsv_ref, do_ref, dwout_bf, dwpa_bf, dwpb_bf, dgf_ref, dgate_ref, dgbn_ref, loss_ref,
             dwout_ref, dwpa_ref, dwpb_ref):
        @pl.when(pl.program_id(0) == 0)
        def _():
            for r in (dwout_ref, dwpa_ref, dwpb_ref, dgf_ref, dgate_ref, dgbn_ref, loss_ref):
                r[...] = jnp.zeros_like(r)

        o = of_ref[...].astype(f32) + ob_ref[...].astype(f32)
        rr = jnp.concatenate(
            [jnp.broadcast_to(lax.rsqrt(jnp.mean(o[:, h * HV:(h + 1) * HV] ** 2, axis=-1, keepdims=True) + EPS), (tl, HV))
             for h in range(NH)], axis=1)
        ohat = o * rr
        on = ohat * gbn_ref[...]
        szb, dszb = _silu_and_grad(zb_ref[...].astype(f32))
        tb = on * szb
        u = ua_ref[...].astype(f32)
        svv = sv_ref[...].astype(f32)
        sza, dsza = _silu_and_grad(za_ref[...].astype(f32))
        ta = u * svv * sza
        ya = _dot(ta, wpa_ref[...])
        yb = _dot(tb, wpb_ref[...])
        g1 = _sigmoid(g1_ref[...].astype(f32))
        g2 = _sigmoid(g2_ref[...].astype(f32))
        m = g1 * ya + g2 * yb
        y2 = _dot(m, wout_ref[...])
        x1 = x_ref[...] + gate_ref[...] * y2
        r1 = lax.rsqrt(jnp.mean(x1 * x1, axis=-1, keepdims=True) + EPS)
        x1n = x1 * r1
        err = x1n * gf_ref[...] - t_ref[...]
        loss_ref[...] += jnp.sum(jnp.sum(err * err, axis=-1, keepdims=True), axis=0, keepdims=True) * (0.5 / D)
        dout = err * (1.0 / D)
        dgf_ref[...] += jnp.sum(dout * x1n, axis=0, keepdims=True)
        dx1n = dout * gf_ref[...]
        dx1 = r1 * (dx1n - x1n * jnp.mean(dx1n * x1n, axis=-1, keepdims=True))
        dx1_ref[...] = dx1
        dgate_ref[...] += jnp.sum(dx1 * y2, axis=0, keepdims=True)
        dy2 = dx1 * gate_ref[...]
        dwout_ref[...] += _tn(m, dy2)
        dm = _nt(dy2, wout_ref[...])
        dya = dm * g1
        dyb = dm * g2
        dzag_ref[:, AW:AW + D] = (dm * ya * g1 * (1.0 - g1)).astype(bf16)
        dzag_ref[:, AW + D:] = (dm * yb * g2 * (1.0 - g2)).astype(bf16)
        dwpa_ref[...] += _tn(ta, dya)
        dta = _nt(dya, wpa_ref[...])
        dzbua_ref[:, AW:] = (dta * svv * sza).astype(bf16)
        dsv_ref[...] = (dta * u * sza).astype(bf16)
        dzag_ref[:, 0:AW] = (dta * u * svv * dsza).astype(bf16)
        dwpb_ref[...] += _tn(tb, dyb)
        dtb = _nt(dyb, wpb_ref[...])
        don = dtb * szb
        dzbua_ref[:, 0:AW] = (dtb * on * dszb).astype(bf16)
        dgbn_ref[...] += jnp.sum(don * ohat, axis=0, keepdims=True)
        doh = don * gbn_ref[...]
        prod = doh * ohat
        mh = jnp.concatenate(
            [jnp.broadcast_to(jnp.mean(prod[:, h * HV:(h + 1) * HV], axis=-1, keepdims=True), (tl, HV)) for h in range(NH)], axis=1)
        do_ref[...] = (rr * (doh - ohat * mh)).astype(bf16)

        @pl.when(pl.program_id(0) == l // tl - 1)
        def _():
            for acc, out in ((dwout_ref, dwout_bf), (dwpa_ref, dwpa_bf), (dwpb_ref, dwpb_bf)):
                out[...] = acc[...].astype(bf16)

    row = lambda w, j: pl.BlockSpec((tl, w), lambda i, j=j: (i, j))
    full = lambda *s: pl.BlockSpec(s, lambda i: (0,) * len(s))
    return pl.pallas_call(
        body, name="mid", grid=(l // tl,),
        out_shape=(jax.ShapeDtypeStruct((l, D), f32), jax.ShapeDtypeStruct((l, 2 * AW), bf16), jax.ShapeDtypeStruct((l, AW + 2 * D), bf16),
                   jax.ShapeDtypeStruct((l, AW), bf16), jax.ShapeDtypeStruct((l, VW), bf16),
                   jax.ShapeDtypeStruct((D, D), bf16), jax.ShapeDtypeStruct((AW, D), bf16), jax.ShapeDtypeStruct((VW, D), bf16),
                   jax.ShapeDtypeStruct((1, D), f32), jax.ShapeDtypeStruct((1, D), f32), jax.ShapeDtypeStruct((1, VW), f32),
                   jax.ShapeDtypeStruct((1, 1), f32)),
        scratch_shapes=[pltpu.VMEM((D, D), f32), pltpu.VMEM((AW, D), f32), pltpu.VMEM((VW, D), f32)],
        in_specs=[row(D, 0), row(D, 0), row(AW, PZB // AW), row(AW, PUA // AW), row(AW, PZA // AW), row(D, PG1 // D), row(D, PG2 // D),
                  row(VW, 0), row(VW, 0), row(AW, 0), full(1, D), full(1, D), full(1, VW), full(AW, D), full(VW, D), full(D, D)],
        out_specs=(row(D, 0), row(2 * AW, 0), row(AW + 2 * D, 0), row(AW, 0), row(VW, 0),
                   full(D, D), full(AW, D), full(VW, D), full(1, D), full(1, D), full(1, VW), full(1, 1)),
        compiler_params=_cp(("arbitrary",)),
    )(x, tgt, p, p, p, p, p, o_f, o_b, sv, gate, gf, gb_norm, w_pa, w_pb, w_out)


def _in_bwd(x, dx1, dqkv, dzbua, dva, dzag, dlr, w_pad, g, scale, tl, comm):
    l = x.shape[0]
    c_in, c_out, c_sems = comm.specs()

    def body(*refs):
        cin = refs[14:14 + len(c_in)]
        outs = refs[14 + len(c_in):]
        comm.run(cin, outs[4:4 + comm.n], outs[4 + comm.n:], l // tl, lambda: compute(*refs[:14], *outs[:4]))

    def compute(x_ref, dx1_ref, a_ref, b_ref, c_ref, e_ref, lr_ref, wa_ref, wb_ref, wc_ref, we_ref, wl_ref, g_ref, sc_ref,
                gx_ref, dsh_ref, dsc_ref, dg_ref):
        @pl.when(pl.program_id(0) == 0)
        def _():
            for r in (dsh_ref, dsc_ref, dg_ref):
                r[...] = jnp.zeros_like(r)

        dh = (_nt(a_ref[...], wa_ref[...]) + _nt(b_ref[...], wb_ref[...]) + _nt(c_ref[...], wc_ref[...]) + _nt(e_ref[...], we_ref[...])
              + _nt(lr_ref[...], wl_ref[...]))
        xv = x_ref[...]
        r = lax.rsqrt(jnp.mean(xv * xv, axis=-1, keepdims=True) + EPS)
        xn = xv * r
        gxn = jnp.sum(dh * xn, axis=0, keepdims=True)
        dsh_ref[...] += jnp.sum(dh, axis=0, keepdims=True)
        dsc_ref[...] += gxn * g_ref[...]
        dg_ref[...] += gxn * (1.0 + sc_ref[...])
        dxn = dh * (g_ref[...] * (1.0 + sc_ref[...]))
        gx_ref[...] = dx1_ref[...] + r * (dxn - xn * jnp.mean(dxn * xn, axis=-1, keepdims=True))

    row = lambda w: pl.BlockSpec((tl, w), lambda i: (i, 0))
    wcol = lambda w, j: pl.BlockSpec((D, w), lambda i, j=j: (0, j))
    vec = pl.BlockSpec((1, D), lambda i: (0, 0))
    return pl.pallas_call(
        body, name="in_bwd", grid=(l // tl,),
        out_shape=(jax.ShapeDtypeStruct((l, D), f32),) + (jax.ShapeDtypeStruct((1, D), f32),) * 3 + tuple(comm.outs),
        in_specs=[row(D), row(D), row(2 * KW + VW), row(2 * AW), row(AW), row(AW + 2 * D), row(LRW),
                  wcol(2 * KW + VW, 0), wcol(2 * AW, PZB // (2 * AW)), wcol(AW, PVA // AW), wcol(AW + 2 * D, PZA // (AW + 2 * D)),
                  wcol(LRW, PLR // LRW), vec, vec] + c_in,
        out_specs=(row(D), vec, vec, vec) + tuple(c_out), scratch_shapes=c_sems,
        compiler_params=_cp(("arbitrary",)),
    )(x, dx1, dqkv, dzbua, dva, dzag, dlr, w_pad, w_pad, w_pad, w_pad, w_pad, g, scale, *comm.ins)


def _tn_matmul(a, bs, tl, name, comm=None, pack=None):
    l, m = a.shape
    k = len(bs)
    comm = comm or _Comm([], [], None)
    c_in, c_out, c_sems = comm.specs()
    acc_shapes = [(m, b.shape[1]) for b in bs]
    n_extra = len(pack.extra) if pack else 0
    n_res = len(pack.outs) if pack else k

    def body(a_ref, *refs):
        b_refs, refs = refs[:k], refs[k:]
        extra, refs = refs[:n_extra], refs[n_extra:]
        cin, refs = refs[:len(c_in)], refs[len(c_in):]
        res, refs = refs[:n_res], refs[n_res:]
        cout, refs = refs[:comm.n], refs[comm.n:]
        accs, sems = (refs[:k], refs[k:]) if pack else (res, refs)

        def compute():
            @pl.when(pl.program_id(0) == 0)
            def _():
                for o_ref in accs:
                    o_ref[...] = jnp.zeros_like(o_ref)

            av = a_ref[...]
            for b_ref, o_ref in zip(b_refs, accs):
                o_ref[...] += _tn(av, b_ref[...])
            if pack:
                pl.when(pl.program_id(0) == l // tl - 1)(lambda: pack.fn(accs, extra, res))

        comm.run(cin, cout, sems, l // tl, compute)

    full = lambda shape: pl.BlockSpec(shape, lambda i: (0,) * len(shape))
    res_shapes = list(pack.outs) if pack else [jax.ShapeDtypeStruct(sh, f32) for sh in acc_shapes]
    return pl.pallas_call(
        body, name=name, grid=(l // tl,), out_shape=tuple(res_shapes) + tuple(comm.outs),
        in_specs=[pl.BlockSpec((tl, m), lambda i: (i, 0))] + [pl.BlockSpec((tl, b.shape[1]), lambda i: (i, 0)) for b in bs]
        + [full(e.shape) for e in (pack.extra if pack else [])] + c_in,
        out_specs=tuple(full(r.shape) for r in res_shapes) + tuple(c_out),
        scratch_shapes=([pltpu.VMEM(sh, f32) for sh in acc_shapes] if pack else []) + c_sems, compiler_params=_cp(("arbitrary",)),
    )(a, *bs, *(pack.extra if pack else []), *comm.ins)


def _pad_gate(w2, gb):
    z = jnp.zeros((RANK, KW), f32)
    tail = jnp.zeros((LRW - 2 * RANK, KW), f32)
    w2f = jnp.concatenate([w2[0], z, tail], axis=0)
    w2b = jnp.concatenate([z, w2[1], tail], axis=0)
    return w2f, w2b, gb[0:1], gb[1:2]


EARLY_A_ROWS = 512


class _NoExchange:
    def __init__(self, w_pa, w_pb, w_out):
        self.weights = (w_pa, w_pb, w_out)

    def gather_proj(self):
        return _Comm([], [], None)

    def proj_weights(self, got):
        return self.weights

    def first(self, dw_out, dw_pa, dw_pb, small):
        return _Comm([], [], None)

    def early_a(self, blocks):
        return _Comm([], [], None)

    def early_b(self, blocks):
        return _Comm([], [], None)

    def late(self, blocks, dgt):
        return _Comm([], [], None)


class _Exchanges:
    def __init__(self, pa, pb, wo):
        self.shards = (pa, pb, wo)

    def gather_proj(self):
        def plan(i, o):
            srcs = [lambda j, r=r: r for r in i]
            dsts = [lambda j: o[0].at[:, _lanes(j)], lambda j: o[1].at[:, _lanes(j)], lambda j: o[2].at[j]]
            return srcs, dsts, None
        sds = jax.ShapeDtypeStruct
        return _Comm(self.shards, [sds((AW, D), bf16), sds((VW, D), bf16), sds((NDEV, 128, D), bf16)], plan)

    def proj_weights(self, got):
        return got[0], got[1], got[2].reshape(D, D)

    def first(self, dw_out, dw_pa, dw_pb, small):
        def plan(i, o):
            srcs = [lambda j: i[0].at[j], lambda j: i[1].at[:, _lanes(j)], lambda j: i[2].at[:, _lanes(j)], lambda j: i[3]]
            dsts = [lambda j, r=r: r.at[j] for r in o]
            return srcs, dsts, None
        sds = jax.ShapeDtypeStruct
        return _Comm([dw_out.reshape(NDEV, 128, D), dw_pa, dw_pb, small],
                     [sds((NDEV, 128, D), bf16), sds((NDEV, AW, 128), bf16), sds((NDEV, VW, 128), bf16),
                      sds((NDEV,) + small.shape, f32)], plan)

    def early_a(self, blocks):
        def plan(i, o):
            return [lambda j: i[0].at[j]], [lambda j: o[0].at[j]], [lambda j: j >= LATE_DESTS - 1]
        return _Comm([blocks], [jax.ShapeDtypeStruct(blocks.shape, bf16)], plan)

    def early_b(self, blocks):
        def plan(i, o):
            return [lambda j: i[0].at[j]], [lambda j: o[0].at[j]], [lambda j: j >= LATE_DESTS - 1]
        return _Comm([blocks], [jax.ShapeDtypeStruct(blocks.shape, bf16)], plan)

    def late(self, blocks, dgt):
        return _LateComm(blocks, dgt)


def _local_step(x, ctx, tgt, mod, modc, norm_g, w_pad, ln_g, ln_b, ws, bs, w2, gb, gb_norm, gf, xch):
    shift, scale, gate = mod[:, 0:D], mod[:, D:2 * D], mod[:, 2 * D:]
    shift_c, scale_c = modc[:, 0:D], modc[:, D:]
    w2f, w2b, gbf, gbb = _pad_gate(w2, gb)

    p, h, *got_proj = _in_proj(x, norm_g, scale, shift, w_pad, 512, xch.gather_proj())
    w_pa, w_pb, w_out = xch.proj_weights(got_proj)
    sc_f, sc_b = _ctx_fwd(ctx, norm_g, scale_c, shift_c, w_pad, w2f, w2b, gbf, gbb)
    o_f, st_f = _gla_fwd(p, w2f, gbf, sc_f, False, 512, "gla_fwd_f")
    o_b, st_b = _gla_fwd(p, w2b, gbb, sc_b, True, 512, "gla_fwd_b")
    sv = _mix_fwd(p, ln_g, ln_b, ws.astype(bf16), bs.T)
    (dx1, dzbua, dzag, dsv, do, dw_out, dw_pa, dw_pb, dgf, dgate, dgbn, loss) = _mid(
        x, tgt, p, o_f, o_b, sv, gate, gf, gb_norm, w_pa, w_pb, w_out, 256)
    dva, dws, dbs_acc, dln_g, dln_b = _mix_bwd(p, dsv, ln_g, ln_b, jnp.swapaxes(ws, 1, 2).astype(bf16))
    small = _pack_rows128(dln_g, dln_b, dws, jnp.sum(dbs_acc, axis=-1), dgbn, dgf, loss)
    blocks_a, blocks_b, *got_first = _tn_matmul(h, [dzbua, dva, dzag], 1024, "dw_early", xch.first(dw_out, dw_pa, dw_pb, small),
                                               _pack_early())

    dqkv_f, dlr_f, dw2f, dgbf, dsc_f, *got_a = _gla_bwd(p, do, st_f, w2f, gbf, None, False, 512, "gla_bwd_f",
                                                        xch.early_a(blocks_a))
    dqkv, dlr, dw2b, dgbb, dsc_b, *got_b = _gla_bwd(p, do, st_b, w2b, gbb, (dqkv_f, dlr_f), True, 512, "gla_bwd_b",
                                                    xch.early_b(blocks_b))
    dwk_c, dwv_c, dwl_c, dmodc, dg_c, dw2c, dgbc = _ctx_bwd(ctx, norm_g, scale_c, shift_c, w_pad, w2f, w2b, gbf, gbb, dsc_f, dsc_b)
    (blocks_late,) = _tn_matmul(h, [dqkv, dlr], 1024, "dw_qkv_lr", pack=_pack_late(dwk_c, dwv_c, dwl_c))
    dw2 = jnp.stack([dw2f[0:RANK] + dw2c[0, 0:RANK], dw2b[RANK:2 * RANK] + dw2c[1, RANK:2 * RANK]])
    dgb = jnp.concatenate([dgbf + dgbc[0], dgbb + dgbc[1]], axis=0)
    dgt = jnp.concatenate([jnp.transpose(dw2.reshape(2, RANK, NDEV, 32), (2, 0, 1, 3)).reshape(NDEV, 2 * RANK * 32),
                           jnp.transpose(dgb.reshape(2, NDEV, 32), (1, 0, 2)).reshape(NDEV, 64),
                           jnp.zeros((NDEV, 64), f32)], axis=1).reshape(NDEV, 9, 128)
    gx, dshift, dscale, dg, *got_late = _in_bwd(x, dx1, dqkv, dzbua, dva, dzag, dlr, w_pad, norm_g, scale, 512,
                                                xch.late(blocks_late, dgt))
    return dict(loss=loss, gx=gx, dmod=jnp.concatenate([dshift, dscale, dgate], axis=1), dmodc=dmodc, dnorm_g=dg + dg_c,
                small=small, blocks_a=blocks_a, blocks_b=blocks_b, blocks_late=blocks_late, dw2=dw2, dgb=dgb,
                dw_pa=dw_pa, dw_pb=dw_pb, dw_out=dw_out, got_first=got_first, got_a=got_a, got_b=got_b, got_late=got_late)


def _pack_rows128(*vs):
    nrows = [max(v.size // 128, 1) for v in vs]

    def body(*refs):
        o_ref = refs[-1]
        r = 0
        for ref, v, n in zip(refs, vs, nrows):
            if v.shape == (1, 1):
                o_ref[r:r + 1, :] = jnp.broadcast_to(ref[...], (1, 128))
            elif v.ndim == 2 and v.shape[0] == 1:
                for k in range(n):
                    o_ref[r + k:r + k + 1, :] = ref[:, k * 128:(k + 1) * 128]
            elif v.ndim == 2:
                o_ref[r:r + n, :] = ref[...]
            else:
                for k in range(v.shape[0]):
                    o_ref[r + k * v.shape[1]:r + (k + 1) * v.shape[1], :] = ref[k]
            r += n

    vm = pl.BlockSpec(memory_space=pltpu.VMEM)
    return pl.pallas_call(body, name="pack_small", out_shape=jax.ShapeDtypeStruct((sum(nrows), 128), f32),
                          in_specs=[vm] * len(vs), out_specs=vm, compiler_params=_cp())(*vs)


def _rows128(*vs):
    out = []
    for t in vs:
        t = t.reshape(-1)
        pad = (-t.shape[0]) % 128
        out.append(jnp.pad(t, (0, pad)) if pad else t)
    return jnp.concatenate(out).reshape(-1, 128)


def kernel(x, c, ctx, c_ctx, w_mod, b_mod, norm_g, w_in, a_ln_g, a_ln_b, a_ws, a_bs, b_gate_w2, b_gate_b, b_norm_g, w_proj_a, w_proj_b, w_out, final_norm_g, loss_target, m_c_ctx, m_w_mod, m_b_mod, m_norm_g, m_w_in, m_a_ln_g, m_a_ln_b, m_a_ws, m_a_bs, m_b_gate_w2, m_b_gate_b, m_b_norm_g, m_w_proj_a, m_w_proj_b, m_w_out, m_final_norm_g, v_c_ctx, v_w_mod, v_b_mod, v_norm_g, v_w_in, v_a_ln_g, v_a_ln_b, v_a_ws, v_a_bs, v_b_gate_w2, v_b_gate_b, v_b_norm_g, v_w_proj_a, v_w_proj_b, v_w_out, v_final_norm_g):
    me = _me()

    gate_mine = _rows128(jnp.concatenate([b_gate_w2.reshape(-1), b_gate_b.reshape(-1)]))
    cs, mods, wg, gates = _gather_first(c, c_ctx.reshape(1, D), w_mod[0], b_mod, w_in[0].astype(bf16), gate_mine)
    w_pad = _repack_w(wg)
    gflat = gates.reshape(NDEV, 9 * 128)
    w2 = jnp.transpose(gflat[:, 0:2 * RANK * 32].reshape(NDEV, 2, RANK, 32), (1, 2, 0, 3)).reshape(2, RANK, KW)
    gb = jnp.transpose(gflat[:, 2 * RANK * 32:2 * RANK * 32 + 64].reshape(NDEV, 2, 32), (1, 0, 2)).reshape(2, KW)

    mods = jnp.transpose(mods, (1, 0, 2)).reshape(16, 3 * D)
    mod = lax.dynamic_slice(mods, (me, 0), (1, 3 * D))
    modc = mods[8:9, 0:2 * D]

    xch = _Exchanges(w_proj_a[0].astype(bf16), w_proj_b[0].astype(bf16), w_out[0].astype(bf16))
    r = _local_step(x[0], ctx[0], loss_target[0], mod, modc, norm_g, w_pad, a_ln_g, a_ln_b, a_ws[0], a_bs[0], w2, gb,
                    b_norm_g, final_norm_g.reshape(1, D), xch)
    p_out, p_pa, p_pb, smalls_e = r["got_first"]
    (p_in_a,) = r["got_a"]
    (p_in_b,) = r["got_b"]
    _, _, p_in_late, p_gt = r["got_late"]

    n_e = (AW + AW + 4 * ACH * ACH + AW + VW + D) // 128
    row = lambda t: t.reshape(1, D)
    rep_e = _adam_params(smalls_e, [a_ln_g, a_ln_b, a_ws, a_bs, b_norm_g, row(final_norm_g)],
                         [m_a_ln_g, m_a_ln_b, m_a_ws, m_a_bs, m_b_norm_g, row(m_final_norm_g)],
                         [v_a_ln_g, v_a_ln_b, v_a_ws, v_a_bs, v_b_norm_g, row(v_final_norm_g)], "adam_rep_early")
    rep_e = [t[0:5] + (t[5].reshape(D),) for t in rep_e]
    loss = jnp.sum(smalls_e[:, n_e, 0])

    dbm = r["dmod"] + jnp.concatenate([r["dmodc"], jnp.zeros((1, D), f32)], axis=1)
    mod_grads = _mod_tail(r["dnorm_g"], dbm, r["dmod"], r["dmodc"], cs, w_mod)
    mod_res = _adam_plain(mod_grads, [w_mod, norm_g, b_mod, row(c_ctx)], [m_w_mod, m_norm_g, m_b_mod, row(m_c_ctx)],
                          [v_w_mod, v_norm_g, v_b_mod, row(v_c_ctx)], "adam_mod")
    wm, ng, bmod_r, cc = ([t[i] for t in mod_res] for i in range(4))

    a_in = _adam_w_in(p_in_a, p_in_b, p_in_late, w_in, m_w_in, v_w_in)
    blk = _adam_blocks([p_pa, p_pb, p_out], [w_proj_a, w_proj_b, w_out], [m_w_proj_a, m_w_proj_b, m_w_out],
                       [v_w_proj_a, v_w_proj_b, v_w_out], "adam_proj_out")
    a_pa, a_pb, a_out = ([t[i] for t in blk] for i in range(3))
    a_gt = _adam_gate(p_gt, b_gate_w2, b_gate_b, m_b_gate_w2, m_b_gate_b, v_b_gate_w2, v_b_gate_b)
    sh = [(a_in[k], a_pa[k], a_pb[k], a_out[k], a_gt[k], a_gt[4 + k]) for k in range(4)]

    outs = [loss, r["gx"][None]]
    for k in range(4):
        lg, lb, aws, abs_, bng, fng = rep_e[k]
        n_g, bmod = ng[k], bmod_r[k]
        s_in, s_pa, s_pb, s_out, s_w2, s_gb = sh[k]
        outs += [cc[k].reshape(D), wm[k], bmod, n_g, s_in, lg, lb, aws, abs_, s_w2, s_gb, bng, s_pa, s_pb, s_out, fng]
    return tuple(outs)
```

```python
import jax
import jax.numpy as jnp
from jax import lax
from jax.experimental import pallas as pl
from jax.experimental.pallas import tpu as pltpu

f32, bf16 = jnp.float32, jnp.bfloat16

D = 1024
CTX = 256
EPS = 1e-6
AW = 512
ACH = 128
GW = 64
KW = 256
VW = 512
NH = 4
HK = 64
HV = 128
RANK = 16
TAU = 16.0
CH = 64
QSCALE = HK ** -0.5
INW = 5152
NDEV = 8

PQ, PK, PV, PZB, PUA, PVA, PZA, PG1, PG2, PLR, PW = 0, 256, 512, 1024, 1536, 2048, 2560, 3072, 4096, 5120, 5248
LRW = 128

ADAM_LR, ADAM_B1, ADAM_B2, ADAM_EPS, ADAM_WD, ADAM_STEP = 0.001, 0.9, 0.999, 1e-08, 0.01, 10

VMEM_LIMIT = 56 * 1024 * 1024
MESH = pl.DeviceIdType.MESH


def _cp(sem=None):
    return pltpu.CompilerParams(dimension_semantics=sem, vmem_limit_bytes=VMEM_LIMIT)


def _dot(a, b):
    return jnp.dot(a.astype(bf16), b.astype(bf16), preferred_element_type=f32)


def _nt(a, b):
    return lax.dot_general(a.astype(bf16), b.astype(bf16), (((1,), (1,)), ((), ())), preferred_element_type=f32)


def _tn(a, b):
    return lax.dot_general(a.astype(bf16), b.astype(bf16), (((0,), (0,)), ((), ())), preferred_element_type=f32)


def _dot_hi(a, b):
    return jnp.dot(a, b, preferred_element_type=f32, precision=lax.Precision.HIGHEST)


def _sigmoid(x):
    return 1.0 / (1.0 + jnp.exp(-x))


def _log_sigmoid(x):
    return jnp.minimum(x, 0.0) - jnp.log(1.0 + jnp.exp(-jnp.abs(x)))


def _silu_and_grad(z):
    s = _sigmoid(z)
    return z * s, s * (1.0 + z * (1.0 - s))


def _me():
    return 4 * lax.axis_index("x") + 2 * lax.axis_index("y") + lax.axis_index("c")


def _peer(k):
    x, y, c = lax.axis_index("x"), lax.axis_index("y"), lax.axis_index("c")
    px = 1 - x if k & 4 else x
    py = 1 - y if k & 2 else y
    pc = 1 - c if k & 1 else c
    return (px, py, pc), 4 * px + 2 * py + pc


def _fanout(srcs, dsts, send_sems, recv_sems, local_sems, owners=None):
    me = _me()
    n = len(srcs)
    owns = lambda a, j: True if owners is None or owners[a] is None else owners[a](j)

    def guarded(cond, fn):
        if cond is True:
            fn()
        else:
            pl.when(cond)(fn)

    def copies(with_recvs):
        local = [pltpu.make_async_copy(srcs[a](me), dsts[a](me), local_sems.at[a]) for a in range(n)]
        sends, recvs = [], []
        for k in range(1, NDEV):
            dev, idx = _peer(k)
            for a in range(n):
                s = (k - 1) * n + a
                sends.append((owns(a, idx), pltpu.make_async_remote_copy(
                    src_ref=srcs[a](idx), dst_ref=dsts[a](me), send_sem=send_sems.at[s], recv_sem=recv_sems.at[s],
                    device_id=dev, device_id_type=MESH)))
                if with_recvs:
                    recvs.append((owns(a, me), pltpu.make_async_remote_copy(
                        src_ref=srcs[a](idx), dst_ref=dsts[a](idx), send_sem=send_sems.at[s], recv_sem=recv_sems.at[s],
                        device_id=dev, device_id_type=MESH)))
        return local, sends, recvs

    def start():
        local, sends, _ = copies(False)
        for a, cp in enumerate(local):
            guarded(owns(a, me), cp.start)
        for cond, cp in sends:
            guarded(cond, cp.start)

    def finish():
        local, sends, recvs = copies(True)
        for cond, cp in recvs:
            guarded(cond, cp.wait_recv)
        for cond, cp in sends:
            guarded(cond, cp.wait_send)
        for a, cp in enumerate(local):
            guarded(owns(a, me), cp.wait)

    return start, finish


class _Comm:
    def __init__(self, ins, outs, plan):
        self.ins, self.outs, self.plan = list(ins), list(outs), plan
        self.n = len(self.outs)

    def specs(self):
        hbm = pl.BlockSpec(memory_space=pl.ANY)
        return [hbm] * len(self.ins), [hbm] * self.n, _fanout_sems(self.n) if self.n else []

    def run(self, in_refs, out_refs, sems, nsteps, compute):
        if not self.n:
            compute()
            return

        def hooks():
            srcs, dsts, owners = self.plan(in_refs, out_refs)
            return _fanout(srcs, dsts, sems[0], sems[1], sems[2], owners)

        pl.when(pl.program_id(0) == 0)(lambda: hooks()[0]())
        compute()
        pl.when(pl.program_id(0) == nsteps - 1)(lambda: hooks()[1]())


LATE_MID_STEP = 1


class _LateComm:
    def __init__(self, blocks, dgt):
        sds = jax.ShapeDtypeStruct
        self.ins = [blocks, dgt]
        self.outs = [sds((D, SHARD), bf16), sds((D, SHARD), bf16), sds((4, D, SHARD), bf16), sds(dgt.shape, f32)]
        self.n = len(self.outs)

    def specs(self):
        hbm = pl.BlockSpec(memory_space=pl.ANY)
        scratch = [pltpu.VMEM((3, D, SHARD), bf16), pltpu.SemaphoreType.DMA((2,)), pltpu.SemaphoreType.DMA((4,)),
                   pltpu.SemaphoreType.DMA((3,))] + _fanout_sems(1)
        return [hbm] * 2, [hbm] * self.n, scratch

    def run(self, in_refs, out_refs, scratch, nsteps, compute):
        late_ref, dgt_ref = in_refs
        sib_ref, pair_ref, parts_ref, gt_ref = out_refs
        vbuf, send_sems, recv_sems, local_sems, g_send, g_recv, g_local = scratch
        x, y, c = lax.axis_index("x"), lax.axis_index("y"), lax.axis_index("c")
        chip = 2 * x + y
        is_owner_chip = chip == 0
        step = pl.program_id(0)

        def to_sibling():
            return pltpu.make_async_remote_copy(src_ref=late_ref.at[1 - c], dst_ref=sib_ref, send_sem=send_sems.at[0],
                                                recv_sem=recv_sems.at[0], device_id=(x, y, 1 - c), device_id_type=MESH)

        def to_owner(k):
            return pltpu.make_async_remote_copy(src_ref=pair_ref, dst_ref=parts_ref.at[k], send_sem=send_sems.at[1],
                                                recv_sem=recv_sems.at[k], device_id=(0, 0, c), device_id_type=MESH)

        def own_copy():
            return pltpu.make_async_copy(pair_ref, parts_ref.at[0], local_sems.at[2])

        def gates():
            return _fanout([lambda j: dgt_ref.at[j]], [lambda j: gt_ref.at[j]], g_send, g_recv, g_local)

        @pl.when(step == 0)
        def _():
            to_sibling().start()
            gates()[0]()

        compute()

        @pl.when(step == LATE_MID_STEP)
        def _():
            mine = pltpu.make_async_copy(late_ref.at[c], vbuf.at[0], local_sems.at[0])
            mine.start()
            to_sibling().wait_recv()
            theirs = pltpu.make_async_copy(sib_ref, vbuf.at[1], local_sems.at[1])
            theirs.start()
            mine.wait()
            theirs.wait()
            vbuf[2] = (vbuf[0].astype(f32) + vbuf[1].astype(f32)).astype(bf16)
            pltpu.sync_copy(vbuf.at[2], pair_ref)
            pl.when(is_owner_chip)(lambda: own_copy().start())
            pl.when(jnp.logical_not(is_owner_chip))(lambda: to_owner(chip).start())

        @pl.when(step == nsteps - 1)
        def _():
            @pl.when(is_owner_chip)
            def _():
                for k in range(1, 4):
                    to_owner(k).wait_recv()
                own_copy().wait()

            pl.when(jnp.logical_not(is_owner_chip))(lambda: to_owner(chip).wait_send())
            to_sibling().wait_send()
            gates()[1]()


def _fanout_sems(n):
    return [pltpu.SemaphoreType.DMA(((NDEV - 1) * n,)), pltpu.SemaphoreType.DMA(((NDEV - 1) * n,)), pltpu.SemaphoreType.DMA((n,))]


def _lanes(j):
    return pl.ds(pl.multiple_of(j * 128, 128), 128)


def _gather_first(c_row, cctx_row, wm, bm, wi, gate):
    def body(c_ref, cctx_ref, wm_ref, bm_ref, wi_ref, g_ref, cs_ref, mods_ref, owi, og, call_ref, mine_ref,
             send_sems, recv_sems, local_sems, c_send, c_recv, c_local, m_send, m_recv, m_local):
        x, y, c = lax.axis_index("x"), lax.axis_index("y"), lax.axis_index("c")
        sibling = (x, y, 1 - c)
        chips = [(1 - x, y), (x, 1 - y), (1 - x, 1 - y)]
        index = lambda px, py, pc: 4 * px + 2 * py + pc
        arrays = ((wi_ref, owi), (g_ref, og))
        n = len(arrays)

        def copy(a, k, block, to, own=False):
            src, out = arrays[a]
            return pltpu.make_async_remote_copy(
                src_ref=src if own else out.at[index(*block)], dst_ref=out.at[index(*block)],
                send_sem=send_sems.at[k * n + a], recv_sem=recv_sems.at[k * n + a], device_id=to, device_id_type=MESH)

        c_start, c_finish = _fanout([lambda j: c_ref], [lambda j: call_ref.at[j]], c_send, c_recv, c_local)
        c_start()
        mine = [pltpu.make_async_copy(src, out.at[index(x, y, c)], local_sems.at[a]) for a, (src, out) in enumerate(arrays)]
        first = [copy(a, 0, (x, y, c), sibling, own=True) for a in range(n)]
        first += [copy(a, 1 + j, (x, y, c), (*chip, c), own=True) for j, chip in enumerate(chips) for a in range(n)]
        for cp in mine + first:
            cp.start()

        c_finish()
        cs = jnp.concatenate([call_ref[j] for j in range(NDEV)] + [cctx_ref[...], jnp.zeros((16 - NDEV - 1, D), f32)], axis=0)
        cs_ref[...] = cs
        s, _ = _silu_and_grad(cs)
        mine_ref[...] = _dot_hi(s, wm_ref[...]) + bm_ref[:, pl.ds(pl.multiple_of(_me() * ncol, 128), ncol)]
        m_start, m_finish = _fanout([lambda j: mine_ref], [lambda j: mods_ref.at[j]], m_send, m_recv, m_local)
        m_start()

        passed = []
        for j, chip in enumerate(chips):
            for a in range(n):
                copy(a, 1 + j, (*chip, c), (x, y, c)).wait_recv()
            for a in range(n):
                cp = copy(a, 4 + j, (*chip, c), sibling)
                cp.start()
                passed.append(cp)
        for a in range(n):
            copy(a, 0, sibling, (x, y, c)).wait_recv()
        for j, chip in enumerate(chips):
            for a in range(n):
                copy(a, 4 + j, (*chip, 1 - c), (x, y, c)).wait_recv()
        for cp in first + passed:
            cp.wait_send()
        for cp in mine:
            cp.wait()
        m_finish()

    hbm = pl.BlockSpec(memory_space=pl.ANY)
    vm = pl.BlockSpec(memory_space=pltpu.VMEM)
    ncol = wm.shape[1]
    return pl.pallas_call(
        body, name="gather_first",
        out_shape=(jax.ShapeDtypeStruct((16, D), f32), jax.ShapeDtypeStruct((NDEV, 16, ncol), f32),
                   jax.ShapeDtypeStruct((NDEV,) + wi.shape, bf16), jax.ShapeDtypeStruct((NDEV,) + gate.shape, f32)),
        in_specs=[vm, vm, vm, vm, hbm, hbm], out_specs=(vm, vm, hbm, hbm),
        scratch_shapes=[pltpu.VMEM((NDEV, 1, D), f32), pltpu.VMEM((16, ncol), f32)] + _fanout_sems(2) + _fanout_sems(1) + _fanout_sems(1),
        compiler_params=_cp(),
    )(c_row, cctx_row, wm, bm, wi, gate)


SHARD = INW // NDEV
ROWS_RP = 128


def _overlap(lo, hi, a, b):
    s, e = max(lo, a), min(hi, b)
    return (s, e) if s < e else None


def _repack_w(wg):
    segs = ((0, 1024, PQ), (1024, 1024 + 2 * RANK, PLR), (1024 + 2 * RANK, INW, PZB))

    def body(g_ref, o_ref):
        for j in range(NDEV):
            lo, hi = j * SHARD, (j + 1) * SHARD
            for a, b, pad0 in segs:
                ov = _overlap(lo, hi, a, b)
                if ov:
                    s, e = ov
                    o_ref[:, pad0 + s - a:pad0 + e - a] = g_ref[j, :, s - lo:e - lo]
        o_ref[:, PLR + 2 * RANK:PW] = jnp.zeros((ROWS_RP, PW - PLR - 2 * RANK), bf16)

    return pl.pallas_call(
        body, name="repack_w", grid=(D // ROWS_RP,), out_shape=jax.ShapeDtypeStruct((D, PW), bf16),
        in_specs=[pl.BlockSpec((NDEV, ROWS_RP, SHARD), lambda i: (0, i, 0))],
        out_specs=pl.BlockSpec((ROWS_RP, PW), lambda i: (i, 0)), compiler_params=_cp(("arbitrary",)),
    )(wg)


LATE_END = 1024 + 2 * RANK
LATE_DESTS = 2


def _pack_blocks(o_ref, srcs, dests, dtype):
    for n, j in enumerate(dests):
        lo, hi = j * SHARD, (j + 1) * SHARD
        done = lo
        for a, b, src in srcs:
            ov = _overlap(lo, hi, a, b)
            if ov:
                s, e = ov
                if s > done:
                    o_ref[n, :, done - lo:s - lo] = jnp.zeros((ROWS_RP, s - done), dtype)
                o_ref[n, :, s - lo:e - lo] = src[:, s - a:e - a].astype(dtype)
                done = e
        if done < hi:
            o_ref[n, :, done - lo:hi - lo] = jnp.zeros((ROWS_RP, hi - done), dtype)


class _Pack:
    def __init__(self, extra, outs, fn):
        self.extra, self.outs, self.fn = list(extra), list(outs), fn


def _pack_early():
    def fn(accs, extra, outs):
        zbua, va, zag = accs
        for r0 in range(0, D, ROWS_RP):
            rows = pl.ds(r0, ROWS_RP)
            out, o0 = (outs[0], r0) if r0 < EARLY_A_ROWS else (outs[1], r0 - EARLY_A_ROWS)
            _pack_blocks(out.at[:, pl.ds(o0, ROWS_RP)], ((LATE_END, 2080, zbua.at[rows]), (2080, 2592, va.at[rows]), (2592, INW, zag.at[rows])),
                         range(NDEV), bf16)

    sds = jax.ShapeDtypeStruct
    return _Pack([], [sds((NDEV, EARLY_A_ROWS, SHARD), bf16), sds((NDEV, D - EARLY_A_ROWS, SHARD), bf16)], fn)


def _pack_late(dwk_c, dwv_c, dwl_c):
    def fn(accs, extra, outs):
        qkv_ref, lr_ref = accs
        kc_ref, vc_ref, lc_ref = extra
        for r0 in range(0, D, ROWS_RP):
            rows = pl.ds(r0, ROWS_RP)
            qkv = qkv_ref[rows, :] + jnp.concatenate([jnp.zeros((ROWS_RP, KW), f32), kc_ref[rows, :], vc_ref[rows, :]], axis=1)
            lr = lr_ref[rows, :] + lc_ref[rows, :]
            _pack_blocks(outs[0].at[:, rows], ((0, 1024, qkv), (1024, LATE_END, lr)), range(LATE_DESTS), bf16)

    return _Pack([dwk_c, dwv_c, dwl_c], [jax.ShapeDtypeStruct((LATE_DESTS, D, SHARD), bf16)], fn)


def _mod_tail(dnorm_parts, dmod_parts, dmodc, cs, wm):
    ncol = wm.shape[2]
    n_dn, n_dm = len(dnorm_parts), len(dmod_parts)

    def body(*refs):
        dn_parts, refs = refs[:n_dn], refs[n_dn:]
        dm_parts, refs = refs[:n_dm], refs[n_dm:]
        dmodc_ref, cs_ref, wm_ref, g_wm, g_ng, g_bm, g_cc = refs[:7]
        dn_ref, dmod_ref, a_dn, a_dmod, a_dmodc, dm_rows, gc_ref, a_gc = refs[7:15]
        s1, r1, l1, s2, r2, l2 = refs[15:21]
        dn = dn_parts[0][...]
        for ref in dn_parts[1:]:
            dn = dn + ref[...]
        dn_ref[...] = dn
        dmod_ref[...] = jnp.concatenate([ref[...] for ref in dm_parts], axis=1)
        start, finish = _fanout([lambda j: dn_ref, lambda j: dmod_ref, lambda j: dmodc_ref],
                                [lambda j, r=r: r.at[j] for r in (a_dn, a_dmod, a_dmodc)], s1, r1, l1)
        start()
        finish()

        def total(ref):
            t = ref[0]
            for j in range(1, NDEV):
                t = t + ref[j]
            return t

        g_ng[...] = total(a_dn)
        dmodc_tot = jnp.concatenate([total(a_dmodc), jnp.zeros((1, D), f32)], axis=1)
        g_bm[...] = total(a_dmod) + dmodc_tot
        dm_rows[...] = jnp.concatenate([a_dmod[j] for j in range(NDEV)] + [dmodc_tot, jnp.zeros((16 - NDEV - 1, 3 * D), f32)], axis=0)
        dm = dm_rows[:, pl.ds(pl.multiple_of(_me() * ncol, 128), ncol)]
        s, ds = _silu_and_grad(cs_ref[...])
        part = lax.dot_general(dm[8:9, :], wm_ref[0], (((1,), (1,)), ((), ())), preferred_element_type=f32,
                               precision=lax.Precision.HIGHEST)
        gc_ref[...] = part * ds[8:9, :]
        start2, finish2 = _fanout([lambda j: gc_ref], [lambda j: a_gc.at[j]], s2, r2, l2)
        start2()
        g = lax.dot_general(s, dm, (((0,), (0,)), ((), ())), preferred_element_type=f32, precision=lax.Precision.HIGHEST)
        g_wm[...] = g[None]
        finish2()
        g_cc[...] = total(a_gc)

    sds = jax.ShapeDtypeStruct
    row = lambda n: pltpu.VMEM((NDEV, 1, n), f32)
    return pl.pallas_call(
        body, name="mod_tail", out_shape=(sds(wm.shape, f32), sds((1, D), f32), sds((1, 3 * D), f32), sds((1, D), f32)),
        scratch_shapes=[pltpu.VMEM((1, D), f32), pltpu.VMEM((1, 3 * D), f32), row(D), row(3 * D), row(2 * D),
                        pltpu.VMEM((16, 3 * D), f32), pltpu.VMEM((1, D), f32), row(D)] + _fanout_sems(3) + _fanout_sems(1),
        compiler_params=_cp(),
    )(*dnorm_parts, *dmod_parts, dmodc, cs, wm)


def _adam_update(g, w_ref, m_ref, v_ref, go_ref, d_ref, mo_ref, vo_ref):
    c1 = 1.0 / (1.0 - ADAM_B1 ** ADAM_STEP)
    c2 = 1.0 / (1.0 - ADAM_B2 ** ADAM_STEP)
    mn = ADAM_B1 * m_ref[...] + (1.0 - ADAM_B1) * g
    vn = ADAM_B2 * v_ref[...] + (1.0 - ADAM_B2) * (g * g)
    go_ref[...] = g
    mo_ref[...] = mn
    vo_ref[...] = vn
    d_ref[...] = -ADAM_LR * ((mn * c1) / (jnp.sqrt(vn * c2) + ADAM_EPS) + ADAM_WD * w_ref[...])


def _adam_w_in(parts_a, parts_b, parts_late, w, m, v):
    na = EARLY_A_ROWS // ROWS_RP
    n = D // ROWS_RP
    whole = SHARD // 128 * 128

    def body(a_ref, b_ref, l_ref, w_hbm, m_hbm, v_hbm, go_hbm, d_hbm, mo_hbm, vo_hbm, ibuf, obuf, isem, osem):
        step = pl.program_id(0)

        def cols(t):
            return pl.ds(pl.multiple_of(t * ROWS_RP, ROWS_RP), ROWS_RP)

        def fetch(t):
            return [pltpu.make_async_copy(r.at[:, 0, cols(t)], ibuf.at[t % 2, k], isem.at[t % 2, k])
                    for k, r in enumerate((w_hbm, m_hbm, v_hbm))]

        def flush(t):
            return [pltpu.make_async_copy(obuf.at[t % 2, k], r.at[:, 0, cols(t)], osem.at[t % 2, k])
                    for k, r in enumerate((go_hbm, d_hbm, mo_hbm, vo_hbm))]

        def start(cps):
            for cp in cps:
                cp.start()

        def wait(cps):
            for cp in cps:
                cp.wait()

        pl.when(step == 0)(lambda: start(fetch(step)))
        pl.when(step + 1 < n)(lambda: start(fetch(step + 1)))
        me = _me()
        first = step < na
        early = jnp.where(first, a_ref[0], b_ref[0]).astype(f32)
        for i in range(1, NDEV):
            early = early + jnp.where(first, a_ref[i], b_ref[i]).astype(f32)
        late = l_ref[0].astype(f32)
        for i in range(1, 4):
            late = late + l_ref[i].astype(f32)
        g = jnp.where(me >= LATE_DESTS - 1, early, 0.0) + jnp.where(me < LATE_DESTS, late, 0.0)
        gt = jnp.concatenate([g[:, c:c + 128].T for c in range(0, whole, 128)] + [g[:, SHARD - 128:].T[128 - (SHARD - whole):]],
                             axis=0)
        wait(fetch(step))
        pl.when(step >= 2)(lambda: wait(flush(step - 2)))
        slot = step % 2
        _adam_update(gt, *(ibuf.at[slot, k] for k in range(3)), *(obuf.at[slot, k] for k in range(4)))
        start(flush(step))

        @pl.when(step == n - 1)
        def _():
            wait(flush(step - 1))
            wait(flush(step))

    hbm = pl.BlockSpec(memory_space=pl.ANY)
    res = pl.pallas_call(
        body, name="adam_w_in", grid=(n,), out_shape=tuple(jax.ShapeDtypeStruct((SHARD, 1, D), f32) for _ in range(4)),
        in_specs=[pl.BlockSpec((NDEV, ROWS_RP, SHARD), lambda i: (0, jnp.minimum(i, na - 1), 0)),
                  pl.BlockSpec((NDEV, ROWS_RP, SHARD), lambda i: (0, jnp.maximum(i - na, 0), 0)),
                  pl.BlockSpec((4, ROWS_RP, SHARD), lambda i: (0, i, 0)), hbm, hbm, hbm],
        out_specs=(hbm, hbm, hbm, hbm),
        scratch_shapes=[pltpu.VMEM((2, 3, SHARD, ROWS_RP), f32), pltpu.VMEM((2, 4, SHARD, ROWS_RP), f32),
                        pltpu.SemaphoreType.DMA((2, 3)), pltpu.SemaphoreType.DMA((2, 4))],
        compiler_params=_cp(("arbitrary",)),
    )(parts_a, parts_b, parts_late, *(jnp.transpose(t, (2, 0, 1)) for t in (w, m, v)))
    return tuple(jnp.transpose(t, (1, 2, 0)) for t in res)


def _adam_params(parts, ws, ms, vs, name):
    p = parts.shape[0]
    k = len(ws)
    nrows = [w.size // 128 for w in ws]
    starts = [sum(nrows[:i]) for i in range(k)]

    def shaped(g, shape):
        if len(shape) == 2:
            return jnp.concatenate([g[r:r + 1] for r in range(g.shape[0])], axis=1)
        return g.reshape(shape)

    def body(*refs):
        g_ref = refs[0]
        w_refs, m_refs, v_refs = refs[1:1 + k], refs[1 + k:1 + 2 * k], refs[1 + 2 * k:1 + 3 * k]
        outs = refs[1 + 3 * k:]
        for i in range(k):
            rows = slice(starts[i], starts[i] + nrows[i])
            g = g_ref[0, rows, :]
            for j in range(1, p):
                g = g + g_ref[j, rows, :]
            _adam_update(shaped(g, ws[i].shape), w_refs[i], m_refs[i], v_refs[i], outs[i], outs[k + i], outs[2 * k + i], outs[3 * k + i])

    vm = pl.BlockSpec(memory_space=pltpu.VMEM)
    res = pl.pallas_call(
        body, name=name, out_shape=tuple(jax.ShapeDtypeStruct(w.shape, f32) for _ in range(4) for w in ws),
        in_specs=[vm] * (1 + 3 * k), out_specs=(vm,) * (4 * k), compiler_params=_cp(),
    )(parts, *ws, *ms, *vs)
    return [res[i * k:(i + 1) * k] for i in range(4)]


def _adam_blocks(parts, ws, ms, vs, name):
    k = len(ws)

    def body(*refs):
        g_refs, w_refs, m_refs, v_refs = refs[0:k], refs[k:2 * k], refs[2 * k:3 * k], refs[3 * k:4 * k]
        outs = refs[4 * k:]
        for i in range(k):
            g = g_refs[i][0].astype(f32)
            for j in range(1, parts[i].shape[0]):
                g = g + g_refs[i][j].astype(f32)
            _adam_update(g[None], w_refs[i], m_refs[i], v_refs[i], outs[i], outs[k + i], outs[2 * k + i], outs[3 * k + i])

    vm = pl.BlockSpec(memory_space=pltpu.VMEM)
    res = pl.pallas_call(
        body, name=name, out_shape=tuple(jax.ShapeDtypeStruct(w.shape, f32) for _ in range(4) for w in ws),
        in_specs=[vm] * (4 * k), out_specs=(vm,) * (4 * k), compiler_params=_cp(),
    )(*parts, *ws, *ms, *vs)
    return [res[i * k:(i + 1) * k] for i in range(4)]


def _adam_plain(gs, ws, ms, vs, name):
    k = len(ws)

    def body(*refs):
        g_refs, w_refs, m_refs, v_refs = refs[0:k], refs[k:2 * k], refs[2 * k:3 * k], refs[3 * k:4 * k]
        outs = refs[4 * k:]
        for i in range(k):
            _adam_update(g_refs[i][...], w_refs[i], m_refs[i], v_refs[i], outs[i], outs[k + i], outs[2 * k + i], outs[3 * k + i])

    vm = pl.BlockSpec(memory_space=pltpu.VMEM)
    res = pl.pallas_call(
        body, name=name, out_shape=tuple(jax.ShapeDtypeStruct(w.shape, f32) for _ in range(4) for w in ws),
        in_specs=[vm] * (4 * k), out_specs=(vm,) * (4 * k), compiler_params=_cp(),
    )(*gs, *ws, *ms, *vs)
    return [res[i * k:(i + 1) * k] for i in range(4)]


def _adam_gate(parts, w2, gb, m_w2, m_gb, v_w2, v_gb):
    per = 128 // 32

    def body(p_ref, w2_ref, gb_ref, mw_ref, mb_ref, vw_ref, vb_ref, *outs):
        g = p_ref[0]
        for j in range(1, parts.shape[0]):
            g = g + p_ref[j]
        for d in range(2):
            gd = jnp.concatenate([g[(d * RANK + r) // per:(d * RANK + r) // per + 1, (r % per) * 32:(r % per + 1) * 32]
                                  for r in range(RANK)], axis=0)
            _adam_update(gd, *(t.at[0, d] for t in (w2_ref, mw_ref, vw_ref) + outs[0:4]))
        last = 2 * RANK // per
        gbv = jnp.concatenate([g[last:last + 1, d * 32:(d + 1) * 32] for d in range(2)], axis=0)
        _adam_update(gbv, *(t.at[0] for t in (gb_ref, mb_ref, vb_ref) + outs[4:8]))

    vm = pl.BlockSpec(memory_space=pltpu.VMEM)
    return pl.pallas_call(
        body, name="adam_gate", out_shape=tuple(jax.ShapeDtypeStruct(w.shape, f32) for w in (w2, gb) for _ in range(4)),
        in_specs=[vm] * 7, out_specs=(vm,) * 8, compiler_params=_cp(),
    )(parts, w2, gb, m_w2, m_gb, v_w2, v_gb)


def _in_proj(x, g, scale, shift, w_pad, tl, comm):
    l = x.shape[0]
    c_in, c_out, c_sems = comm.specs()

    def body(*refs):
        x_ref, g_ref, sc_ref, sh_ref, w_ref = refs[:5]
        cin = refs[5:5 + len(c_in)]
        p_ref, h_ref = refs[5 + len(c_in):7 + len(c_in)]
        cout = refs[7 + len(c_in):7 + len(c_in) + comm.n]
        sems = refs[7 + len(c_in) + comm.n:]

        def compute():
            xv = x_ref[...]
            r = lax.rsqrt(jnp.mean(xv * xv, axis=-1, keepdims=True) + EPS)
            h = (xv * r) * (g_ref[...] * (1.0 + sc_ref[...])) + sh_ref[...]
            hb = h.astype(bf16)
            h_ref[...] = hb
            p_ref[...] = jnp.dot(hb, w_ref[...], preferred_element_type=f32).astype(bf16)

        comm.run(cin, cout, sems, l // tl, compute)

    vec = pl.BlockSpec((1, D), lambda i: (0, 0))
    return pl.pallas_call(
        body, name="in_proj", grid=(l // tl,),
        out_shape=(jax.ShapeDtypeStruct((l, PW), bf16), jax.ShapeDtypeStruct((l, D), bf16)) + tuple(comm.outs),
        in_specs=[pl.BlockSpec((tl, D), lambda i: (i, 0)), vec, vec, vec, pl.BlockSpec((D, PW), lambda i: (0, 0))] + c_in,
        out_specs=(pl.BlockSpec((tl, PW), lambda i: (i, 0)), pl.BlockSpec((tl, D), lambda i: (i, 0))) + tuple(c_out),
        scratch_shapes=c_sems, compiler_params=_cp(("arbitrary",)),
    )(x, g, scale, shift, w_pad, *comm.ins)


def _tri(rev):
    i = lax.broadcasted_iota(jnp.int32, (CH, CH), 0)
    j = lax.broadcasted_iota(jnp.int32, (CH, CH), 1)
    return jnp.where((j >= i) if rev else (j <= i), 1.0, 0.0).astype(f32)


def _head_masks():
    lane = lax.broadcasted_iota(jnp.int32, (1, KW), 1) // HK
    return [jnp.where(lane == h, 1.0, 0.0).astype(f32) for h in range(NH)]


def _block_diag():
    r = lax.broadcasted_iota(jnp.int32, (VW, KW), 0) // HV
    c = lax.broadcasted_iota(jnp.int32, (VW, KW), 1) // HK
    return jnp.where(r == c, 1.0, 0.0).astype(f32)


def _decay(lr, w2, gb, tri, rev):
    logits = _dot(lr, w2) + gb
    a = _log_sigmoid(logits) * (1.0 / TAU)
    c = _dot_hi(tri, a)
    cl = c[0:1, :] if rev else c[CH - 1:CH, :]
    return logits, c, cl


def _state_fwd(k, v, c, cl, st, bd):
    return st * jnp.exp(cl) + bd * _tn(v, k * jnp.exp(cl - c))


def _state_bwd(k, v, c, cl, st0, dst, trit):
    ecl = jnp.exp(cl)
    edec = jnp.exp(cl - c)
    kdec = k * edec
    dv = _nt(kdec, dst)
    dkdec = _dot(v, dst)
    dcl = jnp.sum(dst * st0, axis=0, keepdims=True) * ecl + jnp.sum(dkdec * kdec, axis=0, keepdims=True)
    da = _dot_hi(trit, -dkdec * kdec) + dcl
    return dkdec * edec, dv, da, dst * ecl


def _bdot(a, b):
    return lax.dot_general(a.astype(bf16), b.astype(bf16), (((2,), (1,)), ((0,), (0,))), preferred_element_type=f32)


def _bnt(a, b):
    return lax.dot_general(a.astype(bf16), b.astype(bf16), (((2,), (2,)), ((0,), (0,))), preferred_element_type=f32)


def _btn(a, b):
    return lax.dot_general(a.astype(bf16), b.astype(bf16), (((1,), (1,)), ((0,), (0,))), preferred_element_type=f32)


def _scan_chunks(x, rev):
    nc = x.shape[0]
    hi = x.astype(bf16)
    r1 = x - hi.astype(f32)
    mid = r1.astype(bf16)
    lo = (r1 - mid.astype(f32)).astype(bf16)
    terms = jnp.concatenate([hi, mid, lo], axis=1)
    tri3 = jnp.broadcast_to(jnp.concatenate([_tri(rev)] * 3, axis=1).astype(bf16)[None], (nc, CH, 3 * CH))
    return lax.dot_general(tri3, terms, (((2,), (1,)), ((0,), (0,))), preferred_element_type=f32)


class _Tile:
    pass


def _tile_prep(q_ref, k_ref, lr_ref, w2_ref, gb_ref, rev, nc):
    t = _Tile()
    tg = nc * CH
    t.logits = _dot(lr_ref[...], w2_ref[...]) + gb_ref[...]
    c = _scan_chunks((_log_sigmoid(t.logits) * (1.0 / TAU)).reshape(nc, CH, KW), rev)
    cl = c[:, 0:1, :] if rev else c[:, CH - 1:CH, :]
    k = k_ref[...].astype(f32).reshape(nc, CH, KW)
    t.ec, t.enc, t.edec, t.ecl = jnp.exp(c), jnp.exp(-c), jnp.exp(cl - c), jnp.exp(cl)
    t.qd = q_ref[...].astype(f32).reshape(nc, CH, KW) * t.ec * QSCALE
    t.kd = k * t.enc
    t.kdec = k * t.edec
    hm = _head_masks()
    t.tri4 = jnp.concatenate([_tri(rev)] * NH, axis=0)[None]
    t.qs = jnp.concatenate([t.qd * hm[h] for h in range(NH)], axis=1)
    t.pst = _bnt(t.qs, t.kd) * t.tri4
    return t


def _gla_fwd(p, w2p, gb, s0, rev, tg, name):
    l = p.shape[0]
    nb, nc = l // tg, tg // CH

    def body(q_ref, k_ref, v_ref, lr_ref, w2_ref, gb_ref, s0_ref, o_ref, st_ref, st):
        @pl.when(pl.program_id(0) == 0)
        def _():
            st[...] = s0_ref[...]

        t = _tile_prep(q_ref, k_ref, lr_ref, w2_ref, gb_ref, rev, nc)
        v = v_ref[...].reshape(nc, CH, VW)
        intra = jnp.concatenate([_bdot(t.pst[:, h * CH:(h + 1) * CH], v[:, :, h * HV:(h + 1) * HV]) for h in range(NH)], axis=2)
        bd = _block_diag()
        s = st[...]
        for n in (range(nc - 1, -1, -1) if rev else range(nc)):
            st_ref[n] = s.astype(bf16)
            s = s * t.ecl[n] + bd * _tn(v[n], t.kdec[n])
        st[...] = s
        o_ref[...] = (_bnt(t.qd, st_ref[...]) + intra).reshape(tg, VW).astype(bf16)

    blk = (lambda i: nb - 1 - i) if rev else (lambda i: i)
    return pl.pallas_call(
        body, name=name, grid=(nb,),
        out_shape=(jax.ShapeDtypeStruct((l, VW), bf16), jax.ShapeDtypeStruct((l // CH, VW, KW), bf16)),
        in_specs=[pl.BlockSpec((tg, KW), lambda i: (blk(i), PQ // KW)), pl.BlockSpec((tg, KW), lambda i: (blk(i), PK // KW)),
                  pl.BlockSpec((tg, VW), lambda i: (blk(i), PV // VW)), pl.BlockSpec((tg, LRW), lambda i: (blk(i), PLR // LRW)),
                  pl.BlockSpec((LRW, KW), lambda i: (0, 0)), pl.BlockSpec((1, KW), lambda i: (0, 0)),
                  pl.BlockSpec((VW, KW), lambda i: (0, 0))],
        out_specs=(pl.BlockSpec((tg, VW), lambda i: (blk(i), 0)), pl.BlockSpec((nc, VW, KW), lambda i: (blk(i), 0, 0))),
        scratch_shapes=[pltpu.VMEM((VW, KW), f32)],
        compiler_params=_cp(("arbitrary",)),
    )(p, p, p, p, w2p, gb, s0)


def _gla_bwd(p, do, states, w2p, gb, prev, rev, tg, name, comm):
    l = p.shape[0]
    nb, nc = l // tg, tg // CH
    out_dt = bf16
    c_in, c_out, c_sems = comm.specs()

    def body(*refs):
        q_ref, k_ref, v_ref, lr_ref, do_ref, st_ref, w2_ref, gb_ref = refs[:8]
        refs = refs[8:]
        if prev is not None:
            pq_ref, pl_ref = refs[:2]
            refs = refs[2:]
        cin, refs = refs[:len(c_in)], refs[len(c_in):]
        dqkv_ref, dlr_ref, dw2_ref, dgb_ref, ds0_ref = refs[:5]
        cout, dst, ds_buf, sems = refs[5:5 + comm.n], refs[5 + comm.n], refs[6 + comm.n], refs[7 + comm.n:]
        comm.run(cin, cout, sems, nb, lambda: compute(q_ref, k_ref, v_ref, lr_ref, do_ref, st_ref, w2_ref, gb_ref,
                                                      pq_ref if prev is not None else None, pl_ref if prev is not None else None,
                                                      dqkv_ref, dlr_ref, dw2_ref, dgb_ref, ds0_ref, dst, ds_buf))

    def compute(q_ref, k_ref, v_ref, lr_ref, do_ref, st_ref, w2_ref, gb_ref, pq_ref, pl_ref,
                dqkv_ref, dlr_ref, dw2_ref, dgb_ref, ds0_ref, dst, ds_buf):
        @pl.when(pl.program_id(0) == 0)
        def _():
            dst[...] = jnp.zeros_like(dst)
            dw2_ref[...] = jnp.zeros_like(dw2_ref)
            dgb_ref[...] = jnp.zeros_like(dgb_ref)

        t = _tile_prep(q_ref, k_ref, lr_ref, w2_ref, gb_ref, rev, nc)
        hm = _head_masks()
        v = v_ref[...].reshape(nc, CH, VW)
        do = do_ref[...].reshape(nc, CH, VW)
        heads = lambda a, h: a[:, :, h * HV:(h + 1) * HV]
        dpst = jnp.concatenate([_bnt(heads(do, h), heads(v, h)) for h in range(NH)], axis=1) * t.tri4
        dv = jnp.concatenate([_btn(t.pst[:, h * CH:(h + 1) * CH], heads(do, h)) for h in range(NH)], axis=2)
        dqd = _bdot(do, st_ref[...])
        for h in range(NH):
            dqd = dqd + hm[h] * _bdot(dpst[:, h * CH:(h + 1) * CH], t.kd)
        dkd = _btn(dpst, t.qs)
        bd = _block_diag()
        d = dst[...]
        for n in (range(nc) if rev else range(nc - 1, -1, -1)):
            ds_buf[n] = d
            d = d * t.ecl[n] + bd * _tn(do[n], t.qd[n])
        dst[...] = d
        ds0_ref[...] = d
        ds = ds_buf[...]
        dv = dv + _bnt(t.kdec, ds)
        dkdec = _bdot(v, ds)
        dcl = jnp.sum(ds * st_ref[...].astype(f32), axis=1, keepdims=True) * t.ecl + jnp.sum(dkdec * t.kdec, axis=1, keepdims=True)
        dc = dqd * t.qd - dkd * t.kd - dkdec * t.kdec
        da = _scan_chunks(dc, not rev) + dcl
        dq = dqd * t.ec * QSCALE
        dk = dkd * t.enc + dkdec * t.edec
        dlog = da.reshape(tg, KW) * _sigmoid(-t.logits) * (1.0 / TAU)
        dlr = _nt(dlog, w2_ref[...])
        dw2_ref[...] += _tn(lr_ref[...], dlog)
        dgb_ref[...] += jnp.sum(dlog, axis=0, keepdims=True)
        dqkv = jnp.concatenate([dq, dk, dv], axis=2).reshape(tg, 2 * KW + VW)
        if prev is not None:
            dqkv = dqkv + pq_ref[...]
            dlr = dlr + pl_ref[...]
        dqkv_ref[...] = dqkv.astype(out_dt)
        dlr_ref[...] = dlr.astype(out_dt)

    blk = (lambda i: i) if rev else (lambda i: nb - 1 - i)
    in_specs = [pl.BlockSpec((tg, KW), lambda i: (blk(i), PQ // KW)), pl.BlockSpec((tg, KW), lambda i: (blk(i), PK // KW)),
                pl.BlockSpec((tg, VW), lambda i: (blk(i), PV // VW)), pl.BlockSpec((tg, LRW), lambda i: (blk(i), PLR // LRW)),
                pl.BlockSpec((tg, VW), lambda i: (blk(i), 0)), pl.BlockSpec((nc, VW, KW), lambda i: (blk(i), 0, 0)),
                pl.BlockSpec((LRW, KW), lambda i: (0, 0)), pl.BlockSpec((1, KW), lambda i: (0, 0))]
    args = [p, p, p, p, do, states, w2p, gb]
    if prev is not None:
        in_specs += [pl.BlockSpec((tg, 2 * KW + VW), lambda i: (blk(i), 0)), pl.BlockSpec((tg, LRW), lambda i: (blk(i), 0))]
        args += list(prev)
    return pl.pallas_call(
        body, name=name, grid=(nb,),
        out_shape=(jax.ShapeDtypeStruct((l, 2 * KW + VW), out_dt), jax.ShapeDtypeStruct((l, LRW), out_dt),
                   jax.ShapeDtypeStruct((LRW, KW), f32), jax.ShapeDtypeStruct((1, KW), f32), jax.ShapeDtypeStruct((VW, KW), f32))
        + tuple(comm.outs),
        in_specs=in_specs + c_in,
        out_specs=(pl.BlockSpec((tg, 2 * KW + VW), lambda i: (blk(i), 0)), pl.BlockSpec((tg, LRW), lambda i: (blk(i), 0)),
                   pl.BlockSpec((LRW, KW), lambda i: (0, 0)), pl.BlockSpec((1, KW), lambda i: (0, 0)),
                   pl.BlockSpec((VW, KW), lambda i: (0, 0))) + tuple(c_out),
        scratch_shapes=[pltpu.VMEM((VW, KW), f32), pltpu.VMEM((nc, VW, KW), f32)] + c_sems,
        compiler_params=_cp(("arbitrary",)),
    )(*args, *comm.ins)


def _ctx_hidden(ctx_ref, g_ref, sc_ref, sh_ref):
    xv = ctx_ref[...]
    r = lax.rsqrt(jnp.mean(xv * xv, axis=-1, keepdims=True) + EPS)
    xn = xv * r
    return xn, xn * (g_ref[...] * (1.0 + sc_ref[...])) + sh_ref[...]


_CTX_W_SPECS = [pl.BlockSpec((D, KW), lambda i: (0, PK // KW)), pl.BlockSpec((D, VW), lambda i: (0, PV // VW)),
                pl.BlockSpec((D, LRW), lambda i: (0, PLR // LRW))]


def _ctx_fwd(ctx, g, scale, shift, w_pad, w2f, w2b, gbf, gbb):
    ncc = CTX // CH

    def body(ctx_ref, g_ref, sc_ref, sh_ref, wk_ref, wv_ref, wl_ref, w2f_ref, w2b_ref, gbf_ref, gbb_ref, sf_ref, sb_ref):
        _, hc = _ctx_hidden(ctx_ref, g_ref, sc_ref, sh_ref)
        k, v, lr = _dot(hc, wk_ref[...]), _dot(hc, wv_ref[...]), _dot(hc, wl_ref[...])
        bd = _block_diag()
        for rev, w2_ref, gb_ref, out in ((False, w2f_ref, gbf_ref, sf_ref), (True, w2b_ref, gbb_ref, sb_ref)):
            tri = _tri(rev)
            st = jnp.zeros((VW, KW), f32)
            for j in (range(ncc - 1, -1, -1) if rev else range(ncc)):
                rows = slice(j * CH, (j + 1) * CH)
                _, c, cl = _decay(lr[rows], w2_ref[...], gb_ref[...], tri, rev)
                st = _state_fwd(k[rows], v[rows], c, cl, st, bd)
            out[...] = st

    vec = pl.BlockSpec((1, D), lambda i: (0, 0))
    w2s = pl.BlockSpec((LRW, KW), lambda i: (0, 0))
    gbs = pl.BlockSpec((1, KW), lambda i: (0, 0))
    sts = pl.BlockSpec((VW, KW), lambda i: (0, 0))
    return pl.pallas_call(
        body, name="ctx_fwd", grid=(1,), out_shape=(jax.ShapeDtypeStruct((VW, KW), f32),) * 2,
        in_specs=[pl.BlockSpec((CTX, D), lambda i: (0, 0)), vec, vec, vec] + _CTX_W_SPECS + [w2s, w2s, gbs, gbs],
        out_specs=(sts, sts), compiler_params=_cp(("arbitrary",)),
    )(ctx, g, scale, shift, w_pad, w_pad, w_pad, w2f, w2b, gbf, gbb)


def _ctx_bwd(ctx, g, scale, shift, w_pad, w2f, w2b, gbf, gbb, dsf, dsb):
    ncc = CTX // CH

    def body(ctx_ref, g_ref, sc_ref, sh_ref, wk_ref, wv_ref, wl_ref, w2f_ref, w2b_ref, gbf_ref, gbb_ref, dsf_ref, dsb_ref,
             dwk_ref, dwv_ref, dwl_ref, dmod_ref, dg_ref, dw2_ref, dgb_ref):
        xn, hc = _ctx_hidden(ctx_ref, g_ref, sc_ref, sh_ref)
        k, v, lr = _dot(hc, wk_ref[...]), _dot(hc, wv_ref[...]), _dot(hc, wl_ref[...])
        bd = _block_diag()
        dk_rows, dv_rows, dl_rows = [None] * ncc, [None] * ncc, [None] * ncc
        for d, (rev, w2_ref, gb_ref, ds_ref) in enumerate(((False, w2f_ref, gbf_ref, dsf_ref), (True, w2b_ref, gbb_ref, dsb_ref))):
            tri = _tri(rev)
            order = list(range(ncc - 1, -1, -1) if rev else range(ncc))
            st, saved = jnp.zeros((VW, KW), f32), {}
            for j in order:
                rows = slice(j * CH, (j + 1) * CH)
                logits, c, cl = _decay(lr[rows], w2_ref[...], gb_ref[...], tri, rev)
                saved[j] = (logits, c, cl, st)
                st = _state_fwd(k[rows], v[rows], c, cl, st, bd)
            dst = ds_ref[...]
            dw2 = jnp.zeros((LRW, KW), f32)
            dgb = jnp.zeros((1, KW), f32)
            for j in reversed(order):
                rows = slice(j * CH, (j + 1) * CH)
                logits, c, cl, st0 = saved[j]
                dk, dv, da, dst = _state_bwd(k[rows], v[rows], c, cl, st0, dst, _tri(not rev))
                dlog = da * _sigmoid(-logits) * (1.0 / TAU)
                dl = _nt(dlog, w2_ref[...])
                dw2 = dw2 + _tn(lr[rows], dlog)
                dgb = dgb + jnp.sum(dlog, axis=0, keepdims=True)
                dk_rows[j] = dk if dk_rows[j] is None else dk_rows[j] + dk
                dv_rows[j] = dv if dv_rows[j] is None else dv_rows[j] + dv
                dl_rows[j] = dl if dl_rows[j] is None else dl_rows[j] + dl
            dw2_ref[d] = dw2
            dgb_ref[d] = dgb
        dk, dv, dl = (jnp.concatenate(t, axis=0) for t in (dk_rows, dv_rows, dl_rows))
        dwk_ref[...] = _tn(hc, dk)
        dwv_ref[...] = _tn(hc, dv)
        dwl_ref[...] = _tn(hc, dl)
        dh = _nt(dk, wk_ref[...]) + _nt(dv, wv_ref[...]) + _nt(dl, wl_ref[...])
        gx = dh * xn
        dmod_ref[:, 0:D] = jnp.sum(dh, axis=0, keepdims=True)
        dmod_ref[:, D:2 * D] = jnp.sum(gx, axis=0, keepdims=True) * g_ref[...]
        dg_ref[...] = jnp.sum(gx, axis=0, keepdims=True) * (1.0 + sc_ref[...])

    vec = pl.BlockSpec((1, D), lambda i: (0, 0))
    w2s = pl.BlockSpec((LRW, KW), lambda i: (0, 0))
    gbs = pl.BlockSpec((1, KW), lambda i: (0, 0))
    sts = pl.BlockSpec((VW, KW), lambda i: (0, 0))
    full = lambda *s: pl.BlockSpec(s, lambda i: (0,) * len(s))
    return pl.pallas_call(
        body, name="ctx_bwd", grid=(1,),
        out_shape=(jax.ShapeDtypeStruct((D, KW), f32), jax.ShapeDtypeStruct((D, VW), f32), jax.ShapeDtypeStruct((D, LRW), f32),
                   jax.ShapeDtypeStruct((1, 2 * D), f32), jax.ShapeDtypeStruct((1, D), f32),
                   jax.ShapeDtypeStruct((2, LRW, KW), f32), jax.ShapeDtypeStruct((2, 1, KW), f32)),
        in_specs=[pl.BlockSpec((CTX, D), lambda i: (0, 0)), vec, vec, vec] + _CTX_W_SPECS + [w2s, w2s, gbs, gbs, sts, sts],
        out_specs=(full(D, KW), full(D, VW), full(D, LRW), full(1, 2 * D), full(1, D), full(2, LRW, KW), full(2, 1, KW)),
        compiler_params=_cp(("arbitrary",)),
    )(ctx, g, scale, shift, w_pad, w_pad, w_pad, w2f, w2b, gbf, gbb, dsf, dsb)


def _layernorm(va, g, b):
    mu = jnp.mean(va, axis=-1, keepdims=True)
    xc = va - mu
    rstd = lax.rsqrt(jnp.mean(xc * xc, axis=-1, keepdims=True) + EPS)
    vhat = xc * rstd
    return vhat, rstd, vhat * g + b


MIX_PIECES = 8


def _mix_fwd(p, ln_g, ln_b, ws, bs_t):
    l = p.shape[0]
    half = AW // 2
    rp = l // MIX_PIECES
    cpp = rp // ACH

    def body(p_ref, g_ref, b_ref, ws_ref, bs_ref, sv_hbm, va_buf, col_buf, sv_buf, in_sems, out_sems):
        piece = lambda i: pl.ds(pl.multiple_of(i * rp, rp), rp)
        load = lambda i: pltpu.make_async_copy(p_ref.at[piece(i), pl.ds(PVA, AW)], va_buf.at[piece(i)], in_sems.at[i])
        store = lambda i: pltpu.make_async_copy(sv_buf.at[piece(i)], sv_hbm.at[piece(i)], out_sems.at[i])
        for i in range(MIX_PIECES):
            load(i).start()

        def rows_piece(i, carry):
            load(i).wait()
            for j in range(cpp):
                rows = pl.ds(pl.multiple_of(i * rp + j * ACH, ACH), ACH)
                _, _, vn = _layernorm(va_buf[rows, :].astype(f32), g_ref[...], b_ref[...])
                for gi in range(2):
                    sl = slice(gi * ACH, (gi + 1) * ACH)
                    sv_buf[rows, sl] = (_dot(ws_ref[gi], vn[:, sl]) + bs_ref[:, gi:gi + 1]).astype(bf16)
                col_buf[0, rows, :] = vn[:, half:half + ACH]
                col_buf[1, rows, :] = vn[:, half + ACH:]
            return carry

        lax.fori_loop(0, MIX_PIECES, rows_piece, 0)

        def cols_step(cidx, carry):
            rows = pl.ds(cidx, ACH, stride=GW)
            for gi in range(2, 4):
                col_buf[gi - 2, rows, :] = _dot(ws_ref[gi], col_buf[gi - 2, rows, :]) + bs_ref[:, gi:gi + 1]
            return carry

        lax.fori_loop(0, GW, cols_step, 0, unroll=8)

        def out_piece(i, carry):
            for j in range(cpp):
                rows = pl.ds(pl.multiple_of(i * rp + j * ACH, ACH), ACH)
                sv_buf[rows, half:half + ACH] = col_buf[0, rows, :].astype(bf16)
                sv_buf[rows, half + ACH:] = col_buf[1, rows, :].astype(bf16)
            store(i).start()
            return carry

        lax.fori_loop(0, MIX_PIECES, out_piece, 0)
        for i in range(MIX_PIECES):
            store(i).wait()

    vm = pl.BlockSpec(memory_space=pltpu.VMEM)
    hbm = pl.BlockSpec(memory_space=pl.ANY)
    return pl.pallas_call(
        body, name="mix_fwd", out_shape=jax.ShapeDtypeStruct((l, AW), bf16),
        in_specs=[hbm, vm, vm, vm, vm], out_specs=hbm,
        scratch_shapes=[pltpu.VMEM((l, AW), bf16), pltpu.VMEM((2, l, ACH), f32), pltpu.VMEM((l, AW), bf16),
                        pltpu.SemaphoreType.DMA((MIX_PIECES,)), pltpu.SemaphoreType.DMA((MIX_PIECES,))],
        compiler_params=_cp(),
    )(p, ln_g, ln_b, ws, bs_t)


def _mix_bwd(p, dsv, ln_g, ln_b, ws_t):
    l = p.shape[0]
    half = AW // 2
    rp = l // MIX_PIECES
    cpp = rp // ACH

    def body(p_ref, dsv_hbm, g_ref, b_ref, wst_ref, dva_hbm, dws_ref, dbs_ref, dg_ref, db_ref,
             va_buf, dsv_buf, vn_col, ds_col, dva_buf, va_sems, ds_sems, out_sems):
        piece = lambda i: pl.ds(pl.multiple_of(i * rp, rp), rp)
        load_va = lambda i: pltpu.make_async_copy(p_ref.at[piece(i), pl.ds(PVA, AW)], va_buf.at[piece(i)], va_sems.at[i])
        load_ds = lambda i: pltpu.make_async_copy(dsv_hbm.at[piece(i)], dsv_buf.at[piece(i)], ds_sems.at[i])
        store = lambda i: pltpu.make_async_copy(dva_buf.at[piece(i)], dva_hbm.at[piece(i)], out_sems.at[i])
        for i in range(MIX_PIECES):
            load_va(i).start()
            load_ds(i).start()
        dws_ref[...] = jnp.zeros_like(dws_ref)
        dbs_ref[...] = jnp.zeros_like(dbs_ref)
        dg_ref[...] = jnp.zeros_like(dg_ref)
        db_ref[...] = jnp.zeros_like(db_ref)

        def rows_piece(i, carry):
            load_va(i).wait()
            load_ds(i).wait()
            for j in range(cpp):
                rows = pl.ds(pl.multiple_of(i * rp + j * ACH, ACH), ACH)
                _, _, vn = _layernorm(va_buf[rows, :].astype(f32), g_ref[...], b_ref[...])
                ds = dsv_buf[rows, :].astype(f32)
                for gi in range(2):
                    sl = slice(gi * ACH, (gi + 1) * ACH)
                    dws_ref[gi] += _nt(ds[:, sl], vn[:, sl])
                    dbs_ref[gi] += ds[:, sl]
                for gi in range(2):
                    sl = slice(half + gi * ACH, half + (gi + 1) * ACH)
                    vn_col[gi, rows, :] = vn[:, sl]
                    ds_col[gi, rows, :] = ds[:, sl]
            return carry

        lax.fori_loop(0, MIX_PIECES, rows_piece, 0)

        def cols_step(cidx, carry):
            rows = pl.ds(cidx, ACH, stride=GW)
            for gi in range(2, 4):
                ds = ds_col[gi - 2, rows, :]
                dws_ref[gi] += _nt(ds, vn_col[gi - 2, rows, :])
                dbs_ref[gi] += ds
                ds_col[gi - 2, rows, :] = _dot(wst_ref[gi], ds)
            return carry

        lax.fori_loop(0, GW, cols_step, 0, unroll=8)

        def out_piece(i, carry):
            for j in range(cpp):
                rows = pl.ds(pl.multiple_of(i * rp + j * ACH, ACH), ACH)
                vhat, rstd, _ = _layernorm(va_buf[rows, :].astype(f32), g_ref[...], b_ref[...])
                ds = dsv_buf[rows, :].astype(f32)
                dvn = jnp.concatenate([_dot(wst_ref[0], ds[:, 0:ACH]), _dot(wst_ref[1], ds[:, ACH:half]),
                                       ds_col[0, rows, :], ds_col[1, rows, :]], axis=1)
                dg_ref[...] += jnp.sum(dvn * vhat, axis=0, keepdims=True)
                db_ref[...] += jnp.sum(dvn, axis=0, keepdims=True)
                dvh = dvn * g_ref[...]
                dva = rstd * (dvh - jnp.mean(dvh, axis=-1, keepdims=True) - vhat * jnp.mean(dvh * vhat, axis=-1, keepdims=True))
                dva_buf[rows, :] = dva.astype(bf16)
            store(i).start()
            return carry

        lax.fori_loop(0, MIX_PIECES, out_piece, 0)
        for i in range(MIX_PIECES):
            store(i).wait()

    vm = pl.BlockSpec(memory_space=pltpu.VMEM)
    hbm = pl.BlockSpec(memory_space=pl.ANY)
    dma = lambda: pltpu.SemaphoreType.DMA((MIX_PIECES,))
    return pl.pallas_call(
        body, name="mix_bwd",
        out_shape=(jax.ShapeDtypeStruct((l, AW), bf16), jax.ShapeDtypeStruct((4, ACH, ACH), f32), jax.ShapeDtypeStruct((4, ACH, ACH), f32),
                   jax.ShapeDtypeStruct((1, AW), f32), jax.ShapeDtypeStruct((1, AW), f32)),
        in_specs=[hbm, hbm, vm, vm, vm], out_specs=(hbm, vm, vm, vm, vm),
        scratch_shapes=[pltpu.VMEM((l, AW), bf16), pltpu.VMEM((l, AW), bf16), pltpu.VMEM((2, l, ACH), f32), pltpu.VMEM((2, l, ACH), f32),
                        pltpu.VMEM((l, AW), bf16), dma(), dma(), dma()],
        compiler_params=_cp(),
    )(p, dsv, ln_g, ln_b, ws_t)


def _mid(x, tgt, p, o_f, o_b, sv, gate, gf, gb_norm, w_pa, w_pb, w_out, tl):
    l = x.shape[0]

    def body(x_ref, t_ref, zb_ref, ua_ref, za_ref, g1_ref, g2_ref, of_ref, ob_ref, sv_ref, gate_ref, gf_ref, gbn_ref,
             wpa_ref, wpb_ref, wout_ref,
             dx1_ref, dzbua_ref, dzag_ref, dsv_ref, do_ref, dwout_bf, dwpa_bf, dwpb_bf, dgf_ref, dgate_ref, dgbn_ref, loss_ref,
             dwout_ref, dwpa_ref, dwpb_ref):
        @pl.when(pl.program_id(0) == 0)
        def _():
            for r in (dwout_ref, dwpa_ref, dwpb_ref, dgf_ref, dgate_ref, dgbn_ref, loss_ref):
                r[...] = jnp.zeros_like(r)

        o = of_ref[...].astype(f32) + ob_ref[...].astype(f32)
        rr = jnp.concatenate(
            [jnp.broadcast_to(lax.rsqrt(jnp.mean(o[:, h * HV:(h + 1) * HV] ** 2, axis=-1, keepdims=True) + EPS), (tl, HV))
             for h in range(NH)], axis=1)
        ohat = o * rr
        on = ohat * gbn_ref[...]
        szb, dszb = _silu_and_grad(zb_ref[...].astype(f32))
        tb = on * szb
        u = ua_ref[...].astype(f32)
        svv = sv_ref[...].astype(f32)
        sza, dsza = _silu_and_grad(za_ref[...].astype(f32))
        ta = u * svv * sza
        ya = _dot(ta, wpa_ref[...])
        yb = _dot(tb, wpb_ref[...])
        g1 = _sigmoid(g1_ref[...].astype(f32))
        g2 = _sigmoid(g2_ref[...].astype(f32))
        m = g1 * ya + g2 * yb
        y2 = _dot(m, wout_ref[...])
        x1 = x_ref[...] + gate_ref[...] * y2
        r1 = lax.rsqrt(jnp.mean(x1 * x1, axis=-1, keepdims=True) + EPS)
        x1n = x1 * r1
        err = x1n * gf_ref[...] - t_ref[...]
        loss_ref[...] += jnp.sum(jnp.sum(err * err, axis=-1, keepdims=True), axis=0, keepdims=True) * (0.5 / D)
        dout = err * (1.0 / D)
        dgf_ref[...] += jnp.sum(dout * x1n, axis=0, keepdims=True)
        dx1n = dout * gf_ref[...]
        dx1 = r1 * (dx1n - x1n * jnp.mean(dx1n * x1n, axis=-1, keepdims=True))
        dx1_ref[...] = dx1
        dgate_ref[...] += jnp.sum(dx1 * y2, axis=0, keepdims=True)
        dy2 = dx1 * gate_ref[...]
        dwout_ref[...] += _tn(m, dy2)
        dm = _nt(dy2, wout_ref[...])
        dya = dm * g1
        dyb = dm * g2
        dzag_ref[:, AW:AW + D] = (dm * ya * g1 * (1.0 - g1)).astype(bf16)
        dzag_ref[:, AW + D:] = (dm * yb * g2 * (1.0 - g2)).astype(bf16)
        dwpa_ref[...] += _tn(ta, dya)
        dta = _nt(dya, wpa_ref[...])
        dzbua_ref[:, AW:] = (dta * svv * sza).astype(bf16)
        dsv_ref[...] = (dta * u * sza).astype(bf16)
        dzag_ref[:, 0:AW] = (dta * u * svv * dsza).astype(bf16)
        dwpb_ref[...] += _tn(tb, dyb)
        dtb = _nt(dyb, wpb_ref[...])
        don = dtb * szb
        dzbua_ref[:, 0:AW] = (dtb * on * dszb).astype(bf16)
        dgbn_ref[...] += jnp.sum(don * ohat, axis=0, keepdims=True)
        doh = don * gbn_ref[...]
        prod = doh * ohat
        mh = jnp.concatenate(
            [jnp.broadcast_to(jnp.mean(prod[:, h * HV:(h + 1) * HV], axis=-1, keepdims=True), (tl, HV)) for h in range(NH)], axis=1)
        do_ref[...] = (rr * (doh - ohat * mh)).astype(bf16)

        @pl.when(pl.program_id(0) == l // tl - 1)
        def _():
            for acc, out in ((dwout_ref, dwout_bf), (dwpa_ref, dwpa_bf), (dwpb_ref, dwpb_bf)):
                out[...] = acc[...].astype(bf16)

    row = lambda w, j: pl.BlockSpec((tl, w), lambda i, j=j: (i, j))
    full = lambda *s: pl.BlockSpec(s, lambda i: (0,) * len(s))
    return pl.pallas_call(
        body, name="mid", grid=(l // tl,),
        out_shape=(jax.ShapeDtypeStruct((l, D), f32), jax.ShapeDtypeStruct((l, 2 * AW), bf16), jax.ShapeDtypeStruct((l, AW + 2 * D), bf16),
                   jax.ShapeDtypeStruct((l, AW), bf16), jax.ShapeDtypeStruct((l, VW), bf16),
                   jax.ShapeDtypeStruct((D, D), bf16), jax.ShapeDtypeStruct((AW, D), bf16), jax.ShapeDtypeStruct((VW, D), bf16),
                   jax.ShapeDtypeStruct((1, D), f32), jax.ShapeDtypeStruct((1, D), f32), jax.ShapeDtypeStruct((1, VW), f32),
                   jax.ShapeDtypeStruct((1, 1), f32)),
        scratch_shapes=[pltpu.VMEM((D, D), f32), pltpu.VMEM((AW, D), f32), pltpu.VMEM((VW, D), f32)],
        in_specs=[row(D, 0), row(D, 0), row(AW, PZB // AW), row(AW, PUA // AW), row(AW, PZA // AW), row(D, PG1 // D), row(D, PG2 // D),
                  row(VW, 0), row(VW, 0), row(AW, 0), full(1, D), full(1, D), full(1, VW), full(AW, D), full(VW, D), full(D, D)],
        out_specs=(row(D, 0), row(2 * AW, 0), row(AW + 2 * D, 0), row(AW, 0), row(VW, 0),
                   full(D, D), full(AW, D), full(VW, D), full(1, D), full(1, D), full(1, VW), full(1, 1)),
        compiler_params=_cp(("arbitrary",)),
    )(x, tgt, p, p, p, p, p, o_f, o_b, sv, gate, gf, gb_norm, w_pa, w_pb, w_out)


def _in_bwd(x, dx1, dqkv, dzbua, dva, dzag, dlr, w_pad, g, scale, tl, comm):
    l = x.shape[0]
    c_in, c_out, c_sems = comm.specs()

    def body(*refs):
        cin = refs[14:14 + len(c_in)]
        outs = refs[14 + len(c_in):]
        comm.run(cin, outs[4:4 + comm.n], outs[4 + comm.n:], l // tl, lambda: compute(*refs[:14], *outs[:4]))

    def compute(x_ref, dx1_ref, a_ref, b_ref, c_ref, e_ref, lr_ref, wa_ref, wb_ref, wc_ref, we_ref, wl_ref, g_ref, sc_ref,
                gx_ref, dsh_ref, dsc_ref, dg_ref):
        @pl.when(pl.program_id(0) == 0)
        def _():
            for r in (dsh_ref, dsc_ref, dg_ref):
                r[...] = jnp.zeros_like(r)

        dh = (_nt(a_ref[...], wa_ref[...]) + _nt(b_ref[...], wb_ref[...]) + _nt(c_ref[...], wc_ref[...]) + _nt(e_ref[...], we_ref[...])
              + _nt(lr_ref[...], wl_ref[...]))
        xv = x_ref[...]
        r = lax.rsqrt(jnp.mean(xv * xv, axis=-1, keepdims=True) + EPS)
        xn = xv * r
        gxn = jnp.sum(dh * xn, axis=0, keepdims=True)
        dsh_ref[...] += jnp.sum(dh, axis=0, keepdims=True)
        dsc_ref[...] += gxn * g_ref[...]
        dg_ref[...] += gxn * (1.0 + sc_ref[...])
        dxn = dh * (g_ref[...] * (1.0 + sc_ref[...]))
        gx_ref[...] = dx1_ref[...] + r * (dxn - xn * jnp.mean(dxn * xn, axis=-1, keepdims=True))

    row = lambda w: pl.BlockSpec((tl, w), lambda i: (i, 0))
    wcol = lambda w, j: pl.BlockSpec((D, w), lambda i, j=j: (0, j))
    vec = pl.BlockSpec((1, D), lambda i: (0, 0))
    return pl.pallas_call(
        body, name="in_bwd", grid=(l // tl,),
        out_shape=(jax.ShapeDtypeStruct((l, D), f32),) + (jax.ShapeDtypeStruct((1, D), f32),) * 3 + tuple(comm.outs),
        in_specs=[row(D), row(D), row(2 * KW + VW), row(2 * AW), row(AW), row(AW + 2 * D), row(LRW),
                  wcol(2 * KW + VW, 0), wcol(2 * AW, PZB // (2 * AW)), wcol(AW, PVA // AW), wcol(AW + 2 * D, PZA // (AW + 2 * D)),
                  wcol(LRW, PLR // LRW), vec, vec] + c_in,
        out_specs=(row(D), vec, vec, vec) + tuple(c_out), scratch_shapes=c_sems,
        compiler_params=_cp(("arbitrary",)),
    )(x, dx1, dqkv, dzbua, dva, dzag, dlr, w_pad, w_pad, w_pad, w_pad, w_pad, g, scale, *comm.ins)


def _tn_matmul(a, bs, tl, name, comm=None, pack=None):
    l, m = a.shape
    k = len(bs)
    comm = comm or _Comm([], [], None)
    c_in, c_out, c_sems = comm.specs()
    acc_shapes = [(m, b.shape[1]) for b in bs]
    n_extra = len(pack.extra) if pack else 0
    n_res = len(pack.outs) if pack else k

    def body(a_ref, *refs):
        b_refs, refs = refs[:k], refs[k:]
        extra, refs = refs[:n_extra], refs[n_extra:]
        cin, refs = refs[:len(c_in)], refs[len(c_in):]
        res, refs = refs[:n_res], refs[n_res:]
        cout, refs = refs[:comm.n], refs[comm.n:]
        accs, sems = (refs[:k], refs[k:]) if pack else (res, refs)

        def compute():
            @pl.when(pl.program_id(0) == 0)
            def _():
                for o_ref in accs:
                    o_ref[...] = jnp.zeros_like(o_ref)

            av = a_ref[...]
            for b_ref, o_ref in zip(b_refs, accs):
                o_ref[...] += _tn(av, b_ref[...])
            if pack:
                pl.when(pl.program_id(0) == l // tl - 1)(lambda: pack.fn(accs, extra, res))

        comm.run(cin, cout, sems, l // tl, compute)

    full = lambda shape: pl.BlockSpec(shape, lambda i: (0,) * len(shape))
    res_shapes = list(pack.outs) if pack else [jax.ShapeDtypeStruct(sh, f32) for sh in acc_shapes]
    return pl.pallas_call(
        body, name=name, grid=(l // tl,), out_shape=tuple(res_shapes) + tuple(comm.outs),
        in_specs=[pl.BlockSpec((tl, m), lambda i: (i, 0))] + [pl.BlockSpec((tl, b.shape[1]), lambda i: (i, 0)) for b in bs]
        + [full(e.shape) for e in (pack.extra if pack else [])] + c_in,
        out_specs=tuple(full(r.shape) for r in res_shapes) + tuple(c_out),
        scratch_shapes=([pltpu.VMEM(sh, f32) for sh in acc_shapes] if pack else []) + c_sems, compiler_params=_cp(("arbitrary",)),
    )(a, *bs, *(pack.extra if pack else []), *comm.ins)


def _pad_gate(w2, gb):
    z = jnp.zeros((RANK, KW), f32)
    tail = jnp.zeros((LRW - 2 * RANK, KW), f32)
    w2f = jnp.concatenate([w2[0], z, tail], axis=0)
    w2b = jnp.concatenate([z, w2[1], tail], axis=0)
    return w2f, w2b, gb[0:1], gb[1:2]


EARLY_A_ROWS = 512


class _NoExchange:
    def __init__(self, w_pa, w_pb, w_out):
        self.weights = (w_pa, w_pb, w_out)

    def gather_proj(self):
        return _Comm([], [], None)

    def proj_weights(self, got):
        return self.weights

    def first(self, dw_out, dw_pa, dw_pb, small):
        return _Comm([], [], None)

    def early_a(self, blocks):
        return _Comm([], [], None)

    def early_b(self, blocks):
        return _Comm([], [], None)

    def late(self, blocks, dgt):
        return _Comm([], [], None)


class _Exchanges:
    def __init__(self, pa, pb, wo):
        self.shards = (pa, pb, wo)

    def gather_proj(self):
        def plan(i, o):
            srcs = [lambda j, r=r: r for r in i]
            dsts = [lambda j: o[0].at[:, _lanes(j)], lambda j: o[1].at[:, _lanes(j)], lambda j: o[2].at[j]]
            return srcs, dsts, None
        sds = jax.ShapeDtypeStruct
        return _Comm(self.shards, [sds((AW, D), bf16), sds((VW, D), bf16), sds((NDEV, 128, D), bf16)], plan)

    def proj_weights(self, got):
        return got[0], got[1], got[2].reshape(D, D)

    def first(self, dw_out, dw_pa, dw_pb, small):
        def plan(i, o):
            srcs = [lambda j: i[0].at[j], lambda j: i[1].at[:, _lanes(j)], lambda j: i[2].at[:, _lanes(j)], lambda j: i[3]]
            dsts = [lambda j, r=r: r.at[j] for r in o]
            return srcs, dsts, None
        sds = jax.ShapeDtypeStruct
        return _Comm([dw_out.reshape(NDEV, 128, D), dw_pa, dw_pb, small],
                     [sds((NDEV, 128, D), bf16), sds((NDEV, AW, 128), bf16), sds((NDEV, VW, 128), bf16),
                      sds((NDEV,) + small.shape, f32)], plan)

    def early_a(self, blocks):
        def plan(i, o):
            return [lambda j: i[0].at[j]], [lambda j: o[0].at[j]], [lambda j: j >= LATE_DESTS - 1]
        return _Comm([blocks], [jax.ShapeDtypeStruct(blocks.shape, bf16)], plan)

    def early_b(self, blocks):
        def plan(i, o):
            return [lambda j: i[0].at[j]], [lambda j: o[0].at[j]], [lambda j: j >= LATE_DESTS - 1]
        return _Comm([blocks], [jax.ShapeDtypeStruct(blocks.shape, bf16)], plan)

    def late(self, blocks, dgt):
        return _LateComm(blocks, dgt)


def _local_step(x, ctx, tgt, mod, modc, norm_g, w_pad, ln_g, ln_b, ws, bs, w2, gb, gb_norm, gf, xch):
    shift, scale, gate = mod[:, 0:D], mod[:, D:2 * D], mod[:, 2 * D:]
    shift_c, scale_c = modc[:, 0:D], modc[:, D:]
    w2f, w2b, gbf, gbb = _pad_gate(w2, gb)

    p, h, *got_proj = _in_proj(x, norm_g, scale, shift, w_pad, 512, xch.gather_proj())
    w_pa, w_pb, w_out = xch.proj_weights(got_proj)
    sc_f, sc_b = _ctx_fwd(ctx, norm_g, scale_c, shift_c, w_pad, w2f, w2b, gbf, gbb)
    o_f, st_f = _gla_fwd(p, w2f, gbf, sc_f, False, 512, "gla_fwd_f")
    o_b, st_b = _gla_fwd(p, w2b, gbb, sc_b, True, 512, "gla_fwd_b")
    sv = _mix_fwd(p, ln_g, ln_b, ws.astype(bf16), bs.T)
    (dx1, dzbua, dzag, dsv, do, dw_out, dw_pa, dw_pb, dgf, dgate, dgbn, loss) = _mid(
        x, tgt, p, o_f, o_b, sv, gate, gf, gb_norm, w_pa, w_pb, w_out, 256)
    dva, dws, dbs_acc, dln_g, dln_b = _mix_bwd(p, dsv, ln_g, ln_b, jnp.swapaxes(ws, 1, 2).astype(bf16))
    small = _rows128(dln_g, dln_b, dws, jnp.sum(dbs_acc, axis=-1), dgbn, dgf, jnp.broadcast_to(loss, (1, 128)))
    blocks_a, blocks_b, *got_first = _tn_matmul(h, [dzbua, dva, dzag], 1024, "dw_early", xch.first(dw_out, dw_pa, dw_pb, small),
                                               _pack_early())

    dqkv_f, dlr_f, dw2f, dgbf, dsc_f, *got_a = _gla_bwd(p, do, st_f, w2f, gbf, None, False, 512, "gla_bwd_f",
                                                        xch.early_a(blocks_a))
    dqkv, dlr, dw2b, dgbb, dsc_b, *got_b = _gla_bwd(p, do, st_b, w2b, gbb, (dqkv_f, dlr_f), True, 512, "gla_bwd_b",
                                                    xch.early_b(blocks_b))
    dwk_c, dwv_c, dwl_c, dmodc, dg_c, dw2c, dgbc = _ctx_bwd(ctx, norm_g, scale_c, shift_c, w_pad, w2f, w2b, gbf, gbb, dsc_f, dsc_b)
    (blocks_late,) = _tn_matmul(h, [dqkv, dlr], 1024, "dw_qkv_lr", pack=_pack_late(dwk_c, dwv_c, dwl_c))
    dw2 = jnp.stack([dw2f[0:RANK] + dw2c[0, 0:RANK], dw2b[RANK:2 * RANK] + dw2c[1, RANK:2 * RANK]])
    dgb = jnp.concatenate([dgbf + dgbc[0], dgbb + dgbc[1]], axis=0)
    dgt = jnp.concatenate([jnp.transpose(dw2.reshape(2, RANK, NDEV, 32), (2, 0, 1, 3)).reshape(NDEV, 2 * RANK * 32),
                           jnp.transpose(dgb.reshape(2, NDEV, 32), (1, 0, 2)).reshape(NDEV, 64),
                           jnp.zeros((NDEV, 64), f32)], axis=1).reshape(NDEV, 9, 128)
    gx, dshift, dscale, dg, *got_late = _in_bwd(x, dx1, dqkv, dzbua, dva, dzag, dlr, w_pad, norm_g, scale, 512,
                                                xch.late(blocks_late, dgt))
    return dict(loss=loss, gx=gx, dmod=jnp.concatenate([dshift, dscale, dgate], axis=1), dmodc=dmodc, dnorm_g=dg + dg_c,
                dnorm_parts=(dg, dg_c), dmod_parts=(dshift, dscale, dgate),
                small=small, blocks_a=blocks_a, blocks_b=blocks_b, blocks_late=blocks_late, dw2=dw2, dgb=dgb,
                dw_pa=dw_pa, dw_pb=dw_pb, dw_out=dw_out, got_first=got_first, got_a=got_a, got_b=got_b, got_late=got_late)


def _rows128(*vs):
    out = []
    for t in vs:
        t = t.reshape(-1)
        pad = (-t.shape[0]) % 128
        out.append(jnp.pad(t, (0, pad)) if pad else t)
    return jnp.concatenate(out).reshape(-1, 128)


def kernel(x, c, ctx, c_ctx, w_mod, b_mod, norm_g, w_in, a_ln_g, a_ln_b, a_ws, a_bs, b_gate_w2, b_gate_b, b_norm_g, w_proj_a, w_proj_b, w_out, final_norm_g, loss_target, m_c_ctx, m_w_mod, m_b_mod, m_norm_g, m_w_in, m_a_ln_g, m_a_ln_b, m_a_ws, m_a_bs, m_b_gate_w2, m_b_gate_b, m_b_norm_g, m_w_proj_a, m_w_proj_b, m_w_out, m_final_norm_g, v_c_ctx, v_w_mod, v_b_mod, v_norm_g, v_w_in, v_a_ln_g, v_a_ln_b, v_a_ws, v_a_bs, v_b_gate_w2, v_b_gate_b, v_b_norm_g, v_w_proj_a, v_w_proj_b, v_w_out, v_final_norm_g):
    me = _me()

    gate_mine = _rows128(jnp.concatenate([b_gate_w2.reshape(-1), b_gate_b.reshape(-1)]))
    cs, mods, wg, gates = _gather_first(c, c_ctx.reshape(1, D), w_mod[0], b_mod, w_in[0].astype(bf16), gate_mine)
    w_pad = _repack_w(wg)
    gflat = gates.reshape(NDEV, 9 * 128)
    w2 = jnp.transpose(gflat[:, 0:2 * RANK * 32].reshape(NDEV, 2, RANK, 32), (1, 2, 0, 3)).reshape(2, RANK, KW)
    gb = jnp.transpose(gflat[:, 2 * RANK * 32:2 * RANK * 32 + 64].reshape(NDEV, 2, 32), (1, 0, 2)).reshape(2, KW)

    mods = jnp.transpose(mods, (1, 0, 2)).reshape(16, 3 * D)
    mod = lax.dynamic_slice(mods, (me, 0), (1, 3 * D))
    modc = mods[8:9, 0:2 * D]

    xch = _Exchanges(w_proj_a[0].astype(bf16), w_proj_b[0].astype(bf16), w_out[0].astype(bf16))
    r = _local_step(x[0], ctx[0], loss_target[0], mod, modc, norm_g, w_pad, a_ln_g, a_ln_b, a_ws[0], a_bs[0], w2, gb,
                    b_norm_g, final_norm_g.reshape(1, D), xch)
    p_out, p_pa, p_pb, smalls_e = r["got_first"]
    (p_in_a,) = r["got_a"]
    (p_in_b,) = r["got_b"]
    _, _, p_in_late, p_gt = r["got_late"]

    n_e = (AW + AW + 4 * ACH * ACH + AW + VW + D) // 128
    row = lambda t: t.reshape(1, D)
    rep_e = _adam_params(smalls_e, [a_ln_g, a_ln_b, a_ws, a_bs, b_norm_g, row(final_norm_g)],
                         [m_a_ln_g, m_a_ln_b, m_a_ws, m_a_bs, m_b_norm_g, row(m_final_norm_g)],
                         [v_a_ln_g, v_a_ln_b, v_a_ws, v_a_bs, v_b_norm_g, row(v_final_norm_g)], "adam_rep_early")
    rep_e = [t[0:5] + (t[5].reshape(D),) for t in rep_e]
    loss = jnp.sum(smalls_e[:, n_e, 0])

    mod_grads = _mod_tail(r["dnorm_parts"], r["dmod_parts"], r["dmodc"], cs, w_mod)
    mod_res = _adam_plain(mod_grads, [w_mod, norm_g, b_mod, row(c_ctx)], [m_w_mod, m_norm_g, m_b_mod, row(m_c_ctx)],
                          [v_w_mod, v_norm_g, v_b_mod, row(v_c_ctx)], "adam_mod")
    wm, ng, bmod_r, cc = ([t[i] for t in mod_res] for i in range(4))

    a_in = _adam_w_in(p_in_a, p_in_b, p_in_late, w_in, m_w_in, v_w_in)
    blk = _adam_blocks([p_pa, p_pb, p_out], [w_proj_a, w_proj_b, w_out], [m_w_proj_a, m_w_proj_b, m_w_out],
                       [v_w_proj_a, v_w_proj_b, v_w_out], "adam_proj_out")
    a_pa, a_pb, a_out = ([t[i] for t in blk] for i in range(3))
    a_gt = _adam_gate(p_gt, b_gate_w2, b_gate_b, m_b_gate_w2, m_b_gate_b, v_b_gate_w2, v_b_gate_b)
    sh = [(a_in[k], a_pa[k], a_pb[k], a_out[k], a_gt[k], a_gt[4 + k]) for k in range(4)]

    outs = [loss, r["gx"][None]]
    for k in range(4):
        lg, lb, aws, abs_, bng, fng = rep_e[k]
        n_g, bmod = ng[k], bmod_r[k]
        s_in, s_pa, s_pb, s_out, s_w2, s_gb = sh[k]
        outs += [cc[k].reshape(D), wm[k], bmod, n_g, s_in, lg, lb, aws, abs_, s_w2, s_gb, bng, s_pa, s_pb, s_out, fng]
    return tuple(outs)
```

```python
import jax
import jax.numpy as jnp
from jax import lax
from jax.experimental import pallas as pl
from jax.experimental.pallas import tpu as pltpu

f32, bf16 = jnp.float32, jnp.bfloat16

D = 1024
CTX = 256
EPS = 1e-6
AW = 512
ACH = 128
GW = 64
KW = 256
VW = 512
NH = 4
HK = 64
HV = 128
RANK = 16
TAU = 16.0
CH = 64
QSCALE = HK ** -0.5
INW = 5152
NDEV = 8

PQ, PK, PV, PZB, PUA, PVA, PZA, PG1, PG2, PLR, PW = 0, 256, 512, 1024, 1536, 2048, 2560, 3072, 4096, 5120, 5248
LRW = 128

ADAM_LR, ADAM_B1, ADAM_B2, ADAM_EPS, ADAM_WD, ADAM_STEP = 0.001, 0.9, 0.999, 1e-08, 0.01, 10

VMEM_LIMIT = 56 * 1024 * 1024
MESH = pl.DeviceIdType.MESH


def _cp(sem=None):
    return pltpu.CompilerParams(dimension_semantics=sem, vmem_limit_bytes=VMEM_LIMIT)


def _dot(a, b):
    return jnp.dot(a.astype(bf16), b.astype(bf16), preferred_element_type=f32)


def _nt(a, b):
    return lax.dot_general(a.astype(bf16), b.astype(bf16), (((1,), (1,)), ((), ())), preferred_element_type=f32)


def _tn(a, b):
    return lax.dot_general(a.astype(bf16), b.astype(bf16), (((0,), (0,)), ((), ())), preferred_element_type=f32)


def _dot_hi(a, b):
    return jnp.dot(a, b, preferred_element_type=f32, precision=lax.Precision.HIGHEST)


def _sigmoid(x):
    return 1.0 / (1.0 + jnp.exp(-x))


def _log_sigmoid(x):
    return jnp.minimum(x, 0.0) - jnp.log(1.0 + jnp.exp(-jnp.abs(x)))


def _silu_and_grad(z):
    s = _sigmoid(z)
    return z * s, s * (1.0 + z * (1.0 - s))


def _me():
    return 4 * lax.axis_index("x") + 2 * lax.axis_index("y") + lax.axis_index("c")


def _peer(k):
    x, y, c = lax.axis_index("x"), lax.axis_index("y"), lax.axis_index("c")
    px = 1 - x if k & 4 else x
    py = 1 - y if k & 2 else y
    pc = 1 - c if k & 1 else c
    return (px, py, pc), 4 * px + 2 * py + pc


def _fanout(srcs, dsts, send_sems, recv_sems, local_sems, owners=None):
    me = _me()
    n = len(srcs)
    owns = lambda a, j: True if owners is None or owners[a] is None else owners[a](j)

    def guarded(cond, fn):
        if cond is True:
            fn()
        else:
            pl.when(cond)(fn)

    def copies(with_recvs):
        local = [pltpu.make_async_copy(srcs[a](me), dsts[a](me), local_sems.at[a]) for a in range(n)]
        sends, recvs = [], []
        for k in range(1, NDEV):
            dev, idx = _peer(k)
            for a in range(n):
                s = (k - 1) * n + a
                sends.append((owns(a, idx), pltpu.make_async_remote_copy(
                    src_ref=srcs[a](idx), dst_ref=dsts[a](me), send_sem=send_sems.at[s], recv_sem=recv_sems.at[s],
                    device_id=dev, device_id_type=MESH)))
                if with_recvs:
                    recvs.append((owns(a, me), pltpu.make_async_remote_copy(
                        src_ref=srcs[a](idx), dst_ref=dsts[a](idx), send_sem=send_sems.at[s], recv_sem=recv_sems.at[s],
                        device_id=dev, device_id_type=MESH)))
        return local, sends, recvs

    def start():
        local, sends, _ = copies(False)
        for a, cp in enumerate(local):
            guarded(owns(a, me), cp.start)
        for cond, cp in sends:
            guarded(cond, cp.start)

    def finish():
        local, sends, recvs = copies(True)
        for cond, cp in recvs:
            guarded(cond, cp.wait_recv)
        for cond, cp in sends:
            guarded(cond, cp.wait_send)
        for a, cp in enumerate(local):
            guarded(owns(a, me), cp.wait)

    return start, finish


class _Comm:
    def __init__(self, ins, outs, plan):
        self.ins, self.outs, self.plan = list(ins), list(outs), plan
        self.n = len(self.outs)

    def specs(self):
        hbm = pl.BlockSpec(memory_space=pl.ANY)
        return [hbm] * len(self.ins), [hbm] * self.n, _fanout_sems(self.n) if self.n else []

    def run(self, in_refs, out_refs, sems, nsteps, compute):
        if not self.n:
            compute()
            return

        def hooks():
            srcs, dsts, owners = self.plan(in_refs, out_refs)
            return _fanout(srcs, dsts, sems[0], sems[1], sems[2], owners)

        pl.when(pl.program_id(0) == 0)(lambda: hooks()[0]())
        compute()
        pl.when(pl.program_id(0) == nsteps - 1)(lambda: hooks()[1]())


LATE_MID_STEP = 1


class _LateComm:
    def __init__(self, blocks, dgt):
        sds = jax.ShapeDtypeStruct
        self.ins = [blocks, dgt]
        self.outs = [sds((D, SHARD), bf16), sds((D, SHARD), bf16), sds((4, D, SHARD), bf16), sds(dgt.shape, f32)]
        self.n = len(self.outs)

    def specs(self):
        hbm = pl.BlockSpec(memory_space=pl.ANY)
        scratch = [pltpu.VMEM((3, D, SHARD), bf16), pltpu.SemaphoreType.DMA((2,)), pltpu.SemaphoreType.DMA((4,)),
                   pltpu.SemaphoreType.DMA((3,))] + _fanout_sems(1)
        return [hbm] * 2, [hbm] * self.n, scratch

    def run(self, in_refs, out_refs, scratch, nsteps, compute):
        late_ref, dgt_ref = in_refs
        sib_ref, pair_ref, parts_ref, gt_ref = out_refs
        vbuf, send_sems, recv_sems, local_sems, g_send, g_recv, g_local = scratch
        x, y, c = lax.axis_index("x"), lax.axis_index("y"), lax.axis_index("c")
        chip = 2 * x + y
        is_owner_chip = chip == 0
        step = pl.program_id(0)

        def to_sibling():
            return pltpu.make_async_remote_copy(src_ref=late_ref.at[1 - c], dst_ref=sib_ref, send_sem=send_sems.at[0],
                                                recv_sem=recv_sems.at[0], device_id=(x, y, 1 - c), device_id_type=MESH)

        def to_owner(k):
            return pltpu.make_async_remote_copy(src_ref=pair_ref, dst_ref=parts_ref.at[k], send_sem=send_sems.at[1],
                                                recv_sem=recv_sems.at[k], device_id=(0, 0, c), device_id_type=MESH)

        def own_copy():
            return pltpu.make_async_copy(pair_ref, parts_ref.at[0], local_sems.at[2])

        def gates():
            return _fanout([lambda j: dgt_ref.at[j]], [lambda j: gt_ref.at[j]], g_send, g_recv, g_local)

        @pl.when(step == 0)
        def _():
            to_sibling().start()
            gates()[0]()

        compute()

        @pl.when(step == LATE_MID_STEP)
        def _():
            mine = pltpu.make_async_copy(late_ref.at[c], vbuf.at[0], local_sems.at[0])
            mine.start()
            to_sibling().wait_recv()
            theirs = pltpu.make_async_copy(sib_ref, vbuf.at[1], local_sems.at[1])
            theirs.start()
            mine.wait()
            theirs.wait()
            vbuf[2] = (vbuf[0].astype(f32) + vbuf[1].astype(f32)).astype(bf16)
            pltpu.sync_copy(vbuf.at[2], pair_ref)
            pl.when(is_owner_chip)(lambda: own_copy().start())
            pl.when(jnp.logical_not(is_owner_chip))(lambda: to_owner(chip).start())

        @pl.when(step == nsteps - 1)
        def _():
            @pl.when(is_owner_chip)
            def _():
                for k in range(1, 4):
                    to_owner(k).wait_recv()
                own_copy().wait()

            pl.when(jnp.logical_not(is_owner_chip))(lambda: to_owner(chip).wait_send())
            to_sibling().wait_send()
            gates()[1]()


def _fanout_sems(n):
    return [pltpu.SemaphoreType.DMA(((NDEV - 1) * n,)), pltpu.SemaphoreType.DMA(((NDEV - 1) * n,)), pltpu.SemaphoreType.DMA((n,))]


def _lanes(j):
    return pl.ds(pl.multiple_of(j * 128, 128), 128)


def _gather_first(c_row, cctx_row, wm, bm, wi, w2, gb):
    per = 128 // 32
    gate_rows = 2 * RANK // per + 1

    def body(c_ref, cctx_ref, wm_ref, bm_ref, wi_ref, w2_ref, gb_ref, cs_ref, mods_ref, owi, og, g_ref, call_ref, mine_ref,
             send_sems, recv_sems, local_sems, c_send, c_recv, c_local, m_send, m_recv, m_local):
        g_ref[...] = jnp.zeros_like(g_ref)
        for d in range(2):
            for r in range(RANK):
                q = d * RANK + r
                g_ref[q // per:q // per + 1, (q % per) * 32:(q % per + 1) * 32] = w2_ref[0, d, r:r + 1, :]
            g_ref[gate_rows - 1:gate_rows, d * 32:(d + 1) * 32] = gb_ref[0, d:d + 1, :]
        x, y, c = lax.axis_index("x"), lax.axis_index("y"), lax.axis_index("c")
        sibling = (x, y, 1 - c)
        chips = [(1 - x, y), (x, 1 - y), (1 - x, 1 - y)]
        index = lambda px, py, pc: 4 * px + 2 * py + pc
        arrays = ((wi_ref, owi), (g_ref, og))
        n = len(arrays)

        def copy(a, k, block, to, own=False):
            src, out = arrays[a]
            return pltpu.make_async_remote_copy(
                src_ref=src if own else out.at[index(*block)], dst_ref=out.at[index(*block)],
                send_sem=send_sems.at[k * n + a], recv_sem=recv_sems.at[k * n + a], device_id=to, device_id_type=MESH)

        c_start, c_finish = _fanout([lambda j: c_ref], [lambda j: call_ref.at[j]], c_send, c_recv, c_local)
        c_start()
        mine = [pltpu.make_async_copy(src, out.at[index(x, y, c)], local_sems.at[a]) for a, (src, out) in enumerate(arrays)]
        first = [copy(a, 0, (x, y, c), sibling, own=True) for a in range(n)]
        first += [copy(a, 1 + j, (x, y, c), (*chip, c), own=True) for j, chip in enumerate(chips) for a in range(n)]
        for cp in mine + first:
            cp.start()

        c_finish()
        cs = jnp.concatenate([call_ref[j] for j in range(NDEV)] + [cctx_ref[...], jnp.zeros((16 - NDEV - 1, D), f32)], axis=0)
        cs_ref[...] = cs
        s, _ = _silu_and_grad(cs)
        mine_ref[...] = _dot_hi(s, wm_ref[...]) + bm_ref[:, pl.ds(pl.multiple_of(_me() * ncol, 128), ncol)]
        m_start, m_finish = _fanout([lambda j: mine_ref], [lambda j: mods_ref.at[j]], m_send, m_recv, m_local)
        m_start()

        passed = []
        for j, chip in enumerate(chips):
            for a in range(n):
                copy(a, 1 + j, (*chip, c), (x, y, c)).wait_recv()
            for a in range(n):
                cp = copy(a, 4 + j, (*chip, c), sibling)
                cp.start()
                passed.append(cp)
        for a in range(n):
            copy(a, 0, sibling, (x, y, c)).wait_recv()
        for j, chip in enumerate(chips):
            for a in range(n):
                copy(a, 4 + j, (*chip, 1 - c), (x, y, c)).wait_recv()
        for cp in first + passed:
            cp.wait_send()
        for cp in mine:
            cp.wait()
        m_finish()

    hbm = pl.BlockSpec(memory_space=pl.ANY)
    vm = pl.BlockSpec(memory_space=pltpu.VMEM)
    ncol = wm.shape[1]
    return pl.pallas_call(
        body, name="gather_first",
        out_shape=(jax.ShapeDtypeStruct((16, D), f32), jax.ShapeDtypeStruct((NDEV, 16, ncol), f32),
                   jax.ShapeDtypeStruct((NDEV,) + wi.shape, bf16), jax.ShapeDtypeStruct((NDEV, gate_rows, 128), f32)),
        in_specs=[vm, vm, vm, vm, hbm, vm, vm], out_specs=(vm, vm, hbm, hbm),
        scratch_shapes=[pltpu.VMEM((gate_rows, 128), f32), pltpu.VMEM((NDEV, 1, D), f32), pltpu.VMEM((16, ncol), f32)]
        + _fanout_sems(2) + _fanout_sems(1) + _fanout_sems(1),
        compiler_params=_cp(),
    )(c_row, cctx_row, wm, bm, wi, w2, gb)


SHARD = INW // NDEV
ROWS_RP = 128


def _overlap(lo, hi, a, b):
    s, e = max(lo, a), min(hi, b)
    return (s, e) if s < e else None


def _repack_w(wg):
    segs = ((0, 1024, PQ), (1024, 1024 + 2 * RANK, PLR), (1024 + 2 * RANK, INW, PZB))

    def body(g_ref, o_ref):
        for j in range(NDEV):
            lo, hi = j * SHARD, (j + 1) * SHARD
            for a, b, pad0 in segs:
                ov = _overlap(lo, hi, a, b)
                if ov:
                    s, e = ov
                    o_ref[:, pad0 + s - a:pad0 + e - a] = g_ref[j, :, s - lo:e - lo]
        o_ref[:, PLR + 2 * RANK:PW] = jnp.zeros((ROWS_RP, PW - PLR - 2 * RANK), bf16)

    return pl.pallas_call(
        body, name="repack_w", grid=(D // ROWS_RP,), out_shape=jax.ShapeDtypeStruct((D, PW), bf16),
        in_specs=[pl.BlockSpec((NDEV, ROWS_RP, SHARD), lambda i: (0, i, 0))],
        out_specs=pl.BlockSpec((ROWS_RP, PW), lambda i: (i, 0)), compiler_params=_cp(("arbitrary",)),
    )(wg)


LATE_END = 1024 + 2 * RANK
LATE_DESTS = 2


def _pack_blocks(o_ref, srcs, dests, dtype):
    for n, j in enumerate(dests):
        lo, hi = j * SHARD, (j + 1) * SHARD
        done = lo
        for a, b, src in srcs:
            ov = _overlap(lo, hi, a, b)
            if ov:
                s, e = ov
                if s > done:
                    o_ref[n, :, done - lo:s - lo] = jnp.zeros((ROWS_RP, s - done), dtype)
                o_ref[n, :, s - lo:e - lo] = src[:, s - a:e - a].astype(dtype)
                done = e
        if done < hi:
            o_ref[n, :, done - lo:hi - lo] = jnp.zeros((ROWS_RP, hi - done), dtype)


class _Pack:
    def __init__(self, extra, outs, fn):
        self.extra, self.outs, self.fn = list(extra), list(outs), fn


def _pack_early():
    def fn(accs, extra, outs):
        zbua, va, zag = accs
        for r0 in range(0, D, ROWS_RP):
            rows = pl.ds(r0, ROWS_RP)
            out, o0 = (outs[0], r0) if r0 < EARLY_A_ROWS else (outs[1], r0 - EARLY_A_ROWS)
            _pack_blocks(out.at[:, pl.ds(o0, ROWS_RP)], ((LATE_END, 2080, zbua.at[rows]), (2080, 2592, va.at[rows]), (2592, INW, zag.at[rows])),
                         range(NDEV), bf16)

    sds = jax.ShapeDtypeStruct
    return _Pack([], [sds((NDEV, EARLY_A_ROWS, SHARD), bf16), sds((NDEV, D - EARLY_A_ROWS, SHARD), bf16)], fn)


def _pack_late(dwk_c, dwv_c, dwl_c):
    def fn(accs, extra, outs):
        qkv_ref, lr_ref = accs
        kc_ref, vc_ref, lc_ref = extra
        for r0 in range(0, D, ROWS_RP):
            rows = pl.ds(r0, ROWS_RP)
            qkv = qkv_ref[rows, :] + jnp.concatenate([jnp.zeros((ROWS_RP, KW), f32), kc_ref[rows, :], vc_ref[rows, :]], axis=1)
            lr = lr_ref[rows, :] + lc_ref[rows, :]
            _pack_blocks(outs[0].at[:, rows], ((0, 1024, qkv), (1024, LATE_END, lr)), range(LATE_DESTS), bf16)

    return _Pack([dwk_c, dwv_c, dwl_c], [jax.ShapeDtypeStruct((LATE_DESTS, D, SHARD), bf16)], fn)


def _mod_tail(dnorm_parts, dmod_parts, dmodc, cs, wm):
    ncol = wm.shape[2]
    n_dn, n_dm = len(dnorm_parts), len(dmod_parts)

    def body(*refs):
        dn_parts, refs = refs[:n_dn], refs[n_dn:]
        dm_parts, refs = refs[:n_dm], refs[n_dm:]
        dmodc_ref, cs_ref, wm_ref, g_wm, g_ng, g_bm, g_cc = refs[:7]
        dn_ref, dmod_ref, a_dn, a_dmod, a_dmodc, dm_rows, gc_ref, a_gc = refs[7:15]
        s1, r1, l1, s2, r2, l2 = refs[15:21]
        dn = dn_parts[0][...]
        for ref in dn_parts[1:]:
            dn = dn + ref[...]
        dn_ref[...] = dn
        dmod_ref[...] = jnp.concatenate([ref[...] for ref in dm_parts], axis=1)
        start, finish = _fanout([lambda j: dn_ref, lambda j: dmod_ref, lambda j: dmodc_ref],
                                [lambda j, r=r: r.at[j] for r in (a_dn, a_dmod, a_dmodc)], s1, r1, l1)
        start()
        finish()

        def total(ref):
            t = ref[0]
            for j in range(1, NDEV):
                t = t + ref[j]
            return t

        g_ng[...] = total(a_dn)
        dmodc_tot = jnp.concatenate([total(a_dmodc), jnp.zeros((1, D), f32)], axis=1)
        g_bm[...] = total(a_dmod) + dmodc_tot
        dm_rows[...] = jnp.concatenate([a_dmod[j] for j in range(NDEV)] + [dmodc_tot, jnp.zeros((16 - NDEV - 1, 3 * D), f32)], axis=0)
        dm = dm_rows[:, pl.ds(pl.multiple_of(_me() * ncol, 128), ncol)]
        s, ds = _silu_and_grad(cs_ref[...])
        part = lax.dot_general(dm[8:9, :], wm_ref[0], (((1,), (1,)), ((), ())), preferred_element_type=f32,
                               precision=lax.Precision.HIGHEST)
        gc_ref[...] = part * ds[8:9, :]
        start2, finish2 = _fanout([lambda j: gc_ref], [lambda j: a_gc.at[j]], s2, r2, l2)
        start2()
        g = lax.dot_general(s, dm, (((0,), (0,)), ((), ())), preferred_element_type=f32, precision=lax.Precision.HIGHEST)
        g_wm[...] = g[None]
        finish2()
        g_cc[...] = total(a_gc)

    sds = jax.ShapeDtypeStruct
    row = lambda n: pltpu.VMEM((NDEV, 1, n), f32)
    return pl.pallas_call(
        body, name="mod_tail", out_shape=(sds(wm.shape, f32), sds((1, D), f32), sds((1, 3 * D), f32), sds((1, D), f32)),
        scratch_shapes=[pltpu.VMEM((1, D), f32), pltpu.VMEM((1, 3 * D), f32), row(D), row(3 * D), row(2 * D),
                        pltpu.VMEM((16, 3 * D), f32), pltpu.VMEM((1, D), f32), row(D)] + _fanout_sems(3) + _fanout_sems(1),
        compiler_params=_cp(),
    )(*dnorm_parts, *dmod_parts, dmodc, cs, wm)


def _adam_update(g, w_ref, m_ref, v_ref, go_ref, d_ref, mo_ref, vo_ref):
    c1 = 1.0 / (1.0 - ADAM_B1 ** ADAM_STEP)
    c2 = 1.0 / (1.0 - ADAM_B2 ** ADAM_STEP)
    mn = ADAM_B1 * m_ref[...] + (1.0 - ADAM_B1) * g
    vn = ADAM_B2 * v_ref[...] + (1.0 - ADAM_B2) * (g * g)
    go_ref[...] = g
    mo_ref[...] = mn
    vo_ref[...] = vn
    d_ref[...] = -ADAM_LR * ((mn * c1) / (jnp.sqrt(vn * c2) + ADAM_EPS) + ADAM_WD * w_ref[...])


def _adam_w_in(parts_a, parts_b, parts_late, w, m, v):
    na = EARLY_A_ROWS // ROWS_RP
    n = D // ROWS_RP
    whole = SHARD // 128 * 128

    def body(a_ref, b_ref, l_ref, w_hbm, m_hbm, v_hbm, go_hbm, d_hbm, mo_hbm, vo_hbm, ibuf, obuf, isem, osem):
        step = pl.program_id(0)

        def cols(t):
            return pl.ds(pl.multiple_of(t * ROWS_RP, ROWS_RP), ROWS_RP)

        def fetch(t):
            return [pltpu.make_async_copy(r.at[:, 0, cols(t)], ibuf.at[t % 2, k], isem.at[t % 2, k])
                    for k, r in enumerate((w_hbm, m_hbm, v_hbm))]

        def flush(t):
            return [pltpu.make_async_copy(obuf.at[t % 2, k], r.at[:, 0, cols(t)], osem.at[t % 2, k])
                    for k, r in enumerate((go_hbm, d_hbm, mo_hbm, vo_hbm))]

        def start(cps):
            for cp in cps:
                cp.start()

        def wait(cps):
            for cp in cps:
                cp.wait()

        pl.when(step == 0)(lambda: start(fetch(step)))
        pl.when(step + 1 < n)(lambda: start(fetch(step + 1)))
        me = _me()
        first = step < na
        early = jnp.where(first, a_ref[0], b_ref[0]).astype(f32)
        for i in range(1, NDEV):
            early = early + jnp.where(first, a_ref[i], b_ref[i]).astype(f32)
        late = l_ref[0].astype(f32)
        for i in range(1, 4):
            late = late + l_ref[i].astype(f32)
        g = jnp.where(me >= LATE_DESTS - 1, early, 0.0) + jnp.where(me < LATE_DESTS, late, 0.0)
        gt = jnp.concatenate([g[:, c:c + 128].T for c in range(0, whole, 128)] + [g[:, SHARD - 128:].T[128 - (SHARD - whole):]],
                             axis=0)
        wait(fetch(step))
        pl.when(step >= 2)(lambda: wait(flush(step - 2)))
        slot = step % 2
        _adam_update(gt, *(ibuf.at[slot, k] for k in range(3)), *(obuf.at[slot, k] for k in range(4)))
        start(flush(step))

        @pl.when(step == n - 1)
        def _():
            wait(flush(step - 1))
            wait(flush(step))

    hbm = pl.BlockSpec(memory_space=pl.ANY)
    res = pl.pallas_call(
        body, name="adam_w_in", grid=(n,), out_shape=tuple(jax.ShapeDtypeStruct((SHARD, 1, D), f32) for _ in range(4)),
        in_specs=[pl.BlockSpec((NDEV, ROWS_RP, SHARD), lambda i: (0, jnp.minimum(i, na - 1), 0)),
                  pl.BlockSpec((NDEV, ROWS_RP, SHARD), lambda i: (0, jnp.maximum(i - na, 0), 0)),
                  pl.BlockSpec((4, ROWS_RP, SHARD), lambda i: (0, i, 0)), hbm, hbm, hbm],
        out_specs=(hbm, hbm, hbm, hbm),
        scratch_shapes=[pltpu.VMEM((2, 3, SHARD, ROWS_RP), f32), pltpu.VMEM((2, 4, SHARD, ROWS_RP), f32),
                        pltpu.SemaphoreType.DMA((2, 3)), pltpu.SemaphoreType.DMA((2, 4))],
        compiler_params=_cp(("arbitrary",)),
    )(parts_a, parts_b, parts_late, *(jnp.transpose(t, (2, 0, 1)) for t in (w, m, v)))
    return tuple(jnp.transpose(t, (1, 2, 0)) for t in res)


def _adam_params(parts, ws, ms, vs, name):
    p = parts.shape[0]
    k = len(ws)
    nrows = [w.size // 128 for w in ws]
    starts = [sum(nrows[:i]) for i in range(k)]

    def shaped(g, shape):
        if len(shape) == 2:
            return jnp.concatenate([g[r:r + 1] for r in range(g.shape[0])], axis=1)
        return g.reshape(shape)

    def body(*refs):
        g_ref = refs[0]
        w_refs, m_refs, v_refs = refs[1:1 + k], refs[1 + k:1 + 2 * k], refs[1 + 2 * k:1 + 3 * k]
        outs = refs[1 + 3 * k:]
        for i in range(k):
            rows = slice(starts[i], starts[i] + nrows[i])
            g = g_ref[0, rows, :]
            for j in range(1, p):
                g = g + g_ref[j, rows, :]
            _adam_update(shaped(g, ws[i].shape), w_refs[i], m_refs[i], v_refs[i], outs[i], outs[k + i], outs[2 * k + i], outs[3 * k + i])

    vm = pl.BlockSpec(memory_space=pltpu.VMEM)
    res = pl.pallas_call(
        body, name=name, out_shape=tuple(jax.ShapeDtypeStruct(w.shape, f32) for _ in range(4) for w in ws),
        in_specs=[vm] * (1 + 3 * k), out_specs=(vm,) * (4 * k), compiler_params=_cp(),
    )(parts, *ws, *ms, *vs)
    return [res[i * k:(i + 1) * k] for i in range(4)]


def _adam_blocks(parts, ws, ms, vs, name):
    k = len(ws)

    def body(*refs):
        g_refs, w_refs, m_refs, v_refs = refs[0:k], refs[k:2 * k], refs[2 * k:3 * k], refs[3 * k:4 * k]
        outs = refs[4 * k:]
        for i in range(k):
            g = g_refs[i][0].astype(f32)
            for j in range(1, parts[i].shape[0]):
                g = g + g_refs[i][j].astype(f32)
            _adam_update(g[None], w_refs[i], m_refs[i], v_refs[i], outs[i], outs[k + i], outs[2 * k + i], outs[3 * k + i])

    vm = pl.BlockSpec(memory_space=pltpu.VMEM)
    res = pl.pallas_call(
        body, name=name, out_shape=tuple(jax.ShapeDtypeStruct(w.shape, f32) for _ in range(4) for w in ws),
        in_specs=[vm] * (4 * k), out_specs=(vm,) * (4 * k), compiler_params=_cp(),
    )(*parts, *ws, *ms, *vs)
    return [res[i * k:(i + 1) * k] for i in range(4)]


def _adam_plain(gs, ws, ms, vs, name):
    k = len(ws)

    def body(*refs):
        g_refs, w_refs, m_refs, v_refs = refs[0:k], refs[k:2 * k], refs[2 * k:3 * k], refs[3 * k:4 * k]
        outs = refs[4 * k:]
        for i in range(k):
            _adam_update(g_refs[i][...], w_refs[i], m_refs[i], v_refs[i], outs[i], outs[k + i], outs[2 * k + i], outs[3 * k + i])

    vm = pl.BlockSpec(memory_space=pltpu.VMEM)
    res = pl.pallas_call(
        body, name=name, out_shape=tuple(jax.ShapeDtypeStruct(w.shape, f32) for _ in range(4) for w in ws),
        in_specs=[vm] * (4 * k), out_specs=(vm,) * (4 * k), compiler_params=_cp(),
    )(*gs, *ws, *ms, *vs)
    return [res[i * k:(i + 1) * k] for i in range(4)]


def _adam_gate(parts, w2, gb, m_w2, m_gb, v_w2, v_gb):
    per = 128 // 32

    def body(p_ref, w2_ref, gb_ref, mw_ref, mb_ref, vw_ref, vb_ref, *outs):
        g = p_ref[0]
        for j in range(1, parts.shape[0]):
            g = g + p_ref[j]
        for d in range(2):
            gd = jnp.concatenate([g[(d * RANK + r) // per:(d * RANK + r) // per + 1, (r % per) * 32:(r % per + 1) * 32]
                                  for r in range(RANK)], axis=0)
            _adam_update(gd, *(t.at[0, d] for t in (w2_ref, mw_ref, vw_ref) + outs[0:4]))
        last = 2 * RANK // per
        gbv = jnp.concatenate([g[last:last + 1, d * 32:(d + 1) * 32] for d in range(2)], axis=0)
        _adam_update(gbv, *(t.at[0] for t in (gb_ref, mb_ref, vb_ref) + outs[4:8]))

    vm = pl.BlockSpec(memory_space=pltpu.VMEM)
    return pl.pallas_call(
        body, name="adam_gate", out_shape=tuple(jax.ShapeDtypeStruct(w.shape, f32) for w in (w2, gb) for _ in range(4)),
        in_specs=[vm] * 7, out_specs=(vm,) * 8, compiler_params=_cp(),
    )(parts, w2, gb, m_w2, m_gb, v_w2, v_gb)


def _in_proj(x, g, scale, shift, w_pad, tl, comm):
    l = x.shape[0]
    c_in, c_out, c_sems = comm.specs()

    def body(*refs):
        x_ref, g_ref, sc_ref, sh_ref, w_ref = refs[:5]
        cin = refs[5:5 + len(c_in)]
        p_ref, h_ref = refs[5 + len(c_in):7 + len(c_in)]
        cout = refs[7 + len(c_in):7 + len(c_in) + comm.n]
        sems = refs[7 + len(c_in) + comm.n:]

        def compute():
            xv = x_ref[...]
            r = lax.rsqrt(jnp.mean(xv * xv, axis=-1, keepdims=True) + EPS)
            h = (xv * r) * (g_ref[...] * (1.0 + sc_ref[...])) + sh_ref[...]
            hb = h.astype(bf16)
            h_ref[...] = hb
            p_ref[...] = jnp.dot(hb, w_ref[...], preferred_element_type=f32).astype(bf16)

        comm.run(cin, cout, sems, l // tl, compute)

    vec = pl.BlockSpec((1, D), lambda i: (0, 0))
    return pl.pallas_call(
        body, name="in_proj", grid=(l // tl,),
        out_shape=(jax.ShapeDtypeStruct((l, PW), bf16), jax.ShapeDtypeStruct((l, D), bf16)) + tuple(comm.outs),
        in_specs=[pl.BlockSpec((tl, D), lambda i: (i, 0)), vec, vec, vec, pl.BlockSpec((D, PW), lambda i: (0, 0))] + c_in,
        out_specs=(pl.BlockSpec((tl, PW), lambda i: (i, 0)), pl.BlockSpec((tl, D), lambda i: (i, 0))) + tuple(c_out),
        scratch_shapes=c_sems, compiler_params=_cp(("arbitrary",)),
    )(x, g, scale, shift, w_pad, *comm.ins)


def _tri(rev):
    i = lax.broadcasted_iota(jnp.int32, (CH, CH), 0)
    j = lax.broadcasted_iota(jnp.int32, (CH, CH), 1)
    return jnp.where((j >= i) if rev else (j <= i), 1.0, 0.0).astype(f32)


def _head_masks():
    lane = lax.broadcasted_iota(jnp.int32, (1, KW), 1) // HK
    return [jnp.where(lane == h, 1.0, 0.0).astype(f32) for h in range(NH)]


def _block_diag():
    r = lax.broadcasted_iota(jnp.int32, (VW, KW), 0) // HV
    c = lax.broadcasted_iota(jnp.int32, (VW, KW), 1) // HK
    return jnp.where(r == c, 1.0, 0.0).astype(f32)


def _decay(lr, w2, gb, tri, rev):
    logits = _dot(lr, w2) + gb
    a = _log_sigmoid(logits) * (1.0 / TAU)
    c = _dot_hi(tri, a)
    cl = c[0:1, :] if rev else c[CH - 1:CH, :]
    return logits, c, cl


def _state_fwd(k, v, c, cl, st, bd):
    return st * jnp.exp(cl) + bd * _tn(v, k * jnp.exp(cl - c))


def _state_bwd(k, v, c, cl, st0, dst, trit):
    ecl = jnp.exp(cl)
    edec = jnp.exp(cl - c)
    kdec = k * edec
    dv = _nt(kdec, dst)
    dkdec = _dot(v, dst)
    dcl = jnp.sum(dst * st0, axis=0, keepdims=True) * ecl + jnp.sum(dkdec * kdec, axis=0, keepdims=True)
    da = _dot_hi(trit, -dkdec * kdec) + dcl
    return dkdec * edec, dv, da, dst * ecl


def _bdot(a, b):
    return lax.dot_general(a.astype(bf16), b.astype(bf16), (((2,), (1,)), ((0,), (0,))), preferred_element_type=f32)


def _bnt(a, b):
    return lax.dot_general(a.astype(bf16), b.astype(bf16), (((2,), (2,)), ((0,), (0,))), preferred_element_type=f32)


def _btn(a, b):
    return lax.dot_general(a.astype(bf16), b.astype(bf16), (((1,), (1,)), ((0,), (0,))), preferred_element_type=f32)


def _scan_chunks(x, rev):
    nc = x.shape[0]
    hi = x.astype(bf16)
    r1 = x - hi.astype(f32)
    mid = r1.astype(bf16)
    lo = (r1 - mid.astype(f32)).astype(bf16)
    terms = jnp.concatenate([hi, mid, lo], axis=1)
    tri3 = jnp.broadcast_to(jnp.concatenate([_tri(rev)] * 3, axis=1).astype(bf16)[None], (nc, CH, 3 * CH))
    return lax.dot_general(tri3, terms, (((2,), (1,)), ((0,), (0,))), preferred_element_type=f32)


class _Tile:
    pass


def _tile_prep(q_ref, k_ref, lr_ref, w2_ref, gb_ref, rev, nc):
    t = _Tile()
    tg = nc * CH
    t.logits = _dot(lr_ref[...], w2_ref[...]) + gb_ref[...]
    c = _scan_chunks((_log_sigmoid(t.logits) * (1.0 / TAU)).reshape(nc, CH, KW), rev)
    cl = c[:, 0:1, :] if rev else c[:, CH - 1:CH, :]
    k = k_ref[...].astype(f32).reshape(nc, CH, KW)
    t.ec, t.enc, t.edec, t.ecl = jnp.exp(c), jnp.exp(-c), jnp.exp(cl - c), jnp.exp(cl)
    t.qd = q_ref[...].astype(f32).reshape(nc, CH, KW) * t.ec * QSCALE
    t.kd = k * t.enc
    t.kdec = k * t.edec
    hm = _head_masks()
    t.tri4 = jnp.concatenate([_tri(rev)] * NH, axis=0)[None]
    t.qs = jnp.concatenate([t.qd * hm[h] for h in range(NH)], axis=1)
    t.pst = _bnt(t.qs, t.kd) * t.tri4
    return t


def _gla_fwd(p, w2p, gb, s0, rev, tg, name):
    l = p.shape[0]
    nb, nc = l // tg, tg // CH

    def body(q_ref, k_ref, v_ref, lr_ref, w2_ref, gb_ref, s0_ref, o_ref, st_ref, st):
        @pl.when(pl.program_id(0) == 0)
        def _():
            st[...] = s0_ref[...]

        t = _tile_prep(q_ref, k_ref, lr_ref, w2_ref, gb_ref, rev, nc)
        v = v_ref[...].reshape(nc, CH, VW)
        intra = jnp.concatenate([_bdot(t.pst[:, h * CH:(h + 1) * CH], v[:, :, h * HV:(h + 1) * HV]) for h in range(NH)], axis=2)
        bd = _block_diag()
        s = st[...]
        for n in (range(nc - 1, -1, -1) if rev else range(nc)):
            st_ref[n] = s.astype(bf16)
            s = s * t.ecl[n] + bd * _tn(v[n], t.kdec[n])
        st[...] = s
        o_ref[...] = (_bnt(t.qd, st_ref[...]) + intra).reshape(tg, VW).astype(bf16)

    blk = (lambda i: nb - 1 - i) if rev else (lambda i: i)
    return pl.pallas_call(
        body, name=name, grid=(nb,),
        out_shape=(jax.ShapeDtypeStruct((l, VW), bf16), jax.ShapeDtypeStruct((l // CH, VW, KW), bf16)),
        in_specs=[pl.BlockSpec((tg, KW), lambda i: (blk(i), PQ // KW)), pl.BlockSpec((tg, KW), lambda i: (blk(i), PK // KW)),
                  pl.BlockSpec((tg, VW), lambda i: (blk(i), PV // VW)), pl.BlockSpec((tg, LRW), lambda i: (blk(i), PLR // LRW)),
                  pl.BlockSpec((LRW, KW), lambda i: (0, 0)), pl.BlockSpec((1, KW), lambda i: (0, 0)),
                  pl.BlockSpec((VW, KW), lambda i: (0, 0))],
        out_specs=(pl.BlockSpec((tg, VW), lambda i: (blk(i), 0)), pl.BlockSpec((nc, VW, KW), lambda i: (blk(i), 0, 0))),
        scratch_shapes=[pltpu.VMEM((VW, KW), f32)],
        compiler_params=_cp(("arbitrary",)),
    )(p, p, p, p, w2p, gb, s0)


def _gla_bwd(p, do, states, w2p, gb, prev, rev, tg, name, comm):
    l = p.shape[0]
    nb, nc = l // tg, tg // CH
    out_dt = bf16
    c_in, c_out, c_sems = comm.specs()

    def body(*refs):
        q_ref, k_ref, v_ref, lr_ref, do_ref, st_ref, w2_ref, gb_ref = refs[:8]
        refs = refs[8:]
        if prev is not None:
            pq_ref, pl_ref = refs[:2]
            refs = refs[2:]
        cin, refs = refs[:len(c_in)], refs[len(c_in):]
        dqkv_ref, dlr_ref, dw2_ref, dgb_ref, ds0_ref = refs[:5]
        cout, dst, ds_buf, sems = refs[5:5 + comm.n], refs[5 + comm.n], refs[6 + comm.n], refs[7 + comm.n:]
        comm.run(cin, cout, sems, nb, lambda: compute(q_ref, k_ref, v_ref, lr_ref, do_ref, st_ref, w2_ref, gb_ref,
                                                      pq_ref if prev is not None else None, pl_ref if prev is not None else None,
                                                      dqkv_ref, dlr_ref, dw2_ref, dgb_ref, ds0_ref, dst, ds_buf))

    def compute(q_ref, k_ref, v_ref, lr_ref, do_ref, st_ref, w2_ref, gb_ref, pq_ref, pl_ref,
                dqkv_ref, dlr_ref, dw2_ref, dgb_ref, ds0_ref, dst, ds_buf):
        @pl.when(pl.program_id(0) == 0)
        def _():
            dst[...] = jnp.zeros_like(dst)
            dw2_ref[...] = jnp.zeros_like(dw2_ref)
            dgb_ref[...] = jnp.zeros_like(dgb_ref)

        t = _tile_prep(q_ref, k_ref, lr_ref, w2_ref, gb_ref, rev, nc)
        hm = _head_masks()
        v = v_ref[...].reshape(nc, CH, VW)
        do = do_ref[...].reshape(nc, CH, VW)
        heads = lambda a, h: a[:, :, h * HV:(h + 1) * HV]
        dpst = jnp.concatenate([_bnt(heads(do, h), heads(v, h)) for h in range(NH)], axis=1) * t.tri4
        dv = jnp.concatenate([_btn(t.pst[:, h * CH:(h + 1) * CH], heads(do, h)) for h in range(NH)], axis=2)
        dqd = _bdot(do, st_ref[...])
        for h in range(NH):
            dqd = dqd + hm[h] * _bdot(dpst[:, h * CH:(h + 1) * CH], t.kd)
        dkd = _btn(dpst, t.qs)
        bd = _block_diag()
        d = dst[...]
        for n in (range(nc) if rev else range(nc - 1, -1, -1)):
            ds_buf[n] = d
            d = d * t.ecl[n] + bd * _tn(do[n], t.qd[n])
        dst[...] = d
        ds0_ref[...] = d
        ds = ds_buf[...]
        dv = dv + _bnt(t.kdec, ds)
        dkdec = _bdot(v, ds)
        dcl = jnp.sum(ds * st_ref[...].astype(f32), axis=1, keepdims=True) * t.ecl + jnp.sum(dkdec * t.kdec, axis=1, keepdims=True)
        dc = dqd * t.qd - dkd * t.kd - dkdec * t.kdec
        da = _scan_chunks(dc, not rev) + dcl
        dq = dqd * t.ec * QSCALE
        dk = dkd * t.enc + dkdec * t.edec
        dlog = da.reshape(tg, KW) * _sigmoid(-t.logits) * (1.0 / TAU)
        dlr = _nt(dlog, w2_ref[...])
        dw2_ref[...] += _tn(lr_ref[...], dlog)
        dgb_ref[...] += jnp.sum(dlog, axis=0, keepdims=True)
        dqkv = jnp.concatenate([dq, dk, dv], axis=2).reshape(tg, 2 * KW + VW)
        if prev is not None:
            dqkv = dqkv + pq_ref[...]
            dlr = dlr + pl_ref[...]
        dqkv_ref[...] = dqkv.astype(out_dt)
        dlr_ref[...] = dlr.astype(out_dt)

    blk = (lambda i: i) if rev else (lambda i: nb - 1 - i)
    in_specs = [pl.BlockSpec((tg, KW), lambda i: (blk(i), PQ // KW)), pl.BlockSpec((tg, KW), lambda i: (blk(i), PK // KW)),
                pl.BlockSpec((tg, VW), lambda i: (blk(i), PV // VW)), pl.BlockSpec((tg, LRW), lambda i: (blk(i), PLR // LRW)),
                pl.BlockSpec((tg, VW), lambda i: (blk(i), 0)), pl.BlockSpec((nc, VW, KW), lambda i: (blk(i), 0, 0)),
                pl.BlockSpec((LRW, KW), lambda i: (0, 0)), pl.BlockSpec((1, KW), lambda i: (0, 0))]
    args = [p, p, p, p, do, states, w2p, gb]
    if prev is not None:
        in_specs += [pl.BlockSpec((tg, 2 * KW + VW), lambda i: (blk(i), 0)), pl.BlockSpec((tg, LRW), lambda i: (blk(i), 0))]
        args += list(prev)
    return pl.pallas_call(
        body, name=name, grid=(nb,),
        out_shape=(jax.ShapeDtypeStruct((l, 2 * KW + VW), out_dt), jax.ShapeDtypeStruct((l, LRW), out_dt),
                   jax.ShapeDtypeStruct((LRW, KW), f32), jax.ShapeDtypeStruct((1, KW), f32), jax.ShapeDtypeStruct((VW, KW), f32))
        + tuple(comm.outs),
        in_specs=in_specs + c_in,
        out_specs=(pl.BlockSpec((tg, 2 * KW + VW), lambda i: (blk(i), 0)), pl.BlockSpec((tg, LRW), lambda i: (blk(i), 0)),
                   pl.BlockSpec((LRW, KW), lambda i: (0, 0)), pl.BlockSpec((1, KW), lambda i: (0, 0)),
                   pl.BlockSpec((VW, KW), lambda i: (0, 0))) + tuple(c_out),
        scratch_shapes=[pltpu.VMEM((VW, KW), f32), pltpu.VMEM((nc, VW, KW), f32)] + c_sems,
        compiler_params=_cp(("arbitrary",)),
    )(*args, *comm.ins)


def _ctx_hidden(ctx_ref, g_ref, sc_ref, sh_ref):
    xv = ctx_ref[...]
    r = lax.rsqrt(jnp.mean(xv * xv, axis=-1, keepdims=True) + EPS)
    xn = xv * r
    return xn, xn * (g_ref[...] * (1.0 + sc_ref[...])) + sh_ref[...]


_CTX_W_SPECS = [pl.BlockSpec((D, KW), lambda i: (0, PK // KW)), pl.BlockSpec((D, VW), lambda i: (0, PV // VW)),
                pl.BlockSpec((D, LRW), lambda i: (0, PLR // LRW))]


def _ctx_fwd(ctx, g, scale, shift, w_pad, w2f, w2b, gbf, gbb):
    ncc = CTX // CH

    def body(ctx_ref, g_ref, sc_ref, sh_ref, wk_ref, wv_ref, wl_ref, w2f_ref, w2b_ref, gbf_ref, gbb_ref, sf_ref, sb_ref):
        _, hc = _ctx_hidden(ctx_ref, g_ref, sc_ref, sh_ref)
        k, v, lr = _dot(hc, wk_ref[...]), _dot(hc, wv_ref[...]), _dot(hc, wl_ref[...])
        bd = _block_diag()
        for rev, w2_ref, gb_ref, out in ((False, w2f_ref, gbf_ref, sf_ref), (True, w2b_ref, gbb_ref, sb_ref)):
            tri = _tri(rev)
            st = jnp.zeros((VW, KW), f32)
            for j in (range(ncc - 1, -1, -1) if rev else range(ncc)):
                rows = slice(j * CH, (j + 1) * CH)
                _, c, cl = _decay(lr[rows], w2_ref[...], gb_ref[...], tri, rev)
                st = _state_fwd(k[rows], v[rows], c, cl, st, bd)
            out[...] = st

    vec = pl.BlockSpec((1, D), lambda i: (0, 0))
    w2s = pl.BlockSpec((LRW, KW), lambda i: (0, 0))
    gbs = pl.BlockSpec((1, KW), lambda i: (0, 0))
    sts = pl.BlockSpec((VW, KW), lambda i: (0, 0))
    return pl.pallas_call(
        body, name="ctx_fwd", grid=(1,), out_shape=(jax.ShapeDtypeStruct((VW, KW), f32),) * 2,
        in_specs=[pl.BlockSpec((CTX, D), lambda i: (0, 0)), vec, vec, vec] + _CTX_W_SPECS + [w2s, w2s, gbs, gbs],
        out_specs=(sts, sts), compiler_params=_cp(("arbitrary",)),
    )(ctx, g, scale, shift, w_pad, w_pad, w_pad, w2f, w2b, gbf, gbb)


def _ctx_bwd(ctx, g, scale, shift, w_pad, w2f, w2b, gbf, gbb, dsf, dsb):
    ncc = CTX // CH

    def body(ctx_ref, g_ref, sc_ref, sh_ref, wk_ref, wv_ref, wl_ref, w2f_ref, w2b_ref, gbf_ref, gbb_ref, dsf_ref, dsb_ref,
             dwk_ref, dwv_ref, dwl_ref, dmod_ref, dg_ref, dw2_ref, dgb_ref):
        xn, hc = _ctx_hidden(ctx_ref, g_ref, sc_ref, sh_ref)
        k, v, lr = _dot(hc, wk_ref[...]), _dot(hc, wv_ref[...]), _dot(hc, wl_ref[...])
        bd = _block_diag()
        dk_rows, dv_rows, dl_rows = [None] * ncc, [None] * ncc, [None] * ncc
        for d, (rev, w2_ref, gb_ref, ds_ref) in enumerate(((False, w2f_ref, gbf_ref, dsf_ref), (True, w2b_ref, gbb_ref, dsb_ref))):
            tri = _tri(rev)
            order = list(range(ncc - 1, -1, -1) if rev else range(ncc))
            st, saved = jnp.zeros((VW, KW), f32), {}
            for j in order:
                rows = slice(j * CH, (j + 1) * CH)
                logits, c, cl = _decay(lr[rows], w2_ref[...], gb_ref[...], tri, rev)
                saved[j] = (logits, c, cl, st)
                st = _state_fwd(k[rows], v[rows], c, cl, st, bd)
            dst = ds_ref[...]
            dw2 = jnp.zeros((LRW, KW), f32)
            dgb = jnp.zeros((1, KW), f32)
            for j in reversed(order):
                rows = slice(j * CH, (j + 1) * CH)
                logits, c, cl, st0 = saved[j]
                dk, dv, da, dst = _state_bwd(k[rows], v[rows], c, cl, st0, dst, _tri(not rev))
                dlog = da * _sigmoid(-logits) * (1.0 / TAU)
                dl = _nt(dlog, w2_ref[...])
                dw2 = dw2 + _tn(lr[rows], dlog)
                dgb = dgb + jnp.sum(dlog, axis=0, keepdims=True)
                dk_rows[j] = dk if dk_rows[j] is None else dk_rows[j] + dk
                dv_rows[j] = dv if dv_rows[j] is None else dv_rows[j] + dv
                dl_rows[j] = dl if dl_rows[j] is None else dl_rows[j] + dl
            dw2_ref[d] = dw2
            dgb_ref[d] = dgb
        dk, dv, dl = (jnp.concatenate(t, axis=0) for t in (dk_rows, dv_rows, dl_rows))
        dwk_ref[...] = _tn(hc, dk)
        dwv_ref[...] = _tn(hc, dv)
        dwl_ref[...] = _tn(hc, dl)
        dh = _nt(dk, wk_ref[...]) + _nt(dv, wv_ref[...]) + _nt(dl, wl_ref[...])
        gx = dh * xn
        dmod_ref[:, 0:D] = jnp.sum(dh, axis=0, keepdims=True)
        dmod_ref[:, D:2 * D] = jnp.sum(gx, axis=0, keepdims=True) * g_ref[...]
        dg_ref[...] = jnp.sum(gx, axis=0, keepdims=True) * (1.0 + sc_ref[...])

    vec = pl.BlockSpec((1, D), lambda i: (0, 0))
    w2s = pl.BlockSpec((LRW, KW), lambda i: (0, 0))
    gbs = pl.BlockSpec((1, KW), lambda i: (0, 0))
    sts = pl.BlockSpec((VW, KW), lambda i: (0, 0))
    full = lambda *s: pl.BlockSpec(s, lambda i: (0,) * len(s))
    return pl.pallas_call(
        body, name="ctx_bwd", grid=(1,),
        out_shape=(jax.ShapeDtypeStruct((D, KW), f32), jax.ShapeDtypeStruct((D, VW), f32), jax.ShapeDtypeStruct((D, LRW), f32),
                   jax.ShapeDtypeStruct((1, 2 * D), f32), jax.ShapeDtypeStruct((1, D), f32),
                   jax.ShapeDtypeStruct((2, LRW, KW), f32), jax.ShapeDtypeStruct((2, 1, KW), f32)),
        in_specs=[pl.BlockSpec((CTX, D), lambda i: (0, 0)), vec, vec, vec] + _CTX_W_SPECS + [w2s, w2s, gbs, gbs, sts, sts],
        out_specs=(full(D, KW), full(D, VW), full(D, LRW), full(1, 2 * D), full(1, D), full(2, LRW, KW), full(2, 1, KW)),
        compiler_params=_cp(("arbitrary",)),
    )(ctx, g, scale, shift, w_pad, w_pad, w_pad, w2f, w2b, gbf, gbb, dsf, dsb)


def _layernorm(va, g, b):
    mu = jnp.mean(va, axis=-1, keepdims=True)
    xc = va - mu
    rstd = lax.rsqrt(jnp.mean(xc * xc, axis=-1, keepdims=True) + EPS)
    vhat = xc * rstd
    return vhat, rstd, vhat * g + b


MIX_PIECES = 8


def _mix_fwd(p, ln_g, ln_b, ws, bs_t):
    l = p.shape[0]
    half = AW // 2
    rp = l // MIX_PIECES
    cpp = rp // ACH

    def body(p_ref, g_ref, b_ref, ws_ref, bs_ref, sv_hbm, va_buf, col_buf, sv_buf, in_sems, out_sems):
        piece = lambda i: pl.ds(pl.multiple_of(i * rp, rp), rp)
        load = lambda i: pltpu.make_async_copy(p_ref.at[piece(i), pl.ds(PVA, AW)], va_buf.at[piece(i)], in_sems.at[i])
        store = lambda i: pltpu.make_async_copy(sv_buf.at[piece(i)], sv_hbm.at[piece(i)], out_sems.at[i])
        for i in range(MIX_PIECES):
            load(i).start()

        def rows_piece(i, carry):
            load(i).wait()
            for j in range(cpp):
                rows = pl.ds(pl.multiple_of(i * rp + j * ACH, ACH), ACH)
                _, _, vn = _layernorm(va_buf[rows, :].astype(f32), g_ref[...], b_ref[...])
                for gi in range(2):
                    sl = slice(gi * ACH, (gi + 1) * ACH)
                    sv_buf[rows, sl] = (_dot(ws_ref[gi], vn[:, sl]) + bs_ref[:, gi:gi + 1]).astype(bf16)
                col_buf[0, rows, :] = vn[:, half:half + ACH]
                col_buf[1, rows, :] = vn[:, half + ACH:]
            return carry

        lax.fori_loop(0, MIX_PIECES, rows_piece, 0)

        def cols_step(cidx, carry):
            rows = pl.ds(cidx, ACH, stride=GW)
            for gi in range(2, 4):
                col_buf[gi - 2, rows, :] = _dot(ws_ref[gi], col_buf[gi - 2, rows, :]) + bs_ref[:, gi:gi + 1]
            return carry

        lax.fori_loop(0, GW, cols_step, 0, unroll=8)

        def out_piece(i, carry):
            for j in range(cpp):
                rows = pl.ds(pl.multiple_of(i * rp + j * ACH, ACH), ACH)
                sv_buf[rows, half:half + ACH] = col_buf[0, rows, :].astype(bf16)
                sv_buf[rows, half + ACH:] = col_buf[1, rows, :].astype(bf16)
            store(i).start()
            return carry

        lax.fori_loop(0, MIX_PIECES, out_piece, 0)
        for i in range(MIX_PIECES):
            store(i).wait()

    vm = pl.BlockSpec(memory_space=pltpu.VMEM)
    hbm = pl.BlockSpec(memory_space=pl.ANY)
    return pl.pallas_call(
        body, name="mix_fwd", out_shape=jax.ShapeDtypeStruct((l, AW), bf16),
        in_specs=[hbm, vm, vm, vm, vm], out_specs=hbm,
        scratch_shapes=[pltpu.VMEM((l, AW), bf16), pltpu.VMEM((2, l, ACH), f32), pltpu.VMEM((l, AW), bf16),
                        pltpu.SemaphoreType.DMA((MIX_PIECES,)), pltpu.SemaphoreType.DMA((MIX_PIECES,))],
        compiler_params=_cp(),
    )(p, ln_g, ln_b, ws, bs_t)


def _mix_bwd(p, dsv, ln_g, ln_b, ws_t):
    l = p.shape[0]
    half = AW // 2
    rp = l // MIX_PIECES
    cpp = rp // ACH

    def body(p_ref, dsv_hbm, g_ref, b_ref, wst_ref, dva_hbm, dws_ref, dbs_ref, dg_ref, db_ref,
             va_buf, dsv_buf, vn_col, ds_col, dva_buf, va_sems, ds_sems, out_sems):
        piece = lambda i: pl.ds(pl.multiple_of(i * rp, rp), rp)
        load_va = lambda i: pltpu.make_async_copy(p_ref.at[piece(i), pl.ds(PVA, AW)], va_buf.at[piece(i)], va_sems.at[i])
        load_ds = lambda i: pltpu.make_async_copy(dsv_hbm.at[piece(i)], dsv_buf.at[piece(i)], ds_sems.at[i])
        store = lambda i: pltpu.make_async_copy(dva_buf.at[piece(i)], dva_hbm.at[piece(i)], out_sems.at[i])
        for i in range(MIX_PIECES):
            load_va(i).start()
            load_ds(i).start()
        dws_ref[...] = jnp.zeros_like(dws_ref)
        dbs_ref[...] = jnp.zeros_like(dbs_ref)
        dg_ref[...] = jnp.zeros_like(dg_ref)
        db_ref[...] = jnp.zeros_like(db_ref)

        def rows_piece(i, carry):
            load_va(i).wait()
            load_ds(i).wait()
            for j in range(cpp):
                rows = pl.ds(pl.multiple_of(i * rp + j * ACH, ACH), ACH)
                _, _, vn = _layernorm(va_buf[rows, :].astype(f32), g_ref[...], b_ref[...])
                ds = dsv_buf[rows, :].astype(f32)
                for gi in range(2):
                    sl = slice(gi * ACH, (gi + 1) * ACH)
                    dws_ref[gi] += _nt(ds[:, sl], vn[:, sl])
                    dbs_ref[gi] += ds[:, sl]
                for gi in range(2):
                    sl = slice(half + gi * ACH, half + (gi + 1) * ACH)
                    vn_col[gi, rows, :] = vn[:, sl]
                    ds_col[gi, rows, :] = ds[:, sl]
            return carry

        lax.fori_loop(0, MIX_PIECES, rows_piece, 0)

        def cols_step(cidx, carry):
            rows = pl.ds(cidx, ACH, stride=GW)
            for gi in range(2, 4):
                ds = ds_col[gi - 2, rows, :]
                dws_ref[gi] += _nt(ds, vn_col[gi - 2, rows, :])
                dbs_ref[gi] += ds
                ds_col[gi - 2, rows, :] = _dot(wst_ref[gi], ds)
            return carry

        lax.fori_loop(0, GW, cols_step, 0, unroll=8)

        def out_piece(i, carry):
            for j in range(cpp):
                rows = pl.ds(pl.multiple_of(i * rp + j * ACH, ACH), ACH)
                vhat, rstd, _ = _layernorm(va_buf[rows, :].astype(f32), g_ref[...], b_ref[...])
                ds = dsv_buf[rows, :].astype(f32)
                dvn = jnp.concatenate([_dot(wst_ref[0], ds[:, 0:ACH]), _dot(wst_ref[1], ds[:, ACH:half]),
                                       ds_col[0, rows, :], ds_col[1, rows, :]], axis=1)
                dg_ref[...] += jnp.sum(dvn * vhat, axis=0, keepdims=True)
                db_ref[...] += jnp.sum(dvn, axis=0, keepdims=True)
                dvh = dvn * g_ref[...]
                dva = rstd * (dvh - jnp.mean(dvh, axis=-1, keepdims=True) - vhat * jnp.mean(dvh * vhat, axis=-1, keepdims=True))
                dva_buf[rows, :] = dva.astype(bf16)
            store(i).start()
            return carry

        lax.fori_loop(0, MIX_PIECES, out_piece, 0)
        for i in range(MIX_PIECES):
            store(i).wait()

    vm = pl.BlockSpec(memory_space=pltpu.VMEM)
    hbm = pl.BlockSpec(memory_space=pl.ANY)
    dma = lambda: pltpu.SemaphoreType.DMA((MIX_PIECES,))
    return pl.pallas_call(
        body, name="mix_bwd",
        out_shape=(jax.ShapeDtypeStruct((l, AW), bf16), jax.ShapeDtypeStruct((4, ACH, ACH), f32), jax.ShapeDtypeStruct((4, ACH, ACH), f32),
                   jax.ShapeDtypeStruct((1, AW), f32), jax.ShapeDtypeStruct((1, AW), f32)),
        in_specs=[hbm, hbm, vm, vm, vm], out_specs=(hbm, vm, vm, vm, vm),
        scratch_shapes=[pltpu.VMEM((l, AW), bf16), pltpu.VMEM((l, AW), bf16), pltpu.VMEM((2, l, ACH), f32), pltpu.VMEM((2, l, ACH), f32),
                        pltpu.VMEM((l, AW), bf16), dma(), dma(), dma()],
        compiler_params=_cp(),
    )(p, dsv, ln_g, ln_b, ws_t)


def _mid(x, tgt, p, o_f, o_b, sv, gate, gf, gb_norm, w_pa, w_pb, w_out, tl):
    l = x.shape[0]

    def body(x_ref, t_ref, zb_ref, ua_ref, za_ref, g1_ref, g2_ref, of_ref, ob_ref, sv_ref, gate_ref, gf_ref, gbn_ref,
             wpa_ref, wpb_ref, wout_ref,
             dx1_ref, dzbua_ref, dzag_ref, dsv_ref, do_ref, dwout_bf, dwpa_bf, dwpb_bf, dgf_ref, dgate_ref, dgbn_ref, loss_ref,
             dwout_ref, dwpa_ref, dwpb_ref):
        @pl.when(pl.program_id(0) == 0)
        def _():
            for r in (dwout_ref, dwpa_ref, dwpb_ref, dgf_ref, dgate_ref, dgbn_ref, loss_ref):
                r[...] = jnp.zeros_like(r)

        o = of_ref[...].astype(f32) + ob_ref[...].astype(f32)
        rr = jnp.concatenate(
            [jnp.broadcast_to(lax.rsqrt(jnp.mean(o[:, h * HV:(h + 1) * HV] ** 2, axis=-1, keepdims=True) + EPS), (tl, HV))
             for h in range(NH)], axis=1)
        ohat = o * rr
        on = ohat * gbn_ref[...]
        szb, dszb = _silu_and_grad(zb_ref[...].astype(f32))
        tb = on * szb
        u = ua_ref[...].astype(f32)
        svv = sv_ref[...].astype(f32)
        sza, dsza = _silu_and_grad(za_ref[...].astype(f32))
        ta = u * svv * sza
        ya = _dot(ta, wpa_ref[...])
        yb = _dot(tb, wpb_ref[...])
        g1 = _sigmoid(g1_ref[...].astype(f32))
        g2 = _sigmoid(g2_ref[...].astype(f32))
        m = g1 * ya + g2 * yb
        y2 = _dot(m, wout_ref[...])
        x1 = x_ref[...] + gate_ref[...] * y2
        r1 = lax.rsqrt(jnp.mean(x1 * x1, axis=-1, keepdims=True) + EPS)
        x1n = x1 * r1
        err = x1n * gf_ref[...] - t_ref[...]
        loss_ref[...] += jnp.sum(jnp.sum(err * err, axis=-1, keepdims=True), axis=0, keepdims=True) * (0.5 / D)
        dout = err * (1.0 / D)
        dgf_ref[...] += jnp.sum(dout * x1n, axis=0, keepdims=True)
        dx1n = dout * gf_ref[...]
        dx1 = r1 * (dx1n - x1n * jnp.mean(dx1n * x1n, axis=-1, keepdims=True))
        dx1_ref[...] = dx1
        dgate_ref[...] += jnp.sum(dx1 * y2, axis=0, keepdims=True)
        dy2 = dx1 * gate_ref[...]
        dwout_ref[...] += _tn(m, dy2)
        dm = _nt(dy2, wout_ref[...])
        dya = dm * g1
        dyb = dm * g2
        dzag_ref[:, AW:AW + D] = (dm * ya * g1 * (1.0 - g1)).astype(bf16)
        dzag_ref[:, AW + D:] = (dm * yb * g2 * (1.0 - g2)).astype(bf16)
        dwpa_ref[...] += _tn(ta, dya)
        dta = _nt(dya, wpa_ref[...])
        dzbua_ref[:, AW:] = (dta * svv * sza).astype(bf16)
        dsv_ref[...] = (dta * u * sza).astype(bf16)
        dzag_ref[:, 0:AW] = (dta * u * svv * dsza).astype(bf16)
        dwpb_ref[...] += _tn(tb, dyb)
        dtb = _nt(dyb, wpb_ref[...])
        don = dtb * szb
        dzbua_ref[:, 0:AW] = (dtb * on * dszb).astype(bf16)
        dgbn_ref[...] += jnp.sum(don * ohat, axis=0, keepdims=True)
        doh = don * gbn_ref[...]
        prod = doh * ohat
        mh = jnp.concatenate(
            [jnp.broadcast_to(jnp.mean(prod[:, h * HV:(h + 1) * HV], axis=-1, keepdims=True), (tl, HV)) for h in range(NH)], axis=1)
        do_ref[...] = (rr * (doh - ohat * mh)).astype(bf16)

        @pl.when(pl.program_id(0) == l // tl - 1)
        def _():
            for acc, out in ((dwout_ref, dwout_bf), (dwpa_ref, dwpa_bf), (dwpb_ref, dwpb_bf)):
                out[...] = acc[...].astype(bf16)

    row = lambda w, j: pl.BlockSpec((tl, w), lambda i, j=j: (i, j))
    full = lambda *s: pl.BlockSpec(s, lambda i: (0,) * len(s))
    return pl.pallas_call(
        body, name="mid", grid=(l // tl,),
        out_shape=(jax.ShapeDtypeStruct((l, D), f32), jax.ShapeDtypeStruct((l, 2 * AW), bf16), jax.ShapeDtypeStruct((l, AW + 2 * D), bf16),
                   jax.ShapeDtypeStruct((l, AW), bf16), jax.ShapeDtypeStruct((l, VW), bf16),
                   jax.ShapeDtypeStruct((D, D), bf16), jax.ShapeDtypeStruct((AW, D), bf16), jax.ShapeDtypeStruct((VW, D), bf16),
                   jax.ShapeDtypeStruct((1, D), f32), jax.ShapeDtypeStruct((1, D), f32), jax.ShapeDtypeStruct((1, VW), f32),
                   jax.ShapeDtypeStruct((1, 1), f32)),
        scratch_shapes=[pltpu.VMEM((D, D), f32), pltpu.VMEM((AW, D), f32), pltpu.VMEM((VW, D), f32)],
        in_specs=[row(D, 0), row(D, 0), row(AW, PZB // AW), row(AW, PUA // AW), row(AW, PZA // AW), row(D, PG1 // D), row(D, PG2 // D),
                  row(VW, 0), row(VW, 0), row(AW, 0), full(1, D), full(1, D), full(1, VW), full(AW, D), full(VW, D), full(D, D)],
        out_specs=(row(D, 0), row(2 * AW, 0), row(AW + 2 * D, 0), row(AW, 0), row(VW, 0),
                   full(D, D), full(AW, D), full(VW, D), full(1, D), full(1, D), full(1, VW), full(1, 1)),
        compiler_params=_cp(("arbitrary",)),
    )(x, tgt, p, p, p, p, p, o_f, o_b, sv, gate, gf, gb_norm, w_pa, w_pb, w_out)


def _in_bwd(x, dx1, dqkv, dzbua, dva, dzag, dlr, w_pad, g, scale, tl, comm):
    l = x.shape[0]
    c_in, c_out, c_sems = comm.specs()

    def body(*refs):
        cin = refs[14:14 + len(c_in)]
        outs = refs[14 + len(c_in):]
        comm.run(cin, outs[4:4 + comm.n], outs[4 + comm.n:], l // tl, lambda: compute(*refs[:14], *outs[:4]))

    def compute(x_ref, dx1_ref, a_ref, b_ref, c_ref, e_ref, lr_ref, wa_ref, wb_ref, wc_ref, we_ref, wl_ref, g_ref, sc_ref,
                gx_ref, dsh_ref, dsc_ref, dg_ref):
        @pl.when(pl.program_id(0) == 0)
        def _():
            for r in (dsh_ref, dsc_ref, dg_ref):
                r[...] = jnp.zeros_like(r)

        dh = (_nt(a_ref[...], wa_ref[...]) + _nt(b_ref[...], wb_ref[...]) + _nt(c_ref[...], wc_ref[...]) + _nt(e_ref[...], we_ref[...])
              + _nt(lr_ref[...], wl_ref[...]))
        xv = x_ref[...]
        r = lax.rsqrt(jnp.mean(xv * xv, axis=-1, keepdims=True) + EPS)
        xn = xv * r
        gxn = jnp.sum(dh * xn, axis=0, keepdims=True)
        dsh_ref[...] += jnp.sum(dh, axis=0, keepdims=True)
        dsc_ref[...] += gxn * g_ref[...]
        dg_ref[...] += gxn * (1.0 + sc_ref[...])
        dxn = dh * (g_ref[...] * (1.0 + sc_ref[...]))
        gx_ref[...] = dx1_ref[...] + r * (dxn - xn * jnp.mean(dxn * xn, axis=-1, keepdims=True))

    row = lambda w: pl.BlockSpec((tl, w), lambda i: (i, 0))
    wcol = lambda w, j: pl.BlockSpec((D, w), lambda i, j=j: (0, j))
    vec = pl.BlockSpec((1, D), lambda i: (0, 0))
    return pl.pallas_call(
        body, name="in_bwd", grid=(l // tl,),
        out_shape=(jax.ShapeDtypeStruct((l, D), f32),) + (jax.ShapeDtypeStruct((1, D), f32),) * 3 + tuple(comm.outs),
        in_specs=[row(D), row(D), row(2 * KW + VW), row(2 * AW), row(AW), row(AW + 2 * D), row(LRW),
                  wcol(2 * KW + VW, 0), wcol(2 * AW, PZB // (2 * AW)), wcol(AW, PVA // AW), wcol(AW + 2 * D, PZA // (AW + 2 * D)),
                  wcol(LRW, PLR // LRW), vec, vec] + c_in,
        out_specs=(row(D), vec, vec, vec) + tuple(c_out), scratch_shapes=c_sems,
        compiler_params=_cp(("arbitrary",)),
    )(x, dx1, dqkv, dzbua, dva, dzag, dlr, w_pad, w_pad, w_pad, w_pad, w_pad, g, scale, *comm.ins)


def _tn_matmul(a, bs, tl, name, comm=None, pack=None):
    l, m = a.shape
    k = len(bs)
    comm = comm or _Comm([], [], None)
    c_in, c_out, c_sems = comm.specs()
    acc_shapes = [(m, b.shape[1]) for b in bs]
    n_extra = len(pack.extra) if pack else 0
    n_res = len(pack.outs) if pack else k

    def body(a_ref, *refs):
        b_refs, refs = refs[:k], refs[k:]
        extra, refs = refs[:n_extra], refs[n_extra:]
        cin, refs = refs[:len(c_in)], refs[len(c_in):]
        res, refs = refs[:n_res], refs[n_res:]
        cout, refs = refs[:comm.n], refs[comm.n:]
        accs, sems = (refs[:k], refs[k:]) if pack else (res, refs)

        def compute():
            @pl.when(pl.program_id(0) == 0)
            def _():
                for o_ref in accs:
                    o_ref[...] = jnp.zeros_like(o_ref)

            av = a_ref[...]
            for b_ref, o_ref in zip(b_refs, accs):
                o_ref[...] += _tn(av, b_ref[...])
            if pack:
                pl.when(pl.program_id(0) == l // tl - 1)(lambda: pack.fn(accs, extra, res))

        comm.run(cin, cout, sems, l // tl, compute)

    full = lambda shape: pl.BlockSpec(shape, lambda i: (0,) * len(shape))
    res_shapes = list(pack.outs) if pack else [jax.ShapeDtypeStruct(sh, f32) for sh in acc_shapes]
    return pl.pallas_call(
        body, name=name, grid=(l // tl,), out_shape=tuple(res_shapes) + tuple(comm.outs),
        in_specs=[pl.BlockSpec((tl, m), lambda i: (i, 0))] + [pl.BlockSpec((tl, b.shape[1]), lambda i: (i, 0)) for b in bs]
        + [full(e.shape) for e in (pack.extra if pack else [])] + c_in,
        out_specs=tuple(full(r.shape) for r in res_shapes) + tuple(c_out),
        scratch_shapes=([pltpu.VMEM(sh, f32) for sh in acc_shapes] if pack else []) + c_sems, compiler_params=_cp(("arbitrary",)),
    )(a, *bs, *(pack.extra if pack else []), *comm.ins)


def _pad_gate(w2, gb):
    z = jnp.zeros((RANK, KW), f32)
    tail = jnp.zeros((LRW - 2 * RANK, KW), f32)
    w2f = jnp.concatenate([w2[0], z, tail], axis=0)
    w2b = jnp.concatenate([z, w2[1], tail], axis=0)
    return w2f, w2b, gb[0:1], gb[1:2]


EARLY_A_ROWS = 512


class _NoExchange:
    def __init__(self, w_pa, w_pb, w_out):
        self.weights = (w_pa, w_pb, w_out)

    def gather_proj(self):
        return _Comm([], [], None)

    def proj_weights(self, got):
        return self.weights

    def first(self, dw_out, dw_pa, dw_pb, small):
        return _Comm([], [], None)

    def early_a(self, blocks):
        return _Comm([], [], None)

    def early_b(self, blocks):
        return _Comm([], [], None)

    def late(self, blocks, dgt):
        return _Comm([], [], None)


class _Exchanges:
    def __init__(self, pa, pb, wo):
        self.shards = (pa, pb, wo)

    def gather_proj(self):
        def plan(i, o):
            srcs = [lambda j, r=r: r for r in i]
            dsts = [lambda j: o[0].at[:, _lanes(j)], lambda j: o[1].at[:, _lanes(j)], lambda j: o[2].at[j]]
            return srcs, dsts, None
        sds = jax.ShapeDtypeStruct
        return _Comm(self.shards, [sds((AW, D), bf16), sds((VW, D), bf16), sds((NDEV, 128, D), bf16)], plan)

    def proj_weights(self, got):
        return got[0], got[1], got[2].reshape(D, D)

    def first(self, dw_out, dw_pa, dw_pb, small):
        def plan(i, o):
            srcs = [lambda j: i[0].at[j], lambda j: i[1].at[:, _lanes(j)], lambda j: i[2].at[:, _lanes(j)], lambda j: i[3]]
            dsts = [lambda j, r=r: r.at[j] for r in o]
            return srcs, dsts, None
        sds = jax.ShapeDtypeStruct
        return _Comm([dw_out.reshape(NDEV, 128, D), dw_pa, dw_pb, small],
                     [sds((NDEV, 128, D), bf16), sds((NDEV, AW, 128), bf16), sds((NDEV, VW, 128), bf16),
                      sds((NDEV,) + small.shape, f32)], plan)

    def early_a(self, blocks):
        def plan(i, o):
            return [lambda j: i[0].at[j]], [lambda j: o[0].at[j]], [lambda j: j >= LATE_DESTS - 1]
        return _Comm([blocks], [jax.ShapeDtypeStruct(blocks.shape, bf16)], plan)

    def early_b(self, blocks):
        def plan(i, o):
            return [lambda j: i[0].at[j]], [lambda j: o[0].at[j]], [lambda j: j >= LATE_DESTS - 1]
        return _Comm([blocks], [jax.ShapeDtypeStruct(blocks.shape, bf16)], plan)

    def late(self, blocks, dgt):
        return _LateComm(blocks, dgt)


def _local_step(x, ctx, tgt, mod, modc, norm_g, w_pad, ln_g, ln_b, ws, bs, w2, gb, gb_norm, gf, xch):
    shift, scale, gate = mod[:, 0:D], mod[:, D:2 * D], mod[:, 2 * D:]
    shift_c, scale_c = modc[:, 0:D], modc[:, D:]
    w2f, w2b, gbf, gbb = _pad_gate(w2, gb)

    p, h, *got_proj = _in_proj(x, norm_g, scale, shift, w_pad, 512, xch.gather_proj())
    w_pa, w_pb, w_out = xch.proj_weights(got_proj)
    sc_f, sc_b = _ctx_fwd(ctx, norm_g, scale_c, shift_c, w_pad, w2f, w2b, gbf, gbb)
    o_f, st_f = _gla_fwd(p, w2f, gbf, sc_f, False, 512, "gla_fwd_f")
    o_b, st_b = _gla_fwd(p, w2b, gbb, sc_b, True, 512, "gla_fwd_b")
    sv = _mix_fwd(p, ln_g, ln_b, ws.astype(bf16), bs.T)
    (dx1, dzbua, dzag, dsv, do, dw_out, dw_pa, dw_pb, dgf, dgate, dgbn, loss) = _mid(
        x, tgt, p, o_f, o_b, sv, gate, gf, gb_norm, w_pa, w_pb, w_out, 256)
    dva, dws, dbs_acc, dln_g, dln_b = _mix_bwd(p, dsv, ln_g, ln_b, jnp.swapaxes(ws, 1, 2).astype(bf16))
    small = _rows128(dln_g, dln_b, dws, jnp.sum(dbs_acc, axis=-1), dgbn, dgf, jnp.broadcast_to(loss, (1, 128)))
    blocks_a, blocks_b, *got_first = _tn_matmul(h, [dzbua, dva, dzag], 1024, "dw_early", xch.first(dw_out, dw_pa, dw_pb, small),
                                               _pack_early())

    dqkv_f, dlr_f, dw2f, dgbf, dsc_f, *got_a = _gla_bwd(p, do, st_f, w2f, gbf, None, False, 512, "gla_bwd_f",
                                                        xch.early_a(blocks_a))
    dqkv, dlr, dw2b, dgbb, dsc_b, *got_b = _gla_bwd(p, do, st_b, w2b, gbb, (dqkv_f, dlr_f), True, 512, "gla_bwd_b",
                                                    xch.early_b(blocks_b))
    dwk_c, dwv_c, dwl_c, dmodc, dg_c, dw2c, dgbc = _ctx_bwd(ctx, norm_g, scale_c, shift_c, w_pad, w2f, w2b, gbf, gbb, dsc_f, dsc_b)
    (blocks_late,) = _tn_matmul(h, [dqkv, dlr], 1024, "dw_qkv_lr", pack=_pack_late(dwk_c, dwv_c, dwl_c))
    dw2 = jnp.stack([dw2f[0:RANK] + dw2c[0, 0:RANK], dw2b[RANK:2 * RANK] + dw2c[1, RANK:2 * RANK]])
    dgb = jnp.concatenate([dgbf + dgbc[0], dgbb + dgbc[1]], axis=0)
    dgt = jnp.concatenate([jnp.transpose(dw2.reshape(2, RANK, NDEV, 32), (2, 0, 1, 3)).reshape(NDEV, 2 * RANK * 32),
                           jnp.transpose(dgb.reshape(2, NDEV, 32), (1, 0, 2)).reshape(NDEV, 64),
                           jnp.zeros((NDEV, 64), f32)], axis=1).reshape(NDEV, 9, 128)
    gx, dshift, dscale, dg, *got_late = _in_bwd(x, dx1, dqkv, dzbua, dva, dzag, dlr, w_pad, norm_g, scale, 512,
                                                xch.late(blocks_late, dgt))
    return dict(loss=loss, gx=gx, dmod=jnp.concatenate([dshift, dscale, dgate], axis=1), dmodc=dmodc, dnorm_g=dg + dg_c,
                dnorm_parts=(dg, dg_c), dmod_parts=(dshift, dscale, dgate),
                small=small, blocks_a=blocks_a, blocks_b=blocks_b, blocks_late=blocks_late, dw2=dw2, dgb=dgb,
                dw_pa=dw_pa, dw_pb=dw_pb, dw_out=dw_out, got_first=got_first, got_a=got_a, got_b=got_b, got_late=got_late)


def _rows128(*vs):
    out = []
    for t in vs:
        t = t.reshape(-1)
        pad = (-t.shape[0]) % 128
        out.append(jnp.pad(t, (0, pad)) if pad else t)
    return jnp.concatenate(out).reshape(-1, 128)


def kernel(x, c, ctx, c_ctx, w_mod, b_mod, norm_g, w_in, a_ln_g, a_ln_b, a_ws, a_bs, b_gate_w2, b_gate_b, b_norm_g, w_proj_a, w_proj_b, w_out, final_norm_g, loss_target, m_c_ctx, m_w_mod, m_b_mod, m_norm_g, m_w_in, m_a_ln_g, m_a_ln_b, m_a_ws, m_a_bs, m_b_gate_w2, m_b_gate_b, m_b_norm_g, m_w_proj_a, m_w_proj_b, m_w_out, m_final_norm_g, v_c_ctx, v_w_mod, v_b_mod, v_norm_g, v_w_in, v_a_ln_g, v_a_ln_b, v_a_ws, v_a_bs, v_b_gate_w2, v_b_gate_b, v_b_norm_g, v_w_proj_a, v_w_proj_b, v_w_out, v_final_norm_g):
    me = _me()

    cs, mods, wg, gates = _gather_first(c, c_ctx.reshape(1, D), w_mod[0], b_mod, w_in[0].astype(bf16), b_gate_w2, b_gate_b)
    w_pad = _repack_w(wg)
    gflat = gates.reshape(NDEV, 9 * 128)
    w2 = jnp.transpose(gflat[:, 0:2 * RANK * 32].reshape(NDEV, 2, RANK, 32), (1, 2, 0, 3)).reshape(2, RANK, KW)
    gb = jnp.transpose(gflat[:, 2 * RANK * 32:2 * RANK * 32 + 64].reshape(NDEV, 2, 32), (1, 0, 2)).reshape(2, KW)

    mods = jnp.transpose(mods, (1, 0, 2)).reshape(16, 3 * D)
    mod = lax.dynamic_slice(mods, (me, 0), (1, 3 * D))
    modc = mods[8:9, 0:2 * D]

    xch = _Exchanges(w_proj_a[0].astype(bf16), w_proj_b[0].astype(bf16), w_out[0].astype(bf16))
    r = _local_step(x[0], ctx[0], loss_target[0], mod, modc, norm_g, w_pad, a_ln_g, a_ln_b, a_ws[0], a_bs[0], w2, gb,
                    b_norm_g, final_norm_g.reshape(1, D), xch)
    p_out, p_pa, p_pb, smalls_e = r["got_first"]
    (p_in_a,) = r["got_a"]
    (p_in_b,) = r["got_b"]
    _, _, p_in_late, p_gt = r["got_late"]

    n_e = (AW + AW + 4 * ACH * ACH + AW + VW + D) // 128
    row = lambda t: t.reshape(1, D)
    rep_e = _adam_params(smalls_e, [a_ln_g, a_ln_b, a_ws, a_bs, b_norm_g, row(final_norm_g)],
                         [m_a_ln_g, m_a_ln_b, m_a_ws, m_a_bs, m_b_norm_g, row(m_final_norm_g)],
                         [v_a_ln_g, v_a_ln_b, v_a_ws, v_a_bs, v_b_norm_g, row(v_final_norm_g)], "adam_rep_early")
    rep_e = [t[0:5] + (t[5].reshape(D),) for t in rep_e]
    loss = jnp.sum(smalls_e[:, n_e, 0])

    mod_grads = _mod_tail(r["dnorm_parts"], r["dmod_parts"], r["dmodc"], cs, w_mod)
    mod_res = _adam_plain(mod_grads, [w_mod, norm_g, b_mod, row(c_ctx)], [m_w_mod, m_norm_g, m_b_mod, row(m_c_ctx)],
                          [v_w_mod, v_norm_g, v_b_mod, row(v_c_ctx)], "adam_mod")
    wm, ng, bmod_r, cc = ([t[i] for t in mod_res] for i in range(4))

    a_in = _adam_w_in(p_in_a, p_in_b, p_in_late, w_in, m_w_in, v_w_in)
    blk = _adam_blocks([p_pa, p_pb, p_out], [w_proj_a, w_proj_b, w_out], [m_w_proj_a, m_w_proj_b, m_w_out],
                       [v_w_proj_a, v_w_proj_b, v_w_out], "adam_proj_out")
    a_pa, a_pb, a_out = ([t[i] for t in blk] for i in range(3))
    a_gt = _adam_gate(p_gt, b_gate_w2, b_gate_b, m_b_gate_w2, m_b_gate_b, v_b_gate_w2, v_b_gate_b)
    sh = [(a_in[k], a_pa[k], a_pb[k], a_out[k], a_gt[k], a_gt[4 + k]) for k in range(4)]

    outs = [loss, r["gx"][None]]
    for k in range(4):
        lg, lb, aws, abs_, bng, fng = rep_e[k]
        n_g, bmod = ng[k], bmod_r[k]
        s_in, s_pa, s_pb, s_out, s_w2, s_gb = sh[k]
        outs += [cc[k].reshape(D), wm[k], bmod, n_g, s_in, lg, lb, aws, abs_, s_w2, s_gb, bng, s_pa, s_pb, s_out, fng]
    return tuple(outs)
```

```python
import jax
import jax.numpy as jnp
from jax import lax
from jax.experimental import pallas as pl
from jax.experimental.pallas import tpu as pltpu

f32, bf16 = jnp.float32, jnp.bfloat16

D = 1024
CTX = 256
EPS = 1e-6
AW = 512
ACH = 128
GW = 64
KW = 256
VW = 512
NH = 4
HK = 64
HV = 128
RANK = 16
TAU = 16.0
CH = 64
QSCALE = HK ** -0.5
INW = 5152
NDEV = 8

PQ, PK, PV, PZB, PUA, PVA, PZA, PG1, PG2, PLR, PW = 0, 256, 512, 1024, 1536, 2048, 2560, 3072, 4096, 5120, 5248
LRW = 128

ADAM_LR, ADAM_B1, ADAM_B2, ADAM_EPS, ADAM_WD, ADAM_STEP = 0.001, 0.9, 0.999, 1e-08, 0.01, 10

VMEM_LIMIT = 56 * 1024 * 1024
MESH = pl.DeviceIdType.MESH


def _cp(sem=None):
    return pltpu.CompilerParams(dimension_semantics=sem, vmem_limit_bytes=VMEM_LIMIT)


def _dot(a, b):
    return jnp.dot(a.astype(bf16), b.astype(bf16), preferred_element_type=f32)


def _nt(a, b):
    return lax.dot_general(a.astype(bf16), b.astype(bf16), (((1,), (1,)), ((), ())), preferred_element_type=f32)


def _tn(a, b):
    return lax.dot_general(a.astype(bf16), b.astype(bf16), (((0,), (0,)), ((), ())), preferred_element_type=f32)


def _dot_hi(a, b):
    return jnp.dot(a, b, preferred_element_type=f32, precision=lax.Precision.HIGHEST)


def _sigmoid(x):
    return 1.0 / (1.0 + jnp.exp(-x))


def _log_sigmoid(x):
    return jnp.minimum(x, 0.0) - jnp.log(1.0 + jnp.exp(-jnp.abs(x)))


def _silu_and_grad(z):
    s = _sigmoid(z)
    return z * s, s * (1.0 + z * (1.0 - s))


def _me():
    return 4 * lax.axis_index("x") + 2 * lax.axis_index("y") + lax.axis_index("c")


def _peer(k):
    x, y, c = lax.axis_index("x"), lax.axis_index("y"), lax.axis_index("c")
    px = 1 - x if k & 4 else x
    py = 1 - y if k & 2 else y
    pc = 1 - c if k & 1 else c
    return (px, py, pc), 4 * px + 2 * py + pc


def _fanout(srcs, dsts, send_sems, recv_sems, local_sems, owners=None):
    me = _me()
    n = len(srcs)
    owns = lambda a, j: True if owners is None or owners[a] is None else owners[a](j)

    def guarded(cond, fn):
        if cond is True:
            fn()
        else:
            pl.when(cond)(fn)

    def copies(with_recvs):
        local = [pltpu.make_async_copy(srcs[a](me), dsts[a](me), local_sems.at[a]) for a in range(n)]
        sends, recvs = [], []
        for k in range(1, NDEV):
            dev, idx = _peer(k)
            for a in range(n):
                s = (k - 1) * n + a
                sends.append((owns(a, idx), pltpu.make_async_remote_copy(
                    src_ref=srcs[a](idx), dst_ref=dsts[a](me), send_sem=send_sems.at[s], recv_sem=recv_sems.at[s],
                    device_id=dev, device_id_type=MESH)))
                if with_recvs:
                    recvs.append((owns(a, me), pltpu.make_async_remote_copy(
                        src_ref=srcs[a](idx), dst_ref=dsts[a](idx), send_sem=send_sems.at[s], recv_sem=recv_sems.at[s],
                        device_id=dev, device_id_type=MESH)))
        return local, sends, recvs

    def start():
        local, sends, _ = copies(False)
        for a, cp in enumerate(local):
            guarded(owns(a, me), cp.start)
        for cond, cp in sends:
            guarded(cond, cp.start)

    def finish():
        local, sends, recvs = copies(True)
        for cond, cp in recvs:
            guarded(cond, cp.wait_recv)
        for cond, cp in sends:
            guarded(cond, cp.wait_send)
        for a, cp in enumerate(local):
            guarded(owns(a, me), cp.wait)

    return start, finish


class _Comm:
    def __init__(self, ins, outs, plan):
        self.ins, self.outs, self.plan = list(ins), list(outs), plan
        self.n = len(self.outs)

    def specs(self):
        hbm = pl.BlockSpec(memory_space=pl.ANY)
        return [hbm] * len(self.ins), [hbm] * self.n, _fanout_sems(self.n) if self.n else []

    def run(self, in_refs, out_refs, sems, nsteps, compute):
        if not self.n:
            compute()
            return

        def hooks():
            srcs, dsts, owners = self.plan(in_refs, out_refs)
            return _fanout(srcs, dsts, sems[0], sems[1], sems[2], owners)

        pl.when(pl.program_id(0) == 0)(lambda: hooks()[0]())
        compute()
        pl.when(pl.program_id(0) == nsteps - 1)(lambda: hooks()[1]())


LATE_MID_STEP = 1


class _LateComm:
    def __init__(self, blocks, dgt):
        sds = jax.ShapeDtypeStruct
        self.ins = [blocks, dgt]
        self.outs = [sds((D, SHARD), bf16), sds((D, SHARD), bf16), sds((4, D, SHARD), bf16), sds(dgt.shape, f32)]
        self.n = len(self.outs)

    def specs(self):
        hbm = pl.BlockSpec(memory_space=pl.ANY)
        scratch = [pltpu.VMEM((3, D, SHARD), bf16), pltpu.SemaphoreType.DMA((2,)), pltpu.SemaphoreType.DMA((4,)),
                   pltpu.SemaphoreType.DMA((3,))] + _fanout_sems(1)
        return [hbm] * 2, [hbm] * self.n, scratch

    def run(self, in_refs, out_refs, scratch, nsteps, compute):
        late_ref, dgt_ref = in_refs
        sib_ref, pair_ref, parts_ref, gt_ref = out_refs
        vbuf, send_sems, recv_sems, local_sems, g_send, g_recv, g_local = scratch
        x, y, c = lax.axis_index("x"), lax.axis_index("y"), lax.axis_index("c")
        chip = 2 * x + y
        is_owner_chip = chip == 0
        step = pl.program_id(0)

        def to_sibling():
            return pltpu.make_async_remote_copy(src_ref=late_ref.at[1 - c], dst_ref=sib_ref, send_sem=send_sems.at[0],
                                                recv_sem=recv_sems.at[0], device_id=(x, y, 1 - c), device_id_type=MESH)

        def to_owner(k):
            return pltpu.make_async_remote_copy(src_ref=pair_ref, dst_ref=parts_ref.at[k], send_sem=send_sems.at[1],
                                                recv_sem=recv_sems.at[k], device_id=(0, 0, c), device_id_type=MESH)

        def own_copy():
            return pltpu.make_async_copy(pair_ref, parts_ref.at[0], local_sems.at[2])

        def gates():
            return _fanout([lambda j: dgt_ref.at[j]], [lambda j: gt_ref.at[j]], g_send, g_recv, g_local)

        @pl.when(step == 0)
        def _():
            to_sibling().start()
            gates()[0]()

        compute()

        @pl.when(step == LATE_MID_STEP)
        def _():
            mine = pltpu.make_async_copy(late_ref.at[c], vbuf.at[0], local_sems.at[0])
            mine.start()
            to_sibling().wait_recv()
            theirs = pltpu.make_async_copy(sib_ref, vbuf.at[1], local_sems.at[1])
            theirs.start()
            mine.wait()
            theirs.wait()
            vbuf[2] = (vbuf[0].astype(f32) + vbuf[1].astype(f32)).astype(bf16)
            pltpu.sync_copy(vbuf.at[2], pair_ref)
            pl.when(is_owner_chip)(lambda: own_copy().start())
            pl.when(jnp.logical_not(is_owner_chip))(lambda: to_owner(chip).start())

        @pl.when(step == nsteps - 1)
        def _():
            @pl.when(is_owner_chip)
            def _():
                for k in range(1, 4):
                    to_owner(k).wait_recv()
                own_copy().wait()

            pl.when(jnp.logical_not(is_owner_chip))(lambda: to_owner(chip).wait_send())
            to_sibling().wait_send()
            gates()[1]()


def _fanout_sems(n):
    return [pltpu.SemaphoreType.DMA(((NDEV - 1) * n,)), pltpu.SemaphoreType.DMA(((NDEV - 1) * n,)), pltpu.SemaphoreType.DMA((n,))]


def _lanes(j):
    return pl.ds(pl.multiple_of(j * 128, 128), 128)


def _gather_first(c_row, cctx_row, wm, bm, wi, w2, gb):
    per = 128 // 32
    gate_rows = 2 * RANK // per + 1

    def body(c_ref, cctx_ref, wm_ref, bm_ref, wi_ref, w2_ref, gb_ref, cs_ref, mods_ref, owi, og, g_ref, call_ref, mine_ref,
             send_sems, recv_sems, local_sems, c_send, c_recv, c_local, m_send, m_recv, m_local):
        g_ref[...] = jnp.zeros_like(g_ref)
        for d in range(2):
            for r in range(RANK):
                q = d * RANK + r
                g_ref[q // per:q // per + 1, (q % per) * 32:(q % per + 1) * 32] = w2_ref[0, d, r:r + 1, :]
            g_ref[gate_rows - 1:gate_rows, d * 32:(d + 1) * 32] = gb_ref[0, d:d + 1, :]
        x, y, c = lax.axis_index("x"), lax.axis_index("y"), lax.axis_index("c")
        sibling = (x, y, 1 - c)
        chips = [(1 - x, y), (x, 1 - y), (1 - x, 1 - y)]
        index = lambda px, py, pc: 4 * px + 2 * py + pc
        arrays = ((wi_ref, owi), (g_ref, og))
        n = len(arrays)

        def copy(a, k, block, to, own=False):
            src, out = arrays[a]
            return pltpu.make_async_remote_copy(
                src_ref=src if own else out.at[index(*block)], dst_ref=out.at[index(*block)],
                send_sem=send_sems.at[k * n + a], recv_sem=recv_sems.at[k * n + a], device_id=to, device_id_type=MESH)

        c_start, c_finish = _fanout([lambda j: c_ref], [lambda j: call_ref.at[j]], c_send, c_recv, c_local)
        c_start()
        mine = [pltpu.make_async_copy(src, out.at[index(x, y, c)], local_sems.at[a]) for a, (src, out) in enumerate(arrays)]
        first = [copy(a, 0, (x, y, c), sibling, own=True) for a in range(n)]
        first += [copy(a, 1 + j, (x, y, c), (*chip, c), own=True) for j, chip in enumerate(chips) for a in range(n)]
        for cp in mine + first:
            cp.start()

        c_finish()
        cs = jnp.concatenate([call_ref[j] for j in range(NDEV)] + [cctx_ref[...], jnp.zeros((16 - NDEV - 1, D), f32)], axis=0)
        cs_ref[...] = cs
        s, _ = _silu_and_grad(cs)
        mine_ref[...] = _dot_hi(s, wm_ref[...]) + bm_ref[:, pl.ds(pl.multiple_of(_me() * ncol, 128), ncol)]
        m_start, m_finish = _fanout([lambda j: mine_ref], [lambda j: mods_ref.at[j]], m_send, m_recv, m_local)
        m_start()

        passed = []
        for j, chip in enumerate(chips):
            for a in range(n):
                copy(a, 1 + j, (*chip, c), (x, y, c)).wait_recv()
            for a in range(n):
                cp = copy(a, 4 + j, (*chip, c), sibling)
                cp.start()
                passed.append(cp)
        for a in range(n):
            copy(a, 0, sibling, (x, y, c)).wait_recv()
        for j, chip in enumerate(chips):
            for a in range(n):
                copy(a, 4 + j, (*chip, 1 - c), (x, y, c)).wait_recv()
        for cp in first + passed:
            cp.wait_send()
        for cp in mine:
            cp.wait()
        m_finish()

    hbm = pl.BlockSpec(memory_space=pl.ANY)
    vm = pl.BlockSpec(memory_space=pltpu.VMEM)
    ncol = wm.shape[1]
    return pl.pallas_call(
        body, name="gather_first",
        out_shape=(jax.ShapeDtypeStruct((16, D), f32), jax.ShapeDtypeStruct((NDEV, 16, ncol), f32),
                   jax.ShapeDtypeStruct((NDEV,) + wi.shape, bf16), jax.ShapeDtypeStruct((NDEV, gate_rows, 128), f32)),
        in_specs=[vm, vm, vm, vm, hbm, vm, vm], out_specs=(vm, vm, hbm, hbm),
        scratch_shapes=[pltpu.VMEM((gate_rows, 128), f32), pltpu.VMEM((NDEV, 1, D), f32), pltpu.VMEM((16, ncol), f32)]
        + _fanout_sems(2) + _fanout_sems(1) + _fanout_sems(1),
        compiler_params=_cp(),
    )(c_row, cctx_row, wm, bm, wi, w2, gb)


SHARD = INW // NDEV
ROWS_RP = 128


def _overlap(lo, hi, a, b):
    s, e = max(lo, a), min(hi, b)
    return (s, e) if s < e else None


def _repack_w(wg, gates):
    segs = ((0, 1024, PQ), (1024, 1024 + 2 * RANK, PLR), (1024 + 2 * RANK, INW, PZB))
    per = 128 // 32

    def body(g_ref, gt_ref, o_ref, w2f_ref, w2b_ref, gbf_ref, gbb_ref):
        @pl.when(pl.program_id(0) == 0)
        def _():
            w2f_ref[...] = jnp.zeros_like(w2f_ref)
            w2b_ref[...] = jnp.zeros_like(w2b_ref)
            for dev in range(NDEV):
                cols = slice(dev * 32, (dev + 1) * 32)
                for d, (w2_ref, gb_ref) in enumerate(((w2f_ref, gbf_ref), (w2b_ref, gbb_ref))):
                    for r in range(RANK):
                        q = d * RANK + r
                        w2_ref[q:q + 1, cols] = gt_ref[dev, q // per:q // per + 1, (q % per) * 32:(q % per + 1) * 32]
                    gb_ref[:, cols] = gt_ref[dev, 2 * RANK // per:2 * RANK // per + 1, d * 32:(d + 1) * 32]

        for j in range(NDEV):
            lo, hi = j * SHARD, (j + 1) * SHARD
            for a, b, pad0 in segs:
                ov = _overlap(lo, hi, a, b)
                if ov:
                    s, e = ov
                    o_ref[:, pad0 + s - a:pad0 + e - a] = g_ref[j, :, s - lo:e - lo]
        o_ref[:, PLR + 2 * RANK:PW] = jnp.zeros((ROWS_RP, PW - PLR - 2 * RANK), bf16)

    sds = jax.ShapeDtypeStruct
    full = lambda *s: pl.BlockSpec(s, lambda i: (0,) * len(s))
    return pl.pallas_call(
        body, name="repack_w", grid=(D // ROWS_RP,),
        out_shape=(sds((D, PW), bf16), sds((LRW, KW), f32), sds((LRW, KW), f32), sds((1, KW), f32), sds((1, KW), f32)),
        in_specs=[pl.BlockSpec((NDEV, ROWS_RP, SHARD), lambda i: (0, i, 0)), full(*gates.shape)],
        out_specs=(pl.BlockSpec((ROWS_RP, PW), lambda i: (i, 0)), full(LRW, KW), full(LRW, KW), full(1, KW), full(1, KW)),
        compiler_params=_cp(("arbitrary",)),
    )(wg, gates)


LATE_END = 1024 + 2 * RANK
LATE_DESTS = 2


def _pack_blocks(o_ref, srcs, dests, dtype):
    for n, j in enumerate(dests):
        lo, hi = j * SHARD, (j + 1) * SHARD
        done = lo
        for a, b, src in srcs:
            ov = _overlap(lo, hi, a, b)
            if ov:
                s, e = ov
                if s > done:
                    o_ref[n, :, done - lo:s - lo] = jnp.zeros((ROWS_RP, s - done), dtype)
                o_ref[n, :, s - lo:e - lo] = src[:, s - a:e - a].astype(dtype)
                done = e
        if done < hi:
            o_ref[n, :, done - lo:hi - lo] = jnp.zeros((ROWS_RP, hi - done), dtype)


class _Pack:
    def __init__(self, extra, outs, fn):
        self.extra, self.outs, self.fn = list(extra), list(outs), fn


def _pack_early():
    def fn(accs, extra, outs):
        zbua, va, zag = accs
        for r0 in range(0, D, ROWS_RP):
            rows = pl.ds(r0, ROWS_RP)
            out, o0 = (outs[0], r0) if r0 < EARLY_A_ROWS else (outs[1], r0 - EARLY_A_ROWS)
            _pack_blocks(out.at[:, pl.ds(o0, ROWS_RP)], ((LATE_END, 2080, zbua.at[rows]), (2080, 2592, va.at[rows]), (2592, INW, zag.at[rows])),
                         range(NDEV), bf16)

    sds = jax.ShapeDtypeStruct
    return _Pack([], [sds((NDEV, EARLY_A_ROWS, SHARD), bf16), sds((NDEV, D - EARLY_A_ROWS, SHARD), bf16)], fn)


def _pack_late(dwk_c, dwv_c, dwl_c):
    def fn(accs, extra, outs):
        qkv_ref, lr_ref = accs
        kc_ref, vc_ref, lc_ref = extra
        for r0 in range(0, D, ROWS_RP):
            rows = pl.ds(r0, ROWS_RP)
            qkv = qkv_ref[rows, :] + jnp.concatenate([jnp.zeros((ROWS_RP, KW), f32), kc_ref[rows, :], vc_ref[rows, :]], axis=1)
            lr = lr_ref[rows, :] + lc_ref[rows, :]
            _pack_blocks(outs[0].at[:, rows], ((0, 1024, qkv), (1024, LATE_END, lr)), range(LATE_DESTS), bf16)

    return _Pack([dwk_c, dwv_c, dwl_c], [jax.ShapeDtypeStruct((LATE_DESTS, D, SHARD), bf16)], fn)


def _mod_tail(dnorm_parts, dmod_parts, dmodc, cs, wm):
    ncol = wm.shape[2]
    n_dn, n_dm = len(dnorm_parts), len(dmod_parts)

    def body(*refs):
        dn_parts, refs = refs[:n_dn], refs[n_dn:]
        dm_parts, refs = refs[:n_dm], refs[n_dm:]
        dmodc_ref, cs_ref, wm_ref, g_wm, g_ng, g_bm, g_cc = refs[:7]
        dn_ref, dmod_ref, a_dn, a_dmod, a_dmodc, dm_rows, gc_ref, a_gc = refs[7:15]
        s1, r1, l1, s2, r2, l2 = refs[15:21]
        dn = dn_parts[0][...]
        for ref in dn_parts[1:]:
            dn = dn + ref[...]
        dn_ref[...] = dn
        dmod_ref[...] = jnp.concatenate([ref[...] for ref in dm_parts], axis=1)
        start, finish = _fanout([lambda j: dn_ref, lambda j: dmod_ref, lambda j: dmodc_ref],
                                [lambda j, r=r: r.at[j] for r in (a_dn, a_dmod, a_dmodc)], s1, r1, l1)
        start()
        finish()

        def total(ref):
            t = ref[0]
            for j in range(1, NDEV):
                t = t + ref[j]
            return t

        g_ng[...] = total(a_dn)
        dmodc_tot = jnp.concatenate([total(a_dmodc), jnp.zeros((1, D), f32)], axis=1)
        g_bm[...] = total(a_dmod) + dmodc_tot
        dm_rows[...] = jnp.concatenate([a_dmod[j] for j in range(NDEV)] + [dmodc_tot, jnp.zeros((16 - NDEV - 1, 3 * D), f32)], axis=0)
        dm = dm_rows[:, pl.ds(pl.multiple_of(_me() * ncol, 128), ncol)]
        s, ds = _silu_and_grad(cs_ref[...])
        part = lax.dot_general(dm[8:9, :], wm_ref[0], (((1,), (1,)), ((), ())), preferred_element_type=f32,
                               precision=lax.Precision.HIGHEST)
        gc_ref[...] = part * ds[8:9, :]
        start2, finish2 = _fanout([lambda j: gc_ref], [lambda j: a_gc.at[j]], s2, r2, l2)
        start2()
        g = lax.dot_general(s, dm, (((0,), (0,)), ((), ())), preferred_element_type=f32, precision=lax.Precision.HIGHEST)
        g_wm[...] = g[None]
        finish2()
        g_cc[...] = total(a_gc)

    sds = jax.ShapeDtypeStruct
    row = lambda n: pltpu.VMEM((NDEV, 1, n), f32)
    return pl.pallas_call(
        body, name="mod_tail", out_shape=(sds(wm.shape, f32), sds((1, D), f32), sds((1, 3 * D), f32), sds((1, D), f32)),
        scratch_shapes=[pltpu.VMEM((1, D), f32), pltpu.VMEM((1, 3 * D), f32), row(D), row(3 * D), row(2 * D),
                        pltpu.VMEM((16, 3 * D), f32), pltpu.VMEM((1, D), f32), row(D)] + _fanout_sems(3) + _fanout_sems(1),
        compiler_params=_cp(),
    )(*dnorm_parts, *dmod_parts, dmodc, cs, wm)


def _adam_update(g, w_ref, m_ref, v_ref, go_ref, d_ref, mo_ref, vo_ref):
    c1 = 1.0 / (1.0 - ADAM_B1 ** ADAM_STEP)
    c2 = 1.0 / (1.0 - ADAM_B2 ** ADAM_STEP)
    mn = ADAM_B1 * m_ref[...] + (1.0 - ADAM_B1) * g
    vn = ADAM_B2 * v_ref[...] + (1.0 - ADAM_B2) * (g * g)
    go_ref[...] = g
    mo_ref[...] = mn
    vo_ref[...] = vn
    d_ref[...] = -ADAM_LR * ((mn * c1) / (jnp.sqrt(vn * c2) + ADAM_EPS) + ADAM_WD * w_ref[...])


def _adam_w_in(parts_a, parts_b, parts_late, w, m, v):
    na = EARLY_A_ROWS // ROWS_RP
    n = D // ROWS_RP
    whole = SHARD // 128 * 128

    def body(a_ref, b_ref, l_ref, w_hbm, m_hbm, v_hbm, go_hbm, d_hbm, mo_hbm, vo_hbm, ibuf, obuf, isem, osem):
        step = pl.program_id(0)

        def cols(t):
            return pl.ds(pl.multiple_of(t * ROWS_RP, ROWS_RP), ROWS_RP)

        def fetch(t):
            return [pltpu.make_async_copy(r.at[:, 0, cols(t)], ibuf.at[t % 2, k], isem.at[t % 2, k])
                    for k, r in enumerate((w_hbm, m_hbm, v_hbm))]

        def flush(t):
            return [pltpu.make_async_copy(obuf.at[t % 2, k], r.at[:, 0, cols(t)], osem.at[t % 2, k])
                    for k, r in enumerate((go_hbm, d_hbm, mo_hbm, vo_hbm))]

        def start(cps):
            for cp in cps:
                cp.start()

        def wait(cps):
            for cp in cps:
                cp.wait()

        pl.when(step == 0)(lambda: start(fetch(step)))
        pl.when(step + 1 < n)(lambda: start(fetch(step + 1)))
        me = _me()
        first = step < na
        early = jnp.where(first, a_ref[0], b_ref[0]).astype(f32)
        for i in range(1, NDEV):
            early = early + jnp.where(first, a_ref[i], b_ref[i]).astype(f32)
        late = l_ref[0].astype(f32)
        for i in range(1, 4):
            late = late + l_ref[i].astype(f32)
        g = jnp.where(me >= LATE_DESTS - 1, early, 0.0) + jnp.where(me < LATE_DESTS, late, 0.0)
        gt = jnp.concatenate([g[:, c:c + 128].T for c in range(0, whole, 128)] + [g[:, SHARD - 128:].T[128 - (SHARD - whole):]],
                             axis=0)
        wait(fetch(step))
        pl.when(step >= 2)(lambda: wait(flush(step - 2)))
        slot = step % 2
        _adam_update(gt, *(ibuf.at[slot, k] for k in range(3)), *(obuf.at[slot, k] for k in range(4)))
        start(flush(step))

        @pl.when(step == n - 1)
        def _():
            wait(flush(step - 1))
            wait(flush(step))

    hbm = pl.BlockSpec(memory_space=pl.ANY)
    res = pl.pallas_call(
        body, name="adam_w_in", grid=(n,), out_shape=tuple(jax.ShapeDtypeStruct((SHARD, 1, D), f32) for _ in range(4)),
        in_specs=[pl.BlockSpec((NDEV, ROWS_RP, SHARD), lambda i: (0, jnp.minimum(i, na - 1), 0)),
                  pl.BlockSpec((NDEV, ROWS_RP, SHARD), lambda i: (0, jnp.maximum(i - na, 0), 0)),
                  pl.BlockSpec((4, ROWS_RP, SHARD), lambda i: (0, i, 0)), hbm, hbm, hbm],
        out_specs=(hbm, hbm, hbm, hbm),
        scratch_shapes=[pltpu.VMEM((2, 3, SHARD, ROWS_RP), f32), pltpu.VMEM((2, 4, SHARD, ROWS_RP), f32),
                        pltpu.SemaphoreType.DMA((2, 3)), pltpu.SemaphoreType.DMA((2, 4))],
        compiler_params=_cp(("arbitrary",)),
    )(parts_a, parts_b, parts_late, *(jnp.transpose(t, (2, 0, 1)) for t in (w, m, v)))
    return tuple(jnp.transpose(t, (1, 2, 0)) for t in res)


def _adam_params(parts, ws, ms, vs, name):
    p = parts.shape[0]
    k = len(ws)
    nrows = [w.size // 128 for w in ws]
    starts = [sum(nrows[:i]) for i in range(k)]

    def shaped(g, shape):
        if len(shape) == 2:
            return jnp.concatenate([g[r:r + 1] for r in range(g.shape[0])], axis=1)
        return g.reshape(shape)

    def body(*refs):
        g_ref = refs[0]
        w_refs, m_refs, v_refs = refs[1:1 + k], refs[1 + k:1 + 2 * k], refs[1 + 2 * k:1 + 3 * k]
        outs = refs[1 + 3 * k:]
        for i in range(k):
            rows = slice(starts[i], starts[i] + nrows[i])
            g = g_ref[0, rows, :]
            for j in range(1, p):
                g = g + g_ref[j, rows, :]
            _adam_update(shaped(g, ws[i].shape), w_refs[i], m_refs[i], v_refs[i], outs[i], outs[k + i], outs[2 * k + i], outs[3 * k + i])

    vm = pl.BlockSpec(memory_space=pltpu.VMEM)
    res = pl.pallas_call(
        body, name=name, out_shape=tuple(jax.ShapeDtypeStruct(w.shape, f32) for _ in range(4) for w in ws),
        in_specs=[vm] * (1 + 3 * k), out_specs=(vm,) * (4 * k), compiler_params=_cp(),
    )(parts, *ws, *ms, *vs)
    return [res[i * k:(i + 1) * k] for i in range(4)]


def _adam_blocks(parts, ws, ms, vs, name):
    k = len(ws)

    def body(*refs):
        g_refs, w_refs, m_refs, v_refs = refs[0:k], refs[k:2 * k], refs[2 * k:3 * k], refs[3 * k:4 * k]
        outs = refs[4 * k:]
        for i in range(k):
            g = g_refs[i][0].astype(f32)
            for j in range(1, parts[i].shape[0]):
                g = g + g_refs[i][j].astype(f32)
            _adam_update(g[None], w_refs[i], m_refs[i], v_refs[i], outs[i], outs[k + i], outs[2 * k + i], outs[3 * k + i])

    vm = pl.BlockSpec(memory_space=pltpu.VMEM)
    res = pl.pallas_call(
        body, name=name, out_shape=tuple(jax.ShapeDtypeStruct(w.shape, f32) for _ in range(4) for w in ws),
        in_specs=[vm] * (4 * k), out_specs=(vm,) * (4 * k), compiler_params=_cp(),
    )(*parts, *ws, *ms, *vs)
    return [res[i * k:(i + 1) * k] for i in range(4)]


def _adam_plain(gs, ws, ms, vs, name):
    k = len(ws)

    def body(*refs):
        g_refs, w_refs, m_refs, v_refs = refs[0:k], refs[k:2 * k], refs[2 * k:3 * k], refs[3 * k:4 * k]
        outs = refs[4 * k:]
        for i in range(k):
            _adam_update(g_refs[i][...], w_refs[i], m_refs[i], v_refs[i], outs[i], outs[k + i], outs[2 * k + i], outs[3 * k + i])

    vm = pl.BlockSpec(memory_space=pltpu.VMEM)
    res = pl.pallas_call(
        body, name=name, out_shape=tuple(jax.ShapeDtypeStruct(w.shape, f32) for _ in range(4) for w in ws),
        in_specs=[vm] * (4 * k), out_specs=(vm,) * (4 * k), compiler_params=_cp(),
    )(*gs, *ws, *ms, *vs)
    return [res[i * k:(i + 1) * k] for i in range(4)]


def _adam_gate(parts, w2, gb, m_w2, m_gb, v_w2, v_gb):
    per = 128 // 32

    def body(p_ref, w2_ref, gb_ref, mw_ref, mb_ref, vw_ref, vb_ref, *outs):
        g = p_ref[0]
        for j in range(1, parts.shape[0]):
            g = g + p_ref[j]
        for d in range(2):
            gd = jnp.concatenate([g[(d * RANK + r) // per:(d * RANK + r) // per + 1, (r % per) * 32:(r % per + 1) * 32]
                                  for r in range(RANK)], axis=0)
            _adam_update(gd, *(t.at[0, d] for t in (w2_ref, mw_ref, vw_ref) + outs[0:4]))
        last = 2 * RANK // per
        gbv = jnp.concatenate([g[last:last + 1, d * 32:(d + 1) * 32] for d in range(2)], axis=0)
        _adam_update(gbv, *(t.at[0] for t in (gb_ref, mb_ref, vb_ref) + outs[4:8]))

    vm = pl.BlockSpec(memory_space=pltpu.VMEM)
    return pl.pallas_call(
        body, name="adam_gate", out_shape=tuple(jax.ShapeDtypeStruct(w.shape, f32) for w in (w2, gb) for _ in range(4)),
        in_specs=[vm] * 7, out_specs=(vm,) * 8, compiler_params=_cp(),
    )(parts, w2, gb, m_w2, m_gb, v_w2, v_gb)


def _in_proj(x, g, scale, shift, w_pad, tl, comm):
    l = x.shape[0]
    c_in, c_out, c_sems = comm.specs()

    def body(*refs):
        x_ref, g_ref, sc_ref, sh_ref, w_ref = refs[:5]
        cin = refs[5:5 + len(c_in)]
        p_ref, h_ref = refs[5 + len(c_in):7 + len(c_in)]
        cout = refs[7 + len(c_in):7 + len(c_in) + comm.n]
        sems = refs[7 + len(c_in) + comm.n:]

        def compute():
            xv = x_ref[...]
            r = lax.rsqrt(jnp.mean(xv * xv, axis=-1, keepdims=True) + EPS)
            h = (xv * r) * (g_ref[...] * (1.0 + sc_ref[...])) + sh_ref[...]
            hb = h.astype(bf16)
            h_ref[...] = hb
            p_ref[...] = jnp.dot(hb, w_ref[...], preferred_element_type=f32).astype(bf16)

        comm.run(cin, cout, sems, l // tl, compute)

    vec = pl.BlockSpec((1, D), lambda i: (0, 0))
    return pl.pallas_call(
        body, name="in_proj", grid=(l // tl,),
        out_shape=(jax.ShapeDtypeStruct((l, PW), bf16), jax.ShapeDtypeStruct((l, D), bf16)) + tuple(comm.outs),
        in_specs=[pl.BlockSpec((tl, D), lambda i: (i, 0)), vec, vec, vec, pl.BlockSpec((D, PW), lambda i: (0, 0))] + c_in,
        out_specs=(pl.BlockSpec((tl, PW), lambda i: (i, 0)), pl.BlockSpec((tl, D), lambda i: (i, 0))) + tuple(c_out),
        scratch_shapes=c_sems, compiler_params=_cp(("arbitrary",)),
    )(x, g, scale, shift, w_pad, *comm.ins)


def _tri(rev):
    i = lax.broadcasted_iota(jnp.int32, (CH, CH), 0)
    j = lax.broadcasted_iota(jnp.int32, (CH, CH), 1)
    return jnp.where((j >= i) if rev else (j <= i), 1.0, 0.0).astype(f32)


def _head_masks():
    lane = lax.broadcasted_iota(jnp.int32, (1, KW), 1) // HK
    return [jnp.where(lane == h, 1.0, 0.0).astype(f32) for h in range(NH)]


def _block_diag():
    r = lax.broadcasted_iota(jnp.int32, (VW, KW), 0) // HV
    c = lax.broadcasted_iota(jnp.int32, (VW, KW), 1) // HK
    return jnp.where(r == c, 1.0, 0.0).astype(f32)


def _decay(lr, w2, gb, tri, rev):
    logits = _dot(lr, w2) + gb
    a = _log_sigmoid(logits) * (1.0 / TAU)
    c = _dot_hi(tri, a)
    cl = c[0:1, :] if rev else c[CH - 1:CH, :]
    return logits, c, cl


def _state_fwd(k, v, c, cl, st, bd):
    return st * jnp.exp(cl) + bd * _tn(v, k * jnp.exp(cl - c))


def _state_bwd(k, v, c, cl, st0, dst, trit):
    ecl = jnp.exp(cl)
    edec = jnp.exp(cl - c)
    kdec = k * edec
    dv = _nt(kdec, dst)
    dkdec = _dot(v, dst)
    dcl = jnp.sum(dst * st0, axis=0, keepdims=True) * ecl + jnp.sum(dkdec * kdec, axis=0, keepdims=True)
    da = _dot_hi(trit, -dkdec * kdec) + dcl
    return dkdec * edec, dv, da, dst * ecl


def _bdot(a, b):
    return lax.dot_general(a.astype(bf16), b.astype(bf16), (((2,), (1,)), ((0,), (0,))), preferred_element_type=f32)


def _bnt(a, b):
    return lax.dot_general(a.astype(bf16), b.astype(bf16), (((2,), (2,)), ((0,), (0,))), preferred_element_type=f32)


def _btn(a, b):
    return lax.dot_general(a.astype(bf16), b.astype(bf16), (((1,), (1,)), ((0,), (0,))), preferred_element_type=f32)


def _scan_chunks(x, rev):
    nc = x.shape[0]
    hi = x.astype(bf16)
    r1 = x - hi.astype(f32)
    mid = r1.astype(bf16)
    lo = (r1 - mid.astype(f32)).astype(bf16)
    terms = jnp.concatenate([hi, mid, lo], axis=1)
    tri3 = jnp.broadcast_to(jnp.concatenate([_tri(rev)] * 3, axis=1).astype(bf16)[None], (nc, CH, 3 * CH))
    return lax.dot_general(tri3, terms, (((2,), (1,)), ((0,), (0,))), preferred_element_type=f32)


class _Tile:
    pass


def _tile_prep(q_ref, k_ref, lr_ref, w2_ref, gb_ref, rev, nc):
    t = _Tile()
    tg = nc * CH
    t.logits = _dot(lr_ref[...], w2_ref[...]) + gb_ref[...]
    c = _scan_chunks((_log_sigmoid(t.logits) * (1.0 / TAU)).reshape(nc, CH, KW), rev)
    cl = c[:, 0:1, :] if rev else c[:, CH - 1:CH, :]
    k = k_ref[...].astype(f32).reshape(nc, CH, KW)
    t.ec, t.enc, t.edec, t.ecl = jnp.exp(c), jnp.exp(-c), jnp.exp(cl - c), jnp.exp(cl)
    t.qd = q_ref[...].astype(f32).reshape(nc, CH, KW) * t.ec * QSCALE
    t.kd = k * t.enc
    t.kdec = k * t.edec
    hm = _head_masks()
    t.tri4 = jnp.concatenate([_tri(rev)] * NH, axis=0)[None]
    t.qs = jnp.concatenate([t.qd * hm[h] for h in range(NH)], axis=1)
    t.pst = _bnt(t.qs, t.kd) * t.tri4
    return t


def _gla_fwd(p, w2p, gb, s0, rev, tg, name):
    l = p.shape[0]
    nb, nc = l // tg, tg // CH

    def body(q_ref, k_ref, v_ref, lr_ref, w2_ref, gb_ref, s0_ref, o_ref, st_ref, st):
        @pl.when(pl.program_id(0) == 0)
        def _():
            st[...] = s0_ref[...]

        t = _tile_prep(q_ref, k_ref, lr_ref, w2_ref, gb_ref, rev, nc)
        v = v_ref[...].reshape(nc, CH, VW)
        intra = jnp.concatenate([_bdot(t.pst[:, h * CH:(h + 1) * CH], v[:, :, h * HV:(h + 1) * HV]) for h in range(NH)], axis=2)
        bd = _block_diag()
        s = st[...]
        for n in (range(nc - 1, -1, -1) if rev else range(nc)):
            st_ref[n] = s.astype(bf16)
            s = s * t.ecl[n] + bd * _tn(v[n], t.kdec[n])
        st[...] = s
        o_ref[...] = (_bnt(t.qd, st_ref[...]) + intra).reshape(tg, VW).astype(bf16)

    blk = (lambda i: nb - 1 - i) if rev else (lambda i: i)
    return pl.pallas_call(
        body, name=name, grid=(nb,),
        out_shape=(jax.ShapeDtypeStruct((l, VW), bf16), jax.ShapeDtypeStruct((l // CH, VW, KW), bf16)),
        in_specs=[pl.BlockSpec((tg, KW), lambda i: (blk(i), PQ // KW)), pl.BlockSpec((tg, KW), lambda i: (blk(i), PK // KW)),
                  pl.BlockSpec((tg, VW), lambda i: (blk(i), PV // VW)), pl.BlockSpec((tg, LRW), lambda i: (blk(i), PLR // LRW)),
                  pl.BlockSpec((LRW, KW), lambda i: (0, 0)), pl.BlockSpec((1, KW), lambda i: (0, 0)),
                  pl.BlockSpec((VW, KW), lambda i: (0, 0))],
        out_specs=(pl.BlockSpec((tg, VW), lambda i: (blk(i), 0)), pl.BlockSpec((nc, VW, KW), lambda i: (blk(i), 0, 0))),
        scratch_shapes=[pltpu.VMEM((VW, KW), f32)],
        compiler_params=_cp(("arbitrary",)),
    )(p, p, p, p, w2p, gb, s0)


def _gla_bwd(p, do, states, w2p, gb, prev, rev, tg, name, comm):
    l = p.shape[0]
    nb, nc = l // tg, tg // CH
    out_dt = bf16
    c_in, c_out, c_sems = comm.specs()

    def body(*refs):
        q_ref, k_ref, v_ref, lr_ref, do_ref, st_ref, w2_ref, gb_ref = refs[:8]
        refs = refs[8:]
        if prev is not None:
            pq_ref, pl_ref = refs[:2]
            refs = refs[2:]
        cin, refs = refs[:len(c_in)], refs[len(c_in):]
        dqkv_ref, dlr_ref, dw2_ref, dgb_ref, ds0_ref = refs[:5]
        cout, dst, ds_buf, sems = refs[5:5 + comm.n], refs[5 + comm.n], refs[6 + comm.n], refs[7 + comm.n:]
        comm.run(cin, cout, sems, nb, lambda: compute(q_ref, k_ref, v_ref, lr_ref, do_ref, st_ref, w2_ref, gb_ref,
                                                      pq_ref if prev is not None else None, pl_ref if prev is not None else None,
                                                      dqkv_ref, dlr_ref, dw2_ref, dgb_ref, ds0_ref, dst, ds_buf))

    def compute(q_ref, k_ref, v_ref, lr_ref, do_ref, st_ref, w2_ref, gb_ref, pq_ref, pl_ref,
                dqkv_ref, dlr_ref, dw2_ref, dgb_ref, ds0_ref, dst, ds_buf):
        @pl.when(pl.program_id(0) == 0)
        def _():
            dst[...] = jnp.zeros_like(dst)
            dw2_ref[...] = jnp.zeros_like(dw2_ref)
            dgb_ref[...] = jnp.zeros_like(dgb_ref)

        t = _tile_prep(q_ref, k_ref, lr_ref, w2_ref, gb_ref, rev, nc)
        hm = _head_masks()
        v = v_ref[...].reshape(nc, CH, VW)
        do = do_ref[...].reshape(nc, CH, VW)
        heads = lambda a, h: a[:, :, h * HV:(h + 1) * HV]
        dpst = jnp.concatenate([_bnt(heads(do, h), heads(v, h)) for h in range(NH)], axis=1) * t.tri4
        dv = jnp.concatenate([_btn(t.pst[:, h * CH:(h + 1) * CH], heads(do, h)) for h in range(NH)], axis=2)
        dqd = _bdot(do, st_ref[...])
        for h in range(NH):
            dqd = dqd + hm[h] * _bdot(dpst[:, h * CH:(h + 1) * CH], t.kd)
        dkd = _btn(dpst, t.qs)
        bd = _block_diag()
        d = dst[...]
        for n in (range(nc) if rev else range(nc - 1, -1, -1)):
            ds_buf[n] = d
            d = d * t.ecl[n] + bd * _tn(do[n], t.qd[n])
        dst[...] = d
        ds0_ref[...] = d
        ds = ds_buf[...]
        dv = dv + _bnt(t.kdec, ds)
        dkdec = _bdot(v, ds)
        dcl = jnp.sum(ds * st_ref[...].astype(f32), axis=1, keepdims=True) * t.ecl + jnp.sum(dkdec * t.kdec, axis=1, keepdims=True)
        dc = dqd * t.qd - dkd * t.kd - dkdec * t.kdec
        da = _scan_chunks(dc, not rev) + dcl
        dq = dqd * t.ec * QSCALE
        dk = dkd * t.enc + dkdec * t.edec
        dlog = da.reshape(tg, KW) * _sigmoid(-t.logits) * (1.0 / TAU)
        dlr = _nt(dlog, w2_ref[...])
        dw2_ref[...] += _tn(lr_ref[...], dlog)
        dgb_ref[...] += jnp.sum(dlog, axis=0, keepdims=True)
        dqkv = jnp.concatenate([dq, dk, dv], axis=2).reshape(tg, 2 * KW + VW)
        if prev is not None:
            dqkv = dqkv + pq_ref[...]
            dlr = dlr + pl_ref[...]
        dqkv_ref[...] = dqkv.astype(out_dt)
        dlr_ref[...] = dlr.astype(out_dt)

    blk = (lambda i: i) if rev else (lambda i: nb - 1 - i)
    in_specs = [pl.BlockSpec((tg, KW), lambda i: (blk(i), PQ // KW)), pl.BlockSpec((tg, KW), lambda i: (blk(i), PK // KW)),
                pl.BlockSpec((tg, VW), lambda i: (blk(i), PV // VW)), pl.BlockSpec((tg, LRW), lambda i: (blk(i), PLR // LRW)),
                pl.BlockSpec((tg, VW), lambda i: (blk(i), 0)), pl.BlockSpec((nc, VW, KW), lambda i: (blk(i), 0, 0)),
                pl.BlockSpec((LRW, KW), lambda i: (0, 0)), pl.BlockSpec((1, KW), lambda i: (0, 0))]
    args = [p, p, p, p, do, states, w2p, gb]
    if prev is not None:
        in_specs += [pl.BlockSpec((tg, 2 * KW + VW), lambda i: (blk(i), 0)), pl.BlockSpec((tg, LRW), lambda i: (blk(i), 0))]
        args += list(prev)
    return pl.pallas_call(
        body, name=name, grid=(nb,),
        out_shape=(jax.ShapeDtypeStruct((l, 2 * KW + VW), out_dt), jax.ShapeDtypeStruct((l, LRW), out_dt),
                   jax.ShapeDtypeStruct((LRW, KW), f32), jax.ShapeDtypeStruct((1, KW), f32), jax.ShapeDtypeStruct((VW, KW), f32))
        + tuple(comm.outs),
        in_specs=in_specs + c_in,
        out_specs=(pl.BlockSpec((tg, 2 * KW + VW), lambda i: (blk(i), 0)), pl.BlockSpec((tg, LRW), lambda i: (blk(i), 0)),
                   pl.BlockSpec((LRW, KW), lambda i: (0, 0)), pl.BlockSpec((1, KW), lambda i: (0, 0)),
                   pl.BlockSpec((VW, KW), lambda i: (0, 0))) + tuple(c_out),
        scratch_shapes=[pltpu.VMEM((VW, KW), f32), pltpu.VMEM((nc, VW, KW), f32)] + c_sems,
        compiler_params=_cp(("arbitrary",)),
    )(*args, *comm.ins)


def _ctx_hidden(ctx_ref, g_ref, sc_ref, sh_ref):
    xv = ctx_ref[...]
    r = lax.rsqrt(jnp.mean(xv * xv, axis=-1, keepdims=True) + EPS)
    xn = xv * r
    return xn, xn * (g_ref[...] * (1.0 + sc_ref[...])) + sh_ref[...]


_CTX_W_SPECS = [pl.BlockSpec((D, KW), lambda i: (0, PK // KW)), pl.BlockSpec((D, VW), lambda i: (0, PV // VW)),
                pl.BlockSpec((D, LRW), lambda i: (0, PLR // LRW))]


def _ctx_fwd(ctx, g, scale, shift, w_pad, w2f, w2b, gbf, gbb):
    ncc = CTX // CH

    def body(ctx_ref, g_ref, sc_ref, sh_ref, wk_ref, wv_ref, wl_ref, w2f_ref, w2b_ref, gbf_ref, gbb_ref, sf_ref, sb_ref):
        _, hc = _ctx_hidden(ctx_ref, g_ref, sc_ref, sh_ref)
        k, v, lr = _dot(hc, wk_ref[...]), _dot(hc, wv_ref[...]), _dot(hc, wl_ref[...])
        bd = _block_diag()
        for rev, w2_ref, gb_ref, out in ((False, w2f_ref, gbf_ref, sf_ref), (True, w2b_ref, gbb_ref, sb_ref)):
            tri = _tri(rev)
            st = jnp.zeros((VW, KW), f32)
            for j in (range(ncc - 1, -1, -1) if rev else range(ncc)):
                rows = slice(j * CH, (j + 1) * CH)
                _, c, cl = _decay(lr[rows], w2_ref[...], gb_ref[...], tri, rev)
                st = _state_fwd(k[rows], v[rows], c, cl, st, bd)
            out[...] = st

    vec = pl.BlockSpec((1, D), lambda i: (0, 0))
    w2s = pl.BlockSpec((LRW, KW), lambda i: (0, 0))
    gbs = pl.BlockSpec((1, KW), lambda i: (0, 0))
    sts = pl.BlockSpec((VW, KW), lambda i: (0, 0))
    return pl.pallas_call(
        body, name="ctx_fwd", grid=(1,), out_shape=(jax.ShapeDtypeStruct((VW, KW), f32),) * 2,
        in_specs=[pl.BlockSpec((CTX, D), lambda i: (0, 0)), vec, vec, vec] + _CTX_W_SPECS + [w2s, w2s, gbs, gbs],
        out_specs=(sts, sts), compiler_params=_cp(("arbitrary",)),
    )(ctx, g, scale, shift, w_pad, w_pad, w_pad, w2f, w2b, gbf, gbb)


def _ctx_bwd(ctx, g, scale, shift, w_pad, w2f, w2b, gbf, gbb, dsf, dsb):
    ncc = CTX // CH

    def body(ctx_ref, g_ref, sc_ref, sh_ref, wk_ref, wv_ref, wl_ref, w2f_ref, w2b_ref, gbf_ref, gbb_ref, dsf_ref, dsb_ref,
             dwk_ref, dwv_ref, dwl_ref, dmod_ref, dg_ref, dw2_ref, dgb_ref):
        xn, hc = _ctx_hidden(ctx_ref, g_ref, sc_ref, sh_ref)
        k, v, lr = _dot(hc, wk_ref[...]), _dot(hc, wv_ref[...]), _dot(hc, wl_ref[...])
        bd = _block_diag()
        dk_rows, dv_rows, dl_rows = [None] * ncc, [None] * ncc, [None] * ncc
        for d, (rev, w2_ref, gb_ref, ds_ref) in enumerate(((False, w2f_ref, gbf_ref, dsf_ref), (True, w2b_ref, gbb_ref, dsb_ref))):
            tri = _tri(rev)
            order = list(range(ncc - 1, -1, -1) if rev else range(ncc))
            st, saved = jnp.zeros((VW, KW), f32), {}
            for j in order:
                rows = slice(j * CH, (j + 1) * CH)
                logits, c, cl = _decay(lr[rows], w2_ref[...], gb_ref[...], tri, rev)
                saved[j] = (logits, c, cl, st)
                st = _state_fwd(k[rows], v[rows], c, cl, st, bd)
            dst = ds_ref[...]
            dw2 = jnp.zeros((LRW, KW), f32)
            dgb = jnp.zeros((1, KW), f32)
            for j in reversed(order):
                rows = slice(j * CH, (j + 1) * CH)
                logits, c, cl, st0 = saved[j]
                dk, dv, da, dst = _state_bwd(k[rows], v[rows], c, cl, st0, dst, _tri(not rev))
                dlog = da * _sigmoid(-logits) * (1.0 / TAU)
                dl = _nt(dlog, w2_ref[...])
                dw2 = dw2 + _tn(lr[rows], dlog)
                dgb = dgb + jnp.sum(dlog, axis=0, keepdims=True)
                dk_rows[j] = dk if dk_rows[j] is None else dk_rows[j] + dk
                dv_rows[j] = dv if dv_rows[j] is None else dv_rows[j] + dv
                dl_rows[j] = dl if dl_rows[j] is None else dl_rows[j] + dl
            dw2_ref[d] = dw2
            dgb_ref[d] = dgb
        dk, dv, dl = (jnp.concatenate(t, axis=0) for t in (dk_rows, dv_rows, dl_rows))
        dwk_ref[...] = _tn(hc, dk)
        dwv_ref[...] = _tn(hc, dv)
        dwl_ref[...] = _tn(hc, dl)
        dh = _nt(dk, wk_ref[...]) + _nt(dv, wv_ref[...]) + _nt(dl, wl_ref[...])
        gx = dh * xn
        dmod_ref[:, 0:D] = jnp.sum(dh, axis=0, keepdims=True)
        dmod_ref[:, D:2 * D] = jnp.sum(gx, axis=0, keepdims=True) * g_ref[...]
        dg_ref[...] = jnp.sum(gx, axis=0, keepdims=True) * (1.0 + sc_ref[...])

    vec = pl.BlockSpec((1, D), lambda i: (0, 0))
    w2s = pl.BlockSpec((LRW, KW), lambda i: (0, 0))
    gbs = pl.BlockSpec((1, KW), lambda i: (0, 0))
    sts = pl.BlockSpec((VW, KW), lambda i: (0, 0))
    full = lambda *s: pl.BlockSpec(s, lambda i: (0,) * len(s))
    return pl.pallas_call(
        body, name="ctx_bwd", grid=(1,),
        out_shape=(jax.ShapeDtypeStruct((D, KW), f32), jax.ShapeDtypeStruct((D, VW), f32), jax.ShapeDtypeStruct((D, LRW), f32),
                   jax.ShapeDtypeStruct((1, 2 * D), f32), jax.ShapeDtypeStruct((1, D), f32),
                   jax.ShapeDtypeStruct((2, LRW, KW), f32), jax.ShapeDtypeStruct((2, 1, KW), f32)),
        in_specs=[pl.BlockSpec((CTX, D), lambda i: (0, 0)), vec, vec, vec] + _CTX_W_SPECS + [w2s, w2s, gbs, gbs, sts, sts],
        out_specs=(full(D, KW), full(D, VW), full(D, LRW), full(1, 2 * D), full(1, D), full(2, LRW, KW), full(2, 1, KW)),
        compiler_params=_cp(("arbitrary",)),
    )(ctx, g, scale, shift, w_pad, w_pad, w_pad, w2f, w2b, gbf, gbb, dsf, dsb)


def _layernorm(va, g, b):
    mu = jnp.mean(va, axis=-1, keepdims=True)
    xc = va - mu
    rstd = lax.rsqrt(jnp.mean(xc * xc, axis=-1, keepdims=True) + EPS)
    vhat = xc * rstd
    return vhat, rstd, vhat * g + b


MIX_PIECES = 8


def _mix_fwd(p, ln_g, ln_b, ws, bs_t):
    l = p.shape[0]
    half = AW // 2
    rp = l // MIX_PIECES
    cpp = rp // ACH

    def body(p_ref, g_ref, b_ref, ws_ref, bs_ref, sv_hbm, va_buf, col_buf, sv_buf, in_sems, out_sems):
        piece = lambda i: pl.ds(pl.multiple_of(i * rp, rp), rp)
        load = lambda i: pltpu.make_async_copy(p_ref.at[piece(i), pl.ds(PVA, AW)], va_buf.at[piece(i)], in_sems.at[i])
        store = lambda i: pltpu.make_async_copy(sv_buf.at[piece(i)], sv_hbm.at[piece(i)], out_sems.at[i])
        for i in range(MIX_PIECES):
            load(i).start()

        def rows_piece(i, carry):
            load(i).wait()
            for j in range(cpp):
                rows = pl.ds(pl.multiple_of(i * rp + j * ACH, ACH), ACH)
                _, _, vn = _layernorm(va_buf[rows, :].astype(f32), g_ref[...], b_ref[...])
                for gi in range(2):
                    sl = slice(gi * ACH, (gi + 1) * ACH)
                    sv_buf[rows, sl] = (_dot(ws_ref[gi], vn[:, sl]) + bs_ref[:, gi:gi + 1]).astype(bf16)
                col_buf[0, rows, :] = vn[:, half:half + ACH]
                col_buf[1, rows, :] = vn[:, half + ACH:]
            return carry

        lax.fori_loop(0, MIX_PIECES, rows_piece, 0)

        def cols_step(cidx, carry):
            rows = pl.ds(cidx, ACH, stride=GW)
            for gi in range(2, 4):
                col_buf[gi - 2, rows, :] = _dot(ws_ref[gi], col_buf[gi - 2, rows, :]) + bs_ref[:, gi:gi + 1]
            return carry

        lax.fori_loop(0, GW, cols_step, 0, unroll=8)

        def out_piece(i, carry):
            for j in range(cpp):
                rows = pl.ds(pl.multiple_of(i * rp + j * ACH, ACH), ACH)
                sv_buf[rows, half:half + ACH] = col_buf[0, rows, :].astype(bf16)
                sv_buf[rows, half + ACH:] = col_buf[1, rows, :].astype(bf16)
            store(i).start()
            return carry

        lax.fori_loop(0, MIX_PIECES, out_piece, 0)
        for i in range(MIX_PIECES):
            store(i).wait()

    vm = pl.BlockSpec(memory_space=pltpu.VMEM)
    hbm = pl.BlockSpec(memory_space=pl.ANY)
    return pl.pallas_call(
        body, name="mix_fwd", out_shape=jax.ShapeDtypeStruct((l, AW), bf16),
        in_specs=[hbm, vm, vm, vm, vm], out_specs=hbm,
        scratch_shapes=[pltpu.VMEM((l, AW), bf16), pltpu.VMEM((2, l, ACH), f32), pltpu.VMEM((l, AW), bf16),
                        pltpu.SemaphoreType.DMA((MIX_PIECES,)), pltpu.SemaphoreType.DMA((MIX_PIECES,))],
        compiler_params=_cp(),
    )(p, ln_g, ln_b, ws, bs_t)


def _mix_bwd(p, dsv, ln_g, ln_b, ws_t):
    l = p.shape[0]
    half = AW // 2
    rp = l // MIX_PIECES
    cpp = rp // ACH

    def body(p_ref, dsv_hbm, g_ref, b_ref, wst_ref, dva_hbm, dws_ref, dbs_ref, dg_ref, db_ref,
             va_buf, dsv_buf, vn_col, ds_col, dva_buf, va_sems, ds_sems, out_sems):
        piece = lambda i: pl.ds(pl.multiple_of(i * rp, rp), rp)
        load_va = lambda i: pltpu.make_async_copy(p_ref.at[piece(i), pl.ds(PVA, AW)], va_buf.at[piece(i)], va_sems.at[i])
        load_ds = lambda i: pltpu.make_async_copy(dsv_hbm.at[piece(i)], dsv_buf.at[piece(i)], ds_sems.at[i])
        store = lambda i: pltpu.make_async_copy(dva_buf.at[piece(i)], dva_hbm.at[piece(i)], out_sems.at[i])
        for i in range(MIX_PIECES):
            load_va(i).start()
            load_ds(i).start()
        dws_ref[...] = jnp.zeros_like(dws_ref)
        dbs_ref[...] = jnp.zeros_like(dbs_ref)
        dg_ref[...] = jnp.zeros_like(dg_ref)
        db_ref[...] = jnp.zeros_like(db_ref)

        def rows_piece(i, carry):
            load_va(i).wait()
            load_ds(i).wait()
            for j in range(cpp):
                rows = pl.ds(pl.multiple_of(i * rp + j * ACH, ACH), ACH)
                _, _, vn = _layernorm(va_buf[rows, :].astype(f32), g_ref[...], b_ref[...])
                ds = dsv_buf[rows, :].astype(f32)
                for gi in range(2):
                    sl = slice(gi * ACH, (gi + 1) * ACH)
                    dws_ref[gi] += _nt(ds[:, sl], vn[:, sl])
                    dbs_ref[gi] += ds[:, sl]
                for gi in range(2):
                    sl = slice(half + gi * ACH, half + (gi + 1) * ACH)
                    vn_col[gi, rows, :] = vn[:, sl]
                    ds_col[gi, rows, :] = ds[:, sl]
            return carry

        lax.fori_loop(0, MIX_PIECES, rows_piece, 0)

        def cols_step(cidx, carry):
            rows = pl.ds(cidx, ACH, stride=GW)
            for gi in range(2, 4):
                ds = ds_col[gi - 2, rows, :]
                dws_ref[gi] += _nt(ds, vn_col[gi - 2, rows, :])
                dbs_ref[gi] += ds
                ds_col[gi - 2, rows, :] = _dot(wst_ref[gi], ds)
            return carry

        lax.fori_loop(0, GW, cols_step, 0, unroll=8)

        def out_piece(i, carry):
            for j in range(cpp):
                rows = pl.ds(pl.multiple_of(i * rp + j * ACH, ACH), ACH)
                vhat, rstd, _ = _layernorm(va_buf[rows, :].astype(f32), g_ref[...], b_ref[...])
                ds = dsv_buf[rows, :].astype(f32)
                dvn = jnp.concatenate([_dot(wst_ref[0], ds[:, 0:ACH]), _dot(wst_ref[1], ds[:, ACH:half]),
                                       ds_col[0, rows, :], ds_col[1, rows, :]], axis=1)
                dg_ref[...] += jnp.sum(dvn * vhat, axis=0, keepdims=True)
                db_ref[...] += jnp.sum(dvn, axis=0, keepdims=True)
                dvh = dvn * g_ref[...]
                dva = rstd * (dvh - jnp.mean(dvh, axis=-1, keepdims=True) - vhat * jnp.mean(dvh * vhat, axis=-1, keepdims=True))
                dva_buf[rows, :] = dva.astype(bf16)
            store(i).start()
            return carry

        lax.fori_loop(0, MIX_PIECES, out_piece, 0)
        for i in range(MIX_PIECES):
            store(i).wait()

    vm = pl.BlockSpec(memory_space=pltpu.VMEM)
    hbm = pl.BlockSpec(memory_space=pl.ANY)
    dma = lambda: pltpu.SemaphoreType.DMA((MIX_PIECES,))
    return pl.pallas_call(
        body, name="mix_bwd",
        out_shape=(jax.ShapeDtypeStruct((l, AW), bf16), jax.ShapeDtypeStruct((4, ACH, ACH), f32), jax.ShapeDtypeStruct((4, ACH, ACH), f32),
                   jax.ShapeDtypeStruct((1, AW), f32), jax.ShapeDtypeStruct((1, AW), f32)),
        in_specs=[hbm, hbm, vm, vm, vm], out_specs=(hbm, vm, vm, vm, vm),
        scratch_shapes=[pltpu.VMEM((l, AW), bf16), pltpu.VMEM((l, AW), bf16), pltpu.VMEM((2, l, ACH), f32), pltpu.VMEM((2, l, ACH), f32),
                        pltpu.VMEM((l, AW), bf16), dma(), dma(), dma()],
        compiler_params=_cp(),
    )(p, dsv, ln_g, ln_b, ws_t)


def _mid(x, tgt, p, o_f, o_b, sv, gate, gf, gb_norm, w_pa, w_pb, w_out, tl):
    l = x.shape[0]

    def body(x_ref, t_ref, zb_ref, ua_ref, za_ref, g1_ref, g2_ref, of_ref, ob_ref, sv_ref, gate_ref, gf_ref, gbn_ref,
             wpa_ref, wpb_ref, wout_ref,
             dx1_ref, dzbua_ref, dzag_ref, dsv_ref, do_ref, dwout_bf, dwpa_bf, dwpb_bf, dgf_ref, dgate_ref, dgbn_ref, loss_ref,
             dwout_ref, dwpa_ref, dwpb_ref):
        @pl.when(pl.program_id(0) == 0)
        def _():
            for r in (dwout_ref, dwpa_ref, dwpb_ref, dgf_ref, dgate_ref, dgbn_ref, loss_ref):
                r[...] = jnp.zeros_like(r)

        o = of_ref[...].astype(f32) + ob_ref[...].astype(f32)
        rr = jnp.concatenate(
            [jnp.broadcast_to(lax.rsqrt(jnp.mean(o[:, h * HV:(h + 1) * HV] ** 2, axis=-1, keepdims=True) + EPS), (tl, HV))
             for h in range(NH)], axis=1)
        ohat = o * rr
        on = ohat * gbn_ref[...]
        szb, dszb = _silu_and_grad(zb_ref[...].astype(f32))
        tb = on * szb
        u = ua_ref[...].astype(f32)
        svv = sv_ref[...].astype(f32)
        sza, dsza = _silu_and_grad(za_ref[...].astype(f32))
        ta = u * svv * sza
        ya = _dot(ta, wpa_ref[...])
        yb = _dot(tb, wpb_ref[...])
        g1 = _sigmoid(g1_ref[...].astype(f32))
        g2 = _sigmoid(g2_ref[...].astype(f32))
        m = g1 * ya + g2 * yb
        y2 = _dot(m, wout_ref[...])
        x1 = x_ref[...] + gate_ref[...] * y2
        r1 = lax.rsqrt(jnp.mean(x1 * x1, axis=-1, keepdims=True) + EPS)
        x1n = x1 * r1
        err = x1n * gf_ref[...] - t_ref[...]
        loss_ref[...] += jnp.sum(jnp.sum(err * err, axis=-1, keepdims=True), axis=0, keepdims=True) * (0.5 / D)
        dout = err * (1.0 / D)
        dgf_ref[...] += jnp.sum(dout * x1n, axis=0, keepdims=True)
        dx1n = dout * gf_ref[...]
        dx1 = r1 * (dx1n - x1n * jnp.mean(dx1n * x1n, axis=-1, keepdims=True))
        dx1_ref[...] = dx1
        dgate_ref[...] += jnp.sum(dx1 * y2, axis=0, keepdims=True)
        dy2 = dx1 * gate_ref[...]
        dwout_ref[...] += _tn(m, dy2)
        dm = _nt(dy2, wout_ref[...])
        dya = dm * g1
        dyb = dm * g2
        dzag_ref[:, AW:AW + D] = (dm * ya * g1 * (1.0 - g1)).astype(bf16)
        dzag_ref[:, AW + D:] = (dm * yb * g2 * (1.0 - g2)).astype(bf16)
        dwpa_ref[...] += _tn(ta, dya)
        dta = _nt(dya, wpa_ref[...])
        dzbua_ref[:, AW:] = (dta * svv * sza).astype(bf16)
        dsv_ref[...] = (dta * u * sza).astype(bf16)
        dzag_ref[:, 0:AW] = (dta * u * svv * dsza).astype(bf16)
        dwpb_ref[...] += _tn(tb, dyb)
        dtb = _nt(dyb, wpb_ref[...])
        don = dtb * szb
        dzbua_ref[:, 0:AW] = (dtb * on * dszb).astype(bf16)
        dgbn_ref[...] += jnp.sum(don * ohat, axis=0, keepdims=True)
        doh = don * gbn_ref[...]
        prod = doh * ohat
        mh = jnp.concatenate(
            [jnp.broadcast_to(jnp.mean(prod[:, h * HV:(h + 1) * HV], axis=-1, keepdims=True), (tl, HV)) for h in range(NH)], axis=1)
        do_ref[...] = (rr * (doh - ohat * mh)).astype(bf16)

        @pl.when(pl.program_id(0) == l // tl - 1)
        def _():
            for acc, out in ((dwout_ref, dwout_bf), (dwpa_ref, dwpa_bf), (dwpb_ref, dwpb_bf)):
                out[...] = acc[...].astype(bf16)

    row = lambda w, j: pl.BlockSpec((tl, w), lambda i, j=j: (i, j))
    full = lambda *s: pl.BlockSpec(s, lambda i: (0,) * len(s))
    return pl.pallas_call(
        body, name="mid", grid=(l // tl,),
        out_shape=(jax.ShapeDtypeStruct((l, D), f32), jax.ShapeDtypeStruct((l, 2 * AW), bf16), jax.ShapeDtypeStruct((l, AW + 2 * D), bf16),
                   jax.ShapeDtypeStruct((l, AW), bf16), jax.ShapeDtypeStruct((l, VW), bf16),
                   jax.ShapeDtypeStruct((D, D), bf16), jax.ShapeDtypeStruct((AW, D), bf16), jax.ShapeDtypeStruct((VW, D), bf16),
                   jax.ShapeDtypeStruct((1, D), f32), jax.ShapeDtypeStruct((1, D), f32), jax.ShapeDtypeStruct((1, VW), f32),
                   jax.ShapeDtypeStruct((1, 1), f32)),
        scratch_shapes=[pltpu.VMEM((D, D), f32), pltpu.VMEM((AW, D), f32), pltpu.VMEM((VW, D), f32)],
        in_specs=[row(D, 0), row(D, 0), row(AW, PZB // AW), row(AW, PUA // AW), row(AW, PZA // AW), row(D, PG1 // D), row(D, PG2 // D),
                  row(VW, 0), row(VW, 0), row(AW, 0), full(1, D), full(1, D), full(1, VW), full(AW, D), full(VW, D), full(D, D)],
        out_specs=(row(D, 0), row(2 * AW, 0), row(AW + 2 * D, 0), row(AW, 0), row(VW, 0),
                   full(D, D), full(AW, D), full(VW, D), full(1, D), full(1, D), full(1, VW), full(1, 1)),
        compiler_params=_cp(("arbitrary",)),
    )(x, tgt, p, p, p, p, p, o_f, o_b, sv, gate, gf, gb_norm, w_pa, w_pb, w_out)


def _in_bwd(x, dx1, dqkv, dzbua, dva, dzag, dlr, w_pad, g, scale, tl, comm):
    l = x.shape[0]
    c_in, c_out, c_sems = comm.specs()

    def body(*refs):
        cin = refs[14:14 + len(c_in)]
        outs = refs[14 + len(c_in):]
        comm.run(cin, outs[4:4 + comm.n], outs[4 + comm.n:], l // tl, lambda: compute(*refs[:14], *outs[:4]))

    def compute(x_ref, dx1_ref, a_ref, b_ref, c_ref, e_ref, lr_ref, wa_ref, wb_ref, wc_ref, we_ref, wl_ref, g_ref, sc_ref,
                gx_ref, dsh_ref, dsc_ref, dg_ref):
        @pl.when(pl.program_id(0) == 0)
        def _():
            for r in (dsh_ref, dsc_ref, dg_ref):
                r[...] = jnp.zeros_like(r)

        dh = (_nt(a_ref[...], wa_ref[...]) + _nt(b_ref[...], wb_ref[...]) + _nt(c_ref[...], wc_ref[...]) + _nt(e_ref[...], we_ref[...])
              + _nt(lr_ref[...], wl_ref[...]))
        xv = x_ref[...]
        r = lax.rsqrt(jnp.mean(xv * xv, axis=-1, keepdims=True) + EPS)
        xn = xv * r
        gxn = jnp.sum(dh * xn, axis=0, keepdims=True)
        dsh_ref[...] += jnp.sum(dh, axis=0, keepdims=True)
        dsc_ref[...] += gxn * g_ref[...]
        dg_ref[...] += gxn * (1.0 + sc_ref[...])
        dxn = dh * (g_ref[...] * (1.0 + sc_ref[...]))
        gx_ref[...] = dx1_ref[...] + r * (dxn - xn * jnp.mean(dxn * xn, axis=-1, keepdims=True))

    row = lambda w: pl.BlockSpec((tl, w), lambda i: (i, 0))
    wcol = lambda w, j: pl.BlockSpec((D, w), lambda i, j=j: (0, j))
    vec = pl.BlockSpec((1, D), lambda i: (0, 0))
    return pl.pallas_call(
        body, name="in_bwd", grid=(l // tl,),
        out_shape=(jax.ShapeDtypeStruct((l, D), f32),) + (jax.ShapeDtypeStruct((1, D), f32),) * 3 + tuple(comm.outs),
        in_specs=[row(D), row(D), row(2 * KW + VW), row(2 * AW), row(AW), row(AW + 2 * D), row(LRW),
                  wcol(2 * KW + VW, 0), wcol(2 * AW, PZB // (2 * AW)), wcol(AW, PVA // AW), wcol(AW + 2 * D, PZA // (AW + 2 * D)),
                  wcol(LRW, PLR // LRW), vec, vec] + c_in,
        out_specs=(row(D), vec, vec, vec) + tuple(c_out), scratch_shapes=c_sems,
        compiler_params=_cp(("arbitrary",)),
    )(x, dx1, dqkv, dzbua, dva, dzag, dlr, w_pad, w_pad, w_pad, w_pad, w_pad, g, scale, *comm.ins)


def _tn_matmul(a, bs, tl, name, comm=None, pack=None):
    l, m = a.shape
    k = len(bs)
    comm = comm or _Comm([], [], None)
    c_in, c_out, c_sems = comm.specs()
    acc_shapes = [(m, b.shape[1]) for b in bs]
    n_extra = len(pack.extra) if pack else 0
    n_res = len(pack.outs) if pack else k

    def body(a_ref, *refs):
        b_refs, refs = refs[:k], refs[k:]
        extra, refs = refs[:n_extra], refs[n_extra:]
        cin, refs = refs[:len(c_in)], refs[len(c_in):]
        res, refs = refs[:n_res], refs[n_res:]
        cout, refs = refs[:comm.n], refs[comm.n:]
        accs, sems = (refs[:k], refs[k:]) if pack else (res, refs)

        def compute():
            @pl.when(pl.program_id(0) == 0)
            def _():
                for o_ref in accs:
                    o_ref[...] = jnp.zeros_like(o_ref)

            av = a_ref[...]
            for b_ref, o_ref in zip(b_refs, accs):
                o_ref[...] += _tn(av, b_ref[...])
            if pack:
                pl.when(pl.program_id(0) == l // tl - 1)(lambda: pack.fn(accs, extra, res))

        comm.run(cin, cout, sems, l // tl, compute)

    full = lambda shape: pl.BlockSpec(shape, lambda i: (0,) * len(shape))
    res_shapes = list(pack.outs) if pack else [jax.ShapeDtypeStruct(sh, f32) for sh in acc_shapes]
    return pl.pallas_call(
        body, name=name, grid=(l // tl,), out_shape=tuple(res_shapes) + tuple(comm.outs),
        in_specs=[pl.BlockSpec((tl, m), lambda i: (i, 0))] + [pl.BlockSpec((tl, b.shape[1]), lambda i: (i, 0)) for b in bs]
        + [full(e.shape) for e in (pack.extra if pack else [])] + c_in,
        out_specs=tuple(full(r.shape) for r in res_shapes) + tuple(c_out),
        scratch_shapes=([pltpu.VMEM(sh, f32) for sh in acc_shapes] if pack else []) + c_sems, compiler_params=_cp(("arbitrary",)),
    )(a, *bs, *(pack.extra if pack else []), *comm.ins)


EARLY_A_ROWS = 512


class _NoExchange:
    def __init__(self, w_pa, w_pb, w_out):
        self.weights = (w_pa, w_pb, w_out)

    def gather_proj(self):
        return _Comm([], [], None)

    def proj_weights(self, got):
        return self.weights

    def first(self, dw_out, dw_pa, dw_pb, small):
        return _Comm([], [], None)

    def early_a(self, blocks):
        return _Comm([], [], None)

    def early_b(self, blocks):
        return _Comm([], [], None)

    def late(self, blocks, dgt):
        return _Comm([], [], None)


class _Exchanges:
    def __init__(self, pa, pb, wo):
        self.shards = (pa, pb, wo)

    def gather_proj(self):
        def plan(i, o):
            srcs = [lambda j, r=r: r for r in i]
            dsts = [lambda j: o[0].at[:, _lanes(j)], lambda j: o[1].at[:, _lanes(j)], lambda j: o[2].at[j]]
            return srcs, dsts, None
        sds = jax.ShapeDtypeStruct
        return _Comm(self.shards, [sds((AW, D), bf16), sds((VW, D), bf16), sds((NDEV, 128, D), bf16)], plan)

    def proj_weights(self, got):
        return got[0], got[1], got[2].reshape(D, D)

    def first(self, dw_out, dw_pa, dw_pb, small):
        def plan(i, o):
            srcs = [lambda j: i[0].at[j], lambda j: i[1].at[:, _lanes(j)], lambda j: i[2].at[:, _lanes(j)], lambda j: i[3]]
            dsts = [lambda j, r=r: r.at[j] for r in o]
            return srcs, dsts, None
        sds = jax.ShapeDtypeStruct
        return _Comm([dw_out.reshape(NDEV, 128, D), dw_pa, dw_pb, small],
                     [sds((NDEV, 128, D), bf16), sds((NDEV, AW, 128), bf16), sds((NDEV, VW, 128), bf16),
                      sds((NDEV,) + small.shape, f32)], plan)

    def early_a(self, blocks):
        def plan(i, o):
            return [lambda j: i[0].at[j]], [lambda j: o[0].at[j]], [lambda j: j >= LATE_DESTS - 1]
        return _Comm([blocks], [jax.ShapeDtypeStruct(blocks.shape, bf16)], plan)

    def early_b(self, blocks):
        def plan(i, o):
            return [lambda j: i[0].at[j]], [lambda j: o[0].at[j]], [lambda j: j >= LATE_DESTS - 1]
        return _Comm([blocks], [jax.ShapeDtypeStruct(blocks.shape, bf16)], plan)

    def late(self, blocks, dgt):
        return _LateComm(blocks, dgt)


def _local_step(x, ctx, tgt, mod, modc, norm_g, w_pad, ln_g, ln_b, ws, bs, gate_w, gb_norm, gf, xch):
    shift, scale, gate = mod[:, 0:D], mod[:, D:2 * D], mod[:, 2 * D:]
    shift_c, scale_c = modc[:, 0:D], modc[:, D:]
    w2f, w2b, gbf, gbb = gate_w

    p, h, *got_proj = _in_proj(x, norm_g, scale, shift, w_pad, 512, xch.gather_proj())
    w_pa, w_pb, w_out = xch.proj_weights(got_proj)
    sc_f, sc_b = _ctx_fwd(ctx, norm_g, scale_c, shift_c, w_pad, w2f, w2b, gbf, gbb)
    o_f, st_f = _gla_fwd(p, w2f, gbf, sc_f, False, 512, "gla_fwd_f")
    o_b, st_b = _gla_fwd(p, w2b, gbb, sc_b, True, 512, "gla_fwd_b")
    sv = _mix_fwd(p, ln_g, ln_b, ws.astype(bf16), bs.T)
    (dx1, dzbua, dzag, dsv, do, dw_out, dw_pa, dw_pb, dgf, dgate, dgbn, loss) = _mid(
        x, tgt, p, o_f, o_b, sv, gate, gf, gb_norm, w_pa, w_pb, w_out, 256)
    dva, dws, dbs_acc, dln_g, dln_b = _mix_bwd(p, dsv, ln_g, ln_b, jnp.swapaxes(ws, 1, 2).astype(bf16))
    small = _rows128(dln_g, dln_b, dws, jnp.sum(dbs_acc, axis=-1), dgbn, dgf, jnp.broadcast_to(loss, (1, 128)))
    blocks_a, blocks_b, *got_first = _tn_matmul(h, [dzbua, dva, dzag], 1024, "dw_early", xch.first(dw_out, dw_pa, dw_pb, small),
                                               _pack_early())

    dqkv_f, dlr_f, dw2f, dgbf, dsc_f, *got_a = _gla_bwd(p, do, st_f, w2f, gbf, None, False, 512, "gla_bwd_f",
                                                        xch.early_a(blocks_a))
    dqkv, dlr, dw2b, dgbb, dsc_b, *got_b = _gla_bwd(p, do, st_b, w2b, gbb, (dqkv_f, dlr_f), True, 512, "gla_bwd_b",
                                                    xch.early_b(blocks_b))
    dwk_c, dwv_c, dwl_c, dmodc, dg_c, dw2c, dgbc = _ctx_bwd(ctx, norm_g, scale_c, shift_c, w_pad, w2f, w2b, gbf, gbb, dsc_f, dsc_b)
    (blocks_late,) = _tn_matmul(h, [dqkv, dlr], 1024, "dw_qkv_lr", pack=_pack_late(dwk_c, dwv_c, dwl_c))
    dw2 = jnp.stack([dw2f[0:RANK] + dw2c[0, 0:RANK], dw2b[RANK:2 * RANK] + dw2c[1, RANK:2 * RANK]])
    dgb = jnp.concatenate([dgbf + dgbc[0], dgbb + dgbc[1]], axis=0)
    dgt = jnp.concatenate([jnp.transpose(dw2.reshape(2, RANK, NDEV, 32), (2, 0, 1, 3)).reshape(NDEV, 2 * RANK * 32),
                           jnp.transpose(dgb.reshape(2, NDEV, 32), (1, 0, 2)).reshape(NDEV, 64),
                           jnp.zeros((NDEV, 64), f32)], axis=1).reshape(NDEV, 9, 128)
    gx, dshift, dscale, dg, *got_late = _in_bwd(x, dx1, dqkv, dzbua, dva, dzag, dlr, w_pad, norm_g, scale, 512,
                                                xch.late(blocks_late, dgt))
    return dict(loss=loss, gx=gx, dmod=jnp.concatenate([dshift, dscale, dgate], axis=1), dmodc=dmodc, dnorm_g=dg + dg_c,
                dnorm_parts=(dg, dg_c), dmod_parts=(dshift, dscale, dgate),
                small=small, blocks_a=blocks_a, blocks_b=blocks_b, blocks_late=blocks_late, dw2=dw2, dgb=dgb,
                dw_pa=dw_pa, dw_pb=dw_pb, dw_out=dw_out, got_first=got_first, got_a=got_a, got_b=got_b, got_late=got_late)


def _rows128(*vs):
    out = []
    for t in vs:
        t = t.reshape(-1)
        pad = (-t.shape[0]) % 128
        out.append(jnp.pad(t, (0, pad)) if pad else t)
    return jnp.concatenate(out).reshape(-1, 128)


def kernel(x, c, ctx, c_ctx, w_mod, b_mod, norm_g, w_in, a_ln_g, a_ln_b, a_ws, a_bs, b_gate_w2, b_gate_b, b_norm_g, w_proj_a, w_proj_b, w_out, final_norm_g, loss_target, m_c_ctx, m_w_mod, m_b_mod, m_norm_g, m_w_in, m_a_ln_g, m_a_ln_b, m_a_ws, m_a_bs, m_b_gate_w2, m_b_gate_b, m_b_norm_g, m_w_proj_a, m_w_proj_b, m_w_out, m_final_norm_g, v_c_ctx, v_w_mod, v_b_mod, v_norm_g, v_w_in, v_a_ln_g, v_a_ln_b, v_a_ws, v_a_bs, v_b_gate_w2, v_b_gate_b, v_b_norm_g, v_w_proj_a, v_w_proj_b, v_w_out, v_final_norm_g):
    me = _me()

    cs, mods, wg, gates = _gather_first(c, c_ctx.reshape(1, D), w_mod[0], b_mod, w_in[0].astype(bf16), b_gate_w2, b_gate_b)
    w_pad, *gate_w = _repack_w(wg, gates)

    mods = jnp.transpose(mods, (1, 0, 2)).reshape(16, 3 * D)
    mod = lax.dynamic_slice(mods, (me, 0), (1, 3 * D))
    modc = mods[8:9, 0:2 * D]

    xch = _Exchanges(w_proj_a[0].astype(bf16), w_proj_b[0].astype(bf16), w_out[0].astype(bf16))
    r = _local_step(x[0], ctx[0], loss_target[0], mod, modc, norm_g, w_pad, a_ln_g, a_ln_b, a_ws[0], a_bs[0], gate_w,
                    b_norm_g, final_norm_g.reshape(1, D), xch)
    p_out, p_pa, p_pb, smalls_e = r["got_first"]
    (p_in_a,) = r["got_a"]
    (p_in_b,) = r["got_b"]
    _, _, p_in_late, p_gt = r["got_late"]

    n_e = (AW + AW + 4 * ACH * ACH + AW + VW + D) // 128
    row = lambda t: t.reshape(1, D)
    rep_e = _adam_params(smalls_e, [a_ln_g, a_ln_b, a_ws, a_bs, b_norm_g, row(final_norm_g)],
                         [m_a_ln_g, m_a_ln_b, m_a_ws, m_a_bs, m_b_norm_g, row(m_final_norm_g)],
                         [v_a_ln_g, v_a_ln_b, v_a_ws, v_a_bs, v_b_norm_g, row(v_final_norm_g)], "adam_rep_early")
    rep_e = [t[0:5] + (t[5].reshape(D),) for t in rep_e]
    loss = jnp.sum(smalls_e[:, n_e, 0])

    mod_grads = _mod_tail(r["dnorm_parts"], r["dmod_parts"], r["dmodc"], cs, w_mod)
    mod_res = _adam_plain(mod_grads, [w_mod, norm_g, b_mod, row(c_ctx)], [m_w_mod, m_norm_g, m_b_mod, row(m_c_ctx)],
                          [v_w_mod, v_norm_g, v_b_mod, row(v_c_ctx)], "adam_mod")
    wm, ng, bmod_r, cc = ([t[i] for t in mod_res] for i in range(4))

    a_in = _adam_w_in(p_in_a, p_in_b, p_in_late, w_in, m_w_in, v_w_in)
    blk = _adam_blocks([p_pa, p_pb, p_out], [w_proj_a, w_proj_b, w_out], [m_w_proj_a, m_w_proj_b, m_w_out],
                       [v_w_proj_a, v_w_proj_b, v_w_out], "adam_proj_out")
    a_pa, a_pb, a_out = ([t[i] for t in blk] for i in range(3))
    a_gt = _adam_gate(p_gt, b_gate_w2, b_gate_b, m_b_gate_w2, m_b_gate_b, v_b_gate_w2, v_b_gate_b)
    sh = [(a_in[k], a_pa[k], a_pb[k], a_out[k], a_gt[k], a_gt[4 + k]) for k in range(4)]

    outs = [loss, r["gx"][None]]
    for k in range(4):
        lg, lb, aws, abs_, bng, fng = rep_e[k]
        n_g, bmod = ng[k], bmod_r[k]
        s_in, s_pa, s_pb, s_out, s_w2, s_gb = sh[k]
        outs += [cc[k].reshape(D), wm[k], bmod, n_g, s_in, lg, lb, aws, abs_, s_w2, s_gb, bng, s_pa, s_pb, s_out, fng]
    return tuple(outs)
```

```python
import jax
import jax.numpy as jnp
from jax import lax
from jax.experimental import pallas as pl
from jax.experimental.pallas import tpu as pltpu

f32, bf16 = jnp.float32, jnp.bfloat16

D = 1024
CTX = 256
EPS = 1e-6
AW = 512
ACH = 128
GW = 64
KW = 256
VW = 512
NH = 4
HK = 64
HV = 128
RANK = 16
TAU = 16.0
CH = 64
QSCALE = HK ** -0.5
INW = 5152
NDEV = 8

PQ, PK, PV, PZB, PUA, PVA, PZA, PG1, PG2, PLR, PW = 0, 256, 512, 1024, 1536, 2048, 2560, 3072, 4096, 5120, 5248
LRW = 128

ADAM_LR, ADAM_B1, ADAM_B2, ADAM_EPS, ADAM_WD, ADAM_STEP = 0.001, 0.9, 0.999, 1e-08, 0.01, 10

VMEM_LIMIT = 56 * 1024 * 1024
MESH = pl.DeviceIdType.MESH


def _cp(sem=None):
    return pltpu.CompilerParams(dimension_semantics=sem, vmem_limit_bytes=VMEM_LIMIT)


def _dot(a, b):
    return jnp.dot(a.astype(bf16), b.astype(bf16), preferred_element_type=f32)


def _nt(a, b):
    return lax.dot_general(a.astype(bf16), b.astype(bf16), (((1,), (1,)), ((), ())), preferred_element_type=f32)


def _tn(a, b):
    return lax.dot_general(a.astype(bf16), b.astype(bf16), (((0,), (0,)), ((), ())), preferred_element_type=f32)


def _dot_hi(a, b):
    return jnp.dot(a, b, preferred_element_type=f32, precision=lax.Precision.HIGHEST)


def _sigmoid(x):
    return 1.0 / (1.0 + jnp.exp(-x))


def _log_sigmoid(x):
    return jnp.minimum(x, 0.0) - jnp.log(1.0 + jnp.exp(-jnp.abs(x)))


def _silu_and_grad(z):
    s = _sigmoid(z)
    return z * s, s * (1.0 + z * (1.0 - s))


def _me():
    return 4 * lax.axis_index("x") + 2 * lax.axis_index("y") + lax.axis_index("c")


def _peer(k):
    x, y, c = lax.axis_index("x"), lax.axis_index("y"), lax.axis_index("c")
    px = 1 - x if k & 4 else x
    py = 1 - y if k & 2 else y
    pc = 1 - c if k & 1 else c
    return (px, py, pc), 4 * px + 2 * py + pc


def _fanout(srcs, dsts, send_sems, recv_sems, local_sems, owners=None):
    me = _me()
    n = len(srcs)
    owns = lambda a, j: True if owners is None or owners[a] is None else owners[a](j)

    def guarded(cond, fn):
        if cond is True:
            fn()
        else:
            pl.when(cond)(fn)

    def copies(with_recvs):
        local = [pltpu.make_async_copy(srcs[a](me), dsts[a](me), local_sems.at[a]) for a in range(n)]
        sends, recvs = [], []
        for k in range(1, NDEV):
            dev, idx = _peer(k)
            for a in range(n):
                s = (k - 1) * n + a
                sends.append((owns(a, idx), pltpu.make_async_remote_copy(
                    src_ref=srcs[a](idx), dst_ref=dsts[a](me), send_sem=send_sems.at[s], recv_sem=recv_sems.at[s],
                    device_id=dev, device_id_type=MESH)))
                if with_recvs:
                    recvs.append((owns(a, me), pltpu.make_async_remote_copy(
                        src_ref=srcs[a](idx), dst_ref=dsts[a](idx), send_sem=send_sems.at[s], recv_sem=recv_sems.at[s],
                        device_id=dev, device_id_type=MESH)))
        return local, sends, recvs

    def start():
        local, sends, _ = copies(False)
        for a, cp in enumerate(local):
            guarded(owns(a, me), cp.start)
        for cond, cp in sends:
            guarded(cond, cp.start)

    def finish():
        local, sends, recvs = copies(True)
        for cond, cp in recvs:
            guarded(cond, cp.wait_recv)
        for cond, cp in sends:
            guarded(cond, cp.wait_send)
        for a, cp in enumerate(local):
            guarded(owns(a, me), cp.wait)

    return start, finish


class _Comm:
    def __init__(self, ins, outs, plan):
        self.ins, self.outs, self.plan = list(ins), list(outs), plan
        self.n = len(self.outs)

    def specs(self):
        hbm = pl.BlockSpec(memory_space=pl.ANY)
        return [hbm] * len(self.ins), [hbm] * self.n, _fanout_sems(self.n) if self.n else []

    def run(self, in_refs, out_refs, sems, nsteps, compute):
        if not self.n:
            compute()
            return

        def hooks():
            srcs, dsts, owners = self.plan(in_refs, out_refs)
            return _fanout(srcs, dsts, sems[0], sems[1], sems[2], owners)

        pl.when(pl.program_id(0) == 0)(lambda: hooks()[0]())
        compute()
        pl.when(pl.program_id(0) == nsteps - 1)(lambda: hooks()[1]())


LATE_MID_STEP = 1


class _LateComm:
    def __init__(self, blocks, dgt):
        sds = jax.ShapeDtypeStruct
        self.ins = [blocks, dgt]
        self.outs = [sds((D, SHARD), bf16), sds((D, SHARD), bf16), sds((4, D, SHARD), bf16), sds(dgt.shape, f32)]
        self.n = len(self.outs)

    def specs(self):
        hbm = pl.BlockSpec(memory_space=pl.ANY)
        scratch = [pltpu.VMEM((3, D, SHARD), bf16), pltpu.SemaphoreType.DMA((2,)), pltpu.SemaphoreType.DMA((4,)),
                   pltpu.SemaphoreType.DMA((3,))] + _fanout_sems(1)
        return [hbm] * 2, [hbm] * self.n, scratch

    def run(self, in_refs, out_refs, scratch, nsteps, compute):
        late_ref, dgt_ref = in_refs
        sib_ref, pair_ref, parts_ref, gt_ref = out_refs
        vbuf, send_sems, recv_sems, local_sems, g_send, g_recv, g_local = scratch
        x, y, c = lax.axis_index("x"), lax.axis_index("y"), lax.axis_index("c")
        chip = 2 * x + y
        is_owner_chip = chip == 0
        step = pl.program_id(0)

        def to_sibling():
            return pltpu.make_async_remote_copy(src_ref=late_ref.at[1 - c], dst_ref=sib_ref, send_sem=send_sems.at[0],
                                                recv_sem=recv_sems.at[0], device_id=(x, y, 1 - c), device_id_type=MESH)

        def to_owner(k):
            return pltpu.make_async_remote_copy(src_ref=pair_ref, dst_ref=parts_ref.at[k], send_sem=send_sems.at[1],
                                                recv_sem=recv_sems.at[k], device_id=(0, 0, c), device_id_type=MESH)

        def own_copy():
            return pltpu.make_async_copy(pair_ref, parts_ref.at[0], local_sems.at[2])

        def gates():
            return _fanout([lambda j: dgt_ref.at[j]], [lambda j: gt_ref.at[j]], g_send, g_recv, g_local)

        @pl.when(step == 0)
        def _():
            to_sibling().start()
            gates()[0]()

        compute()

        @pl.when(step == LATE_MID_STEP)
        def _():
            mine = pltpu.make_async_copy(late_ref.at[c], vbuf.at[0], local_sems.at[0])
            mine.start()
            to_sibling().wait_recv()
            theirs = pltpu.make_async_copy(sib_ref, vbuf.at[1], local_sems.at[1])
            theirs.start()
            mine.wait()
            theirs.wait()
            vbuf[2] = (vbuf[0].astype(f32) + vbuf[1].astype(f32)).astype(bf16)
            pltpu.sync_copy(vbuf.at[2], pair_ref)
            pl.when(is_owner_chip)(lambda: own_copy().start())
            pl.when(jnp.logical_not(is_owner_chip))(lambda: to_owner(chip).start())

        @pl.when(step == nsteps - 1)
        def _():
            @pl.when(is_owner_chip)
            def _():
                for k in range(1, 4):
                    to_owner(k).wait_recv()
                own_copy().wait()

            pl.when(jnp.logical_not(is_owner_chip))(lambda: to_owner(chip).wait_send())
            to_sibling().wait_send()
            gates()[1]()


def _fanout_sems(n):
    return [pltpu.SemaphoreType.DMA(((NDEV - 1) * n,)), pltpu.SemaphoreType.DMA(((NDEV - 1) * n,)), pltpu.SemaphoreType.DMA((n,))]


def _lanes(j):
    return pl.ds(pl.multiple_of(j * 128, 128), 128)


def _gather_first(c_row, cctx_row, wm, bm, wi, w2, gb):
    per = 128 // 32
    gate_rows = 2 * RANK // per + 1

    def body(c_ref, cctx_ref, wm_ref, bm_ref, wi_ref, w2_ref, gb_ref, cs_ref, mods_ref, owi, og, g_ref, call_ref, mine_ref,
             send_sems, recv_sems, local_sems, c_send, c_recv, c_local, m_send, m_recv, m_local):
        g_ref[...] = jnp.zeros_like(g_ref)
        for d in range(2):
            for r in range(RANK):
                q = d * RANK + r
                g_ref[q // per:q // per + 1, (q % per) * 32:(q % per + 1) * 32] = w2_ref[0, d, r:r + 1, :]
            g_ref[gate_rows - 1:gate_rows, d * 32:(d + 1) * 32] = gb_ref[0, d:d + 1, :]
        x, y, c = lax.axis_index("x"), lax.axis_index("y"), lax.axis_index("c")
        sibling = (x, y, 1 - c)
        chips = [(1 - x, y), (x, 1 - y), (1 - x, 1 - y)]
        index = lambda px, py, pc: 4 * px + 2 * py + pc
        arrays = ((wi_ref, owi), (g_ref, og))
        n = len(arrays)

        def copy(a, k, block, to, own=False):
            src, out = arrays[a]
            return pltpu.make_async_remote_copy(
                src_ref=src if own else out.at[index(*block)], dst_ref=out.at[index(*block)],
                send_sem=send_sems.at[k * n + a], recv_sem=recv_sems.at[k * n + a], device_id=to, device_id_type=MESH)

        c_start, c_finish = _fanout([lambda j: c_ref], [lambda j: call_ref.at[j]], c_send, c_recv, c_local)
        c_start()
        mine = [pltpu.make_async_copy(src, out.at[index(x, y, c)], local_sems.at[a]) for a, (src, out) in enumerate(arrays)]
        first = [copy(a, 0, (x, y, c), sibling, own=True) for a in range(n)]
        first += [copy(a, 1 + j, (x, y, c), (*chip, c), own=True) for j, chip in enumerate(chips) for a in range(n)]
        for cp in mine + first:
            cp.start()

        c_finish()
        cs = jnp.concatenate([call_ref[j] for j in range(NDEV)] + [cctx_ref[...], jnp.zeros((16 - NDEV - 1, D), f32)], axis=0)
        cs_ref[...] = cs
        s, _ = _silu_and_grad(cs)
        mine_ref[...] = _dot_hi(s, wm_ref[...]) + bm_ref[:, pl.ds(pl.multiple_of(_me() * ncol, 128), ncol)]
        m_start, m_finish = _fanout([lambda j: mine_ref], [lambda j: mods_ref.at[j]], m_send, m_recv, m_local)
        m_start()

        passed = []
        for j, chip in enumerate(chips):
            for a in range(n):
                copy(a, 1 + j, (*chip, c), (x, y, c)).wait_recv()
            for a in range(n):
                cp = copy(a, 4 + j, (*chip, c), sibling)
                cp.start()
                passed.append(cp)
        for a in range(n):
            copy(a, 0, sibling, (x, y, c)).wait_recv()
        for j, chip in enumerate(chips):
            for a in range(n):
                copy(a, 4 + j, (*chip, 1 - c), (x, y, c)).wait_recv()
        for cp in first + passed:
            cp.wait_send()
        for cp in mine:
            cp.wait()
        m_finish()

    hbm = pl.BlockSpec(memory_space=pl.ANY)
    vm = pl.BlockSpec(memory_space=pltpu.VMEM)
    ncol = wm.shape[1]
    return pl.pallas_call(
        body, name="gather_first",
        out_shape=(jax.ShapeDtypeStruct((16, D), f32), jax.ShapeDtypeStruct((NDEV, 16, ncol), f32),
                   jax.ShapeDtypeStruct((NDEV,) + wi.shape, bf16), jax.ShapeDtypeStruct((NDEV, gate_rows, 128), f32)),
        in_specs=[vm, vm, vm, vm, hbm, vm, vm], out_specs=(vm, vm, hbm, hbm),
        scratch_shapes=[pltpu.VMEM((gate_rows, 128), f32), pltpu.VMEM((NDEV, 1, D), f32), pltpu.VMEM((16, ncol), f32)]
        + _fanout_sems(2) + _fanout_sems(1) + _fanout_sems(1),
        compiler_params=_cp(),
    )(c_row, cctx_row, wm, bm, wi, w2, gb)


SHARD = INW // NDEV
ROWS_RP = 128


def _overlap(lo, hi, a, b):
    s, e = max(lo, a), min(hi, b)
    return (s, e) if s < e else None


def _repack_w(wg, gates):
    segs = ((0, 1024, PQ), (1024, 1024 + 2 * RANK, PLR), (1024 + 2 * RANK, INW, PZB))
    per = 128 // 32

    def body(g_ref, gt_ref, o_ref, w2f_ref, w2b_ref, gbf_ref, gbb_ref):
        @pl.when(pl.program_id(0) == 0)
        def _():
            w2f_ref[...] = jnp.zeros_like(w2f_ref)
            w2b_ref[...] = jnp.zeros_like(w2b_ref)
            for dev in range(NDEV):
                cols = slice(dev * 32, (dev + 1) * 32)
                for d, (w2_ref, gb_ref) in enumerate(((w2f_ref, gbf_ref), (w2b_ref, gbb_ref))):
                    for r in range(RANK):
                        q = d * RANK + r
                        w2_ref[q:q + 1, cols] = gt_ref[dev, q // per:q // per + 1, (q % per) * 32:(q % per + 1) * 32]
                    gb_ref[:, cols] = gt_ref[dev, 2 * RANK // per:2 * RANK // per + 1, d * 32:(d + 1) * 32]

        for j in range(NDEV):
            lo, hi = j * SHARD, (j + 1) * SHARD
            for a, b, pad0 in segs:
                ov = _overlap(lo, hi, a, b)
                if ov:
                    s, e = ov
                    o_ref[:, pad0 + s - a:pad0 + e - a] = g_ref[j, :, s - lo:e - lo]
        o_ref[:, PLR + 2 * RANK:PW] = jnp.zeros((ROWS_RP, PW - PLR - 2 * RANK), bf16)

    sds = jax.ShapeDtypeStruct
    full = lambda *s: pl.BlockSpec(s, lambda i: (0,) * len(s))
    return pl.pallas_call(
        body, name="repack_w", grid=(D // ROWS_RP,),
        out_shape=(sds((D, PW), bf16), sds((LRW, KW), f32), sds((LRW, KW), f32), sds((1, KW), f32), sds((1, KW), f32)),
        in_specs=[pl.BlockSpec((NDEV, ROWS_RP, SHARD), lambda i: (0, i, 0)), full(*gates.shape)],
        out_specs=(pl.BlockSpec((ROWS_RP, PW), lambda i: (i, 0)), full(LRW, KW), full(LRW, KW), full(1, KW), full(1, KW)),
        compiler_params=_cp(("arbitrary",)),
    )(wg, gates)


LATE_END = 1024 + 2 * RANK
LATE_DESTS = 2


def _pack_blocks(o_ref, srcs, dests, dtype):
    for n, j in enumerate(dests):
        lo, hi = j * SHARD, (j + 1) * SHARD
        done = lo
        for a, b, src in srcs:
            ov = _overlap(lo, hi, a, b)
            if ov:
                s, e = ov
                if s > done:
                    o_ref[n, :, done - lo:s - lo] = jnp.zeros((ROWS_RP, s - done), dtype)
                o_ref[n, :, s - lo:e - lo] = src[:, s - a:e - a].astype(dtype)
                done = e
        if done < hi:
            o_ref[n, :, done - lo:hi - lo] = jnp.zeros((ROWS_RP, hi - done), dtype)


class _Pack:
    def __init__(self, extra, outs, fn):
        self.extra, self.outs, self.fn = list(extra), list(outs), fn


def _pack_early():
    def fn(accs, extra, outs):
        zbua, va, zag = accs
        for r0 in range(0, D, ROWS_RP):
            rows = pl.ds(r0, ROWS_RP)
            out, o0 = (outs[0], r0) if r0 < EARLY_A_ROWS else (outs[1], r0 - EARLY_A_ROWS)
            _pack_blocks(out.at[:, pl.ds(o0, ROWS_RP)], ((LATE_END, 2080, zbua.at[rows]), (2080, 2592, va.at[rows]), (2592, INW, zag.at[rows])),
                         range(NDEV), bf16)

    sds = jax.ShapeDtypeStruct
    return _Pack([], [sds((NDEV, EARLY_A_ROWS, SHARD), bf16), sds((NDEV, D - EARLY_A_ROWS, SHARD), bf16)], fn)


def _pack_late(dwk_c, dwv_c, dwl_c):
    def fn(accs, extra, outs):
        qkv_ref, lr_ref = accs
        kc_ref, vc_ref, lc_ref = extra
        for r0 in range(0, D, ROWS_RP):
            rows = pl.ds(r0, ROWS_RP)
            qkv = qkv_ref[rows, :] + jnp.concatenate([jnp.zeros((ROWS_RP, KW), f32), kc_ref[rows, :], vc_ref[rows, :]], axis=1)
            lr = lr_ref[rows, :] + lc_ref[rows, :]
            _pack_blocks(outs[0].at[:, rows], ((0, 1024, qkv), (1024, LATE_END, lr)), range(LATE_DESTS), bf16)

    return _Pack([dwk_c, dwv_c, dwl_c], [jax.ShapeDtypeStruct((LATE_DESTS, D, SHARD), bf16)], fn)


def _mod_tail(dnorm_parts, dmod_parts, dmodc, cs, wm):
    ncol = wm.shape[2]
    n_dn, n_dm = len(dnorm_parts), len(dmod_parts)

    def body(*refs):
        dn_parts, refs = refs[:n_dn], refs[n_dn:]
        dm_parts, refs = refs[:n_dm], refs[n_dm:]
        dmodc_ref, cs_ref, wm_ref, g_wm, g_ng, g_bm, g_cc = refs[:7]
        dn_ref, dmod_ref, a_dn, a_dmod, a_dmodc, dm_rows, gc_ref, a_gc = refs[7:15]
        s1, r1, l1, s2, r2, l2 = refs[15:21]
        dn = dn_parts[0][...]
        for ref in dn_parts[1:]:
            dn = dn + ref[...]
        dn_ref[...] = dn
        dmod_ref[...] = jnp.concatenate([ref[...] for ref in dm_parts], axis=1)
        start, finish = _fanout([lambda j: dn_ref, lambda j: dmod_ref, lambda j: dmodc_ref],
                                [lambda j, r=r: r.at[j] for r in (a_dn, a_dmod, a_dmodc)], s1, r1, l1)
        start()
        finish()

        def total(ref):
            t = ref[0]
            for j in range(1, NDEV):
                t = t + ref[j]
            return t

        g_ng[...] = total(a_dn)
        dmodc_tot = jnp.concatenate([total(a_dmodc), jnp.zeros((1, D), f32)], axis=1)
        g_bm[...] = total(a_dmod) + dmodc_tot
        dm_rows[...] = jnp.concatenate([a_dmod[j] for j in range(NDEV)] + [dmodc_tot, jnp.zeros((16 - NDEV - 1, 3 * D), f32)], axis=0)
        dm = dm_rows[:, pl.ds(pl.multiple_of(_me() * ncol, 128), ncol)]
        s, ds = _silu_and_grad(cs_ref[...])
        part = lax.dot_general(dm[8:9, :], wm_ref[0], (((1,), (1,)), ((), ())), preferred_element_type=f32,
                               precision=lax.Precision.HIGHEST)
        gc_ref[...] = part * ds[8:9, :]
        start2, finish2 = _fanout([lambda j: gc_ref], [lambda j: a_gc.at[j]], s2, r2, l2)
        start2()
        g = lax.dot_general(s, dm, (((0,), (0,)), ((), ())), preferred_element_type=f32, precision=lax.Precision.HIGHEST)
        g_wm[...] = g[None]
        finish2()
        g_cc[...] = total(a_gc)

    sds = jax.ShapeDtypeStruct
    row = lambda n: pltpu.VMEM((NDEV, 1, n), f32)
    return pl.pallas_call(
        body, name="mod_tail", out_shape=(sds(wm.shape, f32), sds((1, D), f32), sds((1, 3 * D), f32), sds((1, D), f32)),
        scratch_shapes=[pltpu.VMEM((1, D), f32), pltpu.VMEM((1, 3 * D), f32), row(D), row(3 * D), row(2 * D),
                        pltpu.VMEM((16, 3 * D), f32), pltpu.VMEM((1, D), f32), row(D)] + _fanout_sems(3) + _fanout_sems(1),
        compiler_params=_cp(),
    )(*dnorm_parts, *dmod_parts, dmodc, cs, wm)


def _adam_update(g, w_ref, m_ref, v_ref, go_ref, d_ref, mo_ref, vo_ref):
    c1 = 1.0 / (1.0 - ADAM_B1 ** ADAM_STEP)
    c2 = 1.0 / (1.0 - ADAM_B2 ** ADAM_STEP)
    mn = ADAM_B1 * m_ref[...] + (1.0 - ADAM_B1) * g
    vn = ADAM_B2 * v_ref[...] + (1.0 - ADAM_B2) * (g * g)
    go_ref[...] = g
    mo_ref[...] = mn
    vo_ref[...] = vn
    d_ref[...] = -ADAM_LR * ((mn * c1) / (jnp.sqrt(vn * c2) + ADAM_EPS) + ADAM_WD * w_ref[...])


def _adam_w_in(parts_a, parts_b, parts_late, w, m, v):
    na = EARLY_A_ROWS // ROWS_RP
    n = D // ROWS_RP
    whole = SHARD // 128 * 128

    def body(a_ref, b_ref, l_ref, w_hbm, m_hbm, v_hbm, go_hbm, d_hbm, mo_hbm, vo_hbm, ibuf, obuf, isem, osem):
        step = pl.program_id(0)

        def cols(t):
            return pl.ds(pl.multiple_of(t * ROWS_RP, ROWS_RP), ROWS_RP)

        def fetch(t):
            return [pltpu.make_async_copy(r.at[:, 0, cols(t)], ibuf.at[t % 2, k], isem.at[t % 2, k])
                    for k, r in enumerate((w_hbm, m_hbm, v_hbm))]

        def flush(t):
            return [pltpu.make_async_copy(obuf.at[t % 2, k], r.at[:, 0, cols(t)], osem.at[t % 2, k])
                    for k, r in enumerate((go_hbm, d_hbm, mo_hbm, vo_hbm))]

        def start(cps):
            for cp in cps:
                cp.start()

        def wait(cps):
            for cp in cps:
                cp.wait()

        pl.when(step == 0)(lambda: start(fetch(step)))
        pl.when(step + 1 < n)(lambda: start(fetch(step + 1)))
        me = _me()
        first = step < na
        early = jnp.where(first, a_ref[0], b_ref[0]).astype(f32)
        for i in range(1, NDEV):
            early = early + jnp.where(first, a_ref[i], b_ref[i]).astype(f32)
        late = l_ref[0].astype(f32)
        for i in range(1, 4):
            late = late + l_ref[i].astype(f32)
        g = jnp.where(me >= LATE_DESTS - 1, early, 0.0) + jnp.where(me < LATE_DESTS, late, 0.0)
        gt = jnp.concatenate([g[:, c:c + 128].T for c in range(0, whole, 128)] + [g[:, SHARD - 128:].T[128 - (SHARD - whole):]],
                             axis=0)
        wait(fetch(step))
        pl.when(step >= 2)(lambda: wait(flush(step - 2)))
        slot = step % 2
        _adam_update(gt, *(ibuf.at[slot, k] for k in range(3)), *(obuf.at[slot, k] for k in range(4)))
        start(flush(step))

        @pl.when(step == n - 1)
        def _():
            wait(flush(step - 1))
            wait(flush(step))

    hbm = pl.BlockSpec(memory_space=pl.ANY)
    res = pl.pallas_call(
        body, name="adam_w_in", grid=(n,), out_shape=tuple(jax.ShapeDtypeStruct((SHARD, 1, D), f32) for _ in range(4)),
        in_specs=[pl.BlockSpec((NDEV, ROWS_RP, SHARD), lambda i: (0, jnp.minimum(i, na - 1), 0)),
                  pl.BlockSpec((NDEV, ROWS_RP, SHARD), lambda i: (0, jnp.maximum(i - na, 0), 0)),
                  pl.BlockSpec((4, ROWS_RP, SHARD), lambda i: (0, i, 0)), hbm, hbm, hbm],
        out_specs=(hbm, hbm, hbm, hbm),
        scratch_shapes=[pltpu.VMEM((2, 3, SHARD, ROWS_RP), f32), pltpu.VMEM((2, 4, SHARD, ROWS_RP), f32),
                        pltpu.SemaphoreType.DMA((2, 3)), pltpu.SemaphoreType.DMA((2, 4))],
        compiler_params=_cp(("arbitrary",)),
    )(parts_a, parts_b, parts_late, *(jnp.transpose(t, (2, 0, 1)) for t in (w, m, v)))
    return tuple(jnp.transpose(t, (1, 2, 0)) for t in res)


def _adam_params(parts, ws, ms, vs, name):
    p = parts.shape[0]
    k = len(ws)
    nrows = [w.size // 128 for w in ws]
    starts = [sum(nrows[:i]) for i in range(k)]

    def shaped(g, shape):
        if len(shape) == 2:
            return jnp.concatenate([g[r:r + 1] for r in range(g.shape[0])], axis=1)
        return g.reshape(shape)

    def body(*refs):
        g_ref = refs[0]
        w_refs, m_refs, v_refs = refs[1:1 + k], refs[1 + k:1 + 2 * k], refs[1 + 2 * k:1 + 3 * k]
        outs = refs[1 + 3 * k:]
        for i in range(k):
            rows = slice(starts[i], starts[i] + nrows[i])
            g = g_ref[0, rows, :]
            for j in range(1, p):
                g = g + g_ref[j, rows, :]
            _adam_update(shaped(g, ws[i].shape), w_refs[i], m_refs[i], v_refs[i], outs[i], outs[k + i], outs[2 * k + i], outs[3 * k + i])

    vm = pl.BlockSpec(memory_space=pltpu.VMEM)
    res = pl.pallas_call(
        body, name=name, out_shape=tuple(jax.ShapeDtypeStruct(w.shape, f32) for _ in range(4) for w in ws),
        in_specs=[vm] * (1 + 3 * k), out_specs=(vm,) * (4 * k), compiler_params=_cp(),
    )(parts, *ws, *ms, *vs)
    return [res[i * k:(i + 1) * k] for i in range(4)]


def _adam_blocks(parts, ws, ms, vs, name):
    k = len(ws)

    def body(*refs):
        g_refs, w_refs, m_refs, v_refs = refs[0:k], refs[k:2 * k], refs[2 * k:3 * k], refs[3 * k:4 * k]
        outs = refs[4 * k:]
        for i in range(k):
            g = g_refs[i][0].astype(f32)
            for j in range(1, parts[i].shape[0]):
                g = g + g_refs[i][j].astype(f32)
            _adam_update(g[None], w_refs[i], m_refs[i], v_refs[i], outs[i], outs[k + i], outs[2 * k + i], outs[3 * k + i])

    vm = pl.BlockSpec(memory_space=pltpu.VMEM)
    res = pl.pallas_call(
        body, name=name, out_shape=tuple(jax.ShapeDtypeStruct(w.shape, f32) for _ in range(4) for w in ws),
        in_specs=[vm] * (4 * k), out_specs=(vm,) * (4 * k), compiler_params=_cp(),
    )(*parts, *ws, *ms, *vs)
    return [res[i * k:(i + 1) * k] for i in range(4)]


def _adam_plain(gs, ws, ms, vs, name):
    k = len(ws)

    def body(*refs):
        g_refs, w_refs, m_refs, v_refs = refs[0:k], refs[k:2 * k], refs[2 * k:3 * k], refs[3 * k:4 * k]
        outs = refs[4 * k:]
        for i in range(k):
            _adam_update(g_refs[i][...], w_refs[i], m_refs[i], v_refs[i], outs[i], outs[k + i], outs[2 * k + i], outs[3 * k + i])

    vm = pl.BlockSpec(memory_space=pltpu.VMEM)
    res = pl.pallas_call(
        body, name=name, out_shape=tuple(jax.ShapeDtypeStruct(w.shape, f32) for _ in range(4) for w in ws),
        in_specs=[vm] * (4 * k), out_specs=(vm,) * (4 * k), compiler_params=_cp(),
    )(*gs, *ws, *ms, *vs)
    return [res[i * k:(i + 1) * k] for i in range(4)]


def _adam_gate(parts, w2, gb, m_w2, m_gb, v_w2, v_gb):
    per = 128 // 32

    def body(p_ref, w2_ref, gb_ref, mw_ref, mb_ref, vw_ref, vb_ref, *outs):
        g = p_ref[0]
        for j in range(1, parts.shape[0]):
            g = g + p_ref[j]
        for d in range(2):
            gd = jnp.concatenate([g[(d * RANK + r) // per:(d * RANK + r) // per + 1, (r % per) * 32:(r % per + 1) * 32]
                                  for r in range(RANK)], axis=0)
            _adam_update(gd, *(t.at[0, d] for t in (w2_ref, mw_ref, vw_ref) + outs[0:4]))
        last = 2 * RANK // per
        gbv = jnp.concatenate([g[last:last + 1, d * 32:(d + 1) * 32] for d in range(2)], axis=0)
        _adam_update(gbv, *(t.at[0] for t in (gb_ref, mb_ref, vb_ref) + outs[4:8]))

    vm = pl.BlockSpec(memory_space=pltpu.VMEM)
    return pl.pallas_call(
        body, name="adam_gate", out_shape=tuple(jax.ShapeDtypeStruct(w.shape, f32) for w in (w2, gb) for _ in range(4)),
        in_specs=[vm] * 7, out_specs=(vm,) * 8, compiler_params=_cp(),
    )(parts, w2, gb, m_w2, m_gb, v_w2, v_gb)


def _in_proj(x, g, scale, shift, w_pad, tl, comm):
    l = x.shape[0]
    c_in, c_out, c_sems = comm.specs()

    def body(*refs):
        x_ref, g_ref, sc_ref, sh_ref, w_ref = refs[:5]
        cin = refs[5:5 + len(c_in)]
        p_ref, h_ref = refs[5 + len(c_in):7 + len(c_in)]
        cout = refs[7 + len(c_in):7 + len(c_in) + comm.n]
        sems = refs[7 + len(c_in) + comm.n:]

        def compute():
            xv = x_ref[...]
            r = lax.rsqrt(jnp.mean(xv * xv, axis=-1, keepdims=True) + EPS)
            h = (xv * r) * (g_ref[...] * (1.0 + sc_ref[...])) + sh_ref[...]
            hb = h.astype(bf16)
            h_ref[...] = hb
            p_ref[...] = jnp.dot(hb, w_ref[...], preferred_element_type=f32).astype(bf16)

        comm.run(cin, cout, sems, l // tl, compute)

    vec = pl.BlockSpec((1, D), lambda i: (0, 0))
    return pl.pallas_call(
        body, name="in_proj", grid=(l // tl,),
        out_shape=(jax.ShapeDtypeStruct((l, PW), bf16), jax.ShapeDtypeStruct((l, D), bf16)) + tuple(comm.outs),
        in_specs=[pl.BlockSpec((tl, D), lambda i: (i, 0)), vec, vec, vec, pl.BlockSpec((D, PW), lambda i: (0, 0))] + c_in,
        out_specs=(pl.BlockSpec((tl, PW), lambda i: (i, 0)), pl.BlockSpec((tl, D), lambda i: (i, 0))) + tuple(c_out),
        scratch_shapes=c_sems, compiler_params=_cp(("arbitrary",)),
    )(x, g, scale, shift, w_pad, *comm.ins)


def _tri(rev):
    i = lax.broadcasted_iota(jnp.int32, (CH, CH), 0)
    j = lax.broadcasted_iota(jnp.int32, (CH, CH), 1)
    return jnp.where((j >= i) if rev else (j <= i), 1.0, 0.0).astype(f32)


def _head_masks():
    lane = lax.broadcasted_iota(jnp.int32, (1, KW), 1) // HK
    return [jnp.where(lane == h, 1.0, 0.0).astype(f32) for h in range(NH)]


def _block_diag():
    r = lax.broadcasted_iota(jnp.int32, (VW, KW), 0) // HV
    c = lax.broadcasted_iota(jnp.int32, (VW, KW), 1) // HK
    return jnp.where(r == c, 1.0, 0.0).astype(f32)


def _decay(lr, w2, gb, tri, rev):
    logits = _dot(lr, w2) + gb
    a = _log_sigmoid(logits) * (1.0 / TAU)
    c = _dot_hi(tri, a)
    cl = c[0:1, :] if rev else c[CH - 1:CH, :]
    return logits, c, cl


def _state_fwd(k, v, c, cl, st, bd):
    return st * jnp.exp(cl) + bd * _tn(v, k * jnp.exp(cl - c))


def _state_bwd(k, v, c, cl, st0, dst, trit):
    ecl = jnp.exp(cl)
    edec = jnp.exp(cl - c)
    kdec = k * edec
    dv = _nt(kdec, dst)
    dkdec = _dot(v, dst)
    dcl = jnp.sum(dst * st0, axis=0, keepdims=True) * ecl + jnp.sum(dkdec * kdec, axis=0, keepdims=True)
    da = _dot_hi(trit, -dkdec * kdec) + dcl
    return dkdec * edec, dv, da, dst * ecl


def _bdot(a, b):
    return lax.dot_general(a.astype(bf16), b.astype(bf16), (((2,), (1,)), ((0,), (0,))), preferred_element_type=f32)


def _bnt(a, b):
    return lax.dot_general(a.astype(bf16), b.astype(bf16), (((2,), (2,)), ((0,), (0,))), preferred_element_type=f32)


def _btn(a, b):
    return lax.dot_general(a.astype(bf16), b.astype(bf16), (((1,), (1,)), ((0,), (0,))), preferred_element_type=f32)


def _scan_chunks(x, rev):
    nc = x.shape[0]
    hi = x.astype(bf16)
    r1 = x - hi.astype(f32)
    mid = r1.astype(bf16)
    lo = (r1 - mid.astype(f32)).astype(bf16)
    terms = jnp.concatenate([hi, mid, lo], axis=1)
    tri3 = jnp.broadcast_to(jnp.concatenate([_tri(rev)] * 3, axis=1).astype(bf16)[None], (nc, CH, 3 * CH))
    return lax.dot_general(tri3, terms, (((2,), (1,)), ((0,), (0,))), preferred_element_type=f32)


class _Tile:
    pass


def _tile_prep(q_ref, k_ref, lr_ref, w2_ref, gb_ref, rev, nc):
    t = _Tile()
    tg = nc * CH
    t.logits = _dot(lr_ref[...], w2_ref[...]) + gb_ref[...]
    c = _scan_chunks((_log_sigmoid(t.logits) * (1.0 / TAU)).reshape(nc, CH, KW), rev)
    cl = c[:, 0:1, :] if rev else c[:, CH - 1:CH, :]
    k = k_ref[...].astype(f32).reshape(nc, CH, KW)
    t.ec, t.enc, t.edec, t.ecl = jnp.exp(c), jnp.exp(-c), jnp.exp(cl - c), jnp.exp(cl)
    t.qd = q_ref[...].astype(f32).reshape(nc, CH, KW) * t.ec * QSCALE
    t.kd = k * t.enc
    t.kdec = k * t.edec
    hm = _head_masks()
    t.tri4 = jnp.concatenate([_tri(rev)] * NH, axis=0)[None]
    t.qs = jnp.concatenate([t.qd * hm[h] for h in range(NH)], axis=1)
    t.pst = _bnt(t.qs, t.kd) * t.tri4
    return t


def _gla_fwd(p, w2p, gb, s0, rev, tg, name):
    l = p.shape[0]
    nb, nc = l // tg, tg // CH

    def body(q_ref, k_ref, v_ref, lr_ref, w2_ref, gb_ref, s0_ref, o_ref, st_ref, st):
        @pl.when(pl.program_id(0) == 0)
        def _():
            st[...] = s0_ref[...]

        t = _tile_prep(q_ref, k_ref, lr_ref, w2_ref, gb_ref, rev, nc)
        v = v_ref[...].reshape(nc, CH, VW)
        intra = jnp.concatenate([_bdot(t.pst[:, h * CH:(h + 1) * CH], v[:, :, h * HV:(h + 1) * HV]) for h in range(NH)], axis=2)
        bd = _block_diag()
        s = st[...]
        for n in (range(nc - 1, -1, -1) if rev else range(nc)):
            st_ref[n] = s.astype(bf16)
            s = s * t.ecl[n] + bd * _tn(v[n], t.kdec[n])
        st[...] = s
        o_ref[...] = (_bnt(t.qd, st_ref[...]) + intra).reshape(tg, VW).astype(bf16)

    blk = (lambda i: nb - 1 - i) if rev else (lambda i: i)
    return pl.pallas_call(
        body, name=name, grid=(nb,),
        out_shape=(jax.ShapeDtypeStruct((l, VW), bf16), jax.ShapeDtypeStruct((l // CH, VW, KW), bf16)),
        in_specs=[pl.BlockSpec((tg, KW), lambda i: (blk(i), PQ // KW)), pl.BlockSpec((tg, KW), lambda i: (blk(i), PK // KW)),
                  pl.BlockSpec((tg, VW), lambda i: (blk(i), PV // VW)), pl.BlockSpec((tg, LRW), lambda i: (blk(i), PLR // LRW)),
                  pl.BlockSpec((LRW, KW), lambda i: (0, 0)), pl.BlockSpec((1, KW), lambda i: (0, 0)),
                  pl.BlockSpec((VW, KW), lambda i: (0, 0))],
        out_specs=(pl.BlockSpec((tg, VW), lambda i: (blk(i), 0)), pl.BlockSpec((nc, VW, KW), lambda i: (blk(i), 0, 0))),
        scratch_shapes=[pltpu.VMEM((VW, KW), f32)],
        compiler_params=_cp(("arbitrary",)),
    )(p, p, p, p, w2p, gb, s0)


def _gla_bwd(p, do, states, w2p, gb, prev, rev, tg, name, comm):
    l = p.shape[0]
    nb, nc = l // tg, tg // CH
    out_dt = bf16
    c_in, c_out, c_sems = comm.specs()

    def body(*refs):
        q_ref, k_ref, v_ref, lr_ref, do_ref, st_ref, w2_ref, gb_ref = refs[:8]
        refs = refs[8:]
        if prev is not None:
            pq_ref, pl_ref = refs[:2]
            refs = refs[2:]
        cin, refs = refs[:len(c_in)], refs[len(c_in):]
        dqkv_ref, dlr_ref, dw2_ref, dgb_ref, ds0_ref = refs[:5]
        cout, dst, ds_buf, sems = refs[5:5 + comm.n], refs[5 + comm.n], refs[6 + comm.n], refs[7 + comm.n:]
        comm.run(cin, cout, sems, nb, lambda: compute(q_ref, k_ref, v_ref, lr_ref, do_ref, st_ref, w2_ref, gb_ref,
                                                      pq_ref if prev is not None else None, pl_ref if prev is not None else None,
                                                      dqkv_ref, dlr_ref, dw2_ref, dgb_ref, ds0_ref, dst, ds_buf))

    def compute(q_ref, k_ref, v_ref, lr_ref, do_ref, st_ref, w2_ref, gb_ref, pq_ref, pl_ref,
                dqkv_ref, dlr_ref, dw2_ref, dgb_ref, ds0_ref, dst, ds_buf):
        @pl.when(pl.program_id(0) == 0)
        def _():
            dst[...] = jnp.zeros_like(dst)
            dw2_ref[...] = jnp.zeros_like(dw2_ref)
            dgb_ref[...] = jnp.zeros_like(dgb_ref)

        t = _tile_prep(q_ref, k_ref, lr_ref, w2_ref, gb_ref, rev, nc)
        hm = _head_masks()
        v = v_ref[...].reshape(nc, CH, VW)
        do = do_ref[...].reshape(nc, CH, VW)
        heads = lambda a, h: a[:, :, h * HV:(h + 1) * HV]
        dpst = jnp.concatenate([_bnt(heads(do, h), heads(v, h)) for h in range(NH)], axis=1) * t.tri4
        dv = jnp.concatenate([_btn(t.pst[:, h * CH:(h + 1) * CH], heads(do, h)) for h in range(NH)], axis=2)
        dqd = _bdot(do, st_ref[...])
        for h in range(NH):
            dqd = dqd + hm[h] * _bdot(dpst[:, h * CH:(h + 1) * CH], t.kd)
        dkd = _btn(dpst, t.qs)
        bd = _block_diag()
        d = dst[...]
        for n in (range(nc) if rev else range(nc - 1, -1, -1)):
            ds_buf[n] = d
            d = d * t.ecl[n] + bd * _tn(do[n], t.qd[n])
        dst[...] = d
        ds0_ref[...] = d
        ds = ds_buf[...]
        dv = dv + _bnt(t.kdec, ds)
        dkdec = _bdot(v, ds)
        dcl = jnp.sum(ds * st_ref[...].astype(f32), axis=1, keepdims=True) * t.ecl + jnp.sum(dkdec * t.kdec, axis=1, keepdims=True)
        dc = dqd * t.qd - dkd * t.kd - dkdec * t.kdec
        da = _scan_chunks(dc, not rev) + dcl
        dq = dqd * t.ec * QSCALE
        dk = dkd * t.enc + dkdec * t.edec
        dlog = da.reshape(tg, KW) * _sigmoid(-t.logits) * (1.0 / TAU)
        dlr = _nt(dlog, w2_ref[...])
        dw2_ref[...] += _tn(lr_ref[...], dlog)
        dgb_ref[...] += jnp.sum(dlog, axis=0, keepdims=True)
        dqkv = jnp.concatenate([dq, dk, dv], axis=2).reshape(tg, 2 * KW + VW)
        if prev is not None:
            dqkv = dqkv + pq_ref[...]
            dlr = dlr + pl_ref[...]
        dqkv_ref[...] = dqkv.astype(out_dt)
        dlr_ref[...] = dlr.astype(out_dt)

    blk = (lambda i: i) if rev else (lambda i: nb - 1 - i)
    in_specs = [pl.BlockSpec((tg, KW), lambda i: (blk(i), PQ // KW)), pl.BlockSpec((tg, KW), lambda i: (blk(i), PK // KW)),
                pl.BlockSpec((tg, VW), lambda i: (blk(i), PV // VW)), pl.BlockSpec((tg, LRW), lambda i: (blk(i), PLR // LRW)),
                pl.BlockSpec((tg, VW), lambda i: (blk(i), 0)), pl.BlockSpec((nc, VW, KW), lambda i: (blk(i), 0, 0)),
                pl.BlockSpec((LRW, KW), lambda i: (0, 0)), pl.BlockSpec((1, KW), lambda i: (0, 0))]
    args = [p, p, p, p, do, states, w2p, gb]
    if prev is not None:
        in_specs += [pl.BlockSpec((tg, 2 * KW + VW), lambda i: (blk(i), 0)), pl.BlockSpec((tg, LRW), lambda i: (blk(i), 0))]
        args += list(prev)
    return pl.pallas_call(
        body, name=name, grid=(nb,),
        out_shape=(jax.ShapeDtypeStruct((l, 2 * KW + VW), out_dt), jax.ShapeDtypeStruct((l, LRW), out_dt),
                   jax.ShapeDtypeStruct((LRW, KW), f32), jax.ShapeDtypeStruct((1, KW), f32), jax.ShapeDtypeStruct((VW, KW), f32))
        + tuple(comm.outs),
        in_specs=in_specs + c_in,
        out_specs=(pl.BlockSpec((tg, 2 * KW + VW), lambda i: (blk(i), 0)), pl.BlockSpec((tg, LRW), lambda i: (blk(i), 0)),
                   pl.BlockSpec((LRW, KW), lambda i: (0, 0)), pl.BlockSpec((1, KW), lambda i: (0, 0)),
                   pl.BlockSpec((VW, KW), lambda i: (0, 0))) + tuple(c_out),
        scratch_shapes=[pltpu.VMEM((VW, KW), f32), pltpu.VMEM((nc, VW, KW), f32)] + c_sems,
        compiler_params=_cp(("arbitrary",)),
    )(*args, *comm.ins)


def _ctx_hidden(ctx_ref, g_ref, sc_ref, sh_ref):
    xv = ctx_ref[...]
    r = lax.rsqrt(jnp.mean(xv * xv, axis=-1, keepdims=True) + EPS)
    xn = xv * r
    return xn, xn * (g_ref[...] * (1.0 + sc_ref[...])) + sh_ref[...]


_CTX_W_SPECS = [pl.BlockSpec((D, KW), lambda i: (0, PK // KW)), pl.BlockSpec((D, VW), lambda i: (0, PV // VW)),
                pl.BlockSpec((D, LRW), lambda i: (0, PLR // LRW))]


def _ctx_fwd(ctx, g, scale, shift, w_pad, w2f, w2b, gbf, gbb):
    ncc = CTX // CH

    def body(ctx_ref, g_ref, sc_ref, sh_ref, wk_ref, wv_ref, wl_ref, w2f_ref, w2b_ref, gbf_ref, gbb_ref, sf_ref, sb_ref):
        _, hc = _ctx_hidden(ctx_ref, g_ref, sc_ref, sh_ref)
        k, v, lr = _dot(hc, wk_ref[...]), _dot(hc, wv_ref[...]), _dot(hc, wl_ref[...])
        bd = _block_diag()
        for rev, w2_ref, gb_ref, out in ((False, w2f_ref, gbf_ref, sf_ref), (True, w2b_ref, gbb_ref, sb_ref)):
            tri = _tri(rev)
            st = jnp.zeros((VW, KW), f32)
            for j in (range(ncc - 1, -1, -1) if rev else range(ncc)):
                rows = slice(j * CH, (j + 1) * CH)
                _, c, cl = _decay(lr[rows], w2_ref[...], gb_ref[...], tri, rev)
                st = _state_fwd(k[rows], v[rows], c, cl, st, bd)
            out[...] = st

    vec = pl.BlockSpec((1, D), lambda i: (0, 0))
    w2s = pl.BlockSpec((LRW, KW), lambda i: (0, 0))
    gbs = pl.BlockSpec((1, KW), lambda i: (0, 0))
    sts = pl.BlockSpec((VW, KW), lambda i: (0, 0))
    return pl.pallas_call(
        body, name="ctx_fwd", grid=(1,), out_shape=(jax.ShapeDtypeStruct((VW, KW), f32),) * 2,
        in_specs=[pl.BlockSpec((CTX, D), lambda i: (0, 0)), vec, vec, vec] + _CTX_W_SPECS + [w2s, w2s, gbs, gbs],
        out_specs=(sts, sts), compiler_params=_cp(("arbitrary",)),
    )(ctx, g, scale, shift, w_pad, w_pad, w_pad, w2f, w2b, gbf, gbb)


def _ctx_bwd(ctx, g, scale, shift, w_pad, w2f, w2b, gbf, gbb, dsf, dsb, dw2_scans, dgb_scans):
    ncc = CTX // CH
    per = 128 // 32

    def body(ctx_ref, g_ref, sc_ref, sh_ref, wk_ref, wv_ref, wl_ref, w2f_ref, w2b_ref, gbf_ref, gbb_ref, dsf_ref, dsb_ref,
             sw2f_ref, sw2b_ref, sgbf_ref, sgbb_ref, dwk_ref, dwv_ref, dwl_ref, dmod_ref, dg_ref, dw2_ref, dgb_ref, dgt_ref):
        dgt_ref[...] = jnp.zeros_like(dgt_ref)
        xn, hc = _ctx_hidden(ctx_ref, g_ref, sc_ref, sh_ref)
        k, v, lr = _dot(hc, wk_ref[...]), _dot(hc, wv_ref[...]), _dot(hc, wl_ref[...])
        bd = _block_diag()
        dk_rows, dv_rows, dl_rows = [None] * ncc, [None] * ncc, [None] * ncc
        for d, (rev, w2_ref, gb_ref, ds_ref) in enumerate(((False, w2f_ref, gbf_ref, dsf_ref), (True, w2b_ref, gbb_ref, dsb_ref))):
            tri = _tri(rev)
            order = list(range(ncc - 1, -1, -1) if rev else range(ncc))
            st, saved = jnp.zeros((VW, KW), f32), {}
            for j in order:
                rows = slice(j * CH, (j + 1) * CH)
                logits, c, cl = _decay(lr[rows], w2_ref[...], gb_ref[...], tri, rev)
                saved[j] = (logits, c, cl, st)
                st = _state_fwd(k[rows], v[rows], c, cl, st, bd)
            dst = ds_ref[...]
            dw2 = jnp.zeros((LRW, KW), f32)
            dgb = jnp.zeros((1, KW), f32)
            for j in reversed(order):
                rows = slice(j * CH, (j + 1) * CH)
                logits, c, cl, st0 = saved[j]
                dk, dv, da, dst = _state_bwd(k[rows], v[rows], c, cl, st0, dst, _tri(not rev))
                dlog = da * _sigmoid(-logits) * (1.0 / TAU)
                dl = _nt(dlog, w2_ref[...])
                dw2 = dw2 + _tn(lr[rows], dlog)
                dgb = dgb + jnp.sum(dlog, axis=0, keepdims=True)
                dk_rows[j] = dk if dk_rows[j] is None else dk_rows[j] + dk
                dv_rows[j] = dv if dv_rows[j] is None else dv_rows[j] + dv
                dl_rows[j] = dl if dl_rows[j] is None else dl_rows[j] + dl
            dw2_ref[d] = dw2
            dgb_ref[d] = dgb
            dw2 = dw2 + (sw2f_ref, sw2b_ref)[d][...]
            dgb = dgb + (sgbf_ref, sgbb_ref)[d][...]
            for dev in range(NDEV):
                cols = slice(dev * 32, (dev + 1) * 32)
                for r in range(RANK):
                    q = d * RANK + r
                    dgt_ref[dev, q // per:q // per + 1, (q % per) * 32:(q % per + 1) * 32] = dw2[q:q + 1, cols]
                dgt_ref[dev, 2 * RANK // per:2 * RANK // per + 1, d * 32:(d + 1) * 32] = dgb[:, cols]
        dk, dv, dl = (jnp.concatenate(t, axis=0) for t in (dk_rows, dv_rows, dl_rows))
        dwk_ref[...] = _tn(hc, dk)
        dwv_ref[...] = _tn(hc, dv)
        dwl_ref[...] = _tn(hc, dl)
        dh = _nt(dk, wk_ref[...]) + _nt(dv, wv_ref[...]) + _nt(dl, wl_ref[...])
        gx = dh * xn
        dmod_ref[:, 0:D] = jnp.sum(dh, axis=0, keepdims=True)
        dmod_ref[:, D:2 * D] = jnp.sum(gx, axis=0, keepdims=True) * g_ref[...]
        dg_ref[...] = jnp.sum(gx, axis=0, keepdims=True) * (1.0 + sc_ref[...])

    vec = pl.BlockSpec((1, D), lambda i: (0, 0))
    w2s = pl.BlockSpec((LRW, KW), lambda i: (0, 0))
    gbs = pl.BlockSpec((1, KW), lambda i: (0, 0))
    sts = pl.BlockSpec((VW, KW), lambda i: (0, 0))
    full = lambda *s: pl.BlockSpec(s, lambda i: (0,) * len(s))
    return pl.pallas_call(
        body, name="ctx_bwd", grid=(1,),
        out_shape=(jax.ShapeDtypeStruct((D, KW), f32), jax.ShapeDtypeStruct((D, VW), f32), jax.ShapeDtypeStruct((D, LRW), f32),
                   jax.ShapeDtypeStruct((1, 2 * D), f32), jax.ShapeDtypeStruct((1, D), f32),
                   jax.ShapeDtypeStruct((2, LRW, KW), f32), jax.ShapeDtypeStruct((2, 1, KW), f32),
                   jax.ShapeDtypeStruct((NDEV, 2 * RANK // per + 1, 128), f32)),
        in_specs=[pl.BlockSpec((CTX, D), lambda i: (0, 0)), vec, vec, vec] + _CTX_W_SPECS + [w2s, w2s, gbs, gbs, sts, sts, w2s, w2s, gbs, gbs],
        out_specs=(full(D, KW), full(D, VW), full(D, LRW), full(1, 2 * D), full(1, D), full(2, LRW, KW), full(2, 1, KW),
                   full(NDEV, 2 * RANK // per + 1, 128)),
        compiler_params=_cp(("arbitrary",)),
    )(ctx, g, scale, shift, w_pad, w_pad, w_pad, w2f, w2b, gbf, gbb, dsf, dsb, *dw2_scans, *dgb_scans)


def _layernorm(va, g, b):
    mu = jnp.mean(va, axis=-1, keepdims=True)
    xc = va - mu
    rstd = lax.rsqrt(jnp.mean(xc * xc, axis=-1, keepdims=True) + EPS)
    vhat = xc * rstd
    return vhat, rstd, vhat * g + b


MIX_PIECES = 8


def _mix_fwd(p, ln_g, ln_b, ws, bs_t):
    l = p.shape[0]
    half = AW // 2
    rp = l // MIX_PIECES
    cpp = rp // ACH

    def body(p_ref, g_ref, b_ref, ws_ref, bs_ref, sv_hbm, va_buf, col_buf, sv_buf, in_sems, out_sems):
        piece = lambda i: pl.ds(pl.multiple_of(i * rp, rp), rp)
        load = lambda i: pltpu.make_async_copy(p_ref.at[piece(i), pl.ds(PVA, AW)], va_buf.at[piece(i)], in_sems.at[i])
        store = lambda i: pltpu.make_async_copy(sv_buf.at[piece(i)], sv_hbm.at[piece(i)], out_sems.at[i])
        for i in range(MIX_PIECES):
            load(i).start()

        def rows_piece(i, carry):
            load(i).wait()
            for j in range(cpp):
                rows = pl.ds(pl.multiple_of(i * rp + j * ACH, ACH), ACH)
                _, _, vn = _layernorm(va_buf[rows, :].astype(f32), g_ref[...], b_ref[...])
                for gi in range(2):
                    sl = slice(gi * ACH, (gi + 1) * ACH)
                    sv_buf[rows, sl] = (_dot(ws_ref[gi], vn[:, sl]) + bs_ref[:, gi:gi + 1]).astype(bf16)
                col_buf[0, rows, :] = vn[:, half:half + ACH]
                col_buf[1, rows, :] = vn[:, half + ACH:]
            return carry

        lax.fori_loop(0, MIX_PIECES, rows_piece, 0)

        def cols_step(cidx, carry):
            rows = pl.ds(cidx, ACH, stride=GW)
            for gi in range(2, 4):
                col_buf[gi - 2, rows, :] = _dot(ws_ref[gi], col_buf[gi - 2, rows, :]) + bs_ref[:, gi:gi + 1]
            return carry

        lax.fori_loop(0, GW, cols_step, 0, unroll=8)

        def out_piece(i, carry):
            for j in range(cpp):
                rows = pl.ds(pl.multiple_of(i * rp + j * ACH, ACH), ACH)
                sv_buf[rows, half:half + ACH] = col_buf[0, rows, :].astype(bf16)
                sv_buf[rows, half + ACH:] = col_buf[1, rows, :].astype(bf16)
            store(i).start()
            return carry

        lax.fori_loop(0, MIX_PIECES, out_piece, 0)
        for i in range(MIX_PIECES):
            store(i).wait()

    vm = pl.BlockSpec(memory_space=pltpu.VMEM)
    hbm = pl.BlockSpec(memory_space=pl.ANY)
    return pl.pallas_call(
        body, name="mix_fwd", out_shape=jax.ShapeDtypeStruct((l, AW), bf16),
        in_specs=[hbm, vm, vm, vm, vm], out_specs=hbm,
        scratch_shapes=[pltpu.VMEM((l, AW), bf16), pltpu.VMEM((2, l, ACH), f32), pltpu.VMEM((l, AW), bf16),
                        pltpu.SemaphoreType.DMA((MIX_PIECES,)), pltpu.SemaphoreType.DMA((MIX_PIECES,))],
        compiler_params=_cp(),
    )(p, ln_g, ln_b, ws, bs_t)


def _mix_bwd(p, dsv, ln_g, ln_b, ws_t):
    l = p.shape[0]
    half = AW // 2
    rp = l // MIX_PIECES
    cpp = rp // ACH

    def body(p_ref, dsv_hbm, g_ref, b_ref, wst_ref, dva_hbm, dws_ref, dbs_ref, dg_ref, db_ref,
             va_buf, dsv_buf, vn_col, ds_col, dva_buf, va_sems, ds_sems, out_sems):
        piece = lambda i: pl.ds(pl.multiple_of(i * rp, rp), rp)
        load_va = lambda i: pltpu.make_async_copy(p_ref.at[piece(i), pl.ds(PVA, AW)], va_buf.at[piece(i)], va_sems.at[i])
        load_ds = lambda i: pltpu.make_async_copy(dsv_hbm.at[piece(i)], dsv_buf.at[piece(i)], ds_sems.at[i])
        store = lambda i: pltpu.make_async_copy(dva_buf.at[piece(i)], dva_hbm.at[piece(i)], out_sems.at[i])
        for i in range(MIX_PIECES):
            load_va(i).start()
            load_ds(i).start()
        dws_ref[...] = jnp.zeros_like(dws_ref)
        dbs_ref[...] = jnp.zeros_like(dbs_ref)
        dg_ref[...] = jnp.zeros_like(dg_ref)
        db_ref[...] = jnp.zeros_like(db_ref)

        def rows_piece(i, carry):
            load_va(i).wait()
            load_ds(i).wait()
            for j in range(cpp):
                rows = pl.ds(pl.multiple_of(i * rp + j * ACH, ACH), ACH)
                _, _, vn = _layernorm(va_buf[rows, :].astype(f32), g_ref[...], b_ref[...])
                ds = dsv_buf[rows, :].astype(f32)
                for gi in range(2):
                    sl = slice(gi * ACH, (gi + 1) * ACH)
                    dws_ref[gi] += _nt(ds[:, sl], vn[:, sl])
                    dbs_ref[gi] += ds[:, sl]
                for gi in range(2):
                    sl = slice(half + gi * ACH, half + (gi + 1) * ACH)
                    vn_col[gi, rows, :] = vn[:, sl]
                    ds_col[gi, rows, :] = ds[:, sl]
            return carry

        lax.fori_loop(0, MIX_PIECES, rows_piece, 0)

        def cols_step(cidx, carry):
            rows = pl.ds(cidx, ACH, stride=GW)
            for gi in range(2, 4):
                ds = ds_col[gi - 2, rows, :]
                dws_ref[gi] += _nt(ds, vn_col[gi - 2, rows, :])
                dbs_ref[gi] += ds
                ds_col[gi - 2, rows, :] = _dot(wst_ref[gi], ds)
            return carry

        lax.fori_loop(0, GW, cols_step, 0, unroll=8)

        def out_piece(i, carry):
            for j in range(cpp):
                rows = pl.ds(pl.multiple_of(i * rp + j * ACH, ACH), ACH)
                vhat, rstd, _ = _layernorm(va_buf[rows, :].astype(f32), g_ref[...], b_ref[...])
                ds = dsv_buf[rows, :].astype(f32)
                dvn = jnp.concatenate([_dot(wst_ref[0], ds[:, 0:ACH]), _dot(wst_ref[1], ds[:, ACH:half]),
                                       ds_col[0, rows, :], ds_col[1, rows, :]], axis=1)
                dg_ref[...] += jnp.sum(dvn * vhat, axis=0, keepdims=True)
                db_ref[...] += jnp.sum(dvn, axis=0, keepdims=True)
                dvh = dvn * g_ref[...]
                dva = rstd * (dvh - jnp.mean(dvh, axis=-1, keepdims=True) - vhat * jnp.mean(dvh * vhat, axis=-1, keepdims=True))
                dva_buf[rows, :] = dva.astype(bf16)
            store(i).start()
            return carry

        lax.fori_loop(0, MIX_PIECES, out_piece, 0)
        for i in range(MIX_PIECES):
            store(i).wait()

    vm = pl.BlockSpec(memory_space=pltpu.VMEM)
    hbm = pl.BlockSpec(memory_space=pl.ANY)
    dma = lambda: pltpu.SemaphoreType.DMA((MIX_PIECES,))
    return pl.pallas_call(
        body, name="mix_bwd",
        out_shape=(jax.ShapeDtypeStruct((l, AW), bf16), jax.ShapeDtypeStruct((4, ACH, ACH), f32), jax.ShapeDtypeStruct((4, ACH, ACH), f32),
                   jax.ShapeDtypeStruct((1, AW), f32), jax.ShapeDtypeStruct((1, AW), f32)),
        in_specs=[hbm, hbm, vm, vm, vm], out_specs=(hbm, vm, vm, vm, vm),
        scratch_shapes=[pltpu.VMEM((l, AW), bf16), pltpu.VMEM((l, AW), bf16), pltpu.VMEM((2, l, ACH), f32), pltpu.VMEM((2, l, ACH), f32),
                        pltpu.VMEM((l, AW), bf16), dma(), dma(), dma()],
        compiler_params=_cp(),
    )(p, dsv, ln_g, ln_b, ws_t)


def _mid(x, tgt, p, o_f, o_b, sv, gate, gf, gb_norm, w_pa, w_pb, w_out, tl):
    l = x.shape[0]

    def body(x_ref, t_ref, zb_ref, ua_ref, za_ref, g1_ref, g2_ref, of_ref, ob_ref, sv_ref, gate_ref, gf_ref, gbn_ref,
             wpa_ref, wpb_ref, wout_ref,
             dx1_ref, dzbua_ref, dzag_ref, dsv_ref, do_ref, dwout_bf, dwpa_bf, dwpb_bf, dgf_ref, dgate_ref, dgbn_ref, loss_ref,
             dwout_ref, dwpa_ref, dwpb_ref):
        @pl.when(pl.program_id(0) == 0)
        def _():
            for r in (dwout_ref, dwpa_ref, dwpb_ref, dgf_ref, dgate_ref, dgbn_ref, loss_ref):
                r[...] = jnp.zeros_like(r)

        o = of_ref[...].astype(f32) + ob_ref[...].astype(f32)
        rr = jnp.concatenate(
            [jnp.broadcast_to(lax.rsqrt(jnp.mean(o[:, h * HV:(h + 1) * HV] ** 2, axis=-1, keepdims=True) + EPS), (tl, HV))
             for h in range(NH)], axis=1)
        ohat = o * rr
        on = ohat * gbn_ref[...]
        szb, dszb = _silu_and_grad(zb_ref[...].astype(f32))
        tb = on * szb
        u = ua_ref[...].astype(f32)
        svv = sv_ref[...].astype(f32)
        sza, dsza = _silu_and_grad(za_ref[...].astype(f32))
        ta = u * svv * sza
        ya = _dot(ta, wpa_ref[...])
        yb = _dot(tb, wpb_ref[...])
        g1 = _sigmoid(g1_ref[...].astype(f32))
        g2 = _sigmoid(g2_ref[...].astype(f32))
        m = g1 * ya + g2 * yb
        y2 = _dot(m, wout_ref[...])
        x1 = x_ref[...] + gate_ref[...] * y2
        r1 = lax.rsqrt(jnp.mean(x1 * x1, axis=-1, keepdims=True) + EPS)
        x1n = x1 * r1
        err = x1n * gf_ref[...] - t_ref[...]
        loss_ref[...] += jnp.sum(jnp.sum(err * err, axis=-1, keepdims=True), axis=0, keepdims=True) * (0.5 / D)
        dout = err * (1.0 / D)
        dgf_ref[...] += jnp.sum(dout * x1n, axis=0, keepdims=True)
        dx1n = dout * gf_ref[...]
        dx1 = r1 * (dx1n - x1n * jnp.mean(dx1n * x1n, axis=-1, keepdims=True))
        dx1_ref[...] = dx1
        dgate_ref[...] += jnp.sum(dx1 * y2, axis=0, keepdims=True)
        dy2 = dx1 * gate_ref[...]
        dwout_ref[...] += _tn(m, dy2)
        dm = _nt(dy2, wout_ref[...])
        dya = dm * g1
        dyb = dm * g2
        dzag_ref[:, AW:AW + D] = (dm * ya * g1 * (1.0 - g1)).astype(bf16)
        dzag_ref[:, AW + D:] = (dm * yb * g2 * (1.0 - g2)).astype(bf16)
        dwpa_ref[...] += _tn(ta, dya)
        dta = _nt(dya, wpa_ref[...])
        dzbua_ref[:, AW:] = (dta * svv * sza).astype(bf16)
        dsv_ref[...] = (dta * u * sza).astype(bf16)
        dzag_ref[:, 0:AW] = (dta * u * svv * dsza).astype(bf16)
        dwpb_ref[...] += _tn(tb, dyb)
        dtb = _nt(dyb, wpb_ref[...])
        don = dtb * szb
        dzbua_ref[:, 0:AW] = (dtb * on * dszb).astype(bf16)
        dgbn_ref[...] += jnp.sum(don * ohat, axis=0, keepdims=True)
        doh = don * gbn_ref[...]
        prod = doh * ohat
        mh = jnp.concatenate(
            [jnp.broadcast_to(jnp.mean(prod[:, h * HV:(h + 1) * HV], axis=-1, keepdims=True), (tl, HV)) for h in range(NH)], axis=1)
        do_ref[...] = (rr * (doh - ohat * mh)).astype(bf16)

        @pl.when(pl.program_id(0) == l // tl - 1)
        def _():
            for acc, out in ((dwout_ref, dwout_bf), (dwpa_ref, dwpa_bf), (dwpb_ref, dwpb_bf)):
                out[...] = acc[...].astype(bf16)

    row = lambda w, j: pl.BlockSpec((tl, w), lambda i, j=j: (i, j))
    full = lambda *s: pl.BlockSpec(s, lambda i: (0,) * len(s))
    return pl.pallas_call(
        body, name="mid", grid=(l // tl,),
        out_shape=(jax.ShapeDtypeStruct((l, D), f32), jax.ShapeDtypeStruct((l, 2 * AW), bf16), jax.ShapeDtypeStruct((l, AW + 2 * D), bf16),
                   jax.ShapeDtypeStruct((l, AW), bf16), jax.ShapeDtypeStruct((l, VW), bf16),
                   jax.ShapeDtypeStruct((D, D), bf16), jax.ShapeDtypeStruct((AW, D), bf16), jax.ShapeDtypeStruct((VW, D), bf16),
                   jax.ShapeDtypeStruct((1, D), f32), jax.ShapeDtypeStruct((1, D), f32), jax.ShapeDtypeStruct((1, VW), f32),
                   jax.ShapeDtypeStruct((1, 1), f32)),
        scratch_shapes=[pltpu.VMEM((D, D), f32), pltpu.VMEM((AW, D), f32), pltpu.VMEM((VW, D), f32)],
        in_specs=[row(D, 0), row(D, 0), row(AW, PZB // AW), row(AW, PUA // AW), row(AW, PZA // AW), row(D, PG1 // D), row(D, PG2 // D),
                  row(VW, 0), row(VW, 0), row(AW, 0), full(1, D), full(1, D), full(1, VW), full(AW, D), full(VW, D), full(D, D)],
        out_specs=(row(D, 0), row(2 * AW, 0), row(AW + 2 * D, 0), row(AW, 0), row(VW, 0),
                   full(D, D), full(AW, D), full(VW, D), full(1, D), full(1, D), full(1, VW), full(1, 1)),
        compiler_params=_cp(("arbitrary",)),
    )(x, tgt, p, p, p, p, p, o_f, o_b, sv, gate, gf, gb_norm, w_pa, w_pb, w_out)


def _in_bwd(x, dx1, dqkv, dzbua, dva, dzag, dlr, w_pad, g, scale, tl, comm):
    l = x.shape[0]
    c_in, c_out, c_sems = comm.specs()

    def body(*refs):
        cin = refs[14:14 + len(c_in)]
        outs = refs[14 + len(c_in):]
        comm.run(cin, outs[4:4 + comm.n], outs[4 + comm.n:], l // tl, lambda: compute(*refs[:14], *outs[:4]))

    def compute(x_ref, dx1_ref, a_ref, b_ref, c_ref, e_ref, lr_ref, wa_ref, wb_ref, wc_ref, we_ref, wl_ref, g_ref, sc_ref,
                gx_ref, dsh_ref, dsc_ref, dg_ref):
        @pl.when(pl.program_id(0) == 0)
        def _():
            for r in (dsh_ref, dsc_ref, dg_ref):
                r[...] = jnp.zeros_like(r)

        dh = (_nt(a_ref[...], wa_ref[...]) + _nt(b_ref[...], wb_ref[...]) + _nt(c_ref[...], wc_ref[...]) + _nt(e_ref[...], we_ref[...])
              + _nt(lr_ref[...], wl_ref[...]))
        xv = x_ref[...]
        r = lax.rsqrt(jnp.mean(xv * xv, axis=-1, keepdims=True) + EPS)
        xn = xv * r
        gxn = jnp.sum(dh * xn, axis=0, keepdims=True)
        dsh_ref[...] += jnp.sum(dh, axis=0, keepdims=True)
        dsc_ref[...] += gxn * g_ref[...]
        dg_ref[...] += gxn * (1.0 + sc_ref[...])
        dxn = dh * (g_ref[...] * (1.0 + sc_ref[...]))
        gx_ref[...] = dx1_ref[...] + r * (dxn - xn * jnp.mean(dxn * xn, axis=-1, keepdims=True))

    row = lambda w: pl.BlockSpec((tl, w), lambda i: (i, 0))
    wcol = lambda w, j: pl.BlockSpec((D, w), lambda i, j=j: (0, j))
    vec = pl.BlockSpec((1, D), lambda i: (0, 0))
    return pl.pallas_call(
        body, name="in_bwd", grid=(l // tl,),
        out_shape=(jax.ShapeDtypeStruct((l, D), f32),) + (jax.ShapeDtypeStruct((1, D), f32),) * 3 + tuple(comm.outs),
        in_specs=[row(D), row(D), row(2 * KW + VW), row(2 * AW), row(AW), row(AW + 2 * D), row(LRW),
                  wcol(2 * KW + VW, 0), wcol(2 * AW, PZB // (2 * AW)), wcol(AW, PVA // AW), wcol(AW + 2 * D, PZA // (AW + 2 * D)),
                  wcol(LRW, PLR // LRW), vec, vec] + c_in,
        out_specs=(row(D), vec, vec, vec) + tuple(c_out), scratch_shapes=c_sems,
        compiler_params=_cp(("arbitrary",)),
    )(x, dx1, dqkv, dzbua, dva, dzag, dlr, w_pad, w_pad, w_pad, w_pad, w_pad, g, scale, *comm.ins)


def _tn_matmul(a, bs, tl, name, comm=None, pack=None):
    l, m = a.shape
    k = len(bs)
    comm = comm or _Comm([], [], None)
    c_in, c_out, c_sems = comm.specs()
    acc_shapes = [(m, b.shape[1]) for b in bs]
    n_extra = len(pack.extra) if pack else 0
    n_res = len(pack.outs) if pack else k

    def body(a_ref, *refs):
        b_refs, refs = refs[:k], refs[k:]
        extra, refs = refs[:n_extra], refs[n_extra:]
        cin, refs = refs[:len(c_in)], refs[len(c_in):]
        res, refs = refs[:n_res], refs[n_res:]
        cout, refs = refs[:comm.n], refs[comm.n:]
        accs, sems = (refs[:k], refs[k:]) if pack else (res, refs)

        def compute():
            @pl.when(pl.program_id(0) == 0)
            def _():
                for o_ref in accs:
                    o_ref[...] = jnp.zeros_like(o_ref)

            av = a_ref[...]
            for b_ref, o_ref in zip(b_refs, accs):
                o_ref[...] += _tn(av, b_ref[...])
            if pack:
                pl.when(pl.program_id(0) == l // tl - 1)(lambda: pack.fn(accs, extra, res))

        comm.run(cin, cout, sems, l // tl, compute)

    full = lambda shape: pl.BlockSpec(shape, lambda i: (0,) * len(shape))
    res_shapes = list(pack.outs) if pack else [jax.ShapeDtypeStruct(sh, f32) for sh in acc_shapes]
    return pl.pallas_call(
        body, name=name, grid=(l // tl,), out_shape=tuple(res_shapes) + tuple(comm.outs),
        in_specs=[pl.BlockSpec((tl, m), lambda i: (i, 0))] + [pl.BlockSpec((tl, b.shape[1]), lambda i: (i, 0)) for b in bs]
        + [full(e.shape) for e in (pack.extra if pack else [])] + c_in,
        out_specs=tuple(full(r.shape) for r in res_shapes) + tuple(c_out),
        scratch_shapes=([pltpu.VMEM(sh, f32) for sh in acc_shapes] if pack else []) + c_sems, compiler_params=_cp(("arbitrary",)),
    )(a, *bs, *(pack.extra if pack else []), *comm.ins)


EARLY_A_ROWS = 512


class _NoExchange:
    def __init__(self, w_pa, w_pb, w_out):
        self.weights = (w_pa, w_pb, w_out)

    def gather_proj(self):
        return _Comm([], [], None)

    def proj_weights(self, got):
        return self.weights

    def first(self, dw_out, dw_pa, dw_pb, small):
        return _Comm([], [], None)

    def early_a(self, blocks):
        return _Comm([], [], None)

    def early_b(self, blocks):
        return _Comm([], [], None)

    def late(self, blocks, dgt):
        return _Comm([], [], None)


class _Exchanges:
    def __init__(self, pa, pb, wo):
        self.shards = (pa, pb, wo)

    def gather_proj(self):
        def plan(i, o):
            srcs = [lambda j, r=r: r for r in i]
            dsts = [lambda j: o[0].at[:, _lanes(j)], lambda j: o[1].at[:, _lanes(j)], lambda j: o[2].at[j]]
            return srcs, dsts, None
        sds = jax.ShapeDtypeStruct
        return _Comm(self.shards, [sds((AW, D), bf16), sds((VW, D), bf16), sds((NDEV, 128, D), bf16)], plan)

    def proj_weights(self, got):
        return got[0], got[1], got[2].reshape(D, D)

    def first(self, dw_out, dw_pa, dw_pb, small):
        def plan(i, o):
            srcs = [lambda j: i[0].at[j], lambda j: i[1].at[:, _lanes(j)], lambda j: i[2].at[:, _lanes(j)], lambda j: i[3]]
            dsts = [lambda j, r=r: r.at[j] for r in o]
            return srcs, dsts, None
        sds = jax.ShapeDtypeStruct
        return _Comm([dw_out.reshape(NDEV, 128, D), dw_pa, dw_pb, small],
                     [sds((NDEV, 128, D), bf16), sds((NDEV, AW, 128), bf16), sds((NDEV, VW, 128), bf16),
                      sds((NDEV,) + small.shape, f32)], plan)

    def early_a(self, blocks):
        def plan(i, o):
            return [lambda j: i[0].at[j]], [lambda j: o[0].at[j]], [lambda j: j >= LATE_DESTS - 1]
        return _Comm([blocks], [jax.ShapeDtypeStruct(blocks.shape, bf16)], plan)

    def early_b(self, blocks):
        def plan(i, o):
            return [lambda j: i[0].at[j]], [lambda j: o[0].at[j]], [lambda j: j >= LATE_DESTS - 1]
        return _Comm([blocks], [jax.ShapeDtypeStruct(blocks.shape, bf16)], plan)

    def late(self, blocks, dgt):
        return _LateComm(blocks, dgt)


def _local_step(x, ctx, tgt, mod, modc, norm_g, w_pad, ln_g, ln_b, ws, bs, gate_w, gb_norm, gf, xch):
    shift, scale, gate = mod[:, 0:D], mod[:, D:2 * D], mod[:, 2 * D:]
    shift_c, scale_c = modc[:, 0:D], modc[:, D:]
    w2f, w2b, gbf, gbb = gate_w

    p, h, *got_proj = _in_proj(x, norm_g, scale, shift, w_pad, 512, xch.gather_proj())
    w_pa, w_pb, w_out = xch.proj_weights(got_proj)
    sc_f, sc_b = _ctx_fwd(ctx, norm_g, scale_c, shift_c, w_pad, w2f, w2b, gbf, gbb)
    o_f, st_f = _gla_fwd(p, w2f, gbf, sc_f, False, 512, "gla_fwd_f")
    o_b, st_b = _gla_fwd(p, w2b, gbb, sc_b, True, 512, "gla_fwd_b")
    sv = _mix_fwd(p, ln_g, ln_b, ws.astype(bf16), bs.T)
    (dx1, dzbua, dzag, dsv, do, dw_out, dw_pa, dw_pb, dgf, dgate, dgbn, loss) = _mid(
        x, tgt, p, o_f, o_b, sv, gate, gf, gb_norm, w_pa, w_pb, w_out, 256)
    dva, dws, dbs_acc, dln_g, dln_b = _mix_bwd(p, dsv, ln_g, ln_b, jnp.swapaxes(ws, 1, 2).astype(bf16))
    small = _rows128(dln_g, dln_b, dws, jnp.sum(dbs_acc, axis=-1), dgbn, dgf, jnp.broadcast_to(loss, (1, 128)))
    blocks_a, blocks_b, *got_first = _tn_matmul(h, [dzbua, dva, dzag], 1024, "dw_early", xch.first(dw_out, dw_pa, dw_pb, small),
                                               _pack_early())

    dqkv_f, dlr_f, dw2f, dgbf, dsc_f, *got_a = _gla_bwd(p, do, st_f, w2f, gbf, None, False, 512, "gla_bwd_f",
                                                        xch.early_a(blocks_a))
    dqkv, dlr, dw2b, dgbb, dsc_b, *got_b = _gla_bwd(p, do, st_b, w2b, gbb, (dqkv_f, dlr_f), True, 512, "gla_bwd_b",
                                                    xch.early_b(blocks_b))
    dwk_c, dwv_c, dwl_c, dmodc, dg_c, dw2c, dgbc, dgt = _ctx_bwd(ctx, norm_g, scale_c, shift_c, w_pad, w2f, w2b, gbf, gbb, dsc_f, dsc_b,
                                                                (dw2f, dw2b), (dgbf, dgbb))
    (blocks_late,) = _tn_matmul(h, [dqkv, dlr], 1024, "dw_qkv_lr", pack=_pack_late(dwk_c, dwv_c, dwl_c))
    dw2 = jnp.stack([dw2f[0:RANK] + dw2c[0, 0:RANK], dw2b[RANK:2 * RANK] + dw2c[1, RANK:2 * RANK]])
    dgb = jnp.concatenate([dgbf + dgbc[0], dgbb + dgbc[1]], axis=0)
    gx, dshift, dscale, dg, *got_late = _in_bwd(x, dx1, dqkv, dzbua, dva, dzag, dlr, w_pad, norm_g, scale, 512,
                                                xch.late(blocks_late, dgt))
    return dict(loss=loss, gx=gx, dmod=jnp.concatenate([dshift, dscale, dgate], axis=1), dmodc=dmodc, dnorm_g=dg + dg_c,
                dnorm_parts=(dg, dg_c), dmod_parts=(dshift, dscale, dgate),
                small=small, blocks_a=blocks_a, blocks_b=blocks_b, blocks_late=blocks_late, dw2=dw2, dgb=dgb,
                dw_pa=dw_pa, dw_pb=dw_pb, dw_out=dw_out, got_first=got_first, got_a=got_a, got_b=got_b, got_late=got_late)


def _rows128(*vs):
    out = []
    for t in vs:
        t = t.reshape(-1)
        pad = (-t.shape[0]) % 128
        out.append(jnp.pad(t, (0, pad)) if pad else t)
    return jnp.concatenate(out).reshape(-1, 128)


def kernel(x, c, ctx, c_ctx, w_mod, b_mod, norm_g, w_in, a_ln_g, a_ln_b, a_ws, a_bs, b_gate_w2, b_gate_b, b_norm_g, w_proj_a, w_proj_b, w_out, final_norm_g, loss_target, m_c_ctx, m_w_mod, m_b_mod, m_norm_g, m_w_in, m_a_ln_g, m_a_ln_b, m_a_ws, m_a_bs, m_b_gate_w2, m_b_gate_b, m_b_norm_g, m_w_proj_a, m_w_proj_b, m_w_out, m_final_norm_g, v_c_ctx, v_w_mod, v_b_mod, v_norm_g, v_w_in, v_a_ln_g, v_a_ln_b, v_a_ws, v_a_bs, v_b_gate_w2, v_b_gate_b, v_b_norm_g, v_w_proj_a, v_w_proj_b, v_w_out, v_final_norm_g):
    me = _me()

    cs, mods, wg, gates = _gather_first(c, c_ctx.reshape(1, D), w_mod[0], b_mod, w_in[0].astype(bf16), b_gate_w2, b_gate_b)
    w_pad, *gate_w = _repack_w(wg, gates)

    mods = jnp.transpose(mods, (1, 0, 2)).reshape(16, 3 * D)
    mod = lax.dynamic_slice(mods, (me, 0), (1, 3 * D))
    modc = mods[8:9, 0:2 * D]

    xch = _Exchanges(w_proj_a[0].astype(bf16), w_proj_b[0].astype(bf16), w_out[0].astype(bf16))
    r = _local_step(x[0], ctx[0], loss_target[0], mod, modc, norm_g, w_pad, a_ln_g, a_ln_b, a_ws[0], a_bs[0], gate_w,
                    b_norm_g, final_norm_g.reshape(1, D), xch)
    p_out, p_pa, p_pb, smalls_e = r["got_first"]
    (p_in_a,) = r["got_a"]
    (p_in_b,) = r["got_b"]
    _, _, p_in_late, p_gt = r["got_late"]

    n_e = (AW + AW + 4 * ACH * ACH + AW + VW + D) // 128
    row = lambda t: t.reshape(1, D)
    rep_e = _adam_params(smalls_e, [a_ln_g, a_ln_b, a_ws, a_bs, b_norm_g, row(final_norm_g)],
                         [m_a_ln_g, m_a_ln_b, m_a_ws, m_a_bs, m_b_norm_g, row(m_final_norm_g)],
                         [v_a_ln_g, v_a_ln_b, v_a_ws, v_a_bs, v_b_norm_g, row(v_final_norm_g)], "adam_rep_early")
    rep_e = [t[0:5] + (t[5].reshape(D),) for t in rep_e]
    loss = jnp.sum(smalls_e[:, n_e, 0])

    mod_grads = _mod_tail(r["dnorm_parts"], r["dmod_parts"], r["dmodc"], cs, w_mod)
    mod_res = _adam_plain(mod_grads, [w_mod, norm_g, b_mod, row(c_ctx)], [m_w_mod, m_norm_g, m_b_mod, row(m_c_ctx)],
                          [v_w_mod, v_norm_g, v_b_mod, row(v_c_ctx)], "adam_mod")
    wm, ng, bmod_r, cc = ([t[i] for t in mod_res] for i in range(4))

    a_in = _adam_w_in(p_in_a, p_in_b, p_in_late, w_in, m_w_in, v_w_in)
    blk = _adam_blocks([p_pa, p_pb, p_out], [w_proj_a, w_proj_b, w_out], [m_w_proj_a, m_w_proj_b, m_w_out],
                       [v_w_proj_a, v_w_proj_b, v_w_out], "adam_proj_out")
    a_pa, a_pb, a_out = ([t[i] for t in blk] for i in range(3))
    a_gt = _adam_gate(p_gt, b_gate_w2, b_gate_b, m_b_gate_w2, m_b_gate_b, v_b_gate_w2, v_b_gate_b)
    sh = [(a_in[k], a_pa[k], a_pb[k], a_out[k], a_gt[k], a_gt[4 + k]) for k in range(4)]

    outs = [loss, r["gx"][None]]
    for k in range(4):
        lg, lb, aws, abs_, bng, fng = rep_e[k]
        n_g, bmod = ng[k], bmod_r[k]
        s_in, s_pa, s_pb, s_out, s_w2, s_gb = sh[k]
        outs += [cc[k].reshape(D), wm[k], bmod, n_g, s_in, lg, lb, aws, abs_, s_w2, s_gb, bng, s_pa, s_pb, s_out, fng]
    return tuple(outs)
```

```python
import jax
import jax.numpy as jnp
from jax import lax
from jax.experimental import pallas as pl
from jax.experimental.pallas import tpu as pltpu

f32, bf16 = jnp.float32, jnp.bfloat16

D = 1024
CTX = 256
EPS = 1e-6
AW = 512
ACH = 128
GW = 64
KW = 256
VW = 512
NH = 4
HK = 64
HV = 128
RANK = 16
TAU = 16.0
CH = 64
QSCALE = HK ** -0.5
INW = 5152
NDEV = 8

PQ, PK, PV, PZB, PUA, PVA, PZA, PG1, PG2, PLR, PW = 0, 256, 512, 1024, 1536, 2048, 2560, 3072, 4096, 5120, 5248
LRW = 128

ADAM_LR, ADAM_B1, ADAM_B2, ADAM_EPS, ADAM_WD, ADAM_STEP = 0.001, 0.9, 0.999, 1e-08, 0.01, 10

VMEM_LIMIT = 56 * 1024 * 1024
MESH = pl.DeviceIdType.MESH


def _cp(sem=None):
    return pltpu.CompilerParams(dimension_semantics=sem, vmem_limit_bytes=VMEM_LIMIT)


def _dot(a, b):
    return jnp.dot(a.astype(bf16), b.astype(bf16), preferred_element_type=f32)


def _nt(a, b):
    return lax.dot_general(a.astype(bf16), b.astype(bf16), (((1,), (1,)), ((), ())), preferred_element_type=f32)


def _tn(a, b):
    return lax.dot_general(a.astype(bf16), b.astype(bf16), (((0,), (0,)), ((), ())), preferred_element_type=f32)


def _dot_hi(a, b):
    return jnp.dot(a, b, preferred_element_type=f32, precision=lax.Precision.HIGHEST)


def _sigmoid(x):
    return 1.0 / (1.0 + jnp.exp(-x))


def _log_sigmoid(x):
    return jnp.minimum(x, 0.0) - jnp.log(1.0 + jnp.exp(-jnp.abs(x)))


def _silu_and_grad(z):
    s = _sigmoid(z)
    return z * s, s * (1.0 + z * (1.0 - s))


def _me():
    return 4 * lax.axis_index("x") + 2 * lax.axis_index("y") + lax.axis_index("c")


def _peer(k):
    x, y, c = lax.axis_index("x"), lax.axis_index("y"), lax.axis_index("c")
    px = 1 - x if k & 4 else x
    py = 1 - y if k & 2 else y
    pc = 1 - c if k & 1 else c
    return (px, py, pc), 4 * px + 2 * py + pc


def _fanout(srcs, dsts, send_sems, recv_sems, local_sems, owners=None):
    me = _me()
    n = len(srcs)
    owns = lambda a, j: True if owners is None or owners[a] is None else owners[a](j)

    def guarded(cond, fn):
        if cond is True:
            fn()
        else:
            pl.when(cond)(fn)

    def copies(with_recvs):
        local = [pltpu.make_async_copy(srcs[a](me), dsts[a](me), local_sems.at[a]) for a in range(n)]
        sends, recvs = [], []
        for k in range(1, NDEV):
            dev, idx = _peer(k)
            for a in range(n):
                s = (k - 1) * n + a
                sends.append((owns(a, idx), pltpu.make_async_remote_copy(
                    src_ref=srcs[a](idx), dst_ref=dsts[a](me), send_sem=send_sems.at[s], recv_sem=recv_sems.at[s],
                    device_id=dev, device_id_type=MESH)))
                if with_recvs:
                    recvs.append((owns(a, me), pltpu.make_async_remote_copy(
                        src_ref=srcs[a](idx), dst_ref=dsts[a](idx), send_sem=send_sems.at[s], recv_sem=recv_sems.at[s],
                        device_id=dev, device_id_type=MESH)))
        return local, sends, recvs

    def start():
        local, sends, _ = copies(False)
        for a, cp in enumerate(local):
            guarded(owns(a, me), cp.start)
        for cond, cp in sends:
            guarded(cond, cp.start)

    def finish():
        local, sends, recvs = copies(True)
        for cond, cp in recvs:
            guarded(cond, cp.wait_recv)
        for cond, cp in sends:
            guarded(cond, cp.wait_send)
        for a, cp in enumerate(local):
            guarded(owns(a, me), cp.wait)

    return start, finish


class _Comm:
    def __init__(self, ins, outs, plan):
        self.ins, self.outs, self.plan = list(ins), list(outs), plan
        self.n = len(self.outs)

    def specs(self):
        hbm = pl.BlockSpec(memory_space=pl.ANY)
        return [hbm] * len(self.ins), [hbm] * self.n, _fanout_sems(self.n) if self.n else []

    def run(self, in_refs, out_refs, sems, nsteps, compute):
        if not self.n:
            compute()
            return

        def hooks():
            srcs, dsts, owners = self.plan(in_refs, out_refs)
            return _fanout(srcs, dsts, sems[0], sems[1], sems[2], owners)

        pl.when(pl.program_id(0) == 0)(lambda: hooks()[0]())
        compute()
        pl.when(pl.program_id(0) == nsteps - 1)(lambda: hooks()[1]())


LATE_MID_STEP = 1


class _LateComm:
    def __init__(self, blocks, dgt):
        sds = jax.ShapeDtypeStruct
        self.ins = [blocks, dgt]
        self.outs = [sds((D, SHARD), bf16), sds((D, SHARD), bf16), sds((4, D, SHARD), bf16), sds(dgt.shape, f32)]
        self.n = len(self.outs)

    def specs(self):
        hbm = pl.BlockSpec(memory_space=pl.ANY)
        scratch = [pltpu.VMEM((3, D, SHARD), bf16), pltpu.SemaphoreType.DMA((2,)), pltpu.SemaphoreType.DMA((4,)),
                   pltpu.SemaphoreType.DMA((3,))] + _fanout_sems(1)
        return [hbm] * 2, [hbm] * self.n, scratch

    def run(self, in_refs, out_refs, scratch, nsteps, compute):
        late_ref, dgt_ref = in_refs
        sib_ref, pair_ref, parts_ref, gt_ref = out_refs
        vbuf, send_sems, recv_sems, local_sems, g_send, g_recv, g_local = scratch
        x, y, c = lax.axis_index("x"), lax.axis_index("y"), lax.axis_index("c")
        chip = 2 * x + y
        is_owner_chip = chip == 0
        step = pl.program_id(0)

        def to_sibling():
            return pltpu.make_async_remote_copy(src_ref=late_ref.at[1 - c], dst_ref=sib_ref, send_sem=send_sems.at[0],
                                                recv_sem=recv_sems.at[0], device_id=(x, y, 1 - c), device_id_type=MESH)

        def to_owner(k):
            return pltpu.make_async_remote_copy(src_ref=pair_ref, dst_ref=parts_ref.at[k], send_sem=send_sems.at[1],
                                                recv_sem=recv_sems.at[k], device_id=(0, 0, c), device_id_type=MESH)

        def own_copy():
            return pltpu.make_async_copy(pair_ref, parts_ref.at[0], local_sems.at[2])

        def gates():
            return _fanout([lambda j: dgt_ref.at[j]], [lambda j: gt_ref.at[j]], g_send, g_recv, g_local)

        @pl.when(step == 0)
        def _():
            to_sibling().start()
            gates()[0]()

        compute()

        @pl.when(step == LATE_MID_STEP)
        def _():
            mine = pltpu.make_async_copy(late_ref.at[c], vbuf.at[0], local_sems.at[0])
            mine.start()
            to_sibling().wait_recv()
            theirs = pltpu.make_async_copy(sib_ref, vbuf.at[1], local_sems.at[1])
            theirs.start()
            mine.wait()
            theirs.wait()
            vbuf[2] = (vbuf[0].astype(f32) + vbuf[1].astype(f32)).astype(bf16)
            pltpu.sync_copy(vbuf.at[2], pair_ref)
            pl.when(is_owner_chip)(lambda: own_copy().start())
            pl.when(jnp.logical_not(is_owner_chip))(lambda: to_owner(chip).start())

        @pl.when(step == nsteps - 1)
        def _():
            @pl.when(is_owner_chip)
            def _():
                for k in range(1, 4):
                    to_owner(k).wait_recv()
                own_copy().wait()

            pl.when(jnp.logical_not(is_owner_chip))(lambda: to_owner(chip).wait_send())
            to_sibling().wait_send()
            gates()[1]()


def _fanout_sems(n):
    return [pltpu.SemaphoreType.DMA(((NDEV - 1) * n,)), pltpu.SemaphoreType.DMA(((NDEV - 1) * n,)), pltpu.SemaphoreType.DMA((n,))]


def _lanes(j):
    return pl.ds(pl.multiple_of(j * 128, 128), 128)


def _gather_first(c_row, cctx_row, wm, bm, wi, w2, gb):
    per = 128 // 32
    gate_rows = 2 * RANK // per + 1

    def body(c_ref, cctx_ref, wm_ref, bm_ref, wi_ref, w2_ref, gb_ref, cs_ref, mods_ref, owi, og, g_ref, call_ref, mine_ref,
             send_sems, recv_sems, local_sems, c_send, c_recv, c_local, m_send, m_recv, m_local):
        g_ref[...] = jnp.zeros_like(g_ref)
        for d in range(2):
            for r in range(RANK):
                q = d * RANK + r
                g_ref[q // per:q // per + 1, (q % per) * 32:(q % per + 1) * 32] = w2_ref[0, d, r:r + 1, :]
            g_ref[gate_rows - 1:gate_rows, d * 32:(d + 1) * 32] = gb_ref[0, d:d + 1, :]
        x, y, c = lax.axis_index("x"), lax.axis_index("y"), lax.axis_index("c")
        sibling = (x, y, 1 - c)
        chips = [(1 - x, y), (x, 1 - y), (1 - x, 1 - y)]
        index = lambda px, py, pc: 4 * px + 2 * py + pc
        arrays = ((wi_ref, owi), (g_ref, og))
        n = len(arrays)

        def copy(a, k, block, to, own=False):
            src, out = arrays[a]
            return pltpu.make_async_remote_copy(
                src_ref=src if own else out.at[index(*block)], dst_ref=out.at[index(*block)],
                send_sem=send_sems.at[k * n + a], recv_sem=recv_sems.at[k * n + a], device_id=to, device_id_type=MESH)

        c_start, c_finish = _fanout([lambda j: c_ref], [lambda j: call_ref.at[j]], c_send, c_recv, c_local)
        c_start()
        mine = [pltpu.make_async_copy(src, out.at[index(x, y, c)], local_sems.at[a]) for a, (src, out) in enumerate(arrays)]
        first = [copy(a, 0, (x, y, c), sibling, own=True) for a in range(n)]
        first += [copy(a, 1 + j, (x, y, c), (*chip, c), own=True) for j, chip in enumerate(chips) for a in range(n)]
        for cp in mine + first:
            cp.start()

        c_finish()
        cs = jnp.concatenate([call_ref[j] for j in range(NDEV)] + [cctx_ref[...], jnp.zeros((16 - NDEV - 1, D), f32)], axis=0)
        cs_ref[...] = cs
        s, _ = _silu_and_grad(cs)
        mine_ref[...] = _dot_hi(s, wm_ref[...]) + bm_ref[:, pl.ds(pl.multiple_of(_me() * ncol, 128), ncol)]
        m_start, m_finish = _fanout([lambda j: mine_ref], [lambda j: mods_ref.at[j]], m_send, m_recv, m_local)
        m_start()

        passed = []
        for j, chip in enumerate(chips):
            for a in range(n):
                copy(a, 1 + j, (*chip, c), (x, y, c)).wait_recv()
            for a in range(n):
                cp = copy(a, 4 + j, (*chip, c), sibling)
                cp.start()
                passed.append(cp)
        for a in range(n):
            copy(a, 0, sibling, (x, y, c)).wait_recv()
        for j, chip in enumerate(chips):
            for a in range(n):
                copy(a, 4 + j, (*chip, 1 - c), (x, y, c)).wait_recv()
        for cp in first + passed:
            cp.wait_send()
        for cp in mine:
            cp.wait()
        m_finish()

    hbm = pl.BlockSpec(memory_space=pl.ANY)
    vm = pl.BlockSpec(memory_space=pltpu.VMEM)
    ncol = wm.shape[1]
    return pl.pallas_call(
        body, name="gather_first",
        out_shape=(jax.ShapeDtypeStruct((16, D), f32), jax.ShapeDtypeStruct((NDEV, 16, ncol), f32),
                   jax.ShapeDtypeStruct((NDEV,) + wi.shape, bf16), jax.ShapeDtypeStruct((NDEV, gate_rows, 128), f32)),
        in_specs=[vm, vm, vm, vm, hbm, vm, vm], out_specs=(vm, vm, hbm, hbm),
        scratch_shapes=[pltpu.VMEM((gate_rows, 128), f32), pltpu.VMEM((NDEV, 1, D), f32), pltpu.VMEM((16, ncol), f32)]
        + _fanout_sems(2) + _fanout_sems(1) + _fanout_sems(1),
        compiler_params=_cp(),
    )(c_row, cctx_row, wm, bm, wi, w2, gb)


SHARD = INW // NDEV
ROWS_RP = 128


def _overlap(lo, hi, a, b):
    s, e = max(lo, a), min(hi, b)
    return (s, e) if s < e else None


def _repack_w(wg, gates):
    segs = ((0, 1024, PQ), (1024, 1024 + 2 * RANK, PLR), (1024 + 2 * RANK, INW, PZB))
    per = 128 // 32

    def body(g_ref, gt_ref, o_ref, w2f_ref, w2b_ref, gbf_ref, gbb_ref):
        @pl.when(pl.program_id(0) == 0)
        def _():
            w2f_ref[...] = jnp.zeros_like(w2f_ref)
            w2b_ref[...] = jnp.zeros_like(w2b_ref)
            for dev in range(NDEV):
                cols = slice(dev * 32, (dev + 1) * 32)
                for d, (w2_ref, gb_ref) in enumerate(((w2f_ref, gbf_ref), (w2b_ref, gbb_ref))):
                    for r in range(RANK):
                        q = d * RANK + r
                        w2_ref[q:q + 1, cols] = gt_ref[dev, q // per:q // per + 1, (q % per) * 32:(q % per + 1) * 32]
                    gb_ref[:, cols] = gt_ref[dev, 2 * RANK // per:2 * RANK // per + 1, d * 32:(d + 1) * 32]

        for j in range(NDEV):
            lo, hi = j * SHARD, (j + 1) * SHARD
            for a, b, pad0 in segs:
                ov = _overlap(lo, hi, a, b)
                if ov:
                    s, e = ov
                    o_ref[:, pad0 + s - a:pad0 + e - a] = g_ref[j, :, s - lo:e - lo]
        o_ref[:, PLR + 2 * RANK:PW] = jnp.zeros((ROWS_RP, PW - PLR - 2 * RANK), bf16)

    sds = jax.ShapeDtypeStruct
    full = lambda *s: pl.BlockSpec(s, lambda i: (0,) * len(s))
    return pl.pallas_call(
        body, name="repack_w", grid=(D // ROWS_RP,),
        out_shape=(sds((D, PW), bf16), sds((LRW, KW), f32), sds((LRW, KW), f32), sds((1, KW), f32), sds((1, KW), f32)),
        in_specs=[pl.BlockSpec((NDEV, ROWS_RP, SHARD), lambda i: (0, i, 0)), full(*gates.shape)],
        out_specs=(pl.BlockSpec((ROWS_RP, PW), lambda i: (i, 0)), full(LRW, KW), full(LRW, KW), full(1, KW), full(1, KW)),
        compiler_params=_cp(("arbitrary",)),
    )(wg, gates)


LATE_END = 1024 + 2 * RANK
LATE_DESTS = 2


def _pack_blocks(o_ref, srcs, dests, dtype):
    for n, j in enumerate(dests):
        lo, hi = j * SHARD, (j + 1) * SHARD
        done = lo
        for a, b, src in srcs:
            ov = _overlap(lo, hi, a, b)
            if ov:
                s, e = ov
                if s > done:
                    o_ref[n, :, done - lo:s - lo] = jnp.zeros((ROWS_RP, s - done), dtype)
                o_ref[n, :, s - lo:e - lo] = src[:, s - a:e - a].astype(dtype)
                done = e
        if done < hi:
            o_ref[n, :, done - lo:hi - lo] = jnp.zeros((ROWS_RP, hi - done), dtype)


class _Pack:
    def __init__(self, extra, outs, fn):
        self.extra, self.outs, self.fn = list(extra), list(outs), fn


def _pack_early():
    def fn(accs, extra, outs):
        zbua, va, zag = accs
        for r0 in range(0, D, ROWS_RP):
            rows = pl.ds(r0, ROWS_RP)
            out, o0 = (outs[0], r0) if r0 < EARLY_A_ROWS else (outs[1], r0 - EARLY_A_ROWS)
            _pack_blocks(out.at[:, pl.ds(o0, ROWS_RP)], ((LATE_END, 2080, zbua.at[rows]), (2080, 2592, va.at[rows]), (2592, INW, zag.at[rows])),
                         range(NDEV), bf16)

    sds = jax.ShapeDtypeStruct
    return _Pack([], [sds((NDEV, EARLY_A_ROWS, SHARD), bf16), sds((NDEV, D - EARLY_A_ROWS, SHARD), bf16)], fn)


def _pack_late(dwk_c, dwv_c, dwl_c):
    def fn(accs, extra, outs):
        qkv_ref, lr_ref = accs
        kc_ref, vc_ref, lc_ref = extra
        for r0 in range(0, D, ROWS_RP):
            rows = pl.ds(r0, ROWS_RP)
            qkv = qkv_ref[rows, :] + jnp.concatenate([jnp.zeros((ROWS_RP, KW), f32), kc_ref[rows, :], vc_ref[rows, :]], axis=1)
            lr = lr_ref[rows, :] + lc_ref[rows, :]
            _pack_blocks(outs[0].at[:, rows], ((0, 1024, qkv), (1024, LATE_END, lr)), range(LATE_DESTS), bf16)

    return _Pack([dwk_c, dwv_c, dwl_c], [jax.ShapeDtypeStruct((LATE_DESTS, D, SHARD), bf16)], fn)


def _mod_tail(dnorm_parts, dmod_parts, dmodc, cs, wm):
    ncol = wm.shape[2]
    n_dn, n_dm = len(dnorm_parts), len(dmod_parts)

    def body(*refs):
        dn_parts, refs = refs[:n_dn], refs[n_dn:]
        dm_parts, refs = refs[:n_dm], refs[n_dm:]
        dmodc_ref, cs_ref, wm_ref, g_wm, g_ng, g_bm, g_cc = refs[:7]
        dn_ref, dmod_ref, a_dn, a_dmod, a_dmodc, dm_rows, gc_ref, a_gc = refs[7:15]
        s1, r1, l1, s2, r2, l2 = refs[15:21]
        dn = dn_parts[0][...]
        for ref in dn_parts[1:]:
            dn = dn + ref[...]
        dn_ref[...] = dn
        dmod_ref[...] = jnp.concatenate([ref[...] for ref in dm_parts], axis=1)
        start, finish = _fanout([lambda j: dn_ref, lambda j: dmod_ref, lambda j: dmodc_ref],
                                [lambda j, r=r: r.at[j] for r in (a_dn, a_dmod, a_dmodc)], s1, r1, l1)
        start()
        finish()

        def total(ref):
            t = ref[0]
            for j in range(1, NDEV):
                t = t + ref[j]
            return t

        g_ng[...] = total(a_dn)
        dmodc_tot = jnp.concatenate([total(a_dmodc), jnp.zeros((1, D), f32)], axis=1)
        g_bm[...] = total(a_dmod) + dmodc_tot
        dm_rows[...] = jnp.concatenate([a_dmod[j] for j in range(NDEV)] + [dmodc_tot, jnp.zeros((16 - NDEV - 1, 3 * D), f32)], axis=0)
        dm = dm_rows[:, pl.ds(pl.multiple_of(_me() * ncol, 128), ncol)]
        s, ds = _silu_and_grad(cs_ref[...])
        part = lax.dot_general(dm[8:9, :], wm_ref[0], (((1,), (1,)), ((), ())), preferred_element_type=f32,
                               precision=lax.Precision.HIGHEST)
        gc_ref[...] = part * ds[8:9, :]
        start2, finish2 = _fanout([lambda j: gc_ref], [lambda j: a_gc.at[j]], s2, r2, l2)
        start2()
        g = lax.dot_general(s, dm, (((0,), (0,)), ((), ())), preferred_element_type=f32, precision=lax.Precision.HIGHEST)
        g_wm[...] = g[None]
        finish2()
        g_cc[...] = total(a_gc)

    sds = jax.ShapeDtypeStruct
    row = lambda n: pltpu.VMEM((NDEV, 1, n), f32)
    return pl.pallas_call(
        body, name="mod_tail", out_shape=(sds(wm.shape, f32), sds((1, D), f32), sds((1, 3 * D), f32), sds((1, D), f32)),
        scratch_shapes=[pltpu.VMEM((1, D), f32), pltpu.VMEM((1, 3 * D), f32), row(D), row(3 * D), row(2 * D),
                        pltpu.VMEM((16, 3 * D), f32), pltpu.VMEM((1, D), f32), row(D)] + _fanout_sems(3) + _fanout_sems(1),
        compiler_params=_cp(),
    )(*dnorm_parts, *dmod_parts, dmodc, cs, wm)


def _adam_update(g, w_ref, m_ref, v_ref, go_ref, d_ref, mo_ref, vo_ref):
    c1 = 1.0 / (1.0 - ADAM_B1 ** ADAM_STEP)
    c2 = 1.0 / (1.0 - ADAM_B2 ** ADAM_STEP)
    mn = ADAM_B1 * m_ref[...] + (1.0 - ADAM_B1) * g
    vn = ADAM_B2 * v_ref[...] + (1.0 - ADAM_B2) * (g * g)
    go_ref[...] = g
    mo_ref[...] = mn
    vo_ref[...] = vn
    d_ref[...] = -ADAM_LR * ((mn * c1) / (jnp.sqrt(vn * c2) + ADAM_EPS) + ADAM_WD * w_ref[...])


def _adam_w_in(parts_a, parts_b, parts_late, w, m, v):
    na = EARLY_A_ROWS // ROWS_RP
    n = D // ROWS_RP
    whole = SHARD // 128 * 128

    def body(a_ref, b_ref, l_ref, w_hbm, m_hbm, v_hbm, go_hbm, d_hbm, mo_hbm, vo_hbm, ibuf, obuf, isem, osem):
        step = pl.program_id(0)

        def cols(t):
            return pl.ds(pl.multiple_of(t * ROWS_RP, ROWS_RP), ROWS_RP)

        def fetch(t):
            return [pltpu.make_async_copy(r.at[:, 0, cols(t)], ibuf.at[t % 2, k], isem.at[t % 2, k])
                    for k, r in enumerate((w_hbm, m_hbm, v_hbm))]

        def flush(t):
            return [pltpu.make_async_copy(obuf.at[t % 2, k], r.at[:, 0, cols(t)], osem.at[t % 2, k])
                    for k, r in enumerate((go_hbm, d_hbm, mo_hbm, vo_hbm))]

        def start(cps):
            for k, cp in enumerate(cps):
                cp.start(priority=k % 2)

        def wait(cps):
            for cp in cps:
                cp.wait()

        pl.when(step == 0)(lambda: start(fetch(step)))
        pl.when(step + 1 < n)(lambda: start(fetch(step + 1)))
        me = _me()
        first = step < na
        early = jnp.where(first, a_ref[0], b_ref[0]).astype(f32)
        for i in range(1, NDEV):
            early = early + jnp.where(first, a_ref[i], b_ref[i]).astype(f32)
        late = l_ref[0].astype(f32)
        for i in range(1, 4):
            late = late + l_ref[i].astype(f32)
        g = jnp.where(me >= LATE_DESTS - 1, early, 0.0) + jnp.where(me < LATE_DESTS, late, 0.0)
        gt = jnp.concatenate([g[:, c:c + 128].T for c in range(0, whole, 128)] + [g[:, SHARD - 128:].T[128 - (SHARD - whole):]],
                             axis=0)
        wait(fetch(step))
        pl.when(step >= 2)(lambda: wait(flush(step - 2)))
        slot = step % 2
        _adam_update(gt, *(ibuf.at[slot, k] for k in range(3)), *(obuf.at[slot, k] for k in range(4)))
        start(flush(step))

        @pl.when(step == n - 1)
        def _():
            wait(flush(step - 1))
            wait(flush(step))

    hbm = pl.BlockSpec(memory_space=pl.ANY)
    res = pl.pallas_call(
        body, name="adam_w_in", grid=(n,), out_shape=tuple(jax.ShapeDtypeStruct((SHARD, 1, D), f32) for _ in range(4)),
        in_specs=[pl.BlockSpec((NDEV, ROWS_RP, SHARD), lambda i: (0, jnp.minimum(i, na - 1), 0)),
                  pl.BlockSpec((NDEV, ROWS_RP, SHARD), lambda i: (0, jnp.maximum(i - na, 0), 0)),
                  pl.BlockSpec((4, ROWS_RP, SHARD), lambda i: (0, i, 0)), hbm, hbm, hbm],
        out_specs=(hbm, hbm, hbm, hbm),
        scratch_shapes=[pltpu.VMEM((2, 3, SHARD, ROWS_RP), f32), pltpu.VMEM((2, 4, SHARD, ROWS_RP), f32),
                        pltpu.SemaphoreType.DMA((2, 3)), pltpu.SemaphoreType.DMA((2, 4))],
        compiler_params=_cp(("arbitrary",)),
    )(parts_a, parts_b, parts_late, *(jnp.transpose(t, (2, 0, 1)) for t in (w, m, v)))
    return tuple(jnp.transpose(t, (1, 2, 0)) for t in res)


def _adam_params(parts, ws, ms, vs, name):
    p = parts.shape[0]
    k = len(ws)
    nrows = [w.size // 128 for w in ws]
    starts = [sum(nrows[:i]) for i in range(k)]

    def shaped(g, shape):
        if len(shape) == 2:
            return jnp.concatenate([g[r:r + 1] for r in range(g.shape[0])], axis=1)
        return g.reshape(shape)

    def body(*refs):
        g_ref = refs[0]
        w_refs, m_refs, v_refs = refs[1:1 + k], refs[1 + k:1 + 2 * k], refs[1 + 2 * k:1 + 3 * k]
        outs = refs[1 + 3 * k:]
        for i in range(k):
            rows = slice(starts[i], starts[i] + nrows[i])
            g = g_ref[0, rows, :]
            for j in range(1, p):
                g = g + g_ref[j, rows, :]
            _adam_update(shaped(g, ws[i].shape), w_refs[i], m_refs[i], v_refs[i], outs[i], outs[k + i], outs[2 * k + i], outs[3 * k + i])

    vm = pl.BlockSpec(memory_space=pltpu.VMEM)
    res = pl.pallas_call(
        body, name=name, out_shape=tuple(jax.ShapeDtypeStruct(w.shape, f32) for _ in range(4) for w in ws),
        in_specs=[vm] * (1 + 3 * k), out_specs=(vm,) * (4 * k), compiler_params=_cp(),
    )(parts, *ws, *ms, *vs)
    return [res[i * k:(i + 1) * k] for i in range(4)]


def _adam_blocks(parts, ws, ms, vs, name):
    k = len(ws)

    def body(*refs):
        g_refs, w_refs, m_refs, v_refs = refs[0:k], refs[k:2 * k], refs[2 * k:3 * k], refs[3 * k:4 * k]
        outs = refs[4 * k:]
        for i in range(k):
            g = g_refs[i][0].astype(f32)
            for j in range(1, parts[i].shape[0]):
                g = g + g_refs[i][j].astype(f32)
            _adam_update(g[None], w_refs[i], m_refs[i], v_refs[i], outs[i], outs[k + i], outs[2 * k + i], outs[3 * k + i])

    vm = pl.BlockSpec(memory_space=pltpu.VMEM)
    res = pl.pallas_call(
        body, name=name, out_shape=tuple(jax.ShapeDtypeStruct(w.shape, f32) for _ in range(4) for w in ws),
        in_specs=[vm] * (4 * k), out_specs=(vm,) * (4 * k), compiler_params=_cp(),
    )(*parts, *ws, *ms, *vs)
    return [res[i * k:(i + 1) * k] for i in range(4)]


def _adam_plain(gs, ws, ms, vs, name):
    k = len(ws)

    def body(*refs):
        g_refs, w_refs, m_refs, v_refs = refs[0:k], refs[k:2 * k], refs[2 * k:3 * k], refs[3 * k:4 * k]
        outs = refs[4 * k:]
        for i in range(k):
            _adam_update(g_refs[i][...], w_refs[i], m_refs[i], v_refs[i], outs[i], outs[k + i], outs[2 * k + i], outs[3 * k + i])

    vm = pl.BlockSpec(memory_space=pltpu.VMEM)
    res = pl.pallas_call(
        body, name=name, out_shape=tuple(jax.ShapeDtypeStruct(w.shape, f32) for _ in range(4) for w in ws),
        in_specs=[vm] * (4 * k), out_specs=(vm,) * (4 * k), compiler_params=_cp(),
    )(*gs, *ws, *ms, *vs)
    return [res[i * k:(i + 1) * k] for i in range(4)]


def _adam_gate(parts, w2, gb, m_w2, m_gb, v_w2, v_gb):
    per = 128 // 32

    def body(p_ref, w2_ref, gb_ref, mw_ref, mb_ref, vw_ref, vb_ref, *outs):
        g = p_ref[0]
        for j in range(1, parts.shape[0]):
            g = g + p_ref[j]
        for d in range(2):
            gd = jnp.concatenate([g[(d * RANK + r) // per:(d * RANK + r) // per + 1, (r % per) * 32:(r % per + 1) * 32]
                                  for r in range(RANK)], axis=0)
            _adam_update(gd, *(t.at[0, d] for t in (w2_ref, mw_ref, vw_ref) + outs[0:4]))
        last = 2 * RANK // per
        gbv = jnp.concatenate([g[last:last + 1, d * 32:(d + 1) * 32] for d in range(2)], axis=0)
        _adam_update(gbv, *(t.at[0] for t in (gb_ref, mb_ref, vb_ref) + outs[4:8]))

    vm = pl.BlockSpec(memory_space=pltpu.VMEM)
    return pl.pallas_call(
        body, name="adam_gate", out_shape=tuple(jax.ShapeDtypeStruct(w.shape, f32) for w in (w2, gb) for _ in range(4)),
        in_specs=[vm] * 7, out_specs=(vm,) * 8, compiler_params=_cp(),
    )(parts, w2, gb, m_w2, m_gb, v_w2, v_gb)


def _in_proj(x, g, scale, shift, w_pad, tl, comm):
    l = x.shape[0]
    c_in, c_out, c_sems = comm.specs()

    def body(*refs):
        x_ref, g_ref, sc_ref, sh_ref, w_ref = refs[:5]
        cin = refs[5:5 + len(c_in)]
        p_ref, h_ref = refs[5 + len(c_in):7 + len(c_in)]
        cout = refs[7 + len(c_in):7 + len(c_in) + comm.n]
        sems = refs[7 + len(c_in) + comm.n:]

        def compute():
            xv = x_ref[...]
            r = lax.rsqrt(jnp.mean(xv * xv, axis=-1, keepdims=True) + EPS)
            h = (xv * r) * (g_ref[...] * (1.0 + sc_ref[...])) + sh_ref[...]
            hb = h.astype(bf16)
            h_ref[...] = hb
            p_ref[...] = jnp.dot(hb, w_ref[...], preferred_element_type=f32).astype(bf16)

        comm.run(cin, cout, sems, l // tl, compute)

    vec = pl.BlockSpec((1, D), lambda i: (0, 0))
    return pl.pallas_call(
        body, name="in_proj", grid=(l // tl,),
        out_shape=(jax.ShapeDtypeStruct((l, PW), bf16), jax.ShapeDtypeStruct((l, D), bf16)) + tuple(comm.outs),
        in_specs=[pl.BlockSpec((tl, D), lambda i: (i, 0)), vec, vec, vec, pl.BlockSpec((D, PW), lambda i: (0, 0))] + c_in,
        out_specs=(pl.BlockSpec((tl, PW), lambda i: (i, 0)), pl.BlockSpec((tl, D), lambda i: (i, 0))) + tuple(c_out),
        scratch_shapes=c_sems, compiler_params=_cp(("arbitrary",)),
    )(x, g, scale, shift, w_pad, *comm.ins)


def _tri(rev):
    i = lax.broadcasted_iota(jnp.int32, (CH, CH), 0)
    j = lax.broadcasted_iota(jnp.int32, (CH, CH), 1)
    return jnp.where((j >= i) if rev else (j <= i), 1.0, 0.0).astype(f32)


def _head_masks():
    lane = lax.broadcasted_iota(jnp.int32, (1, KW), 1) // HK
    return [jnp.where(lane == h, 1.0, 0.0).astype(f32) for h in range(NH)]


def _block_diag():
    r = lax.broadcasted_iota(jnp.int32, (VW, KW), 0) // HV
    c = lax.broadcasted_iota(jnp.int32, (VW, KW), 1) // HK
    return jnp.where(r == c, 1.0, 0.0).astype(f32)


def _decay(lr, w2, gb, tri, rev):
    logits = _dot(lr, w2) + gb
    a = _log_sigmoid(logits) * (1.0 / TAU)
    c = _dot_hi(tri, a)
    cl = c[0:1, :] if rev else c[CH - 1:CH, :]
    return logits, c, cl


def _state_fwd(k, v, c, cl, st, bd):
    return st * jnp.exp(cl) + bd * _tn(v, k * jnp.exp(cl - c))


def _state_bwd(k, v, c, cl, st0, dst, trit):
    ecl = jnp.exp(cl)
    edec = jnp.exp(cl - c)
    kdec = k * edec
    dv = _nt(kdec, dst)
    dkdec = _dot(v, dst)
    dcl = jnp.sum(dst * st0, axis=0, keepdims=True) * ecl + jnp.sum(dkdec * kdec, axis=0, keepdims=True)
    da = _dot_hi(trit, -dkdec * kdec) + dcl
    return dkdec * edec, dv, da, dst * ecl


def _bdot(a, b):
    return lax.dot_general(a.astype(bf16), b.astype(bf16), (((2,), (1,)), ((0,), (0,))), preferred_element_type=f32)


def _bnt(a, b):
    return lax.dot_general(a.astype(bf16), b.astype(bf16), (((2,), (2,)), ((0,), (0,))), preferred_element_type=f32)


def _btn(a, b):
    return lax.dot_general(a.astype(bf16), b.astype(bf16), (((1,), (1,)), ((0,), (0,))), preferred_element_type=f32)


def _scan_chunks(x, rev):
    nc = x.shape[0]
    hi = x.astype(bf16)
    r1 = x - hi.astype(f32)
    mid = r1.astype(bf16)
    lo = (r1 - mid.astype(f32)).astype(bf16)
    terms = jnp.concatenate([hi, mid, lo], axis=1)
    tri3 = jnp.broadcast_to(jnp.concatenate([_tri(rev)] * 3, axis=1).astype(bf16)[None], (nc, CH, 3 * CH))
    return lax.dot_general(tri3, terms, (((2,), (1,)), ((0,), (0,))), preferred_element_type=f32)


class _Tile:
    pass


def _tile_prep(q_ref, k_ref, lr_ref, w2_ref, gb_ref, rev, nc):
    t = _Tile()
    tg = nc * CH
    t.logits = _dot(lr_ref[...], w2_ref[...]) + gb_ref[...]
    c = _scan_chunks((_log_sigmoid(t.logits) * (1.0 / TAU)).reshape(nc, CH, KW), rev)
    cl = c[:, 0:1, :] if rev else c[:, CH - 1:CH, :]
    k = k_ref[...].astype(f32).reshape(nc, CH, KW)
    t.ec, t.enc, t.edec, t.ecl = jnp.exp(c), jnp.exp(-c), jnp.exp(cl - c), jnp.exp(cl)
    t.qd = q_ref[...].astype(f32).reshape(nc, CH, KW) * t.ec * QSCALE
    t.kd = k * t.enc
    t.kdec = k * t.edec
    hm = _head_masks()
    t.tri4 = jnp.concatenate([_tri(rev)] * NH, axis=0)[None]
    t.qs = jnp.concatenate([t.qd * hm[h] for h in range(NH)], axis=1)
    t.pst = _bnt(t.qs, t.kd) * t.tri4
    return t


def _gla_fwd(p, w2p, gb, s0, rev, tg, name):
    l = p.shape[0]
    nb, nc = l // tg, tg // CH

    def body(q_ref, k_ref, v_ref, lr_ref, w2_ref, gb_ref, s0_ref, o_ref, st_ref, st):
        @pl.when(pl.program_id(0) == 0)
        def _():
            st[...] = s0_ref[...]

        t = _tile_prep(q_ref, k_ref, lr_ref, w2_ref, gb_ref, rev, nc)
        v = v_ref[...].reshape(nc, CH, VW)
        intra = jnp.concatenate([_bdot(t.pst[:, h * CH:(h + 1) * CH], v[:, :, h * HV:(h + 1) * HV]) for h in range(NH)], axis=2)
        bd = _block_diag()
        s = st[...]
        for n in (range(nc - 1, -1, -1) if rev else range(nc)):
            st_ref[n] = s.astype(bf16)
            s = s * t.ecl[n] + bd * _tn(v[n], t.kdec[n])
        st[...] = s
        o_ref[...] = (_bnt(t.qd, st_ref[...]) + intra).reshape(tg, VW).astype(bf16)

    blk = (lambda i: nb - 1 - i) if rev else (lambda i: i)
    return pl.pallas_call(
        body, name=name, grid=(nb,),
        out_shape=(jax.ShapeDtypeStruct((l, VW), bf16), jax.ShapeDtypeStruct((l // CH, VW, KW), bf16)),
        in_specs=[pl.BlockSpec((tg, KW), lambda i: (blk(i), PQ // KW)), pl.BlockSpec((tg, KW), lambda i: (blk(i), PK // KW)),
                  pl.BlockSpec((tg, VW), lambda i: (blk(i), PV // VW)), pl.BlockSpec((tg, LRW), lambda i: (blk(i), PLR // LRW)),
                  pl.BlockSpec((LRW, KW), lambda i: (0, 0)), pl.BlockSpec((1, KW), lambda i: (0, 0)),
                  pl.BlockSpec((VW, KW), lambda i: (0, 0))],
        out_specs=(pl.BlockSpec((tg, VW), lambda i: (blk(i), 0)), pl.BlockSpec((nc, VW, KW), lambda i: (blk(i), 0, 0))),
        scratch_shapes=[pltpu.VMEM((VW, KW), f32)],
        compiler_params=_cp(("arbitrary",)),
    )(p, p, p, p, w2p, gb, s0)


def _gla_bwd(p, do, states, w2p, gb, prev, rev, tg, name, comm):
    l = p.shape[0]
    nb, nc = l // tg, tg // CH
    out_dt = bf16
    c_in, c_out, c_sems = comm.specs()

    def body(*refs):
        q_ref, k_ref, v_ref, lr_ref, do_ref, st_ref, w2_ref, gb_ref = refs[:8]
        refs = refs[8:]
        if prev is not None:
            pq_ref, pl_ref = refs[:2]
            refs = refs[2:]
        cin, refs = refs[:len(c_in)], refs[len(c_in):]
        dqkv_ref, dlr_ref, dw2_ref, dgb_ref, ds0_ref = refs[:5]
        cout, dst, ds_buf, sems = refs[5:5 + comm.n], refs[5 + comm.n], refs[6 + comm.n], refs[7 + comm.n:]
        comm.run(cin, cout, sems, nb, lambda: compute(q_ref, k_ref, v_ref, lr_ref, do_ref, st_ref, w2_ref, gb_ref,
                                                      pq_ref if prev is not None else None, pl_ref if prev is not None else None,
                                                      dqkv_ref, dlr_ref, dw2_ref, dgb_ref, ds0_ref, dst, ds_buf))

    def compute(q_ref, k_ref, v_ref, lr_ref, do_ref, st_ref, w2_ref, gb_ref, pq_ref, pl_ref,
                dqkv_ref, dlr_ref, dw2_ref, dgb_ref, ds0_ref, dst, ds_buf):
        @pl.when(pl.program_id(0) == 0)
        def _():
            dst[...] = jnp.zeros_like(dst)
            dw2_ref[...] = jnp.zeros_like(dw2_ref)
            dgb_ref[...] = jnp.zeros_like(dgb_ref)

        t = _tile_prep(q_ref, k_ref, lr_ref, w2_ref, gb_ref, rev, nc)
        hm = _head_masks()
        v = v_ref[...].reshape(nc, CH, VW)
        do = do_ref[...].reshape(nc, CH, VW)
        heads = lambda a, h: a[:, :, h * HV:(h + 1) * HV]
        dpst = jnp.concatenate([_bnt(heads(do, h), heads(v, h)) for h in range(NH)], axis=1) * t.tri4
        dv = jnp.concatenate([_btn(t.pst[:, h * CH:(h + 1) * CH], heads(do, h)) for h in range(NH)], axis=2)
        dqd = _bdot(do, st_ref[...])
        for h in range(NH):
            dqd = dqd + hm[h] * _bdot(dpst[:, h * CH:(h + 1) * CH], t.kd)
        dkd = _btn(dpst, t.qs)
        bd = _block_diag()
        d = dst[...]
        for n in (range(nc) if rev else range(nc - 1, -1, -1)):
            ds_buf[n] = d
            d = d * t.ecl[n] + bd * _tn(do[n], t.qd[n])
        dst[...] = d
        ds0_ref[...] = d
        ds = ds_buf[...]
        dv = dv + _bnt(t.kdec, ds)
        dkdec = _bdot(v, ds)
        dcl = jnp.sum(ds * st_ref[...].astype(f32), axis=1, keepdims=True) * t.ecl + jnp.sum(dkdec * t.kdec, axis=1, keepdims=True)
        dc = dqd * t.qd - dkd * t.kd - dkdec * t.kdec
        da = _scan_chunks(dc, not rev) + dcl
        dq = dqd * t.ec * QSCALE
        dk = dkd * t.enc + dkdec * t.edec
        dlog = da.reshape(tg, KW) * _sigmoid(-t.logits) * (1.0 / TAU)
        dlr = _nt(dlog, w2_ref[...])
        dw2_ref[...] += _tn(lr_ref[...], dlog)
        dgb_ref[...] += jnp.sum(dlog, axis=0, keepdims=True)
        dqkv = jnp.concatenate([dq, dk, dv], axis=2).reshape(tg, 2 * KW + VW)
        if prev is not None:
            dqkv = dqkv + pq_ref[...]
            dlr = dlr + pl_ref[...]
        dqkv_ref[...] = dqkv.astype(out_dt)
        dlr_ref[...] = dlr.astype(out_dt)

    blk = (lambda i: i) if rev else (lambda i: nb - 1 - i)
    in_specs = [pl.BlockSpec((tg, KW), lambda i: (blk(i), PQ // KW)), pl.BlockSpec((tg, KW), lambda i: (blk(i), PK // KW)),
                pl.BlockSpec((tg, VW), lambda i: (blk(i), PV // VW)), pl.BlockSpec((tg, LRW), lambda i: (blk(i), PLR // LRW)),
                pl.BlockSpec((tg, VW), lambda i: (blk(i), 0)), pl.BlockSpec((nc, VW, KW), lambda i: (blk(i), 0, 0)),
                pl.BlockSpec((LRW, KW), lambda i: (0, 0)), pl.BlockSpec((1, KW), lambda i: (0, 0))]
    args = [p, p, p, p, do, states, w2p, gb]
    if prev is not None:
        in_specs += [pl.BlockSpec((tg, 2 * KW + VW), lambda i: (blk(i), 0)), pl.BlockSpec((tg, LRW), lambda i: (blk(i), 0))]
        args += list(prev)
    return pl.pallas_call(
        body, name=name, grid=(nb,),
        out_shape=(jax.ShapeDtypeStruct((l, 2 * KW + VW), out_dt), jax.ShapeDtypeStruct((l, LRW), out_dt),
                   jax.ShapeDtypeStruct((LRW, KW), f32), jax.ShapeDtypeStruct((1, KW), f32), jax.ShapeDtypeStruct((VW, KW), f32))
        + tuple(comm.outs),
        in_specs=in_specs + c_in,
        out_specs=(pl.BlockSpec((tg, 2 * KW + VW), lambda i: (blk(i), 0)), pl.BlockSpec((tg, LRW), lambda i: (blk(i), 0)),
                   pl.BlockSpec((LRW, KW), lambda i: (0, 0)), pl.BlockSpec((1, KW), lambda i: (0, 0)),
                   pl.BlockSpec((VW, KW), lambda i: (0, 0))) + tuple(c_out),
        scratch_shapes=[pltpu.VMEM((VW, KW), f32), pltpu.VMEM((nc, VW, KW), f32)] + c_sems,
        compiler_params=_cp(("arbitrary",)),
    )(*args, *comm.ins)


def _ctx_hidden(ctx_ref, g_ref, sc_ref, sh_ref):
    xv = ctx_ref[...]
    r = lax.rsqrt(jnp.mean(xv * xv, axis=-1, keepdims=True) + EPS)
    xn = xv * r
    return xn, xn * (g_ref[...] * (1.0 + sc_ref[...])) + sh_ref[...]


_CTX_W_SPECS = [pl.BlockSpec((D, KW), lambda i: (0, PK // KW)), pl.BlockSpec((D, VW), lambda i: (0, PV // VW)),
                pl.BlockSpec((D, LRW), lambda i: (0, PLR // LRW))]


def _ctx_fwd(ctx, g, scale, shift, w_pad, w2f, w2b, gbf, gbb):
    ncc = CTX // CH

    def body(ctx_ref, g_ref, sc_ref, sh_ref, wk_ref, wv_ref, wl_ref, w2f_ref, w2b_ref, gbf_ref, gbb_ref, sf_ref, sb_ref):
        _, hc = _ctx_hidden(ctx_ref, g_ref, sc_ref, sh_ref)
        k, v, lr = _dot(hc, wk_ref[...]), _dot(hc, wv_ref[...]), _dot(hc, wl_ref[...])
        bd = _block_diag()
        for rev, w2_ref, gb_ref, out in ((False, w2f_ref, gbf_ref, sf_ref), (True, w2b_ref, gbb_ref, sb_ref)):
            tri = _tri(rev)
            st = jnp.zeros((VW, KW), f32)
            for j in (range(ncc - 1, -1, -1) if rev else range(ncc)):
                rows = slice(j * CH, (j + 1) * CH)
                _, c, cl = _decay(lr[rows], w2_ref[...], gb_ref[...], tri, rev)
                st = _state_fwd(k[rows], v[rows], c, cl, st, bd)
            out[...] = st

    vec = pl.BlockSpec((1, D), lambda i: (0, 0))
    w2s = pl.BlockSpec((LRW, KW), lambda i: (0, 0))
    gbs = pl.BlockSpec((1, KW), lambda i: (0, 0))
    sts = pl.BlockSpec((VW, KW), lambda i: (0, 0))
    return pl.pallas_call(
        body, name="ctx_fwd", grid=(1,), out_shape=(jax.ShapeDtypeStruct((VW, KW), f32),) * 2,
        in_specs=[pl.BlockSpec((CTX, D), lambda i: (0, 0)), vec, vec, vec] + _CTX_W_SPECS + [w2s, w2s, gbs, gbs],
        out_specs=(sts, sts), compiler_params=_cp(("arbitrary",)),
    )(ctx, g, scale, shift, w_pad, w_pad, w_pad, w2f, w2b, gbf, gbb)


def _ctx_bwd(ctx, g, scale, shift, w_pad, w2f, w2b, gbf, gbb, dsf, dsb, dw2_scans, dgb_scans):
    ncc = CTX // CH
    per = 128 // 32

    def body(ctx_ref, g_ref, sc_ref, sh_ref, wk_ref, wv_ref, wl_ref, w2f_ref, w2b_ref, gbf_ref, gbb_ref, dsf_ref, dsb_ref,
             sw2f_ref, sw2b_ref, sgbf_ref, sgbb_ref, dwk_ref, dwv_ref, dwl_ref, dmod_ref, dg_ref, dw2_ref, dgb_ref, dgt_ref):
        dgt_ref[...] = jnp.zeros_like(dgt_ref)
        xn, hc = _ctx_hidden(ctx_ref, g_ref, sc_ref, sh_ref)
        k, v, lr = _dot(hc, wk_ref[...]), _dot(hc, wv_ref[...]), _dot(hc, wl_ref[...])
        bd = _block_diag()
        dk_rows, dv_rows, dl_rows = [None] * ncc, [None] * ncc, [None] * ncc
        for d, (rev, w2_ref, gb_ref, ds_ref) in enumerate(((False, w2f_ref, gbf_ref, dsf_ref), (True, w2b_ref, gbb_ref, dsb_ref))):
            tri = _tri(rev)
            order = list(range(ncc - 1, -1, -1) if rev else range(ncc))
            st, saved = jnp.zeros((VW, KW), f32), {}
            for j in order:
                rows = slice(j * CH, (j + 1) * CH)
                logits, c, cl = _decay(lr[rows], w2_ref[...], gb_ref[...], tri, rev)
                saved[j] = (logits, c, cl, st)
                st = _state_fwd(k[rows], v[rows], c, cl, st, bd)
            dst = ds_ref[...]
            dw2 = jnp.zeros((LRW, KW), f32)
            dgb = jnp.zeros((1, KW), f32)
            for j in reversed(order):
                rows = slice(j * CH, (j + 1) * CH)
                logits, c, cl, st0 = saved[j]
                dk, dv, da, dst = _state_bwd(k[rows], v[rows], c, cl, st0, dst, _tri(not rev))
                dlog = da * _sigmoid(-logits) * (1.0 / TAU)
                dl = _nt(dlog, w2_ref[...])
                dw2 = dw2 + _tn(lr[rows], dlog)
                dgb = dgb + jnp.sum(dlog, axis=0, keepdims=True)
                dk_rows[j] = dk if dk_rows[j] is None else dk_rows[j] + dk
                dv_rows[j] = dv if dv_rows[j] is None else dv_rows[j] + dv
                dl_rows[j] = dl if dl_rows[j] is None else dl_rows[j] + dl
            dw2_ref[d] = dw2
            dgb_ref[d] = dgb
            dw2 = dw2 + (sw2f_ref, sw2b_ref)[d][...]
            dgb = dgb + (sgbf_ref, sgbb_ref)[d][...]
            for dev in range(NDEV):
                cols = slice(dev * 32, (dev + 1) * 32)
                for r in range(RANK):
                    q = d * RANK + r
                    dgt_ref[dev, q // per:q // per + 1, (q % per) * 32:(q % per + 1) * 32] = dw2[q:q + 1, cols]
                dgt_ref[dev, 2 * RANK // per:2 * RANK // per + 1, d * 32:(d + 1) * 32] = dgb[:, cols]
        dk, dv, dl = (jnp.concatenate(t, axis=0) for t in (dk_rows, dv_rows, dl_rows))
        dwk_ref[...] = _tn(hc, dk)
        dwv_ref[...] = _tn(hc, dv)
        dwl_ref[...] = _tn(hc, dl)
        dh = _nt(dk, wk_ref[...]) + _nt(dv, wv_ref[...]) + _nt(dl, wl_ref[...])
        gx = dh * xn
        dmod_ref[:, 0:D] = jnp.sum(dh, axis=0, keepdims=True)
        dmod_ref[:, D:2 * D] = jnp.sum(gx, axis=0, keepdims=True) * g_ref[...]
        dg_ref[...] = jnp.sum(gx, axis=0, keepdims=True) * (1.0 + sc_ref[...])

    vec = pl.BlockSpec((1, D), lambda i: (0, 0))
    w2s = pl.BlockSpec((LRW, KW), lambda i: (0, 0))
    gbs = pl.BlockSpec((1, KW), lambda i: (0, 0))
    sts = pl.BlockSpec((VW, KW), lambda i: (0, 0))
    full = lambda *s: pl.BlockSpec(s, lambda i: (0,) * len(s))
    return pl.pallas_call(
        body, name="ctx_bwd", grid=(1,),
        out_shape=(jax.ShapeDtypeStruct((D, KW), f32), jax.ShapeDtypeStruct((D, VW), f32), jax.ShapeDtypeStruct((D, LRW), f32),
                   jax.ShapeDtypeStruct((1, 2 * D), f32), jax.ShapeDtypeStruct((1, D), f32),
                   jax.ShapeDtypeStruct((2, LRW, KW), f32), jax.ShapeDtypeStruct((2, 1, KW), f32),
                   jax.ShapeDtypeStruct((NDEV, 2 * RANK // per + 1, 128), f32)),
        in_specs=[pl.BlockSpec((CTX, D), lambda i: (0, 0)), vec, vec, vec] + _CTX_W_SPECS + [w2s, w2s, gbs, gbs, sts, sts, w2s, w2s, gbs, gbs],
        out_specs=(full(D, KW), full(D, VW), full(D, LRW), full(1, 2 * D), full(1, D), full(2, LRW, KW), full(2, 1, KW),
                   full(NDEV, 2 * RANK // per + 1, 128)),
        compiler_params=_cp(("arbitrary",)),
    )(ctx, g, scale, shift, w_pad, w_pad, w_pad, w2f, w2b, gbf, gbb, dsf, dsb, *dw2_scans, *dgb_scans)


def _layernorm(va, g, b):
    mu = jnp.mean(va, axis=-1, keepdims=True)
    xc = va - mu
    rstd = lax.rsqrt(jnp.mean(xc * xc, axis=-1, keepdims=True) + EPS)
    vhat = xc * rstd
    return vhat, rstd, vhat * g + b


MIX_PIECES = 8


def _mix_fwd(p, ln_g, ln_b, ws, bs_t):
    l = p.shape[0]
    half = AW // 2
    rp = l // MIX_PIECES
    cpp = rp // ACH

    def body(p_ref, g_ref, b_ref, ws_ref, bs_ref, sv_hbm, va_buf, col_buf, sv_buf, in_sems, out_sems):
        piece = lambda i: pl.ds(pl.multiple_of(i * rp, rp), rp)
        load = lambda i: pltpu.make_async_copy(p_ref.at[piece(i), pl.ds(PVA, AW)], va_buf.at[piece(i)], in_sems.at[i])
        store = lambda i: pltpu.make_async_copy(sv_buf.at[piece(i)], sv_hbm.at[piece(i)], out_sems.at[i])
        for i in range(MIX_PIECES):
            load(i).start()

        def rows_piece(i, carry):
            load(i).wait()
            for j in range(cpp):
                rows = pl.ds(pl.multiple_of(i * rp + j * ACH, ACH), ACH)
                _, _, vn = _layernorm(va_buf[rows, :].astype(f32), g_ref[...], b_ref[...])
                for gi in range(2):
                    sl = slice(gi * ACH, (gi + 1) * ACH)
                    sv_buf[rows, sl] = (_dot(ws_ref[gi], vn[:, sl]) + bs_ref[:, gi:gi + 1]).astype(bf16)
                col_buf[0, rows, :] = vn[:, half:half + ACH]
                col_buf[1, rows, :] = vn[:, half + ACH:]
            return carry

        lax.fori_loop(0, MIX_PIECES, rows_piece, 0)

        def cols_step(cidx, carry):
            rows = pl.ds(cidx, ACH, stride=GW)
            for gi in range(2, 4):
                col_buf[gi - 2, rows, :] = _dot(ws_ref[gi], col_buf[gi - 2, rows, :]) + bs_ref[:, gi:gi + 1]
            return carry

        lax.fori_loop(0, GW, cols_step, 0, unroll=8)

        def out_piece(i, carry):
            for j in range(cpp):
                rows = pl.ds(pl.multiple_of(i * rp + j * ACH, ACH), ACH)
                sv_buf[rows, half:half + ACH] = col_buf[0, rows, :].astype(bf16)
                sv_buf[rows, half + ACH:] = col_buf[1, rows, :].astype(bf16)
            store(i).start()
            return carry

        lax.fori_loop(0, MIX_PIECES, out_piece, 0)
        for i in range(MIX_PIECES):
            store(i).wait()

    vm = pl.BlockSpec(memory_space=pltpu.VMEM)
    hbm = pl.BlockSpec(memory_space=pl.ANY)
    return pl.pallas_call(
        body, name="mix_fwd", out_shape=jax.ShapeDtypeStruct((l, AW), bf16),
        in_specs=[hbm, vm, vm, vm, vm], out_specs=hbm,
        scratch_shapes=[pltpu.VMEM((l, AW), bf16), pltpu.VMEM((2, l, ACH), f32), pltpu.VMEM((l, AW), bf16),
                        pltpu.SemaphoreType.DMA((MIX_PIECES,)), pltpu.SemaphoreType.DMA((MIX_PIECES,))],
        compiler_params=_cp(),
    )(p, ln_g, ln_b, ws, bs_t)


def _mix_bwd(p, dsv, ln_g, ln_b, ws_t):
    l = p.shape[0]
    half = AW // 2
    rp = l // MIX_PIECES
    cpp = rp // ACH

    def body(p_ref, dsv_hbm, g_ref, b_ref, wst_ref, dva_hbm, dws_ref, dbs_ref, dg_ref, db_ref,
             va_buf, dsv_buf, vn_col, ds_col, dva_buf, va_sems, ds_sems, out_sems):
        piece = lambda i: pl.ds(pl.multiple_of(i * rp, rp), rp)
        load_va = lambda i: pltpu.make_async_copy(p_ref.at[piece(i), pl.ds(PVA, AW)], va_buf.at[piece(i)], va_sems.at[i])
        load_ds = lambda i: pltpu.make_async_copy(dsv_hbm.at[piece(i)], dsv_buf.at[piece(i)], ds_sems.at[i])
        store = lambda i: pltpu.make_async_copy(dva_buf.at[piece(i)], dva_hbm.at[piece(i)], out_sems.at[i])
        for i in range(MIX_PIECES):
            load_va(i).start()
            load_ds(i).start()
        dws_ref[...] = jnp.zeros_like(dws_ref)
        dbs_ref[...] = jnp.zeros_like(dbs_ref)
        dg_ref[...] = jnp.zeros_like(dg_ref)
        db_ref[...] = jnp.zeros_like(db_ref)

        def rows_piece(i, carry):
            load_va(i).wait()
            load_ds(i).wait()
            for j in range(cpp):
                rows = pl.ds(pl.multiple_of(i * rp + j * ACH, ACH), ACH)
                _, _, vn = _layernorm(va_buf[rows, :].astype(f32), g_ref[...], b_ref[...])
                ds = dsv_buf[rows, :].astype(f32)
                for gi in range(2):
                    sl = slice(gi * ACH, (gi + 1) * ACH)
                    dws_ref[gi] += _nt(ds[:, sl], vn[:, sl])
                    dbs_ref[gi] += ds[:, sl]
                for gi in range(2):
                    sl = slice(half + gi * ACH, half + (gi + 1) * ACH)
                    vn_col[gi, rows, :] = vn[:, sl]
                    ds_col[gi, rows, :] = ds[:, sl]
            return carry

        lax.fori_loop(0, MIX_PIECES, rows_piece, 0)

        def cols_step(cidx, carry):
            rows = pl.ds(cidx, ACH, stride=GW)
            for gi in range(2, 4):
                ds = ds_col[gi - 2, rows, :]
                dws_ref[gi] += _nt(ds, vn_col[gi - 2, rows, :])
                dbs_ref[gi] += ds
                ds_col[gi - 2, rows, :] = _dot(wst_ref[gi], ds)
            return carry

        lax.fori_loop(0, GW, cols_step, 0, unroll=8)

        def out_piece(i, carry):
            for j in range(cpp):
                rows = pl.ds(pl.multiple_of(i * rp + j * ACH, ACH), ACH)
                vhat, rstd, _ = _layernorm(va_buf[rows, :].astype(f32), g_ref[...], b_ref[...])
                ds = dsv_buf[rows, :].astype(f32)
                dvn = jnp.concatenate([_dot(wst_ref[0], ds[:, 0:ACH]), _dot(wst_ref[1], ds[:, ACH:half]),
                                       ds_col[0, rows, :], ds_col[1, rows, :]], axis=1)
                dg_ref[...] += jnp.sum(dvn * vhat, axis=0, keepdims=True)
                db_ref[...] += jnp.sum(dvn, axis=0, keepdims=True)
                dvh = dvn * g_ref[...]
                dva = rstd * (dvh - jnp.mean(dvh, axis=-1, keepdims=True) - vhat * jnp.mean(dvh * vhat, axis=-1, keepdims=True))
                dva_buf[rows, :] = dva.astype(bf16)
            store(i).start()
            return carry

        lax.fori_loop(0, MIX_PIECES, out_piece, 0)
        for i in range(MIX_PIECES):
            store(i).wait()

    vm = pl.BlockSpec(memory_space=pltpu.VMEM)
    hbm = pl.BlockSpec(memory_space=pl.ANY)
    dma = lambda: pltpu.SemaphoreType.DMA((MIX_PIECES,))
    return pl.pallas_call(
        body, name="mix_bwd",
        out_shape=(jax.ShapeDtypeStruct((l, AW), bf16), jax.ShapeDtypeStruct((4, ACH, ACH), f32), jax.ShapeDtypeStruct((4, ACH, ACH), f32),
                   jax.ShapeDtypeStruct((1, AW), f32), jax.ShapeDtypeStruct((1, AW), f32)),
        in_specs=[hbm, hbm, vm, vm, vm], out_specs=(hbm, vm, vm, vm, vm),
        scratch_shapes=[pltpu.VMEM((l, AW), bf16), pltpu.VMEM((l, AW), bf16), pltpu.VMEM((2, l, ACH), f32), pltpu.VMEM((2, l, ACH), f32),
                        pltpu.VMEM((l, AW), bf16), dma(), dma(), dma()],
        compiler_params=_cp(),
    )(p, dsv, ln_g, ln_b, ws_t)


def _mid(x, tgt, p, o_f, o_b, sv, gate, gf, gb_norm, w_pa, w_pb, w_out, tl):
    l = x.shape[0]

    def body(x_ref, t_ref, zb_ref, ua_ref, za_ref, g1_ref, g2_ref, of_ref, ob_ref, sv_ref, gate_ref, gf_ref, gbn_ref,
             wpa_ref, wpb_ref, wout_ref,
             dx1_ref, dzbua_ref, dzag_ref, dsv_ref, do_ref, dwout_bf, dwpa_bf, dwpb_bf, dgf_ref, dgate_ref, dgbn_ref, loss_ref,
             dwout_ref, dwpa_ref, dwpb_ref):
        @pl.when(pl.program_id(0) == 0)
        def _():
            for r in (dwout_ref, dwpa_ref, dwpb_ref, dgf_ref, dgate_ref, dgbn_ref, loss_ref):
                r[...] = jnp.zeros_like(r)

        o = of_ref[...].astype(f32) + ob_ref[...].astype(f32)
        rr = jnp.concatenate(
            [jnp.broadcast_to(lax.rsqrt(jnp.mean(o[:, h * HV:(h + 1) * HV] ** 2, axis=-1, keepdims=True) + EPS), (tl, HV))
             for h in range(NH)], axis=1)
        ohat = o * rr
        on = ohat * gbn_ref[...]
        szb, dszb = _silu_and_grad(zb_ref[...].astype(f32))
        tb = on * szb
        u = ua_ref[...].astype(f32)
        svv = sv_ref[...].astype(f32)
        sza, dsza = _silu_and_grad(za_ref[...].astype(f32))
        ta = u * svv * sza
        ya = _dot(ta, wpa_ref[...])
        yb = _dot(tb, wpb_ref[...])
        g1 = _sigmoid(g1_ref[...].astype(f32))
        g2 = _sigmoid(g2_ref[...].astype(f32))
        m = g1 * ya + g2 * yb
        y2 = _dot(m, wout_ref[...])
        x1 = x_ref[...] + gate_ref[...] * y2
        r1 = lax.rsqrt(jnp.mean(x1 * x1, axis=-1, keepdims=True) + EPS)
        x1n = x1 * r1
        err = x1n * gf_ref[...] - t_ref[...]
        loss_ref[...] += jnp.sum(jnp.sum(err * err, axis=-1, keepdims=True), axis=0, keepdims=True) * (0.5 / D)
        dout = err * (1.0 / D)
        dgf_ref[...] += jnp.sum(dout * x1n, axis=0, keepdims=True)
        dx1n = dout * gf_ref[...]
        dx1 = r1 * (dx1n - x1n * jnp.mean(dx1n * x1n, axis=-1, keepdims=True))
        dx1_ref[...] = dx1
        dgate_ref[...] += jnp.sum(dx1 * y2, axis=0, keepdims=True)
        dy2 = dx1 * gate_ref[...]
        dwout_ref[...] += _tn(m, dy2)
        dm = _nt(dy2, wout_ref[...])
        dya = dm * g1
        dyb = dm * g2
        dzag_ref[:, AW:AW + D] = (dm * ya * g1 * (1.0 - g1)).astype(bf16)
        dzag_ref[:, AW + D:] = (dm * yb * g2 * (1.0 - g2)).astype(bf16)
        dwpa_ref[...] += _tn(ta, dya)
        dta = _nt(dya, wpa_ref[...])
        dzbua_ref[:, AW:] = (dta * svv * sza).astype(bf16)
        dsv_ref[...] = (dta * u * sza).astype(bf16)
        dzag_ref[:, 0:AW] = (dta * u * svv * dsza).astype(bf16)
        dwpb_ref[...] += _tn(tb, dyb)
        dtb = _nt(dyb, wpb_ref[...])
        don = dtb * szb
        dzbua_ref[:, 0:AW] = (dtb * on * dszb).astype(bf16)
        dgbn_ref[...] += jnp.sum(don * ohat, axis=0, keepdims=True)
        doh = don * gbn_ref[...]
        prod = doh * ohat
        mh = jnp.concatenate(
            [jnp.broadcast_to(jnp.mean(prod[:, h * HV:(h + 1) * HV], axis=-1, keepdims=True), (tl, HV)) for h in range(NH)], axis=1)
        do_ref[...] = (rr * (doh - ohat * mh)).astype(bf16)

        @pl.when(pl.program_id(0) == l // tl - 1)
        def _():
            for acc, out in ((dwout_ref, dwout_bf), (dwpa_ref, dwpa_bf), (dwpb_ref, dwpb_bf)):
                out[...] = acc[...].astype(bf16)

    row = lambda w, j: pl.BlockSpec((tl, w), lambda i, j=j: (i, j))
    full = lambda *s: pl.BlockSpec(s, lambda i: (0,) * len(s))
    return pl.pallas_call(
        body, name="mid", grid=(l // tl,),
        out_shape=(jax.ShapeDtypeStruct((l, D), f32), jax.ShapeDtypeStruct((l, 2 * AW), bf16), jax.ShapeDtypeStruct((l, AW + 2 * D), bf16),
                   jax.ShapeDtypeStruct((l, AW), bf16), jax.ShapeDtypeStruct((l, VW), bf16),
                   jax.ShapeDtypeStruct((D, D), bf16), jax.ShapeDtypeStruct((AW, D), bf16), jax.ShapeDtypeStruct((VW, D), bf16),
                   jax.ShapeDtypeStruct((1, D), f32), jax.ShapeDtypeStruct((1, D), f32), jax.ShapeDtypeStruct((1, VW), f32),
                   jax.ShapeDtypeStruct((1, 1), f32)),
        scratch_shapes=[pltpu.VMEM((D, D), f32), pltpu.VMEM((AW, D), f32), pltpu.VMEM((VW, D), f32)],
        in_specs=[row(D, 0), row(D, 0), row(AW, PZB // AW), row(AW, PUA // AW), row(AW, PZA // AW), row(D, PG1 // D), row(D, PG2 // D),
                  row(VW, 0), row(VW, 0), row(AW, 0), full(1, D), full(1, D), full(1, VW), full(AW, D), full(VW, D), full(D, D)],
        out_specs=(row(D, 0), row(2 * AW, 0), row(AW + 2 * D, 0), row(AW, 0), row(VW, 0),
                   full(D, D), full(AW, D), full(VW, D), full(1, D), full(1, D), full(1, VW), full(1, 1)),
        compiler_params=_cp(("arbitrary",)),
    )(x, tgt, p, p, p, p, p, o_f, o_b, sv, gate, gf, gb_norm, w_pa, w_pb, w_out)


def _in_bwd(x, dx1, dqkv, dzbua, dva, dzag, dlr, w_pad, g, scale, tl, comm):
    l = x.shape[0]
    c_in, c_out, c_sems = comm.specs()

    def body(*refs):
        cin = refs[14:14 + len(c_in)]
        outs = refs[14 + len(c_in):]
        comm.run(cin, outs[4:4 + comm.n], outs[4 + comm.n:], l // tl, lambda: compute(*refs[:14], *outs[:4]))

    def compute(x_ref, dx1_ref, a_ref, b_ref, c_ref, e_ref, lr_ref, wa_ref, wb_ref, wc_ref, we_ref, wl_ref, g_ref, sc_ref,
                gx_ref, dsh_ref, dsc_ref, dg_ref):
        @pl.when(pl.program_id(0) == 0)
        def _():
            for r in (dsh_ref, dsc_ref, dg_ref):
                r[...] = jnp.zeros_like(r)

        dh = (_nt(a_ref[...], wa_ref[...]) + _nt(b_ref[...], wb_ref[...]) + _nt(c_ref[...], wc_ref[...]) + _nt(e_ref[...], we_ref[...])
              + _nt(lr_ref[...], wl_ref[...]))
        xv = x_ref[...]
        r = lax.rsqrt(jnp.mean(xv * xv, axis=-1, keepdims=True) + EPS)
        xn = xv * r
        gxn = jnp.sum(dh * xn, axis=0, keepdims=True)
        dsh_ref[...] += jnp.sum(dh, axis=0, keepdims=True)
        dsc_ref[...] += gxn * g_ref[...]
        dg_ref[...] += gxn * (1.0 + sc_ref[...])
        dxn = dh * (g_ref[...] * (1.0 + sc_ref[...]))
        gx_ref[...] = dx1_ref[...] + r * (dxn - xn * jnp.mean(dxn * xn, axis=-1, keepdims=True))

    row = lambda w: pl.BlockSpec((tl, w), lambda i: (i, 0))
    wcol = lambda w, j: pl.BlockSpec((D, w), lambda i, j=j: (0, j))
    vec = pl.BlockSpec((1, D), lambda i: (0, 0))
    return pl.pallas_call(
        body, name="in_bwd", grid=(l // tl,),
        out_shape=(jax.ShapeDtypeStruct((l, D), f32),) + (jax.ShapeDtypeStruct((1, D), f32),) * 3 + tuple(comm.outs),
        in_specs=[row(D), row(D), row(2 * KW + VW), row(2 * AW), row(AW), row(AW + 2 * D), row(LRW),
                  wcol(2 * KW + VW, 0), wcol(2 * AW, PZB // (2 * AW)), wcol(AW, PVA // AW), wcol(AW + 2 * D, PZA // (AW + 2 * D)),
                  wcol(LRW, PLR // LRW), vec, vec] + c_in,
        out_specs=(row(D), vec, vec, vec) + tuple(c_out), scratch_shapes=c_sems,
        compiler_params=_cp(("arbitrary",)),
    )(x, dx1, dqkv, dzbua, dva, dzag, dlr, w_pad, w_pad, w_pad, w_pad, w_pad, g, scale, *comm.ins)


def _tn_matmul(a, bs, tl, name, comm=None, pack=None):
    l, m = a.shape
    k = len(bs)
    comm = comm or _Comm([], [], None)
    c_in, c_out, c_sems = comm.specs()
    acc_shapes = [(m, b.shape[1]) for b in bs]
    n_extra = len(pack.extra) if pack else 0
    n_res = len(pack.outs) if pack else k

    def body(a_ref, *refs):
        b_refs, refs = refs[:k], refs[k:]
        extra, refs = refs[:n_extra], refs[n_extra:]
        cin, refs = refs[:len(c_in)], refs[len(c_in):]
        res, refs = refs[:n_res], refs[n_res:]
        cout, refs = refs[:comm.n], refs[comm.n:]
        accs, sems = (refs[:k], refs[k:]) if pack else (res, refs)

        def compute():
            @pl.when(pl.program_id(0) == 0)
            def _():
                for o_ref in accs:
                    o_ref[...] = jnp.zeros_like(o_ref)

            av = a_ref[...]
            for b_ref, o_ref in zip(b_refs, accs):
                o_ref[...] += _tn(av, b_ref[...])
            if pack:
                pl.when(pl.program_id(0) == l // tl - 1)(lambda: pack.fn(accs, extra, res))

        comm.run(cin, cout, sems, l // tl, compute)

    full = lambda shape: pl.BlockSpec(shape, lambda i: (0,) * len(shape))
    res_shapes = list(pack.outs) if pack else [jax.ShapeDtypeStruct(sh, f32) for sh in acc_shapes]
    return pl.pallas_call(
        body, name=name, grid=(l // tl,), out_shape=tuple(res_shapes) + tuple(comm.outs),
        in_specs=[pl.BlockSpec((tl, m), lambda i: (i, 0))] + [pl.BlockSpec((tl, b.shape[1]), lambda i: (i, 0)) for b in bs]
        + [full(e.shape) for e in (pack.extra if pack else [])] + c_in,
        out_specs=tuple(full(r.shape) for r in res_shapes) + tuple(c_out),
        scratch_shapes=([pltpu.VMEM(sh, f32) for sh in acc_shapes] if pack else []) + c_sems, compiler_params=_cp(("arbitrary",)),
    )(a, *bs, *(pack.extra if pack else []), *comm.ins)


EARLY_A_ROWS = 512


class _NoExchange:
    def __init__(self, w_pa, w_pb, w_out):
        self.weights = (w_pa, w_pb, w_out)

    def gather_proj(self):
        return _Comm([], [], None)

    def proj_weights(self, got):
        return self.weights

    def first(self, dw_out, dw_pa, dw_pb, small):
        return _Comm([], [], None)

    def early_a(self, blocks):
        return _Comm([], [], None)

    def early_b(self, blocks):
        return _Comm([], [], None)

    def late(self, blocks, dgt):
        return _Comm([], [], None)


class _Exchanges:
    def __init__(self, pa, pb, wo):
        self.shards = (pa, pb, wo)

    def gather_proj(self):
        def plan(i, o):
            srcs = [lambda j, r=r: r for r in i]
            dsts = [lambda j: o[0].at[:, _lanes(j)], lambda j: o[1].at[:, _lanes(j)], lambda j: o[2].at[j]]
            return srcs, dsts, None
        sds = jax.ShapeDtypeStruct
        return _Comm(self.shards, [sds((AW, D), bf16), sds((VW, D), bf16), sds((NDEV, 128, D), bf16)], plan)

    def proj_weights(self, got):
        return got[0], got[1], got[2].reshape(D, D)

    def first(self, dw_out, dw_pa, dw_pb, small):
        def plan(i, o):
            srcs = [lambda j: i[0].at[j], lambda j: i[1].at[:, _lanes(j)], lambda j: i[2].at[:, _lanes(j)], lambda j: i[3]]
            dsts = [lambda j, r=r: r.at[j] for r in o]
            return srcs, dsts, None
        sds = jax.ShapeDtypeStruct
        return _Comm([dw_out.reshape(NDEV, 128, D), dw_pa, dw_pb, small],
                     [sds((NDEV, 128, D), bf16), sds((NDEV, AW, 128), bf16), sds((NDEV, VW, 128), bf16),
                      sds((NDEV,) + small.shape, f32)], plan)

    def early_a(self, blocks):
        def plan(i, o):
            return [lambda j: i[0].at[j]], [lambda j: o[0].at[j]], [lambda j: j >= LATE_DESTS - 1]
        return _Comm([blocks], [jax.ShapeDtypeStruct(blocks.shape, bf16)], plan)

    def early_b(self, blocks):
        def plan(i, o):
            return [lambda j: i[0].at[j]], [lambda j: o[0].at[j]], [lambda j: j >= LATE_DESTS - 1]
        return _Comm([blocks], [jax.ShapeDtypeStruct(blocks.shape, bf16)], plan)

    def late(self, blocks, dgt):
        return _LateComm(blocks, dgt)


def _local_step(x, ctx, tgt, mod, modc, norm_g, w_pad, ln_g, ln_b, ws, bs, gate_w, gb_norm, gf, xch):
    shift, scale, gate = mod[:, 0:D], mod[:, D:2 * D], mod[:, 2 * D:]
    shift_c, scale_c = modc[:, 0:D], modc[:, D:]
    w2f, w2b, gbf, gbb = gate_w

    p, h, *got_proj = _in_proj(x, norm_g, scale, shift, w_pad, 512, xch.gather_proj())
    w_pa, w_pb, w_out = xch.proj_weights(got_proj)
    sc_f, sc_b = _ctx_fwd(ctx, norm_g, scale_c, shift_c, w_pad, w2f, w2b, gbf, gbb)
    o_f, st_f = _gla_fwd(p, w2f, gbf, sc_f, False, 512, "gla_fwd_f")
    o_b, st_b = _gla_fwd(p, w2b, gbb, sc_b, True, 512, "gla_fwd_b")
    sv = _mix_fwd(p, ln_g, ln_b, ws.astype(bf16), bs.T)
    (dx1, dzbua, dzag, dsv, do, dw_out, dw_pa, dw_pb, dgf, dgate, dgbn, loss) = _mid(
        x, tgt, p, o_f, o_b, sv, gate, gf, gb_norm, w_pa, w_pb, w_out, 256)
    dva, dws, dbs_acc, dln_g, dln_b = _mix_bwd(p, dsv, ln_g, ln_b, jnp.swapaxes(ws, 1, 2).astype(bf16))
    small = _rows128(dln_g, dln_b, dws, jnp.sum(dbs_acc, axis=-1), dgbn, dgf, jnp.broadcast_to(loss, (1, 128)))
    blocks_a, blocks_b, *got_first = _tn_matmul(h, [dzbua, dva, dzag], 1024, "dw_early", xch.first(dw_out, dw_pa, dw_pb, small),
                                               _pack_early())

    dqkv_f, dlr_f, dw2f, dgbf, dsc_f, *got_a = _gla_bwd(p, do, st_f, w2f, gbf, None, False, 512, "gla_bwd_f",
                                                        xch.early_a(blocks_a))
    dqkv, dlr, dw2b, dgbb, dsc_b, *got_b = _gla_bwd(p, do, st_b, w2b, gbb, (dqkv_f, dlr_f), True, 512, "gla_bwd_b",
                                                    xch.early_b(blocks_b))
    dwk_c, dwv_c, dwl_c, dmodc, dg_c, dw2c, dgbc, dgt = _ctx_bwd(ctx, norm_g, scale_c, shift_c, w_pad, w2f, w2b, gbf, gbb, dsc_f, dsc_b,
                                                                (dw2f, dw2b), (dgbf, dgbb))
    (blocks_late,) = _tn_matmul(h, [dqkv, dlr], 1024, "dw_qkv_lr", pack=_pack_late(dwk_c, dwv_c, dwl_c))
    dw2 = jnp.stack([dw2f[0:RANK] + dw2c[0, 0:RANK], dw2b[RANK:2 * RANK] + dw2c[1, RANK:2 * RANK]])
    dgb = jnp.concatenate([dgbf + dgbc[0], dgbb + dgbc[1]], axis=0)
    gx, dshift, dscale, dg, *got_late = _in_bwd(x, dx1, dqkv, dzbua, dva, dzag, dlr, w_pad, norm_g, scale, 512,
                                                xch.late(blocks_late, dgt))
    return dict(loss=loss, gx=gx, dmod=jnp.concatenate([dshift, dscale, dgate], axis=1), dmodc=dmodc, dnorm_g=dg + dg_c,
                dnorm_parts=(dg, dg_c), dmod_parts=(dshift, dscale, dgate),
                small=small, blocks_a=blocks_a, blocks_b=blocks_b, blocks_late=blocks_late, dw2=dw2, dgb=dgb,
                dw_pa=dw_pa, dw_pb=dw_pb, dw_out=dw_out, got_first=got_first, got_a=got_a, got_b=got_b, got_late=got_late)


def _rows128(*vs):
    out = []
    for t in vs:
        t = t.reshape(-1)
        pad = (-t.shape[0]) % 128
        out.append(jnp.pad(t, (0, pad)) if pad else t)
    return jnp.concatenate(out).reshape(-1, 128)


def kernel(x, c, ctx, c_ctx, w_mod, b_mod, norm_g, w_in, a_ln_g, a_ln_b, a_ws, a_bs, b_gate_w2, b_gate_b, b_norm_g, w_proj_a, w_proj_b, w_out, final_norm_g, loss_target, m_c_ctx, m_w_mod, m_b_mod, m_norm_g, m_w_in, m_a_ln_g, m_a_ln_b, m_a_ws, m_a_bs, m_b_gate_w2, m_b_gate_b, m_b_norm_g, m_w_proj_a, m_w_proj_b, m_w_out, m_final_norm_g, v_c_ctx, v_w_mod, v_b_mod, v_norm_g, v_w_in, v_a_ln_g, v_a_ln_b, v_a_ws, v_a_bs, v_b_gate_w2, v_b_gate_b, v_b_norm_g, v_w_proj_a, v_w_proj_b, v_w_out, v_final_norm_g):
    me = _me()

    cs, mods, wg, gates = _gather_first(c, c_ctx.reshape(1, D), w_mod[0], b_mod, w_in[0].astype(bf16), b_gate_w2, b_gate_b)
    w_pad, *gate_w = _repack_w(wg, gates)

    mods = jnp.transpose(mods, (1, 0, 2)).reshape(16, 3 * D)
    mod = lax.dynamic_slice(mods, (me, 0), (1, 3 * D))
    modc = mods[8:9, 0:2 * D]

    xch = _Exchanges(w_proj_a[0].astype(bf16), w_proj_b[0].astype(bf16), w_out[0].astype(bf16))
    r = _local_step(x[0], ctx[0], loss_target[0], mod, modc, norm_g, w_pad, a_ln_g, a_ln_b, a_ws[0], a_bs[0], gate_w,
                    b_norm_g, final_norm_g.reshape(1, D), xch)
    p_out, p_pa, p_pb, smalls_e = r["got_first"]
    (p_in_a,) = r["got_a"]
    (p_in_b,) = r["got_b"]
    _, _, p_in_late, p_gt = r["got_late"]

    n_e = (AW + AW + 4 * ACH * ACH + AW + VW + D) // 128
    row = lambda t: t.reshape(1, D)
    rep_e = _adam_params(smalls_e, [a_ln_g, a_ln_b, a_ws, a_bs, b_norm_g, row(final_norm_g)],
                         [m_a_ln_g, m_a_ln_b, m_a_ws, m_a_bs, m_b_norm_g, row(m_final_norm_g)],
                         [v_a_ln_g, v_a_ln_b, v_a_ws, v_a_bs, v_b_norm_g, row(v_final_norm_g)], "adam_rep_early")
    rep_e = [t[0:5] + (t[5].reshape(D),) for t in rep_e]
    loss = jnp.sum(smalls_e[:, n_e, 0])

    mod_grads = _mod_tail(r["dnorm_parts"], r["dmod_parts"], r["dmodc"], cs, w_mod)
    mod_res = _adam_plain(mod_grads, [w_mod, norm_g, b_mod, row(c_ctx)], [m_w_mod, m_norm_g, m_b_mod, row(m_c_ctx)],
                          [v_w_mod, v_norm_g, v_b_mod, row(v_c_ctx)], "adam_mod")
    wm, ng, bmod_r, cc = ([t[i] for t in mod_res] for i in range(4))

    a_in = _adam_w_in(p_in_a, p_in_b, p_in_late, w_in, m_w_in, v_w_in)
    blk = _adam_blocks([p_pa, p_pb, p_out], [w_proj_a, w_proj_b, w_out], [m_w_proj_a, m_w_proj_b, m_w_out],
                       [v_w_proj_a, v_w_proj_b, v_w_out], "adam_proj_out")
    a_pa, a_pb, a_out = ([t[i] for t in blk] for i in range(3))
    a_gt = _adam_gate(p_gt, b_gate_w2, b_gate_b, m_b_gate_w2, m_b_gate_b, v_b_gate_w2, v_b_gate_b)
    sh = [(a_in[k], a_pa[k], a_pb[k], a_out[k], a_gt[k], a_gt[4 + k]) for k in range(4)]

    outs = [loss, r["gx"][None]]
    for k in range(4):
        lg, lb, aws, abs_, bng, fng = rep_e[k]
        n_g, bmod = ng[k], bmod_r[k]
        s_in, s_pa, s_pb, s_out, s_w2, s_gb = sh[k]
        outs += [cc[k].reshape(D), wm[k], bmod, n_g, s_in, lg, lb, aws, abs_, s_w2, s_gb, bng, s_pa, s_pb, s_out, fng]
    return tuple(outs)
```

```python
import jax
import jax.numpy as jnp
from jax import lax
from jax.experimental import pallas as pl
from jax.experimental.pallas import tpu as pltpu

f32, bf16 = jnp.float32, jnp.bfloat16

D = 1024
CTX = 256
EPS = 1e-6
AW = 512
ACH = 128
GW = 64
KW = 256
VW = 512
NH = 4
HK = 64
HV = 128
RANK = 16
TAU = 16.0
CH = 64
QSCALE = HK ** -0.5
INW = 5152
NDEV = 8

PQ, PK, PV, PZB, PUA, PVA, PZA, PG1, PG2, PLR, PW = 0, 256, 512, 1024, 1536, 2048, 2560, 3072, 4096, 5120, 5248
LRW = 128

ADAM_LR, ADAM_B1, ADAM_B2, ADAM_EPS, ADAM_WD, ADAM_STEP = 0.001, 0.9, 0.999, 1e-08, 0.01, 10

VMEM_LIMIT = 56 * 1024 * 1024
MESH = pl.DeviceIdType.MESH


def _cp(sem=None):
    return pltpu.CompilerParams(dimension_semantics=sem, vmem_limit_bytes=VMEM_LIMIT)


def _dot(a, b):
    return jnp.dot(a.astype(bf16), b.astype(bf16), preferred_element_type=f32)


def _nt(a, b):
    return lax.dot_general(a.astype(bf16), b.astype(bf16), (((1,), (1,)), ((), ())), preferred_element_type=f32)


def _tn(a, b):
    return lax.dot_general(a.astype(bf16), b.astype(bf16), (((0,), (0,)), ((), ())), preferred_element_type=f32)


def _dot_hi(a, b):
    return jnp.dot(a, b, preferred_element_type=f32, precision=lax.Precision.HIGHEST)


def _sigmoid(x):
    return 1.0 / (1.0 + jnp.exp(-x))


def _log_sigmoid(x):
    return jnp.minimum(x, 0.0) - jnp.log(1.0 + jnp.exp(-jnp.abs(x)))


def _silu_and_grad(z):
    s = _sigmoid(z)
    return z * s, s * (1.0 + z * (1.0 - s))


def _me():
    return 4 * lax.axis_index("x") + 2 * lax.axis_index("y") + lax.axis_index("c")


def _peer(k):
    x, y, c = lax.axis_index("x"), lax.axis_index("y"), lax.axis_index("c")
    px = 1 - x if k & 4 else x
    py = 1 - y if k & 2 else y
    pc = 1 - c if k & 1 else c
    return (px, py, pc), 4 * px + 2 * py + pc


def _fanout(srcs, dsts, send_sems, recv_sems, local_sems, owners=None):
    me = _me()
    n = len(srcs)
    owns = lambda a, j: True if owners is None or owners[a] is None else owners[a](j)

    def guarded(cond, fn):
        if cond is True:
            fn()
        else:
            pl.when(cond)(fn)

    def copies(with_recvs):
        local = [pltpu.make_async_copy(srcs[a](me), dsts[a](me), local_sems.at[a]) for a in range(n)]
        sends, recvs = [], []
        for k in range(1, NDEV):
            dev, idx = _peer(k)
            for a in range(n):
                s = (k - 1) * n + a
                sends.append((owns(a, idx), pltpu.make_async_remote_copy(
                    src_ref=srcs[a](idx), dst_ref=dsts[a](me), send_sem=send_sems.at[s], recv_sem=recv_sems.at[s],
                    device_id=dev, device_id_type=MESH)))
                if with_recvs:
                    recvs.append((owns(a, me), pltpu.make_async_remote_copy(
                        src_ref=srcs[a](idx), dst_ref=dsts[a](idx), send_sem=send_sems.at[s], recv_sem=recv_sems.at[s],
                        device_id=dev, device_id_type=MESH)))
        return local, sends, recvs

    def start():
        local, sends, _ = copies(False)
        for a, cp in enumerate(local):
            guarded(owns(a, me), cp.start)
        for cond, cp in sends:
            guarded(cond, cp.start)

    def finish():
        local, sends, recvs = copies(True)
        for cond, cp in recvs:
            guarded(cond, cp.wait_recv)
        for cond, cp in sends:
            guarded(cond, cp.wait_send)
        for a, cp in enumerate(local):
            guarded(owns(a, me), cp.wait)

    return start, finish


class _Comm:
    def __init__(self, ins, outs, plan):
        self.ins, self.outs, self.plan = list(ins), list(outs), plan
        self.n = len(self.outs)

    def specs(self):
        hbm = pl.BlockSpec(memory_space=pl.ANY)
        return [hbm] * len(self.ins), [hbm] * self.n, _fanout_sems(self.n) if self.n else []

    def run(self, in_refs, out_refs, sems, nsteps, compute):
        if not self.n:
            compute()
            return

        def hooks():
            srcs, dsts, owners = self.plan(in_refs, out_refs)
            return _fanout(srcs, dsts, sems[0], sems[1], sems[2], owners)

        pl.when(pl.program_id(0) == 0)(lambda: hooks()[0]())
        compute()
        pl.when(pl.program_id(0) == nsteps - 1)(lambda: hooks()[1]())


LATE_MID_STEP = 1


class _LateComm:
    def __init__(self, blocks, dgt):
        sds = jax.ShapeDtypeStruct
        self.ins = [blocks, dgt]
        self.outs = [sds((D, SHARD), bf16), sds((D, SHARD), bf16), sds((4, D, SHARD), bf16), sds(dgt.shape, f32)]
        self.n = len(self.outs)

    def specs(self):
        hbm = pl.BlockSpec(memory_space=pl.ANY)
        scratch = [pltpu.VMEM((3, D, SHARD), bf16), pltpu.SemaphoreType.DMA((2,)), pltpu.SemaphoreType.DMA((4,)),
                   pltpu.SemaphoreType.DMA((3,))] + _fanout_sems(1)
        return [hbm] * 2, [hbm] * self.n, scratch

    def run(self, in_refs, out_refs, scratch, nsteps, compute):
        late_ref, dgt_ref = in_refs
        sib_ref, pair_ref, parts_ref, gt_ref = out_refs
        vbuf, send_sems, recv_sems, local_sems, g_send, g_recv, g_local = scratch
        x, y, c = lax.axis_index("x"), lax.axis_index("y"), lax.axis_index("c")
        chip = 2 * x + y
        is_owner_chip = chip == 0
        step = pl.program_id(0)

        def to_sibling():
            return pltpu.make_async_remote_copy(src_ref=late_ref.at[1 - c], dst_ref=sib_ref, send_sem=send_sems.at[0],
                                                recv_sem=recv_sems.at[0], device_id=(x, y, 1 - c), device_id_type=MESH)

        def to_owner(k):
            return pltpu.make_async_remote_copy(src_ref=pair_ref, dst_ref=parts_ref.at[k], send_sem=send_sems.at[1],
                                                recv_sem=recv_sems.at[k], device_id=(0, 0, c), device_id_type=MESH)

        def own_copy():
            return pltpu.make_async_copy(pair_ref, parts_ref.at[0], local_sems.at[2])

        def gates():
            return _fanout([lambda j: dgt_ref.at[j]], [lambda j: gt_ref.at[j]], g_send, g_recv, g_local)

        @pl.when(step == 0)
        def _():
            to_sibling().start()
            gates()[0]()

        compute()

        @pl.when(step == LATE_MID_STEP)
        def _():
            mine = pltpu.make_async_copy(late_ref.at[c], vbuf.at[0], local_sems.at[0])
            mine.start()
            to_sibling().wait_recv()
            theirs = pltpu.make_async_copy(sib_ref, vbuf.at[1], local_sems.at[1])
            theirs.start()
            mine.wait()
            theirs.wait()
            vbuf[2] = (vbuf[0].astype(f32) + vbuf[1].astype(f32)).astype(bf16)
            pltpu.sync_copy(vbuf.at[2], pair_ref)
            pl.when(is_owner_chip)(lambda: own_copy().start())
            pl.when(jnp.logical_not(is_owner_chip))(lambda: to_owner(chip).start())

        @pl.when(step == nsteps - 1)
        def _():
            @pl.when(is_owner_chip)
            def _():
                for k in range(1, 4):
                    to_owner(k).wait_recv()
                own_copy().wait()

            pl.when(jnp.logical_not(is_owner_chip))(lambda: to_owner(chip).wait_send())
            to_sibling().wait_send()
            gates()[1]()


def _fanout_sems(n):
    return [pltpu.SemaphoreType.DMA(((NDEV - 1) * n,)), pltpu.SemaphoreType.DMA(((NDEV - 1) * n,)), pltpu.SemaphoreType.DMA((n,))]


def _lanes(j):
    return pl.ds(pl.multiple_of(j * 128, 128), 128)


def _gather_first(c_row, cctx_row, wm, bm, wi, w2, gb):
    per = 128 // 32
    gate_rows = 2 * RANK // per + 1

    def body(c_ref, cctx_ref, wm_ref, bm_ref, wi_ref, w2_ref, gb_ref, cs_ref, mods_ref, owi, og, g_ref, call_ref, mine_ref,
             send_sems, recv_sems, local_sems, c_send, c_recv, c_local, m_send, m_recv, m_local):
        g_ref[...] = jnp.zeros_like(g_ref)
        for d in range(2):
            for r in range(RANK):
                q = d * RANK + r
                g_ref[q // per:q // per + 1, (q % per) * 32:(q % per + 1) * 32] = w2_ref[0, d, r:r + 1, :]
            g_ref[gate_rows - 1:gate_rows, d * 32:(d + 1) * 32] = gb_ref[0, d:d + 1, :]
        x, y, c = lax.axis_index("x"), lax.axis_index("y"), lax.axis_index("c")
        sibling = (x, y, 1 - c)
        chips = [(1 - x, y), (x, 1 - y), (1 - x, 1 - y)]
        index = lambda px, py, pc: 4 * px + 2 * py + pc
        arrays = ((wi_ref, owi), (g_ref, og))
        n = len(arrays)

        def copy(a, k, block, to, own=False):
            src, out = arrays[a]
            return pltpu.make_async_remote_copy(
                src_ref=src if own else out.at[index(*block)], dst_ref=out.at[index(*block)],
                send_sem=send_sems.at[k * n + a], recv_sem=recv_sems.at[k * n + a], device_id=to, device_id_type=MESH)

        c_start, c_finish = _fanout([lambda j: c_ref], [lambda j: call_ref.at[j]], c_send, c_recv, c_local)
        c_start()
        mine = [pltpu.make_async_copy(src, out.at[index(x, y, c)], local_sems.at[a]) for a, (src, out) in enumerate(arrays)]
        first = [copy(a, 0, (x, y, c), sibling, own=True) for a in range(n)]
        first += [copy(a, 1 + j, (x, y, c), (*chip, c), own=True) for j, chip in enumerate(chips) for a in range(n)]
        for cp in mine + first:
            cp.start()

        c_finish()
        cs = jnp.concatenate([call_ref[j] for j in range(NDEV)] + [cctx_ref[...], jnp.zeros((16 - NDEV - 1, D), f32)], axis=0)
        cs_ref[...] = cs
        s, _ = _silu_and_grad(cs)
        mine_ref[...] = _dot_hi(s, wm_ref[...]) + bm_ref[:, pl.ds(pl.multiple_of(_me() * ncol, 128), ncol)]
        m_start, m_finish = _fanout([lambda j: mine_ref], [lambda j: mods_ref.at[j]], m_send, m_recv, m_local)
        m_start()

        passed = []
        for j, chip in enumerate(chips):
            for a in range(n):
                copy(a, 1 + j, (*chip, c), (x, y, c)).wait_recv()
            for a in range(n):
                cp = copy(a, 4 + j, (*chip, c), sibling)
                cp.start()
                passed.append(cp)
        for a in range(n):
            copy(a, 0, sibling, (x, y, c)).wait_recv()
        for j, chip in enumerate(chips):
            for a in range(n):
                copy(a, 4 + j, (*chip, 1 - c), (x, y, c)).wait_recv()
        for cp in first + passed:
            cp.wait_send()
        for cp in mine:
            cp.wait()
        m_finish()

    hbm = pl.BlockSpec(memory_space=pl.ANY)
    vm = pl.BlockSpec(memory_space=pltpu.VMEM)
    ncol = wm.shape[1]
    return pl.pallas_call(
        body, name="gather_first",
        out_shape=(jax.ShapeDtypeStruct((16, D), f32), jax.ShapeDtypeStruct((NDEV, 16, ncol), f32),
                   jax.ShapeDtypeStruct((NDEV,) + wi.shape, bf16), jax.ShapeDtypeStruct((NDEV, gate_rows, 128), f32)),
        in_specs=[vm, vm, vm, vm, hbm, vm, vm], out_specs=(vm, vm, hbm, hbm),
        scratch_shapes=[pltpu.VMEM((gate_rows, 128), f32), pltpu.VMEM((NDEV, 1, D), f32), pltpu.VMEM((16, ncol), f32)]
        + _fanout_sems(2) + _fanout_sems(1) + _fanout_sems(1),
        compiler_params=_cp(),
    )(c_row, cctx_row, wm, bm, wi, w2, gb)


SHARD = INW // NDEV
ROWS_RP = 128


def _overlap(lo, hi, a, b):
    s, e = max(lo, a), min(hi, b)
    return (s, e) if s < e else None


def _repack_w(wg, gates):
    segs = ((0, 1024, PQ), (1024, 1024 + 2 * RANK, PLR), (1024 + 2 * RANK, INW, PZB))
    per = 128 // 32

    def body(g_ref, gt_ref, o_ref, w2f_ref, w2b_ref, gbf_ref, gbb_ref):
        @pl.when(pl.program_id(0) == 0)
        def _():
            w2f_ref[...] = jnp.zeros_like(w2f_ref)
            w2b_ref[...] = jnp.zeros_like(w2b_ref)
            for dev in range(NDEV):
                cols = slice(dev * 32, (dev + 1) * 32)
                for d, (w2_ref, gb_ref) in enumerate(((w2f_ref, gbf_ref), (w2b_ref, gbb_ref))):
                    for r in range(RANK):
                        q = d * RANK + r
                        w2_ref[q:q + 1, cols] = gt_ref[dev, q // per:q // per + 1, (q % per) * 32:(q % per + 1) * 32]
                    gb_ref[:, cols] = gt_ref[dev, 2 * RANK // per:2 * RANK // per + 1, d * 32:(d + 1) * 32]

        for j in range(NDEV):
            lo, hi = j * SHARD, (j + 1) * SHARD
            for a, b, pad0 in segs:
                ov = _overlap(lo, hi, a, b)
                if ov:
                    s, e = ov
                    o_ref[:, pad0 + s - a:pad0 + e - a] = g_ref[j, :, s - lo:e - lo]
        o_ref[:, PLR + 2 * RANK:PW] = jnp.zeros((ROWS_RP, PW - PLR - 2 * RANK), bf16)

    sds = jax.ShapeDtypeStruct
    full = lambda *s: pl.BlockSpec(s, lambda i: (0,) * len(s))
    return pl.pallas_call(
        body, name="repack_w", grid=(D // ROWS_RP,),
        out_shape=(sds((D, PW), bf16), sds((LRW, KW), f32), sds((LRW, KW), f32), sds((1, KW), f32), sds((1, KW), f32)),
        in_specs=[pl.BlockSpec((NDEV, ROWS_RP, SHARD), lambda i: (0, i, 0)), full(*gates.shape)],
        out_specs=(pl.BlockSpec((ROWS_RP, PW), lambda i: (i, 0)), full(LRW, KW), full(LRW, KW), full(1, KW), full(1, KW)),
        compiler_params=_cp(("arbitrary",)),
    )(wg, gates)


LATE_END = 1024 + 2 * RANK
LATE_DESTS = 2


def _pack_blocks(o_ref, srcs, dests, dtype):
    for n, j in enumerate(dests):
        lo, hi = j * SHARD, (j + 1) * SHARD
        done = lo
        for a, b, src in srcs:
            ov = _overlap(lo, hi, a, b)
            if ov:
                s, e = ov
                if s > done:
                    o_ref[n, :, done - lo:s - lo] = jnp.zeros((ROWS_RP, s - done), dtype)
                o_ref[n, :, s - lo:e - lo] = src[:, s - a:e - a].astype(dtype)
                done = e
        if done < hi:
            o_ref[n, :, done - lo:hi - lo] = jnp.zeros((ROWS_RP, hi - done), dtype)


class _Pack:
    def __init__(self, extra, outs, fn):
        self.extra, self.outs, self.fn = list(extra), list(outs), fn


def _pack_early():
    def fn(accs, extra, outs):
        zbua, va, zag = accs
        for r0 in range(0, D, ROWS_RP):
            rows = pl.ds(r0, ROWS_RP)
            out, o0 = (outs[0], r0) if r0 < EARLY_A_ROWS else (outs[1], r0 - EARLY_A_ROWS)
            _pack_blocks(out.at[:, pl.ds(o0, ROWS_RP)], ((LATE_END, 2080, zbua.at[rows]), (2080, 2592, va.at[rows]), (2592, INW, zag.at[rows])),
                         range(NDEV), bf16)

    sds = jax.ShapeDtypeStruct
    return _Pack([], [sds((NDEV, EARLY_A_ROWS, SHARD), bf16), sds((NDEV, D - EARLY_A_ROWS, SHARD), bf16)], fn)


def _pack_late(dwk_c, dwv_c, dwl_c):
    def fn(accs, extra, outs):
        qkv_ref, lr_ref = accs
        kc_ref, vc_ref, lc_ref = extra
        for r0 in range(0, D, ROWS_RP):
            rows = pl.ds(r0, ROWS_RP)
            qkv = qkv_ref[rows, :] + jnp.concatenate([jnp.zeros((ROWS_RP, KW), f32), kc_ref[rows, :], vc_ref[rows, :]], axis=1)
            lr = lr_ref[rows, :] + lc_ref[rows, :]
            _pack_blocks(outs[0].at[:, rows], ((0, 1024, qkv), (1024, LATE_END, lr)), range(LATE_DESTS), bf16)

    return _Pack([dwk_c, dwv_c, dwl_c], [jax.ShapeDtypeStruct((LATE_DESTS, D, SHARD), bf16)], fn)


def _mod_tail(dnorm_parts, dmod_parts, dmodc, cs, wm):
    ncol = wm.shape[2]
    n_dn, n_dm = len(dnorm_parts), len(dmod_parts)

    def body(*refs):
        dn_parts, refs = refs[:n_dn], refs[n_dn:]
        dm_parts, refs = refs[:n_dm], refs[n_dm:]
        dmodc_ref, cs_ref, wm_ref, g_wm, g_ng, g_bm, g_cc = refs[:7]
        dn_ref, dmod_ref, a_dn, a_dmod, a_dmodc, dm_rows, gc_ref, a_gc = refs[7:15]
        s1, r1, l1, s2, r2, l2 = refs[15:21]
        dn = dn_parts[0][...]
        for ref in dn_parts[1:]:
            dn = dn + ref[...]
        dn_ref[...] = dn
        dmod_ref[...] = jnp.concatenate([ref[...] for ref in dm_parts], axis=1)
        start, finish = _fanout([lambda j: dn_ref, lambda j: dmod_ref, lambda j: dmodc_ref],
                                [lambda j, r=r: r.at[j] for r in (a_dn, a_dmod, a_dmodc)], s1, r1, l1)
        start()
        finish()

        def total(ref):
            t = ref[0]
            for j in range(1, NDEV):
                t = t + ref[j]
            return t

        g_ng[...] = total(a_dn)
        dmodc_tot = jnp.concatenate([total(a_dmodc), jnp.zeros((1, D), f32)], axis=1)
        g_bm[...] = total(a_dmod) + dmodc_tot
        dm_rows[...] = jnp.concatenate([a_dmod[j] for j in range(NDEV)] + [dmodc_tot, jnp.zeros((16 - NDEV - 1, 3 * D), f32)], axis=0)
        dm = dm_rows[:, pl.ds(pl.multiple_of(_me() * ncol, 128), ncol)]
        s, ds = _silu_and_grad(cs_ref[...])
        part = lax.dot_general(dm[8:9, :], wm_ref[0], (((1,), (1,)), ((), ())), preferred_element_type=f32,
                               precision=lax.Precision.HIGHEST)
        gc_ref[...] = part * ds[8:9, :]
        start2, finish2 = _fanout([lambda j: gc_ref], [lambda j: a_gc.at[j]], s2, r2, l2)
        start2()
        g = lax.dot_general(s, dm, (((0,), (0,)), ((), ())), preferred_element_type=f32, precision=lax.Precision.HIGHEST)
        g_wm[...] = g[None]
        finish2()
        g_cc[...] = total(a_gc)

    sds = jax.ShapeDtypeStruct
    row = lambda n: pltpu.VMEM((NDEV, 1, n), f32)
    return pl.pallas_call(
        body, name="mod_tail", out_shape=(sds(wm.shape, f32), sds((1, D), f32), sds((1, 3 * D), f32), sds((1, D), f32)),
        scratch_shapes=[pltpu.VMEM((1, D), f32), pltpu.VMEM((1, 3 * D), f32), row(D), row(3 * D), row(2 * D),
                        pltpu.VMEM((16, 3 * D), f32), pltpu.VMEM((1, D), f32), row(D)] + _fanout_sems(3) + _fanout_sems(1),
        compiler_params=_cp(),
    )(*dnorm_parts, *dmod_parts, dmodc, cs, wm)


def _adam_update(g, w_ref, m_ref, v_ref, go_ref, d_ref, mo_ref, vo_ref):
    c1 = 1.0 / (1.0 - ADAM_B1 ** ADAM_STEP)
    c2 = 1.0 / (1.0 - ADAM_B2 ** ADAM_STEP)
    mn = ADAM_B1 * m_ref[...] + (1.0 - ADAM_B1) * g
    vn = ADAM_B2 * v_ref[...] + (1.0 - ADAM_B2) * (g * g)
    go_ref[...] = g
    mo_ref[...] = mn
    vo_ref[...] = vn
    d_ref[...] = -ADAM_LR * ((mn * c1) / (jnp.sqrt(vn * c2) + ADAM_EPS) + ADAM_WD * w_ref[...])


def _adam_w_in(parts_a, parts_b, parts_late, w, m, v):
    na = EARLY_A_ROWS // ROWS_RP
    n = D // ROWS_RP
    whole = SHARD // 128 * 128

    def body(a_ref, b_ref, l_ref, w_hbm, m_hbm, v_hbm, go_hbm, d_hbm, mo_hbm, vo_hbm, ibuf, obuf, isem, osem):
        step = pl.program_id(0)

        def cols(t):
            return pl.ds(pl.multiple_of(t * ROWS_RP, ROWS_RP), ROWS_RP)

        def fetch(t):
            return [pltpu.make_async_copy(r.at[:, 0, cols(t)], ibuf.at[t % 2, k], isem.at[t % 2, k])
                    for k, r in enumerate((w_hbm, m_hbm, v_hbm))]

        def flush(t):
            return [pltpu.make_async_copy(obuf.at[t % 2, k], r.at[:, 0, cols(t)], osem.at[t % 2, k])
                    for k, r in enumerate((go_hbm, d_hbm, mo_hbm, vo_hbm))]

        def start(cps):
            for cp in cps:
                cp.start()

        def wait(cps):
            for cp in cps:
                cp.wait()

        pl.when(step == 0)(lambda: start(fetch(step)))
        pl.when(step + 1 < n)(lambda: start(fetch(step + 1)))
        me = _me()
        first = step < na
        early = jnp.where(first, a_ref[0], b_ref[0]).astype(f32)
        for i in range(1, NDEV):
            early = early + jnp.where(first, a_ref[i], b_ref[i]).astype(f32)
        late = l_ref[0].astype(f32)
        for i in range(1, 4):
            late = late + l_ref[i].astype(f32)
        g = jnp.where(me >= LATE_DESTS - 1, early, 0.0) + jnp.where(me < LATE_DESTS, late, 0.0)
        gt = jnp.concatenate([g[:, c:c + 128].T for c in range(0, whole, 128)] + [g[:, SHARD - 128:].T[128 - (SHARD - whole):]],
                             axis=0)
        wait(fetch(step))
        pl.when(step >= 2)(lambda: wait(flush(step - 2)))
        slot = step % 2
        _adam_update(gt, *(ibuf.at[slot, k] for k in range(3)), *(obuf.at[slot, k] for k in range(4)))
        start(flush(step))

        @pl.when(step == n - 1)
        def _():
            wait(flush(step - 1))
            wait(flush(step))

    hbm = pl.BlockSpec(memory_space=pl.ANY)
    res = pl.pallas_call(
        body, name="adam_w_in", grid=(n,), out_shape=tuple(jax.ShapeDtypeStruct((SHARD, 1, D), f32) for _ in range(4)),
        in_specs=[pl.BlockSpec((NDEV, ROWS_RP, SHARD), lambda i: (0, jnp.minimum(i, na - 1), 0)),
                  pl.BlockSpec((NDEV, ROWS_RP, SHARD), lambda i: (0, jnp.maximum(i - na, 0), 0)),
                  pl.BlockSpec((4, ROWS_RP, SHARD), lambda i: (0, i, 0)), hbm, hbm, hbm],
        out_specs=(hbm, hbm, hbm, hbm),
        scratch_shapes=[pltpu.VMEM((2, 3, SHARD, ROWS_RP), f32), pltpu.VMEM((2, 4, SHARD, ROWS_RP), f32),
                        pltpu.SemaphoreType.DMA((2, 3)), pltpu.SemaphoreType.DMA((2, 4))],
        compiler_params=_cp(("arbitrary",)),
    )(parts_a, parts_b, parts_late, *(jnp.transpose(t, (2, 0, 1)) for t in (w, m, v)))
    return tuple(jnp.transpose(t, (1, 2, 0)) for t in res)


def _adam_params(parts, ws, ms, vs, name):
    p = parts.shape[0]
    k = len(ws)
    nrows = [w.size // 128 for w in ws]
    starts = [sum(nrows[:i]) for i in range(k)]

    def shaped(g, shape):
        if len(shape) == 2:
            return jnp.concatenate([g[r:r + 1] for r in range(g.shape[0])], axis=1)
        return g.reshape(shape)

    def body(*refs):
        g_ref = refs[0]
        w_refs, m_refs, v_refs = refs[1:1 + k], refs[1 + k:1 + 2 * k], refs[1 + 2 * k:1 + 3 * k]
        outs = refs[1 + 3 * k:]
        for i in range(k):
            rows = slice(starts[i], starts[i] + nrows[i])
            g = g_ref[0, rows, :]
            for j in range(1, p):
                g = g + g_ref[j, rows, :]
            _adam_update(shaped(g, ws[i].shape), w_refs[i], m_refs[i], v_refs[i], outs[i], outs[k + i], outs[2 * k + i], outs[3 * k + i])

    vm = pl.BlockSpec(memory_space=pltpu.VMEM)
    res = pl.pallas_call(
        body, name=name, out_shape=tuple(jax.ShapeDtypeStruct(w.shape, f32) for _ in range(4) for w in ws),
        in_specs=[vm] * (1 + 3 * k), out_specs=(vm,) * (4 * k), compiler_params=_cp(),
    )(parts, *ws, *ms, *vs)
    return [res[i * k:(i + 1) * k] for i in range(4)]


def _adam_blocks(parts, ws, ms, vs, name):
    k = len(ws)

    def body(*refs):
        g_refs, w_refs, m_refs, v_refs = refs[0:k], refs[k:2 * k], refs[2 * k:3 * k], refs[3 * k:4 * k]
        outs = refs[4 * k:]
        for i in range(k):
            g = g_refs[i][0].astype(f32)
            for j in range(1, parts[i].shape[0]):
                g = g + g_refs[i][j].astype(f32)
            _adam_update(g[None], w_refs[i], m_refs[i], v_refs[i], outs[i], outs[k + i], outs[2 * k + i], outs[3 * k + i])

    vm = pl.BlockSpec(memory_space=pltpu.VMEM)
    res = pl.pallas_call(
        body, name=name, out_shape=tuple(jax.ShapeDtypeStruct(w.shape, f32) for _ in range(4) for w in ws),
        in_specs=[vm] * (4 * k), out_specs=(vm,) * (4 * k), compiler_params=_cp(),
    )(*parts, *ws, *ms, *vs)
    return [res[i * k:(i + 1) * k] for i in range(4)]


def _adam_plain(gs, ws, ms, vs, name):
    k = len(ws)

    def body(*refs):
        g_refs, w_refs, m_refs, v_refs = refs[0:k], refs[k:2 * k], refs[2 * k:3 * k], refs[3 * k:4 * k]
        outs = refs[4 * k:]
        for i in range(k):
            _adam_update(g_refs[i][...], w_refs[i], m_refs[i], v_refs[i], outs[i], outs[k + i], outs[2 * k + i], outs[3 * k + i])

    vm = pl.BlockSpec(memory_space=pltpu.VMEM)
    res = pl.pallas_call(
        body, name=name, out_shape=tuple(jax.ShapeDtypeStruct(w.shape, f32) for _ in range(4) for w in ws),
        in_specs=[vm] * (4 * k), out_specs=(vm,) * (4 * k), compiler_params=_cp(),
    )(*gs, *ws, *ms, *vs)
    return [res[i * k:(i + 1) * k] for i in range(4)]


def _adam_gate(parts, w2, gb, m_w2, m_gb, v_w2, v_gb):
    per = 128 // 32

    def body(p_ref, w2_ref, gb_ref, mw_ref, mb_ref, vw_ref, vb_ref, *outs):
        g = p_ref[0]
        for j in range(1, parts.shape[0]):
            g = g + p_ref[j]
        for d in range(2):
            gd = jnp.concatenate([g[(d * RANK + r) // per:(d * RANK + r) // per + 1, (r % per) * 32:(r % per + 1) * 32]
                                  for r in range(RANK)], axis=0)
            _adam_update(gd, *(t.at[0, d] for t in (w2_ref, mw_ref, vw_ref) + outs[0:4]))
        last = 2 * RANK // per
        gbv = jnp.concatenate([g[last:last + 1, d * 32:(d + 1) * 32] for d in range(2)], axis=0)
        _adam_update(gbv, *(t.at[0] for t in (gb_ref, mb_ref, vb_ref) + outs[4:8]))

    vm = pl.BlockSpec(memory_space=pltpu.VMEM)
    return pl.pallas_call(
        body, name="adam_gate", out_shape=tuple(jax.ShapeDtypeStruct(w.shape, f32) for w in (w2, gb) for _ in range(4)),
        in_specs=[vm] * 7, out_specs=(vm,) * 8, compiler_params=_cp(),
    )(parts, w2, gb, m_w2, m_gb, v_w2, v_gb)


def _in_proj(x, g, scale, shift, w_pad, tl, comm):
    l = x.shape[0]
    c_in, c_out, c_sems = comm.specs()

    n = l // tl

    def body(*refs):
        x0_ref, xn_ref, g_ref, sc_ref, sh_ref, w_ref = refs[:6]
        cin = refs[6:6 + len(c_in)]
        p_ref, h_ref = refs[6 + len(c_in):8 + len(c_in)]
        cout = refs[8 + len(c_in):8 + len(c_in) + comm.n]
        hs_ref = refs[8 + len(c_in) + comm.n]
        sems = refs[9 + len(c_in) + comm.n:]

        def modulated(x_ref):
            xv = x_ref[...]
            r = lax.rsqrt(jnp.mean(xv * xv, axis=-1, keepdims=True) + EPS)
            return ((xv * r) * (g_ref[...] * (1.0 + sc_ref[...])) + sh_ref[...]).astype(bf16)

        def compute():
            @pl.when(pl.program_id(0) == 0)
            def _():
                hs_ref[...] = modulated(x0_ref)

            h_ref[...] = hs_ref[...]
            p_ref[...] = jnp.dot(h_ref[...], w_ref[...], preferred_element_type=f32).astype(bf16)
            hs_ref[...] = modulated(xn_ref)

        comm.run(cin, cout, sems, n, compute)

    vec = pl.BlockSpec((1, D), lambda i: (0, 0))
    return pl.pallas_call(
        body, name="in_proj", grid=(n,),
        out_shape=(jax.ShapeDtypeStruct((l, PW), bf16), jax.ShapeDtypeStruct((l, D), bf16)) + tuple(comm.outs),
        in_specs=[pl.BlockSpec((tl, D), lambda i: (0, 0)), pl.BlockSpec((tl, D), lambda i: (jnp.minimum(i + 1, n - 1), 0)),
                  vec, vec, vec, pl.BlockSpec((D, PW), lambda i: (0, 0))] + c_in,
        out_specs=(pl.BlockSpec((tl, PW), lambda i: (i, 0)), pl.BlockSpec((tl, D), lambda i: (i, 0))) + tuple(c_out),
        scratch_shapes=[pltpu.VMEM((tl, D), bf16)] + c_sems, compiler_params=_cp(("arbitrary",)),
    )(x, x, g, scale, shift, w_pad, *comm.ins)


def _tri(rev):
    i = lax.broadcasted_iota(jnp.int32, (CH, CH), 0)
    j = lax.broadcasted_iota(jnp.int32, (CH, CH), 1)
    return jnp.where((j >= i) if rev else (j <= i), 1.0, 0.0).astype(f32)


def _head_masks():
    lane = lax.broadcasted_iota(jnp.int32, (1, KW), 1) // HK
    return [jnp.where(lane == h, 1.0, 0.0).astype(f32) for h in range(NH)]


def _block_diag():
    r = lax.broadcasted_iota(jnp.int32, (VW, KW), 0) // HV
    c = lax.broadcasted_iota(jnp.int32, (VW, KW), 1) // HK
    return jnp.where(r == c, 1.0, 0.0).astype(f32)


def _decay(lr, w2, gb, tri, rev):
    logits = _dot(lr, w2) + gb
    a = _log_sigmoid(logits) * (1.0 / TAU)
    c = _dot_hi(tri, a)
    cl = c[0:1, :] if rev else c[CH - 1:CH, :]
    return logits, c, cl


def _state_fwd(k, v, c, cl, st, bd):
    return st * jnp.exp(cl) + bd * _tn(v, k * jnp.exp(cl - c))


def _state_bwd(k, v, c, cl, st0, dst, trit):
    ecl = jnp.exp(cl)
    edec = jnp.exp(cl - c)
    kdec = k * edec
    dv = _nt(kdec, dst)
    dkdec = _dot(v, dst)
    dcl = jnp.sum(dst * st0, axis=0, keepdims=True) * ecl + jnp.sum(dkdec * kdec, axis=0, keepdims=True)
    da = _dot_hi(trit, -dkdec * kdec) + dcl
    return dkdec * edec, dv, da, dst * ecl


def _bdot(a, b):
    return lax.dot_general(a.astype(bf16), b.astype(bf16), (((2,), (1,)), ((0,), (0,))), preferred_element_type=f32)


def _bnt(a, b):
    return lax.dot_general(a.astype(bf16), b.astype(bf16), (((2,), (2,)), ((0,), (0,))), preferred_element_type=f32)


def _btn(a, b):
    return lax.dot_general(a.astype(bf16), b.astype(bf16), (((1,), (1,)), ((0,), (0,))), preferred_element_type=f32)


def _scan_chunks(x, rev):
    nc = x.shape[0]
    hi = x.astype(bf16)
    r1 = x - hi.astype(f32)
    mid = r1.astype(bf16)
    lo = (r1 - mid.astype(f32)).astype(bf16)
    terms = jnp.concatenate([hi, mid, lo], axis=1)
    tri3 = jnp.broadcast_to(jnp.concatenate([_tri(rev)] * 3, axis=1).astype(bf16)[None], (nc, CH, 3 * CH))
    return lax.dot_general(tri3, terms, (((2,), (1,)), ((0,), (0,))), preferred_element_type=f32)


class _Tile:
    pass


def _tile_prep(q_ref, k_ref, lr_ref, w2_ref, gb_ref, rev, nc):
    t = _Tile()
    tg = nc * CH
    t.logits = _dot(lr_ref[...], w2_ref[...]) + gb_ref[...]
    c = _scan_chunks((_log_sigmoid(t.logits) * (1.0 / TAU)).reshape(nc, CH, KW), rev)
    cl = c[:, 0:1, :] if rev else c[:, CH - 1:CH, :]
    k = k_ref[...].astype(f32).reshape(nc, CH, KW)
    t.ec, t.enc, t.edec, t.ecl = jnp.exp(c), jnp.exp(-c), jnp.exp(cl - c), jnp.exp(cl)
    t.qd = q_ref[...].astype(f32).reshape(nc, CH, KW) * t.ec * QSCALE
    t.kd = k * t.enc
    t.kdec = k * t.edec
    hm = _head_masks()
    t.tri4 = jnp.concatenate([_tri(rev)] * NH, axis=0)[None]
    t.qs = jnp.concatenate([t.qd * hm[h] for h in range(NH)], axis=1)
    t.pst = _bnt(t.qs, t.kd) * t.tri4
    return t


def _gla_fwd(p, w2p, gb, s0, rev, tg, name):
    l = p.shape[0]
    nb, nc = l // tg, tg // CH

    def body(q_ref, k_ref, v_ref, lr_ref, w2_ref, gb_ref, s0_ref, o_ref, st_ref, st):
        @pl.when(pl.program_id(0) == 0)
        def _():
            st[...] = s0_ref[...]

        t = _tile_prep(q_ref, k_ref, lr_ref, w2_ref, gb_ref, rev, nc)
        v = v_ref[...].reshape(nc, CH, VW)
        intra = jnp.concatenate([_bdot(t.pst[:, h * CH:(h + 1) * CH], v[:, :, h * HV:(h + 1) * HV]) for h in range(NH)], axis=2)
        bd = _block_diag()
        s = st[...]
        for n in (range(nc - 1, -1, -1) if rev else range(nc)):
            st_ref[n] = s.astype(bf16)
            s = s * t.ecl[n] + bd * _tn(v[n], t.kdec[n])
        st[...] = s
        o_ref[...] = (_bnt(t.qd, st_ref[...]) + intra).reshape(tg, VW).astype(bf16)

    blk = (lambda i: nb - 1 - i) if rev else (lambda i: i)
    return pl.pallas_call(
        body, name=name, grid=(nb,),
        out_shape=(jax.ShapeDtypeStruct((l, VW), bf16), jax.ShapeDtypeStruct((l // CH, VW, KW), bf16)),
        in_specs=[pl.BlockSpec((tg, KW), lambda i: (blk(i), PQ // KW)), pl.BlockSpec((tg, KW), lambda i: (blk(i), PK // KW)),
                  pl.BlockSpec((tg, VW), lambda i: (blk(i), PV // VW)), pl.BlockSpec((tg, LRW), lambda i: (blk(i), PLR // LRW)),
                  pl.BlockSpec((LRW, KW), lambda i: (0, 0)), pl.BlockSpec((1, KW), lambda i: (0, 0)),
                  pl.BlockSpec((VW, KW), lambda i: (0, 0))],
        out_specs=(pl.BlockSpec((tg, VW), lambda i: (blk(i), 0)), pl.BlockSpec((nc, VW, KW), lambda i: (blk(i), 0, 0))),
        scratch_shapes=[pltpu.VMEM((VW, KW), f32)],
        compiler_params=_cp(("arbitrary",)),
    )(p, p, p, p, w2p, gb, s0)


def _gla_bwd(p, do, states, w2p, gb, prev, rev, tg, name, comm):
    l = p.shape[0]
    nb, nc = l // tg, tg // CH
    out_dt = bf16
    c_in, c_out, c_sems = comm.specs()

    def body(*refs):
        q_ref, k_ref, v_ref, lr_ref, do_ref, st_ref, w2_ref, gb_ref = refs[:8]
        refs = refs[8:]
        if prev is not None:
            pq_ref, pl_ref = refs[:2]
            refs = refs[2:]
        cin, refs = refs[:len(c_in)], refs[len(c_in):]
        dqkv_ref, dlr_ref, dw2_ref, dgb_ref, ds0_ref = refs[:5]
        cout, dst, ds_buf, sems = refs[5:5 + comm.n], refs[5 + comm.n], refs[6 + comm.n], refs[7 + comm.n:]
        comm.run(cin, cout, sems, nb, lambda: compute(q_ref, k_ref, v_ref, lr_ref, do_ref, st_ref, w2_ref, gb_ref,
                                                      pq_ref if prev is not None else None, pl_ref if prev is not None else None,
                                                      dqkv_ref, dlr_ref, dw2_ref, dgb_ref, ds0_ref, dst, ds_buf))

    def compute(q_ref, k_ref, v_ref, lr_ref, do_ref, st_ref, w2_ref, gb_ref, pq_ref, pl_ref,
                dqkv_ref, dlr_ref, dw2_ref, dgb_ref, ds0_ref, dst, ds_buf):
        @pl.when(pl.program_id(0) == 0)
        def _():
            dst[...] = jnp.zeros_like(dst)
            dw2_ref[...] = jnp.zeros_like(dw2_ref)
            dgb_ref[...] = jnp.zeros_like(dgb_ref)

        t = _tile_prep(q_ref, k_ref, lr_ref, w2_ref, gb_ref, rev, nc)
        hm = _head_masks()
        v = v_ref[...].reshape(nc, CH, VW)
        do = do_ref[...].reshape(nc, CH, VW)
        heads = lambda a, h: a[:, :, h * HV:(h + 1) * HV]
        dpst = jnp.concatenate([_bnt(heads(do, h), heads(v, h)) for h in range(NH)], axis=1) * t.tri4
        dv = jnp.concatenate([_btn(t.pst[:, h * CH:(h + 1) * CH], heads(do, h)) for h in range(NH)], axis=2)
        dqd = _bdot(do, st_ref[...])
        for h in range(NH):
            dqd = dqd + hm[h] * _bdot(dpst[:, h * CH:(h + 1) * CH], t.kd)
        dkd = _btn(dpst, t.qs)
        bd = _block_diag()
        d = dst[...]
        for n in (range(nc) if rev else range(nc - 1, -1, -1)):
            ds_buf[n] = d
            d = d * t.ecl[n] + bd * _tn(do[n], t.qd[n])
        dst[...] = d
        ds0_ref[...] = d
        ds = ds_buf[...]
        dv = dv + _bnt(t.kdec, ds)
        dkdec = _bdot(v, ds)
        dcl = jnp.sum(ds * st_ref[...].astype(f32), axis=1, keepdims=True) * t.ecl + jnp.sum(dkdec * t.kdec, axis=1, keepdims=True)
        dc = dqd * t.qd - dkd * t.kd - dkdec * t.kdec
        da = _scan_chunks(dc, not rev) + dcl
        dq = dqd * t.ec * QSCALE
        dk = dkd * t.enc + dkdec * t.edec
        dlog = da.reshape(tg, KW) * _sigmoid(-t.logits) * (1.0 / TAU)
        dlr = _nt(dlog, w2_ref[...])
        dw2_ref[...] += _tn(lr_ref[...], dlog)
        dgb_ref[...] += jnp.sum(dlog, axis=0, keepdims=True)
        dqkv = jnp.concatenate([dq, dk, dv], axis=2).reshape(tg, 2 * KW + VW)
        if prev is not None:
            dqkv = dqkv + pq_ref[...]
            dlr = dlr + pl_ref[...]
        dqkv_ref[...] = dqkv.astype(out_dt)
        dlr_ref[...] = dlr.astype(out_dt)

    blk = (lambda i: i) if rev else (lambda i: nb - 1 - i)
    in_specs = [pl.BlockSpec((tg, KW), lambda i: (blk(i), PQ // KW)), pl.BlockSpec((tg, KW), lambda i: (blk(i), PK // KW)),
                pl.BlockSpec((tg, VW), lambda i: (blk(i), PV // VW)), pl.BlockSpec((tg, LRW), lambda i: (blk(i), PLR // LRW)),
                pl.BlockSpec((tg, VW), lambda i: (blk(i), 0)), pl.BlockSpec((nc, VW, KW), lambda i: (blk(i), 0, 0)),
                pl.BlockSpec((LRW, KW), lambda i: (0, 0)), pl.BlockSpec((1, KW), lambda i: (0, 0))]
    args = [p, p, p, p, do, states, w2p, gb]
    if prev is not None:
        in_specs += [pl.BlockSpec((tg, 2 * KW + VW), lambda i: (blk(i), 0)), pl.BlockSpec((tg, LRW), lambda i: (blk(i), 0))]
        args += list(prev)
    return pl.pallas_call(
        body, name=name, grid=(nb,),
        out_shape=(jax.ShapeDtypeStruct((l, 2 * KW + VW), out_dt), jax.ShapeDtypeStruct((l, LRW), out_dt),
                   jax.ShapeDtypeStruct((LRW, KW), f32), jax.ShapeDtypeStruct((1, KW), f32), jax.ShapeDtypeStruct((VW, KW), f32))
        + tuple(comm.outs),
        in_specs=in_specs + c_in,
        out_specs=(pl.BlockSpec((tg, 2 * KW + VW), lambda i: (blk(i), 0)), pl.BlockSpec((tg, LRW), lambda i: (blk(i), 0)),
                   pl.BlockSpec((LRW, KW), lambda i: (0, 0)), pl.BlockSpec((1, KW), lambda i: (0, 0)),
                   pl.BlockSpec((VW, KW), lambda i: (0, 0))) + tuple(c_out),
        scratch_shapes=[pltpu.VMEM((VW, KW), f32), pltpu.VMEM((nc, VW, KW), f32)] + c_sems,
        compiler_params=_cp(("arbitrary",)),
    )(*args, *comm.ins)


def _ctx_hidden(ctx_ref, g_ref, sc_ref, sh_ref):
    xv = ctx_ref[...]
    r = lax.rsqrt(jnp.mean(xv * xv, axis=-1, keepdims=True) + EPS)
    xn = xv * r
    return xn, xn * (g_ref[...] * (1.0 + sc_ref[...])) + sh_ref[...]


_CTX_W_SPECS = [pl.BlockSpec((D, KW), lambda i: (0, PK // KW)), pl.BlockSpec((D, VW), lambda i: (0, PV // VW)),
                pl.BlockSpec((D, LRW), lambda i: (0, PLR // LRW))]


def _ctx_fwd(ctx, g, scale, shift, w_pad, w2f, w2b, gbf, gbb):
    ncc = CTX // CH

    def body(ctx_ref, g_ref, sc_ref, sh_ref, wk_ref, wv_ref, wl_ref, w2f_ref, w2b_ref, gbf_ref, gbb_ref, sf_ref, sb_ref):
        _, hc = _ctx_hidden(ctx_ref, g_ref, sc_ref, sh_ref)
        k, v, lr = _dot(hc, wk_ref[...]), _dot(hc, wv_ref[...]), _dot(hc, wl_ref[...])
        bd = _block_diag()
        for rev, w2_ref, gb_ref, out in ((False, w2f_ref, gbf_ref, sf_ref), (True, w2b_ref, gbb_ref, sb_ref)):
            tri = _tri(rev)
            st = jnp.zeros((VW, KW), f32)
            for j in (range(ncc - 1, -1, -1) if rev else range(ncc)):
                rows = slice(j * CH, (j + 1) * CH)
                _, c, cl = _decay(lr[rows], w2_ref[...], gb_ref[...], tri, rev)
                st = _state_fwd(k[rows], v[rows], c, cl, st, bd)
            out[...] = st

    vec = pl.BlockSpec((1, D), lambda i: (0, 0))
    w2s = pl.BlockSpec((LRW, KW), lambda i: (0, 0))
    gbs = pl.BlockSpec((1, KW), lambda i: (0, 0))
    sts = pl.BlockSpec((VW, KW), lambda i: (0, 0))
    return pl.pallas_call(
        body, name="ctx_fwd", grid=(1,), out_shape=(jax.ShapeDtypeStruct((VW, KW), f32),) * 2,
        in_specs=[pl.BlockSpec((CTX, D), lambda i: (0, 0)), vec, vec, vec] + _CTX_W_SPECS + [w2s, w2s, gbs, gbs],
        out_specs=(sts, sts), compiler_params=_cp(("arbitrary",)),
    )(ctx, g, scale, shift, w_pad, w_pad, w_pad, w2f, w2b, gbf, gbb)


def _ctx_bwd(ctx, g, scale, shift, w_pad, w2f, w2b, gbf, gbb, dsf, dsb, dw2_scans, dgb_scans):
    ncc = CTX // CH
    per = 128 // 32

    def body(ctx_ref, g_ref, sc_ref, sh_ref, wk_ref, wv_ref, wl_ref, w2f_ref, w2b_ref, gbf_ref, gbb_ref, dsf_ref, dsb_ref,
             sw2f_ref, sw2b_ref, sgbf_ref, sgbb_ref, dwk_ref, dwv_ref, dwl_ref, dmod_ref, dg_ref, dw2_ref, dgb_ref, dgt_ref):
        dgt_ref[...] = jnp.zeros_like(dgt_ref)
        xn, hc = _ctx_hidden(ctx_ref, g_ref, sc_ref, sh_ref)
        k, v, lr = _dot(hc, wk_ref[...]), _dot(hc, wv_ref[...]), _dot(hc, wl_ref[...])
        bd = _block_diag()
        dk_rows, dv_rows, dl_rows = [None] * ncc, [None] * ncc, [None] * ncc
        for d, (rev, w2_ref, gb_ref, ds_ref) in enumerate(((False, w2f_ref, gbf_ref, dsf_ref), (True, w2b_ref, gbb_ref, dsb_ref))):
            tri = _tri(rev)
            order = list(range(ncc - 1, -1, -1) if rev else range(ncc))
            st, saved = jnp.zeros((VW, KW), f32), {}
            for j in order:
                rows = slice(j * CH, (j + 1) * CH)
                logits, c, cl = _decay(lr[rows], w2_ref[...], gb_ref[...], tri, rev)
                saved[j] = (logits, c, cl, st)
                st = _state_fwd(k[rows], v[rows], c, cl, st, bd)
            dst = ds_ref[...]
            dw2 = jnp.zeros((LRW, KW), f32)
            dgb = jnp.zeros((1, KW), f32)
            for j in reversed(order):
                rows = slice(j * CH, (j + 1) * CH)
                logits, c, cl, st0 = saved[j]
                dk, dv, da, dst = _state_bwd(k[rows], v[rows], c, cl, st0, dst, _tri(not rev))
                dlog = da * _sigmoid(-logits) * (1.0 / TAU)
                dl = _nt(dlog, w2_ref[...])
                dw2 = dw2 + _tn(lr[rows], dlog)
                dgb = dgb + jnp.sum(dlog, axis=0, keepdims=True)
                dk_rows[j] = dk if dk_rows[j] is None else dk_rows[j] + dk
                dv_rows[j] = dv if dv_rows[j] is None else dv_rows[j] + dv
                dl_rows[j] = dl if dl_rows[j] is None else dl_rows[j] + dl
            dw2_ref[d] = dw2
            dgb_ref[d] = dgb
            dw2 = dw2 + (sw2f_ref, sw2b_ref)[d][...]
            dgb = dgb + (sgbf_ref, sgbb_ref)[d][...]
            for dev in range(NDEV):
                cols = slice(dev * 32, (dev + 1) * 32)
                for r in range(RANK):
                    q = d * RANK + r
                    dgt_ref[dev, q // per:q // per + 1, (q % per) * 32:(q % per + 1) * 32] = dw2[q:q + 1, cols]
                dgt_ref[dev, 2 * RANK // per:2 * RANK // per + 1, d * 32:(d + 1) * 32] = dgb[:, cols]
        dk, dv, dl = (jnp.concatenate(t, axis=0) for t in (dk_rows, dv_rows, dl_rows))
        dwk_ref[...] = _tn(hc, dk)
        dwv_ref[...] = _tn(hc, dv)
        dwl_ref[...] = _tn(hc, dl)
        dh = _nt(dk, wk_ref[...]) + _nt(dv, wv_ref[...]) + _nt(dl, wl_ref[...])
        gx = dh * xn
        dmod_ref[:, 0:D] = jnp.sum(dh, axis=0, keepdims=True)
        dmod_ref[:, D:2 * D] = jnp.sum(gx, axis=0, keepdims=True) * g_ref[...]
        dg_ref[...] = jnp.sum(gx, axis=0, keepdims=True) * (1.0 + sc_ref[...])

    vec = pl.BlockSpec((1, D), lambda i: (0, 0))
    w2s = pl.BlockSpec((LRW, KW), lambda i: (0, 0))
    gbs = pl.BlockSpec((1, KW), lambda i: (0, 0))
    sts = pl.BlockSpec((VW, KW), lambda i: (0, 0))
    full = lambda *s: pl.BlockSpec(s, lambda i: (0,) * len(s))
    return pl.pallas_call(
        body, name="ctx_bwd", grid=(1,),
        out_shape=(jax.ShapeDtypeStruct((D, KW), f32), jax.ShapeDtypeStruct((D, VW), f32), jax.ShapeDtypeStruct((D, LRW), f32),
                   jax.ShapeDtypeStruct((1, 2 * D), f32), jax.ShapeDtypeStruct((1, D), f32),
                   jax.ShapeDtypeStruct((2, LRW, KW), f32), jax.ShapeDtypeStruct((2, 1, KW), f32),
                   jax.ShapeDtypeStruct((NDEV, 2 * RANK // per + 1, 128), f32)),
        in_specs=[pl.BlockSpec((CTX, D), lambda i: (0, 0)), vec, vec, vec] + _CTX_W_SPECS + [w2s, w2s, gbs, gbs, sts, sts, w2s, w2s, gbs, gbs],
        out_specs=(full(D, KW), full(D, VW), full(D, LRW), full(1, 2 * D), full(1, D), full(2, LRW, KW), full(2, 1, KW),
                   full(NDEV, 2 * RANK // per + 1, 128)),
        compiler_params=_cp(("arbitrary",)),
    )(ctx, g, scale, shift, w_pad, w_pad, w_pad, w2f, w2b, gbf, gbb, dsf, dsb, *dw2_scans, *dgb_scans)


def _layernorm(va, g, b):
    mu = jnp.mean(va, axis=-1, keepdims=True)
    xc = va - mu
    rstd = lax.rsqrt(jnp.mean(xc * xc, axis=-1, keepdims=True) + EPS)
    vhat = xc * rstd
    return vhat, rstd, vhat * g + b


MIX_PIECES = 8


def _mix_fwd(p, ln_g, ln_b, ws, bs_t):
    l = p.shape[0]
    half = AW // 2
    rp = l // MIX_PIECES
    cpp = rp // ACH

    def body(p_ref, g_ref, b_ref, ws_ref, bs_ref, sv_hbm, va_buf, col_buf, sv_buf, in_sems, out_sems):
        piece = lambda i: pl.ds(pl.multiple_of(i * rp, rp), rp)
        load = lambda i: pltpu.make_async_copy(p_ref.at[piece(i), pl.ds(PVA, AW)], va_buf.at[piece(i)], in_sems.at[i])
        store = lambda i: pltpu.make_async_copy(sv_buf.at[piece(i)], sv_hbm.at[piece(i)], out_sems.at[i])
        for i in range(MIX_PIECES):
            load(i).start()

        def rows_piece(i, carry):
            load(i).wait()
            for j in range(cpp):
                rows = pl.ds(pl.multiple_of(i * rp + j * ACH, ACH), ACH)
                _, _, vn = _layernorm(va_buf[rows, :].astype(f32), g_ref[...], b_ref[...])
                for gi in range(2):
                    sl = slice(gi * ACH, (gi + 1) * ACH)
                    sv_buf[rows, sl] = (_dot(ws_ref[gi], vn[:, sl]) + bs_ref[:, gi:gi + 1]).astype(bf16)
                col_buf[0, rows, :] = vn[:, half:half + ACH]
                col_buf[1, rows, :] = vn[:, half + ACH:]
            return carry

        lax.fori_loop(0, MIX_PIECES, rows_piece, 0)

        def cols_step(cidx, carry):
            rows = pl.ds(cidx, ACH, stride=GW)
            for gi in range(2, 4):
                col_buf[gi - 2, rows, :] = _dot(ws_ref[gi], col_buf[gi - 2, rows, :]) + bs_ref[:, gi:gi + 1]
            return carry

        lax.fori_loop(0, GW, cols_step, 0, unroll=8)

        def out_piece(i, carry):
            for j in range(cpp):
                rows = pl.ds(pl.multiple_of(i * rp + j * ACH, ACH), ACH)
                sv_buf[rows, half:half + ACH] = col_buf[0, rows, :].astype(bf16)
                sv_buf[rows, half + ACH:] = col_buf[1, rows, :].astype(bf16)
            store(i).start()
            return carry

        lax.fori_loop(0, MIX_PIECES, out_piece, 0)
        for i in range(MIX_PIECES):
            store(i).wait()

    vm = pl.BlockSpec(memory_space=pltpu.VMEM)
    hbm = pl.BlockSpec(memory_space=pl.ANY)
    return pl.pallas_call(
        body, name="mix_fwd", out_shape=jax.ShapeDtypeStruct((l, AW), bf16),
        in_specs=[hbm, vm, vm, vm, vm], out_specs=hbm,
        scratch_shapes=[pltpu.VMEM((l, AW), bf16), pltpu.VMEM((2, l, ACH), f32), pltpu.VMEM((l, AW), bf16),
                        pltpu.SemaphoreType.DMA((MIX_PIECES,)), pltpu.SemaphoreType.DMA((MIX_PIECES,))],
        compiler_params=_cp(),
    )(p, ln_g, ln_b, ws, bs_t)


def _mix_bwd(p, dsv, ln_g, ln_b, ws_t):
    l = p.shape[0]
    half = AW // 2
    rp = l // MIX_PIECES
    cpp = rp // ACH

    def body(p_ref, dsv_hbm, g_ref, b_ref, wst_ref, dva_hbm, dws_ref, dbs_ref, dg_ref, db_ref,
             va_buf, dsv_buf, vn_col, ds_col, dva_buf, va_sems, ds_sems, out_sems):
        piece = lambda i: pl.ds(pl.multiple_of(i * rp, rp), rp)
        load_va = lambda i: pltpu.make_async_copy(p_ref.at[piece(i), pl.ds(PVA, AW)], va_buf.at[piece(i)], va_sems.at[i])
        load_ds = lambda i: pltpu.make_async_copy(dsv_hbm.at[piece(i)], dsv_buf.at[piece(i)], ds_sems.at[i])
        store = lambda i: pltpu.make_async_copy(dva_buf.at[piece(i)], dva_hbm.at[piece(i)], out_sems.at[i])
        for i in range(MIX_PIECES):
            load_va(i).start()
            load_ds(i).start()
        dws_ref[...] = jnp.zeros_like(dws_ref)
        dbs_ref[...] = jnp.zeros_like(dbs_ref)
        dg_ref[...] = jnp.zeros_like(dg_ref)
        db_ref[...] = jnp.zeros_like(db_ref)

        def rows_piece(i, carry):
            load_va(i).wait()
            load_ds(i).wait()
            for j in range(cpp):
                rows = pl.ds(pl.multiple_of(i * rp + j * ACH, ACH), ACH)
                _, _, vn = _layernorm(va_buf[rows, :].astype(f32), g_ref[...], b_ref[...])
                ds = dsv_buf[rows, :].astype(f32)
                for gi in range(2):
                    sl = slice(gi * ACH, (gi + 1) * ACH)
                    dws_ref[gi] += _nt(ds[:, sl], vn[:, sl])
                    dbs_ref[gi] += ds[:, sl]
                for gi in range(2):
                    sl = slice(half + gi * ACH, half + (gi + 1) * ACH)
                    vn_col[gi, rows, :] = vn[:, sl]
                    ds_col[gi, rows, :] = ds[:, sl]
            return carry

        lax.fori_loop(0, MIX_PIECES, rows_piece, 0)

        def cols_step(cidx, carry):
            rows = pl.ds(cidx, ACH, stride=GW)
            for gi in range(2, 4):
                ds = ds_col[gi - 2, rows, :]
                dws_ref[gi] += _nt(ds, vn_col[gi - 2, rows, :])
                dbs_ref[gi] += ds
                ds_col[gi - 2, rows, :] = _dot(wst_ref[gi], ds)
            return carry

        lax.fori_loop(0, GW, cols_step, 0, unroll=8)

        def out_piece(i, carry):
            for j in range(cpp):
                rows = pl.ds(pl.multiple_of(i * rp + j * ACH, ACH), ACH)
                vhat, rstd, _ = _layernorm(va_buf[rows, :].astype(f32), g_ref[...], b_ref[...])
                ds = dsv_buf[rows, :].astype(f32)
                dvn = jnp.concatenate([_dot(wst_ref[0], ds[:, 0:ACH]), _dot(wst_ref[1], ds[:, ACH:half]),
                                       ds_col[0, rows, :], ds_col[1, rows, :]], axis=1)
                dg_ref[...] += jnp.sum(dvn * vhat, axis=0, keepdims=True)
                db_ref[...] += jnp.sum(dvn, axis=0, keepdims=True)
                dvh = dvn * g_ref[...]
                dva = rstd * (dvh - jnp.mean(dvh, axis=-1, keepdims=True) - vhat * jnp.mean(dvh * vhat, axis=-1, keepdims=True))
                dva_buf[rows, :] = dva.astype(bf16)
            store(i).start()
            return carry

        lax.fori_loop(0, MIX_PIECES, out_piece, 0)
        for i in range(MIX_PIECES):
            store(i).wait()

    vm = pl.BlockSpec(memory_space=pltpu.VMEM)
    hbm = pl.BlockSpec(memory_space=pl.ANY)
    dma = lambda: pltpu.SemaphoreType.DMA((MIX_PIECES,))
    return pl.pallas_call(
        body, name="mix_bwd",
        out_shape=(jax.ShapeDtypeStruct((l, AW), bf16), jax.ShapeDtypeStruct((4, ACH, ACH), f32), jax.ShapeDtypeStruct((4, ACH, ACH), f32),
                   jax.ShapeDtypeStruct((1, AW), f32), jax.ShapeDtypeStruct((1, AW), f32)),
        in_specs=[hbm, hbm, vm, vm, vm], out_specs=(hbm, vm, vm, vm, vm),
        scratch_shapes=[pltpu.VMEM((l, AW), bf16), pltpu.VMEM((l, AW), bf16), pltpu.VMEM((2, l, ACH), f32), pltpu.VMEM((2, l, ACH), f32),
                        pltpu.VMEM((l, AW), bf16), dma(), dma(), dma()],
        compiler_params=_cp(),
    )(p, dsv, ln_g, ln_b, ws_t)


def _mid(x, tgt, p, o_f, o_b, sv, gate, gf, gb_norm, w_pa, w_pb, w_out, tl):
    l = x.shape[0]

    def body(x_ref, t_ref, zb_ref, ua_ref, za_ref, g1_ref, g2_ref, of_ref, ob_ref, sv_ref, gate_ref, gf_ref, gbn_ref,
             wpa_ref, wpb_ref, wout_ref,
             dx1_ref, dzbua_ref, dzag_ref, dsv_ref, do_ref, dwout_bf, dwpa_bf, dwpb_bf, dgf_ref, dgate_ref, dgbn_ref, loss_ref,
             dwout_ref, dwpa_ref, dwpb_ref):
        @pl.when(pl.program_id(0) == 0)
        def _():
            for r in (dwout_ref, dwpa_ref, dwpb_ref, dgf_ref, dgate_ref, dgbn_ref, loss_ref):
                r[...] = jnp.zeros_like(r)

        o = of_ref[...].astype(f32) + ob_ref[...].astype(f32)
        rr = jnp.concatenate(
            [jnp.broadcast_to(lax.rsqrt(jnp.mean(o[:, h * HV:(h + 1) * HV] ** 2, axis=-1, keepdims=True) + EPS), (tl, HV))
             for h in range(NH)], axis=1)
        ohat = o * rr
        on = ohat * gbn_ref[...]
        szb, dszb = _silu_and_grad(zb_ref[...].astype(f32))
        tb = on * szb
        u = ua_ref[...].astype(f32)
        svv = sv_ref[...].astype(f32)
        sza, dsza = _silu_and_grad(za_ref[...].astype(f32))
        ta = u * svv * sza
        ya = _dot(ta, wpa_ref[...])
        yb = _dot(tb, wpb_ref[...])
        g1 = _sigmoid(g1_ref[...].astype(f32))
        g2 = _sigmoid(g2_ref[...].astype(f32))
        m = g1 * ya + g2 * yb
        y2 = _dot(m, wout_ref[...])
        x1 = x_ref[...] + gate_ref[...] * y2
        r1 = lax.rsqrt(jnp.mean(x1 * x1, axis=-1, keepdims=True) + EPS)
        x1n = x1 * r1
        err = x1n * gf_ref[...] - t_ref[...]
        loss_ref[...] += jnp.sum(jnp.sum(err * err, axis=-1, keepdims=True), axis=0, keepdims=True) * (0.5 / D)
        dout = err * (1.0 / D)
        dgf_ref[...] += jnp.sum(dout * x1n, axis=0, keepdims=True)
        dx1n = dout * gf_ref[...]
        dx1 = r1 * (dx1n - x1n * jnp.mean(dx1n * x1n, axis=-1, keepdims=True))
        dx1_ref[...] = dx1
        dgate_ref[...] += jnp.sum(dx1 * y2, axis=0, keepdims=True)
        dy2 = dx1 * gate_ref[...]
        dwout_ref[...] += _tn(m, dy2)
        dm = _nt(dy2, wout_ref[...])
        dya = dm * g1
        dyb = dm * g2
        dzag_ref[:, AW:AW + D] = (dm * ya * g1 * (1.0 - g1)).astype(bf16)
        dzag_ref[:, AW + D:] = (dm * yb * g2 * (1.0 - g2)).astype(bf16)
        dwpa_ref[...] += _tn(ta, dya)
        dta = _nt(dya, wpa_ref[...])
        dzbua_ref[:, AW:] = (dta * svv * sza).astype(bf16)
        dsv_ref[...] = (dta * u * sza).astype(bf16)
        dzag_ref[:, 0:AW] = (dta * u * svv * dsza).astype(bf16)
        dwpb_ref[...] += _tn(tb, dyb)
        dtb = _nt(dyb, wpb_ref[...])
        don = dtb * szb
        dzbua_ref[:, 0:AW] = (dtb * on * dszb).astype(bf16)
        dgbn_ref[...] += jnp.sum(don * ohat, axis=0, keepdims=True)
        doh = don * gbn_ref[...]
        prod = doh * ohat
        mh = jnp.concatenate(
            [jnp.broadcast_to(jnp.mean(prod[:, h * HV:(h + 1) * HV], axis=-1, keepdims=True), (tl, HV)) for h in range(NH)], axis=1)
        do_ref[...] = (rr * (doh - ohat * mh)).astype(bf16)

        @pl.when(pl.program_id(0) == l // tl - 1)
        def _():
            for acc, out in ((dwout_ref, dwout_bf), (dwpa_ref, dwpa_bf), (dwpb_ref, dwpb_bf)):
                out[...] = acc[...].astype(bf16)

    row = lambda w, j: pl.BlockSpec((tl, w), lambda i, j=j: (i, j))
    full = lambda *s: pl.BlockSpec(s, lambda i: (0,) * len(s))
    return pl.pallas_call(
        body, name="mid", grid=(l // tl,),
        out_shape=(jax.ShapeDtypeStruct((l, D), f32), jax.ShapeDtypeStruct((l, 2 * AW), bf16), jax.ShapeDtypeStruct((l, AW + 2 * D), bf16),
                   jax.ShapeDtypeStruct((l, AW), bf16), jax.ShapeDtypeStruct((l, VW), bf16),
                   jax.ShapeDtypeStruct((D, D), bf16), jax.ShapeDtypeStruct((AW, D), bf16), jax.ShapeDtypeStruct((VW, D), bf16),
                   jax.ShapeDtypeStruct((1, D), f32), jax.ShapeDtypeStruct((1, D), f32), jax.ShapeDtypeStruct((1, VW), f32),
                   jax.ShapeDtypeStruct((1, 1), f32)),
        scratch_shapes=[pltpu.VMEM((D, D), f32), pltpu.VMEM((AW, D), f32), pltpu.VMEM((VW, D), f32)],
        in_specs=[row(D, 0), row(D, 0), row(AW, PZB // AW), row(AW, PUA // AW), row(AW, PZA // AW), row(D, PG1 // D), row(D, PG2 // D),
                  row(VW, 0), row(VW, 0), row(AW, 0), full(1, D), full(1, D), full(1, VW), full(AW, D), full(VW, D), full(D, D)],
        out_specs=(row(D, 0), row(2 * AW, 0), row(AW + 2 * D, 0), row(AW, 0), row(VW, 0),
                   full(D, D), full(AW, D), full(VW, D), full(1, D), full(1, D), full(1, VW), full(1, 1)),
        compiler_params=_cp(("arbitrary",)),
    )(x, tgt, p, p, p, p, p, o_f, o_b, sv, gate, gf, gb_norm, w_pa, w_pb, w_out)


def _in_bwd(x, dx1, dqkv, dzbua, dva, dzag, dlr, w_pad, g, scale, tl, comm):
    l = x.shape[0]
    c_in, c_out, c_sems = comm.specs()

    def body(*refs):
        cin = refs[14:14 + len(c_in)]
        outs = refs[14 + len(c_in):]
        comm.run(cin, outs[4:4 + comm.n], outs[4 + comm.n:], l // tl, lambda: compute(*refs[:14], *outs[:4]))

    def compute(x_ref, dx1_ref, a_ref, b_ref, c_ref, e_ref, lr_ref, wa_ref, wb_ref, wc_ref, we_ref, wl_ref, g_ref, sc_ref,
                gx_ref, dsh_ref, dsc_ref, dg_ref):
        @pl.when(pl.program_id(0) == 0)
        def _():
            for r in (dsh_ref, dsc_ref, dg_ref):
                r[...] = jnp.zeros_like(r)

        dh = (_nt(a_ref[...], wa_ref[...]) + _nt(b_ref[...], wb_ref[...]) + _nt(c_ref[...], wc_ref[...]) + _nt(e_ref[...], we_ref[...])
              + _nt(lr_ref[...], wl_ref[...]))
        xv = x_ref[...]
        r = lax.rsqrt(jnp.mean(xv * xv, axis=-1, keepdims=True) + EPS)
        xn = xv * r
        gxn = jnp.sum(dh * xn, axis=0, keepdims=True)
        dsh_ref[...] += jnp.sum(dh, axis=0, keepdims=True)
        dsc_ref[...] += gxn * g_ref[...]
        dg_ref[...] += gxn * (1.0 + sc_ref[...])
        dxn = dh * (g_ref[...] * (1.0 + sc_ref[...]))
        gx_ref[...] = dx1_ref[...] + r * (dxn - xn * jnp.mean(dxn * xn, axis=-1, keepdims=True))

    row = lambda w: pl.BlockSpec((tl, w), lambda i: (i, 0))
    wcol = lambda w, j: pl.BlockSpec((D, w), lambda i, j=j: (0, j))
    vec = pl.BlockSpec((1, D), lambda i: (0, 0))
    return pl.pallas_call(
        body, name="in_bwd", grid=(l // tl,),
        out_shape=(jax.ShapeDtypeStruct((l, D), f32),) + (jax.ShapeDtypeStruct((1, D), f32),) * 3 + tuple(comm.outs),
        in_specs=[row(D), row(D), row(2 * KW + VW), row(2 * AW), row(AW), row(AW + 2 * D), row(LRW),
                  wcol(2 * KW + VW, 0), wcol(2 * AW, PZB // (2 * AW)), wcol(AW, PVA // AW), wcol(AW + 2 * D, PZA // (AW + 2 * D)),
                  wcol(LRW, PLR // LRW), vec, vec] + c_in,
        out_specs=(row(D), vec, vec, vec) + tuple(c_out), scratch_shapes=c_sems,
        compiler_params=_cp(("arbitrary",)),
    )(x, dx1, dqkv, dzbua, dva, dzag, dlr, w_pad, w_pad, w_pad, w_pad, w_pad, g, scale, *comm.ins)


def _tn_matmul(a, bs, tl, name, comm=None, pack=None):
    l, m = a.shape
    k = len(bs)
    comm = comm or _Comm([], [], None)
    c_in, c_out, c_sems = comm.specs()
    acc_shapes = [(m, b.shape[1]) for b in bs]
    n_extra = len(pack.extra) if pack else 0
    n_res = len(pack.outs) if pack else k

    def body(a_ref, *refs):
        b_refs, refs = refs[:k], refs[k:]
        extra, refs = refs[:n_extra], refs[n_extra:]
        cin, refs = refs[:len(c_in)], refs[len(c_in):]
        res, refs = refs[:n_res], refs[n_res:]
        cout, refs = refs[:comm.n], refs[comm.n:]
        accs, sems = (refs[:k], refs[k:]) if pack else (res, refs)

        def compute():
            @pl.when(pl.program_id(0) == 0)
            def _():
                for o_ref in accs:
                    o_ref[...] = jnp.zeros_like(o_ref)

            av = a_ref[...]
            for b_ref, o_ref in zip(b_refs, accs):
                o_ref[...] += _tn(av, b_ref[...])
            if pack:
                pl.when(pl.program_id(0) == l // tl - 1)(lambda: pack.fn(accs, extra, res))

        comm.run(cin, cout, sems, l // tl, compute)

    full = lambda shape: pl.BlockSpec(shape, lambda i: (0,) * len(shape))
    res_shapes = list(pack.outs) if pack else [jax.ShapeDtypeStruct(sh, f32) for sh in acc_shapes]
    return pl.pallas_call(
        body, name=name, grid=(l // tl,), out_shape=tuple(res_shapes) + tuple(comm.outs),
        in_specs=[pl.BlockSpec((tl, m), lambda i: (i, 0))] + [pl.BlockSpec((tl, b.shape[1]), lambda i: (i, 0)) for b in bs]
        + [full(e.shape) for e in (pack.extra if pack else [])] + c_in,
        out_specs=tuple(full(r.shape) for r in res_shapes) + tuple(c_out),
        scratch_shapes=([pltpu.VMEM(sh, f32) for sh in acc_shapes] if pack else []) + c_sems, compiler_params=_cp(("arbitrary",)),
    )(a, *bs, *(pack.extra if pack else []), *comm.ins)


EARLY_A_ROWS = 512


class _NoExchange:
    def __init__(self, w_pa, w_pb, w_out):
        self.weights = (w_pa, w_pb, w_out)

    def gather_proj(self):
        return _Comm([], [], None)

    def proj_weights(self, got):
        return self.weights

    def first(self, dw_out, dw_pa, dw_pb, small):
        return _Comm([], [], None)

    def early_a(self, blocks):
        return _Comm([], [], None)

    def early_b(self, blocks):
        return _Comm([], [], None)

    def late(self, blocks, dgt):
        return _Comm([], [], None)


class _Exchanges:
    def __init__(self, pa, pb, wo):
        self.shards = (pa, pb, wo)

    def gather_proj(self):
        def plan(i, o):
            srcs = [lambda j, r=r: r for r in i]
            dsts = [lambda j: o[0].at[:, _lanes(j)], lambda j: o[1].at[:, _lanes(j)], lambda j: o[2].at[j]]
            return srcs, dsts, None
        sds = jax.ShapeDtypeStruct
        return _Comm(self.shards, [sds((AW, D), bf16), sds((VW, D), bf16), sds((NDEV, 128, D), bf16)], plan)

    def proj_weights(self, got):
        return got[0], got[1], got[2].reshape(D, D)

    def first(self, dw_out, dw_pa, dw_pb, small):
        def plan(i, o):
            srcs = [lambda j: i[0].at[j], lambda j: i[1].at[:, _lanes(j)], lambda j: i[2].at[:, _lanes(j)], lambda j: i[3]]
            dsts = [lambda j, r=r: r.at[j] for r in o]
            return srcs, dsts, None
        sds = jax.ShapeDtypeStruct
        return _Comm([dw_out.reshape(NDEV, 128, D), dw_pa, dw_pb, small],
                     [sds((NDEV, 128, D), bf16), sds((NDEV, AW, 128), bf16), sds((NDEV, VW, 128), bf16),
                      sds((NDEV,) + small.shape, f32)], plan)

    def early_a(self, blocks):
        def plan(i, o):
            return [lambda j: i[0].at[j]], [lambda j: o[0].at[j]], [lambda j: j >= LATE_DESTS - 1]
        return _Comm([blocks], [jax.ShapeDtypeStruct(blocks.shape, bf16)], plan)

    def early_b(self, blocks):
        def plan(i, o):
            return [lambda j: i[0].at[j]], [lambda j: o[0].at[j]], [lambda j: j >= LATE_DESTS - 1]
        return _Comm([blocks], [jax.ShapeDtypeStruct(blocks.shape, bf16)], plan)

    def late(self, blocks, dgt):
        return _LateComm(blocks, dgt)


def _local_step(x, ctx, tgt, mod, modc, norm_g, w_pad, ln_g, ln_b, ws, bs, gate_w, gb_norm, gf, xch):
    shift, scale, gate = mod[:, 0:D], mod[:, D:2 * D], mod[:, 2 * D:]
    shift_c, scale_c = modc[:, 0:D], modc[:, D:]
    w2f, w2b, gbf, gbb = gate_w

    p, h, *got_proj = _in_proj(x, norm_g, scale, shift, w_pad, 512, xch.gather_proj())
    w_pa, w_pb, w_out = xch.proj_weights(got_proj)
    sc_f, sc_b = _ctx_fwd(ctx, norm_g, scale_c, shift_c, w_pad, w2f, w2b, gbf, gbb)
    o_f, st_f = _gla_fwd(p, w2f, gbf, sc_f, False, 512, "gla_fwd_f")
    o_b, st_b = _gla_fwd(p, w2b, gbb, sc_b, True, 512, "gla_fwd_b")
    sv = _mix_fwd(p, ln_g, ln_b, ws.astype(bf16), bs.T)
    (dx1, dzbua, dzag, dsv, do, dw_out, dw_pa, dw_pb, dgf, dgate, dgbn, loss) = _mid(
        x, tgt, p, o_f, o_b, sv, gate, gf, gb_norm, w_pa, w_pb, w_out, 256)
    dva, dws, dbs_acc, dln_g, dln_b = _mix_bwd(p, dsv, ln_g, ln_b, jnp.swapaxes(ws, 1, 2).astype(bf16))
    small = _rows128(dln_g, dln_b, dws, jnp.sum(dbs_acc, axis=-1), dgbn, dgf, jnp.broadcast_to(loss, (1, 128)))
    blocks_a, blocks_b, *got_first = _tn_matmul(h, [dzbua, dva, dzag], 1024, "dw_early", xch.first(dw_out, dw_pa, dw_pb, small),
                                               _pack_early())

    dqkv_f, dlr_f, dw2f, dgbf, dsc_f, *got_a = _gla_bwd(p, do, st_f, w2f, gbf, None, False, 512, "gla_bwd_f",
                                                        xch.early_a(blocks_a))
    dqkv, dlr, dw2b, dgbb, dsc_b, *got_b = _gla_bwd(p, do, st_b, w2b, gbb, (dqkv_f, dlr_f), True, 512, "gla_bwd_b",
                                                    xch.early_b(blocks_b))
    dwk_c, dwv_c, dwl_c, dmodc, dg_c, dw2c, dgbc, dgt = _ctx_bwd(ctx, norm_g, scale_c, shift_c, w_pad, w2f, w2b, gbf, gbb, dsc_f, dsc_b,
                                                                (dw2f, dw2b), (dgbf, dgbb))
    (blocks_late,) = _tn_matmul(h, [dqkv, dlr], 1024, "dw_qkv_lr", pack=_pack_late(dwk_c, dwv_c, dwl_c))
    dw2 = jnp.stack([dw2f[0:RANK] + dw2c[0, 0:RANK], dw2b[RANK:2 * RANK] + dw2c[1, RANK:2 * RANK]])
    dgb = jnp.concatenate([dgbf + dgbc[0], dgbb + dgbc[1]], axis=0)
    gx, dshift, dscale, dg, *got_late = _in_bwd(x, dx1, dqkv, dzbua, dva, dzag, dlr, w_pad, norm_g, scale, 512,
                                                xch.late(blocks_late, dgt))
    return dict(loss=loss, gx=gx, dmod=jnp.concatenate([dshift, dscale, dgate], axis=1), dmodc=dmodc, dnorm_g=dg + dg_c,
                dnorm_parts=(dg, dg_c), dmod_parts=(dshift, dscale, dgate),
                small=small, blocks_a=blocks_a, blocks_b=blocks_b, blocks_late=blocks_late, dw2=dw2, dgb=dgb,
                dw_pa=dw_pa, dw_pb=dw_pb, dw_out=dw_out, got_first=got_first, got_a=got_a, got_b=got_b, got_late=got_late)


def _rows128(*vs):
    out = []
    for t in vs:
        t = t.reshape(-1)
        pad = (-t.shape[0]) % 128
        out.append(jnp.pad(t, (0, pad)) if pad else t)
    return jnp.concatenate(out).reshape(-1, 128)


def kernel(x, c, ctx, c_ctx, w_mod, b_mod, norm_g, w_in, a_ln_g, a_ln_b, a_ws, a_bs, b_gate_w2, b_gate_b, b_norm_g, w_proj_a, w_proj_b, w_out, final_norm_g, loss_target, m_c_ctx, m_w_mod, m_b_mod, m_norm_g, m_w_in, m_a_ln_g, m_a_ln_b, m_a_ws, m_a_bs, m_b_gate_w2, m_b_gate_b, m_b_norm_g, m_w_proj_a, m_w_proj_b, m_w_out, m_final_norm_g, v_c_ctx, v_w_mod, v_b_mod, v_norm_g, v_w_in, v_a_ln_g, v_a_ln_b, v_a_ws, v_a_bs, v_b_gate_w2, v_b_gate_b, v_b_norm_g, v_w_proj_a, v_w_proj_b, v_w_out, v_final_norm_g):
    me = _me()

    cs, mods, wg, gates = _gather_first(c, c_ctx.reshape(1, D), w_mod[0], b_mod, w_in[0].astype(bf16), b_gate_w2, b_gate_b)
    w_pad, *gate_w = _repack_w(wg, gates)

    mods = jnp.transpose(mods, (1, 0, 2)).reshape(16, 3 * D)
    mod = lax.dynamic_slice(mods, (me, 0), (1, 3 * D))
    modc = mods[8:9, 0:2 * D]

    xch = _Exchanges(w_proj_a[0].astype(bf16), w_proj_b[0].astype(bf16), w_out[0].astype(bf16))
    r = _local_step(x[0], ctx[0], loss_target[0], mod, modc, norm_g, w_pad, a_ln_g, a_ln_b, a_ws[0], a_bs[0], gate_w,
                    b_norm_g, final_norm_g.reshape(1, D), xch)
    p_out, p_pa, p_pb, smalls_e = r["got_first"]
    (p_in_a,) = r["got_a"]
    (p_in_b,) = r["got_b"]
    _, _, p_in_late, p_gt = r["got_late"]

    n_e = (AW + AW + 4 * ACH * ACH + AW + VW + D) // 128
    row = lambda t: t.reshape(1, D)
    rep_e = _adam_params(smalls_e, [a_ln_g, a_ln_b, a_ws, a_bs, b_norm_g, row(final_norm_g)],
                         [m_a_ln_g, m_a_ln_b, m_a_ws, m_a_bs, m_b_norm_g, row(m_final_norm_g)],
                         [v_a_ln_g, v_a_ln_b, v_a_ws, v_a_bs, v_b_norm_g, row(v_final_norm_g)], "adam_rep_early")
    rep_e = [t[0:5] + (t[5].reshape(D),) for t in rep_e]
    loss = jnp.sum(smalls_e[:, n_e, 0])

    mod_grads = _mod_tail(r["dnorm_parts"], r["dmod_parts"], r["dmodc"], cs, w_mod)
    mod_res = _adam_plain(mod_grads, [w_mod, norm_g, b_mod, row(c_ctx)], [m_w_mod, m_norm_g, m_b_mod, row(m_c_ctx)],
                          [v_w_mod, v_norm_g, v_b_mod, row(v_c_ctx)], "adam_mod")
    wm, ng, bmod_r, cc = ([t[i] for t in mod_res] for i in range(4))

    a_in = _adam_w_in(p_in_a, p_in_b, p_in_late, w_in, m_w_in, v_w_in)
    blk = _adam_blocks([p_pa, p_pb, p_out], [w_proj_a, w_proj_b, w_out], [m_w_proj_a, m_w_proj_b, m_w_out],
                       [v_w_proj_a, v_w_proj_b, v_w_out], "adam_proj_out")
    a_pa, a_pb, a_out = ([t[i] for t in blk] for i in range(3))
    a_gt = _adam_gate(p_gt, b_gate_w2, b_gate_b, m_b_gate_w2, m_b_gate_b, v_b_gate_w2, v_b_gate_b)
    sh = [(a_in[k], a_pa[k], a_pb[k], a_out[k], a_gt[k], a_gt[4 + k]) for k in range(4)]

    outs = [loss, r["gx"][None]]
    for k in range(4):
        lg, lb, aws, abs_, bng, fng = rep_e[k]
        n_g, bmod = ng[k], bmod_r[k]
        s_in, s_pa, s_pb, s_out, s_w2, s_gb = sh[k]
        outs += [cc[k].reshape(D), wm[k], bmod, n_g, s_in, lg, lb, aws, abs_, s_w2, s_gb, bng, s_pa, s_pb, s_out, fng]
    return tuple(outs)
```
